```python
import jax, jax.numpy as jnp
from jax import lax
import numpy as np

D_MODEL = 1024
BATCH = 8
SEQ = 2048
DEPTH = 1

D_MIX = D_MODEL
HEAD_DIM = 64
ATTN_WIDTH = D_MIX // 2
ATTN_HEADS = ATTN_WIDTH // HEAD_DIM
CONV_DIM = D_MIX - ATTN_WIDTH
CONV_GROUPS = 8
CONV_K = 3
D_FF = 2816
BLOCK_Q = 128
EPS = 1e-6
N_MOD = 6
IN_SPLITS = (ATTN_WIDTH, 2 * ATTN_WIDTH, 3 * ATTN_WIDTH, 3 * ATTN_WIDTH + ATTN_HEADS,
             3 * ATTN_WIDTH + ATTN_HEADS + CONV_DIM, 3 * ATTN_WIDTH + ATTN_HEADS + 2 * CONV_DIM)
D_IN = 3 * ATTN_WIDTH + ATTN_HEADS + 3 * CONV_DIM

kernel_name = "hymba_fox_shortconv_convffn_adaln"


def rmsnorm(x, g):
    xf = x.astype(jnp.float32)
    xf = xf * lax.rsqrt(jnp.mean(xf * xf, axis=-1, keepdims=True) + EPS)
    return (xf * g.astype(jnp.float32)).astype(x.dtype)


def causal_dwconv(u, w):
    s = u.shape[1]
    up = jnp.pad(u, ((0, 0), (CONV_K - 1, 0), (0, 0)))
    y = w[0] * up[:, 0:s]
    for k in range(1, CONV_K):
        y = y + w[k] * up[:, k:k + s]
    return y


def forgetting_attention(q, k, v, logf):
    b, s, h, dh = q.shape
    fcum = jnp.cumsum(logf, axis=1)
    fcum = jnp.transpose(fcum, (0, 2, 1))
    scale = 1.0 / np.sqrt(dh)
    outs = []
    for i in range(s // BLOCK_Q):
        q0, q1 = i * BLOCK_Q, (i + 1) * BLOCK_Q
        qb = q[:, q0:q1]
        kb = k[:, :q1]
        vb = v[:, :q1]
        logits = jnp.einsum('bqhd,bkhd->bhqk', qb, kb).astype(jnp.float32) * scale
        logits = logits + fcum[:, :, q0:q1, None] - fcum[:, :, None, :q1]
        q_pos = q0 + jnp.arange(BLOCK_Q)
        k_pos = jnp.arange(q1)
        causal = k_pos[None, :] <= q_pos[:, None]
        logits = jnp.where(causal[None, None], logits, -jnp.inf)
        p = jax.nn.softmax(logits, axis=-1).astype(v.dtype)
        outs.append(jnp.einsum('bhqk,bkhd->bqhd', p, vb))
    return jnp.concatenate(outs, axis=1)


def hybrid_layer(x, c_act, w_ada, b_ada, norm1_g, w_in, b_forget, q_norm_g, k_norm_g,
                 conv_mix_w, w_out, norm2_g, w_up, ffn_conv_w, w_down):
    b, s, _ = x.shape
    mod = (c_act @ w_ada + b_ada)[:, None, :]
    sh1, sc1, g1, sh2, sc2, g2 = jnp.split(mod, N_MOD, axis=-1)

    h = rmsnorm(x, norm1_g) * (1 + sc1) + sh1
    proj = h @ w_in
    q, k, v, fg, xin, bg, cg = jnp.split(proj, IN_SPLITS, axis=-1)
    q = rmsnorm(q.reshape(b, s, ATTN_HEADS, HEAD_DIM), q_norm_g)
    k = rmsnorm(k.reshape(b, s, ATTN_HEADS, HEAD_DIM), k_norm_g)
    v = v.reshape(b, s, ATTN_HEADS, HEAD_DIM)
    logf = jax.nn.log_sigmoid((fg + b_forget).astype(jnp.float32))
    attn = forgetting_attention(q, k, v, logf).reshape(b, s, ATTN_WIDTH)

    conv = bg * causal_dwconv(cg * xin, conv_mix_w)

    mixed = jnp.concatenate([attn, conv], axis=-1)
    x = x + g1 * (mixed @ w_out)

    h2 = rmsnorm(x, norm2_g) * (1 + sc2) + sh2
    u = causal_dwconv(h2 @ w_up, ffn_conv_w)
    u_gate, u_val = jnp.split(u, 2, axis=-1)
    y = (jax.nn.silu(u_gate) * u_val) @ w_down
    return x + g2 * y


def _fwd_setup_inputs(seed: int = 0) -> dict:
    key = jax.random.key(seed)
    ks = jax.random.split(key, 16)
    f32 = jnp.float32
    nrm = lambda k, shape, std: jax.random.normal(k, shape, f32) * std
    return {
        "x": nrm(ks[0], (BATCH, SEQ, D_MODEL), 1.0),
        "c": nrm(ks[1], (BATCH, D_MODEL), 1.0),
        "w_ada": nrm(ks[2], (DEPTH, D_MODEL, N_MOD * D_MODEL), 0.5 * D_MODEL ** -0.5),
        "b_ada": nrm(ks[3], (DEPTH, N_MOD * D_MODEL), 0.02),
        "norm1_g": 1.0 + nrm(ks[4], (DEPTH, D_MODEL), 0.02),
        "w_in": nrm(ks[5], (DEPTH, D_MODEL, D_IN), D_MODEL ** -0.5),
        "b_forget": jax.random.uniform(ks[6], (DEPTH, ATTN_HEADS), f32, 1.0, 4.0),
        "q_norm_g": 1.0 + nrm(ks[7], (DEPTH, HEAD_DIM), 0.02),
        "k_norm_g": 1.0 + nrm(ks[8], (DEPTH, HEAD_DIM), 0.02),
        "conv_mix_w": nrm(ks[9], (DEPTH, CONV_K, CONV_DIM), CONV_K ** -0.5),
        "w_out": nrm(ks[10], (DEPTH, D_MIX, D_MODEL), D_MIX ** -0.5),
        "norm2_g": 1.0 + nrm(ks[11], (DEPTH, D_MODEL), 0.02),
        "w_up": nrm(ks[12], (DEPTH, D_MODEL, 2 * D_FF), D_MODEL ** -0.5),
        "ffn_conv_w": nrm(ks[13], (DEPTH, CONV_K, 2 * D_FF), CONV_K ** -0.5),
        "w_down": nrm(ks[14], (DEPTH, D_FF, D_MODEL), D_FF ** -0.5),
    }


def _fwd_reference(x, c, w_ada, b_ada, norm1_g, w_in, b_forget, q_norm_g, k_norm_g,
              conv_mix_w, w_out, norm2_g, w_up, ffn_conv_w, w_down):
    c_act = jax.nn.silu(c)
    for l in range(DEPTH):
        x = hybrid_layer(x, c_act, w_ada[l], b_ada[l], norm1_g[l], w_in[l], b_forget[l],
                         q_norm_g[l], k_norm_g[l], conv_mix_w[l], w_out[l], norm2_g[l],
                         w_up[l], ffn_conv_w[l], w_down[l])
    return x


import jax as _jax
import jax.numpy as _jnp

TWIN_FORMAT = 'train_step'
FWD_PARAMS = ['x', 'c', 'w_ada', 'b_ada', 'norm1_g', 'w_in', 'b_forget', 'q_norm_g', 'k_norm_g', 'conv_mix_w', 'w_out', 'norm2_g', 'w_up', 'ffn_conv_w', 'w_down']
TWIN_WEIGHTS = ['w_ada', 'b_ada', 'norm1_g', 'w_in', 'b_forget', 'q_norm_g', 'k_norm_g', 'conv_mix_w', 'w_out', 'norm2_g', 'w_up', 'ffn_conv_w', 'w_down']
TWIN_DIFF_INPUT = 'x'
TWIN_INPUTS = ['x', 'c', 'w_ada', 'b_ada', 'norm1_g', 'w_in', 'b_forget', 'q_norm_g', 'k_norm_g', 'conv_mix_w', 'w_out', 'norm2_g', 'w_up', 'ffn_conv_w', 'w_down', 'loss_target', 'm_w_ada', 'm_b_ada', 'm_norm1_g', 'm_w_in', 'm_b_forget', 'm_q_norm_g', 'm_k_norm_g', 'm_conv_mix_w', 'm_w_out', 'm_norm2_g', 'm_w_up', 'm_ffn_conv_w', 'm_w_down', 'v_w_ada', 'v_b_ada', 'v_norm1_g', 'v_w_in', 'v_b_forget', 'v_q_norm_g', 'v_k_norm_g', 'v_conv_mix_w', 'v_w_out', 'v_norm2_g', 'v_w_up', 'v_ffn_conv_w', 'v_w_down']
TWIN_OUTPUTS = ['loss', 'grad_x', 'grad_w_ada', 'grad_b_ada', 'grad_norm1_g', 'grad_w_in', 'grad_b_forget', 'grad_q_norm_g', 'grad_k_norm_g', 'grad_conv_mix_w', 'grad_w_out', 'grad_norm2_g', 'grad_w_up', 'grad_ffn_conv_w', 'grad_w_down', 'delta_w_ada', 'delta_b_ada', 'delta_norm1_g', 'delta_w_in', 'delta_b_forget', 'delta_q_norm_g', 'delta_k_norm_g', 'delta_conv_mix_w', 'delta_w_out', 'delta_norm2_g', 'delta_w_up', 'delta_ffn_conv_w', 'delta_w_down', 'new_m_w_ada', 'new_m_b_ada', 'new_m_norm1_g', 'new_m_w_in', 'new_m_b_forget', 'new_m_q_norm_g', 'new_m_k_norm_g', 'new_m_conv_mix_w', 'new_m_w_out', 'new_m_norm2_g', 'new_m_w_up', 'new_m_ffn_conv_w', 'new_m_w_down', 'new_v_w_ada', 'new_v_b_ada', 'new_v_norm1_g', 'new_v_w_in', 'new_v_b_forget', 'new_v_q_norm_g', 'new_v_k_norm_g', 'new_v_conv_mix_w', 'new_v_w_out', 'new_v_norm2_g', 'new_v_w_up', 'new_v_ffn_conv_w', 'new_v_w_down']
TWIN_LEAF_KINDS = {'loss': 'loss', 'grad_x': 'grad_x', 'grad_w_ada': 'grad_w', 'grad_b_ada': 'grad_w', 'grad_norm1_g': 'grad_w', 'grad_w_in': 'grad_w', 'grad_b_forget': 'grad_w', 'grad_q_norm_g': 'grad_w', 'grad_k_norm_g': 'grad_w', 'grad_conv_mix_w': 'grad_w', 'grad_w_out': 'grad_w', 'grad_norm2_g': 'grad_w', 'grad_w_up': 'grad_w', 'grad_ffn_conv_w': 'grad_w', 'grad_w_down': 'grad_w', 'delta_w_ada': 'delta_w', 'delta_b_ada': 'delta_w', 'delta_norm1_g': 'delta_w', 'delta_w_in': 'delta_w', 'delta_b_forget': 'delta_w', 'delta_q_norm_g': 'delta_w', 'delta_k_norm_g': 'delta_w', 'delta_conv_mix_w': 'delta_w', 'delta_w_out': 'delta_w', 'delta_norm2_g': 'delta_w', 'delta_w_up': 'delta_w', 'delta_ffn_conv_w': 'delta_w', 'delta_w_down': 'delta_w', 'new_m_w_ada': 'new_m', 'new_m_b_ada': 'new_m', 'new_m_norm1_g': 'new_m', 'new_m_w_in': 'new_m', 'new_m_b_forget': 'new_m', 'new_m_q_norm_g': 'new_m', 'new_m_k_norm_g': 'new_m', 'new_m_conv_mix_w': 'new_m', 'new_m_w_out': 'new_m', 'new_m_norm2_g': 'new_m', 'new_m_w_up': 'new_m', 'new_m_ffn_conv_w': 'new_m', 'new_m_w_down': 'new_m', 'new_v_w_ada': 'new_v', 'new_v_b_ada': 'new_v', 'new_v_norm1_g': 'new_v', 'new_v_w_in': 'new_v', 'new_v_b_forget': 'new_v', 'new_v_q_norm_g': 'new_v', 'new_v_k_norm_g': 'new_v', 'new_v_conv_mix_w': 'new_v', 'new_v_w_out': 'new_v', 'new_v_norm2_g': 'new_v', 'new_v_w_up': 'new_v', 'new_v_ffn_conv_w': 'new_v', 'new_v_w_down': 'new_v'}


def _forward(args):
    return _fwd_reference(*[args[k] for k in FWD_PARAMS])


def _output_shape():
    out = _jax.eval_shape(lambda: _forward(_fwd_setup_inputs(0)))
    return out.shape, out.dtype

N_MICROBATCH = 1
ADAM_LR = 0.001
ADAM_B1 = 0.9
ADAM_B2 = 0.999
ADAM_EPS = 1e-08
ADAM_WD = 0.01
ADAM_STEP = 10
PER_EXAMPLE_BATCH_AXIS = {'x': 0, 'c': 0, 'loss_target': 0}
SHARED_INPUTS = []
_WEIGHT_DTYPES = {'w_ada': _jnp.float32, 'b_ada': _jnp.float32, 'norm1_g': _jnp.float32, 'w_in': _jnp.float32, 'b_forget': _jnp.float32, 'q_norm_g': _jnp.float32, 'k_norm_g': _jnp.float32, 'conv_mix_w': _jnp.float32, 'w_out': _jnp.float32, 'norm2_g': _jnp.float32, 'w_up': _jnp.float32, 'ffn_conv_w': _jnp.float32, 'w_down': _jnp.float32}
MOMENT_SCALE = {'w_ada': 9.805759e-01, 'b_ada': 2.191320e+00, 'norm1_g': 4.084811e+00, 'w_in': 1.377733e-01, 'b_forget': 4.106792e+00, 'q_norm_g': 5.975886e-01, 'k_norm_g': 5.964091e-01, 'conv_mix_w': 1.529289e+00, 'w_out': 1.161511e-01, 'norm2_g': 1.814171e+00, 'w_up': 7.663871e-02, 'ffn_conv_w': 2.863516e-01, 'w_down': 5.770874e-02}


def _to_microbatches(a, axis):
    t = _jnp.moveaxis(a, axis, 0)
    t = t.reshape((N_MICROBATCH, t.shape[0] // N_MICROBATCH) + t.shape[1:])
    return _jnp.moveaxis(t, 1, axis + 1)


def setup_inputs(seed: int = 0) -> dict:
    inp = _fwd_setup_inputs(seed)
    key = _jax.random.fold_in(_jax.random.key(seed), 7919)
    shape, _ = _output_shape()
    out = dict(inp)
    out["loss_target"] = _jax.random.normal(_jax.random.fold_in(key, 0), shape, _jnp.float32)
    for i, name in enumerate(TWIN_WEIGHTS):
        w = inp[name].astype(_jnp.float32)
        if MOMENT_SCALE is None:
            s = _jnp.sqrt(_jnp.mean(_jnp.square(w)) + 1e-30)
        else:
            s = MOMENT_SCALE[name]
        km, kv = _jax.random.split(_jax.random.fold_in(key, i + 1))
        out[name] = w
        out["m_" + name] = s * _jax.random.normal(km, w.shape, _jnp.float32)
        out["v_" + name] = (s * s) * _jax.random.uniform(kv, w.shape, _jnp.float32, 0.5, 1.5)
    if N_MICROBATCH > 1:
        for name, axis in PER_EXAMPLE_BATCH_AXIS.items():
            out[name] = _to_microbatches(out[name], axis)
    return {'x': out['x'], 'c': out['c'], 'w_ada': out['w_ada'], 'b_ada': out['b_ada'], 'norm1_g': out['norm1_g'], 'w_in': out['w_in'], 'b_forget': out['b_forget'], 'q_norm_g': out['q_norm_g'], 'k_norm_g': out['k_norm_g'], 'conv_mix_w': out['conv_mix_w'], 'w_out': out['w_out'], 'norm2_g': out['norm2_g'], 'w_up': out['w_up'], 'ffn_conv_w': out['ffn_conv_w'], 'w_down': out['w_down'], 'loss_target': out['loss_target'], 'm_w_ada': out['m_w_ada'], 'm_b_ada': out['m_b_ada'], 'm_norm1_g': out['m_norm1_g'], 'm_w_in': out['m_w_in'], 'm_b_forget': out['m_b_forget'], 'm_q_norm_g': out['m_q_norm_g'], 'm_k_norm_g': out['m_k_norm_g'], 'm_conv_mix_w': out['m_conv_mix_w'], 'm_w_out': out['m_w_out'], 'm_norm2_g': out['m_norm2_g'], 'm_w_up': out['m_w_up'], 'm_ffn_conv_w': out['m_ffn_conv_w'], 'm_w_down': out['m_w_down'], 'v_w_ada': out['v_w_ada'], 'v_b_ada': out['v_b_ada'], 'v_norm1_g': out['v_norm1_g'], 'v_w_in': out['v_w_in'], 'v_b_forget': out['v_b_forget'], 'v_q_norm_g': out['v_q_norm_g'], 'v_k_norm_g': out['v_k_norm_g'], 'v_conv_mix_w': out['v_conv_mix_w'], 'v_w_out': out['v_w_out'], 'v_norm2_g': out['v_norm2_g'], 'v_w_up': out['v_w_up'], 'v_ffn_conv_w': out['v_ffn_conv_w'], 'v_w_down': out['v_w_down']}


def _loss(weights, diff, rest, loss_target):
    with _jax.named_scope("forward"):
        args = {**rest, TWIN_DIFF_INPUT: diff, **{k: w.astype(_WEIGHT_DTYPES[k]) for k, w in weights.items()}}
        y = _forward(args)
    with _jax.named_scope("loss_head"):
        err = _jnp.square(y.astype(_jnp.float32) - loss_target)
        return 0.5 * _jnp.sum(_jnp.mean(err, axis=-1)) if err.ndim else 0.5 * err


def _adamw(w, g, m, v):
    m = ADAM_B1 * m + (1.0 - ADAM_B1) * g
    v = ADAM_B2 * v + (1.0 - ADAM_B2) * _jnp.square(g)
    m_hat = m / (1.0 - ADAM_B1 ** ADAM_STEP)
    v_hat = v / (1.0 - ADAM_B2 ** ADAM_STEP)
    delta = -ADAM_LR * (m_hat / (_jnp.sqrt(v_hat) + ADAM_EPS) + ADAM_WD * w)
    return delta, m, v


def reference(x, c, w_ada, b_ada, norm1_g, w_in, b_forget, q_norm_g, k_norm_g, conv_mix_w, w_out, norm2_g, w_up, ffn_conv_w, w_down, loss_target, m_w_ada, m_b_ada, m_norm1_g, m_w_in, m_b_forget, m_q_norm_g, m_k_norm_g, m_conv_mix_w, m_w_out, m_norm2_g, m_w_up, m_ffn_conv_w, m_w_down, v_w_ada, v_b_ada, v_norm1_g, v_w_in, v_b_forget, v_q_norm_g, v_k_norm_g, v_conv_mix_w, v_w_out, v_norm2_g, v_w_up, v_ffn_conv_w, v_w_down):
    given = dict(x=x, c=c, w_ada=w_ada, b_ada=b_ada, norm1_g=norm1_g, w_in=w_in, b_forget=b_forget, q_norm_g=q_norm_g, k_norm_g=k_norm_g, conv_mix_w=conv_mix_w, w_out=w_out, norm2_g=norm2_g, w_up=w_up, ffn_conv_w=ffn_conv_w, w_down=w_down, loss_target=loss_target, m_w_ada=m_w_ada, m_b_ada=m_b_ada, m_norm1_g=m_norm1_g, m_w_in=m_w_in, m_b_forget=m_b_forget, m_q_norm_g=m_q_norm_g, m_k_norm_g=m_k_norm_g, m_conv_mix_w=m_conv_mix_w, m_w_out=m_w_out, m_norm2_g=m_norm2_g, m_w_up=m_w_up, m_ffn_conv_w=m_ffn_conv_w, m_w_down=m_w_down, v_w_ada=v_w_ada, v_b_ada=v_b_ada, v_norm1_g=v_norm1_g, v_w_in=v_w_in, v_b_forget=v_b_forget, v_q_norm_g=v_q_norm_g, v_k_norm_g=v_k_norm_g, v_conv_mix_w=v_conv_mix_w, v_w_out=v_w_out, v_norm2_g=v_norm2_g, v_w_up=v_w_up, v_ffn_conv_w=v_ffn_conv_w, v_w_down=v_w_down)
    weights = {n: given[n] for n in TWIN_WEIGHTS}
    shared = {n: given[n] for n in SHARED_INPUTS}
    per_example = {n: given[n] for n in ['x', 'c']}
    grad_fn = _jax.value_and_grad(_loss, argnums=(0, 1))

    def one_microbatch(ex, loss_target):
        ex = dict(ex)
        diff = ex.pop(TWIN_DIFF_INPUT)
        return grad_fn(weights, diff, {**shared, **ex}, loss_target)

    if N_MICROBATCH == 1:
        loss, (grad_w, grad_x) = one_microbatch(per_example, given["loss_target"])
    else:
        def body(carry, xs):
            loss_sum, grad_sum = carry
            l_k, (gw_k, gx_k) = one_microbatch(xs[0], xs[1])
            with _jax.named_scope("update"):
                return (loss_sum + l_k, _jax.tree.map(_jnp.add, grad_sum, gw_k)), gx_k

        init = (_jnp.zeros((), _jnp.float32), _jax.tree.map(_jnp.zeros_like, weights))
        (loss, grad_w), grad_x = _jax.lax.scan(body, init, (per_example, given["loss_target"]))
    with _jax.named_scope("update"):
        delta_w, new_m, new_v = {}, {}, {}
        for n in TWIN_WEIGHTS:
            delta_w[n], new_m[n], new_v[n] = _adamw(weights[n], grad_w[n], given["m_" + n], given["v_" + n])
    return (loss, grad_x, *[grad_w[n] for n in TWIN_WEIGHTS], *[delta_w[n] for n in TWIN_WEIGHTS],
            *[new_m[n] for n in TWIN_WEIGHTS], *[new_v[n] for n in TWIN_WEIGHTS])
```

```python
import functools

import jax
import jax.numpy as jnp
from jax import lax
from jax.experimental import pallas as pl
from jax.experimental.pallas import tpu as pltpu

F32 = jnp.float32
BF16 = jnp.bfloat16

NDEV = 8
D = 1024
HEADS = 8
DH = 64
AW = 512
CW = 512
DFF = 2816
DIN = 3080
DINP = 3200
NMOD = 6
EPS = 1e-6
QK_SCALE = 0.125
LANES = 128
SUB = 8

ADAM_LR = 0.001
ADAM_B1 = 0.9
ADAM_B2 = 0.999
ADAM_EPS = 1e-08
ADAM_WD = 0.01
ADAM_STEP = 10

MESH = pl.DeviceIdType.MESH
ANY = pl.BlockSpec(memory_space=pl.ANY)

NN = (((1,), (0,)), ((), ()))
NT = (((1,), (1,)), ((), ()))
TN = (((0,), (0,)), ((), ()))


def _dot(a, b, dims=NN, precision=None):
    return lax.dot_general(a, b, dims, precision=precision, preferred_element_type=F32)


def _params(sem=None, vmem_mb=None):
    kw = {}
    if sem is not None:
        kw["dimension_semantics"] = sem
    if vmem_mb is not None:
        kw["vmem_limit_bytes"] = vmem_mb * 1024 * 1024
    return pltpu.CompilerParams(**kw)


def _sigmoid(x):
    return 1.0 / (1.0 + jnp.exp(-x))


def _exchange(items, name):
    n = len(items)
    modes = [m for _, m in items]

    def body(*refs):
        srcs, outs = refs[:n], refs[n:2 * n]
        send_sems, recv_sems, loc_sems = refs[2 * n:]
        x, y, c = lax.axis_index("x"), lax.axis_index("y"), lax.axis_index("c")
        me = 4 * x + 2 * y + c

        def piece(a, slot):
            return srcs[a] if modes[a] == "ag" else srcs[a].at[slot]

        local = []
        for a in range(n):
            cp = pltpu.make_async_copy(piece(a, me), outs[a].at[me], loc_sems.at[a])
            cp.start()
            local.append(cp)
        sent = []
        for r in range(1, NDEV):
            px = 1 - x if (r >> 2) & 1 else x
            py = 1 - y if (r >> 1) & 1 else y
            pc = 1 - c if r & 1 else c
            pidx = 4 * px + 2 * py + pc
            for a in range(n):
                cp = pltpu.make_async_remote_copy(
                    src_ref=piece(a, pidx), dst_ref=outs[a].at[me],
                    send_sem=send_sems.at[a, r - 1], recv_sem=recv_sems.at[a, r - 1],
                    device_id=(px, py, pc), device_id_type=MESH)
                cp.start()
                sent.append((cp, a, r, (px, py, pc), pidx))
        for cp, a, r, peer, pidx in sent:
            pltpu.make_async_remote_copy(
                src_ref=piece(a, pidx), dst_ref=outs[a].at[pidx],
                send_sem=send_sems.at[a, r - 1], recv_sem=recv_sems.at[a, r - 1],
                device_id=peer, device_id_type=MESH).wait_recv()
        for cp, *_ in sent:
            cp.wait_send()
        for cp in local:
            cp.wait()

    out_shape = []
    for arr, mode in items:
        sh = (NDEV,) + arr.shape if mode == "ag" else arr.shape
        out_shape.append(jax.ShapeDtypeStruct(sh, arr.dtype))
    return pl.pallas_call(
        body, name=name,
        out_shape=tuple(out_shape),
        in_specs=[ANY] * n, out_specs=tuple([ANY] * n),
        scratch_shapes=[pltpu.SemaphoreType.DMA((n, NDEV - 1)), pltpu.SemaphoreType.DMA((n, NDEV - 1)),
                        pltpu.SemaphoreType.DMA((n,))],
        compiler_params=pltpu.CompilerParams(has_side_effects=True),
    )(*[a for a, _ in items])


def _mm(a, b, mode, out_dtype, tm, tn, name):
    if mode == "nn":
        (m, k), n = a.shape, b.shape[1]
        a_spec = pl.BlockSpec((tm, k), lambda i, j: (i, 0))
        b_spec = pl.BlockSpec((k, tn), lambda i, j: (0, j))
        dims = NN
    elif mode == "nt":
        (m, k), n = a.shape, b.shape[0]
        a_spec = pl.BlockSpec((tm, k), lambda i, j: (i, 0))
        b_spec = pl.BlockSpec((tn, k), lambda i, j: (j, 0))
        dims = NT
    else:
        (k, m), n = a.shape, b.shape[1]
        a_spec = pl.BlockSpec((k, tm), lambda i, j: (0, i))
        b_spec = pl.BlockSpec((k, tn), lambda i, j: (0, j))
        dims = TN
    assert m % tm == 0 and n % tn == 0, (m, n, tm, tn)

    def body(a_ref, b_ref, o_ref):
        o_ref[...] = _dot(a_ref[...], b_ref[...], dims).astype(o_ref.dtype)

    return pl.pallas_call(
        body, name=name, grid=(m // tm, n // tn),
        in_specs=[a_spec, b_spec], out_specs=pl.BlockSpec((tm, tn), lambda i, j: (i, j)),
        out_shape=jax.ShapeDtypeStruct((m, n), out_dtype),
        compiler_params=_params(("parallel", "parallel"), 48),
    )(a, b)


def _shift_down(x, k, fill):
    row = lax.broadcasted_iota(jnp.int32, x.shape, 0)
    y = pltpu.roll(x, k, 0)
    for t in range(k):
        y = jnp.where(row == t, fill[t], y)
    return y


def _shift_up(x, k, fill):
    n = x.shape[0]
    row = lax.broadcasted_iota(jnp.int32, x.shape, 0)
    y = pltpu.roll(x, n - k, 0)
    for t in range(k):
        y = jnp.where(row == n - k + t, fill[t], y)
    return y


def _conv_taps(x, halo, w):
    if halo is None:
        f1, f2 = [0.0], [0.0, 0.0]
    else:
        f1, f2 = [halo[7:8, :]], [halo[6:7, :], halo[7:8, :]]
    s1 = _shift_down(x, 1, f1)
    s2 = _shift_down(x, 2, f2)
    u = w[2:3, :] * x + w[1:2, :] * s1 + w[0:1, :] * s2
    return u, s1, s2


def _conv_taps_t(du, nxt, w):
    if nxt is None:
        f1, f2 = [0.0], [0.0, 0.0]
    else:
        f1, f2 = [nxt[0:1, :]], [nxt[0:1, :], nxt[1:2, :]]
    return w[2:3, :] * du + w[1:2, :] * _shift_up(du, 1, f1) + w[0:1, :] * _shift_up(du, 2, f2)


def _ada_fwd(c_all, w_ada, b_my):
    def body(c_ref, w_ref, b_ref, o_ref):
        cv = c_ref[...]
        act = cv * _sigmoid(cv)
        o_ref[...] = _dot(act, w_ref[...], NN, lax.Precision.HIGHEST) + b_ref[...]

    return pl.pallas_call(
        body, name="ada_fwd",
        out_shape=jax.ShapeDtypeStruct((NDEV, w_ada.shape[1]), F32),
        compiler_params=_params(None, 32),
    )(c_all, w_ada, b_my)


TR = 256


def _row_spec(width, col=0):
    return pl.BlockSpec((TR, width), lambda i, col=col: (i, col))


def _full_spec(shape):
    return pl.BlockSpec(shape, lambda i: (0,) * len(shape))


def _norm_mod_fwd(x, mod, g):
    s = x.shape[0]

    def body(x_ref, mod_ref, g_ref, h_ref):
        xv = x_ref[...]
        r = lax.rsqrt(jnp.mean(xv * xv, axis=-1, keepdims=True) + EPS)
        nrm = xv * r * g_ref[...]
        h_ref[...] = (nrm * (1.0 + mod_ref[1:2, :]) + mod_ref[0:1, :]).astype(BF16)

    return pl.pallas_call(
        body, name="norm1_fwd", grid=(s // TR,),
        in_specs=[_row_spec(D), _full_spec((SUB, D)), _full_spec((1, D))],
        out_specs=_row_spec(D), out_shape=jax.ShapeDtypeStruct((s, D), BF16),
        compiler_params=_params(("parallel",)),
    )(x, mod, g)


def _qkv_prep(proj, gq, gk):
    s = proj.shape[0]

    def body(q_ref, k_ref, v_ref, gq_ref, gk_ref, qo_ref, ko_ref, vo_ref):
        for h in range(HEADS):
            sl = slice(DH * h, DH * (h + 1))
            qh = q_ref[:, sl]
            r = lax.rsqrt(jnp.mean(qh * qh, axis=-1, keepdims=True) + EPS)
            qo_ref[:, sl] = (qh * r * gq_ref[...] * QK_SCALE).astype(BF16)
            kh = k_ref[:, sl]
            r = lax.rsqrt(jnp.mean(kh * kh, axis=-1, keepdims=True) + EPS)
            ko_ref[:, sl] = (kh * r * gk_ref[...]).astype(BF16)
        vo_ref[...] = v_ref[...].astype(BF16)

    o = jax.ShapeDtypeStruct((s, AW), BF16)
    return pl.pallas_call(
        body, name="qkv_prep", grid=(s // TR,),
        in_specs=[_row_spec(AW, 0), _row_spec(AW, 1), _row_spec(AW, 2), _full_spec((1, DH)), _full_spec((1, DH))],
        out_specs=(_row_spec(AW), _row_spec(AW), _row_spec(AW)), out_shape=(o, o, o),
        compiler_params=_params(("parallel",)),
    )(proj, proj, proj, gq, gk)


FG_BLOCK = (3 * AW + 3 * CW) // LANES


def _fgate_fwd(proj, bf_pad):
    s = proj.shape[0]

    def body(fg_ref, b_ref, o_ref, carry_ref):
        i = pl.program_id(0)

        @pl.when(i == 0)
        def _():
            carry_ref[...] = jnp.zeros_like(carry_ref)

        z = fg_ref[...] + b_ref[...]
        logf = jnp.minimum(z, 0.0) - jnp.log1p(jnp.exp(-jnp.abs(z)))
        row = lax.broadcasted_iota(jnp.int32, (TR, TR), 0)
        col = lax.broadcasted_iota(jnp.int32, (TR, TR), 1)
        tri = (col <= row).astype(F32)
        cs = _dot(tri, logf, NN, lax.Precision.HIGHEST) + carry_ref[0:1, :]
        o_ref[...] = cs
        carry_ref[...] = jnp.broadcast_to(cs[TR - 1:TR, :], carry_ref.shape)

    return pl.pallas_call(
        body, name="fgate_fwd", grid=(s // TR,),
        in_specs=[_row_spec(LANES, FG_BLOCK), _full_spec((1, LANES))],
        out_specs=_row_spec(LANES), out_shape=jax.ShapeDtypeStruct((s, LANES), F32),
        scratch_shapes=[pltpu.VMEM((SUB, LANES), F32)],
        compiler_params=_params(("arbitrary",)),
    )(proj, bf_pad)


def _fgate_bwd(dfcol, proj, bf_pad):
    s = proj.shape[0]
    nb = s // TR

    def body(df_ref, fg_ref, b_ref, o_ref, db_ref, carry_ref):
        i = pl.program_id(0)

        @pl.when(i == 0)
        def _():
            carry_ref[...] = jnp.zeros_like(carry_ref)
            db_ref[...] = jnp.zeros_like(db_ref)

        row = lax.broadcasted_iota(jnp.int32, (TR, TR), 0)
        col = lax.broadcasted_iota(jnp.int32, (TR, TR), 1)
        tri = (col >= row).astype(F32)
        dlogf = _dot(tri, df_ref[...], NN, lax.Precision.HIGHEST) + carry_ref[0:1, :]
        carry_ref[...] = jnp.broadcast_to(dlogf[0:1, :], carry_ref.shape)
        z = fg_ref[...] + b_ref[...]
        dfg = dlogf * _sigmoid(-z)
        o_ref[...] = dfg.astype(BF16)
        db_ref[0:1, :] += jnp.sum(dfg, axis=0, keepdims=True)

    rev = lambda col: pl.BlockSpec((TR, LANES), lambda i, col=col: (nb - 1 - i, col))
    return pl.pallas_call(
        body, name="fgate_bwd", grid=(nb,),
        in_specs=[rev(0), rev(FG_BLOCK), _full_spec((1, LANES))],
        out_specs=(rev(0), _full_spec((SUB, LANES))),
        out_shape=(jax.ShapeDtypeStruct((s, LANES), BF16), jax.ShapeDtypeStruct((SUB, LANES), F32)),
        scratch_shapes=[pltpu.VMEM((SUB, LANES), F32)],
        compiler_params=_params(("arbitrary",)),
    )(dfcol, proj, bf_pad)


def _resid_norm2(x, z, mod, g):
    s = x.shape[0]

    def body(x_ref, z_ref, mod_ref, g_ref, x1_ref, h_ref):
        x1 = x_ref[...] + mod_ref[2:3, :] * z_ref[...]
        x1_ref[...] = x1
        r = lax.rsqrt(jnp.mean(x1 * x1, axis=-1, keepdims=True) + EPS)
        nrm = x1 * r * g_ref[...]
        h_ref[...] = (nrm * (1.0 + mod_ref[4:5, :]) + mod_ref[3:4, :]).astype(BF16)

    return pl.pallas_call(
        body, name="resid_norm2", grid=(s // TR,),
        in_specs=[_row_spec(D), _row_spec(D), _full_spec((SUB, D)), _full_spec((1, D))],
        out_specs=(_row_spec(D), _row_spec(D)),
        out_shape=(jax.ShapeDtypeStruct((s, D), F32), jax.ShapeDtypeStruct((s, D), BF16)),
        compiler_params=_params(("parallel",)),
    )(x, z, mod, g)


def _loss_head(x1, y, tgt, mod):
    s = x1.shape[0]

    def body(x1_ref, y_ref, t_ref, mod_ref, dout_ref, dy_ref, vec_ref):
        @pl.when(pl.program_id(0) == 0)
        def _():
            vec_ref[...] = jnp.zeros_like(vec_ref)

        yv = y_ref[...]
        g2 = mod_ref[5:6, :]
        diff = x1_ref[...] + g2 * yv - t_ref[...]
        dout = diff * (1.0 / D)
        dout_ref[...] = dout
        dy_ref[...] = (g2 * dout).astype(BF16)
        vec_ref[0:1, :] += jnp.sum(dout * yv, axis=0, keepdims=True)
        vec_ref[1:2, :] += jnp.sum(diff * diff, axis=0, keepdims=True)

    return pl.pallas_call(
        body, name="loss_head", grid=(s // TR,),
        in_specs=[_row_spec(D), _row_spec(D), _row_spec(D), _full_spec((SUB, D))],
        out_specs=(_row_spec(D), _row_spec(D), _full_spec((SUB, D))),
        out_shape=(jax.ShapeDtypeStruct((s, D), F32), jax.ShapeDtypeStruct((s, D), BF16),
                   jax.ShapeDtypeStruct((SUB, D), F32)),
        compiler_params=_params(("arbitrary",)),
    )(x1, y, tgt, mod)


def _norm_mod_bwd(dh, xin, dres, zin, mod, g, shift_row, scale_row, gate_row, name):
    s = dh.shape[0]
    with_gate = gate_row is not None

    def body(*refs):
        if with_gate:
            dh_ref, x_ref, dres_ref, z_ref, mod_ref, g_ref, dx_ref, dz_ref, vec_ref = refs
        else:
            dh_ref, x_ref, dres_ref, mod_ref, g_ref, dx_ref, vec_ref = refs

        @pl.when(pl.program_id(0) == 0)
        def _():
            vec_ref[...] = jnp.zeros_like(vec_ref)

        xv = x_ref[...]
        dhv = dh_ref[...]
        gv = g_ref[...]
        r = lax.rsqrt(jnp.mean(xv * xv, axis=-1, keepdims=True) + EPS)
        xh = xv * r
        dn = dhv * (1.0 + mod_ref[scale_row:scale_row + 1, :])
        dxh = dn * gv
        dx = dres_ref[...] + r * (dxh - xh * jnp.mean(dxh * xh, axis=-1, keepdims=True))
        dx_ref[...] = dx
        vec_ref[0:1, :] += jnp.sum(dhv, axis=0, keepdims=True)
        vec_ref[1:2, :] += jnp.sum(dhv * (xh * gv), axis=0, keepdims=True)
        vec_ref[2:3, :] += jnp.sum(dn * xh, axis=0, keepdims=True)
        if with_gate:
            dz_ref[...] = (mod_ref[gate_row:gate_row + 1, :] * dx).astype(BF16)
            vec_ref[3:4, :] += jnp.sum(dx * z_ref[...], axis=0, keepdims=True)

    ins = [dh, xin, dres] + ([zin] if with_gate else []) + [mod, g]
    in_specs = [_row_spec(D)] * (4 if with_gate else 3) + [_full_spec((SUB, D)), _full_spec((1, D))]
    out_specs = [_row_spec(D)] + ([_row_spec(D)] if with_gate else []) + [_full_spec((SUB, D))]
    out_shape = [jax.ShapeDtypeStruct((s, D), F32)] + ([jax.ShapeDtypeStruct((s, D), BF16)] if with_gate else []) \
        + [jax.ShapeDtypeStruct((SUB, D), F32)]
    return pl.pallas_call(
        body, name=name, grid=(s // TR,),
        in_specs=in_specs, out_specs=tuple(out_specs), out_shape=tuple(out_shape),
        compiler_params=_params(("arbitrary",)),
    )(*ins)


XIN_BLOCK = 3 * AW // LANES
BG_BLOCK = XIN_BLOCK + CW // LANES
CG_BLOCK = BG_BLOCK + CW // LANES


def _seq_spec(s, first_block):
    return pl.BlockSpec((s, LANES), lambda j, fb=first_block: (0, fb + j))


def _mixconv_fwd(proj, w):
    s = proj.shape[0]

    def body(xin_ref, bg_ref, cg_ref, w_ref, o_ref):
        cx = cg_ref[...] * xin_ref[...]
        cv, _, _ = _conv_taps(cx, None, w_ref[...])
        o_ref[...] = bg_ref[...] * cv

    return pl.pallas_call(
        body, name="mixconv_fwd", grid=(CW // LANES,),
        in_specs=[_seq_spec(s, XIN_BLOCK), _seq_spec(s, BG_BLOCK), _seq_spec(s, CG_BLOCK),
                  pl.BlockSpec((3, LANES), lambda j: (0, j))],
        out_specs=_seq_spec(s, 0), out_shape=jax.ShapeDtypeStruct((s, CW), F32),
        compiler_params=_params(("parallel",), 48),
    )(proj, proj, proj, w)


def _mixconv_bwd(dmixed, proj, w):
    s = proj.shape[0]

    def body(d_ref, xin_ref, bg_ref, cg_ref, w_ref, dxin_ref, dbg_ref, dcg_ref, dw_ref):
        wv = w_ref[...]
        xin, cg, dconv = xin_ref[...], cg_ref[...], d_ref[...]
        cx = cg * xin
        cv, s1, s2 = _conv_taps(cx, None, wv)
        dbg_ref[...] = (dconv * cv).astype(BF16)
        dcv = dconv * bg_ref[...]
        dw_ref[...] = jnp.zeros_like(dw_ref)
        dw_ref[0:1, :] = jnp.sum(dcv * s2, axis=0, keepdims=True)
        dw_ref[1:2, :] = jnp.sum(dcv * s1, axis=0, keepdims=True)
        dw_ref[2:3, :] = jnp.sum(dcv * cx, axis=0, keepdims=True)
        dcx = _conv_taps_t(dcv, None, wv)
        dcg_ref[...] = (dcx * xin).astype(BF16)
        dxin_ref[...] = (dcx * cg).astype(BF16)

    o = jax.ShapeDtypeStruct((s, CW), BF16)
    return pl.pallas_call(
        body, name="mixconv_bwd", grid=(CW // LANES,),
        in_specs=[_seq_spec(s, AW // LANES), _seq_spec(s, XIN_BLOCK), _seq_spec(s, BG_BLOCK), _seq_spec(s, CG_BLOCK),
                  pl.BlockSpec((3, LANES), lambda j: (0, j))],
        out_specs=(_seq_spec(s, 0), _seq_spec(s, 0), _seq_spec(s, 0), pl.BlockSpec((SUB, LANES), lambda j: (0, j))),
        out_shape=(o, o, o, jax.ShapeDtypeStruct((SUB, CW), F32)),
        compiler_params=_params(("parallel",), 48),
    )(dmixed, proj, proj, proj, w)


TA = 256
NEG = -1e30


def _causal_mask():
    row = lax.broadcasted_iota(jnp.int32, (TA, TA), 0)
    col = lax.broadcasted_iota(jnp.int32, (TA, TA), 1)
    return col <= row


def _attn_fwd(qs, kn, vb, fcol, frow):
    s = qs.shape[0]
    nq = s // TA

    def body(q_ref, k_ref, v_ref, fc_ref, fr_ref, o_ref, lse_ref):
        i = pl.program_id(1)
        for hh in range(2):
            sl = slice(DH * hh, DH * (hh + 1))
            q = q_ref[:, sl]
            fi = fc_ref[0, :, hh:hh + 1]

            def block(j, carry, masked):
                m, l, acc = carry
                j0 = pl.multiple_of(j * TA, TA)
                k = k_ref[pl.ds(j0, TA), sl]
                v = v_ref[pl.ds(j0, TA), sl]
                fj = fr_ref[0, hh, pl.ds(j, 1), :]
                sc = _dot(q, k, NT) + fi - fj
                if masked:
                    sc = jnp.where(_causal_mask(), sc, NEG)
                m_new = jnp.maximum(m, jnp.max(sc, axis=-1, keepdims=True))
                alpha = jnp.exp(m - m_new)
                p = jnp.exp(sc - m_new)
                l = alpha * l + jnp.sum(p, axis=-1, keepdims=True)
                acc = alpha * acc + _dot(p.astype(BF16), v)
                return m_new, l, acc

            init = (jnp.full((TA, 1), NEG, F32), jnp.zeros((TA, 1), F32), jnp.zeros((TA, DH), F32))
            carry = lax.fori_loop(0, i, lambda j, cr: block(j, cr, False), init)
            m, l, acc = block(i, carry, True)
            o_ref[:, sl] = acc / l
            lse_ref[0, :, hh:hh + 1] = m + jnp.log(l)

    return pl.pallas_call(
        body, name="attn_fwd", grid=(HEADS // 2, nq),
        in_specs=[pl.BlockSpec((TA, LANES), lambda p, i: (i, p)),
                  pl.BlockSpec((s, LANES), lambda p, i: (0, p)),
                  pl.BlockSpec((s, LANES), lambda p, i: (0, p)),
                  pl.BlockSpec((1, TA, 2), lambda p, i: (p, i, 0)),
                  pl.BlockSpec((1, 2, nq, TA), lambda p, i: (p, 0, 0, 0))],
        out_specs=(pl.BlockSpec((TA, LANES), lambda p, i: (i, p)),
                   pl.BlockSpec((1, TA, 2), lambda p, i: (p, i, 0))),
        out_shape=(jax.ShapeDtypeStruct((s, AW), F32), jax.ShapeDtypeStruct((HEADS // 2, s, 2), F32)),
        compiler_params=_params(("parallel", "arbitrary"), 32),
    )(qs, kn, vb, fcol, frow)


def _attn_bwd(qs, kn, vb, dmixed, o, lse, fcol, frow):
    s = qs.shape[0]
    nq = s // TA

    def body(q_ref, k_ref, v_ref, do_ref, o_ref, lse_ref, fc_ref, fr_ref, dq_ref, dk_ref, dv_ref, df_ref, dfq_ref):
        dk_ref[...] = jnp.zeros_like(dk_ref)
        dv_ref[...] = jnp.zeros_like(dv_ref)
        df_ref[...] = jnp.zeros_like(df_ref)
        for hh in range(2):
            sl = slice(DH * hh, DH * (hh + 1))

            def q_block(i, _):
                i0 = pl.multiple_of(i * TA, TA)
                q = q_ref[pl.ds(i0, TA), sl]
                do = do_ref[pl.ds(i0, TA), sl]
                delta = jnp.sum(do * o_ref[pl.ds(i0, TA), sl], axis=-1, keepdims=True)
                dob = do.astype(BF16)
                lse_i = lse_ref[0, pl.ds(i0, TA), hh:hh + 1]
                fi = fc_ref[0, pl.ds(i0, TA), hh:hh + 1]

                def block(j, carry, masked):
                    dq, rsum = carry
                    j0 = pl.multiple_of(j * TA, TA)
                    k = k_ref[pl.ds(j0, TA), sl]
                    v = v_ref[pl.ds(j0, TA), sl]
                    fj = fr_ref[0, hh, pl.ds(j, 1), :]
                    sc = _dot(q, k, NT) + fi - fj
                    if masked:
                        sc = jnp.where(_causal_mask(), sc, NEG)
                    p = jnp.exp(sc - lse_i)
                    dv_ref[pl.ds(j0, TA), sl] += _dot(p.astype(BF16), dob, TN)
                    dp = _dot(dob, v, NT)
                    ds = p * (dp - delta)
                    df_ref[0, hh, pl.ds(j, 1), :] -= jnp.sum(ds, axis=0, keepdims=True)
                    dsb = ds.astype(BF16)
                    dk_ref[pl.ds(j0, TA), sl] += _dot(dsb, q, TN)
                    return dq + _dot(dsb, k), rsum + jnp.sum(ds, axis=-1, keepdims=True)

                init = (jnp.zeros((TA, DH), F32), jnp.zeros((TA, 1), F32))
                carry = lax.fori_loop(0, i, lambda j, acc: block(j, acc, False), init)
                dq, rsum = block(i, carry, True)
                dq_ref[pl.ds(i0, TA), sl] = dq
                dfq_ref[0, pl.ds(i0, TA), hh:hh + 1] = rsum
                return 0

            lax.fori_loop(0, nq, q_block, 0)

    pair = lambda p: (0, p)
    seq = pl.BlockSpec((s, LANES), pair)
    small = pl.BlockSpec((1, s, 2), lambda p: (p, 0, 0))
    rows = pl.BlockSpec((1, 2, nq, TA), lambda p: (p, 0, 0, 0))
    o32 = jax.ShapeDtypeStruct((s, AW), F32)
    return pl.pallas_call(
        body, name="attn_bwd", grid=(HEADS // 2,),
        in_specs=[seq, seq, seq, seq, seq, small, small, rows],
        out_specs=(seq, seq, seq, rows, small),
        out_shape=(o32, o32, o32, jax.ShapeDtypeStruct((HEADS // 2, 2, nq, TA), F32),
                   jax.ShapeDtypeStruct((HEADS // 2, s, 2), F32)),
        compiler_params=_params(("parallel",), 48),
    )(qs, kn, vb, dmixed, o, lse, fcol, frow)


def _qkv_post(dqs, dkn, dv, proj, gq, gk):
    s = proj.shape[0]

    def body(dq_ref, dk_ref, dv_ref, q_ref, k_ref, gq_ref, gk_ref, dqo_ref, dko_ref, dvo_ref, vec_ref):
        @pl.when(pl.program_id(0) == 0)
        def _():
            vec_ref[...] = jnp.zeros_like(vec_ref)

        def one(d_ref, x_ref, g_ref, o_ref, row, scale):
            dg = jnp.zeros((1, DH), F32)
            for h in range(HEADS):
                sl = slice(DH * h, DH * (h + 1))
                xv = x_ref[:, sl]
                r = lax.rsqrt(jnp.mean(xv * xv, axis=-1, keepdims=True) + EPS)
                xh = xv * r
                dn = d_ref[:, sl] * scale
                dg = dg + jnp.sum(dn * xh, axis=0, keepdims=True)
                dxh = dn * g_ref[...]
                o_ref[:, sl] = (r * (dxh - xh * jnp.mean(dxh * xh, axis=-1, keepdims=True))).astype(BF16)
            vec_ref[row:row + 1, 0:DH] += dg

        one(dq_ref, q_ref, gq_ref, dqo_ref, 0, QK_SCALE)
        one(dk_ref, k_ref, gk_ref, dko_ref, 1, 1.0)
        dvo_ref[...] = dv_ref[...].astype(BF16)

    o = jax.ShapeDtypeStruct((s, AW), BF16)
    return pl.pallas_call(
        body, name="qkv_post", grid=(s // TR,),
        in_specs=[_row_spec(AW), _row_spec(AW), _row_spec(AW), _row_spec(AW, 0), _row_spec(AW, 1),
                  _full_spec((1, DH)), _full_spec((1, DH))],
        out_specs=(_row_spec(AW), _row_spec(AW), _row_spec(AW), _full_spec((SUB, LANES))),
        out_shape=(o, o, o, jax.ShapeDtypeStruct((SUB, LANES), F32)),
        compiler_params=_params(("arbitrary",)),
    )(dqs, dkn, dv, proj, proj, gq, gk)


TF = 256
NJ = DFF // TF
FFN_ROWS_FWD = 1024
FFN_ROWS_BWD = 512


def _ffn_fwd(h2, wup, cw, wd):
    s = h2.shape[0]
    tr = FFN_ROWS_FWD
    nr = s // tr

    def body(h_ref, wg_ref, wv_ref, cg_ref, cv_ref, wd_ref, pg_ref, pv_ref, y_ref, halo_ref):
        r, j = pl.program_id(0), pl.program_id(1)
        hv = h_ref[...]
        pg = _dot(hv, wg_ref[...]).astype(BF16)
        pv = _dot(hv, wv_ref[...]).astype(BF16)
        pg_ref[...] = pg
        pv_ref[...] = pv
        pgf, pvf = pg.astype(F32), pv.astype(F32)
        ug, _, _ = _conv_taps(pgf, jnp.where(r > 0, halo_ref[j, 0], 0.0), cg_ref[...])
        uv, _, _ = _conv_taps(pvf, jnp.where(r > 0, halo_ref[j, 1], 0.0), cv_ref[...])
        halo_ref[j, 0] = pgf[tr - SUB:tr, :]
        halo_ref[j, 1] = pvf[tr - SUB:tr, :]
        act = (ug * _sigmoid(ug) * uv).astype(BF16)
        contrib = _dot(act, wd_ref[...])

        @pl.when(j == 0)
        def _():
            y_ref[...] = contrib

        @pl.when(j > 0)
        def _():
            y_ref[...] += contrib

    pre = jax.ShapeDtypeStruct((s, DFF), BF16)
    return pl.pallas_call(
        body, name="ffn_fwd", grid=(nr, NJ),
        in_specs=[pl.BlockSpec((tr, D), lambda r, j: (r, 0)),
                  pl.BlockSpec((D, TF), lambda r, j: (0, j)),
                  pl.BlockSpec((D, TF), lambda r, j: (0, NJ + j)),
                  pl.BlockSpec((3, TF), lambda r, j: (0, j)),
                  pl.BlockSpec((3, TF), lambda r, j: (0, NJ + j)),
                  pl.BlockSpec((TF, D), lambda r, j: (j, 0))],
        out_specs=(pl.BlockSpec((tr, TF), lambda r, j: (r, j)),
                   pl.BlockSpec((tr, TF), lambda r, j: (r, j)),
                   pl.BlockSpec((tr, D), lambda r, j: (r, 0))),
        out_shape=(pre, pre, jax.ShapeDtypeStruct((s, D), F32)),
        scratch_shapes=[pltpu.VMEM((NJ, 2, SUB, TF), F32)],
        compiler_params=_params(("arbitrary", "arbitrary"), 56),
    )(h2, wup, wup, cw, cw, wd)


def _ffn_bwd(dy, h2, pre_g, pre_v, wup, cw, wd):
    s = h2.shape[0]
    tr = FFN_ROWS_BWD
    nr = s // tr
    hb = tr // (2 * SUB)

    def body(dy_ref, h_ref, pg_ref, pv_ref, hg_ref, hv_ref, wg_ref, wv_ref, cg_ref, cv_ref, wd_ref,
             dh_ref, dwg_ref, dwv_ref, dwd_ref, dcg_ref, dcv_ref, nxt_ref):
        j, r = pl.program_id(0), pl.program_id(1)
        rr = nr - 1 - r
        row0 = pl.multiple_of(rr * tr, tr)
        cwg, cwv = cg_ref[...], cv_ref[...]
        pg, pv = pg_ref[...].astype(F32), pv_ref[...].astype(F32)
        ug, g1, g2 = _conv_taps(pg, jnp.where(rr > 0, hg_ref[SUB:2 * SUB, :].astype(F32), 0.0), cwg)
        uv, v1, v2 = _conv_taps(pv, jnp.where(rr > 0, hv_ref[SUB:2 * SUB, :].astype(F32), 0.0), cwv)
        sg = _sigmoid(ug)
        sil = ug * sg
        act = (sil * uv).astype(BF16)
        dyv = dy_ref[...]
        da = _dot(dyv, wd_ref[...], NT)
        dug = da * uv * (sg * (1.0 + ug * (1.0 - sg)))
        duv = da * sil
        dpg = _conv_taps_t(dug, jnp.where(r > 0, nxt_ref[0], 0.0), cwg)
        dpv = _conv_taps_t(duv, jnp.where(r > 0, nxt_ref[1], 0.0), cwv)
        nxt_ref[0] = dug[0:SUB, :]
        nxt_ref[1] = duv[0:SUB, :]
        dpgb, dpvb = dpg.astype(BF16), dpv.astype(BF16)
        hv = h_ref[...]
        dwd = _dot(act, dyv, TN)
        dwg = _dot(hv, dpgb, TN)
        dwv = _dot(hv, dpvb, TN)
        dh = _dot(dpgb, wg_ref[...], NT) + _dot(dpvb, wv_ref[...], NT)

        def taps(du, x0, x1, x2):
            return (jnp.sum(du * x2, axis=0, keepdims=True), jnp.sum(du * x1, axis=0, keepdims=True),
                    jnp.sum(du * x0, axis=0, keepdims=True))

        tg, tv = taps(dug, pg, g1, g2), taps(duv, pv, v1, v2)

        @pl.when(r == 0)
        def _():
            dwd_ref[...] = dwd
            dwg_ref[...] = dwg
            dwv_ref[...] = dwv
            dcg_ref[...] = jnp.zeros_like(dcg_ref)
            dcv_ref[...] = jnp.zeros_like(dcv_ref)

        @pl.when(r > 0)
        def _():
            dwd_ref[...] += dwd
            dwg_ref[...] += dwg
            dwv_ref[...] += dwv

        for t in range(3):
            dcg_ref[t:t + 1, :] += tg[t]
            dcv_ref[t:t + 1, :] += tv[t]

        @pl.when(j == 0)
        def _():
            dh_ref[pl.ds(row0, tr), :] = dh

        @pl.when(j > 0)
        def _():
            dh_ref[pl.ds(row0, tr), :] += dh

    rows = lambda j, r: (nr - 1 - r, 0)
    tile = lambda j, r: (nr - 1 - r, j)
    halo = lambda j, r: (jnp.maximum((nr - 1 - r) * hb - 1, 0), j)
    return pl.pallas_call(
        body, name="ffn_bwd", grid=(NJ, nr),
        in_specs=[pl.BlockSpec((tr, D), rows), pl.BlockSpec((tr, D), rows),
                  pl.BlockSpec((tr, TF), tile), pl.BlockSpec((tr, TF), tile),
                  pl.BlockSpec((2 * SUB, TF), halo), pl.BlockSpec((2 * SUB, TF), halo),
                  pl.BlockSpec((D, TF), lambda j, r: (0, j)), pl.BlockSpec((D, TF), lambda j, r: (0, NJ + j)),
                  pl.BlockSpec((3, TF), lambda j, r: (0, j)), pl.BlockSpec((3, TF), lambda j, r: (0, NJ + j)),
                  pl.BlockSpec((TF, D), lambda j, r: (j, 0))],
        out_specs=(pl.BlockSpec((s, D), lambda j, r: (0, 0)),
                   pl.BlockSpec((D, TF), lambda j, r: (0, j)), pl.BlockSpec((D, TF), lambda j, r: (0, j)),
                   pl.BlockSpec((TF, D), lambda j, r: (j, 0)),
                   pl.BlockSpec((SUB, TF), lambda j, r: (0, j)), pl.BlockSpec((SUB, TF), lambda j, r: (0, j))),
        out_shape=(jax.ShapeDtypeStruct((s, D), F32),
                   jax.ShapeDtypeStruct((D, DFF), F32), jax.ShapeDtypeStruct((D, DFF), F32),
                   jax.ShapeDtypeStruct((DFF, D), F32),
                   jax.ShapeDtypeStruct((SUB, DFF), F32), jax.ShapeDtypeStruct((SUB, DFF), F32)),
        scratch_shapes=[pltpu.VMEM((2, SUB, TF), F32)],
        compiler_params=_params(("arbitrary", "arbitrary"), 56),
    )(dy, h2, pre_g, pre_v, pre_g, pre_v, wup, wup, cw, cw, wd)


def _adam(w, g, m, v):
    m = ADAM_B1 * m + (1.0 - ADAM_B1) * g
    v = ADAM_B2 * v + (1.0 - ADAM_B2) * (g * g)
    m_hat = m / (1.0 - ADAM_B1 ** ADAM_STEP)
    v_hat = v / (1.0 - ADAM_B2 ** ADAM_STEP)
    delta = -ADAM_LR * (m_hat / (jnp.sqrt(v_hat) + ADAM_EPS) + ADAM_WD * w)
    return delta, m, v


def _adamw_sharded(parts, w, m, v, tr, name):
    rws, cols = w.shape

    def body(p_ref, w_ref, m_ref, v_ref, g_ref, d_ref, mo_ref, vo_ref):
        g = p_ref[0].astype(F32)
        for d in range(1, NDEV):
            g = g + p_ref[d].astype(F32)
        g_ref[...] = g
        d_ref[...], mo_ref[...], vo_ref[...] = _adam(w_ref[...], g, m_ref[...], v_ref[...])

    blk = pl.BlockSpec((tr, cols), lambda i: (i, 0))
    o = jax.ShapeDtypeStruct((rws, cols), F32)
    return pl.pallas_call(
        body, name=name, grid=(rws // tr,),
        in_specs=[pl.BlockSpec((NDEV, tr, cols), lambda i: (0, i, 0)), blk, blk, blk],
        out_specs=(blk, blk, blk, blk), out_shape=(o, o, o, o),
        compiler_params=_params(("parallel",), 48),
    )(parts, w, m, v)


def _adamw_ada(c_all, dmod_my, w, m, v):
    rws, cols = w.shape
    tr = 256

    def body(c_ref, dm_ref, w_ref, m_ref, v_ref, g_ref, d_ref, mo_ref, vo_ref):
        cv = c_ref[...]
        act = cv * _sigmoid(cv)
        g = _dot(act, dm_ref[...], TN, lax.Precision.HIGHEST)
        g_ref[...] = g
        d_ref[...], mo_ref[...], vo_ref[...] = _adam(w_ref[...], g, m_ref[...], v_ref[...])

    blk = pl.BlockSpec((tr, cols), lambda i: (i, 0))
    o = jax.ShapeDtypeStruct((rws, cols), F32)
    return pl.pallas_call(
        body, name="adamw_ada", grid=(rws // tr,),
        in_specs=[pl.BlockSpec((NDEV, tr), lambda i: (0, i)), _full_spec((NDEV, cols)), blk, blk, blk],
        out_specs=(blk, blk, blk, blk), out_shape=(o, o, o, o),
        compiler_params=_params(("parallel",), 48),
    )(c_all, dmod_my, w, m, v)


REP_ROWS = 16
ROW_N1, ROW_N2, ROW_LOSS, ROW_MISC = 6, 7, 8, 9
LANE_BF, LANE_GQ, LANE_GK = 0, 128, 256


def _adamw_small(rep_all, conv_all, wmv):
    n_ff = wmv[6][0].shape[1]

    def body(*refs):
        rep_ref, conv_ref = refs[:2]
        ins = refs[2:2 + 24]
        outs = refs[2 + 24:]
        loss_ref, outs = outs[0], outs[1:]
        g_rep = rep_ref[0]
        g_conv = conv_ref[0]
        for d in range(1, NDEV):
            g_rep = g_rep + rep_ref[d]
            g_conv = g_conv + conv_ref[d]
        loss_ref[...] = (0.5 / D) * jnp.sum(g_rep[ROW_LOSS:ROW_LOSS + 1, :], axis=-1, keepdims=True)
        grads = [
            None,
            g_rep[ROW_N1:ROW_N1 + 1, :],
            g_rep[ROW_MISC:ROW_MISC + 1, LANE_BF:LANE_BF + HEADS],
            g_rep[ROW_MISC:ROW_MISC + 1, LANE_GQ:LANE_GQ + DH],
            g_rep[ROW_MISC:ROW_MISC + 1, LANE_GK:LANE_GK + DH],
            g_rep[ROW_N2:ROW_N2 + 1, :],
            g_conv[0:3, 0:n_ff],
            g_conv[0:3, n_ff:n_ff + DH],
        ]
        for p in range(8):
            w_ref, m_ref, v_ref = ins[3 * p:3 * p + 3]
            g_ref, d_ref, mo_ref, vo_ref = outs[4 * p:4 * p + 4]
            if p == 0:
                for nmod in range(NMOD):
                    sl = slice(D * nmod, D * (nmod + 1))
                    g = g_rep[nmod:nmod + 1, :]
                    g_ref[:, sl] = g
                    d_ref[:, sl], mo_ref[:, sl], vo_ref[:, sl] = _adam(w_ref[:, sl], g, m_ref[:, sl], v_ref[:, sl])
            else:
                g = grads[p]
                g_ref[...] = g
                d_ref[...], mo_ref[...], vo_ref[...] = _adam(w_ref[...], g, m_ref[...], v_ref[...])

    flat = [a for trio in wmv for a in trio]
    out_shape = [jax.ShapeDtypeStruct((1, 1), F32)]
    for trio in wmv:
        out_shape += [jax.ShapeDtypeStruct(trio[0].shape, F32)] * 4
    return pl.pallas_call(
        body, name="adamw_small", out_shape=tuple(out_shape),
        compiler_params=_params(None, 32),
    )(rep_all, conv_all, *flat)


IN_SPLITS = (AW, 2 * AW, 3 * AW, 3 * AW + HEADS, 3 * AW + HEADS + CW, 3 * AW + HEADS + 2 * CW)


def _assemble_w_in(g_in):
    w = jnp.transpose(g_in, (1, 0, 2)).reshape(D, DIN)
    q, k, v, fg, xin, bg, cg = jnp.split(w, IN_SPLITS, axis=1)
    return jnp.concatenate([q, k, v, xin, bg, cg, fg, jnp.zeros((D, DINP - DIN), w.dtype)], axis=1)


def _scatter_dw_in(dwp):
    q, k, v, xin, bg, cg, fg = jnp.split(dwp, (AW, 2 * AW, 3 * AW, 3 * AW + CW, 3 * AW + 2 * CW, 3 * AW + 3 * CW), axis=1)
    w = jnp.concatenate([q, k, v, fg[:, :HEADS], xin, bg, cg], axis=1)
    return jnp.transpose(w.reshape(D, NDEV, DIN // NDEV), (1, 0, 2))


def kernel(x, c, w_ada, b_ada, norm1_g, w_in, b_forget, q_norm_g, k_norm_g, conv_mix_w, w_out, norm2_g, w_up, ffn_conv_w, w_down, loss_target, m_w_ada, m_b_ada, m_norm1_g, m_w_in, m_b_forget, m_q_norm_g, m_k_norm_g, m_conv_mix_w, m_w_out, m_norm2_g, m_w_up, m_ffn_conv_w, m_w_down, v_w_ada, v_b_ada, v_norm1_g, v_w_in, v_b_forget, v_q_norm_g, v_k_norm_g, v_conv_mix_w, v_w_out, v_norm2_g, v_w_up, v_ffn_conv_w, v_w_down):
    me = 4 * lax.axis_index("x") + 2 * lax.axis_index("y") + lax.axis_index("c")
    xs, tgt = x[0], loss_target[0]
    s = xs.shape[0]
    nq = s // TA
    n_ada = w_ada.shape[2]
    n_ff = w_up.shape[2]

    conv_w = jnp.concatenate([ffn_conv_w[0], conv_mix_w[0]], axis=1)
    conv_w = jnp.concatenate([conv_w, jnp.zeros((SUB - 3, conv_w.shape[1]), F32)], axis=0)
    c_all, conv_all, g_in, g_out, g_up, g_down = _exchange(
        [(c.reshape(SUB, D // SUB), "ag"), (conv_w, "ag"),
         (w_in[0].astype(BF16), "ag"), (w_out[0].astype(BF16), "ag"),
         (w_up[0].astype(BF16), "ag"), (w_down[0].astype(BF16), "ag")], "exchange_weights")
    c_all = c_all.reshape(NDEV, D)
    cw_ffn = jnp.transpose(conv_all[:, :3, :n_ff], (1, 0, 2)).reshape(3, 2 * DFF)
    cw_mix = jnp.transpose(conv_all[:, :3, n_ff:], (1, 0, 2)).reshape(3, CW)
    w_in_p = _assemble_w_in(g_in)
    w_out_f = g_out.reshape(D, D)
    w_up_f = jnp.transpose(g_up, (1, 0, 2)).reshape(D, 2 * DFF)
    w_down_f = g_down.reshape(DFF, D)

    b_my = lax.dynamic_slice(b_ada, (0, me * n_ada), (1, n_ada))
    mod_part = _ada_fwd(c_all, w_ada[0], b_my)
    (mod_rows,) = _exchange([(jnp.broadcast_to(mod_part[:, None, :], (NDEV, SUB, n_ada)), "a2a")], "exchange_mod")
    mod = mod_rows[:, 0, :].reshape(NMOD, D)
    mod = jnp.concatenate([mod, jnp.zeros((SUB - NMOD, D), F32)], axis=0)

    h = _norm_mod_fwd(xs, mod, norm1_g)
    proj = _mm(h, w_in_p, "nn", F32, 512, 640, "proj_fwd")
    qs, kn, vb = _qkv_prep(proj, q_norm_g, k_norm_g)
    bf_pad = jnp.concatenate([b_forget, jnp.zeros((1, LANES - HEADS), F32)], axis=1)
    fcum = _fgate_fwd(proj, bf_pad)
    f8 = fcum[:, :HEADS]
    fcol = jnp.transpose(f8.reshape(s, HEADS // 2, 2), (1, 0, 2))
    frow = jnp.transpose(f8).reshape(HEADS // 2, 2, nq, TA)
    attn, lse = _attn_fwd(qs, kn, vb, fcol, frow)
    conv = _mixconv_fwd(proj, cw_mix)
    mixed = jnp.concatenate([attn, conv], axis=1).astype(BF16)
    z = _mm(mixed, w_out_f, "nn", F32, 512, 512, "out_fwd")
    x1, h2 = _resid_norm2(xs, z, mod, norm2_g)
    pre_g, pre_v, y = _ffn_fwd(h2, w_up_f, cw_ffn, w_down_f)
    dout, dy, vec_l = _loss_head(x1, y, tgt, mod)

    dh2, dwup_g, dwup_v, dwd, dcw_g, dcw_v = _ffn_bwd(dy, h2, pre_g, pre_v, w_up_f, cw_ffn, w_down_f)
    dx1, dz, vec_2 = _norm_mod_bwd(dh2, x1, dout, z, mod, norm2_g, 3, 4, 2, "norm2_bwd")
    dmixed = _mm(dz, w_out_f, "nt", F32, 512, 512, "out_bwd_x")
    dwout = _mm(mixed, dz, "tn", BF16, 512, 512, "out_bwd_w")
    dxin, dbg, dcg, dcw_mix = _mixconv_bwd(dmixed, proj, cw_mix)
    dqs, dkn, dv, dfrow, dfq = _attn_bwd(qs, kn, vb, dmixed, attn, lse, fcol, frow)
    dq, dk, dvb, vec_qk = _qkv_post(dqs, dkn, dv, proj, q_norm_g, k_norm_g)
    dfcol = jnp.transpose(dfrow.reshape(HEADS, s)) + jnp.transpose(dfq, (1, 0, 2)).reshape(s, HEADS)
    dfcol = jnp.concatenate([dfcol, jnp.zeros((s, LANES - HEADS), F32)], axis=1)
    dfg, vec_bf = _fgate_bwd(dfcol, proj, bf_pad)
    dproj = jnp.concatenate([dq, dk, dvb, dxin, dbg, dcg, dfg], axis=1)
    dh = _mm(dproj, w_in_p, "nt", F32, 512, 512, "proj_bwd_x")
    dwin_p = _mm(h, dproj, "tn", BF16, 512, 640, "proj_bwd_w")
    grad_x, vec_1 = _norm_mod_bwd(dh, xs, dx1, None, mod, norm1_g, 0, 1, None, "norm1_bwd")

    misc = jnp.zeros((1, D), F32)
    misc = lax.dynamic_update_slice(misc, vec_bf[0:1, :HEADS], (0, LANE_BF))
    misc = lax.dynamic_update_slice(misc, vec_qk[0:1, :DH], (0, LANE_GQ))
    misc = lax.dynamic_update_slice(misc, vec_qk[1:2, :DH], (0, LANE_GK))
    rep = jnp.concatenate([
        vec_1[0:1], vec_1[1:2], vec_2[3:4], vec_2[0:1], vec_2[1:2], vec_l[0:1],
        vec_1[2:3], vec_2[2:3], vec_l[1:2], misc, jnp.zeros((REP_ROWS - 10, D), F32)], axis=0)
    dcw_ffn = jnp.concatenate([dcw_g, dcw_v], axis=1).reshape(SUB, NDEV, n_ff)
    dcw_all = jnp.concatenate([jnp.transpose(dcw_ffn, (1, 0, 2)),
                               jnp.transpose(dcw_mix.reshape(SUB, NDEV, DH), (1, 0, 2))], axis=2)
    dwup = jnp.concatenate([dwup_g, dwup_v], axis=1).astype(BF16)
    dwup_s = jnp.transpose(dwup.reshape(D, NDEV, n_ff), (1, 0, 2))
    rep_all, conv_parts, p_in, p_out, p_up, p_down = _exchange(
        [(rep, "ag"), (dcw_all, "a2a"), (_scatter_dw_in(dwin_p), "a2a"),
         (dwout.reshape(NDEV, D // NDEV, D), "a2a"), (dwup_s, "a2a"),
         (dwd.astype(BF16).reshape(NDEV, DFF // NDEV, D), "a2a")], "exchange_grads")

    dmod_my = lax.dynamic_slice(rep_all[:, :NMOD, :].reshape(NDEV, NMOD * D), (0, me * n_ada), (NDEV, n_ada))
    r_ada = _adamw_ada(c_all, dmod_my, w_ada[0], m_w_ada[0], v_w_ada[0])
    r_in = _adamw_sharded(p_in, w_in[0], m_w_in[0], v_w_in[0], 256, "adamw_in")
    r_out = _adamw_sharded(p_out, w_out[0], m_w_out[0], v_w_out[0], 128, "adamw_out")
    r_up = _adamw_sharded(p_up, w_up[0], m_w_up[0], v_w_up[0], 256, "adamw_up")
    r_down = _adamw_sharded(p_down, w_down[0], m_w_down[0], v_w_down[0], 176, "adamw_down")
    small = _adamw_small(rep_all, conv_parts, [
        [b_ada, m_b_ada, v_b_ada], [norm1_g, m_norm1_g, v_norm1_g], [b_forget, m_b_forget, v_b_forget],
        [q_norm_g, m_q_norm_g, v_q_norm_g], [k_norm_g, m_k_norm_g, v_k_norm_g], [norm2_g, m_norm2_g, v_norm2_g],
        [ffn_conv_w[0], m_ffn_conv_w[0], v_ffn_conv_w[0]], [conv_mix_w[0], m_conv_mix_w[0], v_conv_mix_w[0]]])
    loss = small[0].reshape(())
    r_bada, r_n1, r_bf, r_gq, r_gk, r_n2, r_cf, r_cm = [small[1 + 4 * p:5 + 4 * p] for p in range(8)]
    lead = lambda t: tuple(a[None] for a in t)
    per_w = [lead(r_ada), r_bada, r_n1, lead(r_in), r_bf, r_gq, r_gk, lead(r_cm), lead(r_out), r_n2,
             lead(r_up), lead(r_cf), lead(r_down)]
    outs = [loss, grad_x[None]]
    for field in range(4):
        outs += [t[field] for t in per_w]
    return tuple(outs)
```

```python
import functools

import jax
import jax.numpy as jnp
from jax import lax
from jax.experimental import pallas as pl
from jax.experimental.pallas import tpu as pltpu

F32 = jnp.float32
BF16 = jnp.bfloat16

NDEV = 8
D = 1024
HEADS = 8
DH = 64
AW = 512
CW = 512
DFF = 2816
DIN = 3080
DINP = 3200
NMOD = 6
EPS = 1e-6
QK_SCALE = 0.125
LANES = 128
SUB = 8

ADAM_LR = 0.001
ADAM_B1 = 0.9
ADAM_B2 = 0.999
ADAM_EPS = 1e-08
ADAM_WD = 0.01
ADAM_STEP = 10

MESH = pl.DeviceIdType.MESH
ANY = pl.BlockSpec(memory_space=pl.ANY)

NN = (((1,), (0,)), ((), ()))
NT = (((1,), (1,)), ((), ()))
TN = (((0,), (0,)), ((), ()))


def _dot(a, b, dims=NN, precision=None):
    return lax.dot_general(a, b, dims, precision=precision, preferred_element_type=F32)


def _params(sem=None, vmem_mb=None):
    kw = {}
    if sem is not None:
        kw["dimension_semantics"] = sem
    if vmem_mb is not None:
        kw["vmem_limit_bytes"] = vmem_mb * 1024 * 1024
    return pltpu.CompilerParams(**kw)


def _sigmoid(x):
    return 1.0 / (1.0 + jnp.exp(-x))


class _Exchange:
    def __init__(self, items):
        self.arrays = [a for a, _ in items]
        self.modes = [m for _, m in items]
        self.n = len(items)
        self.out_shape = [jax.ShapeDtypeStruct((NDEV,) + a.shape if m == "ag" else a.shape, a.dtype) for a, m in items]
        self.scratch = [pltpu.SemaphoreType.DMA((self.n, NDEV - 1)), pltpu.SemaphoreType.DMA((self.n, NDEV - 1)),
                        pltpu.SemaphoreType.DMA((self.n,))]

    def _copies(self, srcs, outs, sems):
        send_sems, recv_sems, loc_sems = sems
        x, y, c = lax.axis_index("x"), lax.axis_index("y"), lax.axis_index("c")
        me = 4 * x + 2 * y + c

        def piece(a, slot):
            return srcs[a] if self.modes[a] == "ag" else srcs[a].at[slot]

        local = [pltpu.make_async_copy(piece(a, me), outs[a].at[me], loc_sems.at[a]) for a in range(self.n)]
        sends, recvs = [], []
        for r in range(1, NDEV):
            px = 1 - x if (r >> 2) & 1 else x
            py = 1 - y if (r >> 1) & 1 else y
            pc = 1 - c if r & 1 else c
            pidx = 4 * px + 2 * py + pc
            for a in range(self.n):
                sem = dict(send_sem=send_sems.at[a, r - 1], recv_sem=recv_sems.at[a, r - 1],
                           device_id=(px, py, pc), device_id_type=MESH)
                sends.append(pltpu.make_async_remote_copy(src_ref=piece(a, pidx), dst_ref=outs[a].at[me], **sem))
                recvs.append(pltpu.make_async_remote_copy(src_ref=piece(a, pidx), dst_ref=outs[a].at[pidx], **sem))
        return local, sends, recvs

    def start(self, srcs, outs, sems):
        local, sends, _ = self._copies(srcs, outs, sems)
        for cp in local + sends:
            cp.start()

    def wait(self, srcs, outs, sems):
        local, sends, recvs = self._copies(srcs, outs, sems)
        for cp in recvs:
            cp.wait_recv()
        for cp in sends:
            cp.wait_send()
        for cp in local:
            cp.wait()


def _exchange(items, name):
    ex = _Exchange(items)
    n = ex.n

    def body(*refs):
        srcs, outs, sems = refs[:n], refs[n:2 * n], refs[2 * n:]
        ex.start(srcs, outs, sems)
        ex.wait(srcs, outs, sems)

    return pl.pallas_call(
        body, name=name,
        out_shape=tuple(ex.out_shape),
        in_specs=[ANY] * n, out_specs=tuple([ANY] * n),
        scratch_shapes=ex.scratch,
        compiler_params=pltpu.CompilerParams(has_side_effects=True),
    )(*ex.arrays)


def _mm(a, b, mode, out_dtype, tm, tn, name):
    if mode == "nn":
        (m, k), n = a.shape, b.shape[1]
        a_spec = pl.BlockSpec((tm, k), lambda i, j: (i, 0))
        b_spec = pl.BlockSpec((k, tn), lambda i, j: (0, j))
        dims = NN
    elif mode == "nt":
        (m, k), n = a.shape, b.shape[0]
        a_spec = pl.BlockSpec((tm, k), lambda i, j: (i, 0))
        b_spec = pl.BlockSpec((tn, k), lambda i, j: (j, 0))
        dims = NT
    else:
        (k, m), n = a.shape, b.shape[1]
        a_spec = pl.BlockSpec((k, tm), lambda i, j: (0, i))
        b_spec = pl.BlockSpec((k, tn), lambda i, j: (0, j))
        dims = TN
    assert m % tm == 0 and n % tn == 0, (m, n, tm, tn)

    def body(a_ref, b_ref, o_ref):
        o_ref[...] = _dot(a_ref[...], b_ref[...], dims).astype(o_ref.dtype)

    return pl.pallas_call(
        body, name=name, grid=(m // tm, n // tn),
        in_specs=[a_spec, b_spec], out_specs=pl.BlockSpec((tm, tn), lambda i, j: (i, j)),
        out_shape=jax.ShapeDtypeStruct((m, n), out_dtype),
        compiler_params=_params(("parallel", "parallel"), 48),
    )(a, b)


def _shift_down(x, k, fill):
    row = lax.broadcasted_iota(jnp.int32, x.shape, 0)
    y = pltpu.roll(x, k, 0)
    for t in range(k):
        y = jnp.where(row == t, fill[t], y)
    return y


def _shift_up(x, k, fill):
    n = x.shape[0]
    row = lax.broadcasted_iota(jnp.int32, x.shape, 0)
    y = pltpu.roll(x, n - k, 0)
    for t in range(k):
        y = jnp.where(row == n - k + t, fill[t], y)
    return y


def _conv_taps(x, halo, w):
    if halo is None:
        f1, f2 = [0.0], [0.0, 0.0]
    else:
        f1, f2 = [halo[7:8, :]], [halo[6:7, :], halo[7:8, :]]
    s1 = _shift_down(x, 1, f1)
    s2 = _shift_down(x, 2, f2)
    u = w[2:3, :] * x + w[1:2, :] * s1 + w[0:1, :] * s2
    return u, s1, s2


def _conv_taps_t(du, nxt, w):
    if nxt is None:
        f1, f2 = [0.0], [0.0, 0.0]
    else:
        f1, f2 = [nxt[0:1, :]], [nxt[0:1, :], nxt[1:2, :]]
    return w[2:3, :] * du + w[1:2, :] * _shift_up(du, 1, f1) + w[0:1, :] * _shift_up(du, 2, f2)


def _ada_fwd(c_all, w_ada, b_my):
    def body(c_ref, w_ref, b_ref, o_ref):
        cv = c_ref[...]
        act = cv * _sigmoid(cv)
        o_ref[...] = _dot(act, w_ref[...], NN, lax.Precision.HIGHEST) + b_ref[...]

    return pl.pallas_call(
        body, name="ada_fwd",
        out_shape=jax.ShapeDtypeStruct((NDEV, w_ada.shape[1]), F32),
        compiler_params=_params(None, 32),
    )(c_all, w_ada, b_my)


TR = 256


def _row_spec(width, col=0):
    return pl.BlockSpec((TR, width), lambda i, col=col: (i, col))


def _full_spec(shape):
    return pl.BlockSpec(shape, lambda i: (0,) * len(shape))


def _norm_mod_fwd(x, mod, g):
    s = x.shape[0]

    def body(x_ref, mod_ref, g_ref, h_ref):
        xv = x_ref[...]
        r = lax.rsqrt(jnp.mean(xv * xv, axis=-1, keepdims=True) + EPS)
        nrm = xv * r * g_ref[...]
        h_ref[...] = (nrm * (1.0 + mod_ref[1:2, :]) + mod_ref[0:1, :]).astype(BF16)

    return pl.pallas_call(
        body, name="norm1_fwd", grid=(s // TR,),
        in_specs=[_row_spec(D), _full_spec((SUB, D)), _full_spec((1, D))],
        out_specs=_row_spec(D), out_shape=jax.ShapeDtypeStruct((s, D), BF16),
        compiler_params=_params(("parallel",)),
    )(x, mod, g)


def _qkv_prep(proj, gq, gk):
    s = proj.shape[0]

    def body(q_ref, k_ref, v_ref, gq_ref, gk_ref, qo_ref, ko_ref, vo_ref):
        for h in range(HEADS):
            sl = slice(DH * h, DH * (h + 1))
            qh = q_ref[:, sl]
            r = lax.rsqrt(jnp.mean(qh * qh, axis=-1, keepdims=True) + EPS)
            qo_ref[:, sl] = (qh * r * gq_ref[...] * QK_SCALE).astype(BF16)
            kh = k_ref[:, sl]
            r = lax.rsqrt(jnp.mean(kh * kh, axis=-1, keepdims=True) + EPS)
            ko_ref[:, sl] = (kh * r * gk_ref[...]).astype(BF16)
        vo_ref[...] = v_ref[...].astype(BF16)

    o = jax.ShapeDtypeStruct((s, AW), BF16)
    return pl.pallas_call(
        body, name="qkv_prep", grid=(s // TR,),
        in_specs=[_row_spec(AW, 0), _row_spec(AW, 1), _row_spec(AW, 2), _full_spec((1, DH)), _full_spec((1, DH))],
        out_specs=(_row_spec(AW), _row_spec(AW), _row_spec(AW)), out_shape=(o, o, o),
        compiler_params=_params(("parallel",)),
    )(proj, proj, proj, gq, gk)


FG_BLOCK = (3 * AW + 3 * CW) // LANES


def _fgate_fwd(proj, bf_pad):
    s = proj.shape[0]

    def body(fg_ref, b_ref, o_ref, carry_ref):
        i = pl.program_id(0)

        @pl.when(i == 0)
        def _():
            carry_ref[...] = jnp.zeros_like(carry_ref)

        z = fg_ref[...] + b_ref[...]
        logf = jnp.minimum(z, 0.0) - jnp.log1p(jnp.exp(-jnp.abs(z)))
        row = lax.broadcasted_iota(jnp.int32, (TR, TR), 0)
        col = lax.broadcasted_iota(jnp.int32, (TR, TR), 1)
        tri = (col <= row).astype(F32)
        cs = _dot(tri, logf, NN, lax.Precision.HIGHEST) + carry_ref[0:1, :]
        o_ref[...] = cs
        carry_ref[...] = jnp.broadcast_to(cs[TR - 1:TR, :], carry_ref.shape)

    return pl.pallas_call(
        body, name="fgate_fwd", grid=(s // TR,),
        in_specs=[_row_spec(LANES, FG_BLOCK), _full_spec((1, LANES))],
        out_specs=_row_spec(LANES), out_shape=jax.ShapeDtypeStruct((s, LANES), F32),
        scratch_shapes=[pltpu.VMEM((SUB, LANES), F32)],
        compiler_params=_params(("arbitrary",)),
    )(proj, bf_pad)


def _fgate_bwd(dfcol, proj, bf_pad):
    s = proj.shape[0]
    nb = s // TR

    def body(df_ref, fg_ref, b_ref, o_ref, db_ref, carry_ref):
        i = pl.program_id(0)

        @pl.when(i == 0)
        def _():
            carry_ref[...] = jnp.zeros_like(carry_ref)
            db_ref[...] = jnp.zeros_like(db_ref)

        row = lax.broadcasted_iota(jnp.int32, (TR, TR), 0)
        col = lax.broadcasted_iota(jnp.int32, (TR, TR), 1)
        tri = (col >= row).astype(F32)
        dlogf = _dot(tri, df_ref[...], NN, lax.Precision.HIGHEST) + carry_ref[0:1, :]
        carry_ref[...] = jnp.broadcast_to(dlogf[0:1, :], carry_ref.shape)
        z = fg_ref[...] + b_ref[...]
        dfg = dlogf * _sigmoid(-z)
        o_ref[...] = dfg.astype(BF16)
        db_ref[0:1, :] += jnp.sum(dfg, axis=0, keepdims=True)

    rev = lambda col: pl.BlockSpec((TR, LANES), lambda i, col=col: (nb - 1 - i, col))
    return pl.pallas_call(
        body, name="fgate_bwd", grid=(nb,),
        in_specs=[rev(0), rev(FG_BLOCK), _full_spec((1, LANES))],
        out_specs=(rev(0), _full_spec((SUB, LANES))),
        out_shape=(jax.ShapeDtypeStruct((s, LANES), BF16), jax.ShapeDtypeStruct((SUB, LANES), F32)),
        scratch_shapes=[pltpu.VMEM((SUB, LANES), F32)],
        compiler_params=_params(("arbitrary",)),
    )(dfcol, proj, bf_pad)


def _resid_norm2(x, z, mod, g):
    s = x.shape[0]

    def body(x_ref, z_ref, mod_ref, g_ref, x1_ref, h_ref):
        x1 = x_ref[...] + mod_ref[2:3, :] * z_ref[...]
        x1_ref[...] = x1
        r = lax.rsqrt(jnp.mean(x1 * x1, axis=-1, keepdims=True) + EPS)
        nrm = x1 * r * g_ref[...]
        h_ref[...] = (nrm * (1.0 + mod_ref[4:5, :]) + mod_ref[3:4, :]).astype(BF16)

    return pl.pallas_call(
        body, name="resid_norm2", grid=(s // TR,),
        in_specs=[_row_spec(D), _row_spec(D), _full_spec((SUB, D)), _full_spec((1, D))],
        out_specs=(_row_spec(D), _row_spec(D)),
        out_shape=(jax.ShapeDtypeStruct((s, D), F32), jax.ShapeDtypeStruct((s, D), BF16)),
        compiler_params=_params(("parallel",)),
    )(x, z, mod, g)


def _loss_head(x1, y, tgt, mod):
    s = x1.shape[0]

    def body(x1_ref, y_ref, t_ref, mod_ref, dout_ref, dy_ref, vec_ref):
        @pl.when(pl.program_id(0) == 0)
        def _():
            vec_ref[...] = jnp.zeros_like(vec_ref)

        yv = y_ref[...]
        g2 = mod_ref[5:6, :]
        diff = x1_ref[...] + g2 * yv - t_ref[...]
        dout = diff * (1.0 / D)
        dout_ref[...] = dout
        dy_ref[...] = (g2 * dout).astype(BF16)
        vec_ref[0:1, :] += jnp.sum(dout * yv, axis=0, keepdims=True)
        vec_ref[1:2, :] += jnp.sum(diff * diff, axis=0, keepdims=True)

    return pl.pallas_call(
        body, name="loss_head", grid=(s // TR,),
        in_specs=[_row_spec(D), _row_spec(D), _row_spec(D), _full_spec((SUB, D))],
        out_specs=(_row_spec(D), _row_spec(D), _full_spec((SUB, D))),
        out_shape=(jax.ShapeDtypeStruct((s, D), F32), jax.ShapeDtypeStruct((s, D), BF16),
                   jax.ShapeDtypeStruct((SUB, D), F32)),
        compiler_params=_params(("arbitrary",)),
    )(x1, y, tgt, mod)


def _norm_mod_bwd(dh, xin, dres, zin, mod, g, shift_row, scale_row, gate_row, name):
    s = dh.shape[0]
    with_gate = gate_row is not None

    def body(*refs):
        if with_gate:
            dh_ref, x_ref, dres_ref, z_ref, mod_ref, g_ref, dx_ref, dz_ref, vec_ref = refs
        else:
            dh_ref, x_ref, dres_ref, mod_ref, g_ref, dx_ref, vec_ref = refs

        @pl.when(pl.program_id(0) == 0)
        def _():
            vec_ref[...] = jnp.zeros_like(vec_ref)

        xv = x_ref[...]
        dhv = dh_ref[...]
        gv = g_ref[...]
        r = lax.rsqrt(jnp.mean(xv * xv, axis=-1, keepdims=True) + EPS)
        xh = xv * r
        dn = dhv * (1.0 + mod_ref[scale_row:scale_row + 1, :])
        dxh = dn * gv
        dx = dres_ref[...] + r * (dxh - xh * jnp.mean(dxh * xh, axis=-1, keepdims=True))
        dx_ref[...] = dx
        vec_ref[0:1, :] += jnp.sum(dhv, axis=0, keepdims=True)
        vec_ref[1:2, :] += jnp.sum(dhv * (xh * gv), axis=0, keepdims=True)
        vec_ref[2:3, :] += jnp.sum(dn * xh, axis=0, keepdims=True)
        if with_gate:
            dz_ref[...] = (mod_ref[gate_row:gate_row + 1, :] * dx).astype(BF16)
            vec_ref[3:4, :] += jnp.sum(dx * z_ref[...], axis=0, keepdims=True)

    ins = [dh, xin, dres] + ([zin] if with_gate else []) + [mod, g]
    in_specs = [_row_spec(D)] * (4 if with_gate else 3) + [_full_spec((SUB, D)), _full_spec((1, D))]
    out_specs = [_row_spec(D)] + ([_row_spec(D)] if with_gate else []) + [_full_spec((SUB, D))]
    out_shape = [jax.ShapeDtypeStruct((s, D), F32)] + ([jax.ShapeDtypeStruct((s, D), BF16)] if with_gate else []) \
        + [jax.ShapeDtypeStruct((SUB, D), F32)]
    return pl.pallas_call(
        body, name=name, grid=(s // TR,),
        in_specs=in_specs, out_specs=tuple(out_specs), out_shape=tuple(out_shape),
        compiler_params=_params(("arbitrary",)),
    )(*ins)


XIN_BLOCK = 3 * AW // LANES
BG_BLOCK = XIN_BLOCK + CW // LANES
CG_BLOCK = BG_BLOCK + CW // LANES


def _seq_spec(s, first_block):
    return pl.BlockSpec((s, LANES), lambda j, fb=first_block: (0, fb + j))


def _mixconv_fwd(proj, w):
    s = proj.shape[0]

    def body(xin_ref, bg_ref, cg_ref, w_ref, o_ref):
        cx = cg_ref[...] * xin_ref[...]
        cv, _, _ = _conv_taps(cx, None, w_ref[...])
        o_ref[...] = bg_ref[...] * cv

    return pl.pallas_call(
        body, name="mixconv_fwd", grid=(CW // LANES,),
        in_specs=[_seq_spec(s, XIN_BLOCK), _seq_spec(s, BG_BLOCK), _seq_spec(s, CG_BLOCK),
                  pl.BlockSpec((3, LANES), lambda j: (0, j))],
        out_specs=_seq_spec(s, 0), out_shape=jax.ShapeDtypeStruct((s, CW), F32),
        compiler_params=_params(("parallel",), 48),
    )(proj, proj, proj, w)


def _mixconv_bwd(dmixed, proj, w):
    s = proj.shape[0]

    def body(d_ref, xin_ref, bg_ref, cg_ref, w_ref, dxin_ref, dbg_ref, dcg_ref, dw_ref):
        wv = w_ref[...]
        xin, cg, dconv = xin_ref[...], cg_ref[...], d_ref[...]
        cx = cg * xin
        cv, s1, s2 = _conv_taps(cx, None, wv)
        dbg_ref[...] = (dconv * cv).astype(BF16)
        dcv = dconv * bg_ref[...]
        dw_ref[...] = jnp.zeros_like(dw_ref)
        dw_ref[0:1, :] = jnp.sum(dcv * s2, axis=0, keepdims=True)
        dw_ref[1:2, :] = jnp.sum(dcv * s1, axis=0, keepdims=True)
        dw_ref[2:3, :] = jnp.sum(dcv * cx, axis=0, keepdims=True)
        dcx = _conv_taps_t(dcv, None, wv)
        dcg_ref[...] = (dcx * xin).astype(BF16)
        dxin_ref[...] = (dcx * cg).astype(BF16)

    o = jax.ShapeDtypeStruct((s, CW), BF16)
    return pl.pallas_call(
        body, name="mixconv_bwd", grid=(CW // LANES,),
        in_specs=[_seq_spec(s, AW // LANES), _seq_spec(s, XIN_BLOCK), _seq_spec(s, BG_BLOCK), _seq_spec(s, CG_BLOCK),
                  pl.BlockSpec((3, LANES), lambda j: (0, j))],
        out_specs=(_seq_spec(s, 0), _seq_spec(s, 0), _seq_spec(s, 0), pl.BlockSpec((SUB, LANES), lambda j: (0, j))),
        out_shape=(o, o, o, jax.ShapeDtypeStruct((SUB, CW), F32)),
        compiler_params=_params(("parallel",), 48),
    )(dmixed, proj, proj, proj, w)


TA = 256
NEG = -1e30


def _causal_mask():
    row = lax.broadcasted_iota(jnp.int32, (TA, TA), 0)
    col = lax.broadcasted_iota(jnp.int32, (TA, TA), 1)
    return col <= row


def _attn_fwd(qs, kn, vb, fcol, frow, hosted):
    s = qs.shape[0]
    nq = s // TA
    ex = _Exchange(hosted)
    n = ex.n

    def body(*refs):
        q_ref, k_ref, v_ref, fc_ref, fr_ref = refs[:5]
        srcs = refs[5:5 + n]
        o_ref, lse_ref = refs[5 + n:7 + n]
        outs, sems = refs[7 + n:7 + 2 * n], refs[7 + 2 * n:]
        i = pl.program_id(1)
        first = jnp.logical_and(pl.program_id(0) == 0, i == 0)
        last = jnp.logical_and(pl.program_id(0) == HEADS // 2 - 1, i == nq - 1)

        @pl.when(first)
        def _():
            ex.start(srcs, outs, sems)

        for hh in range(2):
            sl = slice(DH * hh, DH * (hh + 1))
            q = q_ref[:, sl]
            fi = fc_ref[0, :, hh:hh + 1]

            def block(j, carry, masked):
                m, l, acc = carry
                j0 = pl.multiple_of(j * TA, TA)
                k = k_ref[pl.ds(j0, TA), sl]
                v = v_ref[pl.ds(j0, TA), sl]
                fj = fr_ref[0, hh, pl.ds(j, 1), :]
                sc = _dot(q, k, NT) + fi - fj
                if masked:
                    sc = jnp.where(_causal_mask(), sc, NEG)
                m_new = jnp.maximum(m, jnp.max(sc, axis=-1, keepdims=True))
                alpha = jnp.exp(m - m_new)
                p = jnp.exp(sc - m_new)
                l = alpha * l + jnp.sum(p, axis=-1, keepdims=True)
                acc = alpha * acc + _dot(p.astype(BF16), v)
                return m_new, l, acc

            init = (jnp.full((TA, 1), NEG, F32), jnp.zeros((TA, 1), F32), jnp.zeros((TA, DH), F32))
            carry = lax.fori_loop(0, i, lambda j, cr: block(j, cr, False), init)
            m, l, acc = block(i, carry, True)
            o_ref[:, sl] = acc / l
            lse_ref[0, :, hh:hh + 1] = m + jnp.log(l)

        @pl.when(last)
        def _():
            ex.wait(srcs, outs, sems)

    res = pl.pallas_call(
        body, name="attn_fwd", grid=(HEADS // 2, nq),
        in_specs=[pl.BlockSpec((TA, LANES), lambda p, i: (i, p)),
                  pl.BlockSpec((s, LANES), lambda p, i: (0, p)),
                  pl.BlockSpec((s, LANES), lambda p, i: (0, p)),
                  pl.BlockSpec((1, TA, 2), lambda p, i: (p, i, 0)),
                  pl.BlockSpec((1, 2, nq, TA), lambda p, i: (p, 0, 0, 0))] + [ANY] * n,
        out_specs=(pl.BlockSpec((TA, LANES), lambda p, i: (i, p)),
                   pl.BlockSpec((1, TA, 2), lambda p, i: (p, i, 0))) + tuple([ANY] * n),
        out_shape=(jax.ShapeDtypeStruct((s, AW), F32), jax.ShapeDtypeStruct((HEADS // 2, s, 2), F32))
        + tuple(ex.out_shape),
        scratch_shapes=ex.scratch,
        compiler_params=_params(("arbitrary", "arbitrary"), 32),
    )(qs, kn, vb, fcol, frow, *ex.arrays)
    return res[0], res[1], res[2:]


def _attn_bwd(qs, kn, vb, dmixed, o, lse, fcol, frow, hosted):
    s = qs.shape[0]
    nq = s // TA
    ex = _Exchange(hosted)
    n = ex.n

    def body(*refs):
        q_ref, k_ref, v_ref, do_ref, o_ref, lse_ref, fc_ref, fr_ref = refs[:8]
        srcs = refs[8:8 + n]
        dq_ref, dk_ref, dv_ref, df_ref, dfq_ref = refs[8 + n:13 + n]
        outs, sems = refs[13 + n:13 + 2 * n], refs[13 + 2 * n:]

        @pl.when(pl.program_id(0) == 0)
        def _():
            ex.start(srcs, outs, sems)

        dk_ref[...] = jnp.zeros_like(dk_ref)
        dv_ref[...] = jnp.zeros_like(dv_ref)
        df_ref[...] = jnp.zeros_like(df_ref)
        for hh in range(2):
            sl = slice(DH * hh, DH * (hh + 1))

            def q_block(i, _):
                i0 = pl.multiple_of(i * TA, TA)
                q = q_ref[pl.ds(i0, TA), sl]
                do = do_ref[pl.ds(i0, TA), sl]
                delta = jnp.sum(do * o_ref[pl.ds(i0, TA), sl], axis=-1, keepdims=True)
                dob = do.astype(BF16)
                lse_i = lse_ref[0, pl.ds(i0, TA), hh:hh + 1]
                fi = fc_ref[0, pl.ds(i0, TA), hh:hh + 1]

                def block(j, carry, masked):
                    dq, rsum = carry
                    j0 = pl.multiple_of(j * TA, TA)
                    k = k_ref[pl.ds(j0, TA), sl]
                    v = v_ref[pl.ds(j0, TA), sl]
                    fj = fr_ref[0, hh, pl.ds(j, 1), :]
                    sc = _dot(q, k, NT) + fi - fj
                    if masked:
                        sc = jnp.where(_causal_mask(), sc, NEG)
                    p = jnp.exp(sc - lse_i)
                    dv_ref[pl.ds(j0, TA), sl] += _dot(p.astype(BF16), dob, TN)
                    dp = _dot(dob, v, NT)
                    ds = p * (dp - delta)
                    df_ref[0, hh, pl.ds(j, 1), :] -= jnp.sum(ds, axis=0, keepdims=True)
                    dsb = ds.astype(BF16)
                    dk_ref[pl.ds(j0, TA), sl] += _dot(dsb, q, TN)
                    return dq + _dot(dsb, k), rsum + jnp.sum(ds, axis=-1, keepdims=True)

                init = (jnp.zeros((TA, DH), F32), jnp.zeros((TA, 1), F32))
                carry = lax.fori_loop(0, i, lambda j, acc: block(j, acc, False), init)
                dq, rsum = block(i, carry, True)
                dq_ref[pl.ds(i0, TA), sl] = dq
                dfq_ref[0, pl.ds(i0, TA), hh:hh + 1] = rsum
                return 0

            lax.fori_loop(0, nq, q_block, 0)

        @pl.when(pl.program_id(0) == HEADS // 2 - 1)
        def _():
            ex.wait(srcs, outs, sems)

    pair = lambda p: (0, p)
    seq = pl.BlockSpec((s, LANES), pair)
    small = pl.BlockSpec((1, s, 2), lambda p: (p, 0, 0))
    rows = pl.BlockSpec((1, 2, nq, TA), lambda p: (p, 0, 0, 0))
    o32 = jax.ShapeDtypeStruct((s, AW), F32)
    res = pl.pallas_call(
        body, name="attn_bwd", grid=(HEADS // 2,),
        in_specs=[seq, seq, seq, seq, seq, small, small, rows] + [ANY] * n,
        out_specs=(seq, seq, seq, rows, small) + tuple([ANY] * n),
        out_shape=(o32, o32, o32, jax.ShapeDtypeStruct((HEADS // 2, 2, nq, TA), F32),
                   jax.ShapeDtypeStruct((HEADS // 2, s, 2), F32)) + tuple(ex.out_shape),
        scratch_shapes=ex.scratch,
        compiler_params=_params(("arbitrary",), 48),
    )(qs, kn, vb, dmixed, o, lse, fcol, frow, *ex.arrays)
    return res[:5], res[5:]


def _qkv_post(dqs, dkn, dv, proj, gq, gk):
    s = proj.shape[0]

    def body(dq_ref, dk_ref, dv_ref, q_ref, k_ref, gq_ref, gk_ref, dqo_ref, dko_ref, dvo_ref, vec_ref):
        @pl.when(pl.program_id(0) == 0)
        def _():
            vec_ref[...] = jnp.zeros_like(vec_ref)

        def one(d_ref, x_ref, g_ref, o_ref, row, scale):
            dg = jnp.zeros((1, DH), F32)
            for h in range(HEADS):
                sl = slice(DH * h, DH * (h + 1))
                xv = x_ref[:, sl]
                r = lax.rsqrt(jnp.mean(xv * xv, axis=-1, keepdims=True) + EPS)
                xh = xv * r
                dn = d_ref[:, sl] * scale
                dg = dg + jnp.sum(dn * xh, axis=0, keepdims=True)
                dxh = dn * g_ref[...]
                o_ref[:, sl] = (r * (dxh - xh * jnp.mean(dxh * xh, axis=-1, keepdims=True))).astype(BF16)
            vec_ref[row:row + 1, 0:DH] += dg

        one(dq_ref, q_ref, gq_ref, dqo_ref, 0, QK_SCALE)
        one(dk_ref, k_ref, gk_ref, dko_ref, 1, 1.0)
        dvo_ref[...] = dv_ref[...].astype(BF16)

    o = jax.ShapeDtypeStruct((s, AW), BF16)
    return pl.pallas_call(
        body, name="qkv_post", grid=(s // TR,),
        in_specs=[_row_spec(AW), _row_spec(AW), _row_spec(AW), _row_spec(AW, 0), _row_spec(AW, 1),
                  _full_spec((1, DH)), _full_spec((1, DH))],
        out_specs=(_row_spec(AW), _row_spec(AW), _row_spec(AW), _full_spec((SUB, LANES))),
        out_shape=(o, o, o, jax.ShapeDtypeStruct((SUB, LANES), F32)),
        compiler_params=_params(("arbitrary",)),
    )(dqs, dkn, dv, proj, proj, gq, gk)


TF = 256
NJ = DFF // TF
FFN_ROWS_FWD = 1024
FFN_ROWS_BWD = 512


def _ffn_fwd(h2, wup, cw, wd):
    s = h2.shape[0]
    tr = FFN_ROWS_FWD
    nr = s // tr

    def body(h_ref, wg_ref, wv_ref, cg_ref, cv_ref, wd_ref, pg_ref, pv_ref, y_ref, halo_ref):
        r, j = pl.program_id(0), pl.program_id(1)
        hv = h_ref[...]
        pg = _dot(hv, wg_ref[...]).astype(BF16)
        pv = _dot(hv, wv_ref[...]).astype(BF16)
        pg_ref[...] = pg
        pv_ref[...] = pv
        pgf, pvf = pg.astype(F32), pv.astype(F32)
        ug, _, _ = _conv_taps(pgf, jnp.where(r > 0, halo_ref[j, 0], 0.0), cg_ref[...])
        uv, _, _ = _conv_taps(pvf, jnp.where(r > 0, halo_ref[j, 1], 0.0), cv_ref[...])
        halo_ref[j, 0] = pgf[tr - SUB:tr, :]
        halo_ref[j, 1] = pvf[tr - SUB:tr, :]
        act = (ug * _sigmoid(ug) * uv).astype(BF16)
        contrib = _dot(act, wd_ref[...])

        @pl.when(j == 0)
        def _():
            y_ref[...] = contrib

        @pl.when(j > 0)
        def _():
            y_ref[...] += contrib

    pre = jax.ShapeDtypeStruct((s, DFF), BF16)
    return pl.pallas_call(
        body, name="ffn_fwd", grid=(nr, NJ),
        in_specs=[pl.BlockSpec((tr, D), lambda r, j: (r, 0)),
                  pl.BlockSpec((D, TF), lambda r, j: (0, j)),
                  pl.BlockSpec((D, TF), lambda r, j: (0, NJ + j)),
                  pl.BlockSpec((3, TF), lambda r, j: (0, j)),
                  pl.BlockSpec((3, TF), lambda r, j: (0, NJ + j)),
                  pl.BlockSpec((TF, D), lambda r, j: (j, 0))],
        out_specs=(pl.BlockSpec((tr, TF), lambda r, j: (r, j)),
                   pl.BlockSpec((tr, TF), lambda r, j: (r, j)),
                   pl.BlockSpec((tr, D), lambda r, j: (r, 0))),
        out_shape=(pre, pre, jax.ShapeDtypeStruct((s, D), F32)),
        scratch_shapes=[pltpu.VMEM((NJ, 2, SUB, TF), F32)],
        compiler_params=_params(("arbitrary", "arbitrary"), 56),
    )(h2, wup, wup, cw, cw, wd)


def _ffn_bwd(dy, h2, pre_g, pre_v, wup, cw, wd):
    s = h2.shape[0]
    tr = FFN_ROWS_BWD
    nr = s // tr
    hb = tr // (2 * SUB)

    def body(dy_ref, h_ref, pg_ref, pv_ref, hg_ref, hv_ref, wg_ref, wv_ref, cg_ref, cv_ref, wd_ref,
             dh_ref, dwg_ref, dwv_ref, dwd_ref, dcg_ref, dcv_ref, nxt_ref, awg_ref, awv_ref, awd_ref):
        j, r = pl.program_id(0), pl.program_id(1)
        rr = nr - 1 - r
        row0 = pl.multiple_of(rr * tr, tr)
        cwg, cwv = cg_ref[...], cv_ref[...]
        pg, pv = pg_ref[...].astype(F32), pv_ref[...].astype(F32)
        ug, g1, g2 = _conv_taps(pg, jnp.where(rr > 0, hg_ref[SUB:2 * SUB, :].astype(F32), 0.0), cwg)
        uv, v1, v2 = _conv_taps(pv, jnp.where(rr > 0, hv_ref[SUB:2 * SUB, :].astype(F32), 0.0), cwv)
        sg = _sigmoid(ug)
        sil = ug * sg
        act = (sil * uv).astype(BF16)
        dyv = dy_ref[...]
        da = _dot(dyv, wd_ref[...], NT)
        dug = da * uv * (sg * (1.0 + ug * (1.0 - sg)))
        duv = da * sil
        dpg = _conv_taps_t(dug, jnp.where(r > 0, nxt_ref[0], 0.0), cwg)
        dpv = _conv_taps_t(duv, jnp.where(r > 0, nxt_ref[1], 0.0), cwv)
        nxt_ref[0] = dug[0:SUB, :]
        nxt_ref[1] = duv[0:SUB, :]
        dpgb, dpvb = dpg.astype(BF16), dpv.astype(BF16)
        hv = h_ref[...]
        dwd = _dot(act, dyv, TN)
        dwg = _dot(hv, dpgb, TN)
        dwv = _dot(hv, dpvb, TN)
        dh = _dot(dpgb, wg_ref[...], NT) + _dot(dpvb, wv_ref[...], NT)

        def taps(du, x0, x1, x2):
            return (jnp.sum(du * x2, axis=0, keepdims=True), jnp.sum(du * x1, axis=0, keepdims=True),
                    jnp.sum(du * x0, axis=0, keepdims=True))

        tg, tv = taps(dug, pg, g1, g2), taps(duv, pv, v1, v2)

        @pl.when(r == 0)
        def _():
            awd_ref[...] = dwd
            awg_ref[...] = dwg
            awv_ref[...] = dwv
            dcg_ref[...] = jnp.zeros_like(dcg_ref)
            dcv_ref[...] = jnp.zeros_like(dcv_ref)

        @pl.when(r > 0)
        def _():
            awd_ref[...] += dwd
            awg_ref[...] += dwg
            awv_ref[...] += dwv

        @pl.when(r == nr - 1)
        def _():
            dwd_ref[...] = awd_ref[...].astype(BF16)
            dwg_ref[...] = awg_ref[...].astype(BF16)
            dwv_ref[...] = awv_ref[...].astype(BF16)

        for t in range(3):
            dcg_ref[t:t + 1, :] += tg[t]
            dcv_ref[t:t + 1, :] += tv[t]

        @pl.when(j == 0)
        def _():
            dh_ref[pl.ds(row0, tr), :] = dh

        @pl.when(j > 0)
        def _():
            dh_ref[pl.ds(row0, tr), :] += dh

    rows = lambda j, r: (nr - 1 - r, 0)
    tile = lambda j, r: (nr - 1 - r, j)
    halo = lambda j, r: (jnp.maximum((nr - 1 - r) * hb - 1, 0), j)
    return pl.pallas_call(
        body, name="ffn_bwd", grid=(NJ, nr),
        in_specs=[pl.BlockSpec((tr, D), rows), pl.BlockSpec((tr, D), rows),
                  pl.BlockSpec((tr, TF), tile), pl.BlockSpec((tr, TF), tile),
                  pl.BlockSpec((2 * SUB, TF), halo), pl.BlockSpec((2 * SUB, TF), halo),
                  pl.BlockSpec((D, TF), lambda j, r: (0, j)), pl.BlockSpec((D, TF), lambda j, r: (0, NJ + j)),
                  pl.BlockSpec((3, TF), lambda j, r: (0, j)), pl.BlockSpec((3, TF), lambda j, r: (0, NJ + j)),
                  pl.BlockSpec((TF, D), lambda j, r: (j, 0))],
        out_specs=(pl.BlockSpec((s, D), lambda j, r: (0, 0)),
                   pl.BlockSpec((D, TF), lambda j, r: (0, j)), pl.BlockSpec((D, TF), lambda j, r: (0, j)),
                   pl.BlockSpec((TF, D), lambda j, r: (j, 0)),
                   pl.BlockSpec((SUB, TF), lambda j, r: (0, j)), pl.BlockSpec((SUB, TF), lambda j, r: (0, j))),
        out_shape=(jax.ShapeDtypeStruct((s, D), F32),
                   jax.ShapeDtypeStruct((D, DFF), BF16), jax.ShapeDtypeStruct((D, DFF), BF16),
                   jax.ShapeDtypeStruct((DFF, D), BF16),
                   jax.ShapeDtypeStruct((SUB, DFF), F32), jax.ShapeDtypeStruct((SUB, DFF), F32)),
        scratch_shapes=[pltpu.VMEM((2, SUB, TF), F32), pltpu.VMEM((D, TF), F32), pltpu.VMEM((D, TF), F32),
                        pltpu.VMEM((TF, D), F32)],
        compiler_params=_params(("arbitrary", "arbitrary"), 56),
    )(dy, h2, pre_g, pre_v, pre_g, pre_v, wup, wup, cw, cw, wd)


def _adam(w, g, m, v):
    m = ADAM_B1 * m + (1.0 - ADAM_B1) * g
    v = ADAM_B2 * v + (1.0 - ADAM_B2) * (g * g)
    m_hat = m / (1.0 - ADAM_B1 ** ADAM_STEP)
    v_hat = v / (1.0 - ADAM_B2 ** ADAM_STEP)
    delta = -ADAM_LR * (m_hat / (jnp.sqrt(v_hat) + ADAM_EPS) + ADAM_WD * w)
    return delta, m, v


def _adamw_sharded(parts, w, m, v, tr, name):
    rws, cols = w.shape

    def body(p_ref, w_ref, m_ref, v_ref, g_ref, d_ref, mo_ref, vo_ref):
        g = p_ref[0].astype(F32)
        for d in range(1, NDEV):
            g = g + p_ref[d].astype(F32)
        g_ref[...] = g
        d_ref[...], mo_ref[...], vo_ref[...] = _adam(w_ref[...], g, m_ref[...], v_ref[...])

    blk = pl.BlockSpec((tr, cols), lambda i: (i, 0))
    o = jax.ShapeDtypeStruct((rws, cols), F32)
    return pl.pallas_call(
        body, name=name, grid=(rws // tr,),
        in_specs=[pl.BlockSpec((NDEV, tr, cols), lambda i: (0, i, 0)), blk, blk, blk],
        out_specs=(blk, blk, blk, blk), out_shape=(o, o, o, o),
        compiler_params=_params(("parallel",), 48),
    )(parts, w, m, v)


def _adamw_ada(c_all, dmod_my, w, m, v):
    rws, cols = w.shape
    tr = 256

    def body(c_ref, dm_ref, w_ref, m_ref, v_ref, g_ref, d_ref, mo_ref, vo_ref):
        cv = c_ref[...]
        act = cv * _sigmoid(cv)
        g = _dot(act, dm_ref[...], TN, lax.Precision.HIGHEST)
        g_ref[...] = g
        d_ref[...], mo_ref[...], vo_ref[...] = _adam(w_ref[...], g, m_ref[...], v_ref[...])

    blk = pl.BlockSpec((tr, cols), lambda i: (i, 0))
    o = jax.ShapeDtypeStruct((rws, cols), F32)
    return pl.pallas_call(
        body, name="adamw_ada", grid=(rws // tr,),
        in_specs=[pl.BlockSpec((NDEV, tr), lambda i: (0, i)), _full_spec((NDEV, cols)), blk, blk, blk],
        out_specs=(blk, blk, blk, blk), out_shape=(o, o, o, o),
        compiler_params=_params(("parallel",), 48),
    )(c_all, dmod_my, w, m, v)


REP_ROWS = 16
ROW_N1, ROW_N2, ROW_LOSS, ROW_MISC = 6, 7, 8, 9
LANE_BF, LANE_GQ, LANE_GK = 0, 128, 256


def _adamw_small(rep_all, conv_all, wmv):
    n_ff = wmv[6][0].shape[1]

    def body(*refs):
        rep_ref, conv_ref = refs[:2]
        ins = refs[2:2 + 24]
        outs = refs[2 + 24:]
        loss_ref, outs = outs[0], outs[1:]
        g_rep = rep_ref[0]
        g_conv = conv_ref[0]
        for d in range(1, NDEV):
            g_rep = g_rep + rep_ref[d]
            g_conv = g_conv + conv_ref[d]
        loss_ref[...] = (0.5 / D) * jnp.sum(g_rep[ROW_LOSS:ROW_LOSS + 1, :], axis=-1, keepdims=True)
        grads = [
            None,
            g_rep[ROW_N1:ROW_N1 + 1, :],
            g_rep[ROW_MISC:ROW_MISC + 1, LANE_BF:LANE_BF + HEADS],
            g_rep[ROW_MISC:ROW_MISC + 1, LANE_GQ:LANE_GQ + DH],
            g_rep[ROW_MISC:ROW_MISC + 1, LANE_GK:LANE_GK + DH],
            g_rep[ROW_N2:ROW_N2 + 1, :],
            g_conv[0:3, 0:n_ff],
            g_conv[0:3, n_ff:n_ff + DH],
        ]
        for p in range(8):
            w_ref, m_ref, v_ref = ins[3 * p:3 * p + 3]
            g_ref, d_ref, mo_ref, vo_ref = outs[4 * p:4 * p + 4]
            if p == 0:
                for nmod in range(NMOD):
                    sl = slice(D * nmod, D * (nmod + 1))
                    g = g_rep[nmod:nmod + 1, :]
                    g_ref[:, sl] = g
                    d_ref[:, sl], mo_ref[:, sl], vo_ref[:, sl] = _adam(w_ref[:, sl], g, m_ref[:, sl], v_ref[:, sl])
            else:
                g = grads[p]
                g_ref[...] = g
                d_ref[...], mo_ref[...], vo_ref[...] = _adam(w_ref[...], g, m_ref[...], v_ref[...])

    flat = [a for trio in wmv for a in trio]
    out_shape = [jax.ShapeDtypeStruct((1, 1), F32)]
    for trio in wmv:
        out_shape += [jax.ShapeDtypeStruct(trio[0].shape, F32)] * 4
    return pl.pallas_call(
        body, name="adamw_small", out_shape=tuple(out_shape),
        compiler_params=_params(None, 32),
    )(rep_all, conv_all, *flat)


IN_SPLITS = (AW, 2 * AW, 3 * AW, 3 * AW + HEADS, 3 * AW + HEADS + CW, 3 * AW + HEADS + 2 * CW)


def _assemble_w_in(g_in):
    w = jnp.transpose(g_in, (1, 0, 2)).reshape(D, DIN)
    q, k, v, fg, xin, bg, cg = jnp.split(w, IN_SPLITS, axis=1)
    return jnp.concatenate([q, k, v, xin, bg, cg, fg, jnp.zeros((D, DINP - DIN), w.dtype)], axis=1)


def _scatter_dw_in(dwp):
    q, k, v, xin, bg, cg, fg = jnp.split(dwp, (AW, 2 * AW, 3 * AW, 3 * AW + CW, 3 * AW + 2 * CW, 3 * AW + 3 * CW), axis=1)
    w = jnp.concatenate([q, k, v, fg[:, :HEADS], xin, bg, cg], axis=1)
    return jnp.transpose(w.reshape(D, NDEV, DIN // NDEV), (1, 0, 2))


def kernel(x, c, w_ada, b_ada, norm1_g, w_in, b_forget, q_norm_g, k_norm_g, conv_mix_w, w_out, norm2_g, w_up, ffn_conv_w, w_down, loss_target, m_w_ada, m_b_ada, m_norm1_g, m_w_in, m_b_forget, m_q_norm_g, m_k_norm_g, m_conv_mix_w, m_w_out, m_norm2_g, m_w_up, m_ffn_conv_w, m_w_down, v_w_ada, v_b_ada, v_norm1_g, v_w_in, v_b_forget, v_q_norm_g, v_k_norm_g, v_conv_mix_w, v_w_out, v_norm2_g, v_w_up, v_ffn_conv_w, v_w_down):
    me = 4 * lax.axis_index("x") + 2 * lax.axis_index("y") + lax.axis_index("c")
    xs, tgt = x[0], loss_target[0]
    s = xs.shape[0]
    nq = s // TA
    n_ada = w_ada.shape[2]
    n_ff = w_up.shape[2]

    conv_w = jnp.concatenate([ffn_conv_w[0], conv_mix_w[0]], axis=1)
    conv_w = jnp.concatenate([conv_w, jnp.zeros((SUB - 3, conv_w.shape[1]), F32)], axis=0)
    c_all, conv_all, g_in = _exchange(
        [(c.reshape(SUB, D // SUB), "ag"), (conv_w, "ag"), (w_in[0].astype(BF16), "ag")], "exchange_w_in")
    c_all = c_all.reshape(NDEV, D)
    cw_ffn = jnp.transpose(conv_all[:, :3, :n_ff], (1, 0, 2)).reshape(3, 2 * DFF)
    cw_mix = jnp.transpose(conv_all[:, :3, n_ff:], (1, 0, 2)).reshape(3, CW)
    w_in_p = _assemble_w_in(g_in)

    b_my = lax.dynamic_slice(b_ada, (0, me * n_ada), (1, n_ada))
    mod_part = _ada_fwd(c_all, w_ada[0], b_my)
    (mod_rows,) = _exchange([(jnp.broadcast_to(mod_part[:, None, :], (NDEV, SUB, n_ada)), "a2a")], "exchange_mod")
    mod = mod_rows[:, 0, :].reshape(NMOD, D)
    mod = jnp.concatenate([mod, jnp.zeros((SUB - NMOD, D), F32)], axis=0)

    h = _norm_mod_fwd(xs, mod, norm1_g)
    proj = _mm(h, w_in_p, "nn", F32, 512, 640, "proj_fwd")
    qs, kn, vb = _qkv_prep(proj, q_norm_g, k_norm_g)
    bf_pad = jnp.concatenate([b_forget, jnp.zeros((1, LANES - HEADS), F32)], axis=1)
    fcum = _fgate_fwd(proj, bf_pad)
    f8 = fcum[:, :HEADS]
    fcol = jnp.transpose(f8.reshape(s, HEADS // 2, 2), (1, 0, 2))
    frow = jnp.transpose(f8).reshape(HEADS // 2, 2, nq, TA)
    attn, lse, (g_out, g_up, g_down) = _attn_fwd(
        qs, kn, vb, fcol, frow,
        [(w_out[0].astype(BF16), "ag"), (w_up[0].astype(BF16), "ag"), (w_down[0].astype(BF16), "ag")])
    w_out_f = g_out.reshape(D, D)
    w_up_f = jnp.transpose(g_up, (1, 0, 2)).reshape(D, 2 * DFF)
    w_down_f = g_down.reshape(DFF, D)
    conv = _mixconv_fwd(proj, cw_mix)
    mixed = jnp.concatenate([attn, conv], axis=1).astype(BF16)
    z = _mm(mixed, w_out_f, "nn", F32, 512, 512, "out_fwd")
    x1, h2 = _resid_norm2(xs, z, mod, norm2_g)
    pre_g, pre_v, y = _ffn_fwd(h2, w_up_f, cw_ffn, w_down_f)
    dout, dy, vec_l = _loss_head(x1, y, tgt, mod)

    dh2, dwup_g, dwup_v, dwd, dcw_g, dcw_v = _ffn_bwd(dy, h2, pre_g, pre_v, w_up_f, cw_ffn, w_down_f)
    dx1, dz, vec_2 = _norm_mod_bwd(dh2, x1, dout, z, mod, norm2_g, 3, 4, 2, "norm2_bwd")
    dmixed = _mm(dz, w_out_f, "nt", F32, 512, 512, "out_bwd_x")
    dwout = _mm(mixed, dz, "tn", BF16, 512, 512, "out_bwd_w")
    dxin, dbg, dcg, dcw_mix = _mixconv_bwd(dmixed, proj, cw_mix)
    dwup_s = jnp.transpose(jnp.concatenate([dwup_g, dwup_v], axis=1).reshape(D, NDEV, n_ff), (1, 0, 2))
    (dqs, dkn, dv, dfrow, dfq), (p_out, p_up, p_down) = _attn_bwd(
        qs, kn, vb, dmixed, attn, lse, fcol, frow,
        [(dwout.reshape(NDEV, D // NDEV, D), "a2a"), (dwup_s, "a2a"), (dwd.reshape(NDEV, DFF // NDEV, D), "a2a")])
    dq, dk, dvb, vec_qk = _qkv_post(dqs, dkn, dv, proj, q_norm_g, k_norm_g)
    dfcol = jnp.transpose(dfrow.reshape(HEADS, s)) + jnp.transpose(dfq, (1, 0, 2)).reshape(s, HEADS)
    dfcol = jnp.concatenate([dfcol, jnp.zeros((s, LANES - HEADS), F32)], axis=1)
    dfg, vec_bf = _fgate_bwd(dfcol, proj, bf_pad)
    dproj = jnp.concatenate([dq, dk, dvb, dxin, dbg, dcg, dfg], axis=1)
    dh = _mm(dproj, w_in_p, "nt", F32, 512, 512, "proj_bwd_x")
    dwin_p = _mm(h, dproj, "tn", BF16, 512, 640, "proj_bwd_w")
    grad_x, vec_1 = _norm_mod_bwd(dh, xs, dx1, None, mod, norm1_g, 0, 1, None, "norm1_bwd")

    misc = jnp.zeros((1, D), F32)
    misc = lax.dynamic_update_slice(misc, vec_bf[0:1, :HEADS], (0, LANE_BF))
    misc = lax.dynamic_update_slice(misc, vec_qk[0:1, :DH], (0, LANE_GQ))
    misc = lax.dynamic_update_slice(misc, vec_qk[1:2, :DH], (0, LANE_GK))
    rep = jnp.concatenate([
        vec_1[0:1], vec_1[1:2], vec_2[3:4], vec_2[0:1], vec_2[1:2], vec_l[0:1],
        vec_1[2:3], vec_2[2:3], vec_l[1:2], misc, jnp.zeros((REP_ROWS - 10, D), F32)], axis=0)
    dcw_ffn = jnp.concatenate([dcw_g, dcw_v], axis=1).reshape(SUB, NDEV, n_ff)
    dcw_all = jnp.concatenate([jnp.transpose(dcw_ffn, (1, 0, 2)),
                               jnp.transpose(dcw_mix.reshape(SUB, NDEV, DH), (1, 0, 2))], axis=2)
    rep_all, conv_parts, p_in = _exchange(
        [(rep, "ag"), (dcw_all, "a2a"), (_scatter_dw_in(dwin_p), "a2a")], "exchange_grads")

    dmod_my = lax.dynamic_slice(rep_all[:, :NMOD, :].reshape(NDEV, NMOD * D), (0, me * n_ada), (NDEV, n_ada))
    r_ada = _adamw_ada(c_all, dmod_my, w_ada[0], m_w_ada[0], v_w_ada[0])
    r_in = _adamw_sharded(p_in, w_in[0], m_w_in[0], v_w_in[0], 256, "adamw_in")
    r_out = _adamw_sharded(p_out, w_out[0], m_w_out[0], v_w_out[0], 128, "adamw_out")
    r_up = _adamw_sharded(p_up, w_up[0], m_w_up[0], v_w_up[0], 256, "adamw_up")
    r_down = _adamw_sharded(p_down, w_down[0], m_w_down[0], v_w_down[0], 176, "adamw_down")
    small = _adamw_small(rep_all, conv_parts, [
        [b_ada, m_b_ada, v_b_ada], [norm1_g, m_norm1_g, v_norm1_g], [b_forget, m_b_forget, v_b_forget],
        [q_norm_g, m_q_norm_g, v_q_norm_g], [k_norm_g, m_k_norm_g, v_k_norm_g], [norm2_g, m_norm2_g, v_norm2_g],
        [ffn_conv_w[0], m_ffn_conv_w[0], v_ffn_conv_w[0]], [conv_mix_w[0], m_conv_mix_w[0], v_conv_mix_w[0]]])
    loss = small[0].reshape(())
    r_bada, r_n1, r_bf, r_gq, r_gk, r_n2, r_cf, r_cm = [small[1 + 4 * p:5 + 4 * p] for p in range(8)]
    lead = lambda t: tuple(a[None] for a in t)
    per_w = [lead(r_ada), r_bada, r_n1, lead(r_in), r_bf, r_gq, r_gk, lead(r_cm), lead(r_out), r_n2,
             lead(r_up), lead(r_cf), lead(r_down)]
    outs = [loss, grad_x[None]]
    for field in range(4):
        outs += [t[field] for t in per_w]
    return tuple(outs)
```

```python
import functools

import jax
import jax.numpy as jnp
from jax import lax
from jax.experimental import pallas as pl
from jax.experimental.pallas import tpu as pltpu

F32 = jnp.float32
BF16 = jnp.bfloat16

NDEV = 8
D = 1024
HEADS = 8
DH = 64
AW = 512
CW = 512
DFF = 2816
DIN = 3080
DINP = 3200
NMOD = 6
EPS = 1e-6
QK_SCALE = 0.125
LANES = 128
SUB = 8

ADAM_LR = 0.001
ADAM_B1 = 0.9
ADAM_B2 = 0.999
ADAM_EPS = 1e-08
ADAM_WD = 0.01
ADAM_STEP = 10

MESH = pl.DeviceIdType.MESH
ANY = pl.BlockSpec(memory_space=pl.ANY)

NN = (((1,), (0,)), ((), ()))
NT = (((1,), (1,)), ((), ()))
TN = (((0,), (0,)), ((), ()))


def _dot(a, b, dims=NN, precision=None):
    return lax.dot_general(a, b, dims, precision=precision, preferred_element_type=F32)


def _params(sem=None, vmem_mb=None):
    kw = {}
    if sem is not None:
        kw["dimension_semantics"] = sem
    if vmem_mb is not None:
        kw["vmem_limit_bytes"] = vmem_mb * 1024 * 1024
    return pltpu.CompilerParams(**kw)


def _sigmoid(x):
    return 1.0 / (1.0 + jnp.exp(-x))


class _Exchange:
    def __init__(self, items):
        self.arrays = [a for a, _ in items]
        self.modes = [m for _, m in items]
        self.n = len(items)
        self.out_shape = []
        for a, m in items:
            sh = {"ag": (NDEV,) + a.shape, "ag2": (NDEV,) + a.shape, "pair": a.shape[1:]}.get(m, a.shape)
            self.out_shape.append(jax.ShapeDtypeStruct(sh, a.dtype))
        self.scratch = [pltpu.SemaphoreType.DMA((self.n, NDEV - 1)), pltpu.SemaphoreType.DMA((self.n, NDEV - 1)),
                        pltpu.SemaphoreType.DMA((self.n,))]

    def _plan(self, srcs, outs, sems):
        send_sems, recv_sems, loc_sems = sems
        x, y, c = lax.axis_index("x"), lax.axis_index("y"), lax.axis_index("c")
        me, my_chip = 4 * x + 2 * y + c, 2 * x + y
        sib = (x, y, 1 - c)
        local, first, landed, forwards, arrivals = [], [], [], [], []

        def remote(a, k, src, dst, to):
            return pltpu.make_async_remote_copy(src_ref=src, dst_ref=dst, send_sem=send_sems.at[a, k],
                                                recv_sem=recv_sems.at[a, k], device_id=to, device_id_type=MESH)

        for a, mode in enumerate(self.modes):
            src, out = srcs[a], outs[a]
            if mode in ("ag", "a2a"):
                piece = (lambda slot, src=src: src) if mode == "ag" else (lambda slot, src=src: src.at[slot])
                local.append(pltpu.make_async_copy(piece(me), out.at[me], loc_sems.at[a]))
                for r in range(1, NDEV):
                    px = 1 - x if (r >> 2) & 1 else x
                    py = 1 - y if (r >> 1) & 1 else y
                    pc = 1 - c if r & 1 else c
                    pidx = 4 * px + 2 * py + pc
                    first.append(remote(a, r - 1, piece(pidx), out.at[me], (px, py, pc)))
                    arrivals.append(remote(a, r - 1, piece(pidx), out.at[pidx], (px, py, pc)))
            elif mode == "ag2":
                local.append(pltpu.make_async_copy(src, out.at[me], loc_sems.at[a]))
                first.append(remote(a, 0, src, out.at[me], sib))
                arrivals.append(remote(a, 0, src, out.at[me + 1 - 2 * c], sib))
                for j, (px, py) in enumerate([(1 - x, y), (x, 1 - y), (1 - x, 1 - y)]):
                    theirs = out.at[4 * px + 2 * py + c]
                    first.append(remote(a, 1 + j, src, out.at[me], (px, py, c)))
                    landed.append(remote(a, 1 + j, src, theirs, (px, py, c)))
                    forwards.append(remote(a, 4 + j, theirs, theirs, sib))
                    arrivals.append(remote(a, 4 + j, src, out.at[4 * px + 2 * py + 1 - c], sib))
            elif mode == "pair":
                first.append(remote(a, 0, src.at[1 - c], out, sib))
                arrivals.append(remote(a, 0, src.at[1 - c], out, sib))
            else:
                assert mode == "chips", mode
                local.append(pltpu.make_async_copy(src.at[my_chip], out.at[my_chip], loc_sems.at[a]))
                for j, (px, py) in enumerate([(1 - x, y), (x, 1 - y), (1 - x, 1 - y)]):
                    q = 2 * px + py
                    first.append(remote(a, 1 + j, src.at[q], out.at[my_chip], (px, py, c)))
                    arrivals.append(remote(a, 1 + j, src.at[q], out.at[q], (px, py, c)))
        return local, first, landed, forwards, arrivals

    def start(self, srcs, outs, sems):
        local, first, _, _, _ = self._plan(srcs, outs, sems)
        for cp in local + first:
            cp.start()

    def wait(self, srcs, outs, sems):
        local, first, landed, forwards, arrivals = self._plan(srcs, outs, sems)
        for cp, fwd in zip(landed, forwards):
            cp.wait_recv()
            fwd.start()
        for cp in arrivals:
            cp.wait_recv()
        for cp in first + forwards:
            cp.wait_send()
        for cp in local:
            cp.wait()


def _exchange(items, name):
    ex = _Exchange(items)
    n = ex.n

    def body(*refs):
        srcs, outs, sems = refs[:n], refs[n:2 * n], refs[2 * n:]
        ex.start(srcs, outs, sems)
        ex.wait(srcs, outs, sems)

    return pl.pallas_call(
        body, name=name,
        out_shape=tuple(ex.out_shape),
        in_specs=[ANY] * n, out_specs=tuple([ANY] * n),
        scratch_shapes=ex.scratch,
        compiler_params=pltpu.CompilerParams(has_side_effects=True),
    )(*ex.arrays)


def _call(body, inputs, *, name, grid, in_specs, out_specs, out_shape, scratch_shapes=(), vmem_mb=None, hosted=None):
    out_specs, out_shape, scratch_shapes = tuple(out_specs), tuple(out_shape), list(scratch_shapes)
    if not hosted:
        res = pl.pallas_call(
            body, name=name, grid=grid, in_specs=list(in_specs), out_specs=out_specs, out_shape=out_shape,
            scratch_shapes=scratch_shapes, compiler_params=_params(("arbitrary",) * len(grid), vmem_mb),
        )(*inputs)
        return tuple(res), ()
    ex = _Exchange(hosted)
    n, n_in, n_out, n_scr = ex.n, len(inputs), len(out_shape), len(scratch_shapes)

    def hosting_body(*refs):
        ins, srcs = refs[:n_in], refs[n_in:n_in + n]
        outs, landing = refs[n_in + n:n_in + n + n_out], refs[n_in + n + n_out:n_in + 2 * n + n_out]
        scratch, sems = refs[n_in + 2 * n + n_out:n_in + 2 * n + n_out + n_scr], refs[n_in + 2 * n + n_out + n_scr:]
        first = functools.reduce(jnp.logical_and, [pl.program_id(d) == 0 for d in range(len(grid))])
        last = functools.reduce(jnp.logical_and, [pl.program_id(d) == grid[d] - 1 for d in range(len(grid))])

        @pl.when(first)
        def _():
            ex.start(srcs, landing, sems)

        body(*ins, *outs, *scratch)

        @pl.when(last)
        def _():
            ex.wait(srcs, landing, sems)

    res = pl.pallas_call(
        hosting_body, name=name, grid=grid,
        in_specs=list(in_specs) + [ANY] * n, out_specs=out_specs + tuple([ANY] * n),
        out_shape=out_shape + tuple(ex.out_shape), scratch_shapes=scratch_shapes + ex.scratch,
        compiler_params=_params(("arbitrary",) * len(grid), vmem_mb),
    )(*inputs, *ex.arrays)
    return tuple(res[:n_out]), tuple(res[n_out:])


def _mm(a, b, mode, out_dtype, tm, tn, name, hosted=None):
    if mode == "nn":
        (m, k), n = a.shape, b.shape[1]
        a_spec = pl.BlockSpec((tm, k), lambda i, j: (i, 0))
        b_spec = pl.BlockSpec((k, tn), lambda i, j: (0, j))
        dims = NN
    elif mode == "nt":
        (m, k), n = a.shape, b.shape[0]
        a_spec = pl.BlockSpec((tm, k), lambda i, j: (i, 0))
        b_spec = pl.BlockSpec((tn, k), lambda i, j: (j, 0))
        dims = NT
    else:
        (k, m), n = a.shape, b.shape[1]
        a_spec = pl.BlockSpec((k, tm), lambda i, j: (0, i))
        b_spec = pl.BlockSpec((k, tn), lambda i, j: (0, j))
        dims = TN
    assert m % tm == 0 and n % tn == 0, (m, n, tm, tn)

    def body(a_ref, b_ref, o_ref):
        o_ref[...] = _dot(a_ref[...], b_ref[...], dims).astype(o_ref.dtype)

    (out,), moved = _call(
        body, (a, b), name=name, grid=(m // tm, n // tn),
        in_specs=[a_spec, b_spec], out_specs=[pl.BlockSpec((tm, tn), lambda i, j: (i, j))],
        out_shape=[jax.ShapeDtypeStruct((m, n), out_dtype)], vmem_mb=48, hosted=hosted)
    return (out, moved) if hosted else out


def _shift_down(x, k, fill):
    row = lax.broadcasted_iota(jnp.int32, x.shape, 0)
    y = pltpu.roll(x, k, 0)
    for t in range(k):
        y = jnp.where(row == t, fill[t], y)
    return y


def _shift_up(x, k, fill):
    n = x.shape[0]
    row = lax.broadcasted_iota(jnp.int32, x.shape, 0)
    y = pltpu.roll(x, n - k, 0)
    for t in range(k):
        y = jnp.where(row == n - k + t, fill[t], y)
    return y


def _conv_taps(x, halo, w):
    if halo is None:
        f1, f2 = [0.0], [0.0, 0.0]
    else:
        f1, f2 = [halo[7:8, :]], [halo[6:7, :], halo[7:8, :]]
    s1 = _shift_down(x, 1, f1)
    s2 = _shift_down(x, 2, f2)
    u = w[2:3, :] * x + w[1:2, :] * s1 + w[0:1, :] * s2
    return u, s1, s2


def _conv_taps_t(du, nxt, w):
    if nxt is None:
        f1, f2 = [0.0], [0.0, 0.0]
    else:
        f1, f2 = [nxt[0:1, :]], [nxt[0:1, :], nxt[1:2, :]]
    return w[2:3, :] * du + w[1:2, :] * _shift_up(du, 1, f1) + w[0:1, :] * _shift_up(du, 2, f2)


def _ada_fwd(c_all, w_ada, b_my):
    def body(c_ref, w_ref, b_ref, o_ref):
        cv = c_ref[...]
        act = cv * _sigmoid(cv)
        o_ref[...] = _dot(act, w_ref[...], NN, lax.Precision.HIGHEST) + b_ref[...]

    return pl.pallas_call(
        body, name="ada_fwd",
        out_shape=jax.ShapeDtypeStruct((NDEV, w_ada.shape[1]), F32),
        compiler_params=_params(None, 32),
    )(c_all, w_ada, b_my)


TR = 256


def _row_spec(width, col=0):
    return pl.BlockSpec((TR, width), lambda i, col=col: (i, col))


def _full_spec(shape):
    return pl.BlockSpec(shape, lambda i: (0,) * len(shape))


def _norm_mod_fwd(x, mod, g):
    s = x.shape[0]

    def body(x_ref, mod_ref, g_ref, h_ref):
        xv = x_ref[...]
        r = lax.rsqrt(jnp.mean(xv * xv, axis=-1, keepdims=True) + EPS)
        nrm = xv * r * g_ref[...]
        h_ref[...] = (nrm * (1.0 + mod_ref[1:2, :]) + mod_ref[0:1, :]).astype(BF16)

    return pl.pallas_call(
        body, name="norm1_fwd", grid=(s // TR,),
        in_specs=[_row_spec(D), _full_spec((SUB, D)), _full_spec((1, D))],
        out_specs=_row_spec(D), out_shape=jax.ShapeDtypeStruct((s, D), BF16),
        compiler_params=_params(("parallel",)),
    )(x, mod, g)


SLAB = 2 * DH
AUG_F, AUG_ONE, AUG_LSE = 0, 3, 6


def _split3(x):
    hi = x.astype(BF16).astype(F32)
    r1 = x - hi
    mid = r1.astype(BF16).astype(F32)
    return hi, mid, r1 - mid


def _lanes3(lane, first, pieces, other):
    out = other
    for k in range(3):
        out = jnp.where(lane == first + k, pieces[k], out)
    return out


def _qkv_prep(proj, fcum, gq, gk):
    s = proj.shape[0]

    def body(q_ref, k_ref, v_ref, f_ref, gq_ref, gk_ref, qo_ref, ko_ref, vo_ref):
        lane = lax.broadcasted_iota(jnp.int32, (TR, DH), 1)
        ones3 = jnp.where(lane < 3, 1.0, 0.0)
        for h in range(HEADS):
            sl = slice(DH * h, DH * (h + 1))
            lo, hi = slice(SLAB * h, SLAB * h + DH), slice(SLAB * h + DH, SLAB * (h + 1))
            f3 = _split3(f_ref[:, h:h + 1])
            qh = q_ref[:, sl]
            r = lax.rsqrt(jnp.mean(qh * qh, axis=-1, keepdims=True) + EPS)
            qo_ref[:, lo] = (qh * r * gq_ref[...] * QK_SCALE).astype(BF16)
            qo_ref[:, hi] = _lanes3(lane, AUG_F, f3, jnp.where((lane >= AUG_ONE) & (lane < AUG_LSE), 1.0, 0.0)).astype(BF16)
            kh = k_ref[:, sl]
            r = lax.rsqrt(jnp.mean(kh * kh, axis=-1, keepdims=True) + EPS)
            ko_ref[:, lo] = (kh * r * gk_ref[...]).astype(BF16)
            ko_ref[:, hi] = _lanes3(lane, AUG_ONE, [-f for f in f3], jnp.where(lane < AUG_LSE + 3, 1.0, 0.0)).astype(BF16)
            vo_ref[:, lo] = v_ref[:, sl].astype(BF16)
            vo_ref[:, hi] = ones3.astype(BF16)

    o = jax.ShapeDtypeStruct((s, HEADS * SLAB), BF16)
    return pl.pallas_call(
        body, name="qkv_prep", grid=(s // TR,),
        in_specs=[_row_spec(AW, 0), _row_spec(AW, 1), _row_spec(AW, 2), _row_spec(LANES),
                  _full_spec((1, DH)), _full_spec((1, DH))],
        out_specs=(_row_spec(HEADS * SLAB), _row_spec(HEADS * SLAB), _row_spec(HEADS * SLAB)), out_shape=(o, o, o),
        compiler_params=_params(("parallel",)),
    )(proj, proj, proj, fcum, gq, gk)


FG_BLOCK = (3 * AW + 3 * CW) // LANES


def _fgate_fwd(proj, bf_pad):
    s = proj.shape[0]

    def body(fg_ref, b_ref, o_ref, carry_ref):
        i = pl.program_id(0)

        @pl.when(i == 0)
        def _():
            carry_ref[...] = jnp.zeros_like(carry_ref)

        z = fg_ref[...] + b_ref[...]
        logf = jnp.minimum(z, 0.0) - jnp.log1p(jnp.exp(-jnp.abs(z)))
        row = lax.broadcasted_iota(jnp.int32, (TR, TR), 0)
        col = lax.broadcasted_iota(jnp.int32, (TR, TR), 1)
        tri = (col <= row).astype(F32)
        cs = _dot(tri, logf, NN, lax.Precision.HIGHEST) + carry_ref[0:1, :]
        o_ref[...] = cs
        carry_ref[...] = jnp.broadcast_to(cs[TR - 1:TR, :], carry_ref.shape)

    return pl.pallas_call(
        body, name="fgate_fwd", grid=(s // TR,),
        in_specs=[_row_spec(LANES, FG_BLOCK), _full_spec((1, LANES))],
        out_specs=_row_spec(LANES), out_shape=jax.ShapeDtypeStruct((s, LANES), F32),
        scratch_shapes=[pltpu.VMEM((SUB, LANES), F32)],
        compiler_params=_params(("arbitrary",)),
    )(proj, bf_pad)


def _fgate_bwd(dfcol, proj, bf_pad):
    s = proj.shape[0]
    nb = s // TR

    def body(df_ref, fg_ref, b_ref, o_ref, db_ref, carry_ref):
        i = pl.program_id(0)

        @pl.when(i == 0)
        def _():
            carry_ref[...] = jnp.zeros_like(carry_ref)
            db_ref[...] = jnp.zeros_like(db_ref)

        row = lax.broadcasted_iota(jnp.int32, (TR, TR), 0)
        col = lax.broadcasted_iota(jnp.int32, (TR, TR), 1)
        tri = (col >= row).astype(F32)
        dlogf = _dot(tri, df_ref[...], NN, lax.Precision.HIGHEST) + carry_ref[0:1, :]
        carry_ref[...] = jnp.broadcast_to(dlogf[0:1, :], carry_ref.shape)
        z = fg_ref[...] + b_ref[...]
        dfg = dlogf * _sigmoid(-z)
        o_ref[...] = dfg.astype(BF16)
        db_ref[0:1, :] += jnp.sum(dfg, axis=0, keepdims=True)

    rev = lambda col: pl.BlockSpec((TR, LANES), lambda i, col=col: (nb - 1 - i, col))
    return pl.pallas_call(
        body, name="fgate_bwd", grid=(nb,),
        in_specs=[rev(0), rev(FG_BLOCK), _full_spec((1, LANES))],
        out_specs=(rev(0), _full_spec((SUB, LANES))),
        out_shape=(jax.ShapeDtypeStruct((s, LANES), BF16), jax.ShapeDtypeStruct((SUB, LANES), F32)),
        scratch_shapes=[pltpu.VMEM((SUB, LANES), F32)],
        compiler_params=_params(("arbitrary",)),
    )(dfcol, proj, bf_pad)


def _resid_norm2(x, z, mod, g):
    s = x.shape[0]

    def body(x_ref, z_ref, mod_ref, g_ref, x1_ref, h_ref):
        x1 = x_ref[...] + mod_ref[2:3, :] * z_ref[...]
        x1_ref[...] = x1
        r = lax.rsqrt(jnp.mean(x1 * x1, axis=-1, keepdims=True) + EPS)
        nrm = x1 * r * g_ref[...]
        h_ref[...] = (nrm * (1.0 + mod_ref[4:5, :]) + mod_ref[3:4, :]).astype(BF16)

    return pl.pallas_call(
        body, name="resid_norm2", grid=(s // TR,),
        in_specs=[_row_spec(D), _row_spec(D), _full_spec((SUB, D)), _full_spec((1, D))],
        out_specs=(_row_spec(D), _row_spec(D)),
        out_shape=(jax.ShapeDtypeStruct((s, D), F32), jax.ShapeDtypeStruct((s, D), BF16)),
        compiler_params=_params(("parallel",)),
    )(x, z, mod, g)


def _loss_head(x1, y, tgt, mod):
    s = x1.shape[0]

    def body(x1_ref, y_ref, t_ref, mod_ref, dout_ref, dy_ref, vec_ref):
        @pl.when(pl.program_id(0) == 0)
        def _():
            vec_ref[...] = jnp.zeros_like(vec_ref)

        yv = y_ref[...]
        g2 = mod_ref[5:6, :]
        diff = x1_ref[...] + g2 * yv - t_ref[...]
        dout = diff * (1.0 / D)
        dout_ref[...] = dout
        dy_ref[...] = (g2 * dout).astype(BF16)
        vec_ref[0:1, :] += jnp.sum(dout * yv, axis=0, keepdims=True)
        vec_ref[1:2, :] += jnp.sum(diff * diff, axis=0, keepdims=True)

    return pl.pallas_call(
        body, name="loss_head", grid=(s // TR,),
        in_specs=[_row_spec(D), _row_spec(D), _row_spec(D), _full_spec((SUB, D))],
        out_specs=(_row_spec(D), _row_spec(D), _full_spec((SUB, D))),
        out_shape=(jax.ShapeDtypeStruct((s, D), F32), jax.ShapeDtypeStruct((s, D), BF16),
                   jax.ShapeDtypeStruct((SUB, D), F32)),
        compiler_params=_params(("arbitrary",)),
    )(x1, y, tgt, mod)


def _norm_mod_bwd(dh, xin, dres, zin, mod, g, shift_row, scale_row, gate_row, name, hosted=None):
    s = dh.shape[0]
    with_gate = gate_row is not None

    def body(*refs):
        if with_gate:
            dh_ref, x_ref, dres_ref, z_ref, mod_ref, g_ref, dx_ref, dz_ref, vec_ref = refs
        else:
            dh_ref, x_ref, dres_ref, mod_ref, g_ref, dx_ref, vec_ref = refs

        @pl.when(pl.program_id(0) == 0)
        def _():
            vec_ref[...] = jnp.zeros_like(vec_ref)

        xv = x_ref[...]
        dhv = dh_ref[...]
        gv = g_ref[...]
        r = lax.rsqrt(jnp.mean(xv * xv, axis=-1, keepdims=True) + EPS)
        xh = xv * r
        dn = dhv * (1.0 + mod_ref[scale_row:scale_row + 1, :])
        dxh = dn * gv
        dx = dres_ref[...] + r * (dxh - xh * jnp.mean(dxh * xh, axis=-1, keepdims=True))
        dx_ref[...] = dx
        vec_ref[0:1, :] += jnp.sum(dhv, axis=0, keepdims=True)
        vec_ref[1:2, :] += jnp.sum(dhv * (xh * gv), axis=0, keepdims=True)
        vec_ref[2:3, :] += jnp.sum(dn * xh, axis=0, keepdims=True)
        if with_gate:
            dz_ref[...] = (mod_ref[gate_row:gate_row + 1, :] * dx).astype(BF16)
            vec_ref[3:4, :] += jnp.sum(dx * z_ref[...], axis=0, keepdims=True)

    ins = [dh, xin, dres] + ([zin] if with_gate else []) + [mod, g]
    in_specs = [_row_spec(D)] * (4 if with_gate else 3) + [_full_spec((SUB, D)), _full_spec((1, D))]
    out_specs = [_row_spec(D)] + ([_row_spec(D)] if with_gate else []) + [_full_spec((SUB, D))]
    out_shape = [jax.ShapeDtypeStruct((s, D), F32)] + ([jax.ShapeDtypeStruct((s, D), BF16)] if with_gate else []) \
        + [jax.ShapeDtypeStruct((SUB, D), F32)]
    outs, moved = _call(body, ins, name=name, grid=(s // TR,), in_specs=in_specs, out_specs=out_specs,
                        out_shape=out_shape, hosted=hosted)
    return outs + (moved,) if hosted else outs


XIN_BLOCK = 3 * AW // LANES
BG_BLOCK = XIN_BLOCK + CW // LANES
CG_BLOCK = BG_BLOCK + CW // LANES


def _seq_spec(s, first_block):
    return pl.BlockSpec((s, LANES), lambda j, fb=first_block: (0, fb + j))


def _mixconv_fwd(proj, w):
    s = proj.shape[0]

    def body(xin_ref, bg_ref, cg_ref, w_ref, o_ref):
        cx = cg_ref[...] * xin_ref[...]
        cv, _, _ = _conv_taps(cx, None, w_ref[...])
        o_ref[...] = bg_ref[...] * cv

    return pl.pallas_call(
        body, name="mixconv_fwd", grid=(CW // LANES,),
        in_specs=[_seq_spec(s, XIN_BLOCK), _seq_spec(s, BG_BLOCK), _seq_spec(s, CG_BLOCK),
                  pl.BlockSpec((3, LANES), lambda j: (0, j))],
        out_specs=_seq_spec(s, 0), out_shape=jax.ShapeDtypeStruct((s, CW), F32),
        compiler_params=_params(("parallel",), 48),
    )(proj, proj, proj, w)


def _mixconv_bwd(dmixed, proj, w):
    s = proj.shape[0]

    def body(d_ref, xin_ref, bg_ref, cg_ref, w_ref, dxin_ref, dbg_ref, dcg_ref, dw_ref):
        wv = w_ref[...]
        xin, cg, dconv = xin_ref[...], cg_ref[...], d_ref[...]
        cx = cg * xin
        cv, s1, s2 = _conv_taps(cx, None, wv)
        dbg_ref[...] = (dconv * cv).astype(BF16)
        dcv = dconv * bg_ref[...]
        dw_ref[...] = jnp.zeros_like(dw_ref)
        dw_ref[0:1, :] = jnp.sum(dcv * s2, axis=0, keepdims=True)
        dw_ref[1:2, :] = jnp.sum(dcv * s1, axis=0, keepdims=True)
        dw_ref[2:3, :] = jnp.sum(dcv * cx, axis=0, keepdims=True)
        dcx = _conv_taps_t(dcv, None, wv)
        dcg_ref[...] = (dcx * xin).astype(BF16)
        dxin_ref[...] = (dcx * cg).astype(BF16)

    o = jax.ShapeDtypeStruct((s, CW), BF16)
    return pl.pallas_call(
        body, name="mixconv_bwd", grid=(CW // LANES,),
        in_specs=[_seq_spec(s, AW // LANES), _seq_spec(s, XIN_BLOCK), _seq_spec(s, BG_BLOCK), _seq_spec(s, CG_BLOCK),
                  pl.BlockSpec((3, LANES), lambda j: (0, j))],
        out_specs=(_seq_spec(s, 0), _seq_spec(s, 0), _seq_spec(s, 0), pl.BlockSpec((SUB, LANES), lambda j: (0, j))),
        out_shape=(o, o, o, jax.ShapeDtypeStruct((SUB, CW), F32)),
        compiler_params=_params(("parallel",), 48),
    )(dmixed, proj, proj, proj, w)


TA = 256
NEG = -1e30


def _causal_mask():
    row = lax.broadcasted_iota(jnp.int32, (TA, TA), 0)
    col = lax.broadcasted_iota(jnp.int32, (TA, TA), 1)
    return col <= row


def _attn_fwd(qp, kp, vp, hosted):
    s = qp.shape[0]
    nq = s // TA

    def body(q_ref, k_ref, v_ref, o_ref, lse_ref):
        i = pl.program_id(1)
        slabs = [slice(SLAB * hh, SLAB * (hh + 1)) for hh in range(2)]
        q = [q_ref[:, sl] for sl in slabs]

        def block(j, carry, masked):
            j0 = pl.multiple_of(j * TA, TA)
            out = []
            for hh in range(2):
                m, acc = carry[hh]
                sc = _dot(q[hh], k_ref[pl.ds(j0, TA), slabs[hh]], NT)
                if masked:
                    sc = jnp.where(_causal_mask(), sc, NEG)
                m_new = jnp.maximum(m, jnp.max(sc, axis=-1, keepdims=True))
                p = jnp.exp(sc - m_new)
                acc = jnp.exp(m - m_new) * acc + _dot(p.astype(BF16), v_ref[pl.ds(j0, TA), slabs[hh]])
                out.append((m_new, acc))
            return tuple(out)

        init = tuple((jnp.full((TA, 1), NEG, F32), jnp.zeros((TA, SLAB), F32)) for _ in range(2))
        carry = lax.fori_loop(0, i, lambda j, cr: block(j, cr, False), init)
        res = block(i, carry, True)
        for hh in range(2):
            m, acc = res[hh]
            l = acc[:, DH:DH + 1]
            o_ref[:, DH * hh:DH * (hh + 1)] = acc[:, 0:DH] / l
            lse_ref[0, :, hh:hh + 1] = m + jnp.log(l)

    (o, lse), moved = _call(
        body, (qp, kp, vp), name="attn_fwd", grid=(HEADS // 2, nq),
        in_specs=[pl.BlockSpec((TA, 2 * SLAB), lambda p, i: (i, p)),
                  pl.BlockSpec((s, 2 * SLAB), lambda p, i: (0, p)),
                  pl.BlockSpec((s, 2 * SLAB), lambda p, i: (0, p))],
        out_specs=[pl.BlockSpec((TA, LANES), lambda p, i: (i, p)), pl.BlockSpec((1, TA, 2), lambda p, i: (p, i, 0))],
        out_shape=[jax.ShapeDtypeStruct((s, AW), F32), jax.ShapeDtypeStruct((HEADS // 2, s, 2), F32)],
        vmem_mb=32, hosted=hosted)
    return o, lse, moved


def _attn_bwd(qp, kp, vp, dmixed, o, lse, hosted):
    s = qp.shape[0]
    nq = s // TA

    def body(q_ref, k_ref, v_ref, do_ref, o_ref, lse_ref, dq_ref, dk_ref, dv_ref, qb_ref, dob_ref):
        dk_ref[...] = jnp.zeros_like(dk_ref)
        dv_ref[...] = jnp.zeros_like(dv_ref)
        slabs = [slice(SLAB * hh, SLAB * (hh + 1)) for hh in range(2)]
        lane = lax.broadcasted_iota(jnp.int32, (TA, DH), 1)

        def q_block(i, _):
            i0 = pl.multiple_of(i * TA, TA)
            rows = pl.ds(i0, TA)
            for hh in range(2):
                half = slice(DH * hh, DH * (hh + 1))
                do = do_ref[rows, half]
                delta = jnp.sum(do * o_ref[rows, half], axis=-1, keepdims=True)
                dob_ref[hh, :, 0:DH] = do.astype(BF16)
                dob_ref[hh, :, DH:SLAB] = _lanes3(lane, 0, [-d for d in _split3(delta)], 0.0).astype(BF16)
                lse3 = _split3(lse_ref[0, rows, hh:hh + 1])
                qb_ref[hh, :, 0:DH] = q_ref[rows, SLAB * hh:SLAB * hh + DH]
                aug = q_ref[rows, SLAB * hh + DH:SLAB * (hh + 1)].astype(F32)
                qb_ref[hh, :, DH:SLAB] = _lanes3(lane, AUG_LSE, [-x for x in lse3], aug).astype(BF16)

            def block(j, dqs, masked):
                j0 = pl.multiple_of(j * TA, TA)
                out = []
                for hh in range(2):
                    q, dob = qb_ref[hh], dob_ref[hh]
                    k = k_ref[pl.ds(j0, TA), slabs[hh]]
                    sc = _dot(q, k, NT)
                    if masked:
                        sc = jnp.where(_causal_mask(), sc, NEG)
                    p = jnp.exp(sc)
                    dv_ref[pl.ds(j0, TA), slabs[hh]] += _dot(p.astype(BF16), dob, TN)
                    ds = (p * _dot(dob, v_ref[pl.ds(j0, TA), slabs[hh]], NT)).astype(BF16)
                    dk_ref[pl.ds(j0, TA), slabs[hh]] += _dot(ds, q, TN)
                    out.append(dqs[hh] + _dot(ds, k))
                return tuple(out)

            init = (jnp.zeros((TA, SLAB), F32), jnp.zeros((TA, SLAB), F32))
            dqs = lax.fori_loop(0, i, lambda j, acc: block(j, acc, False), init)
            dqs = block(i, dqs, True)
            for hh in range(2):
                dq_ref[rows, slabs[hh]] = dqs[hh]
            return 0

        lax.fori_loop(0, nq, q_block, 0)

    pair = lambda p: (0, p)
    slab2 = pl.BlockSpec((s, 2 * SLAB), pair)
    seq = pl.BlockSpec((s, LANES), pair)
    small = pl.BlockSpec((1, s, 2), lambda p: (p, 0, 0))
    o32 = jax.ShapeDtypeStruct((s, HEADS * SLAB), F32)
    return _call(
        body, (qp, kp, vp, dmixed, o, lse), name="attn_bwd", grid=(HEADS // 2,),
        in_specs=[slab2, slab2, slab2, seq, seq, small], out_specs=[slab2, slab2, slab2], out_shape=[o32, o32, o32],
        scratch_shapes=[pltpu.VMEM((2, TA, SLAB), BF16), pltpu.VMEM((2, TA, SLAB), BF16)], vmem_mb=48, hosted=hosted)


def _qkv_post(dqp, dkp, dvp, proj, gq, gk):
    s = proj.shape[0]

    def body(dq_ref, dk_ref, dv_ref, q_ref, k_ref, gq_ref, gk_ref, dqo_ref, dko_ref, dvo_ref, df_ref, vec_ref):
        @pl.when(pl.program_id(0) == 0)
        def _():
            vec_ref[...] = jnp.zeros_like(vec_ref)

        def one(d_ref, x_ref, g_ref, o_ref, row, scale):
            dg = jnp.zeros((1, DH), F32)
            for h in range(HEADS):
                sl = slice(DH * h, DH * (h + 1))
                xv = x_ref[:, sl]
                r = lax.rsqrt(jnp.mean(xv * xv, axis=-1, keepdims=True) + EPS)
                xh = xv * r
                dn = d_ref[:, SLAB * h:SLAB * h + DH] * scale
                dg = dg + jnp.sum(dn * xh, axis=0, keepdims=True)
                dxh = dn * g_ref[...]
                o_ref[:, sl] = (r * (dxh - xh * jnp.mean(dxh * xh, axis=-1, keepdims=True))).astype(BF16)
            vec_ref[row:row + 1, 0:DH] += dg

        one(dq_ref, q_ref, gq_ref, dqo_ref, 0, QK_SCALE)
        one(dk_ref, k_ref, gk_ref, dko_ref, 1, 1.0)
        lane = lax.broadcasted_iota(jnp.int32, (TR, LANES), 1)
        df = jnp.zeros((TR, LANES), F32)
        for h in range(HEADS):
            dvo_ref[:, DH * h:DH * (h + 1)] = dv_ref[:, SLAB * h:SLAB * h + DH].astype(BF16)
            row_sum = dq_ref[:, SLAB * h + DH:SLAB * h + DH + 1]
            col_sum = dk_ref[:, SLAB * h + DH + AUG_ONE:SLAB * h + DH + AUG_ONE + 1]
            df = jnp.where(lane == h, row_sum - col_sum, df)
        df_ref[...] = df

    o = jax.ShapeDtypeStruct((s, AW), BF16)
    wide = _row_spec(HEADS * SLAB)
    return pl.pallas_call(
        body, name="qkv_post", grid=(s // TR,),
        in_specs=[wide, wide, wide, _row_spec(AW, 0), _row_spec(AW, 1), _full_spec((1, DH)), _full_spec((1, DH))],
        out_specs=(_row_spec(AW), _row_spec(AW), _row_spec(AW), _row_spec(LANES), _full_spec((SUB, LANES))),
        out_shape=(o, o, o, jax.ShapeDtypeStruct((s, LANES), F32), jax.ShapeDtypeStruct((SUB, LANES), F32)),
        compiler_params=_params(("arbitrary",)),
    )(dqp, dkp, dvp, proj, proj, gq, gk)


TF = 256
NJ = DFF // TF
FFN_ROWS_FWD = 1024
FFN_ROWS_BWD = 512


def _ffn_fwd(h2, wup, cw, wd):
    s = h2.shape[0]
    tr = FFN_ROWS_FWD
    nr = s // tr

    def body(h_ref, wg_ref, wv_ref, cg_ref, cv_ref, wd_ref, pg_ref, pv_ref, y_ref, halo_ref):
        r, j = pl.program_id(0), pl.program_id(1)
        hv = h_ref[...]
        pg = _dot(hv, wg_ref[...]).astype(BF16)
        pv = _dot(hv, wv_ref[...]).astype(BF16)
        pg_ref[...] = pg
        pv_ref[...] = pv
        pgf, pvf = pg.astype(F32), pv.astype(F32)
        ug, _, _ = _conv_taps(pgf, jnp.where(r > 0, halo_ref[j, 0], 0.0), cg_ref[...])
        uv, _, _ = _conv_taps(pvf, jnp.where(r > 0, halo_ref[j, 1], 0.0), cv_ref[...])
        halo_ref[j, 0] = pgf[tr - SUB:tr, :]
        halo_ref[j, 1] = pvf[tr - SUB:tr, :]
        act = (ug * _sigmoid(ug) * uv).astype(BF16)
        contrib = _dot(act, wd_ref[...])

        @pl.when(j == 0)
        def _():
            y_ref[...] = contrib

        @pl.when(j > 0)
        def _():
            y_ref[...] += contrib

    pre = jax.ShapeDtypeStruct((s, DFF), BF16)
    return pl.pallas_call(
        body, name="ffn_fwd", grid=(nr, NJ),
        in_specs=[pl.BlockSpec((tr, D), lambda r, j: (r, 0)),
                  pl.BlockSpec((D, TF), lambda r, j: (0, j)),
                  pl.BlockSpec((D, TF), lambda r, j: (0, NJ + j)),
                  pl.BlockSpec((3, TF), lambda r, j: (0, j)),
                  pl.BlockSpec((3, TF), lambda r, j: (0, NJ + j)),
                  pl.BlockSpec((TF, D), lambda r, j: (j, 0))],
        out_specs=(pl.BlockSpec((tr, TF), lambda r, j: (r, j)),
                   pl.BlockSpec((tr, TF), lambda r, j: (r, j)),
                   pl.BlockSpec((tr, D), lambda r, j: (r, 0))),
        out_shape=(pre, pre, jax.ShapeDtypeStruct((s, D), F32)),
        scratch_shapes=[pltpu.VMEM((NJ, 2, SUB, TF), F32)],
        compiler_params=_params(("arbitrary", "arbitrary"), 56),
    )(h2, wup, wup, cw, cw, wd)


def _ffn_bwd(dy, h2, pre_g, pre_v, wup, cw, wd):
    s = h2.shape[0]
    tr = FFN_ROWS_BWD
    nr = s // tr
    hb = tr // (2 * SUB)

    def body(dy_ref, h_ref, pg_ref, pv_ref, hg_ref, hv_ref, wg_ref, wv_ref, cg_ref, cv_ref, wd_ref,
             dh_ref, dwg_ref, dwv_ref, dwd_ref, dcg_ref, dcv_ref, nxt_ref, awg_ref, awv_ref, awd_ref):
        j, r = pl.program_id(0), pl.program_id(1)
        rr = nr - 1 - r
        row0 = pl.multiple_of(rr * tr, tr)
        cwg, cwv = cg_ref[...], cv_ref[...]
        pg, pv = pg_ref[...].astype(F32), pv_ref[...].astype(F32)
        ug, g1, g2 = _conv_taps(pg, jnp.where(rr > 0, hg_ref[SUB:2 * SUB, :].astype(F32), 0.0), cwg)
        uv, v1, v2 = _conv_taps(pv, jnp.where(rr > 0, hv_ref[SUB:2 * SUB, :].astype(F32), 0.0), cwv)
        sg = _sigmoid(ug)
        sil = ug * sg
        act = (sil * uv).astype(BF16)
        dyv = dy_ref[...]
        da = _dot(dyv, wd_ref[...], NT)
        dug = da * uv * (sg * (1.0 + ug * (1.0 - sg)))
        duv = da * sil
        dpg = _conv_taps_t(dug, jnp.where(r > 0, nxt_ref[0], 0.0), cwg)
        dpv = _conv_taps_t(duv, jnp.where(r > 0, nxt_ref[1], 0.0), cwv)
        nxt_ref[0] = dug[0:SUB, :]
        nxt_ref[1] = duv[0:SUB, :]
        dpgb, dpvb = dpg.astype(BF16), dpv.astype(BF16)
        hv = h_ref[...]
        dwd = _dot(act, dyv, TN)
        dwg = _dot(hv, dpgb, TN)
        dwv = _dot(hv, dpvb, TN)
        dh = _dot(dpgb, wg_ref[...], NT) + _dot(dpvb, wv_ref[...], NT)

        def taps(du, x0, x1, x2):
            return (jnp.sum(du * x2, axis=0, keepdims=True), jnp.sum(du * x1, axis=0, keepdims=True),
                    jnp.sum(du * x0, axis=0, keepdims=True))

        tg, tv = taps(dug, pg, g1, g2), taps(duv, pv, v1, v2)

        @pl.when(r == 0)
        def _():
            awd_ref[...] = dwd
            awg_ref[...] = dwg
            awv_ref[...] = dwv
            dcg_ref[...] = jnp.zeros_like(dcg_ref)
            dcv_ref[...] = jnp.zeros_like(dcv_ref)

        @pl.when(r > 0)
        def _():
            awd_ref[...] += dwd
            awg_ref[...] += dwg
            awv_ref[...] += dwv

        @pl.when(r == nr - 1)
        def _():
            dwd_ref[...] = awd_ref[...].astype(BF16)
            dwg_ref[...] = awg_ref[...].astype(BF16)
            dwv_ref[...] = awv_ref[...].astype(BF16)

        for t in range(3):
            dcg_ref[t:t + 1, :] += tg[t]
            dcv_ref[t:t + 1, :] += tv[t]

        @pl.when(j == 0)
        def _():
            dh_ref[pl.ds(row0, tr), :] = dh

        @pl.when(j > 0)
        def _():
            dh_ref[pl.ds(row0, tr), :] += dh

    rows = lambda j, r: (nr - 1 - r, 0)
    tile = lambda j, r: (nr - 1 - r, j)
    halo = lambda j, r: (jnp.maximum((nr - 1 - r) * hb - 1, 0), j)
    return pl.pallas_call(
        body, name="ffn_bwd", grid=(NJ, nr),
        in_specs=[pl.BlockSpec((tr, D), rows), pl.BlockSpec((tr, D), rows),
                  pl.BlockSpec((tr, TF), tile), pl.BlockSpec((tr, TF), tile),
                  pl.BlockSpec((2 * SUB, TF), halo), pl.BlockSpec((2 * SUB, TF), halo),
                  pl.BlockSpec((D, TF), lambda j, r: (0, j)), pl.BlockSpec((D, TF), lambda j, r: (0, NJ + j)),
                  pl.BlockSpec((3, TF), lambda j, r: (0, j)), pl.BlockSpec((3, TF), lambda j, r: (0, NJ + j)),
                  pl.BlockSpec((TF, D), lambda j, r: (j, 0))],
        out_specs=(pl.BlockSpec((s, D), lambda j, r: (0, 0)),
                   pl.BlockSpec((D, TF), lambda j, r: (0, j)), pl.BlockSpec((D, TF), lambda j, r: (0, j)),
                   pl.BlockSpec((TF, D), lambda j, r: (j, 0)),
                   pl.BlockSpec((SUB, TF), lambda j, r: (0, j)), pl.BlockSpec((SUB, TF), lambda j, r: (0, j))),
        out_shape=(jax.ShapeDtypeStruct((s, D), F32),
                   jax.ShapeDtypeStruct((D, DFF), BF16), jax.ShapeDtypeStruct((D, DFF), BF16),
                   jax.ShapeDtypeStruct((DFF, D), BF16),
                   jax.ShapeDtypeStruct((SUB, DFF), F32), jax.ShapeDtypeStruct((SUB, DFF), F32)),
        scratch_shapes=[pltpu.VMEM((2, SUB, TF), F32), pltpu.VMEM((D, TF), F32), pltpu.VMEM((D, TF), F32),
                        pltpu.VMEM((TF, D), F32)],
        compiler_params=_params(("arbitrary", "arbitrary"), 56),
    )(dy, h2, pre_g, pre_v, pre_g, pre_v, wup, wup, cw, cw, wd)


def _adam(w, g, m, v):
    m = ADAM_B1 * m + (1.0 - ADAM_B1) * g
    v = ADAM_B2 * v + (1.0 - ADAM_B2) * (g * g)
    m_hat = m / (1.0 - ADAM_B1 ** ADAM_STEP)
    v_hat = v / (1.0 - ADAM_B2 ** ADAM_STEP)
    delta = -ADAM_LR * (m_hat / (jnp.sqrt(v_hat) + ADAM_EPS) + ADAM_WD * w)
    return delta, m, v


NCHIP = NDEV // 2


def _pair_add(mine, theirs, tr, name):
    _, _, rws, cols = mine.shape

    def body(a_ref, b_ref, o_ref):
        c = lax.axis_index("c")
        o_ref[0] = (a_ref[c, 0].astype(F32) + b_ref[0].astype(F32)).astype(BF16)

    (out,), _ = _call(
        body, (mine, theirs), name=name, grid=(NCHIP, rws // tr),
        in_specs=[pl.BlockSpec((2, 1, tr, cols), lambda q, i: (0, q, i, 0)),
                  pl.BlockSpec((1, tr, cols), lambda q, i: (q, i, 0))],
        out_specs=[pl.BlockSpec((1, tr, cols), lambda q, i: (q, i, 0))],
        out_shape=[jax.ShapeDtypeStruct((NCHIP, rws, cols), BF16)], vmem_mb=32)
    return out


def _adamw_sharded(parts, w, m, v, tr, name, hosted=None):
    rws, cols = w.shape

    def body(p_ref, w_ref, m_ref, v_ref, g_ref, d_ref, mo_ref, vo_ref):
        g = p_ref[0].astype(F32)
        for q in range(1, NCHIP):
            g = g + p_ref[q].astype(F32)
        g_ref[...] = g
        d_ref[...], mo_ref[...], vo_ref[...] = _adam(w_ref[...], g, m_ref[...], v_ref[...])

    blk = pl.BlockSpec((tr, cols), lambda i: (i, 0))
    o = jax.ShapeDtypeStruct((rws, cols), F32)
    outs, moved = _call(
        body, (parts, w, m, v), name=name, grid=(rws // tr,),
        in_specs=[pl.BlockSpec((NCHIP, tr, cols), lambda i: (0, i, 0)), blk, blk, blk],
        out_specs=[blk, blk, blk, blk], out_shape=[o, o, o, o], vmem_mb=48, hosted=hosted)
    return (outs, moved) if hosted else outs


def _adamw_ada(c_all, dmod_my, w, m, v):
    rws, cols = w.shape
    tr = 256

    def body(c_ref, dm_ref, w_ref, m_ref, v_ref, g_ref, d_ref, mo_ref, vo_ref):
        cv = c_ref[...]
        act = cv * _sigmoid(cv)
        g = _dot(act, dm_ref[...], TN, lax.Precision.HIGHEST)
        g_ref[...] = g
        d_ref[...], mo_ref[...], vo_ref[...] = _adam(w_ref[...], g, m_ref[...], v_ref[...])

    blk = pl.BlockSpec((tr, cols), lambda i: (i, 0))
    o = jax.ShapeDtypeStruct((rws, cols), F32)
    return pl.pallas_call(
        body, name="adamw_ada", grid=(rws // tr,),
        in_specs=[pl.BlockSpec((NDEV, tr), lambda i: (0, i)), _full_spec((NDEV, cols)), blk, blk, blk],
        out_specs=(blk, blk, blk, blk), out_shape=(o, o, o, o),
        compiler_params=_params(("parallel",), 48),
    )(c_all, dmod_my, w, m, v)


REP_ROWS = 16
ROW_N1, ROW_N2, ROW_LOSS, ROW_MISC = 6, 7, 8, 9
LANE_BF, LANE_GQ, LANE_GK = 0, 128, 256


def _adamw_small(rep_all, conv_all, wmv):
    n_ff = wmv[6][0].shape[1]

    def body(*refs):
        rep_ref, conv_ref = refs[:2]
        ins = refs[2:2 + 24]
        outs = refs[2 + 24:]
        loss_ref, outs = outs[0], outs[1:]
        g_rep = rep_ref[0]
        g_conv = conv_ref[0]
        for d in range(1, NDEV):
            g_rep = g_rep + rep_ref[d]
            g_conv = g_conv + conv_ref[d]
        loss_ref[...] = (0.5 / D) * jnp.sum(g_rep[ROW_LOSS:ROW_LOSS + 1, :], axis=-1, keepdims=True)
        grads = [
            None,
            g_rep[ROW_N1:ROW_N1 + 1, :],
            g_rep[ROW_MISC:ROW_MISC + 1, LANE_BF:LANE_BF + HEADS],
            g_rep[ROW_MISC:ROW_MISC + 1, LANE_GQ:LANE_GQ + DH],
            g_rep[ROW_MISC:ROW_MISC + 1, LANE_GK:LANE_GK + DH],
            g_rep[ROW_N2:ROW_N2 + 1, :],
            g_conv[0:3, 0:n_ff],
            g_conv[0:3, n_ff:n_ff + DH],
        ]
        for p in range(8):
            w_ref, m_ref, v_ref = ins[3 * p:3 * p + 3]
            g_ref, d_ref, mo_ref, vo_ref = outs[4 * p:4 * p + 4]
            if p == 0:
                for nmod in range(NMOD):
                    sl = slice(D * nmod, D * (nmod + 1))
                    g = g_rep[nmod:nmod + 1, :]
                    g_ref[:, sl] = g
                    d_ref[:, sl], mo_ref[:, sl], vo_ref[:, sl] = _adam(w_ref[:, sl], g, m_ref[:, sl], v_ref[:, sl])
            else:
                g = grads[p]
                g_ref[...] = g
                d_ref[...], mo_ref[...], vo_ref[...] = _adam(w_ref[...], g, m_ref[...], v_ref[...])

    flat = [a for trio in wmv for a in trio]
    out_shape = [jax.ShapeDtypeStruct((1, 1), F32)]
    for trio in wmv:
        out_shape += [jax.ShapeDtypeStruct(trio[0].shape, F32)] * 4
    return pl.pallas_call(
        body, name="adamw_small", out_shape=tuple(out_shape),
        compiler_params=_params(None, 32),
    )(rep_all, conv_all, *flat)


IN_SPLITS = (AW, 2 * AW, 3 * AW, 3 * AW + HEADS, 3 * AW + HEADS + CW, 3 * AW + HEADS + 2 * CW)


def _assemble_w_in(g_in):
    w = jnp.transpose(g_in, (1, 0, 2)).reshape(D, DIN)
    q, k, v, fg, xin, bg, cg = jnp.split(w, IN_SPLITS, axis=1)
    return jnp.concatenate([q, k, v, xin, bg, cg, fg, jnp.zeros((D, DINP - DIN), w.dtype)], axis=1)


def _scatter_dw_in(dwp):
    q, k, v, xin, bg, cg, fg = jnp.split(dwp, (AW, 2 * AW, 3 * AW, 3 * AW + CW, 3 * AW + 2 * CW, 3 * AW + 3 * CW), axis=1)
    w = jnp.concatenate([q, k, v, fg[:, :HEADS], xin, bg, cg], axis=1)
    return jnp.transpose(w.reshape(D, NCHIP, 2, DIN // NDEV), (2, 1, 0, 3))


def _by_core_chip(a):
    return jnp.transpose(a.reshape((NCHIP, 2) + a.shape[1:]), (1, 0, 2, 3))


def kernel(x, c, w_ada, b_ada, norm1_g, w_in, b_forget, q_norm_g, k_norm_g, conv_mix_w, w_out, norm2_g, w_up, ffn_conv_w, w_down, loss_target, m_w_ada, m_b_ada, m_norm1_g, m_w_in, m_b_forget, m_q_norm_g, m_k_norm_g, m_conv_mix_w, m_w_out, m_norm2_g, m_w_up, m_ffn_conv_w, m_w_down, v_w_ada, v_b_ada, v_norm1_g, v_w_in, v_b_forget, v_q_norm_g, v_k_norm_g, v_conv_mix_w, v_w_out, v_norm2_g, v_w_up, v_ffn_conv_w, v_w_down):
    me = 4 * lax.axis_index("x") + 2 * lax.axis_index("y") + lax.axis_index("c")
    xs, tgt = x[0], loss_target[0]
    s = xs.shape[0]
    nq = s // TA
    n_ada = w_ada.shape[2]
    n_ff = w_up.shape[2]

    conv_w = jnp.concatenate([ffn_conv_w[0], conv_mix_w[0]], axis=1)
    conv_w = jnp.concatenate([conv_w, jnp.zeros((SUB - 3, conv_w.shape[1]), F32)], axis=0)
    c_all, conv_all, g_in = _exchange(
        [(c.reshape(SUB, D // SUB), "ag"), (conv_w, "ag"), (w_in[0].astype(BF16), "ag2")], "exchange_w_in")
    c_all = c_all.reshape(NDEV, D)
    cw_ffn = jnp.transpose(conv_all[:, :3, :n_ff], (1, 0, 2)).reshape(3, 2 * DFF)
    cw_mix = jnp.transpose(conv_all[:, :3, n_ff:], (1, 0, 2)).reshape(3, CW)
    w_in_p = _assemble_w_in(g_in)

    b_my = lax.dynamic_slice(b_ada, (0, me * n_ada), (1, n_ada))
    mod_part = _ada_fwd(c_all, w_ada[0], b_my)
    (mod_rows,) = _exchange([(jnp.broadcast_to(mod_part[:, None, :], (NDEV, SUB, n_ada)), "a2a")], "exchange_mod")
    mod = mod_rows[:, 0, :].reshape(NMOD, D)
    mod = jnp.concatenate([mod, jnp.zeros((SUB - NMOD, D), F32)], axis=0)

    h = _norm_mod_fwd(xs, mod, norm1_g)
    proj = _mm(h, w_in_p, "nn", F32, 512, 640, "proj_fwd")
    bf_pad = jnp.concatenate([b_forget, jnp.zeros((1, LANES - HEADS), F32)], axis=1)
    fcum = _fgate_fwd(proj, bf_pad)
    qp, kp, vp = _qkv_prep(proj, fcum, q_norm_g, k_norm_g)
    attn, lse, (g_out, g_up, g_down) = _attn_fwd(
        qp, kp, vp,
        [(w_out[0].astype(BF16), "ag2"), (w_up[0].astype(BF16), "ag2"), (w_down[0].astype(BF16), "ag2")])
    w_out_f = g_out.reshape(D, D)
    w_up_f = jnp.transpose(g_up, (1, 0, 2)).reshape(D, 2 * DFF)
    w_down_f = g_down.reshape(DFF, D)
    conv = _mixconv_fwd(proj, cw_mix)
    mixed = jnp.concatenate([attn, conv], axis=1).astype(BF16)
    z = _mm(mixed, w_out_f, "nn", F32, 512, 512, "out_fwd")
    x1, h2 = _resid_norm2(xs, z, mod, norm2_g)
    pre_g, pre_v, y = _ffn_fwd(h2, w_up_f, cw_ffn, w_down_f)
    dout, dy, vec_l = _loss_head(x1, y, tgt, mod)

    dh2, dwup_g, dwup_v, dwd, dcw_g, dcw_v = _ffn_bwd(dy, h2, pre_g, pre_v, w_up_f, cw_ffn, w_down_f)
    dx1, dz, vec_2 = _norm_mod_bwd(dh2, x1, dout, z, mod, norm2_g, 3, 4, 2, "norm2_bwd")
    dwout = _mm(mixed, dz, "tn", BF16, 512, 512, "out_bwd_w")
    s_out = _by_core_chip(dwout.reshape(NDEV, D // NDEV, D))
    s_down = _by_core_chip(dwd.reshape(NDEV, DFF // NDEV, D))
    s_up = jnp.transpose(jnp.concatenate([dwup_g, dwup_v], axis=1).reshape(D, NCHIP, 2, n_ff), (2, 1, 0, 3))
    dmixed, (t_out, t_up, t_down) = _mm(dz, w_out_f, "nt", F32, 512, 512, "out_bwd_x",
                                        hosted=[(s_out, "pair"), (s_up, "pair"), (s_down, "pair")])
    c_out = _pair_add(s_out, t_out, 128, "pair_add_out")
    c_up = _pair_add(s_up, t_up, 256, "pair_add_up")
    c_down = _pair_add(s_down, t_down, 176, "pair_add_down")
    dxin, dbg, dcg, dcw_mix = _mixconv_bwd(dmixed, proj, cw_mix)
    (dqp, dkp, dvp), (p_out, p_up, p_down) = _attn_bwd(
        qp, kp, vp, dmixed, attn, lse, [(c_out, "chips"), (c_up, "chips"), (c_down, "chips")])
    dq, dk, dvb, dfcol, vec_qk = _qkv_post(dqp, dkp, dvp, proj, q_norm_g, k_norm_g)
    dfg, vec_bf = _fgate_bwd(dfcol, proj, bf_pad)
    dproj = jnp.concatenate([dq, dk, dvb, dxin, dbg, dcg, dfg], axis=1)
    dwin_p = _mm(h, dproj, "tn", BF16, 512, 640, "proj_bwd_w")
    s_in = _scatter_dw_in(dwin_p)
    dh, (t_in,) = _mm(dproj, w_in_p, "nt", F32, 512, 512, "proj_bwd_x", hosted=[(s_in, "pair")])
    c_in = _pair_add(s_in, t_in, 256, "pair_add_in")
    grad_x, vec_1, (p_in,) = _norm_mod_bwd(dh, xs, dx1, None, mod, norm1_g, 0, 1, None, "norm1_bwd",
                                           hosted=[(c_in, "chips")])

    misc = jnp.zeros((1, D), F32)
    misc = lax.dynamic_update_slice(misc, vec_bf[0:1, :HEADS], (0, LANE_BF))
    misc = lax.dynamic_update_slice(misc, vec_qk[0:1, :DH], (0, LANE_GQ))
    misc = lax.dynamic_update_slice(misc, vec_qk[1:2, :DH], (0, LANE_GK))
    rep = jnp.concatenate([
        vec_1[0:1], vec_1[1:2], vec_2[3:4], vec_2[0:1], vec_2[1:2], vec_l[0:1],
        vec_1[2:3], vec_2[2:3], vec_l[1:2], misc, jnp.zeros((REP_ROWS - 10, D), F32)], axis=0)
    dcw_ffn = jnp.concatenate([dcw_g, dcw_v], axis=1).reshape(SUB, NDEV, n_ff)
    dcw_all = jnp.concatenate([jnp.transpose(dcw_ffn, (1, 0, 2)),
                               jnp.transpose(dcw_mix.reshape(SUB, NDEV, DH), (1, 0, 2))], axis=2)
    r_up, (rep_all, conv_parts) = _adamw_sharded(p_up, w_up[0], m_w_up[0], v_w_up[0], 256, "adamw_up",
                                                 hosted=[(rep, "ag"), (dcw_all, "a2a")])
    dmod_my = lax.dynamic_slice(rep_all[:, :NMOD, :].reshape(NDEV, NMOD * D), (0, me * n_ada), (NDEV, n_ada))
    r_ada = _adamw_ada(c_all, dmod_my, w_ada[0], m_w_ada[0], v_w_ada[0])
    r_in = _adamw_sharded(p_in, w_in[0], m_w_in[0], v_w_in[0], 256, "adamw_in")
    r_out = _adamw_sharded(p_out, w_out[0], m_w_out[0], v_w_out[0], 128, "adamw_out")
    r_down = _adamw_sharded(p_down, w_down[0], m_w_down[0], v_w_down[0], 176, "adamw_down")
    small = _adamw_small(rep_all, conv_parts, [
        [b_ada, m_b_ada, v_b_ada], [norm1_g, m_norm1_g, v_norm1_g], [b_forget, m_b_forget, v_b_forget],
        [q_norm_g, m_q_norm_g, v_q_norm_g], [k_norm_g, m_k_norm_g, v_k_norm_g], [norm2_g, m_norm2_g, v_norm2_g],
        [ffn_conv_w[0], m_ffn_conv_w[0], v_ffn_conv_w[0]], [conv_mix_w[0], m_conv_mix_w[0], v_conv_mix_w[0]]])
    loss = small[0].reshape(())
    r_bada, r_n1, r_bf, r_gq, r_gk, r_n2, r_cf, r_cm = [small[1 + 4 * p:5 + 4 * p] for p in range(8)]
    lead = lambda t: tuple(a[None] for a in t)
    per_w = [lead(r_ada), r_bada, r_n1, lead(r_in), r_bf, r_gq, r_gk, lead(r_cm), lead(r_out), r_n2,
             lead(r_up), lead(r_cf), lead(r_down)]
    outs = [loss, grad_x[None]]
    for field in range(4):
        outs += [t[field] for t in per_w]
    return tuple(outs)
```

```python
import functools

import jax
import jax.numpy as jnp
import numpy as np
from jax import lax
from jax.experimental import pallas as pl
from jax.experimental.pallas import tpu as pltpu

F32 = jnp.float32
BF16 = jnp.bfloat16

NDEV = 8
D = 1024
HEADS = 8
DH = 64
AW = 512
CW = 512
DFF = 2816
DIN = 3080
DINP = 3200
NMOD = 6
EPS = 1e-6
QK_SCALE = 0.125
LANES = 128
SUB = 8

ADAM_LR = 0.001
ADAM_B1 = 0.9
ADAM_B2 = 0.999
ADAM_EPS = 1e-08
ADAM_WD = 0.01
ADAM_STEP = 10

MESH = pl.DeviceIdType.MESH
ANY = pl.BlockSpec(memory_space=pl.ANY)

NN = (((1,), (0,)), ((), ()))
NT = (((1,), (1,)), ((), ()))
TN = (((0,), (0,)), ((), ()))


def _dot(a, b, dims=NN, precision=None):
    return lax.dot_general(a, b, dims, precision=precision, preferred_element_type=F32)


def _params(sem=None, vmem_mb=None):
    kw = {}
    if sem is not None:
        kw["dimension_semantics"] = sem
    if vmem_mb is not None:
        kw["vmem_limit_bytes"] = vmem_mb * 1024 * 1024
    return pltpu.CompilerParams(**kw)


def _sigmoid(x):
    return 1.0 / (1.0 + jnp.exp(-x))


class _Exchange:
    def __init__(self, items):
        self.arrays = [a for a, _ in items]
        self.modes = [m for _, m in items]
        self.n = len(items)
        self.out_shape = []
        for a, m in items:
            sh = {"ag": (NDEV,) + a.shape, "ag2": (NDEV,) + a.shape, "pair": a.shape[1:]}.get(m, a.shape)
            self.out_shape.append(jax.ShapeDtypeStruct(sh, a.dtype))
        self.scratch = [pltpu.SemaphoreType.DMA((self.n, NDEV - 1)), pltpu.SemaphoreType.DMA((self.n, NDEV - 1)),
                        pltpu.SemaphoreType.DMA((self.n,))]

    def _plan(self, srcs, outs, sems):
        send_sems, recv_sems, loc_sems = sems
        x, y, c = lax.axis_index("x"), lax.axis_index("y"), lax.axis_index("c")
        me, my_chip = 4 * x + 2 * y + c, 2 * x + y
        sib = (x, y, 1 - c)
        local, first, landed, forwards, arrivals = [], [], [], [], []

        def remote(a, k, src, dst, to):
            return pltpu.make_async_remote_copy(src_ref=src, dst_ref=dst, send_sem=send_sems.at[a, k],
                                                recv_sem=recv_sems.at[a, k], device_id=to, device_id_type=MESH)

        for a, mode in enumerate(self.modes):
            src, out = srcs[a], outs[a]
            if mode in ("ag", "a2a"):
                piece = (lambda slot, src=src: src) if mode == "ag" else (lambda slot, src=src: src.at[slot])
                local.append(pltpu.make_async_copy(piece(me), out.at[me], loc_sems.at[a]))
                for r in range(1, NDEV):
                    px = 1 - x if (r >> 2) & 1 else x
                    py = 1 - y if (r >> 1) & 1 else y
                    pc = 1 - c if r & 1 else c
                    pidx = 4 * px + 2 * py + pc
                    first.append(remote(a, r - 1, piece(pidx), out.at[me], (px, py, pc)))
                    arrivals.append(remote(a, r - 1, piece(pidx), out.at[pidx], (px, py, pc)))
            elif mode == "ag2":
                local.append(pltpu.make_async_copy(src, out.at[me], loc_sems.at[a]))
                first.append(remote(a, 0, src, out.at[me], sib))
                arrivals.append(remote(a, 0, src, out.at[me + 1 - 2 * c], sib))
                for j, (px, py) in enumerate([(1 - x, y), (x, 1 - y), (1 - x, 1 - y)]):
                    theirs = out.at[4 * px + 2 * py + c]
                    first.append(remote(a, 1 + j, src, out.at[me], (px, py, c)))
                    landed.append(remote(a, 1 + j, src, theirs, (px, py, c)))
                    forwards.append(remote(a, 4 + j, theirs, theirs, sib))
                    arrivals.append(remote(a, 4 + j, src, out.at[4 * px + 2 * py + 1 - c], sib))
            elif mode == "pair":
                first.append(remote(a, 0, src.at[1 - c], out, sib))
                arrivals.append(remote(a, 0, src.at[1 - c], out, sib))
            else:
                assert mode == "chips", mode
                local.append(pltpu.make_async_copy(src.at[my_chip], out.at[my_chip], loc_sems.at[a]))
                for j, (px, py) in enumerate([(1 - x, y), (x, 1 - y), (1 - x, 1 - y)]):
                    q = 2 * px + py
                    first.append(remote(a, 1 + j, src.at[q], out.at[my_chip], (px, py, c)))
                    arrivals.append(remote(a, 1 + j, src.at[q], out.at[q], (px, py, c)))
        return local, first, landed, forwards, arrivals

    def start(self, srcs, outs, sems):
        local, first, _, _, _ = self._plan(srcs, outs, sems)
        for cp in local + first:
            cp.start()

    def wait(self, srcs, outs, sems):
        local, first, landed, forwards, arrivals = self._plan(srcs, outs, sems)
        for cp, fwd in zip(landed, forwards):
            cp.wait_recv()
            fwd.start()
        for cp in arrivals:
            cp.wait_recv()
        for cp in first + forwards:
            cp.wait_send()
        for cp in local:
            cp.wait()


def _exchange(items, name):
    ex = _Exchange(items)
    n = ex.n

    def body(*refs):
        srcs, outs, sems = refs[:n], refs[n:2 * n], refs[2 * n:]
        ex.start(srcs, outs, sems)
        ex.wait(srcs, outs, sems)

    return pl.pallas_call(
        body, name=name,
        out_shape=tuple(ex.out_shape),
        in_specs=[ANY] * n, out_specs=tuple([ANY] * n),
        scratch_shapes=ex.scratch,
        compiler_params=pltpu.CompilerParams(has_side_effects=True),
    )(*ex.arrays)


def _call(body, inputs, *, name, grid, in_specs, out_specs, out_shape, scratch_shapes=(), vmem_mb=None, hosted=None):
    out_specs, out_shape, scratch_shapes = tuple(out_specs), tuple(out_shape), list(scratch_shapes)
    if not hosted:
        res = pl.pallas_call(
            body, name=name, grid=grid, in_specs=list(in_specs), out_specs=out_specs, out_shape=out_shape,
            scratch_shapes=scratch_shapes, compiler_params=_params(("arbitrary",) * len(grid), vmem_mb),
        )(*inputs)
        return tuple(res), ()
    ex = _Exchange(hosted)
    n, n_in, n_out, n_scr = ex.n, len(inputs), len(out_shape), len(scratch_shapes)

    def hosting_body(*refs):
        ins, srcs = refs[:n_in], refs[n_in:n_in + n]
        outs, landing = refs[n_in + n:n_in + n + n_out], refs[n_in + n + n_out:n_in + 2 * n + n_out]
        scratch, sems = refs[n_in + 2 * n + n_out:n_in + 2 * n + n_out + n_scr], refs[n_in + 2 * n + n_out + n_scr:]
        first = functools.reduce(jnp.logical_and, [pl.program_id(d) == 0 for d in range(len(grid))])
        last = functools.reduce(jnp.logical_and, [pl.program_id(d) == grid[d] - 1 for d in range(len(grid))])

        @pl.when(first)
        def _():
            ex.start(srcs, landing, sems)

        body(*ins, *outs, *scratch)

        @pl.when(last)
        def _():
            ex.wait(srcs, landing, sems)

    res = pl.pallas_call(
        hosting_body, name=name, grid=grid,
        in_specs=list(in_specs) + [ANY] * n, out_specs=out_specs + tuple([ANY] * n),
        out_shape=out_shape + tuple(ex.out_shape), scratch_shapes=scratch_shapes + ex.scratch,
        compiler_params=_params(("arbitrary",) * len(grid), vmem_mb),
    )(*inputs, *ex.arrays)
    return tuple(res[:n_out]), tuple(res[n_out:])


def _mm(a, b, mode, out_dtype, tm, tn, name, hosted=None):
    if mode == "nn":
        (m, k), n = a.shape, b.shape[1]
        a_spec = pl.BlockSpec((tm, k), lambda i, j: (i, 0))
        b_spec = pl.BlockSpec((k, tn), lambda i, j: (0, j))
        dims = NN
    elif mode == "nt":
        (m, k), n = a.shape, b.shape[0]
        a_spec = pl.BlockSpec((tm, k), lambda i, j: (i, 0))
        b_spec = pl.BlockSpec((tn, k), lambda i, j: (j, 0))
        dims = NT
    else:
        (k, m), n = a.shape, b.shape[1]
        a_spec = pl.BlockSpec((k, tm), lambda i, j: (0, i))
        b_spec = pl.BlockSpec((k, tn), lambda i, j: (0, j))
        dims = TN
    assert m % tm == 0 and n % tn == 0, (m, n, tm, tn)

    def body(a_ref, b_ref, o_ref):
        o_ref[...] = _dot(a_ref[...], b_ref[...], dims).astype(o_ref.dtype)

    (out,), moved = _call(
        body, (a, b), name=name, grid=(m // tm, n // tn),
        in_specs=[a_spec, b_spec], out_specs=[pl.BlockSpec((tm, tn), lambda i, j: (i, j))],
        out_shape=[jax.ShapeDtypeStruct((m, n), out_dtype)], vmem_mb=48, hosted=hosted)
    return (out, moved) if hosted else out


def _shift_down(x, k, fill):
    row = lax.broadcasted_iota(jnp.int32, x.shape, 0)
    y = pltpu.roll(x, k, 0)
    for t in range(k):
        y = jnp.where(row == t, fill[t], y)
    return y


def _shift_up(x, k, fill):
    n = x.shape[0]
    row = lax.broadcasted_iota(jnp.int32, x.shape, 0)
    y = pltpu.roll(x, n - k, 0)
    for t in range(k):
        y = jnp.where(row == n - k + t, fill[t], y)
    return y


def _conv_taps(x, halo, w):
    if halo is None:
        f1, f2 = [0.0], [0.0, 0.0]
    else:
        f1, f2 = [halo[7:8, :]], [halo[6:7, :], halo[7:8, :]]
    s1 = _shift_down(x, 1, f1)
    s2 = _shift_down(x, 2, f2)
    u = w[2:3, :] * x + w[1:2, :] * s1 + w[0:1, :] * s2
    return u, s1, s2


def _conv_taps_t(du, nxt, w):
    if nxt is None:
        f1, f2 = [0.0], [0.0, 0.0]
    else:
        f1, f2 = [nxt[0:1, :]], [nxt[0:1, :], nxt[1:2, :]]
    return w[2:3, :] * du + w[1:2, :] * _shift_up(du, 1, f1) + w[0:1, :] * _shift_up(du, 2, f2)


def _ada_fwd(c_all, w_ada, b_my):
    def body(c_ref, w_ref, b_ref, o_ref):
        cv = c_ref[...]
        act = cv * _sigmoid(cv)
        o_ref[...] = _dot(act, w_ref[...], NN, lax.Precision.HIGHEST) + b_ref[...]

    return pl.pallas_call(
        body, name="ada_fwd",
        out_shape=jax.ShapeDtypeStruct((NDEV, w_ada.shape[1]), F32),
        compiler_params=_params(None, 32),
    )(c_all, w_ada, b_my)


TR = 256


def _row_spec(width, col=0):
    return pl.BlockSpec((TR, width), lambda i, col=col: (i, col))


def _full_spec(shape):
    return pl.BlockSpec(shape, lambda i: (0,) * len(shape))


def _norm_mod_fwd(x, mod, g):
    s = x.shape[0]

    def body(x_ref, mod_ref, g_ref, h_ref):
        xv = x_ref[...]
        r = lax.rsqrt(jnp.mean(xv * xv, axis=-1, keepdims=True) + EPS)
        nrm = xv * r * g_ref[...]
        h_ref[...] = (nrm * (1.0 + mod_ref[1:2, :]) + mod_ref[0:1, :]).astype(BF16)

    return pl.pallas_call(
        body, name="norm1_fwd", grid=(s // TR,),
        in_specs=[_row_spec(D), _full_spec((SUB, D)), _full_spec((1, D))],
        out_specs=_row_spec(D), out_shape=jax.ShapeDtypeStruct((s, D), BF16),
        compiler_params=_params(("parallel",)),
    )(x, mod, g)


SLAB = 2 * DH
AUG_F, AUG_ONE, AUG_LSE = 0, 3, 6


def _split3(x):
    hi = x.astype(BF16).astype(F32)
    r1 = x - hi
    mid = r1.astype(BF16).astype(F32)
    return hi, mid, r1 - mid


def _lanes3(lane, first, pieces, other):
    out = other
    for k in range(3):
        out = jnp.where(lane == first + k, pieces[k], out)
    return out


def _aug_placement():
    eq = np.zeros((3 * LANES, HEADS * SLAB), np.float32)
    ek = np.zeros((3 * LANES, HEADS * SLAB), np.float32)
    ones = np.zeros((SUB, HEADS * SLAB), np.float32)
    for h in range(HEADS):
        aug = SLAB * h + DH
        for k in range(3):
            eq[LANES * k + h, aug + AUG_F + k] = 1.0
            ek[LANES * k + h, aug + AUG_ONE + k] = -1.0
            ones[0, aug + AUG_ONE + k] = 1.0
            ones[1, aug + AUG_F + k] = ones[1, aug + AUG_LSE + k] = 1.0
            ones[2, aug + k] = 1.0
    return jnp.asarray(eq, BF16), jnp.asarray(ek, BF16), jnp.asarray(ones)


def _qkv_prep(proj, fcum, gq, gk):
    s = proj.shape[0]

    def body(q_ref, k_ref, v_ref, f_ref, gq_ref, gk_ref, eq_ref, ek_ref, ones_ref, qo_ref, ko_ref, vo_ref):
        f3 = jnp.concatenate(_split3(f_ref[...]), axis=1).astype(BF16)
        qo_ref[...] = (_dot(f3, eq_ref[...]) + ones_ref[0:1, :]).astype(BF16)
        ko_ref[...] = (_dot(f3, ek_ref[...]) + ones_ref[1:2, :]).astype(BF16)
        vo_ref[...] = jnp.broadcast_to(ones_ref[2:3, :], vo_ref.shape).astype(BF16)
        for h in range(HEADS):
            sl = slice(DH * h, DH * (h + 1))
            lo = slice(SLAB * h, SLAB * h + DH)
            qh = q_ref[:, sl]
            r = lax.rsqrt(jnp.mean(qh * qh, axis=-1, keepdims=True) + EPS)
            qo_ref[:, lo] = (qh * r * gq_ref[...] * QK_SCALE).astype(BF16)
            kh = k_ref[:, sl]
            r = lax.rsqrt(jnp.mean(kh * kh, axis=-1, keepdims=True) + EPS)
            ko_ref[:, lo] = (kh * r * gk_ref[...]).astype(BF16)
            vo_ref[:, lo] = v_ref[:, sl].astype(BF16)

    eq, ek, ones = _aug_placement()
    o = jax.ShapeDtypeStruct((s, HEADS * SLAB), BF16)
    wide = _row_spec(HEADS * SLAB)
    outs, _ = _call(
        body, (proj, proj, proj, fcum, gq, gk, eq, ek, ones), name="qkv_prep", grid=(s // TR,),
        in_specs=[_row_spec(AW, 0), _row_spec(AW, 1), _row_spec(AW, 2), _row_spec(LANES),
                  _full_spec((1, DH)), _full_spec((1, DH)), _full_spec(eq.shape), _full_spec(ek.shape),
                  _full_spec(ones.shape)],
        out_specs=[wide, wide, wide], out_shape=[o, o, o], vmem_mb=32)
    return outs


FG_BLOCK = (3 * AW + 3 * CW) // LANES


def _fgate_fwd(proj, bf_pad):
    s = proj.shape[0]

    def body(fg_ref, b_ref, o_ref, carry_ref):
        i = pl.program_id(0)

        @pl.when(i == 0)
        def _():
            carry_ref[...] = jnp.zeros_like(carry_ref)

        z = fg_ref[...] + b_ref[...]
        logf = jnp.minimum(z, 0.0) - jnp.log1p(jnp.exp(-jnp.abs(z)))
        row = lax.broadcasted_iota(jnp.int32, (TR, TR), 0)
        col = lax.broadcasted_iota(jnp.int32, (TR, TR), 1)
        tri = (col <= row).astype(F32)
        cs = _dot(tri, logf, NN, lax.Precision.HIGHEST) + carry_ref[0:1, :]
        o_ref[...] = cs
        carry_ref[...] = jnp.broadcast_to(cs[TR - 1:TR, :], carry_ref.shape)

    return pl.pallas_call(
        body, name="fgate_fwd", grid=(s // TR,),
        in_specs=[_row_spec(LANES, FG_BLOCK), _full_spec((1, LANES))],
        out_specs=_row_spec(LANES), out_shape=jax.ShapeDtypeStruct((s, LANES), F32),
        scratch_shapes=[pltpu.VMEM((SUB, LANES), F32)],
        compiler_params=_params(("arbitrary",)),
    )(proj, bf_pad)


def _fgate_bwd(dfcol, proj, bf_pad):
    s = proj.shape[0]
    nb = s // TR

    def body(df_ref, fg_ref, b_ref, o_ref, db_ref, carry_ref):
        i = pl.program_id(0)

        @pl.when(i == 0)
        def _():
            carry_ref[...] = jnp.zeros_like(carry_ref)
            db_ref[...] = jnp.zeros_like(db_ref)

        row = lax.broadcasted_iota(jnp.int32, (TR, TR), 0)
        col = lax.broadcasted_iota(jnp.int32, (TR, TR), 1)
        tri = (col >= row).astype(F32)
        dlogf = _dot(tri, df_ref[...], NN, lax.Precision.HIGHEST) + carry_ref[0:1, :]
        carry_ref[...] = jnp.broadcast_to(dlogf[0:1, :], carry_ref.shape)
        z = fg_ref[...] + b_ref[...]
        dfg = dlogf * _sigmoid(-z)
        o_ref[...] = dfg.astype(BF16)
        db_ref[0:1, :] += jnp.sum(dfg, axis=0, keepdims=True)

    rev = lambda col: pl.BlockSpec((TR, LANES), lambda i, col=col: (nb - 1 - i, col))
    return pl.pallas_call(
        body, name="fgate_bwd", grid=(nb,),
        in_specs=[rev(0), rev(FG_BLOCK), _full_spec((1, LANES))],
        out_specs=(rev(0), _full_spec((SUB, LANES))),
        out_shape=(jax.ShapeDtypeStruct((s, LANES), BF16), jax.ShapeDtypeStruct((SUB, LANES), F32)),
        scratch_shapes=[pltpu.VMEM((SUB, LANES), F32)],
        compiler_params=_params(("arbitrary",)),
    )(dfcol, proj, bf_pad)


def _resid_norm2(x, z, mod, g):
    s = x.shape[0]

    def body(x_ref, z_ref, mod_ref, g_ref, x1_ref, h_ref):
        x1 = x_ref[...] + mod_ref[2:3, :] * z_ref[...]
        x1_ref[...] = x1
        r = lax.rsqrt(jnp.mean(x1 * x1, axis=-1, keepdims=True) + EPS)
        nrm = x1 * r * g_ref[...]
        h_ref[...] = (nrm * (1.0 + mod_ref[4:5, :]) + mod_ref[3:4, :]).astype(BF16)

    return pl.pallas_call(
        body, name="resid_norm2", grid=(s // TR,),
        in_specs=[_row_spec(D), _row_spec(D), _full_spec((SUB, D)), _full_spec((1, D))],
        out_specs=(_row_spec(D), _row_spec(D)),
        out_shape=(jax.ShapeDtypeStruct((s, D), F32), jax.ShapeDtypeStruct((s, D), BF16)),
        compiler_params=_params(("parallel",)),
    )(x, z, mod, g)


def _loss_head(x1, y, tgt, mod):
    s = x1.shape[0]

    def body(x1_ref, y_ref, t_ref, mod_ref, dout_ref, dy_ref, vec_ref):
        @pl.when(pl.program_id(0) == 0)
        def _():
            vec_ref[...] = jnp.zeros_like(vec_ref)

        yv = y_ref[...]
        g2 = mod_ref[5:6, :]
        diff = x1_ref[...] + g2 * yv - t_ref[...]
        dout = diff * (1.0 / D)
        dout_ref[...] = dout
        dy_ref[...] = (g2 * dout).astype(BF16)
        vec_ref[0:1, :] += jnp.sum(dout * yv, axis=0, keepdims=True)
        vec_ref[1:2, :] += jnp.sum(diff * diff, axis=0, keepdims=True)

    return pl.pallas_call(
        body, name="loss_head", grid=(s // TR,),
        in_specs=[_row_spec(D), _row_spec(D), _row_spec(D), _full_spec((SUB, D))],
        out_specs=(_row_spec(D), _row_spec(D), _full_spec((SUB, D))),
        out_shape=(jax.ShapeDtypeStruct((s, D), F32), jax.ShapeDtypeStruct((s, D), BF16),
                   jax.ShapeDtypeStruct((SUB, D), F32)),
        compiler_params=_params(("arbitrary",)),
    )(x1, y, tgt, mod)


def _norm_mod_bwd(dh, xin, dres, zin, mod, g, shift_row, scale_row, gate_row, name, hosted=None):
    s = dh.shape[0]
    with_gate = gate_row is not None

    def body(*refs):
        if with_gate:
            dh_ref, x_ref, dres_ref, z_ref, mod_ref, g_ref, dx_ref, dz_ref, vec_ref = refs
        else:
            dh_ref, x_ref, dres_ref, mod_ref, g_ref, dx_ref, vec_ref = refs

        @pl.when(pl.program_id(0) == 0)
        def _():
            vec_ref[...] = jnp.zeros_like(vec_ref)

        xv = x_ref[...]
        dhv = dh_ref[...]
        gv = g_ref[...]
        r = lax.rsqrt(jnp.mean(xv * xv, axis=-1, keepdims=True) + EPS)
        xh = xv * r
        dn = dhv * (1.0 + mod_ref[scale_row:scale_row + 1, :])
        dxh = dn * gv
        dx = dres_ref[...] + r * (dxh - xh * jnp.mean(dxh * xh, axis=-1, keepdims=True))
        dx_ref[...] = dx
        vec_ref[0:1, :] += jnp.sum(dhv, axis=0, keepdims=True)
        vec_ref[1:2, :] += jnp.sum(dhv * (xh * gv), axis=0, keepdims=True)
        vec_ref[2:3, :] += jnp.sum(dn * xh, axis=0, keepdims=True)
        if with_gate:
            dz_ref[...] = (mod_ref[gate_row:gate_row + 1, :] * dx).astype(BF16)
            vec_ref[3:4, :] += jnp.sum(dx * z_ref[...], axis=0, keepdims=True)

    ins = [dh, xin, dres] + ([zin] if with_gate else []) + [mod, g]
    in_specs = [_row_spec(D)] * (4 if with_gate else 3) + [_full_spec((SUB, D)), _full_spec((1, D))]
    out_specs = [_row_spec(D)] + ([_row_spec(D)] if with_gate else []) + [_full_spec((SUB, D))]
    out_shape = [jax.ShapeDtypeStruct((s, D), F32)] + ([jax.ShapeDtypeStruct((s, D), BF16)] if with_gate else []) \
        + [jax.ShapeDtypeStruct((SUB, D), F32)]
    outs, moved = _call(body, ins, name=name, grid=(s // TR,), in_specs=in_specs, out_specs=out_specs,
                        out_shape=out_shape, hosted=hosted)
    return outs + (moved,) if hosted else outs


XIN_BLOCK = 3 * AW // LANES
BG_BLOCK = XIN_BLOCK + CW // LANES
CG_BLOCK = BG_BLOCK + CW // LANES


def _seq_spec(s, first_block):
    return pl.BlockSpec((s, LANES), lambda j, fb=first_block: (0, fb + j))


def _mixconv_fwd(proj, w):
    s = proj.shape[0]

    def body(xin_ref, bg_ref, cg_ref, w_ref, o_ref):
        cx = cg_ref[...] * xin_ref[...]
        cv, _, _ = _conv_taps(cx, None, w_ref[...])
        o_ref[...] = bg_ref[...] * cv

    return pl.pallas_call(
        body, name="mixconv_fwd", grid=(CW // LANES,),
        in_specs=[_seq_spec(s, XIN_BLOCK), _seq_spec(s, BG_BLOCK), _seq_spec(s, CG_BLOCK),
                  pl.BlockSpec((3, LANES), lambda j: (0, j))],
        out_specs=_seq_spec(s, 0), out_shape=jax.ShapeDtypeStruct((s, CW), F32),
        compiler_params=_params(("parallel",), 48),
    )(proj, proj, proj, w)


def _mixconv_bwd(dmixed, proj, w):
    s = proj.shape[0]

    def body(d_ref, xin_ref, bg_ref, cg_ref, w_ref, dxin_ref, dbg_ref, dcg_ref, dw_ref):
        wv = w_ref[...]
        xin, cg, dconv = xin_ref[...], cg_ref[...], d_ref[...]
        cx = cg * xin
        cv, s1, s2 = _conv_taps(cx, None, wv)
        dbg_ref[...] = (dconv * cv).astype(BF16)
        dcv = dconv * bg_ref[...]
        dw_ref[...] = jnp.zeros_like(dw_ref)
        dw_ref[0:1, :] = jnp.sum(dcv * s2, axis=0, keepdims=True)
        dw_ref[1:2, :] = jnp.sum(dcv * s1, axis=0, keepdims=True)
        dw_ref[2:3, :] = jnp.sum(dcv * cx, axis=0, keepdims=True)
        dcx = _conv_taps_t(dcv, None, wv)
        dcg_ref[...] = (dcx * xin).astype(BF16)
        dxin_ref[...] = (dcx * cg).astype(BF16)

    o = jax.ShapeDtypeStruct((s, CW), BF16)
    return pl.pallas_call(
        body, name="mixconv_bwd", grid=(CW // LANES,),
        in_specs=[_seq_spec(s, AW // LANES), _seq_spec(s, XIN_BLOCK), _seq_spec(s, BG_BLOCK), _seq_spec(s, CG_BLOCK),
                  pl.BlockSpec((3, LANES), lambda j: (0, j))],
        out_specs=(_seq_spec(s, 0), _seq_spec(s, 0), _seq_spec(s, 0), pl.BlockSpec((SUB, LANES), lambda j: (0, j))),
        out_shape=(o, o, o, jax.ShapeDtypeStruct((SUB, CW), F32)),
        compiler_params=_params(("parallel",), 48),
    )(dmixed, proj, proj, proj, w)


TA = 256
NEG = -1e30


def _causal_mask():
    row = lax.broadcasted_iota(jnp.int32, (TA, TA), 0)
    col = lax.broadcasted_iota(jnp.int32, (TA, TA), 1)
    return col <= row


def _attn_fwd(qp, kp, vp, hosted):
    s = qp.shape[0]
    nq = s // TA

    def body(q_ref, k_ref, v_ref, o_ref, lse_ref):
        i = pl.program_id(1)
        slabs = [slice(SLAB * hh, SLAB * (hh + 1)) for hh in range(2)]
        q = [q_ref[:, sl] for sl in slabs]

        def block(j, carry, masked):
            j0 = pl.multiple_of(j * TA, TA)
            out = []
            for hh in range(2):
                m, acc = carry[hh]
                sc = _dot(q[hh], k_ref[pl.ds(j0, TA), slabs[hh]], NT)
                if masked:
                    sc = jnp.where(_causal_mask(), sc, NEG)
                m_new = jnp.maximum(m, jnp.max(sc, axis=-1, keepdims=True))
                p = jnp.exp(sc - m_new)
                acc = jnp.exp(m - m_new) * acc + _dot(p.astype(BF16), v_ref[pl.ds(j0, TA), slabs[hh]])
                out.append((m_new, acc))
            return tuple(out)

        init = tuple((jnp.full((TA, 1), NEG, F32), jnp.zeros((TA, SLAB), F32)) for _ in range(2))
        carry = lax.fori_loop(0, i, lambda j, cr: block(j, cr, False), init)
        res = block(i, carry, True)
        for hh in range(2):
            m, acc = res[hh]
            l = acc[:, DH:DH + 1]
            o_ref[:, DH * hh:DH * (hh + 1)] = acc[:, 0:DH] / l
            lse_ref[0, :, hh:hh + 1] = m + jnp.log(l)

    (o, lse), moved = _call(
        body, (qp, kp, vp), name="attn_fwd", grid=(HEADS // 2, nq),
        in_specs=[pl.BlockSpec((TA, 2 * SLAB), lambda p, i: (i, p)),
                  pl.BlockSpec((s, 2 * SLAB), lambda p, i: (0, p)),
                  pl.BlockSpec((s, 2 * SLAB), lambda p, i: (0, p))],
        out_specs=[pl.BlockSpec((TA, LANES), lambda p, i: (i, p)), pl.BlockSpec((1, TA, 2), lambda p, i: (p, i, 0))],
        out_shape=[jax.ShapeDtypeStruct((s, AW), F32), jax.ShapeDtypeStruct((HEADS // 2, s, 2), F32)],
        vmem_mb=32, hosted=hosted)
    return o, lse, moved


def _attn_bwd(qp, kp, vp, dmixed, o, lse, hosted):
    s = qp.shape[0]
    nq = s // TA

    def body(q_ref, k_ref, v_ref, do_ref, o_ref, lse_ref, dq_ref, dk_ref, dv_ref, qb_ref, dob_ref):
        dk_ref[...] = jnp.zeros_like(dk_ref)
        dv_ref[...] = jnp.zeros_like(dv_ref)
        slabs = [slice(SLAB * hh, SLAB * (hh + 1)) for hh in range(2)]
        lane = lax.broadcasted_iota(jnp.int32, (TA, DH), 1)

        def q_block(i, _):
            i0 = pl.multiple_of(i * TA, TA)
            rows = pl.ds(i0, TA)
            for hh in range(2):
                half = slice(DH * hh, DH * (hh + 1))
                do = do_ref[rows, half]
                delta = jnp.sum(do * o_ref[rows, half], axis=-1, keepdims=True)
                dob_ref[hh, :, 0:DH] = do.astype(BF16)
                dob_ref[hh, :, DH:SLAB] = _lanes3(lane, 0, [-d for d in _split3(delta)], 0.0).astype(BF16)
                lse3 = _split3(lse_ref[0, rows, hh:hh + 1])
                qb_ref[hh, :, 0:DH] = q_ref[rows, SLAB * hh:SLAB * hh + DH]
                aug = q_ref[rows, SLAB * hh + DH:SLAB * (hh + 1)].astype(F32)
                qb_ref[hh, :, DH:SLAB] = _lanes3(lane, AUG_LSE, [-x for x in lse3], aug).astype(BF16)

            def block(j, dqs, masked):
                j0 = pl.multiple_of(j * TA, TA)
                out = []
                for hh in range(2):
                    q, dob = qb_ref[hh], dob_ref[hh]
                    k = k_ref[pl.ds(j0, TA), slabs[hh]]
                    sc = _dot(q, k, NT)
                    if masked:
                        sc = jnp.where(_causal_mask(), sc, NEG)
                    p = jnp.exp(sc)
                    dv_ref[pl.ds(j0, TA), slabs[hh]] += _dot(p.astype(BF16), dob, TN)
                    ds = (p * _dot(dob, v_ref[pl.ds(j0, TA), slabs[hh]], NT)).astype(BF16)
                    dk_ref[pl.ds(j0, TA), slabs[hh]] += _dot(ds, q, TN)
                    out.append(dqs[hh] + _dot(ds, k))
                return tuple(out)

            init = (jnp.zeros((TA, SLAB), F32), jnp.zeros((TA, SLAB), F32))
            dqs = lax.fori_loop(0, i, lambda j, acc: block(j, acc, False), init)
            dqs = block(i, dqs, True)
            for hh in range(2):
                dq_ref[rows, slabs[hh]] = dqs[hh]
            return 0

        lax.fori_loop(0, nq, q_block, 0)

    pair = lambda p: (0, p)
    slab2 = pl.BlockSpec((s, 2 * SLAB), pair)
    seq = pl.BlockSpec((s, LANES), pair)
    small = pl.BlockSpec((1, s, 2), lambda p: (p, 0, 0))
    o32 = jax.ShapeDtypeStruct((s, HEADS * SLAB), F32)
    return _call(
        body, (qp, kp, vp, dmixed, o, lse), name="attn_bwd", grid=(HEADS // 2,),
        in_specs=[slab2, slab2, slab2, seq, seq, small], out_specs=[slab2, slab2, slab2], out_shape=[o32, o32, o32],
        scratch_shapes=[pltpu.VMEM((2, TA, SLAB), BF16), pltpu.VMEM((2, TA, SLAB), BF16)], vmem_mb=48, hosted=hosted)


def _qkv_post(dqp, dkp, dvp, proj, gq, gk):
    s = proj.shape[0]

    def body(dq_ref, dk_ref, dv_ref, q_ref, k_ref, gq_ref, gk_ref, dqo_ref, dko_ref, dvo_ref, df_ref, vec_ref):
        @pl.when(pl.program_id(0) == 0)
        def _():
            vec_ref[...] = jnp.zeros_like(vec_ref)

        def one(d_ref, x_ref, g_ref, o_ref, row, scale):
            dg = jnp.zeros((1, DH), F32)
            for h in range(HEADS):
                sl = slice(DH * h, DH * (h + 1))
                xv = x_ref[:, sl]
                r = lax.rsqrt(jnp.mean(xv * xv, axis=-1, keepdims=True) + EPS)
                xh = xv * r
                dn = d_ref[:, SLAB * h:SLAB * h + DH] * scale
                dg = dg + jnp.sum(dn * xh, axis=0, keepdims=True)
                dxh = dn * g_ref[...]
                o_ref[:, sl] = (r * (dxh - xh * jnp.mean(dxh * xh, axis=-1, keepdims=True))).astype(BF16)
            vec_ref[row:row + 1, 0:DH] += dg

        one(dq_ref, q_ref, gq_ref, dqo_ref, 0, QK_SCALE)
        one(dk_ref, k_ref, gk_ref, dko_ref, 1, 1.0)
        lane = lax.broadcasted_iota(jnp.int32, (TR, LANES), 1)
        df = jnp.zeros((TR, LANES), F32)
        for h in range(HEADS):
            dvo_ref[:, DH * h:DH * (h + 1)] = dv_ref[:, SLAB * h:SLAB * h + DH].astype(BF16)
            row_sum = dq_ref[:, SLAB * h + DH:SLAB * h + DH + 1]
            col_sum = dk_ref[:, SLAB * h + DH + AUG_ONE:SLAB * h + DH + AUG_ONE + 1]
            df = jnp.where(lane == h, row_sum - col_sum, df)
        df_ref[...] = df

    o = jax.ShapeDtypeStruct((s, AW), BF16)
    wide = _row_spec(HEADS * SLAB)
    return pl.pallas_call(
        body, name="qkv_post", grid=(s // TR,),
        in_specs=[wide, wide, wide, _row_spec(AW, 0), _row_spec(AW, 1), _full_spec((1, DH)), _full_spec((1, DH))],
        out_specs=(_row_spec(AW), _row_spec(AW), _row_spec(AW), _row_spec(LANES), _full_spec((SUB, LANES))),
        out_shape=(o, o, o, jax.ShapeDtypeStruct((s, LANES), F32), jax.ShapeDtypeStruct((SUB, LANES), F32)),
        compiler_params=_params(("arbitrary",)),
    )(dqp, dkp, dvp, proj, proj, gq, gk)


TF = 256
NJ = DFF // TF
FFN_ROWS_FWD = 1024
FFN_ROWS_BWD = 512


def _ffn_fwd(h2, wup, cw, wd):
    s = h2.shape[0]
    tr = FFN_ROWS_FWD
    nr = s // tr

    def body(h_ref, wg_ref, wv_ref, cg_ref, cv_ref, wd_ref, pg_ref, pv_ref, y_ref, halo_ref):
        r, j = pl.program_id(0), pl.program_id(1)
        hv = h_ref[...]
        pg = _dot(hv, wg_ref[...]).astype(BF16)
        pv = _dot(hv, wv_ref[...]).astype(BF16)
        pg_ref[...] = pg
        pv_ref[...] = pv
        pgf, pvf = pg.astype(F32), pv.astype(F32)
        ug, _, _ = _conv_taps(pgf, jnp.where(r > 0, halo_ref[j, 0], 0.0), cg_ref[...])
        uv, _, _ = _conv_taps(pvf, jnp.where(r > 0, halo_ref[j, 1], 0.0), cv_ref[...])
        halo_ref[j, 0] = pgf[tr - SUB:tr, :]
        halo_ref[j, 1] = pvf[tr - SUB:tr, :]
        act = (ug * _sigmoid(ug) * uv).astype(BF16)
        contrib = _dot(act, wd_ref[...])

        @pl.when(j == 0)
        def _():
            y_ref[...] = contrib

        @pl.when(j > 0)
        def _():
            y_ref[...] += contrib

    pre = jax.ShapeDtypeStruct((s, DFF), BF16)
    return pl.pallas_call(
        body, name="ffn_fwd", grid=(nr, NJ),
        in_specs=[pl.BlockSpec((tr, D), lambda r, j: (r, 0)),
                  pl.BlockSpec((D, TF), lambda r, j: (0, j)),
                  pl.BlockSpec((D, TF), lambda r, j: (0, NJ + j)),
                  pl.BlockSpec((3, TF), lambda r, j: (0, j)),
                  pl.BlockSpec((3, TF), lambda r, j: (0, NJ + j)),
                  pl.BlockSpec((TF, D), lambda r, j: (j, 0))],
        out_specs=(pl.BlockSpec((tr, TF), lambda r, j: (r, j)),
                   pl.BlockSpec((tr, TF), lambda r, j: (r, j)),
                   pl.BlockSpec((tr, D), lambda r, j: (r, 0))),
        out_shape=(pre, pre, jax.ShapeDtypeStruct((s, D), F32)),
        scratch_shapes=[pltpu.VMEM((NJ, 2, SUB, TF), F32)],
        compiler_params=_params(("arbitrary", "arbitrary"), 56),
    )(h2, wup, wup, cw, cw, wd)


def _ffn_bwd(dy, h2, pre_g, pre_v, wup, cw, wd):
    s = h2.shape[0]
    tr = FFN_ROWS_BWD
    nr = s // tr
    hb = tr // (2 * SUB)

    def body(dy_ref, h_ref, pg_ref, pv_ref, hg_ref, hv_ref, wg_ref, wv_ref, cg_ref, cv_ref, wd_ref,
             dh_ref, dwg_ref, dwv_ref, dwd_ref, dcg_ref, dcv_ref, nxt_ref, awg_ref, awv_ref, awd_ref):
        j, r = pl.program_id(0), pl.program_id(1)
        rr = nr - 1 - r
        row0 = pl.multiple_of(rr * tr, tr)
        cwg, cwv = cg_ref[...], cv_ref[...]
        pg, pv = pg_ref[...].astype(F32), pv_ref[...].astype(F32)
        ug, g1, g2 = _conv_taps(pg, jnp.where(rr > 0, hg_ref[SUB:2 * SUB, :].astype(F32), 0.0), cwg)
        uv, v1, v2 = _conv_taps(pv, jnp.where(rr > 0, hv_ref[SUB:2 * SUB, :].astype(F32), 0.0), cwv)
        sg = _sigmoid(ug)
        sil = ug * sg
        act = (sil * uv).astype(BF16)
        dyv = dy_ref[...]
        da = _dot(dyv, wd_ref[...], NT)
        dug = da * uv * (sg * (1.0 + ug * (1.0 - sg)))
        duv = da * sil
        dpg = _conv_taps_t(dug, jnp.where(r > 0, nxt_ref[0], 0.0), cwg)
        dpv = _conv_taps_t(duv, jnp.where(r > 0, nxt_ref[1], 0.0), cwv)
        nxt_ref[0] = dug[0:SUB, :]
        nxt_ref[1] = duv[0:SUB, :]
        dpgb, dpvb = dpg.astype(BF16), dpv.astype(BF16)
        hv = h_ref[...]
        dwd = _dot(act, dyv, TN)
        dwg = _dot(hv, dpgb, TN)
        dwv = _dot(hv, dpvb, TN)
        dh = _dot(dpgb, wg_ref[...], NT) + _dot(dpvb, wv_ref[...], NT)

        def taps(du, x0, x1, x2):
            return (jnp.sum(du * x2, axis=0, keepdims=True), jnp.sum(du * x1, axis=0, keepdims=True),
                    jnp.sum(du * x0, axis=0, keepdims=True))

        tg, tv = taps(dug, pg, g1, g2), taps(duv, pv, v1, v2)

        @pl.when(r == 0)
        def _():
            awd_ref[...] = dwd
            awg_ref[...] = dwg
            awv_ref[...] = dwv
            dcg_ref[...] = jnp.zeros_like(dcg_ref)
            dcv_ref[...] = jnp.zeros_like(dcv_ref)

        @pl.when(r > 0)
        def _():
            awd_ref[...] += dwd
            awg_ref[...] += dwg
            awv_ref[...] += dwv

        @pl.when(r == nr - 1)
        def _():
            dwd_ref[...] = awd_ref[...].astype(BF16)
            dwg_ref[...] = awg_ref[...].astype(BF16)
            dwv_ref[...] = awv_ref[...].astype(BF16)

        for t in range(3):
            dcg_ref[t:t + 1, :] += tg[t]
            dcv_ref[t:t + 1, :] += tv[t]

        @pl.when(j == 0)
        def _():
            dh_ref[pl.ds(row0, tr), :] = dh

        @pl.when(j > 0)
        def _():
            dh_ref[pl.ds(row0, tr), :] += dh

    rows = lambda j, r: (nr - 1 - r, 0)
    tile = lambda j, r: (nr - 1 - r, j)
    halo = lambda j, r: (jnp.maximum((nr - 1 - r) * hb - 1, 0), j)
    return pl.pallas_call(
        body, name="ffn_bwd", grid=(NJ, nr),
        in_specs=[pl.BlockSpec((tr, D), rows), pl.BlockSpec((tr, D), rows),
                  pl.BlockSpec((tr, TF), tile), pl.BlockSpec((tr, TF), tile),
                  pl.BlockSpec((2 * SUB, TF), halo), pl.BlockSpec((2 * SUB, TF), halo),
                  pl.BlockSpec((D, TF), lambda j, r: (0, j)), pl.BlockSpec((D, TF), lambda j, r: (0, NJ + j)),
                  pl.BlockSpec((3, TF), lambda j, r: (0, j)), pl.BlockSpec((3, TF), lambda j, r: (0, NJ + j)),
                  pl.BlockSpec((TF, D), lambda j, r: (j, 0))],
        out_specs=(pl.BlockSpec((s, D), lambda j, r: (0, 0)),
                   pl.BlockSpec((D, TF), lambda j, r: (0, j)), pl.BlockSpec((D, TF), lambda j, r: (0, j)),
                   pl.BlockSpec((TF, D), lambda j, r: (j, 0)),
                   pl.BlockSpec((SUB, TF), lambda j, r: (0, j)), pl.BlockSpec((SUB, TF), lambda j, r: (0, j))),
        out_shape=(jax.ShapeDtypeStruct((s, D), F32),
                   jax.ShapeDtypeStruct((D, DFF), BF16), jax.ShapeDtypeStruct((D, DFF), BF16),
                   jax.ShapeDtypeStruct((DFF, D), BF16),
                   jax.ShapeDtypeStruct((SUB, DFF), F32), jax.ShapeDtypeStruct((SUB, DFF), F32)),
        scratch_shapes=[pltpu.VMEM((2, SUB, TF), F32), pltpu.VMEM((D, TF), F32), pltpu.VMEM((D, TF), F32),
                        pltpu.VMEM((TF, D), F32)],
        compiler_params=_params(("arbitrary", "arbitrary"), 56),
    )(dy, h2, pre_g, pre_v, pre_g, pre_v, wup, wup, cw, cw, wd)


def _adam(w, g, m, v):
    m = ADAM_B1 * m + (1.0 - ADAM_B1) * g
    v = ADAM_B2 * v + (1.0 - ADAM_B2) * (g * g)
    m_hat = m / (1.0 - ADAM_B1 ** ADAM_STEP)
    v_hat = v / (1.0 - ADAM_B2 ** ADAM_STEP)
    delta = -ADAM_LR * (m_hat / (jnp.sqrt(v_hat) + ADAM_EPS) + ADAM_WD * w)
    return delta, m, v


NCHIP = NDEV // 2


def _pair_add(mine, theirs, tr, name):
    _, _, rws, cols = mine.shape

    def body(a_ref, b_ref, o_ref):
        c = lax.axis_index("c")
        o_ref[0] = (a_ref[c, 0].astype(F32) + b_ref[0].astype(F32)).astype(BF16)

    (out,), _ = _call(
        body, (mine, theirs), name=name, grid=(NCHIP, rws // tr),
        in_specs=[pl.BlockSpec((2, 1, tr, cols), lambda q, i: (0, q, i, 0)),
                  pl.BlockSpec((1, tr, cols), lambda q, i: (q, i, 0))],
        out_specs=[pl.BlockSpec((1, tr, cols), lambda q, i: (q, i, 0))],
        out_shape=[jax.ShapeDtypeStruct((NCHIP, rws, cols), BF16)], vmem_mb=32)
    return out


def _adamw_sharded(parts, w, m, v, tr, name, hosted=None):
    rws, cols = w.shape

    def body(p_ref, w_ref, m_ref, v_ref, g_ref, d_ref, mo_ref, vo_ref):
        g = p_ref[0].astype(F32)
        for q in range(1, NCHIP):
            g = g + p_ref[q].astype(F32)
        g_ref[...] = g
        d_ref[...], mo_ref[...], vo_ref[...] = _adam(w_ref[...], g, m_ref[...], v_ref[...])

    blk = pl.BlockSpec((tr, cols), lambda i: (i, 0))
    o = jax.ShapeDtypeStruct((rws, cols), F32)
    outs, moved = _call(
        body, (parts, w, m, v), name=name, grid=(rws // tr,),
        in_specs=[pl.BlockSpec((NCHIP, tr, cols), lambda i: (0, i, 0)), blk, blk, blk],
        out_specs=[blk, blk, blk, blk], out_shape=[o, o, o, o], vmem_mb=48, hosted=hosted)
    return (outs, moved) if hosted else outs


def _adamw_ada(c_all, dmod_my, w, m, v):
    rws, cols = w.shape
    tr = 256

    def body(c_ref, dm_ref, w_ref, m_ref, v_ref, g_ref, d_ref, mo_ref, vo_ref):
        cv = c_ref[...]
        act = cv * _sigmoid(cv)
        g = _dot(act, dm_ref[...], TN, lax.Precision.HIGHEST)
        g_ref[...] = g
        d_ref[...], mo_ref[...], vo_ref[...] = _adam(w_ref[...], g, m_ref[...], v_ref[...])

    blk = pl.BlockSpec((tr, cols), lambda i: (i, 0))
    o = jax.ShapeDtypeStruct((rws, cols), F32)
    return pl.pallas_call(
        body, name="adamw_ada", grid=(rws // tr,),
        in_specs=[pl.BlockSpec((NDEV, tr), lambda i: (0, i)), _full_spec((NDEV, cols)), blk, blk, blk],
        out_specs=(blk, blk, blk, blk), out_shape=(o, o, o, o),
        compiler_params=_params(("parallel",), 48),
    )(c_all, dmod_my, w, m, v)


REP_ROWS = 16
ROW_N1, ROW_N2, ROW_LOSS, ROW_MISC = 6, 7, 8, 9
LANE_BF, LANE_GQ, LANE_GK = 0, 128, 256


def _adamw_small(rep_all, conv_all, wmv):
    n_ff = wmv[6][0].shape[1]

    def body(*refs):
        rep_ref, conv_ref = refs[:2]
        ins = refs[2:2 + 24]
        outs = refs[2 + 24:]
        loss_ref, outs = outs[0], outs[1:]
        g_rep = rep_ref[0]
        g_conv = conv_ref[0]
        for d in range(1, NDEV):
            g_rep = g_rep + rep_ref[d]
            g_conv = g_conv + conv_ref[d]
        loss_ref[...] = (0.5 / D) * jnp.sum(g_rep[ROW_LOSS:ROW_LOSS + 1, :], axis=-1, keepdims=True)
        grads = [
            None,
            g_rep[ROW_N1:ROW_N1 + 1, :],
            g_rep[ROW_MISC:ROW_MISC + 1, LANE_BF:LANE_BF + HEADS],
            g_rep[ROW_MISC:ROW_MISC + 1, LANE_GQ:LANE_GQ + DH],
            g_rep[ROW_MISC:ROW_MISC + 1, LANE_GK:LANE_GK + DH],
            g_rep[ROW_N2:ROW_N2 + 1, :],
            g_conv[0:3, 0:n_ff],
            g_conv[0:3, n_ff:n_ff + DH],
        ]
        for p in range(8):
            w_ref, m_ref, v_ref = ins[3 * p:3 * p + 3]
            g_ref, d_ref, mo_ref, vo_ref = outs[4 * p:4 * p + 4]
            if p == 0:
                for nmod in range(NMOD):
                    sl = slice(D * nmod, D * (nmod + 1))
                    g = g_rep[nmod:nmod + 1, :]
                    g_ref[:, sl] = g
                    d_ref[:, sl], mo_ref[:, sl], vo_ref[:, sl] = _adam(w_ref[:, sl], g, m_ref[:, sl], v_ref[:, sl])
            else:
                g = grads[p]
                g_ref[...] = g
                d_ref[...], mo_ref[...], vo_ref[...] = _adam(w_ref[...], g, m_ref[...], v_ref[...])

    flat = [a for trio in wmv for a in trio]
    out_shape = [jax.ShapeDtypeStruct((1, 1), F32)]
    for trio in wmv:
        out_shape += [jax.ShapeDtypeStruct(trio[0].shape, F32)] * 4
    return pl.pallas_call(
        body, name="adamw_small", out_shape=tuple(out_shape),
        compiler_params=_params(None, 32),
    )(rep_all, conv_all, *flat)


FG_FIRST = 3 * AW
N_IN = DIN // NDEV


def _w_in_runs():
    runs = []
    for d in range(NDEV):
        lo, hi = N_IN * d, N_IN * (d + 1)
        for a, b, shift in ((0, FG_FIRST, 0), (FG_FIRST, FG_FIRST + HEADS, DIN - HEADS - FG_FIRST),
                            (FG_FIRST + HEADS, DIN, -HEADS)):
            a, b = max(a, lo), min(b, hi)
            if a < b:
                runs.append((d, a - lo, a + shift, b - a))
    return runs


W_IN_ROWS = 256


def _assemble_w_in(g_in):
    def body(g_ref, o_ref):
        for d, src, dst, width in _w_in_runs():
            o_ref[:, dst:dst + width] = g_ref[d, :, src:src + width]
        o_ref[:, DIN:DINP] = jnp.zeros((W_IN_ROWS, DINP - DIN), o_ref.dtype)

    (out,), _ = _call(
        body, (g_in,), name="assemble_w_in", grid=(D // W_IN_ROWS,),
        in_specs=[pl.BlockSpec((NDEV, W_IN_ROWS, N_IN), lambda i: (0, i, 0))],
        out_specs=[pl.BlockSpec((W_IN_ROWS, DINP), lambda i: (i, 0))],
        out_shape=[jax.ShapeDtypeStruct((D, DINP), g_in.dtype)], vmem_mb=32)
    return out


def _scatter_dw_in(dwp):
    def body(w_ref, o_ref):
        for d, src, dst, width in _w_in_runs():
            o_ref[d % 2, d // 2, :, src:src + width] = w_ref[:, dst:dst + width]

    (out,), _ = _call(
        body, (dwp,), name="scatter_dw_in", grid=(D // W_IN_ROWS,),
        in_specs=[pl.BlockSpec((W_IN_ROWS, DINP), lambda i: (i, 0))],
        out_specs=[pl.BlockSpec((2, NCHIP, W_IN_ROWS, N_IN), lambda i: (0, 0, i, 0))],
        out_shape=[jax.ShapeDtypeStruct((2, NCHIP, D, N_IN), dwp.dtype)], vmem_mb=32)
    return out


def _by_core_chip(a):
    return jnp.transpose(a.reshape((NCHIP, 2) + a.shape[1:]), (1, 0, 2, 3))


def kernel(x, c, w_ada, b_ada, norm1_g, w_in, b_forget, q_norm_g, k_norm_g, conv_mix_w, w_out, norm2_g, w_up, ffn_conv_w, w_down, loss_target, m_w_ada, m_b_ada, m_norm1_g, m_w_in, m_b_forget, m_q_norm_g, m_k_norm_g, m_conv_mix_w, m_w_out, m_norm2_g, m_w_up, m_ffn_conv_w, m_w_down, v_w_ada, v_b_ada, v_norm1_g, v_w_in, v_b_forget, v_q_norm_g, v_k_norm_g, v_conv_mix_w, v_w_out, v_norm2_g, v_w_up, v_ffn_conv_w, v_w_down):
    me = 4 * lax.axis_index("x") + 2 * lax.axis_index("y") + lax.axis_index("c")
    xs, tgt = x[0], loss_target[0]
    s = xs.shape[0]
    nq = s // TA
    n_ada = w_ada.shape[2]
    n_ff = w_up.shape[2]

    conv_w = jnp.concatenate([ffn_conv_w[0], conv_mix_w[0]], axis=1)
    conv_w = jnp.concatenate([conv_w, jnp.zeros((SUB - 3, conv_w.shape[1]), F32)], axis=0)
    c_all, conv_all, g_in = _exchange(
        [(c.reshape(SUB, D // SUB), "ag"), (conv_w, "ag"), (w_in[0].astype(BF16), "ag2")], "exchange_w_in")
    c_all = c_all.reshape(NDEV, D)
    cw_ffn = jnp.transpose(conv_all[:, :3, :n_ff], (1, 0, 2)).reshape(3, 2 * DFF)
    cw_mix = jnp.transpose(conv_all[:, :3, n_ff:], (1, 0, 2)).reshape(3, CW)
    w_in_p = _assemble_w_in(g_in)

    b_my = lax.dynamic_slice(b_ada, (0, me * n_ada), (1, n_ada))
    mod_part = _ada_fwd(c_all, w_ada[0], b_my)
    (mod_rows,) = _exchange([(jnp.broadcast_to(mod_part[:, None, :], (NDEV, SUB, n_ada)), "a2a")], "exchange_mod")
    mod = mod_rows[:, 0, :].reshape(NMOD, D)
    mod = jnp.concatenate([mod, jnp.zeros((SUB - NMOD, D), F32)], axis=0)

    h = _norm_mod_fwd(xs, mod, norm1_g)
    proj = _mm(h, w_in_p, "nn", F32, 512, 640, "proj_fwd")
    bf_pad = jnp.concatenate([b_forget, jnp.zeros((1, LANES - HEADS), F32)], axis=1)
    fcum = _fgate_fwd(proj, bf_pad)
    qp, kp, vp = _qkv_prep(proj, fcum, q_norm_g, k_norm_g)
    attn, lse, (g_out, g_up, g_down) = _attn_fwd(
        qp, kp, vp,
        [(w_out[0].astype(BF16), "ag2"), (w_up[0].astype(BF16), "ag2"), (w_down[0].astype(BF16), "ag2")])
    w_out_f = g_out.reshape(D, D)
    w_up_f = jnp.transpose(g_up, (1, 0, 2)).reshape(D, 2 * DFF)
    w_down_f = g_down.reshape(DFF, D)
    conv = _mixconv_fwd(proj, cw_mix)
    mixed = jnp.concatenate([attn, conv], axis=1).astype(BF16)
    z = _mm(mixed, w_out_f, "nn", F32, 512, 512, "out_fwd")
    x1, h2 = _resid_norm2(xs, z, mod, norm2_g)
    pre_g, pre_v, y = _ffn_fwd(h2, w_up_f, cw_ffn, w_down_f)
    dout, dy, vec_l = _loss_head(x1, y, tgt, mod)

    dh2, dwup_g, dwup_v, dwd, dcw_g, dcw_v = _ffn_bwd(dy, h2, pre_g, pre_v, w_up_f, cw_ffn, w_down_f)
    dx1, dz, vec_2 = _norm_mod_bwd(dh2, x1, dout, z, mod, norm2_g, 3, 4, 2, "norm2_bwd")
    dwout = _mm(mixed, dz, "tn", BF16, 512, 512, "out_bwd_w")
    s_out = _by_core_chip(dwout.reshape(NDEV, D // NDEV, D))
    s_down = _by_core_chip(dwd.reshape(NDEV, DFF // NDEV, D))
    s_up = jnp.transpose(jnp.concatenate([dwup_g, dwup_v], axis=1).reshape(D, NCHIP, 2, n_ff), (2, 1, 0, 3))
    dmixed, (t_out, t_up, t_down) = _mm(dz, w_out_f, "nt", F32, 512, 512, "out_bwd_x",
                                        hosted=[(s_out, "pair"), (s_up, "pair"), (s_down, "pair")])
    c_out = _pair_add(s_out, t_out, 128, "pair_add_out")
    c_up = _pair_add(s_up, t_up, 256, "pair_add_up")
    c_down = _pair_add(s_down, t_down, 176, "pair_add_down")
    dxin, dbg, dcg, dcw_mix = _mixconv_bwd(dmixed, proj, cw_mix)
    (dqp, dkp, dvp), (p_out, p_up, p_down) = _attn_bwd(
        qp, kp, vp, dmixed, attn, lse, [(c_out, "chips"), (c_up, "chips"), (c_down, "chips")])
    dq, dk, dvb, dfcol, vec_qk = _qkv_post(dqp, dkp, dvp, proj, q_norm_g, k_norm_g)
    dfg, vec_bf = _fgate_bwd(dfcol, proj, bf_pad)
    dproj = jnp.concatenate([dq, dk, dvb, dxin, dbg, dcg, dfg], axis=1)
    dwin_p = _mm(h, dproj, "tn", BF16, 512, 640, "proj_bwd_w")
    s_in = _scatter_dw_in(dwin_p)
    (t_in,) = _exchange([(s_in, "pair")], "exchange_pair_in")
    c_in = _pair_add(s_in, t_in, 256, "pair_add_in")
    dh, (p_in,) = _mm(dproj, w_in_p, "nt", F32, 512, 512, "proj_bwd_x", hosted=[(c_in, "chips")])
    grad_x, vec_1 = _norm_mod_bwd(dh, xs, dx1, None, mod, norm1_g, 0, 1, None, "norm1_bwd")

    misc = jnp.zeros((1, D), F32)
    misc = lax.dynamic_update_slice(misc, vec_bf[0:1, :HEADS], (0, LANE_BF))
    misc = lax.dynamic_update_slice(misc, vec_qk[0:1, :DH], (0, LANE_GQ))
    misc = lax.dynamic_update_slice(misc, vec_qk[1:2, :DH], (0, LANE_GK))
    rep = jnp.concatenate([
        vec_1[0:1], vec_1[1:2], vec_2[3:4], vec_2[0:1], vec_2[1:2], vec_l[0:1],
        vec_1[2:3], vec_2[2:3], vec_l[1:2], misc, jnp.zeros((REP_ROWS - 10, D), F32)], axis=0)
    dcw_ffn = jnp.concatenate([dcw_g, dcw_v], axis=1).reshape(SUB, NDEV, n_ff)
    dcw_all = jnp.concatenate([jnp.transpose(dcw_ffn, (1, 0, 2)),
                               jnp.transpose(dcw_mix.reshape(SUB, NDEV, DH), (1, 0, 2))], axis=2)
    r_out, (rep_all, conv_parts) = _adamw_sharded(p_out, w_out[0], m_w_out[0], v_w_out[0], 128, "adamw_out",
                                                  hosted=[(rep, "ag"), (dcw_all, "a2a")])
    dmod_my = lax.dynamic_slice(rep_all[:, :NMOD, :].reshape(NDEV, NMOD * D), (0, me * n_ada), (NDEV, n_ada))
    r_ada = _adamw_ada(c_all, dmod_my, w_ada[0], m_w_ada[0], v_w_ada[0])
    r_in = _adamw_sharded(p_in, w_in[0], m_w_in[0], v_w_in[0], 256, "adamw_in")
    r_up = _adamw_sharded(p_up, w_up[0], m_w_up[0], v_w_up[0], 256, "adamw_up")
    r_down = _adamw_sharded(p_down, w_down[0], m_w_down[0], v_w_down[0], 176, "adamw_down")
    small = _adamw_small(rep_all, conv_parts, [
        [b_ada, m_b_ada, v_b_ada], [norm1_g, m_norm1_g, v_norm1_g], [b_forget, m_b_forget, v_b_forget],
        [q_norm_g, m_q_norm_g, v_q_norm_g], [k_norm_g, m_k_norm_g, v_k_norm_g], [norm2_g, m_norm2_g, v_norm2_g],
        [ffn_conv_w[0], m_ffn_conv_w[0], v_ffn_conv_w[0]], [conv_mix_w[0], m_conv_mix_w[0], v_conv_mix_w[0]]])
    loss = small[0].reshape(())
    r_bada, r_n1, r_bf, r_gq, r_gk, r_n2, r_cf, r_cm = [small[1 + 4 * p:5 + 4 * p] for p in range(8)]
    lead = lambda t: tuple(a[None] for a in t)
    per_w = [lead(r_ada), r_bada, r_n1, lead(r_in), r_bf, r_gq, r_gk, lead(r_cm), lead(r_out), r_n2,
             lead(r_up), lead(r_cf), lead(r_down)]
    outs = [loss, grad_x[None]]
    for field in range(4):
        outs += [t[field] for t in per_w]
    return tuple(outs)
```

```python
import functools

import jax
import jax.numpy as jnp
import numpy as np
from jax import lax
from jax.experimental import pallas as pl
from jax.experimental.pallas import tpu as pltpu

F32 = jnp.float32
BF16 = jnp.bfloat16

NDEV = 8
D = 1024
HEADS = 8
DH = 64
AW = 512
CW = 512
DFF = 2816
DIN = 3080
DINP = 3200
NMOD = 6
EPS = 1e-6
QK_SCALE = 0.125
LANES = 128
SUB = 8

ADAM_LR = 0.001
ADAM_B1 = 0.9
ADAM_B2 = 0.999
ADAM_EPS = 1e-08
ADAM_WD = 0.01
ADAM_STEP = 10

MESH = pl.DeviceIdType.MESH
ANY = pl.BlockSpec(memory_space=pl.ANY)

NN = (((1,), (0,)), ((), ()))
NT = (((1,), (1,)), ((), ()))
TN = (((0,), (0,)), ((), ()))


def _dot(a, b, dims=NN, precision=None):
    return lax.dot_general(a, b, dims, precision=precision, preferred_element_type=F32)


def _params(sem=None, vmem_mb=None):
    kw = {}
    if sem is not None:
        kw["dimension_semantics"] = sem
    if vmem_mb is not None:
        kw["vmem_limit_bytes"] = vmem_mb * 1024 * 1024
    return pltpu.CompilerParams(**kw)


def _sigmoid(x):
    return 1.0 / (1.0 + jnp.exp(-x))


class _Exchange:
    def __init__(self, items):
        self.arrays = [a for a, _ in items]
        self.modes = [m for _, m in items]
        self.n = len(items)
        self.out_shape = []
        for a, m in items:
            sh = {"ag": (NDEV,) + a.shape, "ag2": (NDEV,) + a.shape, "pair": a.shape[1:]}.get(m, a.shape)
            self.out_shape.append(jax.ShapeDtypeStruct(sh, a.dtype))
        self.scratch = [pltpu.SemaphoreType.DMA((self.n, NDEV - 1)), pltpu.SemaphoreType.DMA((self.n, NDEV - 1)),
                        pltpu.SemaphoreType.DMA((self.n,))]

    def _plan(self, srcs, outs, sems):
        send_sems, recv_sems, loc_sems = sems
        x, y, c = lax.axis_index("x"), lax.axis_index("y"), lax.axis_index("c")
        me, my_chip = 4 * x + 2 * y + c, 2 * x + y
        sib = (x, y, 1 - c)
        local, first, landed, forwards, arrivals = [], [], [], [], []

        def remote(a, k, src, dst, to):
            return pltpu.make_async_remote_copy(src_ref=src, dst_ref=dst, send_sem=send_sems.at[a, k],
                                                recv_sem=recv_sems.at[a, k], device_id=to, device_id_type=MESH)

        for a, mode in enumerate(self.modes):
            src, out = srcs[a], outs[a]
            if mode in ("ag", "a2a"):
                piece = (lambda slot, src=src: src) if mode == "ag" else (lambda slot, src=src: src.at[slot])
                local.append(pltpu.make_async_copy(piece(me), out.at[me], loc_sems.at[a]))
                for r in range(1, NDEV):
                    px = 1 - x if (r >> 2) & 1 else x
                    py = 1 - y if (r >> 1) & 1 else y
                    pc = 1 - c if r & 1 else c
                    pidx = 4 * px + 2 * py + pc
                    first.append(remote(a, r - 1, piece(pidx), out.at[me], (px, py, pc)))
                    arrivals.append(remote(a, r - 1, piece(pidx), out.at[pidx], (px, py, pc)))
            elif mode == "ag2":
                local.append(pltpu.make_async_copy(src, out.at[me], loc_sems.at[a]))
                first.append(remote(a, 0, src, out.at[me], sib))
                arrivals.append(remote(a, 0, src, out.at[me + 1 - 2 * c], sib))
                for j, (px, py) in enumerate([(1 - x, y), (x, 1 - y), (1 - x, 1 - y)]):
                    theirs = out.at[4 * px + 2 * py + c]
                    first.append(remote(a, 1 + j, src, out.at[me], (px, py, c)))
                    landed.append(remote(a, 1 + j, src, theirs, (px, py, c)))
                    forwards.append(remote(a, 4 + j, theirs, theirs, sib))
                    arrivals.append(remote(a, 4 + j, src, out.at[4 * px + 2 * py + 1 - c], sib))
            elif mode == "pair":
                first.append(remote(a, 0, src.at[1 - c], out, sib))
                arrivals.append(remote(a, 0, src.at[1 - c], out, sib))
            else:
                assert mode == "chips", mode
                local.append(pltpu.make_async_copy(src.at[my_chip], out.at[my_chip], loc_sems.at[a]))
                for j, (px, py) in enumerate([(1 - x, y), (x, 1 - y), (1 - x, 1 - y)]):
                    q = 2 * px + py
                    first.append(remote(a, 1 + j, src.at[q], out.at[my_chip], (px, py, c)))
                    arrivals.append(remote(a, 1 + j, src.at[q], out.at[q], (px, py, c)))
        return local, first, landed, forwards, arrivals

    def start(self, srcs, outs, sems):
        local, first, _, _, _ = self._plan(srcs, outs, sems)
        for cp in local + first:
            cp.start()

    def wait(self, srcs, outs, sems):
        local, first, landed, forwards, arrivals = self._plan(srcs, outs, sems)
        for cp, fwd in zip(landed, forwards):
            cp.wait_recv()
            fwd.start()
        for cp in arrivals:
            cp.wait_recv()
        for cp in first + forwards:
            cp.wait_send()
        for cp in local:
            cp.wait()


def _exchange(items, name):
    ex = _Exchange(items)
    n = ex.n

    def body(*refs):
        srcs, outs, sems = refs[:n], refs[n:2 * n], refs[2 * n:]
        ex.start(srcs, outs, sems)
        ex.wait(srcs, outs, sems)

    return pl.pallas_call(
        body, name=name,
        out_shape=tuple(ex.out_shape),
        in_specs=[ANY] * n, out_specs=tuple([ANY] * n),
        scratch_shapes=ex.scratch,
        compiler_params=pltpu.CompilerParams(has_side_effects=True),
    )(*ex.arrays)


def _call(body, inputs, *, name, grid, in_specs, out_specs, out_shape, scratch_shapes=(), vmem_mb=None, hosted=None):
    out_specs, out_shape, scratch_shapes = tuple(out_specs), tuple(out_shape), list(scratch_shapes)
    if not hosted:
        res = pl.pallas_call(
            body, name=name, grid=grid, in_specs=list(in_specs), out_specs=out_specs, out_shape=out_shape,
            scratch_shapes=scratch_shapes, compiler_params=_params(("arbitrary",) * len(grid), vmem_mb),
        )(*inputs)
        return tuple(res), ()
    ex = _Exchange(hosted)
    n, n_in, n_out, n_scr = ex.n, len(inputs), len(out_shape), len(scratch_shapes)

    def hosting_body(*refs):
        ins, srcs = refs[:n_in], refs[n_in:n_in + n]
        outs, landing = refs[n_in + n:n_in + n + n_out], refs[n_in + n + n_out:n_in + 2 * n + n_out]
        scratch, sems = refs[n_in + 2 * n + n_out:n_in + 2 * n + n_out + n_scr], refs[n_in + 2 * n + n_out + n_scr:]
        first = functools.reduce(jnp.logical_and, [pl.program_id(d) == 0 for d in range(len(grid))])
        last = functools.reduce(jnp.logical_and, [pl.program_id(d) == grid[d] - 1 for d in range(len(grid))])

        @pl.when(first)
        def _():
            ex.start(srcs, landing, sems)

        body(*ins, *outs, *scratch)

        @pl.when(last)
        def _():
            ex.wait(srcs, landing, sems)

    res = pl.pallas_call(
        hosting_body, name=name, grid=grid,
        in_specs=list(in_specs) + [ANY] * n, out_specs=out_specs + tuple([ANY] * n),
        out_shape=out_shape + tuple(ex.out_shape), scratch_shapes=scratch_shapes + ex.scratch,
        compiler_params=_params(("arbitrary",) * len(grid), vmem_mb),
    )(*inputs, *ex.arrays)
    return tuple(res[:n_out]), tuple(res[n_out:])


def _mm(a, b, mode, out_dtype, tm, tn, name, hosted=None):
    if mode == "nn":
        (m, k), n = a.shape, b.shape[1]
        a_spec = pl.BlockSpec((tm, k), lambda i, j: (i, 0))
        b_spec = pl.BlockSpec((k, tn), lambda i, j: (0, j))
        dims = NN
    elif mode == "nt":
        (m, k), n = a.shape, b.shape[0]
        a_spec = pl.BlockSpec((tm, k), lambda i, j: (i, 0))
        b_spec = pl.BlockSpec((tn, k), lambda i, j: (j, 0))
        dims = NT
    else:
        (k, m), n = a.shape, b.shape[1]
        a_spec = pl.BlockSpec((k, tm), lambda i, j: (0, i))
        b_spec = pl.BlockSpec((k, tn), lambda i, j: (0, j))
        dims = TN
    assert m % tm == 0 and n % tn == 0, (m, n, tm, tn)

    def body(a_ref, b_ref, o_ref):
        o_ref[...] = _dot(a_ref[...], b_ref[...], dims).astype(o_ref.dtype)

    (out,), moved = _call(
        body, (a, b), name=name, grid=(m // tm, n // tn),
        in_specs=[a_spec, b_spec], out_specs=[pl.BlockSpec((tm, tn), lambda i, j: (i, j))],
        out_shape=[jax.ShapeDtypeStruct((m, n), out_dtype)], vmem_mb=48, hosted=hosted)
    return (out, moved) if hosted else out


def _shift_down(x, k, fill):
    row = lax.broadcasted_iota(jnp.int32, x.shape, 0)
    y = pltpu.roll(x, k, 0)
    for t in range(k):
        y = jnp.where(row == t, fill[t], y)
    return y


def _shift_up(x, k, fill):
    n = x.shape[0]
    row = lax.broadcasted_iota(jnp.int32, x.shape, 0)
    y = pltpu.roll(x, n - k, 0)
    for t in range(k):
        y = jnp.where(row == n - k + t, fill[t], y)
    return y


def _conv_taps(x, halo, w):
    if halo is None:
        f1, f2 = [0.0], [0.0, 0.0]
    else:
        f1, f2 = [halo[7:8, :]], [halo[6:7, :], halo[7:8, :]]
    s1 = _shift_down(x, 1, f1)
    s2 = _shift_down(x, 2, f2)
    u = w[2:3, :] * x + w[1:2, :] * s1 + w[0:1, :] * s2
    return u, s1, s2


def _conv_taps_t(du, nxt, w):
    if nxt is None:
        f1, f2 = [0.0], [0.0, 0.0]
    else:
        f1, f2 = [nxt[0:1, :]], [nxt[0:1, :], nxt[1:2, :]]
    return w[2:3, :] * du + w[1:2, :] * _shift_up(du, 1, f1) + w[0:1, :] * _shift_up(du, 2, f2)


def _ada_fwd(c_all, w_ada, b_my):
    def body(c_ref, w_ref, b_ref, o_ref):
        cv = c_ref[...]
        act = cv * _sigmoid(cv)
        o_ref[...] = _dot(act, w_ref[...], NN, lax.Precision.HIGHEST) + b_ref[...]

    return pl.pallas_call(
        body, name="ada_fwd",
        out_shape=jax.ShapeDtypeStruct((NDEV, w_ada.shape[1]), F32),
        compiler_params=_params(None, 32),
    )(c_all, w_ada, b_my)


TR = 256


def _row_spec(width, col=0):
    return pl.BlockSpec((TR, width), lambda i, col=col: (i, col))


def _full_spec(shape):
    return pl.BlockSpec(shape, lambda i: (0,) * len(shape))


def _norm_mod_fwd(x, mod, g):
    s = x.shape[0]

    def body(x_ref, mod_ref, g_ref, h_ref):
        xv = x_ref[...]
        r = lax.rsqrt(jnp.mean(xv * xv, axis=-1, keepdims=True) + EPS)
        nrm = xv * r * g_ref[...]
        h_ref[...] = (nrm * (1.0 + mod_ref[1:2, :]) + mod_ref[0:1, :]).astype(BF16)

    return pl.pallas_call(
        body, name="norm1_fwd", grid=(s // TR,),
        in_specs=[_row_spec(D), _full_spec((SUB, D)), _full_spec((1, D))],
        out_specs=_row_spec(D), out_shape=jax.ShapeDtypeStruct((s, D), BF16),
        compiler_params=_params(("parallel",)),
    )(x, mod, g)


SLAB = 2 * DH
AUG_F, AUG_ONE, AUG_LSE = 0, 3, 6


def _split3(x):
    hi = x.astype(BF16).astype(F32)
    r1 = x - hi
    mid = r1.astype(BF16).astype(F32)
    return hi, mid, r1 - mid


def _lanes3(lane, first, pieces, other):
    out = other
    for k in range(3):
        out = jnp.where(lane == first + k, pieces[k], out)
    return out


def _aug_placement():
    eq = np.zeros((3 * LANES, HEADS * SLAB), np.float32)
    ek = np.zeros((3 * LANES, HEADS * SLAB), np.float32)
    ones = np.zeros((SUB, HEADS * SLAB), np.float32)
    for h in range(HEADS):
        aug = SLAB * h + DH
        for k in range(3):
            eq[LANES * k + h, aug + AUG_F + k] = 1.0
            ek[LANES * k + h, aug + AUG_ONE + k] = -1.0
            ones[0, aug + AUG_ONE + k] = 1.0
            ones[1, aug + AUG_F + k] = ones[1, aug + AUG_LSE + k] = 1.0
            ones[2, aug + k] = 1.0
    return jnp.asarray(eq, BF16), jnp.asarray(ek, BF16), jnp.asarray(ones)


def _qkv_prep(proj, fcum, gq, gk, hosted):
    s = proj.shape[0]

    def body(q_ref, k_ref, v_ref, f_ref, gq_ref, gk_ref, eq_ref, ek_ref, ones_ref, qo_ref, ko_ref, vo_ref):
        f3 = jnp.concatenate(_split3(f_ref[...]), axis=1).astype(BF16)
        qo_ref[...] = (_dot(f3, eq_ref[...]) + ones_ref[0:1, :]).astype(BF16)
        ko_ref[...] = (_dot(f3, ek_ref[...]) + ones_ref[1:2, :]).astype(BF16)
        vo_ref[...] = jnp.broadcast_to(ones_ref[2:3, :], vo_ref.shape).astype(BF16)
        for h in range(HEADS):
            sl = slice(DH * h, DH * (h + 1))
            lo = slice(SLAB * h, SLAB * h + DH)
            qh = q_ref[:, sl]
            r = lax.rsqrt(jnp.mean(qh * qh, axis=-1, keepdims=True) + EPS)
            qo_ref[:, lo] = (qh * r * gq_ref[...] * QK_SCALE).astype(BF16)
            kh = k_ref[:, sl]
            r = lax.rsqrt(jnp.mean(kh * kh, axis=-1, keepdims=True) + EPS)
            ko_ref[:, lo] = (kh * r * gk_ref[...]).astype(BF16)
            vo_ref[:, lo] = v_ref[:, sl].astype(BF16)

    eq, ek, ones = _aug_placement()
    o = jax.ShapeDtypeStruct((s, HEADS * SLAB), BF16)
    wide = _row_spec(HEADS * SLAB)
    return _call(
        body, (proj, proj, proj, fcum, gq, gk, eq, ek, ones), name="qkv_prep", grid=(s // TR,),
        in_specs=[_row_spec(AW, 0), _row_spec(AW, 1), _row_spec(AW, 2), _row_spec(LANES),
                  _full_spec((1, DH)), _full_spec((1, DH)), _full_spec(eq.shape), _full_spec(ek.shape),
                  _full_spec(ones.shape)],
        out_specs=[wide, wide, wide], out_shape=[o, o, o], vmem_mb=32, hosted=hosted)


FG_BLOCK = (3 * AW + 3 * CW) // LANES


def _fgate_fwd(proj, bf_pad):
    s = proj.shape[0]

    def body(fg_ref, b_ref, o_ref, carry_ref):
        i = pl.program_id(0)

        @pl.when(i == 0)
        def _():
            carry_ref[...] = jnp.zeros_like(carry_ref)

        z = fg_ref[...] + b_ref[...]
        logf = jnp.minimum(z, 0.0) - jnp.log1p(jnp.exp(-jnp.abs(z)))
        row = lax.broadcasted_iota(jnp.int32, (TR, TR), 0)
        col = lax.broadcasted_iota(jnp.int32, (TR, TR), 1)
        tri = (col <= row).astype(F32)
        cs = _dot(tri, logf, NN, lax.Precision.HIGHEST) + carry_ref[0:1, :]
        o_ref[...] = cs
        carry_ref[...] = jnp.broadcast_to(cs[TR - 1:TR, :], carry_ref.shape)

    return pl.pallas_call(
        body, name="fgate_fwd", grid=(s // TR,),
        in_specs=[_row_spec(LANES, FG_BLOCK), _full_spec((1, LANES))],
        out_specs=_row_spec(LANES), out_shape=jax.ShapeDtypeStruct((s, LANES), F32),
        scratch_shapes=[pltpu.VMEM((SUB, LANES), F32)],
        compiler_params=_params(("arbitrary",)),
    )(proj, bf_pad)


def _fgate_bwd(dfcol, proj, bf_pad):
    s = proj.shape[0]
    nb = s // TR

    def body(df_ref, fg_ref, b_ref, o_ref, db_ref, carry_ref):
        i = pl.program_id(0)

        @pl.when(i == 0)
        def _():
            carry_ref[...] = jnp.zeros_like(carry_ref)
            db_ref[...] = jnp.zeros_like(db_ref)

        row = lax.broadcasted_iota(jnp.int32, (TR, TR), 0)
        col = lax.broadcasted_iota(jnp.int32, (TR, TR), 1)
        tri = (col >= row).astype(F32)
        dlogf = _dot(tri, df_ref[...], NN, lax.Precision.HIGHEST) + carry_ref[0:1, :]
        carry_ref[...] = jnp.broadcast_to(dlogf[0:1, :], carry_ref.shape)
        z = fg_ref[...] + b_ref[...]
        dfg = dlogf * _sigmoid(-z)
        o_ref[...] = dfg.astype(BF16)
        db_ref[0:1, :] += jnp.sum(dfg, axis=0, keepdims=True)

    rev = lambda col: pl.BlockSpec((TR, LANES), lambda i, col=col: (nb - 1 - i, col))
    return pl.pallas_call(
        body, name="fgate_bwd", grid=(nb,),
        in_specs=[rev(0), rev(FG_BLOCK), _full_spec((1, LANES))],
        out_specs=(rev(0), _full_spec((SUB, LANES))),
        out_shape=(jax.ShapeDtypeStruct((s, LANES), BF16), jax.ShapeDtypeStruct((SUB, LANES), F32)),
        scratch_shapes=[pltpu.VMEM((SUB, LANES), F32)],
        compiler_params=_params(("arbitrary",)),
    )(dfcol, proj, bf_pad)


def _resid_norm2(x, z, mod, g):
    s = x.shape[0]

    def body(x_ref, z_ref, mod_ref, g_ref, x1_ref, h_ref):
        x1 = x_ref[...] + mod_ref[2:3, :] * z_ref[...]
        x1_ref[...] = x1
        r = lax.rsqrt(jnp.mean(x1 * x1, axis=-1, keepdims=True) + EPS)
        nrm = x1 * r * g_ref[...]
        h_ref[...] = (nrm * (1.0 + mod_ref[4:5, :]) + mod_ref[3:4, :]).astype(BF16)

    return pl.pallas_call(
        body, name="resid_norm2", grid=(s // TR,),
        in_specs=[_row_spec(D), _row_spec(D), _full_spec((SUB, D)), _full_spec((1, D))],
        out_specs=(_row_spec(D), _row_spec(D)),
        out_shape=(jax.ShapeDtypeStruct((s, D), F32), jax.ShapeDtypeStruct((s, D), BF16)),
        compiler_params=_params(("parallel",)),
    )(x, z, mod, g)


def _loss_head(x1, y, tgt, mod):
    s = x1.shape[0]

    def body(x1_ref, y_ref, t_ref, mod_ref, dout_ref, dy_ref, vec_ref):
        @pl.when(pl.program_id(0) == 0)
        def _():
            vec_ref[...] = jnp.zeros_like(vec_ref)

        yv = y_ref[...]
        g2 = mod_ref[5:6, :]
        diff = x1_ref[...] + g2 * yv - t_ref[...]
        dout = diff * (1.0 / D)
        dout_ref[...] = dout
        dy_ref[...] = (g2 * dout).astype(BF16)
        vec_ref[0:1, :] += jnp.sum(dout * yv, axis=0, keepdims=True)
        vec_ref[1:2, :] += jnp.sum(diff * diff, axis=0, keepdims=True)

    return pl.pallas_call(
        body, name="loss_head", grid=(s // TR,),
        in_specs=[_row_spec(D), _row_spec(D), _row_spec(D), _full_spec((SUB, D))],
        out_specs=(_row_spec(D), _row_spec(D), _full_spec((SUB, D))),
        out_shape=(jax.ShapeDtypeStruct((s, D), F32), jax.ShapeDtypeStruct((s, D), BF16),
                   jax.ShapeDtypeStruct((SUB, D), F32)),
        compiler_params=_params(("arbitrary",)),
    )(x1, y, tgt, mod)


def _norm_mod_bwd(dh, xin, dres, zin, mod, g, shift_row, scale_row, gate_row, name, hosted=None):
    s = dh.shape[0]
    with_gate = gate_row is not None

    def body(*refs):
        if with_gate:
            dh_ref, x_ref, dres_ref, z_ref, mod_ref, g_ref, dx_ref, dz_ref, vec_ref = refs
        else:
            dh_ref, x_ref, dres_ref, mod_ref, g_ref, dx_ref, vec_ref = refs

        @pl.when(pl.program_id(0) == 0)
        def _():
            vec_ref[...] = jnp.zeros_like(vec_ref)

        xv = x_ref[...]
        dhv = dh_ref[...]
        gv = g_ref[...]
        r = lax.rsqrt(jnp.mean(xv * xv, axis=-1, keepdims=True) + EPS)
        xh = xv * r
        dn = dhv * (1.0 + mod_ref[scale_row:scale_row + 1, :])
        dxh = dn * gv
        dx = dres_ref[...] + r * (dxh - xh * jnp.mean(dxh * xh, axis=-1, keepdims=True))
        dx_ref[...] = dx
        vec_ref[0:1, :] += jnp.sum(dhv, axis=0, keepdims=True)
        vec_ref[1:2, :] += jnp.sum(dhv * (xh * gv), axis=0, keepdims=True)
        vec_ref[2:3, :] += jnp.sum(dn * xh, axis=0, keepdims=True)
        if with_gate:
            dz_ref[...] = (mod_ref[gate_row:gate_row + 1, :] * dx).astype(BF16)
            vec_ref[3:4, :] += jnp.sum(dx * z_ref[...], axis=0, keepdims=True)

    ins = [dh, xin, dres] + ([zin] if with_gate else []) + [mod, g]
    in_specs = [_row_spec(D)] * (4 if with_gate else 3) + [_full_spec((SUB, D)), _full_spec((1, D))]
    out_specs = [_row_spec(D)] + ([_row_spec(D)] if with_gate else []) + [_full_spec((SUB, D))]
    out_shape = [jax.ShapeDtypeStruct((s, D), F32)] + ([jax.ShapeDtypeStruct((s, D), BF16)] if with_gate else []) \
        + [jax.ShapeDtypeStruct((SUB, D), F32)]
    outs, moved = _call(body, ins, name=name, grid=(s // TR,), in_specs=in_specs, out_specs=out_specs,
                        out_shape=out_shape, hosted=hosted)
    return outs + (moved,) if hosted else outs


XIN_BLOCK = 3 * AW // LANES
BG_BLOCK = XIN_BLOCK + CW // LANES
CG_BLOCK = BG_BLOCK + CW // LANES


def _seq_spec(s, first_block):
    return pl.BlockSpec((s, LANES), lambda j, fb=first_block: (0, fb + j))


def _mixconv_fwd(proj, w):
    s = proj.shape[0]

    def body(xin_ref, bg_ref, cg_ref, w_ref, o_ref):
        cx = cg_ref[...] * xin_ref[...]
        cv, _, _ = _conv_taps(cx, None, w_ref[...])
        o_ref[...] = bg_ref[...] * cv

    return pl.pallas_call(
        body, name="mixconv_fwd", grid=(CW // LANES,),
        in_specs=[_seq_spec(s, XIN_BLOCK), _seq_spec(s, BG_BLOCK), _seq_spec(s, CG_BLOCK),
                  pl.BlockSpec((3, LANES), lambda j: (0, j))],
        out_specs=_seq_spec(s, 0), out_shape=jax.ShapeDtypeStruct((s, CW), F32),
        compiler_params=_params(("parallel",), 48),
    )(proj, proj, proj, w)


def _mixconv_bwd(dmixed, proj, w):
    s = proj.shape[0]

    def body(d_ref, xin_ref, bg_ref, cg_ref, w_ref, dxin_ref, dbg_ref, dcg_ref, dw_ref):
        wv = w_ref[...]
        xin, cg, dconv = xin_ref[...], cg_ref[...], d_ref[...]
        cx = cg * xin
        cv, s1, s2 = _conv_taps(cx, None, wv)
        dbg_ref[...] = (dconv * cv).astype(BF16)
        dcv = dconv * bg_ref[...]
        dw_ref[...] = jnp.zeros_like(dw_ref)
        dw_ref[0:1, :] = jnp.sum(dcv * s2, axis=0, keepdims=True)
        dw_ref[1:2, :] = jnp.sum(dcv * s1, axis=0, keepdims=True)
        dw_ref[2:3, :] = jnp.sum(dcv * cx, axis=0, keepdims=True)
        dcx = _conv_taps_t(dcv, None, wv)
        dcg_ref[...] = (dcx * xin).astype(BF16)
        dxin_ref[...] = (dcx * cg).astype(BF16)

    o = jax.ShapeDtypeStruct((s, CW), BF16)
    return pl.pallas_call(
        body, name="mixconv_bwd", grid=(CW // LANES,),
        in_specs=[_seq_spec(s, AW // LANES), _seq_spec(s, XIN_BLOCK), _seq_spec(s, BG_BLOCK), _seq_spec(s, CG_BLOCK),
                  pl.BlockSpec((3, LANES), lambda j: (0, j))],
        out_specs=(_seq_spec(s, 0), _seq_spec(s, 0), _seq_spec(s, 0), pl.BlockSpec((SUB, LANES), lambda j: (0, j))),
        out_shape=(o, o, o, jax.ShapeDtypeStruct((SUB, CW), F32)),
        compiler_params=_params(("parallel",), 48),
    )(dmixed, proj, proj, proj, w)


TA = 512
NEG = -1e30


def _causal_mask():
    row = lax.broadcasted_iota(jnp.int32, (TA, TA), 0)
    col = lax.broadcasted_iota(jnp.int32, (TA, TA), 1)
    return col <= row


def _attn_fwd(qp, kp, vp, hosted):
    s = qp.shape[0]
    nq = s // TA

    def body(q_ref, k_ref, v_ref, o_ref, lse_ref):
        i = pl.program_id(1)
        slabs = [slice(SLAB * hh, SLAB * (hh + 1)) for hh in range(2)]
        q = [q_ref[:, sl] for sl in slabs]

        def block(j, carry, masked):
            keys = pl.ds(pl.multiple_of(j * TA, TA), TA)
            ms, acc = carry
            m_out, parts = [], []
            for hh in range(2):
                sc = _dot(q[hh], k_ref[keys, slabs[hh]], NT)
                if masked:
                    sc = jnp.where(_causal_mask(), sc, NEG)
                m_new = jnp.maximum(ms[hh], jnp.max(sc, axis=-1, keepdims=True))
                p = jnp.exp(sc - m_new)
                parts.append(jnp.exp(ms[hh] - m_new) * acc[:, slabs[hh]] + _dot(p.astype(BF16), v_ref[keys, slabs[hh]]))
                m_out.append(m_new)
            return tuple(m_out), jnp.concatenate(parts, axis=1)

        init = ((jnp.full((TA, 1), NEG, F32), jnp.full((TA, 1), NEG, F32)), jnp.zeros((TA, 2 * SLAB), F32))
        carry = lax.fori_loop(0, i, lambda j, cr: block(j, cr, False), init)
        ms, acc = block(i, carry, True)
        for hh in range(2):
            l = acc[:, SLAB * hh + DH:SLAB * hh + DH + 1]
            o_ref[:, DH * hh:DH * (hh + 1)] = acc[:, SLAB * hh:SLAB * hh + DH] / l
            lse_ref[0, :, hh:hh + 1] = ms[hh] + jnp.log(l)

    (o, lse), moved = _call(
        body, (qp, kp, vp), name="attn_fwd", grid=(HEADS // 2, nq),
        in_specs=[pl.BlockSpec((TA, 2 * SLAB), lambda p, i: (i, p)),
                  pl.BlockSpec((s, 2 * SLAB), lambda p, i: (0, p)),
                  pl.BlockSpec((s, 2 * SLAB), lambda p, i: (0, p))],
        out_specs=[pl.BlockSpec((TA, LANES), lambda p, i: (i, p)), pl.BlockSpec((1, TA, 2), lambda p, i: (p, i, 0))],
        out_shape=[jax.ShapeDtypeStruct((s, AW), F32), jax.ShapeDtypeStruct((HEADS // 2, s, 2), F32)],
        vmem_mb=32, hosted=hosted)
    return o, lse, moved


def _attn_bwd(qp, kp, vp, dmixed, o, lse, hosted):
    s = qp.shape[0]
    nq = s // TA

    def body(q_ref, k_ref, v_ref, do_ref, o_ref, lse_ref, dq_ref, dk_ref, dv_ref, qb_ref, dob_ref):
        dk_ref[...] = jnp.zeros_like(dk_ref)
        dv_ref[...] = jnp.zeros_like(dv_ref)
        slabs = [slice(SLAB * hh, SLAB * (hh + 1)) for hh in range(2)]
        lane = lax.broadcasted_iota(jnp.int32, (TA, DH), 1)

        def q_block(i, _):
            i0 = pl.multiple_of(i * TA, TA)
            rows = pl.ds(i0, TA)
            for hh in range(2):
                half = slice(DH * hh, DH * (hh + 1))
                do = do_ref[rows, half]
                delta = jnp.sum(do * o_ref[rows, half], axis=-1, keepdims=True)
                dob_ref[hh, :, 0:DH] = do.astype(BF16)
                dob_ref[hh, :, DH:SLAB] = _lanes3(lane, 0, [-d for d in _split3(delta)], 0.0).astype(BF16)
                lse3 = _split3(lse_ref[0, rows, hh:hh + 1])
                qb_ref[hh, :, 0:DH] = q_ref[rows, SLAB * hh:SLAB * hh + DH]
                aug = q_ref[rows, SLAB * hh + DH:SLAB * (hh + 1)].astype(F32)
                qb_ref[hh, :, DH:SLAB] = _lanes3(lane, AUG_LSE, [-x for x in lse3], aug).astype(BF16)

            def block(j, dq, masked):
                keys = pl.ds(pl.multiple_of(j * TA, TA), TA)
                dv, dk, dqc = [], [], []
                for hh in range(2):
                    q, dob = qb_ref[hh], dob_ref[hh]
                    k = k_ref[keys, slabs[hh]]
                    sc = _dot(q, k, NT)
                    if masked:
                        sc = jnp.where(_causal_mask(), sc, NEG)
                    p = jnp.exp(sc)
                    dv.append(_dot(p.astype(BF16), dob, TN))
                    ds = (p * _dot(dob, v_ref[keys, slabs[hh]], NT)).astype(BF16)
                    dk.append(_dot(ds, q, TN))
                    dqc.append(_dot(ds, k))
                dv_ref[keys, :] += jnp.concatenate(dv, axis=1)
                dk_ref[keys, :] += jnp.concatenate(dk, axis=1)
                return dq + jnp.concatenate(dqc, axis=1)

            dq = lax.fori_loop(0, i, lambda j, acc: block(j, acc, False), jnp.zeros((TA, 2 * SLAB), F32))
            dq_ref[rows, :] = block(i, dq, True)
            return 0

        lax.fori_loop(0, nq, q_block, 0)

    pair = lambda p: (0, p)
    slab2 = pl.BlockSpec((s, 2 * SLAB), pair)
    seq = pl.BlockSpec((s, LANES), pair)
    small = pl.BlockSpec((1, s, 2), lambda p: (p, 0, 0))
    o32 = jax.ShapeDtypeStruct((s, HEADS * SLAB), F32)
    return _call(
        body, (qp, kp, vp, dmixed, o, lse), name="attn_bwd", grid=(HEADS // 2,),
        in_specs=[slab2, slab2, slab2, seq, seq, small], out_specs=[slab2, slab2, slab2], out_shape=[o32, o32, o32],
        scratch_shapes=[pltpu.VMEM((2, TA, SLAB), BF16), pltpu.VMEM((2, TA, SLAB), BF16)], vmem_mb=48, hosted=hosted)


def _qkv_post(dqp, dkp, dvp, proj, gq, gk, hosted):
    s = proj.shape[0]

    def body(dq_ref, dk_ref, dv_ref, q_ref, k_ref, gq_ref, gk_ref, dqo_ref, dko_ref, dvo_ref, df_ref, vec_ref):
        @pl.when(pl.program_id(0) == 0)
        def _():
            vec_ref[...] = jnp.zeros_like(vec_ref)

        def one(d_ref, x_ref, g_ref, o_ref, row, scale):
            dg = jnp.zeros((1, DH), F32)
            for h in range(HEADS):
                sl = slice(DH * h, DH * (h + 1))
                xv = x_ref[:, sl]
                r = lax.rsqrt(jnp.mean(xv * xv, axis=-1, keepdims=True) + EPS)
                xh = xv * r
                dn = d_ref[:, SLAB * h:SLAB * h + DH] * scale
                dg = dg + jnp.sum(dn * xh, axis=0, keepdims=True)
                dxh = dn * g_ref[...]
                o_ref[:, sl] = (r * (dxh - xh * jnp.mean(dxh * xh, axis=-1, keepdims=True))).astype(BF16)
            vec_ref[row:row + 1, 0:DH] += dg

        one(dq_ref, q_ref, gq_ref, dqo_ref, 0, QK_SCALE)
        one(dk_ref, k_ref, gk_ref, dko_ref, 1, 1.0)
        lane = lax.broadcasted_iota(jnp.int32, (TR, LANES), 1)
        df = jnp.zeros((TR, LANES), F32)
        for h in range(HEADS):
            dvo_ref[:, DH * h:DH * (h + 1)] = dv_ref[:, SLAB * h:SLAB * h + DH].astype(BF16)
            row_sum = dq_ref[:, SLAB * h + DH:SLAB * h + DH + 1]
            col_sum = dk_ref[:, SLAB * h + DH + AUG_ONE:SLAB * h + DH + AUG_ONE + 1]
            df = jnp.where(lane == h, row_sum - col_sum, df)
        df_ref[...] = df

    o = jax.ShapeDtypeStruct((s, AW), BF16)
    wide = _row_spec(HEADS * SLAB)
    return _call(
        body, (dqp, dkp, dvp, proj, proj, gq, gk), name="qkv_post", grid=(s // TR,),
        in_specs=[wide, wide, wide, _row_spec(AW, 0), _row_spec(AW, 1), _full_spec((1, DH)), _full_spec((1, DH))],
        out_specs=[_row_spec(AW), _row_spec(AW), _row_spec(AW), _row_spec(LANES), _full_spec((SUB, LANES))],
        out_shape=[o, o, o, jax.ShapeDtypeStruct((s, LANES), F32), jax.ShapeDtypeStruct((SUB, LANES), F32)],
        hosted=hosted)


TF = 256
NJ = DFF // TF
FFN_ROWS_FWD = 1024
FFN_ROWS_BWD = 512


def _ffn_fwd(h2, wup, cw, wd):
    s = h2.shape[0]
    tr = FFN_ROWS_FWD
    nr = s // tr

    def body(h_ref, wg_ref, wv_ref, cg_ref, cv_ref, wd_ref, pg_ref, pv_ref, y_ref, halo_ref):
        r, j = pl.program_id(0), pl.program_id(1)
        hv = h_ref[...]
        pg = _dot(hv, wg_ref[...]).astype(BF16)
        pv = _dot(hv, wv_ref[...]).astype(BF16)
        pg_ref[...] = pg
        pv_ref[...] = pv
        pgf, pvf = pg.astype(F32), pv.astype(F32)
        ug, _, _ = _conv_taps(pgf, jnp.where(r > 0, halo_ref[j, 0], 0.0), cg_ref[...])
        uv, _, _ = _conv_taps(pvf, jnp.where(r > 0, halo_ref[j, 1], 0.0), cv_ref[...])
        halo_ref[j, 0] = pgf[tr - SUB:tr, :]
        halo_ref[j, 1] = pvf[tr - SUB:tr, :]
        act = (ug * _sigmoid(ug) * uv).astype(BF16)
        contrib = _dot(act, wd_ref[...])

        @pl.when(j == 0)
        def _():
            y_ref[...] = contrib

        @pl.when(j > 0)
        def _():
            y_ref[...] += contrib

    pre = jax.ShapeDtypeStruct((s, DFF), BF16)
    return pl.pallas_call(
        body, name="ffn_fwd", grid=(nr, NJ),
        in_specs=[pl.BlockSpec((tr, D), lambda r, j: (r, 0)),
                  pl.BlockSpec((D, TF), lambda r, j: (0, j)),
                  pl.BlockSpec((D, TF), lambda r, j: (0, NJ + j)),
                  pl.BlockSpec((3, TF), lambda r, j: (0, j)),
                  pl.BlockSpec((3, TF), lambda r, j: (0, NJ + j)),
                  pl.BlockSpec((TF, D), lambda r, j: (j, 0))],
        out_specs=(pl.BlockSpec((tr, TF), lambda r, j: (r, j)),
                   pl.BlockSpec((tr, TF), lambda r, j: (r, j)),
                   pl.BlockSpec((tr, D), lambda r, j: (r, 0))),
        out_shape=(pre, pre, jax.ShapeDtypeStruct((s, D), F32)),
        scratch_shapes=[pltpu.VMEM((NJ, 2, SUB, TF), F32)],
        compiler_params=_params(("arbitrary", "arbitrary"), 56),
    )(h2, wup, wup, cw, cw, wd)


def _ffn_bwd(dy, h2, pre_g, pre_v, wup, cw, wd):
    s = h2.shape[0]
    tr = FFN_ROWS_BWD
    nr = s // tr
    hb = tr // (2 * SUB)

    def body(dy_ref, h_ref, pg_ref, pv_ref, hg_ref, hv_ref, wg_ref, wv_ref, cg_ref, cv_ref, wd_ref,
             dh_ref, dwg_ref, dwv_ref, dwd_ref, dcg_ref, dcv_ref, nxt_ref, awg_ref, awv_ref, awd_ref):
        j, r = pl.program_id(0), pl.program_id(1)
        rr = nr - 1 - r
        row0 = pl.multiple_of(rr * tr, tr)
        cwg, cwv = cg_ref[...], cv_ref[...]
        pg, pv = pg_ref[...].astype(F32), pv_ref[...].astype(F32)
        ug, g1, g2 = _conv_taps(pg, jnp.where(rr > 0, hg_ref[SUB:2 * SUB, :].astype(F32), 0.0), cwg)
        uv, v1, v2 = _conv_taps(pv, jnp.where(rr > 0, hv_ref[SUB:2 * SUB, :].astype(F32), 0.0), cwv)
        sg = _sigmoid(ug)
        sil = ug * sg
        act = (sil * uv).astype(BF16)
        dyv = dy_ref[...]
        da = _dot(dyv, wd_ref[...], NT)
        dug = da * uv * (sg * (1.0 + ug * (1.0 - sg)))
        duv = da * sil
        dpg = _conv_taps_t(dug, jnp.where(r > 0, nxt_ref[0], 0.0), cwg)
        dpv = _conv_taps_t(duv, jnp.where(r > 0, nxt_ref[1], 0.0), cwv)
        nxt_ref[0] = dug[0:SUB, :]
        nxt_ref[1] = duv[0:SUB, :]
        dpgb, dpvb = dpg.astype(BF16), dpv.astype(BF16)
        hv = h_ref[...]
        dwd = _dot(act, dyv, TN)
        dwg = _dot(hv, dpgb, TN)
        dwv = _dot(hv, dpvb, TN)
        dh = _dot(dpgb, wg_ref[...], NT) + _dot(dpvb, wv_ref[...], NT)

        def taps(du, x0, x1, x2):
            return (jnp.sum(du * x2, axis=0, keepdims=True), jnp.sum(du * x1, axis=0, keepdims=True),
                    jnp.sum(du * x0, axis=0, keepdims=True))

        tg, tv = taps(dug, pg, g1, g2), taps(duv, pv, v1, v2)

        @pl.when(r == 0)
        def _():
            awd_ref[...] = dwd
            awg_ref[...] = dwg
            awv_ref[...] = dwv
            dcg_ref[...] = jnp.zeros_like(dcg_ref)
            dcv_ref[...] = jnp.zeros_like(dcv_ref)

        @pl.when(r > 0)
        def _():
            awd_ref[...] += dwd
            awg_ref[...] += dwg
            awv_ref[...] += dwv

        @pl.when(r == nr - 1)
        def _():
            dwd_ref[...] = awd_ref[...].astype(BF16)
            dwg_ref[...] = awg_ref[...].astype(BF16)
            dwv_ref[...] = awv_ref[...].astype(BF16)

        for t in range(3):
            dcg_ref[t:t + 1, :] += tg[t]
            dcv_ref[t:t + 1, :] += tv[t]

        @pl.when(j == 0)
        def _():
            dh_ref[pl.ds(row0, tr), :] = dh

        @pl.when(j > 0)
        def _():
            dh_ref[pl.ds(row0, tr), :] += dh

    rows = lambda j, r: (nr - 1 - r, 0)
    tile = lambda j, r: (nr - 1 - r, j)
    halo = lambda j, r: (jnp.maximum((nr - 1 - r) * hb - 1, 0), j)
    return pl.pallas_call(
        body, name="ffn_bwd", grid=(NJ, nr),
        in_specs=[pl.BlockSpec((tr, D), rows), pl.BlockSpec((tr, D), rows),
                  pl.BlockSpec((tr, TF), tile), pl.BlockSpec((tr, TF), tile),
                  pl.BlockSpec((2 * SUB, TF), halo), pl.BlockSpec((2 * SUB, TF), halo),
                  pl.BlockSpec((D, TF), lambda j, r: (0, j)), pl.BlockSpec((D, TF), lambda j, r: (0, NJ + j)),
                  pl.BlockSpec((3, TF), lambda j, r: (0, j)), pl.BlockSpec((3, TF), lambda j, r: (0, NJ + j)),
                  pl.BlockSpec((TF, D), lambda j, r: (j, 0))],
        out_specs=(pl.BlockSpec((s, D), lambda j, r: (0, 0)),
                   pl.BlockSpec((D, TF), lambda j, r: (0, j)), pl.BlockSpec((D, TF), lambda j, r: (0, j)),
                   pl.BlockSpec((TF, D), lambda j, r: (j, 0)),
                   pl.BlockSpec((SUB, TF), lambda j, r: (0, j)), pl.BlockSpec((SUB, TF), lambda j, r: (0, j))),
        out_shape=(jax.ShapeDtypeStruct((s, D), F32),
                   jax.ShapeDtypeStruct((D, DFF), BF16), jax.ShapeDtypeStruct((D, DFF), BF16),
                   jax.ShapeDtypeStruct((DFF, D), BF16),
                   jax.ShapeDtypeStruct((SUB, DFF), F32), jax.ShapeDtypeStruct((SUB, DFF), F32)),
        scratch_shapes=[pltpu.VMEM((2, SUB, TF), F32), pltpu.VMEM((D, TF), F32), pltpu.VMEM((D, TF), F32),
                        pltpu.VMEM((TF, D), F32)],
        compiler_params=_params(("arbitrary", "arbitrary"), 56),
    )(dy, h2, pre_g, pre_v, pre_g, pre_v, wup, wup, cw, cw, wd)


def _adam(w, g, m, v):
    m = ADAM_B1 * m + (1.0 - ADAM_B1) * g
    v = ADAM_B2 * v + (1.0 - ADAM_B2) * (g * g)
    m_hat = m / (1.0 - ADAM_B1 ** ADAM_STEP)
    v_hat = v / (1.0 - ADAM_B2 ** ADAM_STEP)
    delta = -ADAM_LR * (m_hat / (jnp.sqrt(v_hat) + ADAM_EPS) + ADAM_WD * w)
    return delta, m, v


NCHIP = NDEV // 2


def _pair_add(mine, theirs, tr, name):
    _, _, rws, cols = mine.shape

    def body(a_ref, b_ref, o_ref):
        c = lax.axis_index("c")
        o_ref[0] = (a_ref[c, 0].astype(F32) + b_ref[0].astype(F32)).astype(BF16)

    (out,), _ = _call(
        body, (mine, theirs), name=name, grid=(NCHIP, rws // tr),
        in_specs=[pl.BlockSpec((2, 1, tr, cols), lambda q, i: (0, q, i, 0)),
                  pl.BlockSpec((1, tr, cols), lambda q, i: (q, i, 0))],
        out_specs=[pl.BlockSpec((1, tr, cols), lambda q, i: (q, i, 0))],
        out_shape=[jax.ShapeDtypeStruct((NCHIP, rws, cols), BF16)], vmem_mb=32)
    return out


def _adamw_sharded(parts, w, m, v, tr, name, hosted=None):
    rws, cols = w.shape

    def body(p_ref, w_ref, m_ref, v_ref, g_ref, d_ref, mo_ref, vo_ref):
        g = p_ref[0].astype(F32)
        for q in range(1, NCHIP):
            g = g + p_ref[q].astype(F32)
        g_ref[...] = g
        d_ref[...], mo_ref[...], vo_ref[...] = _adam(w_ref[...], g, m_ref[...], v_ref[...])

    blk = pl.BlockSpec((tr, cols), lambda i: (i, 0))
    o = jax.ShapeDtypeStruct((rws, cols), F32)
    outs, moved = _call(
        body, (parts, w, m, v), name=name, grid=(rws // tr,),
        in_specs=[pl.BlockSpec((NCHIP, tr, cols), lambda i: (0, i, 0)), blk, blk, blk],
        out_specs=[blk, blk, blk, blk], out_shape=[o, o, o, o], vmem_mb=48, hosted=hosted)
    return (outs, moved) if hosted else outs


def _adamw_ada(c_all, dmod_my, w, m, v):
    rws, cols = w.shape
    tr = 256

    def body(c_ref, dm_ref, w_ref, m_ref, v_ref, g_ref, d_ref, mo_ref, vo_ref):
        cv = c_ref[...]
        act = cv * _sigmoid(cv)
        g = _dot(act, dm_ref[...], TN, lax.Precision.HIGHEST)
        g_ref[...] = g
        d_ref[...], mo_ref[...], vo_ref[...] = _adam(w_ref[...], g, m_ref[...], v_ref[...])

    blk = pl.BlockSpec((tr, cols), lambda i: (i, 0))
    o = jax.ShapeDtypeStruct((rws, cols), F32)
    return pl.pallas_call(
        body, name="adamw_ada", grid=(rws // tr,),
        in_specs=[pl.BlockSpec((NDEV, tr), lambda i: (0, i)), _full_spec((NDEV, cols)), blk, blk, blk],
        out_specs=(blk, blk, blk, blk), out_shape=(o, o, o, o),
        compiler_params=_params(("parallel",), 48),
    )(c_all, dmod_my, w, m, v)


REP_ROWS = 16
ROW_N1, ROW_N2, ROW_LOSS, ROW_MISC = 6, 7, 8, 9
LANE_BF, LANE_GQ, LANE_GK = 0, 128, 256


def _adamw_small(rep_all, conv_all, wmv):
    n_ff = wmv[6][0].shape[1]

    def body(*refs):
        rep_ref, conv_ref = refs[:2]
        ins = refs[2:2 + 24]
        outs = refs[2 + 24:]
        loss_ref, outs = outs[0], outs[1:]
        g_rep = rep_ref[0]
        g_conv = conv_ref[0]
        for d in range(1, NDEV):
            g_rep = g_rep + rep_ref[d]
            g_conv = g_conv + conv_ref[d]
        loss_ref[...] = (0.5 / D) * jnp.sum(g_rep[ROW_LOSS:ROW_LOSS + 1, :], axis=-1, keepdims=True)
        grads = [
            None,
            g_rep[ROW_N1:ROW_N1 + 1, :],
            g_rep[ROW_MISC:ROW_MISC + 1, LANE_BF:LANE_BF + HEADS],
            g_rep[ROW_MISC:ROW_MISC + 1, LANE_GQ:LANE_GQ + DH],
            g_rep[ROW_MISC:ROW_MISC + 1, LANE_GK:LANE_GK + DH],
            g_rep[ROW_N2:ROW_N2 + 1, :],
            g_conv[0:3, 0:n_ff],
            g_conv[0:3, n_ff:n_ff + DH],
        ]
        for p in range(8):
            w_ref, m_ref, v_ref = ins[3 * p:3 * p + 3]
            g_ref, d_ref, mo_ref, vo_ref = outs[4 * p:4 * p + 4]
            if p == 0:
                for nmod in range(NMOD):
                    sl = slice(D * nmod, D * (nmod + 1))
                    g = g_rep[nmod:nmod + 1, :]
                    g_ref[:, sl] = g
                    d_ref[:, sl], mo_ref[:, sl], vo_ref[:, sl] = _adam(w_ref[:, sl], g, m_ref[:, sl], v_ref[:, sl])
            else:
                g = grads[p]
                g_ref[...] = g
                d_ref[...], mo_ref[...], vo_ref[...] = _adam(w_ref[...], g, m_ref[...], v_ref[...])

    flat = [a for trio in wmv for a in trio]
    out_shape = [jax.ShapeDtypeStruct((1, 1), F32)]
    for trio in wmv:
        out_shape += [jax.ShapeDtypeStruct(trio[0].shape, F32)] * 4
    return pl.pallas_call(
        body, name="adamw_small", out_shape=tuple(out_shape),
        compiler_params=_params(None, 32),
    )(rep_all, conv_all, *flat)


FG_FIRST = 3 * AW
N_IN = DIN // NDEV


def _w_in_runs():
    runs = []
    for d in range(NDEV):
        lo, hi = N_IN * d, N_IN * (d + 1)
        for a, b, shift in ((0, FG_FIRST, 0), (FG_FIRST, FG_FIRST + HEADS, DIN - HEADS - FG_FIRST),
                            (FG_FIRST + HEADS, DIN, -HEADS)):
            a, b = max(a, lo), min(b, hi)
            if a < b:
                runs.append((d, a - lo, a + shift, b - a))
    return runs


W_IN_ROWS = 256


def _assemble_w_in(g_in):
    def body(g_ref, o_ref):
        for d, src, dst, width in _w_in_runs():
            o_ref[:, dst:dst + width] = g_ref[d, :, src:src + width]
        o_ref[:, DIN:DINP] = jnp.zeros((W_IN_ROWS, DINP - DIN), o_ref.dtype)

    (out,), _ = _call(
        body, (g_in,), name="assemble_w_in", grid=(D // W_IN_ROWS,),
        in_specs=[pl.BlockSpec((NDEV, W_IN_ROWS, N_IN), lambda i: (0, i, 0))],
        out_specs=[pl.BlockSpec((W_IN_ROWS, DINP), lambda i: (i, 0))],
        out_shape=[jax.ShapeDtypeStruct((D, DINP), g_in.dtype)], vmem_mb=32)
    return out


def _scatter_dw_in(dwp):
    def body(w_ref, o_ref):
        for d, src, dst, width in _w_in_runs():
            o_ref[d % 2, d // 2, :, src:src + width] = w_ref[:, dst:dst + width]

    (out,), _ = _call(
        body, (dwp,), name="scatter_dw_in", grid=(D // W_IN_ROWS,),
        in_specs=[pl.BlockSpec((W_IN_ROWS, DINP), lambda i: (i, 0))],
        out_specs=[pl.BlockSpec((2, NCHIP, W_IN_ROWS, N_IN), lambda i: (0, 0, i, 0))],
        out_shape=[jax.ShapeDtypeStruct((2, NCHIP, D, N_IN), dwp.dtype)], vmem_mb=32)
    return out


def _by_core_chip(a):
    return jnp.transpose(a.reshape((NCHIP, 2) + a.shape[1:]), (1, 0, 2, 3))


def kernel(x, c, w_ada, b_ada, norm1_g, w_in, b_forget, q_norm_g, k_norm_g, conv_mix_w, w_out, norm2_g, w_up, ffn_conv_w, w_down, loss_target, m_w_ada, m_b_ada, m_norm1_g, m_w_in, m_b_forget, m_q_norm_g, m_k_norm_g, m_conv_mix_w, m_w_out, m_norm2_g, m_w_up, m_ffn_conv_w, m_w_down, v_w_ada, v_b_ada, v_norm1_g, v_w_in, v_b_forget, v_q_norm_g, v_k_norm_g, v_conv_mix_w, v_w_out, v_norm2_g, v_w_up, v_ffn_conv_w, v_w_down):
    me = 4 * lax.axis_index("x") + 2 * lax.axis_index("y") + lax.axis_index("c")
    xs, tgt = x[0], loss_target[0]
    s = xs.shape[0]
    nq = s // TA
    n_ada = w_ada.shape[2]
    n_ff = w_up.shape[2]

    conv_w = jnp.concatenate([ffn_conv_w[0], conv_mix_w[0]], axis=1)
    conv_w = jnp.concatenate([conv_w, jnp.zeros((SUB - 3, conv_w.shape[1]), F32)], axis=0)
    c_all, conv_all, g_in = _exchange(
        [(c.reshape(SUB, D // SUB), "ag"), (conv_w, "ag"), (w_in[0].astype(BF16), "ag2")], "exchange_w_in")
    c_all = c_all.reshape(NDEV, D)
    cw_ffn = jnp.transpose(conv_all[:, :3, :n_ff], (1, 0, 2)).reshape(3, 2 * DFF)
    cw_mix = jnp.transpose(conv_all[:, :3, n_ff:], (1, 0, 2)).reshape(3, CW)
    w_in_p = _assemble_w_in(g_in)

    b_my = lax.dynamic_slice(b_ada, (0, me * n_ada), (1, n_ada))
    mod_part = _ada_fwd(c_all, w_ada[0], b_my)
    (mod_rows,) = _exchange([(jnp.broadcast_to(mod_part[:, None, :], (NDEV, SUB, n_ada)), "a2a")], "exchange_mod")
    mod = mod_rows[:, 0, :].reshape(NMOD, D)
    mod = jnp.concatenate([mod, jnp.zeros((SUB - NMOD, D), F32)], axis=0)

    h = _norm_mod_fwd(xs, mod, norm1_g)
    proj, (g_down,) = _mm(h, w_in_p, "nn", F32, 512, 640, "proj_fwd", hosted=[(w_down[0].astype(BF16), "ag2")])
    bf_pad = jnp.concatenate([b_forget, jnp.zeros((1, LANES - HEADS), F32)], axis=1)
    fcum = _fgate_fwd(proj, bf_pad)
    (qp, kp, vp), (g_out,) = _qkv_prep(proj, fcum, q_norm_g, k_norm_g, [(w_out[0].astype(BF16), "ag2")])
    attn, lse, (g_up,) = _attn_fwd(qp, kp, vp, [(w_up[0].astype(BF16), "ag2")])
    w_out_f = g_out.reshape(D, D)
    w_up_f = jnp.transpose(g_up, (1, 0, 2)).reshape(D, 2 * DFF)
    w_down_f = g_down.reshape(DFF, D)
    conv = _mixconv_fwd(proj, cw_mix)
    mixed = jnp.concatenate([attn, conv], axis=1).astype(BF16)
    z = _mm(mixed, w_out_f, "nn", F32, 512, 512, "out_fwd")
    x1, h2 = _resid_norm2(xs, z, mod, norm2_g)
    pre_g, pre_v, y = _ffn_fwd(h2, w_up_f, cw_ffn, w_down_f)
    dout, dy, vec_l = _loss_head(x1, y, tgt, mod)

    dh2, dwup_g, dwup_v, dwd, dcw_g, dcw_v = _ffn_bwd(dy, h2, pre_g, pre_v, w_up_f, cw_ffn, w_down_f)
    dx1, dz, vec_2 = _norm_mod_bwd(dh2, x1, dout, z, mod, norm2_g, 3, 4, 2, "norm2_bwd")
    dwout = _mm(mixed, dz, "tn", BF16, 512, 512, "out_bwd_w")
    s_out = _by_core_chip(dwout.reshape(NDEV, D // NDEV, D))
    s_down = _by_core_chip(dwd.reshape(NDEV, DFF // NDEV, D))
    s_up = jnp.transpose(jnp.concatenate([dwup_g, dwup_v], axis=1).reshape(D, NCHIP, 2, n_ff), (2, 1, 0, 3))
    dmixed, (t_out, t_up, t_down) = _mm(dz, w_out_f, "nt", F32, 512, 512, "out_bwd_x",
                                        hosted=[(s_out, "pair"), (s_up, "pair"), (s_down, "pair")])
    c_out = _pair_add(s_out, t_out, 128, "pair_add_out")
    c_up = _pair_add(s_up, t_up, 256, "pair_add_up")
    c_down = _pair_add(s_down, t_down, 176, "pair_add_down")
    dxin, dbg, dcg, dcw_mix = _mixconv_bwd(dmixed, proj, cw_mix)
    (dqp, dkp, dvp), (p_up,) = _attn_bwd(qp, kp, vp, dmixed, attn, lse, [(c_up, "chips")])
    (dq, dk, dvb, dfcol, vec_qk), (p_out, p_down) = _qkv_post(
        dqp, dkp, dvp, proj, q_norm_g, k_norm_g, [(c_out, "chips"), (c_down, "chips")])
    dfg, vec_bf = _fgate_bwd(dfcol, proj, bf_pad)
    dproj = jnp.concatenate([dq, dk, dvb, dxin, dbg, dcg, dfg], axis=1)
    dwin_p = _mm(h, dproj, "tn", BF16, 512, 640, "proj_bwd_w")
    s_in = _scatter_dw_in(dwin_p)
    (t_in,) = _exchange([(s_in, "pair")], "exchange_pair_in")
    c_in = _pair_add(s_in, t_in, 256, "pair_add_in")
    dh, (p_in,) = _mm(dproj, w_in_p, "nt", F32, 512, 512, "proj_bwd_x", hosted=[(c_in, "chips")])
    grad_x, vec_1 = _norm_mod_bwd(dh, xs, dx1, None, mod, norm1_g, 0, 1, None, "norm1_bwd")

    misc = jnp.zeros((1, D), F32)
    misc = lax.dynamic_update_slice(misc, vec_bf[0:1, :HEADS], (0, LANE_BF))
    misc = lax.dynamic_update_slice(misc, vec_qk[0:1, :DH], (0, LANE_GQ))
    misc = lax.dynamic_update_slice(misc, vec_qk[1:2, :DH], (0, LANE_GK))
    rep = jnp.concatenate([
        vec_1[0:1], vec_1[1:2], vec_2[3:4], vec_2[0:1], vec_2[1:2], vec_l[0:1],
        vec_1[2:3], vec_2[2:3], vec_l[1:2], misc, jnp.zeros((REP_ROWS - 10, D), F32)], axis=0)
    dcw_ffn = jnp.concatenate([dcw_g, dcw_v], axis=1).reshape(SUB, NDEV, n_ff)
    dcw_all = jnp.concatenate([jnp.transpose(dcw_ffn, (1, 0, 2)),
                               jnp.transpose(dcw_mix.reshape(SUB, NDEV, DH), (1, 0, 2))], axis=2)
    r_out, (rep_all, conv_parts) = _adamw_sharded(p_out, w_out[0], m_w_out[0], v_w_out[0], 128, "adamw_out",
                                                  hosted=[(rep, "ag"), (dcw_all, "a2a")])
    dmod_my = lax.dynamic_slice(rep_all[:, :NMOD, :].reshape(NDEV, NMOD * D), (0, me * n_ada), (NDEV, n_ada))
    r_ada = _adamw_ada(c_all, dmod_my, w_ada[0], m_w_ada[0], v_w_ada[0])
    r_in = _adamw_sharded(p_in, w_in[0], m_w_in[0], v_w_in[0], 256, "adamw_in")
    r_up = _adamw_sharded(p_up, w_up[0], m_w_up[0], v_w_up[0], 256, "adamw_up")
    r_down = _adamw_sharded(p_down, w_down[0], m_w_down[0], v_w_down[0], 176, "adamw_down")
    small = _adamw_small(rep_all, conv_parts, [
        [b_ada, m_b_ada, v_b_ada], [norm1_g, m_norm1_g, v_norm1_g], [b_forget, m_b_forget, v_b_forget],
        [q_norm_g, m_q_norm_g, v_q_norm_g], [k_norm_g, m_k_norm_g, v_k_norm_g], [norm2_g, m_norm2_g, v_norm2_g],
        [ffn_conv_w[0], m_ffn_conv_w[0], v_ffn_conv_w[0]], [conv_mix_w[0], m_conv_mix_w[0], v_conv_mix_w[0]]])
    loss = small[0].reshape(())
    r_bada, r_n1, r_bf, r_gq, r_gk, r_n2, r_cf, r_cm = [small[1 + 4 * p:5 + 4 * p] for p in range(8)]
    lead = lambda t: tuple(a[None] for a in t)
    per_w = [lead(r_ada), r_bada, r_n1, lead(r_in), r_bf, r_gq, r_gk, lead(r_cm), lead(r_out), r_n2,
             lead(r_up), lead(r_cf), lead(r_down)]
    outs = [loss, grad_x[None]]
    for field in range(4):
        outs += [t[field] for t in per_w]
    return tuple(outs)
```

```python
import functools

import jax
import jax.numpy as jnp
import numpy as np
from jax import lax
from jax.experimental import pallas as pl
from jax.experimental.pallas import tpu as pltpu

F32 = jnp.float32
BF16 = jnp.bfloat16

NDEV = 8
D = 1024
HEADS = 8
DH = 64
AW = 512
CW = 512
DFF = 2816
DIN = 3080
DINP = 3200
NMOD = 6
EPS = 1e-6
QK_SCALE = 0.125
LANES = 128
SUB = 8

ADAM_LR = 0.001
ADAM_B1 = 0.9
ADAM_B2 = 0.999
ADAM_EPS = 1e-08
ADAM_WD = 0.01
ADAM_STEP = 10

MESH = pl.DeviceIdType.MESH
ANY = pl.BlockSpec(memory_space=pl.ANY)

NN = (((1,), (0,)), ((), ()))
NT = (((1,), (1,)), ((), ()))
TN = (((0,), (0,)), ((), ()))


def _dot(a, b, dims=NN, precision=None):
    return lax.dot_general(a, b, dims, precision=precision, preferred_element_type=F32)


def _params(sem=None, vmem_mb=None):
    kw = {}
    if sem is not None:
        kw["dimension_semantics"] = sem
    if vmem_mb is not None:
        kw["vmem_limit_bytes"] = vmem_mb * 1024 * 1024
    return pltpu.CompilerParams(**kw)


def _sigmoid(x):
    return 1.0 / (1.0 + jnp.exp(-x))


class _Exchange:
    def __init__(self, items):
        self.arrays = [a for a, _ in items]
        self.modes = [m for _, m in items]
        self.n = len(items)
        self.out_shape = []
        for a, m in items:
            sh = {"ag": (NDEV,) + a.shape, "ag2": (NDEV,) + a.shape, "pair": a.shape[:1] + a.shape[2:]}.get(m, a.shape)
            self.out_shape.append(jax.ShapeDtypeStruct(sh, a.dtype))
        self.scratch = [pltpu.SemaphoreType.DMA((self.n, NDEV - 1)), pltpu.SemaphoreType.DMA((self.n, NDEV - 1)),
                        pltpu.SemaphoreType.DMA((self.n,))]

    def _plan(self, srcs, outs, sems):
        send_sems, recv_sems, loc_sems = sems
        x, y, c = lax.axis_index("x"), lax.axis_index("y"), lax.axis_index("c")
        me, my_chip = 4 * x + 2 * y + c, 2 * x + y
        sib = (x, y, 1 - c)
        local, first, landed, forwards, arrivals = [], [], [], [], []

        def remote(a, k, src, dst, to):
            return pltpu.make_async_remote_copy(src_ref=src, dst_ref=dst, send_sem=send_sems.at[a, k],
                                                recv_sem=recv_sems.at[a, k], device_id=to, device_id_type=MESH)

        for a, mode in enumerate(self.modes):
            src, out = srcs[a], outs[a]
            if mode in ("ag", "a2a"):
                piece = (lambda slot, src=src: src) if mode == "ag" else (lambda slot, src=src: src.at[slot])
                local.append(pltpu.make_async_copy(piece(me), out.at[me], loc_sems.at[a]))
                for r in range(1, NDEV):
                    px = 1 - x if (r >> 2) & 1 else x
                    py = 1 - y if (r >> 1) & 1 else y
                    pc = 1 - c if r & 1 else c
                    pidx = 4 * px + 2 * py + pc
                    first.append(remote(a, r - 1, piece(pidx), out.at[me], (px, py, pc)))
                    arrivals.append(remote(a, r - 1, piece(pidx), out.at[pidx], (px, py, pc)))
            elif mode == "ag2":
                local.append(pltpu.make_async_copy(src, out.at[me], loc_sems.at[a]))
                first.append(remote(a, 0, src, out.at[me], sib))
                arrivals.append(remote(a, 0, src, out.at[me + 1 - 2 * c], sib))
                for j, (px, py) in enumerate([(1 - x, y), (x, 1 - y), (1 - x, 1 - y)]):
                    theirs = out.at[4 * px + 2 * py + c]
                    first.append(remote(a, 1 + j, src, out.at[me], (px, py, c)))
                    landed.append(remote(a, 1 + j, src, theirs, (px, py, c)))
                    forwards.append(remote(a, 4 + j, theirs, theirs, sib))
                    arrivals.append(remote(a, 4 + j, src, out.at[4 * px + 2 * py + 1 - c], sib))
            elif mode == "pair":
                for q in range(NDEV // 2):
                    first.append(remote(a, q, src.at[q, 1 - c], out.at[q], sib))
                    arrivals.append(remote(a, q, src.at[q, 1 - c], out.at[q], sib))
            else:
                assert mode == "chips", mode
                local.append(pltpu.make_async_copy(src.at[my_chip], out.at[my_chip], loc_sems.at[a]))
                for j, (px, py) in enumerate([(1 - x, y), (x, 1 - y), (1 - x, 1 - y)]):
                    q = 2 * px + py
                    first.append(remote(a, 1 + j, src.at[q], out.at[my_chip], (px, py, c)))
                    arrivals.append(remote(a, 1 + j, src.at[q], out.at[q], (px, py, c)))
        return local, first, landed, forwards, arrivals

    def start(self, srcs, outs, sems):
        local, first, _, _, _ = self._plan(srcs, outs, sems)
        for cp in local + first:
            cp.start()

    def wait(self, srcs, outs, sems):
        local, first, landed, forwards, arrivals = self._plan(srcs, outs, sems)
        for cp, fwd in zip(landed, forwards):
            cp.wait_recv()
            fwd.start()
        for cp in arrivals:
            cp.wait_recv()
        for cp in first + forwards:
            cp.wait_send()
        for cp in local:
            cp.wait()


def _exchange(items, name):
    ex = _Exchange(items)
    n = ex.n

    def body(*refs):
        srcs, outs, sems = refs[:n], refs[n:2 * n], refs[2 * n:]
        ex.start(srcs, outs, sems)
        ex.wait(srcs, outs, sems)

    return pl.pallas_call(
        body, name=name,
        out_shape=tuple(ex.out_shape),
        in_specs=[ANY] * n, out_specs=tuple([ANY] * n),
        scratch_shapes=ex.scratch,
        compiler_params=pltpu.CompilerParams(has_side_effects=True),
    )(*ex.arrays)


def _call(body, inputs, *, name, grid, in_specs, out_specs, out_shape, scratch_shapes=(), vmem_mb=None, hosted=None):
    out_specs, out_shape, scratch_shapes = tuple(out_specs), tuple(out_shape), list(scratch_shapes)
    if not hosted:
        res = pl.pallas_call(
            body, name=name, grid=grid, in_specs=list(in_specs), out_specs=out_specs, out_shape=out_shape,
            scratch_shapes=scratch_shapes, compiler_params=_params(("arbitrary",) * len(grid), vmem_mb),
        )(*inputs)
        return tuple(res), ()
    ex = _Exchange(hosted)
    n, n_in, n_out, n_scr = ex.n, len(inputs), len(out_shape), len(scratch_shapes)

    def hosting_body(*refs):
        ins, srcs = refs[:n_in], refs[n_in:n_in + n]
        outs, landing = refs[n_in + n:n_in + n + n_out], refs[n_in + n + n_out:n_in + 2 * n + n_out]
        scratch, sems = refs[n_in + 2 * n + n_out:n_in + 2 * n + n_out + n_scr], refs[n_in + 2 * n + n_out + n_scr:]
        first = functools.reduce(jnp.logical_and, [pl.program_id(d) == 0 for d in range(len(grid))])
        last = functools.reduce(jnp.logical_and, [pl.program_id(d) == grid[d] - 1 for d in range(len(grid))])

        @pl.when(first)
        def _():
            ex.start(srcs, landing, sems)

        body(*ins, *outs, *scratch)

        @pl.when(last)
        def _():
            ex.wait(srcs, landing, sems)

    res = pl.pallas_call(
        hosting_body, name=name, grid=grid,
        in_specs=list(in_specs) + [ANY] * n, out_specs=out_specs + tuple([ANY] * n),
        out_shape=out_shape + tuple(ex.out_shape), scratch_shapes=scratch_shapes + ex.scratch,
        compiler_params=_params(("arbitrary",) * len(grid), vmem_mb),
    )(*inputs, *ex.arrays)
    return tuple(res[:n_out]), tuple(res[n_out:])


def _mm(a, b, mode, out_dtype, tm, tn, name, hosted=None):
    if mode == "nn":
        (m, k), n = a.shape, b.shape[1]
        a_spec = pl.BlockSpec((tm, k), lambda i, j: (i, 0))
        b_spec = pl.BlockSpec((k, tn), lambda i, j: (0, j))
        dims = NN
    elif mode == "nt":
        (m, k), n = a.shape, b.shape[0]
        a_spec = pl.BlockSpec((tm, k), lambda i, j: (i, 0))
        b_spec = pl.BlockSpec((tn, k), lambda i, j: (j, 0))
        dims = NT
    else:
        (k, m), n = a.shape, b.shape[1]
        a_spec = pl.BlockSpec((k, tm), lambda i, j: (0, i))
        b_spec = pl.BlockSpec((k, tn), lambda i, j: (0, j))
        dims = TN
    assert m % tm == 0 and n % tn == 0, (m, n, tm, tn)

    def body(a_ref, b_ref, o_ref):
        o_ref[...] = _dot(a_ref[...], b_ref[...], dims).astype(o_ref.dtype)

    (out,), moved = _call(
        body, (a, b), name=name, grid=(m // tm, n // tn),
        in_specs=[a_spec, b_spec], out_specs=[pl.BlockSpec((tm, tn), lambda i, j: (i, j))],
        out_shape=[jax.ShapeDtypeStruct((m, n), out_dtype)], vmem_mb=48, hosted=hosted)
    return (out, moved) if hosted else out


def _shift_down(x, k, fill):
    y = pltpu.roll(x, k, 0)
    row = lax.broadcasted_iota(jnp.int32, (SUB, x.shape[1]), 0)
    head = y[0:SUB, :]
    for t in range(k):
        head = jnp.where(row == t, fill[t], head)
    return jnp.concatenate([head, y[SUB:, :]], axis=0)


def _shift_up(x, k, fill):
    n = x.shape[0]
    y = pltpu.roll(x, n - k, 0)
    row = lax.broadcasted_iota(jnp.int32, (SUB, x.shape[1]), 0)
    tail = y[n - SUB:, :]
    for t in range(k):
        tail = jnp.where(row == SUB - k + t, fill[t], tail)
    return jnp.concatenate([y[:n - SUB, :], tail], axis=0)


def _conv_taps(x, halo, w):
    if halo is None:
        f1, f2 = [0.0], [0.0, 0.0]
    else:
        f1, f2 = [halo[7:8, :]], [halo[6:7, :], halo[7:8, :]]
    s1 = _shift_down(x, 1, f1)
    s2 = _shift_down(x, 2, f2)
    u = w[2:3, :] * x + w[1:2, :] * s1 + w[0:1, :] * s2
    return u, s1, s2


def _conv_taps_t(du, nxt, w):
    if nxt is None:
        f1, f2 = [0.0], [0.0, 0.0]
    else:
        f1, f2 = [nxt[0:1, :]], [nxt[0:1, :], nxt[1:2, :]]
    return w[2:3, :] * du + w[1:2, :] * _shift_up(du, 1, f1) + w[0:1, :] * _shift_up(du, 2, f2)


def _ada_fwd(c_all, w_ada, b_my):
    def body(c_ref, w_ref, b_ref, o_ref):
        cv = c_ref[...]
        act = cv * _sigmoid(cv)
        o_ref[...] = _dot(act, w_ref[...], NN, lax.Precision.HIGHEST) + b_ref[...]

    return pl.pallas_call(
        body, name="ada_fwd",
        out_shape=jax.ShapeDtypeStruct((NDEV, w_ada.shape[1]), F32),
        compiler_params=_params(None, 32),
    )(c_all, w_ada, b_my)


TR = 256


def _row_spec(width, col=0):
    return pl.BlockSpec((TR, width), lambda i, col=col: (i, col))


def _full_spec(shape):
    return pl.BlockSpec(shape, lambda i: (0,) * len(shape))


def _norm_mod_fwd(x, mod, g):
    s = x.shape[0]

    def body(x_ref, mod_ref, g_ref, h_ref):
        xv = x_ref[...]
        r = lax.rsqrt(jnp.mean(xv * xv, axis=-1, keepdims=True) + EPS)
        nrm = xv * r * g_ref[...]
        h_ref[...] = (nrm * (1.0 + mod_ref[1:2, :]) + mod_ref[0:1, :]).astype(BF16)

    return pl.pallas_call(
        body, name="norm1_fwd", grid=(s // TR,),
        in_specs=[_row_spec(D), _full_spec((SUB, D)), _full_spec((1, D))],
        out_specs=_row_spec(D), out_shape=jax.ShapeDtypeStruct((s, D), BF16),
        compiler_params=_params(("parallel",)),
    )(x, mod, g)


SLAB = 2 * DH
AUG_F, AUG_ONE, AUG_LSE = 0, 3, 6


def _split3(x):
    hi = x.astype(BF16).astype(F32)
    r1 = x - hi
    mid = r1.astype(BF16).astype(F32)
    return hi, mid, r1 - mid


def _lanes3(lane, first, pieces, other):
    out = other
    for k in range(3):
        out = jnp.where(lane == first + k, pieces[k], out)
    return out


def _aug_placement():
    eq = np.zeros((3 * LANES, HEADS * SLAB), np.float32)
    ek = np.zeros((3 * LANES, HEADS * SLAB), np.float32)
    ones = np.zeros((SUB, HEADS * SLAB), np.float32)
    for h in range(HEADS):
        aug = SLAB * h + DH
        for k in range(3):
            eq[LANES * k + h, aug + AUG_F + k] = 1.0
            ek[LANES * k + h, aug + AUG_ONE + k] = -1.0
            ones[0, aug + AUG_ONE + k] = 1.0
            ones[1, aug + AUG_F + k] = ones[1, aug + AUG_LSE + k] = 1.0
            ones[2, aug + k] = 1.0
    return jnp.asarray(eq, BF16), jnp.asarray(ek, BF16), jnp.asarray(ones)


def _qkv_prep(proj, fcum, gq, gk, hosted):
    s = proj.shape[0]

    def body(q_ref, k_ref, v_ref, f_ref, gq_ref, gk_ref, eq_ref, ek_ref, ones_ref, qo_ref, ko_ref, vo_ref):
        f3 = jnp.concatenate(_split3(f_ref[...]), axis=1).astype(BF16)
        qo_ref[...] = (_dot(f3, eq_ref[...]) + ones_ref[0:1, :]).astype(BF16)
        ko_ref[...] = (_dot(f3, ek_ref[...]) + ones_ref[1:2, :]).astype(BF16)
        vo_ref[...] = jnp.broadcast_to(ones_ref[2:3, :], vo_ref.shape).astype(BF16)
        for h in range(HEADS):
            sl = slice(DH * h, DH * (h + 1))
            lo = slice(SLAB * h, SLAB * h + DH)
            qh = q_ref[:, sl]
            r = lax.rsqrt(jnp.mean(qh * qh, axis=-1, keepdims=True) + EPS)
            qo_ref[:, lo] = (qh * r * gq_ref[...] * QK_SCALE).astype(BF16)
            kh = k_ref[:, sl]
            r = lax.rsqrt(jnp.mean(kh * kh, axis=-1, keepdims=True) + EPS)
            ko_ref[:, lo] = (kh * r * gk_ref[...]).astype(BF16)
            vo_ref[:, lo] = v_ref[:, sl].astype(BF16)

    eq, ek, ones = _aug_placement()
    o = jax.ShapeDtypeStruct((s, HEADS * SLAB), BF16)
    wide = _row_spec(HEADS * SLAB)
    return _call(
        body, (proj, proj, proj, fcum, gq, gk, eq, ek, ones), name="qkv_prep", grid=(s // TR,),
        in_specs=[_row_spec(AW, 0), _row_spec(AW, 1), _row_spec(AW, 2), _row_spec(LANES),
                  _full_spec((1, DH)), _full_spec((1, DH)), _full_spec(eq.shape), _full_spec(ek.shape),
                  _full_spec(ones.shape)],
        out_specs=[wide, wide, wide], out_shape=[o, o, o], vmem_mb=32, hosted=hosted)


FG_BLOCK = (3 * AW + 3 * CW) // LANES


def _fgate_fwd(proj, bf_pad):
    s = proj.shape[0]

    def body(fg_ref, b_ref, o_ref, carry_ref):
        i = pl.program_id(0)

        @pl.when(i == 0)
        def _():
            carry_ref[...] = jnp.zeros_like(carry_ref)

        z = fg_ref[...] + b_ref[...]
        logf = jnp.minimum(z, 0.0) - jnp.log1p(jnp.exp(-jnp.abs(z)))
        row = lax.broadcasted_iota(jnp.int32, (TR, TR), 0)
        col = lax.broadcasted_iota(jnp.int32, (TR, TR), 1)
        tri = (col <= row).astype(F32)
        cs = _dot(tri, logf, NN, lax.Precision.HIGHEST) + carry_ref[0:1, :]
        o_ref[...] = cs
        carry_ref[...] = jnp.broadcast_to(cs[TR - 1:TR, :], carry_ref.shape)

    return pl.pallas_call(
        body, name="fgate_fwd", grid=(s // TR,),
        in_specs=[_row_spec(LANES, FG_BLOCK), _full_spec((1, LANES))],
        out_specs=_row_spec(LANES), out_shape=jax.ShapeDtypeStruct((s, LANES), F32),
        scratch_shapes=[pltpu.VMEM((SUB, LANES), F32)],
        compiler_params=_params(("arbitrary",)),
    )(proj, bf_pad)


def _fgate_bwd(dfcol, proj, bf_pad):
    s = proj.shape[0]
    nb = s // TR

    def body(df_ref, fg_ref, b_ref, o_ref, db_ref, carry_ref):
        i = pl.program_id(0)

        @pl.when(i == 0)
        def _():
            carry_ref[...] = jnp.zeros_like(carry_ref)
            db_ref[...] = jnp.zeros_like(db_ref)

        row = lax.broadcasted_iota(jnp.int32, (TR, TR), 0)
        col = lax.broadcasted_iota(jnp.int32, (TR, TR), 1)
        tri = (col >= row).astype(F32)
        dlogf = _dot(tri, df_ref[...], NN, lax.Precision.HIGHEST) + carry_ref[0:1, :]
        carry_ref[...] = jnp.broadcast_to(dlogf[0:1, :], carry_ref.shape)
        z = fg_ref[...] + b_ref[...]
        dfg = dlogf * _sigmoid(-z)
        o_ref[...] = dfg.astype(BF16)
        db_ref[0:1, :] += jnp.sum(dfg, axis=0, keepdims=True)

    rev = lambda col: pl.BlockSpec((TR, LANES), lambda i, col=col: (nb - 1 - i, col))
    return pl.pallas_call(
        body, name="fgate_bwd", grid=(nb,),
        in_specs=[rev(0), rev(FG_BLOCK), _full_spec((1, LANES))],
        out_specs=(rev(0), _full_spec((SUB, LANES))),
        out_shape=(jax.ShapeDtypeStruct((s, LANES), BF16), jax.ShapeDtypeStruct((SUB, LANES), F32)),
        scratch_shapes=[pltpu.VMEM((SUB, LANES), F32)],
        compiler_params=_params(("arbitrary",)),
    )(dfcol, proj, bf_pad)


def _resid_norm2(x, z, mod, g):
    s = x.shape[0]

    def body(x_ref, z_ref, mod_ref, g_ref, x1_ref, h_ref):
        x1 = x_ref[...] + mod_ref[2:3, :] * z_ref[...]
        x1_ref[...] = x1
        r = lax.rsqrt(jnp.mean(x1 * x1, axis=-1, keepdims=True) + EPS)
        nrm = x1 * r * g_ref[...]
        h_ref[...] = (nrm * (1.0 + mod_ref[4:5, :]) + mod_ref[3:4, :]).astype(BF16)

    return pl.pallas_call(
        body, name="resid_norm2", grid=(s // TR,),
        in_specs=[_row_spec(D), _row_spec(D), _full_spec((SUB, D)), _full_spec((1, D))],
        out_specs=(_row_spec(D), _row_spec(D)),
        out_shape=(jax.ShapeDtypeStruct((s, D), F32), jax.ShapeDtypeStruct((s, D), BF16)),
        compiler_params=_params(("parallel",)),
    )(x, z, mod, g)


def _loss_head(x1, y, tgt, mod):
    s = x1.shape[0]

    def body(x1_ref, y_ref, t_ref, mod_ref, dout_ref, dy_ref, vec_ref):
        @pl.when(pl.program_id(0) == 0)
        def _():
            vec_ref[...] = jnp.zeros_like(vec_ref)

        yv = y_ref[...]
        g2 = mod_ref[5:6, :]
        diff = x1_ref[...] + g2 * yv - t_ref[...]
        dout = diff * (1.0 / D)
        dout_ref[...] = dout
        dy_ref[...] = (g2 * dout).astype(BF16)
        vec_ref[0:1, :] += jnp.sum(dout * yv, axis=0, keepdims=True)
        vec_ref[1:2, :] += jnp.sum(diff * diff, axis=0, keepdims=True)

    return pl.pallas_call(
        body, name="loss_head", grid=(s // TR,),
        in_specs=[_row_spec(D), _row_spec(D), _row_spec(D), _full_spec((SUB, D))],
        out_specs=(_row_spec(D), _row_spec(D), _full_spec((SUB, D))),
        out_shape=(jax.ShapeDtypeStruct((s, D), F32), jax.ShapeDtypeStruct((s, D), BF16),
                   jax.ShapeDtypeStruct((SUB, D), F32)),
        compiler_params=_params(("arbitrary",)),
    )(x1, y, tgt, mod)


def _norm_mod_bwd(dh, xin, dres, zin, mod, g, shift_row, scale_row, gate_row, name, hosted=None):
    s = dh.shape[0]
    with_gate = gate_row is not None

    def body(*refs):
        if with_gate:
            dh_ref, x_ref, dres_ref, z_ref, mod_ref, g_ref, dx_ref, dz_ref, vec_ref = refs
        else:
            dh_ref, x_ref, dres_ref, mod_ref, g_ref, dx_ref, vec_ref = refs

        @pl.when(pl.program_id(0) == 0)
        def _():
            vec_ref[...] = jnp.zeros_like(vec_ref)

        xv = x_ref[...]
        dhv = dh_ref[...]
        gv = g_ref[...]
        r = lax.rsqrt(jnp.mean(xv * xv, axis=-1, keepdims=True) + EPS)
        xh = xv * r
        dn = dhv * (1.0 + mod_ref[scale_row:scale_row + 1, :])
        dxh = dn * gv
        dx = dres_ref[...] + r * (dxh - xh * jnp.mean(dxh * xh, axis=-1, keepdims=True))
        dx_ref[...] = dx
        vec_ref[0:1, :] += jnp.sum(dhv, axis=0, keepdims=True)
        vec_ref[1:2, :] += jnp.sum(dhv * (xh * gv), axis=0, keepdims=True)
        vec_ref[2:3, :] += jnp.sum(dn * xh, axis=0, keepdims=True)
        if with_gate:
            dz_ref[...] = (mod_ref[gate_row:gate_row + 1, :] * dx).astype(BF16)
            vec_ref[3:4, :] += jnp.sum(dx * z_ref[...], axis=0, keepdims=True)

    ins = [dh, xin, dres] + ([zin] if with_gate else []) + [mod, g]
    in_specs = [_row_spec(D)] * (4 if with_gate else 3) + [_full_spec((SUB, D)), _full_spec((1, D))]
    out_specs = [_row_spec(D)] + ([_row_spec(D)] if with_gate else []) + [_full_spec((SUB, D))]
    out_shape = [jax.ShapeDtypeStruct((s, D), F32)] + ([jax.ShapeDtypeStruct((s, D), BF16)] if with_gate else []) \
        + [jax.ShapeDtypeStruct((SUB, D), F32)]
    outs, moved = _call(body, ins, name=name, grid=(s // TR,), in_specs=in_specs, out_specs=out_specs,
                        out_shape=out_shape, hosted=hosted)
    return outs + (moved,) if hosted else outs


XIN_BLOCK = 3 * AW // LANES
BG_BLOCK = XIN_BLOCK + CW // LANES
CG_BLOCK = BG_BLOCK + CW // LANES


def _seq_spec(s, first_block):
    return pl.BlockSpec((s, LANES), lambda j, fb=first_block: (0, fb + j))


def _mixconv_fwd(proj, w):
    s = proj.shape[0]

    def body(xin_ref, bg_ref, cg_ref, w_ref, o_ref):
        cx = cg_ref[...] * xin_ref[...]
        cv, _, _ = _conv_taps(cx, None, w_ref[...])
        o_ref[...] = bg_ref[...] * cv

    return pl.pallas_call(
        body, name="mixconv_fwd", grid=(CW // LANES,),
        in_specs=[_seq_spec(s, XIN_BLOCK), _seq_spec(s, BG_BLOCK), _seq_spec(s, CG_BLOCK),
                  pl.BlockSpec((3, LANES), lambda j: (0, j))],
        out_specs=_seq_spec(s, 0), out_shape=jax.ShapeDtypeStruct((s, CW), F32),
        compiler_params=_params(("parallel",), 48),
    )(proj, proj, proj, w)


def _mixconv_bwd(dmixed, proj, w):
    s = proj.shape[0]

    def body(d_ref, xin_ref, bg_ref, cg_ref, w_ref, dxin_ref, dbg_ref, dcg_ref, dw_ref):
        wv = w_ref[...]
        xin, cg, dconv = xin_ref[...], cg_ref[...], d_ref[...]
        cx = cg * xin
        cv, s1, s2 = _conv_taps(cx, None, wv)
        dbg_ref[...] = (dconv * cv).astype(BF16)
        dcv = dconv * bg_ref[...]
        dw_ref[...] = jnp.zeros_like(dw_ref)
        dw_ref[0:1, :] = jnp.sum(dcv * s2, axis=0, keepdims=True)
        dw_ref[1:2, :] = jnp.sum(dcv * s1, axis=0, keepdims=True)
        dw_ref[2:3, :] = jnp.sum(dcv * cx, axis=0, keepdims=True)
        dcx = _conv_taps_t(dcv, None, wv)
        dcg_ref[...] = (dcx * xin).astype(BF16)
        dxin_ref[...] = (dcx * cg).astype(BF16)

    o = jax.ShapeDtypeStruct((s, CW), BF16)
    return pl.pallas_call(
        body, name="mixconv_bwd", grid=(CW // LANES,),
        in_specs=[_seq_spec(s, AW // LANES), _seq_spec(s, XIN_BLOCK), _seq_spec(s, BG_BLOCK), _seq_spec(s, CG_BLOCK),
                  pl.BlockSpec((3, LANES), lambda j: (0, j))],
        out_specs=(_seq_spec(s, 0), _seq_spec(s, 0), _seq_spec(s, 0), pl.BlockSpec((SUB, LANES), lambda j: (0, j))),
        out_shape=(o, o, o, jax.ShapeDtypeStruct((SUB, CW), F32)),
        compiler_params=_params(("parallel",), 48),
    )(dmixed, proj, proj, proj, w)


TA = 512
NEG = -1e30


def _causal_mask():
    row = lax.broadcasted_iota(jnp.int32, (TA, TA), 0)
    col = lax.broadcasted_iota(jnp.int32, (TA, TA), 1)
    return col <= row


def _attn_fwd(qp, kp, vp, hosted):
    s = qp.shape[0]
    nq = s // TA

    def body(q_ref, k_ref, v_ref, o_ref, lse_ref):
        i = pl.program_id(1)
        slabs = [slice(SLAB * hh, SLAB * (hh + 1)) for hh in range(2)]
        q = [q_ref[:, sl] for sl in slabs]

        def block(j, carry, masked):
            keys = pl.ds(pl.multiple_of(j * TA, TA), TA)
            ms, acc = carry
            m_out, parts = [], []
            for hh in range(2):
                sc = _dot(q[hh], k_ref[keys, slabs[hh]], NT)
                if masked:
                    sc = jnp.where(_causal_mask(), sc, NEG)
                m_new = jnp.maximum(ms[hh], jnp.max(sc, axis=-1, keepdims=True))
                p = jnp.exp(sc - m_new)
                parts.append(jnp.exp(ms[hh] - m_new) * acc[:, slabs[hh]] + _dot(p.astype(BF16), v_ref[keys, slabs[hh]]))
                m_out.append(m_new)
            return tuple(m_out), jnp.concatenate(parts, axis=1)

        init = ((jnp.full((TA, 1), NEG, F32), jnp.full((TA, 1), NEG, F32)), jnp.zeros((TA, 2 * SLAB), F32))
        carry = lax.fori_loop(0, i, lambda j, cr: block(j, cr, False), init)
        ms, acc = block(i, carry, True)
        for hh in range(2):
            l = acc[:, SLAB * hh + DH:SLAB * hh + DH + 1]
            o_ref[:, DH * hh:DH * (hh + 1)] = acc[:, SLAB * hh:SLAB * hh + DH] / l
            lse_ref[0, :, hh:hh + 1] = ms[hh] + jnp.log(l)

    (o, lse), moved = _call(
        body, (qp, kp, vp), name="attn_fwd", grid=(HEADS // 2, nq),
        in_specs=[pl.BlockSpec((TA, 2 * SLAB), lambda p, i: (i, p)),
                  pl.BlockSpec((s, 2 * SLAB), lambda p, i: (0, p)),
                  pl.BlockSpec((s, 2 * SLAB), lambda p, i: (0, p))],
        out_specs=[pl.BlockSpec((TA, LANES), lambda p, i: (i, p)), pl.BlockSpec((1, TA, 2), lambda p, i: (p, i, 0))],
        out_shape=[jax.ShapeDtypeStruct((s, AW), F32), jax.ShapeDtypeStruct((HEADS // 2, s, 2), F32)],
        vmem_mb=32, hosted=hosted)
    return o, lse, moved


def _attn_bwd(qp, kp, vp, dmixed, o, lse, hosted):
    s = qp.shape[0]
    nq = s // TA

    def body(q_ref, k_ref, v_ref, do_ref, o_ref, lse_ref, dq_ref, dk_ref, dv_ref, qb_ref, dob_ref):
        dk_ref[...] = jnp.zeros_like(dk_ref)
        dv_ref[...] = jnp.zeros_like(dv_ref)
        slabs = [slice(SLAB * hh, SLAB * (hh + 1)) for hh in range(2)]
        lane = lax.broadcasted_iota(jnp.int32, (TA, DH), 1)

        def q_block(i, _):
            i0 = pl.multiple_of(i * TA, TA)
            rows = pl.ds(i0, TA)
            for hh in range(2):
                half = slice(DH * hh, DH * (hh + 1))
                do = do_ref[rows, half]
                delta = jnp.sum(do * o_ref[rows, half], axis=-1, keepdims=True)
                dob_ref[hh, :, 0:DH] = do.astype(BF16)
                dob_ref[hh, :, DH:SLAB] = _lanes3(lane, 0, [-d for d in _split3(delta)], 0.0).astype(BF16)
                lse3 = _split3(lse_ref[0, rows, hh:hh + 1])
                qb_ref[hh, :, 0:DH] = q_ref[rows, SLAB * hh:SLAB * hh + DH]
                aug = q_ref[rows, SLAB * hh + DH:SLAB * (hh + 1)].astype(F32)
                qb_ref[hh, :, DH:SLAB] = _lanes3(lane, AUG_LSE, [-x for x in lse3], aug).astype(BF16)

            def block(j, dq, masked):
                keys = pl.ds(pl.multiple_of(j * TA, TA), TA)
                dv, dk, dqc = [], [], []
                for hh in range(2):
                    q, dob = qb_ref[hh], dob_ref[hh]
                    k = k_ref[keys, slabs[hh]]
                    sc = _dot(q, k, NT)
                    if masked:
                        sc = jnp.where(_causal_mask(), sc, NEG)
                    p = jnp.exp(sc)
                    dv.append(_dot(p.astype(BF16), dob, TN))
                    ds = (p * _dot(dob, v_ref[keys, slabs[hh]], NT)).astype(BF16)
                    dk.append(_dot(ds, q, TN))
                    dqc.append(_dot(ds, k))
                dv_ref[keys, :] += jnp.concatenate(dv, axis=1)
                dk_ref[keys, :] += jnp.concatenate(dk, axis=1)
                return dq + jnp.concatenate(dqc, axis=1)

            dq = lax.fori_loop(0, i, lambda j, acc: block(j, acc, False), jnp.zeros((TA, 2 * SLAB), F32))
            dq_ref[rows, :] = block(i, dq, True)
            return 0

        lax.fori_loop(0, nq, q_block, 0)

    pair = lambda p: (0, p)
    slab2 = pl.BlockSpec((s, 2 * SLAB), pair)
    seq = pl.BlockSpec((s, LANES), pair)
    small = pl.BlockSpec((1, s, 2), lambda p: (p, 0, 0))
    o32 = jax.ShapeDtypeStruct((s, HEADS * SLAB), F32)
    return _call(
        body, (qp, kp, vp, dmixed, o, lse), name="attn_bwd", grid=(HEADS // 2,),
        in_specs=[slab2, slab2, slab2, seq, seq, small], out_specs=[slab2, slab2, slab2], out_shape=[o32, o32, o32],
        scratch_shapes=[pltpu.VMEM((2, TA, SLAB), BF16), pltpu.VMEM((2, TA, SLAB), BF16)], vmem_mb=48, hosted=hosted)


def _qkv_post(dqp, dkp, dvp, proj, gq, gk, hosted):
    s = proj.shape[0]

    def body(dq_ref, dk_ref, dv_ref, q_ref, k_ref, gq_ref, gk_ref, dqo_ref, dko_ref, dvo_ref, df_ref, vec_ref):
        @pl.when(pl.program_id(0) == 0)
        def _():
            vec_ref[...] = jnp.zeros_like(vec_ref)

        def one(d_ref, x_ref, g_ref, o_ref, row, scale):
            dg = jnp.zeros((1, DH), F32)
            for h in range(HEADS):
                sl = slice(DH * h, DH * (h + 1))
                xv = x_ref[:, sl]
                r = lax.rsqrt(jnp.mean(xv * xv, axis=-1, keepdims=True) + EPS)
                xh = xv * r
                dn = d_ref[:, SLAB * h:SLAB * h + DH] * scale
                dg = dg + jnp.sum(dn * xh, axis=0, keepdims=True)
                dxh = dn * g_ref[...]
                o_ref[:, sl] = (r * (dxh - xh * jnp.mean(dxh * xh, axis=-1, keepdims=True))).astype(BF16)
            vec_ref[row:row + 1, 0:DH] += dg

        one(dq_ref, q_ref, gq_ref, dqo_ref, 0, QK_SCALE)
        one(dk_ref, k_ref, gk_ref, dko_ref, 1, 1.0)
        lane = lax.broadcasted_iota(jnp.int32, (TR, LANES), 1)
        df = jnp.zeros((TR, LANES), F32)
        for h in range(HEADS):
            dvo_ref[:, DH * h:DH * (h + 1)] = dv_ref[:, SLAB * h:SLAB * h + DH].astype(BF16)
            row_sum = dq_ref[:, SLAB * h + DH:SLAB * h + DH + 1]
            col_sum = dk_ref[:, SLAB * h + DH + AUG_ONE:SLAB * h + DH + AUG_ONE + 1]
            df = jnp.where(lane == h, row_sum - col_sum, df)
        df_ref[...] = df

    o = jax.ShapeDtypeStruct((s, AW), BF16)
    wide = _row_spec(HEADS * SLAB)
    return _call(
        body, (dqp, dkp, dvp, proj, proj, gq, gk), name="qkv_post", grid=(s // TR,),
        in_specs=[wide, wide, wide, _row_spec(AW, 0), _row_spec(AW, 1), _full_spec((1, DH)), _full_spec((1, DH))],
        out_specs=[_row_spec(AW), _row_spec(AW), _row_spec(AW), _row_spec(LANES), _full_spec((SUB, LANES))],
        out_shape=[o, o, o, jax.ShapeDtypeStruct((s, LANES), F32), jax.ShapeDtypeStruct((SUB, LANES), F32)],
        hosted=hosted)


TF = 256
NJ = DFF // TF
FFN_ROWS_FWD = 1024
FFN_ROWS_BWD = 512


def _ffn_fwd(h2, wup_t, cw, wd):
    s = h2.shape[0]
    tr = FFN_ROWS_FWD
    nr = s // tr

    def body(h_ref, wu_ref, cg_ref, cv_ref, wd_ref, pg_ref, pv_ref, y_ref, halo_ref):
        r, j = pl.program_id(0), pl.program_id(1)
        hv = h_ref[...]
        pg = _dot(hv, wu_ref[0], NT).astype(BF16)
        pv = _dot(hv, wu_ref[1], NT).astype(BF16)
        pg_ref[...] = pg
        pv_ref[...] = pv
        pgf, pvf = pg.astype(F32), pv.astype(F32)
        ug, _, _ = _conv_taps(pgf, jnp.where(r > 0, halo_ref[j, 0], 0.0), cg_ref[...])
        uv, _, _ = _conv_taps(pvf, jnp.where(r > 0, halo_ref[j, 1], 0.0), cv_ref[...])
        halo_ref[j, 0] = pgf[tr - SUB:tr, :]
        halo_ref[j, 1] = pvf[tr - SUB:tr, :]
        act = (ug * _sigmoid(ug) * uv).astype(BF16)
        contrib = _dot(act, wd_ref[...])

        @pl.when(j == 0)
        def _():
            y_ref[...] = contrib

        @pl.when(j > 0)
        def _():
            y_ref[...] += contrib

    pre = jax.ShapeDtypeStruct((s, DFF), BF16)
    return pl.pallas_call(
        body, name="ffn_fwd", grid=(nr, NJ),
        in_specs=[pl.BlockSpec((tr, D), lambda r, j: (r, 0)),
                  pl.BlockSpec((2, TF, D), lambda r, j: (0, j, 0)),
                  pl.BlockSpec((3, TF), lambda r, j: (0, j)),
                  pl.BlockSpec((3, TF), lambda r, j: (0, NJ + j)),
                  pl.BlockSpec((TF, D), lambda r, j: (j, 0))],
        out_specs=(pl.BlockSpec((tr, TF), lambda r, j: (r, j)),
                   pl.BlockSpec((tr, TF), lambda r, j: (r, j)),
                   pl.BlockSpec((tr, D), lambda r, j: (r, 0))),
        out_shape=(pre, pre, jax.ShapeDtypeStruct((s, D), F32)),
        scratch_shapes=[pltpu.VMEM((NJ, 2, SUB, TF), F32)],
        compiler_params=_params(("arbitrary", "arbitrary"), 56),
    )(h2, wup_t, cw, cw, wd)


def _ffn_bwd(dy, h2, pre_g, pre_v, wup_t, cw, wd):
    s = h2.shape[0]
    tr = FFN_ROWS_BWD
    nr = s // tr
    hb = tr // (2 * SUB)

    def body(dy_ref, h_ref, pg_ref, pv_ref, hg_ref, hv_ref, wu_ref, cg_ref, cv_ref, wd_ref,
             dh_ref, dwu_ref, dwd_ref, dcg_ref, dcv_ref, nxt_ref, awu_ref, awd_ref):
        j, r = pl.program_id(0), pl.program_id(1)
        rr = nr - 1 - r
        row0 = pl.multiple_of(rr * tr, tr)
        cwg, cwv = cg_ref[...], cv_ref[...]
        pg, pv = pg_ref[...].astype(F32), pv_ref[...].astype(F32)
        ug, g1, g2 = _conv_taps(pg, jnp.where(rr > 0, hg_ref[SUB:2 * SUB, :].astype(F32), 0.0), cwg)
        uv, v1, v2 = _conv_taps(pv, jnp.where(rr > 0, hv_ref[SUB:2 * SUB, :].astype(F32), 0.0), cwv)
        sg = _sigmoid(ug)
        sil = ug * sg
        act = (sil * uv).astype(BF16)
        dyv = dy_ref[...]
        da = _dot(dyv, wd_ref[...], NT)
        dug = da * uv * (sg * (1.0 + ug * (1.0 - sg)))
        duv = da * sil
        dpg = _conv_taps_t(dug, jnp.where(r > 0, nxt_ref[0], 0.0), cwg)
        dpv = _conv_taps_t(duv, jnp.where(r > 0, nxt_ref[1], 0.0), cwv)
        nxt_ref[0] = dug[0:SUB, :]
        nxt_ref[1] = duv[0:SUB, :]
        dpgb, dpvb = dpg.astype(BF16), dpv.astype(BF16)
        hv = h_ref[...]
        dwd = _dot(act, dyv, TN)
        dwg = _dot(dpgb, hv, TN)
        dwv = _dot(dpvb, hv, TN)
        dh = _dot(dpgb, wu_ref[0]) + _dot(dpvb, wu_ref[1])

        def taps(du, x0, x1, x2):
            return (jnp.sum(du * x2, axis=0, keepdims=True), jnp.sum(du * x1, axis=0, keepdims=True),
                    jnp.sum(du * x0, axis=0, keepdims=True))

        tg, tv = taps(dug, pg, g1, g2), taps(duv, pv, v1, v2)

        @pl.when(r == 0)
        def _():
            awd_ref[...] = dwd
            awu_ref[0] = dwg
            awu_ref[1] = dwv
            dcg_ref[...] = jnp.zeros_like(dcg_ref)
            dcv_ref[...] = jnp.zeros_like(dcv_ref)

        @pl.when(r > 0)
        def _():
            awd_ref[...] += dwd
            awu_ref[0] += dwg
            awu_ref[1] += dwv

        @pl.when(r == nr - 1)
        def _():
            dwd_ref[...] = awd_ref[...].astype(BF16)
            dwu_ref[...] = awu_ref[...].astype(BF16)

        for t in range(3):
            dcg_ref[t:t + 1, :] += tg[t]
            dcv_ref[t:t + 1, :] += tv[t]

        @pl.when(j == 0)
        def _():
            dh_ref[pl.ds(row0, tr), :] = dh

        @pl.when(j > 0)
        def _():
            dh_ref[pl.ds(row0, tr), :] += dh

    rows = lambda j, r: (nr - 1 - r, 0)
    tile = lambda j, r: (nr - 1 - r, j)
    halo = lambda j, r: (jnp.maximum((nr - 1 - r) * hb - 1, 0), j)
    return pl.pallas_call(
        body, name="ffn_bwd", grid=(NJ, nr),
        in_specs=[pl.BlockSpec((tr, D), rows), pl.BlockSpec((tr, D), rows),
                  pl.BlockSpec((tr, TF), tile), pl.BlockSpec((tr, TF), tile),
                  pl.BlockSpec((2 * SUB, TF), halo), pl.BlockSpec((2 * SUB, TF), halo),
                  pl.BlockSpec((2, TF, D), lambda j, r: (0, j, 0)),
                  pl.BlockSpec((3, TF), lambda j, r: (0, j)), pl.BlockSpec((3, TF), lambda j, r: (0, NJ + j)),
                  pl.BlockSpec((TF, D), lambda j, r: (j, 0))],
        out_specs=(pl.BlockSpec((s, D), lambda j, r: (0, 0)),
                   pl.BlockSpec((2, TF, D), lambda j, r: (0, j, 0)),
                   pl.BlockSpec((TF, D), lambda j, r: (j, 0)),
                   pl.BlockSpec((SUB, TF), lambda j, r: (0, j)), pl.BlockSpec((SUB, TF), lambda j, r: (0, j))),
        out_shape=(jax.ShapeDtypeStruct((s, D), F32),
                   jax.ShapeDtypeStruct((2, DFF, D), BF16), jax.ShapeDtypeStruct((DFF, D), BF16),
                   jax.ShapeDtypeStruct((SUB, DFF), F32), jax.ShapeDtypeStruct((SUB, DFF), F32)),
        scratch_shapes=[pltpu.VMEM((2, SUB, TF), F32), pltpu.VMEM((2, TF, D), F32), pltpu.VMEM((TF, D), F32)],
        compiler_params=_params(("arbitrary", "arbitrary"), 56),
    )(dy, h2, pre_g, pre_v, pre_g, pre_v, wup_t, cw, cw, wd)


def _adam(w, g, m, v):
    m = ADAM_B1 * m + (1.0 - ADAM_B1) * g
    v = ADAM_B2 * v + (1.0 - ADAM_B2) * (g * g)
    m_hat = m / (1.0 - ADAM_B1 ** ADAM_STEP)
    v_hat = v / (1.0 - ADAM_B2 ** ADAM_STEP)
    delta = -ADAM_LR * (m_hat / (jnp.sqrt(v_hat) + ADAM_EPS) + ADAM_WD * w)
    return delta, m, v


NCHIP = NDEV // 2


def _pair_add(mine, theirs, tr, name):
    _, _, rws, cols = mine.shape

    def body(a_ref, b_ref, o_ref):
        c = lax.axis_index("c")
        o_ref[0] = (a_ref[0, c].astype(F32) + b_ref[0].astype(F32)).astype(BF16)

    (out,), _ = _call(
        body, (mine, theirs), name=name, grid=(NCHIP, rws // tr),
        in_specs=[pl.BlockSpec((1, 2, tr, cols), lambda q, i: (q, 0, i, 0)),
                  pl.BlockSpec((1, tr, cols), lambda q, i: (q, i, 0))],
        out_specs=[pl.BlockSpec((1, tr, cols), lambda q, i: (q, i, 0))],
        out_shape=[jax.ShapeDtypeStruct((NCHIP, rws, cols), BF16)], vmem_mb=32)
    return out


def _adamw_sharded(parts, w, m, v, tr, name, hosted=None):
    rws, cols = w.shape

    def body(p_ref, w_ref, m_ref, v_ref, g_ref, d_ref, mo_ref, vo_ref):
        g = p_ref[0].astype(F32)
        for q in range(1, NCHIP):
            g = g + p_ref[q].astype(F32)
        g_ref[...] = g
        d_ref[...], mo_ref[...], vo_ref[...] = _adam(w_ref[...], g, m_ref[...], v_ref[...])

    blk = pl.BlockSpec((tr, cols), lambda i: (i, 0))
    o = jax.ShapeDtypeStruct((rws, cols), F32)
    outs, moved = _call(
        body, (parts, w, m, v), name=name, grid=(rws // tr,),
        in_specs=[pl.BlockSpec((NCHIP, tr, cols), lambda i: (0, i, 0)), blk, blk, blk],
        out_specs=[blk, blk, blk, blk], out_shape=[o, o, o, o], vmem_mb=48, hosted=hosted)
    return (outs, moved) if hosted else outs


def _adamw_ada(c_all, dmod_my, w, m, v):
    rws, cols = w.shape
    tr = 256

    def body(c_ref, dm_ref, w_ref, m_ref, v_ref, g_ref, d_ref, mo_ref, vo_ref):
        cv = c_ref[...]
        act = cv * _sigmoid(cv)
        g = _dot(act, dm_ref[...], TN, lax.Precision.HIGHEST)
        g_ref[...] = g
        d_ref[...], mo_ref[...], vo_ref[...] = _adam(w_ref[...], g, m_ref[...], v_ref[...])

    blk = pl.BlockSpec((tr, cols), lambda i: (i, 0))
    o = jax.ShapeDtypeStruct((rws, cols), F32)
    return pl.pallas_call(
        body, name="adamw_ada", grid=(rws // tr,),
        in_specs=[pl.BlockSpec((NDEV, tr), lambda i: (0, i)), _full_spec((NDEV, cols)), blk, blk, blk],
        out_specs=(blk, blk, blk, blk), out_shape=(o, o, o, o),
        compiler_params=_params(("parallel",), 48),
    )(c_all, dmod_my, w, m, v)


REP_ROWS = 16
ROW_N1, ROW_N2, ROW_LOSS, ROW_MISC = 6, 7, 8, 9
LANE_BF, LANE_GQ, LANE_GK = 0, 128, 256


def _adamw_small(rep_all, conv_all, wmv):
    n_ff = wmv[6][0].shape[1]

    def body(*refs):
        rep_ref, conv_ref = refs[:2]
        ins = refs[2:2 + 24]
        outs = refs[2 + 24:]
        loss_ref, outs = outs[0], outs[1:]
        g_rep = rep_ref[0]
        g_conv = conv_ref[0]
        for d in range(1, NDEV):
            g_rep = g_rep + rep_ref[d]
            g_conv = g_conv + conv_ref[d]
        loss_ref[...] = (0.5 / D) * jnp.sum(g_rep[ROW_LOSS:ROW_LOSS + 1, :], axis=-1, keepdims=True)
        grads = [
            None,
            g_rep[ROW_N1:ROW_N1 + 1, :],
            g_rep[ROW_MISC:ROW_MISC + 1, LANE_BF:LANE_BF + HEADS],
            g_rep[ROW_MISC:ROW_MISC + 1, LANE_GQ:LANE_GQ + DH],
            g_rep[ROW_MISC:ROW_MISC + 1, LANE_GK:LANE_GK + DH],
            g_rep[ROW_N2:ROW_N2 + 1, :],
            g_conv[0:3, 0:n_ff],
            g_conv[0:3, n_ff:n_ff + DH],
        ]
        for p in range(8):
            w_ref, m_ref, v_ref = ins[3 * p:3 * p + 3]
            g_ref, d_ref, mo_ref, vo_ref = outs[4 * p:4 * p + 4]
            if p == 0:
                for nmod in range(NMOD):
                    sl = slice(D * nmod, D * (nmod + 1))
                    g = g_rep[nmod:nmod + 1, :]
                    g_ref[:, sl] = g
                    d_ref[:, sl], mo_ref[:, sl], vo_ref[:, sl] = _adam(w_ref[:, sl], g, m_ref[:, sl], v_ref[:, sl])
            else:
                g = grads[p]
                g_ref[...] = g
                d_ref[...], mo_ref[...], vo_ref[...] = _adam(w_ref[...], g, m_ref[...], v_ref[...])

    flat = [a for trio in wmv for a in trio]
    out_shape = [jax.ShapeDtypeStruct((1, 1), F32)]
    for trio in wmv:
        out_shape += [jax.ShapeDtypeStruct(trio[0].shape, F32)] * 4
    return pl.pallas_call(
        body, name="adamw_small", out_shape=tuple(out_shape),
        compiler_params=_params(None, 32),
    )(rep_all, conv_all, *flat)


FG_FIRST = 3 * AW
N_IN = DIN // NDEV


def _w_in_runs():
    runs = []
    for d in range(NDEV):
        lo, hi = N_IN * d, N_IN * (d + 1)
        for a, b, shift in ((0, FG_FIRST, 0), (FG_FIRST, FG_FIRST + HEADS, DIN - HEADS - FG_FIRST),
                            (FG_FIRST + HEADS, DIN, -HEADS)):
            a, b = max(a, lo), min(b, hi)
            if a < b:
                runs.append((d, a - lo, a + shift, b - a))
    return runs


W_IN_ROWS = 256


def _assemble_w_in(g_in):
    def body(g_ref, o_ref):
        for d, src, dst, width in _w_in_runs():
            o_ref[:, dst:dst + width] = g_ref[d, :, src:src + width]
        o_ref[:, DIN:DINP] = jnp.zeros((W_IN_ROWS, DINP - DIN), o_ref.dtype)

    (out,), _ = _call(
        body, (g_in,), name="assemble_w_in", grid=(D // W_IN_ROWS,),
        in_specs=[pl.BlockSpec((NDEV, W_IN_ROWS, N_IN), lambda i: (0, i, 0))],
        out_specs=[pl.BlockSpec((W_IN_ROWS, DINP), lambda i: (i, 0))],
        out_shape=[jax.ShapeDtypeStruct((D, DINP), g_in.dtype)], vmem_mb=32)
    return out


def _scatter_dw_in(dwp):
    def body(w_ref, o_ref):
        for d, src, dst, width in _w_in_runs():
            o_ref[d // 2, d % 2, :, src:src + width] = w_ref[:, dst:dst + width]

    (out,), _ = _call(
        body, (dwp,), name="scatter_dw_in", grid=(D // W_IN_ROWS,),
        in_specs=[pl.BlockSpec((W_IN_ROWS, DINP), lambda i: (i, 0))],
        out_specs=[pl.BlockSpec((NCHIP, 2, W_IN_ROWS, N_IN), lambda i: (0, 0, i, 0))],
        out_shape=[jax.ShapeDtypeStruct((NCHIP, 2, D, N_IN), dwp.dtype)], vmem_mb=32)
    return out


def _by_core_chip(a):
    return jnp.transpose(a.reshape((NCHIP, 2) + a.shape[1:]), (1, 0, 2, 3))


def kernel(x, c, w_ada, b_ada, norm1_g, w_in, b_forget, q_norm_g, k_norm_g, conv_mix_w, w_out, norm2_g, w_up, ffn_conv_w, w_down, loss_target, m_w_ada, m_b_ada, m_norm1_g, m_w_in, m_b_forget, m_q_norm_g, m_k_norm_g, m_conv_mix_w, m_w_out, m_norm2_g, m_w_up, m_ffn_conv_w, m_w_down, v_w_ada, v_b_ada, v_norm1_g, v_w_in, v_b_forget, v_q_norm_g, v_k_norm_g, v_conv_mix_w, v_w_out, v_norm2_g, v_w_up, v_ffn_conv_w, v_w_down):
    me = 4 * lax.axis_index("x") + 2 * lax.axis_index("y") + lax.axis_index("c")
    xs, tgt = x[0], loss_target[0]
    s = xs.shape[0]
    nq = s // TA
    n_ada = w_ada.shape[2]
    n_ff = w_up.shape[2]

    conv_w = jnp.concatenate([ffn_conv_w[0], conv_mix_w[0]], axis=1)
    conv_w = jnp.concatenate([conv_w, jnp.zeros((SUB - 3, conv_w.shape[1]), F32)], axis=0)
    c_all, conv_all, g_in = _exchange(
        [(c.reshape(SUB, D // SUB), "ag"), (conv_w, "ag"), (w_in[0].astype(BF16), "ag2")], "exchange_w_in")
    c_all = c_all.reshape(NDEV, D)
    cw_ffn = jnp.transpose(conv_all[:, :3, :n_ff], (1, 0, 2)).reshape(3, 2 * DFF)
    cw_mix = jnp.transpose(conv_all[:, :3, n_ff:], (1, 0, 2)).reshape(3, CW)
    w_in_p = _assemble_w_in(g_in)

    b_my = lax.dynamic_slice(b_ada, (0, me * n_ada), (1, n_ada))
    mod_part = _ada_fwd(c_all, w_ada[0], b_my)
    (mod_rows,) = _exchange([(jnp.broadcast_to(mod_part[:, None, :], (NDEV, SUB, n_ada)), "a2a")], "exchange_mod")
    mod = mod_rows[:, 0, :].reshape(NMOD, D)
    mod = jnp.concatenate([mod, jnp.zeros((SUB - NMOD, D), F32)], axis=0)

    h = _norm_mod_fwd(xs, mod, norm1_g)
    proj, (g_down,) = _mm(h, w_in_p, "nn", F32, 512, 640, "proj_fwd", hosted=[(w_down[0].astype(BF16), "ag2")])
    bf_pad = jnp.concatenate([b_forget, jnp.zeros((1, LANES - HEADS), F32)], axis=1)
    fcum = _fgate_fwd(proj, bf_pad)
    (qp, kp, vp), (g_out,) = _qkv_prep(proj, fcum, q_norm_g, k_norm_g, [(w_out[0].astype(BF16), "ag2")])
    attn, lse, (g_up,) = _attn_fwd(qp, kp, vp, [(jnp.transpose(w_up[0]).astype(BF16), "ag2")])
    w_out_f = g_out.reshape(D, D)
    w_up_t = g_up.reshape(2, DFF, D)
    w_down_f = g_down.reshape(DFF, D)
    conv = _mixconv_fwd(proj, cw_mix)
    mixed = jnp.concatenate([attn, conv], axis=1).astype(BF16)
    z = _mm(mixed, w_out_f, "nn", F32, 512, 512, "out_fwd")
    x1, h2 = _resid_norm2(xs, z, mod, norm2_g)
    pre_g, pre_v, y = _ffn_fwd(h2, w_up_t, cw_ffn, w_down_f)
    dout, dy, vec_l = _loss_head(x1, y, tgt, mod)

    dh2, dwup_t, dwd, dcw_g, dcw_v = _ffn_bwd(dy, h2, pre_g, pre_v, w_up_t, cw_ffn, w_down_f)
    dx1, dz, vec_2 = _norm_mod_bwd(dh2, x1, dout, z, mod, norm2_g, 3, 4, 2, "norm2_bwd")
    dwout = _mm(mixed, dz, "tn", BF16, 512, 512, "out_bwd_w")
    s_out = dwout.reshape(NCHIP, 2, D // NDEV, D)
    s_down = dwd.reshape(NCHIP, 2, DFF // NDEV, D)
    s_up = dwup_t.reshape(NCHIP, 2, n_ff, D)
    dmixed, (t_out, t_up, t_down) = _mm(dz, w_out_f, "nt", F32, 512, 512, "out_bwd_x",
                                        hosted=[(s_out, "pair"), (s_up, "pair"), (s_down, "pair")])
    c_out = _pair_add(s_out, t_out, 128, "pair_add_out")
    c_up = _pair_add(s_up, t_up, 176, "pair_add_up")
    c_down = _pair_add(s_down, t_down, 176, "pair_add_down")
    dxin, dbg, dcg, dcw_mix = _mixconv_bwd(dmixed, proj, cw_mix)
    (dqp, dkp, dvp), (p_up,) = _attn_bwd(qp, kp, vp, dmixed, attn, lse, [(c_up, "chips")])
    (dq, dk, dvb, dfcol, vec_qk), (p_out, p_down) = _qkv_post(
        dqp, dkp, dvp, proj, q_norm_g, k_norm_g, [(c_out, "chips"), (c_down, "chips")])
    dfg, vec_bf = _fgate_bwd(dfcol, proj, bf_pad)
    dproj = jnp.concatenate([dq, dk, dvb, dxin, dbg, dcg, dfg], axis=1)
    dwin_p = _mm(h, dproj, "tn", BF16, 512, 640, "proj_bwd_w")
    s_in = _scatter_dw_in(dwin_p)
    (t_in,) = _exchange([(s_in, "pair")], "exchange_pair_in")
    c_in = _pair_add(s_in, t_in, 256, "pair_add_in")
    dh, (p_in,) = _mm(dproj, w_in_p, "nt", F32, 512, 512, "proj_bwd_x", hosted=[(c_in, "chips")])
    grad_x, vec_1 = _norm_mod_bwd(dh, xs, dx1, None, mod, norm1_g, 0, 1, None, "norm1_bwd")

    misc = jnp.zeros((1, D), F32)
    misc = lax.dynamic_update_slice(misc, vec_bf[0:1, :HEADS], (0, LANE_BF))
    misc = lax.dynamic_update_slice(misc, vec_qk[0:1, :DH], (0, LANE_GQ))
    misc = lax.dynamic_update_slice(misc, vec_qk[1:2, :DH], (0, LANE_GK))
    rep = jnp.concatenate([
        vec_1[0:1], vec_1[1:2], vec_2[3:4], vec_2[0:1], vec_2[1:2], vec_l[0:1],
        vec_1[2:3], vec_2[2:3], vec_l[1:2], misc, jnp.zeros((REP_ROWS - 10, D), F32)], axis=0)
    dcw_ffn = jnp.concatenate([dcw_g, dcw_v], axis=1).reshape(SUB, NDEV, n_ff)
    dcw_all = jnp.concatenate([jnp.transpose(dcw_ffn, (1, 0, 2)),
                               jnp.transpose(dcw_mix.reshape(SUB, NDEV, DH), (1, 0, 2))], axis=2)
    r_out, (rep_all, conv_parts) = _adamw_sharded(p_out, w_out[0], m_w_out[0], v_w_out[0], 128, "adamw_out",
                                                  hosted=[(rep, "ag"), (dcw_all, "a2a")])
    dmod_my = lax.dynamic_slice(rep_all[:, :NMOD, :].reshape(NDEV, NMOD * D), (0, me * n_ada), (NDEV, n_ada))
    r_ada = _adamw_ada(c_all, dmod_my, w_ada[0], m_w_ada[0], v_w_ada[0])
    r_in = _adamw_sharded(p_in, w_in[0], m_w_in[0], v_w_in[0], 256, "adamw_in")
    r_up = _adamw_sharded(p_up, jnp.transpose(w_up[0]), jnp.transpose(m_w_up[0]), jnp.transpose(v_w_up[0]), 176,
                          "adamw_up")
    r_up = tuple(jnp.transpose(a) for a in r_up)
    r_down = _adamw_sharded(p_down, w_down[0], m_w_down[0], v_w_down[0], 176, "adamw_down")
    small = _adamw_small(rep_all, conv_parts, [
        [b_ada, m_b_ada, v_b_ada], [norm1_g, m_norm1_g, v_norm1_g], [b_forget, m_b_forget, v_b_forget],
        [q_norm_g, m_q_norm_g, v_q_norm_g], [k_norm_g, m_k_norm_g, v_k_norm_g], [norm2_g, m_norm2_g, v_norm2_g],
        [ffn_conv_w[0], m_ffn_conv_w[0], v_ffn_conv_w[0]], [conv_mix_w[0], m_conv_mix_w[0], v_conv_mix_w[0]]])
    loss = small[0].reshape(())
    r_bada, r_n1, r_bf, r_gq, r_gk, r_n2, r_cf, r_cm = [small[1 + 4 * p:5 + 4 * p] for p in range(8)]
    lead = lambda t: tuple(a[None] for a in t)
    per_w = [lead(r_ada), r_bada, r_n1, lead(r_in), r_bf, r_gq, r_gk, lead(r_cm), lead(r_out), r_n2,
             lead(r_up), lead(r_cf), lead(r_down)]
    outs = [loss, grad_x[None]]
    for field in range(4):
        outs += [t[field] for t in per_w]
    return tuple(outs)
```

```python
import functools

import jax
import jax.numpy as jnp
import numpy as np
from jax import lax
from jax.experimental import pallas as pl
from jax.experimental.pallas import tpu as pltpu

F32 = jnp.float32
BF16 = jnp.bfloat16

NDEV = 8
D = 1024
HEADS = 8
DH = 64
AW = 512
CW = 512
DFF = 2816
DIN = 3080
DINP = 3200
NMOD = 6
EPS = 1e-6
QK_SCALE = 0.125
LANES = 128
SUB = 8

ADAM_LR = 0.001
ADAM_B1 = 0.9
ADAM_B2 = 0.999
ADAM_EPS = 1e-08
ADAM_WD = 0.01
ADAM_STEP = 10

MESH = pl.DeviceIdType.MESH
ANY = pl.BlockSpec(memory_space=pl.ANY)

NN = (((1,), (0,)), ((), ()))
NT = (((1,), (1,)), ((), ()))
TN = (((0,), (0,)), ((), ()))


def _dot(a, b, dims=NN, precision=None):
    return lax.dot_general(a, b, dims, precision=precision, preferred_element_type=F32)


def _params(sem=None, vmem_mb=None):
    kw = {}
    if sem is not None:
        kw["dimension_semantics"] = sem
    if vmem_mb is not None:
        kw["vmem_limit_bytes"] = vmem_mb * 1024 * 1024
    return pltpu.CompilerParams(**kw)


def _sigmoid(x):
    return 0.5 * jnp.tanh(0.5 * x) + 0.5


class _Exchange:
    def __init__(self, items):
        self.arrays = [a for a, _ in items]
        self.modes = [m for _, m in items]
        self.n = len(items)
        self.out_shape = []
        for a, m in items:
            sh = {"ag": (NDEV,) + a.shape, "ag2": (NDEV,) + a.shape, "pair": a.shape[:1] + a.shape[2:]}.get(m, a.shape)
            self.out_shape.append(jax.ShapeDtypeStruct(sh, a.dtype))
        self.scratch = [pltpu.SemaphoreType.DMA((self.n, NDEV - 1)), pltpu.SemaphoreType.DMA((self.n, NDEV - 1)),
                        pltpu.SemaphoreType.DMA((self.n,))]

    def _plan(self, srcs, outs, sems):
        send_sems, recv_sems, loc_sems = sems
        x, y, c = lax.axis_index("x"), lax.axis_index("y"), lax.axis_index("c")
        me, my_chip = 4 * x + 2 * y + c, 2 * x + y
        sib = (x, y, 1 - c)
        local, first, landed, forwards, arrivals = [], [], [], [], []

        def remote(a, k, src, dst, to):
            return pltpu.make_async_remote_copy(src_ref=src, dst_ref=dst, send_sem=send_sems.at[a, k],
                                                recv_sem=recv_sems.at[a, k], device_id=to, device_id_type=MESH)

        for a, mode in enumerate(self.modes):
            src, out = srcs[a], outs[a]
            if mode in ("ag", "a2a"):
                piece = (lambda slot, src=src: src) if mode == "ag" else (lambda slot, src=src: src.at[slot])
                local.append(pltpu.make_async_copy(piece(me), out.at[me], loc_sems.at[a]))
                for r in range(1, NDEV):
                    px = 1 - x if (r >> 2) & 1 else x
                    py = 1 - y if (r >> 1) & 1 else y
                    pc = 1 - c if r & 1 else c
                    pidx = 4 * px + 2 * py + pc
                    first.append(remote(a, r - 1, piece(pidx), out.at[me], (px, py, pc)))
                    arrivals.append(remote(a, r - 1, piece(pidx), out.at[pidx], (px, py, pc)))
            elif mode == "ag2":
                local.append(pltpu.make_async_copy(src, out.at[me], loc_sems.at[a]))
                first.append(remote(a, 0, src, out.at[me], sib))
                arrivals.append(remote(a, 0, src, out.at[me + 1 - 2 * c], sib))
                for j, (px, py) in enumerate([(1 - x, y), (x, 1 - y), (1 - x, 1 - y)]):
                    theirs = out.at[4 * px + 2 * py + c]
                    first.append(remote(a, 1 + j, src, out.at[me], (px, py, c)))
                    landed.append(remote(a, 1 + j, src, theirs, (px, py, c)))
                    forwards.append(remote(a, 4 + j, theirs, theirs, sib))
                    arrivals.append(remote(a, 4 + j, src, out.at[4 * px + 2 * py + 1 - c], sib))
            elif mode == "pair":
                for q in range(NDEV // 2):
                    first.append(remote(a, q, src.at[q, 1 - c], out.at[q], sib))
                    arrivals.append(remote(a, q, src.at[q, 1 - c], out.at[q], sib))
            else:
                assert mode == "chips", mode
                local.append(pltpu.make_async_copy(src.at[my_chip], out.at[my_chip], loc_sems.at[a]))
                for j, (px, py) in enumerate([(1 - x, y), (x, 1 - y), (1 - x, 1 - y)]):
                    q = 2 * px + py
                    first.append(remote(a, 1 + j, src.at[q], out.at[my_chip], (px, py, c)))
                    arrivals.append(remote(a, 1 + j, src.at[q], out.at[q], (px, py, c)))
        return local, first, landed, forwards, arrivals

    def start(self, srcs, outs, sems):
        local, first, _, _, _ = self._plan(srcs, outs, sems)
        for cp in local + first:
            cp.start()

    def wait(self, srcs, outs, sems):
        local, first, landed, forwards, arrivals = self._plan(srcs, outs, sems)
        for cp, fwd in zip(landed, forwards):
            cp.wait_recv()
            fwd.start()
        for cp in arrivals:
            cp.wait_recv()
        for cp in first + forwards:
            cp.wait_send()
        for cp in local:
            cp.wait()


def _exchange(items, name):
    ex = _Exchange(items)
    n = ex.n

    def body(*refs):
        srcs, outs, sems = refs[:n], refs[n:2 * n], refs[2 * n:]
        ex.start(srcs, outs, sems)
        ex.wait(srcs, outs, sems)

    return pl.pallas_call(
        body, name=name,
        out_shape=tuple(ex.out_shape),
        in_specs=[ANY] * n, out_specs=tuple([ANY] * n),
        scratch_shapes=ex.scratch,
        compiler_params=pltpu.CompilerParams(has_side_effects=True),
    )(*ex.arrays)


def _call(body, inputs, *, name, grid, in_specs, out_specs, out_shape, scratch_shapes=(), vmem_mb=None, hosted=None):
    out_specs, out_shape, scratch_shapes = tuple(out_specs), tuple(out_shape), list(scratch_shapes)
    if not hosted:
        res = pl.pallas_call(
            body, name=name, grid=grid, in_specs=list(in_specs), out_specs=out_specs, out_shape=out_shape,
            scratch_shapes=scratch_shapes, compiler_params=_params(("arbitrary",) * len(grid), vmem_mb),
        )(*inputs)
        return tuple(res), ()
    ex = _Exchange(hosted)
    n, n_in, n_out, n_scr = ex.n, len(inputs), len(out_shape), len(scratch_shapes)

    def hosting_body(*refs):
        ins, srcs = refs[:n_in], refs[n_in:n_in + n]
        outs, landing = refs[n_in + n:n_in + n + n_out], refs[n_in + n + n_out:n_in + 2 * n + n_out]
        scratch, sems = refs[n_in + 2 * n + n_out:n_in + 2 * n + n_out + n_scr], refs[n_in + 2 * n + n_out + n_scr:]
        first = functools.reduce(jnp.logical_and, [pl.program_id(d) == 0 for d in range(len(grid))])
        last = functools.reduce(jnp.logical_and, [pl.program_id(d) == grid[d] - 1 for d in range(len(grid))])

        @pl.when(first)
        def _():
            ex.start(srcs, landing, sems)

        body(*ins, *outs, *scratch)

        @pl.when(last)
        def _():
            ex.wait(srcs, landing, sems)

    res = pl.pallas_call(
        hosting_body, name=name, grid=grid,
        in_specs=list(in_specs) + [ANY] * n, out_specs=out_specs + tuple([ANY] * n),
        out_shape=out_shape + tuple(ex.out_shape), scratch_shapes=scratch_shapes + ex.scratch,
        compiler_params=_params(("arbitrary",) * len(grid), vmem_mb),
    )(*inputs, *ex.arrays)
    return tuple(res[:n_out]), tuple(res[n_out:])


def _mm(a, b, mode, out_dtype, tm, tn, name, hosted=None):
    if mode == "nn":
        (m, k), n = a.shape, b.shape[1]
        a_spec = pl.BlockSpec((tm, k), lambda i, j: (i, 0))
        b_spec = pl.BlockSpec((k, tn), lambda i, j: (0, j))
        dims = NN
    elif mode == "nt":
        (m, k), n = a.shape, b.shape[0]
        a_spec = pl.BlockSpec((tm, k), lambda i, j: (i, 0))
        b_spec = pl.BlockSpec((tn, k), lambda i, j: (j, 0))
        dims = NT
    else:
        (k, m), n = a.shape, b.shape[1]
        a_spec = pl.BlockSpec((k, tm), lambda i, j: (0, i))
        b_spec = pl.BlockSpec((k, tn), lambda i, j: (0, j))
        dims = TN
    assert m % tm == 0 and n % tn == 0, (m, n, tm, tn)

    def body(a_ref, b_ref, o_ref):
        o_ref[...] = _dot(a_ref[...], b_ref[...], dims).astype(o_ref.dtype)

    (out,), moved = _call(
        body, (a, b), name=name, grid=(m // tm, n // tn),
        in_specs=[a_spec, b_spec], out_specs=[pl.BlockSpec((tm, tn), lambda i, j: (i, j))],
        out_shape=[jax.ShapeDtypeStruct((m, n), out_dtype)], vmem_mb=48, hosted=hosted)
    return (out, moved) if hosted else out


def _shift_down(x, k, fill):
    y = pltpu.roll(x, k, 0)
    row = lax.broadcasted_iota(jnp.int32, (SUB, x.shape[1]), 0)
    head = y[0:SUB, :]
    for t in range(k):
        head = jnp.where(row == t, fill[t], head)
    return jnp.concatenate([head, y[SUB:, :]], axis=0)


def _shift_up(x, k, fill):
    n = x.shape[0]
    y = pltpu.roll(x, n - k, 0)
    row = lax.broadcasted_iota(jnp.int32, (SUB, x.shape[1]), 0)
    tail = y[n - SUB:, :]
    for t in range(k):
        tail = jnp.where(row == SUB - k + t, fill[t], tail)
    return jnp.concatenate([y[:n - SUB, :], tail], axis=0)


def _conv_taps(x, halo, w):
    if halo is None:
        f1, f2 = [0.0], [0.0, 0.0]
    else:
        f1, f2 = [halo[7:8, :]], [halo[6:7, :], halo[7:8, :]]
    s1 = _shift_down(x, 1, f1)
    s2 = _shift_down(x, 2, f2)
    u = w[2:3, :] * x + w[1:2, :] * s1 + w[0:1, :] * s2
    return u, s1, s2


def _conv_taps_t(du, nxt, w):
    if nxt is None:
        f1, f2 = [0.0], [0.0, 0.0]
    else:
        f1, f2 = [nxt[0:1, :]], [nxt[0:1, :], nxt[1:2, :]]
    return w[2:3, :] * du + w[1:2, :] * _shift_up(du, 1, f1) + w[0:1, :] * _shift_up(du, 2, f2)


def _ada_fwd(c_all, w_ada, b_my):
    def body(c_ref, w_ref, b_ref, o_ref):
        cv = c_ref[...]
        act = cv * _sigmoid(cv)
        o_ref[...] = _dot(act, w_ref[...], NN, lax.Precision.HIGHEST) + b_ref[...]

    return pl.pallas_call(
        body, name="ada_fwd",
        out_shape=jax.ShapeDtypeStruct((NDEV, w_ada.shape[1]), F32),
        compiler_params=_params(None, 32),
    )(c_all, w_ada, b_my)


TR = 256


def _row_spec(width, col=0):
    return pl.BlockSpec((TR, width), lambda i, col=col: (i, col))


def _full_spec(shape):
    return pl.BlockSpec(shape, lambda i: (0,) * len(shape))


def _norm_mod_fwd(x, mod, g):
    s = x.shape[0]

    def body(x_ref, mod_ref, g_ref, h_ref):
        xv = x_ref[...]
        r = lax.rsqrt(jnp.mean(xv * xv, axis=-1, keepdims=True) + EPS)
        nrm = xv * r * g_ref[...]
        h_ref[...] = (nrm * (1.0 + mod_ref[1:2, :]) + mod_ref[0:1, :]).astype(BF16)

    return pl.pallas_call(
        body, name="norm1_fwd", grid=(s // TR,),
        in_specs=[_row_spec(D), _full_spec((SUB, D)), _full_spec((1, D))],
        out_specs=_row_spec(D), out_shape=jax.ShapeDtypeStruct((s, D), BF16),
        compiler_params=_params(("parallel",)),
    )(x, mod, g)


SLAB = 2 * DH
AUG_F, AUG_ONE, AUG_LSE = 0, 3, 6


def _split3(x):
    hi = x.astype(BF16).astype(F32)
    r1 = x - hi
    mid = r1.astype(BF16).astype(F32)
    return hi, mid, r1 - mid


def _lanes3(lane, first, pieces, other):
    out = other
    for k in range(3):
        out = jnp.where(lane == first + k, pieces[k], out)
    return out


def _aug_placement():
    eq = np.zeros((3 * LANES, HEADS * SLAB), np.float32)
    ek = np.zeros((3 * LANES, HEADS * SLAB), np.float32)
    ones = np.zeros((SUB, HEADS * SLAB), np.float32)
    for h in range(HEADS):
        aug = SLAB * h + DH
        for k in range(3):
            eq[LANES * k + h, aug + AUG_F + k] = 1.0
            ek[LANES * k + h, aug + AUG_ONE + k] = -1.0
            ones[0, aug + AUG_ONE + k] = 1.0
            ones[1, aug + AUG_F + k] = ones[1, aug + AUG_LSE + k] = 1.0
            ones[2, aug + k] = 1.0
    return jnp.asarray(eq, BF16), jnp.asarray(ek, BF16), jnp.asarray(ones)


def _qkv_prep(proj, fcum, gq, gk, hosted):
    s = proj.shape[0]

    def body(q_ref, k_ref, v_ref, f_ref, gq_ref, gk_ref, eq_ref, ek_ref, ones_ref, qo_ref, ko_ref, vo_ref):
        f3 = jnp.concatenate(_split3(f_ref[...]), axis=1).astype(BF16)
        qo_ref[...] = (_dot(f3, eq_ref[...]) + ones_ref[0:1, :]).astype(BF16)
        ko_ref[...] = (_dot(f3, ek_ref[...]) + ones_ref[1:2, :]).astype(BF16)
        vo_ref[...] = jnp.broadcast_to(ones_ref[2:3, :], vo_ref.shape).astype(BF16)
        for h in range(HEADS):
            sl = slice(DH * h, DH * (h + 1))
            lo = slice(SLAB * h, SLAB * h + DH)
            qh = q_ref[:, sl]
            r = lax.rsqrt(jnp.mean(qh * qh, axis=-1, keepdims=True) + EPS)
            qo_ref[:, lo] = (qh * r * gq_ref[...] * QK_SCALE).astype(BF16)
            kh = k_ref[:, sl]
            r = lax.rsqrt(jnp.mean(kh * kh, axis=-1, keepdims=True) + EPS)
            ko_ref[:, lo] = (kh * r * gk_ref[...]).astype(BF16)
            vo_ref[:, lo] = v_ref[:, sl].astype(BF16)

    eq, ek, ones = _aug_placement()
    o = jax.ShapeDtypeStruct((s, HEADS * SLAB), BF16)
    wide = _row_spec(HEADS * SLAB)
    return _call(
        body, (proj, proj, proj, fcum, gq, gk, eq, ek, ones), name="qkv_prep", grid=(s // TR,),
        in_specs=[_row_spec(AW, 0), _row_spec(AW, 1), _row_spec(AW, 2), _row_spec(LANES),
                  _full_spec((1, DH)), _full_spec((1, DH)), _full_spec(eq.shape), _full_spec(ek.shape),
                  _full_spec(ones.shape)],
        out_specs=[wide, wide, wide], out_shape=[o, o, o], vmem_mb=32, hosted=hosted)


FG_BLOCK = (3 * AW + 3 * CW) // LANES


def _fgate_fwd(proj, bf_pad):
    s = proj.shape[0]

    def body(fg_ref, b_ref, o_ref, carry_ref):
        i = pl.program_id(0)

        @pl.when(i == 0)
        def _():
            carry_ref[...] = jnp.zeros_like(carry_ref)

        z = fg_ref[...] + b_ref[...]
        logf = jnp.minimum(z, 0.0) - jnp.log1p(jnp.exp(-jnp.abs(z)))
        row = lax.broadcasted_iota(jnp.int32, (TR, TR), 0)
        col = lax.broadcasted_iota(jnp.int32, (TR, TR), 1)
        tri = (col <= row).astype(F32)
        cs = _dot(tri, logf, NN, lax.Precision.HIGHEST) + carry_ref[0:1, :]
        o_ref[...] = cs
        carry_ref[...] = jnp.broadcast_to(cs[TR - 1:TR, :], carry_ref.shape)

    return pl.pallas_call(
        body, name="fgate_fwd", grid=(s // TR,),
        in_specs=[_row_spec(LANES, FG_BLOCK), _full_spec((1, LANES))],
        out_specs=_row_spec(LANES), out_shape=jax.ShapeDtypeStruct((s, LANES), F32),
        scratch_shapes=[pltpu.VMEM((SUB, LANES), F32)],
        compiler_params=_params(("arbitrary",)),
    )(proj, bf_pad)


def _fgate_bwd(dfcol, proj, bf_pad):
    s = proj.shape[0]
    nb = s // TR

    def body(df_ref, fg_ref, b_ref, o_ref, db_ref, carry_ref):
        i = pl.program_id(0)

        @pl.when(i == 0)
        def _():
            carry_ref[...] = jnp.zeros_like(carry_ref)
            db_ref[...] = jnp.zeros_like(db_ref)

        row = lax.broadcasted_iota(jnp.int32, (TR, TR), 0)
        col = lax.broadcasted_iota(jnp.int32, (TR, TR), 1)
        tri = (col >= row).astype(F32)
        dlogf = _dot(tri, df_ref[...], NN, lax.Precision.HIGHEST) + carry_ref[0:1, :]
        carry_ref[...] = jnp.broadcast_to(dlogf[0:1, :], carry_ref.shape)
        z = fg_ref[...] + b_ref[...]
        dfg = dlogf * _sigmoid(-z)
        o_ref[...] = dfg.astype(BF16)
        db_ref[0:1, :] += jnp.sum(dfg, axis=0, keepdims=True)

    rev = lambda col: pl.BlockSpec((TR, LANES), lambda i, col=col: (nb - 1 - i, col))
    return pl.pallas_call(
        body, name="fgate_bwd", grid=(nb,),
        in_specs=[rev(0), rev(FG_BLOCK), _full_spec((1, LANES))],
        out_specs=(rev(0), _full_spec((SUB, LANES))),
        out_shape=(jax.ShapeDtypeStruct((s, LANES), BF16), jax.ShapeDtypeStruct((SUB, LANES), F32)),
        scratch_shapes=[pltpu.VMEM((SUB, LANES), F32)],
        compiler_params=_params(("arbitrary",)),
    )(dfcol, proj, bf_pad)


def _resid_norm2(x, z, mod, g):
    s = x.shape[0]

    def body(x_ref, z_ref, mod_ref, g_ref, x1_ref, h_ref):
        x1 = x_ref[...] + mod_ref[2:3, :] * z_ref[...]
        x1_ref[...] = x1
        r = lax.rsqrt(jnp.mean(x1 * x1, axis=-1, keepdims=True) + EPS)
        nrm = x1 * r * g_ref[...]
        h_ref[...] = (nrm * (1.0 + mod_ref[4:5, :]) + mod_ref[3:4, :]).astype(BF16)

    return pl.pallas_call(
        body, name="resid_norm2", grid=(s // TR,),
        in_specs=[_row_spec(D), _row_spec(D), _full_spec((SUB, D)), _full_spec((1, D))],
        out_specs=(_row_spec(D), _row_spec(D)),
        out_shape=(jax.ShapeDtypeStruct((s, D), F32), jax.ShapeDtypeStruct((s, D), BF16)),
        compiler_params=_params(("parallel",)),
    )(x, z, mod, g)


def _loss_head(x1, y, tgt, mod):
    s = x1.shape[0]

    def body(x1_ref, y_ref, t_ref, mod_ref, dout_ref, dy_ref, vec_ref):
        @pl.when(pl.program_id(0) == 0)
        def _():
            vec_ref[...] = jnp.zeros_like(vec_ref)

        yv = y_ref[...]
        g2 = mod_ref[5:6, :]
        diff = x1_ref[...] + g2 * yv - t_ref[...]
        dout = diff * (1.0 / D)
        dout_ref[...] = dout
        dy_ref[...] = (g2 * dout).astype(BF16)
        vec_ref[0:1, :] += jnp.sum(dout * yv, axis=0, keepdims=True)
        vec_ref[1:2, :] += jnp.sum(diff * diff, axis=0, keepdims=True)

    return pl.pallas_call(
        body, name="loss_head", grid=(s // TR,),
        in_specs=[_row_spec(D), _row_spec(D), _row_spec(D), _full_spec((SUB, D))],
        out_specs=(_row_spec(D), _row_spec(D), _full_spec((SUB, D))),
        out_shape=(jax.ShapeDtypeStruct((s, D), F32), jax.ShapeDtypeStruct((s, D), BF16),
                   jax.ShapeDtypeStruct((SUB, D), F32)),
        compiler_params=_params(("arbitrary",)),
    )(x1, y, tgt, mod)


def _norm_mod_bwd(dh, xin, dres, zin, mod, g, shift_row, scale_row, gate_row, name, hosted=None):
    s = dh.shape[0]
    with_gate = gate_row is not None

    def body(*refs):
        if with_gate:
            dh_ref, x_ref, dres_ref, z_ref, mod_ref, g_ref, dx_ref, dz_ref, vec_ref = refs
        else:
            dh_ref, x_ref, dres_ref, mod_ref, g_ref, dx_ref, vec_ref = refs

        @pl.when(pl.program_id(0) == 0)
        def _():
            vec_ref[...] = jnp.zeros_like(vec_ref)

        xv = x_ref[...]
        dhv = dh_ref[...]
        gv = g_ref[...]
        r = lax.rsqrt(jnp.mean(xv * xv, axis=-1, keepdims=True) + EPS)
        xh = xv * r
        dn = dhv * (1.0 + mod_ref[scale_row:scale_row + 1, :])
        dxh = dn * gv
        dx = dres_ref[...] + r * (dxh - xh * jnp.mean(dxh * xh, axis=-1, keepdims=True))
        dx_ref[...] = dx
        vec_ref[0:1, :] += jnp.sum(dhv, axis=0, keepdims=True)
        vec_ref[1:2, :] += jnp.sum(dhv * (xh * gv), axis=0, keepdims=True)
        vec_ref[2:3, :] += jnp.sum(dn * xh, axis=0, keepdims=True)
        if with_gate:
            dz_ref[...] = (mod_ref[gate_row:gate_row + 1, :] * dx).astype(BF16)
            vec_ref[3:4, :] += jnp.sum(dx * z_ref[...], axis=0, keepdims=True)

    ins = [dh, xin, dres] + ([zin] if with_gate else []) + [mod, g]
    in_specs = [_row_spec(D)] * (4 if with_gate else 3) + [_full_spec((SUB, D)), _full_spec((1, D))]
    out_specs = [_row_spec(D)] + ([_row_spec(D)] if with_gate else []) + [_full_spec((SUB, D))]
    out_shape = [jax.ShapeDtypeStruct((s, D), F32)] + ([jax.ShapeDtypeStruct((s, D), BF16)] if with_gate else []) \
        + [jax.ShapeDtypeStruct((SUB, D), F32)]
    outs, moved = _call(body, ins, name=name, grid=(s // TR,), in_specs=in_specs, out_specs=out_specs,
                        out_shape=out_shape, hosted=hosted)
    return outs + (moved,) if hosted else outs


XIN_BLOCK = 3 * AW // LANES
BG_BLOCK = XIN_BLOCK + CW // LANES
CG_BLOCK = BG_BLOCK + CW // LANES


def _seq_spec(s, first_block):
    return pl.BlockSpec((s, LANES), lambda j, fb=first_block: (0, fb + j))


def _mixconv_fwd(proj, w):
    s = proj.shape[0]

    def body(xin_ref, bg_ref, cg_ref, w_ref, o_ref):
        cx = cg_ref[...] * xin_ref[...]
        cv, _, _ = _conv_taps(cx, None, w_ref[...])
        o_ref[...] = bg_ref[...] * cv

    return pl.pallas_call(
        body, name="mixconv_fwd", grid=(CW // LANES,),
        in_specs=[_seq_spec(s, XIN_BLOCK), _seq_spec(s, BG_BLOCK), _seq_spec(s, CG_BLOCK),
                  pl.BlockSpec((3, LANES), lambda j: (0, j))],
        out_specs=_seq_spec(s, 0), out_shape=jax.ShapeDtypeStruct((s, CW), F32),
        compiler_params=_params(("parallel",), 48),
    )(proj, proj, proj, w)


def _mixconv_bwd(dmixed, proj, w):
    s = proj.shape[0]

    def body(d_ref, xin_ref, bg_ref, cg_ref, w_ref, dxin_ref, dbg_ref, dcg_ref, dw_ref):
        wv = w_ref[...]
        xin, cg, dconv = xin_ref[...], cg_ref[...], d_ref[...]
        cx = cg * xin
        cv, s1, s2 = _conv_taps(cx, None, wv)
        dbg_ref[...] = (dconv * cv).astype(BF16)
        dcv = dconv * bg_ref[...]
        dw_ref[...] = jnp.zeros_like(dw_ref)
        dw_ref[0:1, :] = jnp.sum(dcv * s2, axis=0, keepdims=True)
        dw_ref[1:2, :] = jnp.sum(dcv * s1, axis=0, keepdims=True)
        dw_ref[2:3, :] = jnp.sum(dcv * cx, axis=0, keepdims=True)
        dcx = _conv_taps_t(dcv, None, wv)
        dcg_ref[...] = (dcx * xin).astype(BF16)
        dxin_ref[...] = (dcx * cg).astype(BF16)

    o = jax.ShapeDtypeStruct((s, CW), BF16)
    return pl.pallas_call(
        body, name="mixconv_bwd", grid=(CW // LANES,),
        in_specs=[_seq_spec(s, AW // LANES), _seq_spec(s, XIN_BLOCK), _seq_spec(s, BG_BLOCK), _seq_spec(s, CG_BLOCK),
                  pl.BlockSpec((3, LANES), lambda j: (0, j))],
        out_specs=(_seq_spec(s, 0), _seq_spec(s, 0), _seq_spec(s, 0), pl.BlockSpec((SUB, LANES), lambda j: (0, j))),
        out_shape=(o, o, o, jax.ShapeDtypeStruct((SUB, CW), F32)),
        compiler_params=_params(("parallel",), 48),
    )(dmixed, proj, proj, proj, w)


TA = 512
NEG = -1e30


def _causal_mask():
    row = lax.broadcasted_iota(jnp.int32, (TA, TA), 0)
    col = lax.broadcasted_iota(jnp.int32, (TA, TA), 1)
    return col <= row


def _attn_fwd(qp, kp, vp, hosted):
    s = qp.shape[0]
    nq = s // TA

    def body(q_ref, k_ref, v_ref, o_ref, lse_ref):
        i = pl.program_id(1)
        slabs = [slice(SLAB * hh, SLAB * (hh + 1)) for hh in range(2)]
        q = [q_ref[:, sl] for sl in slabs]

        def block(j, carry, masked):
            keys = pl.ds(pl.multiple_of(j * TA, TA), TA)
            ms, acc = carry
            m_out, parts = [], []
            for hh in range(2):
                sc = _dot(q[hh], k_ref[keys, slabs[hh]], NT)
                if masked:
                    sc = jnp.where(_causal_mask(), sc, NEG)
                m_new = jnp.maximum(ms[hh], jnp.max(sc, axis=-1, keepdims=True))
                p = jnp.exp(sc - m_new)
                parts.append(jnp.exp(ms[hh] - m_new) * acc[:, slabs[hh]] + _dot(p.astype(BF16), v_ref[keys, slabs[hh]]))
                m_out.append(m_new)
            return tuple(m_out), jnp.concatenate(parts, axis=1)

        init = ((jnp.full((TA, 1), NEG, F32), jnp.full((TA, 1), NEG, F32)), jnp.zeros((TA, 2 * SLAB), F32))
        carry = lax.fori_loop(0, i, lambda j, cr: block(j, cr, False), init)
        ms, acc = block(i, carry, True)
        for hh in range(2):
            l = acc[:, SLAB * hh + DH:SLAB * hh + DH + 1]
            o_ref[:, DH * hh:DH * (hh + 1)] = acc[:, SLAB * hh:SLAB * hh + DH] / l
            lse_ref[0, :, hh:hh + 1] = ms[hh] + jnp.log(l)

    (o, lse), moved = _call(
        body, (qp, kp, vp), name="attn_fwd", grid=(HEADS // 2, nq),
        in_specs=[pl.BlockSpec((TA, 2 * SLAB), lambda p, i: (i, p)),
                  pl.BlockSpec((s, 2 * SLAB), lambda p, i: (0, p)),
                  pl.BlockSpec((s, 2 * SLAB), lambda p, i: (0, p))],
        out_specs=[pl.BlockSpec((TA, LANES), lambda p, i: (i, p)), pl.BlockSpec((1, TA, 2), lambda p, i: (p, i, 0))],
        out_shape=[jax.ShapeDtypeStruct((s, AW), F32), jax.ShapeDtypeStruct((HEADS // 2, s, 2), F32)],
        vmem_mb=32, hosted=hosted)
    return o, lse, moved


def _attn_bwd(qp, kp, vp, dmixed, o, lse, hosted):
    s = qp.shape[0]
    nq = s // TA

    def body(q_ref, k_ref, v_ref, do_ref, o_ref, lse_ref, dq_ref, dk_ref, dv_ref, qb_ref, dob_ref):
        dk_ref[...] = jnp.zeros_like(dk_ref)
        dv_ref[...] = jnp.zeros_like(dv_ref)
        slabs = [slice(SLAB * hh, SLAB * (hh + 1)) for hh in range(2)]
        lane = lax.broadcasted_iota(jnp.int32, (TA, DH), 1)

        def q_block(i, _):
            i0 = pl.multiple_of(i * TA, TA)
            rows = pl.ds(i0, TA)
            for hh in range(2):
                half = slice(DH * hh, DH * (hh + 1))
                do = do_ref[rows, half]
                delta = jnp.sum(do * o_ref[rows, half], axis=-1, keepdims=True)
                dob_ref[hh, :, 0:DH] = do.astype(BF16)
                dob_ref[hh, :, DH:SLAB] = _lanes3(lane, 0, [-d for d in _split3(delta)], 0.0).astype(BF16)
                lse3 = _split3(lse_ref[0, rows, hh:hh + 1])
                qb_ref[hh, :, 0:DH] = q_ref[rows, SLAB * hh:SLAB * hh + DH]
                aug = q_ref[rows, SLAB * hh + DH:SLAB * (hh + 1)].astype(F32)
                qb_ref[hh, :, DH:SLAB] = _lanes3(lane, AUG_LSE, [-x for x in lse3], aug).astype(BF16)

            def block(j, dq, masked):
                keys = pl.ds(pl.multiple_of(j * TA, TA), TA)
                dv, dk, dqc = [], [], []
                for hh in range(2):
                    q, dob = qb_ref[hh], dob_ref[hh]
                    k = k_ref[keys, slabs[hh]]
                    sc = _dot(q, k, NT)
                    if masked:
                        sc = jnp.where(_causal_mask(), sc, NEG)
                    p = jnp.exp(sc)
                    dv.append(_dot(p.astype(BF16), dob, TN))
                    ds = (p * _dot(dob, v_ref[keys, slabs[hh]], NT)).astype(BF16)
                    dk.append(_dot(ds, q, TN))
                    dqc.append(_dot(ds, k))
                dv_ref[keys, :] += jnp.concatenate(dv, axis=1)
                dk_ref[keys, :] += jnp.concatenate(dk, axis=1)
                return dq + jnp.concatenate(dqc, axis=1)

            dq = lax.fori_loop(0, i, lambda j, acc: block(j, acc, False), jnp.zeros((TA, 2 * SLAB), F32))
            dq_ref[rows, :] = block(i, dq, True)
            return 0

        lax.fori_loop(0, nq, q_block, 0)

    pair = lambda p: (0, p)
    slab2 = pl.BlockSpec((s, 2 * SLAB), pair)
    seq = pl.BlockSpec((s, LANES), pair)
    small = pl.BlockSpec((1, s, 2), lambda p: (p, 0, 0))
    o32 = jax.ShapeDtypeStruct((s, HEADS * SLAB), F32)
    return _call(
        body, (qp, kp, vp, dmixed, o, lse), name="attn_bwd", grid=(HEADS // 2,),
        in_specs=[slab2, slab2, slab2, seq, seq, small], out_specs=[slab2, slab2, slab2], out_shape=[o32, o32, o32],
        scratch_shapes=[pltpu.VMEM((2, TA, SLAB), BF16), pltpu.VMEM((2, TA, SLAB), BF16)], vmem_mb=48, hosted=hosted)


def _qkv_post(dqp, dkp, dvp, proj, gq, gk, hosted):
    s = proj.shape[0]

    def body(dq_ref, dk_ref, dv_ref, q_ref, k_ref, gq_ref, gk_ref, dqo_ref, dko_ref, dvo_ref, df_ref, vec_ref):
        @pl.when(pl.program_id(0) == 0)
        def _():
            vec_ref[...] = jnp.zeros_like(vec_ref)

        def one(d_ref, x_ref, g_ref, o_ref, row, scale):
            dg = jnp.zeros((1, DH), F32)
            for h in range(HEADS):
                sl = slice(DH * h, DH * (h + 1))
                xv = x_ref[:, sl]
                r = lax.rsqrt(jnp.mean(xv * xv, axis=-1, keepdims=True) + EPS)
                xh = xv * r
                dn = d_ref[:, SLAB * h:SLAB * h + DH] * scale
                dg = dg + jnp.sum(dn * xh, axis=0, keepdims=True)
                dxh = dn * g_ref[...]
                o_ref[:, sl] = (r * (dxh - xh * jnp.mean(dxh * xh, axis=-1, keepdims=True))).astype(BF16)
            vec_ref[row:row + 1, 0:DH] += dg

        one(dq_ref, q_ref, gq_ref, dqo_ref, 0, QK_SCALE)
        one(dk_ref, k_ref, gk_ref, dko_ref, 1, 1.0)
        lane = lax.broadcasted_iota(jnp.int32, (TR, LANES), 1)
        df = jnp.zeros((TR, LANES), F32)
        for h in range(HEADS):
            dvo_ref[:, DH * h:DH * (h + 1)] = dv_ref[:, SLAB * h:SLAB * h + DH].astype(BF16)
            row_sum = dq_ref[:, SLAB * h + DH:SLAB * h + DH + 1]
            col_sum = dk_ref[:, SLAB * h + DH + AUG_ONE:SLAB * h + DH + AUG_ONE + 1]
            df = jnp.where(lane == h, row_sum - col_sum, df)
        df_ref[...] = df

    o = jax.ShapeDtypeStruct((s, AW), BF16)
    wide = _row_spec(HEADS * SLAB)
    return _call(
        body, (dqp, dkp, dvp, proj, proj, gq, gk), name="qkv_post", grid=(s // TR,),
        in_specs=[wide, wide, wide, _row_spec(AW, 0), _row_spec(AW, 1), _full_spec((1, DH)), _full_spec((1, DH))],
        out_specs=[_row_spec(AW), _row_spec(AW), _row_spec(AW), _row_spec(LANES), _full_spec((SUB, LANES))],
        out_shape=[o, o, o, jax.ShapeDtypeStruct((s, LANES), F32), jax.ShapeDtypeStruct((SUB, LANES), F32)],
        hosted=hosted)


TF = 256
NJ = DFF // TF
FFN_ROWS_FWD = 1024
FFN_ROWS_BWD = 1024


def _ffn_fwd(h2, wup_t, cw, wd):
    s = h2.shape[0]
    tr = FFN_ROWS_FWD
    nr = s // tr

    def body(h_ref, wu_ref, cg_ref, cv_ref, wd_ref, pg_ref, pv_ref, y_ref, halo_ref):
        r, j = pl.program_id(0), pl.program_id(1)
        hv = h_ref[...]
        pg = _dot(hv, wu_ref[0], NT).astype(BF16)
        pv = _dot(hv, wu_ref[1], NT).astype(BF16)
        pg_ref[...] = pg
        pv_ref[...] = pv
        pgf, pvf = pg.astype(F32), pv.astype(F32)
        ug, _, _ = _conv_taps(pgf, jnp.where(r > 0, halo_ref[j, 0], 0.0), cg_ref[...])
        uv, _, _ = _conv_taps(pvf, jnp.where(r > 0, halo_ref[j, 1], 0.0), cv_ref[...])
        halo_ref[j, 0] = pgf[tr - SUB:tr, :]
        halo_ref[j, 1] = pvf[tr - SUB:tr, :]
        act = (ug * _sigmoid(ug) * uv).astype(BF16)
        contrib = _dot(act, wd_ref[...])

        @pl.when(j == 0)
        def _():
            y_ref[...] = contrib

        @pl.when(j > 0)
        def _():
            y_ref[...] += contrib

    pre = jax.ShapeDtypeStruct((s, DFF), BF16)
    return pl.pallas_call(
        body, name="ffn_fwd", grid=(nr, NJ),
        in_specs=[pl.BlockSpec((tr, D), lambda r, j: (r, 0)),
                  pl.BlockSpec((2, TF, D), lambda r, j: (0, j, 0)),
                  pl.BlockSpec((3, TF), lambda r, j: (0, j)),
                  pl.BlockSpec((3, TF), lambda r, j: (0, NJ + j)),
                  pl.BlockSpec((TF, D), lambda r, j: (j, 0))],
        out_specs=(pl.BlockSpec((tr, TF), lambda r, j: (r, j)),
                   pl.BlockSpec((tr, TF), lambda r, j: (r, j)),
                   pl.BlockSpec((tr, D), lambda r, j: (r, 0))),
        out_shape=(pre, pre, jax.ShapeDtypeStruct((s, D), F32)),
        scratch_shapes=[pltpu.VMEM((NJ, 2, SUB, TF), F32)],
        compiler_params=_params(("arbitrary", "arbitrary"), 56),
    )(h2, wup_t, cw, cw, wd)


def _ffn_bwd(dy, h2, pre_g, pre_v, wup_t, cw, wd):
    s = h2.shape[0]
    tr = FFN_ROWS_BWD
    nr = s // tr
    hb = tr // (2 * SUB)

    def body(dy_ref, h_ref, pg_ref, pv_ref, hg_ref, hv_ref, wu_ref, cg_ref, cv_ref, wd_ref,
             dh_ref, dwu_ref, dwd_ref, dcg_ref, dcv_ref, nxt_ref, awu_ref, awd_ref):
        j, r = pl.program_id(0), pl.program_id(1)
        rr = nr - 1 - r
        row0 = pl.multiple_of(rr * tr, tr)
        cwg, cwv = cg_ref[...], cv_ref[...]
        pg, pv = pg_ref[...].astype(F32), pv_ref[...].astype(F32)
        ug, g1, g2 = _conv_taps(pg, jnp.where(rr > 0, hg_ref[SUB:2 * SUB, :].astype(F32), 0.0), cwg)
        uv, v1, v2 = _conv_taps(pv, jnp.where(rr > 0, hv_ref[SUB:2 * SUB, :].astype(F32), 0.0), cwv)
        sg = _sigmoid(ug)
        sil = ug * sg
        act = (sil * uv).astype(BF16)
        dyv = dy_ref[...]
        da = _dot(dyv, wd_ref[...], NT)
        dug = da * uv * (sg * (1.0 + ug * (1.0 - sg)))
        duv = da * sil
        dpg = _conv_taps_t(dug, jnp.where(r > 0, nxt_ref[0], 0.0), cwg)
        dpv = _conv_taps_t(duv, jnp.where(r > 0, nxt_ref[1], 0.0), cwv)
        nxt_ref[0] = dug[0:SUB, :]
        nxt_ref[1] = duv[0:SUB, :]
        dpgb, dpvb = dpg.astype(BF16), dpv.astype(BF16)
        hv = h_ref[...]
        dwd = _dot(act, dyv, TN)
        dpb = jnp.concatenate([dpgb, dpvb], axis=1)
        dwu = _dot(dpb, hv, TN)
        dh = _dot(dpb, wu_ref[...].reshape(2 * TF, D))

        def taps(du, x0, x1, x2):
            return (jnp.sum(du * x2, axis=0, keepdims=True), jnp.sum(du * x1, axis=0, keepdims=True),
                    jnp.sum(du * x0, axis=0, keepdims=True))

        tg, tv = taps(dug, pg, g1, g2), taps(duv, pv, v1, v2)

        @pl.when(r == 0)
        def _():
            awd_ref[...] = dwd
            awu_ref[...] = dwu
            dcg_ref[...] = jnp.zeros_like(dcg_ref)
            dcv_ref[...] = jnp.zeros_like(dcv_ref)

        @pl.when(r > 0)
        def _():
            awd_ref[...] += dwd
            awu_ref[...] += dwu

        @pl.when(r == nr - 1)
        def _():
            dwd_ref[...] = awd_ref[...].astype(BF16)
            dwu_ref[...] = awu_ref[...].astype(BF16).reshape(2, TF, D)

        for t in range(3):
            dcg_ref[t:t + 1, :] += tg[t]
            dcv_ref[t:t + 1, :] += tv[t]

        @pl.when(j == 0)
        def _():
            dh_ref[pl.ds(row0, tr), :] = dh

        @pl.when(j > 0)
        def _():
            dh_ref[pl.ds(row0, tr), :] += dh

    rows = lambda j, r: (nr - 1 - r, 0)
    tile = lambda j, r: (nr - 1 - r, j)
    halo = lambda j, r: (jnp.maximum((nr - 1 - r) * hb - 1, 0), j)
    return pl.pallas_call(
        body, name="ffn_bwd", grid=(NJ, nr),
        in_specs=[pl.BlockSpec((tr, D), rows), pl.BlockSpec((tr, D), rows),
                  pl.BlockSpec((tr, TF), tile), pl.BlockSpec((tr, TF), tile),
                  pl.BlockSpec((2 * SUB, TF), halo), pl.BlockSpec((2 * SUB, TF), halo),
                  pl.BlockSpec((2, TF, D), lambda j, r: (0, j, 0)),
                  pl.BlockSpec((3, TF), lambda j, r: (0, j)), pl.BlockSpec((3, TF), lambda j, r: (0, NJ + j)),
                  pl.BlockSpec((TF, D), lambda j, r: (j, 0))],
        out_specs=(pl.BlockSpec((s, D), lambda j, r: (0, 0)),
                   pl.BlockSpec((2, TF, D), lambda j, r: (0, j, 0)),
                   pl.BlockSpec((TF, D), lambda j, r: (j, 0)),
                   pl.BlockSpec((SUB, TF), lambda j, r: (0, j)), pl.BlockSpec((SUB, TF), lambda j, r: (0, j))),
        out_shape=(jax.ShapeDtypeStruct((s, D), F32),
                   jax.ShapeDtypeStruct((2, DFF, D), BF16), jax.ShapeDtypeStruct((DFF, D), BF16),
                   jax.ShapeDtypeStruct((SUB, DFF), F32), jax.ShapeDtypeStruct((SUB, DFF), F32)),
        scratch_shapes=[pltpu.VMEM((2, SUB, TF), F32), pltpu.VMEM((2 * TF, D), F32), pltpu.VMEM((TF, D), F32)],
        compiler_params=_params(("arbitrary", "arbitrary"), 56),
    )(dy, h2, pre_g, pre_v, pre_g, pre_v, wup_t, cw, cw, wd)


def _adam(w, g, m, v):
    m = ADAM_B1 * m + (1.0 - ADAM_B1) * g
    v = ADAM_B2 * v + (1.0 - ADAM_B2) * (g * g)
    m_hat = m / (1.0 - ADAM_B1 ** ADAM_STEP)
    v_hat = v / (1.0 - ADAM_B2 ** ADAM_STEP)
    delta = -ADAM_LR * (m_hat / (jnp.sqrt(v_hat) + ADAM_EPS) + ADAM_WD * w)
    return delta, m, v


NCHIP = NDEV // 2


def _pair_add(mine, theirs, tr, name):
    _, _, rws, cols = mine.shape

    def body(a_ref, b_ref, o_ref):
        c = lax.axis_index("c")
        o_ref[0] = (a_ref[0, c].astype(F32) + b_ref[0].astype(F32)).astype(BF16)

    (out,), _ = _call(
        body, (mine, theirs), name=name, grid=(NCHIP, rws // tr),
        in_specs=[pl.BlockSpec((1, 2, tr, cols), lambda q, i: (q, 0, i, 0)),
                  pl.BlockSpec((1, tr, cols), lambda q, i: (q, i, 0))],
        out_specs=[pl.BlockSpec((1, tr, cols), lambda q, i: (q, i, 0))],
        out_shape=[jax.ShapeDtypeStruct((NCHIP, rws, cols), BF16)], vmem_mb=32)
    return out


def _adamw_sharded(parts, w, m, v, tr, name, hosted=None):
    rws, cols = w.shape

    def body(p_ref, w_ref, m_ref, v_ref, g_ref, d_ref, mo_ref, vo_ref):
        g = p_ref[0].astype(F32)
        for q in range(1, NCHIP):
            g = g + p_ref[q].astype(F32)
        g_ref[...] = g
        d_ref[...], mo_ref[...], vo_ref[...] = _adam(w_ref[...], g, m_ref[...], v_ref[...])

    blk = pl.BlockSpec((tr, cols), lambda i: (i, 0))
    o = jax.ShapeDtypeStruct((rws, cols), F32)
    outs, moved = _call(
        body, (parts, w, m, v), name=name, grid=(rws // tr,),
        in_specs=[pl.BlockSpec((NCHIP, tr, cols), lambda i: (0, i, 0)), blk, blk, blk],
        out_specs=[blk, blk, blk, blk], out_shape=[o, o, o, o], vmem_mb=48, hosted=hosted)
    return (outs, moved) if hosted else outs


def _adamw_ada(c_all, dmod_my, w, m, v):
    rws, cols = w.shape
    tr = 256

    def body(c_ref, dm_ref, w_ref, m_ref, v_ref, g_ref, d_ref, mo_ref, vo_ref):
        cv = c_ref[...]
        act = cv * _sigmoid(cv)
        g = _dot(act, dm_ref[...], TN, lax.Precision.HIGHEST)
        g_ref[...] = g
        d_ref[...], mo_ref[...], vo_ref[...] = _adam(w_ref[...], g, m_ref[...], v_ref[...])

    blk = pl.BlockSpec((tr, cols), lambda i: (i, 0))
    o = jax.ShapeDtypeStruct((rws, cols), F32)
    return pl.pallas_call(
        body, name="adamw_ada", grid=(rws // tr,),
        in_specs=[pl.BlockSpec((NDEV, tr), lambda i: (0, i)), _full_spec((NDEV, cols)), blk, blk, blk],
        out_specs=(blk, blk, blk, blk), out_shape=(o, o, o, o),
        compiler_params=_params(("parallel",), 48),
    )(c_all, dmod_my, w, m, v)


REP_ROWS = 16
ROW_N1, ROW_N2, ROW_LOSS, ROW_MISC = 6, 7, 8, 9
LANE_BF, LANE_GQ, LANE_GK = 0, 128, 256


def _adamw_small(rep_all, conv_all, wmv):
    n_ff = wmv[6][0].shape[1]

    def body(*refs):
        rep_ref, conv_ref = refs[:2]
        ins = refs[2:2 + 24]
        outs = refs[2 + 24:]
        loss_ref, outs = outs[0], outs[1:]
        g_rep = rep_ref[0]
        g_conv = conv_ref[0]
        for d in range(1, NDEV):
            g_rep = g_rep + rep_ref[d]
            g_conv = g_conv + conv_ref[d]
        loss_ref[...] = (0.5 / D) * jnp.sum(g_rep[ROW_LOSS:ROW_LOSS + 1, :], axis=-1, keepdims=True)
        grads = [
            None,
            g_rep[ROW_N1:ROW_N1 + 1, :],
            g_rep[ROW_MISC:ROW_MISC + 1, LANE_BF:LANE_BF + HEADS],
            g_rep[ROW_MISC:ROW_MISC + 1, LANE_GQ:LANE_GQ + DH],
            g_rep[ROW_MISC:ROW_MISC + 1, LANE_GK:LANE_GK + DH],
            g_rep[ROW_N2:ROW_N2 + 1, :],
            g_conv[0:3, 0:n_ff],
            g_conv[0:3, n_ff:n_ff + DH],
        ]
        for p in range(8):
            w_ref, m_ref, v_ref = ins[3 * p:3 * p + 3]
            g_ref, d_ref, mo_ref, vo_ref = outs[4 * p:4 * p + 4]
            if p == 0:
                for nmod in range(NMOD):
                    sl = slice(D * nmod, D * (nmod + 1))
                    g = g_rep[nmod:nmod + 1, :]
                    g_ref[:, sl] = g
                    d_ref[:, sl], mo_ref[:, sl], vo_ref[:, sl] = _adam(w_ref[:, sl], g, m_ref[:, sl], v_ref[:, sl])
            else:
                g = grads[p]
                g_ref[...] = g
                d_ref[...], mo_ref[...], vo_ref[...] = _adam(w_ref[...], g, m_ref[...], v_ref[...])

    flat = [a for trio in wmv for a in trio]
    out_shape = [jax.ShapeDtypeStruct((1, 1), F32)]
    for trio in wmv:
        out_shape += [jax.ShapeDtypeStruct(trio[0].shape, F32)] * 4
    return pl.pallas_call(
        body, name="adamw_small", out_shape=tuple(out_shape),
        compiler_params=_params(None, 32),
    )(rep_all, conv_all, *flat)


FG_FIRST = 3 * AW
N_IN = DIN // NDEV


def _w_in_runs():
    runs = []
    for d in range(NDEV):
        lo, hi = N_IN * d, N_IN * (d + 1)
        for a, b, shift in ((0, FG_FIRST, 0), (FG_FIRST, FG_FIRST + HEADS, DIN - HEADS - FG_FIRST),
                            (FG_FIRST + HEADS, DIN, -HEADS)):
            a, b = max(a, lo), min(b, hi)
            if a < b:
                runs.append((d, a - lo, a + shift, b - a))
    return runs


W_IN_ROWS = 256


def _assemble_w_in(g_in):
    def body(g_ref, o_ref):
        for d, src, dst, width in _w_in_runs():
            o_ref[:, dst:dst + width] = g_ref[d, :, src:src + width]
        o_ref[:, DIN:DINP] = jnp.zeros((W_IN_ROWS, DINP - DIN), o_ref.dtype)

    (out,), _ = _call(
        body, (g_in,), name="assemble_w_in", grid=(D // W_IN_ROWS,),
        in_specs=[pl.BlockSpec((NDEV, W_IN_ROWS, N_IN), lambda i: (0, i, 0))],
        out_specs=[pl.BlockSpec((W_IN_ROWS, DINP), lambda i: (i, 0))],
        out_shape=[jax.ShapeDtypeStruct((D, DINP), g_in.dtype)], vmem_mb=32)
    return out


def _scatter_dw_in(dwp):
    def body(w_ref, o_ref):
        for d, src, dst, width in _w_in_runs():
            o_ref[d // 2, d % 2, :, src:src + width] = w_ref[:, dst:dst + width]

    (out,), _ = _call(
        body, (dwp,), name="scatter_dw_in", grid=(D // W_IN_ROWS,),
        in_specs=[pl.BlockSpec((W_IN_ROWS, DINP), lambda i: (i, 0))],
        out_specs=[pl.BlockSpec((NCHIP, 2, W_IN_ROWS, N_IN), lambda i: (0, 0, i, 0))],
        out_shape=[jax.ShapeDtypeStruct((NCHIP, 2, D, N_IN), dwp.dtype)], vmem_mb=32)
    return out


def _by_core_chip(a):
    return jnp.transpose(a.reshape((NCHIP, 2) + a.shape[1:]), (1, 0, 2, 3))


def kernel(x, c, w_ada, b_ada, norm1_g, w_in, b_forget, q_norm_g, k_norm_g, conv_mix_w, w_out, norm2_g, w_up, ffn_conv_w, w_down, loss_target, m_w_ada, m_b_ada, m_norm1_g, m_w_in, m_b_forget, m_q_norm_g, m_k_norm_g, m_conv_mix_w, m_w_out, m_norm2_g, m_w_up, m_ffn_conv_w, m_w_down, v_w_ada, v_b_ada, v_norm1_g, v_w_in, v_b_forget, v_q_norm_g, v_k_norm_g, v_conv_mix_w, v_w_out, v_norm2_g, v_w_up, v_ffn_conv_w, v_w_down):
    me = 4 * lax.axis_index("x") + 2 * lax.axis_index("y") + lax.axis_index("c")
    xs, tgt = x[0], loss_target[0]
    s = xs.shape[0]
    nq = s // TA
    n_ada = w_ada.shape[2]
    n_ff = w_up.shape[2]

    conv_w = jnp.concatenate([ffn_conv_w[0], conv_mix_w[0]], axis=1)
    conv_w = jnp.concatenate([conv_w, jnp.zeros((SUB - 3, conv_w.shape[1]), F32)], axis=0)
    c_all, conv_all, g_in = _exchange(
        [(c.reshape(SUB, D // SUB), "ag"), (conv_w, "ag"), (w_in[0].astype(BF16), "ag2")], "exchange_w_in")
    c_all = c_all.reshape(NDEV, D)
    cw_ffn = jnp.transpose(conv_all[:, :3, :n_ff], (1, 0, 2)).reshape(3, 2 * DFF)
    cw_mix = jnp.transpose(conv_all[:, :3, n_ff:], (1, 0, 2)).reshape(3, CW)
    w_in_p = _assemble_w_in(g_in)

    b_my = lax.dynamic_slice(b_ada, (0, me * n_ada), (1, n_ada))
    mod_part = _ada_fwd(c_all, w_ada[0], b_my)
    (mod_rows,) = _exchange([(jnp.broadcast_to(mod_part[:, None, :], (NDEV, SUB, n_ada)), "a2a")], "exchange_mod")
    mod = mod_rows[:, 0, :].reshape(NMOD, D)
    mod = jnp.concatenate([mod, jnp.zeros((SUB - NMOD, D), F32)], axis=0)

    h = _norm_mod_fwd(xs, mod, norm1_g)
    proj, (g_down,) = _mm(h, w_in_p, "nn", F32, 512, 640, "proj_fwd", hosted=[(w_down[0].astype(BF16), "ag2")])
    bf_pad = jnp.concatenate([b_forget, jnp.zeros((1, LANES - HEADS), F32)], axis=1)
    fcum = _fgate_fwd(proj, bf_pad)
    (qp, kp, vp), (g_out,) = _qkv_prep(proj, fcum, q_norm_g, k_norm_g, [(w_out[0].astype(BF16), "ag2")])
    attn, lse, (g_up,) = _attn_fwd(qp, kp, vp, [(jnp.transpose(w_up[0]).astype(BF16), "ag2")])
    w_out_f = g_out.reshape(D, D)
    w_up_t = g_up.reshape(2, DFF, D)
    w_down_f = g_down.reshape(DFF, D)
    conv = _mixconv_fwd(proj, cw_mix)
    mixed = jnp.concatenate([attn, conv], axis=1).astype(BF16)
    z = _mm(mixed, w_out_f, "nn", F32, 512, 512, "out_fwd")
    x1, h2 = _resid_norm2(xs, z, mod, norm2_g)
    pre_g, pre_v, y = _ffn_fwd(h2, w_up_t, cw_ffn, w_down_f)
    dout, dy, vec_l = _loss_head(x1, y, tgt, mod)

    dh2, dwup_t, dwd, dcw_g, dcw_v = _ffn_bwd(dy, h2, pre_g, pre_v, w_up_t, cw_ffn, w_down_f)
    dx1, dz, vec_2 = _norm_mod_bwd(dh2, x1, dout, z, mod, norm2_g, 3, 4, 2, "norm2_bwd")
    dwout = _mm(mixed, dz, "tn", BF16, 512, 512, "out_bwd_w")
    s_out = dwout.reshape(NCHIP, 2, D // NDEV, D)
    s_down = dwd.reshape(NCHIP, 2, DFF // NDEV, D)
    s_up = dwup_t.reshape(NCHIP, 2, n_ff, D)
    dmixed, (t_out, t_up, t_down) = _mm(dz, w_out_f, "nt", F32, 512, 512, "out_bwd_x",
                                        hosted=[(s_out, "pair"), (s_up, "pair"), (s_down, "pair")])
    c_out = _pair_add(s_out, t_out, 128, "pair_add_out")
    c_up = _pair_add(s_up, t_up, 176, "pair_add_up")
    c_down = _pair_add(s_down, t_down, 176, "pair_add_down")
    dxin, dbg, dcg, dcw_mix = _mixconv_bwd(dmixed, proj, cw_mix)
    (dqp, dkp, dvp), (p_up,) = _attn_bwd(qp, kp, vp, dmixed, attn, lse, [(c_up, "chips")])
    (dq, dk, dvb, dfcol, vec_qk), (p_out, p_down) = _qkv_post(
        dqp, dkp, dvp, proj, q_norm_g, k_norm_g, [(c_out, "chips"), (c_down, "chips")])
    dfg, vec_bf = _fgate_bwd(dfcol, proj, bf_pad)
    dproj = jnp.concatenate([dq, dk, dvb, dxin, dbg, dcg, dfg], axis=1)
    dwin_p = _mm(h, dproj, "tn", BF16, 512, 640, "proj_bwd_w")
    s_in = _scatter_dw_in(dwin_p)
    (t_in,) = _exchange([(s_in, "pair")], "exchange_pair_in")
    c_in = _pair_add(s_in, t_in, 256, "pair_add_in")
    dh, (p_in,) = _mm(dproj, w_in_p, "nt", F32, 512, 512, "proj_bwd_x", hosted=[(c_in, "chips")])
    grad_x, vec_1 = _norm_mod_bwd(dh, xs, dx1, None, mod, norm1_g, 0, 1, None, "norm1_bwd")

    misc = jnp.zeros((1, D), F32)
    misc = lax.dynamic_update_slice(misc, vec_bf[0:1, :HEADS], (0, LANE_BF))
    misc = lax.dynamic_update_slice(misc, vec_qk[0:1, :DH], (0, LANE_GQ))
    misc = lax.dynamic_update_slice(misc, vec_qk[1:2, :DH], (0, LANE_GK))
    rep = jnp.concatenate([
        vec_1[0:1], vec_1[1:2], vec_2[3:4], vec_2[0:1], vec_2[1:2], vec_l[0:1],
        vec_1[2:3], vec_2[2:3], vec_l[1:2], misc, jnp.zeros((REP_ROWS - 10, D), F32)], axis=0)
    dcw_ffn = jnp.concatenate([dcw_g, dcw_v], axis=1).reshape(SUB, NDEV, n_ff)
    dcw_all = jnp.concatenate([jnp.transpose(dcw_ffn, (1, 0, 2)),
                               jnp.transpose(dcw_mix.reshape(SUB, NDEV, DH), (1, 0, 2))], axis=2)
    r_out, (rep_all, conv_parts) = _adamw_sharded(p_out, w_out[0], m_w_out[0], v_w_out[0], 128, "adamw_out",
                                                  hosted=[(rep, "ag"), (dcw_all, "a2a")])
    dmod_my = lax.dynamic_slice(rep_all[:, :NMOD, :].reshape(NDEV, NMOD * D), (0, me * n_ada), (NDEV, n_ada))
    r_ada = _adamw_ada(c_all, dmod_my, w_ada[0], m_w_ada[0], v_w_ada[0])
    r_in = _adamw_sharded(p_in, w_in[0], m_w_in[0], v_w_in[0], 256, "adamw_in")
    r_up = _adamw_sharded(p_up, jnp.transpose(w_up[0]), jnp.transpose(m_w_up[0]), jnp.transpose(v_w_up[0]), 176,
                          "adamw_up")
    r_up = tuple(jnp.transpose(a) for a in r_up)
    r_down = _adamw_sharded(p_down, w_down[0], m_w_down[0], v_w_down[0], 176, "adamw_down")
    small = _adamw_small(rep_all, conv_parts, [
        [b_ada, m_b_ada, v_b_ada], [norm1_g, m_norm1_g, v_norm1_g], [b_forget, m_b_forget, v_b_forget],
        [q_norm_g, m_q_norm_g, v_q_norm_g], [k_norm_g, m_k_norm_g, v_k_norm_g], [norm2_g, m_norm2_g, v_norm2_g],
        [ffn_conv_w[0], m_ffn_conv_w[0], v_ffn_conv_w[0]], [conv_mix_w[0], m_conv_mix_w[0], v_conv_mix_w[0]]])
    loss = small[0].reshape(())
    r_bada, r_n1, r_bf, r_gq, r_gk, r_n2, r_cf, r_cm = [small[1 + 4 * p:5 + 4 * p] for p in range(8)]
    lead = lambda t: tuple(a[None] for a in t)
    per_w = [lead(r_ada), r_bada, r_n1, lead(r_in), r_bf, r_gq, r_gk, lead(r_cm), lead(r_out), r_n2,
             lead(r_up), lead(r_cf), lead(r_down)]
    outs = [loss, grad_x[None]]
    for field in range(4):
        outs += [t[field] for t in per_w]
    return tuple(outs)
```

```python
import functools

import jax
import jax.numpy as jnp
import numpy as np
from jax import lax
from jax.experimental import pallas as pl
from jax.experimental.pallas import tpu as pltpu

F32 = jnp.float32
BF16 = jnp.bfloat16

NDEV = 8
D = 1024
HEADS = 8
DH = 64
AW = 512
CW = 512
DFF = 2816
DIN = 3080
DINP = 3200
NMOD = 6
EPS = 1e-6
QK_SCALE = 0.125
LANES = 128
SUB = 8

ADAM_LR = 0.001
ADAM_B1 = 0.9
ADAM_B2 = 0.999
ADAM_EPS = 1e-08
ADAM_WD = 0.01
ADAM_STEP = 10

MESH = pl.DeviceIdType.MESH
ANY = pl.BlockSpec(memory_space=pl.ANY)

NN = (((1,), (0,)), ((), ()))
NT = (((1,), (1,)), ((), ()))
TN = (((0,), (0,)), ((), ()))


def _dot(a, b, dims=NN, precision=None):
    return lax.dot_general(a, b, dims, precision=precision, preferred_element_type=F32)


def _params(sem=None, vmem_mb=None):
    kw = {}
    if sem is not None:
        kw["dimension_semantics"] = sem
    if vmem_mb is not None:
        kw["vmem_limit_bytes"] = vmem_mb * 1024 * 1024
    return pltpu.CompilerParams(**kw)


def _sigmoid(x):
    return 0.5 * jnp.tanh(0.5 * x) + 0.5


class _Exchange:
    def __init__(self, items):
        self.arrays = [a for a, _ in items]
        self.modes = [m for _, m in items]
        self.n = len(items)
        self.out_shape = []
        for a, m in items:
            sh = {"ag": (NDEV,) + a.shape, "ag2": (NDEV,) + a.shape, "pair": a.shape[:1] + a.shape[2:]}.get(m, a.shape)
            self.out_shape.append(jax.ShapeDtypeStruct(sh, a.dtype))
        self.scratch = [pltpu.SemaphoreType.DMA((self.n, NDEV - 1)), pltpu.SemaphoreType.DMA((self.n, NDEV - 1)),
                        pltpu.SemaphoreType.DMA((self.n,))]

    def _plan(self, srcs, outs, sems):
        send_sems, recv_sems, loc_sems = sems
        x, y, c = lax.axis_index("x"), lax.axis_index("y"), lax.axis_index("c")
        me, my_chip = 4 * x + 2 * y + c, 2 * x + y
        sib = (x, y, 1 - c)
        local, first, landed, forwards, arrivals = [], [], [], [], []

        def remote(a, k, src, dst, to):
            return pltpu.make_async_remote_copy(src_ref=src, dst_ref=dst, send_sem=send_sems.at[a, k],
                                                recv_sem=recv_sems.at[a, k], device_id=to, device_id_type=MESH)

        for a, mode in enumerate(self.modes):
            src, out = srcs[a], outs[a]
            if mode in ("ag", "a2a"):
                piece = (lambda slot, src=src: src) if mode == "ag" else (lambda slot, src=src: src.at[slot])
                local.append(pltpu.make_async_copy(piece(me), out.at[me], loc_sems.at[a]))
                for r in range(1, NDEV):
                    px = 1 - x if (r >> 2) & 1 else x
                    py = 1 - y if (r >> 1) & 1 else y
                    pc = 1 - c if r & 1 else c
                    pidx = 4 * px + 2 * py + pc
                    first.append(remote(a, r - 1, piece(pidx), out.at[me], (px, py, pc)))
                    arrivals.append(remote(a, r - 1, piece(pidx), out.at[pidx], (px, py, pc)))
            elif mode == "ag2":
                local.append(pltpu.make_async_copy(src, out.at[me], loc_sems.at[a]))
                first.append(remote(a, 0, src, out.at[me], sib))
                arrivals.append(remote(a, 0, src, out.at[me + 1 - 2 * c], sib))
                for j, (px, py) in enumerate([(1 - x, y), (x, 1 - y), (1 - x, 1 - y)]):
                    theirs = out.at[4 * px + 2 * py + c]
                    first.append(remote(a, 1 + j, src, out.at[me], (px, py, c)))
                    landed.append(remote(a, 1 + j, src, theirs, (px, py, c)))
                    forwards.append(remote(a, 4 + j, theirs, theirs, sib))
                    arrivals.append(remote(a, 4 + j, src, out.at[4 * px + 2 * py + 1 - c], sib))
            elif mode == "pair":
                for q in range(NDEV // 2):
                    first.append(remote(a, q, src.at[q, 1 - c], out.at[q], sib))
                    arrivals.append(remote(a, q, src.at[q, 1 - c], out.at[q], sib))
            else:
                assert mode == "chips", mode
                local.append(pltpu.make_async_copy(src.at[my_chip], out.at[my_chip], loc_sems.at[a]))
                for j, (px, py) in enumerate([(1 - x, y), (x, 1 - y), (1 - x, 1 - y)]):
                    q = 2 * px + py
                    first.append(remote(a, 1 + j, src.at[q], out.at[my_chip], (px, py, c)))
                    arrivals.append(remote(a, 1 + j, src.at[q], out.at[q], (px, py, c)))
        return local, first, landed, forwards, arrivals

    def start(self, srcs, outs, sems):
        local, first, _, _, _ = self._plan(srcs, outs, sems)
        for cp in local + first:
            cp.start()

    def wait(self, srcs, outs, sems):
        local, first, landed, forwards, arrivals = self._plan(srcs, outs, sems)
        for cp, fwd in zip(landed, forwards):
            cp.wait_recv()
            fwd.start()
        for cp in arrivals:
            cp.wait_recv()
        for cp in first + forwards:
            cp.wait_send()
        for cp in local:
            cp.wait()


def _exchange(items, name):
    ex = _Exchange(items)
    n = ex.n

    def body(*refs):
        srcs, outs, sems = refs[:n], refs[n:2 * n], refs[2 * n:]
        ex.start(srcs, outs, sems)
        ex.wait(srcs, outs, sems)

    return pl.pallas_call(
        body, name=name,
        out_shape=tuple(ex.out_shape),
        in_specs=[ANY] * n, out_specs=tuple([ANY] * n),
        scratch_shapes=ex.scratch,
        compiler_params=pltpu.CompilerParams(has_side_effects=True),
    )(*ex.arrays)


def _call(body, inputs, *, name, grid, in_specs, out_specs, out_shape, scratch_shapes=(), vmem_mb=None, hosted=None):
    out_specs, out_shape, scratch_shapes = tuple(out_specs), tuple(out_shape), list(scratch_shapes)
    if not hosted:
        res = pl.pallas_call(
            body, name=name, grid=grid, in_specs=list(in_specs), out_specs=out_specs, out_shape=out_shape,
            scratch_shapes=scratch_shapes, compiler_params=_params(("arbitrary",) * len(grid), vmem_mb),
        )(*inputs)
        return tuple(res), ()
    ex = _Exchange(hosted)
    n, n_in, n_out, n_scr = ex.n, len(inputs), len(out_shape), len(scratch_shapes)

    def hosting_body(*refs):
        ins, srcs = refs[:n_in], refs[n_in:n_in + n]
        outs, landing = refs[n_in + n:n_in + n + n_out], refs[n_in + n + n_out:n_in + 2 * n + n_out]
        scratch, sems = refs[n_in + 2 * n + n_out:n_in + 2 * n + n_out + n_scr], refs[n_in + 2 * n + n_out + n_scr:]
        first = functools.reduce(jnp.logical_and, [pl.program_id(d) == 0 for d in range(len(grid))])
        last = functools.reduce(jnp.logical_and, [pl.program_id(d) == grid[d] - 1 for d in range(len(grid))])

        @pl.when(first)
        def _():
            ex.start(srcs, landing, sems)

        body(*ins, *outs, *scratch)

        @pl.when(last)
        def _():
            ex.wait(srcs, landing, sems)

    res = pl.pallas_call(
        hosting_body, name=name, grid=grid,
        in_specs=list(in_specs) + [ANY] * n, out_specs=out_specs + tuple([ANY] * n),
        out_shape=out_shape + tuple(ex.out_shape), scratch_shapes=scratch_shapes + ex.scratch,
        compiler_params=_params(("arbitrary",) * len(grid), vmem_mb),
    )(*inputs, *ex.arrays)
    return tuple(res[:n_out]), tuple(res[n_out:])


def _mm(a, b, mode, out_dtype, tm, tn, name, hosted=None):
    if mode == "nn":
        (m, k), n = a.shape, b.shape[1]
        a_spec = pl.BlockSpec((tm, k), lambda i, j: (i, 0))
        b_spec = pl.BlockSpec((k, tn), lambda i, j: (0, j))
        dims = NN
    elif mode == "nt":
        (m, k), n = a.shape, b.shape[0]
        a_spec = pl.BlockSpec((tm, k), lambda i, j: (i, 0))
        b_spec = pl.BlockSpec((tn, k), lambda i, j: (j, 0))
        dims = NT
    else:
        (k, m), n = a.shape, b.shape[1]
        a_spec = pl.BlockSpec((k, tm), lambda i, j: (0, i))
        b_spec = pl.BlockSpec((k, tn), lambda i, j: (0, j))
        dims = TN
    assert m % tm == 0 and n % tn == 0, (m, n, tm, tn)

    def body(a_ref, b_ref, o_ref):
        o_ref[...] = _dot(a_ref[...], b_ref[...], dims).astype(o_ref.dtype)

    (out,), moved = _call(
        body, (a, b), name=name, grid=(m // tm, n // tn),
        in_specs=[a_spec, b_spec], out_specs=[pl.BlockSpec((tm, tn), lambda i, j: (i, j))],
        out_shape=[jax.ShapeDtypeStruct((m, n), out_dtype)], vmem_mb=48, hosted=hosted)
    return (out, moved) if hosted else out


def _shift_down(x, k, fill):
    y = pltpu.roll(x, k, 0)
    row = lax.broadcasted_iota(jnp.int32, (SUB, x.shape[1]), 0)
    head = y[0:SUB, :]
    for t in range(k):
        head = jnp.where(row == t, fill[t], head)
    return jnp.concatenate([head, y[SUB:, :]], axis=0)


def _shift_up(x, k, fill):
    n = x.shape[0]
    y = pltpu.roll(x, n - k, 0)
    row = lax.broadcasted_iota(jnp.int32, (SUB, x.shape[1]), 0)
    tail = y[n - SUB:, :]
    for t in range(k):
        tail = jnp.where(row == SUB - k + t, fill[t], tail)
    return jnp.concatenate([y[:n - SUB, :], tail], axis=0)


def _conv_taps(x, halo, w):
    if halo is None:
        f1, f2 = [0.0], [0.0, 0.0]
    else:
        f1, f2 = [halo[7:8, :]], [halo[6:7, :], halo[7:8, :]]
    s1 = _shift_down(x, 1, f1)
    s2 = _shift_down(x, 2, f2)
    u = w[2:3, :] * x + w[1:2, :] * s1 + w[0:1, :] * s2
    return u, s1, s2


def _conv_taps_t(du, nxt, w):
    if nxt is None:
        f1, f2 = [0.0], [0.0, 0.0]
    else:
        f1, f2 = [nxt[0:1, :]], [nxt[0:1, :], nxt[1:2, :]]
    return w[2:3, :] * du + w[1:2, :] * _shift_up(du, 1, f1) + w[0:1, :] * _shift_up(du, 2, f2)


def _ada_fwd(c_all, w_ada, b_my):
    def body(c_ref, w_ref, b_ref, o_ref):
        cv = c_ref[...]
        act = cv * _sigmoid(cv)
        o_ref[...] = _dot(act, w_ref[...], NN, lax.Precision.HIGHEST) + b_ref[...]

    return pl.pallas_call(
        body, name="ada_fwd",
        out_shape=jax.ShapeDtypeStruct((NDEV, w_ada.shape[1]), F32),
        compiler_params=_params(None, 32),
    )(c_all, w_ada, b_my)


TR = 256


def _row_spec(width, col=0):
    return pl.BlockSpec((TR, width), lambda i, col=col: (i, col))


def _full_spec(shape):
    return pl.BlockSpec(shape, lambda i: (0,) * len(shape))


def _norm_mod_fwd(x, mod, g):
    s = x.shape[0]

    def body(x_ref, mod_ref, g_ref, h_ref):
        xv = x_ref[...]
        r = lax.rsqrt(jnp.mean(xv * xv, axis=-1, keepdims=True) + EPS)
        nrm = xv * r * g_ref[...]
        h_ref[...] = (nrm * (1.0 + mod_ref[1:2, :]) + mod_ref[0:1, :]).astype(BF16)

    return pl.pallas_call(
        body, name="norm1_fwd", grid=(s // TR,),
        in_specs=[_row_spec(D), _full_spec((SUB, D)), _full_spec((1, D))],
        out_specs=_row_spec(D), out_shape=jax.ShapeDtypeStruct((s, D), BF16),
        compiler_params=_params(("parallel",)),
    )(x, mod, g)


SLAB = 2 * DH
AUG_F, AUG_ONE, AUG_LSE = 0, 3, 6


def _split3(x):
    hi = x.astype(BF16).astype(F32)
    r1 = x - hi
    mid = r1.astype(BF16).astype(F32)
    return hi, mid, r1 - mid


def _lanes3(lane, first, pieces, other):
    out = other
    for k in range(3):
        out = jnp.where(lane == first + k, pieces[k], out)
    return out


def _aug_placement():
    eq = np.zeros((3 * LANES, HEADS * SLAB), np.float32)
    ek = np.zeros((3 * LANES, HEADS * SLAB), np.float32)
    ones = np.zeros((SUB, HEADS * SLAB), np.float32)
    for h in range(HEADS):
        aug = SLAB * h + DH
        for k in range(3):
            eq[LANES * k + h, aug + AUG_F + k] = 1.0
            ek[LANES * k + h, aug + AUG_ONE + k] = -1.0
            ones[0, aug + AUG_ONE + k] = 1.0
            ones[1, aug + AUG_F + k] = ones[1, aug + AUG_LSE + k] = 1.0
            ones[2, aug + k] = 1.0
    return jnp.asarray(eq, BF16), jnp.asarray(ek, BF16), jnp.asarray(ones)


def _qkv_prep(proj, fcum, gq, gk, hosted):
    s = proj.shape[0]

    def body(q_ref, k_ref, v_ref, f_ref, gq_ref, gk_ref, eq_ref, ek_ref, ones_ref, qo_ref, ko_ref, vo_ref):
        f3 = jnp.concatenate(_split3(f_ref[...]), axis=1).astype(BF16)
        qo_ref[...] = (_dot(f3, eq_ref[...]) + ones_ref[0:1, :]).astype(BF16)
        ko_ref[...] = (_dot(f3, ek_ref[...]) + ones_ref[1:2, :]).astype(BF16)
        vo_ref[...] = jnp.broadcast_to(ones_ref[2:3, :], vo_ref.shape).astype(BF16)
        for h in range(HEADS):
            sl = slice(DH * h, DH * (h + 1))
            lo = slice(SLAB * h, SLAB * h + DH)
            qh = q_ref[:, sl]
            r = lax.rsqrt(jnp.mean(qh * qh, axis=-1, keepdims=True) + EPS)
            qo_ref[:, lo] = (qh * r * gq_ref[...] * QK_SCALE).astype(BF16)
            kh = k_ref[:, sl]
            r = lax.rsqrt(jnp.mean(kh * kh, axis=-1, keepdims=True) + EPS)
            ko_ref[:, lo] = (kh * r * gk_ref[...]).astype(BF16)
            vo_ref[:, lo] = v_ref[:, sl].astype(BF16)

    eq, ek, ones = _aug_placement()
    o = jax.ShapeDtypeStruct((s, HEADS * SLAB), BF16)
    wide = _row_spec(HEADS * SLAB)
    return _call(
        body, (proj, proj, proj, fcum, gq, gk, eq, ek, ones), name="qkv_prep", grid=(s // TR,),
        in_specs=[_row_spec(AW, 0), _row_spec(AW, 1), _row_spec(AW, 2), _row_spec(LANES),
                  _full_spec((1, DH)), _full_spec((1, DH)), _full_spec(eq.shape), _full_spec(ek.shape),
                  _full_spec(ones.shape)],
        out_specs=[wide, wide, wide], out_shape=[o, o, o], vmem_mb=32, hosted=hosted)


FG_BLOCK = (3 * AW + 3 * CW) // LANES


def _fgate_fwd(proj, bf_pad):
    s = proj.shape[0]

    def body(fg_ref, b_ref, o_ref, carry_ref):
        i = pl.program_id(0)

        @pl.when(i == 0)
        def _():
            carry_ref[...] = jnp.zeros_like(carry_ref)

        z = fg_ref[...] + b_ref[...]
        logf = jnp.minimum(z, 0.0) - jnp.log1p(jnp.exp(-jnp.abs(z)))
        row = lax.broadcasted_iota(jnp.int32, (TR, TR), 0)
        col = lax.broadcasted_iota(jnp.int32, (TR, TR), 1)
        tri = (col <= row).astype(F32)
        cs = _dot(tri, logf, NN, lax.Precision.HIGHEST) + carry_ref[0:1, :]
        o_ref[...] = cs
        carry_ref[...] = jnp.broadcast_to(cs[TR - 1:TR, :], carry_ref.shape)

    return pl.pallas_call(
        body, name="fgate_fwd", grid=(s // TR,),
        in_specs=[_row_spec(LANES, FG_BLOCK), _full_spec((1, LANES))],
        out_specs=_row_spec(LANES), out_shape=jax.ShapeDtypeStruct((s, LANES), F32),
        scratch_shapes=[pltpu.VMEM((SUB, LANES), F32)],
        compiler_params=_params(("arbitrary",)),
    )(proj, bf_pad)


def _fgate_bwd(dfcol, proj, bf_pad):
    s = proj.shape[0]
    nb = s // TR

    def body(df_ref, fg_ref, b_ref, o_ref, db_ref, carry_ref):
        i = pl.program_id(0)

        @pl.when(i == 0)
        def _():
            carry_ref[...] = jnp.zeros_like(carry_ref)
            db_ref[...] = jnp.zeros_like(db_ref)

        row = lax.broadcasted_iota(jnp.int32, (TR, TR), 0)
        col = lax.broadcasted_iota(jnp.int32, (TR, TR), 1)
        tri = (col >= row).astype(F32)
        dlogf = _dot(tri, df_ref[...], NN, lax.Precision.HIGHEST) + carry_ref[0:1, :]
        carry_ref[...] = jnp.broadcast_to(dlogf[0:1, :], carry_ref.shape)
        z = fg_ref[...] + b_ref[...]
        dfg = dlogf * _sigmoid(-z)
        o_ref[...] = dfg.astype(BF16)
        db_ref[0:1, :] += jnp.sum(dfg, axis=0, keepdims=True)

    rev = lambda col: pl.BlockSpec((TR, LANES), lambda i, col=col: (nb - 1 - i, col))
    return pl.pallas_call(
        body, name="fgate_bwd", grid=(nb,),
        in_specs=[rev(0), rev(FG_BLOCK), _full_spec((1, LANES))],
        out_specs=(rev(0), _full_spec((SUB, LANES))),
        out_shape=(jax.ShapeDtypeStruct((s, LANES), BF16), jax.ShapeDtypeStruct((SUB, LANES), F32)),
        scratch_shapes=[pltpu.VMEM((SUB, LANES), F32)],
        compiler_params=_params(("arbitrary",)),
    )(dfcol, proj, bf_pad)


def _resid_norm2(x, z, mod, g):
    s = x.shape[0]

    def body(x_ref, z_ref, mod_ref, g_ref, x1_ref, h_ref):
        x1 = x_ref[...] + mod_ref[2:3, :] * z_ref[...]
        x1_ref[...] = x1
        r = lax.rsqrt(jnp.mean(x1 * x1, axis=-1, keepdims=True) + EPS)
        nrm = x1 * r * g_ref[...]
        h_ref[...] = (nrm * (1.0 + mod_ref[4:5, :]) + mod_ref[3:4, :]).astype(BF16)

    return pl.pallas_call(
        body, name="resid_norm2", grid=(s // TR,),
        in_specs=[_row_spec(D), _row_spec(D), _full_spec((SUB, D)), _full_spec((1, D))],
        out_specs=(_row_spec(D), _row_spec(D)),
        out_shape=(jax.ShapeDtypeStruct((s, D), F32), jax.ShapeDtypeStruct((s, D), BF16)),
        compiler_params=_params(("parallel",)),
    )(x, z, mod, g)


def _loss_head(x1, y, tgt, mod):
    s = x1.shape[0]

    def body(x1_ref, y_ref, t_ref, mod_ref, dout_ref, dy_ref, vec_ref):
        @pl.when(pl.program_id(0) == 0)
        def _():
            vec_ref[...] = jnp.zeros_like(vec_ref)

        yv = y_ref[...]
        g2 = mod_ref[5:6, :]
        diff = x1_ref[...] + g2 * yv - t_ref[...]
        dout = diff * (1.0 / D)
        dout_ref[...] = dout
        dy_ref[...] = (g2 * dout).astype(BF16)
        vec_ref[0:1, :] += jnp.sum(dout * yv, axis=0, keepdims=True)
        vec_ref[1:2, :] += jnp.sum(diff * diff, axis=0, keepdims=True)

    return pl.pallas_call(
        body, name="loss_head", grid=(s // TR,),
        in_specs=[_row_spec(D), _row_spec(D), _row_spec(D), _full_spec((SUB, D))],
        out_specs=(_row_spec(D), _row_spec(D), _full_spec((SUB, D))),
        out_shape=(jax.ShapeDtypeStruct((s, D), F32), jax.ShapeDtypeStruct((s, D), BF16),
                   jax.ShapeDtypeStruct((SUB, D), F32)),
        compiler_params=_params(("arbitrary",)),
    )(x1, y, tgt, mod)


def _norm_mod_bwd(dh, xin, dres, zin, mod, g, shift_row, scale_row, gate_row, name, hosted=None):
    s = dh.shape[0]
    with_gate = gate_row is not None

    def body(*refs):
        if with_gate:
            dh_ref, x_ref, dres_ref, z_ref, mod_ref, g_ref, dx_ref, dz_ref, vec_ref = refs
        else:
            dh_ref, x_ref, dres_ref, mod_ref, g_ref, dx_ref, vec_ref = refs

        @pl.when(pl.program_id(0) == 0)
        def _():
            vec_ref[...] = jnp.zeros_like(vec_ref)

        xv = x_ref[...]
        dhv = dh_ref[...]
        gv = g_ref[...]
        r = lax.rsqrt(jnp.mean(xv * xv, axis=-1, keepdims=True) + EPS)
        xh = xv * r
        dn = dhv * (1.0 + mod_ref[scale_row:scale_row + 1, :])
        dxh = dn * gv
        dx = dres_ref[...] + r * (dxh - xh * jnp.mean(dxh * xh, axis=-1, keepdims=True))
        dx_ref[...] = dx
        vec_ref[0:1, :] += jnp.sum(dhv, axis=0, keepdims=True)
        vec_ref[1:2, :] += jnp.sum(dhv * (xh * gv), axis=0, keepdims=True)
        vec_ref[2:3, :] += jnp.sum(dn * xh, axis=0, keepdims=True)
        if with_gate:
            dz_ref[...] = (mod_ref[gate_row:gate_row + 1, :] * dx).astype(BF16)
            vec_ref[3:4, :] += jnp.sum(dx * z_ref[...], axis=0, keepdims=True)

    ins = [dh, xin, dres] + ([zin] if with_gate else []) + [mod, g]
    in_specs = [_row_spec(D)] * (4 if with_gate else 3) + [_full_spec((SUB, D)), _full_spec((1, D))]
    out_specs = [_row_spec(D)] + ([_row_spec(D)] if with_gate else []) + [_full_spec((SUB, D))]
    out_shape = [jax.ShapeDtypeStruct((s, D), F32)] + ([jax.ShapeDtypeStruct((s, D), BF16)] if with_gate else []) \
        + [jax.ShapeDtypeStruct((SUB, D), F32)]
    outs, moved = _call(body, ins, name=name, grid=(s // TR,), in_specs=in_specs, out_specs=out_specs,
                        out_shape=out_shape, hosted=hosted)
    return outs + (moved,) if hosted else outs


XIN_BLOCK = 3 * AW // LANES
BG_BLOCK = XIN_BLOCK + CW // LANES
CG_BLOCK = BG_BLOCK + CW // LANES


def _seq_spec(s, first_block):
    return pl.BlockSpec((s, LANES), lambda j, fb=first_block: (0, fb + j))


def _mixconv_fwd(proj, w):
    s = proj.shape[0]

    def body(xin_ref, bg_ref, cg_ref, w_ref, o_ref):
        cx = cg_ref[...] * xin_ref[...]
        cv, _, _ = _conv_taps(cx, None, w_ref[...])
        o_ref[...] = bg_ref[...] * cv

    return pl.pallas_call(
        body, name="mixconv_fwd", grid=(CW // LANES,),
        in_specs=[_seq_spec(s, XIN_BLOCK), _seq_spec(s, BG_BLOCK), _seq_spec(s, CG_BLOCK),
                  pl.BlockSpec((3, LANES), lambda j: (0, j))],
        out_specs=_seq_spec(s, 0), out_shape=jax.ShapeDtypeStruct((s, CW), F32),
        compiler_params=_params(("parallel",), 48),
    )(proj, proj, proj, w)


def _mixconv_bwd(dmixed, proj, w):
    s = proj.shape[0]

    def body(d_ref, xin_ref, bg_ref, cg_ref, w_ref, dxin_ref, dbg_ref, dcg_ref, dw_ref):
        wv = w_ref[...]
        xin, cg, dconv = xin_ref[...], cg_ref[...], d_ref[...]
        cx = cg * xin
        cv, s1, s2 = _conv_taps(cx, None, wv)
        dbg_ref[...] = (dconv * cv).astype(BF16)
        dcv = dconv * bg_ref[...]
        dw_ref[...] = jnp.zeros_like(dw_ref)
        dw_ref[0:1, :] = jnp.sum(dcv * s2, axis=0, keepdims=True)
        dw_ref[1:2, :] = jnp.sum(dcv * s1, axis=0, keepdims=True)
        dw_ref[2:3, :] = jnp.sum(dcv * cx, axis=0, keepdims=True)
        dcx = _conv_taps_t(dcv, None, wv)
        dcg_ref[...] = (dcx * xin).astype(BF16)
        dxin_ref[...] = (dcx * cg).astype(BF16)

    o = jax.ShapeDtypeStruct((s, CW), BF16)
    return pl.pallas_call(
        body, name="mixconv_bwd", grid=(CW // LANES,),
        in_specs=[_seq_spec(s, AW // LANES), _seq_spec(s, XIN_BLOCK), _seq_spec(s, BG_BLOCK), _seq_spec(s, CG_BLOCK),
                  pl.BlockSpec((3, LANES), lambda j: (0, j))],
        out_specs=(_seq_spec(s, 0), _seq_spec(s, 0), _seq_spec(s, 0), pl.BlockSpec((SUB, LANES), lambda j: (0, j))),
        out_shape=(o, o, o, jax.ShapeDtypeStruct((SUB, CW), F32)),
        compiler_params=_params(("parallel",), 48),
    )(dmixed, proj, proj, proj, w)


TA = 512
NEG = -1e30


def _causal_mask():
    row = lax.broadcasted_iota(jnp.int32, (TA, TA), 0)
    col = lax.broadcasted_iota(jnp.int32, (TA, TA), 1)
    return col <= row


def _attn_fwd(qp, kp, vp, hosted):
    s = qp.shape[0]
    nq = s // TA

    def body(q_ref, k_ref, v_ref, o_ref, lse_ref):
        i = pl.program_id(1)
        slabs = [slice(SLAB * hh, SLAB * (hh + 1)) for hh in range(2)]
        q = [q_ref[:, sl] for sl in slabs]

        def block(j, carry, masked):
            keys = pl.ds(pl.multiple_of(j * TA, TA), TA)
            ms, acc = carry
            m_out, parts = [], []
            for hh in range(2):
                sc = _dot(q[hh], k_ref[keys, slabs[hh]], NT)
                if masked:
                    sc = jnp.where(_causal_mask(), sc, NEG)
                m_new = jnp.maximum(ms[hh], jnp.max(sc, axis=-1, keepdims=True))
                p = jnp.exp(sc - m_new)
                parts.append(jnp.exp(ms[hh] - m_new) * acc[:, slabs[hh]] + _dot(p.astype(BF16), v_ref[keys, slabs[hh]]))
                m_out.append(m_new)
            return tuple(m_out), jnp.concatenate(parts, axis=1)

        init = ((jnp.full((TA, 1), NEG, F32), jnp.full((TA, 1), NEG, F32)), jnp.zeros((TA, 2 * SLAB), F32))
        carry = lax.fori_loop(0, i, lambda j, cr: block(j, cr, False), init)
        ms, acc = block(i, carry, True)
        for hh in range(2):
            l = acc[:, SLAB * hh + DH:SLAB * hh + DH + 1]
            o_ref[:, DH * hh:DH * (hh + 1)] = acc[:, SLAB * hh:SLAB * hh + DH] / l
            lse_ref[0, :, hh:hh + 1] = ms[hh] + jnp.log(l)

    (o, lse), moved = _call(
        body, (qp, kp, vp), name="attn_fwd", grid=(HEADS // 2, nq),
        in_specs=[pl.BlockSpec((TA, 2 * SLAB), lambda p, i: (i, p)),
                  pl.BlockSpec((s, 2 * SLAB), lambda p, i: (0, p)),
                  pl.BlockSpec((s, 2 * SLAB), lambda p, i: (0, p))],
        out_specs=[pl.BlockSpec((TA, LANES), lambda p, i: (i, p)), pl.BlockSpec((1, TA, 2), lambda p, i: (p, i, 0))],
        out_shape=[jax.ShapeDtypeStruct((s, AW), F32), jax.ShapeDtypeStruct((HEADS // 2, s, 2), F32)],
        vmem_mb=32, hosted=hosted)
    return o, lse, moved


def _attn_bwd(qp, kp, vp, dmixed, o, lse, hosted):
    s = qp.shape[0]
    nq = s // TA

    def body(q_ref, k_ref, v_ref, do_ref, o_ref, lse_ref, dq_ref, dk_ref, dv_ref, qb_ref, dob_ref):
        dk_ref[...] = jnp.zeros_like(dk_ref)
        dv_ref[...] = jnp.zeros_like(dv_ref)
        slabs = [slice(SLAB * hh, SLAB * (hh + 1)) for hh in range(2)]
        lane = lax.broadcasted_iota(jnp.int32, (TA, DH), 1)

        def q_block(i, _):
            i0 = pl.multiple_of(i * TA, TA)
            rows = pl.ds(i0, TA)
            for hh in range(2):
                half = slice(DH * hh, DH * (hh + 1))
                do = do_ref[rows, half]
                delta = jnp.sum(do * o_ref[rows, half], axis=-1, keepdims=True)
                dob_ref[hh, :, 0:DH] = do.astype(BF16)
                dob_ref[hh, :, DH:SLAB] = _lanes3(lane, 0, [-d for d in _split3(delta)], 0.0).astype(BF16)
                lse3 = _split3(lse_ref[0, rows, hh:hh + 1])
                qb_ref[hh, :, 0:DH] = q_ref[rows, SLAB * hh:SLAB * hh + DH]
                aug = q_ref[rows, SLAB * hh + DH:SLAB * (hh + 1)].astype(F32)
                qb_ref[hh, :, DH:SLAB] = _lanes3(lane, AUG_LSE, [-x for x in lse3], aug).astype(BF16)

            def block(j, dq, masked):
                keys = pl.ds(pl.multiple_of(j * TA, TA), TA)
                dv, dk, dqc = [], [], []
                for hh in range(2):
                    q, dob = qb_ref[hh], dob_ref[hh]
                    k = k_ref[keys, slabs[hh]]
                    sc = _dot(q, k, NT)
                    if masked:
                        sc = jnp.where(_causal_mask(), sc, NEG)
                    p = jnp.exp(sc)
                    dv.append(_dot(p.astype(BF16), dob, TN))
                    ds = (p * _dot(dob, v_ref[keys, slabs[hh]], NT)).astype(BF16)
                    dk.append(_dot(ds, q, TN))
                    dqc.append(_dot(ds, k))
                dv_ref[keys, :] += jnp.concatenate(dv, axis=1)
                dk_ref[keys, :] += jnp.concatenate(dk, axis=1)
                return dq + jnp.concatenate(dqc, axis=1)

            dq = lax.fori_loop(0, i, lambda j, acc: block(j, acc, False), jnp.zeros((TA, 2 * SLAB), F32))
            dq_ref[rows, :] = block(i, dq, True)
            return 0

        lax.fori_loop(0, nq, q_block, 0)

    pair = lambda p: (0, p)
    slab2 = pl.BlockSpec((s, 2 * SLAB), pair)
    seq = pl.BlockSpec((s, LANES), pair)
    small = pl.BlockSpec((1, s, 2), lambda p: (p, 0, 0))
    o32 = jax.ShapeDtypeStruct((s, HEADS * SLAB), F32)
    return _call(
        body, (qp, kp, vp, dmixed, o, lse), name="attn_bwd", grid=(HEADS // 2,),
        in_specs=[slab2, slab2, slab2, seq, seq, small], out_specs=[slab2, slab2, slab2], out_shape=[o32, o32, o32],
        scratch_shapes=[pltpu.VMEM((2, TA, SLAB), BF16), pltpu.VMEM((2, TA, SLAB), BF16)], vmem_mb=48, hosted=hosted)


def _qkv_post(dqp, dkp, dvp, proj, gq, gk, hosted):
    s = proj.shape[0]

    def body(dq_ref, dk_ref, dv_ref, q_ref, k_ref, gq_ref, gk_ref, dqo_ref, dko_ref, dvo_ref, df_ref, vec_ref):
        @pl.when(pl.program_id(0) == 0)
        def _():
            vec_ref[...] = jnp.zeros_like(vec_ref)

        def one(d_ref, x_ref, g_ref, o_ref, row, scale):
            dg = jnp.zeros((1, DH), F32)
            for h in range(HEADS):
                sl = slice(DH * h, DH * (h + 1))
                xv = x_ref[:, sl]
                r = lax.rsqrt(jnp.mean(xv * xv, axis=-1, keepdims=True) + EPS)
                xh = xv * r
                dn = d_ref[:, SLAB * h:SLAB * h + DH] * scale
                dg = dg + jnp.sum(dn * xh, axis=0, keepdims=True)
                dxh = dn * g_ref[...]
                o_ref[:, sl] = (r * (dxh - xh * jnp.mean(dxh * xh, axis=-1, keepdims=True))).astype(BF16)
            vec_ref[row:row + 1, 0:DH] += dg

        one(dq_ref, q_ref, gq_ref, dqo_ref, 0, QK_SCALE)
        one(dk_ref, k_ref, gk_ref, dko_ref, 1, 1.0)
        lane = lax.broadcasted_iota(jnp.int32, (TR, LANES), 1)
        df = jnp.zeros((TR, LANES), F32)
        for h in range(HEADS):
            dvo_ref[:, DH * h:DH * (h + 1)] = dv_ref[:, SLAB * h:SLAB * h + DH].astype(BF16)
            row_sum = dq_ref[:, SLAB * h + DH:SLAB * h + DH + 1]
            col_sum = dk_ref[:, SLAB * h + DH + AUG_ONE:SLAB * h + DH + AUG_ONE + 1]
            df = jnp.where(lane == h, row_sum - col_sum, df)
        df_ref[...] = df

    o = jax.ShapeDtypeStruct((s, AW), BF16)
    wide = _row_spec(HEADS * SLAB)
    return _call(
        body, (dqp, dkp, dvp, proj, proj, gq, gk), name="qkv_post", grid=(s // TR,),
        in_specs=[wide, wide, wide, _row_spec(AW, 0), _row_spec(AW, 1), _full_spec((1, DH)), _full_spec((1, DH))],
        out_specs=[_row_spec(AW), _row_spec(AW), _row_spec(AW), _row_spec(LANES), _full_spec((SUB, LANES))],
        out_shape=[o, o, o, jax.ShapeDtypeStruct((s, LANES), F32), jax.ShapeDtypeStruct((SUB, LANES), F32)],
        hosted=hosted)


TF = 256
NJ = DFF // TF
FFN_ROWS_FWD = 1024
FFN_ROWS_BWD = 1024


def _ffn_fwd(h2, wup_t, cw, wd):
    s = h2.shape[0]
    tr = FFN_ROWS_FWD
    nr = s // tr

    def body(h_ref, wu_ref, cg_ref, cv_ref, wd_ref, pg_ref, pv_ref, y_ref, halo_ref):
        r, j = pl.program_id(0), pl.program_id(1)
        hv = h_ref[...]
        pg = _dot(hv, wu_ref[0], NT).astype(BF16)
        pv = _dot(hv, wu_ref[1], NT).astype(BF16)
        pg_ref[...] = pg
        pv_ref[...] = pv
        pgf, pvf = pg.astype(F32), pv.astype(F32)
        ug, _, _ = _conv_taps(pgf, jnp.where(r > 0, halo_ref[j, 0], 0.0), cg_ref[...])
        uv, _, _ = _conv_taps(pvf, jnp.where(r > 0, halo_ref[j, 1], 0.0), cv_ref[...])
        halo_ref[j, 0] = pgf[tr - SUB:tr, :]
        halo_ref[j, 1] = pvf[tr - SUB:tr, :]
        act = (ug * _sigmoid(ug) * uv).astype(BF16)
        contrib = _dot(act, wd_ref[...])

        @pl.when(j == 0)
        def _():
            y_ref[...] = contrib

        @pl.when(j > 0)
        def _():
            y_ref[...] += contrib

    pre = jax.ShapeDtypeStruct((s, DFF), BF16)
    return pl.pallas_call(
        body, name="ffn_fwd", grid=(nr, NJ),
        in_specs=[pl.BlockSpec((tr, D), lambda r, j: (r, 0)),
                  pl.BlockSpec((2, TF, D), lambda r, j: (0, j, 0)),
                  pl.BlockSpec((3, TF), lambda r, j: (0, j)),
                  pl.BlockSpec((3, TF), lambda r, j: (0, NJ + j)),
                  pl.BlockSpec((TF, D), lambda r, j: (j, 0))],
        out_specs=(pl.BlockSpec((tr, TF), lambda r, j: (r, j)),
                   pl.BlockSpec((tr, TF), lambda r, j: (r, j)),
                   pl.BlockSpec((tr, D), lambda r, j: (r, 0))),
        out_shape=(pre, pre, jax.ShapeDtypeStruct((s, D), F32)),
        scratch_shapes=[pltpu.VMEM((NJ, 2, SUB, TF), F32)],
        compiler_params=_params(("arbitrary", "arbitrary"), 56),
    )(h2, wup_t, cw, cw, wd)


def _ffn_bwd(dy, h2, pre_g, pre_v, wup_t, cw, wd):
    s = h2.shape[0]
    tr = FFN_ROWS_BWD
    nr = s // tr
    hb = tr // (2 * SUB)

    def body(dy_ref, h_ref, pg_ref, pv_ref, hg_ref, hv_ref, wu_ref, cg_ref, cv_ref, wd_ref,
             dh_ref, dwu_ref, dwd_ref, dcg_ref, dcv_ref, nxt_ref, awu_ref, awd_ref):
        j, r = pl.program_id(0), pl.program_id(1)
        rr = nr - 1 - r
        row0 = pl.multiple_of(rr * tr, tr)
        cwg, cwv = cg_ref[...], cv_ref[...]
        pg, pv = pg_ref[...].astype(F32), pv_ref[...].astype(F32)
        ug, g1, g2 = _conv_taps(pg, jnp.where(rr > 0, hg_ref[SUB:2 * SUB, :].astype(F32), 0.0), cwg)
        uv, v1, v2 = _conv_taps(pv, jnp.where(rr > 0, hv_ref[SUB:2 * SUB, :].astype(F32), 0.0), cwv)
        sg = _sigmoid(ug)
        sil = ug * sg
        act = (sil * uv).astype(BF16)
        dyv = dy_ref[...]
        da = _dot(dyv, wd_ref[...], NT)
        dug = da * uv * (sg * (1.0 + ug * (1.0 - sg)))
        duv = da * sil
        dpg = _conv_taps_t(dug, jnp.where(r > 0, nxt_ref[0], 0.0), cwg)
        dpv = _conv_taps_t(duv, jnp.where(r > 0, nxt_ref[1], 0.0), cwv)
        nxt_ref[0] = dug[0:SUB, :]
        nxt_ref[1] = duv[0:SUB, :]
        dpgb, dpvb = dpg.astype(BF16), dpv.astype(BF16)
        hv = h_ref[...]
        dwd = _dot(act, dyv, TN)
        dpb = jnp.concatenate([dpgb, dpvb], axis=1)
        dwu = _dot(dpb, hv, TN)
        dh = _dot(dpb, wu_ref[...].reshape(2 * TF, D))

        def taps(du, x0, x1, x2):
            return (jnp.sum(du * x2, axis=0, keepdims=True), jnp.sum(du * x1, axis=0, keepdims=True),
                    jnp.sum(du * x0, axis=0, keepdims=True))

        tg, tv = taps(dug, pg, g1, g2), taps(duv, pv, v1, v2)

        @pl.when(r == 0)
        def _():
            awd_ref[...] = dwd
            awu_ref[...] = dwu
            dcg_ref[...] = jnp.zeros_like(dcg_ref)
            dcv_ref[...] = jnp.zeros_like(dcv_ref)

        @pl.when(r > 0)
        def _():
            awd_ref[...] += dwd
            awu_ref[...] += dwu

        @pl.when(r == nr - 1)
        def _():
            dwd_ref[...] = awd_ref[...].astype(BF16)
            dwu_ref[...] = awu_ref[...].astype(BF16).reshape(2, TF, D)

        for t in range(3):
            dcg_ref[t:t + 1, :] += tg[t]
            dcv_ref[t:t + 1, :] += tv[t]

        @pl.when(j == 0)
        def _():
            dh_ref[pl.ds(row0, tr), :] = dh

        @pl.when(j > 0)
        def _():
            dh_ref[pl.ds(row0, tr), :] += dh

    rows = lambda j, r: (nr - 1 - r, 0)
    tile = lambda j, r: (nr - 1 - r, j)
    halo = lambda j, r: (jnp.maximum((nr - 1 - r) * hb - 1, 0), j)
    return pl.pallas_call(
        body, name="ffn_bwd", grid=(NJ, nr),
        in_specs=[pl.BlockSpec((tr, D), rows), pl.BlockSpec((tr, D), rows),
                  pl.BlockSpec((tr, TF), tile), pl.BlockSpec((tr, TF), tile),
                  pl.BlockSpec((2 * SUB, TF), halo), pl.BlockSpec((2 * SUB, TF), halo),
                  pl.BlockSpec((2, TF, D), lambda j, r: (0, j, 0)),
                  pl.BlockSpec((3, TF), lambda j, r: (0, j)), pl.BlockSpec((3, TF), lambda j, r: (0, NJ + j)),
                  pl.BlockSpec((TF, D), lambda j, r: (j, 0))],
        out_specs=(pl.BlockSpec((s, D), lambda j, r: (0, 0)),
                   pl.BlockSpec((2, TF, D), lambda j, r: (0, j, 0)),
                   pl.BlockSpec((TF, D), lambda j, r: (j, 0)),
                   pl.BlockSpec((SUB, TF), lambda j, r: (0, j)), pl.BlockSpec((SUB, TF), lambda j, r: (0, j))),
        out_shape=(jax.ShapeDtypeStruct((s, D), F32),
                   jax.ShapeDtypeStruct((2, DFF, D), BF16), jax.ShapeDtypeStruct((DFF, D), BF16),
                   jax.ShapeDtypeStruct((SUB, DFF), F32), jax.ShapeDtypeStruct((SUB, DFF), F32)),
        scratch_shapes=[pltpu.VMEM((2, SUB, TF), F32), pltpu.VMEM((2 * TF, D), F32), pltpu.VMEM((TF, D), F32)],
        compiler_params=_params(("arbitrary", "arbitrary"), 56),
    )(dy, h2, pre_g, pre_v, pre_g, pre_v, wup_t, cw, cw, wd)


def _adam(w, g, m, v):
    m = ADAM_B1 * m + (1.0 - ADAM_B1) * g
    v = ADAM_B2 * v + (1.0 - ADAM_B2) * (g * g)
    m_hat = m / (1.0 - ADAM_B1 ** ADAM_STEP)
    v_hat = v / (1.0 - ADAM_B2 ** ADAM_STEP)
    delta = -ADAM_LR * (m_hat / (jnp.sqrt(v_hat) + ADAM_EPS) + ADAM_WD * w)
    return delta, m, v


NCHIP = NDEV // 2


def _pair_add(mine, theirs, tr, name):
    _, _, rws, cols = mine.shape

    def body(a_ref, b_ref, o_ref):
        c = lax.axis_index("c")
        o_ref[0] = (a_ref[0, c].astype(F32) + b_ref[0].astype(F32)).astype(BF16)

    (out,), _ = _call(
        body, (mine, theirs), name=name, grid=(NCHIP, rws // tr),
        in_specs=[pl.BlockSpec((1, 2, tr, cols), lambda q, i: (q, 0, i, 0)),
                  pl.BlockSpec((1, tr, cols), lambda q, i: (q, i, 0))],
        out_specs=[pl.BlockSpec((1, tr, cols), lambda q, i: (q, i, 0))],
        out_shape=[jax.ShapeDtypeStruct((NCHIP, rws, cols), BF16)], vmem_mb=32)
    return out


def _adamw_sharded(parts, w, m, v, tr, name, hosted=None):
    rws, cols = w.shape

    def body(p_ref, w_ref, m_ref, v_ref, g_ref, d_ref, mo_ref, vo_ref):
        g = p_ref[0].astype(F32)
        for q in range(1, NCHIP):
            g = g + p_ref[q].astype(F32)
        g_ref[...] = g
        d_ref[...], mo_ref[...], vo_ref[...] = _adam(w_ref[...], g, m_ref[...], v_ref[...])

    blk = pl.BlockSpec((tr, cols), lambda i: (i, 0))
    o = jax.ShapeDtypeStruct((rws, cols), F32)
    outs, moved = _call(
        body, (parts, w, m, v), name=name, grid=(rws // tr,),
        in_specs=[pl.BlockSpec((NCHIP, tr, cols), lambda i: (0, i, 0)), blk, blk, blk],
        out_specs=[blk, blk, blk, blk], out_shape=[o, o, o, o], vmem_mb=48, hosted=hosted)
    return (outs, moved) if hosted else outs


def _adamw_ada(c_all, dmod_my, w, m, v):
    rws, cols = w.shape
    tr = 256

    def body(c_ref, dm_ref, w_ref, m_ref, v_ref, g_ref, d_ref, mo_ref, vo_ref):
        cv = c_ref[...]
        act = cv * _sigmoid(cv)
        g = _dot(act, dm_ref[...], TN, lax.Precision.HIGHEST)
        g_ref[...] = g
        d_ref[...], mo_ref[...], vo_ref[...] = _adam(w_ref[...], g, m_ref[...], v_ref[...])

    blk = pl.BlockSpec((tr, cols), lambda i: (i, 0))
    o = jax.ShapeDtypeStruct((rws, cols), F32)
    return pl.pallas_call(
        body, name="adamw_ada", grid=(rws // tr,),
        in_specs=[pl.BlockSpec((NDEV, tr), lambda i: (0, i)), _full_spec((NDEV, cols)), blk, blk, blk],
        out_specs=(blk, blk, blk, blk), out_shape=(o, o, o, o),
        compiler_params=_params(("parallel",), 48),
    )(c_all, dmod_my, w, m, v)


REP_ROWS = 16
ROW_N1, ROW_N2, ROW_LOSS, ROW_MISC = 6, 7, 8, 9
LANE_BF, LANE_GQ, LANE_GK = 0, 128, 256


def _adamw_small(rep_all, conv_all, wmv):
    n_ff = wmv[6][0].shape[1]

    def body(*refs):
        rep_ref, conv_ref = refs[:2]
        ins = refs[2:2 + 24]
        outs = refs[2 + 24:]
        loss_ref, outs = outs[0], outs[1:]
        g_rep = rep_ref[0]
        g_conv = conv_ref[0]
        for d in range(1, NDEV):
            g_rep = g_rep + rep_ref[d]
            g_conv = g_conv + conv_ref[d]
        loss_ref[...] = (0.5 / D) * jnp.sum(g_rep[ROW_LOSS:ROW_LOSS + 1, :], axis=-1, keepdims=True)
        grads = [
            None,
            g_rep[ROW_N1:ROW_N1 + 1, :],
            g_rep[ROW_MISC:ROW_MISC + 1, LANE_BF:LANE_BF + HEADS],
            g_rep[ROW_MISC:ROW_MISC + 1, LANE_GQ:LANE_GQ + DH],
            g_rep[ROW_MISC:ROW_MISC + 1, LANE_GK:LANE_GK + DH],
            g_rep[ROW_N2:ROW_N2 + 1, :],
            g_conv[0:3, 0:n_ff],
            g_conv[0:3, n_ff:n_ff + DH],
        ]
        for p in range(8):
            w_ref, m_ref, v_ref = ins[3 * p:3 * p + 3]
            g_ref, d_ref, mo_ref, vo_ref = outs[4 * p:4 * p + 4]
            if p == 0:
                for nmod in range(NMOD):
                    sl = slice(D * nmod, D * (nmod + 1))
                    g = g_rep[nmod:nmod + 1, :]
                    g_ref[:, sl] = g
                    d_ref[:, sl], mo_ref[:, sl], vo_ref[:, sl] = _adam(w_ref[:, sl], g, m_ref[:, sl], v_ref[:, sl])
            else:
                g = grads[p]
                g_ref[...] = g
                d_ref[...], mo_ref[...], vo_ref[...] = _adam(w_ref[...], g, m_ref[...], v_ref[...])

    flat = [a for trio in wmv for a in trio]
    out_shape = [jax.ShapeDtypeStruct((1, 1), F32)]
    for trio in wmv:
        out_shape += [jax.ShapeDtypeStruct(trio[0].shape, F32)] * 4
    return pl.pallas_call(
        body, name="adamw_small", out_shape=tuple(out_shape),
        compiler_params=_params(None, 32),
    )(rep_all, conv_all, *flat)


FG_FIRST = 3 * AW
N_IN = DIN // NDEV


def _w_in_runs():
    runs = []
    for d in range(NDEV):
        lo, hi = N_IN * d, N_IN * (d + 1)
        for a, b, shift in ((0, FG_FIRST, 0), (FG_FIRST, FG_FIRST + HEADS, DIN - HEADS - FG_FIRST),
                            (FG_FIRST + HEADS, DIN, -HEADS)):
            a, b = max(a, lo), min(b, hi)
            if a < b:
                runs.append((d, a - lo, a + shift, b - a))
    return runs


W_IN_ROWS = 256
N_IN_PAD = 512


def _identity(n):
    return (lax.broadcasted_iota(jnp.int32, (n, n), 0) == lax.broadcasted_iota(jnp.int32, (n, n), 1)).astype(BF16)


def _assemble_w_in(g_in):
    def body(g_ref, o_ref, t_ref):
        eye = _identity(W_IN_ROWS)
        shard = None
        for d, src, dst, width in _w_in_runs():
            if d != shard:
                t_ref[:, 0:N_IN] = _dot(eye, g_ref[d], NT).astype(BF16)
                shard = d
            o_ref[:, dst:dst + width] = t_ref[:, src:src + width]
        o_ref[:, DIN:DINP] = jnp.zeros((W_IN_ROWS, DINP - DIN), o_ref.dtype)

    (out,), _ = _call(
        body, (g_in,), name="assemble_w_in", grid=(D // W_IN_ROWS,),
        in_specs=[pl.BlockSpec((NDEV, N_IN, W_IN_ROWS), lambda i: (0, 0, i))],
        out_specs=[pl.BlockSpec((W_IN_ROWS, DINP), lambda i: (i, 0))],
        out_shape=[jax.ShapeDtypeStruct((D, DINP), g_in.dtype)],
        scratch_shapes=[pltpu.VMEM((W_IN_ROWS, N_IN_PAD), BF16)], vmem_mb=32)
    return out


def _scatter_dw_in(dwp):
    def body(w_ref, o_ref, t_ref):
        eye = _identity(W_IN_ROWS)
        runs = _w_in_runs()
        for i, (d, src, dst, width) in enumerate(runs):
            t_ref[:, src:src + width] = w_ref[:, dst:dst + width]
            if i + 1 == len(runs) or runs[i + 1][0] != d:
                o_ref[d // 2, d % 2] = _dot(t_ref[:, 0:N_IN], eye, TN).astype(BF16)

    (out,), _ = _call(
        body, (dwp,), name="scatter_dw_in", grid=(D // W_IN_ROWS,),
        in_specs=[pl.BlockSpec((W_IN_ROWS, DINP), lambda i: (i, 0))],
        out_specs=[pl.BlockSpec((NCHIP, 2, N_IN, W_IN_ROWS), lambda i: (0, 0, 0, i))],
        out_shape=[jax.ShapeDtypeStruct((NCHIP, 2, N_IN, D), dwp.dtype)],
        scratch_shapes=[pltpu.VMEM((W_IN_ROWS, N_IN_PAD), BF16)], vmem_mb=32)
    return out


def kernel(x, c, w_ada, b_ada, norm1_g, w_in, b_forget, q_norm_g, k_norm_g, conv_mix_w, w_out, norm2_g, w_up, ffn_conv_w, w_down, loss_target, m_w_ada, m_b_ada, m_norm1_g, m_w_in, m_b_forget, m_q_norm_g, m_k_norm_g, m_conv_mix_w, m_w_out, m_norm2_g, m_w_up, m_ffn_conv_w, m_w_down, v_w_ada, v_b_ada, v_norm1_g, v_w_in, v_b_forget, v_q_norm_g, v_k_norm_g, v_conv_mix_w, v_w_out, v_norm2_g, v_w_up, v_ffn_conv_w, v_w_down):
    me = 4 * lax.axis_index("x") + 2 * lax.axis_index("y") + lax.axis_index("c")
    xs, tgt = x[0], loss_target[0]
    s = xs.shape[0]
    nq = s // TA
    n_ada = w_ada.shape[2]
    n_ff = w_up.shape[2]

    conv_w = jnp.concatenate([ffn_conv_w[0], conv_mix_w[0]], axis=1)
    conv_w = jnp.concatenate([conv_w, jnp.zeros((SUB - 3, conv_w.shape[1]), F32)], axis=0)
    c_all, conv_all, g_in = _exchange(
        [(c.reshape(SUB, D // SUB), "ag"), (conv_w, "ag"), (jnp.transpose(w_in[0]).astype(BF16), "ag2")],
        "exchange_w_in")
    c_all = c_all.reshape(NDEV, D)
    cw_ffn = jnp.transpose(conv_all[:, :3, :n_ff], (1, 0, 2)).reshape(3, 2 * DFF)
    cw_mix = jnp.transpose(conv_all[:, :3, n_ff:], (1, 0, 2)).reshape(3, CW)
    w_in_p = _assemble_w_in(g_in)

    b_my = lax.dynamic_slice(b_ada, (0, me * n_ada), (1, n_ada))
    mod_part = _ada_fwd(c_all, w_ada[0], b_my)
    (mod_rows,) = _exchange([(jnp.broadcast_to(mod_part[:, None, :], (NDEV, SUB, n_ada)), "a2a")], "exchange_mod")
    mod = mod_rows[:, 0, :].reshape(NMOD, D)
    mod = jnp.concatenate([mod, jnp.zeros((SUB - NMOD, D), F32)], axis=0)

    h = _norm_mod_fwd(xs, mod, norm1_g)
    proj, (g_down,) = _mm(h, w_in_p, "nn", F32, 512, 640, "proj_fwd", hosted=[(w_down[0].astype(BF16), "ag2")])
    bf_pad = jnp.concatenate([b_forget, jnp.zeros((1, LANES - HEADS), F32)], axis=1)
    fcum = _fgate_fwd(proj, bf_pad)
    (qp, kp, vp), (g_out,) = _qkv_prep(proj, fcum, q_norm_g, k_norm_g, [(w_out[0].astype(BF16), "ag2")])
    attn, lse, (g_up,) = _attn_fwd(qp, kp, vp, [(jnp.transpose(w_up[0]).astype(BF16), "ag2")])
    w_out_f = g_out.reshape(D, D)
    w_up_t = g_up.reshape(2, DFF, D)
    w_down_f = g_down.reshape(DFF, D)
    conv = _mixconv_fwd(proj, cw_mix)
    mixed = jnp.concatenate([attn, conv], axis=1).astype(BF16)
    z = _mm(mixed, w_out_f, "nn", F32, 512, 512, "out_fwd")
    x1, h2 = _resid_norm2(xs, z, mod, norm2_g)
    pre_g, pre_v, y = _ffn_fwd(h2, w_up_t, cw_ffn, w_down_f)
    dout, dy, vec_l = _loss_head(x1, y, tgt, mod)

    dh2, dwup_t, dwd, dcw_g, dcw_v = _ffn_bwd(dy, h2, pre_g, pre_v, w_up_t, cw_ffn, w_down_f)
    dx1, dz, vec_2 = _norm_mod_bwd(dh2, x1, dout, z, mod, norm2_g, 3, 4, 2, "norm2_bwd")
    dwout = _mm(mixed, dz, "tn", BF16, 512, 512, "out_bwd_w")
    s_out = dwout.reshape(NCHIP, 2, D // NDEV, D)
    s_down = dwd.reshape(NCHIP, 2, DFF // NDEV, D)
    s_up = dwup_t.reshape(NCHIP, 2, n_ff, D)
    dmixed, (t_out, t_up, t_down) = _mm(dz, w_out_f, "nt", F32, 512, 512, "out_bwd_x",
                                        hosted=[(s_out, "pair"), (s_up, "pair"), (s_down, "pair")])
    c_out = _pair_add(s_out, t_out, 128, "pair_add_out")
    c_up = _pair_add(s_up, t_up, 176, "pair_add_up")
    c_down = _pair_add(s_down, t_down, 176, "pair_add_down")
    dxin, dbg, dcg, dcw_mix = _mixconv_bwd(dmixed, proj, cw_mix)
    (dqp, dkp, dvp), (p_up,) = _attn_bwd(qp, kp, vp, dmixed, attn, lse, [(c_up, "chips")])
    (dq, dk, dvb, dfcol, vec_qk), (p_out, p_down) = _qkv_post(
        dqp, dkp, dvp, proj, q_norm_g, k_norm_g, [(c_out, "chips"), (c_down, "chips")])
    dfg, vec_bf = _fgate_bwd(dfcol, proj, bf_pad)
    dproj = jnp.concatenate([dq, dk, dvb, dxin, dbg, dcg, dfg], axis=1)
    dwin_p = _mm(h, dproj, "tn", BF16, 512, 640, "proj_bwd_w")
    s_in = _scatter_dw_in(dwin_p)
    (t_in,) = _exchange([(s_in, "pair")], "exchange_pair_in")
    c_in = _pair_add(s_in, t_in, N_IN, "pair_add_in")
    dh, (p_in,) = _mm(dproj, w_in_p, "nt", F32, 512, 512, "proj_bwd_x", hosted=[(c_in, "chips")])
    grad_x, vec_1 = _norm_mod_bwd(dh, xs, dx1, None, mod, norm1_g, 0, 1, None, "norm1_bwd")

    misc = jnp.zeros((1, D), F32)
    misc = lax.dynamic_update_slice(misc, vec_bf[0:1, :HEADS], (0, LANE_BF))
    misc = lax.dynamic_update_slice(misc, vec_qk[0:1, :DH], (0, LANE_GQ))
    misc = lax.dynamic_update_slice(misc, vec_qk[1:2, :DH], (0, LANE_GK))
    rep = jnp.concatenate([
        vec_1[0:1], vec_1[1:2], vec_2[3:4], vec_2[0:1], vec_2[1:2], vec_l[0:1],
        vec_1[2:3], vec_2[2:3], vec_l[1:2], misc, jnp.zeros((REP_ROWS - 10, D), F32)], axis=0)
    dcw_ffn = jnp.concatenate([dcw_g, dcw_v], axis=1).reshape(SUB, NDEV, n_ff)
    dcw_all = jnp.concatenate([jnp.transpose(dcw_ffn, (1, 0, 2)),
                               jnp.transpose(dcw_mix.reshape(SUB, NDEV, DH), (1, 0, 2))], axis=2)
    r_out, (rep_all, conv_parts) = _adamw_sharded(p_out, w_out[0], m_w_out[0], v_w_out[0], 128, "adamw_out",
                                                  hosted=[(rep, "ag"), (dcw_all, "a2a")])
    dmod_my = lax.dynamic_slice(rep_all[:, :NMOD, :].reshape(NDEV, NMOD * D), (0, me * n_ada), (NDEV, n_ada))
    r_ada = _adamw_ada(c_all, dmod_my, w_ada[0], m_w_ada[0], v_w_ada[0])
    r_in = _adamw_sharded(p_in, jnp.transpose(w_in[0]), jnp.transpose(m_w_in[0]), jnp.transpose(v_w_in[0]), N_IN,
                          "adamw_in")
    r_in = tuple(jnp.transpose(a) for a in r_in)
    r_up = _adamw_sharded(p_up, jnp.transpose(w_up[0]), jnp.transpose(m_w_up[0]), jnp.transpose(v_w_up[0]), 176,
                          "adamw_up")
    r_up = tuple(jnp.transpose(a) for a in r_up)
    r_down = _adamw_sharded(p_down, w_down[0], m_w_down[0], v_w_down[0], 176, "adamw_down")
    small = _adamw_small(rep_all, conv_parts, [
        [b_ada, m_b_ada, v_b_ada], [norm1_g, m_norm1_g, v_norm1_g], [b_forget, m_b_forget, v_b_forget],
        [q_norm_g, m_q_norm_g, v_q_norm_g], [k_norm_g, m_k_norm_g, v_k_norm_g], [norm2_g, m_norm2_g, v_norm2_g],
        [ffn_conv_w[0], m_ffn_conv_w[0], v_ffn_conv_w[0]], [conv_mix_w[0], m_conv_mix_w[0], v_conv_mix_w[0]]])
    loss = small[0].reshape(())
    r_bada, r_n1, r_bf, r_gq, r_gk, r_n2, r_cf, r_cm = [small[1 + 4 * p:5 + 4 * p] for p in range(8)]
    lead = lambda t: tuple(a[None] for a in t)
    per_w = [lead(r_ada), r_bada, r_n1, lead(r_in), r_bf, r_gq, r_gk, lead(r_cm), lead(r_out), r_n2,
             lead(r_up), lead(r_cf), lead(r_down)]
    outs = [loss, grad_x[None]]
    for field in range(4):
        outs += [t[field] for t in per_w]
    return tuple(outs)
```

```python
import functools

import jax
import jax.numpy as jnp
import numpy as np
from jax import lax
from jax.experimental import pallas as pl
from jax.experimental.pallas import tpu as pltpu

F32 = jnp.float32
BF16 = jnp.bfloat16

NDEV = 8
D = 1024
HEADS = 8
DH = 64
AW = 512
CW = 512
DFF = 2816
DIN = 3080
DINP = 3200
NMOD = 6
EPS = 1e-6
QK_SCALE = 0.125
LANES = 128
SUB = 8

ADAM_LR = 0.001
ADAM_B1 = 0.9
ADAM_B2 = 0.999
ADAM_EPS = 1e-08
ADAM_WD = 0.01
ADAM_STEP = 10

MESH = pl.DeviceIdType.MESH
ANY = pl.BlockSpec(memory_space=pl.ANY)

NN = (((1,), (0,)), ((), ()))
NT = (((1,), (1,)), ((), ()))
TN = (((0,), (0,)), ((), ()))


def _dot(a, b, dims=NN, precision=None):
    return lax.dot_general(a, b, dims, precision=precision, preferred_element_type=F32)


def _params(sem=None, vmem_mb=None):
    kw = {}
    if sem is not None:
        kw["dimension_semantics"] = sem
    if vmem_mb is not None:
        kw["vmem_limit_bytes"] = vmem_mb * 1024 * 1024
    return pltpu.CompilerParams(**kw)


def _sigmoid(x):
    return 0.5 * jnp.tanh(0.5 * x) + 0.5


class _Exchange:
    def __init__(self, items):
        self.arrays = [a for a, _ in items]
        self.modes = [m for _, m in items]
        self.n = len(items)
        self.out_shape = []
        for a, m in items:
            sh = {"ag": (NDEV,) + a.shape, "ag2": (NDEV,) + a.shape, "pair": a.shape[:1] + a.shape[2:]}.get(m, a.shape)
            self.out_shape.append(jax.ShapeDtypeStruct(sh, a.dtype))
        self.scratch = [pltpu.SemaphoreType.DMA((self.n, NDEV - 1)), pltpu.SemaphoreType.DMA((self.n, NDEV - 1)),
                        pltpu.SemaphoreType.DMA((self.n,))]

    def _plan(self, srcs, outs, sems):
        send_sems, recv_sems, loc_sems = sems
        x, y, c = lax.axis_index("x"), lax.axis_index("y"), lax.axis_index("c")
        me, my_chip = 4 * x + 2 * y + c, 2 * x + y
        sib = (x, y, 1 - c)
        local, first, landed, forwards, arrivals = [], [], [], [], []

        def remote(a, k, src, dst, to):
            return pltpu.make_async_remote_copy(src_ref=src, dst_ref=dst, send_sem=send_sems.at[a, k],
                                                recv_sem=recv_sems.at[a, k], device_id=to, device_id_type=MESH)

        for a, mode in enumerate(self.modes):
            src, out = srcs[a], outs[a]
            if mode in ("ag", "a2a"):
                piece = (lambda slot, src=src: src) if mode == "ag" else (lambda slot, src=src: src.at[slot])
                local.append(pltpu.make_async_copy(piece(me), out.at[me], loc_sems.at[a]))
                for r in range(1, NDEV):
                    px = 1 - x if (r >> 2) & 1 else x
                    py = 1 - y if (r >> 1) & 1 else y
                    pc = 1 - c if r & 1 else c
                    pidx = 4 * px + 2 * py + pc
                    first.append(remote(a, r - 1, piece(pidx), out.at[me], (px, py, pc)))
                    arrivals.append(remote(a, r - 1, piece(pidx), out.at[pidx], (px, py, pc)))
            elif mode == "ag2":
                local.append(pltpu.make_async_copy(src, out.at[me], loc_sems.at[a]))
                first.append(remote(a, 0, src, out.at[me], sib))
                arrivals.append(remote(a, 0, src, out.at[me + 1 - 2 * c], sib))
                for j, (px, py) in enumerate([(1 - x, y), (x, 1 - y), (1 - x, 1 - y)]):
                    theirs = out.at[4 * px + 2 * py + c]
                    first.append(remote(a, 1 + j, src, out.at[me], (px, py, c)))
                    landed.append(remote(a, 1 + j, src, theirs, (px, py, c)))
                    forwards.append(remote(a, 4 + j, theirs, theirs, sib))
                    arrivals.append(remote(a, 4 + j, src, out.at[4 * px + 2 * py + 1 - c], sib))
            elif mode == "pair":
                for q in range(NDEV // 2):
                    first.append(remote(a, q, src.at[q, 1 - c], out.at[q], sib))
                    arrivals.append(remote(a, q, src.at[q, 1 - c], out.at[q], sib))
            else:
                assert mode == "chips", mode
                local.append(pltpu.make_async_copy(src.at[my_chip], out.at[my_chip], loc_sems.at[a]))
                for j, (px, py) in enumerate([(1 - x, y), (x, 1 - y), (1 - x, 1 - y)]):
                    q = 2 * px + py
                    first.append(remote(a, 1 + j, src.at[q], out.at[my_chip], (px, py, c)))
                    arrivals.append(remote(a, 1 + j, src.at[q], out.at[q], (px, py, c)))
        return local, first, landed, forwards, arrivals

    def start(self, srcs, outs, sems):
        local, first, _, _, _ = self._plan(srcs, outs, sems)
        for cp in local + first:
            cp.start()

    def wait(self, srcs, outs, sems):
        local, first, landed, forwards, arrivals = self._plan(srcs, outs, sems)
        for cp, fwd in zip(landed, forwards):
            cp.wait_recv()
            fwd.start()
        for cp in arrivals:
            cp.wait_recv()
        for cp in first + forwards:
            cp.wait_send()
        for cp in local:
            cp.wait()


def _exchange(items, name):
    ex = _Exchange(items)
    n = ex.n

    def body(*refs):
        srcs, outs, sems = refs[:n], refs[n:2 * n], refs[2 * n:]
        ex.start(srcs, outs, sems)
        ex.wait(srcs, outs, sems)

    return pl.pallas_call(
        body, name=name,
        out_shape=tuple(ex.out_shape),
        in_specs=[ANY] * n, out_specs=tuple([ANY] * n),
        scratch_shapes=ex.scratch,
        compiler_params=pltpu.CompilerParams(has_side_effects=True),
    )(*ex.arrays)


def _call(body, inputs, *, name, grid, in_specs, out_specs, out_shape, scratch_shapes=(), vmem_mb=None, hosted=None):
    out_specs, out_shape, scratch_shapes = tuple(out_specs), tuple(out_shape), list(scratch_shapes)
    if not hosted:
        res = pl.pallas_call(
            body, name=name, grid=grid, in_specs=list(in_specs), out_specs=out_specs, out_shape=out_shape,
            scratch_shapes=scratch_shapes, compiler_params=_params(("arbitrary",) * len(grid), vmem_mb),
        )(*inputs)
        return tuple(res), ()
    ex = _Exchange(hosted)
    n, n_in, n_out, n_scr = ex.n, len(inputs), len(out_shape), len(scratch_shapes)

    def hosting_body(*refs):
        ins, srcs = refs[:n_in], refs[n_in:n_in + n]
        outs, landing = refs[n_in + n:n_in + n + n_out], refs[n_in + n + n_out:n_in + 2 * n + n_out]
        scratch, sems = refs[n_in + 2 * n + n_out:n_in + 2 * n + n_out + n_scr], refs[n_in + 2 * n + n_out + n_scr:]
        first = functools.reduce(jnp.logical_and, [pl.program_id(d) == 0 for d in range(len(grid))])
        last = functools.reduce(jnp.logical_and, [pl.program_id(d) == grid[d] - 1 for d in range(len(grid))])

        @pl.when(first)
        def _():
            ex.start(srcs, landing, sems)

        body(*ins, *outs, *scratch)

        @pl.when(last)
        def _():
            ex.wait(srcs, landing, sems)

    res = pl.pallas_call(
        hosting_body, name=name, grid=grid,
        in_specs=list(in_specs) + [ANY] * n, out_specs=out_specs + tuple([ANY] * n),
        out_shape=out_shape + tuple(ex.out_shape), scratch_shapes=scratch_shapes + ex.scratch,
        compiler_params=_params(("arbitrary",) * len(grid), vmem_mb),
    )(*inputs, *ex.arrays)
    return tuple(res[:n_out]), tuple(res[n_out:])


def _mm(a, b, mode, out_dtype, tm, tn, name, hosted=None):
    if mode == "nn":
        (m, k), n = a.shape, b.shape[1]
        a_spec = pl.BlockSpec((tm, k), lambda i, j: (i, 0))
        b_spec = pl.BlockSpec((k, tn), lambda i, j: (0, j))
        dims = NN
    elif mode == "nt":
        (m, k), n = a.shape, b.shape[0]
        a_spec = pl.BlockSpec((tm, k), lambda i, j: (i, 0))
        b_spec = pl.BlockSpec((tn, k), lambda i, j: (j, 0))
        dims = NT
    else:
        (k, m), n = a.shape, b.shape[1]
        a_spec = pl.BlockSpec((k, tm), lambda i, j: (0, i))
        b_spec = pl.BlockSpec((k, tn), lambda i, j: (0, j))
        dims = TN
    assert m % tm == 0 and n % tn == 0, (m, n, tm, tn)

    def body(a_ref, b_ref, o_ref):
        o_ref[...] = _dot(a_ref[...], b_ref[...], dims).astype(o_ref.dtype)

    (out,), moved = _call(
        body, (a, b), name=name, grid=(m // tm, n // tn),
        in_specs=[a_spec, b_spec], out_specs=[pl.BlockSpec((tm, tn), lambda i, j: (i, j))],
        out_shape=[jax.ShapeDtypeStruct((m, n), out_dtype)], vmem_mb=48, hosted=hosted)
    return (out, moved) if hosted else out


def _shift_down(x, k, fill):
    y = pltpu.roll(x, k, 0)
    row = lax.broadcasted_iota(jnp.int32, (SUB, x.shape[1]), 0)
    head = y[0:SUB, :]
    for t in range(k):
        head = jnp.where(row == t, fill[t], head)
    return jnp.concatenate([head, y[SUB:, :]], axis=0)


def _shift_up(x, k, fill):
    n = x.shape[0]
    y = pltpu.roll(x, n - k, 0)
    row = lax.broadcasted_iota(jnp.int32, (SUB, x.shape[1]), 0)
    tail = y[n - SUB:, :]
    for t in range(k):
        tail = jnp.where(row == SUB - k + t, fill[t], tail)
    return jnp.concatenate([y[:n - SUB, :], tail], axis=0)


def _conv_taps(x, halo, w):
    if halo is None:
        f1, f2 = [0.0], [0.0, 0.0]
    else:
        f1, f2 = [halo[7:8, :]], [halo[6:7, :], halo[7:8, :]]
    s1 = _shift_down(x, 1, f1)
    s2 = _shift_down(x, 2, f2)
    u = w[2:3, :] * x + w[1:2, :] * s1 + w[0:1, :] * s2
    return u, s1, s2


def _conv_taps_t(du, nxt, w):
    if nxt is None:
        f1, f2 = [0.0], [0.0, 0.0]
    else:
        f1, f2 = [nxt[0:1, :]], [nxt[0:1, :], nxt[1:2, :]]
    return w[2:3, :] * du + w[1:2, :] * _shift_up(du, 1, f1) + w[0:1, :] * _shift_up(du, 2, f2)


def _ada_fwd(c_all, w_ada, b_my):
    def body(c_ref, w_ref, b_ref, o_ref):
        cv = c_ref[...]
        act = cv * _sigmoid(cv)
        o_ref[...] = _dot(act, w_ref[...], NN, lax.Precision.HIGHEST) + b_ref[...]

    return pl.pallas_call(
        body, name="ada_fwd",
        out_shape=jax.ShapeDtypeStruct((NDEV, w_ada.shape[1]), F32),
        compiler_params=_params(None, 32),
    )(c_all, w_ada, b_my)


TR = 256


def _row_spec(width, col=0):
    return pl.BlockSpec((TR, width), lambda i, col=col: (i, col))


def _full_spec(shape):
    return pl.BlockSpec(shape, lambda i: (0,) * len(shape))


def _norm_mod_fwd(x, mod, g):
    s = x.shape[0]

    def body(x_ref, mod_ref, g_ref, h_ref):
        xv = x_ref[...]
        r = lax.rsqrt(jnp.mean(xv * xv, axis=-1, keepdims=True) + EPS)
        nrm = xv * r * g_ref[...]
        h_ref[...] = (nrm * (1.0 + mod_ref[1:2, :]) + mod_ref[0:1, :]).astype(BF16)

    return pl.pallas_call(
        body, name="norm1_fwd", grid=(s // TR,),
        in_specs=[_row_spec(D), _full_spec((SUB, D)), _full_spec((1, D))],
        out_specs=_row_spec(D), out_shape=jax.ShapeDtypeStruct((s, D), BF16),
        compiler_params=_params(("parallel",)),
    )(x, mod, g)


SLAB = 2 * DH
AUG_F, AUG_ONE, AUG_LSE = 0, 3, 6


def _split3(x):
    hi = x.astype(BF16).astype(F32)
    r1 = x - hi
    mid = r1.astype(BF16).astype(F32)
    return hi, mid, r1 - mid


def _lanes3(lane, first, pieces, other):
    out = other
    for k in range(3):
        out = jnp.where(lane == first + k, pieces[k], out)
    return out


def _aug_placement():
    eq = np.zeros((3 * LANES, HEADS * SLAB), np.float32)
    ek = np.zeros((3 * LANES, HEADS * SLAB), np.float32)
    ones = np.zeros((SUB, HEADS * SLAB), np.float32)
    for h in range(HEADS):
        aug = SLAB * h + DH
        for k in range(3):
            eq[LANES * k + h, aug + AUG_F + k] = 1.0
            ek[LANES * k + h, aug + AUG_ONE + k] = -1.0
            ones[0, aug + AUG_ONE + k] = 1.0
            ones[1, aug + AUG_F + k] = ones[1, aug + AUG_LSE + k] = 1.0
            ones[2, aug + k] = 1.0
    return jnp.asarray(eq, BF16), jnp.asarray(ek, BF16), jnp.asarray(ones)


def _qkv_prep(proj, fcum, gq, gk, hosted):
    s = proj.shape[0]

    def body(q_ref, k_ref, v_ref, f_ref, gq_ref, gk_ref, eq_ref, ek_ref, ones_ref, qo_ref, ko_ref, vo_ref):
        f3 = jnp.concatenate(_split3(f_ref[...]), axis=1).astype(BF16)
        qo_ref[...] = (_dot(f3, eq_ref[...]) + ones_ref[0:1, :]).astype(BF16)
        ko_ref[...] = (_dot(f3, ek_ref[...]) + ones_ref[1:2, :]).astype(BF16)
        vo_ref[...] = jnp.broadcast_to(ones_ref[2:3, :], vo_ref.shape).astype(BF16)
        for h in range(HEADS):
            sl = slice(DH * h, DH * (h + 1))
            lo = slice(SLAB * h, SLAB * h + DH)
            qh = q_ref[:, sl]
            r = lax.rsqrt(jnp.mean(qh * qh, axis=-1, keepdims=True) + EPS)
            qo_ref[:, lo] = (qh * r * gq_ref[...] * QK_SCALE).astype(BF16)
            kh = k_ref[:, sl]
            r = lax.rsqrt(jnp.mean(kh * kh, axis=-1, keepdims=True) + EPS)
            ko_ref[:, lo] = (kh * r * gk_ref[...]).astype(BF16)
            vo_ref[:, lo] = v_ref[:, sl].astype(BF16)

    eq, ek, ones = _aug_placement()
    o = jax.ShapeDtypeStruct((s, HEADS * SLAB), BF16)
    wide = _row_spec(HEADS * SLAB)
    return _call(
        body, (proj, proj, proj, fcum, gq, gk, eq, ek, ones), name="qkv_prep", grid=(s // TR,),
        in_specs=[_row_spec(AW, 0), _row_spec(AW, 1), _row_spec(AW, 2), _row_spec(LANES),
                  _full_spec((1, DH)), _full_spec((1, DH)), _full_spec(eq.shape), _full_spec(ek.shape),
                  _full_spec(ones.shape)],
        out_specs=[wide, wide, wide], out_shape=[o, o, o], vmem_mb=32, hosted=hosted)


FG_BLOCK = (3 * AW + 3 * CW) // LANES


def _fgate_fwd(proj, bf_pad):
    s = proj.shape[0]

    def body(fg_ref, b_ref, o_ref, carry_ref):
        i = pl.program_id(0)

        @pl.when(i == 0)
        def _():
            carry_ref[...] = jnp.zeros_like(carry_ref)

        z = fg_ref[...] + b_ref[...]
        logf = jnp.minimum(z, 0.0) - jnp.log1p(jnp.exp(-jnp.abs(z)))
        row = lax.broadcasted_iota(jnp.int32, (TR, TR), 0)
        col = lax.broadcasted_iota(jnp.int32, (TR, TR), 1)
        tri = (col <= row).astype(F32)
        cs = _dot(tri, logf, NN, lax.Precision.HIGHEST) + carry_ref[0:1, :]
        o_ref[...] = cs
        carry_ref[...] = jnp.broadcast_to(cs[TR - 1:TR, :], carry_ref.shape)

    return pl.pallas_call(
        body, name="fgate_fwd", grid=(s // TR,),
        in_specs=[_row_spec(LANES, FG_BLOCK), _full_spec((1, LANES))],
        out_specs=_row_spec(LANES), out_shape=jax.ShapeDtypeStruct((s, LANES), F32),
        scratch_shapes=[pltpu.VMEM((SUB, LANES), F32)],
        compiler_params=_params(("arbitrary",)),
    )(proj, bf_pad)


def _fgate_bwd(dfcol, proj, bf_pad):
    s = proj.shape[0]
    nb = s // TR

    def body(df_ref, fg_ref, b_ref, o_ref, db_ref, carry_ref):
        i = pl.program_id(0)

        @pl.when(i == 0)
        def _():
            carry_ref[...] = jnp.zeros_like(carry_ref)
            db_ref[...] = jnp.zeros_like(db_ref)

        row = lax.broadcasted_iota(jnp.int32, (TR, TR), 0)
        col = lax.broadcasted_iota(jnp.int32, (TR, TR), 1)
        tri = (col >= row).astype(F32)
        dlogf = _dot(tri, df_ref[...], NN, lax.Precision.HIGHEST) + carry_ref[0:1, :]
        carry_ref[...] = jnp.broadcast_to(dlogf[0:1, :], carry_ref.shape)
        z = fg_ref[...] + b_ref[...]
        dfg = dlogf * _sigmoid(-z)
        o_ref[...] = dfg.astype(BF16)
        db_ref[0:1, :] += jnp.sum(dfg, axis=0, keepdims=True)

    rev = lambda col: pl.BlockSpec((TR, LANES), lambda i, col=col: (nb - 1 - i, col))
    return pl.pallas_call(
        body, name="fgate_bwd", grid=(nb,),
        in_specs=[rev(0), rev(FG_BLOCK), _full_spec((1, LANES))],
        out_specs=(rev(0), _full_spec((SUB, LANES))),
        out_shape=(jax.ShapeDtypeStruct((s, LANES), BF16), jax.ShapeDtypeStruct((SUB, LANES), F32)),
        scratch_shapes=[pltpu.VMEM((SUB, LANES), F32)],
        compiler_params=_params(("arbitrary",)),
    )(dfcol, proj, bf_pad)


def _resid_norm2(x, z, mod, g):
    s = x.shape[0]

    def body(x_ref, z_ref, mod_ref, g_ref, x1_ref, h_ref):
        x1 = x_ref[...] + mod_ref[2:3, :] * z_ref[...]
        x1_ref[...] = x1
        r = lax.rsqrt(jnp.mean(x1 * x1, axis=-1, keepdims=True) + EPS)
        nrm = x1 * r * g_ref[...]
        h_ref[...] = (nrm * (1.0 + mod_ref[4:5, :]) + mod_ref[3:4, :]).astype(BF16)

    return pl.pallas_call(
        body, name="resid_norm2", grid=(s // TR,),
        in_specs=[_row_spec(D), _row_spec(D), _full_spec((SUB, D)), _full_spec((1, D))],
        out_specs=(_row_spec(D), _row_spec(D)),
        out_shape=(jax.ShapeDtypeStruct((s, D), F32), jax.ShapeDtypeStruct((s, D), BF16)),
        compiler_params=_params(("parallel",)),
    )(x, z, mod, g)


def _loss_head(x1, y, tgt, mod):
    s = x1.shape[0]

    def body(x1_ref, y_ref, t_ref, mod_ref, dout_ref, dy_ref, vec_ref):
        @pl.when(pl.program_id(0) == 0)
        def _():
            vec_ref[...] = jnp.zeros_like(vec_ref)

        yv = y_ref[...]
        g2 = mod_ref[5:6, :]
        diff = x1_ref[...] + g2 * yv - t_ref[...]
        dout = diff * (1.0 / D)
        dout_ref[...] = dout
        dy_ref[...] = (g2 * dout).astype(BF16)
        vec_ref[0:1, :] += jnp.sum(dout * yv, axis=0, keepdims=True)
        vec_ref[1:2, :] += jnp.sum(diff * diff, axis=0, keepdims=True)

    return pl.pallas_call(
        body, name="loss_head", grid=(s // TR,),
        in_specs=[_row_spec(D), _row_spec(D), _row_spec(D), _full_spec((SUB, D))],
        out_specs=(_row_spec(D), _row_spec(D), _full_spec((SUB, D))),
        out_shape=(jax.ShapeDtypeStruct((s, D), F32), jax.ShapeDtypeStruct((s, D), BF16),
                   jax.ShapeDtypeStruct((SUB, D), F32)),
        compiler_params=_params(("arbitrary",)),
    )(x1, y, tgt, mod)


def _norm_mod_bwd(dh, xin, dres, zin, mod, g, shift_row, scale_row, gate_row, name, hosted=None):
    s = dh.shape[0]
    with_gate = gate_row is not None

    def body(*refs):
        if with_gate:
            dh_ref, x_ref, dres_ref, z_ref, mod_ref, g_ref, dx_ref, dz_ref, vec_ref = refs
        else:
            dh_ref, x_ref, dres_ref, mod_ref, g_ref, dx_ref, vec_ref = refs

        @pl.when(pl.program_id(0) == 0)
        def _():
            vec_ref[...] = jnp.zeros_like(vec_ref)

        xv = x_ref[...]
        dhv = dh_ref[...]
        gv = g_ref[...]
        r = lax.rsqrt(jnp.mean(xv * xv, axis=-1, keepdims=True) + EPS)
        xh = xv * r
        dn = dhv * (1.0 + mod_ref[scale_row:scale_row + 1, :])
        dxh = dn * gv
        dx = dres_ref[...] + r * (dxh - xh * jnp.mean(dxh * xh, axis=-1, keepdims=True))
        dx_ref[...] = dx
        vec_ref[0:1, :] += jnp.sum(dhv, axis=0, keepdims=True)
        vec_ref[1:2, :] += jnp.sum(dhv * (xh * gv), axis=0, keepdims=True)
        vec_ref[2:3, :] += jnp.sum(dn * xh, axis=0, keepdims=True)
        if with_gate:
            dz_ref[...] = (mod_ref[gate_row:gate_row + 1, :] * dx).astype(BF16)
            vec_ref[3:4, :] += jnp.sum(dx * z_ref[...], axis=0, keepdims=True)

    ins = [dh, xin, dres] + ([zin] if with_gate else []) + [mod, g]
    in_specs = [_row_spec(D)] * (4 if with_gate else 3) + [_full_spec((SUB, D)), _full_spec((1, D))]
    out_specs = [_row_spec(D)] + ([_row_spec(D)] if with_gate else []) + [_full_spec((SUB, D))]
    out_shape = [jax.ShapeDtypeStruct((s, D), F32)] + ([jax.ShapeDtypeStruct((s, D), BF16)] if with_gate else []) \
        + [jax.ShapeDtypeStruct((SUB, D), F32)]
    outs, moved = _call(body, ins, name=name, grid=(s // TR,), in_specs=in_specs, out_specs=out_specs,
                        out_shape=out_shape, hosted=hosted)
    return outs + (moved,) if hosted else outs


XIN_BLOCK = 3 * AW // LANES
BG_BLOCK = XIN_BLOCK + CW // LANES
CG_BLOCK = BG_BLOCK + CW // LANES


def _seq_spec(s, first_block):
    return pl.BlockSpec((s, LANES), lambda j, fb=first_block: (0, fb + j))


def _mixconv_fwd(proj, w):
    s = proj.shape[0]

    def body(xin_ref, bg_ref, cg_ref, w_ref, o_ref):
        cx = cg_ref[...] * xin_ref[...]
        cv, _, _ = _conv_taps(cx, None, w_ref[...])
        o_ref[...] = bg_ref[...] * cv

    return pl.pallas_call(
        body, name="mixconv_fwd", grid=(CW // LANES,),
        in_specs=[_seq_spec(s, XIN_BLOCK), _seq_spec(s, BG_BLOCK), _seq_spec(s, CG_BLOCK),
                  pl.BlockSpec((3, LANES), lambda j: (0, j))],
        out_specs=_seq_spec(s, 0), out_shape=jax.ShapeDtypeStruct((s, CW), F32),
        compiler_params=_params(("parallel",), 48),
    )(proj, proj, proj, w)


def _mixconv_bwd(dmixed, proj, w):
    s = proj.shape[0]

    def body(d_ref, xin_ref, bg_ref, cg_ref, w_ref, dxin_ref, dbg_ref, dcg_ref, dw_ref):
        wv = w_ref[...]
        xin, cg, dconv = xin_ref[...], cg_ref[...], d_ref[...]
        cx = cg * xin
        cv, s1, s2 = _conv_taps(cx, None, wv)
        dbg_ref[...] = (dconv * cv).astype(BF16)
        dcv = dconv * bg_ref[...]
        dw_ref[...] = jnp.zeros_like(dw_ref)
        dw_ref[0:1, :] = jnp.sum(dcv * s2, axis=0, keepdims=True)
        dw_ref[1:2, :] = jnp.sum(dcv * s1, axis=0, keepdims=True)
        dw_ref[2:3, :] = jnp.sum(dcv * cx, axis=0, keepdims=True)
        dcx = _conv_taps_t(dcv, None, wv)
        dcg_ref[...] = (dcx * xin).astype(BF16)
        dxin_ref[...] = (dcx * cg).astype(BF16)

    o = jax.ShapeDtypeStruct((s, CW), BF16)
    return pl.pallas_call(
        body, name="mixconv_bwd", grid=(CW // LANES,),
        in_specs=[_seq_spec(s, AW // LANES), _seq_spec(s, XIN_BLOCK), _seq_spec(s, BG_BLOCK), _seq_spec(s, CG_BLOCK),
                  pl.BlockSpec((3, LANES), lambda j: (0, j))],
        out_specs=(_seq_spec(s, 0), _seq_spec(s, 0), _seq_spec(s, 0), pl.BlockSpec((SUB, LANES), lambda j: (0, j))),
        out_shape=(o, o, o, jax.ShapeDtypeStruct((SUB, CW), F32)),
        compiler_params=_params(("parallel",), 48),
    )(dmixed, proj, proj, proj, w)


TA = 512
NEG = -1e30


def _causal_mask():
    row = lax.broadcasted_iota(jnp.int32, (TA, TA), 0)
    col = lax.broadcasted_iota(jnp.int32, (TA, TA), 1)
    return col <= row


def _attn_fwd(qp, kp, vp, hosted):
    s = qp.shape[0]
    nq = s // TA

    def body(q_ref, k_ref, v_ref, o_ref, lse_ref):
        i = pl.program_id(1)
        slabs = [slice(SLAB * hh, SLAB * (hh + 1)) for hh in range(2)]
        q = [q_ref[:, sl] for sl in slabs]

        def block(j, carry, masked):
            keys = pl.ds(pl.multiple_of(j * TA, TA), TA)
            ms, acc = carry
            m_out, parts = [], []
            for hh in range(2):
                sc = _dot(q[hh], k_ref[keys, slabs[hh]], NT)
                if masked:
                    sc = jnp.where(_causal_mask(), sc, NEG)
                m_new = jnp.maximum(ms[hh], jnp.max(sc, axis=-1, keepdims=True))
                p = jnp.exp(sc - m_new)
                parts.append(jnp.exp(ms[hh] - m_new) * acc[:, slabs[hh]] + _dot(p.astype(BF16), v_ref[keys, slabs[hh]]))
                m_out.append(m_new)
            return tuple(m_out), jnp.concatenate(parts, axis=1)

        init = ((jnp.full((TA, 1), NEG, F32), jnp.full((TA, 1), NEG, F32)), jnp.zeros((TA, 2 * SLAB), F32))
        carry = lax.fori_loop(0, i, lambda j, cr: block(j, cr, False), init)
        ms, acc = block(i, carry, True)
        for hh in range(2):
            l = acc[:, SLAB * hh + DH:SLAB * hh + DH + 1]
            o_ref[:, DH * hh:DH * (hh + 1)] = acc[:, SLAB * hh:SLAB * hh + DH] / l
            lse_ref[0, :, hh:hh + 1] = ms[hh] + jnp.log(l)

    (o, lse), moved = _call(
        body, (qp, kp, vp), name="attn_fwd", grid=(HEADS // 2, nq),
        in_specs=[pl.BlockSpec((TA, 2 * SLAB), lambda p, i: (i, p)),
                  pl.BlockSpec((s, 2 * SLAB), lambda p, i: (0, p)),
                  pl.BlockSpec((s, 2 * SLAB), lambda p, i: (0, p))],
        out_specs=[pl.BlockSpec((TA, LANES), lambda p, i: (i, p)), pl.BlockSpec((1, TA, 2), lambda p, i: (p, i, 0))],
        out_shape=[jax.ShapeDtypeStruct((s, AW), F32), jax.ShapeDtypeStruct((HEADS // 2, s, 2), F32)],
        vmem_mb=32, hosted=hosted)
    return o, lse, moved


def _attn_bwd(qp, kp, vp, dmixed, o, lse, hosted):
    s = qp.shape[0]
    nq = s // TA

    def body(q_ref, k_ref, v_ref, do_ref, o_ref, lse_ref, dq_ref, dk_ref, dv_ref, qb_ref, dob_ref):
        dk_ref[...] = jnp.zeros_like(dk_ref)
        dv_ref[...] = jnp.zeros_like(dv_ref)
        slabs = [slice(SLAB * hh, SLAB * (hh + 1)) for hh in range(2)]
        lane = lax.broadcasted_iota(jnp.int32, (TA, DH), 1)

        def q_block(i, _):
            i0 = pl.multiple_of(i * TA, TA)
            rows = pl.ds(i0, TA)
            for hh in range(2):
                half = slice(DH * hh, DH * (hh + 1))
                do = do_ref[rows, half]
                delta = jnp.sum(do * o_ref[rows, half], axis=-1, keepdims=True)
                dob_ref[hh, :, 0:DH] = do.astype(BF16)
                dob_ref[hh, :, DH:SLAB] = _lanes3(lane, 0, [-d for d in _split3(delta)], 0.0).astype(BF16)
                lse3 = _split3(lse_ref[0, rows, hh:hh + 1])
                qb_ref[hh, :, 0:DH] = q_ref[rows, SLAB * hh:SLAB * hh + DH]
                aug = q_ref[rows, SLAB * hh + DH:SLAB * (hh + 1)].astype(F32)
                qb_ref[hh, :, DH:SLAB] = _lanes3(lane, AUG_LSE, [-x for x in lse3], aug).astype(BF16)

            def block(j, dq, masked):
                keys = pl.ds(pl.multiple_of(j * TA, TA), TA)
                dv, dk, dqc = [], [], []
                for hh in range(2):
                    q, dob = qb_ref[hh], dob_ref[hh]
                    k = k_ref[keys, slabs[hh]]
                    sc = _dot(q, k, NT)
                    if masked:
                        sc = jnp.where(_causal_mask(), sc, NEG)
                    p = jnp.exp(sc)
                    dv.append(_dot(p.astype(BF16), dob, TN))
                    ds = (p * _dot(dob, v_ref[keys, slabs[hh]], NT)).astype(BF16)
                    dk.append(_dot(ds, q, TN))
                    dqc.append(_dot(ds, k))
                dv_ref[keys, :] += jnp.concatenate(dv, axis=1)
                dk_ref[keys, :] += jnp.concatenate(dk, axis=1)
                return dq + jnp.concatenate(dqc, axis=1)

            dq = lax.fori_loop(0, i, lambda j, acc: block(j, acc, False), jnp.zeros((TA, 2 * SLAB), F32))
            dq_ref[rows, :] = block(i, dq, True)
            return 0

        lax.fori_loop(0, nq, q_block, 0)

    pair = lambda p: (0, p)
    slab2 = pl.BlockSpec((s, 2 * SLAB), pair)
    seq = pl.BlockSpec((s, LANES), pair)
    small = pl.BlockSpec((1, s, 2), lambda p: (p, 0, 0))
    o32 = jax.ShapeDtypeStruct((s, HEADS * SLAB), F32)
    return _call(
        body, (qp, kp, vp, dmixed, o, lse), name="attn_bwd", grid=(HEADS // 2,),
        in_specs=[slab2, slab2, slab2, seq, seq, small], out_specs=[slab2, slab2, slab2], out_shape=[o32, o32, o32],
        scratch_shapes=[pltpu.VMEM((2, TA, SLAB), BF16), pltpu.VMEM((2, TA, SLAB), BF16)], vmem_mb=48, hosted=hosted)


def _qkv_post(dqp, dkp, dvp, proj, gq, gk, hosted):
    s = proj.shape[0]

    def body(dq_ref, dk_ref, dv_ref, q_ref, k_ref, gq_ref, gk_ref, dqo_ref, dko_ref, dvo_ref, df_ref, vec_ref):
        @pl.when(pl.program_id(0) == 0)
        def _():
            vec_ref[...] = jnp.zeros_like(vec_ref)

        def one(d_ref, x_ref, g_ref, o_ref, row, scale):
            dg = jnp.zeros((1, DH), F32)
            for h in range(HEADS):
                sl = slice(DH * h, DH * (h + 1))
                xv = x_ref[:, sl]
                r = lax.rsqrt(jnp.mean(xv * xv, axis=-1, keepdims=True) + EPS)
                xh = xv * r
                dn = d_ref[:, SLAB * h:SLAB * h + DH] * scale
                dg = dg + jnp.sum(dn * xh, axis=0, keepdims=True)
                dxh = dn * g_ref[...]
                o_ref[:, sl] = (r * (dxh - xh * jnp.mean(dxh * xh, axis=-1, keepdims=True))).astype(BF16)
            vec_ref[row:row + 1, 0:DH] += dg

        one(dq_ref, q_ref, gq_ref, dqo_ref, 0, QK_SCALE)
        one(dk_ref, k_ref, gk_ref, dko_ref, 1, 1.0)
        lane = lax.broadcasted_iota(jnp.int32, (TR, LANES), 1)
        df = jnp.zeros((TR, LANES), F32)
        for h in range(HEADS):
            dvo_ref[:, DH * h:DH * (h + 1)] = dv_ref[:, SLAB * h:SLAB * h + DH].astype(BF16)
            row_sum = dq_ref[:, SLAB * h + DH:SLAB * h + DH + 1]
            col_sum = dk_ref[:, SLAB * h + DH + AUG_ONE:SLAB * h + DH + AUG_ONE + 1]
            df = jnp.where(lane == h, row_sum - col_sum, df)
        df_ref[...] = df

    o = jax.ShapeDtypeStruct((s, AW), BF16)
    wide = _row_spec(HEADS * SLAB)
    return _call(
        body, (dqp, dkp, dvp, proj, proj, gq, gk), name="qkv_post", grid=(s // TR,),
        in_specs=[wide, wide, wide, _row_spec(AW, 0), _row_spec(AW, 1), _full_spec((1, DH)), _full_spec((1, DH))],
        out_specs=[_row_spec(AW), _row_spec(AW), _row_spec(AW), _row_spec(LANES), _full_spec((SUB, LANES))],
        out_shape=[o, o, o, jax.ShapeDtypeStruct((s, LANES), F32), jax.ShapeDtypeStruct((SUB, LANES), F32)],
        hosted=hosted)


TF = 256
NJ = DFF // TF
FFN_ROWS_FWD = 1024
FFN_ROWS_BWD = 1024


def _ffn_fwd(h2, wup_t, cw, wd):
    s = h2.shape[0]
    tr = FFN_ROWS_FWD
    nr = s // tr

    def body(h_ref, wu_ref, cg_ref, cv_ref, wd_ref, pg_ref, pv_ref, y_ref, halo_ref, act_ref):
        r, j = pl.program_id(0), pl.program_id(1)
        hv = h_ref[...]
        pg = _dot(hv, wu_ref[0], NT).astype(BF16)
        pv = _dot(hv, wu_ref[1], NT).astype(BF16)
        pg_ref[...] = pg
        pv_ref[...] = pv
        pgf, pvf = pg.astype(F32), pv.astype(F32)
        ug, _, _ = _conv_taps(pgf, jnp.where(r > 0, halo_ref[j, 0], 0.0), cg_ref[...])
        uv, _, _ = _conv_taps(pvf, jnp.where(r > 0, halo_ref[j, 1], 0.0), cv_ref[...])
        halo_ref[j, 0] = pgf[tr - SUB:tr, :]
        halo_ref[j, 1] = pvf[tr - SUB:tr, :]
        act = (ug * _sigmoid(ug) * uv).astype(BF16)
        for t in range(NJ):
            @pl.when(j == t)
            def _(t=t):
                act_ref[:, t * TF:(t + 1) * TF] = act

        @pl.when(j == NJ - 1)
        def _():
            y_ref[...] = _dot(act_ref[...], wd_ref[...])

    pre = jax.ShapeDtypeStruct((s, DFF), BF16)
    return pl.pallas_call(
        body, name="ffn_fwd", grid=(nr, NJ),
        in_specs=[pl.BlockSpec((tr, D), lambda r, j: (r, 0)),
                  pl.BlockSpec((2, TF, D), lambda r, j: (0, j, 0)),
                  pl.BlockSpec((3, TF), lambda r, j: (0, j)),
                  pl.BlockSpec((3, TF), lambda r, j: (0, NJ + j)),
                  pl.BlockSpec((DFF, D), lambda r, j: (0, 0))],
        out_specs=(pl.BlockSpec((tr, TF), lambda r, j: (r, j)),
                   pl.BlockSpec((tr, TF), lambda r, j: (r, j)),
                   pl.BlockSpec((tr, D), lambda r, j: (r, 0))),
        out_shape=(pre, pre, jax.ShapeDtypeStruct((s, D), F32)),
        scratch_shapes=[pltpu.VMEM((NJ, 2, SUB, TF), F32), pltpu.VMEM((tr, DFF), BF16)],
        compiler_params=_params(("arbitrary", "arbitrary"), 56),
    )(h2, wup_t, cw, cw, wd)


def _ffn_bwd(dy, h2, pre_g, pre_v, wup_t, cw, wd):
    s = h2.shape[0]
    tr = FFN_ROWS_BWD
    nr = s // tr
    hb = tr // (2 * SUB)

    def body(dy_ref, h_ref, pg_ref, pv_ref, hg_ref, hv_ref, wu_ref, cg_ref, cv_ref, wd_ref,
             dh_ref, dwu_ref, dwd_ref, dcg_ref, dcv_ref, nxt_ref, awu_ref, awd_ref):
        j, r = pl.program_id(0), pl.program_id(1)
        rr = nr - 1 - r
        row0 = pl.multiple_of(rr * tr, tr)
        cwg, cwv = cg_ref[...], cv_ref[...]
        pg, pv = pg_ref[...].astype(F32), pv_ref[...].astype(F32)
        ug, g1, g2 = _conv_taps(pg, jnp.where(rr > 0, hg_ref[SUB:2 * SUB, :].astype(F32), 0.0), cwg)
        uv, v1, v2 = _conv_taps(pv, jnp.where(rr > 0, hv_ref[SUB:2 * SUB, :].astype(F32), 0.0), cwv)
        sg = _sigmoid(ug)
        sil = ug * sg
        act = (sil * uv).astype(BF16)
        dyv = dy_ref[...]
        da = _dot(dyv, wd_ref[...], NT)
        dug = da * uv * (sg * (1.0 + ug * (1.0 - sg)))
        duv = da * sil
        dpg = _conv_taps_t(dug, jnp.where(r > 0, nxt_ref[0], 0.0), cwg)
        dpv = _conv_taps_t(duv, jnp.where(r > 0, nxt_ref[1], 0.0), cwv)
        nxt_ref[0] = dug[0:SUB, :]
        nxt_ref[1] = duv[0:SUB, :]
        dpgb, dpvb = dpg.astype(BF16), dpv.astype(BF16)
        hv = h_ref[...]
        dwd = _dot(act, dyv, TN)
        dpb = jnp.concatenate([dpgb, dpvb], axis=1)
        dwu = _dot(dpb, hv, TN)
        dh = _dot(dpb, wu_ref[...].reshape(2 * TF, D))

        def taps(du, x0, x1, x2):
            return (jnp.sum(du * x2, axis=0, keepdims=True), jnp.sum(du * x1, axis=0, keepdims=True),
                    jnp.sum(du * x0, axis=0, keepdims=True))

        tg, tv = taps(dug, pg, g1, g2), taps(duv, pv, v1, v2)

        @pl.when(r == 0)
        def _():
            awd_ref[...] = dwd
            awu_ref[...] = dwu
            dcg_ref[...] = jnp.zeros_like(dcg_ref)
            dcv_ref[...] = jnp.zeros_like(dcv_ref)

        @pl.when(r > 0)
        def _():
            awd_ref[...] += dwd
            awu_ref[...] += dwu

        @pl.when(r == nr - 1)
        def _():
            dwd_ref[...] = awd_ref[...].astype(BF16)
            dwu_ref[...] = awu_ref[...].astype(BF16).reshape(2, TF, D)

        for t in range(3):
            dcg_ref[t:t + 1, :] += tg[t]
            dcv_ref[t:t + 1, :] += tv[t]

        @pl.when(j == 0)
        def _():
            dh_ref[pl.ds(row0, tr), :] = dh

        @pl.when(j > 0)
        def _():
            dh_ref[pl.ds(row0, tr), :] += dh

    rows = lambda j, r: (nr - 1 - r, 0)
    tile = lambda j, r: (nr - 1 - r, j)
    halo = lambda j, r: (jnp.maximum((nr - 1 - r) * hb - 1, 0), j)
    return pl.pallas_call(
        body, name="ffn_bwd", grid=(NJ, nr),
        in_specs=[pl.BlockSpec((tr, D), rows), pl.BlockSpec((tr, D), rows),
                  pl.BlockSpec((tr, TF), tile), pl.BlockSpec((tr, TF), tile),
                  pl.BlockSpec((2 * SUB, TF), halo), pl.BlockSpec((2 * SUB, TF), halo),
                  pl.BlockSpec((2, TF, D), lambda j, r: (0, j, 0)),
                  pl.BlockSpec((3, TF), lambda j, r: (0, j)), pl.BlockSpec((3, TF), lambda j, r: (0, NJ + j)),
                  pl.BlockSpec((TF, D), lambda j, r: (j, 0))],
        out_specs=(pl.BlockSpec((s, D), lambda j, r: (0, 0)),
                   pl.BlockSpec((2, TF, D), lambda j, r: (0, j, 0)),
                   pl.BlockSpec((TF, D), lambda j, r: (j, 0)),
                   pl.BlockSpec((SUB, TF), lambda j, r: (0, j)), pl.BlockSpec((SUB, TF), lambda j, r: (0, j))),
        out_shape=(jax.ShapeDtypeStruct((s, D), F32),
                   jax.ShapeDtypeStruct((2, DFF, D), BF16), jax.ShapeDtypeStruct((DFF, D), BF16),
                   jax.ShapeDtypeStruct((SUB, DFF), F32), jax.ShapeDtypeStruct((SUB, DFF), F32)),
        scratch_shapes=[pltpu.VMEM((2, SUB, TF), F32), pltpu.VMEM((2 * TF, D), F32), pltpu.VMEM((TF, D), F32)],
        compiler_params=_params(("arbitrary", "arbitrary"), 56),
    )(dy, h2, pre_g, pre_v, pre_g, pre_v, wup_t, cw, cw, wd)


def _adam(w, g, m, v):
    m = ADAM_B1 * m + (1.0 - ADAM_B1) * g
    v = ADAM_B2 * v + (1.0 - ADAM_B2) * (g * g)
    m_hat = m / (1.0 - ADAM_B1 ** ADAM_STEP)
    v_hat = v / (1.0 - ADAM_B2 ** ADAM_STEP)
    delta = -ADAM_LR * (m_hat / (jnp.sqrt(v_hat) + ADAM_EPS) + ADAM_WD * w)
    return delta, m, v


NCHIP = NDEV // 2


def _pair_add(mine, theirs, tr, name):
    _, _, rws, cols = mine.shape

    def body(a_ref, b_ref, o_ref):
        c = lax.axis_index("c")
        o_ref[0] = (a_ref[0, c].astype(F32) + b_ref[0].astype(F32)).astype(BF16)

    (out,), _ = _call(
        body, (mine, theirs), name=name, grid=(NCHIP, rws // tr),
        in_specs=[pl.BlockSpec((1, 2, tr, cols), lambda q, i: (q, 0, i, 0)),
                  pl.BlockSpec((1, tr, cols), lambda q, i: (q, i, 0))],
        out_specs=[pl.BlockSpec((1, tr, cols), lambda q, i: (q, i, 0))],
        out_shape=[jax.ShapeDtypeStruct((NCHIP, rws, cols), BF16)], vmem_mb=32)
    return out


def _adamw_sharded(parts, w, m, v, tr, name, hosted=None):
    rws, cols = w.shape

    def body(p_ref, w_ref, m_ref, v_ref, g_ref, d_ref, mo_ref, vo_ref):
        g = p_ref[0].astype(F32)
        for q in range(1, NCHIP):
            g = g + p_ref[q].astype(F32)
        g_ref[...] = g
        d_ref[...], mo_ref[...], vo_ref[...] = _adam(w_ref[...], g, m_ref[...], v_ref[...])

    blk = pl.BlockSpec((tr, cols), lambda i: (i, 0))
    o = jax.ShapeDtypeStruct((rws, cols), F32)
    outs, moved = _call(
        body, (parts, w, m, v), name=name, grid=(rws // tr,),
        in_specs=[pl.BlockSpec((NCHIP, tr, cols), lambda i: (0, i, 0)), blk, blk, blk],
        out_specs=[blk, blk, blk, blk], out_shape=[o, o, o, o], vmem_mb=48, hosted=hosted)
    return (outs, moved) if hosted else outs


def _adamw_ada(c_all, dmod_my, w, m, v):
    rws, cols = w.shape
    tr = 256

    def body(c_ref, dm_ref, w_ref, m_ref, v_ref, g_ref, d_ref, mo_ref, vo_ref):
        cv = c_ref[...]
        act = cv * _sigmoid(cv)
        g = _dot(act, dm_ref[...], TN, lax.Precision.HIGHEST)
        g_ref[...] = g
        d_ref[...], mo_ref[...], vo_ref[...] = _adam(w_ref[...], g, m_ref[...], v_ref[...])

    blk = pl.BlockSpec((tr, cols), lambda i: (i, 0))
    o = jax.ShapeDtypeStruct((rws, cols), F32)
    return pl.pallas_call(
        body, name="adamw_ada", grid=(rws // tr,),
        in_specs=[pl.BlockSpec((NDEV, tr), lambda i: (0, i)), _full_spec((NDEV, cols)), blk, blk, blk],
        out_specs=(blk, blk, blk, blk), out_shape=(o, o, o, o),
        compiler_params=_params(("parallel",), 48),
    )(c_all, dmod_my, w, m, v)


REP_ROWS = 16
ROW_N1, ROW_N2, ROW_LOSS, ROW_MISC = 6, 7, 8, 9
LANE_BF, LANE_GQ, LANE_GK = 0, 128, 256


def _adamw_small(rep_all, conv_all, wmv):
    n_ff = wmv[6][0].shape[1]

    def body(*refs):
        rep_ref, conv_ref = refs[:2]
        ins = refs[2:2 + 24]
        outs = refs[2 + 24:]
        loss_ref, outs = outs[0], outs[1:]
        g_rep = rep_ref[0]
        g_conv = conv_ref[0]
        for d in range(1, NDEV):
            g_rep = g_rep + rep_ref[d]
            g_conv = g_conv + conv_ref[d]
        loss_ref[...] = (0.5 / D) * jnp.sum(g_rep[ROW_LOSS:ROW_LOSS + 1, :], axis=-1, keepdims=True)
        grads = [
            None,
            g_rep[ROW_N1:ROW_N1 + 1, :],
            g_rep[ROW_MISC:ROW_MISC + 1, LANE_BF:LANE_BF + HEADS],
            g_rep[ROW_MISC:ROW_MISC + 1, LANE_GQ:LANE_GQ + DH],
            g_rep[ROW_MISC:ROW_MISC + 1, LANE_GK:LANE_GK + DH],
            g_rep[ROW_N2:ROW_N2 + 1, :],
            g_conv[0:3, 0:n_ff],
            g_conv[0:3, n_ff:n_ff + DH],
        ]
        for p in range(8):
            w_ref, m_ref, v_ref = ins[3 * p:3 * p + 3]
            g_ref, d_ref, mo_ref, vo_ref = outs[4 * p:4 * p + 4]
            if p == 0:
                for nmod in range(NMOD):
                    sl = slice(D * nmod, D * (nmod + 1))
                    g = g_rep[nmod:nmod + 1, :]
                    g_ref[:, sl] = g
                    d_ref[:, sl], mo_ref[:, sl], vo_ref[:, sl] = _adam(w_ref[:, sl], g, m_ref[:, sl], v_ref[:, sl])
            else:
                g = grads[p]
                g_ref[...] = g
                d_ref[...], mo_ref[...], vo_ref[...] = _adam(w_ref[...], g, m_ref[...], v_ref[...])

    flat = [a for trio in wmv for a in trio]
    out_shape = [jax.ShapeDtypeStruct((1, 1), F32)]
    for trio in wmv:
        out_shape += [jax.ShapeDtypeStruct(trio[0].shape, F32)] * 4
    return pl.pallas_call(
        body, name="adamw_small", out_shape=tuple(out_shape),
        compiler_params=_params(None, 32),
    )(rep_all, conv_all, *flat)


FG_FIRST = 3 * AW
N_IN = DIN // NDEV


def _w_in_runs():
    runs = []
    for d in range(NDEV):
        lo, hi = N_IN * d, N_IN * (d + 1)
        for a, b, shift in ((0, FG_FIRST, 0), (FG_FIRST, FG_FIRST + HEADS, DIN - HEADS - FG_FIRST),
                            (FG_FIRST + HEADS, DIN, -HEADS)):
            a, b = max(a, lo), min(b, hi)
            if a < b:
                runs.append((d, a - lo, a + shift, b - a))
    return runs


W_IN_ROWS = 256
N_IN_PAD = 512


def _identity(n):
    return (lax.broadcasted_iota(jnp.int32, (n, n), 0) == lax.broadcasted_iota(jnp.int32, (n, n), 1)).astype(BF16)


def _assemble_w_in(g_in):
    def body(g_ref, o_ref, t_ref):
        eye = _identity(W_IN_ROWS)
        shard = None
        for d, src, dst, width in _w_in_runs():
            if d != shard:
                t_ref[:, 0:N_IN] = _dot(eye, g_ref[d], NT).astype(BF16)
                shard = d
            o_ref[:, dst:dst + width] = t_ref[:, src:src + width]
        o_ref[:, DIN:DINP] = jnp.zeros((W_IN_ROWS, DINP - DIN), o_ref.dtype)

    (out,), _ = _call(
        body, (g_in,), name="assemble_w_in", grid=(D // W_IN_ROWS,),
        in_specs=[pl.BlockSpec((NDEV, N_IN, W_IN_ROWS), lambda i: (0, 0, i))],
        out_specs=[pl.BlockSpec((W_IN_ROWS, DINP), lambda i: (i, 0))],
        out_shape=[jax.ShapeDtypeStruct((D, DINP), g_in.dtype)],
        scratch_shapes=[pltpu.VMEM((W_IN_ROWS, N_IN_PAD), BF16)], vmem_mb=32)
    return out


def _scatter_dw_in(dwp):
    def body(w_ref, o_ref, t_ref):
        eye = _identity(W_IN_ROWS)
        runs = _w_in_runs()
        for i, (d, src, dst, width) in enumerate(runs):
            t_ref[:, src:src + width] = w_ref[:, dst:dst + width]
            if i + 1 == len(runs) or runs[i + 1][0] != d:
                o_ref[d // 2, d % 2] = _dot(t_ref[:, 0:N_IN], eye, TN).astype(BF16)

    (out,), _ = _call(
        body, (dwp,), name="scatter_dw_in", grid=(D // W_IN_ROWS,),
        in_specs=[pl.BlockSpec((W_IN_ROWS, DINP), lambda i: (i, 0))],
        out_specs=[pl.BlockSpec((NCHIP, 2, N_IN, W_IN_ROWS), lambda i: (0, 0, 0, i))],
        out_shape=[jax.ShapeDtypeStruct((NCHIP, 2, N_IN, D), dwp.dtype)],
        scratch_shapes=[pltpu.VMEM((W_IN_ROWS, N_IN_PAD), BF16)], vmem_mb=32)
    return out


def kernel(x, c, w_ada, b_ada, norm1_g, w_in, b_forget, q_norm_g, k_norm_g, conv_mix_w, w_out, norm2_g, w_up, ffn_conv_w, w_down, loss_target, m_w_ada, m_b_ada, m_norm1_g, m_w_in, m_b_forget, m_q_norm_g, m_k_norm_g, m_conv_mix_w, m_w_out, m_norm2_g, m_w_up, m_ffn_conv_w, m_w_down, v_w_ada, v_b_ada, v_norm1_g, v_w_in, v_b_forget, v_q_norm_g, v_k_norm_g, v_conv_mix_w, v_w_out, v_norm2_g, v_w_up, v_ffn_conv_w, v_w_down):
    me = 4 * lax.axis_index("x") + 2 * lax.axis_index("y") + lax.axis_index("c")
    xs, tgt = x[0], loss_target[0]
    s = xs.shape[0]
    nq = s // TA
    n_ada = w_ada.shape[2]
    n_ff = w_up.shape[2]

    conv_w = jnp.concatenate([ffn_conv_w[0], conv_mix_w[0]], axis=1)
    conv_w = jnp.concatenate([conv_w, jnp.zeros((SUB - 3, conv_w.shape[1]), F32)], axis=0)
    c_all, conv_all, g_in = _exchange(
        [(c.reshape(SUB, D // SUB), "ag"), (conv_w, "ag"), (jnp.transpose(w_in[0]).astype(BF16), "ag2")],
        "exchange_w_in")
    c_all = c_all.reshape(NDEV, D)
    cw_ffn = jnp.transpose(conv_all[:, :3, :n_ff], (1, 0, 2)).reshape(3, 2 * DFF)
    cw_mix = jnp.transpose(conv_all[:, :3, n_ff:], (1, 0, 2)).reshape(3, CW)
    w_in_p = _assemble_w_in(g_in)

    b_my = lax.dynamic_slice(b_ada, (0, me * n_ada), (1, n_ada))
    mod_part = _ada_fwd(c_all, w_ada[0], b_my)
    (mod_rows,) = _exchange([(jnp.broadcast_to(mod_part[:, None, :], (NDEV, SUB, n_ada)), "a2a")], "exchange_mod")
    mod = mod_rows[:, 0, :].reshape(NMOD, D)
    mod = jnp.concatenate([mod, jnp.zeros((SUB - NMOD, D), F32)], axis=0)

    h = _norm_mod_fwd(xs, mod, norm1_g)
    proj, (g_down,) = _mm(h, w_in_p, "nn", F32, 1024, 640, "proj_fwd", hosted=[(w_down[0].astype(BF16), "ag2")])
    bf_pad = jnp.concatenate([b_forget, jnp.zeros((1, LANES - HEADS), F32)], axis=1)
    fcum = _fgate_fwd(proj, bf_pad)
    (qp, kp, vp), (g_out,) = _qkv_prep(proj, fcum, q_norm_g, k_norm_g, [(w_out[0].astype(BF16), "ag2")])
    attn, lse, (g_up,) = _attn_fwd(qp, kp, vp, [(jnp.transpose(w_up[0]).astype(BF16), "ag2")])
    w_out_f = g_out.reshape(D, D)
    w_up_t = g_up.reshape(2, DFF, D)
    w_down_f = g_down.reshape(DFF, D)
    conv = _mixconv_fwd(proj, cw_mix)
    mixed = jnp.concatenate([attn, conv], axis=1).astype(BF16)
    z = _mm(mixed, w_out_f, "nn", F32, 1024, 1024, "out_fwd")
    x1, h2 = _resid_norm2(xs, z, mod, norm2_g)
    pre_g, pre_v, y = _ffn_fwd(h2, w_up_t, cw_ffn, w_down_f)
    dout, dy, vec_l = _loss_head(x1, y, tgt, mod)

    dh2, dwup_t, dwd, dcw_g, dcw_v = _ffn_bwd(dy, h2, pre_g, pre_v, w_up_t, cw_ffn, w_down_f)
    dx1, dz, vec_2 = _norm_mod_bwd(dh2, x1, dout, z, mod, norm2_g, 3, 4, 2, "norm2_bwd")
    dwout = _mm(mixed, dz, "tn", BF16, 1024, 1024, "out_bwd_w")
    s_out = dwout.reshape(NCHIP, 2, D // NDEV, D)
    s_down = dwd.reshape(NCHIP, 2, DFF // NDEV, D)
    s_up = dwup_t.reshape(NCHIP, 2, n_ff, D)
    dmixed, (t_out, t_up, t_down) = _mm(dz, w_out_f, "nt", F32, 1024, 1024, "out_bwd_x",
                                        hosted=[(s_out, "pair"), (s_up, "pair"), (s_down, "pair")])
    c_out = _pair_add(s_out, t_out, 128, "pair_add_out")
    c_up = _pair_add(s_up, t_up, 176, "pair_add_up")
    c_down = _pair_add(s_down, t_down, 176, "pair_add_down")
    dxin, dbg, dcg, dcw_mix = _mixconv_bwd(dmixed, proj, cw_mix)
    (dqp, dkp, dvp), (p_up,) = _attn_bwd(qp, kp, vp, dmixed, attn, lse, [(c_up, "chips")])
    (dq, dk, dvb, dfcol, vec_qk), (p_out, p_down) = _qkv_post(
        dqp, dkp, dvp, proj, q_norm_g, k_norm_g, [(c_out, "chips"), (c_down, "chips")])
    dfg, vec_bf = _fgate_bwd(dfcol, proj, bf_pad)
    dproj = jnp.concatenate([dq, dk, dvb, dxin, dbg, dcg, dfg], axis=1)
    dwin_p = _mm(h, dproj, "tn", BF16, 1024, 640, "proj_bwd_w")
    s_in = _scatter_dw_in(dwin_p)
    (t_in,) = _exchange([(s_in, "pair")], "exchange_pair_in")
    c_in = _pair_add(s_in, t_in, N_IN, "pair_add_in")
    dh, (p_in,) = _mm(dproj, w_in_p, "nt", F32, 1024, 512, "proj_bwd_x", hosted=[(c_in, "chips")])
    grad_x, vec_1 = _norm_mod_bwd(dh, xs, dx1, None, mod, norm1_g, 0, 1, None, "norm1_bwd")

    misc = jnp.zeros((1, D), F32)
    misc = lax.dynamic_update_slice(misc, vec_bf[0:1, :HEADS], (0, LANE_BF))
    misc = lax.dynamic_update_slice(misc, vec_qk[0:1, :DH], (0, LANE_GQ))
    misc = lax.dynamic_update_slice(misc, vec_qk[1:2, :DH], (0, LANE_GK))
    rep = jnp.concatenate([
        vec_1[0:1], vec_1[1:2], vec_2[3:4], vec_2[0:1], vec_2[1:2], vec_l[0:1],
        vec_1[2:3], vec_2[2:3], vec_l[1:2], misc, jnp.zeros((REP_ROWS - 10, D), F32)], axis=0)
    dcw_ffn = jnp.concatenate([dcw_g, dcw_v], axis=1).reshape(SUB, NDEV, n_ff)
    dcw_all = jnp.concatenate([jnp.transpose(dcw_ffn, (1, 0, 2)),
                               jnp.transpose(dcw_mix.reshape(SUB, NDEV, DH), (1, 0, 2))], axis=2)
    r_out, (rep_all, conv_parts) = _adamw_sharded(p_out, w_out[0], m_w_out[0], v_w_out[0], 128, "adamw_out",
                                                  hosted=[(rep, "ag"), (dcw_all, "a2a")])
    dmod_my = lax.dynamic_slice(rep_all[:, :NMOD, :].reshape(NDEV, NMOD * D), (0, me * n_ada), (NDEV, n_ada))
    r_ada = _adamw_ada(c_all, dmod_my, w_ada[0], m_w_ada[0], v_w_ada[0])
    r_in = _adamw_sharded(p_in, jnp.transpose(w_in[0]), jnp.transpose(m_w_in[0]), jnp.transpose(v_w_in[0]), N_IN,
                          "adamw_in")
    r_in = tuple(jnp.transpose(a) for a in r_in)
    r_up = _adamw_sharded(p_up, jnp.transpose(w_up[0]), jnp.transpose(m_w_up[0]), jnp.transpose(v_w_up[0]), 176,
                          "adamw_up")
    r_up = tuple(jnp.transpose(a) for a in r_up)
    r_down = _adamw_sharded(p_down, w_down[0], m_w_down[0], v_w_down[0], 176, "adamw_down")
    small = _adamw_small(rep_all, conv_parts, [
        [b_ada, m_b_ada, v_b_ada], [norm1_g, m_norm1_g, v_norm1_g], [b_forget, m_b_forget, v_b_forget],
        [q_norm_g, m_q_norm_g, v_q_norm_g], [k_norm_g, m_k_norm_g, v_k_norm_g], [norm2_g, m_norm2_g, v_norm2_g],
        [ffn_conv_w[0], m_ffn_conv_w[0], v_ffn_conv_w[0]], [conv_mix_w[0], m_conv_mix_w[0], v_conv_mix_w[0]]])
    loss = small[0].reshape(())
    r_bada, r_n1, r_bf, r_gq, r_gk, r_n2, r_cf, r_cm = [small[1 + 4 * p:5 + 4 * p] for p in range(8)]
    lead = lambda t: tuple(a[None] for a in t)
    per_w = [lead(r_ada), r_bada, r_n1, lead(r_in), r_bf, r_gq, r_gk, lead(r_cm), lead(r_out), r_n2,
             lead(r_up), lead(r_cf), lead(r_down)]
    outs = [loss, grad_x[None]]
    for field in range(4):
        outs += [t[field] for t in per_w]
    return tuple(outs)
```

```python
import functools

import jax
import jax.numpy as jnp
import numpy as np
from jax import lax
from jax.experimental import pallas as pl
from jax.experimental.pallas import tpu as pltpu

F32 = jnp.float32
BF16 = jnp.bfloat16

NDEV = 8
D = 1024
HEADS = 8
DH = 64
AW = 512
CW = 512
DFF = 2816
DIN = 3080
DINP = 3200
NMOD = 6
EPS = 1e-6
QK_SCALE = 0.125
LANES = 128
SUB = 8

ADAM_LR = 0.001
ADAM_B1 = 0.9
ADAM_B2 = 0.999
ADAM_EPS = 1e-08
ADAM_WD = 0.01
ADAM_STEP = 10

MESH = pl.DeviceIdType.MESH
ANY = pl.BlockSpec(memory_space=pl.ANY)

NN = (((1,), (0,)), ((), ()))
NT = (((1,), (1,)), ((), ()))
TN = (((0,), (0,)), ((), ()))


def _dot(a, b, dims=NN, precision=None):
    return lax.dot_general(a, b, dims, precision=precision, preferred_element_type=F32)


VMEM_LIMIT_MB = 60


def _params(sem=None, vmem_mb=None):
    del vmem_mb
    kw = {"vmem_limit_bytes": VMEM_LIMIT_MB * 1024 * 1024}
    if sem is not None:
        kw["dimension_semantics"] = sem
    return pltpu.CompilerParams(**kw)


def _sigmoid(x):
    return 0.5 * jnp.tanh(0.5 * x) + 0.5


class _Exchange:
    def __init__(self, items):
        self.arrays = [a for a, _ in items]
        self.modes = [m for _, m in items]
        self.n = len(items)
        self.out_shape = []
        for a, m in items:
            sh = {"ag": (NDEV,) + a.shape, "ag2": (NDEV,) + a.shape, "pair": a.shape[:1] + a.shape[2:]}.get(m, a.shape)
            self.out_shape.append(jax.ShapeDtypeStruct(sh, a.dtype))
        self.scratch = [pltpu.SemaphoreType.DMA((self.n, NDEV - 1)), pltpu.SemaphoreType.DMA((self.n, NDEV - 1)),
                        pltpu.SemaphoreType.DMA((self.n,))]

    def _plan(self, srcs, outs, sems):
        send_sems, recv_sems, loc_sems = sems
        x, y, c = lax.axis_index("x"), lax.axis_index("y"), lax.axis_index("c")
        me, my_chip = 4 * x + 2 * y + c, 2 * x + y
        sib = (x, y, 1 - c)
        local, first, landed, forwards, arrivals = [], [], [], [], []

        def remote(a, k, src, dst, to):
            return pltpu.make_async_remote_copy(src_ref=src, dst_ref=dst, send_sem=send_sems.at[a, k],
                                                recv_sem=recv_sems.at[a, k], device_id=to, device_id_type=MESH)

        for a, mode in enumerate(self.modes):
            src, out = srcs[a], outs[a]
            if mode in ("ag", "a2a"):
                piece = (lambda slot, src=src: src) if mode == "ag" else (lambda slot, src=src: src.at[slot])
                local.append(pltpu.make_async_copy(piece(me), out.at[me], loc_sems.at[a]))
                for r in range(1, NDEV):
                    px = 1 - x if (r >> 2) & 1 else x
                    py = 1 - y if (r >> 1) & 1 else y
                    pc = 1 - c if r & 1 else c
                    pidx = 4 * px + 2 * py + pc
                    first.append(remote(a, r - 1, piece(pidx), out.at[me], (px, py, pc)))
                    arrivals.append(remote(a, r - 1, piece(pidx), out.at[pidx], (px, py, pc)))
            elif mode == "ag2":
                local.append(pltpu.make_async_copy(src, out.at[me], loc_sems.at[a]))
                first.append(remote(a, 0, src, out.at[me], sib))
                arrivals.append(remote(a, 0, src, out.at[me + 1 - 2 * c], sib))
                for j, (px, py) in enumerate([(1 - x, y), (x, 1 - y), (1 - x, 1 - y)]):
                    theirs = out.at[4 * px + 2 * py + c]
                    first.append(remote(a, 1 + j, src, out.at[me], (px, py, c)))
                    landed.append(remote(a, 1 + j, src, theirs, (px, py, c)))
                    forwards.append(remote(a, 4 + j, theirs, theirs, sib))
                    arrivals.append(remote(a, 4 + j, src, out.at[4 * px + 2 * py + 1 - c], sib))
            elif mode == "pair":
                for q in range(NDEV // 2):
                    first.append(remote(a, q, src.at[q, 1 - c], out.at[q], sib))
                    arrivals.append(remote(a, q, src.at[q, 1 - c], out.at[q], sib))
            else:
                assert mode == "chips", mode
                local.append(pltpu.make_async_copy(src.at[my_chip], out.at[my_chip], loc_sems.at[a]))
                for j, (px, py) in enumerate([(1 - x, y), (x, 1 - y), (1 - x, 1 - y)]):
                    q = 2 * px + py
                    first.append(remote(a, 1 + j, src.at[q], out.at[my_chip], (px, py, c)))
                    arrivals.append(remote(a, 1 + j, src.at[q], out.at[q], (px, py, c)))
        return local, first, landed, forwards, arrivals

    def start(self, srcs, outs, sems):
        local, first, _, _, _ = self._plan(srcs, outs, sems)
        for cp in local + first:
            cp.start()

    def wait(self, srcs, outs, sems):
        local, first, landed, forwards, arrivals = self._plan(srcs, outs, sems)
        for cp, fwd in zip(landed, forwards):
            cp.wait_recv()
            fwd.start()
        for cp in arrivals:
            cp.wait_recv()
        for cp in first + forwards:
            cp.wait_send()
        for cp in local:
            cp.wait()


def _exchange(items, name):
    ex = _Exchange(items)
    n = ex.n

    def body(*refs):
        srcs, outs, sems = refs[:n], refs[n:2 * n], refs[2 * n:]
        ex.start(srcs, outs, sems)
        ex.wait(srcs, outs, sems)

    return pl.pallas_call(
        body, name=name,
        out_shape=tuple(ex.out_shape),
        in_specs=[ANY] * n, out_specs=tuple([ANY] * n),
        scratch_shapes=ex.scratch,
        compiler_params=pltpu.CompilerParams(has_side_effects=True),
    )(*ex.arrays)


def _call(body, inputs, *, name, grid, in_specs, out_specs, out_shape, scratch_shapes=(), vmem_mb=None, hosted=None):
    out_specs, out_shape, scratch_shapes = tuple(out_specs), tuple(out_shape), list(scratch_shapes)
    if not hosted:
        res = pl.pallas_call(
            body, name=name, grid=grid, in_specs=list(in_specs), out_specs=out_specs, out_shape=out_shape,
            scratch_shapes=scratch_shapes, compiler_params=_params(("arbitrary",) * len(grid), vmem_mb),
        )(*inputs)
        return tuple(res), ()
    ex = _Exchange(hosted)
    n, n_in, n_out, n_scr = ex.n, len(inputs), len(out_shape), len(scratch_shapes)

    def hosting_body(*refs):
        ins, srcs = refs[:n_in], refs[n_in:n_in + n]
        outs, landing = refs[n_in + n:n_in + n + n_out], refs[n_in + n + n_out:n_in + 2 * n + n_out]
        scratch, sems = refs[n_in + 2 * n + n_out:n_in + 2 * n + n_out + n_scr], refs[n_in + 2 * n + n_out + n_scr:]
        first = functools.reduce(jnp.logical_and, [pl.program_id(d) == 0 for d in range(len(grid))])
        last = functools.reduce(jnp.logical_and, [pl.program_id(d) == grid[d] - 1 for d in range(len(grid))])

        @pl.when(first)
        def _():
            ex.start(srcs, landing, sems)

        body(*ins, *outs, *scratch)

        @pl.when(last)
        def _():
            ex.wait(srcs, landing, sems)

    res = pl.pallas_call(
        hosting_body, name=name, grid=grid,
        in_specs=list(in_specs) + [ANY] * n, out_specs=out_specs + tuple([ANY] * n),
        out_shape=out_shape + tuple(ex.out_shape), scratch_shapes=scratch_shapes + ex.scratch,
        compiler_params=_params(("arbitrary",) * len(grid), vmem_mb),
    )(*inputs, *ex.arrays)
    return tuple(res[:n_out]), tuple(res[n_out:])


def _mm(a, b, mode, out_dtype, tm, tn, name, hosted=None):
    if mode == "nn":
        (m, k), n = a.shape, b.shape[1]
        a_spec = pl.BlockSpec((tm, k), lambda i, j: (i, 0))
        b_spec = pl.BlockSpec((k, tn), lambda i, j: (0, j))
        dims = NN
    elif mode == "nt":
        (m, k), n = a.shape, b.shape[0]
        a_spec = pl.BlockSpec((tm, k), lambda i, j: (i, 0))
        b_spec = pl.BlockSpec((tn, k), lambda i, j: (j, 0))
        dims = NT
    else:
        (k, m), n = a.shape, b.shape[1]
        a_spec = pl.BlockSpec((k, tm), lambda i, j: (0, i))
        b_spec = pl.BlockSpec((k, tn), lambda i, j: (0, j))
        dims = TN
    assert m % tm == 0 and n % tn == 0, (m, n, tm, tn)

    def body(a_ref, b_ref, o_ref):
        o_ref[...] = _dot(a_ref[...], b_ref[...], dims).astype(o_ref.dtype)

    (out,), moved = _call(
        body, (a, b), name=name, grid=(m // tm, n // tn),
        in_specs=[a_spec, b_spec], out_specs=[pl.BlockSpec((tm, tn), lambda i, j: (i, j))],
        out_shape=[jax.ShapeDtypeStruct((m, n), out_dtype)], vmem_mb=48, hosted=hosted)
    return (out, moved) if hosted else out


def _shift_down(x, k, fill):
    y = pltpu.roll(x, k, 0)
    row = lax.broadcasted_iota(jnp.int32, (SUB, x.shape[1]), 0)
    head = y[0:SUB, :]
    for t in range(k):
        head = jnp.where(row == t, fill[t], head)
    return jnp.concatenate([head, y[SUB:, :]], axis=0)


def _shift_up(x, k, fill):
    n = x.shape[0]
    y = pltpu.roll(x, n - k, 0)
    row = lax.broadcasted_iota(jnp.int32, (SUB, x.shape[1]), 0)
    tail = y[n - SUB:, :]
    for t in range(k):
        tail = jnp.where(row == SUB - k + t, fill[t], tail)
    return jnp.concatenate([y[:n - SUB, :], tail], axis=0)


def _conv_taps(x, halo, w):
    if halo is None:
        f1, f2 = [0.0], [0.0, 0.0]
    else:
        f1, f2 = [halo[7:8, :]], [halo[6:7, :], halo[7:8, :]]
    s1 = _shift_down(x, 1, f1)
    s2 = _shift_down(x, 2, f2)
    u = w[2:3, :] * x + w[1:2, :] * s1 + w[0:1, :] * s2
    return u, s1, s2


def _conv_taps_t(du, nxt, w):
    if nxt is None:
        f1, f2 = [0.0], [0.0, 0.0]
    else:
        f1, f2 = [nxt[0:1, :]], [nxt[0:1, :], nxt[1:2, :]]
    return w[2:3, :] * du + w[1:2, :] * _shift_up(du, 1, f1) + w[0:1, :] * _shift_up(du, 2, f2)


def _ada_fwd(c_all, w_ada, b_my):
    def body(c_ref, w_ref, b_ref, o_ref):
        cv = c_ref[...]
        act = cv * _sigmoid(cv)
        o_ref[...] = _dot(act, w_ref[...], NN, lax.Precision.HIGHEST) + b_ref[...]

    return pl.pallas_call(
        body, name="ada_fwd",
        out_shape=jax.ShapeDtypeStruct((NDEV, w_ada.shape[1]), F32),
        compiler_params=_params(None, 32),
    )(c_all, w_ada, b_my)


TR = 256


def _row_spec(width, col=0):
    return pl.BlockSpec((TR, width), lambda i, col=col: (i, col))


def _full_spec(shape):
    return pl.BlockSpec(shape, lambda i: (0,) * len(shape))


def _norm_mod_fwd(x, mod, g):
    s = x.shape[0]

    def body(x_ref, mod_ref, g_ref, h_ref):
        xv = x_ref[...]
        r = lax.rsqrt(jnp.mean(xv * xv, axis=-1, keepdims=True) + EPS)
        nrm = xv * r * g_ref[...]
        h_ref[...] = (nrm * (1.0 + mod_ref[1:2, :]) + mod_ref[0:1, :]).astype(BF16)

    return pl.pallas_call(
        body, name="norm1_fwd", grid=(s // TR,),
        in_specs=[_row_spec(D), _full_spec((SUB, D)), _full_spec((1, D))],
        out_specs=_row_spec(D), out_shape=jax.ShapeDtypeStruct((s, D), BF16),
        compiler_params=_params(("parallel",)),
    )(x, mod, g)


SLAB = 2 * DH
AUG_F, AUG_ONE, AUG_LSE = 0, 3, 6


def _split3(x):
    hi = x.astype(BF16).astype(F32)
    r1 = x - hi
    mid = r1.astype(BF16).astype(F32)
    return hi, mid, r1 - mid


def _lanes3(lane, first, pieces, other):
    out = other
    for k in range(3):
        out = jnp.where(lane == first + k, pieces[k], out)
    return out


def _aug_placement():
    eq = np.zeros((3 * LANES, HEADS * SLAB), np.float32)
    ek = np.zeros((3 * LANES, HEADS * SLAB), np.float32)
    ones = np.zeros((SUB, HEADS * SLAB), np.float32)
    for h in range(HEADS):
        aug = SLAB * h + DH
        for k in range(3):
            eq[LANES * k + h, aug + AUG_F + k] = 1.0
            ek[LANES * k + h, aug + AUG_ONE + k] = -1.0
            ones[0, aug + AUG_ONE + k] = 1.0
            ones[1, aug + AUG_F + k] = ones[1, aug + AUG_LSE + k] = 1.0
            ones[2, aug + k] = 1.0
    return jnp.asarray(eq, BF16), jnp.asarray(ek, BF16), jnp.asarray(ones)


def _qkv_prep(proj, fcum, gq, gk, hosted):
    s = proj.shape[0]

    def body(q_ref, k_ref, v_ref, f_ref, gq_ref, gk_ref, eq_ref, ek_ref, ones_ref, qo_ref, ko_ref, vo_ref):
        f3 = jnp.concatenate(_split3(f_ref[...]), axis=1).astype(BF16)
        qo_ref[...] = (_dot(f3, eq_ref[...]) + ones_ref[0:1, :]).astype(BF16)
        ko_ref[...] = (_dot(f3, ek_ref[...]) + ones_ref[1:2, :]).astype(BF16)
        vo_ref[...] = jnp.broadcast_to(ones_ref[2:3, :], vo_ref.shape).astype(BF16)
        for h in range(HEADS):
            sl = slice(DH * h, DH * (h + 1))
            lo = slice(SLAB * h, SLAB * h + DH)
            qh = q_ref[:, sl]
            r = lax.rsqrt(jnp.mean(qh * qh, axis=-1, keepdims=True) + EPS)
            qo_ref[:, lo] = (qh * r * gq_ref[...] * QK_SCALE).astype(BF16)
            kh = k_ref[:, sl]
            r = lax.rsqrt(jnp.mean(kh * kh, axis=-1, keepdims=True) + EPS)
            ko_ref[:, lo] = (kh * r * gk_ref[...]).astype(BF16)
            vo_ref[:, lo] = v_ref[:, sl].astype(BF16)

    eq, ek, ones = _aug_placement()
    o = jax.ShapeDtypeStruct((s, HEADS * SLAB), BF16)
    wide = _row_spec(HEADS * SLAB)
    return _call(
        body, (proj, proj, proj, fcum, gq, gk, eq, ek, ones), name="qkv_prep", grid=(s // TR,),
        in_specs=[_row_spec(AW, 0), _row_spec(AW, 1), _row_spec(AW, 2), _row_spec(LANES),
                  _full_spec((1, DH)), _full_spec((1, DH)), _full_spec(eq.shape), _full_spec(ek.shape),
                  _full_spec(ones.shape)],
        out_specs=[wide, wide, wide], out_shape=[o, o, o], vmem_mb=32, hosted=hosted)


FG_BLOCK = (3 * AW + 3 * CW) // LANES


def _fgate_fwd(proj, bf_pad):
    s = proj.shape[0]

    def body(fg_ref, b_ref, o_ref, carry_ref):
        i = pl.program_id(0)

        @pl.when(i == 0)
        def _():
            carry_ref[...] = jnp.zeros_like(carry_ref)

        z = fg_ref[...] + b_ref[...]
        logf = jnp.minimum(z, 0.0) - jnp.log1p(jnp.exp(-jnp.abs(z)))
        row = lax.broadcasted_iota(jnp.int32, (TR, TR), 0)
        col = lax.broadcasted_iota(jnp.int32, (TR, TR), 1)
        tri = (col <= row).astype(F32)
        cs = _dot(tri, logf, NN, lax.Precision.HIGHEST) + carry_ref[0:1, :]
        o_ref[...] = cs
        carry_ref[...] = jnp.broadcast_to(cs[TR - 1:TR, :], carry_ref.shape)

    return pl.pallas_call(
        body, name="fgate_fwd", grid=(s // TR,),
        in_specs=[_row_spec(LANES, FG_BLOCK), _full_spec((1, LANES))],
        out_specs=_row_spec(LANES), out_shape=jax.ShapeDtypeStruct((s, LANES), F32),
        scratch_shapes=[pltpu.VMEM((SUB, LANES), F32)],
        compiler_params=_params(("arbitrary",)),
    )(proj, bf_pad)


def _fgate_bwd(dfcol, proj, bf_pad):
    s = proj.shape[0]
    nb = s // TR

    def body(df_ref, fg_ref, b_ref, o_ref, db_ref, carry_ref):
        i = pl.program_id(0)

        @pl.when(i == 0)
        def _():
            carry_ref[...] = jnp.zeros_like(carry_ref)
            db_ref[...] = jnp.zeros_like(db_ref)

        row = lax.broadcasted_iota(jnp.int32, (TR, TR), 0)
        col = lax.broadcasted_iota(jnp.int32, (TR, TR), 1)
        tri = (col >= row).astype(F32)
        dlogf = _dot(tri, df_ref[...], NN, lax.Precision.HIGHEST) + carry_ref[0:1, :]
        carry_ref[...] = jnp.broadcast_to(dlogf[0:1, :], carry_ref.shape)
        z = fg_ref[...] + b_ref[...]
        dfg = dlogf * _sigmoid(-z)
        o_ref[...] = dfg.astype(BF16)
        db_ref[0:1, :] += jnp.sum(dfg, axis=0, keepdims=True)

    rev = lambda col: pl.BlockSpec((TR, LANES), lambda i, col=col: (nb - 1 - i, col))
    return pl.pallas_call(
        body, name="fgate_bwd", grid=(nb,),
        in_specs=[rev(0), rev(FG_BLOCK), _full_spec((1, LANES))],
        out_specs=(rev(0), _full_spec((SUB, LANES))),
        out_shape=(jax.ShapeDtypeStruct((s, LANES), BF16), jax.ShapeDtypeStruct((SUB, LANES), F32)),
        scratch_shapes=[pltpu.VMEM((SUB, LANES), F32)],
        compiler_params=_params(("arbitrary",)),
    )(dfcol, proj, bf_pad)


def _resid_norm2(x, z, mod, g):
    s = x.shape[0]

    def body(x_ref, z_ref, mod_ref, g_ref, x1_ref, h_ref):
        x1 = x_ref[...] + mod_ref[2:3, :] * z_ref[...]
        x1_ref[...] = x1
        r = lax.rsqrt(jnp.mean(x1 * x1, axis=-1, keepdims=True) + EPS)
        nrm = x1 * r * g_ref[...]
        h_ref[...] = (nrm * (1.0 + mod_ref[4:5, :]) + mod_ref[3:4, :]).astype(BF16)

    return pl.pallas_call(
        body, name="resid_norm2", grid=(s // TR,),
        in_specs=[_row_spec(D), _row_spec(D), _full_spec((SUB, D)), _full_spec((1, D))],
        out_specs=(_row_spec(D), _row_spec(D)),
        out_shape=(jax.ShapeDtypeStruct((s, D), F32), jax.ShapeDtypeStruct((s, D), BF16)),
        compiler_params=_params(("parallel",)),
    )(x, z, mod, g)


def _loss_head(x1, y, tgt, mod):
    s = x1.shape[0]

    def body(x1_ref, y_ref, t_ref, mod_ref, dout_ref, dy_ref, vec_ref):
        @pl.when(pl.program_id(0) == 0)
        def _():
            vec_ref[...] = jnp.zeros_like(vec_ref)

        yv = y_ref[...]
        g2 = mod_ref[5:6, :]
        diff = x1_ref[...] + g2 * yv - t_ref[...]
        dout = diff * (1.0 / D)
        dout_ref[...] = dout
        dy_ref[...] = (g2 * dout).astype(BF16)
        vec_ref[0:1, :] += jnp.sum(dout * yv, axis=0, keepdims=True)
        vec_ref[1:2, :] += jnp.sum(diff * diff, axis=0, keepdims=True)

    return pl.pallas_call(
        body, name="loss_head", grid=(s // TR,),
        in_specs=[_row_spec(D), _row_spec(D), _row_spec(D), _full_spec((SUB, D))],
        out_specs=(_row_spec(D), _row_spec(D), _full_spec((SUB, D))),
        out_shape=(jax.ShapeDtypeStruct((s, D), F32), jax.ShapeDtypeStruct((s, D), BF16),
                   jax.ShapeDtypeStruct((SUB, D), F32)),
        compiler_params=_params(("arbitrary",)),
    )(x1, y, tgt, mod)


def _norm_mod_bwd(dh, xin, dres, zin, mod, g, shift_row, scale_row, gate_row, name, hosted=None):
    s = dh.shape[0]
    with_gate = gate_row is not None

    def body(*refs):
        if with_gate:
            dh_ref, x_ref, dres_ref, z_ref, mod_ref, g_ref, dx_ref, dz_ref, vec_ref = refs
        else:
            dh_ref, x_ref, dres_ref, mod_ref, g_ref, dx_ref, vec_ref = refs

        @pl.when(pl.program_id(0) == 0)
        def _():
            vec_ref[...] = jnp.zeros_like(vec_ref)

        xv = x_ref[...]
        dhv = dh_ref[...]
        gv = g_ref[...]
        r = lax.rsqrt(jnp.mean(xv * xv, axis=-1, keepdims=True) + EPS)
        xh = xv * r
        dn = dhv * (1.0 + mod_ref[scale_row:scale_row + 1, :])
        dxh = dn * gv
        dx = dres_ref[...] + r * (dxh - xh * jnp.mean(dxh * xh, axis=-1, keepdims=True))
        dx_ref[...] = dx
        vec_ref[0:1, :] += jnp.sum(dhv, axis=0, keepdims=True)
        vec_ref[1:2, :] += jnp.sum(dhv * (xh * gv), axis=0, keepdims=True)
        vec_ref[2:3, :] += jnp.sum(dn * xh, axis=0, keepdims=True)
        if with_gate:
            dz_ref[...] = (mod_ref[gate_row:gate_row + 1, :] * dx).astype(BF16)
            vec_ref[3:4, :] += jnp.sum(dx * z_ref[...], axis=0, keepdims=True)

    ins = [dh, xin, dres] + ([zin] if with_gate else []) + [mod, g]
    in_specs = [_row_spec(D)] * (4 if with_gate else 3) + [_full_spec((SUB, D)), _full_spec((1, D))]
    out_specs = [_row_spec(D)] + ([_row_spec(D)] if with_gate else []) + [_full_spec((SUB, D))]
    out_shape = [jax.ShapeDtypeStruct((s, D), F32)] + ([jax.ShapeDtypeStruct((s, D), BF16)] if with_gate else []) \
        + [jax.ShapeDtypeStruct((SUB, D), F32)]
    outs, moved = _call(body, ins, name=name, grid=(s // TR,), in_specs=in_specs, out_specs=out_specs,
                        out_shape=out_shape, hosted=hosted)
    return outs + (moved,) if hosted else outs


XIN_BLOCK = 3 * AW // LANES
BG_BLOCK = XIN_BLOCK + CW // LANES
CG_BLOCK = BG_BLOCK + CW // LANES


def _seq_spec(s, first_block):
    return pl.BlockSpec((s, LANES), lambda j, fb=first_block: (0, fb + j))


def _mixconv_fwd(proj, w):
    s = proj.shape[0]

    def body(xin_ref, bg_ref, cg_ref, w_ref, o_ref):
        cx = cg_ref[...] * xin_ref[...]
        cv, _, _ = _conv_taps(cx, None, w_ref[...])
        o_ref[...] = bg_ref[...] * cv

    return pl.pallas_call(
        body, name="mixconv_fwd", grid=(CW // LANES,),
        in_specs=[_seq_spec(s, XIN_BLOCK), _seq_spec(s, BG_BLOCK), _seq_spec(s, CG_BLOCK),
                  pl.BlockSpec((3, LANES), lambda j: (0, j))],
        out_specs=_seq_spec(s, 0), out_shape=jax.ShapeDtypeStruct((s, CW), F32),
        compiler_params=_params(("parallel",), 48),
    )(proj, proj, proj, w)


def _mixconv_bwd(dmixed, proj, w):
    s = proj.shape[0]

    def body(d_ref, xin_ref, bg_ref, cg_ref, w_ref, dxin_ref, dbg_ref, dcg_ref, dw_ref):
        wv = w_ref[...]
        xin, cg, dconv = xin_ref[...], cg_ref[...], d_ref[...]
        cx = cg * xin
        cv, s1, s2 = _conv_taps(cx, None, wv)
        dbg_ref[...] = (dconv * cv).astype(BF16)
        dcv = dconv * bg_ref[...]
        dw_ref[...] = jnp.zeros_like(dw_ref)
        dw_ref[0:1, :] = jnp.sum(dcv * s2, axis=0, keepdims=True)
        dw_ref[1:2, :] = jnp.sum(dcv * s1, axis=0, keepdims=True)
        dw_ref[2:3, :] = jnp.sum(dcv * cx, axis=0, keepdims=True)
        dcx = _conv_taps_t(dcv, None, wv)
        dcg_ref[...] = (dcx * xin).astype(BF16)
        dxin_ref[...] = (dcx * cg).astype(BF16)

    o = jax.ShapeDtypeStruct((s, CW), BF16)
    return pl.pallas_call(
        body, name="mixconv_bwd", grid=(CW // LANES,),
        in_specs=[_seq_spec(s, AW // LANES), _seq_spec(s, XIN_BLOCK), _seq_spec(s, BG_BLOCK), _seq_spec(s, CG_BLOCK),
                  pl.BlockSpec((3, LANES), lambda j: (0, j))],
        out_specs=(_seq_spec(s, 0), _seq_spec(s, 0), _seq_spec(s, 0), pl.BlockSpec((SUB, LANES), lambda j: (0, j))),
        out_shape=(o, o, o, jax.ShapeDtypeStruct((SUB, CW), F32)),
        compiler_params=_params(("parallel",), 48),
    )(dmixed, proj, proj, proj, w)


TA = 512
NEG = -1e30


def _causal_mask():
    row = lax.broadcasted_iota(jnp.int32, (TA, TA), 0)
    col = lax.broadcasted_iota(jnp.int32, (TA, TA), 1)
    return col <= row


def _attn_fwd(qp, kp, vp, hosted):
    s = qp.shape[0]
    nq = s // TA

    def body(q_ref, k_ref, v_ref, o_ref, lse_ref):
        i = pl.program_id(1)
        slabs = [slice(SLAB * hh, SLAB * (hh + 1)) for hh in range(2)]
        q = [q_ref[:, sl] for sl in slabs]

        def block(j, carry, masked):
            keys = pl.ds(pl.multiple_of(j * TA, TA), TA)
            ms, acc = carry
            m_out, parts = [], []
            for hh in range(2):
                sc = _dot(q[hh], k_ref[keys, slabs[hh]], NT)
                if masked:
                    sc = jnp.where(_causal_mask(), sc, NEG)
                m_new = jnp.maximum(ms[hh], jnp.max(sc, axis=-1, keepdims=True))
                p = jnp.exp(sc - m_new)
                parts.append(jnp.exp(ms[hh] - m_new) * acc[:, slabs[hh]] + _dot(p.astype(BF16), v_ref[keys, slabs[hh]]))
                m_out.append(m_new)
            return tuple(m_out), jnp.concatenate(parts, axis=1)

        init = ((jnp.full((TA, 1), NEG, F32), jnp.full((TA, 1), NEG, F32)), jnp.zeros((TA, 2 * SLAB), F32))
        carry = lax.fori_loop(0, i, lambda j, cr: block(j, cr, False), init)
        ms, acc = block(i, carry, True)
        for hh in range(2):
            l = acc[:, SLAB * hh + DH:SLAB * hh + DH + 1]
            o_ref[:, DH * hh:DH * (hh + 1)] = acc[:, SLAB * hh:SLAB * hh + DH] / l
            lse_ref[0, :, hh:hh + 1] = ms[hh] + jnp.log(l)

    (o, lse), moved = _call(
        body, (qp, kp, vp), name="attn_fwd", grid=(HEADS // 2, nq),
        in_specs=[pl.BlockSpec((TA, 2 * SLAB), lambda p, i: (i, p)),
                  pl.BlockSpec((s, 2 * SLAB), lambda p, i: (0, p)),
                  pl.BlockSpec((s, 2 * SLAB), lambda p, i: (0, p))],
        out_specs=[pl.BlockSpec((TA, LANES), lambda p, i: (i, p)), pl.BlockSpec((1, TA, 2), lambda p, i: (p, i, 0))],
        out_shape=[jax.ShapeDtypeStruct((s, AW), F32), jax.ShapeDtypeStruct((HEADS // 2, s, 2), F32)],
        vmem_mb=32, hosted=hosted)
    return o, lse, moved


def _attn_bwd(qp, kp, vp, dmixed, o, lse, hosted):
    s = qp.shape[0]
    nq = s // TA

    def body(q_ref, k_ref, v_ref, do_ref, o_ref, lse_ref, dq_ref, dk_ref, dv_ref, qb_ref, dob_ref):
        dk_ref[...] = jnp.zeros_like(dk_ref)
        dv_ref[...] = jnp.zeros_like(dv_ref)
        slabs = [slice(SLAB * hh, SLAB * (hh + 1)) for hh in range(2)]
        lane = lax.broadcasted_iota(jnp.int32, (TA, DH), 1)

        def q_block(i, _):
            i0 = pl.multiple_of(i * TA, TA)
            rows = pl.ds(i0, TA)
            for hh in range(2):
                half = slice(DH * hh, DH * (hh + 1))
                do = do_ref[rows, half]
                delta = jnp.sum(do * o_ref[rows, half], axis=-1, keepdims=True)
                dob_ref[hh, :, 0:DH] = do.astype(BF16)
                dob_ref[hh, :, DH:SLAB] = _lanes3(lane, 0, [-d for d in _split3(delta)], 0.0).astype(BF16)
                lse3 = _split3(lse_ref[0, rows, hh:hh + 1])
                qb_ref[hh, :, 0:DH] = q_ref[rows, SLAB * hh:SLAB * hh + DH]
                aug = q_ref[rows, SLAB * hh + DH:SLAB * (hh + 1)].astype(F32)
                qb_ref[hh, :, DH:SLAB] = _lanes3(lane, AUG_LSE, [-x for x in lse3], aug).astype(BF16)

            def block(j, dq, masked):
                keys = pl.ds(pl.multiple_of(j * TA, TA), TA)
                dv, dk, dqc = [], [], []
                for hh in range(2):
                    q, dob = qb_ref[hh], dob_ref[hh]
                    k = k_ref[keys, slabs[hh]]
                    sc = _dot(q, k, NT)
                    if masked:
                        sc = jnp.where(_causal_mask(), sc, NEG)
                    p = jnp.exp(sc)
                    dv.append(_dot(p.astype(BF16), dob, TN))
                    ds = (p * _dot(dob, v_ref[keys, slabs[hh]], NT)).astype(BF16)
                    dk.append(_dot(ds, q, TN))
                    dqc.append(_dot(ds, k))
                dv_ref[keys, :] += jnp.concatenate(dv, axis=1)
                dk_ref[keys, :] += jnp.concatenate(dk, axis=1)
                return dq + jnp.concatenate(dqc, axis=1)

            dq = lax.fori_loop(0, i, lambda j, acc: block(j, acc, False), jnp.zeros((TA, 2 * SLAB), F32))
            dq_ref[rows, :] = block(i, dq, True)
            return 0

        lax.fori_loop(0, nq, q_block, 0)

    pair = lambda p: (0, p)
    slab2 = pl.BlockSpec((s, 2 * SLAB), pair)
    seq = pl.BlockSpec((s, LANES), pair)
    small = pl.BlockSpec((1, s, 2), lambda p: (p, 0, 0))
    o32 = jax.ShapeDtypeStruct((s, HEADS * SLAB), F32)
    return _call(
        body, (qp, kp, vp, dmixed, o, lse), name="attn_bwd", grid=(HEADS // 2,),
        in_specs=[slab2, slab2, slab2, seq, seq, small], out_specs=[slab2, slab2, slab2], out_shape=[o32, o32, o32],
        scratch_shapes=[pltpu.VMEM((2, TA, SLAB), BF16), pltpu.VMEM((2, TA, SLAB), BF16)], vmem_mb=48, hosted=hosted)


def _qkv_post(dqp, dkp, dvp, proj, gq, gk, hosted):
    s = proj.shape[0]

    def body(dq_ref, dk_ref, dv_ref, q_ref, k_ref, gq_ref, gk_ref, dqo_ref, dko_ref, dvo_ref, df_ref, vec_ref):
        @pl.when(pl.program_id(0) == 0)
        def _():
            vec_ref[...] = jnp.zeros_like(vec_ref)

        def one(d_ref, x_ref, g_ref, o_ref, row, scale):
            dg = jnp.zeros((1, DH), F32)
            for h in range(HEADS):
                sl = slice(DH * h, DH * (h + 1))
                xv = x_ref[:, sl]
                r = lax.rsqrt(jnp.mean(xv * xv, axis=-1, keepdims=True) + EPS)
                xh = xv * r
                dn = d_ref[:, SLAB * h:SLAB * h + DH] * scale
                dg = dg + jnp.sum(dn * xh, axis=0, keepdims=True)
                dxh = dn * g_ref[...]
                o_ref[:, sl] = (r * (dxh - xh * jnp.mean(dxh * xh, axis=-1, keepdims=True))).astype(BF16)
            vec_ref[row:row + 1, 0:DH] += dg

        one(dq_ref, q_ref, gq_ref, dqo_ref, 0, QK_SCALE)
        one(dk_ref, k_ref, gk_ref, dko_ref, 1, 1.0)
        lane = lax.broadcasted_iota(jnp.int32, (TR, LANES), 1)
        df = jnp.zeros((TR, LANES), F32)
        for h in range(HEADS):
            dvo_ref[:, DH * h:DH * (h + 1)] = dv_ref[:, SLAB * h:SLAB * h + DH].astype(BF16)
            row_sum = dq_ref[:, SLAB * h + DH:SLAB * h + DH + 1]
            col_sum = dk_ref[:, SLAB * h + DH + AUG_ONE:SLAB * h + DH + AUG_ONE + 1]
            df = jnp.where(lane == h, row_sum - col_sum, df)
        df_ref[...] = df

    o = jax.ShapeDtypeStruct((s, AW), BF16)
    wide = _row_spec(HEADS * SLAB)
    return _call(
        body, (dqp, dkp, dvp, proj, proj, gq, gk), name="qkv_post", grid=(s // TR,),
        in_specs=[wide, wide, wide, _row_spec(AW, 0), _row_spec(AW, 1), _full_spec((1, DH)), _full_spec((1, DH))],
        out_specs=[_row_spec(AW), _row_spec(AW), _row_spec(AW), _row_spec(LANES), _full_spec((SUB, LANES))],
        out_shape=[o, o, o, jax.ShapeDtypeStruct((s, LANES), F32), jax.ShapeDtypeStruct((SUB, LANES), F32)],
        hosted=hosted)


TF = 256
NJ = DFF // TF
FFN_ROWS_FWD = 1024
FFN_ROWS_BWD = 1024


def _ffn_fwd(h2, wup_t, cw, wd):
    s = h2.shape[0]
    tr = FFN_ROWS_FWD
    nr = s // tr

    def body(h_ref, wu_ref, cg_ref, cv_ref, wd_ref, pg_ref, pv_ref, y_ref, halo_ref, act_ref):
        r, j = pl.program_id(0), pl.program_id(1)
        hv = h_ref[...]
        pg = _dot(hv, wu_ref[0], NT).astype(BF16)
        pv = _dot(hv, wu_ref[1], NT).astype(BF16)
        pg_ref[...] = pg
        pv_ref[...] = pv
        pgf, pvf = pg.astype(F32), pv.astype(F32)
        ug, _, _ = _conv_taps(pgf, jnp.where(r > 0, halo_ref[j, 0], 0.0), cg_ref[...])
        uv, _, _ = _conv_taps(pvf, jnp.where(r > 0, halo_ref[j, 1], 0.0), cv_ref[...])
        halo_ref[j, 0] = pgf[tr - SUB:tr, :]
        halo_ref[j, 1] = pvf[tr - SUB:tr, :]
        act = (ug * _sigmoid(ug) * uv).astype(BF16)
        for t in range(NJ):
            @pl.when(j == t)
            def _(t=t):
                act_ref[:, t * TF:(t + 1) * TF] = act

        @pl.when(j == NJ - 1)
        def _():
            y_ref[...] = _dot(act_ref[...], wd_ref[...])

    pre = jax.ShapeDtypeStruct((s, DFF), BF16)
    return pl.pallas_call(
        body, name="ffn_fwd", grid=(nr, NJ),
        in_specs=[pl.BlockSpec((tr, D), lambda r, j: (r, 0)),
                  pl.BlockSpec((2, TF, D), lambda r, j: (0, j, 0)),
                  pl.BlockSpec((3, TF), lambda r, j: (0, j)),
                  pl.BlockSpec((3, TF), lambda r, j: (0, NJ + j)),
                  pl.BlockSpec((DFF, D), lambda r, j: (0, 0))],
        out_specs=(pl.BlockSpec((tr, TF), lambda r, j: (r, j)),
                   pl.BlockSpec((tr, TF), lambda r, j: (r, j)),
                   pl.BlockSpec((tr, D), lambda r, j: (r, 0))),
        out_shape=(pre, pre, jax.ShapeDtypeStruct((s, D), F32)),
        scratch_shapes=[pltpu.VMEM((NJ, 2, SUB, TF), F32), pltpu.VMEM((tr, DFF), BF16)],
        compiler_params=_params(("arbitrary", "arbitrary"), 56),
    )(h2, wup_t, cw, cw, wd)


def _ffn_bwd(dy, h2, pre_g, pre_v, wup_t, cw, wd):
    s = h2.shape[0]
    tr = FFN_ROWS_BWD
    nr = s // tr
    hb = tr // (2 * SUB)

    def body(dy_ref, h_ref, pg_ref, pv_ref, hg_ref, hv_ref, wu_ref, cg_ref, cv_ref, wd_ref,
             dh_ref, dwu_ref, dwd_ref, dcg_ref, dcv_ref, nxt_ref, awu_ref, awd_ref):
        j, r = pl.program_id(0), pl.program_id(1)
        rr = nr - 1 - r
        row0 = pl.multiple_of(rr * tr, tr)
        cwg, cwv = cg_ref[...], cv_ref[...]
        pg, pv = pg_ref[...].astype(F32), pv_ref[...].astype(F32)
        ug, g1, g2 = _conv_taps(pg, jnp.where(rr > 0, hg_ref[SUB:2 * SUB, :].astype(F32), 0.0), cwg)
        uv, v1, v2 = _conv_taps(pv, jnp.where(rr > 0, hv_ref[SUB:2 * SUB, :].astype(F32), 0.0), cwv)
        sg = _sigmoid(ug)
        sil = ug * sg
        act = (sil * uv).astype(BF16)
        dyv = dy_ref[...]
        da = _dot(dyv, wd_ref[...], NT)
        dug = da * uv * (sg * (1.0 + ug * (1.0 - sg)))
        duv = da * sil
        dpg = _conv_taps_t(dug, jnp.where(r > 0, nxt_ref[0], 0.0), cwg)
        dpv = _conv_taps_t(duv, jnp.where(r > 0, nxt_ref[1], 0.0), cwv)
        nxt_ref[0] = dug[0:SUB, :]
        nxt_ref[1] = duv[0:SUB, :]
        dpgb, dpvb = dpg.astype(BF16), dpv.astype(BF16)
        hv = h_ref[...]
        dwd = _dot(act, dyv, TN)
        dpb = jnp.concatenate([dpgb, dpvb], axis=1)
        dwu = _dot(dpb, hv, TN)
        dh = _dot(dpb, wu_ref[...].reshape(2 * TF, D))

        def taps(du, x0, x1, x2):
            return (jnp.sum(du * x2, axis=0, keepdims=True), jnp.sum(du * x1, axis=0, keepdims=True),
                    jnp.sum(du * x0, axis=0, keepdims=True))

        tg, tv = taps(dug, pg, g1, g2), taps(duv, pv, v1, v2)

        @pl.when(r == 0)
        def _():
            awd_ref[...] = dwd
            awu_ref[...] = dwu
            dcg_ref[...] = jnp.zeros_like(dcg_ref)
            dcv_ref[...] = jnp.zeros_like(dcv_ref)

        @pl.when(r > 0)
        def _():
            awd_ref[...] += dwd
            awu_ref[...] += dwu

        @pl.when(r == nr - 1)
        def _():
            dwd_ref[...] = awd_ref[...].astype(BF16)
            dwu_ref[...] = awu_ref[...].astype(BF16).reshape(2, TF, D)

        for t in range(3):
            dcg_ref[t:t + 1, :] += tg[t]
            dcv_ref[t:t + 1, :] += tv[t]

        @pl.when(j == 0)
        def _():
            dh_ref[pl.ds(row0, tr), :] = dh

        @pl.when(j > 0)
        def _():
            dh_ref[pl.ds(row0, tr), :] += dh

    rows = lambda j, r: (nr - 1 - r, 0)
    tile = lambda j, r: (nr - 1 - r, j)
    halo = lambda j, r: (jnp.maximum((nr - 1 - r) * hb - 1, 0), j)
    return pl.pallas_call(
        body, name="ffn_bwd", grid=(NJ, nr),
        in_specs=[pl.BlockSpec((tr, D), rows), pl.BlockSpec((tr, D), rows),
                  pl.BlockSpec((tr, TF), tile), pl.BlockSpec((tr, TF), tile),
                  pl.BlockSpec((2 * SUB, TF), halo), pl.BlockSpec((2 * SUB, TF), halo),
                  pl.BlockSpec((2, TF, D), lambda j, r: (0, j, 0)),
                  pl.BlockSpec((3, TF), lambda j, r: (0, j)), pl.BlockSpec((3, TF), lambda j, r: (0, NJ + j)),
                  pl.BlockSpec((TF, D), lambda j, r: (j, 0))],
        out_specs=(pl.BlockSpec((s, D), lambda j, r: (0, 0)),
                   pl.BlockSpec((2, TF, D), lambda j, r: (0, j, 0)),
                   pl.BlockSpec((TF, D), lambda j, r: (j, 0)),
                   pl.BlockSpec((SUB, TF), lambda j, r: (0, j)), pl.BlockSpec((SUB, TF), lambda j, r: (0, j))),
        out_shape=(jax.ShapeDtypeStruct((s, D), F32),
                   jax.ShapeDtypeStruct((2, DFF, D), BF16), jax.ShapeDtypeStruct((DFF, D), BF16),
                   jax.ShapeDtypeStruct((SUB, DFF), F32), jax.ShapeDtypeStruct((SUB, DFF), F32)),
        scratch_shapes=[pltpu.VMEM((2, SUB, TF), F32), pltpu.VMEM((2 * TF, D), F32), pltpu.VMEM((TF, D), F32)],
        compiler_params=_params(("arbitrary", "arbitrary"), 56),
    )(dy, h2, pre_g, pre_v, pre_g, pre_v, wup_t, cw, cw, wd)


def _adam(w, g, m, v):
    m = ADAM_B1 * m + (1.0 - ADAM_B1) * g
    v = ADAM_B2 * v + (1.0 - ADAM_B2) * (g * g)
    m_hat = m / (1.0 - ADAM_B1 ** ADAM_STEP)
    v_hat = v / (1.0 - ADAM_B2 ** ADAM_STEP)
    delta = -ADAM_LR * (m_hat / (jnp.sqrt(v_hat) + ADAM_EPS) + ADAM_WD * w)
    return delta, m, v


NCHIP = NDEV // 2


def _pair_add(mine, theirs, tr, name):
    _, _, rws, cols = mine.shape

    def body(a_ref, b_ref, o_ref):
        c = lax.axis_index("c")
        o_ref[0] = (a_ref[0, c].astype(F32) + b_ref[0].astype(F32)).astype(BF16)

    (out,), _ = _call(
        body, (mine, theirs), name=name, grid=(NCHIP, rws // tr),
        in_specs=[pl.BlockSpec((1, 2, tr, cols), lambda q, i: (q, 0, i, 0)),
                  pl.BlockSpec((1, tr, cols), lambda q, i: (q, i, 0))],
        out_specs=[pl.BlockSpec((1, tr, cols), lambda q, i: (q, i, 0))],
        out_shape=[jax.ShapeDtypeStruct((NCHIP, rws, cols), BF16)], vmem_mb=32)
    return out


def _adamw_sharded(parts, w, m, v, tr, name, hosted=None):
    rws, cols = w.shape

    def body(p_ref, w_ref, m_ref, v_ref, g_ref, d_ref, mo_ref, vo_ref):
        g = p_ref[0].astype(F32)
        for q in range(1, NCHIP):
            g = g + p_ref[q].astype(F32)
        g_ref[...] = g
        d_ref[...], mo_ref[...], vo_ref[...] = _adam(w_ref[...], g, m_ref[...], v_ref[...])

    blk = pl.BlockSpec((tr, cols), lambda i: (i, 0))
    o = jax.ShapeDtypeStruct((rws, cols), F32)
    outs, moved = _call(
        body, (parts, w, m, v), name=name, grid=(rws // tr,),
        in_specs=[pl.BlockSpec((NCHIP, tr, cols), lambda i: (0, i, 0)), blk, blk, blk],
        out_specs=[blk, blk, blk, blk], out_shape=[o, o, o, o], vmem_mb=48, hosted=hosted)
    return (outs, moved) if hosted else outs


def _adamw_ada(c_all, dmod_my, w, m, v):
    rws, cols = w.shape
    tr = 256

    def body(c_ref, dm_ref, w_ref, m_ref, v_ref, g_ref, d_ref, mo_ref, vo_ref):
        cv = c_ref[...]
        act = cv * _sigmoid(cv)
        g = _dot(act, dm_ref[...], TN, lax.Precision.HIGHEST)
        g_ref[...] = g
        d_ref[...], mo_ref[...], vo_ref[...] = _adam(w_ref[...], g, m_ref[...], v_ref[...])

    blk = pl.BlockSpec((tr, cols), lambda i: (i, 0))
    o = jax.ShapeDtypeStruct((rws, cols), F32)
    return pl.pallas_call(
        body, name="adamw_ada", grid=(rws // tr,),
        in_specs=[pl.BlockSpec((NDEV, tr), lambda i: (0, i)), _full_spec((NDEV, cols)), blk, blk, blk],
        out_specs=(blk, blk, blk, blk), out_shape=(o, o, o, o),
        compiler_params=_params(("parallel",), 48),
    )(c_all, dmod_my, w, m, v)


REP_ROWS = 16
ROW_N1, ROW_N2, ROW_LOSS, ROW_MISC = 6, 7, 8, 9
LANE_BF, LANE_GQ, LANE_GK = 0, 128, 256


def _adamw_small(rep_all, conv_all, wmv):
    n_ff = wmv[6][0].shape[1]

    def body(*refs):
        rep_ref, conv_ref = refs[:2]
        ins = refs[2:2 + 24]
        outs = refs[2 + 24:]
        loss_ref, outs = outs[0], outs[1:]
        g_rep = rep_ref[0]
        g_conv = conv_ref[0]
        for d in range(1, NDEV):
            g_rep = g_rep + rep_ref[d]
            g_conv = g_conv + conv_ref[d]
        loss_ref[...] = (0.5 / D) * jnp.sum(g_rep[ROW_LOSS:ROW_LOSS + 1, :], axis=-1, keepdims=True)
        grads = [
            None,
            g_rep[ROW_N1:ROW_N1 + 1, :],
            g_rep[ROW_MISC:ROW_MISC + 1, LANE_BF:LANE_BF + HEADS],
            g_rep[ROW_MISC:ROW_MISC + 1, LANE_GQ:LANE_GQ + DH],
            g_rep[ROW_MISC:ROW_MISC + 1, LANE_GK:LANE_GK + DH],
            g_rep[ROW_N2:ROW_N2 + 1, :],
            g_conv[0:3, 0:n_ff],
            g_conv[0:3, n_ff:n_ff + DH],
        ]
        for p in range(8):
            w_ref, m_ref, v_ref = ins[3 * p:3 * p + 3]
            g_ref, d_ref, mo_ref, vo_ref = outs[4 * p:4 * p + 4]
            if p == 0:
                for nmod in range(NMOD):
                    sl = slice(D * nmod, D * (nmod + 1))
                    g = g_rep[nmod:nmod + 1, :]
                    g_ref[:, sl] = g
                    d_ref[:, sl], mo_ref[:, sl], vo_ref[:, sl] = _adam(w_ref[:, sl], g, m_ref[:, sl], v_ref[:, sl])
            else:
                g = grads[p]
                g_ref[...] = g
                d_ref[...], mo_ref[...], vo_ref[...] = _adam(w_ref[...], g, m_ref[...], v_ref[...])

    flat = [a for trio in wmv for a in trio]
    out_shape = [jax.ShapeDtypeStruct((1, 1), F32)]
    for trio in wmv:
        out_shape += [jax.ShapeDtypeStruct(trio[0].shape, F32)] * 4
    return pl.pallas_call(
        body, name="adamw_small", out_shape=tuple(out_shape),
        compiler_params=_params(None, 32),
    )(rep_all, conv_all, *flat)


FG_FIRST = 3 * AW
N_IN = DIN // NDEV


def _w_in_runs():
    runs = []
    for d in range(NDEV):
        lo, hi = N_IN * d, N_IN * (d + 1)
        for a, b, shift in ((0, FG_FIRST, 0), (FG_FIRST, FG_FIRST + HEADS, DIN - HEADS - FG_FIRST),
                            (FG_FIRST + HEADS, DIN, -HEADS)):
            a, b = max(a, lo), min(b, hi)
            if a < b:
                runs.append((d, a - lo, a + shift, b - a))
    return runs


W_IN_ROWS = 256
N_IN_PAD = 512


def _identity(n):
    return (lax.broadcasted_iota(jnp.int32, (n, n), 0) == lax.broadcasted_iota(jnp.int32, (n, n), 1)).astype(BF16)


def _assemble_w_in(g_in, hosted):
    def body(g_ref, o_ref, t_ref):
        eye = _identity(W_IN_ROWS)
        shard = None
        for d, src, dst, width in _w_in_runs():
            if d != shard:
                t_ref[:, 0:N_IN] = _dot(eye, g_ref[d], NT).astype(BF16)
                shard = d
            o_ref[:, dst:dst + width] = t_ref[:, src:src + width]
        o_ref[:, DIN:DINP] = jnp.zeros((W_IN_ROWS, DINP - DIN), o_ref.dtype)

    (out,), moved = _call(
        body, (g_in,), name="assemble_w_in", grid=(D // W_IN_ROWS,),
        in_specs=[pl.BlockSpec((NDEV, N_IN, W_IN_ROWS), lambda i: (0, 0, i))],
        out_specs=[pl.BlockSpec((W_IN_ROWS, DINP), lambda i: (i, 0))],
        out_shape=[jax.ShapeDtypeStruct((D, DINP), g_in.dtype)],
        scratch_shapes=[pltpu.VMEM((W_IN_ROWS, N_IN_PAD), BF16)], vmem_mb=32, hosted=hosted)
    return out, moved


def _scatter_dw_in(dwp):
    def body(w_ref, o_ref, t_ref):
        eye = _identity(W_IN_ROWS)
        runs = _w_in_runs()
        for i, (d, src, dst, width) in enumerate(runs):
            t_ref[:, src:src + width] = w_ref[:, dst:dst + width]
            if i + 1 == len(runs) or runs[i + 1][0] != d:
                o_ref[d // 2, d % 2] = _dot(t_ref[:, 0:N_IN], eye, TN).astype(BF16)

    (out,), _ = _call(
        body, (dwp,), name="scatter_dw_in", grid=(D // W_IN_ROWS,),
        in_specs=[pl.BlockSpec((W_IN_ROWS, DINP), lambda i: (i, 0))],
        out_specs=[pl.BlockSpec((NCHIP, 2, N_IN, W_IN_ROWS), lambda i: (0, 0, 0, i))],
        out_shape=[jax.ShapeDtypeStruct((NCHIP, 2, N_IN, D), dwp.dtype)],
        scratch_shapes=[pltpu.VMEM((W_IN_ROWS, N_IN_PAD), BF16)], vmem_mb=32)
    return out


def kernel(x, c, w_ada, b_ada, norm1_g, w_in, b_forget, q_norm_g, k_norm_g, conv_mix_w, w_out, norm2_g, w_up, ffn_conv_w, w_down, loss_target, m_w_ada, m_b_ada, m_norm1_g, m_w_in, m_b_forget, m_q_norm_g, m_k_norm_g, m_conv_mix_w, m_w_out, m_norm2_g, m_w_up, m_ffn_conv_w, m_w_down, v_w_ada, v_b_ada, v_norm1_g, v_w_in, v_b_forget, v_q_norm_g, v_k_norm_g, v_conv_mix_w, v_w_out, v_norm2_g, v_w_up, v_ffn_conv_w, v_w_down):
    me = 4 * lax.axis_index("x") + 2 * lax.axis_index("y") + lax.axis_index("c")
    xs, tgt = x[0], loss_target[0]
    s = xs.shape[0]
    nq = s // TA
    n_ada = w_ada.shape[2]
    n_ff = w_up.shape[2]

    conv_w = jnp.concatenate([ffn_conv_w[0], conv_mix_w[0]], axis=1)
    conv_w = jnp.concatenate([conv_w, jnp.zeros((SUB - 3, conv_w.shape[1]), F32)], axis=0)
    c_all, conv_all, g_in = _exchange(
        [(c.reshape(SUB, D // SUB), "ag"), (conv_w, "ag"), (jnp.transpose(w_in[0]).astype(BF16), "ag2")],
        "exchange_w_in")
    c_all = c_all.reshape(NDEV, D)
    cw_ffn = jnp.transpose(conv_all[:, :3, :n_ff], (1, 0, 2)).reshape(3, 2 * DFF)
    cw_mix = jnp.transpose(conv_all[:, :3, n_ff:], (1, 0, 2)).reshape(3, CW)

    b_my = lax.dynamic_slice(b_ada, (0, me * n_ada), (1, n_ada))
    mod_part = _ada_fwd(c_all, w_ada[0], b_my)
    w_in_p, (mod_rows,) = _assemble_w_in(
        g_in, [(jnp.broadcast_to(mod_part[:, None, :], (NDEV, SUB, n_ada)), "a2a")])
    mod = mod_rows[:, 0, :].reshape(NMOD, D)
    mod = jnp.concatenate([mod, jnp.zeros((SUB - NMOD, D), F32)], axis=0)

    h = _norm_mod_fwd(xs, mod, norm1_g)
    proj, (g_down,) = _mm(h, w_in_p, "nn", F32, 1024, 640, "proj_fwd", hosted=[(w_down[0].astype(BF16), "ag2")])
    bf_pad = jnp.concatenate([b_forget, jnp.zeros((1, LANES - HEADS), F32)], axis=1)
    fcum = _fgate_fwd(proj, bf_pad)
    (qp, kp, vp), (g_out,) = _qkv_prep(proj, fcum, q_norm_g, k_norm_g, [(w_out[0].astype(BF16), "ag2")])
    attn, lse, (g_up,) = _attn_fwd(qp, kp, vp, [(jnp.transpose(w_up[0]).astype(BF16), "ag2")])
    w_out_f = g_out.reshape(D, D)
    w_up_t = g_up.reshape(2, DFF, D)
    w_down_f = g_down.reshape(DFF, D)
    conv = _mixconv_fwd(proj, cw_mix)
    mixed = jnp.concatenate([attn, conv], axis=1).astype(BF16)
    z = _mm(mixed, w_out_f, "nn", F32, 1024, 1024, "out_fwd")
    x1, h2 = _resid_norm2(xs, z, mod, norm2_g)
    pre_g, pre_v, y = _ffn_fwd(h2, w_up_t, cw_ffn, w_down_f)
    dout, dy, vec_l = _loss_head(x1, y, tgt, mod)

    dh2, dwup_t, dwd, dcw_g, dcw_v = _ffn_bwd(dy, h2, pre_g, pre_v, w_up_t, cw_ffn, w_down_f)
    dx1, dz, vec_2 = _norm_mod_bwd(dh2, x1, dout, z, mod, norm2_g, 3, 4, 2, "norm2_bwd")
    dwout = _mm(mixed, dz, "tn", BF16, 1024, 1024, "out_bwd_w")
    s_out = dwout.reshape(NCHIP, 2, D // NDEV, D)
    s_down = dwd.reshape(NCHIP, 2, DFF // NDEV, D)
    s_up = dwup_t.reshape(NCHIP, 2, n_ff, D)
    dmixed, (t_out, t_up, t_down) = _mm(dz, w_out_f, "nt", F32, 1024, 1024, "out_bwd_x",
                                        hosted=[(s_out, "pair"), (s_up, "pair"), (s_down, "pair")])
    c_out = _pair_add(s_out, t_out, 128, "pair_add_out")
    c_up = _pair_add(s_up, t_up, 176, "pair_add_up")
    c_down = _pair_add(s_down, t_down, 176, "pair_add_down")
    dxin, dbg, dcg, dcw_mix = _mixconv_bwd(dmixed, proj, cw_mix)
    (dqp, dkp, dvp), (p_up,) = _attn_bwd(qp, kp, vp, dmixed, attn, lse, [(c_up, "chips")])
    (dq, dk, dvb, dfcol, vec_qk), (p_out, p_down) = _qkv_post(
        dqp, dkp, dvp, proj, q_norm_g, k_norm_g, [(c_out, "chips"), (c_down, "chips")])
    dfg, vec_bf = _fgate_bwd(dfcol, proj, bf_pad)
    dproj = jnp.concatenate([dq, dk, dvb, dxin, dbg, dcg, dfg], axis=1)
    dwin_p = _mm(h, dproj, "tn", BF16, 1024, 640, "proj_bwd_w")
    s_in = _scatter_dw_in(dwin_p)
    (t_in,) = _exchange([(s_in, "pair")], "exchange_pair_in")
    c_in = _pair_add(s_in, t_in, N_IN, "pair_add_in")
    dh, (p_in,) = _mm(dproj, w_in_p, "nt", F32, 1024, 512, "proj_bwd_x", hosted=[(c_in, "chips")])
    grad_x, vec_1 = _norm_mod_bwd(dh, xs, dx1, None, mod, norm1_g, 0, 1, None, "norm1_bwd")

    misc = jnp.zeros((1, D), F32)
    misc = lax.dynamic_update_slice(misc, vec_bf[0:1, :HEADS], (0, LANE_BF))
    misc = lax.dynamic_update_slice(misc, vec_qk[0:1, :DH], (0, LANE_GQ))
    misc = lax.dynamic_update_slice(misc, vec_qk[1:2, :DH], (0, LANE_GK))
    rep = jnp.concatenate([
        vec_1[0:1], vec_1[1:2], vec_2[3:4], vec_2[0:1], vec_2[1:2], vec_l[0:1],
        vec_1[2:3], vec_2[2:3], vec_l[1:2], misc, jnp.zeros((REP_ROWS - 10, D), F32)], axis=0)
    dcw_ffn = jnp.concatenate([dcw_g, dcw_v], axis=1).reshape(SUB, NDEV, n_ff)
    dcw_all = jnp.concatenate([jnp.transpose(dcw_ffn, (1, 0, 2)),
                               jnp.transpose(dcw_mix.reshape(SUB, NDEV, DH), (1, 0, 2))], axis=2)
    r_out, (rep_all, conv_parts) = _adamw_sharded(p_out, w_out[0], m_w_out[0], v_w_out[0], 128, "adamw_out",
                                                  hosted=[(rep, "ag"), (dcw_all, "a2a")])
    dmod_my = lax.dynamic_slice(rep_all[:, :NMOD, :].reshape(NDEV, NMOD * D), (0, me * n_ada), (NDEV, n_ada))
    r_ada = _adamw_ada(c_all, dmod_my, w_ada[0], m_w_ada[0], v_w_ada[0])
    r_in = _adamw_sharded(p_in, jnp.transpose(w_in[0]), jnp.transpose(m_w_in[0]), jnp.transpose(v_w_in[0]), N_IN,
                          "adamw_in")
    r_in = tuple(jnp.transpose(a) for a in r_in)
    r_up = _adamw_sharded(p_up, jnp.transpose(w_up[0]), jnp.transpose(m_w_up[0]), jnp.transpose(v_w_up[0]), 176,
                          "adamw_up")
    r_up = tuple(jnp.transpose(a) for a in r_up)
    r_down = _adamw_sharded(p_down, w_down[0], m_w_down[0], v_w_down[0], 176, "adamw_down")
    small = _adamw_small(rep_all, conv_parts, [
        [b_ada, m_b_ada, v_b_ada], [norm1_g, m_norm1_g, v_norm1_g], [b_forget, m_b_forget, v_b_forget],
        [q_norm_g, m_q_norm_g, v_q_norm_g], [k_norm_g, m_k_norm_g, v_k_norm_g], [norm2_g, m_norm2_g, v_norm2_g],
        [ffn_conv_w[0], m_ffn_conv_w[0], v_ffn_conv_w[0]], [conv_mix_w[0], m_conv_mix_w[0], v_conv_mix_w[0]]])
    loss = small[0].reshape(())
    r_bada, r_n1, r_bf, r_gq, r_gk, r_n2, r_cf, r_cm = [small[1 + 4 * p:5 + 4 * p] for p in range(8)]
    lead = lambda t: tuple(a[None] for a in t)
    per_w = [lead(r_ada), r_bada, r_n1, lead(r_in), r_bf, r_gq, r_gk, lead(r_cm), lead(r_out), r_n2,
             lead(r_up), lead(r_cf), lead(r_down)]
    outs = [loss, grad_x[None]]
    for field in range(4):
        outs += [t[field] for t in per_w]
    return tuple(outs)
```

```python
import functools

import jax
import jax.numpy as jnp
import numpy as np
from jax import lax
from jax.experimental import pallas as pl
from jax.experimental.pallas import tpu as pltpu

F32 = jnp.float32
BF16 = jnp.bfloat16

NDEV = 8
D = 1024
HEADS = 8
DH = 64
AW = 512
CW = 512
DFF = 2816
DIN = 3080
DINP = 3200
NMOD = 6
EPS = 1e-6
QK_SCALE = 0.125
LANES = 128
SUB = 8

ADAM_LR = 0.001
ADAM_B1 = 0.9
ADAM_B2 = 0.999
ADAM_EPS = 1e-08
ADAM_WD = 0.01
ADAM_STEP = 10

MESH = pl.DeviceIdType.MESH
ANY = pl.BlockSpec(memory_space=pl.ANY)

NN = (((1,), (0,)), ((), ()))
NT = (((1,), (1,)), ((), ()))
TN = (((0,), (0,)), ((), ()))


def _dot(a, b, dims=NN, precision=None):
    return lax.dot_general(a, b, dims, precision=precision, preferred_element_type=F32)


def _params(sem=None, vmem_mb=None):
    kw = {}
    if sem is not None:
        kw["dimension_semantics"] = sem
    if vmem_mb is not None:
        kw["vmem_limit_bytes"] = vmem_mb * 1024 * 1024
    return pltpu.CompilerParams(**kw)


def _sigmoid(x):
    return 0.5 * jnp.tanh(0.5 * x) + 0.5


class _Exchange:
    def __init__(self, items):
        self.arrays = [a for a, _ in items]
        self.modes = [m for _, m in items]
        self.n = len(items)
        self.out_shape = []
        for a, m in items:
            sh = {"ag": (NDEV,) + a.shape, "ag2": (NDEV,) + a.shape, "pair": a.shape[:1] + a.shape[2:]}.get(m, a.shape)
            self.out_shape.append(jax.ShapeDtypeStruct(sh, a.dtype))
        self.scratch = [pltpu.SemaphoreType.DMA((self.n, NDEV - 1)), pltpu.SemaphoreType.DMA((self.n, NDEV - 1)),
                        pltpu.SemaphoreType.DMA((self.n,))]

    def _plan(self, srcs, outs, sems):
        send_sems, recv_sems, loc_sems = sems
        x, y, c = lax.axis_index("x"), lax.axis_index("y"), lax.axis_index("c")
        me, my_chip = 4 * x + 2 * y + c, 2 * x + y
        sib = (x, y, 1 - c)
        local, first, landed, forwards, arrivals = [], [], [], [], []

        def remote(a, k, src, dst, to):
            return pltpu.make_async_remote_copy(src_ref=src, dst_ref=dst, send_sem=send_sems.at[a, k],
                                                recv_sem=recv_sems.at[a, k], device_id=to, device_id_type=MESH)

        for a, mode in enumerate(self.modes):
            src, out = srcs[a], outs[a]
            if mode in ("ag", "a2a"):
                piece = (lambda slot, src=src: src) if mode == "ag" else (lambda slot, src=src: src.at[slot])
                local.append(pltpu.make_async_copy(piece(me), out.at[me], loc_sems.at[a]))
                for r in range(1, NDEV):
                    px = 1 - x if (r >> 2) & 1 else x
                    py = 1 - y if (r >> 1) & 1 else y
                    pc = 1 - c if r & 1 else c
                    pidx = 4 * px + 2 * py + pc
                    first.append(remote(a, r - 1, piece(pidx), out.at[me], (px, py, pc)))
                    arrivals.append(remote(a, r - 1, piece(pidx), out.at[pidx], (px, py, pc)))
            elif mode == "ag2":
                local.append(pltpu.make_async_copy(src, out.at[me], loc_sems.at[a]))
                first.append(remote(a, 0, src, out.at[me], sib))
                arrivals.append(remote(a, 0, src, out.at[me + 1 - 2 * c], sib))
                for j, (px, py) in enumerate([(1 - x, y), (x, 1 - y), (1 - x, 1 - y)]):
                    theirs = out.at[4 * px + 2 * py + c]
                    first.append(remote(a, 1 + j, src, out.at[me], (px, py, c)))
                    landed.append(remote(a, 1 + j, src, theirs, (px, py, c)))
                    forwards.append(remote(a, 4 + j, theirs, theirs, sib))
                    arrivals.append(remote(a, 4 + j, src, out.at[4 * px + 2 * py + 1 - c], sib))
            elif mode == "pair":
                for q in range(NDEV // 2):
                    first.append(remote(a, q, src.at[q, 1 - c], out.at[q], sib))
                    arrivals.append(remote(a, q, src.at[q, 1 - c], out.at[q], sib))
            else:
                assert mode == "chips", mode
                local.append(pltpu.make_async_copy(src.at[my_chip], out.at[my_chip], loc_sems.at[a]))
                for j, (px, py) in enumerate([(1 - x, y), (x, 1 - y), (1 - x, 1 - y)]):
                    q = 2 * px + py
                    first.append(remote(a, 1 + j, src.at[q], out.at[my_chip], (px, py, c)))
                    arrivals.append(remote(a, 1 + j, src.at[q], out.at[q], (px, py, c)))
        return local, first, landed, forwards, arrivals

    def start(self, srcs, outs, sems):
        local, first, _, _, _ = self._plan(srcs, outs, sems)
        for cp in local + first:
            cp.start()

    def wait(self, srcs, outs, sems):
        local, first, landed, forwards, arrivals = self._plan(srcs, outs, sems)
        for cp, fwd in zip(landed, forwards):
            cp.wait_recv()
            fwd.start()
        for cp in arrivals:
            cp.wait_recv()
        for cp in first + forwards:
            cp.wait_send()
        for cp in local:
            cp.wait()


def _exchange(items, name):
    ex = _Exchange(items)
    n = ex.n

    def body(*refs):
        srcs, outs, sems = refs[:n], refs[n:2 * n], refs[2 * n:]
        ex.start(srcs, outs, sems)
        ex.wait(srcs, outs, sems)

    return pl.pallas_call(
        body, name=name,
        out_shape=tuple(ex.out_shape),
        in_specs=[ANY] * n, out_specs=tuple([ANY] * n),
        scratch_shapes=ex.scratch,
        compiler_params=pltpu.CompilerParams(has_side_effects=True),
    )(*ex.arrays)


def _call(body, inputs, *, name, grid, in_specs, out_specs, out_shape, scratch_shapes=(), vmem_mb=None, hosted=None):
    out_specs, out_shape, scratch_shapes = tuple(out_specs), tuple(out_shape), list(scratch_shapes)
    if not hosted:
        res = pl.pallas_call(
            body, name=name, grid=grid, in_specs=list(in_specs), out_specs=out_specs, out_shape=out_shape,
            scratch_shapes=scratch_shapes, compiler_params=_params(("arbitrary",) * len(grid), vmem_mb),
        )(*inputs)
        return tuple(res), ()
    ex = _Exchange(hosted)
    n, n_in, n_out, n_scr = ex.n, len(inputs), len(out_shape), len(scratch_shapes)

    def hosting_body(*refs):
        ins, srcs = refs[:n_in], refs[n_in:n_in + n]
        outs, landing = refs[n_in + n:n_in + n + n_out], refs[n_in + n + n_out:n_in + 2 * n + n_out]
        scratch, sems = refs[n_in + 2 * n + n_out:n_in + 2 * n + n_out + n_scr], refs[n_in + 2 * n + n_out + n_scr:]
        first = functools.reduce(jnp.logical_and, [pl.program_id(d) == 0 for d in range(len(grid))])
        last = functools.reduce(jnp.logical_and, [pl.program_id(d) == grid[d] - 1 for d in range(len(grid))])

        @pl.when(first)
        def _():
            ex.start(srcs, landing, sems)

        body(*ins, *outs, *scratch)

        @pl.when(last)
        def _():
            ex.wait(srcs, landing, sems)

    res = pl.pallas_call(
        hosting_body, name=name, grid=grid,
        in_specs=list(in_specs) + [ANY] * n, out_specs=out_specs + tuple([ANY] * n),
        out_shape=out_shape + tuple(ex.out_shape), scratch_shapes=scratch_shapes + ex.scratch,
        compiler_params=_params(("arbitrary",) * len(grid), vmem_mb),
    )(*inputs, *ex.arrays)
    return tuple(res[:n_out]), tuple(res[n_out:])


def _mm(a, b, mode, out_dtype, tm, tn, name, hosted=None, vmem_mb=24):
    if mode == "nn":
        (m, k), n = a.shape, b.shape[1]
        a_spec = pl.BlockSpec((tm, k), lambda i, j: (i, 0))
        b_spec = pl.BlockSpec((k, tn), lambda i, j: (0, j))
        dims = NN
    elif mode == "nt":
        (m, k), n = a.shape, b.shape[0]
        a_spec = pl.BlockSpec((tm, k), lambda i, j: (i, 0))
        b_spec = pl.BlockSpec((tn, k), lambda i, j: (j, 0))
        dims = NT
    else:
        (k, m), n = a.shape, b.shape[1]
        a_spec = pl.BlockSpec((k, tm), lambda i, j: (0, i))
        b_spec = pl.BlockSpec((k, tn), lambda i, j: (0, j))
        dims = TN
    assert m % tm == 0 and n % tn == 0, (m, n, tm, tn)

    def body(a_ref, b_ref, o_ref):
        o_ref[...] = _dot(a_ref[...], b_ref[...], dims).astype(o_ref.dtype)

    (out,), moved = _call(
        body, (a, b), name=name, grid=(m // tm, n // tn),
        in_specs=[a_spec, b_spec], out_specs=[pl.BlockSpec((tm, tn), lambda i, j: (i, j))],
        out_shape=[jax.ShapeDtypeStruct((m, n), out_dtype)], vmem_mb=vmem_mb, hosted=hosted)
    return (out, moved) if hosted else out


def _shift_down(x, k, fill):
    y = pltpu.roll(x, k, 0)
    row = lax.broadcasted_iota(jnp.int32, (SUB, x.shape[1]), 0)
    head = y[0:SUB, :]
    for t in range(k):
        head = jnp.where(row == t, fill[t], head)
    return jnp.concatenate([head, y[SUB:, :]], axis=0)


def _shift_up(x, k, fill):
    n = x.shape[0]
    y = pltpu.roll(x, n - k, 0)
    row = lax.broadcasted_iota(jnp.int32, (SUB, x.shape[1]), 0)
    tail = y[n - SUB:, :]
    for t in range(k):
        tail = jnp.where(row == SUB - k + t, fill[t], tail)
    return jnp.concatenate([y[:n - SUB, :], tail], axis=0)


def _conv_taps(x, halo, w):
    if halo is None:
        f1, f2 = [0.0], [0.0, 0.0]
    else:
        f1, f2 = [halo[7:8, :]], [halo[6:7, :], halo[7:8, :]]
    s1 = _shift_down(x, 1, f1)
    s2 = _shift_down(x, 2, f2)
    u = w[2:3, :] * x + w[1:2, :] * s1 + w[0:1, :] * s2
    return u, s1, s2


def _conv_taps_t(du, nxt, w):
    if nxt is None:
        f1, f2 = [0.0], [0.0, 0.0]
    else:
        f1, f2 = [nxt[0:1, :]], [nxt[0:1, :], nxt[1:2, :]]
    return w[2:3, :] * du + w[1:2, :] * _shift_up(du, 1, f1) + w[0:1, :] * _shift_up(du, 2, f2)


def _ada_fwd(c_all, w_ada, b_my):
    def body(c_ref, w_ref, b_ref, o_ref):
        cv = c_ref[...]
        act = cv * _sigmoid(cv)
        o_ref[...] = _dot(act, w_ref[...], NN, lax.Precision.HIGHEST) + b_ref[...]

    return pl.pallas_call(
        body, name="ada_fwd",
        out_shape=jax.ShapeDtypeStruct((NDEV, w_ada.shape[1]), F32),
        compiler_params=_params(None, 32),
    )(c_all, w_ada, b_my)


TR = 256


def _row_spec(width, col=0):
    return pl.BlockSpec((TR, width), lambda i, col=col: (i, col))


def _full_spec(shape):
    return pl.BlockSpec(shape, lambda i: (0,) * len(shape))


def _norm_mod_fwd(x, mod, g):
    s = x.shape[0]

    def body(x_ref, mod_ref, g_ref, h_ref):
        xv = x_ref[...]
        r = lax.rsqrt(jnp.mean(xv * xv, axis=-1, keepdims=True) + EPS)
        nrm = xv * r * g_ref[...]
        h_ref[...] = (nrm * (1.0 + mod_ref[1:2, :]) + mod_ref[0:1, :]).astype(BF16)

    return pl.pallas_call(
        body, name="norm1_fwd", grid=(s // TR,),
        in_specs=[_row_spec(D), _full_spec((SUB, D)), _full_spec((1, D))],
        out_specs=_row_spec(D), out_shape=jax.ShapeDtypeStruct((s, D), BF16),
        compiler_params=_params(("parallel",)),
    )(x, mod, g)


SLAB = 2 * DH
AUG_F, AUG_ONE, AUG_LSE = 0, 3, 6


def _split3(x):
    hi = x.astype(BF16).astype(F32)
    r1 = x - hi
    mid = r1.astype(BF16).astype(F32)
    return hi, mid, r1 - mid


def _lanes3(lane, first, pieces, other):
    out = other
    for k in range(3):
        out = jnp.where(lane == first + k, pieces[k], out)
    return out


def _aug_placement():
    eq = np.zeros((3 * LANES, HEADS * SLAB), np.float32)
    ek = np.zeros((3 * LANES, HEADS * SLAB), np.float32)
    ones = np.zeros((SUB, HEADS * SLAB), np.float32)
    for h in range(HEADS):
        aug = SLAB * h + DH
        for k in range(3):
            eq[LANES * k + h, aug + AUG_F + k] = 1.0
            ek[LANES * k + h, aug + AUG_ONE + k] = -1.0
            ones[0, aug + AUG_ONE + k] = 1.0
            ones[1, aug + AUG_F + k] = ones[1, aug + AUG_LSE + k] = 1.0
            ones[2, aug + k] = 1.0
    return jnp.asarray(eq, BF16), jnp.asarray(ek, BF16), jnp.asarray(ones)


def _qkv_prep(proj, fcum, gq, gk, hosted):
    s = proj.shape[0]

    def body(q_ref, k_ref, v_ref, f_ref, gq_ref, gk_ref, eq_ref, ek_ref, ones_ref, qo_ref, ko_ref, vo_ref):
        f3 = jnp.concatenate(_split3(f_ref[...]), axis=1).astype(BF16)
        qo_ref[...] = (_dot(f3, eq_ref[...]) + ones_ref[0:1, :]).astype(BF16)
        ko_ref[...] = (_dot(f3, ek_ref[...]) + ones_ref[1:2, :]).astype(BF16)
        vo_ref[...] = jnp.broadcast_to(ones_ref[2:3, :], vo_ref.shape).astype(BF16)
        for h in range(HEADS):
            sl = slice(DH * h, DH * (h + 1))
            lo = slice(SLAB * h, SLAB * h + DH)
            qh = q_ref[:, sl]
            r = lax.rsqrt(jnp.mean(qh * qh, axis=-1, keepdims=True) + EPS)
            qo_ref[:, lo] = (qh * r * gq_ref[...] * QK_SCALE).astype(BF16)
            kh = k_ref[:, sl]
            r = lax.rsqrt(jnp.mean(kh * kh, axis=-1, keepdims=True) + EPS)
            ko_ref[:, lo] = (kh * r * gk_ref[...]).astype(BF16)
            vo_ref[:, lo] = v_ref[:, sl].astype(BF16)

    eq, ek, ones = _aug_placement()
    o = jax.ShapeDtypeStruct((s, HEADS * SLAB), BF16)
    wide = _row_spec(HEADS * SLAB)
    return _call(
        body, (proj, proj, proj, fcum, gq, gk, eq, ek, ones), name="qkv_prep", grid=(s // TR,),
        in_specs=[_row_spec(AW, 0), _row_spec(AW, 1), _row_spec(AW, 2), _row_spec(LANES),
                  _full_spec((1, DH)), _full_spec((1, DH)), _full_spec(eq.shape), _full_spec(ek.shape),
                  _full_spec(ones.shape)],
        out_specs=[wide, wide, wide], out_shape=[o, o, o], vmem_mb=16, hosted=hosted)


FG_BLOCK = (3 * AW + 3 * CW) // LANES


def _fgate_fwd(proj, bf_pad):
    s = proj.shape[0]

    def body(fg_ref, b_ref, o_ref, carry_ref):
        i = pl.program_id(0)

        @pl.when(i == 0)
        def _():
            carry_ref[...] = jnp.zeros_like(carry_ref)

        z = fg_ref[...] + b_ref[...]
        logf = jnp.minimum(z, 0.0) - jnp.log1p(jnp.exp(-jnp.abs(z)))
        row = lax.broadcasted_iota(jnp.int32, (TR, TR), 0)
        col = lax.broadcasted_iota(jnp.int32, (TR, TR), 1)
        tri = (col <= row).astype(F32)
        cs = _dot(tri, logf, NN, lax.Precision.HIGHEST) + carry_ref[0:1, :]
        o_ref[...] = cs
        carry_ref[...] = jnp.broadcast_to(cs[TR - 1:TR, :], carry_ref.shape)

    return pl.pallas_call(
        body, name="fgate_fwd", grid=(s // TR,),
        in_specs=[_row_spec(LANES, FG_BLOCK), _full_spec((1, LANES))],
        out_specs=_row_spec(LANES), out_shape=jax.ShapeDtypeStruct((s, LANES), F32),
        scratch_shapes=[pltpu.VMEM((SUB, LANES), F32)],
        compiler_params=_params(("arbitrary",)),
    )(proj, bf_pad)


def _fgate_bwd(dfcol, proj, bf_pad):
    s = proj.shape[0]
    nb = s // TR

    def body(df_ref, fg_ref, b_ref, o_ref, db_ref, carry_ref):
        i = pl.program_id(0)

        @pl.when(i == 0)
        def _():
            carry_ref[...] = jnp.zeros_like(carry_ref)
            db_ref[...] = jnp.zeros_like(db_ref)

        row = lax.broadcasted_iota(jnp.int32, (TR, TR), 0)
        col = lax.broadcasted_iota(jnp.int32, (TR, TR), 1)
        tri = (col >= row).astype(F32)
        dlogf = _dot(tri, df_ref[...], NN, lax.Precision.HIGHEST) + carry_ref[0:1, :]
        carry_ref[...] = jnp.broadcast_to(dlogf[0:1, :], carry_ref.shape)
        z = fg_ref[...] + b_ref[...]
        dfg = dlogf * _sigmoid(-z)
        o_ref[...] = dfg.astype(BF16)
        db_ref[0:1, :] += jnp.sum(dfg, axis=0, keepdims=True)

    rev = lambda col: pl.BlockSpec((TR, LANES), lambda i, col=col: (nb - 1 - i, col))
    return pl.pallas_call(
        body, name="fgate_bwd", grid=(nb,),
        in_specs=[rev(0), rev(FG_BLOCK), _full_spec((1, LANES))],
        out_specs=(rev(0), _full_spec((SUB, LANES))),
        out_shape=(jax.ShapeDtypeStruct((s, LANES), BF16), jax.ShapeDtypeStruct((SUB, LANES), F32)),
        scratch_shapes=[pltpu.VMEM((SUB, LANES), F32)],
        compiler_params=_params(("arbitrary",)),
    )(dfcol, proj, bf_pad)


def _resid_norm2(x, z, mod, g):
    s = x.shape[0]

    def body(x_ref, z_ref, mod_ref, g_ref, x1_ref, h_ref):
        x1 = x_ref[...] + mod_ref[2:3, :] * z_ref[...]
        x1_ref[...] = x1
        r = lax.rsqrt(jnp.mean(x1 * x1, axis=-1, keepdims=True) + EPS)
        nrm = x1 * r * g_ref[...]
        h_ref[...] = (nrm * (1.0 + mod_ref[4:5, :]) + mod_ref[3:4, :]).astype(BF16)

    return pl.pallas_call(
        body, name="resid_norm2", grid=(s // TR,),
        in_specs=[_row_spec(D), _row_spec(D), _full_spec((SUB, D)), _full_spec((1, D))],
        out_specs=(_row_spec(D), _row_spec(D)),
        out_shape=(jax.ShapeDtypeStruct((s, D), F32), jax.ShapeDtypeStruct((s, D), BF16)),
        compiler_params=_params(("parallel",)),
    )(x, z, mod, g)


def _loss_head(x1, y, tgt, mod):
    s = x1.shape[0]

    def body(x1_ref, y_ref, t_ref, mod_ref, dout_ref, dy_ref, vec_ref):
        @pl.when(pl.program_id(0) == 0)
        def _():
            vec_ref[...] = jnp.zeros_like(vec_ref)

        yv = y_ref[...]
        g2 = mod_ref[5:6, :]
        diff = x1_ref[...] + g2 * yv - t_ref[...]
        dout = diff * (1.0 / D)
        dout_ref[...] = dout
        dy_ref[...] = (g2 * dout).astype(BF16)
        vec_ref[0:1, :] += jnp.sum(dout * yv, axis=0, keepdims=True)
        vec_ref[1:2, :] += jnp.sum(diff * diff, axis=0, keepdims=True)

    return pl.pallas_call(
        body, name="loss_head", grid=(s // TR,),
        in_specs=[_row_spec(D), _row_spec(D), _row_spec(D), _full_spec((SUB, D))],
        out_specs=(_row_spec(D), _row_spec(D), _full_spec((SUB, D))),
        out_shape=(jax.ShapeDtypeStruct((s, D), F32), jax.ShapeDtypeStruct((s, D), BF16),
                   jax.ShapeDtypeStruct((SUB, D), F32)),
        compiler_params=_params(("arbitrary",)),
    )(x1, y, tgt, mod)


def _norm_mod_bwd(dh, xin, dres, zin, mod, g, shift_row, scale_row, gate_row, name, hosted=None):
    s = dh.shape[0]
    with_gate = gate_row is not None

    def body(*refs):
        if with_gate:
            dh_ref, x_ref, dres_ref, z_ref, mod_ref, g_ref, dx_ref, dz_ref, vec_ref = refs
        else:
            dh_ref, x_ref, dres_ref, mod_ref, g_ref, dx_ref, vec_ref = refs

        @pl.when(pl.program_id(0) == 0)
        def _():
            vec_ref[...] = jnp.zeros_like(vec_ref)

        xv = x_ref[...]
        dhv = dh_ref[...]
        gv = g_ref[...]
        r = lax.rsqrt(jnp.mean(xv * xv, axis=-1, keepdims=True) + EPS)
        xh = xv * r
        dn = dhv * (1.0 + mod_ref[scale_row:scale_row + 1, :])
        dxh = dn * gv
        dx = dres_ref[...] + r * (dxh - xh * jnp.mean(dxh * xh, axis=-1, keepdims=True))
        dx_ref[...] = dx
        vec_ref[0:1, :] += jnp.sum(dhv, axis=0, keepdims=True)
        vec_ref[1:2, :] += jnp.sum(dhv * (xh * gv), axis=0, keepdims=True)
        vec_ref[2:3, :] += jnp.sum(dn * xh, axis=0, keepdims=True)
        if with_gate:
            dz_ref[...] = (mod_ref[gate_row:gate_row + 1, :] * dx).astype(BF16)
            vec_ref[3:4, :] += jnp.sum(dx * z_ref[...], axis=0, keepdims=True)

    ins = [dh, xin, dres] + ([zin] if with_gate else []) + [mod, g]
    in_specs = [_row_spec(D)] * (4 if with_gate else 3) + [_full_spec((SUB, D)), _full_spec((1, D))]
    out_specs = [_row_spec(D)] + ([_row_spec(D)] if with_gate else []) + [_full_spec((SUB, D))]
    out_shape = [jax.ShapeDtypeStruct((s, D), F32)] + ([jax.ShapeDtypeStruct((s, D), BF16)] if with_gate else []) \
        + [jax.ShapeDtypeStruct((SUB, D), F32)]
    outs, moved = _call(body, ins, name=name, grid=(s // TR,), in_specs=in_specs, out_specs=out_specs,
                        out_shape=out_shape, hosted=hosted)
    return outs + (moved,) if hosted else outs


XIN_BLOCK = 3 * AW // LANES
BG_BLOCK = XIN_BLOCK + CW // LANES
CG_BLOCK = BG_BLOCK + CW // LANES


def _seq_spec(s, first_block):
    return pl.BlockSpec((s, LANES), lambda j, fb=first_block: (0, fb + j))


def _mixconv_fwd(proj, w):
    s = proj.shape[0]

    def body(xin_ref, bg_ref, cg_ref, w_ref, o_ref):
        cx = cg_ref[...] * xin_ref[...]
        cv, _, _ = _conv_taps(cx, None, w_ref[...])
        o_ref[...] = bg_ref[...] * cv

    return pl.pallas_call(
        body, name="mixconv_fwd", grid=(CW // LANES,),
        in_specs=[_seq_spec(s, XIN_BLOCK), _seq_spec(s, BG_BLOCK), _seq_spec(s, CG_BLOCK),
                  pl.BlockSpec((3, LANES), lambda j: (0, j))],
        out_specs=_seq_spec(s, 0), out_shape=jax.ShapeDtypeStruct((s, CW), F32),
        compiler_params=_params(("parallel",), 32),
    )(proj, proj, proj, w)


def _mixconv_bwd(dmixed, proj, w):
    s = proj.shape[0]

    def body(d_ref, xin_ref, bg_ref, cg_ref, w_ref, dxin_ref, dbg_ref, dcg_ref, dw_ref):
        wv = w_ref[...]
        xin, cg, dconv = xin_ref[...], cg_ref[...], d_ref[...]
        cx = cg * xin
        cv, s1, s2 = _conv_taps(cx, None, wv)
        dbg_ref[...] = (dconv * cv).astype(BF16)
        dcv = dconv * bg_ref[...]
        dw_ref[...] = jnp.zeros_like(dw_ref)
        dw_ref[0:1, :] = jnp.sum(dcv * s2, axis=0, keepdims=True)
        dw_ref[1:2, :] = jnp.sum(dcv * s1, axis=0, keepdims=True)
        dw_ref[2:3, :] = jnp.sum(dcv * cx, axis=0, keepdims=True)
        dcx = _conv_taps_t(dcv, None, wv)
        dcg_ref[...] = (dcx * xin).astype(BF16)
        dxin_ref[...] = (dcx * cg).astype(BF16)

    o = jax.ShapeDtypeStruct((s, CW), BF16)
    return pl.pallas_call(
        body, name="mixconv_bwd", grid=(CW // LANES,),
        in_specs=[_seq_spec(s, AW // LANES), _seq_spec(s, XIN_BLOCK), _seq_spec(s, BG_BLOCK), _seq_spec(s, CG_BLOCK),
                  pl.BlockSpec((3, LANES), lambda j: (0, j))],
        out_specs=(_seq_spec(s, 0), _seq_spec(s, 0), _seq_spec(s, 0), pl.BlockSpec((SUB, LANES), lambda j: (0, j))),
        out_shape=(o, o, o, jax.ShapeDtypeStruct((SUB, CW), F32)),
        compiler_params=_params(("parallel",), 32),
    )(dmixed, proj, proj, proj, w)


TA = 512
NEG = -1e30


def _causal_mask():
    row = lax.broadcasted_iota(jnp.int32, (TA, TA), 0)
    col = lax.broadcasted_iota(jnp.int32, (TA, TA), 1)
    return col <= row


def _attn_fwd(qp, kp, vp, hosted):
    s = qp.shape[0]
    nq = s // TA

    def body(q_ref, k_ref, v_ref, o_ref, lse_ref):
        i = pl.program_id(1)
        slabs = [slice(SLAB * hh, SLAB * (hh + 1)) for hh in range(2)]
        q = [q_ref[:, sl] for sl in slabs]

        def block(j, carry, masked):
            keys = pl.ds(pl.multiple_of(j * TA, TA), TA)
            ms, acc = carry
            m_out, parts = [], []
            for hh in range(2):
                sc = _dot(q[hh], k_ref[keys, slabs[hh]], NT)
                if masked:
                    sc = jnp.where(_causal_mask(), sc, NEG)
                m_new = jnp.maximum(ms[hh], jnp.max(sc, axis=-1, keepdims=True))
                p = jnp.exp(sc - m_new)
                parts.append(jnp.exp(ms[hh] - m_new) * acc[:, slabs[hh]] + _dot(p.astype(BF16), v_ref[keys, slabs[hh]]))
                m_out.append(m_new)
            return tuple(m_out), jnp.concatenate(parts, axis=1)

        init = ((jnp.full((TA, 1), NEG, F32), jnp.full((TA, 1), NEG, F32)), jnp.zeros((TA, 2 * SLAB), F32))
        carry = lax.fori_loop(0, i, lambda j, cr: block(j, cr, False), init)
        ms, acc = block(i, carry, True)
        for hh in range(2):
            l = acc[:, SLAB * hh + DH:SLAB * hh + DH + 1]
            o_ref[:, DH * hh:DH * (hh + 1)] = acc[:, SLAB * hh:SLAB * hh + DH] / l
            lse_ref[0, :, hh:hh + 1] = ms[hh] + jnp.log(l)

    (o, lse), moved = _call(
        body, (qp, kp, vp), name="attn_fwd", grid=(HEADS // 2, nq),
        in_specs=[pl.BlockSpec((TA, 2 * SLAB), lambda p, i: (i, p)),
                  pl.BlockSpec((s, 2 * SLAB), lambda p, i: (0, p)),
                  pl.BlockSpec((s, 2 * SLAB), lambda p, i: (0, p))],
        out_specs=[pl.BlockSpec((TA, LANES), lambda p, i: (i, p)), pl.BlockSpec((1, TA, 2), lambda p, i: (p, i, 0))],
        out_shape=[jax.ShapeDtypeStruct((s, AW), F32), jax.ShapeDtypeStruct((HEADS // 2, s, 2), F32)],
        vmem_mb=24, hosted=hosted)
    return o, lse, moved


def _attn_bwd(qp, kp, vp, dmixed, o, lse, hosted):
    s = qp.shape[0]
    nq = s // TA

    def body(q_ref, k_ref, v_ref, do_ref, o_ref, lse_ref, dq_ref, dk_ref, dv_ref, qb_ref, dob_ref):
        dk_ref[...] = jnp.zeros_like(dk_ref)
        dv_ref[...] = jnp.zeros_like(dv_ref)
        slabs = [slice(SLAB * hh, SLAB * (hh + 1)) for hh in range(2)]
        lane = lax.broadcasted_iota(jnp.int32, (TA, DH), 1)

        def q_block(i, _):
            i0 = pl.multiple_of(i * TA, TA)
            rows = pl.ds(i0, TA)
            for hh in range(2):
                half = slice(DH * hh, DH * (hh + 1))
                do = do_ref[rows, half]
                delta = jnp.sum(do * o_ref[rows, half], axis=-1, keepdims=True)
                dob_ref[hh, :, 0:DH] = do.astype(BF16)
                dob_ref[hh, :, DH:SLAB] = _lanes3(lane, 0, [-d for d in _split3(delta)], 0.0).astype(BF16)
                lse3 = _split3(lse_ref[0, rows, hh:hh + 1])
                qb_ref[hh, :, 0:DH] = q_ref[rows, SLAB * hh:SLAB * hh + DH]
                aug = q_ref[rows, SLAB * hh + DH:SLAB * (hh + 1)].astype(F32)
                qb_ref[hh, :, DH:SLAB] = _lanes3(lane, AUG_LSE, [-x for x in lse3], aug).astype(BF16)

            def block(j, dq, masked):
                keys = pl.ds(pl.multiple_of(j * TA, TA), TA)
                dv, dk, dqc = [], [], []
                for hh in range(2):
                    q, dob = qb_ref[hh], dob_ref[hh]
                    k = k_ref[keys, slabs[hh]]
                    sc = _dot(q, k, NT)
                    if masked:
                        sc = jnp.where(_causal_mask(), sc, NEG)
                    p = jnp.exp(sc)
                    dv.append(_dot(p.astype(BF16), dob, TN))
                    ds = (p * _dot(dob, v_ref[keys, slabs[hh]], NT)).astype(BF16)
                    dk.append(_dot(ds, q, TN))
                    dqc.append(_dot(ds, k))
                dv_ref[keys, :] += jnp.concatenate(dv, axis=1)
                dk_ref[keys, :] += jnp.concatenate(dk, axis=1)
                return dq + jnp.concatenate(dqc, axis=1)

            dq = lax.fori_loop(0, i, lambda j, acc: block(j, acc, False), jnp.zeros((TA, 2 * SLAB), F32))
            dq_ref[rows, :] = block(i, dq, True)
            return 0

        lax.fori_loop(0, nq, q_block, 0)

    pair = lambda p: (0, p)
    slab2 = pl.BlockSpec((s, 2 * SLAB), pair)
    seq = pl.BlockSpec((s, LANES), pair)
    small = pl.BlockSpec((1, s, 2), lambda p: (p, 0, 0))
    o32 = jax.ShapeDtypeStruct((s, HEADS * SLAB), F32)
    return _call(
        body, (qp, kp, vp, dmixed, o, lse), name="attn_bwd", grid=(HEADS // 2,),
        in_specs=[slab2, slab2, slab2, seq, seq, small], out_specs=[slab2, slab2, slab2], out_shape=[o32, o32, o32],
        scratch_shapes=[pltpu.VMEM((2, TA, SLAB), BF16), pltpu.VMEM((2, TA, SLAB), BF16)], vmem_mb=40, hosted=hosted)


def _qkv_post(dqp, dkp, dvp, proj, gq, gk, hosted):
    s = proj.shape[0]

    def body(dq_ref, dk_ref, dv_ref, q_ref, k_ref, gq_ref, gk_ref, dqo_ref, dko_ref, dvo_ref, df_ref, vec_ref):
        @pl.when(pl.program_id(0) == 0)
        def _():
            vec_ref[...] = jnp.zeros_like(vec_ref)

        def one(d_ref, x_ref, g_ref, o_ref, row, scale):
            dg = jnp.zeros((1, DH), F32)
            for h in range(HEADS):
                sl = slice(DH * h, DH * (h + 1))
                xv = x_ref[:, sl]
                r = lax.rsqrt(jnp.mean(xv * xv, axis=-1, keepdims=True) + EPS)
                xh = xv * r
                dn = d_ref[:, SLAB * h:SLAB * h + DH] * scale
                dg = dg + jnp.sum(dn * xh, axis=0, keepdims=True)
                dxh = dn * g_ref[...]
                o_ref[:, sl] = (r * (dxh - xh * jnp.mean(dxh * xh, axis=-1, keepdims=True))).astype(BF16)
            vec_ref[row:row + 1, 0:DH] += dg

        one(dq_ref, q_ref, gq_ref, dqo_ref, 0, QK_SCALE)
        one(dk_ref, k_ref, gk_ref, dko_ref, 1, 1.0)
        lane = lax.broadcasted_iota(jnp.int32, (TR, LANES), 1)
        df = jnp.zeros((TR, LANES), F32)
        for h in range(HEADS):
            dvo_ref[:, DH * h:DH * (h + 1)] = dv_ref[:, SLAB * h:SLAB * h + DH].astype(BF16)
            row_sum = dq_ref[:, SLAB * h + DH:SLAB * h + DH + 1]
            col_sum = dk_ref[:, SLAB * h + DH + AUG_ONE:SLAB * h + DH + AUG_ONE + 1]
            df = jnp.where(lane == h, row_sum - col_sum, df)
        df_ref[...] = df

    o = jax.ShapeDtypeStruct((s, AW), BF16)
    wide = _row_spec(HEADS * SLAB)
    return _call(
        body, (dqp, dkp, dvp, proj, proj, gq, gk), name="qkv_post", grid=(s // TR,),
        in_specs=[wide, wide, wide, _row_spec(AW, 0), _row_spec(AW, 1), _full_spec((1, DH)), _full_spec((1, DH))],
        out_specs=[_row_spec(AW), _row_spec(AW), _row_spec(AW), _row_spec(LANES), _full_spec((SUB, LANES))],
        out_shape=[o, o, o, jax.ShapeDtypeStruct((s, LANES), F32), jax.ShapeDtypeStruct((SUB, LANES), F32)],
        hosted=hosted)


TF = 256
NJ = DFF // TF
FFN_ROWS_FWD = 1024
FFN_ROWS_BWD = 1024


def _ffn_fwd(h2, wup_t, cw, wd):
    s = h2.shape[0]
    tr = FFN_ROWS_FWD
    nr = s // tr

    def body(h_ref, wu_ref, cg_ref, cv_ref, wd_ref, pg_ref, pv_ref, y_ref, halo_ref, act_ref):
        r, j = pl.program_id(0), pl.program_id(1)
        hv = h_ref[...]
        pg = _dot(hv, wu_ref[0], NT).astype(BF16)
        pv = _dot(hv, wu_ref[1], NT).astype(BF16)
        pg_ref[...] = pg
        pv_ref[...] = pv
        pgf, pvf = pg.astype(F32), pv.astype(F32)
        ug, _, _ = _conv_taps(pgf, jnp.where(r > 0, halo_ref[j, 0], 0.0), cg_ref[...])
        uv, _, _ = _conv_taps(pvf, jnp.where(r > 0, halo_ref[j, 1], 0.0), cv_ref[...])
        halo_ref[j, 0] = pgf[tr - SUB:tr, :]
        halo_ref[j, 1] = pvf[tr - SUB:tr, :]
        act = (ug * _sigmoid(ug) * uv).astype(BF16)
        for t in range(NJ):
            @pl.when(j == t)
            def _(t=t):
                act_ref[:, t * TF:(t + 1) * TF] = act

        @pl.when(j == NJ - 1)
        def _():
            y_ref[...] = _dot(act_ref[...], wd_ref[...])

    pre = jax.ShapeDtypeStruct((s, DFF), BF16)
    return pl.pallas_call(
        body, name="ffn_fwd", grid=(nr, NJ),
        in_specs=[pl.BlockSpec((tr, D), lambda r, j: (r, 0)),
                  pl.BlockSpec((2, TF, D), lambda r, j: (0, j, 0)),
                  pl.BlockSpec((3, TF), lambda r, j: (0, j)),
                  pl.BlockSpec((3, TF), lambda r, j: (0, NJ + j)),
                  pl.BlockSpec((DFF, D), lambda r, j: (0, 0))],
        out_specs=(pl.BlockSpec((tr, TF), lambda r, j: (r, j)),
                   pl.BlockSpec((tr, TF), lambda r, j: (r, j)),
                   pl.BlockSpec((tr, D), lambda r, j: (r, 0))),
        out_shape=(pre, pre, jax.ShapeDtypeStruct((s, D), F32)),
        scratch_shapes=[pltpu.VMEM((NJ, 2, SUB, TF), F32), pltpu.VMEM((tr, DFF), BF16)],
        compiler_params=_params(("arbitrary", "arbitrary"), 56),
    )(h2, wup_t, cw, cw, wd)


def _ffn_bwd(dy, h2, pre_g, pre_v, wup_t, cw, wd):
    s = h2.shape[0]
    tr = FFN_ROWS_BWD
    nr = s // tr
    hb = tr // (2 * SUB)

    def body(dy_ref, h_ref, pg_ref, pv_ref, hg_ref, hv_ref, wu_ref, cg_ref, cv_ref, wd_ref,
             dh_ref, dwu_ref, dwd_ref, dcg_ref, dcv_ref, nxt_ref, awu_ref, awd_ref):
        j, r = pl.program_id(0), pl.program_id(1)
        rr = nr - 1 - r
        row0 = pl.multiple_of(rr * tr, tr)
        cwg, cwv = cg_ref[...], cv_ref[...]
        pg, pv = pg_ref[...].astype(F32), pv_ref[...].astype(F32)
        ug, g1, g2 = _conv_taps(pg, jnp.where(rr > 0, hg_ref[SUB:2 * SUB, :].astype(F32), 0.0), cwg)
        uv, v1, v2 = _conv_taps(pv, jnp.where(rr > 0, hv_ref[SUB:2 * SUB, :].astype(F32), 0.0), cwv)
        sg = _sigmoid(ug)
        sil = ug * sg
        act = (sil * uv).astype(BF16)
        dyv = dy_ref[...]
        da = _dot(dyv, wd_ref[...], NT)
        dug = da * uv * (sg * (1.0 + ug * (1.0 - sg)))
        duv = da * sil
        dpg = _conv_taps_t(dug, jnp.where(r > 0, nxt_ref[0], 0.0), cwg)
        dpv = _conv_taps_t(duv, jnp.where(r > 0, nxt_ref[1], 0.0), cwv)
        nxt_ref[0] = dug[0:SUB, :]
        nxt_ref[1] = duv[0:SUB, :]
        dpgb, dpvb = dpg.astype(BF16), dpv.astype(BF16)
        hv = h_ref[...]
        dwd = _dot(act, dyv, TN)
        dpb = jnp.concatenate([dpgb, dpvb], axis=1)
        dwu = _dot(dpb, hv, TN)
        dh = _dot(dpb, wu_ref[...].reshape(2 * TF, D))

        def taps(du, x0, x1, x2):
            return (jnp.sum(du * x2, axis=0, keepdims=True), jnp.sum(du * x1, axis=0, keepdims=True),
                    jnp.sum(du * x0, axis=0, keepdims=True))

        tg, tv = taps(dug, pg, g1, g2), taps(duv, pv, v1, v2)

        @pl.when(r == 0)
        def _():
            awd_ref[...] = dwd
            awu_ref[...] = dwu
            dcg_ref[...] = jnp.zeros_like(dcg_ref)
            dcv_ref[...] = jnp.zeros_like(dcv_ref)

        @pl.when(r > 0)
        def _():
            awd_ref[...] += dwd
            awu_ref[...] += dwu

        @pl.when(r == nr - 1)
        def _():
            dwd_ref[...] = awd_ref[...].astype(BF16)
            dwu_ref[...] = awu_ref[...].astype(BF16).reshape(2, TF, D)

        for t in range(3):
            dcg_ref[t:t + 1, :] += tg[t]
            dcv_ref[t:t + 1, :] += tv[t]

        @pl.when(j == 0)
        def _():
            dh_ref[pl.ds(row0, tr), :] = dh

        @pl.when(j > 0)
        def _():
            dh_ref[pl.ds(row0, tr), :] += dh

    rows = lambda j, r: (nr - 1 - r, 0)
    tile = lambda j, r: (nr - 1 - r, j)
    halo = lambda j, r: (jnp.maximum((nr - 1 - r) * hb - 1, 0), j)
    return pl.pallas_call(
        body, name="ffn_bwd", grid=(NJ, nr),
        in_specs=[pl.BlockSpec((tr, D), rows), pl.BlockSpec((tr, D), rows),
                  pl.BlockSpec((tr, TF), tile), pl.BlockSpec((tr, TF), tile),
                  pl.BlockSpec((2 * SUB, TF), halo), pl.BlockSpec((2 * SUB, TF), halo),
                  pl.BlockSpec((2, TF, D), lambda j, r: (0, j, 0)),
                  pl.BlockSpec((3, TF), lambda j, r: (0, j)), pl.BlockSpec((3, TF), lambda j, r: (0, NJ + j)),
                  pl.BlockSpec((TF, D), lambda j, r: (j, 0))],
        out_specs=(pl.BlockSpec((s, D), lambda j, r: (0, 0)),
                   pl.BlockSpec((2, TF, D), lambda j, r: (0, j, 0)),
                   pl.BlockSpec((TF, D), lambda j, r: (j, 0)),
                   pl.BlockSpec((SUB, TF), lambda j, r: (0, j)), pl.BlockSpec((SUB, TF), lambda j, r: (0, j))),
        out_shape=(jax.ShapeDtypeStruct((s, D), F32),
                   jax.ShapeDtypeStruct((2, DFF, D), BF16), jax.ShapeDtypeStruct((DFF, D), BF16),
                   jax.ShapeDtypeStruct((SUB, DFF), F32), jax.ShapeDtypeStruct((SUB, DFF), F32)),
        scratch_shapes=[pltpu.VMEM((2, SUB, TF), F32), pltpu.VMEM((2 * TF, D), F32), pltpu.VMEM((TF, D), F32)],
        compiler_params=_params(("arbitrary", "arbitrary"), 56),
    )(dy, h2, pre_g, pre_v, pre_g, pre_v, wup_t, cw, cw, wd)


def _adam(w, g, m, v):
    m = ADAM_B1 * m + (1.0 - ADAM_B1) * g
    v = ADAM_B2 * v + (1.0 - ADAM_B2) * (g * g)
    m_hat = m / (1.0 - ADAM_B1 ** ADAM_STEP)
    v_hat = v / (1.0 - ADAM_B2 ** ADAM_STEP)
    delta = -ADAM_LR * (m_hat / (jnp.sqrt(v_hat) + ADAM_EPS) + ADAM_WD * w)
    return delta, m, v


NCHIP = NDEV // 2


def _pair_add(mine, theirs, tr, name):
    _, _, rws, cols = mine.shape

    def body(a_ref, b_ref, o_ref):
        c = lax.axis_index("c")
        o_ref[0] = (a_ref[0, c].astype(F32) + b_ref[0].astype(F32)).astype(BF16)

    (out,), _ = _call(
        body, (mine, theirs), name=name, grid=(NCHIP, rws // tr),
        in_specs=[pl.BlockSpec((1, 2, tr, cols), lambda q, i: (q, 0, i, 0)),
                  pl.BlockSpec((1, tr, cols), lambda q, i: (q, i, 0))],
        out_specs=[pl.BlockSpec((1, tr, cols), lambda q, i: (q, i, 0))],
        out_shape=[jax.ShapeDtypeStruct((NCHIP, rws, cols), BF16)], vmem_mb=16)
    return out


def _adamw_sharded(parts, w, m, v, tr, name, hosted=None):
    rws, cols = w.shape

    def body(p_ref, w_ref, m_ref, v_ref, g_ref, d_ref, mo_ref, vo_ref):
        g = p_ref[0].astype(F32)
        for q in range(1, NCHIP):
            g = g + p_ref[q].astype(F32)
        g_ref[...] = g
        d_ref[...], mo_ref[...], vo_ref[...] = _adam(w_ref[...], g, m_ref[...], v_ref[...])

    blk = pl.BlockSpec((tr, cols), lambda i: (i, 0))
    o = jax.ShapeDtypeStruct((rws, cols), F32)
    outs, moved = _call(
        body, (parts, w, m, v), name=name, grid=(rws // tr,),
        in_specs=[pl.BlockSpec((NCHIP, tr, cols), lambda i: (0, i, 0)), blk, blk, blk],
        out_specs=[blk, blk, blk, blk], out_shape=[o, o, o, o], vmem_mb=36 if tr > 256 else 24, hosted=hosted)
    return (outs, moved) if hosted else outs


def _adamw_ada(c_all, dmod_my, w, m, v):
    rws, cols = w.shape
    tr = 256

    def body(c_ref, dm_ref, w_ref, m_ref, v_ref, g_ref, d_ref, mo_ref, vo_ref):
        cv = c_ref[...]
        act = cv * _sigmoid(cv)
        g = _dot(act, dm_ref[...], TN, lax.Precision.HIGHEST)
        g_ref[...] = g
        d_ref[...], mo_ref[...], vo_ref[...] = _adam(w_ref[...], g, m_ref[...], v_ref[...])

    blk = pl.BlockSpec((tr, cols), lambda i: (i, 0))
    o = jax.ShapeDtypeStruct((rws, cols), F32)
    return pl.pallas_call(
        body, name="adamw_ada", grid=(rws // tr,),
        in_specs=[pl.BlockSpec((NDEV, tr), lambda i: (0, i)), _full_spec((NDEV, cols)), blk, blk, blk],
        out_specs=(blk, blk, blk, blk), out_shape=(o, o, o, o),
        compiler_params=_params(("parallel",), 32),
    )(c_all, dmod_my, w, m, v)


REP_ROWS = 16
ROW_N1, ROW_N2, ROW_LOSS, ROW_MISC = 6, 7, 8, 9
LANE_BF, LANE_GQ, LANE_GK = 0, 128, 256


def _adamw_small(rep_all, conv_all, wmv):
    n_ff = wmv[6][0].shape[1]

    def body(*refs):
        rep_ref, conv_ref = refs[:2]
        ins = refs[2:2 + 24]
        outs = refs[2 + 24:]
        loss_ref, outs = outs[0], outs[1:]
        g_rep = rep_ref[0]
        g_conv = conv_ref[0]
        for d in range(1, NDEV):
            g_rep = g_rep + rep_ref[d]
            g_conv = g_conv + conv_ref[d]
        loss_ref[...] = (0.5 / D) * jnp.sum(g_rep[ROW_LOSS:ROW_LOSS + 1, :], axis=-1, keepdims=True)
        grads = [
            None,
            g_rep[ROW_N1:ROW_N1 + 1, :],
            g_rep[ROW_MISC:ROW_MISC + 1, LANE_BF:LANE_BF + HEADS],
            g_rep[ROW_MISC:ROW_MISC + 1, LANE_GQ:LANE_GQ + DH],
            g_rep[ROW_MISC:ROW_MISC + 1, LANE_GK:LANE_GK + DH],
            g_rep[ROW_N2:ROW_N2 + 1, :],
            g_conv[0:3, 0:n_ff],
            g_conv[0:3, n_ff:n_ff + DH],
        ]
        for p in range(8):
            w_ref, m_ref, v_ref = ins[3 * p:3 * p + 3]
            g_ref, d_ref, mo_ref, vo_ref = outs[4 * p:4 * p + 4]
            if p == 0:
                for nmod in range(NMOD):
                    sl = slice(D * nmod, D * (nmod + 1))
                    g = g_rep[nmod:nmod + 1, :]
                    g_ref[:, sl] = g
                    d_ref[:, sl], mo_ref[:, sl], vo_ref[:, sl] = _adam(w_ref[:, sl], g, m_ref[:, sl], v_ref[:, sl])
            else:
                g = grads[p]
                g_ref[...] = g
                d_ref[...], mo_ref[...], vo_ref[...] = _adam(w_ref[...], g, m_ref[...], v_ref[...])

    flat = [a for trio in wmv for a in trio]
    out_shape = [jax.ShapeDtypeStruct((1, 1), F32)]
    for trio in wmv:
        out_shape += [jax.ShapeDtypeStruct(trio[0].shape, F32)] * 4
    return pl.pallas_call(
        body, name="adamw_small", out_shape=tuple(out_shape),
        compiler_params=_params(None, 32),
    )(rep_all, conv_all, *flat)


FG_FIRST = 3 * AW
N_IN = DIN // NDEV


def _w_in_runs():
    runs = []
    for d in range(NDEV):
        lo, hi = N_IN * d, N_IN * (d + 1)
        for a, b, shift in ((0, FG_FIRST, 0), (FG_FIRST, FG_FIRST + HEADS, DIN - HEADS - FG_FIRST),
                            (FG_FIRST + HEADS, DIN, -HEADS)):
            a, b = max(a, lo), min(b, hi)
            if a < b:
                runs.append((d, a - lo, a + shift, b - a))
    return runs


W_IN_ROWS = 256
N_IN_PAD = 512


def _identity(n):
    return (lax.broadcasted_iota(jnp.int32, (n, n), 0) == lax.broadcasted_iota(jnp.int32, (n, n), 1)).astype(BF16)


def _assemble_w_in(g_in, hosted):
    def body(g_ref, o_ref, t_ref):
        eye = _identity(W_IN_ROWS)
        shard = None
        for d, src, dst, width in _w_in_runs():
            if d != shard:
                t_ref[:, 0:N_IN] = _dot(eye, g_ref[d], NT).astype(BF16)
                shard = d
            o_ref[:, dst:dst + width] = t_ref[:, src:src + width]
        o_ref[:, DIN:DINP] = jnp.zeros((W_IN_ROWS, DINP - DIN), o_ref.dtype)

    (out,), moved = _call(
        body, (g_in,), name="assemble_w_in", grid=(D // W_IN_ROWS,),
        in_specs=[pl.BlockSpec((NDEV, N_IN, W_IN_ROWS), lambda i: (0, 0, i))],
        out_specs=[pl.BlockSpec((W_IN_ROWS, DINP), lambda i: (i, 0))],
        out_shape=[jax.ShapeDtypeStruct((D, DINP), g_in.dtype)],
        scratch_shapes=[pltpu.VMEM((W_IN_ROWS, N_IN_PAD), BF16)], vmem_mb=16, hosted=hosted)
    return out, moved


def _scatter_dw_in(dwp):
    def body(w_ref, o_ref, t_ref):
        eye = _identity(W_IN_ROWS)
        runs = _w_in_runs()
        for i, (d, src, dst, width) in enumerate(runs):
            t_ref[:, src:src + width] = w_ref[:, dst:dst + width]
            if i + 1 == len(runs) or runs[i + 1][0] != d:
                o_ref[d // 2, d % 2] = _dot(t_ref[:, 0:N_IN], eye, TN).astype(BF16)

    (out,), _ = _call(
        body, (dwp,), name="scatter_dw_in", grid=(D // W_IN_ROWS,),
        in_specs=[pl.BlockSpec((W_IN_ROWS, DINP), lambda i: (i, 0))],
        out_specs=[pl.BlockSpec((NCHIP, 2, N_IN, W_IN_ROWS), lambda i: (0, 0, 0, i))],
        out_shape=[jax.ShapeDtypeStruct((NCHIP, 2, N_IN, D), dwp.dtype)],
        scratch_shapes=[pltpu.VMEM((W_IN_ROWS, N_IN_PAD), BF16)], vmem_mb=16)
    return out


def kernel(x, c, w_ada, b_ada, norm1_g, w_in, b_forget, q_norm_g, k_norm_g, conv_mix_w, w_out, norm2_g, w_up, ffn_conv_w, w_down, loss_target, m_w_ada, m_b_ada, m_norm1_g, m_w_in, m_b_forget, m_q_norm_g, m_k_norm_g, m_conv_mix_w, m_w_out, m_norm2_g, m_w_up, m_ffn_conv_w, m_w_down, v_w_ada, v_b_ada, v_norm1_g, v_w_in, v_b_forget, v_q_norm_g, v_k_norm_g, v_conv_mix_w, v_w_out, v_norm2_g, v_w_up, v_ffn_conv_w, v_w_down):
    me = 4 * lax.axis_index("x") + 2 * lax.axis_index("y") + lax.axis_index("c")
    xs, tgt = x[0], loss_target[0]
    s = xs.shape[0]
    nq = s // TA
    n_ada = w_ada.shape[2]
    n_ff = w_up.shape[2]

    conv_w = jnp.concatenate([ffn_conv_w[0], conv_mix_w[0]], axis=1)
    conv_w = jnp.concatenate([conv_w, jnp.zeros((SUB - 3, conv_w.shape[1]), F32)], axis=0)
    c_all, conv_all, g_in = _exchange(
        [(c.reshape(SUB, D // SUB), "ag"), (conv_w, "ag"), (jnp.transpose(w_in[0]).astype(BF16), "ag2")],
        "exchange_w_in")
    c_all = c_all.reshape(NDEV, D)
    cw_ffn = jnp.transpose(conv_all[:, :3, :n_ff], (1, 0, 2)).reshape(3, 2 * DFF)
    cw_mix = jnp.transpose(conv_all[:, :3, n_ff:], (1, 0, 2)).reshape(3, CW)

    b_my = lax.dynamic_slice(b_ada, (0, me * n_ada), (1, n_ada))
    mod_part = _ada_fwd(c_all, w_ada[0], b_my)
    w_in_p, (mod_rows,) = _assemble_w_in(
        g_in, [(jnp.broadcast_to(mod_part[:, None, :], (NDEV, SUB, n_ada)), "a2a")])
    mod = mod_rows[:, 0, :].reshape(NMOD, D)
    mod = jnp.concatenate([mod, jnp.zeros((SUB - NMOD, D), F32)], axis=0)

    h = _norm_mod_fwd(xs, mod, norm1_g)
    proj, (g_down,) = _mm(h, w_in_p, "nn", F32, 1024, 640, "proj_fwd", hosted=[(w_down[0].astype(BF16), "ag2")])
    bf_pad = jnp.concatenate([b_forget, jnp.zeros((1, LANES - HEADS), F32)], axis=1)
    fcum = _fgate_fwd(proj, bf_pad)
    (qp, kp, vp), (g_out,) = _qkv_prep(proj, fcum, q_norm_g, k_norm_g, [(w_out[0].astype(BF16), "ag2")])
    attn, lse, (g_up,) = _attn_fwd(qp, kp, vp, [(jnp.transpose(w_up[0]).astype(BF16), "ag2")])
    w_out_f = g_out.reshape(D, D)
    w_up_t = g_up.reshape(2, DFF, D)
    w_down_f = g_down.reshape(DFF, D)
    conv = _mixconv_fwd(proj, cw_mix)
    mixed = jnp.concatenate([attn, conv], axis=1).astype(BF16)
    z = _mm(mixed, w_out_f, "nn", F32, 1024, 1024, "out_fwd")
    x1, h2 = _resid_norm2(xs, z, mod, norm2_g)
    pre_g, pre_v, y = _ffn_fwd(h2, w_up_t, cw_ffn, w_down_f)
    dout, dy, vec_l = _loss_head(x1, y, tgt, mod)

    dh2, dwup_t, dwd, dcw_g, dcw_v = _ffn_bwd(dy, h2, pre_g, pre_v, w_up_t, cw_ffn, w_down_f)
    dx1, dz, vec_2 = _norm_mod_bwd(dh2, x1, dout, z, mod, norm2_g, 3, 4, 2, "norm2_bwd")
    dwout = _mm(mixed, dz, "tn", BF16, 1024, 1024, "out_bwd_w")
    s_out = dwout.reshape(NCHIP, 2, D // NDEV, D)
    s_down = dwd.reshape(NCHIP, 2, DFF // NDEV, D)
    s_up = dwup_t.reshape(NCHIP, 2, n_ff, D)
    dmixed, (t_out, t_up, t_down) = _mm(dz, w_out_f, "nt", F32, 1024, 1024, "out_bwd_x",
                                        hosted=[(s_out, "pair"), (s_up, "pair"), (s_down, "pair")])
    c_out = _pair_add(s_out, t_out, 128, "pair_add_out")
    c_up = _pair_add(s_up, t_up, 176, "pair_add_up")
    c_down = _pair_add(s_down, t_down, 176, "pair_add_down")
    dxin, dbg, dcg, dcw_mix = _mixconv_bwd(dmixed, proj, cw_mix)
    (dqp, dkp, dvp), (p_up,) = _attn_bwd(qp, kp, vp, dmixed, attn, lse, [(c_up, "chips")])
    (dq, dk, dvb, dfcol, vec_qk), (p_out, p_down) = _qkv_post(
        dqp, dkp, dvp, proj, q_norm_g, k_norm_g, [(c_out, "chips"), (c_down, "chips")])
    dfg, vec_bf = _fgate_bwd(dfcol, proj, bf_pad)
    dproj = jnp.concatenate([dq, dk, dvb, dxin, dbg, dcg, dfg], axis=1)
    dwin_p = _mm(h, dproj, "tn", BF16, 1024, 640, "proj_bwd_w")
    s_in = _scatter_dw_in(dwin_p)
    (t_in,) = _exchange([(s_in, "pair")], "exchange_pair_in")
    c_in = _pair_add(s_in, t_in, N_IN, "pair_add_in")
    dh, (p_in,) = _mm(dproj, w_in_p, "nt", F32, 1024, 512, "proj_bwd_x", hosted=[(c_in, "chips")], vmem_mb=36)
    grad_x, vec_1 = _norm_mod_bwd(dh, xs, dx1, None, mod, norm1_g, 0, 1, None, "norm1_bwd")

    misc = jnp.zeros((1, D), F32)
    misc = lax.dynamic_update_slice(misc, vec_bf[0:1, :HEADS], (0, LANE_BF))
    misc = lax.dynamic_update_slice(misc, vec_qk[0:1, :DH], (0, LANE_GQ))
    misc = lax.dynamic_update_slice(misc, vec_qk[1:2, :DH], (0, LANE_GK))
    rep = jnp.concatenate([
        vec_1[0:1], vec_1[1:2], vec_2[3:4], vec_2[0:1], vec_2[1:2], vec_l[0:1],
        vec_1[2:3], vec_2[2:3], vec_l[1:2], misc, jnp.zeros((REP_ROWS - 10, D), F32)], axis=0)
    dcw_ffn = jnp.concatenate([dcw_g, dcw_v], axis=1).reshape(SUB, NDEV, n_ff)
    dcw_all = jnp.concatenate([jnp.transpose(dcw_ffn, (1, 0, 2)),
                               jnp.transpose(dcw_mix.reshape(SUB, NDEV, DH), (1, 0, 2))], axis=2)
    r_out, (rep_all, conv_parts) = _adamw_sharded(p_out, w_out[0], m_w_out[0], v_w_out[0], 128, "adamw_out",
                                                  hosted=[(rep, "ag"), (dcw_all, "a2a")])
    dmod_my = lax.dynamic_slice(rep_all[:, :NMOD, :].reshape(NDEV, NMOD * D), (0, me * n_ada), (NDEV, n_ada))
    r_ada = _adamw_ada(c_all, dmod_my, w_ada[0], m_w_ada[0], v_w_ada[0])
    r_in = _adamw_sharded(p_in, jnp.transpose(w_in[0]), jnp.transpose(m_w_in[0]), jnp.transpose(v_w_in[0]), N_IN,
                          "adamw_in")
    r_in = tuple(jnp.transpose(a) for a in r_in)
    r_up = _adamw_sharded(p_up, jnp.transpose(w_up[0]), jnp.transpose(m_w_up[0]), jnp.transpose(v_w_up[0]), 176,
                          "adamw_up")
    r_up = tuple(jnp.transpose(a) for a in r_up)
    r_down = _adamw_sharded(p_down, w_down[0], m_w_down[0], v_w_down[0], 176, "adamw_down")
    small = _adamw_small(rep_all, conv_parts, [
        [b_ada, m_b_ada, v_b_ada], [norm1_g, m_norm1_g, v_norm1_g], [b_forget, m_b_forget, v_b_forget],
        [q_norm_g, m_q_norm_g, v_q_norm_g], [k_norm_g, m_k_norm_g, v_k_norm_g], [norm2_g, m_norm2_g, v_norm2_g],
        [ffn_conv_w[0], m_ffn_conv_w[0], v_ffn_conv_w[0]], [conv_mix_w[0], m_conv_mix_w[0], v_conv_mix_w[0]]])
    loss = small[0].reshape(())
    r_bada, r_n1, r_bf, r_gq, r_gk, r_n2, r_cf, r_cm = [small[1 + 4 * p:5 + 4 * p] for p in range(8)]
    lead = lambda t: tuple(a[None] for a in t)
    per_w = [lead(r_ada), r_bada, r_n1, lead(r_in), r_bf, r_gq, r_gk, lead(r_cm), lead(r_out), r_n2,
             lead(r_up), lead(r_cf), lead(r_down)]
    outs = [loss, grad_x[None]]
    for field in range(4):
        outs += [t[field] for t in per_w]
    return tuple(outs)
```

```python
import functools

import jax
import jax.numpy as jnp
import numpy as np
from jax import lax
from jax.experimental import pallas as pl
from jax.experimental.pallas import tpu as pltpu
from jax.experimental.pallas import tpu_sc as plsc

F32 = jnp.float32
BF16 = jnp.bfloat16

NDEV = 8
D = 1024
HEADS = 8
DH = 64
AW = 512
CW = 512
DFF = 2816
DIN = 3080
DINP = 3200
NMOD = 6
EPS = 1e-6
QK_SCALE = 0.125
LANES = 128
SUB = 8

ADAM_LR = 0.001
ADAM_B1 = 0.9
ADAM_B2 = 0.999
ADAM_EPS = 1e-08
ADAM_WD = 0.01
ADAM_STEP = 10

MESH = pl.DeviceIdType.MESH
ANY = pl.BlockSpec(memory_space=pl.ANY)

NN = (((1,), (0,)), ((), ()))
NT = (((1,), (1,)), ((), ()))
TN = (((0,), (0,)), ((), ()))


def _dot(a, b, dims=NN, precision=None):
    return lax.dot_general(a, b, dims, precision=precision, preferred_element_type=F32)


def _params(sem=None, vmem_mb=None):
    kw = {}
    if sem is not None:
        kw["dimension_semantics"] = sem
    if vmem_mb is not None:
        kw["vmem_limit_bytes"] = vmem_mb * 1024 * 1024
    return pltpu.CompilerParams(**kw)


def _sigmoid(x):
    return 0.5 * jnp.tanh(0.5 * x) + 0.5


class _Exchange:
    def __init__(self, items):
        self.arrays = [a for a, _ in items]
        self.modes = [m for _, m in items]
        self.n = len(items)
        self.out_shape = []
        for a, m in items:
            sh = {"ag": (NDEV,) + a.shape, "ag2": (NDEV,) + a.shape, "pair": a.shape[:1] + a.shape[2:]}.get(m, a.shape)
            self.out_shape.append(jax.ShapeDtypeStruct(sh, a.dtype))
        self.scratch = [pltpu.SemaphoreType.DMA((self.n, NDEV - 1)), pltpu.SemaphoreType.DMA((self.n, NDEV - 1)),
                        pltpu.SemaphoreType.DMA((self.n,))]

    def _plan(self, srcs, outs, sems):
        send_sems, recv_sems, loc_sems = sems
        x, y, c = lax.axis_index("x"), lax.axis_index("y"), lax.axis_index("c")
        me, my_chip = 4 * x + 2 * y + c, 2 * x + y
        sib = (x, y, 1 - c)
        local, first, landed, forwards, arrivals = [], [], [], [], []

        def remote(a, k, src, dst, to):
            return pltpu.make_async_remote_copy(src_ref=src, dst_ref=dst, send_sem=send_sems.at[a, k],
                                                recv_sem=recv_sems.at[a, k], device_id=to, device_id_type=MESH)

        for a, mode in enumerate(self.modes):
            src, out = srcs[a], outs[a]
            if mode in ("ag", "a2a"):
                piece = (lambda slot, src=src: src) if mode == "ag" else (lambda slot, src=src: src.at[slot])
                local.append(pltpu.make_async_copy(piece(me), out.at[me], loc_sems.at[a]))
                for r in range(1, NDEV):
                    px = 1 - x if (r >> 2) & 1 else x
                    py = 1 - y if (r >> 1) & 1 else y
                    pc = 1 - c if r & 1 else c
                    pidx = 4 * px + 2 * py + pc
                    first.append(remote(a, r - 1, piece(pidx), out.at[me], (px, py, pc)))
                    arrivals.append(remote(a, r - 1, piece(pidx), out.at[pidx], (px, py, pc)))
            elif mode == "ag2":
                local.append(pltpu.make_async_copy(src, out.at[me], loc_sems.at[a]))
                first.append(remote(a, 0, src, out.at[me], sib))
                arrivals.append(remote(a, 0, src, out.at[me + 1 - 2 * c], sib))
                for j, (px, py) in enumerate([(1 - x, y), (x, 1 - y), (1 - x, 1 - y)]):
                    theirs = out.at[4 * px + 2 * py + c]
                    first.append(remote(a, 1 + j, src, out.at[me], (px, py, c)))
                    landed.append(remote(a, 1 + j, src, theirs, (px, py, c)))
                    forwards.append(remote(a, 4 + j, theirs, theirs, sib))
                    arrivals.append(remote(a, 4 + j, src, out.at[4 * px + 2 * py + 1 - c], sib))
            elif mode == "pair":
                for q in range(NDEV // 2):
                    first.append(remote(a, q, src.at[q, 1 - c], out.at[q], sib))
                    arrivals.append(remote(a, q, src.at[q, 1 - c], out.at[q], sib))
            else:
                assert mode == "chips", mode
                local.append(pltpu.make_async_copy(src.at[my_chip], out.at[my_chip], loc_sems.at[a]))
                for j, (px, py) in enumerate([(1 - x, y), (x, 1 - y), (1 - x, 1 - y)]):
                    q = 2 * px + py
                    first.append(remote(a, 1 + j, src.at[q], out.at[my_chip], (px, py, c)))
                    arrivals.append(remote(a, 1 + j, src.at[q], out.at[q], (px, py, c)))
        return local, first, landed, forwards, arrivals

    def start(self, srcs, outs, sems):
        local, first, _, _, _ = self._plan(srcs, outs, sems)
        for cp in local + first:
            cp.start()

    def wait(self, srcs, outs, sems):
        local, first, landed, forwards, arrivals = self._plan(srcs, outs, sems)
        for cp, fwd in zip(landed, forwards):
            cp.wait_recv()
            fwd.start()
        for cp in arrivals:
            cp.wait_recv()
        for cp in first + forwards:
            cp.wait_send()
        for cp in local:
            cp.wait()


def _exchange(items, name):
    ex = _Exchange(items)
    n = ex.n

    def body(*refs):
        srcs, outs, sems = refs[:n], refs[n:2 * n], refs[2 * n:]
        ex.start(srcs, outs, sems)
        ex.wait(srcs, outs, sems)

    return pl.pallas_call(
        body, name=name,
        out_shape=tuple(ex.out_shape),
        in_specs=[ANY] * n, out_specs=tuple([ANY] * n),
        scratch_shapes=ex.scratch,
        compiler_params=pltpu.CompilerParams(has_side_effects=True),
    )(*ex.arrays)


def _sequencer_exchange(items, name, collective_id):
    ex = _Exchange(items)
    srcs = [jax.new_ref(a, memory_space=pltpu.MemorySpace.HBM) for a in ex.arrays]
    outs = [jax.empty_ref(sh, memory_space=pltpu.MemorySpace.HBM) for sh in ex.out_shape]

    @pl.kernel(mesh=plsc.ScalarSubcoreMesh(axis_name="sequencer", num_cores=1), name=name,
               scratch_types=tuple(ex.scratch), compiler_params=pltpu.CompilerParams(collective_id=collective_id))
    def launch(send_sems, recv_sems, loc_sems):
        x, y, c = lax.axis_index("x"), lax.axis_index("y"), lax.axis_index("c")
        barrier = pltpu.get_barrier_semaphore()
        peers = [(x, y, 1 - c), (1 - x, y, c), (x, 1 - y, c), (1 - x, 1 - y, c)]
        for peer in peers:
            pl.semaphore_signal(barrier, inc=1, device_id=peer, device_id_type=MESH)
        pl.semaphore_wait(barrier, len(peers))
        sems = (send_sems, recv_sems, loc_sems)
        ex.start(srcs, outs, sems)
        ex.wait(srcs, outs, sems)

    launch()
    return [o[...] for o in outs]


def _call(body, inputs, *, name, grid, in_specs, out_specs, out_shape, scratch_shapes=(), vmem_mb=None, hosted=None):
    out_specs, out_shape, scratch_shapes = tuple(out_specs), tuple(out_shape), list(scratch_shapes)
    if not hosted:
        res = pl.pallas_call(
            body, name=name, grid=grid, in_specs=list(in_specs), out_specs=out_specs, out_shape=out_shape,
            scratch_shapes=scratch_shapes, compiler_params=_params(("arbitrary",) * len(grid), vmem_mb),
        )(*inputs)
        return tuple(res), ()
    ex = _Exchange(hosted)
    n, n_in, n_out, n_scr = ex.n, len(inputs), len(out_shape), len(scratch_shapes)

    def hosting_body(*refs):
        ins, srcs = refs[:n_in], refs[n_in:n_in + n]
        outs, landing = refs[n_in + n:n_in + n + n_out], refs[n_in + n + n_out:n_in + 2 * n + n_out]
        scratch, sems = refs[n_in + 2 * n + n_out:n_in + 2 * n + n_out + n_scr], refs[n_in + 2 * n + n_out + n_scr:]
        first = functools.reduce(jnp.logical_and, [pl.program_id(d) == 0 for d in range(len(grid))])
        last = functools.reduce(jnp.logical_and, [pl.program_id(d) == grid[d] - 1 for d in range(len(grid))])

        @pl.when(first)
        def _():
            ex.start(srcs, landing, sems)

        body(*ins, *outs, *scratch)

        @pl.when(last)
        def _():
            ex.wait(srcs, landing, sems)

    res = pl.pallas_call(
        hosting_body, name=name, grid=grid,
        in_specs=list(in_specs) + [ANY] * n, out_specs=out_specs + tuple([ANY] * n),
        out_shape=out_shape + tuple(ex.out_shape), scratch_shapes=scratch_shapes + ex.scratch,
        compiler_params=_params(("arbitrary",) * len(grid), vmem_mb),
    )(*inputs, *ex.arrays)
    return tuple(res[:n_out]), tuple(res[n_out:])


def _mm(a, b, mode, out_dtype, tm, tn, name, hosted=None, vmem_mb=24):
    if mode == "nn":
        (m, k), n = a.shape, b.shape[1]
        a_spec = pl.BlockSpec((tm, k), lambda i, j: (i, 0))
        b_spec = pl.BlockSpec((k, tn), lambda i, j: (0, j))
        dims = NN
    elif mode == "nt":
        (m, k), n = a.shape, b.shape[0]
        a_spec = pl.BlockSpec((tm, k), lambda i, j: (i, 0))
        b_spec = pl.BlockSpec((tn, k), lambda i, j: (j, 0))
        dims = NT
    else:
        (k, m), n = a.shape, b.shape[1]
        a_spec = pl.BlockSpec((k, tm), lambda i, j: (0, i))
        b_spec = pl.BlockSpec((k, tn), lambda i, j: (0, j))
        dims = TN
    assert m % tm == 0 and n % tn == 0, (m, n, tm, tn)

    def body(a_ref, b_ref, o_ref):
        o_ref[...] = _dot(a_ref[...], b_ref[...], dims).astype(o_ref.dtype)

    (out,), moved = _call(
        body, (a, b), name=name, grid=(m // tm, n // tn),
        in_specs=[a_spec, b_spec], out_specs=[pl.BlockSpec((tm, tn), lambda i, j: (i, j))],
        out_shape=[jax.ShapeDtypeStruct((m, n), out_dtype)], vmem_mb=vmem_mb, hosted=hosted)
    return (out, moved) if hosted else out


def _shift_down(x, k, fill):
    y = pltpu.roll(x, k, 0)
    row = lax.broadcasted_iota(jnp.int32, (SUB, x.shape[1]), 0)
    head = y[0:SUB, :]
    for t in range(k):
        head = jnp.where(row == t, fill[t], head)
    return jnp.concatenate([head, y[SUB:, :]], axis=0)


def _shift_up(x, k, fill):
    n = x.shape[0]
    y = pltpu.roll(x, n - k, 0)
    row = lax.broadcasted_iota(jnp.int32, (SUB, x.shape[1]), 0)
    tail = y[n - SUB:, :]
    for t in range(k):
        tail = jnp.where(row == SUB - k + t, fill[t], tail)
    return jnp.concatenate([y[:n - SUB, :], tail], axis=0)


def _conv_taps(x, halo, w):
    if halo is None:
        f1, f2 = [0.0], [0.0, 0.0]
    else:
        f1, f2 = [halo[7:8, :]], [halo[6:7, :], halo[7:8, :]]
    s1 = _shift_down(x, 1, f1)
    s2 = _shift_down(x, 2, f2)
    u = w[2:3, :] * x + w[1:2, :] * s1 + w[0:1, :] * s2
    return u, s1, s2


def _conv_taps_t(du, nxt, w):
    if nxt is None:
        f1, f2 = [0.0], [0.0, 0.0]
    else:
        f1, f2 = [nxt[0:1, :]], [nxt[0:1, :], nxt[1:2, :]]
    return w[2:3, :] * du + w[1:2, :] * _shift_up(du, 1, f1) + w[0:1, :] * _shift_up(du, 2, f2)


def _ada_fwd(c_all, w_ada, b_my):
    def body(c_ref, w_ref, b_ref, o_ref):
        cv = c_ref[...]
        act = cv * _sigmoid(cv)
        o_ref[...] = _dot(act, w_ref[...], NN, lax.Precision.HIGHEST) + b_ref[...]

    return pl.pallas_call(
        body, name="ada_fwd",
        out_shape=jax.ShapeDtypeStruct((NDEV, w_ada.shape[1]), F32),
        compiler_params=_params(None, 32),
    )(c_all, w_ada, b_my)


TR = 256


def _row_spec(width, col=0):
    return pl.BlockSpec((TR, width), lambda i, col=col: (i, col))


def _full_spec(shape):
    return pl.BlockSpec(shape, lambda i: (0,) * len(shape))


def _norm_mod_fwd(x, mod, g):
    s = x.shape[0]

    def body(x_ref, mod_ref, g_ref, h_ref):
        xv = x_ref[...]
        r = lax.rsqrt(jnp.mean(xv * xv, axis=-1, keepdims=True) + EPS)
        nrm = xv * r * g_ref[...]
        h_ref[...] = (nrm * (1.0 + mod_ref[1:2, :]) + mod_ref[0:1, :]).astype(BF16)

    return pl.pallas_call(
        body, name="norm1_fwd", grid=(s // TR,),
        in_specs=[_row_spec(D), _full_spec((SUB, D)), _full_spec((1, D))],
        out_specs=_row_spec(D), out_shape=jax.ShapeDtypeStruct((s, D), BF16),
        compiler_params=_params(("parallel",)),
    )(x, mod, g)


SLAB = 2 * DH
AUG_F, AUG_ONE, AUG_LSE = 0, 3, 6


def _split3(x):
    hi = x.astype(BF16).astype(F32)
    r1 = x - hi
    mid = r1.astype(BF16).astype(F32)
    return hi, mid, r1 - mid


def _lanes3(lane, first, pieces, other):
    out = other
    for k in range(3):
        out = jnp.where(lane == first + k, pieces[k], out)
    return out


def _aug_placement():
    eq = np.zeros((3 * LANES, HEADS * SLAB), np.float32)
    ek = np.zeros((3 * LANES, HEADS * SLAB), np.float32)
    ones = np.zeros((SUB, HEADS * SLAB), np.float32)
    for h in range(HEADS):
        aug = SLAB * h + DH
        for k in range(3):
            eq[LANES * k + h, aug + AUG_F + k] = 1.0
            ek[LANES * k + h, aug + AUG_ONE + k] = -1.0
            ones[0, aug + AUG_ONE + k] = 1.0
            ones[1, aug + AUG_F + k] = ones[1, aug + AUG_LSE + k] = 1.0
            ones[2, aug + k] = 1.0
    return jnp.asarray(eq, BF16), jnp.asarray(ek, BF16), jnp.asarray(ones)


def _qkv_prep(proj, fcum, gq, gk, hosted):
    s = proj.shape[0]

    def body(q_ref, k_ref, v_ref, f_ref, gq_ref, gk_ref, eq_ref, ek_ref, ones_ref, qo_ref, ko_ref, vo_ref):
        f3 = jnp.concatenate(_split3(f_ref[...]), axis=1).astype(BF16)
        qo_ref[...] = (_dot(f3, eq_ref[...]) + ones_ref[0:1, :]).astype(BF16)
        ko_ref[...] = (_dot(f3, ek_ref[...]) + ones_ref[1:2, :]).astype(BF16)
        vo_ref[...] = jnp.broadcast_to(ones_ref[2:3, :], vo_ref.shape).astype(BF16)
        for h in range(HEADS):
            sl = slice(DH * h, DH * (h + 1))
            lo = slice(SLAB * h, SLAB * h + DH)
            qh = q_ref[:, sl]
            r = lax.rsqrt(jnp.mean(qh * qh, axis=-1, keepdims=True) + EPS)
            qo_ref[:, lo] = (qh * r * gq_ref[...] * QK_SCALE).astype(BF16)
            kh = k_ref[:, sl]
            r = lax.rsqrt(jnp.mean(kh * kh, axis=-1, keepdims=True) + EPS)
            ko_ref[:, lo] = (kh * r * gk_ref[...]).astype(BF16)
            vo_ref[:, lo] = v_ref[:, sl].astype(BF16)

    eq, ek, ones = _aug_placement()
    o = jax.ShapeDtypeStruct((s, HEADS * SLAB), BF16)
    wide = _row_spec(HEADS * SLAB)
    return _call(
        body, (proj, proj, proj, fcum, gq, gk, eq, ek, ones), name="qkv_prep", grid=(s // TR,),
        in_specs=[_row_spec(AW, 0), _row_spec(AW, 1), _row_spec(AW, 2), _row_spec(LANES),
                  _full_spec((1, DH)), _full_spec((1, DH)), _full_spec(eq.shape), _full_spec(ek.shape),
                  _full_spec(ones.shape)],
        out_specs=[wide, wide, wide], out_shape=[o, o, o], vmem_mb=16, hosted=hosted)


FG_BLOCK = (3 * AW + 3 * CW) // LANES


def _fgate_fwd(proj, bf_pad):
    s = proj.shape[0]

    def body(fg_ref, b_ref, o_ref, carry_ref):
        i = pl.program_id(0)

        @pl.when(i == 0)
        def _():
            carry_ref[...] = jnp.zeros_like(carry_ref)

        z = fg_ref[...] + b_ref[...]
        logf = jnp.minimum(z, 0.0) - jnp.log1p(jnp.exp(-jnp.abs(z)))
        row = lax.broadcasted_iota(jnp.int32, (TR, TR), 0)
        col = lax.broadcasted_iota(jnp.int32, (TR, TR), 1)
        tri = (col <= row).astype(F32)
        cs = _dot(tri, logf, NN, lax.Precision.HIGHEST) + carry_ref[0:1, :]
        o_ref[...] = cs
        carry_ref[...] = jnp.broadcast_to(cs[TR - 1:TR, :], carry_ref.shape)

    return pl.pallas_call(
        body, name="fgate_fwd", grid=(s // TR,),
        in_specs=[_row_spec(LANES, FG_BLOCK), _full_spec((1, LANES))],
        out_specs=_row_spec(LANES), out_shape=jax.ShapeDtypeStruct((s, LANES), F32),
        scratch_shapes=[pltpu.VMEM((SUB, LANES), F32)],
        compiler_params=_params(("arbitrary",)),
    )(proj, bf_pad)


def _fgate_bwd(dfcol, proj, bf_pad):
    s = proj.shape[0]
    nb = s // TR

    def body(df_ref, fg_ref, b_ref, o_ref, db_ref, carry_ref):
        i = pl.program_id(0)

        @pl.when(i == 0)
        def _():
            carry_ref[...] = jnp.zeros_like(carry_ref)
            db_ref[...] = jnp.zeros_like(db_ref)

        row = lax.broadcasted_iota(jnp.int32, (TR, TR), 0)
        col = lax.broadcasted_iota(jnp.int32, (TR, TR), 1)
        tri = (col >= row).astype(F32)
        dlogf = _dot(tri, df_ref[...], NN, lax.Precision.HIGHEST) + carry_ref[0:1, :]
        carry_ref[...] = jnp.broadcast_to(dlogf[0:1, :], carry_ref.shape)
        z = fg_ref[...] + b_ref[...]
        dfg = dlogf * _sigmoid(-z)
        o_ref[...] = dfg.astype(BF16)
        db_ref[0:1, :] += jnp.sum(dfg, axis=0, keepdims=True)

    rev = lambda col: pl.BlockSpec((TR, LANES), lambda i, col=col: (nb - 1 - i, col))
    return pl.pallas_call(
        body, name="fgate_bwd", grid=(nb,),
        in_specs=[rev(0), rev(FG_BLOCK), _full_spec((1, LANES))],
        out_specs=(rev(0), _full_spec((SUB, LANES))),
        out_shape=(jax.ShapeDtypeStruct((s, LANES), BF16), jax.ShapeDtypeStruct((SUB, LANES), F32)),
        scratch_shapes=[pltpu.VMEM((SUB, LANES), F32)],
        compiler_params=_params(("arbitrary",)),
    )(dfcol, proj, bf_pad)


def _resid_norm2(x, z, mod, g):
    s = x.shape[0]

    def body(x_ref, z_ref, mod_ref, g_ref, x1_ref, h_ref):
        x1 = x_ref[...] + mod_ref[2:3, :] * z_ref[...]
        x1_ref[...] = x1
        r = lax.rsqrt(jnp.mean(x1 * x1, axis=-1, keepdims=True) + EPS)
        nrm = x1 * r * g_ref[...]
        h_ref[...] = (nrm * (1.0 + mod_ref[4:5, :]) + mod_ref[3:4, :]).astype(BF16)

    return pl.pallas_call(
        body, name="resid_norm2", grid=(s // TR,),
        in_specs=[_row_spec(D), _row_spec(D), _full_spec((SUB, D)), _full_spec((1, D))],
        out_specs=(_row_spec(D), _row_spec(D)),
        out_shape=(jax.ShapeDtypeStruct((s, D), F32), jax.ShapeDtypeStruct((s, D), BF16)),
        compiler_params=_params(("parallel",)),
    )(x, z, mod, g)


def _loss_head(x1, y, tgt, mod):
    s = x1.shape[0]

    def body(x1_ref, y_ref, t_ref, mod_ref, dout_ref, dy_ref, vec_ref):
        @pl.when(pl.program_id(0) == 0)
        def _():
            vec_ref[...] = jnp.zeros_like(vec_ref)

        yv = y_ref[...]
        g2 = mod_ref[5:6, :]
        diff = x1_ref[...] + g2 * yv - t_ref[...]
        dout = diff * (1.0 / D)
        dout_ref[...] = dout
        dy_ref[...] = (g2 * dout).astype(BF16)
        vec_ref[0:1, :] += jnp.sum(dout * yv, axis=0, keepdims=True)
        vec_ref[1:2, :] += jnp.sum(diff * diff, axis=0, keepdims=True)

    return pl.pallas_call(
        body, name="loss_head", grid=(s // TR,),
        in_specs=[_row_spec(D), _row_spec(D), _row_spec(D), _full_spec((SUB, D))],
        out_specs=(_row_spec(D), _row_spec(D), _full_spec((SUB, D))),
        out_shape=(jax.ShapeDtypeStruct((s, D), F32), jax.ShapeDtypeStruct((s, D), BF16),
                   jax.ShapeDtypeStruct((SUB, D), F32)),
        compiler_params=_params(("arbitrary",)),
    )(x1, y, tgt, mod)


def _norm_mod_bwd(dh, xin, dres, zin, mod, g, shift_row, scale_row, gate_row, name, hosted=None):
    s = dh.shape[0]
    with_gate = gate_row is not None

    def body(*refs):
        if with_gate:
            dh_ref, x_ref, dres_ref, z_ref, mod_ref, g_ref, dx_ref, dz_ref, vec_ref = refs
        else:
            dh_ref, x_ref, dres_ref, mod_ref, g_ref, dx_ref, vec_ref = refs

        @pl.when(pl.program_id(0) == 0)
        def _():
            vec_ref[...] = jnp.zeros_like(vec_ref)

        xv = x_ref[...]
        dhv = dh_ref[...]
        gv = g_ref[...]
        r = lax.rsqrt(jnp.mean(xv * xv, axis=-1, keepdims=True) + EPS)
        xh = xv * r
        dn = dhv * (1.0 + mod_ref[scale_row:scale_row + 1, :])
        dxh = dn * gv
        dx = dres_ref[...] + r * (dxh - xh * jnp.mean(dxh * xh, axis=-1, keepdims=True))
        dx_ref[...] = dx
        vec_ref[0:1, :] += jnp.sum(dhv, axis=0, keepdims=True)
        vec_ref[1:2, :] += jnp.sum(dhv * (xh * gv), axis=0, keepdims=True)
        vec_ref[2:3, :] += jnp.sum(dn * xh, axis=0, keepdims=True)
        if with_gate:
            dz_ref[...] = (mod_ref[gate_row:gate_row + 1, :] * dx).astype(BF16)
            vec_ref[3:4, :] += jnp.sum(dx * z_ref[...], axis=0, keepdims=True)

    ins = [dh, xin, dres] + ([zin] if with_gate else []) + [mod, g]
    in_specs = [_row_spec(D)] * (4 if with_gate else 3) + [_full_spec((SUB, D)), _full_spec((1, D))]
    out_specs = [_row_spec(D)] + ([_row_spec(D)] if with_gate else []) + [_full_spec((SUB, D))]
    out_shape = [jax.ShapeDtypeStruct((s, D), F32)] + ([jax.ShapeDtypeStruct((s, D), BF16)] if with_gate else []) \
        + [jax.ShapeDtypeStruct((SUB, D), F32)]
    outs, moved = _call(body, ins, name=name, grid=(s // TR,), in_specs=in_specs, out_specs=out_specs,
                        out_shape=out_shape, hosted=hosted)
    return outs + (moved,) if hosted else outs


XIN_BLOCK = 3 * AW // LANES
BG_BLOCK = XIN_BLOCK + CW // LANES
CG_BLOCK = BG_BLOCK + CW // LANES


def _seq_spec(s, first_block):
    return pl.BlockSpec((s, LANES), lambda j, fb=first_block: (0, fb + j))


def _mixconv_fwd(proj, w):
    s = proj.shape[0]

    def body(xin_ref, bg_ref, cg_ref, w_ref, o_ref):
        cx = cg_ref[...] * xin_ref[...]
        cv, _, _ = _conv_taps(cx, None, w_ref[...])
        o_ref[...] = bg_ref[...] * cv

    return pl.pallas_call(
        body, name="mixconv_fwd", grid=(CW // LANES,),
        in_specs=[_seq_spec(s, XIN_BLOCK), _seq_spec(s, BG_BLOCK), _seq_spec(s, CG_BLOCK),
                  pl.BlockSpec((3, LANES), lambda j: (0, j))],
        out_specs=_seq_spec(s, 0), out_shape=jax.ShapeDtypeStruct((s, CW), F32),
        compiler_params=_params(("parallel",), 32),
    )(proj, proj, proj, w)


def _mixconv_bwd(dmixed, proj, w):
    s = proj.shape[0]

    def body(d_ref, xin_ref, bg_ref, cg_ref, w_ref, dxin_ref, dbg_ref, dcg_ref, dw_ref):
        wv = w_ref[...]
        xin, cg, dconv = xin_ref[...], cg_ref[...], d_ref[...]
        cx = cg * xin
        cv, s1, s2 = _conv_taps(cx, None, wv)
        dbg_ref[...] = (dconv * cv).astype(BF16)
        dcv = dconv * bg_ref[...]
        dw_ref[...] = jnp.zeros_like(dw_ref)
        dw_ref[0:1, :] = jnp.sum(dcv * s2, axis=0, keepdims=True)
        dw_ref[1:2, :] = jnp.sum(dcv * s1, axis=0, keepdims=True)
        dw_ref[2:3, :] = jnp.sum(dcv * cx, axis=0, keepdims=True)
        dcx = _conv_taps_t(dcv, None, wv)
        dcg_ref[...] = (dcx * xin).astype(BF16)
        dxin_ref[...] = (dcx * cg).astype(BF16)

    o = jax.ShapeDtypeStruct((s, CW), BF16)
    return pl.pallas_call(
        body, name="mixconv_bwd", grid=(CW // LANES,),
        in_specs=[_seq_spec(s, AW // LANES), _seq_spec(s, XIN_BLOCK), _seq_spec(s, BG_BLOCK), _seq_spec(s, CG_BLOCK),
                  pl.BlockSpec((3, LANES), lambda j: (0, j))],
        out_specs=(_seq_spec(s, 0), _seq_spec(s, 0), _seq_spec(s, 0), pl.BlockSpec((SUB, LANES), lambda j: (0, j))),
        out_shape=(o, o, o, jax.ShapeDtypeStruct((SUB, CW), F32)),
        compiler_params=_params(("parallel",), 32),
    )(dmixed, proj, proj, proj, w)


TA = 512
NEG = -1e30


def _causal_mask():
    row = lax.broadcasted_iota(jnp.int32, (TA, TA), 0)
    col = lax.broadcasted_iota(jnp.int32, (TA, TA), 1)
    return col <= row


def _attn_fwd(qp, kp, vp, hosted):
    s = qp.shape[0]
    nq = s // TA

    def body(q_ref, k_ref, v_ref, o_ref, lse_ref):
        i = pl.program_id(1)
        slabs = [slice(SLAB * hh, SLAB * (hh + 1)) for hh in range(2)]
        q = [q_ref[:, sl] for sl in slabs]

        def block(j, carry, masked):
            keys = pl.ds(pl.multiple_of(j * TA, TA), TA)
            ms, acc = carry
            m_out, parts = [], []
            for hh in range(2):
                sc = _dot(q[hh], k_ref[keys, slabs[hh]], NT)
                if masked:
                    sc = jnp.where(_causal_mask(), sc, NEG)
                m_new = jnp.maximum(ms[hh], jnp.max(sc, axis=-1, keepdims=True))
                p = jnp.exp(sc - m_new)
                parts.append(jnp.exp(ms[hh] - m_new) * acc[:, slabs[hh]] + _dot(p.astype(BF16), v_ref[keys, slabs[hh]]))
                m_out.append(m_new)
            return tuple(m_out), jnp.concatenate(parts, axis=1)

        init = ((jnp.full((TA, 1), NEG, F32), jnp.full((TA, 1), NEG, F32)), jnp.zeros((TA, 2 * SLAB), F32))
        carry = lax.fori_loop(0, i, lambda j, cr: block(j, cr, False), init)
        ms, acc = block(i, carry, True)
        for hh in range(2):
            l = acc[:, SLAB * hh + DH:SLAB * hh + DH + 1]
            o_ref[:, DH * hh:DH * (hh + 1)] = acc[:, SLAB * hh:SLAB * hh + DH] / l
            lse_ref[0, :, hh:hh + 1] = ms[hh] + jnp.log(l)

    (o, lse), moved = _call(
        body, (qp, kp, vp), name="attn_fwd", grid=(HEADS // 2, nq),
        in_specs=[pl.BlockSpec((TA, 2 * SLAB), lambda p, i: (i, p)),
                  pl.BlockSpec((s, 2 * SLAB), lambda p, i: (0, p)),
                  pl.BlockSpec((s, 2 * SLAB), lambda p, i: (0, p))],
        out_specs=[pl.BlockSpec((TA, LANES), lambda p, i: (i, p)), pl.BlockSpec((1, TA, 2), lambda p, i: (p, i, 0))],
        out_shape=[jax.ShapeDtypeStruct((s, AW), F32), jax.ShapeDtypeStruct((HEADS // 2, s, 2), F32)],
        vmem_mb=24, hosted=hosted)
    return o, lse, moved


def _attn_bwd(qp, kp, vp, dmixed, o, lse, hosted):
    s = qp.shape[0]
    nq = s // TA

    def body(q_ref, k_ref, v_ref, do_ref, o_ref, lse_ref, dq_ref, dk_ref, dv_ref, qb_ref, dob_ref):
        dk_ref[...] = jnp.zeros_like(dk_ref)
        dv_ref[...] = jnp.zeros_like(dv_ref)
        slabs = [slice(SLAB * hh, SLAB * (hh + 1)) for hh in range(2)]
        lane = lax.broadcasted_iota(jnp.int32, (TA, DH), 1)

        def q_block(i, _):
            i0 = pl.multiple_of(i * TA, TA)
            rows = pl.ds(i0, TA)
            for hh in range(2):
                half = slice(DH * hh, DH * (hh + 1))
                do = do_ref[rows, half]
                delta = jnp.sum(do * o_ref[rows, half], axis=-1, keepdims=True)
                dob_ref[hh, :, 0:DH] = do.astype(BF16)
                dob_ref[hh, :, DH:SLAB] = _lanes3(lane, 0, [-d for d in _split3(delta)], 0.0).astype(BF16)
                lse3 = _split3(lse_ref[0, rows, hh:hh + 1])
                qb_ref[hh, :, 0:DH] = q_ref[rows, SLAB * hh:SLAB * hh + DH]
                aug = q_ref[rows, SLAB * hh + DH:SLAB * (hh + 1)].astype(F32)
                qb_ref[hh, :, DH:SLAB] = _lanes3(lane, AUG_LSE, [-x for x in lse3], aug).astype(BF16)

            def block(j, dq, masked):
                keys = pl.ds(pl.multiple_of(j * TA, TA), TA)
                dv, dk, dqc = [], [], []
                for hh in range(2):
                    q, dob = qb_ref[hh], dob_ref[hh]
                    k = k_ref[keys, slabs[hh]]
                    sc = _dot(q, k, NT)
                    if masked:
                        sc = jnp.where(_causal_mask(), sc, NEG)
                    p = jnp.exp(sc)
                    dv.append(_dot(p.astype(BF16), dob, TN))
                    ds = (p * _dot(dob, v_ref[keys, slabs[hh]], NT)).astype(BF16)
                    dk.append(_dot(ds, q, TN))
                    dqc.append(_dot(ds, k))
                dv_ref[keys, :] += jnp.concatenate(dv, axis=1)
                dk_ref[keys, :] += jnp.concatenate(dk, axis=1)
                return dq + jnp.concatenate(dqc, axis=1)

            dq = lax.fori_loop(0, i, lambda j, acc: block(j, acc, False), jnp.zeros((TA, 2 * SLAB), F32))
            dq_ref[rows, :] = block(i, dq, True)
            return 0

        lax.fori_loop(0, nq, q_block, 0)

    pair = lambda p: (0, p)
    slab2 = pl.BlockSpec((s, 2 * SLAB), pair)
    seq = pl.BlockSpec((s, LANES), pair)
    small = pl.BlockSpec((1, s, 2), lambda p: (p, 0, 0))
    o32 = jax.ShapeDtypeStruct((s, HEADS * SLAB), F32)
    return _call(
        body, (qp, kp, vp, dmixed, o, lse), name="attn_bwd", grid=(HEADS // 2,),
        in_specs=[slab2, slab2, slab2, seq, seq, small], out_specs=[slab2, slab2, slab2], out_shape=[o32, o32, o32],
        scratch_shapes=[pltpu.VMEM((2, TA, SLAB), BF16), pltpu.VMEM((2, TA, SLAB), BF16)], vmem_mb=40, hosted=hosted)


def _qkv_post(dqp, dkp, dvp, proj, gq, gk, hosted):
    s = proj.shape[0]

    def body(dq_ref, dk_ref, dv_ref, q_ref, k_ref, gq_ref, gk_ref, dqo_ref, dko_ref, dvo_ref, df_ref, vec_ref):
        @pl.when(pl.program_id(0) == 0)
        def _():
            vec_ref[...] = jnp.zeros_like(vec_ref)

        def one(d_ref, x_ref, g_ref, o_ref, row, scale):
            dg = jnp.zeros((1, DH), F32)
            for h in range(HEADS):
                sl = slice(DH * h, DH * (h + 1))
                xv = x_ref[:, sl]
                r = lax.rsqrt(jnp.mean(xv * xv, axis=-1, keepdims=True) + EPS)
                xh = xv * r
                dn = d_ref[:, SLAB * h:SLAB * h + DH] * scale
                dg = dg + jnp.sum(dn * xh, axis=0, keepdims=True)
                dxh = dn * g_ref[...]
                o_ref[:, sl] = (r * (dxh - xh * jnp.mean(dxh * xh, axis=-1, keepdims=True))).astype(BF16)
            vec_ref[row:row + 1, 0:DH] += dg

        one(dq_ref, q_ref, gq_ref, dqo_ref, 0, QK_SCALE)
        one(dk_ref, k_ref, gk_ref, dko_ref, 1, 1.0)
        lane = lax.broadcasted_iota(jnp.int32, (TR, LANES), 1)
        df = jnp.zeros((TR, LANES), F32)
        for h in range(HEADS):
            dvo_ref[:, DH * h:DH * (h + 1)] = dv_ref[:, SLAB * h:SLAB * h + DH].astype(BF16)
            row_sum = dq_ref[:, SLAB * h + DH:SLAB * h + DH + 1]
            col_sum = dk_ref[:, SLAB * h + DH + AUG_ONE:SLAB * h + DH + AUG_ONE + 1]
            df = jnp.where(lane == h, row_sum - col_sum, df)
        df_ref[...] = df

    o = jax.ShapeDtypeStruct((s, AW), BF16)
    wide = _row_spec(HEADS * SLAB)
    return _call(
        body, (dqp, dkp, dvp, proj, proj, gq, gk), name="qkv_post", grid=(s // TR,),
        in_specs=[wide, wide, wide, _row_spec(AW, 0), _row_spec(AW, 1), _full_spec((1, DH)), _full_spec((1, DH))],
        out_specs=[_row_spec(AW), _row_spec(AW), _row_spec(AW), _row_spec(LANES), _full_spec((SUB, LANES))],
        out_shape=[o, o, o, jax.ShapeDtypeStruct((s, LANES), F32), jax.ShapeDtypeStruct((SUB, LANES), F32)],
        hosted=hosted)


TF = 256
NJ = DFF // TF
FFN_ROWS_FWD = 1024
FFN_ROWS_BWD = 1024


def _ffn_fwd(h2, wup_t, cw, wd):
    s = h2.shape[0]
    tr = FFN_ROWS_FWD
    nr = s // tr

    def body(h_ref, wu_ref, cg_ref, cv_ref, wd_ref, pg_ref, pv_ref, y_ref, halo_ref, act_ref):
        r, j = pl.program_id(0), pl.program_id(1)
        hv = h_ref[...]
        pg = _dot(hv, wu_ref[0], NT).astype(BF16)
        pv = _dot(hv, wu_ref[1], NT).astype(BF16)
        pg_ref[...] = pg
        pv_ref[...] = pv
        pgf, pvf = pg.astype(F32), pv.astype(F32)
        ug, _, _ = _conv_taps(pgf, jnp.where(r > 0, halo_ref[j, 0], 0.0), cg_ref[...])
        uv, _, _ = _conv_taps(pvf, jnp.where(r > 0, halo_ref[j, 1], 0.0), cv_ref[...])
        halo_ref[j, 0] = pgf[tr - SUB:tr, :]
        halo_ref[j, 1] = pvf[tr - SUB:tr, :]
        act = (ug * _sigmoid(ug) * uv).astype(BF16)
        for t in range(NJ):
            @pl.when(j == t)
            def _(t=t):
                act_ref[:, t * TF:(t + 1) * TF] = act

        @pl.when(j == NJ - 1)
        def _():
            y_ref[...] = _dot(act_ref[...], wd_ref[...])

    pre = jax.ShapeDtypeStruct((s, DFF), BF16)
    return pl.pallas_call(
        body, name="ffn_fwd", grid=(nr, NJ),
        in_specs=[pl.BlockSpec((tr, D), lambda r, j: (r, 0)),
                  pl.BlockSpec((2, TF, D), lambda r, j: (0, j, 0)),
                  pl.BlockSpec((3, TF), lambda r, j: (0, j)),
                  pl.BlockSpec((3, TF), lambda r, j: (0, NJ + j)),
                  pl.BlockSpec((DFF, D), lambda r, j: (0, 0))],
        out_specs=(pl.BlockSpec((tr, TF), lambda r, j: (r, j)),
                   pl.BlockSpec((tr, TF), lambda r, j: (r, j)),
                   pl.BlockSpec((tr, D), lambda r, j: (r, 0))),
        out_shape=(pre, pre, jax.ShapeDtypeStruct((s, D), F32)),
        scratch_shapes=[pltpu.VMEM((NJ, 2, SUB, TF), F32), pltpu.VMEM((tr, DFF), BF16)],
        compiler_params=_params(("arbitrary", "arbitrary"), 56),
    )(h2, wup_t, cw, cw, wd)


def _ffn_bwd(dy, h2, pre_g, pre_v, wup_t, cw, wd):
    s = h2.shape[0]
    tr = FFN_ROWS_BWD
    nr = s // tr
    hb = tr // (2 * SUB)

    def body(dy_ref, h_ref, pg_ref, pv_ref, hg_ref, hv_ref, wu_ref, cg_ref, cv_ref, wd_ref,
             dh_ref, dwu_ref, dwd_ref, dcg_ref, dcv_ref, nxt_ref, awu_ref, awd_ref):
        j, r = pl.program_id(0), pl.program_id(1)
        rr = nr - 1 - r
        row0 = pl.multiple_of(rr * tr, tr)
        cwg, cwv = cg_ref[...], cv_ref[...]
        pg, pv = pg_ref[...].astype(F32), pv_ref[...].astype(F32)
        ug, g1, g2 = _conv_taps(pg, jnp.where(rr > 0, hg_ref[SUB:2 * SUB, :].astype(F32), 0.0), cwg)
        uv, v1, v2 = _conv_taps(pv, jnp.where(rr > 0, hv_ref[SUB:2 * SUB, :].astype(F32), 0.0), cwv)
        sg = _sigmoid(ug)
        sil = ug * sg
        act = (sil * uv).astype(BF16)
        dyv = dy_ref[...]
        da = _dot(dyv, wd_ref[...], NT)
        dug = da * uv * (sg * (1.0 + ug * (1.0 - sg)))
        duv = da * sil
        dpg = _conv_taps_t(dug, jnp.where(r > 0, nxt_ref[0], 0.0), cwg)
        dpv = _conv_taps_t(duv, jnp.where(r > 0, nxt_ref[1], 0.0), cwv)
        nxt_ref[0] = dug[0:SUB, :]
        nxt_ref[1] = duv[0:SUB, :]
        dpgb, dpvb = dpg.astype(BF16), dpv.astype(BF16)
        hv = h_ref[...]
        dwd = _dot(act, dyv, TN)
        dpb = jnp.concatenate([dpgb, dpvb], axis=1)
        dwu = _dot(dpb, hv, TN)
        dh = _dot(dpb, wu_ref[...].reshape(2 * TF, D))

        def taps(du, x0, x1, x2):
            return (jnp.sum(du * x2, axis=0, keepdims=True), jnp.sum(du * x1, axis=0, keepdims=True),
                    jnp.sum(du * x0, axis=0, keepdims=True))

        tg, tv = taps(dug, pg, g1, g2), taps(duv, pv, v1, v2)

        @pl.when(r == 0)
        def _():
            awd_ref[...] = dwd
            awu_ref[...] = dwu
            dcg_ref[...] = jnp.zeros_like(dcg_ref)
            dcv_ref[...] = jnp.zeros_like(dcv_ref)

        @pl.when(r > 0)
        def _():
            awd_ref[...] += dwd
            awu_ref[...] += dwu

        @pl.when(r == nr - 1)
        def _():
            dwd_ref[...] = awd_ref[...].astype(BF16)
            dwu_ref[...] = awu_ref[...].astype(BF16).reshape(2, TF, D)

        for t in range(3):
            dcg_ref[t:t + 1, :] += tg[t]
            dcv_ref[t:t + 1, :] += tv[t]

        @pl.when(j == 0)
        def _():
            dh_ref[pl.ds(row0, tr), :] = dh

        @pl.when(j > 0)
        def _():
            dh_ref[pl.ds(row0, tr), :] += dh

    rows = lambda j, r: (nr - 1 - r, 0)
    tile = lambda j, r: (nr - 1 - r, j)
    halo = lambda j, r: (jnp.maximum((nr - 1 - r) * hb - 1, 0), j)
    return pl.pallas_call(
        body, name="ffn_bwd", grid=(NJ, nr),
        in_specs=[pl.BlockSpec((tr, D), rows), pl.BlockSpec((tr, D), rows),
                  pl.BlockSpec((tr, TF), tile), pl.BlockSpec((tr, TF), tile),
                  pl.BlockSpec((2 * SUB, TF), halo), pl.BlockSpec((2 * SUB, TF), halo),
                  pl.BlockSpec((2, TF, D), lambda j, r: (0, j, 0)),
                  pl.BlockSpec((3, TF), lambda j, r: (0, j)), pl.BlockSpec((3, TF), lambda j, r: (0, NJ + j)),
                  pl.BlockSpec((TF, D), lambda j, r: (j, 0))],
        out_specs=(pl.BlockSpec((s, D), lambda j, r: (0, 0)),
                   pl.BlockSpec((2, TF, D), lambda j, r: (0, j, 0)),
                   pl.BlockSpec((TF, D), lambda j, r: (j, 0)),
                   pl.BlockSpec((SUB, TF), lambda j, r: (0, j)), pl.BlockSpec((SUB, TF), lambda j, r: (0, j))),
        out_shape=(jax.ShapeDtypeStruct((s, D), F32),
                   jax.ShapeDtypeStruct((2, DFF, D), BF16), jax.ShapeDtypeStruct((DFF, D), BF16),
                   jax.ShapeDtypeStruct((SUB, DFF), F32), jax.ShapeDtypeStruct((SUB, DFF), F32)),
        scratch_shapes=[pltpu.VMEM((2, SUB, TF), F32), pltpu.VMEM((2 * TF, D), F32), pltpu.VMEM((TF, D), F32)],
        compiler_params=_params(("arbitrary", "arbitrary"), 56),
    )(dy, h2, pre_g, pre_v, pre_g, pre_v, wup_t, cw, cw, wd)


def _adam(w, g, m, v):
    m = ADAM_B1 * m + (1.0 - ADAM_B1) * g
    v = ADAM_B2 * v + (1.0 - ADAM_B2) * (g * g)
    m_hat = m / (1.0 - ADAM_B1 ** ADAM_STEP)
    v_hat = v / (1.0 - ADAM_B2 ** ADAM_STEP)
    delta = -ADAM_LR * (m_hat / (jnp.sqrt(v_hat) + ADAM_EPS) + ADAM_WD * w)
    return delta, m, v


NCHIP = NDEV // 2


def _pair_add(mine, theirs, tr, name):
    _, _, rws, cols = mine.shape

    def body(a_ref, b_ref, o_ref):
        c = lax.axis_index("c")
        o_ref[0] = (a_ref[0, c].astype(F32) + b_ref[0].astype(F32)).astype(BF16)

    (out,), _ = _call(
        body, (mine, theirs), name=name, grid=(NCHIP, rws // tr),
        in_specs=[pl.BlockSpec((1, 2, tr, cols), lambda q, i: (q, 0, i, 0)),
                  pl.BlockSpec((1, tr, cols), lambda q, i: (q, i, 0))],
        out_specs=[pl.BlockSpec((1, tr, cols), lambda q, i: (q, i, 0))],
        out_shape=[jax.ShapeDtypeStruct((NCHIP, rws, cols), BF16)], vmem_mb=16)
    return out


def _adamw_sharded(parts, w, m, v, tr, name, hosted=None):
    rws, cols = w.shape

    def body(p_ref, w_ref, m_ref, v_ref, g_ref, d_ref, mo_ref, vo_ref):
        g = p_ref[0].astype(F32)
        for q in range(1, NCHIP):
            g = g + p_ref[q].astype(F32)
        g_ref[...] = g
        d_ref[...], mo_ref[...], vo_ref[...] = _adam(w_ref[...], g, m_ref[...], v_ref[...])

    blk = pl.BlockSpec((tr, cols), lambda i: (i, 0))
    o = jax.ShapeDtypeStruct((rws, cols), F32)
    outs, moved = _call(
        body, (parts, w, m, v), name=name, grid=(rws // tr,),
        in_specs=[pl.BlockSpec((NCHIP, tr, cols), lambda i: (0, i, 0)), blk, blk, blk],
        out_specs=[blk, blk, blk, blk], out_shape=[o, o, o, o], vmem_mb=36 if tr > 256 else 24, hosted=hosted)
    return (outs, moved) if hosted else outs


def _adamw_ada(c_all, dmod_my, w, m, v):
    rws, cols = w.shape
    tr = 256

    def body(c_ref, dm_ref, w_ref, m_ref, v_ref, g_ref, d_ref, mo_ref, vo_ref):
        cv = c_ref[...]
        act = cv * _sigmoid(cv)
        g = _dot(act, dm_ref[...], TN, lax.Precision.HIGHEST)
        g_ref[...] = g
        d_ref[...], mo_ref[...], vo_ref[...] = _adam(w_ref[...], g, m_ref[...], v_ref[...])

    blk = pl.BlockSpec((tr, cols), lambda i: (i, 0))
    o = jax.ShapeDtypeStruct((rws, cols), F32)
    return pl.pallas_call(
        body, name="adamw_ada", grid=(rws // tr,),
        in_specs=[pl.BlockSpec((NDEV, tr), lambda i: (0, i)), _full_spec((NDEV, cols)), blk, blk, blk],
        out_specs=(blk, blk, blk, blk), out_shape=(o, o, o, o),
        compiler_params=_params(("parallel",), 32),
    )(c_all, dmod_my, w, m, v)


REP_ROWS = 16
ROW_N1, ROW_N2, ROW_LOSS, ROW_MISC = 6, 7, 8, 9
LANE_BF, LANE_GQ, LANE_GK = 0, 128, 256


def _adamw_small(rep_all, conv_all, wmv):
    n_ff = wmv[6][0].shape[1]

    def body(*refs):
        rep_ref, conv_ref = refs[:2]
        ins = refs[2:2 + 24]
        outs = refs[2 + 24:]
        loss_ref, outs = outs[0], outs[1:]
        g_rep = rep_ref[0]
        g_conv = conv_ref[0]
        for d in range(1, NDEV):
            g_rep = g_rep + rep_ref[d]
            g_conv = g_conv + conv_ref[d]
        loss_ref[...] = (0.5 / D) * jnp.sum(g_rep[ROW_LOSS:ROW_LOSS + 1, :], axis=-1, keepdims=True)
        grads = [
            None,
            g_rep[ROW_N1:ROW_N1 + 1, :],
            g_rep[ROW_MISC:ROW_MISC + 1, LANE_BF:LANE_BF + HEADS],
            g_rep[ROW_MISC:ROW_MISC + 1, LANE_GQ:LANE_GQ + DH],
            g_rep[ROW_MISC:ROW_MISC + 1, LANE_GK:LANE_GK + DH],
            g_rep[ROW_N2:ROW_N2 + 1, :],
            g_conv[0:3, 0:n_ff],
            g_conv[0:3, n_ff:n_ff + DH],
        ]
        for p in range(8):
            w_ref, m_ref, v_ref = ins[3 * p:3 * p + 3]
            g_ref, d_ref, mo_ref, vo_ref = outs[4 * p:4 * p + 4]
            if p == 0:
                for nmod in range(NMOD):
                    sl = slice(D * nmod, D * (nmod + 1))
                    g = g_rep[nmod:nmod + 1, :]
                    g_ref[:, sl] = g
                    d_ref[:, sl], mo_ref[:, sl], vo_ref[:, sl] = _adam(w_ref[:, sl], g, m_ref[:, sl], v_ref[:, sl])
            else:
                g = grads[p]
                g_ref[...] = g
                d_ref[...], mo_ref[...], vo_ref[...] = _adam(w_ref[...], g, m_ref[...], v_ref[...])

    flat = [a for trio in wmv for a in trio]
    out_shape = [jax.ShapeDtypeStruct((1, 1), F32)]
    for trio in wmv:
        out_shape += [jax.ShapeDtypeStruct(trio[0].shape, F32)] * 4
    return pl.pallas_call(
        body, name="adamw_small", out_shape=tuple(out_shape),
        compiler_params=_params(None, 32),
    )(rep_all, conv_all, *flat)


FG_FIRST = 3 * AW
N_IN = DIN // NDEV


def _w_in_runs():
    runs = []
    for d in range(NDEV):
        lo, hi = N_IN * d, N_IN * (d + 1)
        for a, b, shift in ((0, FG_FIRST, 0), (FG_FIRST, FG_FIRST + HEADS, DIN - HEADS - FG_FIRST),
                            (FG_FIRST + HEADS, DIN, -HEADS)):
            a, b = max(a, lo), min(b, hi)
            if a < b:
                runs.append((d, a - lo, a + shift, b - a))
    return runs


W_IN_ROWS = 256
N_IN_PAD = 512


def _identity(n):
    return (lax.broadcasted_iota(jnp.int32, (n, n), 0) == lax.broadcasted_iota(jnp.int32, (n, n), 1)).astype(BF16)


def _assemble_w_in(g_in, hosted):
    def body(g_ref, o_ref, t_ref):
        eye = _identity(W_IN_ROWS)
        shard = None
        for d, src, dst, width in _w_in_runs():
            if d != shard:
                t_ref[:, 0:N_IN] = _dot(eye, g_ref[d], NT).astype(BF16)
                shard = d
            o_ref[:, dst:dst + width] = t_ref[:, src:src + width]
        o_ref[:, DIN:DINP] = jnp.zeros((W_IN_ROWS, DINP - DIN), o_ref.dtype)

    (out,), moved = _call(
        body, (g_in,), name="assemble_w_in", grid=(D // W_IN_ROWS,),
        in_specs=[pl.BlockSpec((NDEV, N_IN, W_IN_ROWS), lambda i: (0, 0, i))],
        out_specs=[pl.BlockSpec((W_IN_ROWS, DINP), lambda i: (i, 0))],
        out_shape=[jax.ShapeDtypeStruct((D, DINP), g_in.dtype)],
        scratch_shapes=[pltpu.VMEM((W_IN_ROWS, N_IN_PAD), BF16)], vmem_mb=16, hosted=hosted)
    return out, moved


def _scatter_dw_in(dwp):
    def body(w_ref, o_ref, t_ref):
        eye = _identity(W_IN_ROWS)
        runs = _w_in_runs()
        for i, (d, src, dst, width) in enumerate(runs):
            t_ref[:, src:src + width] = w_ref[:, dst:dst + width]
            if i + 1 == len(runs) or runs[i + 1][0] != d:
                o_ref[d // 2, d % 2] = _dot(t_ref[:, 0:N_IN], eye, TN).astype(BF16)

    (out,), _ = _call(
        body, (dwp,), name="scatter_dw_in", grid=(D // W_IN_ROWS,),
        in_specs=[pl.BlockSpec((W_IN_ROWS, DINP), lambda i: (i, 0))],
        out_specs=[pl.BlockSpec((NCHIP, 2, N_IN, W_IN_ROWS), lambda i: (0, 0, 0, i))],
        out_shape=[jax.ShapeDtypeStruct((NCHIP, 2, N_IN, D), dwp.dtype)],
        scratch_shapes=[pltpu.VMEM((W_IN_ROWS, N_IN_PAD), BF16)], vmem_mb=16)
    return out


def kernel(x, c, w_ada, b_ada, norm1_g, w_in, b_forget, q_norm_g, k_norm_g, conv_mix_w, w_out, norm2_g, w_up, ffn_conv_w, w_down, loss_target, m_w_ada, m_b_ada, m_norm1_g, m_w_in, m_b_forget, m_q_norm_g, m_k_norm_g, m_conv_mix_w, m_w_out, m_norm2_g, m_w_up, m_ffn_conv_w, m_w_down, v_w_ada, v_b_ada, v_norm1_g, v_w_in, v_b_forget, v_q_norm_g, v_k_norm_g, v_conv_mix_w, v_w_out, v_norm2_g, v_w_up, v_ffn_conv_w, v_w_down):
    me = 4 * lax.axis_index("x") + 2 * lax.axis_index("y") + lax.axis_index("c")
    xs, tgt = x[0], loss_target[0]
    s = xs.shape[0]
    nq = s // TA
    n_ada = w_ada.shape[2]
    n_ff = w_up.shape[2]

    conv_w = jnp.concatenate([ffn_conv_w[0], conv_mix_w[0]], axis=1)
    conv_w = jnp.concatenate([conv_w, jnp.zeros((SUB - 3, conv_w.shape[1]), F32)], axis=0)
    c_all, conv_all, g_in = _exchange(
        [(c.reshape(SUB, D // SUB), "ag"), (conv_w, "ag"), (jnp.transpose(w_in[0]).astype(BF16), "ag2")],
        "exchange_w_in")
    g_in, w_out_b, w_up_b, w_down_b = lax.optimization_barrier(
        (g_in, w_out[0].astype(BF16), jnp.transpose(w_up[0]).astype(BF16), w_down[0].astype(BF16)))
    g_out, g_up, g_down = _sequencer_exchange(
        [(w_out_b, "ag2"), (w_up_b, "ag2"), (w_down_b, "ag2")], "gather_weights", collective_id=1)
    c_all = c_all.reshape(NDEV, D)
    cw_ffn = jnp.transpose(conv_all[:, :3, :n_ff], (1, 0, 2)).reshape(3, 2 * DFF)
    cw_mix = jnp.transpose(conv_all[:, :3, n_ff:], (1, 0, 2)).reshape(3, CW)

    b_my = lax.dynamic_slice(b_ada, (0, me * n_ada), (1, n_ada))
    mod_part = _ada_fwd(c_all, w_ada[0], b_my)
    w_in_p, (mod_rows,) = _assemble_w_in(
        g_in, [(jnp.broadcast_to(mod_part[:, None, :], (NDEV, SUB, n_ada)), "a2a")])
    mod = mod_rows[:, 0, :].reshape(NMOD, D)
    mod = jnp.concatenate([mod, jnp.zeros((SUB - NMOD, D), F32)], axis=0)

    h = _norm_mod_fwd(xs, mod, norm1_g)
    proj = _mm(h, w_in_p, "nn", F32, 1024, 640, "proj_fwd")
    bf_pad = jnp.concatenate([b_forget, jnp.zeros((1, LANES - HEADS), F32)], axis=1)
    fcum = _fgate_fwd(proj, bf_pad)
    (qp, kp, vp), _ = _qkv_prep(proj, fcum, q_norm_g, k_norm_g, None)
    attn, lse, _ = _attn_fwd(qp, kp, vp, None)
    w_out_f = g_out.reshape(D, D)
    w_up_t = g_up.reshape(2, DFF, D)
    w_down_f = g_down.reshape(DFF, D)
    conv = _mixconv_fwd(proj, cw_mix)
    mixed = jnp.concatenate([attn, conv], axis=1).astype(BF16)
    z = _mm(mixed, w_out_f, "nn", F32, 1024, 1024, "out_fwd")
    x1, h2 = _resid_norm2(xs, z, mod, norm2_g)
    pre_g, pre_v, y = _ffn_fwd(h2, w_up_t, cw_ffn, w_down_f)
    dout, dy, vec_l = _loss_head(x1, y, tgt, mod)

    dh2, dwup_t, dwd, dcw_g, dcw_v = _ffn_bwd(dy, h2, pre_g, pre_v, w_up_t, cw_ffn, w_down_f)
    dx1, dz, vec_2 = _norm_mod_bwd(dh2, x1, dout, z, mod, norm2_g, 3, 4, 2, "norm2_bwd")
    dwout = _mm(mixed, dz, "tn", BF16, 1024, 1024, "out_bwd_w")
    s_out = dwout.reshape(NCHIP, 2, D // NDEV, D)
    s_down = dwd.reshape(NCHIP, 2, DFF // NDEV, D)
    s_up = dwup_t.reshape(NCHIP, 2, n_ff, D)
    dmixed, (t_out, t_up, t_down) = _mm(dz, w_out_f, "nt", F32, 1024, 1024, "out_bwd_x",
                                        hosted=[(s_out, "pair"), (s_up, "pair"), (s_down, "pair")])
    c_out = _pair_add(s_out, t_out, 128, "pair_add_out")
    c_up = _pair_add(s_up, t_up, 176, "pair_add_up")
    c_down = _pair_add(s_down, t_down, 176, "pair_add_down")
    dxin, dbg, dcg, dcw_mix = _mixconv_bwd(dmixed, proj, cw_mix)
    (dqp, dkp, dvp), (p_up,) = _attn_bwd(qp, kp, vp, dmixed, attn, lse, [(c_up, "chips")])
    (dq, dk, dvb, dfcol, vec_qk), (p_out, p_down) = _qkv_post(
        dqp, dkp, dvp, proj, q_norm_g, k_norm_g, [(c_out, "chips"), (c_down, "chips")])
    dfg, vec_bf = _fgate_bwd(dfcol, proj, bf_pad)
    dproj = jnp.concatenate([dq, dk, dvb, dxin, dbg, dcg, dfg], axis=1)
    dwin_p = _mm(h, dproj, "tn", BF16, 1024, 640, "proj_bwd_w")
    s_in = _scatter_dw_in(dwin_p)
    (t_in,) = _exchange([(s_in, "pair")], "exchange_pair_in")
    c_in = _pair_add(s_in, t_in, N_IN, "pair_add_in")
    dh, (p_in,) = _mm(dproj, w_in_p, "nt", F32, 1024, 512, "proj_bwd_x", hosted=[(c_in, "chips")], vmem_mb=36)
    grad_x, vec_1 = _norm_mod_bwd(dh, xs, dx1, None, mod, norm1_g, 0, 1, None, "norm1_bwd")

    misc = jnp.zeros((1, D), F32)
    misc = lax.dynamic_update_slice(misc, vec_bf[0:1, :HEADS], (0, LANE_BF))
    misc = lax.dynamic_update_slice(misc, vec_qk[0:1, :DH], (0, LANE_GQ))
    misc = lax.dynamic_update_slice(misc, vec_qk[1:2, :DH], (0, LANE_GK))
    rep = jnp.concatenate([
        vec_1[0:1], vec_1[1:2], vec_2[3:4], vec_2[0:1], vec_2[1:2], vec_l[0:1],
        vec_1[2:3], vec_2[2:3], vec_l[1:2], misc, jnp.zeros((REP_ROWS - 10, D), F32)], axis=0)
    dcw_ffn = jnp.concatenate([dcw_g, dcw_v], axis=1).reshape(SUB, NDEV, n_ff)
    dcw_all = jnp.concatenate([jnp.transpose(dcw_ffn, (1, 0, 2)),
                               jnp.transpose(dcw_mix.reshape(SUB, NDEV, DH), (1, 0, 2))], axis=2)
    r_out, (rep_all, conv_parts) = _adamw_sharded(p_out, w_out[0], m_w_out[0], v_w_out[0], 128, "adamw_out",
                                                  hosted=[(rep, "ag"), (dcw_all, "a2a")])
    dmod_my = lax.dynamic_slice(rep_all[:, :NMOD, :].reshape(NDEV, NMOD * D), (0, me * n_ada), (NDEV, n_ada))
    r_ada = _adamw_ada(c_all, dmod_my, w_ada[0], m_w_ada[0], v_w_ada[0])
    r_in = _adamw_sharded(p_in, jnp.transpose(w_in[0]), jnp.transpose(m_w_in[0]), jnp.transpose(v_w_in[0]), N_IN,
                          "adamw_in")
    r_in = tuple(jnp.transpose(a) for a in r_in)
    r_up = _adamw_sharded(p_up, jnp.transpose(w_up[0]), jnp.transpose(m_w_up[0]), jnp.transpose(v_w_up[0]), 176,
                          "adamw_up")
    r_up = tuple(jnp.transpose(a) for a in r_up)
    r_down = _adamw_sharded(p_down, w_down[0], m_w_down[0], v_w_down[0], 176, "adamw_down")
    small = _adamw_small(rep_all, conv_parts, [
        [b_ada, m_b_ada, v_b_ada], [norm1_g, m_norm1_g, v_norm1_g], [b_forget, m_b_forget, v_b_forget],
        [q_norm_g, m_q_norm_g, v_q_norm_g], [k_norm_g, m_k_norm_g, v_k_norm_g], [norm2_g, m_norm2_g, v_norm2_g],
        [ffn_conv_w[0], m_ffn_conv_w[0], v_ffn_conv_w[0]], [conv_mix_w[0], m_conv_mix_w[0], v_conv_mix_w[0]]])
    loss = small[0].reshape(())
    r_bada, r_n1, r_bf, r_gq, r_gk, r_n2, r_cf, r_cm = [small[1 + 4 * p:5 + 4 * p] for p in range(8)]
    lead = lambda t: tuple(a[None] for a in t)
    per_w = [lead(r_ada), r_bada, r_n1, lead(r_in), r_bf, r_gq, r_gk, lead(r_cm), lead(r_out), r_n2,
             lead(r_up), lead(r_cf), lead(r_down)]
    outs = [loss, grad_x[None]]
    for field in range(4):
        outs += [t[field] for t in per_w]
    return tuple(outs)
```

```python
import functools

import jax
import jax.numpy as jnp
import numpy as np
from jax import lax
from jax.experimental import pallas as pl
from jax.experimental.pallas import tpu as pltpu
from jax.experimental.pallas import tpu_sc as plsc

F32 = jnp.float32
BF16 = jnp.bfloat16

NDEV = 8
D = 1024
HEADS = 8
DH = 64
AW = 512
CW = 512
DFF = 2816
DIN = 3080
DINP = 3200
NMOD = 6
EPS = 1e-6
QK_SCALE = 0.125
LANES = 128
SUB = 8

ADAM_LR = 0.001
ADAM_B1 = 0.9
ADAM_B2 = 0.999
ADAM_EPS = 1e-08
ADAM_WD = 0.01
ADAM_STEP = 10

MESH = pl.DeviceIdType.MESH
ANY = pl.BlockSpec(memory_space=pl.ANY)

NN = (((1,), (0,)), ((), ()))
NT = (((1,), (1,)), ((), ()))
TN = (((0,), (0,)), ((), ()))


def _dot(a, b, dims=NN, precision=None):
    return lax.dot_general(a, b, dims, precision=precision, preferred_element_type=F32)


def _params(sem=None, vmem_mb=None):
    kw = {}
    if sem is not None:
        kw["dimension_semantics"] = sem
    if vmem_mb is not None:
        kw["vmem_limit_bytes"] = vmem_mb * 1024 * 1024
    return pltpu.CompilerParams(**kw)


def _sigmoid(x):
    return 0.5 * jnp.tanh(0.5 * x) + 0.5


class _Exchange:
    def __init__(self, items):
        self.arrays = [a for a, _ in items]
        self.modes = [m for _, m in items]
        self.n = len(items)
        self.out_shape = []
        for a, m in items:
            sh = {"ag": (NDEV,) + a.shape, "ag2": (NDEV,) + a.shape, "pair": a.shape[:1] + a.shape[2:]}.get(m, a.shape)
            self.out_shape.append(jax.ShapeDtypeStruct(sh, a.dtype))
        self.scratch = [pltpu.SemaphoreType.DMA((self.n, NDEV - 1)), pltpu.SemaphoreType.DMA((self.n, NDEV - 1)),
                        pltpu.SemaphoreType.DMA((self.n,))]

    def _plan(self, srcs, outs, sems):
        send_sems, recv_sems, loc_sems = sems
        x, y, c = lax.axis_index("x"), lax.axis_index("y"), lax.axis_index("c")
        me, my_chip = 4 * x + 2 * y + c, 2 * x + y
        sib = (x, y, 1 - c)
        local, first, landed, forwards, arrivals = [], [], [], [], []

        def remote(a, k, src, dst, to):
            return pltpu.make_async_remote_copy(src_ref=src, dst_ref=dst, send_sem=send_sems.at[a, k],
                                                recv_sem=recv_sems.at[a, k], device_id=to, device_id_type=MESH)

        for a, mode in enumerate(self.modes):
            src, out = srcs[a], outs[a]
            if mode in ("ag", "a2a"):
                piece = (lambda slot, src=src: src) if mode == "ag" else (lambda slot, src=src: src.at[slot])
                local.append(pltpu.make_async_copy(piece(me), out.at[me], loc_sems.at[a]))
                for r in range(1, NDEV):
                    px = 1 - x if (r >> 2) & 1 else x
                    py = 1 - y if (r >> 1) & 1 else y
                    pc = 1 - c if r & 1 else c
                    pidx = 4 * px + 2 * py + pc
                    first.append(remote(a, r - 1, piece(pidx), out.at[me], (px, py, pc)))
                    arrivals.append(remote(a, r - 1, piece(pidx), out.at[pidx], (px, py, pc)))
            elif mode == "ag2":
                local.append(pltpu.make_async_copy(src, out.at[me], loc_sems.at[a]))
                first.append(remote(a, 0, src, out.at[me], sib))
                arrivals.append(remote(a, 0, src, out.at[me + 1 - 2 * c], sib))
                for j, (px, py) in enumerate([(1 - x, y), (x, 1 - y), (1 - x, 1 - y)]):
                    theirs = out.at[4 * px + 2 * py + c]
                    first.append(remote(a, 1 + j, src, out.at[me], (px, py, c)))
                    landed.append(remote(a, 1 + j, src, theirs, (px, py, c)))
                    forwards.append(remote(a, 4 + j, theirs, theirs, sib))
                    arrivals.append(remote(a, 4 + j, src, out.at[4 * px + 2 * py + 1 - c], sib))
            elif mode == "pair":
                for q in range(NDEV // 2):
                    first.append(remote(a, q, src.at[q, 1 - c], out.at[q], sib))
                    arrivals.append(remote(a, q, src.at[q, 1 - c], out.at[q], sib))
            else:
                assert mode == "chips", mode
                local.append(pltpu.make_async_copy(src.at[my_chip], out.at[my_chip], loc_sems.at[a]))
                for j, (px, py) in enumerate([(1 - x, y), (x, 1 - y), (1 - x, 1 - y)]):
                    q = 2 * px + py
                    first.append(remote(a, 1 + j, src.at[q], out.at[my_chip], (px, py, c)))
                    arrivals.append(remote(a, 1 + j, src.at[q], out.at[q], (px, py, c)))
        return local, first, landed, forwards, arrivals

    def start(self, srcs, outs, sems):
        local, first, _, _, _ = self._plan(srcs, outs, sems)
        for cp in local + first:
            cp.start()

    def wait(self, srcs, outs, sems):
        local, first, landed, forwards, arrivals = self._plan(srcs, outs, sems)
        for cp, fwd in zip(landed, forwards):
            cp.wait_recv()
            fwd.start()
        for cp in arrivals:
            cp.wait_recv()
        for cp in first + forwards:
            cp.wait_send()
        for cp in local:
            cp.wait()


def _exchange(items, name):
    ex = _Exchange(items)
    n = ex.n

    def body(*refs):
        srcs, outs, sems = refs[:n], refs[n:2 * n], refs[2 * n:]
        ex.start(srcs, outs, sems)
        ex.wait(srcs, outs, sems)

    return pl.pallas_call(
        body, name=name,
        out_shape=tuple(ex.out_shape),
        in_specs=[ANY] * n, out_specs=tuple([ANY] * n),
        scratch_shapes=ex.scratch,
        compiler_params=pltpu.CompilerParams(has_side_effects=True),
    )(*ex.arrays)


def _sequencer_exchange(items, name, collective_id):
    ex = _Exchange(items)
    srcs = [jax.new_ref(a, memory_space=pltpu.MemorySpace.HBM) for a in ex.arrays]
    outs = [jax.empty_ref(sh, memory_space=pltpu.MemorySpace.HBM) for sh in ex.out_shape]

    @pl.kernel(mesh=plsc.ScalarSubcoreMesh(axis_name="sequencer", num_cores=1), name=name,
               scratch_types=tuple(ex.scratch), compiler_params=pltpu.CompilerParams(collective_id=collective_id))
    def launch(send_sems, recv_sems, loc_sems):
        x, y, c = lax.axis_index("x"), lax.axis_index("y"), lax.axis_index("c")
        barrier = pltpu.get_barrier_semaphore()
        peers = [(x, y, 1 - c), (1 - x, y, c), (x, 1 - y, c), (1 - x, 1 - y, c)]
        for peer in peers:
            pl.semaphore_signal(barrier, inc=1, device_id=peer, device_id_type=MESH)
        pl.semaphore_wait(barrier, len(peers))
        sems = (send_sems, recv_sems, loc_sems)
        ex.start(srcs, outs, sems)
        ex.wait(srcs, outs, sems)

    launch()
    return [o[...] for o in outs]


def _call(body, inputs, *, name, grid, in_specs, out_specs, out_shape, scratch_shapes=(), vmem_mb=None, hosted=None):
    out_specs, out_shape, scratch_shapes = tuple(out_specs), tuple(out_shape), list(scratch_shapes)
    if not hosted:
        res = pl.pallas_call(
            body, name=name, grid=grid, in_specs=list(in_specs), out_specs=out_specs, out_shape=out_shape,
            scratch_shapes=scratch_shapes, compiler_params=_params(("arbitrary",) * len(grid), vmem_mb),
        )(*inputs)
        return tuple(res), ()
    ex = _Exchange(hosted)
    n, n_in, n_out, n_scr = ex.n, len(inputs), len(out_shape), len(scratch_shapes)

    def hosting_body(*refs):
        ins, srcs = refs[:n_in], refs[n_in:n_in + n]
        outs, landing = refs[n_in + n:n_in + n + n_out], refs[n_in + n + n_out:n_in + 2 * n + n_out]
        scratch, sems = refs[n_in + 2 * n + n_out:n_in + 2 * n + n_out + n_scr], refs[n_in + 2 * n + n_out + n_scr:]
        first = functools.reduce(jnp.logical_and, [pl.program_id(d) == 0 for d in range(len(grid))])
        last = functools.reduce(jnp.logical_and, [pl.program_id(d) == grid[d] - 1 for d in range(len(grid))])

        @pl.when(first)
        def _():
            ex.start(srcs, landing, sems)

        body(*ins, *outs, *scratch)

        @pl.when(last)
        def _():
            ex.wait(srcs, landing, sems)

    res = pl.pallas_call(
        hosting_body, name=name, grid=grid,
        in_specs=list(in_specs) + [ANY] * n, out_specs=out_specs + tuple([ANY] * n),
        out_shape=out_shape + tuple(ex.out_shape), scratch_shapes=scratch_shapes + ex.scratch,
        compiler_params=_params(("arbitrary",) * len(grid), vmem_mb),
    )(*inputs, *ex.arrays)
    return tuple(res[:n_out]), tuple(res[n_out:])


def _mm(a, b, mode, out_dtype, tm, tn, name, hosted=None, vmem_mb=24):
    if mode == "nn":
        (m, k), n = a.shape, b.shape[1]
        a_spec = pl.BlockSpec((tm, k), lambda i, j: (i, 0))
        b_spec = pl.BlockSpec((k, tn), lambda i, j: (0, j))
        dims = NN
    elif mode == "nt":
        (m, k), n = a.shape, b.shape[0]
        a_spec = pl.BlockSpec((tm, k), lambda i, j: (i, 0))
        b_spec = pl.BlockSpec((tn, k), lambda i, j: (j, 0))
        dims = NT
    else:
        (k, m), n = a.shape, b.shape[1]
        a_spec = pl.BlockSpec((k, tm), lambda i, j: (0, i))
        b_spec = pl.BlockSpec((k, tn), lambda i, j: (0, j))
        dims = TN
    assert m % tm == 0 and n % tn == 0, (m, n, tm, tn)

    def body(a_ref, b_ref, o_ref):
        o_ref[...] = _dot(a_ref[...], b_ref[...], dims).astype(o_ref.dtype)

    (out,), moved = _call(
        body, (a, b), name=name, grid=(m // tm, n // tn),
        in_specs=[a_spec, b_spec], out_specs=[pl.BlockSpec((tm, tn), lambda i, j: (i, j))],
        out_shape=[jax.ShapeDtypeStruct((m, n), out_dtype)], vmem_mb=vmem_mb, hosted=hosted)
    return (out, moved) if hosted else out


def _shift_down(x, k, fill):
    y = pltpu.roll(x, k, 0)
    row = lax.broadcasted_iota(jnp.int32, (SUB, x.shape[1]), 0)
    head = y[0:SUB, :]
    for t in range(k):
        head = jnp.where(row == t, fill[t], head)
    return jnp.concatenate([head, y[SUB:, :]], axis=0)


def _shift_up(x, k, fill):
    n = x.shape[0]
    y = pltpu.roll(x, n - k, 0)
    row = lax.broadcasted_iota(jnp.int32, (SUB, x.shape[1]), 0)
    tail = y[n - SUB:, :]
    for t in range(k):
        tail = jnp.where(row == SUB - k + t, fill[t], tail)
    return jnp.concatenate([y[:n - SUB, :], tail], axis=0)


def _conv_taps(x, halo, w):
    if halo is None:
        f1, f2 = [0.0], [0.0, 0.0]
    else:
        f1, f2 = [halo[7:8, :]], [halo[6:7, :], halo[7:8, :]]
    s1 = _shift_down(x, 1, f1)
    s2 = _shift_down(x, 2, f2)
    u = w[2:3, :] * x + w[1:2, :] * s1 + w[0:1, :] * s2
    return u, s1, s2


def _conv_taps_t(du, nxt, w):
    if nxt is None:
        f1, f2 = [0.0], [0.0, 0.0]
    else:
        f1, f2 = [nxt[0:1, :]], [nxt[0:1, :], nxt[1:2, :]]
    return w[2:3, :] * du + w[1:2, :] * _shift_up(du, 1, f1) + w[0:1, :] * _shift_up(du, 2, f2)


def _ada_fwd(c_all, w_ada, b_my):
    def body(c_ref, w_ref, b_ref, o_ref):
        cv = c_ref[...]
        act = cv * _sigmoid(cv)
        o_ref[...] = _dot(act, w_ref[...], NN, lax.Precision.HIGHEST) + b_ref[...]

    return pl.pallas_call(
        body, name="ada_fwd",
        out_shape=jax.ShapeDtypeStruct((NDEV, w_ada.shape[1]), F32),
        compiler_params=_params(None, 32),
    )(c_all, w_ada, b_my)


TR = 256
TRE = 512


def _row_spec(width, col=0, rows=TR):
    return pl.BlockSpec((rows, width), lambda i, col=col: (i, col))


def _erow(width):
    return _row_spec(width, rows=TRE)


def _full_spec(shape):
    return pl.BlockSpec(shape, lambda i: (0,) * len(shape))


def _norm_mod_fwd(x, mod, g):
    s = x.shape[0]

    def body(x_ref, mod_ref, g_ref, h_ref):
        xv = x_ref[...]
        r = lax.rsqrt(jnp.mean(xv * xv, axis=-1, keepdims=True) + EPS)
        nrm = xv * r * g_ref[...]
        h_ref[...] = (nrm * (1.0 + mod_ref[1:2, :]) + mod_ref[0:1, :]).astype(BF16)

    return pl.pallas_call(
        body, name="norm1_fwd", grid=(s // TRE,),
        in_specs=[_erow(D), _full_spec((SUB, D)), _full_spec((1, D))],
        out_specs=_erow(D), out_shape=jax.ShapeDtypeStruct((s, D), BF16),
        compiler_params=_params(("parallel",), 16),
    )(x, mod, g)


SLAB = 2 * DH
AUG_F, AUG_ONE, AUG_LSE = 0, 3, 6


def _split3(x):
    hi = x.astype(BF16).astype(F32)
    r1 = x - hi
    mid = r1.astype(BF16).astype(F32)
    return hi, mid, r1 - mid


def _lanes3(lane, first, pieces, other):
    out = other
    for k in range(3):
        out = jnp.where(lane == first + k, pieces[k], out)
    return out


def _aug_placement():
    eq = np.zeros((3 * LANES, HEADS * SLAB), np.float32)
    ek = np.zeros((3 * LANES, HEADS * SLAB), np.float32)
    ones = np.zeros((SUB, HEADS * SLAB), np.float32)
    for h in range(HEADS):
        aug = SLAB * h + DH
        for k in range(3):
            eq[LANES * k + h, aug + AUG_F + k] = 1.0
            ek[LANES * k + h, aug + AUG_ONE + k] = -1.0
            ones[0, aug + AUG_ONE + k] = 1.0
            ones[1, aug + AUG_F + k] = ones[1, aug + AUG_LSE + k] = 1.0
            ones[2, aug + k] = 1.0
    return jnp.asarray(eq, BF16), jnp.asarray(ek, BF16), jnp.asarray(ones)


def _qkv_prep(proj, fcum, gq, gk, hosted):
    s = proj.shape[0]

    def body(q_ref, k_ref, v_ref, f_ref, gq_ref, gk_ref, eq_ref, ek_ref, ones_ref, qo_ref, ko_ref, vo_ref):
        f3 = jnp.concatenate(_split3(f_ref[...]), axis=1).astype(BF16)
        qo_ref[...] = (_dot(f3, eq_ref[...]) + ones_ref[0:1, :]).astype(BF16)
        ko_ref[...] = (_dot(f3, ek_ref[...]) + ones_ref[1:2, :]).astype(BF16)
        vo_ref[...] = jnp.broadcast_to(ones_ref[2:3, :], vo_ref.shape).astype(BF16)
        for h in range(HEADS):
            sl = slice(DH * h, DH * (h + 1))
            lo = slice(SLAB * h, SLAB * h + DH)
            qh = q_ref[:, sl]
            r = lax.rsqrt(jnp.mean(qh * qh, axis=-1, keepdims=True) + EPS)
            qo_ref[:, lo] = (qh * r * gq_ref[...] * QK_SCALE).astype(BF16)
            kh = k_ref[:, sl]
            r = lax.rsqrt(jnp.mean(kh * kh, axis=-1, keepdims=True) + EPS)
            ko_ref[:, lo] = (kh * r * gk_ref[...]).astype(BF16)
            vo_ref[:, lo] = v_ref[:, sl].astype(BF16)

    eq, ek, ones = _aug_placement()
    o = jax.ShapeDtypeStruct((s, HEADS * SLAB), BF16)
    wide = _row_spec(HEADS * SLAB)
    return _call(
        body, (proj, proj, proj, fcum, gq, gk, eq, ek, ones), name="qkv_prep", grid=(s // TR,),
        in_specs=[_row_spec(AW, 0), _row_spec(AW, 1), _row_spec(AW, 2), _row_spec(LANES),
                  _full_spec((1, DH)), _full_spec((1, DH)), _full_spec(eq.shape), _full_spec(ek.shape),
                  _full_spec(ones.shape)],
        out_specs=[wide, wide, wide], out_shape=[o, o, o], vmem_mb=16, hosted=hosted)


FG_BLOCK = (3 * AW + 3 * CW) // LANES


def _fgate_fwd(proj, bf_pad):
    s = proj.shape[0]

    def body(fg_ref, b_ref, o_ref, carry_ref):
        i = pl.program_id(0)

        @pl.when(i == 0)
        def _():
            carry_ref[...] = jnp.zeros_like(carry_ref)

        z = fg_ref[...] + b_ref[...]
        logf = jnp.minimum(z, 0.0) - jnp.log1p(jnp.exp(-jnp.abs(z)))
        row = lax.broadcasted_iota(jnp.int32, (TR, TR), 0)
        col = lax.broadcasted_iota(jnp.int32, (TR, TR), 1)
        tri = (col <= row).astype(F32)
        cs = _dot(tri, logf, NN, lax.Precision.HIGHEST) + carry_ref[0:1, :]
        o_ref[...] = cs
        carry_ref[...] = jnp.broadcast_to(cs[TR - 1:TR, :], carry_ref.shape)

    return pl.pallas_call(
        body, name="fgate_fwd", grid=(s // TR,),
        in_specs=[_row_spec(LANES, FG_BLOCK), _full_spec((1, LANES))],
        out_specs=_row_spec(LANES), out_shape=jax.ShapeDtypeStruct((s, LANES), F32),
        scratch_shapes=[pltpu.VMEM((SUB, LANES), F32)],
        compiler_params=_params(("arbitrary",)),
    )(proj, bf_pad)


def _fgate_bwd(dfcol, proj, bf_pad):
    s = proj.shape[0]
    nb = s // TR

    def body(df_ref, fg_ref, b_ref, o_ref, db_ref, carry_ref):
        i = pl.program_id(0)

        @pl.when(i == 0)
        def _():
            carry_ref[...] = jnp.zeros_like(carry_ref)
            db_ref[...] = jnp.zeros_like(db_ref)

        row = lax.broadcasted_iota(jnp.int32, (TR, TR), 0)
        col = lax.broadcasted_iota(jnp.int32, (TR, TR), 1)
        tri = (col >= row).astype(F32)
        dlogf = _dot(tri, df_ref[...], NN, lax.Precision.HIGHEST) + carry_ref[0:1, :]
        carry_ref[...] = jnp.broadcast_to(dlogf[0:1, :], carry_ref.shape)
        z = fg_ref[...] + b_ref[...]
        dfg = dlogf * _sigmoid(-z)
        o_ref[...] = dfg.astype(BF16)
        db_ref[0:1, :] += jnp.sum(dfg, axis=0, keepdims=True)

    rev = lambda col: pl.BlockSpec((TR, LANES), lambda i, col=col: (nb - 1 - i, col))
    return pl.pallas_call(
        body, name="fgate_bwd", grid=(nb,),
        in_specs=[rev(0), rev(FG_BLOCK), _full_spec((1, LANES))],
        out_specs=(rev(0), _full_spec((SUB, LANES))),
        out_shape=(jax.ShapeDtypeStruct((s, LANES), BF16), jax.ShapeDtypeStruct((SUB, LANES), F32)),
        scratch_shapes=[pltpu.VMEM((SUB, LANES), F32)],
        compiler_params=_params(("arbitrary",)),
    )(dfcol, proj, bf_pad)


def _resid_norm2(x, z, mod, g):
    s = x.shape[0]

    def body(x_ref, z_ref, mod_ref, g_ref, x1_ref, h_ref):
        x1 = x_ref[...] + mod_ref[2:3, :] * z_ref[...]
        x1_ref[...] = x1
        r = lax.rsqrt(jnp.mean(x1 * x1, axis=-1, keepdims=True) + EPS)
        nrm = x1 * r * g_ref[...]
        h_ref[...] = (nrm * (1.0 + mod_ref[4:5, :]) + mod_ref[3:4, :]).astype(BF16)

    return pl.pallas_call(
        body, name="resid_norm2", grid=(s // TRE,),
        in_specs=[_erow(D), _erow(D), _full_spec((SUB, D)), _full_spec((1, D))],
        out_specs=(_erow(D), _erow(D)),
        out_shape=(jax.ShapeDtypeStruct((s, D), F32), jax.ShapeDtypeStruct((s, D), BF16)),
        compiler_params=_params(("parallel",), 24),
    )(x, z, mod, g)


def _loss_head(x1, y, tgt, mod):
    s = x1.shape[0]

    def body(x1_ref, y_ref, t_ref, mod_ref, dout_ref, dy_ref, vec_ref):
        @pl.when(pl.program_id(0) == 0)
        def _():
            vec_ref[...] = jnp.zeros_like(vec_ref)

        yv = y_ref[...]
        g2 = mod_ref[5:6, :]
        diff = x1_ref[...] + g2 * yv - t_ref[...]
        dout = diff * (1.0 / D)
        dout_ref[...] = dout
        dy_ref[...] = (g2 * dout).astype(BF16)
        vec_ref[0:1, :] += jnp.sum(dout * yv, axis=0, keepdims=True)
        vec_ref[1:2, :] += jnp.sum(diff * diff, axis=0, keepdims=True)

    return pl.pallas_call(
        body, name="loss_head", grid=(s // TRE,),
        in_specs=[_erow(D), _erow(D), _erow(D), _full_spec((SUB, D))],
        out_specs=(_erow(D), _erow(D), _full_spec((SUB, D))),
        out_shape=(jax.ShapeDtypeStruct((s, D), F32), jax.ShapeDtypeStruct((s, D), BF16),
                   jax.ShapeDtypeStruct((SUB, D), F32)),
        compiler_params=_params(("arbitrary",), 24),
    )(x1, y, tgt, mod)


def _norm_mod_bwd(dh, xin, dres, zin, mod, g, scale_row, gate_row, name):
    s = dh.shape[0]
    with_gate = gate_row is not None

    def body(*refs):
        if with_gate:
            dh_ref, x_ref, dres_ref, z_ref, mod_ref, g_ref, dx_ref, dz_ref, vec_ref = refs
        else:
            dh_ref, x_ref, dres_ref, mod_ref, g_ref, dx_ref, vec_ref = refs

        @pl.when(pl.program_id(0) == 0)
        def _():
            vec_ref[...] = jnp.zeros_like(vec_ref)

        xv = x_ref[...]
        dhv = dh_ref[...]
        gv = g_ref[...]
        r = lax.rsqrt(jnp.mean(xv * xv, axis=-1, keepdims=True) + EPS)
        xh = xv * r
        dn = dhv * (1.0 + mod_ref[scale_row:scale_row + 1, :])
        dxh = dn * gv
        dx = dres_ref[...] + r * (dxh - xh * jnp.mean(dxh * xh, axis=-1, keepdims=True))
        dx_ref[...] = dx
        vec_ref[0:1, :] += jnp.sum(dhv, axis=0, keepdims=True)
        vec_ref[1:2, :] += jnp.sum(dhv * (xh * gv), axis=0, keepdims=True)
        vec_ref[2:3, :] += jnp.sum(dn * xh, axis=0, keepdims=True)
        if with_gate:
            dz_ref[...] = (mod_ref[gate_row:gate_row + 1, :] * dx).astype(BF16)
            vec_ref[3:4, :] += jnp.sum(dx * z_ref[...], axis=0, keepdims=True)

    ins = [dh, xin, dres] + ([zin] if with_gate else []) + [mod, g]
    in_specs = [_erow(D)] * (4 if with_gate else 3) + [_full_spec((SUB, D)), _full_spec((1, D))]
    out_specs = [_erow(D)] + ([_erow(D)] if with_gate else []) + [_full_spec((SUB, D))]
    out_shape = [jax.ShapeDtypeStruct((s, D), F32)] + ([jax.ShapeDtypeStruct((s, D), BF16)] if with_gate else []) \
        + [jax.ShapeDtypeStruct((SUB, D), F32)]
    outs, _ = _call(body, ins, name=name, grid=(s // TRE,), in_specs=in_specs, out_specs=out_specs,
                    out_shape=out_shape, vmem_mb=32)
    return outs


XIN_BLOCK = 3 * AW // LANES
BG_BLOCK = XIN_BLOCK + CW // LANES
CG_BLOCK = BG_BLOCK + CW // LANES


def _seq_spec(s, first_block):
    return pl.BlockSpec((s, LANES), lambda j, fb=first_block: (0, fb + j))


def _mixconv_fwd(proj, w):
    s = proj.shape[0]

    def body(xin_ref, bg_ref, cg_ref, w_ref, o_ref):
        cx = cg_ref[...] * xin_ref[...]
        cv, _, _ = _conv_taps(cx, None, w_ref[...])
        o_ref[...] = bg_ref[...] * cv

    return pl.pallas_call(
        body, name="mixconv_fwd", grid=(CW // LANES,),
        in_specs=[_seq_spec(s, XIN_BLOCK), _seq_spec(s, BG_BLOCK), _seq_spec(s, CG_BLOCK),
                  pl.BlockSpec((3, LANES), lambda j: (0, j))],
        out_specs=_seq_spec(s, 0), out_shape=jax.ShapeDtypeStruct((s, CW), F32),
        compiler_params=_params(("parallel",), 32),
    )(proj, proj, proj, w)


def _mixconv_bwd(dmixed, proj, w):
    s = proj.shape[0]

    def body(d_ref, xin_ref, bg_ref, cg_ref, w_ref, dxin_ref, dbg_ref, dcg_ref, dw_ref):
        wv = w_ref[...]
        xin, cg, dconv = xin_ref[...], cg_ref[...], d_ref[...]
        cx = cg * xin
        cv, s1, s2 = _conv_taps(cx, None, wv)
        dbg_ref[...] = (dconv * cv).astype(BF16)
        dcv = dconv * bg_ref[...]
        dw_ref[...] = jnp.zeros_like(dw_ref)
        dw_ref[0:1, :] = jnp.sum(dcv * s2, axis=0, keepdims=True)
        dw_ref[1:2, :] = jnp.sum(dcv * s1, axis=0, keepdims=True)
        dw_ref[2:3, :] = jnp.sum(dcv * cx, axis=0, keepdims=True)
        dcx = _conv_taps_t(dcv, None, wv)
        dcg_ref[...] = (dcx * xin).astype(BF16)
        dxin_ref[...] = (dcx * cg).astype(BF16)

    o = jax.ShapeDtypeStruct((s, CW), BF16)
    return pl.pallas_call(
        body, name="mixconv_bwd", grid=(CW // LANES,),
        in_specs=[_seq_spec(s, AW // LANES), _seq_spec(s, XIN_BLOCK), _seq_spec(s, BG_BLOCK), _seq_spec(s, CG_BLOCK),
                  pl.BlockSpec((3, LANES), lambda j: (0, j))],
        out_specs=(_seq_spec(s, 0), _seq_spec(s, 0), _seq_spec(s, 0), pl.BlockSpec((SUB, LANES), lambda j: (0, j))),
        out_shape=(o, o, o, jax.ShapeDtypeStruct((SUB, CW), F32)),
        compiler_params=_params(("parallel",), 32),
    )(dmixed, proj, proj, proj, w)


TA = 512
NEG = -1e30


def _causal_mask():
    row = lax.broadcasted_iota(jnp.int32, (TA, TA), 0)
    col = lax.broadcasted_iota(jnp.int32, (TA, TA), 1)
    return col <= row


def _attn_fwd(qp, kp, vp, hosted):
    s = qp.shape[0]
    nq = s // TA

    def body(q_ref, k_ref, v_ref, o_ref, lse_ref):
        i = pl.program_id(1)
        slabs = [slice(SLAB * hh, SLAB * (hh + 1)) for hh in range(2)]
        q = [q_ref[:, sl] for sl in slabs]

        def block(j, carry, masked):
            keys = pl.ds(pl.multiple_of(j * TA, TA), TA)
            ms, acc = carry
            m_out, parts = [], []
            for hh in range(2):
                sc = _dot(q[hh], k_ref[keys, slabs[hh]], NT)
                if masked:
                    sc = jnp.where(_causal_mask(), sc, NEG)
                m_new = jnp.maximum(ms[hh], jnp.max(sc, axis=-1, keepdims=True))
                p = jnp.exp(sc - m_new)
                parts.append(jnp.exp(ms[hh] - m_new) * acc[:, slabs[hh]] + _dot(p.astype(BF16), v_ref[keys, slabs[hh]]))
                m_out.append(m_new)
            return tuple(m_out), jnp.concatenate(parts, axis=1)

        init = ((jnp.full((TA, 1), NEG, F32), jnp.full((TA, 1), NEG, F32)), jnp.zeros((TA, 2 * SLAB), F32))
        carry = lax.fori_loop(0, i, lambda j, cr: block(j, cr, False), init)
        ms, acc = block(i, carry, True)
        for hh in range(2):
            l = acc[:, SLAB * hh + DH:SLAB * hh + DH + 1]
            o_ref[:, DH * hh:DH * (hh + 1)] = acc[:, SLAB * hh:SLAB * hh + DH] / l
            lse_ref[0, :, hh:hh + 1] = ms[hh] + jnp.log(l)

    (o, lse), moved = _call(
        body, (qp, kp, vp), name="attn_fwd", grid=(HEADS // 2, nq),
        in_specs=[pl.BlockSpec((TA, 2 * SLAB), lambda p, i: (i, p)),
                  pl.BlockSpec((s, 2 * SLAB), lambda p, i: (0, p)),
                  pl.BlockSpec((s, 2 * SLAB), lambda p, i: (0, p))],
        out_specs=[pl.BlockSpec((TA, LANES), lambda p, i: (i, p)), pl.BlockSpec((1, TA, 2), lambda p, i: (p, i, 0))],
        out_shape=[jax.ShapeDtypeStruct((s, AW), F32), jax.ShapeDtypeStruct((HEADS // 2, s, 2), F32)],
        vmem_mb=24, hosted=hosted)
    return o, lse, moved


def _attn_bwd(qp, kp, vp, dmixed, o, lse, hosted):
    s = qp.shape[0]
    nq = s // TA

    def body(q_ref, k_ref, v_ref, do_ref, o_ref, lse_ref, dq_ref, dk_ref, dv_ref, qb_ref, dob_ref):
        dk_ref[...] = jnp.zeros_like(dk_ref)
        dv_ref[...] = jnp.zeros_like(dv_ref)
        slabs = [slice(SLAB * hh, SLAB * (hh + 1)) for hh in range(2)]
        lane = lax.broadcasted_iota(jnp.int32, (TA, DH), 1)

        def q_block(i, _):
            i0 = pl.multiple_of(i * TA, TA)
            rows = pl.ds(i0, TA)
            for hh in range(2):
                half = slice(DH * hh, DH * (hh + 1))
                do = do_ref[rows, half]
                delta = jnp.sum(do * o_ref[rows, half], axis=-1, keepdims=True)
                dob_ref[hh, :, 0:DH] = do.astype(BF16)
                dob_ref[hh, :, DH:SLAB] = _lanes3(lane, 0, [-d for d in _split3(delta)], 0.0).astype(BF16)
                lse3 = _split3(lse_ref[0, rows, hh:hh + 1])
                qb_ref[hh, :, 0:DH] = q_ref[rows, SLAB * hh:SLAB * hh + DH]
                aug = q_ref[rows, SLAB * hh + DH:SLAB * (hh + 1)].astype(F32)
                qb_ref[hh, :, DH:SLAB] = _lanes3(lane, AUG_LSE, [-x for x in lse3], aug).astype(BF16)

            def block(j, dq, masked):
                keys = pl.ds(pl.multiple_of(j * TA, TA), TA)
                dv, dk, dqc = [], [], []
                for hh in range(2):
                    q, dob = qb_ref[hh], dob_ref[hh]
                    k = k_ref[keys, slabs[hh]]
                    sc = _dot(q, k, NT)
                    if masked:
                        sc = jnp.where(_causal_mask(), sc, NEG)
                    p = jnp.exp(sc)
                    dv.append(_dot(p.astype(BF16), dob, TN))
                    ds = (p * _dot(dob, v_ref[keys, slabs[hh]], NT)).astype(BF16)
                    dk.append(_dot(ds, q, TN))
                    dqc.append(_dot(ds, k))
                dv_ref[keys, :] += jnp.concatenate(dv, axis=1)
                dk_ref[keys, :] += jnp.concatenate(dk, axis=1)
                return dq + jnp.concatenate(dqc, axis=1)

            dq = lax.fori_loop(0, i, lambda j, acc: block(j, acc, False), jnp.zeros((TA, 2 * SLAB), F32))
            dq_ref[rows, :] = block(i, dq, True)
            return 0

        lax.fori_loop(0, nq, q_block, 0)

    pair = lambda p: (0, p)
    slab2 = pl.BlockSpec((s, 2 * SLAB), pair)
    seq = pl.BlockSpec((s, LANES), pair)
    small = pl.BlockSpec((1, s, 2), lambda p: (p, 0, 0))
    o32 = jax.ShapeDtypeStruct((s, HEADS * SLAB), F32)
    return _call(
        body, (qp, kp, vp, dmixed, o, lse), name="attn_bwd", grid=(HEADS // 2,),
        in_specs=[slab2, slab2, slab2, seq, seq, small], out_specs=[slab2, slab2, slab2], out_shape=[o32, o32, o32],
        scratch_shapes=[pltpu.VMEM((2, TA, SLAB), BF16), pltpu.VMEM((2, TA, SLAB), BF16)], vmem_mb=40, hosted=hosted)


def _qkv_post(dqp, dkp, dvp, proj, gq, gk, hosted):
    s = proj.shape[0]

    def body(dq_ref, dk_ref, dv_ref, q_ref, k_ref, gq_ref, gk_ref, dqo_ref, dko_ref, dvo_ref, df_ref, vec_ref):
        @pl.when(pl.program_id(0) == 0)
        def _():
            vec_ref[...] = jnp.zeros_like(vec_ref)

        def one(d_ref, x_ref, g_ref, o_ref, row, scale):
            dg = jnp.zeros((1, DH), F32)
            for h in range(HEADS):
                sl = slice(DH * h, DH * (h + 1))
                xv = x_ref[:, sl]
                r = lax.rsqrt(jnp.mean(xv * xv, axis=-1, keepdims=True) + EPS)
                xh = xv * r
                dn = d_ref[:, SLAB * h:SLAB * h + DH] * scale
                dg = dg + jnp.sum(dn * xh, axis=0, keepdims=True)
                dxh = dn * g_ref[...]
                o_ref[:, sl] = (r * (dxh - xh * jnp.mean(dxh * xh, axis=-1, keepdims=True))).astype(BF16)
            vec_ref[row:row + 1, 0:DH] += dg

        one(dq_ref, q_ref, gq_ref, dqo_ref, 0, QK_SCALE)
        one(dk_ref, k_ref, gk_ref, dko_ref, 1, 1.0)
        lane = lax.broadcasted_iota(jnp.int32, (TR, LANES), 1)
        df = jnp.zeros((TR, LANES), F32)
        for h in range(HEADS):
            dvo_ref[:, DH * h:DH * (h + 1)] = dv_ref[:, SLAB * h:SLAB * h + DH].astype(BF16)
            row_sum = dq_ref[:, SLAB * h + DH:SLAB * h + DH + 1]
            col_sum = dk_ref[:, SLAB * h + DH + AUG_ONE:SLAB * h + DH + AUG_ONE + 1]
            df = jnp.where(lane == h, row_sum - col_sum, df)
        df_ref[...] = df

    o = jax.ShapeDtypeStruct((s, AW), BF16)
    wide = _row_spec(HEADS * SLAB)
    return _call(
        body, (dqp, dkp, dvp, proj, proj, gq, gk), name="qkv_post", grid=(s // TR,),
        in_specs=[wide, wide, wide, _row_spec(AW, 0), _row_spec(AW, 1), _full_spec((1, DH)), _full_spec((1, DH))],
        out_specs=[_row_spec(AW), _row_spec(AW), _row_spec(AW), _row_spec(LANES), _full_spec((SUB, LANES))],
        out_shape=[o, o, o, jax.ShapeDtypeStruct((s, LANES), F32), jax.ShapeDtypeStruct((SUB, LANES), F32)],
        hosted=hosted)


TF = 256
NJ = DFF // TF
FFN_ROWS_FWD = 1024
FFN_ROWS_BWD = 1024


def _ffn_fwd(h2, wup_t, cw, wd):
    s = h2.shape[0]
    tr = FFN_ROWS_FWD
    nr = s // tr

    def body(h_ref, wu_ref, cg_ref, cv_ref, wd_ref, pg_ref, pv_ref, y_ref, halo_ref, act_ref):
        r, j = pl.program_id(0), pl.program_id(1)
        hv = h_ref[...]
        pg = _dot(hv, wu_ref[0], NT).astype(BF16)
        pv = _dot(hv, wu_ref[1], NT).astype(BF16)
        pg_ref[...] = pg
        pv_ref[...] = pv
        pgf, pvf = pg.astype(F32), pv.astype(F32)
        ug, _, _ = _conv_taps(pgf, jnp.where(r > 0, halo_ref[j, 0], 0.0), cg_ref[...])
        uv, _, _ = _conv_taps(pvf, jnp.where(r > 0, halo_ref[j, 1], 0.0), cv_ref[...])
        halo_ref[j, 0] = pgf[tr - SUB:tr, :]
        halo_ref[j, 1] = pvf[tr - SUB:tr, :]
        act = (ug * _sigmoid(ug) * uv).astype(BF16)
        for t in range(NJ):
            @pl.when(j == t)
            def _(t=t):
                act_ref[:, t * TF:(t + 1) * TF] = act

        @pl.when(j == NJ - 1)
        def _():
            y_ref[...] = _dot(act_ref[...], wd_ref[...])

    pre = jax.ShapeDtypeStruct((s, DFF), BF16)
    return pl.pallas_call(
        body, name="ffn_fwd", grid=(nr, NJ),
        in_specs=[pl.BlockSpec((tr, D), lambda r, j: (r, 0)),
                  pl.BlockSpec((2, TF, D), lambda r, j: (0, j, 0)),
                  pl.BlockSpec((3, TF), lambda r, j: (0, j)),
                  pl.BlockSpec((3, TF), lambda r, j: (0, NJ + j)),
                  pl.BlockSpec((DFF, D), lambda r, j: (0, 0))],
        out_specs=(pl.BlockSpec((tr, TF), lambda r, j: (r, j)),
                   pl.BlockSpec((tr, TF), lambda r, j: (r, j)),
                   pl.BlockSpec((tr, D), lambda r, j: (r, 0))),
        out_shape=(pre, pre, jax.ShapeDtypeStruct((s, D), F32)),
        scratch_shapes=[pltpu.VMEM((NJ, 2, SUB, TF), F32), pltpu.VMEM((tr, DFF), BF16)],
        compiler_params=_params(("arbitrary", "arbitrary"), 56),
    )(h2, wup_t, cw, cw, wd)


def _ffn_bwd(dy, h2, pre_g, pre_v, wup_t, cw, wd):
    s = h2.shape[0]
    tr = FFN_ROWS_BWD
    nr = s // tr
    hb = tr // (2 * SUB)

    def body(dy_ref, h_ref, pg_ref, pv_ref, hg_ref, hv_ref, wu_ref, cg_ref, cv_ref, wd_ref,
             dh_ref, dwu_ref, dwd_ref, dcg_ref, dcv_ref, nxt_ref, awu_ref, awd_ref):
        j, r = pl.program_id(0), pl.program_id(1)
        rr = nr - 1 - r
        row0 = pl.multiple_of(rr * tr, tr)
        cwg, cwv = cg_ref[...], cv_ref[...]
        pg, pv = pg_ref[...].astype(F32), pv_ref[...].astype(F32)
        ug, g1, g2 = _conv_taps(pg, jnp.where(rr > 0, hg_ref[SUB:2 * SUB, :].astype(F32), 0.0), cwg)
        uv, v1, v2 = _conv_taps(pv, jnp.where(rr > 0, hv_ref[SUB:2 * SUB, :].astype(F32), 0.0), cwv)
        sg = _sigmoid(ug)
        sil = ug * sg
        act = (sil * uv).astype(BF16)
        dyv = dy_ref[...]
        da = _dot(dyv, wd_ref[...], NT)
        dug = da * uv * (sg * (1.0 + ug * (1.0 - sg)))
        duv = da * sil
        dpg = _conv_taps_t(dug, jnp.where(r > 0, nxt_ref[0], 0.0), cwg)
        dpv = _conv_taps_t(duv, jnp.where(r > 0, nxt_ref[1], 0.0), cwv)
        nxt_ref[0] = dug[0:SUB, :]
        nxt_ref[1] = duv[0:SUB, :]
        dpgb, dpvb = dpg.astype(BF16), dpv.astype(BF16)
        hv = h_ref[...]
        dwd = _dot(act, dyv, TN)
        dpb = jnp.concatenate([dpgb, dpvb], axis=1)
        dwu = _dot(dpb, hv, TN)
        dh = _dot(dpb, wu_ref[...].reshape(2 * TF, D))

        def taps(du, x0, x1, x2):
            return (jnp.sum(du * x2, axis=0, keepdims=True), jnp.sum(du * x1, axis=0, keepdims=True),
                    jnp.sum(du * x0, axis=0, keepdims=True))

        tg, tv = taps(dug, pg, g1, g2), taps(duv, pv, v1, v2)

        @pl.when(r == 0)
        def _():
            awd_ref[...] = dwd
            awu_ref[...] = dwu
            dcg_ref[...] = jnp.zeros_like(dcg_ref)
            dcv_ref[...] = jnp.zeros_like(dcv_ref)

        @pl.when(r > 0)
        def _():
            awd_ref[...] += dwd
            awu_ref[...] += dwu

        @pl.when(r == nr - 1)
        def _():
            dwd_ref[...] = awd_ref[...].astype(BF16)
            dwu_ref[...] = awu_ref[...].astype(BF16).reshape(2, TF, D)

        for t in range(3):
            dcg_ref[t:t + 1, :] += tg[t]
            dcv_ref[t:t + 1, :] += tv[t]

        @pl.when(j == 0)
        def _():
            dh_ref[pl.ds(row0, tr), :] = dh

        @pl.when(j > 0)
        def _():
            dh_ref[pl.ds(row0, tr), :] += dh

    rows = lambda j, r: (nr - 1 - r, 0)
    tile = lambda j, r: (nr - 1 - r, j)
    halo = lambda j, r: (jnp.maximum((nr - 1 - r) * hb - 1, 0), j)
    return pl.pallas_call(
        body, name="ffn_bwd", grid=(NJ, nr),
        in_specs=[pl.BlockSpec((tr, D), rows), pl.BlockSpec((tr, D), rows),
                  pl.BlockSpec((tr, TF), tile), pl.BlockSpec((tr, TF), tile),
                  pl.BlockSpec((2 * SUB, TF), halo), pl.BlockSpec((2 * SUB, TF), halo),
                  pl.BlockSpec((2, TF, D), lambda j, r: (0, j, 0)),
                  pl.BlockSpec((3, TF), lambda j, r: (0, j)), pl.BlockSpec((3, TF), lambda j, r: (0, NJ + j)),
                  pl.BlockSpec((TF, D), lambda j, r: (j, 0))],
        out_specs=(pl.BlockSpec((s, D), lambda j, r: (0, 0)),
                   pl.BlockSpec((2, TF, D), lambda j, r: (0, j, 0)),
                   pl.BlockSpec((TF, D), lambda j, r: (j, 0)),
                   pl.BlockSpec((SUB, TF), lambda j, r: (0, j)), pl.BlockSpec((SUB, TF), lambda j, r: (0, j))),
        out_shape=(jax.ShapeDtypeStruct((s, D), F32),
                   jax.ShapeDtypeStruct((2, DFF, D), BF16), jax.ShapeDtypeStruct((DFF, D), BF16),
                   jax.ShapeDtypeStruct((SUB, DFF), F32), jax.ShapeDtypeStruct((SUB, DFF), F32)),
        scratch_shapes=[pltpu.VMEM((2, SUB, TF), F32), pltpu.VMEM((2 * TF, D), F32), pltpu.VMEM((TF, D), F32)],
        compiler_params=_params(("arbitrary", "arbitrary"), 56),
    )(dy, h2, pre_g, pre_v, pre_g, pre_v, wup_t, cw, cw, wd)


def _adam(w, g, m, v):
    m = ADAM_B1 * m + (1.0 - ADAM_B1) * g
    v = ADAM_B2 * v + (1.0 - ADAM_B2) * (g * g)
    m_hat = m / (1.0 - ADAM_B1 ** ADAM_STEP)
    v_hat = v / (1.0 - ADAM_B2 ** ADAM_STEP)
    delta = -ADAM_LR * (m_hat / (jnp.sqrt(v_hat) + ADAM_EPS) + ADAM_WD * w)
    return delta, m, v


NCHIP = NDEV // 2


def _pair_add(mine, theirs, tr, name):
    _, _, rws, cols = mine.shape

    def body(a_ref, b_ref, o_ref):
        c = lax.axis_index("c")
        o_ref[0] = (a_ref[0, c].astype(F32) + b_ref[0].astype(F32)).astype(BF16)

    (out,), _ = _call(
        body, (mine, theirs), name=name, grid=(NCHIP, rws // tr),
        in_specs=[pl.BlockSpec((1, 2, tr, cols), lambda q, i: (q, 0, i, 0)),
                  pl.BlockSpec((1, tr, cols), lambda q, i: (q, i, 0))],
        out_specs=[pl.BlockSpec((1, tr, cols), lambda q, i: (q, i, 0))],
        out_shape=[jax.ShapeDtypeStruct((NCHIP, rws, cols), BF16)], vmem_mb=16)
    return out


def _adamw_sharded(parts, w, m, v, tr, name, hosted=None):
    rws, cols = w.shape

    def body(p_ref, w_ref, m_ref, v_ref, g_ref, d_ref, mo_ref, vo_ref):
        g = p_ref[0].astype(F32)
        for q in range(1, NCHIP):
            g = g + p_ref[q].astype(F32)
        g_ref[...] = g
        d_ref[...], mo_ref[...], vo_ref[...] = _adam(w_ref[...], g, m_ref[...], v_ref[...])

    blk = pl.BlockSpec((tr, cols), lambda i: (i, 0))
    o = jax.ShapeDtypeStruct((rws, cols), F32)
    outs, moved = _call(
        body, (parts, w, m, v), name=name, grid=(rws // tr,),
        in_specs=[pl.BlockSpec((NCHIP, tr, cols), lambda i: (0, i, 0)), blk, blk, blk],
        out_specs=[blk, blk, blk, blk], out_shape=[o, o, o, o], vmem_mb=36 if tr > 256 else 24, hosted=hosted)
    return (outs, moved) if hosted else outs


def _adamw_ada(c_all, dmod_my, w, m, v):
    rws, cols = w.shape
    tr = 256

    def body(c_ref, dm_ref, w_ref, m_ref, v_ref, g_ref, d_ref, mo_ref, vo_ref):
        cv = c_ref[...]
        act = cv * _sigmoid(cv)
        g = _dot(act, dm_ref[...], TN, lax.Precision.HIGHEST)
        g_ref[...] = g
        d_ref[...], mo_ref[...], vo_ref[...] = _adam(w_ref[...], g, m_ref[...], v_ref[...])

    blk = pl.BlockSpec((tr, cols), lambda i: (i, 0))
    o = jax.ShapeDtypeStruct((rws, cols), F32)
    return pl.pallas_call(
        body, name="adamw_ada", grid=(rws // tr,),
        in_specs=[pl.BlockSpec((NDEV, tr), lambda i: (0, i)), _full_spec((NDEV, cols)), blk, blk, blk],
        out_specs=(blk, blk, blk, blk), out_shape=(o, o, o, o),
        compiler_params=_params(("parallel",), 32),
    )(c_all, dmod_my, w, m, v)


REP_ROWS = 16
ROW_N1, ROW_N2, ROW_LOSS, ROW_MISC = 6, 7, 8, 9
LANE_BF, LANE_GQ, LANE_GK = 0, 128, 256


def _adamw_small(rep_all, conv_all, wmv):
    n_ff = wmv[6][0].shape[1]

    def body(*refs):
        rep_ref, conv_ref = refs[:2]
        ins = refs[2:2 + 24]
        outs = refs[2 + 24:]
        loss_ref, outs = outs[0], outs[1:]
        g_rep = rep_ref[0]
        g_conv = conv_ref[0]
        for d in range(1, NDEV):
            g_rep = g_rep + rep_ref[d]
            g_conv = g_conv + conv_ref[d]
        loss_ref[...] = (0.5 / D) * jnp.sum(g_rep[ROW_LOSS:ROW_LOSS + 1, :], axis=-1, keepdims=True)
        grads = [
            None,
            g_rep[ROW_N1:ROW_N1 + 1, :],
            g_rep[ROW_MISC:ROW_MISC + 1, LANE_BF:LANE_BF + HEADS],
            g_rep[ROW_MISC:ROW_MISC + 1, LANE_GQ:LANE_GQ + DH],
            g_rep[ROW_MISC:ROW_MISC + 1, LANE_GK:LANE_GK + DH],
            g_rep[ROW_N2:ROW_N2 + 1, :],
            g_conv[0:3, 0:n_ff],
            g_conv[0:3, n_ff:n_ff + DH],
        ]
        for p in range(8):
            w_ref, m_ref, v_ref = ins[3 * p:3 * p + 3]
            g_ref, d_ref, mo_ref, vo_ref = outs[4 * p:4 * p + 4]
            if p == 0:
                for nmod in range(NMOD):
                    sl = slice(D * nmod, D * (nmod + 1))
                    g = g_rep[nmod:nmod + 1, :]
                    g_ref[:, sl] = g
                    d_ref[:, sl], mo_ref[:, sl], vo_ref[:, sl] = _adam(w_ref[:, sl], g, m_ref[:, sl], v_ref[:, sl])
            else:
                g = grads[p]
                g_ref[...] = g
                d_ref[...], mo_ref[...], vo_ref[...] = _adam(w_ref[...], g, m_ref[...], v_ref[...])

    flat = [a for trio in wmv for a in trio]
    out_shape = [jax.ShapeDtypeStruct((1, 1), F32)]
    for trio in wmv:
        out_shape += [jax.ShapeDtypeStruct(trio[0].shape, F32)] * 4
    return pl.pallas_call(
        body, name="adamw_small", out_shape=tuple(out_shape),
        compiler_params=_params(None, 32),
    )(rep_all, conv_all, *flat)


FG_FIRST = 3 * AW
N_IN = DIN // NDEV


def _w_in_runs():
    runs = []
    for d in range(NDEV):
        lo, hi = N_IN * d, N_IN * (d + 1)
        for a, b, shift in ((0, FG_FIRST, 0), (FG_FIRST, FG_FIRST + HEADS, DIN - HEADS - FG_FIRST),
                            (FG_FIRST + HEADS, DIN, -HEADS)):
            a, b = max(a, lo), min(b, hi)
            if a < b:
                runs.append((d, a - lo, a + shift, b - a))
    return runs


W_IN_ROWS = 256
N_IN_PAD = 512


def _identity(n):
    return (lax.broadcasted_iota(jnp.int32, (n, n), 0) == lax.broadcasted_iota(jnp.int32, (n, n), 1)).astype(BF16)


def _assemble_w_in(g_in, hosted):
    def body(g_ref, o_ref, t_ref):
        eye = _identity(W_IN_ROWS)
        shard = None
        for d, src, dst, width in _w_in_runs():
            if d != shard:
                t_ref[:, 0:N_IN] = _dot(eye, g_ref[d], NT).astype(BF16)
                shard = d
            o_ref[:, dst:dst + width] = t_ref[:, src:src + width]
        o_ref[:, DIN:DINP] = jnp.zeros((W_IN_ROWS, DINP - DIN), o_ref.dtype)

    (out,), moved = _call(
        body, (g_in,), name="assemble_w_in", grid=(D // W_IN_ROWS,),
        in_specs=[pl.BlockSpec((NDEV, N_IN, W_IN_ROWS), lambda i: (0, 0, i))],
        out_specs=[pl.BlockSpec((W_IN_ROWS, DINP), lambda i: (i, 0))],
        out_shape=[jax.ShapeDtypeStruct((D, DINP), g_in.dtype)],
        scratch_shapes=[pltpu.VMEM((W_IN_ROWS, N_IN_PAD), BF16)], vmem_mb=16, hosted=hosted)
    return out, moved


def _scatter_dw_in(dwp):
    def body(w_ref, o_ref, t_ref):
        eye = _identity(W_IN_ROWS)
        runs = _w_in_runs()
        for i, (d, src, dst, width) in enumerate(runs):
            t_ref[:, src:src + width] = w_ref[:, dst:dst + width]
            if i + 1 == len(runs) or runs[i + 1][0] != d:
                o_ref[d // 2, d % 2] = _dot(t_ref[:, 0:N_IN], eye, TN).astype(BF16)

    (out,), _ = _call(
        body, (dwp,), name="scatter_dw_in", grid=(D // W_IN_ROWS,),
        in_specs=[pl.BlockSpec((W_IN_ROWS, DINP), lambda i: (i, 0))],
        out_specs=[pl.BlockSpec((NCHIP, 2, N_IN, W_IN_ROWS), lambda i: (0, 0, 0, i))],
        out_shape=[jax.ShapeDtypeStruct((NCHIP, 2, N_IN, D), dwp.dtype)],
        scratch_shapes=[pltpu.VMEM((W_IN_ROWS, N_IN_PAD), BF16)], vmem_mb=16)
    return out


def kernel(x, c, w_ada, b_ada, norm1_g, w_in, b_forget, q_norm_g, k_norm_g, conv_mix_w, w_out, norm2_g, w_up, ffn_conv_w, w_down, loss_target, m_w_ada, m_b_ada, m_norm1_g, m_w_in, m_b_forget, m_q_norm_g, m_k_norm_g, m_conv_mix_w, m_w_out, m_norm2_g, m_w_up, m_ffn_conv_w, m_w_down, v_w_ada, v_b_ada, v_norm1_g, v_w_in, v_b_forget, v_q_norm_g, v_k_norm_g, v_conv_mix_w, v_w_out, v_norm2_g, v_w_up, v_ffn_conv_w, v_w_down):
    me = 4 * lax.axis_index("x") + 2 * lax.axis_index("y") + lax.axis_index("c")
    xs, tgt = x[0], loss_target[0]
    n_ada = w_ada.shape[2]
    n_ff = w_up.shape[2]

    conv_w = jnp.concatenate([ffn_conv_w[0], conv_mix_w[0]], axis=1)
    conv_w = jnp.concatenate([conv_w, jnp.zeros((SUB - 3, conv_w.shape[1]), F32)], axis=0)
    c_all, conv_all, g_in = _exchange(
        [(c.reshape(SUB, D // SUB), "ag"), (conv_w, "ag"), (jnp.transpose(w_in[0]).astype(BF16), "ag2")],
        "exchange_w_in")
    g_in, w_out_b, w_up_b, w_down_b = lax.optimization_barrier(
        (g_in, w_out[0].astype(BF16), jnp.transpose(w_up[0]).astype(BF16), w_down[0].astype(BF16)))
    g_out, g_up, g_down = _sequencer_exchange(
        [(w_out_b, "ag2"), (w_up_b, "ag2"), (w_down_b, "ag2")], "gather_weights", collective_id=1)
    c_all = c_all.reshape(NDEV, D)
    cw_ffn = jnp.transpose(conv_all[:, :3, :n_ff], (1, 0, 2)).reshape(3, 2 * DFF)
    cw_mix = jnp.transpose(conv_all[:, :3, n_ff:], (1, 0, 2)).reshape(3, CW)

    b_my = lax.dynamic_slice(b_ada, (0, me * n_ada), (1, n_ada))
    mod_part = _ada_fwd(c_all, w_ada[0], b_my)
    w_in_p, (mod_rows,) = _assemble_w_in(
        g_in, [(jnp.broadcast_to(mod_part[:, None, :], (NDEV, SUB, n_ada)), "a2a")])
    mod = mod_rows[:, 0, :].reshape(NMOD, D)
    mod = jnp.concatenate([mod, jnp.zeros((SUB - NMOD, D), F32)], axis=0)

    h = _norm_mod_fwd(xs, mod, norm1_g)
    proj = _mm(h, w_in_p, "nn", F32, 1024, 640, "proj_fwd")
    bf_pad = jnp.concatenate([b_forget, jnp.zeros((1, LANES - HEADS), F32)], axis=1)
    fcum = _fgate_fwd(proj, bf_pad)
    (qp, kp, vp), _ = _qkv_prep(proj, fcum, q_norm_g, k_norm_g, None)
    attn, lse, _ = _attn_fwd(qp, kp, vp, None)
    w_out_f = g_out.reshape(D, D)
    w_up_t = g_up.reshape(2, DFF, D)
    w_down_f = g_down.reshape(DFF, D)
    conv = _mixconv_fwd(proj, cw_mix)
    mixed = jnp.concatenate([attn, conv], axis=1).astype(BF16)
    z = _mm(mixed, w_out_f, "nn", F32, 1024, 1024, "out_fwd")
    x1, h2 = _resid_norm2(xs, z, mod, norm2_g)
    pre_g, pre_v, y = _ffn_fwd(h2, w_up_t, cw_ffn, w_down_f)
    dout, dy, vec_l = _loss_head(x1, y, tgt, mod)

    dh2, dwup_t, dwd, dcw_g, dcw_v = _ffn_bwd(dy, h2, pre_g, pre_v, w_up_t, cw_ffn, w_down_f)
    dx1, dz, vec_2 = _norm_mod_bwd(dh2, x1, dout, z, mod, norm2_g, 4, 2, "norm2_bwd")
    dwout = _mm(mixed, dz, "tn", BF16, 1024, 1024, "out_bwd_w")
    s_out = dwout.reshape(NCHIP, 2, D // NDEV, D)
    s_down = dwd.reshape(NCHIP, 2, DFF // NDEV, D)
    s_up = dwup_t.reshape(NCHIP, 2, n_ff, D)
    dmixed, (t_out, t_up, t_down) = _mm(dz, w_out_f, "nt", F32, 1024, 1024, "out_bwd_x",
                                        hosted=[(s_out, "pair"), (s_up, "pair"), (s_down, "pair")])
    c_out = _pair_add(s_out, t_out, 128, "pair_add_out")
    c_up = _pair_add(s_up, t_up, 176, "pair_add_up")
    c_down = _pair_add(s_down, t_down, 176, "pair_add_down")
    dxin, dbg, dcg, dcw_mix = _mixconv_bwd(dmixed, proj, cw_mix)
    (dqp, dkp, dvp), (p_up, p_down) = _attn_bwd(qp, kp, vp, dmixed, attn, lse, [(c_up, "chips"), (c_down, "chips")])
    (dq, dk, dvb, dfcol, vec_qk), (p_out,) = _qkv_post(dqp, dkp, dvp, proj, q_norm_g, k_norm_g, [(c_out, "chips")])
    dfg, vec_bf = _fgate_bwd(dfcol, proj, bf_pad)
    dproj = jnp.concatenate([dq, dk, dvb, dxin, dbg, dcg, dfg], axis=1)
    dwin_p = _mm(h, dproj, "tn", BF16, 1024, 640, "proj_bwd_w")
    s_in = _scatter_dw_in(dwin_p)
    (t_in,) = _exchange([(s_in, "pair")], "exchange_pair_in")
    c_in = _pair_add(s_in, t_in, N_IN, "pair_add_in")
    dh, (p_in,) = _mm(dproj, w_in_p, "nt", F32, 1024, 512, "proj_bwd_x", hosted=[(c_in, "chips")], vmem_mb=36)
    grad_x, vec_1 = _norm_mod_bwd(dh, xs, dx1, None, mod, norm1_g, 1, None, "norm1_bwd")

    misc = jnp.zeros((1, D), F32)
    misc = lax.dynamic_update_slice(misc, vec_bf[0:1, :HEADS], (0, LANE_BF))
    misc = lax.dynamic_update_slice(misc, vec_qk[0:1, :DH], (0, LANE_GQ))
    misc = lax.dynamic_update_slice(misc, vec_qk[1:2, :DH], (0, LANE_GK))
    rep = jnp.concatenate([
        vec_1[0:1], vec_1[1:2], vec_2[3:4], vec_2[0:1], vec_2[1:2], vec_l[0:1],
        vec_1[2:3], vec_2[2:3], vec_l[1:2], misc, jnp.zeros((REP_ROWS - 10, D), F32)], axis=0)
    dcw_ffn = jnp.concatenate([dcw_g, dcw_v], axis=1).reshape(SUB, NDEV, n_ff)
    dcw_all = jnp.concatenate([jnp.transpose(dcw_ffn, (1, 0, 2)),
                               jnp.transpose(dcw_mix.reshape(SUB, NDEV, DH), (1, 0, 2))], axis=2)
    r_out, (rep_all, conv_parts) = _adamw_sharded(p_out, w_out[0], m_w_out[0], v_w_out[0], 128, "adamw_out",
                                                  hosted=[(rep, "ag"), (dcw_all, "a2a")])
    dmod_my = lax.dynamic_slice(rep_all[:, :NMOD, :].reshape(NDEV, NMOD * D), (0, me * n_ada), (NDEV, n_ada))
    r_ada = _adamw_ada(c_all, dmod_my, w_ada[0], m_w_ada[0], v_w_ada[0])
    r_in = _adamw_sharded(p_in, jnp.transpose(w_in[0]), jnp.transpose(m_w_in[0]), jnp.transpose(v_w_in[0]), N_IN,
                          "adamw_in")
    r_in = tuple(jnp.transpose(a) for a in r_in)
    r_up = _adamw_sharded(p_up, jnp.transpose(w_up[0]), jnp.transpose(m_w_up[0]), jnp.transpose(v_w_up[0]), 176,
                          "adamw_up")
    r_up = tuple(jnp.transpose(a) for a in r_up)
    r_down = _adamw_sharded(p_down, w_down[0], m_w_down[0], v_w_down[0], 176, "adamw_down")
    small = _adamw_small(rep_all, conv_parts, [
        [b_ada, m_b_ada, v_b_ada], [norm1_g, m_norm1_g, v_norm1_g], [b_forget, m_b_forget, v_b_forget],
        [q_norm_g, m_q_norm_g, v_q_norm_g], [k_norm_g, m_k_norm_g, v_k_norm_g], [norm2_g, m_norm2_g, v_norm2_g],
        [ffn_conv_w[0], m_ffn_conv_w[0], v_ffn_conv_w[0]], [conv_mix_w[0], m_conv_mix_w[0], v_conv_mix_w[0]]])
    loss = small[0].reshape(())
    r_bada, r_n1, r_bf, r_gq, r_gk, r_n2, r_cf, r_cm = [small[1 + 4 * p:5 + 4 * p] for p in range(8)]
    lead = lambda t: tuple(a[None] for a in t)
    per_w = [lead(r_ada), r_bada, r_n1, lead(r_in), r_bf, r_gq, r_gk, lead(r_cm), lead(r_out), r_n2,
             lead(r_up), lead(r_cf), lead(r_down)]
    outs = [loss, grad_x[None]]
    for field in range(4):
        outs += [t[field] for t in per_w]
    return tuple(outs)
```

```python
import functools

import jax
import jax.numpy as jnp
import numpy as np
from jax import lax
from jax.experimental import pallas as pl
from jax.experimental.pallas import tpu as pltpu
from jax.experimental.pallas import tpu_sc as plsc

F32 = jnp.float32
BF16 = jnp.bfloat16

NDEV = 8
D = 1024
HEADS = 8
DH = 64
AW = 512
CW = 512
DFF = 2816
DIN = 3080
DINP = 3200
NMOD = 6
EPS = 1e-6
QK_SCALE = 0.125
LANES = 128
SUB = 8

ADAM_LR = 0.001
ADAM_B1 = 0.9
ADAM_B2 = 0.999
ADAM_EPS = 1e-08
ADAM_WD = 0.01
ADAM_STEP = 10

MESH = pl.DeviceIdType.MESH
ANY = pl.BlockSpec(memory_space=pl.ANY)

NN = (((1,), (0,)), ((), ()))
NT = (((1,), (1,)), ((), ()))
TN = (((0,), (0,)), ((), ()))


def _dot(a, b, dims=NN, precision=None):
    return lax.dot_general(a, b, dims, precision=precision, preferred_element_type=F32)


def _params(sem=None, vmem_mb=None):
    kw = {}
    if sem is not None:
        kw["dimension_semantics"] = sem
    if vmem_mb is not None:
        kw["vmem_limit_bytes"] = vmem_mb * 1024 * 1024
    return pltpu.CompilerParams(**kw)


def _sigmoid(x):
    return 0.5 * jnp.tanh(0.5 * x) + 0.5


class _Exchange:
    def __init__(self, items):
        self.arrays = [pltpu.with_memory_space_constraint(a, pltpu.HBM) for a, _ in items]
        self.modes = [m for _, m in items]
        self.n = len(items)
        self.out_shape = []
        for a, m in items:
            sh = {"ag": (NDEV,) + a.shape, "ag2": (NDEV,) + a.shape, "pair": a.shape[:1] + a.shape[2:]}.get(m, a.shape)
            self.out_shape.append(jax.ShapeDtypeStruct(sh, a.dtype))
        self.scratch = [pltpu.SemaphoreType.DMA((self.n, NDEV - 1)), pltpu.SemaphoreType.DMA((self.n, NDEV - 1)),
                        pltpu.SemaphoreType.DMA((self.n,))]

    def _plan(self, srcs, outs, sems):
        send_sems, recv_sems, loc_sems = sems
        x, y, c = lax.axis_index("x"), lax.axis_index("y"), lax.axis_index("c")
        me, my_chip = 4 * x + 2 * y + c, 2 * x + y
        sib = (x, y, 1 - c)
        local, first, landed, forwards, arrivals = [], [], [], [], []

        def remote(a, k, src, dst, to):
            return pltpu.make_async_remote_copy(src_ref=src, dst_ref=dst, send_sem=send_sems.at[a, k],
                                                recv_sem=recv_sems.at[a, k], device_id=to, device_id_type=MESH)

        for a, mode in enumerate(self.modes):
            src, out = srcs[a], outs[a]
            if mode in ("ag", "a2a"):
                piece = (lambda slot, src=src: src) if mode == "ag" else (lambda slot, src=src: src.at[slot])
                local.append(pltpu.make_async_copy(piece(me), out.at[me], loc_sems.at[a]))
                for r in range(1, NDEV):
                    px = 1 - x if (r >> 2) & 1 else x
                    py = 1 - y if (r >> 1) & 1 else y
                    pc = 1 - c if r & 1 else c
                    pidx = 4 * px + 2 * py + pc
                    first.append(remote(a, r - 1, piece(pidx), out.at[me], (px, py, pc)))
                    arrivals.append(remote(a, r - 1, piece(pidx), out.at[pidx], (px, py, pc)))
            elif mode == "ag2":
                local.append(pltpu.make_async_copy(src, out.at[me], loc_sems.at[a]))
                first.append(remote(a, 0, src, out.at[me], sib))
                arrivals.append(remote(a, 0, src, out.at[me + 1 - 2 * c], sib))
                for j, (px, py) in enumerate([(1 - x, y), (x, 1 - y), (1 - x, 1 - y)]):
                    theirs = out.at[4 * px + 2 * py + c]
                    first.append(remote(a, 1 + j, src, out.at[me], (px, py, c)))
                    landed.append(remote(a, 1 + j, src, theirs, (px, py, c)))
                    forwards.append(remote(a, 4 + j, theirs, theirs, sib))
                    arrivals.append(remote(a, 4 + j, src, out.at[4 * px + 2 * py + 1 - c], sib))
            elif mode == "pair":
                for q in range(NDEV // 2):
                    first.append(remote(a, q, src.at[q, 1 - c], out.at[q], sib))
                    arrivals.append(remote(a, q, src.at[q, 1 - c], out.at[q], sib))
            else:
                assert mode == "chips", mode
                local.append(pltpu.make_async_copy(src.at[my_chip], out.at[my_chip], loc_sems.at[a]))
                for j, (px, py) in enumerate([(1 - x, y), (x, 1 - y), (1 - x, 1 - y)]):
                    q = 2 * px + py
                    first.append(remote(a, 1 + j, src.at[q], out.at[my_chip], (px, py, c)))
                    arrivals.append(remote(a, 1 + j, src.at[q], out.at[q], (px, py, c)))
        return local, first, landed, forwards, arrivals

    def start(self, srcs, outs, sems):
        local, first, _, _, _ = self._plan(srcs, outs, sems)
        for cp in local + first:
            cp.start()

    def wait(self, srcs, outs, sems):
        local, first, landed, forwards, arrivals = self._plan(srcs, outs, sems)
        for cp, fwd in zip(landed, forwards):
            cp.wait_recv()
            fwd.start()
        for cp in arrivals:
            cp.wait_recv()
        for cp in first + forwards:
            cp.wait_send()
        for cp in local:
            cp.wait()


def _exchange(items, name):
    ex = _Exchange(items)
    n = ex.n

    def body(*refs):
        srcs, outs, sems = refs[:n], refs[n:2 * n], refs[2 * n:]
        ex.start(srcs, outs, sems)
        ex.wait(srcs, outs, sems)

    return pl.pallas_call(
        body, name=name,
        out_shape=tuple(ex.out_shape),
        in_specs=[ANY] * n, out_specs=tuple([ANY] * n),
        scratch_shapes=ex.scratch,
        compiler_params=pltpu.CompilerParams(has_side_effects=True),
    )(*ex.arrays)


def _sequencer_exchange(items, name, collective_id):
    ex = _Exchange(items)
    srcs = [jax.new_ref(a, memory_space=pltpu.MemorySpace.HBM) for a in ex.arrays]
    outs = [jax.empty_ref(sh, memory_space=pltpu.MemorySpace.HBM) for sh in ex.out_shape]

    @pl.kernel(mesh=plsc.ScalarSubcoreMesh(axis_name="sequencer", num_cores=1), name=name,
               scratch_types=tuple(ex.scratch), compiler_params=pltpu.CompilerParams(collective_id=collective_id))
    def launch(send_sems, recv_sems, loc_sems):
        x, y, c = lax.axis_index("x"), lax.axis_index("y"), lax.axis_index("c")
        barrier = pltpu.get_barrier_semaphore()
        peers = [(x, y, 1 - c), (1 - x, y, c), (x, 1 - y, c), (1 - x, 1 - y, c)]
        for peer in peers:
            pl.semaphore_signal(barrier, inc=1, device_id=peer, device_id_type=MESH)
        pl.semaphore_wait(barrier, len(peers))
        sems = (send_sems, recv_sems, loc_sems)
        ex.start(srcs, outs, sems)
        ex.wait(srcs, outs, sems)

    launch()
    return [o[...] for o in outs]


def _call(body, inputs, *, name, grid, in_specs, out_specs, out_shape, scratch_shapes=(), vmem_mb=None, hosted=None):
    out_specs, out_shape, scratch_shapes = tuple(out_specs), tuple(out_shape), list(scratch_shapes)
    if not hosted:
        res = pl.pallas_call(
            body, name=name, grid=grid, in_specs=list(in_specs), out_specs=out_specs, out_shape=out_shape,
            scratch_shapes=scratch_shapes, compiler_params=_params(("arbitrary",) * len(grid), vmem_mb),
        )(*inputs)
        return tuple(res), ()
    ex = _Exchange(hosted)
    n, n_in, n_out, n_scr = ex.n, len(inputs), len(out_shape), len(scratch_shapes)

    def hosting_body(*refs):
        ins, srcs = refs[:n_in], refs[n_in:n_in + n]
        outs, landing = refs[n_in + n:n_in + n + n_out], refs[n_in + n + n_out:n_in + 2 * n + n_out]
        scratch, sems = refs[n_in + 2 * n + n_out:n_in + 2 * n + n_out + n_scr], refs[n_in + 2 * n + n_out + n_scr:]
        first = functools.reduce(jnp.logical_and, [pl.program_id(d) == 0 for d in range(len(grid))])
        last = functools.reduce(jnp.logical_and, [pl.program_id(d) == grid[d] - 1 for d in range(len(grid))])

        @pl.when(first)
        def _():
            ex.start(srcs, landing, sems)

        body(*ins, *outs, *scratch)

        @pl.when(last)
        def _():
            ex.wait(srcs, landing, sems)

    res = pl.pallas_call(
        hosting_body, name=name, grid=grid,
        in_specs=list(in_specs) + [ANY] * n, out_specs=out_specs + tuple([ANY] * n),
        out_shape=out_shape + tuple(ex.out_shape), scratch_shapes=scratch_shapes + ex.scratch,
        compiler_params=_params(("arbitrary",) * len(grid), vmem_mb),
    )(*inputs, *ex.arrays)
    return tuple(res[:n_out]), tuple(res[n_out:])


def _mm(a, b, mode, out_dtype, tm, tn, name, hosted=None, vmem_mb=24):
    if mode == "nn":
        (m, k), n = a.shape, b.shape[1]
        a_spec = pl.BlockSpec((tm, k), lambda i, j: (i, 0))
        b_spec = pl.BlockSpec((k, tn), lambda i, j: (0, j))
        dims = NN
    elif mode == "nt":
        (m, k), n = a.shape, b.shape[0]
        a_spec = pl.BlockSpec((tm, k), lambda i, j: (i, 0))
        b_spec = pl.BlockSpec((tn, k), lambda i, j: (j, 0))
        dims = NT
    else:
        (k, m), n = a.shape, b.shape[1]
        a_spec = pl.BlockSpec((k, tm), lambda i, j: (0, i))
        b_spec = pl.BlockSpec((k, tn), lambda i, j: (0, j))
        dims = TN
    assert m % tm == 0 and n % tn == 0, (m, n, tm, tn)

    def body(a_ref, b_ref, o_ref):
        o_ref[...] = _dot(a_ref[...], b_ref[...], dims).astype(o_ref.dtype)

    (out,), moved = _call(
        body, (a, b), name=name, grid=(m // tm, n // tn),
        in_specs=[a_spec, b_spec], out_specs=[pl.BlockSpec((tm, tn), lambda i, j: (i, j))],
        out_shape=[jax.ShapeDtypeStruct((m, n), out_dtype)], vmem_mb=vmem_mb, hosted=hosted)
    return (out, moved) if hosted else out


def _shift_down(x, k, fill):
    y = pltpu.roll(x, k, 0)
    row = lax.broadcasted_iota(jnp.int32, (SUB, x.shape[1]), 0)
    head = y[0:SUB, :]
    for t in range(k):
        head = jnp.where(row == t, fill[t], head)
    return jnp.concatenate([head, y[SUB:, :]], axis=0)


def _shift_up(x, k, fill):
    n = x.shape[0]
    y = pltpu.roll(x, n - k, 0)
    row = lax.broadcasted_iota(jnp.int32, (SUB, x.shape[1]), 0)
    tail = y[n - SUB:, :]
    for t in range(k):
        tail = jnp.where(row == SUB - k + t, fill[t], tail)
    return jnp.concatenate([y[:n - SUB, :], tail], axis=0)


def _conv_taps(x, halo, w):
    if halo is None:
        f1, f2 = [0.0], [0.0, 0.0]
    else:
        f1, f2 = [halo[7:8, :]], [halo[6:7, :], halo[7:8, :]]
    s1 = _shift_down(x, 1, f1)
    s2 = _shift_down(x, 2, f2)
    u = w[2:3, :] * x + w[1:2, :] * s1 + w[0:1, :] * s2
    return u, s1, s2


def _conv_taps_t(du, nxt, w):
    if nxt is None:
        f1, f2 = [0.0], [0.0, 0.0]
    else:
        f1, f2 = [nxt[0:1, :]], [nxt[0:1, :], nxt[1:2, :]]
    return w[2:3, :] * du + w[1:2, :] * _shift_up(du, 1, f1) + w[0:1, :] * _shift_up(du, 2, f2)


def _ada_fwd(c_all, w_ada, b_my):
    def body(c_ref, w_ref, b_ref, o_ref):
        cv = c_ref[...]
        act = cv * _sigmoid(cv)
        o_ref[...] = _dot(act, w_ref[...], NN, lax.Precision.HIGHEST) + b_ref[...]

    return pl.pallas_call(
        body, name="ada_fwd",
        out_shape=jax.ShapeDtypeStruct((NDEV, w_ada.shape[1]), F32),
        compiler_params=_params(None, 32),
    )(c_all, w_ada, b_my)


TR = 256
TRE = 512


def _row_spec(width, col=0, rows=TR):
    return pl.BlockSpec((rows, width), lambda i, col=col: (i, col))


def _erow(width):
    return _row_spec(width, rows=TRE)


def _full_spec(shape):
    return pl.BlockSpec(shape, lambda i: (0,) * len(shape))


def _norm_mod_fwd(x, mod, g):
    s = x.shape[0]

    def body(x_ref, mod_ref, g_ref, h_ref):
        xv = x_ref[...]
        r = lax.rsqrt(jnp.mean(xv * xv, axis=-1, keepdims=True) + EPS)
        nrm = xv * r * g_ref[...]
        h_ref[...] = (nrm * (1.0 + mod_ref[1:2, :]) + mod_ref[0:1, :]).astype(BF16)

    return pl.pallas_call(
        body, name="norm1_fwd", grid=(s // TRE,),
        in_specs=[_erow(D), _full_spec((SUB, D)), _full_spec((1, D))],
        out_specs=_erow(D), out_shape=jax.ShapeDtypeStruct((s, D), BF16),
        compiler_params=_params(("parallel",), 16),
    )(x, mod, g)


SLAB = 2 * DH
AUG_F, AUG_ONE, AUG_LSE = 0, 3, 6


def _split3(x):
    hi = x.astype(BF16).astype(F32)
    r1 = x - hi
    mid = r1.astype(BF16).astype(F32)
    return hi, mid, r1 - mid


def _lanes3(lane, first, pieces, other):
    out = other
    for k in range(3):
        out = jnp.where(lane == first + k, pieces[k], out)
    return out


def _aug_placement():
    eq = np.zeros((3 * LANES, HEADS * SLAB), np.float32)
    ek = np.zeros((3 * LANES, HEADS * SLAB), np.float32)
    ones = np.zeros((SUB, HEADS * SLAB), np.float32)
    for h in range(HEADS):
        aug = SLAB * h + DH
        for k in range(3):
            eq[LANES * k + h, aug + AUG_F + k] = 1.0
            ek[LANES * k + h, aug + AUG_ONE + k] = -1.0
            ones[0, aug + AUG_ONE + k] = 1.0
            ones[1, aug + AUG_F + k] = ones[1, aug + AUG_LSE + k] = 1.0
            ones[2, aug + k] = 1.0
    return jnp.asarray(eq, BF16), jnp.asarray(ek, BF16), jnp.asarray(ones)


def _qkv_prep(proj, fcum, gq, gk, hosted):
    s = proj.shape[0]

    def body(q_ref, k_ref, v_ref, f_ref, gq_ref, gk_ref, eq_ref, ek_ref, ones_ref, qo_ref, ko_ref, vo_ref):
        f3 = jnp.concatenate(_split3(f_ref[...]), axis=1).astype(BF16)
        qo_ref[...] = (_dot(f3, eq_ref[...]) + ones_ref[0:1, :]).astype(BF16)
        ko_ref[...] = (_dot(f3, ek_ref[...]) + ones_ref[1:2, :]).astype(BF16)
        vo_ref[...] = jnp.broadcast_to(ones_ref[2:3, :], vo_ref.shape).astype(BF16)
        for h in range(HEADS):
            sl = slice(DH * h, DH * (h + 1))
            lo = slice(SLAB * h, SLAB * h + DH)
            qh = q_ref[:, sl]
            r = lax.rsqrt(jnp.mean(qh * qh, axis=-1, keepdims=True) + EPS)
            qo_ref[:, lo] = (qh * r * gq_ref[...] * QK_SCALE).astype(BF16)
            kh = k_ref[:, sl]
            r = lax.rsqrt(jnp.mean(kh * kh, axis=-1, keepdims=True) + EPS)
            ko_ref[:, lo] = (kh * r * gk_ref[...]).astype(BF16)
            vo_ref[:, lo] = v_ref[:, sl].astype(BF16)

    eq, ek, ones = _aug_placement()
    o = jax.ShapeDtypeStruct((s, HEADS * SLAB), BF16)
    wide = _row_spec(HEADS * SLAB)
    return _call(
        body, (proj, proj, proj, fcum, gq, gk, eq, ek, ones), name="qkv_prep", grid=(s // TR,),
        in_specs=[_row_spec(AW, 0), _row_spec(AW, 1), _row_spec(AW, 2), _row_spec(LANES),
                  _full_spec((1, DH)), _full_spec((1, DH)), _full_spec(eq.shape), _full_spec(ek.shape),
                  _full_spec(ones.shape)],
        out_specs=[wide, wide, wide], out_shape=[o, o, o], vmem_mb=16, hosted=hosted)


FG_BLOCK = (3 * AW + 3 * CW) // LANES


def _fgate_fwd(proj, bf_pad):
    s = proj.shape[0]

    def body(fg_ref, b_ref, o_ref, carry_ref):
        i = pl.program_id(0)

        @pl.when(i == 0)
        def _():
            carry_ref[...] = jnp.zeros_like(carry_ref)

        z = fg_ref[...] + b_ref[...]
        logf = jnp.minimum(z, 0.0) - jnp.log1p(jnp.exp(-jnp.abs(z)))
        row = lax.broadcasted_iota(jnp.int32, (TR, TR), 0)
        col = lax.broadcasted_iota(jnp.int32, (TR, TR), 1)
        tri = (col <= row).astype(F32)
        cs = _dot(tri, logf, NN, lax.Precision.HIGHEST) + carry_ref[0:1, :]
        o_ref[...] = cs
        carry_ref[...] = jnp.broadcast_to(cs[TR - 1:TR, :], carry_ref.shape)

    return pl.pallas_call(
        body, name="fgate_fwd", grid=(s // TR,),
        in_specs=[_row_spec(LANES, FG_BLOCK), _full_spec((1, LANES))],
        out_specs=_row_spec(LANES), out_shape=jax.ShapeDtypeStruct((s, LANES), F32),
        scratch_shapes=[pltpu.VMEM((SUB, LANES), F32)],
        compiler_params=_params(("arbitrary",)),
    )(proj, bf_pad)


def _fgate_bwd(dfcol, proj, bf_pad):
    s = proj.shape[0]
    nb = s // TR

    def body(df_ref, fg_ref, b_ref, o_ref, db_ref, carry_ref):
        i = pl.program_id(0)

        @pl.when(i == 0)
        def _():
            carry_ref[...] = jnp.zeros_like(carry_ref)
            db_ref[...] = jnp.zeros_like(db_ref)

        row = lax.broadcasted_iota(jnp.int32, (TR, TR), 0)
        col = lax.broadcasted_iota(jnp.int32, (TR, TR), 1)
        tri = (col >= row).astype(F32)
        dlogf = _dot(tri, df_ref[...], NN, lax.Precision.HIGHEST) + carry_ref[0:1, :]
        carry_ref[...] = jnp.broadcast_to(dlogf[0:1, :], carry_ref.shape)
        z = fg_ref[...] + b_ref[...]
        dfg = dlogf * _sigmoid(-z)
        o_ref[...] = dfg.astype(BF16)
        db_ref[0:1, :] += jnp.sum(dfg, axis=0, keepdims=True)

    rev = lambda col: pl.BlockSpec((TR, LANES), lambda i, col=col: (nb - 1 - i, col))
    return pl.pallas_call(
        body, name="fgate_bwd", grid=(nb,),
        in_specs=[rev(0), rev(FG_BLOCK), _full_spec((1, LANES))],
        out_specs=(rev(0), _full_spec((SUB, LANES))),
        out_shape=(jax.ShapeDtypeStruct((s, LANES), BF16), jax.ShapeDtypeStruct((SUB, LANES), F32)),
        scratch_shapes=[pltpu.VMEM((SUB, LANES), F32)],
        compiler_params=_params(("arbitrary",)),
    )(dfcol, proj, bf_pad)


def _resid_norm2(x, z, mod, g):
    s = x.shape[0]

    def body(x_ref, z_ref, mod_ref, g_ref, x1_ref, h_ref):
        x1 = x_ref[...] + mod_ref[2:3, :] * z_ref[...]
        x1_ref[...] = x1
        r = lax.rsqrt(jnp.mean(x1 * x1, axis=-1, keepdims=True) + EPS)
        nrm = x1 * r * g_ref[...]
        h_ref[...] = (nrm * (1.0 + mod_ref[4:5, :]) + mod_ref[3:4, :]).astype(BF16)

    return pl.pallas_call(
        body, name="resid_norm2", grid=(s // TRE,),
        in_specs=[_erow(D), _erow(D), _full_spec((SUB, D)), _full_spec((1, D))],
        out_specs=(_erow(D), _erow(D)),
        out_shape=(jax.ShapeDtypeStruct((s, D), F32), jax.ShapeDtypeStruct((s, D), BF16)),
        compiler_params=_params(("parallel",), 24),
    )(x, z, mod, g)


def _loss_head(x1, y, tgt, mod):
    s = x1.shape[0]

    def body(x1_ref, y_ref, t_ref, mod_ref, dout_ref, dy_ref, vec_ref):
        @pl.when(pl.program_id(0) == 0)
        def _():
            vec_ref[...] = jnp.zeros_like(vec_ref)

        yv = y_ref[...]
        g2 = mod_ref[5:6, :]
        diff = x1_ref[...] + g2 * yv - t_ref[...]
        dout = diff * (1.0 / D)
        dout_ref[...] = dout
        dy_ref[...] = (g2 * dout).astype(BF16)
        vec_ref[0:1, :] += jnp.sum(dout * yv, axis=0, keepdims=True)
        vec_ref[1:2, :] += jnp.sum(diff * diff, axis=0, keepdims=True)

    return pl.pallas_call(
        body, name="loss_head", grid=(s // TRE,),
        in_specs=[_erow(D), _erow(D), _erow(D), _full_spec((SUB, D))],
        out_specs=(_erow(D), _erow(D), _full_spec((SUB, D))),
        out_shape=(jax.ShapeDtypeStruct((s, D), F32), jax.ShapeDtypeStruct((s, D), BF16),
                   jax.ShapeDtypeStruct((SUB, D), F32)),
        compiler_params=_params(("arbitrary",), 24),
    )(x1, y, tgt, mod)


def _norm_mod_bwd(dh, xin, dres, zin, mod, g, scale_row, gate_row, name):
    s = dh.shape[0]
    with_gate = gate_row is not None

    def body(*refs):
        if with_gate:
            dh_ref, x_ref, dres_ref, z_ref, mod_ref, g_ref, dx_ref, dz_ref, vec_ref = refs
        else:
            dh_ref, x_ref, dres_ref, mod_ref, g_ref, dx_ref, vec_ref = refs

        @pl.when(pl.program_id(0) == 0)
        def _():
            vec_ref[...] = jnp.zeros_like(vec_ref)

        xv = x_ref[...]
        dhv = dh_ref[...]
        gv = g_ref[...]
        r = lax.rsqrt(jnp.mean(xv * xv, axis=-1, keepdims=True) + EPS)
        xh = xv * r
        dn = dhv * (1.0 + mod_ref[scale_row:scale_row + 1, :])
        dxh = dn * gv
        dx = dres_ref[...] + r * (dxh - xh * jnp.mean(dxh * xh, axis=-1, keepdims=True))
        dx_ref[...] = dx
        vec_ref[0:1, :] += jnp.sum(dhv, axis=0, keepdims=True)
        vec_ref[1:2, :] += jnp.sum(dhv * (xh * gv), axis=0, keepdims=True)
        vec_ref[2:3, :] += jnp.sum(dn * xh, axis=0, keepdims=True)
        if with_gate:
            dz_ref[...] = (mod_ref[gate_row:gate_row + 1, :] * dx).astype(BF16)
            vec_ref[3:4, :] += jnp.sum(dx * z_ref[...], axis=0, keepdims=True)

    ins = [dh, xin, dres] + ([zin] if with_gate else []) + [mod, g]
    in_specs = [_erow(D)] * (4 if with_gate else 3) + [_full_spec((SUB, D)), _full_spec((1, D))]
    out_specs = [_erow(D)] + ([_erow(D)] if with_gate else []) + [_full_spec((SUB, D))]
    out_shape = [jax.ShapeDtypeStruct((s, D), F32)] + ([jax.ShapeDtypeStruct((s, D), BF16)] if with_gate else []) \
        + [jax.ShapeDtypeStruct((SUB, D), F32)]
    outs, _ = _call(body, ins, name=name, grid=(s // TRE,), in_specs=in_specs, out_specs=out_specs,
                    out_shape=out_shape, vmem_mb=32)
    return outs


XIN_BLOCK = 3 * AW // LANES
BG_BLOCK = XIN_BLOCK + CW // LANES
CG_BLOCK = BG_BLOCK + CW // LANES


def _seq_spec(s, first_block):
    return pl.BlockSpec((s, LANES), lambda j, fb=first_block: (0, fb + j))


def _mixconv_fwd(proj, w):
    s = proj.shape[0]

    def body(xin_ref, bg_ref, cg_ref, w_ref, o_ref):
        cx = cg_ref[...] * xin_ref[...]
        cv, _, _ = _conv_taps(cx, None, w_ref[...])
        o_ref[...] = bg_ref[...] * cv

    return pl.pallas_call(
        body, name="mixconv_fwd", grid=(CW // LANES,),
        in_specs=[_seq_spec(s, XIN_BLOCK), _seq_spec(s, BG_BLOCK), _seq_spec(s, CG_BLOCK),
                  pl.BlockSpec((3, LANES), lambda j: (0, j))],
        out_specs=_seq_spec(s, 0), out_shape=jax.ShapeDtypeStruct((s, CW), F32),
        compiler_params=_params(("parallel",), 32),
    )(proj, proj, proj, w)


def _mixconv_bwd(dmixed, proj, w):
    s = proj.shape[0]

    def body(d_ref, xin_ref, bg_ref, cg_ref, w_ref, dxin_ref, dbg_ref, dcg_ref, dw_ref):
        wv = w_ref[...]
        xin, cg, dconv = xin_ref[...], cg_ref[...], d_ref[...]
        cx = cg * xin
        cv, s1, s2 = _conv_taps(cx, None, wv)
        dbg_ref[...] = (dconv * cv).astype(BF16)
        dcv = dconv * bg_ref[...]
        dw_ref[...] = jnp.zeros_like(dw_ref)
        dw_ref[0:1, :] = jnp.sum(dcv * s2, axis=0, keepdims=True)
        dw_ref[1:2, :] = jnp.sum(dcv * s1, axis=0, keepdims=True)
        dw_ref[2:3, :] = jnp.sum(dcv * cx, axis=0, keepdims=True)
        dcx = _conv_taps_t(dcv, None, wv)
        dcg_ref[...] = (dcx * xin).astype(BF16)
        dxin_ref[...] = (dcx * cg).astype(BF16)

    o = jax.ShapeDtypeStruct((s, CW), BF16)
    return pl.pallas_call(
        body, name="mixconv_bwd", grid=(CW // LANES,),
        in_specs=[_seq_spec(s, AW // LANES), _seq_spec(s, XIN_BLOCK), _seq_spec(s, BG_BLOCK), _seq_spec(s, CG_BLOCK),
                  pl.BlockSpec((3, LANES), lambda j: (0, j))],
        out_specs=(_seq_spec(s, 0), _seq_spec(s, 0), _seq_spec(s, 0), pl.BlockSpec((SUB, LANES), lambda j: (0, j))),
        out_shape=(o, o, o, jax.ShapeDtypeStruct((SUB, CW), F32)),
        compiler_params=_params(("parallel",), 32),
    )(dmixed, proj, proj, proj, w)


TA = 512
NEG = -1e30


def _causal_mask():
    row = lax.broadcasted_iota(jnp.int32, (TA, TA), 0)
    col = lax.broadcasted_iota(jnp.int32, (TA, TA), 1)
    return col <= row


def _attn_fwd(qp, kp, vp, hosted):
    s = qp.shape[0]
    nq = s // TA

    def body(q_ref, k_ref, v_ref, o_ref, lse_ref):
        i = pl.program_id(1)
        slabs = [slice(SLAB * hh, SLAB * (hh + 1)) for hh in range(2)]
        q = [q_ref[:, sl] for sl in slabs]

        def block(j, carry, masked):
            keys = pl.ds(pl.multiple_of(j * TA, TA), TA)
            ms, acc = carry
            m_out, parts = [], []
            for hh in range(2):
                sc = _dot(q[hh], k_ref[keys, slabs[hh]], NT)
                if masked:
                    sc = jnp.where(_causal_mask(), sc, NEG)
                m_new = jnp.maximum(ms[hh], jnp.max(sc, axis=-1, keepdims=True))
                p = jnp.exp(sc - m_new)
                parts.append(jnp.exp(ms[hh] - m_new) * acc[:, slabs[hh]] + _dot(p.astype(BF16), v_ref[keys, slabs[hh]]))
                m_out.append(m_new)
            return tuple(m_out), jnp.concatenate(parts, axis=1)

        init = ((jnp.full((TA, 1), NEG, F32), jnp.full((TA, 1), NEG, F32)), jnp.zeros((TA, 2 * SLAB), F32))
        carry = lax.fori_loop(0, i, lambda j, cr: block(j, cr, False), init)
        ms, acc = block(i, carry, True)
        for hh in range(2):
            l = acc[:, SLAB * hh + DH:SLAB * hh + DH + 1]
            o_ref[:, DH * hh:DH * (hh + 1)] = acc[:, SLAB * hh:SLAB * hh + DH] / l
            lse_ref[0, :, hh:hh + 1] = ms[hh] + jnp.log(l)

    (o, lse), moved = _call(
        body, (qp, kp, vp), name="attn_fwd", grid=(HEADS // 2, nq),
        in_specs=[pl.BlockSpec((TA, 2 * SLAB), lambda p, i: (i, p)),
                  pl.BlockSpec((s, 2 * SLAB), lambda p, i: (0, p)),
                  pl.BlockSpec((s, 2 * SLAB), lambda p, i: (0, p))],
        out_specs=[pl.BlockSpec((TA, LANES), lambda p, i: (i, p)), pl.BlockSpec((1, TA, 2), lambda p, i: (p, i, 0))],
        out_shape=[jax.ShapeDtypeStruct((s, AW), F32), jax.ShapeDtypeStruct((HEADS // 2, s, 2), F32)],
        vmem_mb=24, hosted=hosted)
    return o, lse, moved


def _attn_bwd(qp, kp, vp, dmixed, o, lse, hosted):
    s = qp.shape[0]
    nq = s // TA

    def body(q_ref, k_ref, v_ref, do_ref, o_ref, lse_ref, dq_ref, dk_ref, dv_ref, qb_ref, dob_ref):
        dk_ref[...] = jnp.zeros_like(dk_ref)
        dv_ref[...] = jnp.zeros_like(dv_ref)
        slabs = [slice(SLAB * hh, SLAB * (hh + 1)) for hh in range(2)]
        lane = lax.broadcasted_iota(jnp.int32, (TA, DH), 1)

        def q_block(i, _):
            i0 = pl.multiple_of(i * TA, TA)
            rows = pl.ds(i0, TA)
            for hh in range(2):
                half = slice(DH * hh, DH * (hh + 1))
                do = do_ref[rows, half]
                delta = jnp.sum(do * o_ref[rows, half], axis=-1, keepdims=True)
                dob_ref[hh, :, 0:DH] = do.astype(BF16)
                dob_ref[hh, :, DH:SLAB] = _lanes3(lane, 0, [-d for d in _split3(delta)], 0.0).astype(BF16)
                lse3 = _split3(lse_ref[0, rows, hh:hh + 1])
                qb_ref[hh, :, 0:DH] = q_ref[rows, SLAB * hh:SLAB * hh + DH]
                aug = q_ref[rows, SLAB * hh + DH:SLAB * (hh + 1)].astype(F32)
                qb_ref[hh, :, DH:SLAB] = _lanes3(lane, AUG_LSE, [-x for x in lse3], aug).astype(BF16)

            def block(j, dq, masked):
                keys = pl.ds(pl.multiple_of(j * TA, TA), TA)
                dv, dk, dqc = [], [], []
                for hh in range(2):
                    q, dob = qb_ref[hh], dob_ref[hh]
                    k = k_ref[keys, slabs[hh]]
                    sc = _dot(q, k, NT)
                    if masked:
                        sc = jnp.where(_causal_mask(), sc, NEG)
                    p = jnp.exp(sc)
                    dv.append(_dot(p.astype(BF16), dob, TN))
                    ds = (p * _dot(dob, v_ref[keys, slabs[hh]], NT)).astype(BF16)
                    dk.append(_dot(ds, q, TN))
                    dqc.append(_dot(ds, k))
                dv_ref[keys, :] += jnp.concatenate(dv, axis=1)
                dk_ref[keys, :] += jnp.concatenate(dk, axis=1)
                return dq + jnp.concatenate(dqc, axis=1)

            dq = lax.fori_loop(0, i, lambda j, acc: block(j, acc, False), jnp.zeros((TA, 2 * SLAB), F32))
            dq_ref[rows, :] = block(i, dq, True)
            return 0

        lax.fori_loop(0, nq, q_block, 0)

    pair = lambda p: (0, p)
    slab2 = pl.BlockSpec((s, 2 * SLAB), pair)
    seq = pl.BlockSpec((s, LANES), pair)
    small = pl.BlockSpec((1, s, 2), lambda p: (p, 0, 0))
    o32 = jax.ShapeDtypeStruct((s, HEADS * SLAB), F32)
    return _call(
        body, (qp, kp, vp, dmixed, o, lse), name="attn_bwd", grid=(HEADS // 2,),
        in_specs=[slab2, slab2, slab2, seq, seq, small], out_specs=[slab2, slab2, slab2], out_shape=[o32, o32, o32],
        scratch_shapes=[pltpu.VMEM((2, TA, SLAB), BF16), pltpu.VMEM((2, TA, SLAB), BF16)], vmem_mb=40, hosted=hosted)


def _qkv_post(dqp, dkp, dvp, proj, gq, gk, hosted):
    s = proj.shape[0]

    def body(dq_ref, dk_ref, dv_ref, q_ref, k_ref, gq_ref, gk_ref, dqo_ref, dko_ref, dvo_ref, df_ref, vec_ref):
        @pl.when(pl.program_id(0) == 0)
        def _():
            vec_ref[...] = jnp.zeros_like(vec_ref)

        def one(d_ref, x_ref, g_ref, o_ref, row, scale):
            dg = jnp.zeros((1, DH), F32)
            for h in range(HEADS):
                sl = slice(DH * h, DH * (h + 1))
                xv = x_ref[:, sl]
                r = lax.rsqrt(jnp.mean(xv * xv, axis=-1, keepdims=True) + EPS)
                xh = xv * r
                dn = d_ref[:, SLAB * h:SLAB * h + DH] * scale
                dg = dg + jnp.sum(dn * xh, axis=0, keepdims=True)
                dxh = dn * g_ref[...]
                o_ref[:, sl] = (r * (dxh - xh * jnp.mean(dxh * xh, axis=-1, keepdims=True))).astype(BF16)
            vec_ref[row:row + 1, 0:DH] += dg

        one(dq_ref, q_ref, gq_ref, dqo_ref, 0, QK_SCALE)
        one(dk_ref, k_ref, gk_ref, dko_ref, 1, 1.0)
        lane = lax.broadcasted_iota(jnp.int32, (TR, LANES), 1)
        df = jnp.zeros((TR, LANES), F32)
        for h in range(HEADS):
            dvo_ref[:, DH * h:DH * (h + 1)] = dv_ref[:, SLAB * h:SLAB * h + DH].astype(BF16)
            row_sum = dq_ref[:, SLAB * h + DH:SLAB * h + DH + 1]
            col_sum = dk_ref[:, SLAB * h + DH + AUG_ONE:SLAB * h + DH + AUG_ONE + 1]
            df = jnp.where(lane == h, row_sum - col_sum, df)
        df_ref[...] = df

    o = jax.ShapeDtypeStruct((s, AW), BF16)
    wide = _row_spec(HEADS * SLAB)
    return _call(
        body, (dqp, dkp, dvp, proj, proj, gq, gk), name="qkv_post", grid=(s // TR,),
        in_specs=[wide, wide, wide, _row_spec(AW, 0), _row_spec(AW, 1), _full_spec((1, DH)), _full_spec((1, DH))],
        out_specs=[_row_spec(AW), _row_spec(AW), _row_spec(AW), _row_spec(LANES), _full_spec((SUB, LANES))],
        out_shape=[o, o, o, jax.ShapeDtypeStruct((s, LANES), F32), jax.ShapeDtypeStruct((SUB, LANES), F32)],
        hosted=hosted)


TF = 256
NJ = DFF // TF
FFN_ROWS_FWD = 1024
FFN_ROWS_BWD = 1024


def _ffn_fwd(h2, wup_t, cw, wd):
    s = h2.shape[0]
    tr = FFN_ROWS_FWD
    nr = s // tr

    def body(h_ref, wu_ref, cg_ref, cv_ref, wd_ref, pg_ref, pv_ref, y_ref, halo_ref, act_ref):
        r, j = pl.program_id(0), pl.program_id(1)
        hv = h_ref[...]
        pg = _dot(hv, wu_ref[0], NT).astype(BF16)
        pv = _dot(hv, wu_ref[1], NT).astype(BF16)
        pg_ref[...] = pg
        pv_ref[...] = pv
        pgf, pvf = pg.astype(F32), pv.astype(F32)
        ug, _, _ = _conv_taps(pgf, jnp.where(r > 0, halo_ref[j, 0], 0.0), cg_ref[...])
        uv, _, _ = _conv_taps(pvf, jnp.where(r > 0, halo_ref[j, 1], 0.0), cv_ref[...])
        halo_ref[j, 0] = pgf[tr - SUB:tr, :]
        halo_ref[j, 1] = pvf[tr - SUB:tr, :]
        act = (ug * _sigmoid(ug) * uv).astype(BF16)
        for t in range(NJ):
            @pl.when(j == t)
            def _(t=t):
                act_ref[:, t * TF:(t + 1) * TF] = act

        @pl.when(j == NJ - 1)
        def _():
            y_ref[...] = _dot(act_ref[...], wd_ref[...])

    pre = jax.ShapeDtypeStruct((s, DFF), BF16)
    return pl.pallas_call(
        body, name="ffn_fwd", grid=(nr, NJ),
        in_specs=[pl.BlockSpec((tr, D), lambda r, j: (r, 0)),
                  pl.BlockSpec((2, TF, D), lambda r, j: (0, j, 0)),
                  pl.BlockSpec((3, TF), lambda r, j: (0, j)),
                  pl.BlockSpec((3, TF), lambda r, j: (0, NJ + j)),
                  pl.BlockSpec((DFF, D), lambda r, j: (0, 0))],
        out_specs=(pl.BlockSpec((tr, TF), lambda r, j: (r, j)),
                   pl.BlockSpec((tr, TF), lambda r, j: (r, j)),
                   pl.BlockSpec((tr, D), lambda r, j: (r, 0))),
        out_shape=(pre, pre, jax.ShapeDtypeStruct((s, D), F32)),
        scratch_shapes=[pltpu.VMEM((NJ, 2, SUB, TF), F32), pltpu.VMEM((tr, DFF), BF16)],
        compiler_params=_params(("arbitrary", "arbitrary"), 56),
    )(h2, wup_t, cw, cw, wd)


def _ffn_bwd(dy, h2, pre_g, pre_v, wup_t, cw, wd):
    s = h2.shape[0]
    tr = FFN_ROWS_BWD
    nr = s // tr
    hb = tr // (2 * SUB)

    def body(dy_ref, h_ref, pg_ref, pv_ref, hg_ref, hv_ref, wu_ref, cg_ref, cv_ref, wd_ref,
             dh_ref, dwu_ref, dwd_ref, dcg_ref, dcv_ref, nxt_ref, awu_ref, awd_ref):
        j, r = pl.program_id(0), pl.program_id(1)
        rr = nr - 1 - r
        row0 = pl.multiple_of(rr * tr, tr)
        cwg, cwv = cg_ref[...], cv_ref[...]
        pg, pv = pg_ref[...].astype(F32), pv_ref[...].astype(F32)
        ug, g1, g2 = _conv_taps(pg, jnp.where(rr > 0, hg_ref[SUB:2 * SUB, :].astype(F32), 0.0), cwg)
        uv, v1, v2 = _conv_taps(pv, jnp.where(rr > 0, hv_ref[SUB:2 * SUB, :].astype(F32), 0.0), cwv)
        sg = _sigmoid(ug)
        sil = ug * sg
        act = (sil * uv).astype(BF16)
        dyv = dy_ref[...]
        da = _dot(dyv, wd_ref[...], NT)
        dug = da * uv * (sg * (1.0 + ug * (1.0 - sg)))
        duv = da * sil
        dpg = _conv_taps_t(dug, jnp.where(r > 0, nxt_ref[0], 0.0), cwg)
        dpv = _conv_taps_t(duv, jnp.where(r > 0, nxt_ref[1], 0.0), cwv)
        nxt_ref[0] = dug[0:SUB, :]
        nxt_ref[1] = duv[0:SUB, :]
        dpgb, dpvb = dpg.astype(BF16), dpv.astype(BF16)
        hv = h_ref[...]
        dwd = _dot(act, dyv, TN)
        dpb = jnp.concatenate([dpgb, dpvb], axis=1)
        dwu = _dot(dpb, hv, TN)
        dh = _dot(dpb, wu_ref[...].reshape(2 * TF, D))

        def taps(du, x0, x1, x2):
            return (jnp.sum(du * x2, axis=0, keepdims=True), jnp.sum(du * x1, axis=0, keepdims=True),
                    jnp.sum(du * x0, axis=0, keepdims=True))

        tg, tv = taps(dug, pg, g1, g2), taps(duv, pv, v1, v2)

        @pl.when(r == 0)
        def _():
            awd_ref[...] = dwd
            awu_ref[...] = dwu
            dcg_ref[...] = jnp.zeros_like(dcg_ref)
            dcv_ref[...] = jnp.zeros_like(dcv_ref)

        @pl.when(r > 0)
        def _():
            awd_ref[...] += dwd
            awu_ref[...] += dwu

        @pl.when(r == nr - 1)
        def _():
            dwd_ref[...] = awd_ref[...].astype(BF16)
            dwu_ref[...] = awu_ref[...].astype(BF16).reshape(2, TF, D)

        for t in range(3):
            dcg_ref[t:t + 1, :] += tg[t]
            dcv_ref[t:t + 1, :] += tv[t]

        @pl.when(j == 0)
        def _():
            dh_ref[pl.ds(row0, tr), :] = dh

        @pl.when(j > 0)
        def _():
            dh_ref[pl.ds(row0, tr), :] += dh

    rows = lambda j, r: (nr - 1 - r, 0)
    tile = lambda j, r: (nr - 1 - r, j)
    halo = lambda j, r: (jnp.maximum((nr - 1 - r) * hb - 1, 0), j)
    return pl.pallas_call(
        body, name="ffn_bwd", grid=(NJ, nr),
        in_specs=[pl.BlockSpec((tr, D), rows), pl.BlockSpec((tr, D), rows),
                  pl.BlockSpec((tr, TF), tile), pl.BlockSpec((tr, TF), tile),
                  pl.BlockSpec((2 * SUB, TF), halo), pl.BlockSpec((2 * SUB, TF), halo),
                  pl.BlockSpec((2, TF, D), lambda j, r: (0, j, 0)),
                  pl.BlockSpec((3, TF), lambda j, r: (0, j)), pl.BlockSpec((3, TF), lambda j, r: (0, NJ + j)),
                  pl.BlockSpec((TF, D), lambda j, r: (j, 0))],
        out_specs=(pl.BlockSpec((s, D), lambda j, r: (0, 0)),
                   pl.BlockSpec((2, TF, D), lambda j, r: (0, j, 0)),
                   pl.BlockSpec((TF, D), lambda j, r: (j, 0)),
                   pl.BlockSpec((SUB, TF), lambda j, r: (0, j)), pl.BlockSpec((SUB, TF), lambda j, r: (0, j))),
        out_shape=(jax.ShapeDtypeStruct((s, D), F32),
                   jax.ShapeDtypeStruct((2, DFF, D), BF16), jax.ShapeDtypeStruct((DFF, D), BF16),
                   jax.ShapeDtypeStruct((SUB, DFF), F32), jax.ShapeDtypeStruct((SUB, DFF), F32)),
        scratch_shapes=[pltpu.VMEM((2, SUB, TF), F32), pltpu.VMEM((2 * TF, D), F32), pltpu.VMEM((TF, D), F32)],
        compiler_params=_params(("arbitrary", "arbitrary"), 56),
    )(dy, h2, pre_g, pre_v, pre_g, pre_v, wup_t, cw, cw, wd)


def _adam(w, g, m, v):
    m = ADAM_B1 * m + (1.0 - ADAM_B1) * g
    v = ADAM_B2 * v + (1.0 - ADAM_B2) * (g * g)
    m_hat = m / (1.0 - ADAM_B1 ** ADAM_STEP)
    v_hat = v / (1.0 - ADAM_B2 ** ADAM_STEP)
    delta = -ADAM_LR * (m_hat / (jnp.sqrt(v_hat) + ADAM_EPS) + ADAM_WD * w)
    return delta, m, v


NCHIP = NDEV // 2


def _pair_add(mine, theirs, tr, name):
    _, _, rws, cols = mine.shape

    def body(a_ref, b_ref, o_ref):
        c = lax.axis_index("c")
        o_ref[0] = (a_ref[0, c].astype(F32) + b_ref[0].astype(F32)).astype(BF16)

    (out,), _ = _call(
        body, (mine, theirs), name=name, grid=(NCHIP, rws // tr),
        in_specs=[pl.BlockSpec((1, 2, tr, cols), lambda q, i: (q, 0, i, 0)),
                  pl.BlockSpec((1, tr, cols), lambda q, i: (q, i, 0))],
        out_specs=[pl.BlockSpec((1, tr, cols), lambda q, i: (q, i, 0))],
        out_shape=[jax.ShapeDtypeStruct((NCHIP, rws, cols), BF16)], vmem_mb=16)
    return out


def _adamw_sharded(parts, w, m, v, tr, name, hosted=None):
    rws, cols = w.shape

    def body(p_ref, w_ref, m_ref, v_ref, g_ref, d_ref, mo_ref, vo_ref):
        g = p_ref[0].astype(F32)
        for q in range(1, NCHIP):
            g = g + p_ref[q].astype(F32)
        g_ref[...] = g
        d_ref[...], mo_ref[...], vo_ref[...] = _adam(w_ref[...], g, m_ref[...], v_ref[...])

    blk = pl.BlockSpec((tr, cols), lambda i: (i, 0))
    o = jax.ShapeDtypeStruct((rws, cols), F32)
    outs, moved = _call(
        body, (parts, w, m, v), name=name, grid=(rws // tr,),
        in_specs=[pl.BlockSpec((NCHIP, tr, cols), lambda i: (0, i, 0)), blk, blk, blk],
        out_specs=[blk, blk, blk, blk], out_shape=[o, o, o, o], vmem_mb=36 if tr > 256 else 24, hosted=hosted)
    return (outs, moved) if hosted else outs


def _adamw_ada(c_all, dmod_my, w, m, v):
    rws, cols = w.shape
    tr = 256

    def body(c_ref, dm_ref, w_ref, m_ref, v_ref, g_ref, d_ref, mo_ref, vo_ref):
        cv = c_ref[...]
        act = cv * _sigmoid(cv)
        g = _dot(act, dm_ref[...], TN, lax.Precision.HIGHEST)
        g_ref[...] = g
        d_ref[...], mo_ref[...], vo_ref[...] = _adam(w_ref[...], g, m_ref[...], v_ref[...])

    blk = pl.BlockSpec((tr, cols), lambda i: (i, 0))
    o = jax.ShapeDtypeStruct((rws, cols), F32)
    return pl.pallas_call(
        body, name="adamw_ada", grid=(rws // tr,),
        in_specs=[pl.BlockSpec((NDEV, tr), lambda i: (0, i)), _full_spec((NDEV, cols)), blk, blk, blk],
        out_specs=(blk, blk, blk, blk), out_shape=(o, o, o, o),
        compiler_params=_params(("parallel",), 32),
    )(c_all, dmod_my, w, m, v)


REP_ROWS = 16
ROW_N1, ROW_N2, ROW_LOSS, ROW_MISC = 6, 7, 8, 9
LANE_BF, LANE_GQ, LANE_GK = 0, 128, 256


def _adamw_small(rep_all, conv_all, wmv):
    n_ff = wmv[6][0].shape[1]

    def body(*refs):
        rep_ref, conv_ref = refs[:2]
        ins = refs[2:2 + 24]
        outs = refs[2 + 24:]
        loss_ref, outs = outs[0], outs[1:]
        g_rep = rep_ref[0]
        g_conv = conv_ref[0]
        for d in range(1, NDEV):
            g_rep = g_rep + rep_ref[d]
            g_conv = g_conv + conv_ref[d]
        loss_ref[...] = (0.5 / D) * jnp.sum(g_rep[ROW_LOSS:ROW_LOSS + 1, :], axis=-1, keepdims=True)
        grads = [
            None,
            g_rep[ROW_N1:ROW_N1 + 1, :],
            g_rep[ROW_MISC:ROW_MISC + 1, LANE_BF:LANE_BF + HEADS],
            g_rep[ROW_MISC:ROW_MISC + 1, LANE_GQ:LANE_GQ + DH],
            g_rep[ROW_MISC:ROW_MISC + 1, LANE_GK:LANE_GK + DH],
            g_rep[ROW_N2:ROW_N2 + 1, :],
            g_conv[0:3, 0:n_ff],
            g_conv[0:3, n_ff:n_ff + DH],
        ]
        for p in range(8):
            w_ref, m_ref, v_ref = ins[3 * p:3 * p + 3]
            g_ref, d_ref, mo_ref, vo_ref = outs[4 * p:4 * p + 4]
            if p == 0:
                for nmod in range(NMOD):
                    sl = slice(D * nmod, D * (nmod + 1))
                    g = g_rep[nmod:nmod + 1, :]
                    g_ref[:, sl] = g
                    d_ref[:, sl], mo_ref[:, sl], vo_ref[:, sl] = _adam(w_ref[:, sl], g, m_ref[:, sl], v_ref[:, sl])
            else:
                g = grads[p]
                g_ref[...] = g
                d_ref[...], mo_ref[...], vo_ref[...] = _adam(w_ref[...], g, m_ref[...], v_ref[...])

    flat = [a for trio in wmv for a in trio]
    out_shape = [jax.ShapeDtypeStruct((1, 1), F32)]
    for trio in wmv:
        out_shape += [jax.ShapeDtypeStruct(trio[0].shape, F32)] * 4
    return pl.pallas_call(
        body, name="adamw_small", out_shape=tuple(out_shape),
        compiler_params=_params(None, 32),
    )(rep_all, conv_all, *flat)


FG_FIRST = 3 * AW
N_IN = DIN // NDEV


def _w_in_runs():
    runs = []
    for d in range(NDEV):
        lo, hi = N_IN * d, N_IN * (d + 1)
        for a, b, shift in ((0, FG_FIRST, 0), (FG_FIRST, FG_FIRST + HEADS, DIN - HEADS - FG_FIRST),
                            (FG_FIRST + HEADS, DIN, -HEADS)):
            a, b = max(a, lo), min(b, hi)
            if a < b:
                runs.append((d, a - lo, a + shift, b - a))
    return runs


W_IN_ROWS = 256
N_IN_PAD = 512


def _identity(n):
    return (lax.broadcasted_iota(jnp.int32, (n, n), 0) == lax.broadcasted_iota(jnp.int32, (n, n), 1)).astype(BF16)


def _assemble_w_in(g_in, hosted):
    def body(g_ref, o_ref, t_ref):
        eye = _identity(W_IN_ROWS)
        shard = None
        for d, src, dst, width in _w_in_runs():
            if d != shard:
                t_ref[:, 0:N_IN] = _dot(eye, g_ref[d], NT).astype(BF16)
                shard = d
            o_ref[:, dst:dst + width] = t_ref[:, src:src + width]
        o_ref[:, DIN:DINP] = jnp.zeros((W_IN_ROWS, DINP - DIN), o_ref.dtype)

    (out,), moved = _call(
        body, (g_in,), name="assemble_w_in", grid=(D // W_IN_ROWS,),
        in_specs=[pl.BlockSpec((NDEV, N_IN, W_IN_ROWS), lambda i: (0, 0, i))],
        out_specs=[pl.BlockSpec((W_IN_ROWS, DINP), lambda i: (i, 0))],
        out_shape=[jax.ShapeDtypeStruct((D, DINP), g_in.dtype)],
        scratch_shapes=[pltpu.VMEM((W_IN_ROWS, N_IN_PAD), BF16)], vmem_mb=16, hosted=hosted)
    return out, moved


def _scatter_dw_in(dwp):
    def body(w_ref, o_ref, t_ref):
        eye = _identity(W_IN_ROWS)
        runs = _w_in_runs()
        for i, (d, src, dst, width) in enumerate(runs):
            t_ref[:, src:src + width] = w_ref[:, dst:dst + width]
            if i + 1 == len(runs) or runs[i + 1][0] != d:
                o_ref[d // 2, d % 2] = _dot(t_ref[:, 0:N_IN], eye, TN).astype(BF16)

    (out,), _ = _call(
        body, (dwp,), name="scatter_dw_in", grid=(D // W_IN_ROWS,),
        in_specs=[pl.BlockSpec((W_IN_ROWS, DINP), lambda i: (i, 0))],
        out_specs=[pl.BlockSpec((NCHIP, 2, N_IN, W_IN_ROWS), lambda i: (0, 0, 0, i))],
        out_shape=[jax.ShapeDtypeStruct((NCHIP, 2, N_IN, D), dwp.dtype)],
        scratch_shapes=[pltpu.VMEM((W_IN_ROWS, N_IN_PAD), BF16)], vmem_mb=16)
    return out


def kernel(x, c, w_ada, b_ada, norm1_g, w_in, b_forget, q_norm_g, k_norm_g, conv_mix_w, w_out, norm2_g, w_up, ffn_conv_w, w_down, loss_target, m_w_ada, m_b_ada, m_norm1_g, m_w_in, m_b_forget, m_q_norm_g, m_k_norm_g, m_conv_mix_w, m_w_out, m_norm2_g, m_w_up, m_ffn_conv_w, m_w_down, v_w_ada, v_b_ada, v_norm1_g, v_w_in, v_b_forget, v_q_norm_g, v_k_norm_g, v_conv_mix_w, v_w_out, v_norm2_g, v_w_up, v_ffn_conv_w, v_w_down):
    me = 4 * lax.axis_index("x") + 2 * lax.axis_index("y") + lax.axis_index("c")
    xs, tgt = x[0], loss_target[0]
    n_ada = w_ada.shape[2]
    n_ff = w_up.shape[2]

    conv_w = jnp.concatenate([ffn_conv_w[0], conv_mix_w[0]], axis=1)
    conv_w = jnp.concatenate([conv_w, jnp.zeros((SUB - 3, conv_w.shape[1]), F32)], axis=0)
    c_all, conv_all, g_in = _exchange(
        [(c.reshape(SUB, D // SUB), "ag"), (conv_w, "ag"), (jnp.transpose(w_in[0]).astype(BF16), "ag2")],
        "exchange_w_in")
    g_in, w_out_b, w_up_b, w_down_b = lax.optimization_barrier(
        (g_in, w_out[0].astype(BF16), jnp.transpose(w_up[0]).astype(BF16), w_down[0].astype(BF16)))
    g_out, g_up, g_down = _sequencer_exchange(
        [(w_out_b, "ag2"), (w_up_b, "ag2"), (w_down_b, "ag2")], "gather_weights", collective_id=1)
    c_all = c_all.reshape(NDEV, D)
    cw_ffn = jnp.transpose(conv_all[:, :3, :n_ff], (1, 0, 2)).reshape(3, 2 * DFF)
    cw_mix = jnp.transpose(conv_all[:, :3, n_ff:], (1, 0, 2)).reshape(3, CW)

    b_my = lax.dynamic_slice(b_ada, (0, me * n_ada), (1, n_ada))
    mod_part = _ada_fwd(c_all, w_ada[0], b_my)
    w_in_p, (mod_rows,) = _assemble_w_in(
        g_in, [(jnp.broadcast_to(mod_part[:, None, :], (NDEV, SUB, n_ada)), "a2a")])
    mod = mod_rows[:, 0, :].reshape(NMOD, D)
    mod = jnp.concatenate([mod, jnp.zeros((SUB - NMOD, D), F32)], axis=0)

    h = _norm_mod_fwd(xs, mod, norm1_g)
    proj = _mm(h, w_in_p, "nn", F32, 1024, 640, "proj_fwd")
    bf_pad = jnp.concatenate([b_forget, jnp.zeros((1, LANES - HEADS), F32)], axis=1)
    fcum = _fgate_fwd(proj, bf_pad)
    (qp, kp, vp), _ = _qkv_prep(proj, fcum, q_norm_g, k_norm_g, None)
    attn, lse, _ = _attn_fwd(qp, kp, vp, None)
    w_out_f = g_out.reshape(D, D)
    w_up_t = g_up.reshape(2, DFF, D)
    w_down_f = g_down.reshape(DFF, D)
    conv = _mixconv_fwd(proj, cw_mix)
    mixed = jnp.concatenate([attn, conv], axis=1).astype(BF16)
    z = _mm(mixed, w_out_f, "nn", F32, 1024, 1024, "out_fwd")
    x1, h2 = _resid_norm2(xs, z, mod, norm2_g)
    pre_g, pre_v, y = _ffn_fwd(h2, w_up_t, cw_ffn, w_down_f)
    dout, dy, vec_l = _loss_head(x1, y, tgt, mod)

    dh2, dwup_t, dwd, dcw_g, dcw_v = _ffn_bwd(dy, h2, pre_g, pre_v, w_up_t, cw_ffn, w_down_f)
    dx1, dz, vec_2 = _norm_mod_bwd(dh2, x1, dout, z, mod, norm2_g, 4, 2, "norm2_bwd")
    dwout = _mm(mixed, dz, "tn", BF16, 1024, 1024, "out_bwd_w")
    s_out = dwout.reshape(NCHIP, 2, D // NDEV, D)
    s_down = dwd.reshape(NCHIP, 2, DFF // NDEV, D)
    s_up = dwup_t.reshape(NCHIP, 2, n_ff, D)
    dmixed, (t_out, t_up, t_down) = _mm(dz, w_out_f, "nt", F32, 1024, 1024, "out_bwd_x",
                                        hosted=[(s_out, "pair"), (s_up, "pair"), (s_down, "pair")])
    c_out = _pair_add(s_out, t_out, 128, "pair_add_out")
    c_up = _pair_add(s_up, t_up, 176, "pair_add_up")
    c_down = _pair_add(s_down, t_down, 176, "pair_add_down")
    dxin, dbg, dcg, dcw_mix = _mixconv_bwd(dmixed, proj, cw_mix)
    (dqp, dkp, dvp), (p_up, p_down) = _attn_bwd(qp, kp, vp, dmixed, attn, lse, [(c_up, "chips"), (c_down, "chips")])
    (dq, dk, dvb, dfcol, vec_qk), (p_out,) = _qkv_post(dqp, dkp, dvp, proj, q_norm_g, k_norm_g, [(c_out, "chips")])
    dfg, vec_bf = _fgate_bwd(dfcol, proj, bf_pad)
    dproj = jnp.concatenate([dq, dk, dvb, dxin, dbg, dcg, dfg], axis=1)
    dwin_p = _mm(h, dproj, "tn", BF16, 1024, 640, "proj_bwd_w")
    s_in = _scatter_dw_in(dwin_p)
    (t_in,) = _exchange([(s_in, "pair")], "exchange_pair_in")
    c_in = _pair_add(s_in, t_in, N_IN, "pair_add_in")
    dh, (p_in,) = _mm(dproj, w_in_p, "nt", F32, 1024, 512, "proj_bwd_x", hosted=[(c_in, "chips")], vmem_mb=36)
    grad_x, vec_1 = _norm_mod_bwd(dh, xs, dx1, None, mod, norm1_g, 1, None, "norm1_bwd")

    misc = jnp.zeros((1, D), F32)
    misc = lax.dynamic_update_slice(misc, vec_bf[0:1, :HEADS], (0, LANE_BF))
    misc = lax.dynamic_update_slice(misc, vec_qk[0:1, :DH], (0, LANE_GQ))
    misc = lax.dynamic_update_slice(misc, vec_qk[1:2, :DH], (0, LANE_GK))
    rep = jnp.concatenate([
        vec_1[0:1], vec_1[1:2], vec_2[3:4], vec_2[0:1], vec_2[1:2], vec_l[0:1],
        vec_1[2:3], vec_2[2:3], vec_l[1:2], misc, jnp.zeros((REP_ROWS - 10, D), F32)], axis=0)
    dcw_ffn = jnp.concatenate([dcw_g, dcw_v], axis=1).reshape(SUB, NDEV, n_ff)
    dcw_all = jnp.concatenate([jnp.transpose(dcw_ffn, (1, 0, 2)),
                               jnp.transpose(dcw_mix.reshape(SUB, NDEV, DH), (1, 0, 2))], axis=2)
    r_out, (rep_all, conv_parts) = _adamw_sharded(p_out, w_out[0], m_w_out[0], v_w_out[0], 128, "adamw_out",
                                                  hosted=[(rep, "ag"), (dcw_all, "a2a")])
    dmod_my = lax.dynamic_slice(rep_all[:, :NMOD, :].reshape(NDEV, NMOD * D), (0, me * n_ada), (NDEV, n_ada))
    r_ada = _adamw_ada(c_all, dmod_my, w_ada[0], m_w_ada[0], v_w_ada[0])
    r_in = _adamw_sharded(p_in, jnp.transpose(w_in[0]), jnp.transpose(m_w_in[0]), jnp.transpose(v_w_in[0]), N_IN,
                          "adamw_in")
    r_in = tuple(jnp.transpose(a) for a in r_in)
    r_up = _adamw_sharded(p_up, jnp.transpose(w_up[0]), jnp.transpose(m_w_up[0]), jnp.transpose(v_w_up[0]), 176,
                          "adamw_up")
    r_up = tuple(jnp.transpose(a) for a in r_up)
    r_down = _adamw_sharded(p_down, w_down[0], m_w_down[0], v_w_down[0], 176, "adamw_down")
    small = _adamw_small(rep_all, conv_parts, [
        [b_ada, m_b_ada, v_b_ada], [norm1_g, m_norm1_g, v_norm1_g], [b_forget, m_b_forget, v_b_forget],
        [q_norm_g, m_q_norm_g, v_q_norm_g], [k_norm_g, m_k_norm_g, v_k_norm_g], [norm2_g, m_norm2_g, v_norm2_g],
        [ffn_conv_w[0], m_ffn_conv_w[0], v_ffn_conv_w[0]], [conv_mix_w[0], m_conv_mix_w[0], v_conv_mix_w[0]]])
    loss = small[0].reshape(())
    r_bada, r_n1, r_bf, r_gq, r_gk, r_n2, r_cf, r_cm = [small[1 + 4 * p:5 + 4 * p] for p in range(8)]
    lead = lambda t: tuple(a[None] for a in t)
    per_w = [lead(r_ada), r_bada, r_n1, lead(r_in), r_bf, r_gq, r_gk, lead(r_cm), lead(r_out), r_n2,
             lead(r_up), lead(r_cf), lead(r_down)]
    outs = [loss, grad_x[None]]
    for field in range(4):
        outs += [t[field] for t in per_w]
    return tuple(outs)
```

```python
import functools

import jax
import jax.numpy as jnp
import numpy as np
from jax import lax
from jax.experimental import pallas as pl
from jax.experimental.pallas import tpu as pltpu
from jax.experimental.pallas import tpu_sc as plsc

F32 = jnp.float32
BF16 = jnp.bfloat16

NDEV = 8
D = 1024
HEADS = 8
DH = 64
AW = 512
CW = 512
DFF = 2816
DIN = 3080
DINP = 3200
NMOD = 6
EPS = 1e-6
QK_SCALE = 0.125
LANES = 128
SUB = 8

ADAM_LR = 0.001
ADAM_B1 = 0.9
ADAM_B2 = 0.999
ADAM_EPS = 1e-08
ADAM_WD = 0.01
ADAM_STEP = 10

MESH = pl.DeviceIdType.MESH
ANY = pl.BlockSpec(memory_space=pl.ANY)

NN = (((1,), (0,)), ((), ()))
NT = (((1,), (1,)), ((), ()))
TN = (((0,), (0,)), ((), ()))


def _dot(a, b, dims=NN, precision=None):
    return lax.dot_general(a, b, dims, precision=precision, preferred_element_type=F32)


def _params(sem=None, vmem_mb=None):
    kw = {}
    if sem is not None:
        kw["dimension_semantics"] = sem
    if vmem_mb is not None:
        kw["vmem_limit_bytes"] = vmem_mb * 1024 * 1024
    return pltpu.CompilerParams(**kw)


def _sigmoid(x):
    return 0.5 * jnp.tanh(0.5 * x) + 0.5


class _Exchange:
    def __init__(self, items):
        self.arrays = [pltpu.with_memory_space_constraint(a, pltpu.HBM) for a, _ in items]
        self.modes = [m for _, m in items]
        self.n = len(items)
        self.out_shape = []
        for a, m in items:
            sh = {"ag": (NDEV,) + a.shape, "ag2": (NDEV,) + a.shape, "pair": a.shape[:1] + a.shape[2:]}.get(m, a.shape)
            self.out_shape.append(jax.ShapeDtypeStruct(sh, a.dtype))
        self.scratch = [pltpu.SemaphoreType.DMA((self.n, NDEV - 1)), pltpu.SemaphoreType.DMA((self.n, NDEV - 1)),
                        pltpu.SemaphoreType.DMA((self.n,))]

    def _plan(self, srcs, outs, sems):
        send_sems, recv_sems, loc_sems = sems
        x, y, c = lax.axis_index("x"), lax.axis_index("y"), lax.axis_index("c")
        me, my_chip = 4 * x + 2 * y + c, 2 * x + y
        sib = (x, y, 1 - c)
        local, first, landed, forwards, arrivals = [], [], [], [], []

        def remote(a, k, src, dst, to):
            return pltpu.make_async_remote_copy(src_ref=src, dst_ref=dst, send_sem=send_sems.at[a, k],
                                                recv_sem=recv_sems.at[a, k], device_id=to, device_id_type=MESH)

        for a, mode in enumerate(self.modes):
            src, out = srcs[a], outs[a]
            if mode in ("ag", "a2a"):
                piece = (lambda slot, src=src: src) if mode == "ag" else (lambda slot, src=src: src.at[slot])
                local.append(pltpu.make_async_copy(piece(me), out.at[me], loc_sems.at[a]))
                for r in range(1, NDEV):
                    px = 1 - x if (r >> 2) & 1 else x
                    py = 1 - y if (r >> 1) & 1 else y
                    pc = 1 - c if r & 1 else c
                    pidx = 4 * px + 2 * py + pc
                    first.append(remote(a, r - 1, piece(pidx), out.at[me], (px, py, pc)))
                    arrivals.append(remote(a, r - 1, piece(pidx), out.at[pidx], (px, py, pc)))
            elif mode == "ag2":
                local.append(pltpu.make_async_copy(src, out.at[me], loc_sems.at[a]))
                first.append(remote(a, 0, src, out.at[me], sib))
                arrivals.append(remote(a, 0, src, out.at[me + 1 - 2 * c], sib))
                for j, (px, py) in enumerate([(1 - x, y), (x, 1 - y), (1 - x, 1 - y)]):
                    theirs = out.at[4 * px + 2 * py + c]
                    first.append(remote(a, 1 + j, src, out.at[me], (px, py, c)))
                    landed.append(remote(a, 1 + j, src, theirs, (px, py, c)))
                    forwards.append(remote(a, 4 + j, theirs, theirs, sib))
                    arrivals.append(remote(a, 4 + j, src, out.at[4 * px + 2 * py + 1 - c], sib))
            elif mode == "pair":
                for q in range(NDEV // 2):
                    first.append(remote(a, q, src.at[q, 1 - c], out.at[q], sib))
                    arrivals.append(remote(a, q, src.at[q, 1 - c], out.at[q], sib))
            else:
                assert mode == "chips", mode
                local.append(pltpu.make_async_copy(src.at[my_chip], out.at[my_chip], loc_sems.at[a]))
                for j, (px, py) in enumerate([(1 - x, y), (x, 1 - y), (1 - x, 1 - y)]):
                    q = 2 * px + py
                    first.append(remote(a, 1 + j, src.at[q], out.at[my_chip], (px, py, c)))
                    arrivals.append(remote(a, 1 + j, src.at[q], out.at[q], (px, py, c)))
        return local, first, landed, forwards, arrivals

    def start(self, srcs, outs, sems):
        local, first, _, _, _ = self._plan(srcs, outs, sems)
        for cp in local + first:
            cp.start()

    def wait(self, srcs, outs, sems):
        local, first, landed, forwards, arrivals = self._plan(srcs, outs, sems)
        for cp, fwd in zip(landed, forwards):
            cp.wait_recv()
            fwd.start()
        for cp in arrivals:
            cp.wait_recv()
        for cp in first + forwards:
            cp.wait_send()
        for cp in local:
            cp.wait()


def _exchange(items, name):
    ex = _Exchange(items)
    n = ex.n

    def body(*refs):
        srcs, outs, sems = refs[:n], refs[n:2 * n], refs[2 * n:]
        ex.start(srcs, outs, sems)
        ex.wait(srcs, outs, sems)

    return pl.pallas_call(
        body, name=name,
        out_shape=tuple(ex.out_shape),
        in_specs=[ANY] * n, out_specs=tuple([ANY] * n),
        scratch_shapes=ex.scratch,
        compiler_params=pltpu.CompilerParams(has_side_effects=True),
    )(*ex.arrays)


def _sequencer_exchange(items, name, collective_id, all_peers=False):
    ex = _Exchange(items)
    srcs = [jax.new_ref(a, memory_space=pltpu.MemorySpace.HBM) for a in ex.arrays]
    outs = [jax.empty_ref(sh, memory_space=pltpu.MemorySpace.HBM) for sh in ex.out_shape]

    @pl.kernel(mesh=plsc.ScalarSubcoreMesh(axis_name="sequencer", num_cores=1), name=name,
               scratch_types=tuple(ex.scratch), compiler_params=pltpu.CompilerParams(collective_id=collective_id))
    def launch(send_sems, recv_sems, loc_sems):
        x, y, c = lax.axis_index("x"), lax.axis_index("y"), lax.axis_index("c")
        barrier = pltpu.get_barrier_semaphore()
        peers = [(x, y, 1 - c), (1 - x, y, c), (x, 1 - y, c), (1 - x, 1 - y, c)]
        if all_peers:
            peers += [(1 - x, y, 1 - c), (x, 1 - y, 1 - c), (1 - x, 1 - y, 1 - c)]
        for peer in peers:
            pl.semaphore_signal(barrier, inc=1, device_id=peer, device_id_type=MESH)
        pl.semaphore_wait(barrier, len(peers))
        sems = (send_sems, recv_sems, loc_sems)
        ex.start(srcs, outs, sems)
        ex.wait(srcs, outs, sems)

    launch()
    return [o[...] for o in outs]


def _call(body, inputs, *, name, grid, in_specs, out_specs, out_shape, scratch_shapes=(), vmem_mb=None, hosted=None):
    out_specs, out_shape, scratch_shapes = tuple(out_specs), tuple(out_shape), list(scratch_shapes)
    if not hosted:
        res = pl.pallas_call(
            body, name=name, grid=grid, in_specs=list(in_specs), out_specs=out_specs, out_shape=out_shape,
            scratch_shapes=scratch_shapes, compiler_params=_params(("arbitrary",) * len(grid), vmem_mb),
        )(*inputs)
        return tuple(res), ()
    ex = _Exchange(hosted)
    n, n_in, n_out, n_scr = ex.n, len(inputs), len(out_shape), len(scratch_shapes)

    def hosting_body(*refs):
        ins, srcs = refs[:n_in], refs[n_in:n_in + n]
        outs, landing = refs[n_in + n:n_in + n + n_out], refs[n_in + n + n_out:n_in + 2 * n + n_out]
        scratch, sems = refs[n_in + 2 * n + n_out:n_in + 2 * n + n_out + n_scr], refs[n_in + 2 * n + n_out + n_scr:]
        first = functools.reduce(jnp.logical_and, [pl.program_id(d) == 0 for d in range(len(grid))])
        last = functools.reduce(jnp.logical_and, [pl.program_id(d) == grid[d] - 1 for d in range(len(grid))])

        @pl.when(first)
        def _():
            ex.start(srcs, landing, sems)

        body(*ins, *outs, *scratch)

        @pl.when(last)
        def _():
            ex.wait(srcs, landing, sems)

    res = pl.pallas_call(
        hosting_body, name=name, grid=grid,
        in_specs=list(in_specs) + [ANY] * n, out_specs=out_specs + tuple([ANY] * n),
        out_shape=out_shape + tuple(ex.out_shape), scratch_shapes=scratch_shapes + ex.scratch,
        compiler_params=_params(("arbitrary",) * len(grid), vmem_mb),
    )(*inputs, *ex.arrays)
    return tuple(res[:n_out]), tuple(res[n_out:])


def _mm(a, b, mode, out_dtype, tm, tn, name, hosted=None, vmem_mb=24):
    if mode == "nn":
        (m, k), n = a.shape, b.shape[1]
        a_spec = pl.BlockSpec((tm, k), lambda i, j: (i, 0))
        b_spec = pl.BlockSpec((k, tn), lambda i, j: (0, j))
        dims = NN
    elif mode == "nt":
        (m, k), n = a.shape, b.shape[0]
        a_spec = pl.BlockSpec((tm, k), lambda i, j: (i, 0))
        b_spec = pl.BlockSpec((tn, k), lambda i, j: (j, 0))
        dims = NT
    else:
        (k, m), n = a.shape, b.shape[1]
        a_spec = pl.BlockSpec((k, tm), lambda i, j: (0, i))
        b_spec = pl.BlockSpec((k, tn), lambda i, j: (0, j))
        dims = TN
    assert m % tm == 0 and n % tn == 0, (m, n, tm, tn)

    def body(a_ref, b_ref, o_ref):
        o_ref[...] = _dot(a_ref[...], b_ref[...], dims).astype(o_ref.dtype)

    (out,), moved = _call(
        body, (a, b), name=name, grid=(m // tm, n // tn),
        in_specs=[a_spec, b_spec], out_specs=[pl.BlockSpec((tm, tn), lambda i, j: (i, j))],
        out_shape=[jax.ShapeDtypeStruct((m, n), out_dtype)], vmem_mb=vmem_mb, hosted=hosted)
    return (out, moved) if hosted else out


def _shift_down(x, k, fill):
    y = pltpu.roll(x, k, 0)
    row = lax.broadcasted_iota(jnp.int32, (SUB, x.shape[1]), 0)
    head = y[0:SUB, :]
    for t in range(k):
        head = jnp.where(row == t, fill[t], head)
    return jnp.concatenate([head, y[SUB:, :]], axis=0)


def _shift_up(x, k, fill):
    n = x.shape[0]
    y = pltpu.roll(x, n - k, 0)
    row = lax.broadcasted_iota(jnp.int32, (SUB, x.shape[1]), 0)
    tail = y[n - SUB:, :]
    for t in range(k):
        tail = jnp.where(row == SUB - k + t, fill[t], tail)
    return jnp.concatenate([y[:n - SUB, :], tail], axis=0)


def _conv_taps(x, halo, w):
    if halo is None:
        f1, f2 = [0.0], [0.0, 0.0]
    else:
        f1, f2 = [halo[7:8, :]], [halo[6:7, :], halo[7:8, :]]
    s1 = _shift_down(x, 1, f1)
    s2 = _shift_down(x, 2, f2)
    u = w[2:3, :] * x + w[1:2, :] * s1 + w[0:1, :] * s2
    return u, s1, s2


def _conv_taps_t(du, nxt, w):
    if nxt is None:
        f1, f2 = [0.0], [0.0, 0.0]
    else:
        f1, f2 = [nxt[0:1, :]], [nxt[0:1, :], nxt[1:2, :]]
    return w[2:3, :] * du + w[1:2, :] * _shift_up(du, 1, f1) + w[0:1, :] * _shift_up(du, 2, f2)


def _ada_fwd(c_all, w_ada, b_my):
    def body(c_ref, w_ref, b_ref, o_ref):
        cv = c_ref[...]
        act = cv * _sigmoid(cv)
        o_ref[...] = _dot(act, w_ref[...], NN, lax.Precision.HIGHEST) + b_ref[...]

    return pl.pallas_call(
        body, name="ada_fwd",
        out_shape=jax.ShapeDtypeStruct((NDEV, w_ada.shape[1]), F32),
        compiler_params=_params(None, 32),
    )(c_all, w_ada, b_my)


TR = 256
TRE = 512


def _row_spec(width, col=0, rows=TR):
    return pl.BlockSpec((rows, width), lambda i, col=col: (i, col))


def _erow(width):
    return _row_spec(width, rows=TRE)


def _full_spec(shape):
    return pl.BlockSpec(shape, lambda i: (0,) * len(shape))


def _norm_mod_fwd(x, mod, g):
    s = x.shape[0]

    def body(x_ref, mod_ref, g_ref, h_ref):
        xv = x_ref[...]
        r = lax.rsqrt(jnp.mean(xv * xv, axis=-1, keepdims=True) + EPS)
        nrm = xv * r * g_ref[...]
        h_ref[...] = (nrm * (1.0 + mod_ref[1:2, :]) + mod_ref[0:1, :]).astype(BF16)

    return pl.pallas_call(
        body, name="norm1_fwd", grid=(s // TRE,),
        in_specs=[_erow(D), _full_spec((SUB, D)), _full_spec((1, D))],
        out_specs=_erow(D), out_shape=jax.ShapeDtypeStruct((s, D), BF16),
        compiler_params=_params(("parallel",), 16),
    )(x, mod, g)


SLAB = 2 * DH
AUG_F, AUG_ONE, AUG_LSE = 0, 3, 6


def _split3(x):
    hi = x.astype(BF16).astype(F32)
    r1 = x - hi
    mid = r1.astype(BF16).astype(F32)
    return hi, mid, r1 - mid


def _lanes3(lane, first, pieces, other):
    out = other
    for k in range(3):
        out = jnp.where(lane == first + k, pieces[k], out)
    return out


def _aug_placement():
    eq = np.zeros((3 * LANES, HEADS * SLAB), np.float32)
    ek = np.zeros((3 * LANES, HEADS * SLAB), np.float32)
    ones = np.zeros((SUB, HEADS * SLAB), np.float32)
    for h in range(HEADS):
        aug = SLAB * h + DH
        for k in range(3):
            eq[LANES * k + h, aug + AUG_F + k] = 1.0
            ek[LANES * k + h, aug + AUG_ONE + k] = -1.0
            ones[0, aug + AUG_ONE + k] = 1.0
            ones[1, aug + AUG_F + k] = ones[1, aug + AUG_LSE + k] = 1.0
            ones[2, aug + k] = 1.0
    return jnp.asarray(eq, BF16), jnp.asarray(ek, BF16), jnp.asarray(ones)


def _qkv_prep(proj, fcum, gq, gk, hosted):
    s = proj.shape[0]

    def body(q_ref, k_ref, v_ref, f_ref, gq_ref, gk_ref, eq_ref, ek_ref, ones_ref, qo_ref, ko_ref, vo_ref):
        f3 = jnp.concatenate(_split3(f_ref[...]), axis=1).astype(BF16)
        qo_ref[...] = (_dot(f3, eq_ref[...]) + ones_ref[0:1, :]).astype(BF16)
        ko_ref[...] = (_dot(f3, ek_ref[...]) + ones_ref[1:2, :]).astype(BF16)
        vo_ref[...] = jnp.broadcast_to(ones_ref[2:3, :], vo_ref.shape).astype(BF16)
        for h in range(HEADS):
            sl = slice(DH * h, DH * (h + 1))
            lo = slice(SLAB * h, SLAB * h + DH)
            qh = q_ref[:, sl]
            r = lax.rsqrt(jnp.mean(qh * qh, axis=-1, keepdims=True) + EPS)
            qo_ref[:, lo] = (qh * r * gq_ref[...] * QK_SCALE).astype(BF16)
            kh = k_ref[:, sl]
            r = lax.rsqrt(jnp.mean(kh * kh, axis=-1, keepdims=True) + EPS)
            ko_ref[:, lo] = (kh * r * gk_ref[...]).astype(BF16)
            vo_ref[:, lo] = v_ref[:, sl].astype(BF16)

    eq, ek, ones = _aug_placement()
    o = jax.ShapeDtypeStruct((s, HEADS * SLAB), BF16)
    wide = _row_spec(HEADS * SLAB)
    return _call(
        body, (proj, proj, proj, fcum, gq, gk, eq, ek, ones), name="qkv_prep", grid=(s // TR,),
        in_specs=[_row_spec(AW, 0), _row_spec(AW, 1), _row_spec(AW, 2), _row_spec(LANES),
                  _full_spec((1, DH)), _full_spec((1, DH)), _full_spec(eq.shape), _full_spec(ek.shape),
                  _full_spec(ones.shape)],
        out_specs=[wide, wide, wide], out_shape=[o, o, o], vmem_mb=16, hosted=hosted)


FG_BLOCK = (3 * AW + 3 * CW) // LANES


def _fgate_fwd(proj, bf_pad):
    s = proj.shape[0]

    def body(fg_ref, b_ref, o_ref, carry_ref):
        i = pl.program_id(0)

        @pl.when(i == 0)
        def _():
            carry_ref[...] = jnp.zeros_like(carry_ref)

        z = fg_ref[...] + b_ref[...]
        logf = jnp.minimum(z, 0.0) - jnp.log1p(jnp.exp(-jnp.abs(z)))
        row = lax.broadcasted_iota(jnp.int32, (TR, TR), 0)
        col = lax.broadcasted_iota(jnp.int32, (TR, TR), 1)
        tri = (col <= row).astype(F32)
        cs = _dot(tri, logf, NN, lax.Precision.HIGHEST) + carry_ref[0:1, :]
        o_ref[...] = cs
        carry_ref[...] = jnp.broadcast_to(cs[TR - 1:TR, :], carry_ref.shape)

    return pl.pallas_call(
        body, name="fgate_fwd", grid=(s // TR,),
        in_specs=[_row_spec(LANES, FG_BLOCK), _full_spec((1, LANES))],
        out_specs=_row_spec(LANES), out_shape=jax.ShapeDtypeStruct((s, LANES), F32),
        scratch_shapes=[pltpu.VMEM((SUB, LANES), F32)],
        compiler_params=_params(("arbitrary",)),
    )(proj, bf_pad)


def _fgate_bwd(dfcol, proj, bf_pad):
    s = proj.shape[0]
    nb = s // TR

    def body(df_ref, fg_ref, b_ref, o_ref, db_ref, carry_ref):
        i = pl.program_id(0)

        @pl.when(i == 0)
        def _():
            carry_ref[...] = jnp.zeros_like(carry_ref)
            db_ref[...] = jnp.zeros_like(db_ref)

        row = lax.broadcasted_iota(jnp.int32, (TR, TR), 0)
        col = lax.broadcasted_iota(jnp.int32, (TR, TR), 1)
        tri = (col >= row).astype(F32)
        dlogf = _dot(tri, df_ref[...], NN, lax.Precision.HIGHEST) + carry_ref[0:1, :]
        carry_ref[...] = jnp.broadcast_to(dlogf[0:1, :], carry_ref.shape)
        z = fg_ref[...] + b_ref[...]
        dfg = dlogf * _sigmoid(-z)
        o_ref[...] = dfg.astype(BF16)
        db_ref[0:1, :] += jnp.sum(dfg, axis=0, keepdims=True)

    rev = lambda col: pl.BlockSpec((TR, LANES), lambda i, col=col: (nb - 1 - i, col))
    return pl.pallas_call(
        body, name="fgate_bwd", grid=(nb,),
        in_specs=[rev(0), rev(FG_BLOCK), _full_spec((1, LANES))],
        out_specs=(rev(0), _full_spec((SUB, LANES))),
        out_shape=(jax.ShapeDtypeStruct((s, LANES), BF16), jax.ShapeDtypeStruct((SUB, LANES), F32)),
        scratch_shapes=[pltpu.VMEM((SUB, LANES), F32)],
        compiler_params=_params(("arbitrary",)),
    )(dfcol, proj, bf_pad)


def _resid_norm2(x, z, mod, g):
    s = x.shape[0]

    def body(x_ref, z_ref, mod_ref, g_ref, x1_ref, h_ref):
        x1 = x_ref[...] + mod_ref[2:3, :] * z_ref[...]
        x1_ref[...] = x1
        r = lax.rsqrt(jnp.mean(x1 * x1, axis=-1, keepdims=True) + EPS)
        nrm = x1 * r * g_ref[...]
        h_ref[...] = (nrm * (1.0 + mod_ref[4:5, :]) + mod_ref[3:4, :]).astype(BF16)

    return pl.pallas_call(
        body, name="resid_norm2", grid=(s // TRE,),
        in_specs=[_erow(D), _erow(D), _full_spec((SUB, D)), _full_spec((1, D))],
        out_specs=(_erow(D), _erow(D)),
        out_shape=(jax.ShapeDtypeStruct((s, D), F32), jax.ShapeDtypeStruct((s, D), BF16)),
        compiler_params=_params(("parallel",), 24),
    )(x, z, mod, g)


def _loss_head(x1, y, tgt, mod):
    s = x1.shape[0]

    def body(x1_ref, y_ref, t_ref, mod_ref, dout_ref, dy_ref, vec_ref):
        @pl.when(pl.program_id(0) == 0)
        def _():
            vec_ref[...] = jnp.zeros_like(vec_ref)

        yv = y_ref[...]
        g2 = mod_ref[5:6, :]
        diff = x1_ref[...] + g2 * yv - t_ref[...]
        dout = diff * (1.0 / D)
        dout_ref[...] = dout
        dy_ref[...] = (g2 * dout).astype(BF16)
        vec_ref[0:1, :] += jnp.sum(dout * yv, axis=0, keepdims=True)
        vec_ref[1:2, :] += jnp.sum(diff * diff, axis=0, keepdims=True)

    return pl.pallas_call(
        body, name="loss_head", grid=(s // TRE,),
        in_specs=[_erow(D), _erow(D), _erow(D), _full_spec((SUB, D))],
        out_specs=(_erow(D), _erow(D), _full_spec((SUB, D))),
        out_shape=(jax.ShapeDtypeStruct((s, D), F32), jax.ShapeDtypeStruct((s, D), BF16),
                   jax.ShapeDtypeStruct((SUB, D), F32)),
        compiler_params=_params(("arbitrary",), 24),
    )(x1, y, tgt, mod)


def _norm_mod_bwd(dh, xin, dres, zin, mod, g, scale_row, gate_row, name):
    s = dh.shape[0]
    with_gate = gate_row is not None

    def body(*refs):
        if with_gate:
            dh_ref, x_ref, dres_ref, z_ref, mod_ref, g_ref, dx_ref, dz_ref, vec_ref = refs
        else:
            dh_ref, x_ref, dres_ref, mod_ref, g_ref, dx_ref, vec_ref = refs

        @pl.when(pl.program_id(0) == 0)
        def _():
            vec_ref[...] = jnp.zeros_like(vec_ref)

        xv = x_ref[...]
        dhv = dh_ref[...]
        gv = g_ref[...]
        r = lax.rsqrt(jnp.mean(xv * xv, axis=-1, keepdims=True) + EPS)
        xh = xv * r
        dn = dhv * (1.0 + mod_ref[scale_row:scale_row + 1, :])
        dxh = dn * gv
        dx = dres_ref[...] + r * (dxh - xh * jnp.mean(dxh * xh, axis=-1, keepdims=True))
        dx_ref[...] = dx
        vec_ref[0:1, :] += jnp.sum(dhv, axis=0, keepdims=True)
        vec_ref[1:2, :] += jnp.sum(dhv * (xh * gv), axis=0, keepdims=True)
        vec_ref[2:3, :] += jnp.sum(dn * xh, axis=0, keepdims=True)
        if with_gate:
            dz_ref[...] = (mod_ref[gate_row:gate_row + 1, :] * dx).astype(BF16)
            vec_ref[3:4, :] += jnp.sum(dx * z_ref[...], axis=0, keepdims=True)

    ins = [dh, xin, dres] + ([zin] if with_gate else []) + [mod, g]
    in_specs = [_erow(D)] * (4 if with_gate else 3) + [_full_spec((SUB, D)), _full_spec((1, D))]
    out_specs = [_erow(D)] + ([_erow(D)] if with_gate else []) + [_full_spec((SUB, D))]
    out_shape = [jax.ShapeDtypeStruct((s, D), F32)] + ([jax.ShapeDtypeStruct((s, D), BF16)] if with_gate else []) \
        + [jax.ShapeDtypeStruct((SUB, D), F32)]
    outs, _ = _call(body, ins, name=name, grid=(s // TRE,), in_specs=in_specs, out_specs=out_specs,
                    out_shape=out_shape, vmem_mb=32)
    return outs


XIN_BLOCK = 3 * AW // LANES
BG_BLOCK = XIN_BLOCK + CW // LANES
CG_BLOCK = BG_BLOCK + CW // LANES


def _seq_spec(s, first_block):
    return pl.BlockSpec((s, LANES), lambda j, fb=first_block: (0, fb + j))


def _mixconv_fwd(proj, w):
    s = proj.shape[0]

    def body(xin_ref, bg_ref, cg_ref, w_ref, o_ref):
        cx = cg_ref[...] * xin_ref[...]
        cv, _, _ = _conv_taps(cx, None, w_ref[...])
        o_ref[...] = bg_ref[...] * cv

    return pl.pallas_call(
        body, name="mixconv_fwd", grid=(CW // LANES,),
        in_specs=[_seq_spec(s, XIN_BLOCK), _seq_spec(s, BG_BLOCK), _seq_spec(s, CG_BLOCK),
                  pl.BlockSpec((3, LANES), lambda j: (0, j))],
        out_specs=_seq_spec(s, 0), out_shape=jax.ShapeDtypeStruct((s, CW), F32),
        compiler_params=_params(("parallel",), 32),
    )(proj, proj, proj, w)


def _mixconv_bwd(dmixed, proj, w):
    s = proj.shape[0]

    def body(d_ref, xin_ref, bg_ref, cg_ref, w_ref, dxin_ref, dbg_ref, dcg_ref, dw_ref):
        wv = w_ref[...]
        xin, cg, dconv = xin_ref[...], cg_ref[...], d_ref[...]
        cx = cg * xin
        cv, s1, s2 = _conv_taps(cx, None, wv)
        dbg_ref[...] = (dconv * cv).astype(BF16)
        dcv = dconv * bg_ref[...]
        dw_ref[...] = jnp.zeros_like(dw_ref)
        dw_ref[0:1, :] = jnp.sum(dcv * s2, axis=0, keepdims=True)
        dw_ref[1:2, :] = jnp.sum(dcv * s1, axis=0, keepdims=True)
        dw_ref[2:3, :] = jnp.sum(dcv * cx, axis=0, keepdims=True)
        dcx = _conv_taps_t(dcv, None, wv)
        dcg_ref[...] = (dcx * xin).astype(BF16)
        dxin_ref[...] = (dcx * cg).astype(BF16)

    o = jax.ShapeDtypeStruct((s, CW), BF16)
    return pl.pallas_call(
        body, name="mixconv_bwd", grid=(CW // LANES,),
        in_specs=[_seq_spec(s, AW // LANES), _seq_spec(s, XIN_BLOCK), _seq_spec(s, BG_BLOCK), _seq_spec(s, CG_BLOCK),
                  pl.BlockSpec((3, LANES), lambda j: (0, j))],
        out_specs=(_seq_spec(s, 0), _seq_spec(s, 0), _seq_spec(s, 0), pl.BlockSpec((SUB, LANES), lambda j: (0, j))),
        out_shape=(o, o, o, jax.ShapeDtypeStruct((SUB, CW), F32)),
        compiler_params=_params(("parallel",), 32),
    )(dmixed, proj, proj, proj, w)


TA = 512
NEG = -1e30


def _causal_mask():
    row = lax.broadcasted_iota(jnp.int32, (TA, TA), 0)
    col = lax.broadcasted_iota(jnp.int32, (TA, TA), 1)
    return col <= row


def _attn_fwd(qp, kp, vp, hosted):
    s = qp.shape[0]
    nq = s // TA

    def body(q_ref, k_ref, v_ref, o_ref, lse_ref):
        i = pl.program_id(1)
        slabs = [slice(SLAB * hh, SLAB * (hh + 1)) for hh in range(2)]
        q = [q_ref[:, sl] for sl in slabs]

        def block(j, carry, masked):
            keys = pl.ds(pl.multiple_of(j * TA, TA), TA)
            ms, acc = carry
            m_out, parts = [], []
            for hh in range(2):
                sc = _dot(q[hh], k_ref[keys, slabs[hh]], NT)
                if masked:
                    sc = jnp.where(_causal_mask(), sc, NEG)
                m_new = jnp.maximum(ms[hh], jnp.max(sc, axis=-1, keepdims=True))
                p = jnp.exp(sc - m_new)
                parts.append(jnp.exp(ms[hh] - m_new) * acc[:, slabs[hh]] + _dot(p.astype(BF16), v_ref[keys, slabs[hh]]))
                m_out.append(m_new)
            return tuple(m_out), jnp.concatenate(parts, axis=1)

        init = ((jnp.full((TA, 1), NEG, F32), jnp.full((TA, 1), NEG, F32)), jnp.zeros((TA, 2 * SLAB), F32))
        carry = lax.fori_loop(0, i, lambda j, cr: block(j, cr, False), init)
        ms, acc = block(i, carry, True)
        for hh in range(2):
            l = acc[:, SLAB * hh + DH:SLAB * hh + DH + 1]
            o_ref[:, DH * hh:DH * (hh + 1)] = acc[:, SLAB * hh:SLAB * hh + DH] / l
            lse_ref[0, :, hh:hh + 1] = ms[hh] + jnp.log(l)

    (o, lse), moved = _call(
        body, (qp, kp, vp), name="attn_fwd", grid=(HEADS // 2, nq),
        in_specs=[pl.BlockSpec((TA, 2 * SLAB), lambda p, i: (i, p)),
                  pl.BlockSpec((s, 2 * SLAB), lambda p, i: (0, p)),
                  pl.BlockSpec((s, 2 * SLAB), lambda p, i: (0, p))],
        out_specs=[pl.BlockSpec((TA, LANES), lambda p, i: (i, p)), pl.BlockSpec((1, TA, 2), lambda p, i: (p, i, 0))],
        out_shape=[jax.ShapeDtypeStruct((s, AW), F32), jax.ShapeDtypeStruct((HEADS // 2, s, 2), F32)],
        vmem_mb=24, hosted=hosted)
    return o, lse, moved


def _attn_bwd(qp, kp, vp, dmixed, o, lse, hosted):
    s = qp.shape[0]
    nq = s // TA

    def body(q_ref, k_ref, v_ref, do_ref, o_ref, lse_ref, dq_ref, dk_ref, dv_ref, qb_ref, dob_ref):
        dk_ref[...] = jnp.zeros_like(dk_ref)
        dv_ref[...] = jnp.zeros_like(dv_ref)
        slabs = [slice(SLAB * hh, SLAB * (hh + 1)) for hh in range(2)]
        lane = lax.broadcasted_iota(jnp.int32, (TA, DH), 1)

        def q_block(i, _):
            i0 = pl.multiple_of(i * TA, TA)
            rows = pl.ds(i0, TA)
            for hh in range(2):
                half = slice(DH * hh, DH * (hh + 1))
                do = do_ref[rows, half]
                delta = jnp.sum(do * o_ref[rows, half], axis=-1, keepdims=True)
                dob_ref[hh, :, 0:DH] = do.astype(BF16)
                dob_ref[hh, :, DH:SLAB] = _lanes3(lane, 0, [-d for d in _split3(delta)], 0.0).astype(BF16)
                lse3 = _split3(lse_ref[0, rows, hh:hh + 1])
                qb_ref[hh, :, 0:DH] = q_ref[rows, SLAB * hh:SLAB * hh + DH]
                aug = q_ref[rows, SLAB * hh + DH:SLAB * (hh + 1)].astype(F32)
                qb_ref[hh, :, DH:SLAB] = _lanes3(lane, AUG_LSE, [-x for x in lse3], aug).astype(BF16)

            def block(j, dq, masked):
                keys = pl.ds(pl.multiple_of(j * TA, TA), TA)
                dv, dk, dqc = [], [], []
                for hh in range(2):
                    q, dob = qb_ref[hh], dob_ref[hh]
                    k = k_ref[keys, slabs[hh]]
                    sc = _dot(q, k, NT)
                    if masked:
                        sc = jnp.where(_causal_mask(), sc, NEG)
                    p = jnp.exp(sc)
                    dv.append(_dot(p.astype(BF16), dob, TN))
                    ds = (p * _dot(dob, v_ref[keys, slabs[hh]], NT)).astype(BF16)
                    dk.append(_dot(ds, q, TN))
                    dqc.append(_dot(ds, k))
                dv_ref[keys, :] += jnp.concatenate(dv, axis=1)
                dk_ref[keys, :] += jnp.concatenate(dk, axis=1)
                return dq + jnp.concatenate(dqc, axis=1)

            dq = lax.fori_loop(0, i, lambda j, acc: block(j, acc, False), jnp.zeros((TA, 2 * SLAB), F32))
            dq_ref[rows, :] = block(i, dq, True)
            return 0

        lax.fori_loop(0, nq, q_block, 0)

    pair = lambda p: (0, p)
    slab2 = pl.BlockSpec((s, 2 * SLAB), pair)
    seq = pl.BlockSpec((s, LANES), pair)
    small = pl.BlockSpec((1, s, 2), lambda p: (p, 0, 0))
    o32 = jax.ShapeDtypeStruct((s, HEADS * SLAB), F32)
    return _call(
        body, (qp, kp, vp, dmixed, o, lse), name="attn_bwd", grid=(HEADS // 2,),
        in_specs=[slab2, slab2, slab2, seq, seq, small], out_specs=[slab2, slab2, slab2], out_shape=[o32, o32, o32],
        scratch_shapes=[pltpu.VMEM((2, TA, SLAB), BF16), pltpu.VMEM((2, TA, SLAB), BF16)], vmem_mb=40, hosted=hosted)


def _qkv_post(dqp, dkp, dvp, proj, gq, gk, hosted):
    s = proj.shape[0]

    def body(dq_ref, dk_ref, dv_ref, q_ref, k_ref, gq_ref, gk_ref, dqo_ref, dko_ref, dvo_ref, df_ref, vec_ref):
        @pl.when(pl.program_id(0) == 0)
        def _():
            vec_ref[...] = jnp.zeros_like(vec_ref)

        def one(d_ref, x_ref, g_ref, o_ref, row, scale):
            dg = jnp.zeros((1, DH), F32)
            for h in range(HEADS):
                sl = slice(DH * h, DH * (h + 1))
                xv = x_ref[:, sl]
                r = lax.rsqrt(jnp.mean(xv * xv, axis=-1, keepdims=True) + EPS)
                xh = xv * r
                dn = d_ref[:, SLAB * h:SLAB * h + DH] * scale
                dg = dg + jnp.sum(dn * xh, axis=0, keepdims=True)
                dxh = dn * g_ref[...]
                o_ref[:, sl] = (r * (dxh - xh * jnp.mean(dxh * xh, axis=-1, keepdims=True))).astype(BF16)
            vec_ref[row:row + 1, 0:DH] += dg

        one(dq_ref, q_ref, gq_ref, dqo_ref, 0, QK_SCALE)
        one(dk_ref, k_ref, gk_ref, dko_ref, 1, 1.0)
        lane = lax.broadcasted_iota(jnp.int32, (TR, LANES), 1)
        df = jnp.zeros((TR, LANES), F32)
        for h in range(HEADS):
            dvo_ref[:, DH * h:DH * (h + 1)] = dv_ref[:, SLAB * h:SLAB * h + DH].astype(BF16)
            row_sum = dq_ref[:, SLAB * h + DH:SLAB * h + DH + 1]
            col_sum = dk_ref[:, SLAB * h + DH + AUG_ONE:SLAB * h + DH + AUG_ONE + 1]
            df = jnp.where(lane == h, row_sum - col_sum, df)
        df_ref[...] = df

    o = jax.ShapeDtypeStruct((s, AW), BF16)
    wide = _row_spec(HEADS * SLAB)
    return _call(
        body, (dqp, dkp, dvp, proj, proj, gq, gk), name="qkv_post", grid=(s // TR,),
        in_specs=[wide, wide, wide, _row_spec(AW, 0), _row_spec(AW, 1), _full_spec((1, DH)), _full_spec((1, DH))],
        out_specs=[_row_spec(AW), _row_spec(AW), _row_spec(AW), _row_spec(LANES), _full_spec((SUB, LANES))],
        out_shape=[o, o, o, jax.ShapeDtypeStruct((s, LANES), F32), jax.ShapeDtypeStruct((SUB, LANES), F32)],
        hosted=hosted)


TF = 256
NJ = DFF // TF
FFN_ROWS_FWD = 1024
FFN_ROWS_BWD = 1024


def _ffn_fwd(h2, wup_t, cw, wd):
    s = h2.shape[0]
    tr = FFN_ROWS_FWD
    nr = s // tr

    def body(h_ref, wu_ref, cg_ref, cv_ref, wd_ref, pg_ref, pv_ref, y_ref, halo_ref, act_ref):
        r, j = pl.program_id(0), pl.program_id(1)
        hv = h_ref[...]
        pg = _dot(hv, wu_ref[0], NT).astype(BF16)
        pv = _dot(hv, wu_ref[1], NT).astype(BF16)
        pg_ref[...] = pg
        pv_ref[...] = pv
        pgf, pvf = pg.astype(F32), pv.astype(F32)
        ug, _, _ = _conv_taps(pgf, jnp.where(r > 0, halo_ref[j, 0], 0.0), cg_ref[...])
        uv, _, _ = _conv_taps(pvf, jnp.where(r > 0, halo_ref[j, 1], 0.0), cv_ref[...])
        halo_ref[j, 0] = pgf[tr - SUB:tr, :]
        halo_ref[j, 1] = pvf[tr - SUB:tr, :]
        act = (ug * _sigmoid(ug) * uv).astype(BF16)
        for t in range(NJ):
            @pl.when(j == t)
            def _(t=t):
                act_ref[:, t * TF:(t + 1) * TF] = act

        @pl.when(j == NJ - 1)
        def _():
            y_ref[...] = _dot(act_ref[...], wd_ref[...])

    pre = jax.ShapeDtypeStruct((s, DFF), BF16)
    return pl.pallas_call(
        body, name="ffn_fwd", grid=(nr, NJ),
        in_specs=[pl.BlockSpec((tr, D), lambda r, j: (r, 0)),
                  pl.BlockSpec((2, TF, D), lambda r, j: (0, j, 0)),
                  pl.BlockSpec((3, TF), lambda r, j: (0, j)),
                  pl.BlockSpec((3, TF), lambda r, j: (0, NJ + j)),
                  pl.BlockSpec((DFF, D), lambda r, j: (0, 0))],
        out_specs=(pl.BlockSpec((tr, TF), lambda r, j: (r, j)),
                   pl.BlockSpec((tr, TF), lambda r, j: (r, j)),
                   pl.BlockSpec((tr, D), lambda r, j: (r, 0))),
        out_shape=(pre, pre, jax.ShapeDtypeStruct((s, D), F32)),
        scratch_shapes=[pltpu.VMEM((NJ, 2, SUB, TF), F32), pltpu.VMEM((tr, DFF), BF16)],
        compiler_params=_params(("arbitrary", "arbitrary"), 56),
    )(h2, wup_t, cw, cw, wd)


def _ffn_bwd(dy, h2, pre_g, pre_v, wup_t, cw, wd):
    s = h2.shape[0]
    tr = FFN_ROWS_BWD
    nr = s // tr
    hb = tr // (2 * SUB)

    def body(dy_ref, h_ref, pg_ref, pv_ref, hg_ref, hv_ref, wu_ref, cg_ref, cv_ref, wd_ref,
             dh_ref, dwu_ref, dwd_ref, dcg_ref, dcv_ref, nxt_ref, awu_ref, awd_ref):
        j, r = pl.program_id(0), pl.program_id(1)
        rr = nr - 1 - r
        row0 = pl.multiple_of(rr * tr, tr)
        cwg, cwv = cg_ref[...], cv_ref[...]
        pg, pv = pg_ref[...].astype(F32), pv_ref[...].astype(F32)
        ug, g1, g2 = _conv_taps(pg, jnp.where(rr > 0, hg_ref[SUB:2 * SUB, :].astype(F32), 0.0), cwg)
        uv, v1, v2 = _conv_taps(pv, jnp.where(rr > 0, hv_ref[SUB:2 * SUB, :].astype(F32), 0.0), cwv)
        sg = _sigmoid(ug)
        sil = ug * sg
        act = (sil * uv).astype(BF16)
        dyv = dy_ref[...]
        da = _dot(dyv, wd_ref[...], NT)
        dug = da * uv * (sg * (1.0 + ug * (1.0 - sg)))
        duv = da * sil
        dpg = _conv_taps_t(dug, jnp.where(r > 0, nxt_ref[0], 0.0), cwg)
        dpv = _conv_taps_t(duv, jnp.where(r > 0, nxt_ref[1], 0.0), cwv)
        nxt_ref[0] = dug[0:SUB, :]
        nxt_ref[1] = duv[0:SUB, :]
        dpgb, dpvb = dpg.astype(BF16), dpv.astype(BF16)
        hv = h_ref[...]
        dwd = _dot(act, dyv, TN)
        dpb = jnp.concatenate([dpgb, dpvb], axis=1)
        dwu = _dot(dpb, hv, TN)
        dh = _dot(dpb, wu_ref[...].reshape(2 * TF, D))

        def taps(du, x0, x1, x2):
            return (jnp.sum(du * x2, axis=0, keepdims=True), jnp.sum(du * x1, axis=0, keepdims=True),
                    jnp.sum(du * x0, axis=0, keepdims=True))

        tg, tv = taps(dug, pg, g1, g2), taps(duv, pv, v1, v2)

        @pl.when(r == 0)
        def _():
            awd_ref[...] = dwd
            awu_ref[...] = dwu
            dcg_ref[...] = jnp.zeros_like(dcg_ref)
            dcv_ref[...] = jnp.zeros_like(dcv_ref)

        @pl.when(r > 0)
        def _():
            awd_ref[...] += dwd
            awu_ref[...] += dwu

        @pl.when(r == nr - 1)
        def _():
            dwd_ref[...] = awd_ref[...].astype(BF16)
            dwu_ref[...] = awu_ref[...].astype(BF16).reshape(2, TF, D)

        for t in range(3):
            dcg_ref[t:t + 1, :] += tg[t]
            dcv_ref[t:t + 1, :] += tv[t]

        @pl.when(j == 0)
        def _():
            dh_ref[pl.ds(row0, tr), :] = dh

        @pl.when(j > 0)
        def _():
            dh_ref[pl.ds(row0, tr), :] += dh

    rows = lambda j, r: (nr - 1 - r, 0)
    tile = lambda j, r: (nr - 1 - r, j)
    halo = lambda j, r: (jnp.maximum((nr - 1 - r) * hb - 1, 0), j)
    return pl.pallas_call(
        body, name="ffn_bwd", grid=(NJ, nr),
        in_specs=[pl.BlockSpec((tr, D), rows), pl.BlockSpec((tr, D), rows),
                  pl.BlockSpec((tr, TF), tile), pl.BlockSpec((tr, TF), tile),
                  pl.BlockSpec((2 * SUB, TF), halo), pl.BlockSpec((2 * SUB, TF), halo),
                  pl.BlockSpec((2, TF, D), lambda j, r: (0, j, 0)),
                  pl.BlockSpec((3, TF), lambda j, r: (0, j)), pl.BlockSpec((3, TF), lambda j, r: (0, NJ + j)),
                  pl.BlockSpec((TF, D), lambda j, r: (j, 0))],
        out_specs=(pl.BlockSpec((s, D), lambda j, r: (0, 0)),
                   pl.BlockSpec((2, TF, D), lambda j, r: (0, j, 0)),
                   pl.BlockSpec((TF, D), lambda j, r: (j, 0)),
                   pl.BlockSpec((SUB, TF), lambda j, r: (0, j)), pl.BlockSpec((SUB, TF), lambda j, r: (0, j))),
        out_shape=(jax.ShapeDtypeStruct((s, D), F32),
                   jax.ShapeDtypeStruct((2, DFF, D), BF16), jax.ShapeDtypeStruct((DFF, D), BF16),
                   jax.ShapeDtypeStruct((SUB, DFF), F32), jax.ShapeDtypeStruct((SUB, DFF), F32)),
        scratch_shapes=[pltpu.VMEM((2, SUB, TF), F32), pltpu.VMEM((2 * TF, D), F32), pltpu.VMEM((TF, D), F32)],
        compiler_params=_params(("arbitrary", "arbitrary"), 56),
    )(dy, h2, pre_g, pre_v, pre_g, pre_v, wup_t, cw, cw, wd)


def _adam(w, g, m, v):
    m = ADAM_B1 * m + (1.0 - ADAM_B1) * g
    v = ADAM_B2 * v + (1.0 - ADAM_B2) * (g * g)
    m_hat = m / (1.0 - ADAM_B1 ** ADAM_STEP)
    v_hat = v / (1.0 - ADAM_B2 ** ADAM_STEP)
    delta = -ADAM_LR * (m_hat / (jnp.sqrt(v_hat) + ADAM_EPS) + ADAM_WD * w)
    return delta, m, v


NCHIP = NDEV // 2


def _pair_add(mine, theirs, tr, name):
    _, _, rws, cols = mine.shape

    def body(a_ref, b_ref, o_ref):
        c = lax.axis_index("c")
        o_ref[0] = (a_ref[0, c].astype(F32) + b_ref[0].astype(F32)).astype(BF16)

    (out,), _ = _call(
        body, (mine, theirs), name=name, grid=(NCHIP, rws // tr),
        in_specs=[pl.BlockSpec((1, 2, tr, cols), lambda q, i: (q, 0, i, 0)),
                  pl.BlockSpec((1, tr, cols), lambda q, i: (q, i, 0))],
        out_specs=[pl.BlockSpec((1, tr, cols), lambda q, i: (q, i, 0))],
        out_shape=[jax.ShapeDtypeStruct((NCHIP, rws, cols), BF16)], vmem_mb=16)
    return out


def _adamw_sharded(parts, w, m, v, tr, name, hosted=None):
    rws, cols = w.shape
    n_parts = parts.shape[0]

    def body(p_ref, w_ref, m_ref, v_ref, g_ref, d_ref, mo_ref, vo_ref):
        g = p_ref[0].astype(F32)
        for q in range(1, n_parts):
            g = g + p_ref[q].astype(F32)
        g_ref[...] = g
        d_ref[...], mo_ref[...], vo_ref[...] = _adam(w_ref[...], g, m_ref[...], v_ref[...])

    blk = pl.BlockSpec((tr, cols), lambda i: (i, 0))
    o = jax.ShapeDtypeStruct((rws, cols), F32)
    outs, moved = _call(
        body, (parts, w, m, v), name=name, grid=(rws // tr,),
        in_specs=[pl.BlockSpec((n_parts, tr, cols), lambda i: (0, i, 0)), blk, blk, blk],
        out_specs=[blk, blk, blk, blk], out_shape=[o, o, o, o], vmem_mb=44 if tr > 256 else 24, hosted=hosted)
    return (outs, moved) if hosted else outs


def _adamw_ada(c_all, dmod_my, w, m, v):
    rws, cols = w.shape
    tr = 256

    def body(c_ref, dm_ref, w_ref, m_ref, v_ref, g_ref, d_ref, mo_ref, vo_ref):
        cv = c_ref[...]
        act = cv * _sigmoid(cv)
        g = _dot(act, dm_ref[...], TN, lax.Precision.HIGHEST)
        g_ref[...] = g
        d_ref[...], mo_ref[...], vo_ref[...] = _adam(w_ref[...], g, m_ref[...], v_ref[...])

    blk = pl.BlockSpec((tr, cols), lambda i: (i, 0))
    o = jax.ShapeDtypeStruct((rws, cols), F32)
    return pl.pallas_call(
        body, name="adamw_ada", grid=(rws // tr,),
        in_specs=[pl.BlockSpec((NDEV, tr), lambda i: (0, i)), _full_spec((NDEV, cols)), blk, blk, blk],
        out_specs=(blk, blk, blk, blk), out_shape=(o, o, o, o),
        compiler_params=_params(("parallel",), 32),
    )(c_all, dmod_my, w, m, v)


REP_ROWS = 16
ROW_N1, ROW_N2, ROW_LOSS, ROW_MISC = 6, 7, 8, 9
LANE_BF, LANE_GQ, LANE_GK = 0, 128, 256


def _adamw_small(rep_all, conv_all, wmv):
    n_ff = wmv[6][0].shape[1]

    def body(*refs):
        rep_ref, conv_ref = refs[:2]
        ins = refs[2:2 + 24]
        outs = refs[2 + 24:]
        loss_ref, outs = outs[0], outs[1:]
        g_rep = rep_ref[0]
        g_conv = conv_ref[0]
        for d in range(1, NDEV):
            g_rep = g_rep + rep_ref[d]
            g_conv = g_conv + conv_ref[d]
        loss_ref[...] = (0.5 / D) * jnp.sum(g_rep[ROW_LOSS:ROW_LOSS + 1, :], axis=-1, keepdims=True)
        grads = [
            None,
            g_rep[ROW_N1:ROW_N1 + 1, :],
            g_rep[ROW_MISC:ROW_MISC + 1, LANE_BF:LANE_BF + HEADS],
            g_rep[ROW_MISC:ROW_MISC + 1, LANE_GQ:LANE_GQ + DH],
            g_rep[ROW_MISC:ROW_MISC + 1, LANE_GK:LANE_GK + DH],
            g_rep[ROW_N2:ROW_N2 + 1, :],
            g_conv[0:3, 0:n_ff],
            g_conv[0:3, n_ff:n_ff + DH],
        ]
        for p in range(8):
            w_ref, m_ref, v_ref = ins[3 * p:3 * p + 3]
            g_ref, d_ref, mo_ref, vo_ref = outs[4 * p:4 * p + 4]
            if p == 0:
                for nmod in range(NMOD):
                    sl = slice(D * nmod, D * (nmod + 1))
                    g = g_rep[nmod:nmod + 1, :]
                    g_ref[:, sl] = g
                    d_ref[:, sl], mo_ref[:, sl], vo_ref[:, sl] = _adam(w_ref[:, sl], g, m_ref[:, sl], v_ref[:, sl])
            else:
                g = grads[p]
                g_ref[...] = g
                d_ref[...], mo_ref[...], vo_ref[...] = _adam(w_ref[...], g, m_ref[...], v_ref[...])

    flat = [a for trio in wmv for a in trio]
    out_shape = [jax.ShapeDtypeStruct((1, 1), F32)]
    for trio in wmv:
        out_shape += [jax.ShapeDtypeStruct(trio[0].shape, F32)] * 4
    return pl.pallas_call(
        body, name="adamw_small", out_shape=tuple(out_shape),
        compiler_params=_params(None, 32),
    )(rep_all, conv_all, *flat)


FG_FIRST = 3 * AW
N_IN = DIN // NDEV


def _w_in_runs():
    runs = []
    for d in range(NDEV):
        lo, hi = N_IN * d, N_IN * (d + 1)
        for a, b, shift in ((0, FG_FIRST, 0), (FG_FIRST, FG_FIRST + HEADS, DIN - HEADS - FG_FIRST),
                            (FG_FIRST + HEADS, DIN, -HEADS)):
            a, b = max(a, lo), min(b, hi)
            if a < b:
                runs.append((d, a - lo, a + shift, b - a))
    return runs


W_IN_ROWS = 256
N_IN_PAD = 512


def _identity(n):
    return (lax.broadcasted_iota(jnp.int32, (n, n), 0) == lax.broadcasted_iota(jnp.int32, (n, n), 1)).astype(BF16)


def _assemble_w_in(g_in, hosted):
    def body(g_ref, o_ref, t_ref):
        eye = _identity(W_IN_ROWS)
        shard = None
        for d, src, dst, width in _w_in_runs():
            if d != shard:
                t_ref[:, 0:N_IN] = _dot(eye, g_ref[d], NT).astype(BF16)
                shard = d
            o_ref[:, dst:dst + width] = t_ref[:, src:src + width]
        o_ref[:, DIN:DINP] = jnp.zeros((W_IN_ROWS, DINP - DIN), o_ref.dtype)

    (out,), moved = _call(
        body, (g_in,), name="assemble_w_in", grid=(D // W_IN_ROWS,),
        in_specs=[pl.BlockSpec((NDEV, N_IN, W_IN_ROWS), lambda i: (0, 0, i))],
        out_specs=[pl.BlockSpec((W_IN_ROWS, DINP), lambda i: (i, 0))],
        out_shape=[jax.ShapeDtypeStruct((D, DINP), g_in.dtype)],
        scratch_shapes=[pltpu.VMEM((W_IN_ROWS, N_IN_PAD), BF16)], vmem_mb=16, hosted=hosted)
    return out, moved


def _scatter_dw_in(dwp):
    def body(w_ref, o_ref, t_ref):
        eye = _identity(W_IN_ROWS)
        runs = _w_in_runs()
        for i, (d, src, dst, width) in enumerate(runs):
            t_ref[:, src:src + width] = w_ref[:, dst:dst + width]
            if i + 1 == len(runs) or runs[i + 1][0] != d:
                o_ref[d // 2, d % 2] = _dot(t_ref[:, 0:N_IN], eye, TN).astype(BF16)

    (out,), _ = _call(
        body, (dwp,), name="scatter_dw_in", grid=(D // W_IN_ROWS,),
        in_specs=[pl.BlockSpec((W_IN_ROWS, DINP), lambda i: (i, 0))],
        out_specs=[pl.BlockSpec((NCHIP, 2, N_IN, W_IN_ROWS), lambda i: (0, 0, 0, i))],
        out_shape=[jax.ShapeDtypeStruct((NCHIP, 2, N_IN, D), dwp.dtype)],
        scratch_shapes=[pltpu.VMEM((W_IN_ROWS, N_IN_PAD), BF16)], vmem_mb=16)
    return out


def kernel(x, c, w_ada, b_ada, norm1_g, w_in, b_forget, q_norm_g, k_norm_g, conv_mix_w, w_out, norm2_g, w_up, ffn_conv_w, w_down, loss_target, m_w_ada, m_b_ada, m_norm1_g, m_w_in, m_b_forget, m_q_norm_g, m_k_norm_g, m_conv_mix_w, m_w_out, m_norm2_g, m_w_up, m_ffn_conv_w, m_w_down, v_w_ada, v_b_ada, v_norm1_g, v_w_in, v_b_forget, v_q_norm_g, v_k_norm_g, v_conv_mix_w, v_w_out, v_norm2_g, v_w_up, v_ffn_conv_w, v_w_down):
    me = 4 * lax.axis_index("x") + 2 * lax.axis_index("y") + lax.axis_index("c")
    xs, tgt = x[0], loss_target[0]
    n_ada = w_ada.shape[2]
    n_ff = w_up.shape[2]

    conv_w = jnp.concatenate([ffn_conv_w[0], conv_mix_w[0]], axis=1)
    conv_w = jnp.concatenate([conv_w, jnp.zeros((SUB - 3, conv_w.shape[1]), F32)], axis=0)
    c_all, conv_all, g_in = _exchange(
        [(c.reshape(SUB, D // SUB), "ag"), (conv_w, "ag"), (jnp.transpose(w_in[0]).astype(BF16), "ag2")],
        "exchange_w_in")
    g_in, w_out_b, w_up_b, w_down_b = lax.optimization_barrier(
        (g_in, w_out[0].astype(BF16), jnp.transpose(w_up[0]).astype(BF16), w_down[0].astype(BF16)))
    g_out, g_up, g_down = _sequencer_exchange(
        [(w_out_b, "ag2"), (w_up_b, "ag2"), (w_down_b, "ag2")], "gather_weights", collective_id=1)
    c_all = c_all.reshape(NDEV, D)
    cw_ffn = jnp.transpose(conv_all[:, :3, :n_ff], (1, 0, 2)).reshape(3, 2 * DFF)
    cw_mix = jnp.transpose(conv_all[:, :3, n_ff:], (1, 0, 2)).reshape(3, CW)

    b_my = lax.dynamic_slice(b_ada, (0, me * n_ada), (1, n_ada))
    mod_part = _ada_fwd(c_all, w_ada[0], b_my)
    w_in_p, (mod_rows,) = _assemble_w_in(
        g_in, [(jnp.broadcast_to(mod_part[:, None, :], (NDEV, SUB, n_ada)), "a2a")])
    mod = mod_rows[:, 0, :].reshape(NMOD, D)
    mod = jnp.concatenate([mod, jnp.zeros((SUB - NMOD, D), F32)], axis=0)

    h = _norm_mod_fwd(xs, mod, norm1_g)
    proj = _mm(h, w_in_p, "nn", F32, 1024, 640, "proj_fwd")
    bf_pad = jnp.concatenate([b_forget, jnp.zeros((1, LANES - HEADS), F32)], axis=1)
    fcum = _fgate_fwd(proj, bf_pad)
    (qp, kp, vp), _ = _qkv_prep(proj, fcum, q_norm_g, k_norm_g, None)
    attn, lse, _ = _attn_fwd(qp, kp, vp, None)
    w_out_f = g_out.reshape(D, D)
    w_up_t = g_up.reshape(2, DFF, D)
    w_down_f = g_down.reshape(DFF, D)
    conv = _mixconv_fwd(proj, cw_mix)
    mixed = jnp.concatenate([attn, conv], axis=1).astype(BF16)
    z = _mm(mixed, w_out_f, "nn", F32, 1024, 1024, "out_fwd")
    x1, h2 = _resid_norm2(xs, z, mod, norm2_g)
    pre_g, pre_v, y = _ffn_fwd(h2, w_up_t, cw_ffn, w_down_f)
    dout, dy, vec_l = _loss_head(x1, y, tgt, mod)

    dh2, dwup_t, dwd, dcw_g, dcw_v = _ffn_bwd(dy, h2, pre_g, pre_v, w_up_t, cw_ffn, w_down_f)
    dx1, dz, vec_2 = _norm_mod_bwd(dh2, x1, dout, z, mod, norm2_g, 4, 2, "norm2_bwd")
    dwout = _mm(mixed, dz, "tn", BF16, 1024, 1024, "out_bwd_w")
    s_out = dwout.reshape(NCHIP, 2, D // NDEV, D)
    s_down = dwd.reshape(NCHIP, 2, DFF // NDEV, D)
    s_up = dwup_t.reshape(NCHIP, 2, n_ff, D)
    dmixed, (t_out, t_up, t_down) = _mm(dz, w_out_f, "nt", F32, 1024, 1024, "out_bwd_x",
                                        hosted=[(s_out, "pair"), (s_up, "pair"), (s_down, "pair")])
    c_out = _pair_add(s_out, t_out, 128, "pair_add_out")
    c_up = _pair_add(s_up, t_up, 176, "pair_add_up")
    c_down = _pair_add(s_down, t_down, 176, "pair_add_down")
    dxin, dbg, dcg, dcw_mix = _mixconv_bwd(dmixed, proj, cw_mix)
    (dqp, dkp, dvp), (p_up, p_down, p_out) = _attn_bwd(
        qp, kp, vp, dmixed, attn, lse, [(c_up, "chips"), (c_down, "chips"), (c_out, "chips")])
    (dq, dk, dvb, dfcol, vec_qk), _ = _qkv_post(dqp, dkp, dvp, proj, q_norm_g, k_norm_g, None)
    dfg, vec_bf = _fgate_bwd(dfcol, proj, bf_pad)
    dproj = jnp.concatenate([dq, dk, dvb, dxin, dbg, dcg, dfg], axis=1)
    dwin_p = _mm(h, dproj, "tn", BF16, 1024, 640, "proj_bwd_w")
    s_in = _scatter_dw_in(dwin_p).reshape(NDEV, N_IN, D)
    (p_in,) = _sequencer_exchange([(s_in, "a2a")], "scatter_dw_in_partials", collective_id=2, all_peers=True)
    dh = _mm(dproj, w_in_p, "nt", F32, 1024, 512, "proj_bwd_x", vmem_mb=36)
    grad_x, vec_1 = _norm_mod_bwd(dh, xs, dx1, None, mod, norm1_g, 1, None, "norm1_bwd")

    misc = jnp.zeros((1, D), F32)
    misc = lax.dynamic_update_slice(misc, vec_bf[0:1, :HEADS], (0, LANE_BF))
    misc = lax.dynamic_update_slice(misc, vec_qk[0:1, :DH], (0, LANE_GQ))
    misc = lax.dynamic_update_slice(misc, vec_qk[1:2, :DH], (0, LANE_GK))
    rep = jnp.concatenate([
        vec_1[0:1], vec_1[1:2], vec_2[3:4], vec_2[0:1], vec_2[1:2], vec_l[0:1],
        vec_1[2:3], vec_2[2:3], vec_l[1:2], misc, jnp.zeros((REP_ROWS - 10, D), F32)], axis=0)
    dcw_ffn = jnp.concatenate([dcw_g, dcw_v], axis=1).reshape(SUB, NDEV, n_ff)
    dcw_all = jnp.concatenate([jnp.transpose(dcw_ffn, (1, 0, 2)),
                               jnp.transpose(dcw_mix.reshape(SUB, NDEV, DH), (1, 0, 2))], axis=2)
    r_out, (rep_all, conv_parts) = _adamw_sharded(p_out, w_out[0], m_w_out[0], v_w_out[0], 128, "adamw_out",
                                                  hosted=[(rep, "ag"), (dcw_all, "a2a")])
    dmod_my = lax.dynamic_slice(rep_all[:, :NMOD, :].reshape(NDEV, NMOD * D), (0, me * n_ada), (NDEV, n_ada))
    r_ada = _adamw_ada(c_all, dmod_my, w_ada[0], m_w_ada[0], v_w_ada[0])
    r_in = _adamw_sharded(p_in, jnp.transpose(w_in[0]), jnp.transpose(m_w_in[0]), jnp.transpose(v_w_in[0]), N_IN,
                          "adamw_in")
    r_in = tuple(jnp.transpose(a) for a in r_in)
    r_up = _adamw_sharded(p_up, jnp.transpose(w_up[0]), jnp.transpose(m_w_up[0]), jnp.transpose(v_w_up[0]), 176,
                          "adamw_up")
    r_up = tuple(jnp.transpose(a) for a in r_up)
    r_down = _adamw_sharded(p_down, w_down[0], m_w_down[0], v_w_down[0], 176, "adamw_down")
    small = _adamw_small(rep_all, conv_parts, [
        [b_ada, m_b_ada, v_b_ada], [norm1_g, m_norm1_g, v_norm1_g], [b_forget, m_b_forget, v_b_forget],
        [q_norm_g, m_q_norm_g, v_q_norm_g], [k_norm_g, m_k_norm_g, v_k_norm_g], [norm2_g, m_norm2_g, v_norm2_g],
        [ffn_conv_w[0], m_ffn_conv_w[0], v_ffn_conv_w[0]], [conv_mix_w[0], m_conv_mix_w[0], v_conv_mix_w[0]]])
    loss = small[0].reshape(())
    r_bada, r_n1, r_bf, r_gq, r_gk, r_n2, r_cf, r_cm = [small[1 + 4 * p:5 + 4 * p] for p in range(8)]
    lead = lambda t: tuple(a[None] for a in t)
    per_w = [lead(r_ada), r_bada, r_n1, lead(r_in), r_bf, r_gq, r_gk, lead(r_cm), lead(r_out), r_n2,
             lead(r_up), lead(r_cf), lead(r_down)]
    outs = [loss, grad_x[None]]
    for field in range(4):
        outs += [t[field] for t in per_w]
    return tuple(outs)
```

```python
import functools

import jax
import jax.numpy as jnp
import numpy as np
from jax import lax
from jax.experimental import pallas as pl
from jax.experimental.pallas import tpu as pltpu
from jax.experimental.pallas import tpu_sc as plsc

F32 = jnp.float32
BF16 = jnp.bfloat16

NDEV = 8
D = 1024
HEADS = 8
DH = 64
AW = 512
CW = 512
DFF = 2816
DIN = 3080
DINP = 3200
NMOD = 6
EPS = 1e-6
QK_SCALE = 0.125
LANES = 128
SUB = 8

ADAM_LR = 0.001
ADAM_B1 = 0.9
ADAM_B2 = 0.999
ADAM_EPS = 1e-08
ADAM_WD = 0.01
ADAM_STEP = 10

MESH = pl.DeviceIdType.MESH
ANY = pl.BlockSpec(memory_space=pl.ANY)

NN = (((1,), (0,)), ((), ()))
NT = (((1,), (1,)), ((), ()))
TN = (((0,), (0,)), ((), ()))


def _dot(a, b, dims=NN, precision=None):
    return lax.dot_general(a, b, dims, precision=precision, preferred_element_type=F32)


def _params(sem=None, vmem_mb=None):
    kw = {}
    if sem is not None:
        kw["dimension_semantics"] = sem
    if vmem_mb is not None:
        kw["vmem_limit_bytes"] = vmem_mb * 1024 * 1024
    return pltpu.CompilerParams(**kw)


def _sigmoid(x):
    return 0.5 * jnp.tanh(0.5 * x) + 0.5


class _Exchange:
    def __init__(self, items):
        self.arrays = [pltpu.with_memory_space_constraint(a, pltpu.HBM) for a, _ in items]
        self.modes = [m for _, m in items]
        self.n = len(items)
        self.out_shape = []
        for a, m in items:
            sh = {"ag": (NDEV,) + a.shape, "ag2": (NDEV,) + a.shape, "pair": a.shape[:1] + a.shape[2:]}.get(m, a.shape)
            self.out_shape.append(jax.ShapeDtypeStruct(sh, a.dtype))
        self.scratch = [pltpu.SemaphoreType.DMA((self.n, NDEV - 1)), pltpu.SemaphoreType.DMA((self.n, NDEV - 1)),
                        pltpu.SemaphoreType.DMA((self.n,))]

    def _plan(self, srcs, outs, sems):
        send_sems, recv_sems, loc_sems = sems
        x, y, c = lax.axis_index("x"), lax.axis_index("y"), lax.axis_index("c")
        me, my_chip = 4 * x + 2 * y + c, 2 * x + y
        sib = (x, y, 1 - c)
        local, first, landed, forwards, arrivals = [], [], [], [], []

        def remote(a, k, src, dst, to):
            return pltpu.make_async_remote_copy(src_ref=src, dst_ref=dst, send_sem=send_sems.at[a, k],
                                                recv_sem=recv_sems.at[a, k], device_id=to, device_id_type=MESH)

        for a, mode in enumerate(self.modes):
            src, out = srcs[a], outs[a]
            if mode in ("ag", "a2a"):
                piece = (lambda slot, src=src: src) if mode == "ag" else (lambda slot, src=src: src.at[slot])
                local.append(pltpu.make_async_copy(piece(me), out.at[me], loc_sems.at[a]))
                for r in range(1, NDEV):
                    px = 1 - x if (r >> 2) & 1 else x
                    py = 1 - y if (r >> 1) & 1 else y
                    pc = 1 - c if r & 1 else c
                    pidx = 4 * px + 2 * py + pc
                    first.append(remote(a, r - 1, piece(pidx), out.at[me], (px, py, pc)))
                    arrivals.append(remote(a, r - 1, piece(pidx), out.at[pidx], (px, py, pc)))
            elif mode == "ag2":
                local.append(pltpu.make_async_copy(src, out.at[me], loc_sems.at[a]))
                first.append(remote(a, 0, src, out.at[me], sib))
                arrivals.append(remote(a, 0, src, out.at[me + 1 - 2 * c], sib))
                for j, (px, py) in enumerate([(1 - x, y), (x, 1 - y), (1 - x, 1 - y)]):
                    theirs = out.at[4 * px + 2 * py + c]
                    first.append(remote(a, 1 + j, src, out.at[me], (px, py, c)))
                    landed.append(remote(a, 1 + j, src, theirs, (px, py, c)))
                    forwards.append(remote(a, 4 + j, theirs, theirs, sib))
                    arrivals.append(remote(a, 4 + j, src, out.at[4 * px + 2 * py + 1 - c], sib))
            elif mode == "pair":
                for q in range(NDEV // 2):
                    first.append(remote(a, q, src.at[q, 1 - c], out.at[q], sib))
                    arrivals.append(remote(a, q, src.at[q, 1 - c], out.at[q], sib))
            else:
                assert mode == "chips", mode
                local.append(pltpu.make_async_copy(src.at[my_chip], out.at[my_chip], loc_sems.at[a]))
                for j, (px, py) in enumerate([(1 - x, y), (x, 1 - y), (1 - x, 1 - y)]):
                    q = 2 * px + py
                    first.append(remote(a, 1 + j, src.at[q], out.at[my_chip], (px, py, c)))
                    arrivals.append(remote(a, 1 + j, src.at[q], out.at[q], (px, py, c)))
        return local, first, landed, forwards, arrivals

    def start(self, srcs, outs, sems):
        local, first, _, _, _ = self._plan(srcs, outs, sems)
        for cp in local + first:
            cp.start()

    def wait(self, srcs, outs, sems):
        local, first, landed, forwards, arrivals = self._plan(srcs, outs, sems)
        for cp, fwd in zip(landed, forwards):
            cp.wait_recv()
            fwd.start()
        for cp in arrivals:
            cp.wait_recv()
        for cp in first + forwards:
            cp.wait_send()
        for cp in local:
            cp.wait()


def _exchange(items, name):
    ex = _Exchange(items)
    n = ex.n

    def body(*refs):
        srcs, outs, sems = refs[:n], refs[n:2 * n], refs[2 * n:]
        ex.start(srcs, outs, sems)
        ex.wait(srcs, outs, sems)

    return pl.pallas_call(
        body, name=name,
        out_shape=tuple(ex.out_shape),
        in_specs=[ANY] * n, out_specs=tuple([ANY] * n),
        scratch_shapes=ex.scratch,
        compiler_params=pltpu.CompilerParams(has_side_effects=True),
    )(*ex.arrays)


def _sequencer_exchange(items, name, collective_id, all_peers=False):
    ex = _Exchange(items)
    srcs = [jax.new_ref(a, memory_space=pltpu.MemorySpace.HBM) for a in ex.arrays]
    outs = [jax.empty_ref(sh, memory_space=pltpu.MemorySpace.HBM) for sh in ex.out_shape]

    @pl.kernel(mesh=plsc.ScalarSubcoreMesh(axis_name="sequencer", num_cores=1), name=name,
               scratch_types=tuple(ex.scratch), compiler_params=pltpu.CompilerParams(collective_id=collective_id))
    def launch(send_sems, recv_sems, loc_sems):
        x, y, c = lax.axis_index("x"), lax.axis_index("y"), lax.axis_index("c")
        barrier = pltpu.get_barrier_semaphore()
        peers = [(x, y, 1 - c), (1 - x, y, c), (x, 1 - y, c), (1 - x, 1 - y, c)]
        if all_peers:
            peers += [(1 - x, y, 1 - c), (x, 1 - y, 1 - c), (1 - x, 1 - y, 1 - c)]
        for peer in peers:
            pl.semaphore_signal(barrier, inc=1, device_id=peer, device_id_type=MESH)
        pl.semaphore_wait(barrier, len(peers))
        sems = (send_sems, recv_sems, loc_sems)
        ex.start(srcs, outs, sems)
        ex.wait(srcs, outs, sems)

    launch()
    return [o[...] for o in outs]


def _call(body, inputs, *, name, grid, in_specs, out_specs, out_shape, scratch_shapes=(), vmem_mb=None, hosted=None):
    out_specs, out_shape, scratch_shapes = tuple(out_specs), tuple(out_shape), list(scratch_shapes)
    if not hosted:
        res = pl.pallas_call(
            body, name=name, grid=grid, in_specs=list(in_specs), out_specs=out_specs, out_shape=out_shape,
            scratch_shapes=scratch_shapes, compiler_params=_params(("arbitrary",) * len(grid), vmem_mb),
        )(*inputs)
        return tuple(res), ()
    ex = _Exchange(hosted)
    n, n_in, n_out, n_scr = ex.n, len(inputs), len(out_shape), len(scratch_shapes)

    def hosting_body(*refs):
        ins, srcs = refs[:n_in], refs[n_in:n_in + n]
        outs, landing = refs[n_in + n:n_in + n + n_out], refs[n_in + n + n_out:n_in + 2 * n + n_out]
        scratch, sems = refs[n_in + 2 * n + n_out:n_in + 2 * n + n_out + n_scr], refs[n_in + 2 * n + n_out + n_scr:]
        first = functools.reduce(jnp.logical_and, [pl.program_id(d) == 0 for d in range(len(grid))])
        last = functools.reduce(jnp.logical_and, [pl.program_id(d) == grid[d] - 1 for d in range(len(grid))])

        @pl.when(first)
        def _():
            ex.start(srcs, landing, sems)

        body(*ins, *outs, *scratch)

        @pl.when(last)
        def _():
            ex.wait(srcs, landing, sems)

    res = pl.pallas_call(
        hosting_body, name=name, grid=grid,
        in_specs=list(in_specs) + [ANY] * n, out_specs=out_specs + tuple([ANY] * n),
        out_shape=out_shape + tuple(ex.out_shape), scratch_shapes=scratch_shapes + ex.scratch,
        compiler_params=_params(("arbitrary",) * len(grid), vmem_mb),
    )(*inputs, *ex.arrays)
    return tuple(res[:n_out]), tuple(res[n_out:])


def _mm(a, b, mode, out_dtype, tm, tn, name, hosted=None, vmem_mb=24):
    if mode == "nn":
        (m, k), n = a.shape, b.shape[1]
        a_spec = pl.BlockSpec((tm, k), lambda i, j: (i, 0))
        b_spec = pl.BlockSpec((k, tn), lambda i, j: (0, j))
        dims = NN
    elif mode == "nt":
        (m, k), n = a.shape, b.shape[0]
        a_spec = pl.BlockSpec((tm, k), lambda i, j: (i, 0))
        b_spec = pl.BlockSpec((tn, k), lambda i, j: (j, 0))
        dims = NT
    else:
        (k, m), n = a.shape, b.shape[1]
        a_spec = pl.BlockSpec((k, tm), lambda i, j: (0, i))
        b_spec = pl.BlockSpec((k, tn), lambda i, j: (0, j))
        dims = TN
    assert m % tm == 0 and n % tn == 0, (m, n, tm, tn)

    def body(a_ref, b_ref, o_ref):
        o_ref[...] = _dot(a_ref[...], b_ref[...], dims).astype(o_ref.dtype)

    (out,), moved = _call(
        body, (a, b), name=name, grid=(m // tm, n // tn),
        in_specs=[a_spec, b_spec], out_specs=[pl.BlockSpec((tm, tn), lambda i, j: (i, j))],
        out_shape=[jax.ShapeDtypeStruct((m, n), out_dtype)], vmem_mb=vmem_mb, hosted=hosted)
    return (out, moved) if hosted else out


def _shift_down(x, k, fill):
    y = pltpu.roll(x, k, 0)
    row = lax.broadcasted_iota(jnp.int32, (SUB, x.shape[1]), 0)
    head = y[0:SUB, :]
    for t in range(k):
        head = jnp.where(row == t, fill[t], head)
    return jnp.concatenate([head, y[SUB:, :]], axis=0)


def _shift_up(x, k, fill):
    n = x.shape[0]
    y = pltpu.roll(x, n - k, 0)
    row = lax.broadcasted_iota(jnp.int32, (SUB, x.shape[1]), 0)
    tail = y[n - SUB:, :]
    for t in range(k):
        tail = jnp.where(row == SUB - k + t, fill[t], tail)
    return jnp.concatenate([y[:n - SUB, :], tail], axis=0)


def _conv_taps(x, halo, w):
    if halo is None:
        f1, f2 = [0.0], [0.0, 0.0]
    else:
        f1, f2 = [halo[7:8, :]], [halo[6:7, :], halo[7:8, :]]
    s1 = _shift_down(x, 1, f1)
    s2 = _shift_down(x, 2, f2)
    u = w[2:3, :] * x + w[1:2, :] * s1 + w[0:1, :] * s2
    return u, s1, s2


def _conv_taps_t(du, nxt, w):
    if nxt is None:
        f1, f2 = [0.0], [0.0, 0.0]
    else:
        f1, f2 = [nxt[0:1, :]], [nxt[0:1, :], nxt[1:2, :]]
    return w[2:3, :] * du + w[1:2, :] * _shift_up(du, 1, f1) + w[0:1, :] * _shift_up(du, 2, f2)


def _ada_fwd(c_all, w_ada, b_my):
    def body(c_ref, w_ref, b_ref, o_ref):
        cv = c_ref[...]
        act = cv * _sigmoid(cv)
        o_ref[...] = _dot(act, w_ref[...], NN, lax.Precision.HIGHEST) + b_ref[...]

    return pl.pallas_call(
        body, name="ada_fwd",
        out_shape=jax.ShapeDtypeStruct((NDEV, w_ada.shape[1]), F32),
        compiler_params=_params(None, 32),
    )(c_all, w_ada, b_my)


TR = 256
TRE = 512


def _row_spec(width, col=0, rows=TR):
    return pl.BlockSpec((rows, width), lambda i, col=col: (i, col))


def _erow(width):
    return _row_spec(width, rows=TRE)


def _full_spec(shape):
    return pl.BlockSpec(shape, lambda i: (0,) * len(shape))


def _norm_mod_fwd(x, mod, g):
    s = x.shape[0]

    def body(x_ref, mod_ref, g_ref, h_ref):
        xv = x_ref[...]
        r = lax.rsqrt(jnp.mean(xv * xv, axis=-1, keepdims=True) + EPS)
        nrm = xv * r * g_ref[...]
        h_ref[...] = (nrm * (1.0 + mod_ref[1:2, :]) + mod_ref[0:1, :]).astype(BF16)

    return pl.pallas_call(
        body, name="norm1_fwd", grid=(s // TRE,),
        in_specs=[_erow(D), _full_spec((SUB, D)), _full_spec((1, D))],
        out_specs=_erow(D), out_shape=jax.ShapeDtypeStruct((s, D), BF16),
        compiler_params=_params(("parallel",), 16),
    )(x, mod, g)


SLAB = 2 * DH
AUG_F, AUG_ONE, AUG_LSE = 0, 3, 6


def _split3(x):
    hi = x.astype(BF16).astype(F32)
    r1 = x - hi
    mid = r1.astype(BF16).astype(F32)
    return hi, mid, r1 - mid


def _lanes3(lane, first, pieces, other):
    out = other
    for k in range(3):
        out = jnp.where(lane == first + k, pieces[k], out)
    return out


def _aug_placement():
    eq = np.zeros((3 * LANES, HEADS * SLAB), np.float32)
    ek = np.zeros((3 * LANES, HEADS * SLAB), np.float32)
    ones = np.zeros((SUB, HEADS * SLAB), np.float32)
    for h in range(HEADS):
        aug = SLAB * h + DH
        for k in range(3):
            eq[LANES * k + h, aug + AUG_F + k] = 1.0
            ek[LANES * k + h, aug + AUG_ONE + k] = -1.0
            ones[0, aug + AUG_ONE + k] = 1.0
            ones[1, aug + AUG_F + k] = ones[1, aug + AUG_LSE + k] = 1.0
            ones[2, aug + k] = 1.0
    return jnp.asarray(eq, BF16), jnp.asarray(ek, BF16), jnp.asarray(ones)


def _qkv_prep(proj, fcum, gq, gk, hosted):
    s = proj.shape[0]

    def body(q_ref, k_ref, v_ref, f_ref, gq_ref, gk_ref, eq_ref, ek_ref, ones_ref, qo_ref, ko_ref, vo_ref):
        f3 = jnp.concatenate(_split3(f_ref[...]), axis=1).astype(BF16)
        qo_ref[...] = (_dot(f3, eq_ref[...]) + ones_ref[0:1, :]).astype(BF16)
        ko_ref[...] = (_dot(f3, ek_ref[...]) + ones_ref[1:2, :]).astype(BF16)
        vo_ref[...] = jnp.broadcast_to(ones_ref[2:3, :], vo_ref.shape).astype(BF16)
        for h in range(HEADS):
            sl = slice(DH * h, DH * (h + 1))
            lo = slice(SLAB * h, SLAB * h + DH)
            qh = q_ref[:, sl]
            r = lax.rsqrt(jnp.mean(qh * qh, axis=-1, keepdims=True) + EPS)
            qo_ref[:, lo] = (qh * r * gq_ref[...] * QK_SCALE).astype(BF16)
            kh = k_ref[:, sl]
            r = lax.rsqrt(jnp.mean(kh * kh, axis=-1, keepdims=True) + EPS)
            ko_ref[:, lo] = (kh * r * gk_ref[...]).astype(BF16)
            vo_ref[:, lo] = v_ref[:, sl].astype(BF16)

    eq, ek, ones = _aug_placement()
    o = jax.ShapeDtypeStruct((s, HEADS * SLAB), BF16)
    wide = _row_spec(HEADS * SLAB)
    return _call(
        body, (proj, proj, proj, fcum, gq, gk, eq, ek, ones), name="qkv_prep", grid=(s // TR,),
        in_specs=[_row_spec(AW, 0), _row_spec(AW, 1), _row_spec(AW, 2), _row_spec(LANES),
                  _full_spec((1, DH)), _full_spec((1, DH)), _full_spec(eq.shape), _full_spec(ek.shape),
                  _full_spec(ones.shape)],
        out_specs=[wide, wide, wide], out_shape=[o, o, o], vmem_mb=16, hosted=hosted)


FG_BLOCK = (3 * AW + 3 * CW) // LANES


def _fgate_fwd(proj, bf_pad):
    s = proj.shape[0]

    def body(fg_ref, b_ref, o_ref, carry_ref):
        i = pl.program_id(0)

        @pl.when(i == 0)
        def _():
            carry_ref[...] = jnp.zeros_like(carry_ref)

        z = fg_ref[...] + b_ref[...]
        logf = jnp.minimum(z, 0.0) - jnp.log1p(jnp.exp(-jnp.abs(z)))
        row = lax.broadcasted_iota(jnp.int32, (TR, TR), 0)
        col = lax.broadcasted_iota(jnp.int32, (TR, TR), 1)
        tri = (col <= row).astype(F32)
        cs = _dot(tri, logf, NN, lax.Precision.HIGHEST) + carry_ref[0:1, :]
        o_ref[...] = cs
        carry_ref[...] = jnp.broadcast_to(cs[TR - 1:TR, :], carry_ref.shape)

    return pl.pallas_call(
        body, name="fgate_fwd", grid=(s // TR,),
        in_specs=[_row_spec(LANES, FG_BLOCK), _full_spec((1, LANES))],
        out_specs=_row_spec(LANES), out_shape=jax.ShapeDtypeStruct((s, LANES), F32),
        scratch_shapes=[pltpu.VMEM((SUB, LANES), F32)],
        compiler_params=_params(("arbitrary",)),
    )(proj, bf_pad)


def _fgate_bwd(dfcol, proj, bf_pad):
    s = proj.shape[0]
    nb = s // TR

    def body(df_ref, fg_ref, b_ref, o_ref, db_ref, carry_ref):
        i = pl.program_id(0)

        @pl.when(i == 0)
        def _():
            carry_ref[...] = jnp.zeros_like(carry_ref)
            db_ref[...] = jnp.zeros_like(db_ref)

        row = lax.broadcasted_iota(jnp.int32, (TR, TR), 0)
        col = lax.broadcasted_iota(jnp.int32, (TR, TR), 1)
        tri = (col >= row).astype(F32)
        dlogf = _dot(tri, df_ref[...], NN, lax.Precision.HIGHEST) + carry_ref[0:1, :]
        carry_ref[...] = jnp.broadcast_to(dlogf[0:1, :], carry_ref.shape)
        z = fg_ref[...] + b_ref[...]
        dfg = dlogf * _sigmoid(-z)
        o_ref[...] = dfg.astype(BF16)
        db_ref[0:1, :] += jnp.sum(dfg, axis=0, keepdims=True)

    rev = lambda col: pl.BlockSpec((TR, LANES), lambda i, col=col: (nb - 1 - i, col))
    return pl.pallas_call(
        body, name="fgate_bwd", grid=(nb,),
        in_specs=[rev(0), rev(FG_BLOCK), _full_spec((1, LANES))],
        out_specs=(rev(0), _full_spec((SUB, LANES))),
        out_shape=(jax.ShapeDtypeStruct((s, LANES), BF16), jax.ShapeDtypeStruct((SUB, LANES), F32)),
        scratch_shapes=[pltpu.VMEM((SUB, LANES), F32)],
        compiler_params=_params(("arbitrary",)),
    )(dfcol, proj, bf_pad)


def _resid_norm2(x, z, mod, g):
    s = x.shape[0]

    def body(x_ref, z_ref, mod_ref, g_ref, x1_ref, h_ref):
        x1 = x_ref[...] + mod_ref[2:3, :] * z_ref[...]
        x1_ref[...] = x1
        r = lax.rsqrt(jnp.mean(x1 * x1, axis=-1, keepdims=True) + EPS)
        nrm = x1 * r * g_ref[...]
        h_ref[...] = (nrm * (1.0 + mod_ref[4:5, :]) + mod_ref[3:4, :]).astype(BF16)

    return pl.pallas_call(
        body, name="resid_norm2", grid=(s // TRE,),
        in_specs=[_erow(D), _erow(D), _full_spec((SUB, D)), _full_spec((1, D))],
        out_specs=(_erow(D), _erow(D)),
        out_shape=(jax.ShapeDtypeStruct((s, D), F32), jax.ShapeDtypeStruct((s, D), BF16)),
        compiler_params=_params(("parallel",), 24),
    )(x, z, mod, g)


def _loss_head(x1, y, tgt, mod):
    s = x1.shape[0]

    def body(x1_ref, y_ref, t_ref, mod_ref, dout_ref, dy_ref, vec_ref):
        @pl.when(pl.program_id(0) == 0)
        def _():
            vec_ref[...] = jnp.zeros_like(vec_ref)

        yv = y_ref[...]
        g2 = mod_ref[5:6, :]
        diff = x1_ref[...] + g2 * yv - t_ref[...]
        dout = diff * (1.0 / D)
        dout_ref[...] = dout
        dy_ref[...] = (g2 * dout).astype(BF16)
        vec_ref[0:1, :] += jnp.sum(dout * yv, axis=0, keepdims=True)
        vec_ref[1:2, :] += jnp.sum(diff * diff, axis=0, keepdims=True)

    return pl.pallas_call(
        body, name="loss_head", grid=(s // TRE,),
        in_specs=[_erow(D), _erow(D), _erow(D), _full_spec((SUB, D))],
        out_specs=(_erow(D), _erow(D), _full_spec((SUB, D))),
        out_shape=(jax.ShapeDtypeStruct((s, D), F32), jax.ShapeDtypeStruct((s, D), BF16),
                   jax.ShapeDtypeStruct((SUB, D), F32)),
        compiler_params=_params(("arbitrary",), 24),
    )(x1, y, tgt, mod)


def _norm_mod_bwd(dh, xin, dres, zin, mod, g, scale_row, gate_row, name):
    s = dh.shape[0]
    with_gate = gate_row is not None

    def body(*refs):
        if with_gate:
            dh_ref, x_ref, dres_ref, z_ref, mod_ref, g_ref, dx_ref, dz_ref, vec_ref = refs
        else:
            dh_ref, x_ref, dres_ref, mod_ref, g_ref, dx_ref, vec_ref = refs

        @pl.when(pl.program_id(0) == 0)
        def _():
            vec_ref[...] = jnp.zeros_like(vec_ref)

        xv = x_ref[...]
        dhv = dh_ref[...]
        gv = g_ref[...]
        r = lax.rsqrt(jnp.mean(xv * xv, axis=-1, keepdims=True) + EPS)
        xh = xv * r
        dn = dhv * (1.0 + mod_ref[scale_row:scale_row + 1, :])
        dxh = dn * gv
        dx = dres_ref[...] + r * (dxh - xh * jnp.mean(dxh * xh, axis=-1, keepdims=True))
        dx_ref[...] = dx
        vec_ref[0:1, :] += jnp.sum(dhv, axis=0, keepdims=True)
        vec_ref[1:2, :] += jnp.sum(dhv * (xh * gv), axis=0, keepdims=True)
        vec_ref[2:3, :] += jnp.sum(dn * xh, axis=0, keepdims=True)
        if with_gate:
            dz_ref[...] = (mod_ref[gate_row:gate_row + 1, :] * dx).astype(BF16)
            vec_ref[3:4, :] += jnp.sum(dx * z_ref[...], axis=0, keepdims=True)

    ins = [dh, xin, dres] + ([zin] if with_gate else []) + [mod, g]
    in_specs = [_erow(D)] * (4 if with_gate else 3) + [_full_spec((SUB, D)), _full_spec((1, D))]
    out_specs = [_erow(D)] + ([_erow(D)] if with_gate else []) + [_full_spec((SUB, D))]
    out_shape = [jax.ShapeDtypeStruct((s, D), F32)] + ([jax.ShapeDtypeStruct((s, D), BF16)] if with_gate else []) \
        + [jax.ShapeDtypeStruct((SUB, D), F32)]
    outs, _ = _call(body, ins, name=name, grid=(s // TRE,), in_specs=in_specs, out_specs=out_specs,
                    out_shape=out_shape, vmem_mb=32)
    return outs


XIN_BLOCK = 3 * AW // LANES
BG_BLOCK = XIN_BLOCK + CW // LANES
CG_BLOCK = BG_BLOCK + CW // LANES


def _seq_spec(s, first_block):
    return pl.BlockSpec((s, LANES), lambda j, fb=first_block: (0, fb + j))


def _mixconv_fwd(proj, w):
    s = proj.shape[0]

    def body(xin_ref, bg_ref, cg_ref, w_ref, o_ref):
        cx = cg_ref[...] * xin_ref[...]
        cv, _, _ = _conv_taps(cx, None, w_ref[...])
        o_ref[...] = bg_ref[...] * cv

    return pl.pallas_call(
        body, name="mixconv_fwd", grid=(CW // LANES,),
        in_specs=[_seq_spec(s, XIN_BLOCK), _seq_spec(s, BG_BLOCK), _seq_spec(s, CG_BLOCK),
                  pl.BlockSpec((3, LANES), lambda j: (0, j))],
        out_specs=_seq_spec(s, 0), out_shape=jax.ShapeDtypeStruct((s, CW), F32),
        compiler_params=_params(("parallel",), 32),
    )(proj, proj, proj, w)


def _mixconv_bwd(dmixed, proj, w):
    s = proj.shape[0]

    def body(d_ref, xin_ref, bg_ref, cg_ref, w_ref, dxin_ref, dbg_ref, dcg_ref, dw_ref):
        wv = w_ref[...]
        xin, cg, dconv = xin_ref[...], cg_ref[...], d_ref[...]
        cx = cg * xin
        cv, s1, s2 = _conv_taps(cx, None, wv)
        dbg_ref[...] = (dconv * cv).astype(BF16)
        dcv = dconv * bg_ref[...]
        dw_ref[...] = jnp.zeros_like(dw_ref)
        dw_ref[0:1, :] = jnp.sum(dcv * s2, axis=0, keepdims=True)
        dw_ref[1:2, :] = jnp.sum(dcv * s1, axis=0, keepdims=True)
        dw_ref[2:3, :] = jnp.sum(dcv * cx, axis=0, keepdims=True)
        dcx = _conv_taps_t(dcv, None, wv)
        dcg_ref[...] = (dcx * xin).astype(BF16)
        dxin_ref[...] = (dcx * cg).astype(BF16)

    o = jax.ShapeDtypeStruct((s, CW), BF16)
    return pl.pallas_call(
        body, name="mixconv_bwd", grid=(CW // LANES,),
        in_specs=[_seq_spec(s, AW // LANES), _seq_spec(s, XIN_BLOCK), _seq_spec(s, BG_BLOCK), _seq_spec(s, CG_BLOCK),
                  pl.BlockSpec((3, LANES), lambda j: (0, j))],
        out_specs=(_seq_spec(s, 0), _seq_spec(s, 0), _seq_spec(s, 0), pl.BlockSpec((SUB, LANES), lambda j: (0, j))),
        out_shape=(o, o, o, jax.ShapeDtypeStruct((SUB, CW), F32)),
        compiler_params=_params(("parallel",), 32),
    )(dmixed, proj, proj, proj, w)


TA = 512
NEG = -1e30


def _causal_mask():
    row = lax.broadcasted_iota(jnp.int32, (TA, TA), 0)
    col = lax.broadcasted_iota(jnp.int32, (TA, TA), 1)
    return col <= row


def _attn_fwd(qp, kp, vp, hosted):
    s = qp.shape[0]
    nq = s // TA

    def body(q_ref, k_ref, v_ref, o_ref, lse_ref):
        i = pl.program_id(1)
        slabs = [slice(SLAB * hh, SLAB * (hh + 1)) for hh in range(2)]
        q = [q_ref[:, sl] for sl in slabs]

        def block(j, carry, masked):
            keys = pl.ds(pl.multiple_of(j * TA, TA), TA)
            ms, acc = carry
            m_out, parts = [], []
            for hh in range(2):
                sc = _dot(q[hh], k_ref[keys, slabs[hh]], NT)
                if masked:
                    sc = jnp.where(_causal_mask(), sc, NEG)
                m_new = jnp.maximum(ms[hh], jnp.max(sc, axis=-1, keepdims=True))
                p = jnp.exp(sc - m_new)
                parts.append(jnp.exp(ms[hh] - m_new) * acc[:, slabs[hh]] + _dot(p.astype(BF16), v_ref[keys, slabs[hh]]))
                m_out.append(m_new)
            return tuple(m_out), jnp.concatenate(parts, axis=1)

        init = ((jnp.full((TA, 1), NEG, F32), jnp.full((TA, 1), NEG, F32)), jnp.zeros((TA, 2 * SLAB), F32))
        carry = lax.fori_loop(0, i, lambda j, cr: block(j, cr, False), init)
        ms, acc = block(i, carry, True)
        for hh in range(2):
            l = acc[:, SLAB * hh + DH:SLAB * hh + DH + 1]
            o_ref[:, DH * hh:DH * (hh + 1)] = acc[:, SLAB * hh:SLAB * hh + DH] / l
            lse_ref[0, :, hh:hh + 1] = ms[hh] + jnp.log(l)

    (o, lse), moved = _call(
        body, (qp, kp, vp), name="attn_fwd", grid=(HEADS // 2, nq),
        in_specs=[pl.BlockSpec((TA, 2 * SLAB), lambda p, i: (i, p)),
                  pl.BlockSpec((s, 2 * SLAB), lambda p, i: (0, p)),
                  pl.BlockSpec((s, 2 * SLAB), lambda p, i: (0, p))],
        out_specs=[pl.BlockSpec((TA, LANES), lambda p, i: (i, p)), pl.BlockSpec((1, TA, 2), lambda p, i: (p, i, 0))],
        out_shape=[jax.ShapeDtypeStruct((s, AW), F32), jax.ShapeDtypeStruct((HEADS // 2, s, 2), F32)],
        vmem_mb=24, hosted=hosted)
    return o, lse, moved


def _attn_bwd(qp, kp, vp, dmixed, o, lse, hosted):
    s = qp.shape[0]
    nq = s // TA

    def body(q_ref, k_ref, v_ref, do_ref, o_ref, lse_ref, dq_ref, dk_ref, dv_ref, qb_ref, dob_ref):
        dk_ref[...] = jnp.zeros_like(dk_ref)
        dv_ref[...] = jnp.zeros_like(dv_ref)
        slabs = [slice(SLAB * hh, SLAB * (hh + 1)) for hh in range(2)]
        lane = lax.broadcasted_iota(jnp.int32, (TA, DH), 1)

        def q_block(i, _):
            i0 = pl.multiple_of(i * TA, TA)
            rows = pl.ds(i0, TA)
            for hh in range(2):
                half = slice(DH * hh, DH * (hh + 1))
                do = do_ref[rows, half]
                delta = jnp.sum(do * o_ref[rows, half], axis=-1, keepdims=True)
                dob_ref[hh, :, 0:DH] = do.astype(BF16)
                dob_ref[hh, :, DH:SLAB] = _lanes3(lane, 0, [-d for d in _split3(delta)], 0.0).astype(BF16)
                lse3 = _split3(lse_ref[0, rows, hh:hh + 1])
                qb_ref[hh, :, 0:DH] = q_ref[rows, SLAB * hh:SLAB * hh + DH]
                aug = q_ref[rows, SLAB * hh + DH:SLAB * (hh + 1)].astype(F32)
                qb_ref[hh, :, DH:SLAB] = _lanes3(lane, AUG_LSE, [-x for x in lse3], aug).astype(BF16)

            def block(j, dq, masked):
                keys = pl.ds(pl.multiple_of(j * TA, TA), TA)
                dv, dk, dqc = [], [], []
                for hh in range(2):
                    q, dob = qb_ref[hh], dob_ref[hh]
                    k = k_ref[keys, slabs[hh]]
                    sc = _dot(q, k, NT)
                    if masked:
                        sc = jnp.where(_causal_mask(), sc, NEG)
                    p = jnp.exp(sc)
                    dv.append(_dot(p.astype(BF16), dob, TN))
                    ds = (p * _dot(dob, v_ref[keys, slabs[hh]], NT)).astype(BF16)
                    dk.append(_dot(ds, q, TN))
                    dqc.append(_dot(ds, k))
                dv_ref[keys, :] += jnp.concatenate(dv, axis=1)
                dk_ref[keys, :] += jnp.concatenate(dk, axis=1)
                return dq + jnp.concatenate(dqc, axis=1)

            dq = lax.fori_loop(0, i, lambda j, acc: block(j, acc, False), jnp.zeros((TA, 2 * SLAB), F32))
            dq_ref[rows, :] = block(i, dq, True)
            return 0

        lax.fori_loop(0, nq, q_block, 0)

    pair = lambda p: (0, p)
    slab2 = pl.BlockSpec((s, 2 * SLAB), pair)
    seq = pl.BlockSpec((s, LANES), pair)
    small = pl.BlockSpec((1, s, 2), lambda p: (p, 0, 0))
    o32 = jax.ShapeDtypeStruct((s, HEADS * SLAB), F32)
    return _call(
        body, (qp, kp, vp, dmixed, o, lse), name="attn_bwd", grid=(HEADS // 2,),
        in_specs=[slab2, slab2, slab2, seq, seq, small], out_specs=[slab2, slab2, slab2], out_shape=[o32, o32, o32],
        scratch_shapes=[pltpu.VMEM((2, TA, SLAB), BF16), pltpu.VMEM((2, TA, SLAB), BF16)], vmem_mb=40, hosted=hosted)


def _qkv_post(dqp, dkp, dvp, proj, gq, gk, hosted):
    s = proj.shape[0]

    def body(dq_ref, dk_ref, dv_ref, q_ref, k_ref, gq_ref, gk_ref, dqo_ref, dko_ref, dvo_ref, df_ref, vec_ref):
        @pl.when(pl.program_id(0) == 0)
        def _():
            vec_ref[...] = jnp.zeros_like(vec_ref)

        def one(d_ref, x_ref, g_ref, o_ref, row, scale):
            dg = jnp.zeros((1, DH), F32)
            for h in range(HEADS):
                sl = slice(DH * h, DH * (h + 1))
                xv = x_ref[:, sl]
                r = lax.rsqrt(jnp.mean(xv * xv, axis=-1, keepdims=True) + EPS)
                xh = xv * r
                dn = d_ref[:, SLAB * h:SLAB * h + DH] * scale
                dg = dg + jnp.sum(dn * xh, axis=0, keepdims=True)
                dxh = dn * g_ref[...]
                o_ref[:, sl] = (r * (dxh - xh * jnp.mean(dxh * xh, axis=-1, keepdims=True))).astype(BF16)
            vec_ref[row:row + 1, 0:DH] += dg

        one(dq_ref, q_ref, gq_ref, dqo_ref, 0, QK_SCALE)
        one(dk_ref, k_ref, gk_ref, dko_ref, 1, 1.0)
        lane = lax.broadcasted_iota(jnp.int32, (TR, LANES), 1)
        df = jnp.zeros((TR, LANES), F32)
        for h in range(HEADS):
            dvo_ref[:, DH * h:DH * (h + 1)] = dv_ref[:, SLAB * h:SLAB * h + DH].astype(BF16)
            row_sum = dq_ref[:, SLAB * h + DH:SLAB * h + DH + 1]
            col_sum = dk_ref[:, SLAB * h + DH + AUG_ONE:SLAB * h + DH + AUG_ONE + 1]
            df = jnp.where(lane == h, row_sum - col_sum, df)
        df_ref[...] = df

    o = jax.ShapeDtypeStruct((s, AW), BF16)
    wide = _row_spec(HEADS * SLAB)
    return _call(
        body, (dqp, dkp, dvp, proj, proj, gq, gk), name="qkv_post", grid=(s // TR,),
        in_specs=[wide, wide, wide, _row_spec(AW, 0), _row_spec(AW, 1), _full_spec((1, DH)), _full_spec((1, DH))],
        out_specs=[_row_spec(AW), _row_spec(AW), _row_spec(AW), _row_spec(LANES), _full_spec((SUB, LANES))],
        out_shape=[o, o, o, jax.ShapeDtypeStruct((s, LANES), F32), jax.ShapeDtypeStruct((SUB, LANES), F32)],
        hosted=hosted)


TF = 256
NJ = DFF // TF
FFN_ROWS_FWD = 1024
FFN_ROWS_BWD = 1024


def _ffn_fwd(h2, wup_t, cw, wd):
    s = h2.shape[0]
    tr = FFN_ROWS_FWD
    nr = s // tr

    def body(h_ref, wu_ref, cg_ref, cv_ref, wd_ref, pg_ref, pv_ref, y_ref, halo_ref, act_ref):
        r, j = pl.program_id(0), pl.program_id(1)
        hv = h_ref[...]
        pg = _dot(hv, wu_ref[0], NT).astype(BF16)
        pv = _dot(hv, wu_ref[1], NT).astype(BF16)
        pg_ref[...] = pg
        pv_ref[...] = pv
        pgf, pvf = pg.astype(F32), pv.astype(F32)
        ug, _, _ = _conv_taps(pgf, jnp.where(r > 0, halo_ref[j, 0], 0.0), cg_ref[...])
        uv, _, _ = _conv_taps(pvf, jnp.where(r > 0, halo_ref[j, 1], 0.0), cv_ref[...])
        halo_ref[j, 0] = pgf[tr - SUB:tr, :]
        halo_ref[j, 1] = pvf[tr - SUB:tr, :]
        act = (ug * _sigmoid(ug) * uv).astype(BF16)
        for t in range(NJ):
            @pl.when(j == t)
            def _(t=t):
                act_ref[:, t * TF:(t + 1) * TF] = act

        @pl.when(j == NJ - 1)
        def _():
            y_ref[...] = _dot(act_ref[...], wd_ref[...])

    pre = jax.ShapeDtypeStruct((s, DFF), BF16)
    return pl.pallas_call(
        body, name="ffn_fwd", grid=(nr, NJ),
        in_specs=[pl.BlockSpec((tr, D), lambda r, j: (r, 0)),
                  pl.BlockSpec((2, TF, D), lambda r, j: (0, j, 0)),
                  pl.BlockSpec((3, TF), lambda r, j: (0, j)),
                  pl.BlockSpec((3, TF), lambda r, j: (0, NJ + j)),
                  pl.BlockSpec((DFF, D), lambda r, j: (0, 0))],
        out_specs=(pl.BlockSpec((tr, TF), lambda r, j: (r, j)),
                   pl.BlockSpec((tr, TF), lambda r, j: (r, j)),
                   pl.BlockSpec((tr, D), lambda r, j: (r, 0))),
        out_shape=(pre, pre, jax.ShapeDtypeStruct((s, D), F32)),
        scratch_shapes=[pltpu.VMEM((NJ, 2, SUB, TF), F32), pltpu.VMEM((tr, DFF), BF16)],
        compiler_params=_params(("arbitrary", "arbitrary"), 56),
    )(h2, wup_t, cw, cw, wd)


def _ffn_bwd(dy, h2, pre_g, pre_v, wup_t, cw, wd):
    s = h2.shape[0]
    tr = FFN_ROWS_BWD
    nr = s // tr
    hb = tr // (2 * SUB)

    def body(dy_ref, h_ref, pg_ref, pv_ref, hg_ref, hv_ref, wu_ref, cg_ref, cv_ref, wd_ref,
             dh_ref, dwu_ref, dwd_ref, dcg_ref, dcv_ref, nxt_ref, awu_ref, awd_ref):
        j, r = pl.program_id(0), pl.program_id(1)
        rr = nr - 1 - r
        row0 = pl.multiple_of(rr * tr, tr)
        cwg, cwv = cg_ref[...], cv_ref[...]
        pg, pv = pg_ref[...].astype(F32), pv_ref[...].astype(F32)
        ug, g1, g2 = _conv_taps(pg, jnp.where(rr > 0, hg_ref[SUB:2 * SUB, :].astype(F32), 0.0), cwg)
        uv, v1, v2 = _conv_taps(pv, jnp.where(rr > 0, hv_ref[SUB:2 * SUB, :].astype(F32), 0.0), cwv)
        sg = _sigmoid(ug)
        sil = ug * sg
        act = (sil * uv).astype(BF16)
        dyv = dy_ref[...]
        da = _dot(dyv, wd_ref[...], NT)
        dug = da * uv * (sg * (1.0 + ug * (1.0 - sg)))
        duv = da * sil
        dpg = _conv_taps_t(dug, jnp.where(r > 0, nxt_ref[0], 0.0), cwg)
        dpv = _conv_taps_t(duv, jnp.where(r > 0, nxt_ref[1], 0.0), cwv)
        nxt_ref[0] = dug[0:SUB, :]
        nxt_ref[1] = duv[0:SUB, :]
        dpgb, dpvb = dpg.astype(BF16), dpv.astype(BF16)
        hv = h_ref[...]
        dwd = _dot(act, dyv, TN)
        dpb = jnp.concatenate([dpgb, dpvb], axis=1)
        dwu = _dot(dpb, hv, TN)
        dh = _dot(dpb, wu_ref[...].reshape(2 * TF, D))

        def taps(du, x0, x1, x2):
            return (jnp.sum(du * x2, axis=0, keepdims=True), jnp.sum(du * x1, axis=0, keepdims=True),
                    jnp.sum(du * x0, axis=0, keepdims=True))

        tg, tv = taps(dug, pg, g1, g2), taps(duv, pv, v1, v2)

        @pl.when(r == 0)
        def _():
            awd_ref[...] = dwd
            awu_ref[...] = dwu
            dcg_ref[...] = jnp.zeros_like(dcg_ref)
            dcv_ref[...] = jnp.zeros_like(dcv_ref)

        @pl.when(r > 0)
        def _():
            awd_ref[...] += dwd
            awu_ref[...] += dwu

        @pl.when(r == nr - 1)
        def _():
            dwd_ref[...] = awd_ref[...].astype(BF16)
            dwu_ref[...] = awu_ref[...].astype(BF16).reshape(2, TF, D)

        for t in range(3):
            dcg_ref[t:t + 1, :] += tg[t]
            dcv_ref[t:t + 1, :] += tv[t]

        @pl.when(j == 0)
        def _():
            dh_ref[pl.ds(row0, tr), :] = dh

        @pl.when(j > 0)
        def _():
            dh_ref[pl.ds(row0, tr), :] += dh

    rows = lambda j, r: (nr - 1 - r, 0)
    tile = lambda j, r: (nr - 1 - r, j)
    halo = lambda j, r: (jnp.maximum((nr - 1 - r) * hb - 1, 0), j)
    return pl.pallas_call(
        body, name="ffn_bwd", grid=(NJ, nr),
        in_specs=[pl.BlockSpec((tr, D), rows), pl.BlockSpec((tr, D), rows),
                  pl.BlockSpec((tr, TF), tile), pl.BlockSpec((tr, TF), tile),
                  pl.BlockSpec((2 * SUB, TF), halo), pl.BlockSpec((2 * SUB, TF), halo),
                  pl.BlockSpec((2, TF, D), lambda j, r: (0, j, 0)),
                  pl.BlockSpec((3, TF), lambda j, r: (0, j)), pl.BlockSpec((3, TF), lambda j, r: (0, NJ + j)),
                  pl.BlockSpec((TF, D), lambda j, r: (j, 0))],
        out_specs=(pl.BlockSpec((s, D), lambda j, r: (0, 0)),
                   pl.BlockSpec((2, TF, D), lambda j, r: (0, j, 0)),
                   pl.BlockSpec((TF, D), lambda j, r: (j, 0)),
                   pl.BlockSpec((SUB, TF), lambda j, r: (0, j)), pl.BlockSpec((SUB, TF), lambda j, r: (0, j))),
        out_shape=(jax.ShapeDtypeStruct((s, D), F32),
                   jax.ShapeDtypeStruct((2, DFF, D), BF16), jax.ShapeDtypeStruct((DFF, D), BF16),
                   jax.ShapeDtypeStruct((SUB, DFF), F32), jax.ShapeDtypeStruct((SUB, DFF), F32)),
        scratch_shapes=[pltpu.VMEM((2, SUB, TF), F32), pltpu.VMEM((2 * TF, D), F32), pltpu.VMEM((TF, D), F32)],
        compiler_params=_params(("arbitrary", "arbitrary"), 56),
    )(dy, h2, pre_g, pre_v, pre_g, pre_v, wup_t, cw, cw, wd)


def _adam(w, g, m, v):
    m = ADAM_B1 * m + (1.0 - ADAM_B1) * g
    v = ADAM_B2 * v + (1.0 - ADAM_B2) * (g * g)
    m_hat = m / (1.0 - ADAM_B1 ** ADAM_STEP)
    v_hat = v / (1.0 - ADAM_B2 ** ADAM_STEP)
    delta = -ADAM_LR * (m_hat / (jnp.sqrt(v_hat) + ADAM_EPS) + ADAM_WD * w)
    return delta, m, v


NCHIP = NDEV // 2


def _pair_add(mine, theirs, tr, name):
    _, _, rws, cols = mine.shape

    def body(a_ref, b_ref, o_ref):
        c = lax.axis_index("c")
        o_ref[0] = (a_ref[0, c].astype(F32) + b_ref[0].astype(F32)).astype(BF16)

    (out,), _ = _call(
        body, (mine, theirs), name=name, grid=(NCHIP, rws // tr),
        in_specs=[pl.BlockSpec((1, 2, tr, cols), lambda q, i: (q, 0, i, 0)),
                  pl.BlockSpec((1, tr, cols), lambda q, i: (q, i, 0))],
        out_specs=[pl.BlockSpec((1, tr, cols), lambda q, i: (q, i, 0))],
        out_shape=[jax.ShapeDtypeStruct((NCHIP, rws, cols), BF16)], vmem_mb=16)
    return out


def _adamw_sharded(parts, w, m, v, tr, name, hosted=None):
    rws, cols = w.shape
    n_parts = parts.shape[0]

    def body(p_ref, w_ref, m_ref, v_ref, g_ref, d_ref, mo_ref, vo_ref):
        g = p_ref[0].astype(F32)
        for q in range(1, n_parts):
            g = g + p_ref[q].astype(F32)
        g_ref[...] = g
        d_ref[...], mo_ref[...], vo_ref[...] = _adam(w_ref[...], g, m_ref[...], v_ref[...])

    blk = pl.BlockSpec((tr, cols), lambda i: (i, 0))
    o = jax.ShapeDtypeStruct((rws, cols), F32)
    outs, moved = _call(
        body, (parts, w, m, v), name=name, grid=(rws // tr,),
        in_specs=[pl.BlockSpec((n_parts, tr, cols), lambda i: (0, i, 0)), blk, blk, blk],
        out_specs=[blk, blk, blk, blk], out_shape=[o, o, o, o], vmem_mb=44 if tr > 256 else 24, hosted=hosted)
    return (outs, moved) if hosted else outs


def _adamw_ada(c_all, dmod_my, w, m, v):
    rws, cols = w.shape
    tr = 256

    def body(c_ref, dm_ref, w_ref, m_ref, v_ref, g_ref, d_ref, mo_ref, vo_ref):
        cv = c_ref[...]
        act = cv * _sigmoid(cv)
        g = _dot(act, dm_ref[...], TN, lax.Precision.HIGHEST)
        g_ref[...] = g
        d_ref[...], mo_ref[...], vo_ref[...] = _adam(w_ref[...], g, m_ref[...], v_ref[...])

    blk = pl.BlockSpec((tr, cols), lambda i: (i, 0))
    o = jax.ShapeDtypeStruct((rws, cols), F32)
    return pl.pallas_call(
        body, name="adamw_ada", grid=(rws // tr,),
        in_specs=[pl.BlockSpec((NDEV, tr), lambda i: (0, i)), _full_spec((NDEV, cols)), blk, blk, blk],
        out_specs=(blk, blk, blk, blk), out_shape=(o, o, o, o),
        compiler_params=_params(("parallel",), 32),
    )(c_all, dmod_my, w, m, v)


REP_ROWS = 16
ROW_N1, ROW_N2, ROW_LOSS, ROW_MISC = 6, 7, 8, 9
LANE_BF, LANE_GQ, LANE_GK = 0, 128, 256


def _adamw_small(rep_all, conv_all, wmv):
    n_ff = wmv[6][0].shape[1]

    def body(*refs):
        rep_ref, conv_ref = refs[:2]
        ins = refs[2:2 + 24]
        outs = refs[2 + 24:]
        loss_ref, outs = outs[0], outs[1:]
        g_rep = rep_ref[0]
        g_conv = conv_ref[0]
        for d in range(1, NDEV):
            g_rep = g_rep + rep_ref[d]
            g_conv = g_conv + conv_ref[d]
        loss_ref[...] = (0.5 / D) * jnp.sum(g_rep[ROW_LOSS:ROW_LOSS + 1, :], axis=-1, keepdims=True)
        grads = [
            None,
            g_rep[ROW_N1:ROW_N1 + 1, :],
            g_rep[ROW_MISC:ROW_MISC + 1, LANE_BF:LANE_BF + HEADS],
            g_rep[ROW_MISC:ROW_MISC + 1, LANE_GQ:LANE_GQ + DH],
            g_rep[ROW_MISC:ROW_MISC + 1, LANE_GK:LANE_GK + DH],
            g_rep[ROW_N2:ROW_N2 + 1, :],
            g_conv[0:3, 0:n_ff],
            g_conv[0:3, n_ff:n_ff + DH],
        ]
        for p in range(8):
            w_ref, m_ref, v_ref = ins[3 * p:3 * p + 3]
            g_ref, d_ref, mo_ref, vo_ref = outs[4 * p:4 * p + 4]
            if p == 0:
                for nmod in range(NMOD):
                    sl = slice(D * nmod, D * (nmod + 1))
                    g = g_rep[nmod:nmod + 1, :]
                    g_ref[:, sl] = g
                    d_ref[:, sl], mo_ref[:, sl], vo_ref[:, sl] = _adam(w_ref[:, sl], g, m_ref[:, sl], v_ref[:, sl])
            else:
                g = grads[p]
                g_ref[...] = g
                d_ref[...], mo_ref[...], vo_ref[...] = _adam(w_ref[...], g, m_ref[...], v_ref[...])

    flat = [a for trio in wmv for a in trio]
    out_shape = [jax.ShapeDtypeStruct((1, 1), F32)]
    for trio in wmv:
        out_shape += [jax.ShapeDtypeStruct(trio[0].shape, F32)] * 4
    return pl.pallas_call(
        body, name="adamw_small", out_shape=tuple(out_shape),
        compiler_params=_params(None, 32),
    )(rep_all, conv_all, *flat)


FG_FIRST = 3 * AW
N_IN = DIN // NDEV


def _w_in_runs():
    runs = []
    for d in range(NDEV):
        lo, hi = N_IN * d, N_IN * (d + 1)
        for a, b, shift in ((0, FG_FIRST, 0), (FG_FIRST, FG_FIRST + HEADS, DIN - HEADS - FG_FIRST),
                            (FG_FIRST + HEADS, DIN, -HEADS)):
            a, b = max(a, lo), min(b, hi)
            if a < b:
                runs.append((d, a - lo, a + shift, b - a))
    return runs


W_IN_ROWS = 256
N_IN_PAD = 512


def _identity(n):
    return (lax.broadcasted_iota(jnp.int32, (n, n), 0) == lax.broadcasted_iota(jnp.int32, (n, n), 1)).astype(BF16)


def _assemble_w_in(g_in, hosted):
    def body(g_ref, o_ref, t_ref):
        eye = _identity(W_IN_ROWS)
        shard = None
        for d, src, dst, width in _w_in_runs():
            if d != shard:
                t_ref[:, 0:N_IN] = _dot(eye, g_ref[d], NT).astype(BF16)
                shard = d
            o_ref[:, dst:dst + width] = t_ref[:, src:src + width]
        o_ref[:, DIN:DINP] = jnp.zeros((W_IN_ROWS, DINP - DIN), o_ref.dtype)

    (out,), moved = _call(
        body, (g_in,), name="assemble_w_in", grid=(D // W_IN_ROWS,),
        in_specs=[pl.BlockSpec((NDEV, N_IN, W_IN_ROWS), lambda i: (0, 0, i))],
        out_specs=[pl.BlockSpec((W_IN_ROWS, DINP), lambda i: (i, 0))],
        out_shape=[jax.ShapeDtypeStruct((D, DINP), g_in.dtype)],
        scratch_shapes=[pltpu.VMEM((W_IN_ROWS, N_IN_PAD), BF16)], vmem_mb=16, hosted=hosted)
    return out, moved


def _scatter_dw_in(dwp):
    def body(w_ref, o_ref, t_ref):
        eye = _identity(W_IN_ROWS)
        runs = _w_in_runs()
        for i, (d, src, dst, width) in enumerate(runs):
            t_ref[:, src:src + width] = w_ref[:, dst:dst + width]
            if i + 1 == len(runs) or runs[i + 1][0] != d:
                o_ref[d // 2, d % 2] = _dot(t_ref[:, 0:N_IN], eye, TN).astype(BF16)

    (out,), _ = _call(
        body, (dwp,), name="scatter_dw_in", grid=(D // W_IN_ROWS,),
        in_specs=[pl.BlockSpec((W_IN_ROWS, DINP), lambda i: (i, 0))],
        out_specs=[pl.BlockSpec((NCHIP, 2, N_IN, W_IN_ROWS), lambda i: (0, 0, 0, i))],
        out_shape=[jax.ShapeDtypeStruct((NCHIP, 2, N_IN, D), dwp.dtype)],
        scratch_shapes=[pltpu.VMEM((W_IN_ROWS, N_IN_PAD), BF16)], vmem_mb=16)
    return out


def kernel(x, c, w_ada, b_ada, norm1_g, w_in, b_forget, q_norm_g, k_norm_g, conv_mix_w, w_out, norm2_g, w_up, ffn_conv_w, w_down, loss_target, m_w_ada, m_b_ada, m_norm1_g, m_w_in, m_b_forget, m_q_norm_g, m_k_norm_g, m_conv_mix_w, m_w_out, m_norm2_g, m_w_up, m_ffn_conv_w, m_w_down, v_w_ada, v_b_ada, v_norm1_g, v_w_in, v_b_forget, v_q_norm_g, v_k_norm_g, v_conv_mix_w, v_w_out, v_norm2_g, v_w_up, v_ffn_conv_w, v_w_down):
    me = 4 * lax.axis_index("x") + 2 * lax.axis_index("y") + lax.axis_index("c")
    xs, tgt = x[0], loss_target[0]
    n_ada = w_ada.shape[2]
    n_ff = w_up.shape[2]

    conv_w = jnp.concatenate([ffn_conv_w[0], conv_mix_w[0]], axis=1)
    conv_w = jnp.concatenate([conv_w, jnp.zeros((SUB - 3, conv_w.shape[1]), F32)], axis=0)
    (g_in,) = _sequencer_exchange([(jnp.transpose(w_in[0]).astype(BF16), "ag2")], "gather_w_in", collective_id=3)
    c_all, conv_all = _exchange([(c.reshape(SUB, D // SUB), "ag"), (conv_w, "ag")], "exchange_c")
    g_in, w_out_b, w_up_b, w_down_b = lax.optimization_barrier(
        (g_in, w_out[0].astype(BF16), jnp.transpose(w_up[0]).astype(BF16), w_down[0].astype(BF16)))
    g_out, g_up, g_down = _sequencer_exchange(
        [(w_out_b, "ag2"), (w_up_b, "ag2"), (w_down_b, "ag2")], "gather_weights", collective_id=1)
    c_all = c_all.reshape(NDEV, D)
    cw_ffn = jnp.transpose(conv_all[:, :3, :n_ff], (1, 0, 2)).reshape(3, 2 * DFF)
    cw_mix = jnp.transpose(conv_all[:, :3, n_ff:], (1, 0, 2)).reshape(3, CW)

    b_my = lax.dynamic_slice(b_ada, (0, me * n_ada), (1, n_ada))
    mod_part = _ada_fwd(c_all, w_ada[0], b_my)
    (mod_rows,) = _exchange([(jnp.broadcast_to(mod_part[:, None, :], (NDEV, SUB, n_ada)), "a2a")], "exchange_mod")
    mod = mod_rows[:, 0, :].reshape(NMOD, D)
    mod = jnp.concatenate([mod, jnp.zeros((SUB - NMOD, D), F32)], axis=0)

    h = _norm_mod_fwd(xs, mod, norm1_g)
    w_in_p, _ = _assemble_w_in(g_in, None)
    proj = _mm(h, w_in_p, "nn", F32, 1024, 640, "proj_fwd")
    bf_pad = jnp.concatenate([b_forget, jnp.zeros((1, LANES - HEADS), F32)], axis=1)
    fcum = _fgate_fwd(proj, bf_pad)
    (qp, kp, vp), _ = _qkv_prep(proj, fcum, q_norm_g, k_norm_g, None)
    attn, lse, _ = _attn_fwd(qp, kp, vp, None)
    w_out_f = g_out.reshape(D, D)
    w_up_t = g_up.reshape(2, DFF, D)
    w_down_f = g_down.reshape(DFF, D)
    conv = _mixconv_fwd(proj, cw_mix)
    mixed = jnp.concatenate([attn, conv], axis=1).astype(BF16)
    z = _mm(mixed, w_out_f, "nn", F32, 1024, 1024, "out_fwd")
    x1, h2 = _resid_norm2(xs, z, mod, norm2_g)
    pre_g, pre_v, y = _ffn_fwd(h2, w_up_t, cw_ffn, w_down_f)
    dout, dy, vec_l = _loss_head(x1, y, tgt, mod)

    dh2, dwup_t, dwd, dcw_g, dcw_v = _ffn_bwd(dy, h2, pre_g, pre_v, w_up_t, cw_ffn, w_down_f)
    dx1, dz, vec_2 = _norm_mod_bwd(dh2, x1, dout, z, mod, norm2_g, 4, 2, "norm2_bwd")
    dwout = _mm(mixed, dz, "tn", BF16, 1024, 1024, "out_bwd_w")
    s_out = dwout.reshape(NCHIP, 2, D // NDEV, D)
    s_down = dwd.reshape(NCHIP, 2, DFF // NDEV, D)
    s_up = dwup_t.reshape(NCHIP, 2, n_ff, D)
    dmixed, (t_out, t_up, t_down) = _mm(dz, w_out_f, "nt", F32, 1024, 1024, "out_bwd_x",
                                        hosted=[(s_out, "pair"), (s_up, "pair"), (s_down, "pair")])
    c_out = _pair_add(s_out, t_out, 128, "pair_add_out")
    c_up = _pair_add(s_up, t_up, 176, "pair_add_up")
    c_down = _pair_add(s_down, t_down, 176, "pair_add_down")
    dxin, dbg, dcg, dcw_mix = _mixconv_bwd(dmixed, proj, cw_mix)
    (dqp, dkp, dvp), (p_up, p_down, p_out) = _attn_bwd(
        qp, kp, vp, dmixed, attn, lse, [(c_up, "chips"), (c_down, "chips"), (c_out, "chips")])
    (dq, dk, dvb, dfcol, vec_qk), _ = _qkv_post(dqp, dkp, dvp, proj, q_norm_g, k_norm_g, None)
    dfg, vec_bf = _fgate_bwd(dfcol, proj, bf_pad)
    dproj = jnp.concatenate([dq, dk, dvb, dxin, dbg, dcg, dfg], axis=1)
    dwin_p = _mm(h, dproj, "tn", BF16, 1024, 640, "proj_bwd_w")
    s_in = _scatter_dw_in(dwin_p).reshape(NDEV, N_IN, D)
    (p_in,) = _sequencer_exchange([(s_in, "a2a")], "scatter_dw_in_partials", collective_id=2, all_peers=True)
    dh = _mm(dproj, w_in_p, "nt", F32, 1024, 512, "proj_bwd_x", vmem_mb=36)
    grad_x, vec_1 = _norm_mod_bwd(dh, xs, dx1, None, mod, norm1_g, 1, None, "norm1_bwd")

    misc = jnp.zeros((1, D), F32)
    misc = lax.dynamic_update_slice(misc, vec_bf[0:1, :HEADS], (0, LANE_BF))
    misc = lax.dynamic_update_slice(misc, vec_qk[0:1, :DH], (0, LANE_GQ))
    misc = lax.dynamic_update_slice(misc, vec_qk[1:2, :DH], (0, LANE_GK))
    rep = jnp.concatenate([
        vec_1[0:1], vec_1[1:2], vec_2[3:4], vec_2[0:1], vec_2[1:2], vec_l[0:1],
        vec_1[2:3], vec_2[2:3], vec_l[1:2], misc, jnp.zeros((REP_ROWS - 10, D), F32)], axis=0)
    dcw_ffn = jnp.concatenate([dcw_g, dcw_v], axis=1).reshape(SUB, NDEV, n_ff)
    dcw_all = jnp.concatenate([jnp.transpose(dcw_ffn, (1, 0, 2)),
                               jnp.transpose(dcw_mix.reshape(SUB, NDEV, DH), (1, 0, 2))], axis=2)
    r_out, (rep_all, conv_parts) = _adamw_sharded(p_out, w_out[0], m_w_out[0], v_w_out[0], 128, "adamw_out",
                                                  hosted=[(rep, "ag"), (dcw_all, "a2a")])
    dmod_my = lax.dynamic_slice(rep_all[:, :NMOD, :].reshape(NDEV, NMOD * D), (0, me * n_ada), (NDEV, n_ada))
    r_ada = _adamw_ada(c_all, dmod_my, w_ada[0], m_w_ada[0], v_w_ada[0])
    r_in = _adamw_sharded(p_in, jnp.transpose(w_in[0]), jnp.transpose(m_w_in[0]), jnp.transpose(v_w_in[0]), N_IN,
                          "adamw_in")
    r_in = tuple(jnp.transpose(a) for a in r_in)
    r_up = _adamw_sharded(p_up, jnp.transpose(w_up[0]), jnp.transpose(m_w_up[0]), jnp.transpose(v_w_up[0]), 176,
                          "adamw_up")
    r_up = tuple(jnp.transpose(a) for a in r_up)
    r_down = _adamw_sharded(p_down, w_down[0], m_w_down[0], v_w_down[0], 176, "adamw_down")
    small = _adamw_small(rep_all, conv_parts, [
        [b_ada, m_b_ada, v_b_ada], [norm1_g, m_norm1_g, v_norm1_g], [b_forget, m_b_forget, v_b_forget],
        [q_norm_g, m_q_norm_g, v_q_norm_g], [k_norm_g, m_k_norm_g, v_k_norm_g], [norm2_g, m_norm2_g, v_norm2_g],
        [ffn_conv_w[0], m_ffn_conv_w[0], v_ffn_conv_w[0]], [conv_mix_w[0], m_conv_mix_w[0], v_conv_mix_w[0]]])
    loss = small[0].reshape(())
    r_bada, r_n1, r_bf, r_gq, r_gk, r_n2, r_cf, r_cm = [small[1 + 4 * p:5 + 4 * p] for p in range(8)]
    lead = lambda t: tuple(a[None] for a in t)
    per_w = [lead(r_ada), r_bada, r_n1, lead(r_in), r_bf, r_gq, r_gk, lead(r_cm), lead(r_out), r_n2,
             lead(r_up), lead(r_cf), lead(r_down)]
    outs = [loss, grad_x[None]]
    for field in range(4):
        outs += [t[field] for t in per_w]
    return tuple(outs)
```

```python
import functools

import jax
import jax.numpy as jnp
import numpy as np
from jax import lax
from jax.experimental import pallas as pl
from jax.experimental.pallas import tpu as pltpu
from jax.experimental.pallas import tpu_sc as plsc

F32 = jnp.float32
BF16 = jnp.bfloat16

NDEV = 8
D = 1024
HEADS = 8
DH = 64
AW = 512
CW = 512
DFF = 2816
DIN = 3080
DINP = 3200
NMOD = 6
EPS = 1e-6
QK_SCALE = 0.125
LANES = 128
SUB = 8

ADAM_LR = 0.001
ADAM_B1 = 0.9
ADAM_B2 = 0.999
ADAM_EPS = 1e-08
ADAM_WD = 0.01
ADAM_STEP = 10

MESH = pl.DeviceIdType.MESH
ANY = pl.BlockSpec(memory_space=pl.ANY)

NN = (((1,), (0,)), ((), ()))
NT = (((1,), (1,)), ((), ()))
TN = (((0,), (0,)), ((), ()))


def _dot(a, b, dims=NN, precision=None):
    return lax.dot_general(a, b, dims, precision=precision, preferred_element_type=F32)


def _params(sem=None, vmem_mb=None):
    kw = {}
    if sem is not None:
        kw["dimension_semantics"] = sem
    if vmem_mb is not None:
        kw["vmem_limit_bytes"] = vmem_mb * 1024 * 1024
    return pltpu.CompilerParams(**kw)


def _sigmoid(x):
    return 0.5 * jnp.tanh(0.5 * x) + 0.5


class _Exchange:
    def __init__(self, items):
        self.arrays = [pltpu.with_memory_space_constraint(a, pltpu.HBM) for a, _ in items]
        self.modes = [m for _, m in items]
        self.n = len(items)
        self.out_shape = []
        for a, m in items:
            sh = {"ag": (NDEV,) + a.shape, "ag2": (NDEV,) + a.shape, "pair": a.shape[:1] + a.shape[2:]}.get(m, a.shape)
            self.out_shape.append(jax.ShapeDtypeStruct(sh, a.dtype))
        self.scratch = [pltpu.SemaphoreType.DMA((self.n, NDEV - 1)), pltpu.SemaphoreType.DMA((self.n, NDEV - 1)),
                        pltpu.SemaphoreType.DMA((self.n,))]

    def _plan(self, srcs, outs, sems):
        send_sems, recv_sems, loc_sems = sems
        x, y, c = lax.axis_index("x"), lax.axis_index("y"), lax.axis_index("c")
        me, my_chip = 4 * x + 2 * y + c, 2 * x + y
        sib = (x, y, 1 - c)
        local, first, landed, forwards, arrivals = [], [], [], [], []

        def remote(a, k, src, dst, to):
            return pltpu.make_async_remote_copy(src_ref=src, dst_ref=dst, send_sem=send_sems.at[a, k],
                                                recv_sem=recv_sems.at[a, k], device_id=to, device_id_type=MESH)

        for a, mode in enumerate(self.modes):
            src, out = srcs[a], outs[a]
            if mode in ("ag", "a2a"):
                piece = (lambda slot, src=src: src) if mode == "ag" else (lambda slot, src=src: src.at[slot])
                local.append(pltpu.make_async_copy(piece(me), out.at[me], loc_sems.at[a]))
                for r in range(1, NDEV):
                    px = 1 - x if (r >> 2) & 1 else x
                    py = 1 - y if (r >> 1) & 1 else y
                    pc = 1 - c if r & 1 else c
                    pidx = 4 * px + 2 * py + pc
                    first.append(remote(a, r - 1, piece(pidx), out.at[me], (px, py, pc)))
                    arrivals.append(remote(a, r - 1, piece(pidx), out.at[pidx], (px, py, pc)))
            elif mode == "ag2":
                local.append(pltpu.make_async_copy(src, out.at[me], loc_sems.at[a]))
                first.append(remote(a, 0, src, out.at[me], sib))
                arrivals.append(remote(a, 0, src, out.at[me + 1 - 2 * c], sib))
                for j, (px, py) in enumerate([(1 - x, y), (x, 1 - y), (1 - x, 1 - y)]):
                    theirs = out.at[4 * px + 2 * py + c]
                    first.append(remote(a, 1 + j, src, out.at[me], (px, py, c)))
                    landed.append(remote(a, 1 + j, src, theirs, (px, py, c)))
                    forwards.append(remote(a, 4 + j, theirs, theirs, sib))
                    arrivals.append(remote(a, 4 + j, src, out.at[4 * px + 2 * py + 1 - c], sib))
            elif mode == "pair":
                for q in range(NDEV // 2):
                    first.append(remote(a, q, src.at[q, 1 - c], out.at[q], sib))
                    arrivals.append(remote(a, q, src.at[q, 1 - c], out.at[q], sib))
            else:
                assert mode == "chips", mode
                local.append(pltpu.make_async_copy(src.at[my_chip], out.at[my_chip], loc_sems.at[a]))
                for j, (px, py) in enumerate([(1 - x, y), (x, 1 - y), (1 - x, 1 - y)]):
                    q = 2 * px + py
                    first.append(remote(a, 1 + j, src.at[q], out.at[my_chip], (px, py, c)))
                    arrivals.append(remote(a, 1 + j, src.at[q], out.at[q], (px, py, c)))
        return local, first, landed, forwards, arrivals

    def start(self, srcs, outs, sems):
        local, first, _, _, _ = self._plan(srcs, outs, sems)
        for cp in local + first:
            cp.start()

    def wait(self, srcs, outs, sems):
        local, first, landed, forwards, arrivals = self._plan(srcs, outs, sems)
        for cp, fwd in zip(landed, forwards):
            cp.wait_recv()
            fwd.start()
        for cp in arrivals:
            cp.wait_recv()
        for cp in first + forwards:
            cp.wait_send()
        for cp in local:
            cp.wait()


def _exchange(items, name):
    ex = _Exchange(items)
    n = ex.n

    def body(*refs):
        srcs, outs, sems = refs[:n], refs[n:2 * n], refs[2 * n:]
        ex.start(srcs, outs, sems)
        ex.wait(srcs, outs, sems)

    return pl.pallas_call(
        body, name=name,
        out_shape=tuple(ex.out_shape),
        in_specs=[ANY] * n, out_specs=tuple([ANY] * n),
        scratch_shapes=ex.scratch,
        compiler_params=pltpu.CompilerParams(has_side_effects=True),
    )(*ex.arrays)


def _sequencer_exchange(items, name, collective_id, all_peers=False):
    ex = _Exchange(items)
    srcs = [jax.new_ref(a, memory_space=pltpu.MemorySpace.HBM) for a in ex.arrays]
    outs = [jax.empty_ref(sh, memory_space=pltpu.MemorySpace.HBM) for sh in ex.out_shape]

    @pl.kernel(mesh=plsc.ScalarSubcoreMesh(axis_name="sequencer", num_cores=1), name=name,
               scratch_types=tuple(ex.scratch), compiler_params=pltpu.CompilerParams(collective_id=collective_id))
    def launch(send_sems, recv_sems, loc_sems):
        x, y, c = lax.axis_index("x"), lax.axis_index("y"), lax.axis_index("c")
        barrier = pltpu.get_barrier_semaphore()
        peers = [(x, y, 1 - c), (1 - x, y, c), (x, 1 - y, c), (1 - x, 1 - y, c)]
        if all_peers:
            peers += [(1 - x, y, 1 - c), (x, 1 - y, 1 - c), (1 - x, 1 - y, 1 - c)]
        for peer in peers:
            pl.semaphore_signal(barrier, inc=1, device_id=peer, device_id_type=MESH)
        pl.semaphore_wait(barrier, len(peers))
        sems = (send_sems, recv_sems, loc_sems)
        ex.start(srcs, outs, sems)
        ex.wait(srcs, outs, sems)

    launch()
    return [o[...] for o in outs]


def _call(body, inputs, *, name, grid, in_specs, out_specs, out_shape, scratch_shapes=(), vmem_mb=None, hosted=None):
    out_specs, out_shape, scratch_shapes = tuple(out_specs), tuple(out_shape), list(scratch_shapes)
    if not hosted:
        res = pl.pallas_call(
            body, name=name, grid=grid, in_specs=list(in_specs), out_specs=out_specs, out_shape=out_shape,
            scratch_shapes=scratch_shapes, compiler_params=_params(("arbitrary",) * len(grid), vmem_mb),
        )(*inputs)
        return tuple(res), ()
    ex = _Exchange(hosted)
    n, n_in, n_out, n_scr = ex.n, len(inputs), len(out_shape), len(scratch_shapes)

    def hosting_body(*refs):
        ins, srcs = refs[:n_in], refs[n_in:n_in + n]
        outs, landing = refs[n_in + n:n_in + n + n_out], refs[n_in + n + n_out:n_in + 2 * n + n_out]
        scratch, sems = refs[n_in + 2 * n + n_out:n_in + 2 * n + n_out + n_scr], refs[n_in + 2 * n + n_out + n_scr:]
        first = functools.reduce(jnp.logical_and, [pl.program_id(d) == 0 for d in range(len(grid))])
        last = functools.reduce(jnp.logical_and, [pl.program_id(d) == grid[d] - 1 for d in range(len(grid))])

        @pl.when(first)
        def _():
            ex.start(srcs, landing, sems)

        body(*ins, *outs, *scratch)

        @pl.when(last)
        def _():
            ex.wait(srcs, landing, sems)

    res = pl.pallas_call(
        hosting_body, name=name, grid=grid,
        in_specs=list(in_specs) + [ANY] * n, out_specs=out_specs + tuple([ANY] * n),
        out_shape=out_shape + tuple(ex.out_shape), scratch_shapes=scratch_shapes + ex.scratch,
        compiler_params=_params(("arbitrary",) * len(grid), vmem_mb),
    )(*inputs, *ex.arrays)
    return tuple(res[:n_out]), tuple(res[n_out:])


def _mm(a, b, mode, out_dtype, tm, tn, name, hosted=None, vmem_mb=24):
    if mode == "nn":
        (m, k), n = a.shape, b.shape[1]
        a_spec = pl.BlockSpec((tm, k), lambda i, j: (i, 0))
        b_spec = pl.BlockSpec((k, tn), lambda i, j: (0, j))
        dims = NN
    elif mode == "nt":
        (m, k), n = a.shape, b.shape[0]
        a_spec = pl.BlockSpec((tm, k), lambda i, j: (i, 0))
        b_spec = pl.BlockSpec((tn, k), lambda i, j: (j, 0))
        dims = NT
    else:
        (k, m), n = a.shape, b.shape[1]
        a_spec = pl.BlockSpec((k, tm), lambda i, j: (0, i))
        b_spec = pl.BlockSpec((k, tn), lambda i, j: (0, j))
        dims = TN
    assert m % tm == 0 and n % tn == 0, (m, n, tm, tn)

    def body(a_ref, b_ref, o_ref):
        o_ref[...] = _dot(a_ref[...], b_ref[...], dims).astype(o_ref.dtype)

    (out,), moved = _call(
        body, (a, b), name=name, grid=(m // tm, n // tn),
        in_specs=[a_spec, b_spec], out_specs=[pl.BlockSpec((tm, tn), lambda i, j: (i, j))],
        out_shape=[jax.ShapeDtypeStruct((m, n), out_dtype)], vmem_mb=vmem_mb, hosted=hosted)
    return (out, moved) if hosted else out


def _shift_down(x, k, fill):
    y = pltpu.roll(x, k, 0)
    row = lax.broadcasted_iota(jnp.int32, (SUB, x.shape[1]), 0)
    head = y[0:SUB, :]
    for t in range(k):
        head = jnp.where(row == t, fill[t], head)
    return jnp.concatenate([head, y[SUB:, :]], axis=0)


def _shift_up(x, k, fill):
    n = x.shape[0]
    y = pltpu.roll(x, n - k, 0)
    row = lax.broadcasted_iota(jnp.int32, (SUB, x.shape[1]), 0)
    tail = y[n - SUB:, :]
    for t in range(k):
        tail = jnp.where(row == SUB - k + t, fill[t], tail)
    return jnp.concatenate([y[:n - SUB, :], tail], axis=0)


def _conv_taps(x, halo, w):
    if halo is None:
        f1, f2 = [0.0], [0.0, 0.0]
    else:
        f1, f2 = [halo[7:8, :]], [halo[6:7, :], halo[7:8, :]]
    s1 = _shift_down(x, 1, f1)
    s2 = _shift_down(x, 2, f2)
    u = w[2:3, :] * x + w[1:2, :] * s1 + w[0:1, :] * s2
    return u, s1, s2


def _conv_taps_t(du, nxt, w):
    if nxt is None:
        f1, f2 = [0.0], [0.0, 0.0]
    else:
        f1, f2 = [nxt[0:1, :]], [nxt[0:1, :], nxt[1:2, :]]
    return w[2:3, :] * du + w[1:2, :] * _shift_up(du, 1, f1) + w[0:1, :] * _shift_up(du, 2, f2)


def _ada_fwd(c_all, w_ada, b_my):
    def body(c_ref, w_ref, b_ref, o_ref):
        cv = c_ref[...]
        act = cv * _sigmoid(cv)
        o_ref[...] = _dot(act, w_ref[...], NN, lax.Precision.HIGHEST) + b_ref[...]

    return pl.pallas_call(
        body, name="ada_fwd",
        out_shape=jax.ShapeDtypeStruct((NDEV, w_ada.shape[1]), F32),
        compiler_params=_params(None, 32),
    )(c_all, w_ada, b_my)


TR = 256
TRE = 512


def _row_spec(width, col=0, rows=TR):
    return pl.BlockSpec((rows, width), lambda i, col=col: (i, col))


def _erow(width):
    return _row_spec(width, rows=TRE)


def _full_spec(shape):
    return pl.BlockSpec(shape, lambda i: (0,) * len(shape))


def _norm_mod_fwd(x, mod, g):
    s = x.shape[0]

    def body(x_ref, mod_ref, g_ref, h_ref):
        xv = x_ref[...]
        r = lax.rsqrt(jnp.mean(xv * xv, axis=-1, keepdims=True) + EPS)
        nrm = xv * r * g_ref[...]
        h_ref[...] = (nrm * (1.0 + mod_ref[1:2, :]) + mod_ref[0:1, :]).astype(BF16)

    return pl.pallas_call(
        body, name="norm1_fwd", grid=(s // TRE,),
        in_specs=[_erow(D), _full_spec((SUB, D)), _full_spec((1, D))],
        out_specs=_erow(D), out_shape=jax.ShapeDtypeStruct((s, D), BF16),
        compiler_params=_params(("parallel",), 16),
    )(x, mod, g)


SLAB = 2 * DH
AUG_F, AUG_ONE, AUG_LSE = 0, 3, 6


def _split3(x):
    hi = x.astype(BF16).astype(F32)
    r1 = x - hi
    mid = r1.astype(BF16).astype(F32)
    return hi, mid, r1 - mid


def _lanes3(lane, first, pieces, other):
    out = other
    for k in range(3):
        out = jnp.where(lane == first + k, pieces[k], out)
    return out


def _aug_placement():
    eq = np.zeros((3 * LANES, HEADS * SLAB), np.float32)
    ek = np.zeros((3 * LANES, HEADS * SLAB), np.float32)
    ones = np.zeros((SUB, HEADS * SLAB), np.float32)
    for h in range(HEADS):
        aug = SLAB * h + DH
        for k in range(3):
            eq[LANES * k + h, aug + AUG_F + k] = 1.0
            ek[LANES * k + h, aug + AUG_ONE + k] = -1.0
            ones[0, aug + AUG_ONE + k] = 1.0
            ones[1, aug + AUG_F + k] = ones[1, aug + AUG_LSE + k] = 1.0
            ones[2, aug + k] = 1.0
    return jnp.asarray(eq, BF16), jnp.asarray(ek, BF16), jnp.asarray(ones)


def _qkv_prep(proj, fcum, gq, gk, hosted):
    s = proj.shape[0]

    def body(q_ref, k_ref, v_ref, f_ref, gq_ref, gk_ref, eq_ref, ek_ref, ones_ref, qo_ref, ko_ref, vo_ref):
        f3 = jnp.concatenate(_split3(f_ref[...]), axis=1).astype(BF16)
        qo_ref[...] = (_dot(f3, eq_ref[...]) + ones_ref[0:1, :]).astype(BF16)
        ko_ref[...] = (_dot(f3, ek_ref[...]) + ones_ref[1:2, :]).astype(BF16)
        vo_ref[...] = jnp.broadcast_to(ones_ref[2:3, :], vo_ref.shape).astype(BF16)
        for h in range(HEADS):
            sl = slice(DH * h, DH * (h + 1))
            lo = slice(SLAB * h, SLAB * h + DH)
            qh = q_ref[:, sl]
            r = lax.rsqrt(jnp.mean(qh * qh, axis=-1, keepdims=True) + EPS)
            qo_ref[:, lo] = (qh * r * gq_ref[...] * QK_SCALE).astype(BF16)
            kh = k_ref[:, sl]
            r = lax.rsqrt(jnp.mean(kh * kh, axis=-1, keepdims=True) + EPS)
            ko_ref[:, lo] = (kh * r * gk_ref[...]).astype(BF16)
            vo_ref[:, lo] = v_ref[:, sl].astype(BF16)

    eq, ek, ones = _aug_placement()
    o = jax.ShapeDtypeStruct((s, HEADS * SLAB), BF16)
    wide = _row_spec(HEADS * SLAB)
    return _call(
        body, (proj, proj, proj, fcum, gq, gk, eq, ek, ones), name="qkv_prep", grid=(s // TR,),
        in_specs=[_row_spec(AW, 0), _row_spec(AW, 1), _row_spec(AW, 2), _row_spec(LANES),
                  _full_spec((1, DH)), _full_spec((1, DH)), _full_spec(eq.shape), _full_spec(ek.shape),
                  _full_spec(ones.shape)],
        out_specs=[wide, wide, wide], out_shape=[o, o, o], vmem_mb=16, hosted=hosted)


FG_BLOCK = (3 * AW + 3 * CW) // LANES


def _fgate_fwd(proj, bf_pad):
    s = proj.shape[0]

    def body(fg_ref, b_ref, o_ref, carry_ref):
        i = pl.program_id(0)

        @pl.when(i == 0)
        def _():
            carry_ref[...] = jnp.zeros_like(carry_ref)

        z = fg_ref[...] + b_ref[...]
        logf = jnp.minimum(z, 0.0) - jnp.log1p(jnp.exp(-jnp.abs(z)))
        row = lax.broadcasted_iota(jnp.int32, (TR, TR), 0)
        col = lax.broadcasted_iota(jnp.int32, (TR, TR), 1)
        tri = (col <= row).astype(F32)
        cs = _dot(tri, logf, NN, lax.Precision.HIGHEST) + carry_ref[0:1, :]
        o_ref[...] = cs
        carry_ref[...] = jnp.broadcast_to(cs[TR - 1:TR, :], carry_ref.shape)

    return pl.pallas_call(
        body, name="fgate_fwd", grid=(s // TR,),
        in_specs=[_row_spec(LANES, FG_BLOCK), _full_spec((1, LANES))],
        out_specs=_row_spec(LANES), out_shape=jax.ShapeDtypeStruct((s, LANES), F32),
        scratch_shapes=[pltpu.VMEM((SUB, LANES), F32)],
        compiler_params=_params(("arbitrary",)),
    )(proj, bf_pad)


def _fgate_bwd(dfcol, proj, bf_pad):
    s = proj.shape[0]
    nb = s // TR

    def body(df_ref, fg_ref, b_ref, o_ref, db_ref, carry_ref):
        i = pl.program_id(0)

        @pl.when(i == 0)
        def _():
            carry_ref[...] = jnp.zeros_like(carry_ref)
            db_ref[...] = jnp.zeros_like(db_ref)

        row = lax.broadcasted_iota(jnp.int32, (TR, TR), 0)
        col = lax.broadcasted_iota(jnp.int32, (TR, TR), 1)
        tri = (col >= row).astype(F32)
        dlogf = _dot(tri, df_ref[...], NN, lax.Precision.HIGHEST) + carry_ref[0:1, :]
        carry_ref[...] = jnp.broadcast_to(dlogf[0:1, :], carry_ref.shape)
        z = fg_ref[...] + b_ref[...]
        dfg = dlogf * _sigmoid(-z)
        o_ref[...] = dfg.astype(BF16)
        db_ref[0:1, :] += jnp.sum(dfg, axis=0, keepdims=True)

    rev = lambda col: pl.BlockSpec((TR, LANES), lambda i, col=col: (nb - 1 - i, col))
    return pl.pallas_call(
        body, name="fgate_bwd", grid=(nb,),
        in_specs=[rev(0), rev(FG_BLOCK), _full_spec((1, LANES))],
        out_specs=(rev(0), _full_spec((SUB, LANES))),
        out_shape=(jax.ShapeDtypeStruct((s, LANES), BF16), jax.ShapeDtypeStruct((SUB, LANES), F32)),
        scratch_shapes=[pltpu.VMEM((SUB, LANES), F32)],
        compiler_params=_params(("arbitrary",)),
    )(dfcol, proj, bf_pad)


def _resid_norm2(x, z, mod, g):
    s = x.shape[0]

    def body(x_ref, z_ref, mod_ref, g_ref, x1_ref, h_ref):
        x1 = x_ref[...] + mod_ref[2:3, :] * z_ref[...]
        x1_ref[...] = x1
        r = lax.rsqrt(jnp.mean(x1 * x1, axis=-1, keepdims=True) + EPS)
        nrm = x1 * r * g_ref[...]
        h_ref[...] = (nrm * (1.0 + mod_ref[4:5, :]) + mod_ref[3:4, :]).astype(BF16)

    return pl.pallas_call(
        body, name="resid_norm2", grid=(s // TRE,),
        in_specs=[_erow(D), _erow(D), _full_spec((SUB, D)), _full_spec((1, D))],
        out_specs=(_erow(D), _erow(D)),
        out_shape=(jax.ShapeDtypeStruct((s, D), F32), jax.ShapeDtypeStruct((s, D), BF16)),
        compiler_params=_params(("parallel",), 24),
    )(x, z, mod, g)


def _loss_head(x1, y, tgt, mod):
    s = x1.shape[0]

    def body(x1_ref, y_ref, t_ref, mod_ref, dout_ref, dy_ref, vec_ref):
        @pl.when(pl.program_id(0) == 0)
        def _():
            vec_ref[...] = jnp.zeros_like(vec_ref)

        yv = y_ref[...]
        g2 = mod_ref[5:6, :]
        diff = x1_ref[...] + g2 * yv - t_ref[...]
        dout = diff * (1.0 / D)
        dout_ref[...] = dout
        dy_ref[...] = (g2 * dout).astype(BF16)
        vec_ref[0:1, :] += jnp.sum(dout * yv, axis=0, keepdims=True)
        vec_ref[1:2, :] += jnp.sum(diff * diff, axis=0, keepdims=True)

    return pl.pallas_call(
        body, name="loss_head", grid=(s // TRE,),
        in_specs=[_erow(D), _erow(D), _erow(D), _full_spec((SUB, D))],
        out_specs=(_erow(D), _erow(D), _full_spec((SUB, D))),
        out_shape=(jax.ShapeDtypeStruct((s, D), F32), jax.ShapeDtypeStruct((s, D), BF16),
                   jax.ShapeDtypeStruct((SUB, D), F32)),
        compiler_params=_params(("arbitrary",), 24),
    )(x1, y, tgt, mod)


def _norm_mod_bwd(dh, xin, dres, zin, mod, g, scale_row, gate_row, name):
    s = dh.shape[0]
    with_gate = gate_row is not None

    def body(*refs):
        if with_gate:
            dh_ref, x_ref, dres_ref, z_ref, mod_ref, g_ref, dx_ref, dz_ref, vec_ref = refs
        else:
            dh_ref, x_ref, dres_ref, mod_ref, g_ref, dx_ref, vec_ref = refs

        @pl.when(pl.program_id(0) == 0)
        def _():
            vec_ref[...] = jnp.zeros_like(vec_ref)

        xv = x_ref[...]
        dhv = dh_ref[...]
        gv = g_ref[...]
        r = lax.rsqrt(jnp.mean(xv * xv, axis=-1, keepdims=True) + EPS)
        xh = xv * r
        dn = dhv * (1.0 + mod_ref[scale_row:scale_row + 1, :])
        dxh = dn * gv
        dx = dres_ref[...] + r * (dxh - xh * jnp.mean(dxh * xh, axis=-1, keepdims=True))
        dx_ref[...] = dx
        vec_ref[0:1, :] += jnp.sum(dhv, axis=0, keepdims=True)
        vec_ref[1:2, :] += jnp.sum(dhv * (xh * gv), axis=0, keepdims=True)
        vec_ref[2:3, :] += jnp.sum(dn * xh, axis=0, keepdims=True)
        if with_gate:
            dz_ref[...] = (mod_ref[gate_row:gate_row + 1, :] * dx).astype(BF16)
            vec_ref[3:4, :] += jnp.sum(dx * z_ref[...], axis=0, keepdims=True)

    ins = [dh, xin, dres] + ([zin] if with_gate else []) + [mod, g]
    in_specs = [_erow(D)] * (4 if with_gate else 3) + [_full_spec((SUB, D)), _full_spec((1, D))]
    out_specs = [_erow(D)] + ([_erow(D)] if with_gate else []) + [_full_spec((SUB, D))]
    out_shape = [jax.ShapeDtypeStruct((s, D), F32)] + ([jax.ShapeDtypeStruct((s, D), BF16)] if with_gate else []) \
        + [jax.ShapeDtypeStruct((SUB, D), F32)]
    outs, _ = _call(body, ins, name=name, grid=(s // TRE,), in_specs=in_specs, out_specs=out_specs,
                    out_shape=out_shape, vmem_mb=32)
    return outs


XIN_BLOCK = 3 * AW // LANES
BG_BLOCK = XIN_BLOCK + CW // LANES
CG_BLOCK = BG_BLOCK + CW // LANES


def _seq_spec(s, first_block):
    return pl.BlockSpec((s, LANES), lambda j, fb=first_block: (0, fb + j))


def _mixconv_fwd(proj, w):
    s = proj.shape[0]

    def body(xin_ref, bg_ref, cg_ref, w_ref, o_ref):
        cx = cg_ref[...] * xin_ref[...]
        cv, _, _ = _conv_taps(cx, None, w_ref[...])
        o_ref[...] = bg_ref[...] * cv

    return pl.pallas_call(
        body, name="mixconv_fwd", grid=(CW // LANES,),
        in_specs=[_seq_spec(s, XIN_BLOCK), _seq_spec(s, BG_BLOCK), _seq_spec(s, CG_BLOCK),
                  pl.BlockSpec((3, LANES), lambda j: (0, j))],
        out_specs=_seq_spec(s, 0), out_shape=jax.ShapeDtypeStruct((s, CW), F32),
        compiler_params=_params(("parallel",), 32),
    )(proj, proj, proj, w)


def _mixconv_bwd(dmixed, proj, w):
    s = proj.shape[0]

    def body(d_ref, xin_ref, bg_ref, cg_ref, w_ref, dxin_ref, dbg_ref, dcg_ref, dw_ref):
        wv = w_ref[...]
        xin, cg, dconv = xin_ref[...], cg_ref[...], d_ref[...]
        cx = cg * xin
        cv, s1, s2 = _conv_taps(cx, None, wv)
        dbg_ref[...] = (dconv * cv).astype(BF16)
        dcv = dconv * bg_ref[...]
        dw_ref[...] = jnp.zeros_like(dw_ref)
        dw_ref[0:1, :] = jnp.sum(dcv * s2, axis=0, keepdims=True)
        dw_ref[1:2, :] = jnp.sum(dcv * s1, axis=0, keepdims=True)
        dw_ref[2:3, :] = jnp.sum(dcv * cx, axis=0, keepdims=True)
        dcx = _conv_taps_t(dcv, None, wv)
        dcg_ref[...] = (dcx * xin).astype(BF16)
        dxin_ref[...] = (dcx * cg).astype(BF16)

    o = jax.ShapeDtypeStruct((s, CW), BF16)
    return pl.pallas_call(
        body, name="mixconv_bwd", grid=(CW // LANES,),
        in_specs=[_seq_spec(s, AW // LANES), _seq_spec(s, XIN_BLOCK), _seq_spec(s, BG_BLOCK), _seq_spec(s, CG_BLOCK),
                  pl.BlockSpec((3, LANES), lambda j: (0, j))],
        out_specs=(_seq_spec(s, 0), _seq_spec(s, 0), _seq_spec(s, 0), pl.BlockSpec((SUB, LANES), lambda j: (0, j))),
        out_shape=(o, o, o, jax.ShapeDtypeStruct((SUB, CW), F32)),
        compiler_params=_params(("parallel",), 32),
    )(dmixed, proj, proj, proj, w)


TA = 512
NEG = -1e30


def _causal_mask():
    row = lax.broadcasted_iota(jnp.int32, (TA, TA), 0)
    col = lax.broadcasted_iota(jnp.int32, (TA, TA), 1)
    return col <= row


def _attn_fwd(qp, kp, vp, hosted):
    s = qp.shape[0]
    nq = s // TA

    def body(q_ref, k_ref, v_ref, o_ref, lse_ref):
        i = pl.program_id(1)
        slabs = [slice(SLAB * hh, SLAB * (hh + 1)) for hh in range(2)]
        q = [q_ref[:, sl] for sl in slabs]

        def block(j, carry, masked):
            keys = pl.ds(pl.multiple_of(j * TA, TA), TA)
            ms, acc = carry
            m_out, parts = [], []
            for hh in range(2):
                sc = _dot(q[hh], k_ref[keys, slabs[hh]], NT)
                if masked:
                    sc = jnp.where(_causal_mask(), sc, NEG)
                m_new = jnp.maximum(ms[hh], jnp.max(sc, axis=-1, keepdims=True))
                p = jnp.exp(sc - m_new)
                parts.append(jnp.exp(ms[hh] - m_new) * acc[:, slabs[hh]] + _dot(p.astype(BF16), v_ref[keys, slabs[hh]]))
                m_out.append(m_new)
            return tuple(m_out), jnp.concatenate(parts, axis=1)

        init = ((jnp.full((TA, 1), NEG, F32), jnp.full((TA, 1), NEG, F32)), jnp.zeros((TA, 2 * SLAB), F32))
        carry = lax.fori_loop(0, i, lambda j, cr: block(j, cr, False), init)
        ms, acc = block(i, carry, True)
        for hh in range(2):
            l = acc[:, SLAB * hh + DH:SLAB * hh + DH + 1]
            o_ref[:, DH * hh:DH * (hh + 1)] = acc[:, SLAB * hh:SLAB * hh + DH] / l
            lse_ref[0, :, hh:hh + 1] = ms[hh] + jnp.log(l)

    (o, lse), moved = _call(
        body, (qp, kp, vp), name="attn_fwd", grid=(HEADS // 2, nq),
        in_specs=[pl.BlockSpec((TA, 2 * SLAB), lambda p, i: (i, p)),
                  pl.BlockSpec((s, 2 * SLAB), lambda p, i: (0, p)),
                  pl.BlockSpec((s, 2 * SLAB), lambda p, i: (0, p))],
        out_specs=[pl.BlockSpec((TA, LANES), lambda p, i: (i, p)), pl.BlockSpec((1, TA, 2), lambda p, i: (p, i, 0))],
        out_shape=[jax.ShapeDtypeStruct((s, AW), F32), jax.ShapeDtypeStruct((HEADS // 2, s, 2), F32)],
        vmem_mb=24, hosted=hosted)
    return o, lse, moved


def _attn_bwd(qp, kp, vp, dmixed, o, lse, hosted):
    s = qp.shape[0]
    nq = s // TA

    def body(q_ref, k_ref, v_ref, do_ref, o_ref, lse_ref, dq_ref, dk_ref, dv_ref, qb_ref, dob_ref):
        dk_ref[...] = jnp.zeros_like(dk_ref)
        dv_ref[...] = jnp.zeros_like(dv_ref)
        slabs = [slice(SLAB * hh, SLAB * (hh + 1)) for hh in range(2)]
        lane = lax.broadcasted_iota(jnp.int32, (TA, DH), 1)

        def q_block(i, _):
            i0 = pl.multiple_of(i * TA, TA)
            rows = pl.ds(i0, TA)
            for hh in range(2):
                half = slice(DH * hh, DH * (hh + 1))
                do = do_ref[rows, half]
                delta = jnp.sum(do * o_ref[rows, half], axis=-1, keepdims=True)
                dob_ref[hh, :, 0:DH] = do.astype(BF16)
                dob_ref[hh, :, DH:SLAB] = _lanes3(lane, 0, [-d for d in _split3(delta)], 0.0).astype(BF16)
                lse3 = _split3(lse_ref[0, rows, hh:hh + 1])
                qb_ref[hh, :, 0:DH] = q_ref[rows, SLAB * hh:SLAB * hh + DH]
                aug = q_ref[rows, SLAB * hh + DH:SLAB * (hh + 1)].astype(F32)
                qb_ref[hh, :, DH:SLAB] = _lanes3(lane, AUG_LSE, [-x for x in lse3], aug).astype(BF16)

            def block(j, dq, masked):
                keys = pl.ds(pl.multiple_of(j * TA, TA), TA)
                dv, dk, dqc = [], [], []
                for hh in range(2):
                    q, dob = qb_ref[hh], dob_ref[hh]
                    k = k_ref[keys, slabs[hh]]
                    sc = _dot(q, k, NT)
                    if masked:
                        sc = jnp.where(_causal_mask(), sc, NEG)
                    p = jnp.exp(sc)
                    dv.append(_dot(p.astype(BF16), dob, TN))
                    ds = (p * _dot(dob, v_ref[keys, slabs[hh]], NT)).astype(BF16)
                    dk.append(_dot(ds, q, TN))
                    dqc.append(_dot(ds, k))
                dv_ref[keys, :] += jnp.concatenate(dv, axis=1)
                dk_ref[keys, :] += jnp.concatenate(dk, axis=1)
                return dq + jnp.concatenate(dqc, axis=1)

            dq = lax.fori_loop(0, i, lambda j, acc: block(j, acc, False), jnp.zeros((TA, 2 * SLAB), F32))
            dq_ref[rows, :] = block(i, dq, True)
            return 0

        lax.fori_loop(0, nq, q_block, 0)

    pair = lambda p: (0, p)
    slab2 = pl.BlockSpec((s, 2 * SLAB), pair)
    seq = pl.BlockSpec((s, LANES), pair)
    small = pl.BlockSpec((1, s, 2), lambda p: (p, 0, 0))
    o32 = jax.ShapeDtypeStruct((s, HEADS * SLAB), F32)
    return _call(
        body, (qp, kp, vp, dmixed, o, lse), name="attn_bwd", grid=(HEADS // 2,),
        in_specs=[slab2, slab2, slab2, seq, seq, small], out_specs=[slab2, slab2, slab2], out_shape=[o32, o32, o32],
        scratch_shapes=[pltpu.VMEM((2, TA, SLAB), BF16), pltpu.VMEM((2, TA, SLAB), BF16)], vmem_mb=40, hosted=hosted)


def _qkv_post(dqp, dkp, dvp, proj, gq, gk, hosted):
    s = proj.shape[0]

    def body(dq_ref, dk_ref, dv_ref, q_ref, k_ref, gq_ref, gk_ref, dqo_ref, dko_ref, dvo_ref, df_ref, vec_ref):
        @pl.when(pl.program_id(0) == 0)
        def _():
            vec_ref[...] = jnp.zeros_like(vec_ref)

        def one(d_ref, x_ref, g_ref, o_ref, row, scale):
            dg = jnp.zeros((1, DH), F32)
            for h in range(HEADS):
                sl = slice(DH * h, DH * (h + 1))
                xv = x_ref[:, sl]
                r = lax.rsqrt(jnp.mean(xv * xv, axis=-1, keepdims=True) + EPS)
                xh = xv * r
                dn = d_ref[:, SLAB * h:SLAB * h + DH] * scale
                dg = dg + jnp.sum(dn * xh, axis=0, keepdims=True)
                dxh = dn * g_ref[...]
                o_ref[:, sl] = (r * (dxh - xh * jnp.mean(dxh * xh, axis=-1, keepdims=True))).astype(BF16)
            vec_ref[row:row + 1, 0:DH] += dg

        one(dq_ref, q_ref, gq_ref, dqo_ref, 0, QK_SCALE)
        one(dk_ref, k_ref, gk_ref, dko_ref, 1, 1.0)
        lane = lax.broadcasted_iota(jnp.int32, (TR, LANES), 1)
        df = jnp.zeros((TR, LANES), F32)
        for h in range(HEADS):
            dvo_ref[:, DH * h:DH * (h + 1)] = dv_ref[:, SLAB * h:SLAB * h + DH].astype(BF16)
            row_sum = dq_ref[:, SLAB * h + DH:SLAB * h + DH + 1]
            col_sum = dk_ref[:, SLAB * h + DH + AUG_ONE:SLAB * h + DH + AUG_ONE + 1]
            df = jnp.where(lane == h, row_sum - col_sum, df)
        df_ref[...] = df

    o = jax.ShapeDtypeStruct((s, AW), BF16)
    wide = _row_spec(HEADS * SLAB)
    return _call(
        body, (dqp, dkp, dvp, proj, proj, gq, gk), name="qkv_post", grid=(s // TR,),
        in_specs=[wide, wide, wide, _row_spec(AW, 0), _row_spec(AW, 1), _full_spec((1, DH)), _full_spec((1, DH))],
        out_specs=[_row_spec(AW), _row_spec(AW), _row_spec(AW), _row_spec(LANES), _full_spec((SUB, LANES))],
        out_shape=[o, o, o, jax.ShapeDtypeStruct((s, LANES), F32), jax.ShapeDtypeStruct((SUB, LANES), F32)],
        hosted=hosted)


TF = 256
NJ = DFF // TF
FFN_ROWS_FWD = 1024
FFN_ROWS_BWD = 1024


def _ffn_fwd(h2, wup_t, cw, wd):
    s = h2.shape[0]
    tr = FFN_ROWS_FWD
    nr = s // tr

    def body(h_ref, wu_ref, cg_ref, cv_ref, wd_ref, pg_ref, pv_ref, y_ref, halo_ref, act_ref):
        r, j = pl.program_id(0), pl.program_id(1)
        hv = h_ref[...]
        pg = _dot(hv, wu_ref[0], NT).astype(BF16)
        pv = _dot(hv, wu_ref[1], NT).astype(BF16)
        pg_ref[...] = pg
        pv_ref[...] = pv
        pgf, pvf = pg.astype(F32), pv.astype(F32)
        ug, _, _ = _conv_taps(pgf, jnp.where(r > 0, halo_ref[j, 0], 0.0), cg_ref[...])
        uv, _, _ = _conv_taps(pvf, jnp.where(r > 0, halo_ref[j, 1], 0.0), cv_ref[...])
        halo_ref[j, 0] = pgf[tr - SUB:tr, :]
        halo_ref[j, 1] = pvf[tr - SUB:tr, :]
        act = (ug * _sigmoid(ug) * uv).astype(BF16)
        for t in range(NJ):
            @pl.when(j == t)
            def _(t=t):
                act_ref[:, t * TF:(t + 1) * TF] = act

        @pl.when(j == NJ - 1)
        def _():
            y_ref[...] = _dot(act_ref[...], wd_ref[...])

    pre = jax.ShapeDtypeStruct((s, DFF), BF16)
    return pl.pallas_call(
        body, name="ffn_fwd", grid=(nr, NJ),
        in_specs=[pl.BlockSpec((tr, D), lambda r, j: (r, 0)),
                  pl.BlockSpec((2, TF, D), lambda r, j: (0, j, 0)),
                  pl.BlockSpec((3, TF), lambda r, j: (0, j)),
                  pl.BlockSpec((3, TF), lambda r, j: (0, NJ + j)),
                  pl.BlockSpec((DFF, D), lambda r, j: (0, 0))],
        out_specs=(pl.BlockSpec((tr, TF), lambda r, j: (r, j)),
                   pl.BlockSpec((tr, TF), lambda r, j: (r, j)),
                   pl.BlockSpec((tr, D), lambda r, j: (r, 0))),
        out_shape=(pre, pre, jax.ShapeDtypeStruct((s, D), F32)),
        scratch_shapes=[pltpu.VMEM((NJ, 2, SUB, TF), F32), pltpu.VMEM((tr, DFF), BF16)],
        compiler_params=_params(("arbitrary", "arbitrary"), 56),
    )(h2, wup_t, cw, cw, wd)


def _ffn_bwd(dy, h2, pre_g, pre_v, wup_t, cw, wd):
    s = h2.shape[0]
    tr = FFN_ROWS_BWD
    nr = s // tr
    hb = tr // (2 * SUB)

    def body(dy_ref, h_ref, pg_ref, pv_ref, hg_ref, hv_ref, wu_ref, cg_ref, cv_ref, wd_ref,
             dh_ref, dwu_ref, dwd_ref, dcg_ref, dcv_ref, nxt_ref, awu_ref, awd_ref):
        j, r = pl.program_id(0), pl.program_id(1)
        rr = nr - 1 - r
        row0 = pl.multiple_of(rr * tr, tr)
        cwg, cwv = cg_ref[...], cv_ref[...]
        pg, pv = pg_ref[...].astype(F32), pv_ref[...].astype(F32)
        ug, g1, g2 = _conv_taps(pg, jnp.where(rr > 0, hg_ref[SUB:2 * SUB, :].astype(F32), 0.0), cwg)
        uv, v1, v2 = _conv_taps(pv, jnp.where(rr > 0, hv_ref[SUB:2 * SUB, :].astype(F32), 0.0), cwv)
        sg = _sigmoid(ug)
        sil = ug * sg
        act = (sil * uv).astype(BF16)
        dyv = dy_ref[...]
        da = _dot(dyv, wd_ref[...], NT)
        dug = da * uv * (sg * (1.0 + ug * (1.0 - sg)))
        duv = da * sil
        dpg = _conv_taps_t(dug, jnp.where(r > 0, nxt_ref[0], 0.0), cwg)
        dpv = _conv_taps_t(duv, jnp.where(r > 0, nxt_ref[1], 0.0), cwv)
        nxt_ref[0] = dug[0:SUB, :]
        nxt_ref[1] = duv[0:SUB, :]
        dpgb, dpvb = dpg.astype(BF16), dpv.astype(BF16)
        hv = h_ref[...]
        dwd = _dot(act, dyv, TN)
        dpb = jnp.concatenate([dpgb, dpvb], axis=1)
        dwu = _dot(dpb, hv, TN)
        dh = _dot(dpb, wu_ref[...].reshape(2 * TF, D))

        def taps(du, x0, x1, x2):
            return (jnp.sum(du * x2, axis=0, keepdims=True), jnp.sum(du * x1, axis=0, keepdims=True),
                    jnp.sum(du * x0, axis=0, keepdims=True))

        tg, tv = taps(dug, pg, g1, g2), taps(duv, pv, v1, v2)

        @pl.when(r == 0)
        def _():
            awd_ref[...] = dwd
            awu_ref[...] = dwu
            dcg_ref[...] = jnp.zeros_like(dcg_ref)
            dcv_ref[...] = jnp.zeros_like(dcv_ref)

        @pl.when(r > 0)
        def _():
            awd_ref[...] += dwd
            awu_ref[...] += dwu

        @pl.when(r == nr - 1)
        def _():
            dwd_ref[...] = awd_ref[...].astype(BF16)
            dwu_ref[...] = awu_ref[...].astype(BF16).reshape(2, TF, D)

        for t in range(3):
            dcg_ref[t:t + 1, :] += tg[t]
            dcv_ref[t:t + 1, :] += tv[t]

        @pl.when(j == 0)
        def _():
            dh_ref[pl.ds(row0, tr), :] = dh

        @pl.when(j > 0)
        def _():
            dh_ref[pl.ds(row0, tr), :] += dh

    rows = lambda j, r: (nr - 1 - r, 0)
    tile = lambda j, r: (nr - 1 - r, j)
    halo = lambda j, r: (jnp.maximum((nr - 1 - r) * hb - 1, 0), j)
    return pl.pallas_call(
        body, name="ffn_bwd", grid=(NJ, nr),
        in_specs=[pl.BlockSpec((tr, D), rows), pl.BlockSpec((tr, D), rows),
                  pl.BlockSpec((tr, TF), tile), pl.BlockSpec((tr, TF), tile),
                  pl.BlockSpec((2 * SUB, TF), halo), pl.BlockSpec((2 * SUB, TF), halo),
                  pl.BlockSpec((2, TF, D), lambda j, r: (0, j, 0)),
                  pl.BlockSpec((3, TF), lambda j, r: (0, j)), pl.BlockSpec((3, TF), lambda j, r: (0, NJ + j)),
                  pl.BlockSpec((TF, D), lambda j, r: (j, 0))],
        out_specs=(pl.BlockSpec((s, D), lambda j, r: (0, 0)),
                   pl.BlockSpec((2, TF, D), lambda j, r: (0, j, 0)),
                   pl.BlockSpec((TF, D), lambda j, r: (j, 0)),
                   pl.BlockSpec((SUB, TF), lambda j, r: (0, j)), pl.BlockSpec((SUB, TF), lambda j, r: (0, j))),
        out_shape=(jax.ShapeDtypeStruct((s, D), F32),
                   jax.ShapeDtypeStruct((2, DFF, D), BF16), jax.ShapeDtypeStruct((DFF, D), BF16),
                   jax.ShapeDtypeStruct((SUB, DFF), F32), jax.ShapeDtypeStruct((SUB, DFF), F32)),
        scratch_shapes=[pltpu.VMEM((2, SUB, TF), F32), pltpu.VMEM((2 * TF, D), F32), pltpu.VMEM((TF, D), F32)],
        compiler_params=_params(("arbitrary", "arbitrary"), 56),
    )(dy, h2, pre_g, pre_v, pre_g, pre_v, wup_t, cw, cw, wd)


def _adam(w, g, m, v):
    m = ADAM_B1 * m + (1.0 - ADAM_B1) * g
    v = ADAM_B2 * v + (1.0 - ADAM_B2) * (g * g)
    m_hat = m / (1.0 - ADAM_B1 ** ADAM_STEP)
    v_hat = v / (1.0 - ADAM_B2 ** ADAM_STEP)
    delta = -ADAM_LR * (m_hat / (jnp.sqrt(v_hat) + ADAM_EPS) + ADAM_WD * w)
    return delta, m, v


NCHIP = NDEV // 2


def _pair_add(mine, theirs, tr, name):
    _, _, rws, cols = mine.shape

    def body(a_ref, b_ref, o_ref):
        c = lax.axis_index("c")
        o_ref[0] = (a_ref[0, c].astype(F32) + b_ref[0].astype(F32)).astype(BF16)

    (out,), _ = _call(
        body, (mine, theirs), name=name, grid=(NCHIP, rws // tr),
        in_specs=[pl.BlockSpec((1, 2, tr, cols), lambda q, i: (q, 0, i, 0)),
                  pl.BlockSpec((1, tr, cols), lambda q, i: (q, i, 0))],
        out_specs=[pl.BlockSpec((1, tr, cols), lambda q, i: (q, i, 0))],
        out_shape=[jax.ShapeDtypeStruct((NCHIP, rws, cols), BF16)], vmem_mb=16)
    return out


def _adamw_sharded(parts, w, m, v, tr, name, hosted=None):
    rws, cols = w.shape
    n_parts = parts.shape[0]

    def body(p_ref, w_ref, m_ref, v_ref, g_ref, d_ref, mo_ref, vo_ref):
        g = p_ref[0].astype(F32)
        for q in range(1, n_parts):
            g = g + p_ref[q].astype(F32)
        g_ref[...] = g
        d_ref[...], mo_ref[...], vo_ref[...] = _adam(w_ref[...], g, m_ref[...], v_ref[...])

    blk = pl.BlockSpec((tr, cols), lambda i: (i, 0))
    o = jax.ShapeDtypeStruct((rws, cols), F32)
    outs, moved = _call(
        body, (parts, w, m, v), name=name, grid=(rws // tr,),
        in_specs=[pl.BlockSpec((n_parts, tr, cols), lambda i: (0, i, 0)), blk, blk, blk],
        out_specs=[blk, blk, blk, blk], out_shape=[o, o, o, o], vmem_mb=44 if tr > 256 else 24, hosted=hosted)
    return (outs, moved) if hosted else outs


def _adamw_ada(c_all, dmod_my, w, m, v):
    rws, cols = w.shape
    tr = 256

    def body(c_ref, dm_ref, w_ref, m_ref, v_ref, g_ref, d_ref, mo_ref, vo_ref):
        cv = c_ref[...]
        act = cv * _sigmoid(cv)
        g = _dot(act, dm_ref[...], TN, lax.Precision.HIGHEST)
        g_ref[...] = g
        d_ref[...], mo_ref[...], vo_ref[...] = _adam(w_ref[...], g, m_ref[...], v_ref[...])

    blk = pl.BlockSpec((tr, cols), lambda i: (i, 0))
    o = jax.ShapeDtypeStruct((rws, cols), F32)
    return pl.pallas_call(
        body, name="adamw_ada", grid=(rws // tr,),
        in_specs=[pl.BlockSpec((NDEV, tr), lambda i: (0, i)), _full_spec((NDEV, cols)), blk, blk, blk],
        out_specs=(blk, blk, blk, blk), out_shape=(o, o, o, o),
        compiler_params=_params(("parallel",), 32),
    )(c_all, dmod_my, w, m, v)


REP_ROWS = 16
ROW_N1, ROW_N2, ROW_LOSS, ROW_MISC = 6, 7, 8, 9
LANE_BF, LANE_GQ, LANE_GK = 0, 128, 256


def _adamw_small(rep_all, conv_all, wmv):
    n_ff = wmv[6][0].shape[1]

    def body(*refs):
        rep_ref, conv_ref = refs[:2]
        ins = refs[2:2 + 24]
        outs = refs[2 + 24:]
        loss_ref, outs = outs[0], outs[1:]
        g_rep = rep_ref[0]
        g_conv = conv_ref[0]
        for d in range(1, NDEV):
            g_rep = g_rep + rep_ref[d]
            g_conv = g_conv + conv_ref[d]
        loss_ref[...] = (0.5 / D) * jnp.sum(g_rep[ROW_LOSS:ROW_LOSS + 1, :], axis=-1, keepdims=True)
        grads = [
            None,
            g_rep[ROW_N1:ROW_N1 + 1, :],
            g_rep[ROW_MISC:ROW_MISC + 1, LANE_BF:LANE_BF + HEADS],
            g_rep[ROW_MISC:ROW_MISC + 1, LANE_GQ:LANE_GQ + DH],
            g_rep[ROW_MISC:ROW_MISC + 1, LANE_GK:LANE_GK + DH],
            g_rep[ROW_N2:ROW_N2 + 1, :],
            g_conv[0:3, 0:n_ff],
            g_conv[0:3, n_ff:n_ff + DH],
        ]
        for p in range(8):
            w_ref, m_ref, v_ref = ins[3 * p:3 * p + 3]
            g_ref, d_ref, mo_ref, vo_ref = outs[4 * p:4 * p + 4]
            if p == 0:
                for nmod in range(NMOD):
                    sl = slice(D * nmod, D * (nmod + 1))
                    g = g_rep[nmod:nmod + 1, :]
                    g_ref[:, sl] = g
                    d_ref[:, sl], mo_ref[:, sl], vo_ref[:, sl] = _adam(w_ref[:, sl], g, m_ref[:, sl], v_ref[:, sl])
            else:
                g = grads[p]
                g_ref[...] = g
                d_ref[...], mo_ref[...], vo_ref[...] = _adam(w_ref[...], g, m_ref[...], v_ref[...])

    flat = [a for trio in wmv for a in trio]
    out_shape = [jax.ShapeDtypeStruct((1, 1), F32)]
    for trio in wmv:
        out_shape += [jax.ShapeDtypeStruct(trio[0].shape, F32)] * 4
    return pl.pallas_call(
        body, name="adamw_small", out_shape=tuple(out_shape),
        compiler_params=_params(None, 32),
    )(rep_all, conv_all, *flat)


FG_FIRST = 3 * AW
N_IN = DIN // NDEV


def _w_in_runs():
    runs = []
    for d in range(NDEV):
        lo, hi = N_IN * d, N_IN * (d + 1)
        for a, b, shift in ((0, FG_FIRST, 0), (FG_FIRST, FG_FIRST + HEADS, DIN - HEADS - FG_FIRST),
                            (FG_FIRST + HEADS, DIN, -HEADS)):
            a, b = max(a, lo), min(b, hi)
            if a < b:
                runs.append((d, a - lo, a + shift, b - a))
    return runs


W_IN_ROWS = 256
N_IN_PAD = 512


def _identity(n):
    return (lax.broadcasted_iota(jnp.int32, (n, n), 0) == lax.broadcasted_iota(jnp.int32, (n, n), 1)).astype(BF16)


def _assemble_w_in(g_in, hosted):
    def body(g_ref, o_ref, t_ref):
        eye = _identity(W_IN_ROWS)
        shard = None
        for d, src, dst, width in _w_in_runs():
            if d != shard:
                t_ref[:, 0:N_IN] = _dot(eye, g_ref[d], NT).astype(BF16)
                shard = d
            o_ref[:, dst:dst + width] = t_ref[:, src:src + width]
        o_ref[:, DIN:DINP] = jnp.zeros((W_IN_ROWS, DINP - DIN), o_ref.dtype)

    (out,), moved = _call(
        body, (g_in,), name="assemble_w_in", grid=(D // W_IN_ROWS,),
        in_specs=[pl.BlockSpec((NDEV, N_IN, W_IN_ROWS), lambda i: (0, 0, i))],
        out_specs=[pl.BlockSpec((W_IN_ROWS, DINP), lambda i: (i, 0))],
        out_shape=[jax.ShapeDtypeStruct((D, DINP), g_in.dtype)],
        scratch_shapes=[pltpu.VMEM((W_IN_ROWS, N_IN_PAD), BF16)], vmem_mb=16, hosted=hosted)
    return out, moved


def _scatter_dw_in(dwp):
    def body(w_ref, o_ref, t_ref):
        eye = _identity(W_IN_ROWS)
        runs = _w_in_runs()
        for i, (d, src, dst, width) in enumerate(runs):
            t_ref[:, src:src + width] = w_ref[:, dst:dst + width]
            if i + 1 == len(runs) or runs[i + 1][0] != d:
                o_ref[d // 2, d % 2] = _dot(t_ref[:, 0:N_IN], eye, TN).astype(BF16)

    (out,), _ = _call(
        body, (dwp,), name="scatter_dw_in", grid=(D // W_IN_ROWS,),
        in_specs=[pl.BlockSpec((W_IN_ROWS, DINP), lambda i: (i, 0))],
        out_specs=[pl.BlockSpec((NCHIP, 2, N_IN, W_IN_ROWS), lambda i: (0, 0, 0, i))],
        out_shape=[jax.ShapeDtypeStruct((NCHIP, 2, N_IN, D), dwp.dtype)],
        scratch_shapes=[pltpu.VMEM((W_IN_ROWS, N_IN_PAD), BF16)], vmem_mb=16)
    return out


def kernel(x, c, w_ada, b_ada, norm1_g, w_in, b_forget, q_norm_g, k_norm_g, conv_mix_w, w_out, norm2_g, w_up, ffn_conv_w, w_down, loss_target, m_w_ada, m_b_ada, m_norm1_g, m_w_in, m_b_forget, m_q_norm_g, m_k_norm_g, m_conv_mix_w, m_w_out, m_norm2_g, m_w_up, m_ffn_conv_w, m_w_down, v_w_ada, v_b_ada, v_norm1_g, v_w_in, v_b_forget, v_q_norm_g, v_k_norm_g, v_conv_mix_w, v_w_out, v_norm2_g, v_w_up, v_ffn_conv_w, v_w_down):
    me = 4 * lax.axis_index("x") + 2 * lax.axis_index("y") + lax.axis_index("c")
    xs, tgt = x[0], loss_target[0]
    n_ada = w_ada.shape[2]
    n_ff = w_up.shape[2]

    conv_w = jnp.concatenate([ffn_conv_w[0], conv_mix_w[0]], axis=1)
    conv_w = jnp.concatenate([conv_w, jnp.zeros((SUB - 3, conv_w.shape[1]), F32)], axis=0)
    c_all, conv_all, g_in = _exchange(
        [(c.reshape(SUB, D // SUB), "ag"), (conv_w, "ag"), (jnp.transpose(w_in[0]).astype(BF16), "ag2")],
        "exchange_w_in")
    g_in, w_out_b, w_up_b, w_down_b = lax.optimization_barrier(
        (g_in, w_out[0].astype(BF16), jnp.transpose(w_up[0]).astype(BF16), w_down[0].astype(BF16)))
    g_out, g_up, g_down = _sequencer_exchange(
        [(w_out_b, "ag2"), (w_up_b, "ag2"), (w_down_b, "ag2")], "gather_weights", collective_id=1)
    c_all = c_all.reshape(NDEV, D)
    cw_ffn = jnp.transpose(conv_all[:, :3, :n_ff], (1, 0, 2)).reshape(3, 2 * DFF)
    cw_mix = jnp.transpose(conv_all[:, :3, n_ff:], (1, 0, 2)).reshape(3, CW)

    b_my = lax.dynamic_slice(b_ada, (0, me * n_ada), (1, n_ada))
    mod_part = _ada_fwd(c_all, w_ada[0], b_my)
    w_in_p, (mod_rows,) = _assemble_w_in(
        g_in, [(jnp.broadcast_to(mod_part[:, None, :], (NDEV, SUB, n_ada)), "a2a")])
    mod = mod_rows[:, 0, :].reshape(NMOD, D)
    mod = jnp.concatenate([mod, jnp.zeros((SUB - NMOD, D), F32)], axis=0)

    h = _norm_mod_fwd(xs, mod, norm1_g)
    proj = _mm(h, w_in_p, "nn", F32, 1024, 640, "proj_fwd")
    bf_pad = jnp.concatenate([b_forget, jnp.zeros((1, LANES - HEADS), F32)], axis=1)
    fcum = _fgate_fwd(proj, bf_pad)
    (qp, kp, vp), _ = _qkv_prep(proj, fcum, q_norm_g, k_norm_g, None)
    attn, lse, _ = _attn_fwd(qp, kp, vp, None)
    w_out_f = g_out.reshape(D, D)
    w_up_t = g_up.reshape(2, DFF, D)
    w_down_f = g_down.reshape(DFF, D)
    conv = _mixconv_fwd(proj, cw_mix)
    mixed = jnp.concatenate([attn, conv], axis=1).astype(BF16)
    z = _mm(mixed, w_out_f, "nn", F32, 1024, 1024, "out_fwd")
    x1, h2 = _resid_norm2(xs, z, mod, norm2_g)
    pre_g, pre_v, y = _ffn_fwd(h2, w_up_t, cw_ffn, w_down_f)
    dout, dy, vec_l = _loss_head(x1, y, tgt, mod)

    dh2, dwup_t, dwd, dcw_g, dcw_v = _ffn_bwd(dy, h2, pre_g, pre_v, w_up_t, cw_ffn, w_down_f)
    dx1, dz, vec_2 = _norm_mod_bwd(dh2, x1, dout, z, mod, norm2_g, 4, 2, "norm2_bwd")
    dwout = _mm(mixed, dz, "tn", BF16, 1024, 1024, "out_bwd_w")
    s_out = dwout.reshape(NCHIP, 2, D // NDEV, D)
    s_down = dwd.reshape(NCHIP, 2, DFF // NDEV, D)
    s_up = dwup_t.reshape(NCHIP, 2, n_ff, D)
    dmixed, (t_out, t_up, t_down) = _mm(dz, w_out_f, "nt", F32, 1024, 1024, "out_bwd_x",
                                        hosted=[(s_out, "pair"), (s_up, "pair"), (s_down, "pair")])
    c_out = _pair_add(s_out, t_out, 128, "pair_add_out")
    c_up = _pair_add(s_up, t_up, 176, "pair_add_up")
    c_down = _pair_add(s_down, t_down, 176, "pair_add_down")
    dxin, dbg, dcg, dcw_mix = _mixconv_bwd(dmixed, proj, cw_mix)
    (dqp, dkp, dvp), (p_up, p_down, p_out) = _attn_bwd(
        qp, kp, vp, dmixed, attn, lse, [(c_up, "chips"), (c_down, "chips"), (c_out, "chips")])
    (dq, dk, dvb, dfcol, vec_qk), _ = _qkv_post(dqp, dkp, dvp, proj, q_norm_g, k_norm_g, None)
    dfg, vec_bf = _fgate_bwd(dfcol, proj, bf_pad)
    dproj = jnp.concatenate([dq, dk, dvb, dxin, dbg, dcg, dfg], axis=1)
    dwin_p = _mm(h, dproj, "tn", BF16, 1024, 640, "proj_bwd_w")
    s_in = _scatter_dw_in(dwin_p).reshape(NDEV, N_IN, D)
    (p_in,) = _sequencer_exchange([(s_in, "a2a")], "scatter_dw_in_partials", collective_id=2, all_peers=True)
    dh = _mm(dproj, w_in_p, "nt", F32, 1024, 512, "proj_bwd_x", vmem_mb=36)
    grad_x, vec_1 = _norm_mod_bwd(dh, xs, dx1, None, mod, norm1_g, 1, None, "norm1_bwd")

    misc = jnp.zeros((1, D), F32)
    misc = lax.dynamic_update_slice(misc, vec_bf[0:1, :HEADS], (0, LANE_BF))
    misc = lax.dynamic_update_slice(misc, vec_qk[0:1, :DH], (0, LANE_GQ))
    misc = lax.dynamic_update_slice(misc, vec_qk[1:2, :DH], (0, LANE_GK))
    rep = jnp.concatenate([
        vec_1[0:1], vec_1[1:2], vec_2[3:4], vec_2[0:1], vec_2[1:2], vec_l[0:1],
        vec_1[2:3], vec_2[2:3], vec_l[1:2], misc, jnp.zeros((REP_ROWS - 10, D), F32)], axis=0)
    dcw_ffn = jnp.concatenate([dcw_g, dcw_v], axis=1).reshape(SUB, NDEV, n_ff)
    dcw_all = jnp.concatenate([jnp.transpose(dcw_ffn, (1, 0, 2)),
                               jnp.transpose(dcw_mix.reshape(SUB, NDEV, DH), (1, 0, 2))], axis=2)
    r_up = _adamw_sharded(p_up, jnp.transpose(w_up[0]), jnp.transpose(m_w_up[0]), jnp.transpose(v_w_up[0]), 176,
                          "adamw_up")
    r_down = _adamw_sharded(p_down, w_down[0], m_w_down[0], v_w_down[0], 176, "adamw_down")
    rep, dcw_all, r_up, r_down = lax.optimization_barrier((rep, dcw_all, r_up, r_down))
    r_up = tuple(jnp.transpose(a) for a in r_up)
    r_out, (rep_all, conv_parts) = _adamw_sharded(p_out, w_out[0], m_w_out[0], v_w_out[0], 128, "adamw_out",
                                                  hosted=[(rep, "ag"), (dcw_all, "a2a")])
    dmod_my = lax.dynamic_slice(rep_all[:, :NMOD, :].reshape(NDEV, NMOD * D), (0, me * n_ada), (NDEV, n_ada))
    r_ada = _adamw_ada(c_all, dmod_my, w_ada[0], m_w_ada[0], v_w_ada[0])
    r_in = _adamw_sharded(p_in, jnp.transpose(w_in[0]), jnp.transpose(m_w_in[0]), jnp.transpose(v_w_in[0]), N_IN,
                          "adamw_in")
    r_in = tuple(jnp.transpose(a) for a in r_in)
    small = _adamw_small(rep_all, conv_parts, [
        [b_ada, m_b_ada, v_b_ada], [norm1_g, m_norm1_g, v_norm1_g], [b_forget, m_b_forget, v_b_forget],
        [q_norm_g, m_q_norm_g, v_q_norm_g], [k_norm_g, m_k_norm_g, v_k_norm_g], [norm2_g, m_norm2_g, v_norm2_g],
        [ffn_conv_w[0], m_ffn_conv_w[0], v_ffn_conv_w[0]], [conv_mix_w[0], m_conv_mix_w[0], v_conv_mix_w[0]]])
    loss = small[0].reshape(())
    r_bada, r_n1, r_bf, r_gq, r_gk, r_n2, r_cf, r_cm = [small[1 + 4 * p:5 + 4 * p] for p in range(8)]
    lead = lambda t: tuple(a[None] for a in t)
    per_w = [lead(r_ada), r_bada, r_n1, lead(r_in), r_bf, r_gq, r_gk, lead(r_cm), lead(r_out), r_n2,
             lead(r_up), lead(r_cf), lead(r_down)]
    outs = [loss, grad_x[None]]
    for field in range(4):
        outs += [t[field] for t in per_w]
    return tuple(outs)
```

```python
import functools

import jax
import jax.numpy as jnp
import numpy as np
from jax import lax
from jax.experimental import pallas as pl
from jax.experimental.pallas import tpu as pltpu
from jax.experimental.pallas import tpu_sc as plsc

F32 = jnp.float32
BF16 = jnp.bfloat16

NDEV = 8
D = 1024
HEADS = 8
DH = 64
AW = 512
CW = 512
DFF = 2816
DIN = 3080
DINP = 3200
NMOD = 6
EPS = 1e-6
QK_SCALE = 0.125
LANES = 128
SUB = 8

ADAM_LR = 0.001
ADAM_B1 = 0.9
ADAM_B2 = 0.999
ADAM_EPS = 1e-08
ADAM_WD = 0.01
ADAM_STEP = 10

MESH = pl.DeviceIdType.MESH
ANY = pl.BlockSpec(memory_space=pl.ANY)

NN = (((1,), (0,)), ((), ()))
NT = (((1,), (1,)), ((), ()))
TN = (((0,), (0,)), ((), ()))


def _dot(a, b, dims=NN, precision=None):
    return lax.dot_general(a, b, dims, precision=precision, preferred_element_type=F32)


def _params(sem=None, vmem_mb=None):
    kw = {}
    if sem is not None:
        kw["dimension_semantics"] = sem
    if vmem_mb is not None:
        kw["vmem_limit_bytes"] = vmem_mb * 1024 * 1024
    return pltpu.CompilerParams(**kw)


def _sigmoid(x):
    return 0.5 * jnp.tanh(0.5 * x) + 0.5


class _Exchange:
    def __init__(self, items):
        self.arrays = [pltpu.with_memory_space_constraint(a, pltpu.HBM) for a, _ in items]
        self.modes = [m for _, m in items]
        self.n = len(items)
        self.out_shape = []
        for a, m in items:
            sh = {"ag": (NDEV,) + a.shape, "ag2": (NDEV,) + a.shape, "pair": a.shape[:1] + a.shape[2:]}.get(m, a.shape)
            self.out_shape.append(jax.ShapeDtypeStruct(sh, a.dtype))
        self.scratch = [pltpu.SemaphoreType.DMA((self.n, NDEV - 1)), pltpu.SemaphoreType.DMA((self.n, NDEV - 1)),
                        pltpu.SemaphoreType.DMA((self.n,))]

    def _plan(self, srcs, outs, sems):
        send_sems, recv_sems, loc_sems = sems
        x, y, c = lax.axis_index("x"), lax.axis_index("y"), lax.axis_index("c")
        me, my_chip = 4 * x + 2 * y + c, 2 * x + y
        sib = (x, y, 1 - c)
        local, first, landed, forwards, arrivals = [], [], [], [], []

        def remote(a, k, src, dst, to):
            return pltpu.make_async_remote_copy(src_ref=src, dst_ref=dst, send_sem=send_sems.at[a, k],
                                                recv_sem=recv_sems.at[a, k], device_id=to, device_id_type=MESH)

        for a, mode in enumerate(self.modes):
            src, out = srcs[a], outs[a]
            if mode in ("ag", "a2a"):
                piece = (lambda slot, src=src: src) if mode == "ag" else (lambda slot, src=src: src.at[slot])
                local.append(pltpu.make_async_copy(piece(me), out.at[me], loc_sems.at[a]))
                for r in range(1, NDEV):
                    px = 1 - x if (r >> 2) & 1 else x
                    py = 1 - y if (r >> 1) & 1 else y
                    pc = 1 - c if r & 1 else c
                    pidx = 4 * px + 2 * py + pc
                    first.append(remote(a, r - 1, piece(pidx), out.at[me], (px, py, pc)))
                    arrivals.append(remote(a, r - 1, piece(pidx), out.at[pidx], (px, py, pc)))
            elif mode == "ag2":
                local.append(pltpu.make_async_copy(src, out.at[me], loc_sems.at[a]))
                first.append(remote(a, 0, src, out.at[me], sib))
                arrivals.append(remote(a, 0, src, out.at[me + 1 - 2 * c], sib))
                for j, (px, py) in enumerate([(1 - x, y), (x, 1 - y), (1 - x, 1 - y)]):
                    theirs = out.at[4 * px + 2 * py + c]
                    first.append(remote(a, 1 + j, src, out.at[me], (px, py, c)))
                    landed.append(remote(a, 1 + j, src, theirs, (px, py, c)))
                    forwards.append(remote(a, 4 + j, theirs, theirs, sib))
                    arrivals.append(remote(a, 4 + j, src, out.at[4 * px + 2 * py + 1 - c], sib))
            elif mode == "pair":
                for q in range(NDEV // 2):
                    first.append(remote(a, q, src.at[q, 1 - c], out.at[q], sib))
                    arrivals.append(remote(a, q, src.at[q, 1 - c], out.at[q], sib))
            else:
                assert mode == "chips", mode
                local.append(pltpu.make_async_copy(src.at[my_chip], out.at[my_chip], loc_sems.at[a]))
                for j, (px, py) in enumerate([(1 - x, y), (x, 1 - y), (1 - x, 1 - y)]):
                    q = 2 * px + py
                    first.append(remote(a, 1 + j, src.at[q], out.at[my_chip], (px, py, c)))
                    arrivals.append(remote(a, 1 + j, src.at[q], out.at[q], (px, py, c)))
        return local, first, landed, forwards, arrivals

    def start(self, srcs, outs, sems):
        local, first, _, _, _ = self._plan(srcs, outs, sems)
        for cp in local + first:
            cp.start()

    def wait(self, srcs, outs, sems):
        local, first, landed, forwards, arrivals = self._plan(srcs, outs, sems)
        for cp, fwd in zip(landed, forwards):
            cp.wait_recv()
            fwd.start()
        for cp in arrivals:
            cp.wait_recv()
        for cp in first + forwards:
            cp.wait_send()
        for cp in local:
            cp.wait()


def _exchange(items, name):
    ex = _Exchange(items)
    n = ex.n

    def body(*refs):
        srcs, outs, sems = refs[:n], refs[n:2 * n], refs[2 * n:]
        ex.start(srcs, outs, sems)
        ex.wait(srcs, outs, sems)

    return pl.pallas_call(
        body, name=name,
        out_shape=tuple(ex.out_shape),
        in_specs=[ANY] * n, out_specs=tuple([ANY] * n),
        scratch_shapes=ex.scratch,
        compiler_params=pltpu.CompilerParams(has_side_effects=True),
    )(*ex.arrays)


def _sequencer_exchange(items, name, collective_id, all_peers=False):
    ex = _Exchange(items)
    srcs = [jax.new_ref(a, memory_space=pltpu.MemorySpace.HBM) for a in ex.arrays]
    outs = [jax.empty_ref(sh, memory_space=pltpu.MemorySpace.HBM) for sh in ex.out_shape]

    @pl.kernel(mesh=plsc.ScalarSubcoreMesh(axis_name="sequencer", num_cores=1), name=name,
               scratch_types=tuple(ex.scratch), compiler_params=pltpu.CompilerParams(collective_id=collective_id))
    def launch(send_sems, recv_sems, loc_sems):
        x, y, c = lax.axis_index("x"), lax.axis_index("y"), lax.axis_index("c")
        barrier = pltpu.get_barrier_semaphore()
        peers = [(x, y, 1 - c), (1 - x, y, c), (x, 1 - y, c), (1 - x, 1 - y, c)]
        if all_peers:
            peers += [(1 - x, y, 1 - c), (x, 1 - y, 1 - c), (1 - x, 1 - y, 1 - c)]
        for peer in peers:
            pl.semaphore_signal(barrier, inc=1, device_id=peer, device_id_type=MESH)
        pl.semaphore_wait(barrier, len(peers))
        sems = (send_sems, recv_sems, loc_sems)
        ex.start(srcs, outs, sems)
        ex.wait(srcs, outs, sems)

    launch()
    return [o[...] for o in outs]


def _call(body, inputs, *, name, grid, in_specs, out_specs, out_shape, scratch_shapes=(), vmem_mb=None, hosted=None):
    out_specs, out_shape, scratch_shapes = tuple(out_specs), tuple(out_shape), list(scratch_shapes)
    if not hosted:
        res = pl.pallas_call(
            body, name=name, grid=grid, in_specs=list(in_specs), out_specs=out_specs, out_shape=out_shape,
            scratch_shapes=scratch_shapes, compiler_params=_params(("arbitrary",) * len(grid), vmem_mb),
        )(*inputs)
        return tuple(res), ()
    ex = _Exchange(hosted)
    n, n_in, n_out, n_scr = ex.n, len(inputs), len(out_shape), len(scratch_shapes)

    def hosting_body(*refs):
        ins, srcs = refs[:n_in], refs[n_in:n_in + n]
        outs, landing = refs[n_in + n:n_in + n + n_out], refs[n_in + n + n_out:n_in + 2 * n + n_out]
        scratch, sems = refs[n_in + 2 * n + n_out:n_in + 2 * n + n_out + n_scr], refs[n_in + 2 * n + n_out + n_scr:]
        first = functools.reduce(jnp.logical_and, [pl.program_id(d) == 0 for d in range(len(grid))])
        last = functools.reduce(jnp.logical_and, [pl.program_id(d) == grid[d] - 1 for d in range(len(grid))])

        @pl.when(first)
        def _():
            ex.start(srcs, landing, sems)

        body(*ins, *outs, *scratch)

        @pl.when(last)
        def _():
            ex.wait(srcs, landing, sems)

    res = pl.pallas_call(
        hosting_body, name=name, grid=grid,
        in_specs=list(in_specs) + [ANY] * n, out_specs=out_specs + tuple([ANY] * n),
        out_shape=out_shape + tuple(ex.out_shape), scratch_shapes=scratch_shapes + ex.scratch,
        compiler_params=_params(("arbitrary",) * len(grid), vmem_mb),
    )(*inputs, *ex.arrays)
    return tuple(res[:n_out]), tuple(res[n_out:])


def _mm(a, b, mode, out_dtype, tm, tn, name, hosted=None, vmem_mb=24):
    if mode == "nn":
        (m, k), n = a.shape, b.shape[1]
        a_spec = pl.BlockSpec((tm, k), lambda i, j: (i, 0))
        b_spec = pl.BlockSpec((k, tn), lambda i, j: (0, j))
        dims = NN
    elif mode == "nt":
        (m, k), n = a.shape, b.shape[0]
        a_spec = pl.BlockSpec((tm, k), lambda i, j: (i, 0))
        b_spec = pl.BlockSpec((tn, k), lambda i, j: (j, 0))
        dims = NT
    else:
        (k, m), n = a.shape, b.shape[1]
        a_spec = pl.BlockSpec((k, tm), lambda i, j: (0, i))
        b_spec = pl.BlockSpec((k, tn), lambda i, j: (0, j))
        dims = TN
    assert m % tm == 0 and n % tn == 0, (m, n, tm, tn)

    def body(a_ref, b_ref, o_ref):
        o_ref[...] = _dot(a_ref[...], b_ref[...], dims).astype(o_ref.dtype)

    (out,), moved = _call(
        body, (a, b), name=name, grid=(m // tm, n // tn),
        in_specs=[a_spec, b_spec], out_specs=[pl.BlockSpec((tm, tn), lambda i, j: (i, j))],
        out_shape=[jax.ShapeDtypeStruct((m, n), out_dtype)], vmem_mb=vmem_mb, hosted=hosted)
    return (out, moved) if hosted else out


def _shift_down(x, k, fill):
    y = pltpu.roll(x, k, 0)
    row = lax.broadcasted_iota(jnp.int32, (SUB, x.shape[1]), 0)
    head = y[0:SUB, :]
    for t in range(k):
        head = jnp.where(row == t, fill[t], head)
    return jnp.concatenate([head, y[SUB:, :]], axis=0)


def _shift_up(x, k, fill):
    n = x.shape[0]
    y = pltpu.roll(x, n - k, 0)
    row = lax.broadcasted_iota(jnp.int32, (SUB, x.shape[1]), 0)
    tail = y[n - SUB:, :]
    for t in range(k):
        tail = jnp.where(row == SUB - k + t, fill[t], tail)
    return jnp.concatenate([y[:n - SUB, :], tail], axis=0)


def _conv_taps(x, halo, w):
    if halo is None:
        f1, f2 = [0.0], [0.0, 0.0]
    else:
        f1, f2 = [halo[7:8, :]], [halo[6:7, :], halo[7:8, :]]
    s1 = _shift_down(x, 1, f1)
    s2 = _shift_down(x, 2, f2)
    u = w[2:3, :] * x + w[1:2, :] * s1 + w[0:1, :] * s2
    return u, s1, s2


def _conv_taps_t(du, nxt, w):
    if nxt is None:
        f1, f2 = [0.0], [0.0, 0.0]
    else:
        f1, f2 = [nxt[0:1, :]], [nxt[0:1, :], nxt[1:2, :]]
    return w[2:3, :] * du + w[1:2, :] * _shift_up(du, 1, f1) + w[0:1, :] * _shift_up(du, 2, f2)


def _ada_fwd(c_all, w_ada, b_my):
    def body(c_ref, w_ref, b_ref, o_ref):
        cv = c_ref[...]
        act = cv * _sigmoid(cv)
        o_ref[...] = _dot(act, w_ref[...], NN, lax.Precision.HIGHEST) + b_ref[...]

    return pl.pallas_call(
        body, name="ada_fwd",
        out_shape=jax.ShapeDtypeStruct((NDEV, w_ada.shape[1]), F32),
        compiler_params=_params(None, 32),
    )(c_all, w_ada, b_my)


TR = 256
TRE = 512


def _row_spec(width, col=0, rows=TR):
    return pl.BlockSpec((rows, width), lambda i, col=col: (i, col))


def _erow(width):
    return _row_spec(width, rows=TRE)


def _full_spec(shape):
    return pl.BlockSpec(shape, lambda i: (0,) * len(shape))


def _norm_mod_fwd(x, mod, g):
    s = x.shape[0]

    def body(x_ref, mod_ref, g_ref, h_ref):
        xv = x_ref[...]
        r = lax.rsqrt(jnp.mean(xv * xv, axis=-1, keepdims=True) + EPS)
        nrm = xv * r * g_ref[...]
        h_ref[...] = (nrm * (1.0 + mod_ref[1:2, :]) + mod_ref[0:1, :]).astype(BF16)

    return pl.pallas_call(
        body, name="norm1_fwd", grid=(s // TRE,),
        in_specs=[_erow(D), _full_spec((SUB, D)), _full_spec((1, D))],
        out_specs=_erow(D), out_shape=jax.ShapeDtypeStruct((s, D), BF16),
        compiler_params=_params(("parallel",), 16),
    )(x, mod, g)


SLAB = 2 * DH
AUG_F, AUG_ONE, AUG_LSE = 0, 3, 6


def _split3(x):
    hi = x.astype(BF16).astype(F32)
    r1 = x - hi
    mid = r1.astype(BF16).astype(F32)
    return hi, mid, r1 - mid


def _lanes3(lane, first, pieces, other):
    out = other
    for k in range(3):
        out = jnp.where(lane == first + k, pieces[k], out)
    return out


def _aug_placement():
    eq = np.zeros((3 * LANES, HEADS * SLAB), np.float32)
    ek = np.zeros((3 * LANES, HEADS * SLAB), np.float32)
    ones = np.zeros((SUB, HEADS * SLAB), np.float32)
    for h in range(HEADS):
        aug = SLAB * h + DH
        for k in range(3):
            eq[LANES * k + h, aug + AUG_F + k] = 1.0
            ek[LANES * k + h, aug + AUG_ONE + k] = -1.0
            ones[0, aug + AUG_ONE + k] = 1.0
            ones[1, aug + AUG_F + k] = ones[1, aug + AUG_LSE + k] = 1.0
            ones[2, aug + k] = 1.0
    return jnp.asarray(eq, BF16), jnp.asarray(ek, BF16), jnp.asarray(ones)


def _qkv_prep(proj, fcum, gq, gk, hosted):
    s = proj.shape[0]

    def body(q_ref, k_ref, v_ref, f_ref, gq_ref, gk_ref, eq_ref, ek_ref, ones_ref, qo_ref, ko_ref, vo_ref):
        f3 = jnp.concatenate(_split3(f_ref[...]), axis=1).astype(BF16)
        qo_ref[...] = (_dot(f3, eq_ref[...]) + ones_ref[0:1, :]).astype(BF16)
        ko_ref[...] = (_dot(f3, ek_ref[...]) + ones_ref[1:2, :]).astype(BF16)
        vo_ref[...] = jnp.broadcast_to(ones_ref[2:3, :], vo_ref.shape).astype(BF16)
        for h in range(HEADS):
            sl = slice(DH * h, DH * (h + 1))
            lo = slice(SLAB * h, SLAB * h + DH)
            qh = q_ref[:, sl]
            r = lax.rsqrt(jnp.mean(qh * qh, axis=-1, keepdims=True) + EPS)
            qo_ref[:, lo] = (qh * r * gq_ref[...] * QK_SCALE).astype(BF16)
            kh = k_ref[:, sl]
            r = lax.rsqrt(jnp.mean(kh * kh, axis=-1, keepdims=True) + EPS)
            ko_ref[:, lo] = (kh * r * gk_ref[...]).astype(BF16)
            vo_ref[:, lo] = v_ref[:, sl].astype(BF16)

    eq, ek, ones = _aug_placement()
    o = jax.ShapeDtypeStruct((s, HEADS * SLAB), BF16)
    wide = _row_spec(HEADS * SLAB)
    return _call(
        body, (proj, proj, proj, fcum, gq, gk, eq, ek, ones), name="qkv_prep", grid=(s // TR,),
        in_specs=[_row_spec(AW, 0), _row_spec(AW, 1), _row_spec(AW, 2), _row_spec(LANES),
                  _full_spec((1, DH)), _full_spec((1, DH)), _full_spec(eq.shape), _full_spec(ek.shape),
                  _full_spec(ones.shape)],
        out_specs=[wide, wide, wide], out_shape=[o, o, o], vmem_mb=16, hosted=hosted)


FG_BLOCK = (3 * AW + 3 * CW) // LANES


def _fgate_fwd(proj, bf_pad):
    s = proj.shape[0]

    def body(fg_ref, b_ref, o_ref, carry_ref):
        i = pl.program_id(0)

        @pl.when(i == 0)
        def _():
            carry_ref[...] = jnp.zeros_like(carry_ref)

        z = fg_ref[...] + b_ref[...]
        logf = jnp.minimum(z, 0.0) - jnp.log1p(jnp.exp(-jnp.abs(z)))
        row = lax.broadcasted_iota(jnp.int32, (TR, TR), 0)
        col = lax.broadcasted_iota(jnp.int32, (TR, TR), 1)
        tri = (col <= row).astype(F32)
        cs = _dot(tri, logf, NN, lax.Precision.HIGHEST) + carry_ref[0:1, :]
        o_ref[...] = cs
        carry_ref[...] = jnp.broadcast_to(cs[TR - 1:TR, :], carry_ref.shape)

    return pl.pallas_call(
        body, name="fgate_fwd", grid=(s // TR,),
        in_specs=[_row_spec(LANES, FG_BLOCK), _full_spec((1, LANES))],
        out_specs=_row_spec(LANES), out_shape=jax.ShapeDtypeStruct((s, LANES), F32),
        scratch_shapes=[pltpu.VMEM((SUB, LANES), F32)],
        compiler_params=_params(("arbitrary",)),
    )(proj, bf_pad)


def _fgate_bwd(dfcol, proj, bf_pad):
    s = proj.shape[0]
    nb = s // TR

    def body(df_ref, fg_ref, b_ref, o_ref, db_ref, carry_ref):
        i = pl.program_id(0)

        @pl.when(i == 0)
        def _():
            carry_ref[...] = jnp.zeros_like(carry_ref)
            db_ref[...] = jnp.zeros_like(db_ref)

        row = lax.broadcasted_iota(jnp.int32, (TR, TR), 0)
        col = lax.broadcasted_iota(jnp.int32, (TR, TR), 1)
        tri = (col >= row).astype(F32)
        dlogf = _dot(tri, df_ref[...], NN, lax.Precision.HIGHEST) + carry_ref[0:1, :]
        carry_ref[...] = jnp.broadcast_to(dlogf[0:1, :], carry_ref.shape)
        z = fg_ref[...] + b_ref[...]
        dfg = dlogf * _sigmoid(-z)
        o_ref[...] = dfg.astype(BF16)
        db_ref[0:1, :] += jnp.sum(dfg, axis=0, keepdims=True)

    rev = lambda col: pl.BlockSpec((TR, LANES), lambda i, col=col: (nb - 1 - i, col))
    return pl.pallas_call(
        body, name="fgate_bwd", grid=(nb,),
        in_specs=[rev(0), rev(FG_BLOCK), _full_spec((1, LANES))],
        out_specs=(rev(0), _full_spec((SUB, LANES))),
        out_shape=(jax.ShapeDtypeStruct((s, LANES), BF16), jax.ShapeDtypeStruct((SUB, LANES), F32)),
        scratch_shapes=[pltpu.VMEM((SUB, LANES), F32)],
        compiler_params=_params(("arbitrary",)),
    )(dfcol, proj, bf_pad)


def _resid_norm2(x, z, mod, g):
    s = x.shape[0]

    def body(x_ref, z_ref, mod_ref, g_ref, x1_ref, h_ref):
        x1 = x_ref[...] + mod_ref[2:3, :] * z_ref[...]
        x1_ref[...] = x1
        r = lax.rsqrt(jnp.mean(x1 * x1, axis=-1, keepdims=True) + EPS)
        nrm = x1 * r * g_ref[...]
        h_ref[...] = (nrm * (1.0 + mod_ref[4:5, :]) + mod_ref[3:4, :]).astype(BF16)

    return pl.pallas_call(
        body, name="resid_norm2", grid=(s // TRE,),
        in_specs=[_erow(D), _erow(D), _full_spec((SUB, D)), _full_spec((1, D))],
        out_specs=(_erow(D), _erow(D)),
        out_shape=(jax.ShapeDtypeStruct((s, D), F32), jax.ShapeDtypeStruct((s, D), BF16)),
        compiler_params=_params(("parallel",), 24),
    )(x, z, mod, g)


def _loss_head(x1, y, tgt, mod):
    s = x1.shape[0]

    def body(x1_ref, y_ref, t_ref, mod_ref, dout_ref, dy_ref, vec_ref):
        @pl.when(pl.program_id(0) == 0)
        def _():
            vec_ref[...] = jnp.zeros_like(vec_ref)

        yv = y_ref[...]
        g2 = mod_ref[5:6, :]
        diff = x1_ref[...] + g2 * yv - t_ref[...]
        dout = diff * (1.0 / D)
        dout_ref[...] = dout
        dy_ref[...] = (g2 * dout).astype(BF16)
        vec_ref[0:1, :] += jnp.sum(dout * yv, axis=0, keepdims=True)
        vec_ref[1:2, :] += jnp.sum(diff * diff, axis=0, keepdims=True)

    return pl.pallas_call(
        body, name="loss_head", grid=(s // TRE,),
        in_specs=[_erow(D), _erow(D), _erow(D), _full_spec((SUB, D))],
        out_specs=(_erow(D), _erow(D), _full_spec((SUB, D))),
        out_shape=(jax.ShapeDtypeStruct((s, D), F32), jax.ShapeDtypeStruct((s, D), BF16),
                   jax.ShapeDtypeStruct((SUB, D), F32)),
        compiler_params=_params(("arbitrary",), 24),
    )(x1, y, tgt, mod)


def _norm_mod_bwd(dh, xin, dres, zin, mod, g, scale_row, gate_row, name, hosted=None):
    s = dh.shape[0]
    with_gate = gate_row is not None

    def body(*refs):
        if with_gate:
            dh_ref, x_ref, dres_ref, z_ref, mod_ref, g_ref, dx_ref, dz_ref, vec_ref = refs
        else:
            dh_ref, x_ref, dres_ref, mod_ref, g_ref, dx_ref, vec_ref = refs

        @pl.when(pl.program_id(0) == 0)
        def _():
            vec_ref[...] = jnp.zeros_like(vec_ref)

        xv = x_ref[...]
        dhv = dh_ref[...]
        gv = g_ref[...]
        r = lax.rsqrt(jnp.mean(xv * xv, axis=-1, keepdims=True) + EPS)
        xh = xv * r
        dn = dhv * (1.0 + mod_ref[scale_row:scale_row + 1, :])
        dxh = dn * gv
        dx = dres_ref[...] + r * (dxh - xh * jnp.mean(dxh * xh, axis=-1, keepdims=True))
        dx_ref[...] = dx
        vec_ref[0:1, :] += jnp.sum(dhv, axis=0, keepdims=True)
        vec_ref[1:2, :] += jnp.sum(dhv * (xh * gv), axis=0, keepdims=True)
        vec_ref[2:3, :] += jnp.sum(dn * xh, axis=0, keepdims=True)
        if with_gate:
            dz_ref[...] = (mod_ref[gate_row:gate_row + 1, :] * dx).astype(BF16)
            vec_ref[3:4, :] += jnp.sum(dx * z_ref[...], axis=0, keepdims=True)

    ins = [dh, xin, dres] + ([zin] if with_gate else []) + [mod, g]
    in_specs = [_erow(D)] * (4 if with_gate else 3) + [_full_spec((SUB, D)), _full_spec((1, D))]
    out_specs = [_erow(D)] + ([_erow(D)] if with_gate else []) + [_full_spec((SUB, D))]
    out_shape = [jax.ShapeDtypeStruct((s, D), F32)] + ([jax.ShapeDtypeStruct((s, D), BF16)] if with_gate else []) \
        + [jax.ShapeDtypeStruct((SUB, D), F32)]
    outs, moved = _call(body, ins, name=name, grid=(s // TRE,), in_specs=in_specs, out_specs=out_specs,
                        out_shape=out_shape, vmem_mb=32, hosted=hosted)
    return outs + (moved,) if hosted else outs


XIN_BLOCK = 3 * AW // LANES
BG_BLOCK = XIN_BLOCK + CW // LANES
CG_BLOCK = BG_BLOCK + CW // LANES


def _seq_spec(s, first_block):
    return pl.BlockSpec((s, LANES), lambda j, fb=first_block: (0, fb + j))


def _mixconv_fwd(proj, w):
    s = proj.shape[0]

    def body(xin_ref, bg_ref, cg_ref, w_ref, o_ref):
        cx = cg_ref[...] * xin_ref[...]
        cv, _, _ = _conv_taps(cx, None, w_ref[...])
        o_ref[...] = bg_ref[...] * cv

    return pl.pallas_call(
        body, name="mixconv_fwd", grid=(CW // LANES,),
        in_specs=[_seq_spec(s, XIN_BLOCK), _seq_spec(s, BG_BLOCK), _seq_spec(s, CG_BLOCK),
                  pl.BlockSpec((3, LANES), lambda j: (0, j))],
        out_specs=_seq_spec(s, 0), out_shape=jax.ShapeDtypeStruct((s, CW), F32),
        compiler_params=_params(("parallel",), 32),
    )(proj, proj, proj, w)


def _mixconv_bwd(dmixed, proj, w):
    s = proj.shape[0]

    def body(d_ref, xin_ref, bg_ref, cg_ref, w_ref, dxin_ref, dbg_ref, dcg_ref, dw_ref):
        wv = w_ref[...]
        xin, cg, dconv = xin_ref[...], cg_ref[...], d_ref[...]
        cx = cg * xin
        cv, s1, s2 = _conv_taps(cx, None, wv)
        dbg_ref[...] = (dconv * cv).astype(BF16)
        dcv = dconv * bg_ref[...]
        dw_ref[...] = jnp.zeros_like(dw_ref)
        dw_ref[0:1, :] = jnp.sum(dcv * s2, axis=0, keepdims=True)
        dw_ref[1:2, :] = jnp.sum(dcv * s1, axis=0, keepdims=True)
        dw_ref[2:3, :] = jnp.sum(dcv * cx, axis=0, keepdims=True)
        dcx = _conv_taps_t(dcv, None, wv)
        dcg_ref[...] = (dcx * xin).astype(BF16)
        dxin_ref[...] = (dcx * cg).astype(BF16)

    o = jax.ShapeDtypeStruct((s, CW), BF16)
    return pl.pallas_call(
        body, name="mixconv_bwd", grid=(CW // LANES,),
        in_specs=[_seq_spec(s, AW // LANES), _seq_spec(s, XIN_BLOCK), _seq_spec(s, BG_BLOCK), _seq_spec(s, CG_BLOCK),
                  pl.BlockSpec((3, LANES), lambda j: (0, j))],
        out_specs=(_seq_spec(s, 0), _seq_spec(s, 0), _seq_spec(s, 0), pl.BlockSpec((SUB, LANES), lambda j: (0, j))),
        out_shape=(o, o, o, jax.ShapeDtypeStruct((SUB, CW), F32)),
        compiler_params=_params(("parallel",), 32),
    )(dmixed, proj, proj, proj, w)


TA = 512
NEG = -1e30


def _causal_mask():
    row = lax.broadcasted_iota(jnp.int32, (TA, TA), 0)
    col = lax.broadcasted_iota(jnp.int32, (TA, TA), 1)
    return col <= row


def _attn_fwd(qp, kp, vp, hosted):
    s = qp.shape[0]
    nq = s // TA

    def body(q_ref, k_ref, v_ref, o_ref, lse_ref):
        i = pl.program_id(1)
        slabs = [slice(SLAB * hh, SLAB * (hh + 1)) for hh in range(2)]
        q = [q_ref[:, sl] for sl in slabs]

        def block(j, carry, masked):
            keys = pl.ds(pl.multiple_of(j * TA, TA), TA)
            ms, acc = carry
            m_out, parts = [], []
            for hh in range(2):
                sc = _dot(q[hh], k_ref[keys, slabs[hh]], NT)
                if masked:
                    sc = jnp.where(_causal_mask(), sc, NEG)
                m_new = jnp.maximum(ms[hh], jnp.max(sc, axis=-1, keepdims=True))
                p = jnp.exp(sc - m_new)
                parts.append(jnp.exp(ms[hh] - m_new) * acc[:, slabs[hh]] + _dot(p.astype(BF16), v_ref[keys, slabs[hh]]))
                m_out.append(m_new)
            return tuple(m_out), jnp.concatenate(parts, axis=1)

        init = ((jnp.full((TA, 1), NEG, F32), jnp.full((TA, 1), NEG, F32)), jnp.zeros((TA, 2 * SLAB), F32))
        carry = lax.fori_loop(0, i, lambda j, cr: block(j, cr, False), init)
        ms, acc = block(i, carry, True)
        for hh in range(2):
            l = acc[:, SLAB * hh + DH:SLAB * hh + DH + 1]
            o_ref[:, DH * hh:DH * (hh + 1)] = acc[:, SLAB * hh:SLAB * hh + DH] / l
            lse_ref[0, :, hh:hh + 1] = ms[hh] + jnp.log(l)

    (o, lse), moved = _call(
        body, (qp, kp, vp), name="attn_fwd", grid=(HEADS // 2, nq),
        in_specs=[pl.BlockSpec((TA, 2 * SLAB), lambda p, i: (i, p)),
                  pl.BlockSpec((s, 2 * SLAB), lambda p, i: (0, p)),
                  pl.BlockSpec((s, 2 * SLAB), lambda p, i: (0, p))],
        out_specs=[pl.BlockSpec((TA, LANES), lambda p, i: (i, p)), pl.BlockSpec((1, TA, 2), lambda p, i: (p, i, 0))],
        out_shape=[jax.ShapeDtypeStruct((s, AW), F32), jax.ShapeDtypeStruct((HEADS // 2, s, 2), F32)],
        vmem_mb=24, hosted=hosted)
    return o, lse, moved


def _attn_bwd(qp, kp, vp, dmixed, o, lse, hosted):
    s = qp.shape[0]
    nq = s // TA

    def body(q_ref, k_ref, v_ref, do_ref, o_ref, lse_ref, dq_ref, dk_ref, dv_ref, qb_ref, dob_ref):
        dk_ref[...] = jnp.zeros_like(dk_ref)
        dv_ref[...] = jnp.zeros_like(dv_ref)
        slabs = [slice(SLAB * hh, SLAB * (hh + 1)) for hh in range(2)]
        lane = lax.broadcasted_iota(jnp.int32, (TA, DH), 1)

        def q_block(i, _):
            i0 = pl.multiple_of(i * TA, TA)
            rows = pl.ds(i0, TA)
            for hh in range(2):
                half = slice(DH * hh, DH * (hh + 1))
                do = do_ref[rows, half]
                delta = jnp.sum(do * o_ref[rows, half], axis=-1, keepdims=True)
                dob_ref[hh, :, 0:DH] = do.astype(BF16)
                dob_ref[hh, :, DH:SLAB] = _lanes3(lane, 0, [-d for d in _split3(delta)], 0.0).astype(BF16)
                lse3 = _split3(lse_ref[0, rows, hh:hh + 1])
                qb_ref[hh, :, 0:DH] = q_ref[rows, SLAB * hh:SLAB * hh + DH]
                aug = q_ref[rows, SLAB * hh + DH:SLAB * (hh + 1)].astype(F32)
                qb_ref[hh, :, DH:SLAB] = _lanes3(lane, AUG_LSE, [-x for x in lse3], aug).astype(BF16)

            def block(j, dq, masked):
                keys = pl.ds(pl.multiple_of(j * TA, TA), TA)
                dv, dk, dqc = [], [], []
                for hh in range(2):
                    q, dob = qb_ref[hh], dob_ref[hh]
                    k = k_ref[keys, slabs[hh]]
                    sc = _dot(q, k, NT)
                    if masked:
                        sc = jnp.where(_causal_mask(), sc, NEG)
                    p = jnp.exp(sc)
                    dv.append(_dot(p.astype(BF16), dob, TN))
                    ds = (p * _dot(dob, v_ref[keys, slabs[hh]], NT)).astype(BF16)
                    dk.append(_dot(ds, q, TN))
                    dqc.append(_dot(ds, k))
                dv_ref[keys, :] += jnp.concatenate(dv, axis=1)
                dk_ref[keys, :] += jnp.concatenate(dk, axis=1)
                return dq + jnp.concatenate(dqc, axis=1)

            dq = lax.fori_loop(0, i, lambda j, acc: block(j, acc, False), jnp.zeros((TA, 2 * SLAB), F32))
            dq_ref[rows, :] = block(i, dq, True)
            return 0

        lax.fori_loop(0, nq, q_block, 0)

    pair = lambda p: (0, p)
    slab2 = pl.BlockSpec((s, 2 * SLAB), pair)
    seq = pl.BlockSpec((s, LANES), pair)
    small = pl.BlockSpec((1, s, 2), lambda p: (p, 0, 0))
    o32 = jax.ShapeDtypeStruct((s, HEADS * SLAB), F32)
    return _call(
        body, (qp, kp, vp, dmixed, o, lse), name="attn_bwd", grid=(HEADS // 2,),
        in_specs=[slab2, slab2, slab2, seq, seq, small], out_specs=[slab2, slab2, slab2], out_shape=[o32, o32, o32],
        scratch_shapes=[pltpu.VMEM((2, TA, SLAB), BF16), pltpu.VMEM((2, TA, SLAB), BF16)], vmem_mb=40, hosted=hosted)


def _qkv_post(dqp, dkp, dvp, proj, gq, gk, hosted):
    s = proj.shape[0]

    def body(dq_ref, dk_ref, dv_ref, q_ref, k_ref, gq_ref, gk_ref, dqo_ref, dko_ref, dvo_ref, df_ref, vec_ref):
        @pl.when(pl.program_id(0) == 0)
        def _():
            vec_ref[...] = jnp.zeros_like(vec_ref)

        def one(d_ref, x_ref, g_ref, o_ref, row, scale):
            dg = jnp.zeros((1, DH), F32)
            for h in range(HEADS):
                sl = slice(DH * h, DH * (h + 1))
                xv = x_ref[:, sl]
                r = lax.rsqrt(jnp.mean(xv * xv, axis=-1, keepdims=True) + EPS)
                xh = xv * r
                dn = d_ref[:, SLAB * h:SLAB * h + DH] * scale
                dg = dg + jnp.sum(dn * xh, axis=0, keepdims=True)
                dxh = dn * g_ref[...]
                o_ref[:, sl] = (r * (dxh - xh * jnp.mean(dxh * xh, axis=-1, keepdims=True))).astype(BF16)
            vec_ref[row:row + 1, 0:DH] += dg

        one(dq_ref, q_ref, gq_ref, dqo_ref, 0, QK_SCALE)
        one(dk_ref, k_ref, gk_ref, dko_ref, 1, 1.0)
        lane = lax.broadcasted_iota(jnp.int32, (TR, LANES), 1)
        df = jnp.zeros((TR, LANES), F32)
        for h in range(HEADS):
            dvo_ref[:, DH * h:DH * (h + 1)] = dv_ref[:, SLAB * h:SLAB * h + DH].astype(BF16)
            row_sum = dq_ref[:, SLAB * h + DH:SLAB * h + DH + 1]
            col_sum = dk_ref[:, SLAB * h + DH + AUG_ONE:SLAB * h + DH + AUG_ONE + 1]
            df = jnp.where(lane == h, row_sum - col_sum, df)
        df_ref[...] = df

    o = jax.ShapeDtypeStruct((s, AW), BF16)
    wide = _row_spec(HEADS * SLAB)
    return _call(
        body, (dqp, dkp, dvp, proj, proj, gq, gk), name="qkv_post", grid=(s // TR,),
        in_specs=[wide, wide, wide, _row_spec(AW, 0), _row_spec(AW, 1), _full_spec((1, DH)), _full_spec((1, DH))],
        out_specs=[_row_spec(AW), _row_spec(AW), _row_spec(AW), _row_spec(LANES), _full_spec((SUB, LANES))],
        out_shape=[o, o, o, jax.ShapeDtypeStruct((s, LANES), F32), jax.ShapeDtypeStruct((SUB, LANES), F32)],
        hosted=hosted)


TF = 256
NJ = DFF // TF
FFN_ROWS_FWD = 1024
FFN_ROWS_BWD = 1024


def _ffn_fwd(h2, wup_t, cw, wd):
    s = h2.shape[0]
    tr = FFN_ROWS_FWD
    nr = s // tr

    def body(h_ref, wu_ref, cg_ref, cv_ref, wd_ref, pg_ref, pv_ref, y_ref, halo_ref, act_ref):
        r, j = pl.program_id(0), pl.program_id(1)
        hv = h_ref[...]
        pg = _dot(hv, wu_ref[0], NT).astype(BF16)
        pv = _dot(hv, wu_ref[1], NT).astype(BF16)
        pg_ref[...] = pg
        pv_ref[...] = pv
        pgf, pvf = pg.astype(F32), pv.astype(F32)
        ug, _, _ = _conv_taps(pgf, jnp.where(r > 0, halo_ref[j, 0], 0.0), cg_ref[...])
        uv, _, _ = _conv_taps(pvf, jnp.where(r > 0, halo_ref[j, 1], 0.0), cv_ref[...])
        halo_ref[j, 0] = pgf[tr - SUB:tr, :]
        halo_ref[j, 1] = pvf[tr - SUB:tr, :]
        act = (ug * _sigmoid(ug) * uv).astype(BF16)
        for t in range(NJ):
            @pl.when(j == t)
            def _(t=t):
                act_ref[:, t * TF:(t + 1) * TF] = act

        @pl.when(j == NJ - 1)
        def _():
            y_ref[...] = _dot(act_ref[...], wd_ref[...])

    pre = jax.ShapeDtypeStruct((s, DFF), BF16)
    return pl.pallas_call(
        body, name="ffn_fwd", grid=(nr, NJ),
        in_specs=[pl.BlockSpec((tr, D), lambda r, j: (r, 0)),
                  pl.BlockSpec((2, TF, D), lambda r, j: (0, j, 0)),
                  pl.BlockSpec((3, TF), lambda r, j: (0, j)),
                  pl.BlockSpec((3, TF), lambda r, j: (0, NJ + j)),
                  pl.BlockSpec((DFF, D), lambda r, j: (0, 0))],
        out_specs=(pl.BlockSpec((tr, TF), lambda r, j: (r, j)),
                   pl.BlockSpec((tr, TF), lambda r, j: (r, j)),
                   pl.BlockSpec((tr, D), lambda r, j: (r, 0))),
        out_shape=(pre, pre, jax.ShapeDtypeStruct((s, D), F32)),
        scratch_shapes=[pltpu.VMEM((NJ, 2, SUB, TF), F32), pltpu.VMEM((tr, DFF), BF16)],
        compiler_params=_params(("arbitrary", "arbitrary"), 56),
    )(h2, wup_t, cw, cw, wd)


def _ffn_bwd(dy, h2, pre_g, pre_v, wup_t, cw, wd):
    s = h2.shape[0]
    tr = FFN_ROWS_BWD
    nr = s // tr
    hb = tr // (2 * SUB)

    def body(dy_ref, h_ref, pg_ref, pv_ref, hg_ref, hv_ref, wu_ref, cg_ref, cv_ref, wd_ref,
             dh_ref, dwu_ref, dwd_ref, dcg_ref, dcv_ref, nxt_ref, awu_ref, awd_ref):
        j, r = pl.program_id(0), pl.program_id(1)
        rr = nr - 1 - r
        row0 = pl.multiple_of(rr * tr, tr)
        cwg, cwv = cg_ref[...], cv_ref[...]
        pg, pv = pg_ref[...].astype(F32), pv_ref[...].astype(F32)
        ug, g1, g2 = _conv_taps(pg, jnp.where(rr > 0, hg_ref[SUB:2 * SUB, :].astype(F32), 0.0), cwg)
        uv, v1, v2 = _conv_taps(pv, jnp.where(rr > 0, hv_ref[SUB:2 * SUB, :].astype(F32), 0.0), cwv)
        sg = _sigmoid(ug)
        sil = ug * sg
        act = (sil * uv).astype(BF16)
        dyv = dy_ref[...]
        da = _dot(dyv, wd_ref[...], NT)
        dug = da * uv * (sg * (1.0 + ug * (1.0 - sg)))
        duv = da * sil
        dpg = _conv_taps_t(dug, jnp.where(r > 0, nxt_ref[0], 0.0), cwg)
        dpv = _conv_taps_t(duv, jnp.where(r > 0, nxt_ref[1], 0.0), cwv)
        nxt_ref[0] = dug[0:SUB, :]
        nxt_ref[1] = duv[0:SUB, :]
        dpgb, dpvb = dpg.astype(BF16), dpv.astype(BF16)
        hv = h_ref[...]
        dwd = _dot(act, dyv, TN)
        dpb = jnp.concatenate([dpgb, dpvb], axis=1)
        dwu = _dot(dpb, hv, TN)
        dh = _dot(dpb, wu_ref[...].reshape(2 * TF, D))

        def taps(du, x0, x1, x2):
            return (jnp.sum(du * x2, axis=0, keepdims=True), jnp.sum(du * x1, axis=0, keepdims=True),
                    jnp.sum(du * x0, axis=0, keepdims=True))

        tg, tv = taps(dug, pg, g1, g2), taps(duv, pv, v1, v2)

        @pl.when(r == 0)
        def _():
            awd_ref[...] = dwd
            awu_ref[...] = dwu
            dcg_ref[...] = jnp.zeros_like(dcg_ref)
            dcv_ref[...] = jnp.zeros_like(dcv_ref)

        @pl.when(r > 0)
        def _():
            awd_ref[...] += dwd
            awu_ref[...] += dwu

        @pl.when(r == nr - 1)
        def _():
            dwd_ref[...] = awd_ref[...].astype(BF16)
            dwu_ref[...] = awu_ref[...].astype(BF16).reshape(2, TF, D)

        for t in range(3):
            dcg_ref[t:t + 1, :] += tg[t]
            dcv_ref[t:t + 1, :] += tv[t]

        @pl.when(j == 0)
        def _():
            dh_ref[pl.ds(row0, tr), :] = dh

        @pl.when(j > 0)
        def _():
            dh_ref[pl.ds(row0, tr), :] += dh

    rows = lambda j, r: (nr - 1 - r, 0)
    tile = lambda j, r: (nr - 1 - r, j)
    halo = lambda j, r: (jnp.maximum((nr - 1 - r) * hb - 1, 0), j)
    return pl.pallas_call(
        body, name="ffn_bwd", grid=(NJ, nr),
        in_specs=[pl.BlockSpec((tr, D), rows), pl.BlockSpec((tr, D), rows),
                  pl.BlockSpec((tr, TF), tile), pl.BlockSpec((tr, TF), tile),
                  pl.BlockSpec((2 * SUB, TF), halo), pl.BlockSpec((2 * SUB, TF), halo),
                  pl.BlockSpec((2, TF, D), lambda j, r: (0, j, 0)),
                  pl.BlockSpec((3, TF), lambda j, r: (0, j)), pl.BlockSpec((3, TF), lambda j, r: (0, NJ + j)),
                  pl.BlockSpec((TF, D), lambda j, r: (j, 0))],
        out_specs=(pl.BlockSpec((s, D), lambda j, r: (0, 0)),
                   pl.BlockSpec((2, TF, D), lambda j, r: (0, j, 0)),
                   pl.BlockSpec((TF, D), lambda j, r: (j, 0)),
                   pl.BlockSpec((SUB, TF), lambda j, r: (0, j)), pl.BlockSpec((SUB, TF), lambda j, r: (0, j))),
        out_shape=(jax.ShapeDtypeStruct((s, D), F32),
                   jax.ShapeDtypeStruct((2, DFF, D), BF16), jax.ShapeDtypeStruct((DFF, D), BF16),
                   jax.ShapeDtypeStruct((SUB, DFF), F32), jax.ShapeDtypeStruct((SUB, DFF), F32)),
        scratch_shapes=[pltpu.VMEM((2, SUB, TF), F32), pltpu.VMEM((2 * TF, D), F32), pltpu.VMEM((TF, D), F32)],
        compiler_params=_params(("arbitrary", "arbitrary"), 56),
    )(dy, h2, pre_g, pre_v, pre_g, pre_v, wup_t, cw, cw, wd)


def _adam(w, g, m, v):
    m = ADAM_B1 * m + (1.0 - ADAM_B1) * g
    v = ADAM_B2 * v + (1.0 - ADAM_B2) * (g * g)
    m_hat = m / (1.0 - ADAM_B1 ** ADAM_STEP)
    v_hat = v / (1.0 - ADAM_B2 ** ADAM_STEP)
    delta = -ADAM_LR * (m_hat / (jnp.sqrt(v_hat) + ADAM_EPS) + ADAM_WD * w)
    return delta, m, v


NCHIP = NDEV // 2


def _pair_add(mine, theirs, tr, name):
    _, _, rws, cols = mine.shape

    def body(a_ref, b_ref, o_ref):
        c = lax.axis_index("c")
        o_ref[0] = (a_ref[0, c].astype(F32) + b_ref[0].astype(F32)).astype(BF16)

    (out,), _ = _call(
        body, (mine, theirs), name=name, grid=(NCHIP, rws // tr),
        in_specs=[pl.BlockSpec((1, 2, tr, cols), lambda q, i: (q, 0, i, 0)),
                  pl.BlockSpec((1, tr, cols), lambda q, i: (q, i, 0))],
        out_specs=[pl.BlockSpec((1, tr, cols), lambda q, i: (q, i, 0))],
        out_shape=[jax.ShapeDtypeStruct((NCHIP, rws, cols), BF16)], vmem_mb=16)
    return out


def _adamw_sharded(parts, w, m, v, tr, name, hosted=None):
    rws, cols = w.shape
    n_parts = parts.shape[0]

    def body(p_ref, w_ref, m_ref, v_ref, g_ref, d_ref, mo_ref, vo_ref):
        g = p_ref[0].astype(F32)
        for q in range(1, n_parts):
            g = g + p_ref[q].astype(F32)
        g_ref[...] = g
        d_ref[...], mo_ref[...], vo_ref[...] = _adam(w_ref[...], g, m_ref[...], v_ref[...])

    blk = pl.BlockSpec((tr, cols), lambda i: (i, 0))
    o = jax.ShapeDtypeStruct((rws, cols), F32)
    outs, moved = _call(
        body, (parts, w, m, v), name=name, grid=(rws // tr,),
        in_specs=[pl.BlockSpec((n_parts, tr, cols), lambda i: (0, i, 0)), blk, blk, blk],
        out_specs=[blk, blk, blk, blk], out_shape=[o, o, o, o], vmem_mb=44 if tr > 256 else 24, hosted=hosted)
    return (outs, moved) if hosted else outs


def _adamw_ada(c_all, dmod_my, w, m, v):
    rws, cols = w.shape
    tr = 256

    def body(c_ref, dm_ref, w_ref, m_ref, v_ref, g_ref, d_ref, mo_ref, vo_ref):
        cv = c_ref[...]
        act = cv * _sigmoid(cv)
        g = _dot(act, dm_ref[...], TN, lax.Precision.HIGHEST)
        g_ref[...] = g
        d_ref[...], mo_ref[...], vo_ref[...] = _adam(w_ref[...], g, m_ref[...], v_ref[...])

    blk = pl.BlockSpec((tr, cols), lambda i: (i, 0))
    o = jax.ShapeDtypeStruct((rws, cols), F32)
    return pl.pallas_call(
        body, name="adamw_ada", grid=(rws // tr,),
        in_specs=[pl.BlockSpec((NDEV, tr), lambda i: (0, i)), _full_spec((NDEV, cols)), blk, blk, blk],
        out_specs=(blk, blk, blk, blk), out_shape=(o, o, o, o),
        compiler_params=_params(("parallel",), 32),
    )(c_all, dmod_my, w, m, v)


REP_ROWS = 16
ROW_N1, ROW_N2, ROW_LOSS, ROW_MISC = 6, 7, 8, 9
LANE_BF, LANE_GQ, LANE_GK = 0, 128, 256


def _adamw_small(rep_all, conv_all, wmv):
    n_ff = wmv[6][0].shape[1]

    def body(*refs):
        rep_ref, conv_ref = refs[:2]
        ins = refs[2:2 + 24]
        outs = refs[2 + 24:]
        loss_ref, outs = outs[0], outs[1:]
        g_rep = rep_ref[0]
        g_conv = conv_ref[0]
        for d in range(1, NDEV):
            g_rep = g_rep + rep_ref[d]
            g_conv = g_conv + conv_ref[d]
        loss_ref[...] = (0.5 / D) * jnp.sum(g_rep[ROW_LOSS:ROW_LOSS + 1, :], axis=-1, keepdims=True)
        grads = [
            None,
            g_rep[ROW_N1:ROW_N1 + 1, :],
            g_rep[ROW_MISC:ROW_MISC + 1, LANE_BF:LANE_BF + HEADS],
            g_rep[ROW_MISC:ROW_MISC + 1, LANE_GQ:LANE_GQ + DH],
            g_rep[ROW_MISC:ROW_MISC + 1, LANE_GK:LANE_GK + DH],
            g_rep[ROW_N2:ROW_N2 + 1, :],
            g_conv[0:3, 0:n_ff],
            g_conv[0:3, n_ff:n_ff + DH],
        ]
        for p in range(8):
            w_ref, m_ref, v_ref = ins[3 * p:3 * p + 3]
            g_ref, d_ref, mo_ref, vo_ref = outs[4 * p:4 * p + 4]
            if p == 0:
                for nmod in range(NMOD):
                    sl = slice(D * nmod, D * (nmod + 1))
                    g = g_rep[nmod:nmod + 1, :]
                    g_ref[:, sl] = g
                    d_ref[:, sl], mo_ref[:, sl], vo_ref[:, sl] = _adam(w_ref[:, sl], g, m_ref[:, sl], v_ref[:, sl])
            else:
                g = grads[p]
                g_ref[...] = g
                d_ref[...], mo_ref[...], vo_ref[...] = _adam(w_ref[...], g, m_ref[...], v_ref[...])

    flat = [a for trio in wmv for a in trio]
    out_shape = [jax.ShapeDtypeStruct((1, 1), F32)]
    for trio in wmv:
        out_shape += [jax.ShapeDtypeStruct(trio[0].shape, F32)] * 4
    return pl.pallas_call(
        body, name="adamw_small", out_shape=tuple(out_shape),
        compiler_params=_params(None, 32),
    )(rep_all, conv_all, *flat)


FG_FIRST = 3 * AW
N_IN = DIN // NDEV


def _w_in_runs():
    runs = []
    for d in range(NDEV):
        lo, hi = N_IN * d, N_IN * (d + 1)
        for a, b, shift in ((0, FG_FIRST, 0), (FG_FIRST, FG_FIRST + HEADS, DIN - HEADS - FG_FIRST),
                            (FG_FIRST + HEADS, DIN, -HEADS)):
            a, b = max(a, lo), min(b, hi)
            if a < b:
                runs.append((d, a - lo, a + shift, b - a))
    return runs


W_IN_ROWS = 256
N_IN_PAD = 512


def _identity(n):
    return (lax.broadcasted_iota(jnp.int32, (n, n), 0) == lax.broadcasted_iota(jnp.int32, (n, n), 1)).astype(BF16)


def _assemble_w_in(g_in, hosted):
    def body(g_ref, o_ref, t_ref):
        eye = _identity(W_IN_ROWS)
        shard = None
        for d, src, dst, width in _w_in_runs():
            if d != shard:
                t_ref[:, 0:N_IN] = _dot(eye, g_ref[d], NT).astype(BF16)
                shard = d
            o_ref[:, dst:dst + width] = t_ref[:, src:src + width]
        o_ref[:, DIN:DINP] = jnp.zeros((W_IN_ROWS, DINP - DIN), o_ref.dtype)

    (out,), moved = _call(
        body, (g_in,), name="assemble_w_in", grid=(D // W_IN_ROWS,),
        in_specs=[pl.BlockSpec((NDEV, N_IN, W_IN_ROWS), lambda i: (0, 0, i))],
        out_specs=[pl.BlockSpec((W_IN_ROWS, DINP), lambda i: (i, 0))],
        out_shape=[jax.ShapeDtypeStruct((D, DINP), g_in.dtype)],
        scratch_shapes=[pltpu.VMEM((W_IN_ROWS, N_IN_PAD), BF16)], vmem_mb=16, hosted=hosted)
    return out, moved


def _scatter_dw_in(dwp):
    def body(w_ref, o_ref, t_ref):
        eye = _identity(W_IN_ROWS)
        runs = _w_in_runs()
        for i, (d, src, dst, width) in enumerate(runs):
            t_ref[:, src:src + width] = w_ref[:, dst:dst + width]
            if i + 1 == len(runs) or runs[i + 1][0] != d:
                o_ref[d // 2, d % 2] = _dot(t_ref[:, 0:N_IN], eye, TN).astype(BF16)

    (out,), _ = _call(
        body, (dwp,), name="scatter_dw_in", grid=(D // W_IN_ROWS,),
        in_specs=[pl.BlockSpec((W_IN_ROWS, DINP), lambda i: (i, 0))],
        out_specs=[pl.BlockSpec((NCHIP, 2, N_IN, W_IN_ROWS), lambda i: (0, 0, 0, i))],
        out_shape=[jax.ShapeDtypeStruct((NCHIP, 2, N_IN, D), dwp.dtype)],
        scratch_shapes=[pltpu.VMEM((W_IN_ROWS, N_IN_PAD), BF16)], vmem_mb=16)
    return out


def kernel(x, c, w_ada, b_ada, norm1_g, w_in, b_forget, q_norm_g, k_norm_g, conv_mix_w, w_out, norm2_g, w_up, ffn_conv_w, w_down, loss_target, m_w_ada, m_b_ada, m_norm1_g, m_w_in, m_b_forget, m_q_norm_g, m_k_norm_g, m_conv_mix_w, m_w_out, m_norm2_g, m_w_up, m_ffn_conv_w, m_w_down, v_w_ada, v_b_ada, v_norm1_g, v_w_in, v_b_forget, v_q_norm_g, v_k_norm_g, v_conv_mix_w, v_w_out, v_norm2_g, v_w_up, v_ffn_conv_w, v_w_down):
    me = 4 * lax.axis_index("x") + 2 * lax.axis_index("y") + lax.axis_index("c")
    xs, tgt = x[0], loss_target[0]
    n_ada = w_ada.shape[2]
    n_ff = w_up.shape[2]

    conv_w = jnp.concatenate([ffn_conv_w[0], conv_mix_w[0]], axis=1)
    conv_w = jnp.concatenate([conv_w, jnp.zeros((SUB - 3, conv_w.shape[1]), F32)], axis=0)
    c_all, conv_all, g_in = _exchange(
        [(c.reshape(SUB, D // SUB), "ag"), (conv_w, "ag"), (jnp.transpose(w_in[0]).astype(BF16), "ag2")],
        "exchange_w_in")
    g_in, w_out_b, w_up_b, w_down_b = lax.optimization_barrier(
        (g_in, w_out[0].astype(BF16), jnp.transpose(w_up[0]).astype(BF16), w_down[0].astype(BF16)))
    g_out, g_up, g_down = _sequencer_exchange(
        [(w_out_b, "ag2"), (w_up_b, "ag2"), (w_down_b, "ag2")], "gather_weights", collective_id=1)
    c_all = c_all.reshape(NDEV, D)
    cw_ffn = jnp.transpose(conv_all[:, :3, :n_ff], (1, 0, 2)).reshape(3, 2 * DFF)
    cw_mix = jnp.transpose(conv_all[:, :3, n_ff:], (1, 0, 2)).reshape(3, CW)

    b_my = lax.dynamic_slice(b_ada, (0, me * n_ada), (1, n_ada))
    mod_part = _ada_fwd(c_all, w_ada[0], b_my)
    w_in_p, (mod_rows,) = _assemble_w_in(
        g_in, [(jnp.broadcast_to(mod_part[:, None, :], (NDEV, SUB, n_ada)), "a2a")])
    mod = mod_rows[:, 0, :].reshape(NMOD, D)
    mod = jnp.concatenate([mod, jnp.zeros((SUB - NMOD, D), F32)], axis=0)

    h = _norm_mod_fwd(xs, mod, norm1_g)
    proj = _mm(h, w_in_p, "nn", F32, 1024, 640, "proj_fwd")
    bf_pad = jnp.concatenate([b_forget, jnp.zeros((1, LANES - HEADS), F32)], axis=1)
    fcum = _fgate_fwd(proj, bf_pad)
    (qp, kp, vp), _ = _qkv_prep(proj, fcum, q_norm_g, k_norm_g, None)
    attn, lse, _ = _attn_fwd(qp, kp, vp, None)
    w_out_f = g_out.reshape(D, D)
    w_up_t = g_up.reshape(2, DFF, D)
    w_down_f = g_down.reshape(DFF, D)
    conv = _mixconv_fwd(proj, cw_mix)
    mixed = jnp.concatenate([attn, conv], axis=1).astype(BF16)
    z = _mm(mixed, w_out_f, "nn", F32, 1024, 1024, "out_fwd")
    x1, h2 = _resid_norm2(xs, z, mod, norm2_g)
    pre_g, pre_v, y = _ffn_fwd(h2, w_up_t, cw_ffn, w_down_f)
    dout, dy, vec_l = _loss_head(x1, y, tgt, mod)

    dh2, dwup_t, dwd, dcw_g, dcw_v = _ffn_bwd(dy, h2, pre_g, pre_v, w_up_t, cw_ffn, w_down_f)
    s_down = dwd.reshape(NCHIP, 2, DFF // NDEV, D)
    s_up = dwup_t.reshape(NCHIP, 2, n_ff, D)
    dx1, dz, vec_2, (t_up, t_down) = _norm_mod_bwd(dh2, x1, dout, z, mod, norm2_g, 4, 2, "norm2_bwd",
                                                   hosted=[(s_up, "pair"), (s_down, "pair")])
    dwout = _mm(mixed, dz, "tn", BF16, 1024, 1024, "out_bwd_w")
    s_out = dwout.reshape(NCHIP, 2, D // NDEV, D)
    dmixed, (t_out,) = _mm(dz, w_out_f, "nt", F32, 1024, 1024, "out_bwd_x", hosted=[(s_out, "pair")])
    c_out = _pair_add(s_out, t_out, 128, "pair_add_out")
    c_up = _pair_add(s_up, t_up, 176, "pair_add_up")
    c_down = _pair_add(s_down, t_down, 176, "pair_add_down")
    dxin, dbg, dcg, dcw_mix = _mixconv_bwd(dmixed, proj, cw_mix)
    (dqp, dkp, dvp), (p_up, p_down, p_out) = _attn_bwd(
        qp, kp, vp, dmixed, attn, lse, [(c_up, "chips"), (c_down, "chips"), (c_out, "chips")])
    (dq, dk, dvb, dfcol, vec_qk), _ = _qkv_post(dqp, dkp, dvp, proj, q_norm_g, k_norm_g, None)
    dfg, vec_bf = _fgate_bwd(dfcol, proj, bf_pad)
    dproj = jnp.concatenate([dq, dk, dvb, dxin, dbg, dcg, dfg], axis=1)
    dwin_p = _mm(h, dproj, "tn", BF16, 1024, 640, "proj_bwd_w")
    s_in = _scatter_dw_in(dwin_p).reshape(NDEV, N_IN, D)
    (p_in,) = _sequencer_exchange([(s_in, "a2a")], "scatter_dw_in_partials", collective_id=2, all_peers=True)
    dh = _mm(dproj, w_in_p, "nt", F32, 1024, 512, "proj_bwd_x", vmem_mb=36)
    grad_x, vec_1 = _norm_mod_bwd(dh, xs, dx1, None, mod, norm1_g, 1, None, "norm1_bwd")

    misc = jnp.zeros((1, D), F32)
    misc = lax.dynamic_update_slice(misc, vec_bf[0:1, :HEADS], (0, LANE_BF))
    misc = lax.dynamic_update_slice(misc, vec_qk[0:1, :DH], (0, LANE_GQ))
    misc = lax.dynamic_update_slice(misc, vec_qk[1:2, :DH], (0, LANE_GK))
    rep = jnp.concatenate([
        vec_1[0:1], vec_1[1:2], vec_2[3:4], vec_2[0:1], vec_2[1:2], vec_l[0:1],
        vec_1[2:3], vec_2[2:3], vec_l[1:2], misc, jnp.zeros((REP_ROWS - 10, D), F32)], axis=0)
    dcw_ffn = jnp.concatenate([dcw_g, dcw_v], axis=1).reshape(SUB, NDEV, n_ff)
    dcw_all = jnp.concatenate([jnp.transpose(dcw_ffn, (1, 0, 2)),
                               jnp.transpose(dcw_mix.reshape(SUB, NDEV, DH), (1, 0, 2))], axis=2)
    r_up = _adamw_sharded(p_up, jnp.transpose(w_up[0]), jnp.transpose(m_w_up[0]), jnp.transpose(v_w_up[0]), 176,
                          "adamw_up")
    r_down = _adamw_sharded(p_down, w_down[0], m_w_down[0], v_w_down[0], 176, "adamw_down")
    rep, dcw_all, r_up, r_down = lax.optimization_barrier((rep, dcw_all, r_up, r_down))
    r_up = tuple(jnp.transpose(a) for a in r_up)
    r_out, (rep_all, conv_parts) = _adamw_sharded(p_out, w_out[0], m_w_out[0], v_w_out[0], 128, "adamw_out",
                                                  hosted=[(rep, "ag"), (dcw_all, "a2a")])
    dmod_my = lax.dynamic_slice(rep_all[:, :NMOD, :].reshape(NDEV, NMOD * D), (0, me * n_ada), (NDEV, n_ada))
    r_ada = _adamw_ada(c_all, dmod_my, w_ada[0], m_w_ada[0], v_w_ada[0])
    r_in = _adamw_sharded(p_in, jnp.transpose(w_in[0]), jnp.transpose(m_w_in[0]), jnp.transpose(v_w_in[0]), N_IN,
                          "adamw_in")
    r_in = tuple(jnp.transpose(a) for a in r_in)
    small = _adamw_small(rep_all, conv_parts, [
        [b_ada, m_b_ada, v_b_ada], [norm1_g, m_norm1_g, v_norm1_g], [b_forget, m_b_forget, v_b_forget],
        [q_norm_g, m_q_norm_g, v_q_norm_g], [k_norm_g, m_k_norm_g, v_k_norm_g], [norm2_g, m_norm2_g, v_norm2_g],
        [ffn_conv_w[0], m_ffn_conv_w[0], v_ffn_conv_w[0]], [conv_mix_w[0], m_conv_mix_w[0], v_conv_mix_w[0]]])
    loss = small[0].reshape(())
    r_bada, r_n1, r_bf, r_gq, r_gk, r_n2, r_cf, r_cm = [small[1 + 4 * p:5 + 4 * p] for p in range(8)]
    lead = lambda t: tuple(a[None] for a in t)
    per_w = [lead(r_ada), r_bada, r_n1, lead(r_in), r_bf, r_gq, r_gk, lead(r_cm), lead(r_out), r_n2,
             lead(r_up), lead(r_cf), lead(r_down)]
    outs = [loss, grad_x[None]]
    for field in range(4):
        outs += [t[field] for t in per_w]
    return tuple(outs)
```

```python
import functools

import jax
import jax.numpy as jnp
import numpy as np
from jax import lax
from jax.experimental import pallas as pl
from jax.experimental.pallas import tpu as pltpu
from jax.experimental.pallas import tpu_sc as plsc

F32 = jnp.float32
BF16 = jnp.bfloat16

NDEV = 8
D = 1024
HEADS = 8
DH = 64
AW = 512
CW = 512
DFF = 2816
DIN = 3080
DINP = 3200
NMOD = 6
EPS = 1e-6
QK_SCALE = 0.125
LANES = 128
SUB = 8

ADAM_LR = 0.001
ADAM_B1 = 0.9
ADAM_B2 = 0.999
ADAM_EPS = 1e-08
ADAM_WD = 0.01
ADAM_STEP = 10

MESH = pl.DeviceIdType.MESH
ANY = pl.BlockSpec(memory_space=pl.ANY)

NN = (((1,), (0,)), ((), ()))
NT = (((1,), (1,)), ((), ()))
TN = (((0,), (0,)), ((), ()))


def _dot(a, b, dims=NN, precision=None):
    return lax.dot_general(a, b, dims, precision=precision, preferred_element_type=F32)


def _params(sem=None, vmem_mb=None):
    kw = {}
    if sem is not None:
        kw["dimension_semantics"] = sem
    if vmem_mb is not None:
        kw["vmem_limit_bytes"] = vmem_mb * 1024 * 1024
    return pltpu.CompilerParams(**kw)


def _sigmoid(x):
    return 0.5 * jnp.tanh(0.5 * x) + 0.5


class _Exchange:
    def __init__(self, items):
        self.arrays = [pltpu.with_memory_space_constraint(a, pltpu.HBM) for a, _ in items]
        self.modes = [m for _, m in items]
        self.n = len(items)
        self.out_shape = []
        for a, m in items:
            sh = {"ag": (NDEV,) + a.shape, "ag2": (NDEV,) + a.shape, "pair": a.shape[:1] + a.shape[2:]}.get(m, a.shape)
            self.out_shape.append(jax.ShapeDtypeStruct(sh, a.dtype))
        self.scratch = [pltpu.SemaphoreType.DMA((self.n, NDEV - 1)), pltpu.SemaphoreType.DMA((self.n, NDEV - 1)),
                        pltpu.SemaphoreType.DMA((self.n,))]

    def _plan(self, srcs, outs, sems):
        send_sems, recv_sems, loc_sems = sems
        x, y, c = lax.axis_index("x"), lax.axis_index("y"), lax.axis_index("c")
        me, my_chip = 4 * x + 2 * y + c, 2 * x + y
        sib = (x, y, 1 - c)
        local, first, landed, forwards, arrivals = [], [], [], [], []

        def remote(a, k, src, dst, to):
            return pltpu.make_async_remote_copy(src_ref=src, dst_ref=dst, send_sem=send_sems.at[a, k],
                                                recv_sem=recv_sems.at[a, k], device_id=to, device_id_type=MESH)

        for a, mode in enumerate(self.modes):
            src, out = srcs[a], outs[a]
            if mode in ("ag", "a2a"):
                piece = (lambda slot, src=src: src) if mode == "ag" else (lambda slot, src=src: src.at[slot])
                local.append(pltpu.make_async_copy(piece(me), out.at[me], loc_sems.at[a]))
                for r in range(1, NDEV):
                    px = 1 - x if (r >> 2) & 1 else x
                    py = 1 - y if (r >> 1) & 1 else y
                    pc = 1 - c if r & 1 else c
                    pidx = 4 * px + 2 * py + pc
                    first.append(remote(a, r - 1, piece(pidx), out.at[me], (px, py, pc)))
                    arrivals.append(remote(a, r - 1, piece(pidx), out.at[pidx], (px, py, pc)))
            elif mode == "ag2":
                local.append(pltpu.make_async_copy(src, out.at[me], loc_sems.at[a]))
                first.append(remote(a, 0, src, out.at[me], sib))
                arrivals.append(remote(a, 0, src, out.at[me + 1 - 2 * c], sib))
                for j, (px, py) in enumerate([(1 - x, y), (x, 1 - y), (1 - x, 1 - y)]):
                    theirs = out.at[4 * px + 2 * py + c]
                    first.append(remote(a, 1 + j, src, out.at[me], (px, py, c)))
                    landed.append(remote(a, 1 + j, src, theirs, (px, py, c)))
                    forwards.append(remote(a, 4 + j, theirs, theirs, sib))
                    arrivals.append(remote(a, 4 + j, src, out.at[4 * px + 2 * py + 1 - c], sib))
            elif mode == "pair":
                for q in range(NDEV // 2):
                    first.append(remote(a, q, src.at[q, 1 - c], out.at[q], sib))
                    arrivals.append(remote(a, q, src.at[q, 1 - c], out.at[q], sib))
            else:
                assert mode == "chips", mode
                local.append(pltpu.make_async_copy(src.at[my_chip], out.at[my_chip], loc_sems.at[a]))
                for j, (px, py) in enumerate([(1 - x, y), (x, 1 - y), (1 - x, 1 - y)]):
                    q = 2 * px + py
                    first.append(remote(a, 1 + j, src.at[q], out.at[my_chip], (px, py, c)))
                    arrivals.append(remote(a, 1 + j, src.at[q], out.at[q], (px, py, c)))
        return local, first, landed, forwards, arrivals

    def start(self, srcs, outs, sems):
        local, first, _, _, _ = self._plan(srcs, outs, sems)
        for cp in local + first:
            cp.start()

    def wait(self, srcs, outs, sems):
        local, first, landed, forwards, arrivals = self._plan(srcs, outs, sems)
        for cp, fwd in zip(landed, forwards):
            cp.wait_recv()
            fwd.start()
        for cp in arrivals:
            cp.wait_recv()
        for cp in first + forwards:
            cp.wait_send()
        for cp in local:
            cp.wait()


def _exchange(items, name):
    ex = _Exchange(items)
    n = ex.n

    def body(*refs):
        srcs, outs, sems = refs[:n], refs[n:2 * n], refs[2 * n:]
        ex.start(srcs, outs, sems)
        ex.wait(srcs, outs, sems)

    return pl.pallas_call(
        body, name=name,
        out_shape=tuple(ex.out_shape),
        in_specs=[ANY] * n, out_specs=tuple([ANY] * n),
        scratch_shapes=ex.scratch,
        compiler_params=pltpu.CompilerParams(has_side_effects=True),
    )(*ex.arrays)


def _sequencer_exchange(items, name, collective_id, all_peers=False):
    ex = _Exchange(items)
    srcs = [jax.new_ref(a, memory_space=pltpu.MemorySpace.HBM) for a in ex.arrays]
    outs = [jax.empty_ref(sh, memory_space=pltpu.MemorySpace.HBM) for sh in ex.out_shape]

    @pl.kernel(mesh=plsc.ScalarSubcoreMesh(axis_name="sequencer", num_cores=1), name=name,
               scratch_types=tuple(ex.scratch), compiler_params=pltpu.CompilerParams(collective_id=collective_id))
    def launch(send_sems, recv_sems, loc_sems):
        x, y, c = lax.axis_index("x"), lax.axis_index("y"), lax.axis_index("c")
        barrier = pltpu.get_barrier_semaphore()
        peers = [(x, y, 1 - c), (1 - x, y, c), (x, 1 - y, c), (1 - x, 1 - y, c)]
        if all_peers:
            peers += [(1 - x, y, 1 - c), (x, 1 - y, 1 - c), (1 - x, 1 - y, 1 - c)]
        for peer in peers:
            pl.semaphore_signal(barrier, inc=1, device_id=peer, device_id_type=MESH)
        pl.semaphore_wait(barrier, len(peers))
        sems = (send_sems, recv_sems, loc_sems)
        ex.start(srcs, outs, sems)
        ex.wait(srcs, outs, sems)

    launch()
    return [o[...] for o in outs]


def _call(body, inputs, *, name, grid, in_specs, out_specs, out_shape, scratch_shapes=(), vmem_mb=None, hosted=None):
    out_specs, out_shape, scratch_shapes = tuple(out_specs), tuple(out_shape), list(scratch_shapes)
    if not hosted:
        res = pl.pallas_call(
            body, name=name, grid=grid, in_specs=list(in_specs), out_specs=out_specs, out_shape=out_shape,
            scratch_shapes=scratch_shapes, compiler_params=_params(("arbitrary",) * len(grid), vmem_mb),
        )(*inputs)
        return tuple(res), ()
    ex = _Exchange(hosted)
    n, n_in, n_out, n_scr = ex.n, len(inputs), len(out_shape), len(scratch_shapes)

    def hosting_body(*refs):
        ins, srcs = refs[:n_in], refs[n_in:n_in + n]
        outs, landing = refs[n_in + n:n_in + n + n_out], refs[n_in + n + n_out:n_in + 2 * n + n_out]
        scratch, sems = refs[n_in + 2 * n + n_out:n_in + 2 * n + n_out + n_scr], refs[n_in + 2 * n + n_out + n_scr:]
        first = functools.reduce(jnp.logical_and, [pl.program_id(d) == 0 for d in range(len(grid))])
        last = functools.reduce(jnp.logical_and, [pl.program_id(d) == grid[d] - 1 for d in range(len(grid))])

        @pl.when(first)
        def _():
            ex.start(srcs, landing, sems)

        body(*ins, *outs, *scratch)

        @pl.when(last)
        def _():
            ex.wait(srcs, landing, sems)

    res = pl.pallas_call(
        hosting_body, name=name, grid=grid,
        in_specs=list(in_specs) + [ANY] * n, out_specs=out_specs + tuple([ANY] * n),
        out_shape=out_shape + tuple(ex.out_shape), scratch_shapes=scratch_shapes + ex.scratch,
        compiler_params=_params(("arbitrary",) * len(grid), vmem_mb),
    )(*inputs, *ex.arrays)
    return tuple(res[:n_out]), tuple(res[n_out:])


def _mm(a, b, mode, out_dtype, tm, tn, name, hosted=None, vmem_mb=24):
    if mode == "nn":
        (m, k), n = a.shape, b.shape[1]
        a_spec = pl.BlockSpec((tm, k), lambda i, j: (i, 0))
        b_spec = pl.BlockSpec((k, tn), lambda i, j: (0, j))
        dims = NN
    elif mode == "nt":
        (m, k), n = a.shape, b.shape[0]
        a_spec = pl.BlockSpec((tm, k), lambda i, j: (i, 0))
        b_spec = pl.BlockSpec((tn, k), lambda i, j: (j, 0))
        dims = NT
    else:
        (k, m), n = a.shape, b.shape[1]
        a_spec = pl.BlockSpec((k, tm), lambda i, j: (0, i))
        b_spec = pl.BlockSpec((k, tn), lambda i, j: (0, j))
        dims = TN
    assert m % tm == 0 and n % tn == 0, (m, n, tm, tn)

    def body(a_ref, b_ref, o_ref):
        o_ref[...] = _dot(a_ref[...], b_ref[...], dims).astype(o_ref.dtype)

    (out,), moved = _call(
        body, (a, b), name=name, grid=(m // tm, n // tn),
        in_specs=[a_spec, b_spec], out_specs=[pl.BlockSpec((tm, tn), lambda i, j: (i, j))],
        out_shape=[jax.ShapeDtypeStruct((m, n), out_dtype)], vmem_mb=vmem_mb, hosted=hosted)
    return (out, moved) if hosted else out


def _shift_down(x, k, fill):
    y = pltpu.roll(x, k, 0)
    row = lax.broadcasted_iota(jnp.int32, (SUB, x.shape[1]), 0)
    head = y[0:SUB, :]
    for t in range(k):
        head = jnp.where(row == t, fill[t], head)
    return jnp.concatenate([head, y[SUB:, :]], axis=0)


def _shift_up(x, k, fill):
    n = x.shape[0]
    y = pltpu.roll(x, n - k, 0)
    row = lax.broadcasted_iota(jnp.int32, (SUB, x.shape[1]), 0)
    tail = y[n - SUB:, :]
    for t in range(k):
        tail = jnp.where(row == SUB - k + t, fill[t], tail)
    return jnp.concatenate([y[:n - SUB, :], tail], axis=0)


def _conv_taps(x, halo, w):
    if halo is None:
        f1, f2 = [0.0], [0.0, 0.0]
    else:
        f1, f2 = [halo[7:8, :]], [halo[6:7, :], halo[7:8, :]]
    s1 = _shift_down(x, 1, f1)
    s2 = _shift_down(x, 2, f2)
    u = w[2:3, :] * x + w[1:2, :] * s1 + w[0:1, :] * s2
    return u, s1, s2


def _conv_taps_t(du, nxt, w):
    if nxt is None:
        f1, f2 = [0.0], [0.0, 0.0]
    else:
        f1, f2 = [nxt[0:1, :]], [nxt[0:1, :], nxt[1:2, :]]
    return w[2:3, :] * du + w[1:2, :] * _shift_up(du, 1, f1) + w[0:1, :] * _shift_up(du, 2, f2)


def _ada_fwd(c_all, w_ada, b_my):
    def body(c_ref, w_ref, b_ref, o_ref):
        cv = c_ref[...]
        act = cv * _sigmoid(cv)
        o_ref[...] = _dot(act, w_ref[...], NN, lax.Precision.HIGHEST) + b_ref[...]

    return pl.pallas_call(
        body, name="ada_fwd",
        out_shape=jax.ShapeDtypeStruct((NDEV, w_ada.shape[1]), F32),
        compiler_params=_params(None, 32),
    )(c_all, w_ada, b_my)


TR = 256
TRE = 512


def _row_spec(width, col=0, rows=TR):
    return pl.BlockSpec((rows, width), lambda i, col=col: (i, col))


def _erow(width):
    return _row_spec(width, rows=TRE)


def _full_spec(shape):
    return pl.BlockSpec(shape, lambda i: (0,) * len(shape))


def _norm_mod_fwd(x, mod, g):
    s = x.shape[0]

    def body(x_ref, mod_ref, g_ref, h_ref):
        xv = x_ref[...]
        r = lax.rsqrt(jnp.mean(xv * xv, axis=-1, keepdims=True) + EPS)
        nrm = xv * r * g_ref[...]
        h_ref[...] = (nrm * (1.0 + mod_ref[1:2, :]) + mod_ref[0:1, :]).astype(BF16)

    return pl.pallas_call(
        body, name="norm1_fwd", grid=(s // TRE,),
        in_specs=[_erow(D), _full_spec((SUB, D)), _full_spec((1, D))],
        out_specs=_erow(D), out_shape=jax.ShapeDtypeStruct((s, D), BF16),
        compiler_params=_params(("parallel",), 16),
    )(x, mod, g)


SLAB = 2 * DH
AUG_F, AUG_ONE, AUG_LSE = 0, 3, 6


def _split3(x):
    hi = x.astype(BF16).astype(F32)
    r1 = x - hi
    mid = r1.astype(BF16).astype(F32)
    return hi, mid, r1 - mid


def _lanes3(lane, first, pieces, other):
    out = other
    for k in range(3):
        out = jnp.where(lane == first + k, pieces[k], out)
    return out


def _aug_placement():
    eq = np.zeros((3 * LANES, HEADS * SLAB), np.float32)
    ek = np.zeros((3 * LANES, HEADS * SLAB), np.float32)
    ones = np.zeros((SUB, HEADS * SLAB), np.float32)
    for h in range(HEADS):
        aug = SLAB * h + DH
        for k in range(3):
            eq[LANES * k + h, aug + AUG_F + k] = 1.0
            ek[LANES * k + h, aug + AUG_ONE + k] = -1.0
            ones[0, aug + AUG_ONE + k] = 1.0
            ones[1, aug + AUG_F + k] = ones[1, aug + AUG_LSE + k] = 1.0
            ones[2, aug + k] = 1.0
    return jnp.asarray(eq, BF16), jnp.asarray(ek, BF16), jnp.asarray(ones)


def _qkv_prep(proj, fcum, gq, gk, hosted):
    s = proj.shape[0]

    def body(q_ref, k_ref, v_ref, f_ref, gq_ref, gk_ref, eq_ref, ek_ref, ones_ref, qo_ref, ko_ref, vo_ref):
        f3 = jnp.concatenate(_split3(f_ref[...]), axis=1).astype(BF16)
        qo_ref[...] = (_dot(f3, eq_ref[...]) + ones_ref[0:1, :]).astype(BF16)
        ko_ref[...] = (_dot(f3, ek_ref[...]) + ones_ref[1:2, :]).astype(BF16)
        vo_ref[...] = jnp.broadcast_to(ones_ref[2:3, :], vo_ref.shape).astype(BF16)
        for h in range(HEADS):
            sl = slice(DH * h, DH * (h + 1))
            lo = slice(SLAB * h, SLAB * h + DH)
            qh = q_ref[:, sl]
            r = lax.rsqrt(jnp.mean(qh * qh, axis=-1, keepdims=True) + EPS)
            qo_ref[:, lo] = (qh * r * gq_ref[...] * QK_SCALE).astype(BF16)
            kh = k_ref[:, sl]
            r = lax.rsqrt(jnp.mean(kh * kh, axis=-1, keepdims=True) + EPS)
            ko_ref[:, lo] = (kh * r * gk_ref[...]).astype(BF16)
            vo_ref[:, lo] = v_ref[:, sl].astype(BF16)

    eq, ek, ones = _aug_placement()
    o = jax.ShapeDtypeStruct((s, HEADS * SLAB), BF16)
    wide = _row_spec(HEADS * SLAB)
    return _call(
        body, (proj, proj, proj, fcum, gq, gk, eq, ek, ones), name="qkv_prep", grid=(s // TR,),
        in_specs=[_row_spec(AW, 0), _row_spec(AW, 1), _row_spec(AW, 2), _row_spec(LANES),
                  _full_spec((1, DH)), _full_spec((1, DH)), _full_spec(eq.shape), _full_spec(ek.shape),
                  _full_spec(ones.shape)],
        out_specs=[wide, wide, wide], out_shape=[o, o, o], vmem_mb=16, hosted=hosted)


FG_BLOCK = (3 * AW + 3 * CW) // LANES


def _fgate_fwd(proj, bf_pad):
    s = proj.shape[0]

    def body(fg_ref, b_ref, o_ref, carry_ref):
        i = pl.program_id(0)

        @pl.when(i == 0)
        def _():
            carry_ref[...] = jnp.zeros_like(carry_ref)

        z = fg_ref[...] + b_ref[...]
        logf = jnp.minimum(z, 0.0) - jnp.log1p(jnp.exp(-jnp.abs(z)))
        row = lax.broadcasted_iota(jnp.int32, (TR, TR), 0)
        col = lax.broadcasted_iota(jnp.int32, (TR, TR), 1)
        tri = (col <= row).astype(F32)
        cs = _dot(tri, logf, NN, lax.Precision.HIGHEST) + carry_ref[0:1, :]
        o_ref[...] = cs
        carry_ref[...] = jnp.broadcast_to(cs[TR - 1:TR, :], carry_ref.shape)

    return pl.pallas_call(
        body, name="fgate_fwd", grid=(s // TR,),
        in_specs=[_row_spec(LANES, FG_BLOCK), _full_spec((1, LANES))],
        out_specs=_row_spec(LANES), out_shape=jax.ShapeDtypeStruct((s, LANES), F32),
        scratch_shapes=[pltpu.VMEM((SUB, LANES), F32)],
        compiler_params=_params(("arbitrary",)),
    )(proj, bf_pad)


def _fgate_bwd(dfcol, proj, bf_pad):
    s = proj.shape[0]
    nb = s // TR

    def body(df_ref, fg_ref, b_ref, o_ref, db_ref, carry_ref):
        i = pl.program_id(0)

        @pl.when(i == 0)
        def _():
            carry_ref[...] = jnp.zeros_like(carry_ref)
            db_ref[...] = jnp.zeros_like(db_ref)

        row = lax.broadcasted_iota(jnp.int32, (TR, TR), 0)
        col = lax.broadcasted_iota(jnp.int32, (TR, TR), 1)
        tri = (col >= row).astype(F32)
        dlogf = _dot(tri, df_ref[...], NN, lax.Precision.HIGHEST) + carry_ref[0:1, :]
        carry_ref[...] = jnp.broadcast_to(dlogf[0:1, :], carry_ref.shape)
        z = fg_ref[...] + b_ref[...]
        dfg = dlogf * _sigmoid(-z)
        o_ref[...] = dfg.astype(BF16)
        db_ref[0:1, :] += jnp.sum(dfg, axis=0, keepdims=True)

    rev = lambda col: pl.BlockSpec((TR, LANES), lambda i, col=col: (nb - 1 - i, col))
    return pl.pallas_call(
        body, name="fgate_bwd", grid=(nb,),
        in_specs=[rev(0), rev(FG_BLOCK), _full_spec((1, LANES))],
        out_specs=(rev(0), _full_spec((SUB, LANES))),
        out_shape=(jax.ShapeDtypeStruct((s, LANES), BF16), jax.ShapeDtypeStruct((SUB, LANES), F32)),
        scratch_shapes=[pltpu.VMEM((SUB, LANES), F32)],
        compiler_params=_params(("arbitrary",)),
    )(dfcol, proj, bf_pad)


def _resid_norm2(x, z, mod, g):
    s = x.shape[0]

    def body(x_ref, z_ref, mod_ref, g_ref, x1_ref, h_ref):
        x1 = x_ref[...] + mod_ref[2:3, :] * z_ref[...]
        x1_ref[...] = x1
        r = lax.rsqrt(jnp.mean(x1 * x1, axis=-1, keepdims=True) + EPS)
        nrm = x1 * r * g_ref[...]
        h_ref[...] = (nrm * (1.0 + mod_ref[4:5, :]) + mod_ref[3:4, :]).astype(BF16)

    return pl.pallas_call(
        body, name="resid_norm2", grid=(s // TRE,),
        in_specs=[_erow(D), _erow(D), _full_spec((SUB, D)), _full_spec((1, D))],
        out_specs=(_erow(D), _erow(D)),
        out_shape=(jax.ShapeDtypeStruct((s, D), F32), jax.ShapeDtypeStruct((s, D), BF16)),
        compiler_params=_params(("parallel",), 24),
    )(x, z, mod, g)


def _loss_head(x1, y, tgt, mod):
    s = x1.shape[0]

    def body(x1_ref, y_ref, t_ref, mod_ref, dout_ref, dy_ref, vec_ref):
        @pl.when(pl.program_id(0) == 0)
        def _():
            vec_ref[...] = jnp.zeros_like(vec_ref)

        yv = y_ref[...]
        g2 = mod_ref[5:6, :]
        diff = x1_ref[...] + g2 * yv - t_ref[...]
        dout = diff * (1.0 / D)
        dout_ref[...] = dout
        dy_ref[...] = (g2 * dout).astype(BF16)
        vec_ref[0:1, :] += jnp.sum(dout * yv, axis=0, keepdims=True)
        vec_ref[1:2, :] += jnp.sum(diff * diff, axis=0, keepdims=True)

    return pl.pallas_call(
        body, name="loss_head", grid=(s // TRE,),
        in_specs=[_erow(D), _erow(D), _erow(D), _full_spec((SUB, D))],
        out_specs=(_erow(D), _erow(D), _full_spec((SUB, D))),
        out_shape=(jax.ShapeDtypeStruct((s, D), F32), jax.ShapeDtypeStruct((s, D), BF16),
                   jax.ShapeDtypeStruct((SUB, D), F32)),
        compiler_params=_params(("arbitrary",), 24),
    )(x1, y, tgt, mod)


def _norm_mod_bwd(dh, xin, dres, zin, mod, g, scale_row, gate_row, name, hosted=None):
    s = dh.shape[0]
    with_gate = gate_row is not None

    def body(*refs):
        if with_gate:
            dh_ref, x_ref, dres_ref, z_ref, mod_ref, g_ref, dx_ref, dz_ref, vec_ref = refs
        else:
            dh_ref, x_ref, dres_ref, mod_ref, g_ref, dx_ref, vec_ref = refs

        @pl.when(pl.program_id(0) == 0)
        def _():
            vec_ref[...] = jnp.zeros_like(vec_ref)

        xv = x_ref[...]
        dhv = dh_ref[...]
        gv = g_ref[...]
        r = lax.rsqrt(jnp.mean(xv * xv, axis=-1, keepdims=True) + EPS)
        xh = xv * r
        dn = dhv * (1.0 + mod_ref[scale_row:scale_row + 1, :])
        dxh = dn * gv
        dx = dres_ref[...] + r * (dxh - xh * jnp.mean(dxh * xh, axis=-1, keepdims=True))
        dx_ref[...] = dx
        vec_ref[0:1, :] += jnp.sum(dhv, axis=0, keepdims=True)
        vec_ref[1:2, :] += jnp.sum(dhv * (xh * gv), axis=0, keepdims=True)
        vec_ref[2:3, :] += jnp.sum(dn * xh, axis=0, keepdims=True)
        if with_gate:
            dz_ref[...] = (mod_ref[gate_row:gate_row + 1, :] * dx).astype(BF16)
            vec_ref[3:4, :] += jnp.sum(dx * z_ref[...], axis=0, keepdims=True)

    ins = [dh, xin, dres] + ([zin] if with_gate else []) + [mod, g]
    in_specs = [_erow(D)] * (4 if with_gate else 3) + [_full_spec((SUB, D)), _full_spec((1, D))]
    out_specs = [_erow(D)] + ([_erow(D)] if with_gate else []) + [_full_spec((SUB, D))]
    out_shape = [jax.ShapeDtypeStruct((s, D), F32)] + ([jax.ShapeDtypeStruct((s, D), BF16)] if with_gate else []) \
        + [jax.ShapeDtypeStruct((SUB, D), F32)]
    outs, moved = _call(body, ins, name=name, grid=(s // TRE,), in_specs=in_specs, out_specs=out_specs,
                        out_shape=out_shape, vmem_mb=32, hosted=hosted)
    return outs + (moved,) if hosted else outs


XIN_BLOCK = 3 * AW // LANES
BG_BLOCK = XIN_BLOCK + CW // LANES
CG_BLOCK = BG_BLOCK + CW // LANES


def _seq_spec(s, first_block):
    return pl.BlockSpec((s, LANES), lambda j, fb=first_block: (0, fb + j))


def _mixconv_fwd(proj, w):
    s = proj.shape[0]

    def body(xin_ref, bg_ref, cg_ref, w_ref, o_ref):
        cx = cg_ref[...] * xin_ref[...]
        cv, _, _ = _conv_taps(cx, None, w_ref[...])
        o_ref[...] = bg_ref[...] * cv

    return pl.pallas_call(
        body, name="mixconv_fwd", grid=(CW // LANES,),
        in_specs=[_seq_spec(s, XIN_BLOCK), _seq_spec(s, BG_BLOCK), _seq_spec(s, CG_BLOCK),
                  pl.BlockSpec((3, LANES), lambda j: (0, j))],
        out_specs=_seq_spec(s, 0), out_shape=jax.ShapeDtypeStruct((s, CW), F32),
        compiler_params=_params(("parallel",), 32),
    )(proj, proj, proj, w)


def _mixconv_bwd(dmixed, proj, w):
    s = proj.shape[0]

    def body(d_ref, xin_ref, bg_ref, cg_ref, w_ref, dxin_ref, dbg_ref, dcg_ref, dw_ref):
        wv = w_ref[...]
        xin, cg, dconv = xin_ref[...], cg_ref[...], d_ref[...]
        cx = cg * xin
        cv, s1, s2 = _conv_taps(cx, None, wv)
        dbg_ref[...] = (dconv * cv).astype(BF16)
        dcv = dconv * bg_ref[...]
        dw_ref[...] = jnp.zeros_like(dw_ref)
        dw_ref[0:1, :] = jnp.sum(dcv * s2, axis=0, keepdims=True)
        dw_ref[1:2, :] = jnp.sum(dcv * s1, axis=0, keepdims=True)
        dw_ref[2:3, :] = jnp.sum(dcv * cx, axis=0, keepdims=True)
        dcx = _conv_taps_t(dcv, None, wv)
        dcg_ref[...] = (dcx * xin).astype(BF16)
        dxin_ref[...] = (dcx * cg).astype(BF16)

    o = jax.ShapeDtypeStruct((s, CW), BF16)
    return pl.pallas_call(
        body, name="mixconv_bwd", grid=(CW // LANES,),
        in_specs=[_seq_spec(s, AW // LANES), _seq_spec(s, XIN_BLOCK), _seq_spec(s, BG_BLOCK), _seq_spec(s, CG_BLOCK),
                  pl.BlockSpec((3, LANES), lambda j: (0, j))],
        out_specs=(_seq_spec(s, 0), _seq_spec(s, 0), _seq_spec(s, 0), pl.BlockSpec((SUB, LANES), lambda j: (0, j))),
        out_shape=(o, o, o, jax.ShapeDtypeStruct((SUB, CW), F32)),
        compiler_params=_params(("parallel",), 32),
    )(dmixed, proj, proj, proj, w)


TA = 512
NEG = -1e30


def _causal_mask():
    row = lax.broadcasted_iota(jnp.int32, (TA, TA), 0)
    col = lax.broadcasted_iota(jnp.int32, (TA, TA), 1)
    return col <= row


def _attn_fwd(qp, kp, vp, hosted):
    s = qp.shape[0]
    nq = s // TA

    def body(q_ref, k_ref, v_ref, o_ref, lse_ref):
        i = pl.program_id(1)
        slabs = [slice(SLAB * hh, SLAB * (hh + 1)) for hh in range(2)]
        q = [q_ref[:, sl] for sl in slabs]

        def block(j, carry, masked):
            keys = pl.ds(pl.multiple_of(j * TA, TA), TA)
            ms, acc = carry
            m_out, parts = [], []
            for hh in range(2):
                sc = _dot(q[hh], k_ref[keys, slabs[hh]], NT)
                if masked:
                    sc = jnp.where(_causal_mask(), sc, NEG)
                m_new = jnp.maximum(ms[hh], jnp.max(sc, axis=-1, keepdims=True))
                p = jnp.exp(sc - m_new)
                parts.append(jnp.exp(ms[hh] - m_new) * acc[:, slabs[hh]] + _dot(p.astype(BF16), v_ref[keys, slabs[hh]]))
                m_out.append(m_new)
            return tuple(m_out), jnp.concatenate(parts, axis=1)

        init = ((jnp.full((TA, 1), NEG, F32), jnp.full((TA, 1), NEG, F32)), jnp.zeros((TA, 2 * SLAB), F32))
        carry = lax.fori_loop(0, i, lambda j, cr: block(j, cr, False), init)
        ms, acc = block(i, carry, True)
        for hh in range(2):
            l = acc[:, SLAB * hh + DH:SLAB * hh + DH + 1]
            o_ref[:, DH * hh:DH * (hh + 1)] = acc[:, SLAB * hh:SLAB * hh + DH] / l
            lse_ref[0, :, hh:hh + 1] = ms[hh] + jnp.log(l)

    (o, lse), moved = _call(
        body, (qp, kp, vp), name="attn_fwd", grid=(HEADS // 2, nq),
        in_specs=[pl.BlockSpec((TA, 2 * SLAB), lambda p, i: (i, p)),
                  pl.BlockSpec((s, 2 * SLAB), lambda p, i: (0, p)),
                  pl.BlockSpec((s, 2 * SLAB), lambda p, i: (0, p))],
        out_specs=[pl.BlockSpec((TA, LANES), lambda p, i: (i, p)), pl.BlockSpec((1, TA, 2), lambda p, i: (p, i, 0))],
        out_shape=[jax.ShapeDtypeStruct((s, AW), F32), jax.ShapeDtypeStruct((HEADS // 2, s, 2), F32)],
        vmem_mb=24, hosted=hosted)
    return o, lse, moved


def _attn_bwd(qp, kp, vp, dmixed, o, lse, hosted):
    s = qp.shape[0]
    nq = s // TA

    def body(q_ref, k_ref, v_ref, do_ref, o_ref, lse_ref, dq_ref, dk_ref, dv_ref, qb_ref, dob_ref):
        dk_ref[...] = jnp.zeros_like(dk_ref)
        dv_ref[...] = jnp.zeros_like(dv_ref)
        slabs = [slice(SLAB * hh, SLAB * (hh + 1)) for hh in range(2)]
        lane = lax.broadcasted_iota(jnp.int32, (TA, DH), 1)

        def q_block(i, _):
            i0 = pl.multiple_of(i * TA, TA)
            rows = pl.ds(i0, TA)
            for hh in range(2):
                half = slice(DH * hh, DH * (hh + 1))
                do = do_ref[rows, half]
                delta = jnp.sum(do * o_ref[rows, half], axis=-1, keepdims=True)
                dob_ref[hh, :, 0:DH] = do.astype(BF16)
                dob_ref[hh, :, DH:SLAB] = _lanes3(lane, 0, [-d for d in _split3(delta)], 0.0).astype(BF16)
                lse3 = _split3(lse_ref[0, rows, hh:hh + 1])
                qb_ref[hh, :, 0:DH] = q_ref[rows, SLAB * hh:SLAB * hh + DH]
                aug = q_ref[rows, SLAB * hh + DH:SLAB * (hh + 1)].astype(F32)
                qb_ref[hh, :, DH:SLAB] = _lanes3(lane, AUG_LSE, [-x for x in lse3], aug).astype(BF16)

            def block(j, dq, masked):
                keys = pl.ds(pl.multiple_of(j * TA, TA), TA)
                dv, dk, dqc = [], [], []
                for hh in range(2):
                    q, dob = qb_ref[hh], dob_ref[hh]
                    k = k_ref[keys, slabs[hh]]
                    sc = _dot(q, k, NT)
                    if masked:
                        sc = jnp.where(_causal_mask(), sc, NEG)
                    p = jnp.exp(sc)
                    dv.append(_dot(p.astype(BF16), dob, TN))
                    ds = (p * _dot(dob, v_ref[keys, slabs[hh]], NT)).astype(BF16)
                    dk.append(_dot(ds, q, TN))
                    dqc.append(_dot(ds, k))
                dv_ref[keys, :] += jnp.concatenate(dv, axis=1)
                dk_ref[keys, :] += jnp.concatenate(dk, axis=1)
                return dq + jnp.concatenate(dqc, axis=1)

            dq = lax.fori_loop(0, i, lambda j, acc: block(j, acc, False), jnp.zeros((TA, 2 * SLAB), F32))
            dq_ref[rows, :] = block(i, dq, True)
            return 0

        lax.fori_loop(0, nq, q_block, 0)

    pair = lambda p: (0, p)
    slab2 = pl.BlockSpec((s, 2 * SLAB), pair)
    seq = pl.BlockSpec((s, LANES), pair)
    small = pl.BlockSpec((1, s, 2), lambda p: (p, 0, 0))
    o32 = jax.ShapeDtypeStruct((s, HEADS * SLAB), F32)
    return _call(
        body, (qp, kp, vp, dmixed, o, lse), name="attn_bwd", grid=(HEADS // 2,),
        in_specs=[slab2, slab2, slab2, seq, seq, small], out_specs=[slab2, slab2, slab2], out_shape=[o32, o32, o32],
        scratch_shapes=[pltpu.VMEM((2, TA, SLAB), BF16), pltpu.VMEM((2, TA, SLAB), BF16)], vmem_mb=40, hosted=hosted)


def _qkv_post(dqp, dkp, dvp, proj, gq, gk, hosted):
    s = proj.shape[0]

    def body(dq_ref, dk_ref, dv_ref, q_ref, k_ref, gq_ref, gk_ref, dqo_ref, dko_ref, dvo_ref, df_ref, vec_ref):
        @pl.when(pl.program_id(0) == 0)
        def _():
            vec_ref[...] = jnp.zeros_like(vec_ref)

        def one(d_ref, x_ref, g_ref, o_ref, row, scale):
            dg = jnp.zeros((1, DH), F32)
            for h in range(HEADS):
                sl = slice(DH * h, DH * (h + 1))
                xv = x_ref[:, sl]
                r = lax.rsqrt(jnp.mean(xv * xv, axis=-1, keepdims=True) + EPS)
                xh = xv * r
                dn = d_ref[:, SLAB * h:SLAB * h + DH] * scale
                dg = dg + jnp.sum(dn * xh, axis=0, keepdims=True)
                dxh = dn * g_ref[...]
                o_ref[:, sl] = (r * (dxh - xh * jnp.mean(dxh * xh, axis=-1, keepdims=True))).astype(BF16)
            vec_ref[row:row + 1, 0:DH] += dg

        one(dq_ref, q_ref, gq_ref, dqo_ref, 0, QK_SCALE)
        one(dk_ref, k_ref, gk_ref, dko_ref, 1, 1.0)
        lane = lax.broadcasted_iota(jnp.int32, (TR, LANES), 1)
        df = jnp.zeros((TR, LANES), F32)
        for h in range(HEADS):
            dvo_ref[:, DH * h:DH * (h + 1)] = dv_ref[:, SLAB * h:SLAB * h + DH].astype(BF16)
            row_sum = dq_ref[:, SLAB * h + DH:SLAB * h + DH + 1]
            col_sum = dk_ref[:, SLAB * h + DH + AUG_ONE:SLAB * h + DH + AUG_ONE + 1]
            df = jnp.where(lane == h, row_sum - col_sum, df)
        df_ref[...] = df

    o = jax.ShapeDtypeStruct((s, AW), BF16)
    wide = _row_spec(HEADS * SLAB)
    return _call(
        body, (dqp, dkp, dvp, proj, proj, gq, gk), name="qkv_post", grid=(s // TR,),
        in_specs=[wide, wide, wide, _row_spec(AW, 0), _row_spec(AW, 1), _full_spec((1, DH)), _full_spec((1, DH))],
        out_specs=[_row_spec(AW), _row_spec(AW), _row_spec(AW), _row_spec(LANES), _full_spec((SUB, LANES))],
        out_shape=[o, o, o, jax.ShapeDtypeStruct((s, LANES), F32), jax.ShapeDtypeStruct((SUB, LANES), F32)],
        hosted=hosted)


TF = 256
NJ = DFF // TF
FFN_ROWS_FWD = 1024
FFN_ROWS_BWD = 1024


def _ffn_fwd(h2, wup_t, cw, wd):
    s = h2.shape[0]
    tr = FFN_ROWS_FWD
    nr = s // tr

    def body(h_ref, wu_ref, cg_ref, cv_ref, wd_ref, pg_ref, pv_ref, y_ref, halo_ref, act_ref):
        r, j = pl.program_id(0), pl.program_id(1)
        hv = h_ref[...]
        pg = _dot(hv, wu_ref[0], NT).astype(BF16)
        pv = _dot(hv, wu_ref[1], NT).astype(BF16)
        pg_ref[...] = pg
        pv_ref[...] = pv
        pgf, pvf = pg.astype(F32), pv.astype(F32)
        ug, _, _ = _conv_taps(pgf, jnp.where(r > 0, halo_ref[j, 0], 0.0), cg_ref[...])
        uv, _, _ = _conv_taps(pvf, jnp.where(r > 0, halo_ref[j, 1], 0.0), cv_ref[...])
        halo_ref[j, 0] = pgf[tr - SUB:tr, :]
        halo_ref[j, 1] = pvf[tr - SUB:tr, :]
        act = (ug * _sigmoid(ug) * uv).astype(BF16)
        for t in range(NJ):
            @pl.when(j == t)
            def _(t=t):
                act_ref[:, t * TF:(t + 1) * TF] = act

        @pl.when(j == NJ - 1)
        def _():
            y_ref[...] = _dot(act_ref[...], wd_ref[...])

    pre = jax.ShapeDtypeStruct((s, DFF), BF16)
    return pl.pallas_call(
        body, name="ffn_fwd", grid=(nr, NJ),
        in_specs=[pl.BlockSpec((tr, D), lambda r, j: (r, 0)),
                  pl.BlockSpec((2, TF, D), lambda r, j: (0, j, 0)),
                  pl.BlockSpec((3, TF), lambda r, j: (0, j)),
                  pl.BlockSpec((3, TF), lambda r, j: (0, NJ + j)),
                  pl.BlockSpec((DFF, D), lambda r, j: (0, 0))],
        out_specs=(pl.BlockSpec((tr, TF), lambda r, j: (r, j)),
                   pl.BlockSpec((tr, TF), lambda r, j: (r, j)),
                   pl.BlockSpec((tr, D), lambda r, j: (r, 0))),
        out_shape=(pre, pre, jax.ShapeDtypeStruct((s, D), F32)),
        scratch_shapes=[pltpu.VMEM((NJ, 2, SUB, TF), F32), pltpu.VMEM((tr, DFF), BF16)],
        compiler_params=_params(("arbitrary", "arbitrary"), 56),
    )(h2, wup_t, cw, cw, wd)


def _ffn_bwd(dy, h2, pre_g, pre_v, wup_t, cw, wd):
    s = h2.shape[0]
    tr = FFN_ROWS_BWD
    nr = s // tr
    hb = tr // (2 * SUB)

    def body(dy_ref, h_ref, pg_ref, pv_ref, hg_ref, hv_ref, wu_ref, cg_ref, cv_ref, wd_ref,
             dh_ref, dwu_ref, dwd_ref, dcg_ref, dcv_ref, nxt_ref, awu_ref, awd_ref):
        j, r = pl.program_id(0), pl.program_id(1)
        rr = nr - 1 - r
        row0 = pl.multiple_of(rr * tr, tr)
        cwg, cwv = cg_ref[...], cv_ref[...]
        pg, pv = pg_ref[...].astype(F32), pv_ref[...].astype(F32)
        ug, g1, g2 = _conv_taps(pg, jnp.where(rr > 0, hg_ref[SUB:2 * SUB, :].astype(F32), 0.0), cwg)
        uv, v1, v2 = _conv_taps(pv, jnp.where(rr > 0, hv_ref[SUB:2 * SUB, :].astype(F32), 0.0), cwv)
        sg = _sigmoid(ug)
        sil = ug * sg
        act = (sil * uv).astype(BF16)
        dyv = dy_ref[...]
        da = _dot(dyv, wd_ref[...], NT)
        dug = da * uv * (sg * (1.0 + ug * (1.0 - sg)))
        duv = da * sil
        dpg = _conv_taps_t(dug, jnp.where(r > 0, nxt_ref[0], 0.0), cwg)
        dpv = _conv_taps_t(duv, jnp.where(r > 0, nxt_ref[1], 0.0), cwv)
        nxt_ref[0] = dug[0:SUB, :]
        nxt_ref[1] = duv[0:SUB, :]
        dpgb, dpvb = dpg.astype(BF16), dpv.astype(BF16)
        hv = h_ref[...]
        dwd = _dot(act, dyv, TN)
        dpb = jnp.concatenate([dpgb, dpvb], axis=1)
        dwu = _dot(dpb, hv, TN)
        dh = _dot(dpb, wu_ref[...].reshape(2 * TF, D))

        def taps(du, x0, x1, x2):
            return (jnp.sum(du * x2, axis=0, keepdims=True), jnp.sum(du * x1, axis=0, keepdims=True),
                    jnp.sum(du * x0, axis=0, keepdims=True))

        tg, tv = taps(dug, pg, g1, g2), taps(duv, pv, v1, v2)

        @pl.when(r == 0)
        def _():
            awd_ref[...] = dwd
            awu_ref[...] = dwu
            dcg_ref[...] = jnp.zeros_like(dcg_ref)
            dcv_ref[...] = jnp.zeros_like(dcv_ref)

        @pl.when(r > 0)
        def _():
            awd_ref[...] += dwd
            awu_ref[...] += dwu

        @pl.when(r == nr - 1)
        def _():
            dwd_ref[...] = awd_ref[...].astype(BF16)
            dwu_ref[...] = awu_ref[...].astype(BF16).reshape(2, TF, D)

        for t in range(3):
            dcg_ref[t:t + 1, :] += tg[t]
            dcv_ref[t:t + 1, :] += tv[t]

        @pl.when(j == 0)
        def _():
            dh_ref[pl.ds(row0, tr), :] = dh

        @pl.when(j > 0)
        def _():
            dh_ref[pl.ds(row0, tr), :] += dh

    rows = lambda j, r: (nr - 1 - r, 0)
    tile = lambda j, r: (nr - 1 - r, j)
    halo = lambda j, r: (jnp.maximum((nr - 1 - r) * hb - 1, 0), j)
    return pl.pallas_call(
        body, name="ffn_bwd", grid=(NJ, nr),
        in_specs=[pl.BlockSpec((tr, D), rows), pl.BlockSpec((tr, D), rows),
                  pl.BlockSpec((tr, TF), tile), pl.BlockSpec((tr, TF), tile),
                  pl.BlockSpec((2 * SUB, TF), halo), pl.BlockSpec((2 * SUB, TF), halo),
                  pl.BlockSpec((2, TF, D), lambda j, r: (0, j, 0)),
                  pl.BlockSpec((3, TF), lambda j, r: (0, j)), pl.BlockSpec((3, TF), lambda j, r: (0, NJ + j)),
                  pl.BlockSpec((TF, D), lambda j, r: (j, 0))],
        out_specs=(pl.BlockSpec((s, D), lambda j, r: (0, 0)),
                   pl.BlockSpec((2, TF, D), lambda j, r: (0, j, 0)),
                   pl.BlockSpec((TF, D), lambda j, r: (j, 0)),
                   pl.BlockSpec((SUB, TF), lambda j, r: (0, j)), pl.BlockSpec((SUB, TF), lambda j, r: (0, j))),
        out_shape=(jax.ShapeDtypeStruct((s, D), F32),
                   jax.ShapeDtypeStruct((2, DFF, D), BF16), jax.ShapeDtypeStruct((DFF, D), BF16),
                   jax.ShapeDtypeStruct((SUB, DFF), F32), jax.ShapeDtypeStruct((SUB, DFF), F32)),
        scratch_shapes=[pltpu.VMEM((2, SUB, TF), F32), pltpu.VMEM((2 * TF, D), F32), pltpu.VMEM((TF, D), F32)],
        compiler_params=_params(("arbitrary", "arbitrary"), 56),
    )(dy, h2, pre_g, pre_v, pre_g, pre_v, wup_t, cw, cw, wd)


def _adam(w, g, m, v):
    m = ADAM_B1 * m + (1.0 - ADAM_B1) * g
    v = ADAM_B2 * v + (1.0 - ADAM_B2) * (g * g)
    m_hat = m / (1.0 - ADAM_B1 ** ADAM_STEP)
    v_hat = v / (1.0 - ADAM_B2 ** ADAM_STEP)
    delta = -ADAM_LR * (m_hat / (jnp.sqrt(v_hat) + ADAM_EPS) + ADAM_WD * w)
    return delta, m, v


NCHIP = NDEV // 2


def _pair_add(mine, theirs, tr, name):
    _, _, rws, cols = mine.shape

    def body(a_ref, b_ref, o_ref):
        c = lax.axis_index("c")
        o_ref[0] = (a_ref[0, c].astype(F32) + b_ref[0].astype(F32)).astype(BF16)

    (out,), _ = _call(
        body, (mine, theirs), name=name, grid=(NCHIP, rws // tr),
        in_specs=[pl.BlockSpec((1, 2, tr, cols), lambda q, i: (q, 0, i, 0)),
                  pl.BlockSpec((1, tr, cols), lambda q, i: (q, i, 0))],
        out_specs=[pl.BlockSpec((1, tr, cols), lambda q, i: (q, i, 0))],
        out_shape=[jax.ShapeDtypeStruct((NCHIP, rws, cols), BF16)], vmem_mb=60)
    return out


def _adamw_sharded(parts, w, m, v, tr, name, hosted=None):
    rws, cols = w.shape
    n_parts = parts.shape[0]

    def body(p_ref, w_ref, m_ref, v_ref, g_ref, d_ref, mo_ref, vo_ref):
        g = p_ref[0].astype(F32)
        for q in range(1, n_parts):
            g = g + p_ref[q].astype(F32)
        g_ref[...] = g
        d_ref[...], mo_ref[...], vo_ref[...] = _adam(w_ref[...], g, m_ref[...], v_ref[...])

    blk = pl.BlockSpec((tr, cols), lambda i: (i, 0))
    o = jax.ShapeDtypeStruct((rws, cols), F32)
    outs, moved = _call(
        body, (parts, w, m, v), name=name, grid=(rws // tr,),
        in_specs=[pl.BlockSpec((n_parts, tr, cols), lambda i: (0, i, 0)), blk, blk, blk],
        out_specs=[blk, blk, blk, blk], out_shape=[o, o, o, o], vmem_mb=60, hosted=hosted)
    return (outs, moved) if hosted else outs


def _adamw_ada(c_all, dmod_my, w, m, v):
    rws, cols = w.shape
    tr = 256

    def body(c_ref, dm_ref, w_ref, m_ref, v_ref, g_ref, d_ref, mo_ref, vo_ref):
        cv = c_ref[...]
        act = cv * _sigmoid(cv)
        g = _dot(act, dm_ref[...], TN, lax.Precision.HIGHEST)
        g_ref[...] = g
        d_ref[...], mo_ref[...], vo_ref[...] = _adam(w_ref[...], g, m_ref[...], v_ref[...])

    blk = pl.BlockSpec((tr, cols), lambda i: (i, 0))
    o = jax.ShapeDtypeStruct((rws, cols), F32)
    return pl.pallas_call(
        body, name="adamw_ada", grid=(rws // tr,),
        in_specs=[pl.BlockSpec((NDEV, tr), lambda i: (0, i)), _full_spec((NDEV, cols)), blk, blk, blk],
        out_specs=(blk, blk, blk, blk), out_shape=(o, o, o, o),
        compiler_params=_params(("parallel",), 60),
    )(c_all, dmod_my, w, m, v)


REP_ROWS = 16
ROW_N1, ROW_N2, ROW_LOSS, ROW_MISC = 6, 7, 8, 9
LANE_BF, LANE_GQ, LANE_GK = 0, 128, 256


def _adamw_small(rep_all, conv_all, wmv):
    n_ff = wmv[6][0].shape[1]

    def body(*refs):
        rep_ref, conv_ref = refs[:2]
        ins = refs[2:2 + 24]
        outs = refs[2 + 24:]
        loss_ref, outs = outs[0], outs[1:]
        g_rep = rep_ref[0]
        g_conv = conv_ref[0]
        for d in range(1, NDEV):
            g_rep = g_rep + rep_ref[d]
            g_conv = g_conv + conv_ref[d]
        loss_ref[...] = (0.5 / D) * jnp.sum(g_rep[ROW_LOSS:ROW_LOSS + 1, :], axis=-1, keepdims=True)
        grads = [
            None,
            g_rep[ROW_N1:ROW_N1 + 1, :],
            g_rep[ROW_MISC:ROW_MISC + 1, LANE_BF:LANE_BF + HEADS],
            g_rep[ROW_MISC:ROW_MISC + 1, LANE_GQ:LANE_GQ + DH],
            g_rep[ROW_MISC:ROW_MISC + 1, LANE_GK:LANE_GK + DH],
            g_rep[ROW_N2:ROW_N2 + 1, :],
            g_conv[0:3, 0:n_ff],
            g_conv[0:3, n_ff:n_ff + DH],
        ]
        for p in range(8):
            w_ref, m_ref, v_ref = ins[3 * p:3 * p + 3]
            g_ref, d_ref, mo_ref, vo_ref = outs[4 * p:4 * p + 4]
            if p == 0:
                for nmod in range(NMOD):
                    sl = slice(D * nmod, D * (nmod + 1))
                    g = g_rep[nmod:nmod + 1, :]
                    g_ref[:, sl] = g
                    d_ref[:, sl], mo_ref[:, sl], vo_ref[:, sl] = _adam(w_ref[:, sl], g, m_ref[:, sl], v_ref[:, sl])
            else:
                g = grads[p]
                g_ref[...] = g
                d_ref[...], mo_ref[...], vo_ref[...] = _adam(w_ref[...], g, m_ref[...], v_ref[...])

    flat = [a for trio in wmv for a in trio]
    out_shape = [jax.ShapeDtypeStruct((1, 1), F32)]
    for trio in wmv:
        out_shape += [jax.ShapeDtypeStruct(trio[0].shape, F32)] * 4
    return pl.pallas_call(
        body, name="adamw_small", out_shape=tuple(out_shape),
        compiler_params=_params(None, 32),
    )(rep_all, conv_all, *flat)


FG_FIRST = 3 * AW
N_IN = DIN // NDEV


def _w_in_runs():
    runs = []
    for d in range(NDEV):
        lo, hi = N_IN * d, N_IN * (d + 1)
        for a, b, shift in ((0, FG_FIRST, 0), (FG_FIRST, FG_FIRST + HEADS, DIN - HEADS - FG_FIRST),
                            (FG_FIRST + HEADS, DIN, -HEADS)):
            a, b = max(a, lo), min(b, hi)
            if a < b:
                runs.append((d, a - lo, a + shift, b - a))
    return runs


W_IN_ROWS = 256
N_IN_PAD = 512


def _identity(n):
    return (lax.broadcasted_iota(jnp.int32, (n, n), 0) == lax.broadcasted_iota(jnp.int32, (n, n), 1)).astype(BF16)


def _assemble_w_in(g_in, hosted):
    def body(g_ref, o_ref, t_ref):
        eye = _identity(W_IN_ROWS)
        shard = None
        for d, src, dst, width in _w_in_runs():
            if d != shard:
                t_ref[:, 0:N_IN] = _dot(eye, g_ref[d], NT).astype(BF16)
                shard = d
            o_ref[:, dst:dst + width] = t_ref[:, src:src + width]
        o_ref[:, DIN:DINP] = jnp.zeros((W_IN_ROWS, DINP - DIN), o_ref.dtype)

    (out,), moved = _call(
        body, (g_in,), name="assemble_w_in", grid=(D // W_IN_ROWS,),
        in_specs=[pl.BlockSpec((NDEV, N_IN, W_IN_ROWS), lambda i: (0, 0, i))],
        out_specs=[pl.BlockSpec((W_IN_ROWS, DINP), lambda i: (i, 0))],
        out_shape=[jax.ShapeDtypeStruct((D, DINP), g_in.dtype)],
        scratch_shapes=[pltpu.VMEM((W_IN_ROWS, N_IN_PAD), BF16)], vmem_mb=16, hosted=hosted)
    return out, moved


def _scatter_dw_in(dwp):
    def body(w_ref, o_ref, t_ref):
        eye = _identity(W_IN_ROWS)
        runs = _w_in_runs()
        for i, (d, src, dst, width) in enumerate(runs):
            t_ref[:, src:src + width] = w_ref[:, dst:dst + width]
            if i + 1 == len(runs) or runs[i + 1][0] != d:
                o_ref[d // 2, d % 2] = _dot(t_ref[:, 0:N_IN], eye, TN).astype(BF16)

    (out,), _ = _call(
        body, (dwp,), name="scatter_dw_in", grid=(D // W_IN_ROWS,),
        in_specs=[pl.BlockSpec((W_IN_ROWS, DINP), lambda i: (i, 0))],
        out_specs=[pl.BlockSpec((NCHIP, 2, N_IN, W_IN_ROWS), lambda i: (0, 0, 0, i))],
        out_shape=[jax.ShapeDtypeStruct((NCHIP, 2, N_IN, D), dwp.dtype)],
        scratch_shapes=[pltpu.VMEM((W_IN_ROWS, N_IN_PAD), BF16)], vmem_mb=16)
    return out


def kernel(x, c, w_ada, b_ada, norm1_g, w_in, b_forget, q_norm_g, k_norm_g, conv_mix_w, w_out, norm2_g, w_up, ffn_conv_w, w_down, loss_target, m_w_ada, m_b_ada, m_norm1_g, m_w_in, m_b_forget, m_q_norm_g, m_k_norm_g, m_conv_mix_w, m_w_out, m_norm2_g, m_w_up, m_ffn_conv_w, m_w_down, v_w_ada, v_b_ada, v_norm1_g, v_w_in, v_b_forget, v_q_norm_g, v_k_norm_g, v_conv_mix_w, v_w_out, v_norm2_g, v_w_up, v_ffn_conv_w, v_w_down):
    me = 4 * lax.axis_index("x") + 2 * lax.axis_index("y") + lax.axis_index("c")
    xs, tgt = x[0], loss_target[0]
    n_ada = w_ada.shape[2]
    n_ff = w_up.shape[2]

    conv_w = jnp.concatenate([ffn_conv_w[0], conv_mix_w[0]], axis=1)
    conv_w = jnp.concatenate([conv_w, jnp.zeros((SUB - 3, conv_w.shape[1]), F32)], axis=0)
    c_all, conv_all, g_in = _exchange(
        [(c.reshape(SUB, D // SUB), "ag"), (conv_w, "ag"), (jnp.transpose(w_in[0]).astype(BF16), "ag2")],
        "exchange_w_in")
    g_in, w_out_b, w_up_b, w_down_b = lax.optimization_barrier(
        (g_in, w_out[0].astype(BF16), jnp.transpose(w_up[0]).astype(BF16), w_down[0].astype(BF16)))
    g_out, g_up, g_down = _sequencer_exchange(
        [(w_out_b, "ag2"), (w_up_b, "ag2"), (w_down_b, "ag2")], "gather_weights", collective_id=1)
    c_all = c_all.reshape(NDEV, D)
    cw_ffn = jnp.transpose(conv_all[:, :3, :n_ff], (1, 0, 2)).reshape(3, 2 * DFF)
    cw_mix = jnp.transpose(conv_all[:, :3, n_ff:], (1, 0, 2)).reshape(3, CW)

    b_my = lax.dynamic_slice(b_ada, (0, me * n_ada), (1, n_ada))
    mod_part = _ada_fwd(c_all, w_ada[0], b_my)
    w_in_p, (mod_rows,) = _assemble_w_in(
        g_in, [(jnp.broadcast_to(mod_part[:, None, :], (NDEV, SUB, n_ada)), "a2a")])
    mod = mod_rows[:, 0, :].reshape(NMOD, D)
    mod = jnp.concatenate([mod, jnp.zeros((SUB - NMOD, D), F32)], axis=0)

    h = _norm_mod_fwd(xs, mod, norm1_g)
    proj = _mm(h, w_in_p, "nn", F32, 1024, 640, "proj_fwd")
    bf_pad = jnp.concatenate([b_forget, jnp.zeros((1, LANES - HEADS), F32)], axis=1)
    fcum = _fgate_fwd(proj, bf_pad)
    (qp, kp, vp), _ = _qkv_prep(proj, fcum, q_norm_g, k_norm_g, None)
    attn, lse, _ = _attn_fwd(qp, kp, vp, None)
    w_out_f = g_out.reshape(D, D)
    w_up_t = g_up.reshape(2, DFF, D)
    w_down_f = g_down.reshape(DFF, D)
    conv = _mixconv_fwd(proj, cw_mix)
    mixed = jnp.concatenate([attn, conv], axis=1).astype(BF16)
    z = _mm(mixed, w_out_f, "nn", F32, 1024, 1024, "out_fwd")
    x1, h2 = _resid_norm2(xs, z, mod, norm2_g)
    pre_g, pre_v, y = _ffn_fwd(h2, w_up_t, cw_ffn, w_down_f)
    dout, dy, vec_l = _loss_head(x1, y, tgt, mod)

    dh2, dwup_t, dwd, dcw_g, dcw_v = _ffn_bwd(dy, h2, pre_g, pre_v, w_up_t, cw_ffn, w_down_f)
    s_down = dwd.reshape(NCHIP, 2, DFF // NDEV, D)
    s_up = dwup_t.reshape(NCHIP, 2, n_ff, D)
    dx1, dz, vec_2, (t_up, t_down) = _norm_mod_bwd(dh2, x1, dout, z, mod, norm2_g, 4, 2, "norm2_bwd",
                                                   hosted=[(s_up, "pair"), (s_down, "pair")])
    dwout = _mm(mixed, dz, "tn", BF16, 1024, 1024, "out_bwd_w")
    s_out = dwout.reshape(NCHIP, 2, D // NDEV, D)
    dmixed, (t_out,) = _mm(dz, w_out_f, "nt", F32, 1024, 1024, "out_bwd_x", hosted=[(s_out, "pair")])
    c_out = _pair_add(s_out, t_out, 128, "pair_add_out")
    c_up = _pair_add(s_up, t_up, 176, "pair_add_up")
    c_down = _pair_add(s_down, t_down, 176, "pair_add_down")
    dxin, dbg, dcg, dcw_mix = _mixconv_bwd(dmixed, proj, cw_mix)
    (dqp, dkp, dvp), (p_up, p_down, p_out) = _attn_bwd(
        qp, kp, vp, dmixed, attn, lse, [(c_up, "chips"), (c_down, "chips"), (c_out, "chips")])
    (dq, dk, dvb, dfcol, vec_qk), _ = _qkv_post(dqp, dkp, dvp, proj, q_norm_g, k_norm_g, None)
    dfg, vec_bf = _fgate_bwd(dfcol, proj, bf_pad)
    dproj = jnp.concatenate([dq, dk, dvb, dxin, dbg, dcg, dfg], axis=1)
    dwin_p = _mm(h, dproj, "tn", BF16, 1024, 640, "proj_bwd_w")
    s_in = _scatter_dw_in(dwin_p).reshape(NDEV, N_IN, D)
    (p_in,) = _sequencer_exchange([(s_in, "a2a")], "scatter_dw_in_partials", collective_id=2, all_peers=True)
    dh = _mm(dproj, w_in_p, "nt", F32, 1024, 512, "proj_bwd_x", vmem_mb=36)
    grad_x, vec_1 = _norm_mod_bwd(dh, xs, dx1, None, mod, norm1_g, 1, None, "norm1_bwd")

    misc = jnp.zeros((1, D), F32)
    misc = lax.dynamic_update_slice(misc, vec_bf[0:1, :HEADS], (0, LANE_BF))
    misc = lax.dynamic_update_slice(misc, vec_qk[0:1, :DH], (0, LANE_GQ))
    misc = lax.dynamic_update_slice(misc, vec_qk[1:2, :DH], (0, LANE_GK))
    rep = jnp.concatenate([
        vec_1[0:1], vec_1[1:2], vec_2[3:4], vec_2[0:1], vec_2[1:2], vec_l[0:1],
        vec_1[2:3], vec_2[2:3], vec_l[1:2], misc, jnp.zeros((REP_ROWS - 10, D), F32)], axis=0)
    dcw_ffn = jnp.concatenate([dcw_g, dcw_v], axis=1).reshape(SUB, NDEV, n_ff)
    dcw_all = jnp.concatenate([jnp.transpose(dcw_ffn, (1, 0, 2)),
                               jnp.transpose(dcw_mix.reshape(SUB, NDEV, DH), (1, 0, 2))], axis=2)
    r_up = _adamw_sharded(p_up, jnp.transpose(w_up[0]), jnp.transpose(m_w_up[0]), jnp.transpose(v_w_up[0]), 176,
                          "adamw_up")
    r_down = _adamw_sharded(p_down, w_down[0], m_w_down[0], v_w_down[0], 176, "adamw_down")
    rep, dcw_all, r_up, r_down = lax.optimization_barrier((rep, dcw_all, r_up, r_down))
    r_up = tuple(jnp.transpose(a) for a in r_up)
    r_out, (rep_all, conv_parts) = _adamw_sharded(p_out, w_out[0], m_w_out[0], v_w_out[0], 128, "adamw_out",
                                                  hosted=[(rep, "ag"), (dcw_all, "a2a")])
    dmod_my = lax.dynamic_slice(rep_all[:, :NMOD, :].reshape(NDEV, NMOD * D), (0, me * n_ada), (NDEV, n_ada))
    r_ada = _adamw_ada(c_all, dmod_my, w_ada[0], m_w_ada[0], v_w_ada[0])
    r_in = _adamw_sharded(p_in, jnp.transpose(w_in[0]), jnp.transpose(m_w_in[0]), jnp.transpose(v_w_in[0]), N_IN,
                          "adamw_in")
    r_in = tuple(jnp.transpose(a) for a in r_in)
    small = _adamw_small(rep_all, conv_parts, [
        [b_ada, m_b_ada, v_b_ada], [norm1_g, m_norm1_g, v_norm1_g], [b_forget, m_b_forget, v_b_forget],
        [q_norm_g, m_q_norm_g, v_q_norm_g], [k_norm_g, m_k_norm_g, v_k_norm_g], [norm2_g, m_norm2_g, v_norm2_g],
        [ffn_conv_w[0], m_ffn_conv_w[0], v_ffn_conv_w[0]], [conv_mix_w[0], m_conv_mix_w[0], v_conv_mix_w[0]]])
    loss = small[0].reshape(())
    r_bada, r_n1, r_bf, r_gq, r_gk, r_n2, r_cf, r_cm = [small[1 + 4 * p:5 + 4 * p] for p in range(8)]
    lead = lambda t: tuple(a[None] for a in t)
    per_w = [lead(r_ada), r_bada, r_n1, lead(r_in), r_bf, r_gq, r_gk, lead(r_cm), lead(r_out), r_n2,
             lead(r_up), lead(r_cf), lead(r_down)]
    outs = [loss, grad_x[None]]
    for field in range(4):
        outs += [t[field] for t in per_w]
    return tuple(outs)
```

```python
import functools

import jax
import jax.numpy as jnp
import numpy as np
from jax import lax
from jax.experimental import pallas as pl
from jax.experimental.pallas import tpu as pltpu
from jax.experimental.pallas import tpu_sc as plsc

F32 = jnp.float32
BF16 = jnp.bfloat16

NDEV = 8
D = 1024
HEADS = 8
DH = 64
AW = 512
CW = 512
DFF = 2816
DIN = 3080
DINP = 3200
NMOD = 6
EPS = 1e-6
QK_SCALE = 0.125
LANES = 128
SUB = 8

ADAM_LR = 0.001
ADAM_B1 = 0.9
ADAM_B2 = 0.999
ADAM_EPS = 1e-08
ADAM_WD = 0.01
ADAM_STEP = 10

MESH = pl.DeviceIdType.MESH
ANY = pl.BlockSpec(memory_space=pl.ANY)

NN = (((1,), (0,)), ((), ()))
NT = (((1,), (1,)), ((), ()))
TN = (((0,), (0,)), ((), ()))


def _dot(a, b, dims=NN, precision=None):
    return lax.dot_general(a, b, dims, precision=precision, preferred_element_type=F32)


def _params(sem=None, vmem_mb=None):
    kw = {}
    if sem is not None:
        kw["dimension_semantics"] = sem
    if vmem_mb is not None:
        kw["vmem_limit_bytes"] = vmem_mb * 1024 * 1024
    return pltpu.CompilerParams(**kw)


def _sigmoid(x):
    return 0.5 * jnp.tanh(0.5 * x) + 0.5


class _Exchange:
    def __init__(self, items):
        self.arrays = [pltpu.with_memory_space_constraint(a, pltpu.HBM) for a, _ in items]
        self.modes = [m for _, m in items]
        self.n = len(items)
        self.out_shape = []
        for a, m in items:
            sh = {"ag": (NDEV,) + a.shape, "ag2": (NDEV,) + a.shape, "pair": a.shape[:1] + a.shape[2:]}.get(m, a.shape)
            self.out_shape.append(jax.ShapeDtypeStruct(sh, a.dtype))
        self.scratch = [pltpu.SemaphoreType.DMA((self.n, NDEV - 1)), pltpu.SemaphoreType.DMA((self.n, NDEV - 1)),
                        pltpu.SemaphoreType.DMA((self.n,))]

    def _plan(self, srcs, outs, sems):
        send_sems, recv_sems, loc_sems = sems
        x, y, c = lax.axis_index("x"), lax.axis_index("y"), lax.axis_index("c")
        me, my_chip = 4 * x + 2 * y + c, 2 * x + y
        sib = (x, y, 1 - c)
        local, first, landed, forwards, arrivals = [], [], [], [], []

        def remote(a, k, src, dst, to):
            return pltpu.make_async_remote_copy(src_ref=src, dst_ref=dst, send_sem=send_sems.at[a, k],
                                                recv_sem=recv_sems.at[a, k], device_id=to, device_id_type=MESH)

        for a, mode in enumerate(self.modes):
            src, out = srcs[a], outs[a]
            if mode in ("ag", "a2a"):
                piece = (lambda slot, src=src: src) if mode == "ag" else (lambda slot, src=src: src.at[slot])
                local.append(pltpu.make_async_copy(piece(me), out.at[me], loc_sems.at[a]))
                for r in range(1, NDEV):
                    px = 1 - x if (r >> 2) & 1 else x
                    py = 1 - y if (r >> 1) & 1 else y
                    pc = 1 - c if r & 1 else c
                    pidx = 4 * px + 2 * py + pc
                    first.append(remote(a, r - 1, piece(pidx), out.at[me], (px, py, pc)))
                    arrivals.append(remote(a, r - 1, piece(pidx), out.at[pidx], (px, py, pc)))
            elif mode == "ag2":
                local.append(pltpu.make_async_copy(src, out.at[me], loc_sems.at[a]))
                first.append(remote(a, 0, src, out.at[me], sib))
                arrivals.append(remote(a, 0, src, out.at[me + 1 - 2 * c], sib))
                for j, (px, py) in enumerate([(1 - x, y), (x, 1 - y), (1 - x, 1 - y)]):
                    theirs = out.at[4 * px + 2 * py + c]
                    first.append(remote(a, 1 + j, src, out.at[me], (px, py, c)))
                    landed.append(remote(a, 1 + j, src, theirs, (px, py, c)))
                    forwards.append(remote(a, 4 + j, theirs, theirs, sib))
                    arrivals.append(remote(a, 4 + j, src, out.at[4 * px + 2 * py + 1 - c], sib))
            elif mode == "pair":
                for q in range(NDEV // 2):
                    first.append(remote(a, q, src.at[q, 1 - c], out.at[q], sib))
                    arrivals.append(remote(a, q, src.at[q, 1 - c], out.at[q], sib))
            else:
                assert mode == "chips", mode
                local.append(pltpu.make_async_copy(src.at[my_chip], out.at[my_chip], loc_sems.at[a]))
                for j, (px, py) in enumerate([(1 - x, y), (x, 1 - y), (1 - x, 1 - y)]):
                    q = 2 * px + py
                    first.append(remote(a, 1 + j, src.at[q], out.at[my_chip], (px, py, c)))
                    arrivals.append(remote(a, 1 + j, src.at[q], out.at[q], (px, py, c)))
        return local, first, landed, forwards, arrivals

    def start(self, srcs, outs, sems):
        local, first, _, _, _ = self._plan(srcs, outs, sems)
        for cp in local + first:
            cp.start()

    def wait(self, srcs, outs, sems):
        local, first, landed, forwards, arrivals = self._plan(srcs, outs, sems)
        for cp, fwd in zip(landed, forwards):
            cp.wait_recv()
            fwd.start()
        for cp in arrivals:
            cp.wait_recv()
        for cp in first + forwards:
            cp.wait_send()
        for cp in local:
            cp.wait()


def _exchange(items, name):
    ex = _Exchange(items)
    n = ex.n

    def body(*refs):
        srcs, outs, sems = refs[:n], refs[n:2 * n], refs[2 * n:]
        ex.start(srcs, outs, sems)
        ex.wait(srcs, outs, sems)

    return pl.pallas_call(
        body, name=name,
        out_shape=tuple(ex.out_shape),
        in_specs=[ANY] * n, out_specs=tuple([ANY] * n),
        scratch_shapes=ex.scratch,
        compiler_params=pltpu.CompilerParams(has_side_effects=True),
    )(*ex.arrays)


def _sequencer_exchange(items, name, collective_id, all_peers=False):
    ex = _Exchange(items)
    srcs = [jax.new_ref(a, memory_space=pltpu.MemorySpace.HBM) for a in ex.arrays]
    outs = [jax.empty_ref(sh, memory_space=pltpu.MemorySpace.HBM) for sh in ex.out_shape]

    @pl.kernel(mesh=plsc.ScalarSubcoreMesh(axis_name="sequencer", num_cores=1), name=name,
               scratch_types=tuple(ex.scratch), compiler_params=pltpu.CompilerParams(collective_id=collective_id))
    def launch(send_sems, recv_sems, loc_sems):
        x, y, c = lax.axis_index("x"), lax.axis_index("y"), lax.axis_index("c")
        barrier = pltpu.get_barrier_semaphore()
        peers = [(x, y, 1 - c), (1 - x, y, c), (x, 1 - y, c), (1 - x, 1 - y, c)]
        if all_peers:
            peers += [(1 - x, y, 1 - c), (x, 1 - y, 1 - c), (1 - x, 1 - y, 1 - c)]
        for peer in peers:
            pl.semaphore_signal(barrier, inc=1, device_id=peer, device_id_type=MESH)
        pl.semaphore_wait(barrier, len(peers))
        sems = (send_sems, recv_sems, loc_sems)
        ex.start(srcs, outs, sems)
        ex.wait(srcs, outs, sems)

    launch()
    return [o[...] for o in outs]


def _call(body, inputs, *, name, grid, in_specs, out_specs, out_shape, scratch_shapes=(), vmem_mb=None, hosted=None):
    out_specs, out_shape, scratch_shapes = tuple(out_specs), tuple(out_shape), list(scratch_shapes)
    if not hosted:
        res = pl.pallas_call(
            body, name=name, grid=grid, in_specs=list(in_specs), out_specs=out_specs, out_shape=out_shape,
            scratch_shapes=scratch_shapes, compiler_params=_params(("arbitrary",) * len(grid), vmem_mb),
        )(*inputs)
        return tuple(res), ()
    ex = _Exchange(hosted)
    n, n_in, n_out, n_scr = ex.n, len(inputs), len(out_shape), len(scratch_shapes)

    def hosting_body(*refs):
        ins, srcs = refs[:n_in], refs[n_in:n_in + n]
        outs, landing = refs[n_in + n:n_in + n + n_out], refs[n_in + n + n_out:n_in + 2 * n + n_out]
        scratch, sems = refs[n_in + 2 * n + n_out:n_in + 2 * n + n_out + n_scr], refs[n_in + 2 * n + n_out + n_scr:]
        first = functools.reduce(jnp.logical_and, [pl.program_id(d) == 0 for d in range(len(grid))])
        last = functools.reduce(jnp.logical_and, [pl.program_id(d) == grid[d] - 1 for d in range(len(grid))])

        @pl.when(first)
        def _():
            ex.start(srcs, landing, sems)

        body(*ins, *outs, *scratch)

        @pl.when(last)
        def _():
            ex.wait(srcs, landing, sems)

    res = pl.pallas_call(
        hosting_body, name=name, grid=grid,
        in_specs=list(in_specs) + [ANY] * n, out_specs=out_specs + tuple([ANY] * n),
        out_shape=out_shape + tuple(ex.out_shape), scratch_shapes=scratch_shapes + ex.scratch,
        compiler_params=_params(("arbitrary",) * len(grid), vmem_mb),
    )(*inputs, *ex.arrays)
    return tuple(res[:n_out]), tuple(res[n_out:])


def _mm(a, b, mode, out_dtype, tm, tn, name, hosted=None, vmem_mb=24):
    if mode == "nn":
        (m, k), n = a.shape, b.shape[1]
        a_spec = pl.BlockSpec((tm, k), lambda i, j: (i, 0))
        b_spec = pl.BlockSpec((k, tn), lambda i, j: (0, j))
        dims = NN
    elif mode == "nt":
        (m, k), n = a.shape, b.shape[0]
        a_spec = pl.BlockSpec((tm, k), lambda i, j: (i, 0))
        b_spec = pl.BlockSpec((tn, k), lambda i, j: (j, 0))
        dims = NT
    else:
        (k, m), n = a.shape, b.shape[1]
        a_spec = pl.BlockSpec((k, tm), lambda i, j: (0, i))
        b_spec = pl.BlockSpec((k, tn), lambda i, j: (0, j))
        dims = TN
    assert m % tm == 0 and n % tn == 0, (m, n, tm, tn)

    def body(a_ref, b_ref, o_ref):
        o_ref[...] = _dot(a_ref[...], b_ref[...], dims).astype(o_ref.dtype)

    (out,), moved = _call(
        body, (a, b), name=name, grid=(m // tm, n // tn),
        in_specs=[a_spec, b_spec], out_specs=[pl.BlockSpec((tm, tn), lambda i, j: (i, j))],
        out_shape=[jax.ShapeDtypeStruct((m, n), out_dtype)], vmem_mb=vmem_mb, hosted=hosted)
    return (out, moved) if hosted else out


def _shift_down(x, k, fill):
    y = pltpu.roll(x, k, 0)
    row = lax.broadcasted_iota(jnp.int32, (SUB, x.shape[1]), 0)
    head = y[0:SUB, :]
    for t in range(k):
        head = jnp.where(row == t, fill[t], head)
    return jnp.concatenate([head, y[SUB:, :]], axis=0)


def _shift_up(x, k, fill):
    n = x.shape[0]
    y = pltpu.roll(x, n - k, 0)
    row = lax.broadcasted_iota(jnp.int32, (SUB, x.shape[1]), 0)
    tail = y[n - SUB:, :]
    for t in range(k):
        tail = jnp.where(row == SUB - k + t, fill[t], tail)
    return jnp.concatenate([y[:n - SUB, :], tail], axis=0)


def _conv_taps(x, halo, w):
    if halo is None:
        f1, f2 = [0.0], [0.0, 0.0]
    else:
        f1, f2 = [halo[7:8, :]], [halo[6:7, :], halo[7:8, :]]
    s1 = _shift_down(x, 1, f1)
    s2 = _shift_down(x, 2, f2)
    u = w[2:3, :] * x + w[1:2, :] * s1 + w[0:1, :] * s2
    return u, s1, s2


def _conv_taps_t(du, nxt, w):
    if nxt is None:
        f1, f2 = [0.0], [0.0, 0.0]
    else:
        f1, f2 = [nxt[0:1, :]], [nxt[0:1, :], nxt[1:2, :]]
    return w[2:3, :] * du + w[1:2, :] * _shift_up(du, 1, f1) + w[0:1, :] * _shift_up(du, 2, f2)


def _ada_fwd(c_all, w_ada, b_my):
    def body(c_ref, w_ref, b_ref, o_ref):
        cv = c_ref[...]
        act = cv * _sigmoid(cv)
        o_ref[...] = _dot(act, w_ref[...], NN, lax.Precision.HIGHEST) + b_ref[...]

    return pl.pallas_call(
        body, name="ada_fwd",
        out_shape=jax.ShapeDtypeStruct((NDEV, w_ada.shape[1]), F32),
        compiler_params=_params(None, 32),
    )(c_all, w_ada, b_my)


TR = 256
TRE = 512


def _row_spec(width, col=0, rows=TR):
    return pl.BlockSpec((rows, width), lambda i, col=col: (i, col))


def _erow(width):
    return _row_spec(width, rows=TRE)


def _full_spec(shape):
    return pl.BlockSpec(shape, lambda i: (0,) * len(shape))


def _norm_mod_fwd(x, mod, g):
    s = x.shape[0]

    def body(x_ref, mod_ref, g_ref, h_ref):
        xv = x_ref[...]
        r = lax.rsqrt(jnp.mean(xv * xv, axis=-1, keepdims=True) + EPS)
        nrm = xv * r * g_ref[...]
        h_ref[...] = (nrm * (1.0 + mod_ref[1:2, :]) + mod_ref[0:1, :]).astype(BF16)

    return pl.pallas_call(
        body, name="norm1_fwd", grid=(s // TRE,),
        in_specs=[_erow(D), _full_spec((SUB, D)), _full_spec((1, D))],
        out_specs=_erow(D), out_shape=jax.ShapeDtypeStruct((s, D), BF16),
        compiler_params=_params(("parallel",), 16),
    )(x, mod, g)


SLAB = 2 * DH
AUG_F, AUG_ONE, AUG_LSE = 0, 3, 6


def _split3(x):
    hi = x.astype(BF16).astype(F32)
    r1 = x - hi
    mid = r1.astype(BF16).astype(F32)
    return hi, mid, r1 - mid


def _lanes3(lane, first, pieces, other):
    out = other
    for k in range(3):
        out = jnp.where(lane == first + k, pieces[k], out)
    return out


def _aug_placement():
    eq = np.zeros((3 * LANES, HEADS * SLAB), np.float32)
    ek = np.zeros((3 * LANES, HEADS * SLAB), np.float32)
    ones = np.zeros((SUB, HEADS * SLAB), np.float32)
    for h in range(HEADS):
        aug = SLAB * h + DH
        for k in range(3):
            eq[LANES * k + h, aug + AUG_F + k] = 1.0
            ek[LANES * k + h, aug + AUG_ONE + k] = -1.0
            ones[0, aug + AUG_ONE + k] = 1.0
            ones[1, aug + AUG_F + k] = ones[1, aug + AUG_LSE + k] = 1.0
            ones[2, aug + k] = 1.0
    return jnp.asarray(eq, BF16), jnp.asarray(ek, BF16), jnp.asarray(ones)


def _qkv_prep(proj, fcum, gq, gk):
    s = proj.shape[0]

    def body(q_ref, k_ref, v_ref, f_ref, gq_ref, gk_ref, eq_ref, ek_ref, ones_ref, qo_ref, ko_ref, vo_ref):
        f3 = jnp.concatenate(_split3(f_ref[...]), axis=1).astype(BF16)
        qo_ref[...] = (_dot(f3, eq_ref[...]) + ones_ref[0:1, :]).astype(BF16)
        ko_ref[...] = (_dot(f3, ek_ref[...]) + ones_ref[1:2, :]).astype(BF16)
        vo_ref[...] = jnp.broadcast_to(ones_ref[2:3, :], vo_ref.shape).astype(BF16)
        for h in range(HEADS):
            sl = slice(DH * h, DH * (h + 1))
            lo = slice(SLAB * h, SLAB * h + DH)
            qh = q_ref[:, sl]
            r = lax.rsqrt(jnp.mean(qh * qh, axis=-1, keepdims=True) + EPS)
            qo_ref[:, lo] = (qh * r * gq_ref[...] * QK_SCALE).astype(BF16)
            kh = k_ref[:, sl]
            r = lax.rsqrt(jnp.mean(kh * kh, axis=-1, keepdims=True) + EPS)
            ko_ref[:, lo] = (kh * r * gk_ref[...]).astype(BF16)
            vo_ref[:, lo] = v_ref[:, sl].astype(BF16)

    eq, ek, ones = _aug_placement()
    o = jax.ShapeDtypeStruct((s, HEADS * SLAB), BF16)
    wide = _row_spec(HEADS * SLAB)
    outs, _ = _call(
        body, (proj, proj, proj, fcum, gq, gk, eq, ek, ones), name="qkv_prep", grid=(s // TR,),
        in_specs=[_row_spec(AW, 0), _row_spec(AW, 1), _row_spec(AW, 2), _row_spec(LANES),
                  _full_spec((1, DH)), _full_spec((1, DH)), _full_spec(eq.shape), _full_spec(ek.shape),
                  _full_spec(ones.shape)],
        out_specs=[wide, wide, wide], out_shape=[o, o, o], vmem_mb=16)
    return outs


FG_BLOCK = (3 * AW + 3 * CW) // LANES


def _fgate_fwd(proj, bf_pad):
    s = proj.shape[0]

    def body(fg_ref, b_ref, o_ref, carry_ref):
        i = pl.program_id(0)

        @pl.when(i == 0)
        def _():
            carry_ref[...] = jnp.zeros_like(carry_ref)

        z = fg_ref[...] + b_ref[...]
        logf = jnp.minimum(z, 0.0) - jnp.log1p(jnp.exp(-jnp.abs(z)))
        row = lax.broadcasted_iota(jnp.int32, (TR, TR), 0)
        col = lax.broadcasted_iota(jnp.int32, (TR, TR), 1)
        tri = (col <= row).astype(F32)
        cs = _dot(tri, logf, NN, lax.Precision.HIGHEST) + carry_ref[0:1, :]
        o_ref[...] = cs
        carry_ref[...] = jnp.broadcast_to(cs[TR - 1:TR, :], carry_ref.shape)

    return pl.pallas_call(
        body, name="fgate_fwd", grid=(s // TR,),
        in_specs=[_row_spec(LANES, FG_BLOCK), _full_spec((1, LANES))],
        out_specs=_row_spec(LANES), out_shape=jax.ShapeDtypeStruct((s, LANES), F32),
        scratch_shapes=[pltpu.VMEM((SUB, LANES), F32)],
        compiler_params=_params(("arbitrary",)),
    )(proj, bf_pad)


def _fgate_bwd(dfcol, proj, bf_pad):
    s = proj.shape[0]
    nb = s // TR

    def body(df_ref, fg_ref, b_ref, o_ref, db_ref, carry_ref):
        i = pl.program_id(0)

        @pl.when(i == 0)
        def _():
            carry_ref[...] = jnp.zeros_like(carry_ref)
            db_ref[...] = jnp.zeros_like(db_ref)

        row = lax.broadcasted_iota(jnp.int32, (TR, TR), 0)
        col = lax.broadcasted_iota(jnp.int32, (TR, TR), 1)
        tri = (col >= row).astype(F32)
        dlogf = _dot(tri, df_ref[...], NN, lax.Precision.HIGHEST) + carry_ref[0:1, :]
        carry_ref[...] = jnp.broadcast_to(dlogf[0:1, :], carry_ref.shape)
        z = fg_ref[...] + b_ref[...]
        dfg = dlogf * _sigmoid(-z)
        o_ref[...] = dfg.astype(BF16)
        db_ref[0:1, :] += jnp.sum(dfg, axis=0, keepdims=True)

    rev = lambda col: pl.BlockSpec((TR, LANES), lambda i, col=col: (nb - 1 - i, col))
    return pl.pallas_call(
        body, name="fgate_bwd", grid=(nb,),
        in_specs=[rev(0), rev(FG_BLOCK), _full_spec((1, LANES))],
        out_specs=(rev(0), _full_spec((SUB, LANES))),
        out_shape=(jax.ShapeDtypeStruct((s, LANES), BF16), jax.ShapeDtypeStruct((SUB, LANES), F32)),
        scratch_shapes=[pltpu.VMEM((SUB, LANES), F32)],
        compiler_params=_params(("arbitrary",)),
    )(dfcol, proj, bf_pad)


def _resid_norm2(x, z, mod, g):
    s = x.shape[0]

    def body(x_ref, z_ref, mod_ref, g_ref, x1_ref, h_ref):
        x1 = x_ref[...] + mod_ref[2:3, :] * z_ref[...]
        x1_ref[...] = x1
        r = lax.rsqrt(jnp.mean(x1 * x1, axis=-1, keepdims=True) + EPS)
        nrm = x1 * r * g_ref[...]
        h_ref[...] = (nrm * (1.0 + mod_ref[4:5, :]) + mod_ref[3:4, :]).astype(BF16)

    return pl.pallas_call(
        body, name="resid_norm2", grid=(s // TRE,),
        in_specs=[_erow(D), _erow(D), _full_spec((SUB, D)), _full_spec((1, D))],
        out_specs=(_erow(D), _erow(D)),
        out_shape=(jax.ShapeDtypeStruct((s, D), F32), jax.ShapeDtypeStruct((s, D), BF16)),
        compiler_params=_params(("parallel",), 24),
    )(x, z, mod, g)


def _loss_head(x1, y, tgt, mod):
    s = x1.shape[0]

    def body(x1_ref, y_ref, t_ref, mod_ref, dout_ref, dy_ref, vec_ref):
        @pl.when(pl.program_id(0) == 0)
        def _():
            vec_ref[...] = jnp.zeros_like(vec_ref)

        yv = y_ref[...]
        g2 = mod_ref[5:6, :]
        diff = x1_ref[...] + g2 * yv - t_ref[...]
        dout = diff * (1.0 / D)
        dout_ref[...] = dout
        dy_ref[...] = (g2 * dout).astype(BF16)
        vec_ref[0:1, :] += jnp.sum(dout * yv, axis=0, keepdims=True)
        vec_ref[1:2, :] += jnp.sum(diff * diff, axis=0, keepdims=True)

    return pl.pallas_call(
        body, name="loss_head", grid=(s // TRE,),
        in_specs=[_erow(D), _erow(D), _erow(D), _full_spec((SUB, D))],
        out_specs=(_erow(D), _erow(D), _full_spec((SUB, D))),
        out_shape=(jax.ShapeDtypeStruct((s, D), F32), jax.ShapeDtypeStruct((s, D), BF16),
                   jax.ShapeDtypeStruct((SUB, D), F32)),
        compiler_params=_params(("arbitrary",), 24),
    )(x1, y, tgt, mod)


def _norm_mod_bwd(dh, xin, dres, zin, mod, g, scale_row, gate_row, name, hosted=None):
    s = dh.shape[0]
    with_gate = gate_row is not None

    def body(*refs):
        if with_gate:
            dh_ref, x_ref, dres_ref, z_ref, mod_ref, g_ref, dx_ref, dz_ref, vec_ref = refs
        else:
            dh_ref, x_ref, dres_ref, mod_ref, g_ref, dx_ref, vec_ref = refs

        @pl.when(pl.program_id(0) == 0)
        def _():
            vec_ref[...] = jnp.zeros_like(vec_ref)

        xv = x_ref[...]
        dhv = dh_ref[...]
        gv = g_ref[...]
        r = lax.rsqrt(jnp.mean(xv * xv, axis=-1, keepdims=True) + EPS)
        xh = xv * r
        dn = dhv * (1.0 + mod_ref[scale_row:scale_row + 1, :])
        dxh = dn * gv
        dx = dres_ref[...] + r * (dxh - xh * jnp.mean(dxh * xh, axis=-1, keepdims=True))
        dx_ref[...] = dx
        vec_ref[0:1, :] += jnp.sum(dhv, axis=0, keepdims=True)
        vec_ref[1:2, :] += jnp.sum(dhv * (xh * gv), axis=0, keepdims=True)
        vec_ref[2:3, :] += jnp.sum(dn * xh, axis=0, keepdims=True)
        if with_gate:
            dz_ref[...] = (mod_ref[gate_row:gate_row + 1, :] * dx).astype(BF16)
            vec_ref[3:4, :] += jnp.sum(dx * z_ref[...], axis=0, keepdims=True)

    ins = [dh, xin, dres] + ([zin] if with_gate else []) + [mod, g]
    in_specs = [_erow(D)] * (4 if with_gate else 3) + [_full_spec((SUB, D)), _full_spec((1, D))]
    out_specs = [_erow(D)] + ([_erow(D)] if with_gate else []) + [_full_spec((SUB, D))]
    out_shape = [jax.ShapeDtypeStruct((s, D), F32)] + ([jax.ShapeDtypeStruct((s, D), BF16)] if with_gate else []) \
        + [jax.ShapeDtypeStruct((SUB, D), F32)]
    outs, moved = _call(body, ins, name=name, grid=(s // TRE,), in_specs=in_specs, out_specs=out_specs,
                        out_shape=out_shape, vmem_mb=32, hosted=hosted)
    return outs + (moved,) if hosted else outs


XIN_BLOCK = 3 * AW // LANES
BG_BLOCK = XIN_BLOCK + CW // LANES
CG_BLOCK = BG_BLOCK + CW // LANES


def _seq_spec(s, first_block):
    return pl.BlockSpec((s, LANES), lambda j, fb=first_block: (0, fb + j))


def _mixconv_fwd(proj, w):
    s = proj.shape[0]

    def body(xin_ref, bg_ref, cg_ref, w_ref, o_ref):
        cx = cg_ref[...] * xin_ref[...]
        cv, _, _ = _conv_taps(cx, None, w_ref[...])
        o_ref[...] = bg_ref[...] * cv

    return pl.pallas_call(
        body, name="mixconv_fwd", grid=(CW // LANES,),
        in_specs=[_seq_spec(s, XIN_BLOCK), _seq_spec(s, BG_BLOCK), _seq_spec(s, CG_BLOCK),
                  pl.BlockSpec((3, LANES), lambda j: (0, j))],
        out_specs=_seq_spec(s, 0), out_shape=jax.ShapeDtypeStruct((s, CW), F32),
        compiler_params=_params(("parallel",), 32),
    )(proj, proj, proj, w)


def _mixconv_bwd(dmixed, proj, w):
    s = proj.shape[0]

    def body(d_ref, xin_ref, bg_ref, cg_ref, w_ref, dxin_ref, dbg_ref, dcg_ref, dw_ref):
        wv = w_ref[...]
        xin, cg, dconv = xin_ref[...], cg_ref[...], d_ref[...]
        cx = cg * xin
        cv, s1, s2 = _conv_taps(cx, None, wv)
        dbg_ref[...] = (dconv * cv).astype(BF16)
        dcv = dconv * bg_ref[...]
        dw_ref[...] = jnp.zeros_like(dw_ref)
        dw_ref[0:1, :] = jnp.sum(dcv * s2, axis=0, keepdims=True)
        dw_ref[1:2, :] = jnp.sum(dcv * s1, axis=0, keepdims=True)
        dw_ref[2:3, :] = jnp.sum(dcv * cx, axis=0, keepdims=True)
        dcx = _conv_taps_t(dcv, None, wv)
        dcg_ref[...] = (dcx * xin).astype(BF16)
        dxin_ref[...] = (dcx * cg).astype(BF16)

    o = jax.ShapeDtypeStruct((s, CW), BF16)
    return pl.pallas_call(
        body, name="mixconv_bwd", grid=(CW // LANES,),
        in_specs=[_seq_spec(s, AW // LANES), _seq_spec(s, XIN_BLOCK), _seq_spec(s, BG_BLOCK), _seq_spec(s, CG_BLOCK),
                  pl.BlockSpec((3, LANES), lambda j: (0, j))],
        out_specs=(_seq_spec(s, 0), _seq_spec(s, 0), _seq_spec(s, 0), pl.BlockSpec((SUB, LANES), lambda j: (0, j))),
        out_shape=(o, o, o, jax.ShapeDtypeStruct((SUB, CW), F32)),
        compiler_params=_params(("parallel",), 32),
    )(dmixed, proj, proj, proj, w)


TA = 512
NEG = -1e30


def _causal_mask():
    row = lax.broadcasted_iota(jnp.int32, (TA, TA), 0)
    col = lax.broadcasted_iota(jnp.int32, (TA, TA), 1)
    return col <= row


def _attn_fwd(qp, kp, vp):
    s = qp.shape[0]
    nq = s // TA

    def body(q_ref, k_ref, v_ref, o_ref, lse_ref):
        i = pl.program_id(1)
        slabs = [slice(SLAB * hh, SLAB * (hh + 1)) for hh in range(2)]
        q = [q_ref[:, sl] for sl in slabs]

        def block(j, carry, masked):
            keys = pl.ds(pl.multiple_of(j * TA, TA), TA)
            ms, acc = carry
            m_out, parts = [], []
            for hh in range(2):
                sc = _dot(q[hh], k_ref[keys, slabs[hh]], NT)
                if masked:
                    sc = jnp.where(_causal_mask(), sc, NEG)
                m_new = jnp.maximum(ms[hh], jnp.max(sc, axis=-1, keepdims=True))
                p = jnp.exp(sc - m_new)
                parts.append(jnp.exp(ms[hh] - m_new) * acc[:, slabs[hh]] + _dot(p.astype(BF16), v_ref[keys, slabs[hh]]))
                m_out.append(m_new)
            return tuple(m_out), jnp.concatenate(parts, axis=1)

        init = ((jnp.full((TA, 1), NEG, F32), jnp.full((TA, 1), NEG, F32)), jnp.zeros((TA, 2 * SLAB), F32))
        carry = lax.fori_loop(0, i, lambda j, cr: block(j, cr, False), init)
        ms, acc = block(i, carry, True)
        for hh in range(2):
            l = acc[:, SLAB * hh + DH:SLAB * hh + DH + 1]
            o_ref[:, DH * hh:DH * (hh + 1)] = acc[:, SLAB * hh:SLAB * hh + DH] / l
            lse_ref[0, :, hh:hh + 1] = ms[hh] + jnp.log(l)

    (o, lse), _ = _call(
        body, (qp, kp, vp), name="attn_fwd", grid=(HEADS // 2, nq),
        in_specs=[pl.BlockSpec((TA, 2 * SLAB), lambda p, i: (i, p)),
                  pl.BlockSpec((s, 2 * SLAB), lambda p, i: (0, p)),
                  pl.BlockSpec((s, 2 * SLAB), lambda p, i: (0, p))],
        out_specs=[pl.BlockSpec((TA, LANES), lambda p, i: (i, p)), pl.BlockSpec((1, TA, 2), lambda p, i: (p, i, 0))],
        out_shape=[jax.ShapeDtypeStruct((s, AW), F32), jax.ShapeDtypeStruct((HEADS // 2, s, 2), F32)],
        vmem_mb=24)
    return o, lse


def _attn_bwd(qp, kp, vp, dmixed, o, lse, hosted):
    s = qp.shape[0]
    nq = s // TA

    def body(q_ref, k_ref, v_ref, do_ref, o_ref, lse_ref, dq_ref, dk_ref, dv_ref, qb_ref, dob_ref):
        dk_ref[...] = jnp.zeros_like(dk_ref)
        dv_ref[...] = jnp.zeros_like(dv_ref)
        slabs = [slice(SLAB * hh, SLAB * (hh + 1)) for hh in range(2)]
        lane = lax.broadcasted_iota(jnp.int32, (TA, DH), 1)

        def q_block(i, _):
            i0 = pl.multiple_of(i * TA, TA)
            rows = pl.ds(i0, TA)
            for hh in range(2):
                half = slice(DH * hh, DH * (hh + 1))
                do = do_ref[rows, half]
                delta = jnp.sum(do * o_ref[rows, half], axis=-1, keepdims=True)
                dob_ref[hh, :, 0:DH] = do.astype(BF16)
                dob_ref[hh, :, DH:SLAB] = _lanes3(lane, 0, [-d for d in _split3(delta)], 0.0).astype(BF16)
                lse3 = _split3(lse_ref[0, rows, hh:hh + 1])
                qb_ref[hh, :, 0:DH] = q_ref[rows, SLAB * hh:SLAB * hh + DH]
                aug = q_ref[rows, SLAB * hh + DH:SLAB * (hh + 1)].astype(F32)
                qb_ref[hh, :, DH:SLAB] = _lanes3(lane, AUG_LSE, [-x for x in lse3], aug).astype(BF16)

            def block(j, dq, masked):
                keys = pl.ds(pl.multiple_of(j * TA, TA), TA)
                dv, dk, dqc = [], [], []
                for hh in range(2):
                    q, dob = qb_ref[hh], dob_ref[hh]
                    k = k_ref[keys, slabs[hh]]
                    sc = _dot(q, k, NT)
                    if masked:
                        sc = jnp.where(_causal_mask(), sc, NEG)
                    p = jnp.exp(sc)
                    dv.append(_dot(p.astype(BF16), dob, TN))
                    ds = (p * _dot(dob, v_ref[keys, slabs[hh]], NT)).astype(BF16)
                    dk.append(_dot(ds, q, TN))
                    dqc.append(_dot(ds, k))
                dv_ref[keys, :] += jnp.concatenate(dv, axis=1)
                dk_ref[keys, :] += jnp.concatenate(dk, axis=1)
                return dq + jnp.concatenate(dqc, axis=1)

            dq = lax.fori_loop(0, i, lambda j, acc: block(j, acc, False), jnp.zeros((TA, 2 * SLAB), F32))
            dq_ref[rows, :] = block(i, dq, True)
            return 0

        lax.fori_loop(0, nq, q_block, 0)

    pair = lambda p: (0, p)
    slab2 = pl.BlockSpec((s, 2 * SLAB), pair)
    seq = pl.BlockSpec((s, LANES), pair)
    small = pl.BlockSpec((1, s, 2), lambda p: (p, 0, 0))
    o32 = jax.ShapeDtypeStruct((s, HEADS * SLAB), F32)
    return _call(
        body, (qp, kp, vp, dmixed, o, lse), name="attn_bwd", grid=(HEADS // 2,),
        in_specs=[slab2, slab2, slab2, seq, seq, small], out_specs=[slab2, slab2, slab2], out_shape=[o32, o32, o32],
        scratch_shapes=[pltpu.VMEM((2, TA, SLAB), BF16), pltpu.VMEM((2, TA, SLAB), BF16)], vmem_mb=40, hosted=hosted)


def _qkv_post(dqp, dkp, dvp, proj, gq, gk):
    s = proj.shape[0]

    def body(dq_ref, dk_ref, dv_ref, q_ref, k_ref, gq_ref, gk_ref, dqo_ref, dko_ref, dvo_ref, df_ref, vec_ref):
        @pl.when(pl.program_id(0) == 0)
        def _():
            vec_ref[...] = jnp.zeros_like(vec_ref)

        def one(d_ref, x_ref, g_ref, o_ref, row, scale):
            dg = jnp.zeros((1, DH), F32)
            for h in range(HEADS):
                sl = slice(DH * h, DH * (h + 1))
                xv = x_ref[:, sl]
                r = lax.rsqrt(jnp.mean(xv * xv, axis=-1, keepdims=True) + EPS)
                xh = xv * r
                dn = d_ref[:, SLAB * h:SLAB * h + DH] * scale
                dg = dg + jnp.sum(dn * xh, axis=0, keepdims=True)
                dxh = dn * g_ref[...]
                o_ref[:, sl] = (r * (dxh - xh * jnp.mean(dxh * xh, axis=-1, keepdims=True))).astype(BF16)
            vec_ref[row:row + 1, 0:DH] += dg

        one(dq_ref, q_ref, gq_ref, dqo_ref, 0, QK_SCALE)
        one(dk_ref, k_ref, gk_ref, dko_ref, 1, 1.0)
        lane = lax.broadcasted_iota(jnp.int32, (TR, LANES), 1)
        df = jnp.zeros((TR, LANES), F32)
        for h in range(HEADS):
            dvo_ref[:, DH * h:DH * (h + 1)] = dv_ref[:, SLAB * h:SLAB * h + DH].astype(BF16)
            row_sum = dq_ref[:, SLAB * h + DH:SLAB * h + DH + 1]
            col_sum = dk_ref[:, SLAB * h + DH + AUG_ONE:SLAB * h + DH + AUG_ONE + 1]
            df = jnp.where(lane == h, row_sum - col_sum, df)
        df_ref[...] = df

    o = jax.ShapeDtypeStruct((s, AW), BF16)
    wide = _row_spec(HEADS * SLAB)
    outs, _ = _call(
        body, (dqp, dkp, dvp, proj, proj, gq, gk), name="qkv_post", grid=(s // TR,),
        in_specs=[wide, wide, wide, _row_spec(AW, 0), _row_spec(AW, 1), _full_spec((1, DH)), _full_spec((1, DH))],
        out_specs=[_row_spec(AW), _row_spec(AW), _row_spec(AW), _row_spec(LANES), _full_spec((SUB, LANES))],
        out_shape=[o, o, o, jax.ShapeDtypeStruct((s, LANES), F32), jax.ShapeDtypeStruct((SUB, LANES), F32)])
    return outs


TF = 256
NJ = DFF // TF
FFN_ROWS_FWD = 1024
FFN_ROWS_BWD = 1024


def _ffn_fwd(h2, wup_t, cw, wd):
    s = h2.shape[0]
    tr = FFN_ROWS_FWD
    nr = s // tr

    def body(h_ref, wu_ref, cg_ref, cv_ref, wd_ref, pg_ref, pv_ref, y_ref, halo_ref, act_ref):
        r, j = pl.program_id(0), pl.program_id(1)
        hv = h_ref[...]
        pg = _dot(hv, wu_ref[0], NT).astype(BF16)
        pv = _dot(hv, wu_ref[1], NT).astype(BF16)
        pg_ref[...] = pg
        pv_ref[...] = pv
        pgf, pvf = pg.astype(F32), pv.astype(F32)
        ug, _, _ = _conv_taps(pgf, jnp.where(r > 0, halo_ref[j, 0], 0.0), cg_ref[...])
        uv, _, _ = _conv_taps(pvf, jnp.where(r > 0, halo_ref[j, 1], 0.0), cv_ref[...])
        halo_ref[j, 0] = pgf[tr - SUB:tr, :]
        halo_ref[j, 1] = pvf[tr - SUB:tr, :]
        act = (ug * _sigmoid(ug) * uv).astype(BF16)
        for t in range(NJ):
            @pl.when(j == t)
            def _(t=t):
                act_ref[:, t * TF:(t + 1) * TF] = act

        @pl.when(j == NJ - 1)
        def _():
            y_ref[...] = _dot(act_ref[...], wd_ref[...])

    pre = jax.ShapeDtypeStruct((s, DFF), BF16)
    return pl.pallas_call(
        body, name="ffn_fwd", grid=(nr, NJ),
        in_specs=[pl.BlockSpec((tr, D), lambda r, j: (r, 0)),
                  pl.BlockSpec((2, TF, D), lambda r, j: (0, j, 0)),
                  pl.BlockSpec((3, TF), lambda r, j: (0, j)),
                  pl.BlockSpec((3, TF), lambda r, j: (0, NJ + j)),
                  pl.BlockSpec((DFF, D), lambda r, j: (0, 0))],
        out_specs=(pl.BlockSpec((tr, TF), lambda r, j: (r, j)),
                   pl.BlockSpec((tr, TF), lambda r, j: (r, j)),
                   pl.BlockSpec((tr, D), lambda r, j: (r, 0))),
        out_shape=(pre, pre, jax.ShapeDtypeStruct((s, D), F32)),
        scratch_shapes=[pltpu.VMEM((NJ, 2, SUB, TF), F32), pltpu.VMEM((tr, DFF), BF16)],
        compiler_params=_params(("arbitrary", "arbitrary"), 56),
    )(h2, wup_t, cw, cw, wd)


def _ffn_bwd(dy, h2, pre_g, pre_v, wup_t, cw, wd):
    s = h2.shape[0]
    tr = FFN_ROWS_BWD
    nr = s // tr
    hb = tr // (2 * SUB)

    def body(dy_ref, h_ref, pg_ref, pv_ref, hg_ref, hv_ref, wu_ref, cg_ref, cv_ref, wd_ref,
             dh_ref, dwu_ref, dwd_ref, dcg_ref, dcv_ref, nxt_ref, awu_ref, awd_ref):
        j, r = pl.program_id(0), pl.program_id(1)
        rr = nr - 1 - r
        row0 = pl.multiple_of(rr * tr, tr)
        cwg, cwv = cg_ref[...], cv_ref[...]
        pg, pv = pg_ref[...].astype(F32), pv_ref[...].astype(F32)
        ug, g1, g2 = _conv_taps(pg, jnp.where(rr > 0, hg_ref[SUB:2 * SUB, :].astype(F32), 0.0), cwg)
        uv, v1, v2 = _conv_taps(pv, jnp.where(rr > 0, hv_ref[SUB:2 * SUB, :].astype(F32), 0.0), cwv)
        sg = _sigmoid(ug)
        sil = ug * sg
        act = (sil * uv).astype(BF16)
        dyv = dy_ref[...]
        da = _dot(dyv, wd_ref[...], NT)
        dug = da * uv * (sg * (1.0 + ug * (1.0 - sg)))
        duv = da * sil
        dpg = _conv_taps_t(dug, jnp.where(r > 0, nxt_ref[0], 0.0), cwg)
        dpv = _conv_taps_t(duv, jnp.where(r > 0, nxt_ref[1], 0.0), cwv)
        nxt_ref[0] = dug[0:SUB, :]
        nxt_ref[1] = duv[0:SUB, :]
        dpgb, dpvb = dpg.astype(BF16), dpv.astype(BF16)
        hv = h_ref[...]
        dwd = _dot(act, dyv, TN)
        dpb = jnp.concatenate([dpgb, dpvb], axis=1)
        dwu = _dot(dpb, hv, TN)
        dh = _dot(dpb, wu_ref[...].reshape(2 * TF, D))

        def taps(du, x0, x1, x2):
            return (jnp.sum(du * x2, axis=0, keepdims=True), jnp.sum(du * x1, axis=0, keepdims=True),
                    jnp.sum(du * x0, axis=0, keepdims=True))

        tg, tv = taps(dug, pg, g1, g2), taps(duv, pv, v1, v2)

        @pl.when(r == 0)
        def _():
            awd_ref[...] = dwd
            awu_ref[...] = dwu
            dcg_ref[...] = jnp.zeros_like(dcg_ref)
            dcv_ref[...] = jnp.zeros_like(dcv_ref)

        @pl.when(r > 0)
        def _():
            awd_ref[...] += dwd
            awu_ref[...] += dwu

        @pl.when(r == nr - 1)
        def _():
            dwd_ref[...] = awd_ref[...].astype(BF16)
            dwu_ref[...] = awu_ref[...].astype(BF16).reshape(2, TF, D)

        for t in range(3):
            dcg_ref[t:t + 1, :] += tg[t]
            dcv_ref[t:t + 1, :] += tv[t]

        @pl.when(j == 0)
        def _():
            dh_ref[pl.ds(row0, tr), :] = dh

        @pl.when(j > 0)
        def _():
            dh_ref[pl.ds(row0, tr), :] += dh

    rows = lambda j, r: (nr - 1 - r, 0)
    tile = lambda j, r: (nr - 1 - r, j)
    halo = lambda j, r: (jnp.maximum((nr - 1 - r) * hb - 1, 0), j)
    return pl.pallas_call(
        body, name="ffn_bwd", grid=(NJ, nr),
        in_specs=[pl.BlockSpec((tr, D), rows), pl.BlockSpec((tr, D), rows),
                  pl.BlockSpec((tr, TF), tile), pl.BlockSpec((tr, TF), tile),
                  pl.BlockSpec((2 * SUB, TF), halo), pl.BlockSpec((2 * SUB, TF), halo),
                  pl.BlockSpec((2, TF, D), lambda j, r: (0, j, 0)),
                  pl.BlockSpec((3, TF), lambda j, r: (0, j)), pl.BlockSpec((3, TF), lambda j, r: (0, NJ + j)),
                  pl.BlockSpec((TF, D), lambda j, r: (j, 0))],
        out_specs=(pl.BlockSpec((s, D), lambda j, r: (0, 0)),
                   pl.BlockSpec((2, TF, D), lambda j, r: (0, j, 0)),
                   pl.BlockSpec((TF, D), lambda j, r: (j, 0)),
                   pl.BlockSpec((SUB, TF), lambda j, r: (0, j)), pl.BlockSpec((SUB, TF), lambda j, r: (0, j))),
        out_shape=(jax.ShapeDtypeStruct((s, D), F32),
                   jax.ShapeDtypeStruct((2, DFF, D), BF16), jax.ShapeDtypeStruct((DFF, D), BF16),
                   jax.ShapeDtypeStruct((SUB, DFF), F32), jax.ShapeDtypeStruct((SUB, DFF), F32)),
        scratch_shapes=[pltpu.VMEM((2, SUB, TF), F32), pltpu.VMEM((2 * TF, D), F32), pltpu.VMEM((TF, D), F32)],
        compiler_params=_params(("arbitrary", "arbitrary"), 56),
    )(dy, h2, pre_g, pre_v, pre_g, pre_v, wup_t, cw, cw, wd)


def _adam(w, g, m, v):
    m = ADAM_B1 * m + (1.0 - ADAM_B1) * g
    v = ADAM_B2 * v + (1.0 - ADAM_B2) * (g * g)
    m_hat = m / (1.0 - ADAM_B1 ** ADAM_STEP)
    v_hat = v / (1.0 - ADAM_B2 ** ADAM_STEP)
    delta = -ADAM_LR * (m_hat / (jnp.sqrt(v_hat) + ADAM_EPS) + ADAM_WD * w)
    return delta, m, v


NCHIP = NDEV // 2


def _pair_add(mine, theirs, tr, name):
    _, _, rws, cols = mine.shape

    def body(a_ref, b_ref, o_ref):
        c = lax.axis_index("c")
        o_ref[0] = (a_ref[0, c].astype(F32) + b_ref[0].astype(F32)).astype(BF16)

    (out,), _ = _call(
        body, (mine, theirs), name=name, grid=(NCHIP, rws // tr),
        in_specs=[pl.BlockSpec((1, 2, tr, cols), lambda q, i: (q, 0, i, 0)),
                  pl.BlockSpec((1, tr, cols), lambda q, i: (q, i, 0))],
        out_specs=[pl.BlockSpec((1, tr, cols), lambda q, i: (q, i, 0))],
        out_shape=[jax.ShapeDtypeStruct((NCHIP, rws, cols), BF16)], vmem_mb=16)
    return out


def _adamw_sharded(parts, w, m, v, tr, name, hosted=None):
    rws, cols = w.shape
    n_parts = parts.shape[0]

    def body(p_ref, w_ref, m_ref, v_ref, g_ref, d_ref, mo_ref, vo_ref):
        g = p_ref[0].astype(F32)
        for q in range(1, n_parts):
            g = g + p_ref[q].astype(F32)
        g_ref[...] = g
        d_ref[...], mo_ref[...], vo_ref[...] = _adam(w_ref[...], g, m_ref[...], v_ref[...])

    blk = pl.BlockSpec((tr, cols), lambda i: (i, 0))
    o = jax.ShapeDtypeStruct((rws, cols), F32)
    outs, moved = _call(
        body, (parts, w, m, v), name=name, grid=(rws // tr,),
        in_specs=[pl.BlockSpec((n_parts, tr, cols), lambda i: (0, i, 0)), blk, blk, blk],
        out_specs=[blk, blk, blk, blk], out_shape=[o, o, o, o], vmem_mb=44 if tr > 256 else 24, hosted=hosted)
    return (outs, moved) if hosted else outs


def _adamw_ada(c_all, dmod_my, w, m, v):
    rws, cols = w.shape
    tr = 256

    def body(c_ref, dm_ref, w_ref, m_ref, v_ref, g_ref, d_ref, mo_ref, vo_ref):
        cv = c_ref[...]
        act = cv * _sigmoid(cv)
        g = _dot(act, dm_ref[...], TN, lax.Precision.HIGHEST)
        g_ref[...] = g
        d_ref[...], mo_ref[...], vo_ref[...] = _adam(w_ref[...], g, m_ref[...], v_ref[...])

    blk = pl.BlockSpec((tr, cols), lambda i: (i, 0))
    o = jax.ShapeDtypeStruct((rws, cols), F32)
    return pl.pallas_call(
        body, name="adamw_ada", grid=(rws // tr,),
        in_specs=[pl.BlockSpec((NDEV, tr), lambda i: (0, i)), _full_spec((NDEV, cols)), blk, blk, blk],
        out_specs=(blk, blk, blk, blk), out_shape=(o, o, o, o),
        compiler_params=_params(("parallel",), 32),
    )(c_all, dmod_my, w, m, v)


REP_ROWS = 16
ROW_N1, ROW_N2, ROW_LOSS, ROW_MISC = 6, 7, 8, 9
LANE_BF, LANE_GQ, LANE_GK = 0, 128, 256


def _adamw_small(rep_all, conv_all, wmv):
    n_ff = wmv[6][0].shape[1]

    def body(*refs):
        rep_ref, conv_ref = refs[:2]
        ins = refs[2:2 + 24]
        outs = refs[2 + 24:]
        loss_ref, outs = outs[0], outs[1:]
        g_rep = rep_ref[0]
        g_conv = conv_ref[0]
        for d in range(1, NDEV):
            g_rep = g_rep + rep_ref[d]
            g_conv = g_conv + conv_ref[d]
        loss_ref[...] = (0.5 / D) * jnp.sum(g_rep[ROW_LOSS:ROW_LOSS + 1, :], axis=-1, keepdims=True)
        grads = [
            None,
            g_rep[ROW_N1:ROW_N1 + 1, :],
            g_rep[ROW_MISC:ROW_MISC + 1, LANE_BF:LANE_BF + HEADS],
            g_rep[ROW_MISC:ROW_MISC + 1, LANE_GQ:LANE_GQ + DH],
            g_rep[ROW_MISC:ROW_MISC + 1, LANE_GK:LANE_GK + DH],
            g_rep[ROW_N2:ROW_N2 + 1, :],
            g_conv[0:3, 0:n_ff],
            g_conv[0:3, n_ff:n_ff + DH],
        ]
        for p in range(8):
            w_ref, m_ref, v_ref = ins[3 * p:3 * p + 3]
            g_ref, d_ref, mo_ref, vo_ref = outs[4 * p:4 * p + 4]
            if p == 0:
                for nmod in range(NMOD):
                    sl = slice(D * nmod, D * (nmod + 1))
                    g = g_rep[nmod:nmod + 1, :]
                    g_ref[:, sl] = g
                    d_ref[:, sl], mo_ref[:, sl], vo_ref[:, sl] = _adam(w_ref[:, sl], g, m_ref[:, sl], v_ref[:, sl])
            else:
                g = grads[p]
                g_ref[...] = g
                d_ref[...], mo_ref[...], vo_ref[...] = _adam(w_ref[...], g, m_ref[...], v_ref[...])

    flat = [a for trio in wmv for a in trio]
    out_shape = [jax.ShapeDtypeStruct((1, 1), F32)]
    for trio in wmv:
        out_shape += [jax.ShapeDtypeStruct(trio[0].shape, F32)] * 4
    return pl.pallas_call(
        body, name="adamw_small", out_shape=tuple(out_shape),
        compiler_params=_params(None, 32),
    )(rep_all, conv_all, *flat)


FG_FIRST = 3 * AW
N_IN = DIN // NDEV


def _w_in_runs():
    runs = []
    for d in range(NDEV):
        lo, hi = N_IN * d, N_IN * (d + 1)
        for a, b, shift in ((0, FG_FIRST, 0), (FG_FIRST, FG_FIRST + HEADS, DIN - HEADS - FG_FIRST),
                            (FG_FIRST + HEADS, DIN, -HEADS)):
            a, b = max(a, lo), min(b, hi)
            if a < b:
                runs.append((d, a - lo, a + shift, b - a))
    return runs


W_IN_ROWS = 256
N_IN_PAD = 512


def _identity(n):
    return (lax.broadcasted_iota(jnp.int32, (n, n), 0) == lax.broadcasted_iota(jnp.int32, (n, n), 1)).astype(BF16)


def _assemble_w_in(g_in, hosted):
    def body(g_ref, o_ref, t_ref):
        eye = _identity(W_IN_ROWS)
        shard = None
        for d, src, dst, width in _w_in_runs():
            if d != shard:
                t_ref[:, 0:N_IN] = _dot(eye, g_ref[d], NT).astype(BF16)
                shard = d
            o_ref[:, dst:dst + width] = t_ref[:, src:src + width]
        o_ref[:, DIN:DINP] = jnp.zeros((W_IN_ROWS, DINP - DIN), o_ref.dtype)

    (out,), moved = _call(
        body, (g_in,), name="assemble_w_in", grid=(D // W_IN_ROWS,),
        in_specs=[pl.BlockSpec((NDEV, N_IN, W_IN_ROWS), lambda i: (0, 0, i))],
        out_specs=[pl.BlockSpec((W_IN_ROWS, DINP), lambda i: (i, 0))],
        out_shape=[jax.ShapeDtypeStruct((D, DINP), g_in.dtype)],
        scratch_shapes=[pltpu.VMEM((W_IN_ROWS, N_IN_PAD), BF16)], vmem_mb=16, hosted=hosted)
    return out, moved


def _scatter_dw_in(dwp):
    def body(w_ref, o_ref, t_ref):
        eye = _identity(W_IN_ROWS)
        runs = _w_in_runs()
        for i, (d, src, dst, width) in enumerate(runs):
            t_ref[:, src:src + width] = w_ref[:, dst:dst + width]
            if i + 1 == len(runs) or runs[i + 1][0] != d:
                o_ref[d // 2, d % 2] = _dot(t_ref[:, 0:N_IN], eye, TN).astype(BF16)

    (out,), _ = _call(
        body, (dwp,), name="scatter_dw_in", grid=(D // W_IN_ROWS,),
        in_specs=[pl.BlockSpec((W_IN_ROWS, DINP), lambda i: (i, 0))],
        out_specs=[pl.BlockSpec((NCHIP, 2, N_IN, W_IN_ROWS), lambda i: (0, 0, 0, i))],
        out_shape=[jax.ShapeDtypeStruct((NCHIP, 2, N_IN, D), dwp.dtype)],
        scratch_shapes=[pltpu.VMEM((W_IN_ROWS, N_IN_PAD), BF16)], vmem_mb=16)
    return out


def kernel(x, c, w_ada, b_ada, norm1_g, w_in, b_forget, q_norm_g, k_norm_g, conv_mix_w, w_out, norm2_g, w_up, ffn_conv_w, w_down, loss_target, m_w_ada, m_b_ada, m_norm1_g, m_w_in, m_b_forget, m_q_norm_g, m_k_norm_g, m_conv_mix_w, m_w_out, m_norm2_g, m_w_up, m_ffn_conv_w, m_w_down, v_w_ada, v_b_ada, v_norm1_g, v_w_in, v_b_forget, v_q_norm_g, v_k_norm_g, v_conv_mix_w, v_w_out, v_norm2_g, v_w_up, v_ffn_conv_w, v_w_down):
    me = 4 * lax.axis_index("x") + 2 * lax.axis_index("y") + lax.axis_index("c")
    xs, tgt = x[0], loss_target[0]
    n_ada = w_ada.shape[2]
    n_ff = w_up.shape[2]

    conv_w = jnp.concatenate([ffn_conv_w[0], conv_mix_w[0]], axis=1)
    conv_w = jnp.concatenate([conv_w, jnp.zeros((SUB - 3, conv_w.shape[1]), F32)], axis=0)
    c_all, conv_all, g_in = _exchange(
        [(c.reshape(SUB, D // SUB), "ag"), (conv_w, "ag"), (jnp.transpose(w_in[0]).astype(BF16), "ag2")],
        "exchange_w_in")
    g_in, w_out_b, w_up_b, w_down_b = lax.optimization_barrier(
        (g_in, w_out[0].astype(BF16), jnp.transpose(w_up[0]).astype(BF16), w_down[0].astype(BF16)))
    g_out, g_up, g_down = _sequencer_exchange(
        [(w_out_b, "ag2"), (w_up_b, "ag2"), (w_down_b, "ag2")], "gather_weights", collective_id=1)
    c_all = c_all.reshape(NDEV, D)
    cw_ffn = jnp.transpose(conv_all[:, :3, :n_ff], (1, 0, 2)).reshape(3, 2 * DFF)
    cw_mix = jnp.transpose(conv_all[:, :3, n_ff:], (1, 0, 2)).reshape(3, CW)

    b_my = lax.dynamic_slice(b_ada, (0, me * n_ada), (1, n_ada))
    mod_part = _ada_fwd(c_all, w_ada[0], b_my)
    w_in_p, (mod_rows,) = _assemble_w_in(
        g_in, [(jnp.broadcast_to(mod_part[:, None, :], (NDEV, SUB, n_ada)), "a2a")])
    mod = mod_rows[:, 0, :].reshape(NMOD, D)
    mod = jnp.concatenate([mod, jnp.zeros((SUB - NMOD, D), F32)], axis=0)

    h = _norm_mod_fwd(xs, mod, norm1_g)
    proj = _mm(h, w_in_p, "nn", F32, 1024, 640, "proj_fwd")
    bf_pad = jnp.concatenate([b_forget, jnp.zeros((1, LANES - HEADS), F32)], axis=1)
    fcum = _fgate_fwd(proj, bf_pad)
    qp, kp, vp = _qkv_prep(proj, fcum, q_norm_g, k_norm_g)
    attn, lse = _attn_fwd(qp, kp, vp)
    w_out_f = g_out.reshape(D, D)
    w_up_t = g_up.reshape(2, DFF, D)
    w_down_f = g_down.reshape(DFF, D)
    conv = _mixconv_fwd(proj, cw_mix)
    mixed = jnp.concatenate([attn, conv], axis=1).astype(BF16)
    z = _mm(mixed, w_out_f, "nn", F32, 1024, 1024, "out_fwd")
    x1, h2 = _resid_norm2(xs, z, mod, norm2_g)
    pre_g, pre_v, y = _ffn_fwd(h2, w_up_t, cw_ffn, w_down_f)
    dout, dy, vec_l = _loss_head(x1, y, tgt, mod)

    dh2, dwup_t, dwd, dcw_g, dcw_v = _ffn_bwd(dy, h2, pre_g, pre_v, w_up_t, cw_ffn, w_down_f)
    s_down = dwd.reshape(NCHIP, 2, DFF // NDEV, D)
    s_up = dwup_t.reshape(NCHIP, 2, n_ff, D)
    dx1, dz, vec_2, (t_up, t_down) = _norm_mod_bwd(dh2, x1, dout, z, mod, norm2_g, 4, 2, "norm2_bwd",
                                                   hosted=[(s_up, "pair"), (s_down, "pair")])
    dwout = _mm(mixed, dz, "tn", BF16, 1024, 1024, "out_bwd_w")
    s_out = dwout.reshape(NCHIP, 2, D // NDEV, D)
    dmixed, (t_out,) = _mm(dz, w_out_f, "nt", F32, 1024, 1024, "out_bwd_x", hosted=[(s_out, "pair")])
    c_out = _pair_add(s_out, t_out, 128, "pair_add_out")
    c_up = _pair_add(s_up, t_up, 176, "pair_add_up")
    c_down = _pair_add(s_down, t_down, 176, "pair_add_down")
    dxin, dbg, dcg, dcw_mix = _mixconv_bwd(dmixed, proj, cw_mix)
    (dqp, dkp, dvp), (p_up, p_down, p_out) = _attn_bwd(
        qp, kp, vp, dmixed, attn, lse, [(c_up, "chips"), (c_down, "chips"), (c_out, "chips")])
    dq, dk, dvb, dfcol, vec_qk = _qkv_post(dqp, dkp, dvp, proj, q_norm_g, k_norm_g)
    dfg, vec_bf = _fgate_bwd(dfcol, proj, bf_pad)
    dproj = jnp.concatenate([dq, dk, dvb, dxin, dbg, dcg, dfg], axis=1)
    dwin_p = _mm(h, dproj, "tn", BF16, 1024, 640, "proj_bwd_w")
    s_in = _scatter_dw_in(dwin_p).reshape(NDEV, N_IN, D)
    (p_in,) = _sequencer_exchange([(s_in, "a2a")], "scatter_dw_in_partials", collective_id=2, all_peers=True)
    dh = _mm(dproj, w_in_p, "nt", F32, 1024, 512, "proj_bwd_x", vmem_mb=36)
    grad_x, vec_1 = _norm_mod_bwd(dh, xs, dx1, None, mod, norm1_g, 1, None, "norm1_bwd")

    gap = lambda n: jnp.zeros((1, n), F32)
    misc = jnp.concatenate([
        vec_bf[0:1, :HEADS], gap(LANE_GQ - LANE_BF - HEADS), vec_qk[0:1, :DH], gap(LANE_GK - LANE_GQ - DH),
        vec_qk[1:2, :DH], gap(D - LANE_GK - DH)], axis=1)
    rep = jnp.concatenate([
        vec_1[0:1], vec_1[1:2], vec_2[3:4], vec_2[0:1], vec_2[1:2], vec_l[0:1],
        vec_1[2:3], vec_2[2:3], vec_l[1:2], misc, jnp.zeros((REP_ROWS - 10, D), F32)], axis=0)
    dcw_ffn = jnp.concatenate([dcw_g, dcw_v], axis=1).reshape(SUB, NDEV, n_ff)
    dcw_all = jnp.concatenate([jnp.transpose(dcw_ffn, (1, 0, 2)),
                               jnp.transpose(dcw_mix.reshape(SUB, NDEV, DH), (1, 0, 2))], axis=2)
    r_up = _adamw_sharded(p_up, jnp.transpose(w_up[0]), jnp.transpose(m_w_up[0]), jnp.transpose(v_w_up[0]), 176,
                          "adamw_up")
    r_down = _adamw_sharded(p_down, w_down[0], m_w_down[0], v_w_down[0], 176, "adamw_down")
    rep, dcw_all, r_up, r_down = lax.optimization_barrier((rep, dcw_all, r_up, r_down))
    r_up = tuple(jnp.transpose(a) for a in r_up)
    r_out, (rep_all, conv_parts) = _adamw_sharded(p_out, w_out[0], m_w_out[0], v_w_out[0], 128, "adamw_out",
                                                  hosted=[(rep, "ag"), (dcw_all, "a2a")])
    dmod_my = lax.dynamic_slice(rep_all[:, :NMOD, :].reshape(NDEV, NMOD * D), (0, me * n_ada), (NDEV, n_ada))
    r_ada = _adamw_ada(c_all, dmod_my, w_ada[0], m_w_ada[0], v_w_ada[0])
    r_in = _adamw_sharded(p_in, jnp.transpose(w_in[0]), jnp.transpose(m_w_in[0]), jnp.transpose(v_w_in[0]), N_IN,
                          "adamw_in")
    r_in = tuple(jnp.transpose(a) for a in r_in)
    small = _adamw_small(rep_all, conv_parts, [
        [b_ada, m_b_ada, v_b_ada], [norm1_g, m_norm1_g, v_norm1_g], [b_forget, m_b_forget, v_b_forget],
        [q_norm_g, m_q_norm_g, v_q_norm_g], [k_norm_g, m_k_norm_g, v_k_norm_g], [norm2_g, m_norm2_g, v_norm2_g],
        [ffn_conv_w[0], m_ffn_conv_w[0], v_ffn_conv_w[0]], [conv_mix_w[0], m_conv_mix_w[0], v_conv_mix_w[0]]])
    loss = small[0].reshape(())
    r_bada, r_n1, r_bf, r_gq, r_gk, r_n2, r_cf, r_cm = [small[1 + 4 * p:5 + 4 * p] for p in range(8)]
    lead = lambda t: tuple(a[None] for a in t)
    per_w = [lead(r_ada), r_bada, r_n1, lead(r_in), r_bf, r_gq, r_gk, lead(r_cm), lead(r_out), r_n2,
             lead(r_up), lead(r_cf), lead(r_down)]
    outs = [loss, grad_x[None]]
    for field in range(4):
        outs += [t[field] for t in per_w]
    return tuple(outs)
```

```python
import functools

import jax
import jax.numpy as jnp
import numpy as np
from jax import lax
from jax.experimental import pallas as pl
from jax.experimental.pallas import tpu as pltpu
from jax.experimental.pallas import tpu_sc as plsc

F32 = jnp.float32
BF16 = jnp.bfloat16

NDEV = 8
D = 1024
HEADS = 8
DH = 64
AW = 512
CW = 512
DFF = 2816
DIN = 3080
DINP = 3200
NMOD = 6
EPS = 1e-6
QK_SCALE = 0.125
LANES = 128
SUB = 8

ADAM_LR = 0.001
ADAM_B1 = 0.9
ADAM_B2 = 0.999
ADAM_EPS = 1e-08
ADAM_WD = 0.01
ADAM_STEP = 10

MESH = pl.DeviceIdType.MESH
ANY = pl.BlockSpec(memory_space=pl.ANY)

NN = (((1,), (0,)), ((), ()))
NT = (((1,), (1,)), ((), ()))
TN = (((0,), (0,)), ((), ()))


def _dot(a, b, dims=NN, precision=None):
    return lax.dot_general(a, b, dims, precision=precision, preferred_element_type=F32)


def _params(sem=None, vmem_mb=None):
    kw = {}
    if sem is not None:
        kw["dimension_semantics"] = sem
    if vmem_mb is not None:
        kw["vmem_limit_bytes"] = vmem_mb * 1024 * 1024
    return pltpu.CompilerParams(**kw)


def _sigmoid(x):
    return 0.5 * jnp.tanh(0.5 * x) + 0.5


class _Exchange:
    def __init__(self, items):
        self.arrays = [pltpu.with_memory_space_constraint(a, pltpu.HBM) for a, _ in items]
        self.modes = [m for _, m in items]
        self.n = len(items)
        self.out_shape = []
        for a, m in items:
            sh = {"ag": (NDEV,) + a.shape, "ag2": (NDEV,) + a.shape, "pair": a.shape[:1] + a.shape[2:]}.get(m, a.shape)
            self.out_shape.append(jax.ShapeDtypeStruct(sh, a.dtype))
        self.scratch = [pltpu.SemaphoreType.DMA((self.n, NDEV - 1)), pltpu.SemaphoreType.DMA((self.n, NDEV - 1)),
                        pltpu.SemaphoreType.DMA((self.n,))]

    def _plan(self, srcs, outs, sems):
        send_sems, recv_sems, loc_sems = sems
        x, y, c = lax.axis_index("x"), lax.axis_index("y"), lax.axis_index("c")
        me, my_chip = 4 * x + 2 * y + c, 2 * x + y
        sib = (x, y, 1 - c)
        local, first, landed, forwards, arrivals = [], [], [], [], []

        def remote(a, k, src, dst, to):
            return pltpu.make_async_remote_copy(src_ref=src, dst_ref=dst, send_sem=send_sems.at[a, k],
                                                recv_sem=recv_sems.at[a, k], device_id=to, device_id_type=MESH)

        for a, mode in enumerate(self.modes):
            src, out = srcs[a], outs[a]
            if mode in ("ag", "a2a"):
                piece = (lambda slot, src=src: src) if mode == "ag" else (lambda slot, src=src: src.at[slot])
                local.append(pltpu.make_async_copy(piece(me), out.at[me], loc_sems.at[a]))
                for r in range(1, NDEV):
                    px = 1 - x if (r >> 2) & 1 else x
                    py = 1 - y if (r >> 1) & 1 else y
                    pc = 1 - c if r & 1 else c
                    pidx = 4 * px + 2 * py + pc
                    first.append(remote(a, r - 1, piece(pidx), out.at[me], (px, py, pc)))
                    arrivals.append(remote(a, r - 1, piece(pidx), out.at[pidx], (px, py, pc)))
            elif mode == "ag2":
                local.append(pltpu.make_async_copy(src, out.at[me], loc_sems.at[a]))
                first.append(remote(a, 0, src, out.at[me], sib))
                arrivals.append(remote(a, 0, src, out.at[me + 1 - 2 * c], sib))
                for j, (px, py) in enumerate([(1 - x, y), (x, 1 - y), (1 - x, 1 - y)]):
                    theirs = out.at[4 * px + 2 * py + c]
                    first.append(remote(a, 1 + j, src, out.at[me], (px, py, c)))
                    landed.append(remote(a, 1 + j, src, theirs, (px, py, c)))
                    forwards.append(remote(a, 4 + j, theirs, theirs, sib))
                    arrivals.append(remote(a, 4 + j, src, out.at[4 * px + 2 * py + 1 - c], sib))
            elif mode == "pair":
                for q in range(NDEV // 2):
                    first.append(remote(a, q, src.at[q, 1 - c], out.at[q], sib))
                    arrivals.append(remote(a, q, src.at[q, 1 - c], out.at[q], sib))
            else:
                assert mode == "chips", mode
                local.append(pltpu.make_async_copy(src.at[my_chip], out.at[my_chip], loc_sems.at[a]))
                for j, (px, py) in enumerate([(1 - x, y), (x, 1 - y), (1 - x, 1 - y)]):
                    q = 2 * px + py
                    first.append(remote(a, 1 + j, src.at[q], out.at[my_chip], (px, py, c)))
                    arrivals.append(remote(a, 1 + j, src.at[q], out.at[q], (px, py, c)))
        return local, first, landed, forwards, arrivals

    def start(self, srcs, outs, sems):
        local, first, _, _, _ = self._plan(srcs, outs, sems)
        for cp in local + first:
            cp.start()

    def wait(self, srcs, outs, sems):
        local, first, landed, forwards, arrivals = self._plan(srcs, outs, sems)
        for cp, fwd in zip(landed, forwards):
            cp.wait_recv()
            fwd.start()
        for cp in arrivals:
            cp.wait_recv()
        for cp in first + forwards:
            cp.wait_send()
        for cp in local:
            cp.wait()


def _exchange(items, name):
    ex = _Exchange(items)
    n = ex.n

    def body(*refs):
        srcs, outs, sems = refs[:n], refs[n:2 * n], refs[2 * n:]
        ex.start(srcs, outs, sems)
        ex.wait(srcs, outs, sems)

    return pl.pallas_call(
        body, name=name,
        out_shape=tuple(ex.out_shape),
        in_specs=[ANY] * n, out_specs=tuple([ANY] * n),
        scratch_shapes=ex.scratch,
        compiler_params=pltpu.CompilerParams(has_side_effects=True),
    )(*ex.arrays)


def _sequencer_exchange(items, name, collective_id, all_peers=False):
    ex = _Exchange(items)
    srcs = [jax.new_ref(a, memory_space=pltpu.MemorySpace.HBM) for a in ex.arrays]
    outs = [jax.empty_ref(sh, memory_space=pltpu.MemorySpace.HBM) for sh in ex.out_shape]

    @pl.kernel(mesh=plsc.ScalarSubcoreMesh(axis_name="sequencer", num_cores=1), name=name,
               scratch_types=tuple(ex.scratch), compiler_params=pltpu.CompilerParams(collective_id=collective_id))
    def launch(send_sems, recv_sems, loc_sems):
        x, y, c = lax.axis_index("x"), lax.axis_index("y"), lax.axis_index("c")
        barrier = pltpu.get_barrier_semaphore()
        peers = [(x, y, 1 - c), (1 - x, y, c), (x, 1 - y, c), (1 - x, 1 - y, c)]
        if all_peers:
            peers += [(1 - x, y, 1 - c), (x, 1 - y, 1 - c), (1 - x, 1 - y, 1 - c)]
        for peer in peers:
            pl.semaphore_signal(barrier, inc=1, device_id=peer, device_id_type=MESH)
        pl.semaphore_wait(barrier, len(peers))
        sems = (send_sems, recv_sems, loc_sems)
        ex.start(srcs, outs, sems)
        ex.wait(srcs, outs, sems)

    launch()
    return [o[...] for o in outs]


def _call(body, inputs, *, name, grid, in_specs, out_specs, out_shape, scratch_shapes=(), vmem_mb=None, hosted=None):
    out_specs, out_shape, scratch_shapes = tuple(out_specs), tuple(out_shape), list(scratch_shapes)
    if not hosted:
        res = pl.pallas_call(
            body, name=name, grid=grid, in_specs=list(in_specs), out_specs=out_specs, out_shape=out_shape,
            scratch_shapes=scratch_shapes, compiler_params=_params(("arbitrary",) * len(grid), vmem_mb),
        )(*inputs)
        return tuple(res), ()
    ex = _Exchange(hosted)
    n, n_in, n_out, n_scr = ex.n, len(inputs), len(out_shape), len(scratch_shapes)

    def hosting_body(*refs):
        ins, srcs = refs[:n_in], refs[n_in:n_in + n]
        outs, landing = refs[n_in + n:n_in + n + n_out], refs[n_in + n + n_out:n_in + 2 * n + n_out]
        scratch, sems = refs[n_in + 2 * n + n_out:n_in + 2 * n + n_out + n_scr], refs[n_in + 2 * n + n_out + n_scr:]
        first = functools.reduce(jnp.logical_and, [pl.program_id(d) == 0 for d in range(len(grid))])
        last = functools.reduce(jnp.logical_and, [pl.program_id(d) == grid[d] - 1 for d in range(len(grid))])

        @pl.when(first)
        def _():
            ex.start(srcs, landing, sems)

        body(*ins, *outs, *scratch)

        @pl.when(last)
        def _():
            ex.wait(srcs, landing, sems)

    res = pl.pallas_call(
        hosting_body, name=name, grid=grid,
        in_specs=list(in_specs) + [ANY] * n, out_specs=out_specs + tuple([ANY] * n),
        out_shape=out_shape + tuple(ex.out_shape), scratch_shapes=scratch_shapes + ex.scratch,
        compiler_params=_params(("arbitrary",) * len(grid), vmem_mb),
    )(*inputs, *ex.arrays)
    return tuple(res[:n_out]), tuple(res[n_out:])


def _mm(a, b, mode, out_dtype, tm, tn, name, hosted=None, vmem_mb=24):
    if mode == "nn":
        (m, k), n = a.shape, b.shape[1]
        a_spec = pl.BlockSpec((tm, k), lambda i, j: (i, 0))
        b_spec = pl.BlockSpec((k, tn), lambda i, j: (0, j))
        dims = NN
    elif mode == "nt":
        (m, k), n = a.shape, b.shape[0]
        a_spec = pl.BlockSpec((tm, k), lambda i, j: (i, 0))
        b_spec = pl.BlockSpec((tn, k), lambda i, j: (j, 0))
        dims = NT
    else:
        (k, m), n = a.shape, b.shape[1]
        a_spec = pl.BlockSpec((k, tm), lambda i, j: (0, i))
        b_spec = pl.BlockSpec((k, tn), lambda i, j: (0, j))
        dims = TN
    assert m % tm == 0 and n % tn == 0, (m, n, tm, tn)

    def body(a_ref, b_ref, o_ref):
        o_ref[...] = _dot(a_ref[...], b_ref[...], dims).astype(o_ref.dtype)

    (out,), moved = _call(
        body, (a, b), name=name, grid=(m // tm, n // tn),
        in_specs=[a_spec, b_spec], out_specs=[pl.BlockSpec((tm, tn), lambda i, j: (i, j))],
        out_shape=[jax.ShapeDtypeStruct((m, n), out_dtype)], vmem_mb=vmem_mb, hosted=hosted)
    return (out, moved) if hosted else out


def _shift_down(x, k, fill):
    y = pltpu.roll(x, k, 0)
    row = lax.broadcasted_iota(jnp.int32, (SUB, x.shape[1]), 0)
    head = y[0:SUB, :]
    for t in range(k):
        head = jnp.where(row == t, fill[t], head)
    return jnp.concatenate([head, y[SUB:, :]], axis=0)


def _shift_up(x, k, fill):
    n = x.shape[0]
    y = pltpu.roll(x, n - k, 0)
    row = lax.broadcasted_iota(jnp.int32, (SUB, x.shape[1]), 0)
    tail = y[n - SUB:, :]
    for t in range(k):
        tail = jnp.where(row == SUB - k + t, fill[t], tail)
    return jnp.concatenate([y[:n - SUB, :], tail], axis=0)


def _conv_taps(x, halo, w):
    if halo is None:
        f1, f2 = [0.0], [0.0, 0.0]
    else:
        f1, f2 = [halo[7:8, :]], [halo[6:7, :], halo[7:8, :]]
    s1 = _shift_down(x, 1, f1)
    s2 = _shift_down(x, 2, f2)
    u = w[2:3, :] * x + w[1:2, :] * s1 + w[0:1, :] * s2
    return u, s1, s2


def _conv_taps_t(du, nxt, w):
    if nxt is None:
        f1, f2 = [0.0], [0.0, 0.0]
    else:
        f1, f2 = [nxt[0:1, :]], [nxt[0:1, :], nxt[1:2, :]]
    return w[2:3, :] * du + w[1:2, :] * _shift_up(du, 1, f1) + w[0:1, :] * _shift_up(du, 2, f2)


def _ada_fwd(c_all, w_ada, b_my):
    def body(c_ref, w_ref, b_ref, o_ref):
        cv = c_ref[...]
        act = cv * _sigmoid(cv)
        o_ref[...] = _dot(act, w_ref[...], NN, lax.Precision.HIGHEST) + b_ref[...]

    return pl.pallas_call(
        body, name="ada_fwd",
        out_shape=jax.ShapeDtypeStruct((NDEV, w_ada.shape[1]), F32),
        compiler_params=_params(None, 32),
    )(c_all, w_ada, b_my)


TR = 256
TRE = 512


def _row_spec(width, col=0, rows=TR):
    return pl.BlockSpec((rows, width), lambda i, col=col: (i, col))


def _erow(width):
    return _row_spec(width, rows=TRE)


def _full_spec(shape):
    return pl.BlockSpec(shape, lambda i: (0,) * len(shape))


def _norm_mod_fwd(x, mod, g):
    s = x.shape[0]

    def body(x_ref, mod_ref, g_ref, h_ref):
        xv = x_ref[...]
        r = lax.rsqrt(jnp.mean(xv * xv, axis=-1, keepdims=True) + EPS)
        nrm = xv * r * g_ref[...]
        h_ref[...] = (nrm * (1.0 + mod_ref[1:2, :]) + mod_ref[0:1, :]).astype(BF16)

    return pl.pallas_call(
        body, name="norm1_fwd", grid=(s // TRE,),
        in_specs=[_erow(D), _full_spec((SUB, D)), _full_spec((1, D))],
        out_specs=_erow(D), out_shape=jax.ShapeDtypeStruct((s, D), BF16),
        compiler_params=_params(("parallel",), 16),
    )(x, mod, g)


SLAB = 2 * DH
AUG_F, AUG_ONE, AUG_LSE = 0, 3, 6


def _split3(x):
    hi = x.astype(BF16).astype(F32)
    r1 = x - hi
    mid = r1.astype(BF16).astype(F32)
    return hi, mid, r1 - mid


def _lanes3(lane, first, pieces, other):
    out = other
    for k in range(3):
        out = jnp.where(lane == first + k, pieces[k], out)
    return out


def _aug_placement():
    eq = np.zeros((3 * LANES, HEADS * SLAB), np.float32)
    ek = np.zeros((3 * LANES, HEADS * SLAB), np.float32)
    ones = np.zeros((SUB, HEADS * SLAB), np.float32)
    for h in range(HEADS):
        aug = SLAB * h + DH
        for k in range(3):
            eq[LANES * k + h, aug + AUG_F + k] = 1.0
            ek[LANES * k + h, aug + AUG_ONE + k] = -1.0
            ones[0, aug + AUG_ONE + k] = 1.0
            ones[1, aug + AUG_F + k] = ones[1, aug + AUG_LSE + k] = 1.0
            ones[2, aug + k] = 1.0
    return jnp.asarray(eq, BF16), jnp.asarray(ek, BF16), jnp.asarray(ones)


FG_BLOCK = (3 * AW + 3 * CW) // LANES


def _qkv_prep(proj, bf_pad, gq, gk):
    s = proj.shape[0]

    def body(q_ref, k_ref, v_ref, fg_ref, b_ref, gq_ref, gk_ref, eq_ref, ek_ref, ones_ref, qo_ref, ko_ref, vo_ref,
             carry_ref):
        @pl.when(pl.program_id(0) == 0)
        def _():
            carry_ref[...] = jnp.zeros_like(carry_ref)

        z = fg_ref[...] + b_ref[...]
        logf = jnp.minimum(z, 0.0) - jnp.log1p(jnp.exp(-jnp.abs(z)))
        row = lax.broadcasted_iota(jnp.int32, (TR, TR), 0)
        col = lax.broadcasted_iota(jnp.int32, (TR, TR), 1)
        fcum = _dot((col <= row).astype(F32), logf, NN, lax.Precision.HIGHEST) + carry_ref[0:1, :]
        carry_ref[...] = jnp.broadcast_to(fcum[TR - 1:TR, :], carry_ref.shape)
        f3 = jnp.concatenate(_split3(fcum), axis=1).astype(BF16)
        qo_ref[...] = (_dot(f3, eq_ref[...]) + ones_ref[0:1, :]).astype(BF16)
        ko_ref[...] = (_dot(f3, ek_ref[...]) + ones_ref[1:2, :]).astype(BF16)
        vo_ref[...] = jnp.broadcast_to(ones_ref[2:3, :], vo_ref.shape).astype(BF16)
        for h in range(HEADS):
            sl = slice(DH * h, DH * (h + 1))
            lo = slice(SLAB * h, SLAB * h + DH)
            qh = q_ref[:, sl]
            r = lax.rsqrt(jnp.mean(qh * qh, axis=-1, keepdims=True) + EPS)
            qo_ref[:, lo] = (qh * r * gq_ref[...] * QK_SCALE).astype(BF16)
            kh = k_ref[:, sl]
            r = lax.rsqrt(jnp.mean(kh * kh, axis=-1, keepdims=True) + EPS)
            ko_ref[:, lo] = (kh * r * gk_ref[...]).astype(BF16)
            vo_ref[:, lo] = v_ref[:, sl].astype(BF16)

    eq, ek, ones = _aug_placement()
    o = jax.ShapeDtypeStruct((s, HEADS * SLAB), BF16)
    wide = _row_spec(HEADS * SLAB)
    outs, _ = _call(
        body, (proj, proj, proj, proj, bf_pad, gq, gk, eq, ek, ones), name="qkv_prep", grid=(s // TR,),
        in_specs=[_row_spec(AW, 0), _row_spec(AW, 1), _row_spec(AW, 2), _row_spec(LANES, FG_BLOCK),
                  _full_spec((1, LANES)), _full_spec((1, DH)), _full_spec((1, DH)), _full_spec(eq.shape),
                  _full_spec(ek.shape), _full_spec(ones.shape)],
        out_specs=[wide, wide, wide], out_shape=[o, o, o],
        scratch_shapes=[pltpu.VMEM((SUB, LANES), F32)], vmem_mb=16)
    return outs


def _resid_norm2(x, z, mod, g):
    s = x.shape[0]

    def body(x_ref, z_ref, mod_ref, g_ref, x1_ref, h_ref):
        x1 = x_ref[...] + mod_ref[2:3, :] * z_ref[...]
        x1_ref[...] = x1
        r = lax.rsqrt(jnp.mean(x1 * x1, axis=-1, keepdims=True) + EPS)
        nrm = x1 * r * g_ref[...]
        h_ref[...] = (nrm * (1.0 + mod_ref[4:5, :]) + mod_ref[3:4, :]).astype(BF16)

    return pl.pallas_call(
        body, name="resid_norm2", grid=(s // TRE,),
        in_specs=[_erow(D), _erow(D), _full_spec((SUB, D)), _full_spec((1, D))],
        out_specs=(_erow(D), _erow(D)),
        out_shape=(jax.ShapeDtypeStruct((s, D), F32), jax.ShapeDtypeStruct((s, D), BF16)),
        compiler_params=_params(("parallel",), 24),
    )(x, z, mod, g)


def _loss_head(x1, y, tgt, mod):
    s = x1.shape[0]

    def body(x1_ref, y_ref, t_ref, mod_ref, dout_ref, dy_ref, vec_ref):
        @pl.when(pl.program_id(0) == 0)
        def _():
            vec_ref[...] = jnp.zeros_like(vec_ref)

        yv = y_ref[...]
        g2 = mod_ref[5:6, :]
        diff = x1_ref[...] + g2 * yv - t_ref[...]
        dout = diff * (1.0 / D)
        dout_ref[...] = dout
        dy_ref[...] = (g2 * dout).astype(BF16)
        vec_ref[0:1, :] += jnp.sum(dout * yv, axis=0, keepdims=True)
        vec_ref[1:2, :] += jnp.sum(diff * diff, axis=0, keepdims=True)

    return pl.pallas_call(
        body, name="loss_head", grid=(s // TRE,),
        in_specs=[_erow(D), _erow(D), _erow(D), _full_spec((SUB, D))],
        out_specs=(_erow(D), _erow(D), _full_spec((SUB, D))),
        out_shape=(jax.ShapeDtypeStruct((s, D), F32), jax.ShapeDtypeStruct((s, D), BF16),
                   jax.ShapeDtypeStruct((SUB, D), F32)),
        compiler_params=_params(("arbitrary",), 24),
    )(x1, y, tgt, mod)


def _norm_mod_bwd(dh, xin, dres, zin, mod, g, scale_row, gate_row, name, hosted=None):
    s = dh.shape[0]
    with_gate = gate_row is not None

    def body(*refs):
        if with_gate:
            dh_ref, x_ref, dres_ref, z_ref, mod_ref, g_ref, dx_ref, dz_ref, vec_ref = refs
        else:
            dh_ref, x_ref, dres_ref, mod_ref, g_ref, dx_ref, vec_ref = refs

        @pl.when(pl.program_id(0) == 0)
        def _():
            vec_ref[...] = jnp.zeros_like(vec_ref)

        xv = x_ref[...]
        dhv = dh_ref[...]
        gv = g_ref[...]
        r = lax.rsqrt(jnp.mean(xv * xv, axis=-1, keepdims=True) + EPS)
        xh = xv * r
        dn = dhv * (1.0 + mod_ref[scale_row:scale_row + 1, :])
        dxh = dn * gv
        dx = dres_ref[...] + r * (dxh - xh * jnp.mean(dxh * xh, axis=-1, keepdims=True))
        dx_ref[...] = dx
        vec_ref[0:1, :] += jnp.sum(dhv, axis=0, keepdims=True)
        vec_ref[1:2, :] += jnp.sum(dhv * (xh * gv), axis=0, keepdims=True)
        vec_ref[2:3, :] += jnp.sum(dn * xh, axis=0, keepdims=True)
        if with_gate:
            dz_ref[...] = (mod_ref[gate_row:gate_row + 1, :] * dx).astype(BF16)
            vec_ref[3:4, :] += jnp.sum(dx * z_ref[...], axis=0, keepdims=True)

    ins = [dh, xin, dres] + ([zin] if with_gate else []) + [mod, g]
    in_specs = [_erow(D)] * (4 if with_gate else 3) + [_full_spec((SUB, D)), _full_spec((1, D))]
    out_specs = [_erow(D)] + ([_erow(D)] if with_gate else []) + [_full_spec((SUB, D))]
    out_shape = [jax.ShapeDtypeStruct((s, D), F32)] + ([jax.ShapeDtypeStruct((s, D), BF16)] if with_gate else []) \
        + [jax.ShapeDtypeStruct((SUB, D), F32)]
    outs, moved = _call(body, ins, name=name, grid=(s // TRE,), in_specs=in_specs, out_specs=out_specs,
                        out_shape=out_shape, vmem_mb=32, hosted=hosted)
    return outs + (moved,) if hosted else outs


XIN_BLOCK = 3 * AW // LANES
BG_BLOCK = XIN_BLOCK + CW // LANES
CG_BLOCK = BG_BLOCK + CW // LANES


def _seq_spec(s, first_block):
    return pl.BlockSpec((s, LANES), lambda j, fb=first_block: (0, fb + j))


def _mixconv_fwd(proj, w):
    s = proj.shape[0]

    def body(xin_ref, bg_ref, cg_ref, w_ref, o_ref):
        cx = cg_ref[...] * xin_ref[...]
        cv, _, _ = _conv_taps(cx, None, w_ref[...])
        o_ref[...] = bg_ref[...] * cv

    return pl.pallas_call(
        body, name="mixconv_fwd", grid=(CW // LANES,),
        in_specs=[_seq_spec(s, XIN_BLOCK), _seq_spec(s, BG_BLOCK), _seq_spec(s, CG_BLOCK),
                  pl.BlockSpec((3, LANES), lambda j: (0, j))],
        out_specs=_seq_spec(s, 0), out_shape=jax.ShapeDtypeStruct((s, CW), F32),
        compiler_params=_params(("parallel",), 32),
    )(proj, proj, proj, w)


def _mixconv_bwd(dmixed, proj, w):
    s = proj.shape[0]

    def body(d_ref, xin_ref, bg_ref, cg_ref, w_ref, dxin_ref, dbg_ref, dcg_ref, dw_ref):
        wv = w_ref[...]
        xin, cg, dconv = xin_ref[...], cg_ref[...], d_ref[...]
        cx = cg * xin
        cv, s1, s2 = _conv_taps(cx, None, wv)
        dbg_ref[...] = (dconv * cv).astype(BF16)
        dcv = dconv * bg_ref[...]
        dw_ref[...] = jnp.zeros_like(dw_ref)
        dw_ref[0:1, :] = jnp.sum(dcv * s2, axis=0, keepdims=True)
        dw_ref[1:2, :] = jnp.sum(dcv * s1, axis=0, keepdims=True)
        dw_ref[2:3, :] = jnp.sum(dcv * cx, axis=0, keepdims=True)
        dcx = _conv_taps_t(dcv, None, wv)
        dcg_ref[...] = (dcx * xin).astype(BF16)
        dxin_ref[...] = (dcx * cg).astype(BF16)

    o = jax.ShapeDtypeStruct((s, CW), BF16)
    return pl.pallas_call(
        body, name="mixconv_bwd", grid=(CW // LANES,),
        in_specs=[_seq_spec(s, AW // LANES), _seq_spec(s, XIN_BLOCK), _seq_spec(s, BG_BLOCK), _seq_spec(s, CG_BLOCK),
                  pl.BlockSpec((3, LANES), lambda j: (0, j))],
        out_specs=(_seq_spec(s, 0), _seq_spec(s, 0), _seq_spec(s, 0), pl.BlockSpec((SUB, LANES), lambda j: (0, j))),
        out_shape=(o, o, o, jax.ShapeDtypeStruct((SUB, CW), F32)),
        compiler_params=_params(("parallel",), 32),
    )(dmixed, proj, proj, proj, w)


TA = 512
NEG = -1e30


def _causal_mask():
    row = lax.broadcasted_iota(jnp.int32, (TA, TA), 0)
    col = lax.broadcasted_iota(jnp.int32, (TA, TA), 1)
    return col <= row


def _attn_fwd(qp, kp, vp):
    s = qp.shape[0]
    nq = s // TA

    def body(q_ref, k_ref, v_ref, o_ref, lse_ref):
        i = pl.program_id(1)
        slabs = [slice(SLAB * hh, SLAB * (hh + 1)) for hh in range(2)]
        q = [q_ref[:, sl] for sl in slabs]

        def block(j, carry, masked):
            keys = pl.ds(pl.multiple_of(j * TA, TA), TA)
            ms, acc = carry
            m_out, parts = [], []
            for hh in range(2):
                sc = _dot(q[hh], k_ref[keys, slabs[hh]], NT)
                if masked:
                    sc = jnp.where(_causal_mask(), sc, NEG)
                m_new = jnp.maximum(ms[hh], jnp.max(sc, axis=-1, keepdims=True))
                p = jnp.exp(sc - m_new)
                parts.append(jnp.exp(ms[hh] - m_new) * acc[:, slabs[hh]] + _dot(p.astype(BF16), v_ref[keys, slabs[hh]]))
                m_out.append(m_new)
            return tuple(m_out), jnp.concatenate(parts, axis=1)

        init = ((jnp.full((TA, 1), NEG, F32), jnp.full((TA, 1), NEG, F32)), jnp.zeros((TA, 2 * SLAB), F32))
        carry = lax.fori_loop(0, i, lambda j, cr: block(j, cr, False), init)
        ms, acc = block(i, carry, True)
        for hh in range(2):
            l = acc[:, SLAB * hh + DH:SLAB * hh + DH + 1]
            o_ref[:, DH * hh:DH * (hh + 1)] = acc[:, SLAB * hh:SLAB * hh + DH] / l
            lse_ref[0, :, hh:hh + 1] = ms[hh] + jnp.log(l)

    (o, lse), _ = _call(
        body, (qp, kp, vp), name="attn_fwd", grid=(HEADS // 2, nq),
        in_specs=[pl.BlockSpec((TA, 2 * SLAB), lambda p, i: (i, p)),
                  pl.BlockSpec((s, 2 * SLAB), lambda p, i: (0, p)),
                  pl.BlockSpec((s, 2 * SLAB), lambda p, i: (0, p))],
        out_specs=[pl.BlockSpec((TA, LANES), lambda p, i: (i, p)), pl.BlockSpec((1, TA, 2), lambda p, i: (p, i, 0))],
        out_shape=[jax.ShapeDtypeStruct((s, AW), F32), jax.ShapeDtypeStruct((HEADS // 2, s, 2), F32)],
        vmem_mb=24)
    return o, lse


def _attn_bwd(qp, kp, vp, dmixed, o, lse, hosted):
    s = qp.shape[0]
    nq = s // TA

    def body(q_ref, k_ref, v_ref, do_ref, o_ref, lse_ref, dq_ref, dk_ref, dv_ref, qb_ref, dob_ref):
        dk_ref[...] = jnp.zeros_like(dk_ref)
        dv_ref[...] = jnp.zeros_like(dv_ref)
        slabs = [slice(SLAB * hh, SLAB * (hh + 1)) for hh in range(2)]
        lane = lax.broadcasted_iota(jnp.int32, (TA, DH), 1)

        def q_block(i, _):
            i0 = pl.multiple_of(i * TA, TA)
            rows = pl.ds(i0, TA)
            for hh in range(2):
                half = slice(DH * hh, DH * (hh + 1))
                do = do_ref[rows, half]
                delta = jnp.sum(do * o_ref[rows, half], axis=-1, keepdims=True)
                dob_ref[hh, :, 0:DH] = do.astype(BF16)
                dob_ref[hh, :, DH:SLAB] = _lanes3(lane, 0, [-d for d in _split3(delta)], 0.0).astype(BF16)
                lse3 = _split3(lse_ref[0, rows, hh:hh + 1])
                qb_ref[hh, :, 0:DH] = q_ref[rows, SLAB * hh:SLAB * hh + DH]
                aug = q_ref[rows, SLAB * hh + DH:SLAB * (hh + 1)].astype(F32)
                qb_ref[hh, :, DH:SLAB] = _lanes3(lane, AUG_LSE, [-x for x in lse3], aug).astype(BF16)

            def block(j, dq, masked):
                keys = pl.ds(pl.multiple_of(j * TA, TA), TA)
                dv, dk, dqc = [], [], []
                for hh in range(2):
                    q, dob = qb_ref[hh], dob_ref[hh]
                    k = k_ref[keys, slabs[hh]]
                    sc = _dot(q, k, NT)
                    if masked:
                        sc = jnp.where(_causal_mask(), sc, NEG)
                    p = jnp.exp(sc)
                    dv.append(_dot(p.astype(BF16), dob, TN))
                    ds = (p * _dot(dob, v_ref[keys, slabs[hh]], NT)).astype(BF16)
                    dk.append(_dot(ds, q, TN))
                    dqc.append(_dot(ds, k))
                dv_ref[keys, :] += jnp.concatenate(dv, axis=1)
                dk_ref[keys, :] += jnp.concatenate(dk, axis=1)
                return dq + jnp.concatenate(dqc, axis=1)

            dq = lax.fori_loop(0, i, lambda j, acc: block(j, acc, False), jnp.zeros((TA, 2 * SLAB), F32))
            dq_ref[rows, :] = block(i, dq, True)
            return 0

        lax.fori_loop(0, nq, q_block, 0)

    pair = lambda p: (0, p)
    slab2 = pl.BlockSpec((s, 2 * SLAB), pair)
    seq = pl.BlockSpec((s, LANES), pair)
    small = pl.BlockSpec((1, s, 2), lambda p: (p, 0, 0))
    o32 = jax.ShapeDtypeStruct((s, HEADS * SLAB), F32)
    return _call(
        body, (qp, kp, vp, dmixed, o, lse), name="attn_bwd", grid=(HEADS // 2,),
        in_specs=[slab2, slab2, slab2, seq, seq, small], out_specs=[slab2, slab2, slab2], out_shape=[o32, o32, o32],
        scratch_shapes=[pltpu.VMEM((2, TA, SLAB), BF16), pltpu.VMEM((2, TA, SLAB), BF16)], vmem_mb=40, hosted=hosted)


def _qkv_post(dqp, dkp, dvp, proj, bf_pad, gq, gk):
    s = proj.shape[0]
    nb = s // TR

    def body(dq_ref, dk_ref, dv_ref, q_ref, k_ref, fg_ref, b_ref, gq_ref, gk_ref, dqo_ref, dko_ref, dvo_ref, dfg_ref,
             vec_ref, carry_ref):
        @pl.when(pl.program_id(0) == 0)
        def _():
            vec_ref[...] = jnp.zeros_like(vec_ref)
            carry_ref[...] = jnp.zeros_like(carry_ref)

        def one(d_ref, x_ref, g_ref, o_ref, row, scale):
            dg = jnp.zeros((1, DH), F32)
            for h in range(HEADS):
                sl = slice(DH * h, DH * (h + 1))
                xv = x_ref[:, sl]
                r = lax.rsqrt(jnp.mean(xv * xv, axis=-1, keepdims=True) + EPS)
                xh = xv * r
                dn = d_ref[:, SLAB * h:SLAB * h + DH] * scale
                dg = dg + jnp.sum(dn * xh, axis=0, keepdims=True)
                dxh = dn * g_ref[...]
                o_ref[:, sl] = (r * (dxh - xh * jnp.mean(dxh * xh, axis=-1, keepdims=True))).astype(BF16)
            vec_ref[row:row + 1, 0:DH] += dg

        one(dq_ref, q_ref, gq_ref, dqo_ref, 0, QK_SCALE)
        one(dk_ref, k_ref, gk_ref, dko_ref, 1, 1.0)
        lane = lax.broadcasted_iota(jnp.int32, (TR, LANES), 1)
        df = jnp.zeros((TR, LANES), F32)
        for h in range(HEADS):
            dvo_ref[:, DH * h:DH * (h + 1)] = dv_ref[:, SLAB * h:SLAB * h + DH].astype(BF16)
            row_sum = dq_ref[:, SLAB * h + DH:SLAB * h + DH + 1]
            col_sum = dk_ref[:, SLAB * h + DH + AUG_ONE:SLAB * h + DH + AUG_ONE + 1]
            df = jnp.where(lane == h, row_sum - col_sum, df)
        row = lax.broadcasted_iota(jnp.int32, (TR, TR), 0)
        col = lax.broadcasted_iota(jnp.int32, (TR, TR), 1)
        dlogf = _dot((col >= row).astype(F32), df, NN, lax.Precision.HIGHEST) + carry_ref[0:1, :]
        carry_ref[...] = jnp.broadcast_to(dlogf[0:1, :], carry_ref.shape)
        dfg = dlogf * _sigmoid(-(fg_ref[...] + b_ref[...]))
        dfg_ref[...] = dfg.astype(BF16)
        vec_ref[2:3, :] += jnp.sum(dfg, axis=0, keepdims=True)

    o = jax.ShapeDtypeStruct((s, AW), BF16)
    rev = lambda width, col=0: pl.BlockSpec((TR, width), lambda i, col=col: (nb - 1 - i, col))
    wide = rev(HEADS * SLAB)
    outs, _ = _call(
        body, (dqp, dkp, dvp, proj, proj, proj, bf_pad, gq, gk), name="qkv_post", grid=(nb,),
        in_specs=[wide, wide, wide, rev(AW, 0), rev(AW, 1), rev(LANES, FG_BLOCK), _full_spec((1, LANES)),
                  _full_spec((1, DH)), _full_spec((1, DH))],
        out_specs=[rev(AW), rev(AW), rev(AW), rev(LANES), _full_spec((SUB, LANES))],
        out_shape=[o, o, o, jax.ShapeDtypeStruct((s, LANES), BF16), jax.ShapeDtypeStruct((SUB, LANES), F32)],
        scratch_shapes=[pltpu.VMEM((SUB, LANES), F32)])
    return outs


TF = 256
NJ = DFF // TF
FFN_ROWS_FWD = 1024
FFN_ROWS_BWD = 1024


def _ffn_fwd(h2, wup_t, cw, wd):
    s = h2.shape[0]
    tr = FFN_ROWS_FWD
    nr = s // tr

    def body(h_ref, wu_ref, cg_ref, cv_ref, wd_ref, pg_ref, pv_ref, y_ref, halo_ref, act_ref):
        r, j = pl.program_id(0), pl.program_id(1)
        hv = h_ref[...]
        pg = _dot(hv, wu_ref[0], NT).astype(BF16)
        pv = _dot(hv, wu_ref[1], NT).astype(BF16)
        pg_ref[...] = pg
        pv_ref[...] = pv
        pgf, pvf = pg.astype(F32), pv.astype(F32)
        ug, _, _ = _conv_taps(pgf, jnp.where(r > 0, halo_ref[j, 0], 0.0), cg_ref[...])
        uv, _, _ = _conv_taps(pvf, jnp.where(r > 0, halo_ref[j, 1], 0.0), cv_ref[...])
        halo_ref[j, 0] = pgf[tr - SUB:tr, :]
        halo_ref[j, 1] = pvf[tr - SUB:tr, :]
        act = (ug * _sigmoid(ug) * uv).astype(BF16)
        for t in range(NJ):
            @pl.when(j == t)
            def _(t=t):
                act_ref[:, t * TF:(t + 1) * TF] = act

        @pl.when(j == NJ - 1)
        def _():
            y_ref[...] = _dot(act_ref[...], wd_ref[...])

    pre = jax.ShapeDtypeStruct((s, DFF), BF16)
    return pl.pallas_call(
        body, name="ffn_fwd", grid=(nr, NJ),
        in_specs=[pl.BlockSpec((tr, D), lambda r, j: (r, 0)),
                  pl.BlockSpec((2, TF, D), lambda r, j: (0, j, 0)),
                  pl.BlockSpec((3, TF), lambda r, j: (0, j)),
                  pl.BlockSpec((3, TF), lambda r, j: (0, NJ + j)),
                  pl.BlockSpec((DFF, D), lambda r, j: (0, 0))],
        out_specs=(pl.BlockSpec((tr, TF), lambda r, j: (r, j)),
                   pl.BlockSpec((tr, TF), lambda r, j: (r, j)),
                   pl.BlockSpec((tr, D), lambda r, j: (r, 0))),
        out_shape=(pre, pre, jax.ShapeDtypeStruct((s, D), F32)),
        scratch_shapes=[pltpu.VMEM((NJ, 2, SUB, TF), F32), pltpu.VMEM((tr, DFF), BF16)],
        compiler_params=_params(("arbitrary", "arbitrary"), 56),
    )(h2, wup_t, cw, cw, wd)


def _ffn_bwd(dy, h2, pre_g, pre_v, wup_t, cw, wd):
    s = h2.shape[0]
    tr = FFN_ROWS_BWD
    nr = s // tr
    hb = tr // (2 * SUB)

    def body(dy_ref, h_ref, pg_ref, pv_ref, hg_ref, hv_ref, wu_ref, cg_ref, cv_ref, wd_ref,
             dh_ref, dwu_ref, dwd_ref, dcg_ref, dcv_ref, nxt_ref, awu_ref, awd_ref):
        j, r = pl.program_id(0), pl.program_id(1)
        rr = nr - 1 - r
        row0 = pl.multiple_of(rr * tr, tr)
        cwg, cwv = cg_ref[...], cv_ref[...]
        pg, pv = pg_ref[...].astype(F32), pv_ref[...].astype(F32)
        ug, g1, g2 = _conv_taps(pg, jnp.where(rr > 0, hg_ref[SUB:2 * SUB, :].astype(F32), 0.0), cwg)
        uv, v1, v2 = _conv_taps(pv, jnp.where(rr > 0, hv_ref[SUB:2 * SUB, :].astype(F32), 0.0), cwv)
        sg = _sigmoid(ug)
        sil = ug * sg
        act = (sil * uv).astype(BF16)
        dyv = dy_ref[...]
        da = _dot(dyv, wd_ref[...], NT)
        dug = da * uv * (sg * (1.0 + ug * (1.0 - sg)))
        duv = da * sil
        dpg = _conv_taps_t(dug, jnp.where(r > 0, nxt_ref[0], 0.0), cwg)
        dpv = _conv_taps_t(duv, jnp.where(r > 0, nxt_ref[1], 0.0), cwv)
        nxt_ref[0] = dug[0:SUB, :]
        nxt_ref[1] = duv[0:SUB, :]
        dpgb, dpvb = dpg.astype(BF16), dpv.astype(BF16)
        hv = h_ref[...]
        dwd = _dot(act, dyv, TN)
        dpb = jnp.concatenate([dpgb, dpvb], axis=1)
        dwu = _dot(dpb, hv, TN)
        dh = _dot(dpb, wu_ref[...].reshape(2 * TF, D))

        def taps(du, x0, x1, x2):
            return (jnp.sum(du * x2, axis=0, keepdims=True), jnp.sum(du * x1, axis=0, keepdims=True),
                    jnp.sum(du * x0, axis=0, keepdims=True))

        tg, tv = taps(dug, pg, g1, g2), taps(duv, pv, v1, v2)

        @pl.when(r == 0)
        def _():
            awd_ref[...] = dwd
            awu_ref[...] = dwu
            dcg_ref[...] = jnp.zeros_like(dcg_ref)
            dcv_ref[...] = jnp.zeros_like(dcv_ref)

        @pl.when(r > 0)
        def _():
            awd_ref[...] += dwd
            awu_ref[...] += dwu

        @pl.when(r == nr - 1)
        def _():
            dwd_ref[...] = awd_ref[...].astype(BF16)
            dwu_ref[...] = awu_ref[...].astype(BF16).reshape(2, TF, D)

        for t in range(3):
            dcg_ref[t:t + 1, :] += tg[t]
            dcv_ref[t:t + 1, :] += tv[t]

        @pl.when(j == 0)
        def _():
            dh_ref[pl.ds(row0, tr), :] = dh

        @pl.when(j > 0)
        def _():
            dh_ref[pl.ds(row0, tr), :] += dh

    rows = lambda j, r: (nr - 1 - r, 0)
    tile = lambda j, r: (nr - 1 - r, j)
    halo = lambda j, r: (jnp.maximum((nr - 1 - r) * hb - 1, 0), j)
    return pl.pallas_call(
        body, name="ffn_bwd", grid=(NJ, nr),
        in_specs=[pl.BlockSpec((tr, D), rows), pl.BlockSpec((tr, D), rows),
                  pl.BlockSpec((tr, TF), tile), pl.BlockSpec((tr, TF), tile),
                  pl.BlockSpec((2 * SUB, TF), halo), pl.BlockSpec((2 * SUB, TF), halo),
                  pl.BlockSpec((2, TF, D), lambda j, r: (0, j, 0)),
                  pl.BlockSpec((3, TF), lambda j, r: (0, j)), pl.BlockSpec((3, TF), lambda j, r: (0, NJ + j)),
                  pl.BlockSpec((TF, D), lambda j, r: (j, 0))],
        out_specs=(pl.BlockSpec((s, D), lambda j, r: (0, 0)),
                   pl.BlockSpec((2, TF, D), lambda j, r: (0, j, 0)),
                   pl.BlockSpec((TF, D), lambda j, r: (j, 0)),
                   pl.BlockSpec((SUB, TF), lambda j, r: (0, j)), pl.BlockSpec((SUB, TF), lambda j, r: (0, j))),
        out_shape=(jax.ShapeDtypeStruct((s, D), F32),
                   jax.ShapeDtypeStruct((2, DFF, D), BF16), jax.ShapeDtypeStruct((DFF, D), BF16),
                   jax.ShapeDtypeStruct((SUB, DFF), F32), jax.ShapeDtypeStruct((SUB, DFF), F32)),
        scratch_shapes=[pltpu.VMEM((2, SUB, TF), F32), pltpu.VMEM((2 * TF, D), F32), pltpu.VMEM((TF, D), F32)],
        compiler_params=_params(("arbitrary", "arbitrary"), 56),
    )(dy, h2, pre_g, pre_v, pre_g, pre_v, wup_t, cw, cw, wd)


def _adam(w, g, m, v):
    m = ADAM_B1 * m + (1.0 - ADAM_B1) * g
    v = ADAM_B2 * v + (1.0 - ADAM_B2) * (g * g)
    m_hat = m / (1.0 - ADAM_B1 ** ADAM_STEP)
    v_hat = v / (1.0 - ADAM_B2 ** ADAM_STEP)
    delta = -ADAM_LR * (m_hat / (jnp.sqrt(v_hat) + ADAM_EPS) + ADAM_WD * w)
    return delta, m, v


NCHIP = NDEV // 2


def _pair_add(mine, theirs, tr, name):
    _, _, rws, cols = mine.shape

    def body(a_ref, b_ref, o_ref):
        c = lax.axis_index("c")
        o_ref[0] = (a_ref[0, c].astype(F32) + b_ref[0].astype(F32)).astype(BF16)

    (out,), _ = _call(
        body, (mine, theirs), name=name, grid=(NCHIP, rws // tr),
        in_specs=[pl.BlockSpec((1, 2, tr, cols), lambda q, i: (q, 0, i, 0)),
                  pl.BlockSpec((1, tr, cols), lambda q, i: (q, i, 0))],
        out_specs=[pl.BlockSpec((1, tr, cols), lambda q, i: (q, i, 0))],
        out_shape=[jax.ShapeDtypeStruct((NCHIP, rws, cols), BF16)], vmem_mb=16)
    return out


def _adamw_sharded(parts, w, m, v, tr, name, hosted=None):
    rws, cols = w.shape
    n_parts = parts.shape[0]

    def body(p_ref, w_ref, m_ref, v_ref, g_ref, d_ref, mo_ref, vo_ref):
        g = p_ref[0].astype(F32)
        for q in range(1, n_parts):
            g = g + p_ref[q].astype(F32)
        g_ref[...] = g
        d_ref[...], mo_ref[...], vo_ref[...] = _adam(w_ref[...], g, m_ref[...], v_ref[...])

    blk = pl.BlockSpec((tr, cols), lambda i: (i, 0))
    o = jax.ShapeDtypeStruct((rws, cols), F32)
    outs, moved = _call(
        body, (parts, w, m, v), name=name, grid=(rws // tr,),
        in_specs=[pl.BlockSpec((n_parts, tr, cols), lambda i: (0, i, 0)), blk, blk, blk],
        out_specs=[blk, blk, blk, blk], out_shape=[o, o, o, o], vmem_mb=44 if tr > 256 else 24, hosted=hosted)
    return (outs, moved) if hosted else outs


def _adamw_ada(c_all, dmod_my, w, m, v):
    rws, cols = w.shape
    tr = 256

    def body(c_ref, dm_ref, w_ref, m_ref, v_ref, g_ref, d_ref, mo_ref, vo_ref):
        cv = c_ref[...]
        act = cv * _sigmoid(cv)
        g = _dot(act, dm_ref[...], TN, lax.Precision.HIGHEST)
        g_ref[...] = g
        d_ref[...], mo_ref[...], vo_ref[...] = _adam(w_ref[...], g, m_ref[...], v_ref[...])

    blk = pl.BlockSpec((tr, cols), lambda i: (i, 0))
    o = jax.ShapeDtypeStruct((rws, cols), F32)
    return pl.pallas_call(
        body, name="adamw_ada", grid=(rws // tr,),
        in_specs=[pl.BlockSpec((NDEV, tr), lambda i: (0, i)), _full_spec((NDEV, cols)), blk, blk, blk],
        out_specs=(blk, blk, blk, blk), out_shape=(o, o, o, o),
        compiler_params=_params(("parallel",), 32),
    )(c_all, dmod_my, w, m, v)


REP_ROWS = 16
ROW_N1, ROW_N2, ROW_LOSS, ROW_MISC = 6, 7, 8, 9
LANE_BF, LANE_GQ, LANE_GK = 0, 128, 256


def _adamw_small(rep_all, conv_all, wmv):
    n_ff = wmv[6][0].shape[1]

    def body(*refs):
        rep_ref, conv_ref = refs[:2]
        ins = refs[2:2 + 24]
        outs = refs[2 + 24:]
        loss_ref, outs = outs[0], outs[1:]
        g_rep = rep_ref[0]
        g_conv = conv_ref[0]
        for d in range(1, NDEV):
            g_rep = g_rep + rep_ref[d]
            g_conv = g_conv + conv_ref[d]
        loss_ref[...] = (0.5 / D) * jnp.sum(g_rep[ROW_LOSS:ROW_LOSS + 1, :], axis=-1, keepdims=True)
        grads = [
            None,
            g_rep[ROW_N1:ROW_N1 + 1, :],
            g_rep[ROW_MISC:ROW_MISC + 1, LANE_BF:LANE_BF + HEADS],
            g_rep[ROW_MISC:ROW_MISC + 1, LANE_GQ:LANE_GQ + DH],
            g_rep[ROW_MISC:ROW_MISC + 1, LANE_GK:LANE_GK + DH],
            g_rep[ROW_N2:ROW_N2 + 1, :],
            g_conv[0:3, 0:n_ff],
            g_conv[0:3, n_ff:n_ff + DH],
        ]
        for p in range(8):
            w_ref, m_ref, v_ref = ins[3 * p:3 * p + 3]
            g_ref, d_ref, mo_ref, vo_ref = outs[4 * p:4 * p + 4]
            if p == 0:
                for nmod in range(NMOD):
                    sl = slice(D * nmod, D * (nmod + 1))
                    g = g_rep[nmod:nmod + 1, :]
                    g_ref[:, sl] = g
                    d_ref[:, sl], mo_ref[:, sl], vo_ref[:, sl] = _adam(w_ref[:, sl], g, m_ref[:, sl], v_ref[:, sl])
            else:
                g = grads[p]
                g_ref[...] = g
                d_ref[...], mo_ref[...], vo_ref[...] = _adam(w_ref[...], g, m_ref[...], v_ref[...])

    flat = [a for trio in wmv for a in trio]
    out_shape = [jax.ShapeDtypeStruct((1, 1), F32)]
    for trio in wmv:
        out_shape += [jax.ShapeDtypeStruct(trio[0].shape, F32)] * 4
    return pl.pallas_call(
        body, name="adamw_small", out_shape=tuple(out_shape),
        compiler_params=_params(None, 32),
    )(rep_all, conv_all, *flat)


FG_FIRST = 3 * AW
N_IN = DIN // NDEV


def _w_in_runs():
    runs = []
    for d in range(NDEV):
        lo, hi = N_IN * d, N_IN * (d + 1)
        for a, b, shift in ((0, FG_FIRST, 0), (FG_FIRST, FG_FIRST + HEADS, DIN - HEADS - FG_FIRST),
                            (FG_FIRST + HEADS, DIN, -HEADS)):
            a, b = max(a, lo), min(b, hi)
            if a < b:
                runs.append((d, a - lo, a + shift, b - a))
    return runs


W_IN_ROWS = 256
N_IN_PAD = 512


def _identity(n):
    return (lax.broadcasted_iota(jnp.int32, (n, n), 0) == lax.broadcasted_iota(jnp.int32, (n, n), 1)).astype(BF16)


def _assemble_w_in(g_in, hosted):
    def body(g_ref, o_ref, t_ref):
        eye = _identity(W_IN_ROWS)
        shard = None
        for d, src, dst, width in _w_in_runs():
            if d != shard:
                t_ref[:, 0:N_IN] = _dot(eye, g_ref[d], NT).astype(BF16)
                shard = d
            o_ref[:, dst:dst + width] = t_ref[:, src:src + width]
        o_ref[:, DIN:DINP] = jnp.zeros((W_IN_ROWS, DINP - DIN), o_ref.dtype)

    (out,), moved = _call(
        body, (g_in,), name="assemble_w_in", grid=(D // W_IN_ROWS,),
        in_specs=[pl.BlockSpec((NDEV, N_IN, W_IN_ROWS), lambda i: (0, 0, i))],
        out_specs=[pl.BlockSpec((W_IN_ROWS, DINP), lambda i: (i, 0))],
        out_shape=[jax.ShapeDtypeStruct((D, DINP), g_in.dtype)],
        scratch_shapes=[pltpu.VMEM((W_IN_ROWS, N_IN_PAD), BF16)], vmem_mb=16, hosted=hosted)
    return out, moved


def _scatter_dw_in(dwp):
    def body(w_ref, o_ref, t_ref):
        eye = _identity(W_IN_ROWS)
        runs = _w_in_runs()
        for i, (d, src, dst, width) in enumerate(runs):
            t_ref[:, src:src + width] = w_ref[:, dst:dst + width]
            if i + 1 == len(runs) or runs[i + 1][0] != d:
                o_ref[d // 2, d % 2] = _dot(t_ref[:, 0:N_IN], eye, TN).astype(BF16)

    (out,), _ = _call(
        body, (dwp,), name="scatter_dw_in", grid=(D // W_IN_ROWS,),
        in_specs=[pl.BlockSpec((W_IN_ROWS, DINP), lambda i: (i, 0))],
        out_specs=[pl.BlockSpec((NCHIP, 2, N_IN, W_IN_ROWS), lambda i: (0, 0, 0, i))],
        out_shape=[jax.ShapeDtypeStruct((NCHIP, 2, N_IN, D), dwp.dtype)],
        scratch_shapes=[pltpu.VMEM((W_IN_ROWS, N_IN_PAD), BF16)], vmem_mb=16)
    return out


def kernel(x, c, w_ada, b_ada, norm1_g, w_in, b_forget, q_norm_g, k_norm_g, conv_mix_w, w_out, norm2_g, w_up, ffn_conv_w, w_down, loss_target, m_w_ada, m_b_ada, m_norm1_g, m_w_in, m_b_forget, m_q_norm_g, m_k_norm_g, m_conv_mix_w, m_w_out, m_norm2_g, m_w_up, m_ffn_conv_w, m_w_down, v_w_ada, v_b_ada, v_norm1_g, v_w_in, v_b_forget, v_q_norm_g, v_k_norm_g, v_conv_mix_w, v_w_out, v_norm2_g, v_w_up, v_ffn_conv_w, v_w_down):
    me = 4 * lax.axis_index("x") + 2 * lax.axis_index("y") + lax.axis_index("c")
    xs, tgt = x[0], loss_target[0]
    n_ada = w_ada.shape[2]
    n_ff = w_up.shape[2]

    conv_w = jnp.concatenate([ffn_conv_w[0], conv_mix_w[0]], axis=1)
    conv_w = jnp.concatenate([conv_w, jnp.zeros((SUB - 3, conv_w.shape[1]), F32)], axis=0)
    c_all, conv_all, g_in = _exchange(
        [(c.reshape(SUB, D // SUB), "ag"), (conv_w, "ag"), (jnp.transpose(w_in[0]).astype(BF16), "ag2")],
        "exchange_w_in")
    g_in, w_out_b, w_up_b, w_down_b = lax.optimization_barrier(
        (g_in, w_out[0].astype(BF16), jnp.transpose(w_up[0]).astype(BF16), w_down[0].astype(BF16)))
    g_out, g_up, g_down = _sequencer_exchange(
        [(w_out_b, "ag2"), (w_up_b, "ag2"), (w_down_b, "ag2")], "gather_weights", collective_id=1)
    c_all = c_all.reshape(NDEV, D)
    cw_ffn = jnp.transpose(conv_all[:, :3, :n_ff], (1, 0, 2)).reshape(3, 2 * DFF)
    cw_mix = jnp.transpose(conv_all[:, :3, n_ff:], (1, 0, 2)).reshape(3, CW)

    b_my = lax.dynamic_slice(b_ada, (0, me * n_ada), (1, n_ada))
    mod_part = _ada_fwd(c_all, w_ada[0], b_my)
    w_in_p, (mod_rows,) = _assemble_w_in(
        g_in, [(jnp.broadcast_to(mod_part[:, None, :], (NDEV, SUB, n_ada)), "a2a")])
    mod = mod_rows[:, 0, :].reshape(NMOD, D)
    mod = jnp.concatenate([mod, jnp.zeros((SUB - NMOD, D), F32)], axis=0)

    h = _norm_mod_fwd(xs, mod, norm1_g)
    proj = _mm(h, w_in_p, "nn", F32, 1024, 640, "proj_fwd")
    bf_pad = jnp.concatenate([b_forget, jnp.zeros((1, LANES - HEADS), F32)], axis=1)
    qp, kp, vp = _qkv_prep(proj, bf_pad, q_norm_g, k_norm_g)
    attn, lse = _attn_fwd(qp, kp, vp)
    w_out_f = g_out.reshape(D, D)
    w_up_t = g_up.reshape(2, DFF, D)
    w_down_f = g_down.reshape(DFF, D)
    conv = _mixconv_fwd(proj, cw_mix)
    mixed = jnp.concatenate([attn, conv], axis=1).astype(BF16)
    z = _mm(mixed, w_out_f, "nn", F32, 1024, 1024, "out_fwd")
    x1, h2 = _resid_norm2(xs, z, mod, norm2_g)
    pre_g, pre_v, y = _ffn_fwd(h2, w_up_t, cw_ffn, w_down_f)
    dout, dy, vec_l = _loss_head(x1, y, tgt, mod)

    dh2, dwup_t, dwd, dcw_g, dcw_v = _ffn_bwd(dy, h2, pre_g, pre_v, w_up_t, cw_ffn, w_down_f)
    s_down = dwd.reshape(NCHIP, 2, DFF // NDEV, D)
    s_up = dwup_t.reshape(NCHIP, 2, n_ff, D)
    dx1, dz, vec_2, (t_up, t_down) = _norm_mod_bwd(dh2, x1, dout, z, mod, norm2_g, 4, 2, "norm2_bwd",
                                                   hosted=[(s_up, "pair"), (s_down, "pair")])
    dwout = _mm(mixed, dz, "tn", BF16, 1024, 1024, "out_bwd_w")
    s_out = dwout.reshape(NCHIP, 2, D // NDEV, D)
    dmixed, (t_out,) = _mm(dz, w_out_f, "nt", F32, 1024, 1024, "out_bwd_x", hosted=[(s_out, "pair")])
    c_out = _pair_add(s_out, t_out, 128, "pair_add_out")
    c_up = _pair_add(s_up, t_up, 176, "pair_add_up")
    c_down = _pair_add(s_down, t_down, 176, "pair_add_down")
    dxin, dbg, dcg, dcw_mix = _mixconv_bwd(dmixed, proj, cw_mix)
    (dqp, dkp, dvp), (p_up, p_down, p_out) = _attn_bwd(
        qp, kp, vp, dmixed, attn, lse, [(c_up, "chips"), (c_down, "chips"), (c_out, "chips")])
    dq, dk, dvb, dfg, vec_qk = _qkv_post(dqp, dkp, dvp, proj, bf_pad, q_norm_g, k_norm_g)
    dproj = jnp.concatenate([dq, dk, dvb, dxin, dbg, dcg, dfg], axis=1)
    dwin_p = _mm(h, dproj, "tn", BF16, 1024, 640, "proj_bwd_w")
    s_in = _scatter_dw_in(dwin_p).reshape(NDEV, N_IN, D)
    (p_in,) = _sequencer_exchange([(s_in, "a2a")], "scatter_dw_in_partials", collective_id=2, all_peers=True)
    dh = _mm(dproj, w_in_p, "nt", F32, 1024, 512, "proj_bwd_x", vmem_mb=36)
    grad_x, vec_1 = _norm_mod_bwd(dh, xs, dx1, None, mod, norm1_g, 1, None, "norm1_bwd")

    gap = lambda n: jnp.zeros((1, n), F32)
    misc = jnp.concatenate([
        vec_qk[2:3, :HEADS], gap(LANE_GQ - LANE_BF - HEADS), vec_qk[0:1, :DH], gap(LANE_GK - LANE_GQ - DH),
        vec_qk[1:2, :DH], gap(D - LANE_GK - DH)], axis=1)
    rep = jnp.concatenate([
        vec_1[0:1], vec_1[1:2], vec_2[3:4], vec_2[0:1], vec_2[1:2], vec_l[0:1],
        vec_1[2:3], vec_2[2:3], vec_l[1:2], misc, jnp.zeros((REP_ROWS - 10, D), F32)], axis=0)
    dcw_ffn = jnp.concatenate([dcw_g, dcw_v], axis=1).reshape(SUB, NDEV, n_ff)
    dcw_all = jnp.concatenate([jnp.transpose(dcw_ffn, (1, 0, 2)),
                               jnp.transpose(dcw_mix.reshape(SUB, NDEV, DH), (1, 0, 2))], axis=2)
    r_up = _adamw_sharded(p_up, jnp.transpose(w_up[0]), jnp.transpose(m_w_up[0]), jnp.transpose(v_w_up[0]), 176,
                          "adamw_up")
    r_down = _adamw_sharded(p_down, w_down[0], m_w_down[0], v_w_down[0], 176, "adamw_down")
    rep, dcw_all, r_up, r_down = lax.optimization_barrier((rep, dcw_all, r_up, r_down))
    r_up = tuple(jnp.transpose(a) for a in r_up)
    r_out, (rep_all, conv_parts) = _adamw_sharded(p_out, w_out[0], m_w_out[0], v_w_out[0], 128, "adamw_out",
                                                  hosted=[(rep, "ag"), (dcw_all, "a2a")])
    dmod_my = lax.dynamic_slice(rep_all[:, :NMOD, :].reshape(NDEV, NMOD * D), (0, me * n_ada), (NDEV, n_ada))
    r_ada = _adamw_ada(c_all, dmod_my, w_ada[0], m_w_ada[0], v_w_ada[0])
    r_in = _adamw_sharded(p_in, jnp.transpose(w_in[0]), jnp.transpose(m_w_in[0]), jnp.transpose(v_w_in[0]), N_IN,
                          "adamw_in")
    r_in = tuple(jnp.transpose(a) for a in r_in)
    small = _adamw_small(rep_all, conv_parts, [
        [b_ada, m_b_ada, v_b_ada], [norm1_g, m_norm1_g, v_norm1_g], [b_forget, m_b_forget, v_b_forget],
        [q_norm_g, m_q_norm_g, v_q_norm_g], [k_norm_g, m_k_norm_g, v_k_norm_g], [norm2_g, m_norm2_g, v_norm2_g],
        [ffn_conv_w[0], m_ffn_conv_w[0], v_ffn_conv_w[0]], [conv_mix_w[0], m_conv_mix_w[0], v_conv_mix_w[0]]])
    loss = small[0].reshape(())
    r_bada, r_n1, r_bf, r_gq, r_gk, r_n2, r_cf, r_cm = [small[1 + 4 * p:5 + 4 * p] for p in range(8)]
    lead = lambda t: tuple(a[None] for a in t)
    per_w = [lead(r_ada), r_bada, r_n1, lead(r_in), r_bf, r_gq, r_gk, lead(r_cm), lead(r_out), r_n2,
             lead(r_up), lead(r_cf), lead(r_down)]
    outs = [loss, grad_x[None]]
    for field in range(4):
        outs += [t[field] for t in per_w]
    return tuple(outs)
```

```python
import functools

import jax
import jax.numpy as jnp
import numpy as np
from jax import lax
from jax.experimental import pallas as pl
from jax.experimental.pallas import tpu as pltpu
from jax.experimental.pallas import tpu_sc as plsc

F32 = jnp.float32
BF16 = jnp.bfloat16

NDEV = 8
D = 1024
HEADS = 8
DH = 64
AW = 512
CW = 512
DFF = 2816
DIN = 3080
DINP = 3200
NMOD = 6
EPS = 1e-6
QK_SCALE = 0.125
LANES = 128
SUB = 8

ADAM_LR = 0.001
ADAM_B1 = 0.9
ADAM_B2 = 0.999
ADAM_EPS = 1e-08
ADAM_WD = 0.01
ADAM_STEP = 10

MESH = pl.DeviceIdType.MESH
ANY = pl.BlockSpec(memory_space=pl.ANY)

NN = (((1,), (0,)), ((), ()))
NT = (((1,), (1,)), ((), ()))
TN = (((0,), (0,)), ((), ()))


def _dot(a, b, dims=NN, precision=None):
    return lax.dot_general(a, b, dims, precision=precision, preferred_element_type=F32)


def _params(sem=None, vmem_mb=None):
    kw = {}
    if sem is not None:
        kw["dimension_semantics"] = sem
    if vmem_mb is not None:
        kw["vmem_limit_bytes"] = vmem_mb * 1024 * 1024
    return pltpu.CompilerParams(**kw)


def _sigmoid(x):
    return 0.5 * jnp.tanh(0.5 * x) + 0.5


class _Exchange:
    def __init__(self, items):
        self.arrays = [pltpu.with_memory_space_constraint(a, pltpu.HBM) for a, _ in items]
        self.modes = [m for _, m in items]
        self.n = len(items)
        self.out_shape = []
        for a, m in items:
            sh = {"ag": (NDEV,) + a.shape, "ag2": (NDEV,) + a.shape, "pair": a.shape[:1] + a.shape[2:]}.get(m, a.shape)
            self.out_shape.append(jax.ShapeDtypeStruct(sh, a.dtype))
        self.scratch = [pltpu.SemaphoreType.DMA((self.n, NDEV - 1)), pltpu.SemaphoreType.DMA((self.n, NDEV - 1)),
                        pltpu.SemaphoreType.DMA((self.n,))]

    def _plan(self, srcs, outs, sems):
        send_sems, recv_sems, loc_sems = sems
        x, y, c = lax.axis_index("x"), lax.axis_index("y"), lax.axis_index("c")
        me, my_chip = 4 * x + 2 * y + c, 2 * x + y
        sib = (x, y, 1 - c)
        local, first, landed, forwards, arrivals = [], [], [], [], []

        def remote(a, k, src, dst, to):
            return pltpu.make_async_remote_copy(src_ref=src, dst_ref=dst, send_sem=send_sems.at[a, k],
                                                recv_sem=recv_sems.at[a, k], device_id=to, device_id_type=MESH)

        for a, mode in enumerate(self.modes):
            src, out = srcs[a], outs[a]
            if mode in ("ag", "a2a"):
                piece = (lambda slot, src=src: src) if mode == "ag" else (lambda slot, src=src: src.at[slot])
                local.append(pltpu.make_async_copy(piece(me), out.at[me], loc_sems.at[a]))
                for r in range(1, NDEV):
                    px = 1 - x if (r >> 2) & 1 else x
                    py = 1 - y if (r >> 1) & 1 else y
                    pc = 1 - c if r & 1 else c
                    pidx = 4 * px + 2 * py + pc
                    first.append(remote(a, r - 1, piece(pidx), out.at[me], (px, py, pc)))
                    arrivals.append(remote(a, r - 1, piece(pidx), out.at[pidx], (px, py, pc)))
            elif mode == "ag2":
                local.append(pltpu.make_async_copy(src, out.at[me], loc_sems.at[a]))
                first.append(remote(a, 0, src, out.at[me], sib))
                arrivals.append(remote(a, 0, src, out.at[me + 1 - 2 * c], sib))
                for j, (px, py) in enumerate([(1 - x, y), (x, 1 - y), (1 - x, 1 - y)]):
                    theirs = out.at[4 * px + 2 * py + c]
                    first.append(remote(a, 1 + j, src, out.at[me], (px, py, c)))
                    landed.append(remote(a, 1 + j, src, theirs, (px, py, c)))
                    forwards.append(remote(a, 4 + j, theirs, theirs, sib))
                    arrivals.append(remote(a, 4 + j, src, out.at[4 * px + 2 * py + 1 - c], sib))
            elif mode == "pair":
                for q in range(NDEV // 2):
                    first.append(remote(a, q, src.at[q, 1 - c], out.at[q], sib))
                    arrivals.append(remote(a, q, src.at[q, 1 - c], out.at[q], sib))
            else:
                assert mode == "chips", mode
                local.append(pltpu.make_async_copy(src.at[my_chip], out.at[my_chip], loc_sems.at[a]))
                for j, (px, py) in enumerate([(1 - x, y), (x, 1 - y), (1 - x, 1 - y)]):
                    q = 2 * px + py
                    first.append(remote(a, 1 + j, src.at[q], out.at[my_chip], (px, py, c)))
                    arrivals.append(remote(a, 1 + j, src.at[q], out.at[q], (px, py, c)))
        return local, first, landed, forwards, arrivals

    def start(self, srcs, outs, sems):
        local, first, _, _, _ = self._plan(srcs, outs, sems)
        for cp in local + first:
            cp.start()

    def wait(self, srcs, outs, sems):
        local, first, landed, forwards, arrivals = self._plan(srcs, outs, sems)
        for cp, fwd in zip(landed, forwards):
            cp.wait_recv()
            fwd.start()
        for cp in arrivals:
            cp.wait_recv()
        for cp in first + forwards:
            cp.wait_send()
        for cp in local:
            cp.wait()


def _exchange(items, name):
    ex = _Exchange(items)
    n = ex.n

    def body(*refs):
        srcs, outs, sems = refs[:n], refs[n:2 * n], refs[2 * n:]
        ex.start(srcs, outs, sems)
        ex.wait(srcs, outs, sems)

    return pl.pallas_call(
        body, name=name,
        out_shape=tuple(ex.out_shape),
        in_specs=[ANY] * n, out_specs=tuple([ANY] * n),
        scratch_shapes=ex.scratch,
        compiler_params=pltpu.CompilerParams(has_side_effects=True),
    )(*ex.arrays)


def _sequencer_exchange(items, name, collective_id, all_peers=False):
    ex = _Exchange(items)
    srcs = [jax.new_ref(a, memory_space=pltpu.MemorySpace.HBM) for a in ex.arrays]
    outs = [jax.empty_ref(sh, memory_space=pltpu.MemorySpace.HBM) for sh in ex.out_shape]

    @pl.kernel(mesh=plsc.ScalarSubcoreMesh(axis_name="sequencer", num_cores=1), name=name,
               scratch_types=tuple(ex.scratch), compiler_params=pltpu.CompilerParams(collective_id=collective_id))
    def launch(send_sems, recv_sems, loc_sems):
        x, y, c = lax.axis_index("x"), lax.axis_index("y"), lax.axis_index("c")
        barrier = pltpu.get_barrier_semaphore()
        peers = [(x, y, 1 - c), (1 - x, y, c), (x, 1 - y, c), (1 - x, 1 - y, c)]
        if all_peers:
            peers += [(1 - x, y, 1 - c), (x, 1 - y, 1 - c), (1 - x, 1 - y, 1 - c)]
        for peer in peers:
            pl.semaphore_signal(barrier, inc=1, device_id=peer, device_id_type=MESH)
        pl.semaphore_wait(barrier, len(peers))
        sems = (send_sems, recv_sems, loc_sems)
        ex.start(srcs, outs, sems)
        ex.wait(srcs, outs, sems)

    launch()
    return [o[...] for o in outs]


def _call(body, inputs, *, name, grid, in_specs, out_specs, out_shape, scratch_shapes=(), vmem_mb=None, hosted=None):
    out_specs, out_shape, scratch_shapes = tuple(out_specs), tuple(out_shape), list(scratch_shapes)
    if not hosted:
        res = pl.pallas_call(
            body, name=name, grid=grid, in_specs=list(in_specs), out_specs=out_specs, out_shape=out_shape,
            scratch_shapes=scratch_shapes, compiler_params=_params(("arbitrary",) * len(grid), vmem_mb),
        )(*inputs)
        return tuple(res), ()
    ex = _Exchange(hosted)
    n, n_in, n_out, n_scr = ex.n, len(inputs), len(out_shape), len(scratch_shapes)

    def hosting_body(*refs):
        ins, srcs = refs[:n_in], refs[n_in:n_in + n]
        outs, landing = refs[n_in + n:n_in + n + n_out], refs[n_in + n + n_out:n_in + 2 * n + n_out]
        scratch, sems = refs[n_in + 2 * n + n_out:n_in + 2 * n + n_out + n_scr], refs[n_in + 2 * n + n_out + n_scr:]
        first = functools.reduce(jnp.logical_and, [pl.program_id(d) == 0 for d in range(len(grid))])
        last = functools.reduce(jnp.logical_and, [pl.program_id(d) == grid[d] - 1 for d in range(len(grid))])

        @pl.when(first)
        def _():
            ex.start(srcs, landing, sems)

        body(*ins, *outs, *scratch)

        @pl.when(last)
        def _():
            ex.wait(srcs, landing, sems)

    res = pl.pallas_call(
        hosting_body, name=name, grid=grid,
        in_specs=list(in_specs) + [ANY] * n, out_specs=out_specs + tuple([ANY] * n),
        out_shape=out_shape + tuple(ex.out_shape), scratch_shapes=scratch_shapes + ex.scratch,
        compiler_params=_params(("arbitrary",) * len(grid), vmem_mb),
    )(*inputs, *ex.arrays)
    return tuple(res[:n_out]), tuple(res[n_out:])


def _mm(a, b, mode, out_dtype, tm, tn, name, hosted=None, vmem_mb=24):
    if mode == "nn":
        (m, k), n = a.shape, b.shape[1]
        a_spec = pl.BlockSpec((tm, k), lambda i, j: (i, 0))
        b_spec = pl.BlockSpec((k, tn), lambda i, j: (0, j))
        dims = NN
    elif mode == "nt":
        (m, k), n = a.shape, b.shape[0]
        a_spec = pl.BlockSpec((tm, k), lambda i, j: (i, 0))
        b_spec = pl.BlockSpec((tn, k), lambda i, j: (j, 0))
        dims = NT
    else:
        (k, m), n = a.shape, b.shape[1]
        a_spec = pl.BlockSpec((k, tm), lambda i, j: (0, i))
        b_spec = pl.BlockSpec((k, tn), lambda i, j: (0, j))
        dims = TN
    assert m % tm == 0 and n % tn == 0, (m, n, tm, tn)

    def body(a_ref, b_ref, o_ref):
        o_ref[...] = _dot(a_ref[...], b_ref[...], dims).astype(o_ref.dtype)

    (out,), moved = _call(
        body, (a, b), name=name, grid=(m // tm, n // tn),
        in_specs=[a_spec, b_spec], out_specs=[pl.BlockSpec((tm, tn), lambda i, j: (i, j))],
        out_shape=[jax.ShapeDtypeStruct((m, n), out_dtype)], vmem_mb=vmem_mb, hosted=hosted)
    return (out, moved) if hosted else out


def _shift_down(x, k, fill):
    y = pltpu.roll(x, k, 0)
    row = lax.broadcasted_iota(jnp.int32, (SUB, x.shape[1]), 0)
    head = y[0:SUB, :]
    for t in range(k):
        head = jnp.where(row == t, fill[t], head)
    return jnp.concatenate([head, y[SUB:, :]], axis=0)


def _shift_up(x, k, fill):
    n = x.shape[0]
    y = pltpu.roll(x, n - k, 0)
    row = lax.broadcasted_iota(jnp.int32, (SUB, x.shape[1]), 0)
    tail = y[n - SUB:, :]
    for t in range(k):
        tail = jnp.where(row == SUB - k + t, fill[t], tail)
    return jnp.concatenate([y[:n - SUB, :], tail], axis=0)


def _conv_taps(x, halo, w):
    if halo is None:
        f1, f2 = [0.0], [0.0, 0.0]
    else:
        f1, f2 = [halo[7:8, :]], [halo[6:7, :], halo[7:8, :]]
    s1 = _shift_down(x, 1, f1)
    s2 = _shift_down(x, 2, f2)
    u = w[2:3, :] * x + w[1:2, :] * s1 + w[0:1, :] * s2
    return u, s1, s2


def _conv_taps_t(du, nxt, w):
    if nxt is None:
        f1, f2 = [0.0], [0.0, 0.0]
    else:
        f1, f2 = [nxt[0:1, :]], [nxt[0:1, :], nxt[1:2, :]]
    return w[2:3, :] * du + w[1:2, :] * _shift_up(du, 1, f1) + w[0:1, :] * _shift_up(du, 2, f2)


def _ada_fwd(c_all, w_ada, b_my):
    def body(c_ref, w_ref, b_ref, o_ref):
        cv = c_ref[...]
        act = cv * _sigmoid(cv)
        o_ref[...] = _dot(act, w_ref[...], NN, lax.Precision.HIGHEST) + b_ref[...]

    return pl.pallas_call(
        body, name="ada_fwd",
        out_shape=jax.ShapeDtypeStruct((NDEV, w_ada.shape[1]), F32),
        compiler_params=_params(None, 32),
    )(c_all, w_ada, b_my)


TR = 256
TRE = 512


def _row_spec(width, col=0, rows=TR):
    return pl.BlockSpec((rows, width), lambda i, col=col: (i, col))


def _erow(width):
    return _row_spec(width, rows=TRE)


def _full_spec(shape):
    return pl.BlockSpec(shape, lambda i: (0,) * len(shape))


def _norm_proj_fwd(x, mod, g, w, tm, tn):
    s, n = x.shape[0], w.shape[1]

    def body(x_ref, mod_ref, g_ref, w_ref, h_ref, o_ref):
        @pl.when(pl.program_id(1) == 0)
        def _():
            xv = x_ref[...]
            r = lax.rsqrt(jnp.mean(xv * xv, axis=-1, keepdims=True) + EPS)
            nrm = xv * r * g_ref[...]
            h_ref[...] = (nrm * (1.0 + mod_ref[1:2, :]) + mod_ref[0:1, :]).astype(BF16)

        o_ref[...] = _dot(h_ref[...], w_ref[...])

    outs, _ = _call(
        body, (x, mod, g, w), name="proj_fwd", grid=(s // tm, n // tn),
        in_specs=[pl.BlockSpec((tm, D), lambda i, j: (i, 0)), pl.BlockSpec((SUB, D), lambda i, j: (0, 0)),
                  pl.BlockSpec((1, D), lambda i, j: (0, 0)), pl.BlockSpec((D, tn), lambda i, j: (0, j))],
        out_specs=[pl.BlockSpec((tm, D), lambda i, j: (i, 0)), pl.BlockSpec((tm, tn), lambda i, j: (i, j))],
        out_shape=[jax.ShapeDtypeStruct((s, D), BF16), jax.ShapeDtypeStruct((s, n), F32)], vmem_mb=28)
    return outs


SLAB = 2 * DH
AUG_F, AUG_ONE, AUG_LSE = 0, 3, 6


def _split3(x):
    hi = x.astype(BF16).astype(F32)
    r1 = x - hi
    mid = r1.astype(BF16).astype(F32)
    return hi, mid, r1 - mid


def _lanes3(lane, first, pieces, other):
    out = other
    for k in range(3):
        out = jnp.where(lane == first + k, pieces[k], out)
    return out


def _aug_placement():
    eq = np.zeros((3 * LANES, HEADS * SLAB), np.float32)
    ek = np.zeros((3 * LANES, HEADS * SLAB), np.float32)
    ones = np.zeros((SUB, HEADS * SLAB), np.float32)
    for h in range(HEADS):
        aug = SLAB * h + DH
        for k in range(3):
            eq[LANES * k + h, aug + AUG_F + k] = 1.0
            ek[LANES * k + h, aug + AUG_ONE + k] = -1.0
            ones[0, aug + AUG_ONE + k] = 1.0
            ones[1, aug + AUG_F + k] = ones[1, aug + AUG_LSE + k] = 1.0
            ones[2, aug + k] = 1.0
    return jnp.asarray(eq, BF16), jnp.asarray(ek, BF16), jnp.asarray(ones)


FG_BLOCK = (3 * AW + 3 * CW) // LANES


def _qkv_prep(proj, bf_pad, gq, gk):
    s = proj.shape[0]

    def body(q_ref, k_ref, v_ref, fg_ref, b_ref, gq_ref, gk_ref, eq_ref, ek_ref, ones_ref, qo_ref, ko_ref, vo_ref,
             carry_ref):
        @pl.when(pl.program_id(0) == 0)
        def _():
            carry_ref[...] = jnp.zeros_like(carry_ref)

        z = fg_ref[...] + b_ref[...]
        logf = jnp.minimum(z, 0.0) - jnp.log1p(jnp.exp(-jnp.abs(z)))
        row = lax.broadcasted_iota(jnp.int32, (TR, TR), 0)
        col = lax.broadcasted_iota(jnp.int32, (TR, TR), 1)
        fcum = _dot((col <= row).astype(F32), logf, NN, lax.Precision.HIGHEST) + carry_ref[0:1, :]
        carry_ref[...] = jnp.broadcast_to(fcum[TR - 1:TR, :], carry_ref.shape)
        f3 = jnp.concatenate(_split3(fcum), axis=1).astype(BF16)
        qo_ref[...] = (_dot(f3, eq_ref[...]) + ones_ref[0:1, :]).astype(BF16)
        ko_ref[...] = (_dot(f3, ek_ref[...]) + ones_ref[1:2, :]).astype(BF16)
        vo_ref[...] = jnp.broadcast_to(ones_ref[2:3, :], vo_ref.shape).astype(BF16)
        for h in range(HEADS):
            sl = slice(DH * h, DH * (h + 1))
            lo = slice(SLAB * h, SLAB * h + DH)
            qh = q_ref[:, sl]
            r = lax.rsqrt(jnp.mean(qh * qh, axis=-1, keepdims=True) + EPS)
            qo_ref[:, lo] = (qh * r * gq_ref[...] * QK_SCALE).astype(BF16)
            kh = k_ref[:, sl]
            r = lax.rsqrt(jnp.mean(kh * kh, axis=-1, keepdims=True) + EPS)
            ko_ref[:, lo] = (kh * r * gk_ref[...]).astype(BF16)
            vo_ref[:, lo] = v_ref[:, sl].astype(BF16)

    eq, ek, ones = _aug_placement()
    o = jax.ShapeDtypeStruct((s, HEADS * SLAB), BF16)
    wide = _row_spec(HEADS * SLAB)
    outs, _ = _call(
        body, (proj, proj, proj, proj, bf_pad, gq, gk, eq, ek, ones), name="qkv_prep", grid=(s // TR,),
        in_specs=[_row_spec(AW, 0), _row_spec(AW, 1), _row_spec(AW, 2), _row_spec(LANES, FG_BLOCK),
                  _full_spec((1, LANES)), _full_spec((1, DH)), _full_spec((1, DH)), _full_spec(eq.shape),
                  _full_spec(ek.shape), _full_spec(ones.shape)],
        out_specs=[wide, wide, wide], out_shape=[o, o, o],
        scratch_shapes=[pltpu.VMEM((SUB, LANES), F32)], vmem_mb=16)
    return outs


def _resid_norm2(x, z, mod, g):
    s = x.shape[0]

    def body(x_ref, z_ref, mod_ref, g_ref, x1_ref, h_ref):
        x1 = x_ref[...] + mod_ref[2:3, :] * z_ref[...]
        x1_ref[...] = x1
        r = lax.rsqrt(jnp.mean(x1 * x1, axis=-1, keepdims=True) + EPS)
        nrm = x1 * r * g_ref[...]
        h_ref[...] = (nrm * (1.0 + mod_ref[4:5, :]) + mod_ref[3:4, :]).astype(BF16)

    return pl.pallas_call(
        body, name="resid_norm2", grid=(s // TRE,),
        in_specs=[_erow(D), _erow(D), _full_spec((SUB, D)), _full_spec((1, D))],
        out_specs=(_erow(D), _erow(D)),
        out_shape=(jax.ShapeDtypeStruct((s, D), F32), jax.ShapeDtypeStruct((s, D), BF16)),
        compiler_params=_params(("parallel",), 24),
    )(x, z, mod, g)


def _loss_head(x1, y, tgt, mod):
    s = x1.shape[0]

    def body(x1_ref, y_ref, t_ref, mod_ref, dout_ref, dy_ref, vec_ref):
        @pl.when(pl.program_id(0) == 0)
        def _():
            vec_ref[...] = jnp.zeros_like(vec_ref)

        yv = y_ref[...]
        g2 = mod_ref[5:6, :]
        diff = x1_ref[...] + g2 * yv - t_ref[...]
        dout = diff * (1.0 / D)
        dout_ref[...] = dout
        dy_ref[...] = (g2 * dout).astype(BF16)
        vec_ref[0:1, :] += jnp.sum(dout * yv, axis=0, keepdims=True)
        vec_ref[1:2, :] += jnp.sum(diff * diff, axis=0, keepdims=True)

    return pl.pallas_call(
        body, name="loss_head", grid=(s // TRE,),
        in_specs=[_erow(D), _erow(D), _erow(D), _full_spec((SUB, D))],
        out_specs=(_erow(D), _erow(D), _full_spec((SUB, D))),
        out_shape=(jax.ShapeDtypeStruct((s, D), F32), jax.ShapeDtypeStruct((s, D), BF16),
                   jax.ShapeDtypeStruct((SUB, D), F32)),
        compiler_params=_params(("arbitrary",), 24),
    )(x1, y, tgt, mod)


def _norm_mod_bwd(dh, xin, dres, zin, mod, g, scale_row, gate_row, name, hosted=None):
    s = dh.shape[0]
    with_gate = gate_row is not None

    def body(*refs):
        if with_gate:
            dh_ref, x_ref, dres_ref, z_ref, mod_ref, g_ref, dx_ref, dz_ref, vec_ref = refs
        else:
            dh_ref, x_ref, dres_ref, mod_ref, g_ref, dx_ref, vec_ref = refs

        @pl.when(pl.program_id(0) == 0)
        def _():
            vec_ref[...] = jnp.zeros_like(vec_ref)

        xv = x_ref[...]
        dhv = dh_ref[...]
        gv = g_ref[...]
        r = lax.rsqrt(jnp.mean(xv * xv, axis=-1, keepdims=True) + EPS)
        xh = xv * r
        dn = dhv * (1.0 + mod_ref[scale_row:scale_row + 1, :])
        dxh = dn * gv
        dx = dres_ref[...] + r * (dxh - xh * jnp.mean(dxh * xh, axis=-1, keepdims=True))
        dx_ref[...] = dx
        vec_ref[0:1, :] += jnp.sum(dhv, axis=0, keepdims=True)
        vec_ref[1:2, :] += jnp.sum(dhv * (xh * gv), axis=0, keepdims=True)
        vec_ref[2:3, :] += jnp.sum(dn * xh, axis=0, keepdims=True)
        if with_gate:
            dz_ref[...] = (mod_ref[gate_row:gate_row + 1, :] * dx).astype(BF16)
            vec_ref[3:4, :] += jnp.sum(dx * z_ref[...], axis=0, keepdims=True)

    ins = [dh, xin, dres] + ([zin] if with_gate else []) + [mod, g]
    in_specs = [_erow(D)] * (4 if with_gate else 3) + [_full_spec((SUB, D)), _full_spec((1, D))]
    out_specs = [_erow(D)] + ([_erow(D)] if with_gate else []) + [_full_spec((SUB, D))]
    out_shape = [jax.ShapeDtypeStruct((s, D), F32)] + ([jax.ShapeDtypeStruct((s, D), BF16)] if with_gate else []) \
        + [jax.ShapeDtypeStruct((SUB, D), F32)]
    outs, moved = _call(body, ins, name=name, grid=(s // TRE,), in_specs=in_specs, out_specs=out_specs,
                        out_shape=out_shape, vmem_mb=32, hosted=hosted)
    return outs + (moved,) if hosted else outs


XIN_BLOCK = 3 * AW // LANES
BG_BLOCK = XIN_BLOCK + CW // LANES
CG_BLOCK = BG_BLOCK + CW // LANES


def _seq_spec(s, first_block):
    return pl.BlockSpec((s, LANES), lambda j, fb=first_block: (0, fb + j))


def _mixconv_fwd(proj, w):
    s = proj.shape[0]

    def body(xin_ref, bg_ref, cg_ref, w_ref, o_ref):
        cx = cg_ref[...] * xin_ref[...]
        cv, _, _ = _conv_taps(cx, None, w_ref[...])
        o_ref[...] = bg_ref[...] * cv

    return pl.pallas_call(
        body, name="mixconv_fwd", grid=(CW // LANES,),
        in_specs=[_seq_spec(s, XIN_BLOCK), _seq_spec(s, BG_BLOCK), _seq_spec(s, CG_BLOCK),
                  pl.BlockSpec((3, LANES), lambda j: (0, j))],
        out_specs=_seq_spec(s, 0), out_shape=jax.ShapeDtypeStruct((s, CW), F32),
        compiler_params=_params(("parallel",), 32),
    )(proj, proj, proj, w)


def _mixconv_bwd(dmixed, proj, w):
    s = proj.shape[0]

    def body(d_ref, xin_ref, bg_ref, cg_ref, w_ref, dxin_ref, dbg_ref, dcg_ref, dw_ref):
        wv = w_ref[...]
        xin, cg, dconv = xin_ref[...], cg_ref[...], d_ref[...]
        cx = cg * xin
        cv, s1, s2 = _conv_taps(cx, None, wv)
        dbg_ref[...] = (dconv * cv).astype(BF16)
        dcv = dconv * bg_ref[...]
        dw_ref[...] = jnp.zeros_like(dw_ref)
        dw_ref[0:1, :] = jnp.sum(dcv * s2, axis=0, keepdims=True)
        dw_ref[1:2, :] = jnp.sum(dcv * s1, axis=0, keepdims=True)
        dw_ref[2:3, :] = jnp.sum(dcv * cx, axis=0, keepdims=True)
        dcx = _conv_taps_t(dcv, None, wv)
        dcg_ref[...] = (dcx * xin).astype(BF16)
        dxin_ref[...] = (dcx * cg).astype(BF16)

    o = jax.ShapeDtypeStruct((s, CW), BF16)
    return pl.pallas_call(
        body, name="mixconv_bwd", grid=(CW // LANES,),
        in_specs=[_seq_spec(s, AW // LANES), _seq_spec(s, XIN_BLOCK), _seq_spec(s, BG_BLOCK), _seq_spec(s, CG_BLOCK),
                  pl.BlockSpec((3, LANES), lambda j: (0, j))],
        out_specs=(_seq_spec(s, 0), _seq_spec(s, 0), _seq_spec(s, 0), pl.BlockSpec((SUB, LANES), lambda j: (0, j))),
        out_shape=(o, o, o, jax.ShapeDtypeStruct((SUB, CW), F32)),
        compiler_params=_params(("parallel",), 32),
    )(dmixed, proj, proj, proj, w)


TA = 512
NEG = -1e30


def _causal_mask():
    row = lax.broadcasted_iota(jnp.int32, (TA, TA), 0)
    col = lax.broadcasted_iota(jnp.int32, (TA, TA), 1)
    return col <= row


def _attn_fwd(qp, kp, vp):
    s = qp.shape[0]
    nq = s // TA

    def body(q_ref, k_ref, v_ref, o_ref, lse_ref):
        i = pl.program_id(1)
        slabs = [slice(SLAB * hh, SLAB * (hh + 1)) for hh in range(2)]
        q = [q_ref[:, sl] for sl in slabs]

        def block(j, carry, masked):
            keys = pl.ds(pl.multiple_of(j * TA, TA), TA)
            ms, acc = carry
            m_out, parts = [], []
            for hh in range(2):
                sc = _dot(q[hh], k_ref[keys, slabs[hh]], NT)
                if masked:
                    sc = jnp.where(_causal_mask(), sc, NEG)
                m_new = jnp.maximum(ms[hh], jnp.max(sc, axis=-1, keepdims=True))
                p = jnp.exp(sc - m_new)
                parts.append(jnp.exp(ms[hh] - m_new) * acc[:, slabs[hh]] + _dot(p.astype(BF16), v_ref[keys, slabs[hh]]))
                m_out.append(m_new)
            return tuple(m_out), jnp.concatenate(parts, axis=1)

        init = ((jnp.full((TA, 1), NEG, F32), jnp.full((TA, 1), NEG, F32)), jnp.zeros((TA, 2 * SLAB), F32))
        carry = lax.fori_loop(0, i, lambda j, cr: block(j, cr, False), init)
        ms, acc = block(i, carry, True)
        for hh in range(2):
            l = acc[:, SLAB * hh + DH:SLAB * hh + DH + 1]
            o_ref[:, DH * hh:DH * (hh + 1)] = acc[:, SLAB * hh:SLAB * hh + DH] / l
            lse_ref[0, :, hh:hh + 1] = ms[hh] + jnp.log(l)

    (o, lse), _ = _call(
        body, (qp, kp, vp), name="attn_fwd", grid=(HEADS // 2, nq),
        in_specs=[pl.BlockSpec((TA, 2 * SLAB), lambda p, i: (i, p)),
                  pl.BlockSpec((s, 2 * SLAB), lambda p, i: (0, p)),
                  pl.BlockSpec((s, 2 * SLAB), lambda p, i: (0, p))],
        out_specs=[pl.BlockSpec((TA, LANES), lambda p, i: (i, p)), pl.BlockSpec((1, TA, 2), lambda p, i: (p, i, 0))],
        out_shape=[jax.ShapeDtypeStruct((s, AW), F32), jax.ShapeDtypeStruct((HEADS // 2, s, 2), F32)],
        vmem_mb=24)
    return o, lse


def _attn_bwd(qp, kp, vp, dmixed, o, lse, hosted):
    s = qp.shape[0]
    nq = s // TA

    def body(q_ref, k_ref, v_ref, do_ref, o_ref, lse_ref, dq_ref, dk_ref, dv_ref, qb_ref, dob_ref):
        dk_ref[...] = jnp.zeros_like(dk_ref)
        dv_ref[...] = jnp.zeros_like(dv_ref)
        slabs = [slice(SLAB * hh, SLAB * (hh + 1)) for hh in range(2)]
        lane = lax.broadcasted_iota(jnp.int32, (TA, DH), 1)

        def q_block(i, _):
            i0 = pl.multiple_of(i * TA, TA)
            rows = pl.ds(i0, TA)
            for hh in range(2):
                half = slice(DH * hh, DH * (hh + 1))
                do = do_ref[rows, half]
                delta = jnp.sum(do * o_ref[rows, half], axis=-1, keepdims=True)
                dob_ref[hh, :, 0:DH] = do.astype(BF16)
                dob_ref[hh, :, DH:SLAB] = _lanes3(lane, 0, [-d for d in _split3(delta)], 0.0).astype(BF16)
                lse3 = _split3(lse_ref[0, rows, hh:hh + 1])
                qb_ref[hh, :, 0:DH] = q_ref[rows, SLAB * hh:SLAB * hh + DH]
                aug = q_ref[rows, SLAB * hh + DH:SLAB * (hh + 1)].astype(F32)
                qb_ref[hh, :, DH:SLAB] = _lanes3(lane, AUG_LSE, [-x for x in lse3], aug).astype(BF16)

            def block(j, dq, masked):
                keys = pl.ds(pl.multiple_of(j * TA, TA), TA)
                dv, dk, dqc = [], [], []
                for hh in range(2):
                    q, dob = qb_ref[hh], dob_ref[hh]
                    k = k_ref[keys, slabs[hh]]
                    sc = _dot(q, k, NT)
                    if masked:
                        sc = jnp.where(_causal_mask(), sc, NEG)
                    p = jnp.exp(sc)
                    dv.append(_dot(p.astype(BF16), dob, TN))
                    ds = (p * _dot(dob, v_ref[keys, slabs[hh]], NT)).astype(BF16)
                    dk.append(_dot(ds, q, TN))
                    dqc.append(_dot(ds, k))
                dv_ref[keys, :] += jnp.concatenate(dv, axis=1)
                dk_ref[keys, :] += jnp.concatenate(dk, axis=1)
                return dq + jnp.concatenate(dqc, axis=1)

            dq = lax.fori_loop(0, i, lambda j, acc: block(j, acc, False), jnp.zeros((TA, 2 * SLAB), F32))
            dq_ref[rows, :] = block(i, dq, True)
            return 0

        lax.fori_loop(0, nq, q_block, 0)

    pair = lambda p: (0, p)
    slab2 = pl.BlockSpec((s, 2 * SLAB), pair)
    seq = pl.BlockSpec((s, LANES), pair)
    small = pl.BlockSpec((1, s, 2), lambda p: (p, 0, 0))
    o32 = jax.ShapeDtypeStruct((s, HEADS * SLAB), F32)
    return _call(
        body, (qp, kp, vp, dmixed, o, lse), name="attn_bwd", grid=(HEADS // 2,),
        in_specs=[slab2, slab2, slab2, seq, seq, small], out_specs=[slab2, slab2, slab2], out_shape=[o32, o32, o32],
        scratch_shapes=[pltpu.VMEM((2, TA, SLAB), BF16), pltpu.VMEM((2, TA, SLAB), BF16)], vmem_mb=40, hosted=hosted)


def _qkv_post(dqp, dkp, dvp, proj, bf_pad, gq, gk):
    s = proj.shape[0]
    nb = s // TR

    def body(dq_ref, dk_ref, dv_ref, q_ref, k_ref, fg_ref, b_ref, gq_ref, gk_ref, dqo_ref, dko_ref, dvo_ref, dfg_ref,
             vec_ref, carry_ref):
        @pl.when(pl.program_id(0) == 0)
        def _():
            vec_ref[...] = jnp.zeros_like(vec_ref)
            carry_ref[...] = jnp.zeros_like(carry_ref)

        def one(d_ref, x_ref, g_ref, o_ref, row, scale):
            dg = jnp.zeros((1, DH), F32)
            for h in range(HEADS):
                sl = slice(DH * h, DH * (h + 1))
                xv = x_ref[:, sl]
                r = lax.rsqrt(jnp.mean(xv * xv, axis=-1, keepdims=True) + EPS)
                xh = xv * r
                dn = d_ref[:, SLAB * h:SLAB * h + DH] * scale
                dg = dg + jnp.sum(dn * xh, axis=0, keepdims=True)
                dxh = dn * g_ref[...]
                o_ref[:, sl] = (r * (dxh - xh * jnp.mean(dxh * xh, axis=-1, keepdims=True))).astype(BF16)
            vec_ref[row:row + 1, 0:DH] += dg

        one(dq_ref, q_ref, gq_ref, dqo_ref, 0, QK_SCALE)
        one(dk_ref, k_ref, gk_ref, dko_ref, 1, 1.0)
        lane = lax.broadcasted_iota(jnp.int32, (TR, LANES), 1)
        df = jnp.zeros((TR, LANES), F32)
        for h in range(HEADS):
            dvo_ref[:, DH * h:DH * (h + 1)] = dv_ref[:, SLAB * h:SLAB * h + DH].astype(BF16)
            row_sum = dq_ref[:, SLAB * h + DH:SLAB * h + DH + 1]
            col_sum = dk_ref[:, SLAB * h + DH + AUG_ONE:SLAB * h + DH + AUG_ONE + 1]
            df = jnp.where(lane == h, row_sum - col_sum, df)
        row = lax.broadcasted_iota(jnp.int32, (TR, TR), 0)
        col = lax.broadcasted_iota(jnp.int32, (TR, TR), 1)
        dlogf = _dot((col >= row).astype(F32), df, NN, lax.Precision.HIGHEST) + carry_ref[0:1, :]
        carry_ref[...] = jnp.broadcast_to(dlogf[0:1, :], carry_ref.shape)
        dfg = dlogf * _sigmoid(-(fg_ref[...] + b_ref[...]))
        dfg_ref[...] = dfg.astype(BF16)
        vec_ref[2:3, :] += jnp.sum(dfg, axis=0, keepdims=True)

    o = jax.ShapeDtypeStruct((s, AW), BF16)
    rev = lambda width, col=0: pl.BlockSpec((TR, width), lambda i, col=col: (nb - 1 - i, col))
    wide = rev(HEADS * SLAB)
    outs, _ = _call(
        body, (dqp, dkp, dvp, proj, proj, proj, bf_pad, gq, gk), name="qkv_post", grid=(nb,),
        in_specs=[wide, wide, wide, rev(AW, 0), rev(AW, 1), rev(LANES, FG_BLOCK), _full_spec((1, LANES)),
                  _full_spec((1, DH)), _full_spec((1, DH))],
        out_specs=[rev(AW), rev(AW), rev(AW), rev(LANES), _full_spec((SUB, LANES))],
        out_shape=[o, o, o, jax.ShapeDtypeStruct((s, LANES), BF16), jax.ShapeDtypeStruct((SUB, LANES), F32)],
        scratch_shapes=[pltpu.VMEM((SUB, LANES), F32)])
    return outs


TF = 256
NJ = DFF // TF
FFN_ROWS_FWD = 1024
FFN_ROWS_BWD = 1024


def _ffn_fwd(h2, wup_t, cw, wd):
    s = h2.shape[0]
    tr = FFN_ROWS_FWD
    nr = s // tr

    def body(h_ref, wu_ref, cg_ref, cv_ref, wd_ref, pg_ref, pv_ref, y_ref, halo_ref, act_ref):
        r, j = pl.program_id(0), pl.program_id(1)
        hv = h_ref[...]
        pg = _dot(hv, wu_ref[0], NT).astype(BF16)
        pv = _dot(hv, wu_ref[1], NT).astype(BF16)
        pg_ref[...] = pg
        pv_ref[...] = pv
        pgf, pvf = pg.astype(F32), pv.astype(F32)
        ug, _, _ = _conv_taps(pgf, jnp.where(r > 0, halo_ref[j, 0], 0.0), cg_ref[...])
        uv, _, _ = _conv_taps(pvf, jnp.where(r > 0, halo_ref[j, 1], 0.0), cv_ref[...])
        halo_ref[j, 0] = pgf[tr - SUB:tr, :]
        halo_ref[j, 1] = pvf[tr - SUB:tr, :]
        act = (ug * _sigmoid(ug) * uv).astype(BF16)
        for t in range(NJ):
            @pl.when(j == t)
            def _(t=t):
                act_ref[:, t * TF:(t + 1) * TF] = act

        @pl.when(j == NJ - 1)
        def _():
            y_ref[...] = _dot(act_ref[...], wd_ref[...])

    pre = jax.ShapeDtypeStruct((s, DFF), BF16)
    return pl.pallas_call(
        body, name="ffn_fwd", grid=(nr, NJ),
        in_specs=[pl.BlockSpec((tr, D), lambda r, j: (r, 0)),
                  pl.BlockSpec((2, TF, D), lambda r, j: (0, j, 0)),
                  pl.BlockSpec((3, TF), lambda r, j: (0, j)),
                  pl.BlockSpec((3, TF), lambda r, j: (0, NJ + j)),
                  pl.BlockSpec((DFF, D), lambda r, j: (0, 0))],
        out_specs=(pl.BlockSpec((tr, TF), lambda r, j: (r, j)),
                   pl.BlockSpec((tr, TF), lambda r, j: (r, j)),
                   pl.BlockSpec((tr, D), lambda r, j: (r, 0))),
        out_shape=(pre, pre, jax.ShapeDtypeStruct((s, D), F32)),
        scratch_shapes=[pltpu.VMEM((NJ, 2, SUB, TF), F32), pltpu.VMEM((tr, DFF), BF16)],
        compiler_params=_params(("arbitrary", "arbitrary"), 56),
    )(h2, wup_t, cw, cw, wd)


def _ffn_bwd(dy, h2, pre_g, pre_v, wup_t, cw, wd):
    s = h2.shape[0]
    tr = FFN_ROWS_BWD
    nr = s // tr
    hb = tr // (2 * SUB)

    def body(dy_ref, h_ref, pg_ref, pv_ref, hg_ref, hv_ref, wu_ref, cg_ref, cv_ref, wd_ref,
             dh_ref, dwu_ref, dwd_ref, dcg_ref, dcv_ref, nxt_ref, awu_ref, awd_ref):
        j, r = pl.program_id(0), pl.program_id(1)
        rr = nr - 1 - r
        row0 = pl.multiple_of(rr * tr, tr)
        cwg, cwv = cg_ref[...], cv_ref[...]
        pg, pv = pg_ref[...].astype(F32), pv_ref[...].astype(F32)
        ug, g1, g2 = _conv_taps(pg, jnp.where(rr > 0, hg_ref[SUB:2 * SUB, :].astype(F32), 0.0), cwg)
        uv, v1, v2 = _conv_taps(pv, jnp.where(rr > 0, hv_ref[SUB:2 * SUB, :].astype(F32), 0.0), cwv)
        sg = _sigmoid(ug)
        sil = ug * sg
        act = (sil * uv).astype(BF16)
        dyv = dy_ref[...]
        da = _dot(dyv, wd_ref[...], NT)
        dug = da * uv * (sg * (1.0 + ug * (1.0 - sg)))
        duv = da * sil
        dpg = _conv_taps_t(dug, jnp.where(r > 0, nxt_ref[0], 0.0), cwg)
        dpv = _conv_taps_t(duv, jnp.where(r > 0, nxt_ref[1], 0.0), cwv)
        nxt_ref[0] = dug[0:SUB, :]
        nxt_ref[1] = duv[0:SUB, :]
        dpgb, dpvb = dpg.astype(BF16), dpv.astype(BF16)
        hv = h_ref[...]
        dwd = _dot(act, dyv, TN)
        dpb = jnp.concatenate([dpgb, dpvb], axis=1)
        dwu = _dot(dpb, hv, TN)
        dh = _dot(dpb, wu_ref[...].reshape(2 * TF, D))

        def taps(du, x0, x1, x2):
            return (jnp.sum(du * x2, axis=0, keepdims=True), jnp.sum(du * x1, axis=0, keepdims=True),
                    jnp.sum(du * x0, axis=0, keepdims=True))

        tg, tv = taps(dug, pg, g1, g2), taps(duv, pv, v1, v2)

        @pl.when(r == 0)
        def _():
            awd_ref[...] = dwd
            awu_ref[...] = dwu
            dcg_ref[...] = jnp.zeros_like(dcg_ref)
            dcv_ref[...] = jnp.zeros_like(dcv_ref)

        @pl.when(r > 0)
        def _():
            awd_ref[...] += dwd
            awu_ref[...] += dwu

        @pl.when(r == nr - 1)
        def _():
            dwd_ref[...] = awd_ref[...].astype(BF16)
            dwu_ref[...] = awu_ref[...].astype(BF16).reshape(2, TF, D)

        for t in range(3):
            dcg_ref[t:t + 1, :] += tg[t]
            dcv_ref[t:t + 1, :] += tv[t]

        @pl.when(j == 0)
        def _():
            dh_ref[pl.ds(row0, tr), :] = dh

        @pl.when(j > 0)
        def _():
            dh_ref[pl.ds(row0, tr), :] += dh

    rows = lambda j, r: (nr - 1 - r, 0)
    tile = lambda j, r: (nr - 1 - r, j)
    halo = lambda j, r: (jnp.maximum((nr - 1 - r) * hb - 1, 0), j)
    return pl.pallas_call(
        body, name="ffn_bwd", grid=(NJ, nr),
        in_specs=[pl.BlockSpec((tr, D), rows), pl.BlockSpec((tr, D), rows),
                  pl.BlockSpec((tr, TF), tile), pl.BlockSpec((tr, TF), tile),
                  pl.BlockSpec((2 * SUB, TF), halo), pl.BlockSpec((2 * SUB, TF), halo),
                  pl.BlockSpec((2, TF, D), lambda j, r: (0, j, 0)),
                  pl.BlockSpec((3, TF), lambda j, r: (0, j)), pl.BlockSpec((3, TF), lambda j, r: (0, NJ + j)),
                  pl.BlockSpec((TF, D), lambda j, r: (j, 0))],
        out_specs=(pl.BlockSpec((s, D), lambda j, r: (0, 0)),
                   pl.BlockSpec((2, TF, D), lambda j, r: (0, j, 0)),
                   pl.BlockSpec((TF, D), lambda j, r: (j, 0)),
                   pl.BlockSpec((SUB, TF), lambda j, r: (0, j)), pl.BlockSpec((SUB, TF), lambda j, r: (0, j))),
        out_shape=(jax.ShapeDtypeStruct((s, D), F32),
                   jax.ShapeDtypeStruct((2, DFF, D), BF16), jax.ShapeDtypeStruct((DFF, D), BF16),
                   jax.ShapeDtypeStruct((SUB, DFF), F32), jax.ShapeDtypeStruct((SUB, DFF), F32)),
        scratch_shapes=[pltpu.VMEM((2, SUB, TF), F32), pltpu.VMEM((2 * TF, D), F32), pltpu.VMEM((TF, D), F32)],
        compiler_params=_params(("arbitrary", "arbitrary"), 56),
    )(dy, h2, pre_g, pre_v, pre_g, pre_v, wup_t, cw, cw, wd)


def _adam(w, g, m, v):
    m = ADAM_B1 * m + (1.0 - ADAM_B1) * g
    v = ADAM_B2 * v + (1.0 - ADAM_B2) * (g * g)
    m_hat = m / (1.0 - ADAM_B1 ** ADAM_STEP)
    v_hat = v / (1.0 - ADAM_B2 ** ADAM_STEP)
    delta = -ADAM_LR * (m_hat / (jnp.sqrt(v_hat) + ADAM_EPS) + ADAM_WD * w)
    return delta, m, v


NCHIP = NDEV // 2


def _pair_add(pairs, name):
    n = len(pairs)

    def body(*refs):
        c = lax.axis_index("c")
        for a_ref, b_ref, o_ref in zip(refs[:n], refs[n:2 * n], refs[2 * n:]):
            o_ref[0] = (a_ref[0, c].astype(F32) + b_ref[0].astype(F32)).astype(BF16)

    mine_specs = [pl.BlockSpec((1,) + m.shape[1:], lambda q: (q, 0, 0, 0)) for m, _ in pairs]
    slot_specs = [pl.BlockSpec((1,) + t.shape[1:], lambda q: (q, 0, 0)) for _, t in pairs]
    outs, _ = _call(
        body, [m for m, _ in pairs] + [t for _, t in pairs], name=name, grid=(NCHIP,),
        in_specs=mine_specs + slot_specs, out_specs=slot_specs,
        out_shape=[jax.ShapeDtypeStruct(t.shape, BF16) for _, t in pairs], vmem_mb=32)
    return outs


def _adamw_sharded(parts, w, m, v, tr, name, hosted=None):
    rws, cols = w.shape
    n_parts = parts.shape[0]

    def body(p_ref, w_ref, m_ref, v_ref, g_ref, d_ref, mo_ref, vo_ref):
        g = p_ref[0].astype(F32)
        for q in range(1, n_parts):
            g = g + p_ref[q].astype(F32)
        g_ref[...] = g
        d_ref[...], mo_ref[...], vo_ref[...] = _adam(w_ref[...], g, m_ref[...], v_ref[...])

    blk = pl.BlockSpec((tr, cols), lambda i: (i, 0))
    o = jax.ShapeDtypeStruct((rws, cols), F32)
    outs, moved = _call(
        body, (parts, w, m, v), name=name, grid=(rws // tr,),
        in_specs=[pl.BlockSpec((n_parts, tr, cols), lambda i: (0, i, 0)), blk, blk, blk],
        out_specs=[blk, blk, blk, blk], out_shape=[o, o, o, o], vmem_mb=44 if tr > 256 else 24, hosted=hosted)
    return (outs, moved) if hosted else outs


def _adamw_ada(c_all, dmod_my, w, m, v):
    rws, cols = w.shape
    tr = 256

    def body(c_ref, dm_ref, w_ref, m_ref, v_ref, g_ref, d_ref, mo_ref, vo_ref):
        cv = c_ref[...]
        act = cv * _sigmoid(cv)
        g = _dot(act, dm_ref[...], TN, lax.Precision.HIGHEST)
        g_ref[...] = g
        d_ref[...], mo_ref[...], vo_ref[...] = _adam(w_ref[...], g, m_ref[...], v_ref[...])

    blk = pl.BlockSpec((tr, cols), lambda i: (i, 0))
    o = jax.ShapeDtypeStruct((rws, cols), F32)
    return pl.pallas_call(
        body, name="adamw_ada", grid=(rws // tr,),
        in_specs=[pl.BlockSpec((NDEV, tr), lambda i: (0, i)), _full_spec((NDEV, cols)), blk, blk, blk],
        out_specs=(blk, blk, blk, blk), out_shape=(o, o, o, o),
        compiler_params=_params(("parallel",), 32),
    )(c_all, dmod_my, w, m, v)


REP_ROWS = 16
ROW_N1, ROW_N2, ROW_LOSS, ROW_MISC = 6, 7, 8, 9
LANE_BF, LANE_GQ, LANE_GK = 0, 128, 256


def _adamw_small(rep_all, conv_all, wmv):
    n_ff = wmv[6][0].shape[1]

    def body(*refs):
        rep_ref, conv_ref = refs[:2]
        ins = refs[2:2 + 24]
        outs = refs[2 + 24:]
        loss_ref, outs = outs[0], outs[1:]
        g_rep = rep_ref[0]
        g_conv = conv_ref[0]
        for d in range(1, NDEV):
            g_rep = g_rep + rep_ref[d]
            g_conv = g_conv + conv_ref[d]
        loss_ref[...] = (0.5 / D) * jnp.sum(g_rep[ROW_LOSS:ROW_LOSS + 1, :], axis=-1, keepdims=True)
        grads = [
            None,
            g_rep[ROW_N1:ROW_N1 + 1, :],
            g_rep[ROW_MISC:ROW_MISC + 1, LANE_BF:LANE_BF + HEADS],
            g_rep[ROW_MISC:ROW_MISC + 1, LANE_GQ:LANE_GQ + DH],
            g_rep[ROW_MISC:ROW_MISC + 1, LANE_GK:LANE_GK + DH],
            g_rep[ROW_N2:ROW_N2 + 1, :],
            g_conv[0:3, 0:n_ff],
            g_conv[0:3, n_ff:n_ff + DH],
        ]
        for p in range(8):
            w_ref, m_ref, v_ref = ins[3 * p:3 * p + 3]
            g_ref, d_ref, mo_ref, vo_ref = outs[4 * p:4 * p + 4]
            if p == 0:
                for nmod in range(NMOD):
                    sl = slice(D * nmod, D * (nmod + 1))
                    g = g_rep[nmod:nmod + 1, :]
                    g_ref[:, sl] = g
                    d_ref[:, sl], mo_ref[:, sl], vo_ref[:, sl] = _adam(w_ref[:, sl], g, m_ref[:, sl], v_ref[:, sl])
            else:
                g = grads[p]
                g_ref[...] = g
                d_ref[...], mo_ref[...], vo_ref[...] = _adam(w_ref[...], g, m_ref[...], v_ref[...])

    flat = [a for trio in wmv for a in trio]
    out_shape = [jax.ShapeDtypeStruct((1, 1), F32)]
    for trio in wmv:
        out_shape += [jax.ShapeDtypeStruct(trio[0].shape, F32)] * 4
    return pl.pallas_call(
        body, name="adamw_small", out_shape=tuple(out_shape),
        compiler_params=_params(None, 32),
    )(rep_all, conv_all, *flat)


FG_FIRST = 3 * AW
N_IN = DIN // NDEV


def _w_in_runs():
    runs = []
    for d in range(NDEV):
        lo, hi = N_IN * d, N_IN * (d + 1)
        for a, b, shift in ((0, FG_FIRST, 0), (FG_FIRST, FG_FIRST + HEADS, DIN - HEADS - FG_FIRST),
                            (FG_FIRST + HEADS, DIN, -HEADS)):
            a, b = max(a, lo), min(b, hi)
            if a < b:
                runs.append((d, a - lo, a + shift, b - a))
    return runs


W_IN_ROWS = 256
N_IN_PAD = 512


def _identity(n):
    return (lax.broadcasted_iota(jnp.int32, (n, n), 0) == lax.broadcasted_iota(jnp.int32, (n, n), 1)).astype(BF16)


def _assemble_w_in(g_in, hosted):
    def body(g_ref, o_ref, t_ref):
        eye = _identity(W_IN_ROWS)
        shard = None
        for d, src, dst, width in _w_in_runs():
            if d != shard:
                t_ref[:, 0:N_IN] = _dot(eye, g_ref[d], NT).astype(BF16)
                shard = d
            o_ref[:, dst:dst + width] = t_ref[:, src:src + width]
        o_ref[:, DIN:DINP] = jnp.zeros((W_IN_ROWS, DINP - DIN), o_ref.dtype)

    (out,), moved = _call(
        body, (g_in,), name="assemble_w_in", grid=(D // W_IN_ROWS,),
        in_specs=[pl.BlockSpec((NDEV, N_IN, W_IN_ROWS), lambda i: (0, 0, i))],
        out_specs=[pl.BlockSpec((W_IN_ROWS, DINP), lambda i: (i, 0))],
        out_shape=[jax.ShapeDtypeStruct((D, DINP), g_in.dtype)],
        scratch_shapes=[pltpu.VMEM((W_IN_ROWS, N_IN_PAD), BF16)], vmem_mb=16, hosted=hosted)
    return out, moved


def _scatter_dw_in(dwp):
    def body(w_ref, o_ref, t_ref):
        eye = _identity(W_IN_ROWS)
        runs = _w_in_runs()
        for i, (d, src, dst, width) in enumerate(runs):
            t_ref[:, src:src + width] = w_ref[:, dst:dst + width]
            if i + 1 == len(runs) or runs[i + 1][0] != d:
                o_ref[d // 2, d % 2] = _dot(t_ref[:, 0:N_IN], eye, TN).astype(BF16)

    (out,), _ = _call(
        body, (dwp,), name="scatter_dw_in", grid=(D // W_IN_ROWS,),
        in_specs=[pl.BlockSpec((W_IN_ROWS, DINP), lambda i: (i, 0))],
        out_specs=[pl.BlockSpec((NCHIP, 2, N_IN, W_IN_ROWS), lambda i: (0, 0, 0, i))],
        out_shape=[jax.ShapeDtypeStruct((NCHIP, 2, N_IN, D), dwp.dtype)],
        scratch_shapes=[pltpu.VMEM((W_IN_ROWS, N_IN_PAD), BF16)], vmem_mb=16)
    return out


def kernel(x, c, w_ada, b_ada, norm1_g, w_in, b_forget, q_norm_g, k_norm_g, conv_mix_w, w_out, norm2_g, w_up, ffn_conv_w, w_down, loss_target, m_w_ada, m_b_ada, m_norm1_g, m_w_in, m_b_forget, m_q_norm_g, m_k_norm_g, m_conv_mix_w, m_w_out, m_norm2_g, m_w_up, m_ffn_conv_w, m_w_down, v_w_ada, v_b_ada, v_norm1_g, v_w_in, v_b_forget, v_q_norm_g, v_k_norm_g, v_conv_mix_w, v_w_out, v_norm2_g, v_w_up, v_ffn_conv_w, v_w_down):
    me = 4 * lax.axis_index("x") + 2 * lax.axis_index("y") + lax.axis_index("c")
    xs, tgt = x[0], loss_target[0]
    n_ada = w_ada.shape[2]
    n_ff = w_up.shape[2]

    conv_w = jnp.concatenate([ffn_conv_w[0], conv_mix_w[0]], axis=1)
    conv_w = jnp.concatenate([conv_w, jnp.zeros((SUB - 3, conv_w.shape[1]), F32)], axis=0)
    c_all, conv_all, g_in = _exchange(
        [(c.reshape(SUB, D // SUB), "ag"), (conv_w, "ag"), (jnp.transpose(w_in[0]).astype(BF16), "ag2")],
        "exchange_w_in")
    g_in, w_out_b, w_up_b, w_down_b = lax.optimization_barrier(
        (g_in, w_out[0].astype(BF16), jnp.transpose(w_up[0]).astype(BF16), w_down[0].astype(BF16)))
    g_out, g_up, g_down = _sequencer_exchange(
        [(w_out_b, "ag2"), (w_up_b, "ag2"), (w_down_b, "ag2")], "gather_weights", collective_id=1)
    c_all = c_all.reshape(NDEV, D)
    cw_ffn = jnp.transpose(conv_all[:, :3, :n_ff], (1, 0, 2)).reshape(3, 2 * DFF)
    cw_mix = jnp.transpose(conv_all[:, :3, n_ff:], (1, 0, 2)).reshape(3, CW)

    b_my = lax.dynamic_slice(b_ada, (0, me * n_ada), (1, n_ada))
    mod_part = _ada_fwd(c_all, w_ada[0], b_my)
    w_in_p, (mod_rows,) = _assemble_w_in(
        g_in, [(jnp.broadcast_to(mod_part[:, None, :], (NDEV, SUB, n_ada)), "a2a")])
    mod = mod_rows[:, 0, :].reshape(NMOD, D)
    mod = jnp.concatenate([mod, jnp.zeros((SUB - NMOD, D), F32)], axis=0)

    h, proj = _norm_proj_fwd(xs, mod, norm1_g, w_in_p, 1024, 640)
    bf_pad = jnp.concatenate([b_forget, jnp.zeros((1, LANES - HEADS), F32)], axis=1)
    qp, kp, vp = _qkv_prep(proj, bf_pad, q_norm_g, k_norm_g)
    attn, lse = _attn_fwd(qp, kp, vp)
    w_out_f = g_out.reshape(D, D)
    w_up_t = g_up.reshape(2, DFF, D)
    w_down_f = g_down.reshape(DFF, D)
    conv = _mixconv_fwd(proj, cw_mix)
    mixed = jnp.concatenate([attn, conv], axis=1).astype(BF16)
    z = _mm(mixed, w_out_f, "nn", F32, 1024, 1024, "out_fwd")
    x1, h2 = _resid_norm2(xs, z, mod, norm2_g)
    pre_g, pre_v, y = _ffn_fwd(h2, w_up_t, cw_ffn, w_down_f)
    dout, dy, vec_l = _loss_head(x1, y, tgt, mod)

    dh2, dwup_t, dwd, dcw_g, dcw_v = _ffn_bwd(dy, h2, pre_g, pre_v, w_up_t, cw_ffn, w_down_f)
    s_down = dwd.reshape(NCHIP, 2, DFF // NDEV, D)
    s_up = dwup_t.reshape(NCHIP, 2, n_ff, D)
    dx1, dz, vec_2, (t_up, t_down) = _norm_mod_bwd(dh2, x1, dout, z, mod, norm2_g, 4, 2, "norm2_bwd",
                                                   hosted=[(s_up, "pair"), (s_down, "pair")])
    dwout = _mm(mixed, dz, "tn", BF16, 1024, 1024, "out_bwd_w")
    s_out = dwout.reshape(NCHIP, 2, D // NDEV, D)
    dmixed, (t_out,) = _mm(dz, w_out_f, "nt", F32, 1024, 1024, "out_bwd_x", hosted=[(s_out, "pair")])
    c_out, c_up, c_down = _pair_add([(s_out, t_out), (s_up, t_up), (s_down, t_down)], "pair_add")
    dxin, dbg, dcg, dcw_mix = _mixconv_bwd(dmixed, proj, cw_mix)
    (dqp, dkp, dvp), (p_up, p_down, p_out) = _attn_bwd(
        qp, kp, vp, dmixed, attn, lse, [(c_up, "chips"), (c_down, "chips"), (c_out, "chips")])
    dq, dk, dvb, dfg, vec_qk = _qkv_post(dqp, dkp, dvp, proj, bf_pad, q_norm_g, k_norm_g)
    dproj = jnp.concatenate([dq, dk, dvb, dxin, dbg, dcg, dfg], axis=1)
    dwin_p = _mm(h, dproj, "tn", BF16, 1024, 640, "proj_bwd_w")
    s_in = _scatter_dw_in(dwin_p).reshape(NDEV, N_IN, D)
    (p_in,) = _sequencer_exchange([(s_in, "a2a")], "scatter_dw_in_partials", collective_id=2, all_peers=True)
    dh = _mm(dproj, w_in_p, "nt", F32, 1024, 512, "proj_bwd_x", vmem_mb=36)
    grad_x, vec_1 = _norm_mod_bwd(dh, xs, dx1, None, mod, norm1_g, 1, None, "norm1_bwd")

    gap = lambda n: jnp.zeros((1, n), F32)
    misc = jnp.concatenate([
        vec_qk[2:3, :HEADS], gap(LANE_GQ - LANE_BF - HEADS), vec_qk[0:1, :DH], gap(LANE_GK - LANE_GQ - DH),
        vec_qk[1:2, :DH], gap(D - LANE_GK - DH)], axis=1)
    rep = jnp.concatenate([
        vec_1[0:1], vec_1[1:2], vec_2[3:4], vec_2[0:1], vec_2[1:2], vec_l[0:1],
        vec_1[2:3], vec_2[2:3], vec_l[1:2], misc, jnp.zeros((REP_ROWS - 10, D), F32)], axis=0)
    dcw_ffn = jnp.concatenate([dcw_g, dcw_v], axis=1).reshape(SUB, NDEV, n_ff)
    dcw_all = jnp.concatenate([jnp.transpose(dcw_ffn, (1, 0, 2)),
                               jnp.transpose(dcw_mix.reshape(SUB, NDEV, DH), (1, 0, 2))], axis=2)
    r_up = _adamw_sharded(p_up, jnp.transpose(w_up[0]), jnp.transpose(m_w_up[0]), jnp.transpose(v_w_up[0]), 176,
                          "adamw_up")
    r_down = _adamw_sharded(p_down, w_down[0], m_w_down[0], v_w_down[0], 176, "adamw_down")
    rep, dcw_all, r_up, r_down = lax.optimization_barrier((rep, dcw_all, r_up, r_down))
    r_up = tuple(jnp.transpose(a) for a in r_up)
    r_out, (rep_all, conv_parts) = _adamw_sharded(p_out, w_out[0], m_w_out[0], v_w_out[0], 128, "adamw_out",
                                                  hosted=[(rep, "ag"), (dcw_all, "a2a")])
    dmod_my = lax.dynamic_slice(rep_all[:, :NMOD, :].reshape(NDEV, NMOD * D), (0, me * n_ada), (NDEV, n_ada))
    r_ada = _adamw_ada(c_all, dmod_my, w_ada[0], m_w_ada[0], v_w_ada[0])
    r_in = _adamw_sharded(p_in, jnp.transpose(w_in[0]), jnp.transpose(m_w_in[0]), jnp.transpose(v_w_in[0]), N_IN,
                          "adamw_in")
    r_in = tuple(jnp.transpose(a) for a in r_in)
    small = _adamw_small(rep_all, conv_parts, [
        [b_ada, m_b_ada, v_b_ada], [norm1_g, m_norm1_g, v_norm1_g], [b_forget, m_b_forget, v_b_forget],
        [q_norm_g, m_q_norm_g, v_q_norm_g], [k_norm_g, m_k_norm_g, v_k_norm_g], [norm2_g, m_norm2_g, v_norm2_g],
        [ffn_conv_w[0], m_ffn_conv_w[0], v_ffn_conv_w[0]], [conv_mix_w[0], m_conv_mix_w[0], v_conv_mix_w[0]]])
    loss = small[0].reshape(())
    r_bada, r_n1, r_bf, r_gq, r_gk, r_n2, r_cf, r_cm = [small[1 + 4 * p:5 + 4 * p] for p in range(8)]
    lead = lambda t: tuple(a[None] for a in t)
    per_w = [lead(r_ada), r_bada, r_n1, lead(r_in), r_bf, r_gq, r_gk, lead(r_cm), lead(r_out), r_n2,
             lead(r_up), lead(r_cf), lead(r_down)]
    outs = [loss, grad_x[None]]
    for field in range(4):
        outs += [t[field] for t in per_w]
    return tuple(outs)
```

```python
import functools

import jax
import jax.numpy as jnp
import numpy as np
from jax import lax
from jax.experimental import pallas as pl
from jax.experimental.pallas import tpu as pltpu
from jax.experimental.pallas import tpu_sc as plsc

F32 = jnp.float32
BF16 = jnp.bfloat16

NDEV = 8
D = 1024
HEADS = 8
DH = 64
AW = 512
CW = 512
DFF = 2816
DIN = 3080
DINP = 3200
NMOD = 6
EPS = 1e-6
QK_SCALE = 0.125
LANES = 128
SUB = 8

ADAM_LR = 0.001
ADAM_B1 = 0.9
ADAM_B2 = 0.999
ADAM_EPS = 1e-08
ADAM_WD = 0.01
ADAM_STEP = 10

MESH = pl.DeviceIdType.MESH
ANY = pl.BlockSpec(memory_space=pl.ANY)

NN = (((1,), (0,)), ((), ()))
NT = (((1,), (1,)), ((), ()))
TN = (((0,), (0,)), ((), ()))


def _dot(a, b, dims=NN, precision=None):
    return lax.dot_general(a, b, dims, precision=precision, preferred_element_type=F32)


def _params(sem=None, vmem_mb=None):
    kw = {}
    if sem is not None:
        kw["dimension_semantics"] = sem
    if vmem_mb is not None:
        kw["vmem_limit_bytes"] = vmem_mb * 1024 * 1024
    return pltpu.CompilerParams(**kw)


def _sigmoid(x):
    return 0.5 * jnp.tanh(0.5 * x) + 0.5


class _Exchange:
    def __init__(self, items):
        self.arrays = [pltpu.with_memory_space_constraint(a, pltpu.HBM) for a, _ in items]
        self.modes = [m for _, m in items]
        self.n = len(items)
        self.out_shape = []
        for a, m in items:
            sh = {"ag": (NDEV,) + a.shape, "ag2": (NDEV,) + a.shape, "pair": a.shape[:1] + a.shape[2:]}.get(m, a.shape)
            self.out_shape.append(jax.ShapeDtypeStruct(sh, a.dtype))
        self.scratch = [pltpu.SemaphoreType.DMA((self.n, NDEV - 1)), pltpu.SemaphoreType.DMA((self.n, NDEV - 1)),
                        pltpu.SemaphoreType.DMA((self.n,))]

    def _plan(self, srcs, outs, sems):
        send_sems, recv_sems, loc_sems = sems
        x, y, c = lax.axis_index("x"), lax.axis_index("y"), lax.axis_index("c")
        me, my_chip = 4 * x + 2 * y + c, 2 * x + y
        sib = (x, y, 1 - c)
        local, first, landed, forwards, arrivals = [], [], [], [], []

        def remote(a, k, src, dst, to):
            return pltpu.make_async_remote_copy(src_ref=src, dst_ref=dst, send_sem=send_sems.at[a, k],
                                                recv_sem=recv_sems.at[a, k], device_id=to, device_id_type=MESH)

        for a, mode in enumerate(self.modes):
            src, out = srcs[a], outs[a]
            if mode in ("ag", "a2a"):
                piece = (lambda slot, src=src: src) if mode == "ag" else (lambda slot, src=src: src.at[slot])
                local.append(pltpu.make_async_copy(piece(me), out.at[me], loc_sems.at[a]))
                for r in range(1, NDEV):
                    px = 1 - x if (r >> 2) & 1 else x
                    py = 1 - y if (r >> 1) & 1 else y
                    pc = 1 - c if r & 1 else c
                    pidx = 4 * px + 2 * py + pc
                    first.append(remote(a, r - 1, piece(pidx), out.at[me], (px, py, pc)))
                    arrivals.append(remote(a, r - 1, piece(pidx), out.at[pidx], (px, py, pc)))
            elif mode == "ag2":
                local.append(pltpu.make_async_copy(src, out.at[me], loc_sems.at[a]))
                first.append(remote(a, 0, src, out.at[me], sib))
                arrivals.append(remote(a, 0, src, out.at[me + 1 - 2 * c], sib))
                for j, (px, py) in enumerate([(1 - x, y), (x, 1 - y), (1 - x, 1 - y)]):
                    theirs = out.at[4 * px + 2 * py + c]
                    first.append(remote(a, 1 + j, src, out.at[me], (px, py, c)))
                    landed.append(remote(a, 1 + j, src, theirs, (px, py, c)))
                    forwards.append(remote(a, 4 + j, theirs, theirs, sib))
                    arrivals.append(remote(a, 4 + j, src, out.at[4 * px + 2 * py + 1 - c], sib))
            elif mode == "pair":
                for q in range(NDEV // 2):
                    first.append(remote(a, q, src.at[q, 1 - c], out.at[q], sib))
                    arrivals.append(remote(a, q, src.at[q, 1 - c], out.at[q], sib))
            else:
                assert mode == "chips", mode
                local.append(pltpu.make_async_copy(src.at[my_chip], out.at[my_chip], loc_sems.at[a]))
                for j, (px, py) in enumerate([(1 - x, y), (x, 1 - y), (1 - x, 1 - y)]):
                    q = 2 * px + py
                    first.append(remote(a, 1 + j, src.at[q], out.at[my_chip], (px, py, c)))
                    arrivals.append(remote(a, 1 + j, src.at[q], out.at[q], (px, py, c)))
        return local, first, landed, forwards, arrivals

    def start(self, srcs, outs, sems):
        local, first, _, _, _ = self._plan(srcs, outs, sems)
        for cp in local + first:
            cp.start()

    def wait(self, srcs, outs, sems):
        local, first, landed, forwards, arrivals = self._plan(srcs, outs, sems)
        for cp, fwd in zip(landed, forwards):
            cp.wait_recv()
            fwd.start()
        for cp in arrivals:
            cp.wait_recv()
        for cp in first + forwards:
            cp.wait_send()
        for cp in local:
            cp.wait()


def _exchange(items, name):
    ex = _Exchange(items)
    n = ex.n

    def body(*refs):
        srcs, outs, sems = refs[:n], refs[n:2 * n], refs[2 * n:]
        ex.start(srcs, outs, sems)
        ex.wait(srcs, outs, sems)

    return pl.pallas_call(
        body, name=name,
        out_shape=tuple(ex.out_shape),
        in_specs=[ANY] * n, out_specs=tuple([ANY] * n),
        scratch_shapes=ex.scratch,
        compiler_params=pltpu.CompilerParams(has_side_effects=True),
    )(*ex.arrays)


def _sequencer_exchange(items, name, collective_id, all_peers=False):
    ex = _Exchange(items)
    srcs = [jax.new_ref(a, memory_space=pltpu.MemorySpace.HBM) for a in ex.arrays]
    outs = [jax.empty_ref(sh, memory_space=pltpu.MemorySpace.HBM) for sh in ex.out_shape]

    @pl.kernel(mesh=plsc.ScalarSubcoreMesh(axis_name="sequencer", num_cores=1), name=name,
               scratch_types=tuple(ex.scratch), compiler_params=pltpu.CompilerParams(collective_id=collective_id))
    def launch(send_sems, recv_sems, loc_sems):
        x, y, c = lax.axis_index("x"), lax.axis_index("y"), lax.axis_index("c")
        barrier = pltpu.get_barrier_semaphore()
        peers = [(x, y, 1 - c), (1 - x, y, c), (x, 1 - y, c), (1 - x, 1 - y, c)]
        if all_peers:
            peers += [(1 - x, y, 1 - c), (x, 1 - y, 1 - c), (1 - x, 1 - y, 1 - c)]
        for peer in peers:
            pl.semaphore_signal(barrier, inc=1, device_id=peer, device_id_type=MESH)
        pl.semaphore_wait(barrier, len(peers))
        sems = (send_sems, recv_sems, loc_sems)
        ex.start(srcs, outs, sems)
        ex.wait(srcs, outs, sems)

    launch()
    return [o[...] for o in outs]


def _call(body, inputs, *, name, grid, in_specs, out_specs, out_shape, scratch_shapes=(), vmem_mb=None, hosted=None):
    out_specs, out_shape, scratch_shapes = tuple(out_specs), tuple(out_shape), list(scratch_shapes)
    if not hosted:
        res = pl.pallas_call(
            body, name=name, grid=grid, in_specs=list(in_specs), out_specs=out_specs, out_shape=out_shape,
            scratch_shapes=scratch_shapes, compiler_params=_params(("arbitrary",) * len(grid), vmem_mb),
        )(*inputs)
        return tuple(res), ()
    ex = _Exchange(hosted)
    n, n_in, n_out, n_scr = ex.n, len(inputs), len(out_shape), len(scratch_shapes)

    def hosting_body(*refs):
        ins, srcs = refs[:n_in], refs[n_in:n_in + n]
        outs, landing = refs[n_in + n:n_in + n + n_out], refs[n_in + n + n_out:n_in + 2 * n + n_out]
        scratch, sems = refs[n_in + 2 * n + n_out:n_in + 2 * n + n_out + n_scr], refs[n_in + 2 * n + n_out + n_scr:]
        first = functools.reduce(jnp.logical_and, [pl.program_id(d) == 0 for d in range(len(grid))])
        last = functools.reduce(jnp.logical_and, [pl.program_id(d) == grid[d] - 1 for d in range(len(grid))])

        @pl.when(first)
        def _():
            ex.start(srcs, landing, sems)

        body(*ins, *outs, *scratch)

        @pl.when(last)
        def _():
            ex.wait(srcs, landing, sems)

    res = pl.pallas_call(
        hosting_body, name=name, grid=grid,
        in_specs=list(in_specs) + [ANY] * n, out_specs=out_specs + tuple([ANY] * n),
        out_shape=out_shape + tuple(ex.out_shape), scratch_shapes=scratch_shapes + ex.scratch,
        compiler_params=_params(("arbitrary",) * len(grid), vmem_mb),
    )(*inputs, *ex.arrays)
    return tuple(res[:n_out]), tuple(res[n_out:])


def _mm(a, b, mode, out_dtype, tm, tn, name, hosted=None, vmem_mb=24):
    if mode == "nn":
        (m, k), n = a.shape, b.shape[1]
        a_spec = pl.BlockSpec((tm, k), lambda i, j: (i, 0))
        b_spec = pl.BlockSpec((k, tn), lambda i, j: (0, j))
        dims = NN
    elif mode == "nt":
        (m, k), n = a.shape, b.shape[0]
        a_spec = pl.BlockSpec((tm, k), lambda i, j: (i, 0))
        b_spec = pl.BlockSpec((tn, k), lambda i, j: (j, 0))
        dims = NT
    else:
        (k, m), n = a.shape, b.shape[1]
        a_spec = pl.BlockSpec((k, tm), lambda i, j: (0, i))
        b_spec = pl.BlockSpec((k, tn), lambda i, j: (0, j))
        dims = TN
    assert m % tm == 0 and n % tn == 0, (m, n, tm, tn)

    def body(a_ref, b_ref, o_ref):
        o_ref[...] = _dot(a_ref[...], b_ref[...], dims).astype(o_ref.dtype)

    (out,), moved = _call(
        body, (a, b), name=name, grid=(m // tm, n // tn),
        in_specs=[a_spec, b_spec], out_specs=[pl.BlockSpec((tm, tn), lambda i, j: (i, j))],
        out_shape=[jax.ShapeDtypeStruct((m, n), out_dtype)], vmem_mb=vmem_mb, hosted=hosted)
    return (out, moved) if hosted else out


def _shift_down(x, k, fill):
    y = pltpu.roll(x, k, 0)
    row = lax.broadcasted_iota(jnp.int32, (SUB, x.shape[1]), 0)
    head = y[0:SUB, :]
    for t in range(k):
        head = jnp.where(row == t, fill[t], head)
    return jnp.concatenate([head, y[SUB:, :]], axis=0)


def _shift_up(x, k, fill):
    n = x.shape[0]
    y = pltpu.roll(x, n - k, 0)
    row = lax.broadcasted_iota(jnp.int32, (SUB, x.shape[1]), 0)
    tail = y[n - SUB:, :]
    for t in range(k):
        tail = jnp.where(row == SUB - k + t, fill[t], tail)
    return jnp.concatenate([y[:n - SUB, :], tail], axis=0)


def _conv_taps(x, halo, w):
    if halo is None:
        f1, f2 = [0.0], [0.0, 0.0]
    else:
        f1, f2 = [halo[7:8, :]], [halo[6:7, :], halo[7:8, :]]
    s1 = _shift_down(x, 1, f1)
    s2 = _shift_down(x, 2, f2)
    u = w[2:3, :] * x + w[1:2, :] * s1 + w[0:1, :] * s2
    return u, s1, s2


def _conv_taps_t(du, nxt, w):
    if nxt is None:
        f1, f2 = [0.0], [0.0, 0.0]
    else:
        f1, f2 = [nxt[0:1, :]], [nxt[0:1, :], nxt[1:2, :]]
    return w[2:3, :] * du + w[1:2, :] * _shift_up(du, 1, f1) + w[0:1, :] * _shift_up(du, 2, f2)


def _ada_fwd(c_all, w_ada, b_my):
    def body(c_ref, w_ref, b_ref, o_ref):
        cv = c_ref[...]
        act = cv * _sigmoid(cv)
        o_ref[...] = _dot(act, w_ref[...], NN, lax.Precision.HIGHEST) + b_ref[...]

    out = jax.ShapeDtypeStruct((NDEV, w_ada.shape[1]), F32)
    return pl.pallas_call(
        body, name="ada_fwd", grid=(1,),
        in_specs=[_full_spec(c_all.shape), _full_spec(w_ada.shape), _full_spec(b_my.shape)],
        out_specs=_full_spec(out.shape), out_shape=out, compiler_params=_params(("arbitrary",), 32),
    )(c_all, w_ada, b_my)


TR = 256
TRE = 512


def _row_spec(width, col=0, rows=TR):
    return pl.BlockSpec((rows, width), lambda i, col=col: (i, col))


def _erow(width):
    return _row_spec(width, rows=TRE)


def _full_spec(shape):
    return pl.BlockSpec(shape, lambda i: (0,) * len(shape))


def _norm_mod_fwd(x, mod, g):
    s = x.shape[0]

    def body(x_ref, mod_ref, g_ref, h_ref):
        xv = x_ref[...]
        r = lax.rsqrt(jnp.mean(xv * xv, axis=-1, keepdims=True) + EPS)
        nrm = xv * r * g_ref[...]
        h_ref[...] = (nrm * (1.0 + mod_ref[1:2, :]) + mod_ref[0:1, :]).astype(BF16)

    return pl.pallas_call(
        body, name="norm1_fwd", grid=(s // TRE,),
        in_specs=[_erow(D), _full_spec((SUB, D)), _full_spec((1, D))],
        out_specs=_erow(D), out_shape=jax.ShapeDtypeStruct((s, D), BF16),
        compiler_params=_params(("parallel",), 16),
    )(x, mod, g)


SLAB = 2 * DH
AUG_F, AUG_ONE, AUG_LSE = 0, 3, 6


def _split3(x):
    hi = x.astype(BF16).astype(F32)
    r1 = x - hi
    mid = r1.astype(BF16).astype(F32)
    return hi, mid, r1 - mid


def _lanes3(lane, first, pieces, other):
    out = other
    for k in range(3):
        out = jnp.where(lane == first + k, pieces[k], out)
    return out


def _aug_placement():
    eq = np.zeros((3 * LANES, HEADS * SLAB), np.float32)
    ek = np.zeros((3 * LANES, HEADS * SLAB), np.float32)
    ones = np.zeros((SUB, HEADS * SLAB), np.float32)
    for h in range(HEADS):
        aug = SLAB * h + DH
        for k in range(3):
            eq[LANES * k + h, aug + AUG_F + k] = 1.0
            ek[LANES * k + h, aug + AUG_ONE + k] = -1.0
            ones[0, aug + AUG_ONE + k] = 1.0
            ones[1, aug + AUG_F + k] = ones[1, aug + AUG_LSE + k] = 1.0
            ones[2, aug + k] = 1.0
    return jnp.asarray(eq, BF16), jnp.asarray(ek, BF16), jnp.asarray(ones)


FG_BLOCK = (3 * AW + 3 * CW) // LANES


def _qkv_prep(proj, bf_pad, gq, gk):
    s = proj.shape[0]

    def body(q_ref, k_ref, v_ref, fg_ref, b_ref, gq_ref, gk_ref, eq_ref, ek_ref, ones_ref, qo_ref, ko_ref, vo_ref,
             carry_ref):
        @pl.when(pl.program_id(0) == 0)
        def _():
            carry_ref[...] = jnp.zeros_like(carry_ref)

        z = fg_ref[...] + b_ref[...]
        logf = jnp.minimum(z, 0.0) - jnp.log1p(jnp.exp(-jnp.abs(z)))
        row = lax.broadcasted_iota(jnp.int32, (TR, TR), 0)
        col = lax.broadcasted_iota(jnp.int32, (TR, TR), 1)
        fcum = _dot((col <= row).astype(F32), logf, NN, lax.Precision.HIGHEST) + carry_ref[0:1, :]
        carry_ref[...] = jnp.broadcast_to(fcum[TR - 1:TR, :], carry_ref.shape)
        f3 = jnp.concatenate(_split3(fcum), axis=1).astype(BF16)
        qo_ref[...] = (_dot(f3, eq_ref[...]) + ones_ref[0:1, :]).astype(BF16)
        ko_ref[...] = (_dot(f3, ek_ref[...]) + ones_ref[1:2, :]).astype(BF16)
        vo_ref[...] = jnp.broadcast_to(ones_ref[2:3, :], vo_ref.shape).astype(BF16)
        for h in range(HEADS):
            sl = slice(DH * h, DH * (h + 1))
            lo = slice(SLAB * h, SLAB * h + DH)
            qh = q_ref[:, sl]
            r = lax.rsqrt(jnp.mean(qh * qh, axis=-1, keepdims=True) + EPS)
            qo_ref[:, lo] = (qh * r * gq_ref[...] * QK_SCALE).astype(BF16)
            kh = k_ref[:, sl]
            r = lax.rsqrt(jnp.mean(kh * kh, axis=-1, keepdims=True) + EPS)
            ko_ref[:, lo] = (kh * r * gk_ref[...]).astype(BF16)
            vo_ref[:, lo] = v_ref[:, sl].astype(BF16)

    eq, ek, ones = _aug_placement()
    o = jax.ShapeDtypeStruct((s, HEADS * SLAB), BF16)
    wide = _row_spec(HEADS * SLAB)
    outs, _ = _call(
        body, (proj, proj, proj, proj, bf_pad, gq, gk, eq, ek, ones), name="qkv_prep", grid=(s // TR,),
        in_specs=[_row_spec(AW, 0), _row_spec(AW, 1), _row_spec(AW, 2), _row_spec(LANES, FG_BLOCK),
                  _full_spec((1, LANES)), _full_spec((1, DH)), _full_spec((1, DH)), _full_spec(eq.shape),
                  _full_spec(ek.shape), _full_spec(ones.shape)],
        out_specs=[wide, wide, wide], out_shape=[o, o, o],
        scratch_shapes=[pltpu.VMEM((SUB, LANES), F32)], vmem_mb=16)
    return outs


def _resid_norm2(x, z, mod, g):
    s = x.shape[0]

    def body(x_ref, z_ref, mod_ref, g_ref, x1_ref, h_ref):
        x1 = x_ref[...] + mod_ref[2:3, :] * z_ref[...]
        x1_ref[...] = x1
        r = lax.rsqrt(jnp.mean(x1 * x1, axis=-1, keepdims=True) + EPS)
        nrm = x1 * r * g_ref[...]
        h_ref[...] = (nrm * (1.0 + mod_ref[4:5, :]) + mod_ref[3:4, :]).astype(BF16)

    return pl.pallas_call(
        body, name="resid_norm2", grid=(s // TRE,),
        in_specs=[_erow(D), _erow(D), _full_spec((SUB, D)), _full_spec((1, D))],
        out_specs=(_erow(D), _erow(D)),
        out_shape=(jax.ShapeDtypeStruct((s, D), F32), jax.ShapeDtypeStruct((s, D), BF16)),
        compiler_params=_params(("parallel",), 24),
    )(x, z, mod, g)


def _loss_head(x1, y, tgt, mod):
    s = x1.shape[0]

    def body(x1_ref, y_ref, t_ref, mod_ref, dout_ref, dy_ref, vec_ref):
        @pl.when(pl.program_id(0) == 0)
        def _():
            vec_ref[...] = jnp.zeros_like(vec_ref)

        yv = y_ref[...]
        g2 = mod_ref[5:6, :]
        diff = x1_ref[...] + g2 * yv - t_ref[...]
        dout = diff * (1.0 / D)
        dout_ref[...] = dout
        dy_ref[...] = (g2 * dout).astype(BF16)
        vec_ref[0:1, :] += jnp.sum(dout * yv, axis=0, keepdims=True)
        vec_ref[1:2, :] += jnp.sum(diff * diff, axis=0, keepdims=True)

    return pl.pallas_call(
        body, name="loss_head", grid=(s // TRE,),
        in_specs=[_erow(D), _erow(D), _erow(D), _full_spec((SUB, D))],
        out_specs=(_erow(D), _erow(D), _full_spec((SUB, D))),
        out_shape=(jax.ShapeDtypeStruct((s, D), F32), jax.ShapeDtypeStruct((s, D), BF16),
                   jax.ShapeDtypeStruct((SUB, D), F32)),
        compiler_params=_params(("arbitrary",), 24),
    )(x1, y, tgt, mod)


def _norm_mod_bwd(dh, xin, dres, zin, mod, g, scale_row, gate_row, name, hosted=None):
    s = dh.shape[0]
    with_gate = gate_row is not None

    def body(*refs):
        if with_gate:
            dh_ref, x_ref, dres_ref, z_ref, mod_ref, g_ref, dx_ref, dz_ref, vec_ref = refs
        else:
            dh_ref, x_ref, dres_ref, mod_ref, g_ref, dx_ref, vec_ref = refs

        @pl.when(pl.program_id(0) == 0)
        def _():
            vec_ref[...] = jnp.zeros_like(vec_ref)

        xv = x_ref[...]
        dhv = dh_ref[...]
        gv = g_ref[...]
        r = lax.rsqrt(jnp.mean(xv * xv, axis=-1, keepdims=True) + EPS)
        xh = xv * r
        dn = dhv * (1.0 + mod_ref[scale_row:scale_row + 1, :])
        dxh = dn * gv
        dx = dres_ref[...] + r * (dxh - xh * jnp.mean(dxh * xh, axis=-1, keepdims=True))
        dx_ref[...] = dx
        vec_ref[0:1, :] += jnp.sum(dhv, axis=0, keepdims=True)
        vec_ref[1:2, :] += jnp.sum(dhv * (xh * gv), axis=0, keepdims=True)
        vec_ref[2:3, :] += jnp.sum(dn * xh, axis=0, keepdims=True)
        if with_gate:
            dz_ref[...] = (mod_ref[gate_row:gate_row + 1, :] * dx).astype(BF16)
            vec_ref[3:4, :] += jnp.sum(dx * z_ref[...], axis=0, keepdims=True)

    ins = [dh, xin, dres] + ([zin] if with_gate else []) + [mod, g]
    in_specs = [_erow(D)] * (4 if with_gate else 3) + [_full_spec((SUB, D)), _full_spec((1, D))]
    out_specs = [_erow(D)] + ([_erow(D)] if with_gate else []) + [_full_spec((SUB, D))]
    out_shape = [jax.ShapeDtypeStruct((s, D), F32)] + ([jax.ShapeDtypeStruct((s, D), BF16)] if with_gate else []) \
        + [jax.ShapeDtypeStruct((SUB, D), F32)]
    outs, moved = _call(body, ins, name=name, grid=(s // TRE,), in_specs=in_specs, out_specs=out_specs,
                        out_shape=out_shape, vmem_mb=32, hosted=hosted)
    return outs + (moved,) if hosted else outs


XIN_BLOCK = 3 * AW // LANES
BG_BLOCK = XIN_BLOCK + CW // LANES
CG_BLOCK = BG_BLOCK + CW // LANES


def _seq_spec(s, first_block):
    return pl.BlockSpec((s, LANES), lambda j, fb=first_block: (0, fb + j))


def _mixconv_fwd(proj, w):
    s = proj.shape[0]

    def body(xin_ref, bg_ref, cg_ref, w_ref, o_ref):
        cx = cg_ref[...] * xin_ref[...]
        cv, _, _ = _conv_taps(cx, None, w_ref[...])
        o_ref[...] = bg_ref[...] * cv

    return pl.pallas_call(
        body, name="mixconv_fwd", grid=(CW // LANES,),
        in_specs=[_seq_spec(s, XIN_BLOCK), _seq_spec(s, BG_BLOCK), _seq_spec(s, CG_BLOCK),
                  pl.BlockSpec((3, LANES), lambda j: (0, j))],
        out_specs=_seq_spec(s, 0), out_shape=jax.ShapeDtypeStruct((s, CW), F32),
        compiler_params=_params(("parallel",), 32),
    )(proj, proj, proj, w)


def _mixconv_bwd(dmixed, proj, w):
    s = proj.shape[0]

    def body(d_ref, xin_ref, bg_ref, cg_ref, w_ref, dxin_ref, dbg_ref, dcg_ref, dw_ref):
        wv = w_ref[...]
        xin, cg, dconv = xin_ref[...], cg_ref[...], d_ref[...]
        cx = cg * xin
        cv, s1, s2 = _conv_taps(cx, None, wv)
        dbg_ref[...] = (dconv * cv).astype(BF16)
        dcv = dconv * bg_ref[...]
        dw_ref[...] = jnp.zeros_like(dw_ref)
        dw_ref[0:1, :] = jnp.sum(dcv * s2, axis=0, keepdims=True)
        dw_ref[1:2, :] = jnp.sum(dcv * s1, axis=0, keepdims=True)
        dw_ref[2:3, :] = jnp.sum(dcv * cx, axis=0, keepdims=True)
        dcx = _conv_taps_t(dcv, None, wv)
        dcg_ref[...] = (dcx * xin).astype(BF16)
        dxin_ref[...] = (dcx * cg).astype(BF16)

    o = jax.ShapeDtypeStruct((s, CW), BF16)
    return pl.pallas_call(
        body, name="mixconv_bwd", grid=(CW // LANES,),
        in_specs=[_seq_spec(s, AW // LANES), _seq_spec(s, XIN_BLOCK), _seq_spec(s, BG_BLOCK), _seq_spec(s, CG_BLOCK),
                  pl.BlockSpec((3, LANES), lambda j: (0, j))],
        out_specs=(_seq_spec(s, 0), _seq_spec(s, 0), _seq_spec(s, 0), pl.BlockSpec((SUB, LANES), lambda j: (0, j))),
        out_shape=(o, o, o, jax.ShapeDtypeStruct((SUB, CW), F32)),
        compiler_params=_params(("parallel",), 32),
    )(dmixed, proj, proj, proj, w)


TA = 512
NEG = -1e30


def _causal_mask():
    row = lax.broadcasted_iota(jnp.int32, (TA, TA), 0)
    col = lax.broadcasted_iota(jnp.int32, (TA, TA), 1)
    return col <= row


def _attn_fwd(qp, kp, vp):
    s = qp.shape[0]
    nq = s // TA

    def body(q_ref, k_ref, v_ref, o_ref, lse_ref):
        i = pl.program_id(1)
        slabs = [slice(SLAB * hh, SLAB * (hh + 1)) for hh in range(2)]
        q = [q_ref[:, sl] for sl in slabs]

        def block(j, carry, masked):
            keys = pl.ds(pl.multiple_of(j * TA, TA), TA)
            ms, acc = carry
            m_out, parts = [], []
            for hh in range(2):
                sc = _dot(q[hh], k_ref[keys, slabs[hh]], NT)
                if masked:
                    sc = jnp.where(_causal_mask(), sc, NEG)
                m_new = jnp.maximum(ms[hh], jnp.max(sc, axis=-1, keepdims=True))
                p = jnp.exp(sc - m_new)
                parts.append(jnp.exp(ms[hh] - m_new) * acc[:, slabs[hh]] + _dot(p.astype(BF16), v_ref[keys, slabs[hh]]))
                m_out.append(m_new)
            return tuple(m_out), jnp.concatenate(parts, axis=1)

        init = ((jnp.full((TA, 1), NEG, F32), jnp.full((TA, 1), NEG, F32)), jnp.zeros((TA, 2 * SLAB), F32))
        carry = lax.fori_loop(0, i, lambda j, cr: block(j, cr, False), init)
        ms, acc = block(i, carry, True)
        for hh in range(2):
            l = acc[:, SLAB * hh + DH:SLAB * hh + DH + 1]
            o_ref[:, DH * hh:DH * (hh + 1)] = acc[:, SLAB * hh:SLAB * hh + DH] / l
            lse_ref[0, :, hh:hh + 1] = ms[hh] + jnp.log(l)

    (o, lse), _ = _call(
        body, (qp, kp, vp), name="attn_fwd", grid=(HEADS // 2, nq),
        in_specs=[pl.BlockSpec((TA, 2 * SLAB), lambda p, i: (i, p)),
                  pl.BlockSpec((s, 2 * SLAB), lambda p, i: (0, p)),
                  pl.BlockSpec((s, 2 * SLAB), lambda p, i: (0, p))],
        out_specs=[pl.BlockSpec((TA, LANES), lambda p, i: (i, p)), pl.BlockSpec((1, TA, 2), lambda p, i: (p, i, 0))],
        out_shape=[jax.ShapeDtypeStruct((s, AW), F32), jax.ShapeDtypeStruct((HEADS // 2, s, 2), F32)],
        vmem_mb=24)
    return o, lse


def _attn_bwd(qp, kp, vp, dmixed, o, lse, hosted):
    s = qp.shape[0]
    nq = s // TA

    def body(q_ref, k_ref, v_ref, do_ref, o_ref, lse_ref, dq_ref, dk_ref, dv_ref, qb_ref, dob_ref):
        dk_ref[...] = jnp.zeros_like(dk_ref)
        dv_ref[...] = jnp.zeros_like(dv_ref)
        slabs = [slice(SLAB * hh, SLAB * (hh + 1)) for hh in range(2)]
        lane = lax.broadcasted_iota(jnp.int32, (TA, DH), 1)

        def q_block(i, _):
            i0 = pl.multiple_of(i * TA, TA)
            rows = pl.ds(i0, TA)
            for hh in range(2):
                half = slice(DH * hh, DH * (hh + 1))
                do = do_ref[rows, half]
                delta = jnp.sum(do * o_ref[rows, half], axis=-1, keepdims=True)
                dob_ref[hh, :, 0:DH] = do.astype(BF16)
                dob_ref[hh, :, DH:SLAB] = _lanes3(lane, 0, [-d for d in _split3(delta)], 0.0).astype(BF16)
                lse3 = _split3(lse_ref[0, rows, hh:hh + 1])
                qb_ref[hh, :, 0:DH] = q_ref[rows, SLAB * hh:SLAB * hh + DH]
                aug = q_ref[rows, SLAB * hh + DH:SLAB * (hh + 1)].astype(F32)
                qb_ref[hh, :, DH:SLAB] = _lanes3(lane, AUG_LSE, [-x for x in lse3], aug).astype(BF16)

            def block(j, dq, masked):
                keys = pl.ds(pl.multiple_of(j * TA, TA), TA)
                dv, dk, dqc = [], [], []
                for hh in range(2):
                    q, dob = qb_ref[hh], dob_ref[hh]
                    k = k_ref[keys, slabs[hh]]
                    sc = _dot(q, k, NT)
                    if masked:
                        sc = jnp.where(_causal_mask(), sc, NEG)
                    p = jnp.exp(sc)
                    dv.append(_dot(p.astype(BF16), dob, TN))
                    ds = (p * _dot(dob, v_ref[keys, slabs[hh]], NT)).astype(BF16)
                    dk.append(_dot(ds, q, TN))
                    dqc.append(_dot(ds, k))
                dv_ref[keys, :] += jnp.concatenate(dv, axis=1)
                dk_ref[keys, :] += jnp.concatenate(dk, axis=1)
                return dq + jnp.concatenate(dqc, axis=1)

            dq = lax.fori_loop(0, i, lambda j, acc: block(j, acc, False), jnp.zeros((TA, 2 * SLAB), F32))
            dq_ref[rows, :] = block(i, dq, True)
            return 0

        lax.fori_loop(0, nq, q_block, 0)

    pair = lambda p: (0, p)
    slab2 = pl.BlockSpec((s, 2 * SLAB), pair)
    seq = pl.BlockSpec((s, LANES), pair)
    small = pl.BlockSpec((1, s, 2), lambda p: (p, 0, 0))
    o32 = jax.ShapeDtypeStruct((s, HEADS * SLAB), F32)
    return _call(
        body, (qp, kp, vp, dmixed, o, lse), name="attn_bwd", grid=(HEADS // 2,),
        in_specs=[slab2, slab2, slab2, seq, seq, small], out_specs=[slab2, slab2, slab2], out_shape=[o32, o32, o32],
        scratch_shapes=[pltpu.VMEM((2, TA, SLAB), BF16), pltpu.VMEM((2, TA, SLAB), BF16)], vmem_mb=40, hosted=hosted)


def _qkv_post(dqp, dkp, dvp, proj, bf_pad, gq, gk):
    s = proj.shape[0]
    nb = s // TR

    def body(dq_ref, dk_ref, dv_ref, q_ref, k_ref, fg_ref, b_ref, gq_ref, gk_ref, dqo_ref, dko_ref, dvo_ref, dfg_ref,
             vec_ref, carry_ref):
        @pl.when(pl.program_id(0) == 0)
        def _():
            vec_ref[...] = jnp.zeros_like(vec_ref)
            carry_ref[...] = jnp.zeros_like(carry_ref)

        def one(d_ref, x_ref, g_ref, o_ref, row, scale):
            dg = jnp.zeros((1, DH), F32)
            for h in range(HEADS):
                sl = slice(DH * h, DH * (h + 1))
                xv = x_ref[:, sl]
                r = lax.rsqrt(jnp.mean(xv * xv, axis=-1, keepdims=True) + EPS)
                xh = xv * r
                dn = d_ref[:, SLAB * h:SLAB * h + DH] * scale
                dg = dg + jnp.sum(dn * xh, axis=0, keepdims=True)
                dxh = dn * g_ref[...]
                o_ref[:, sl] = (r * (dxh - xh * jnp.mean(dxh * xh, axis=-1, keepdims=True))).astype(BF16)
            vec_ref[row:row + 1, 0:DH] += dg

        one(dq_ref, q_ref, gq_ref, dqo_ref, 0, QK_SCALE)
        one(dk_ref, k_ref, gk_ref, dko_ref, 1, 1.0)
        lane = lax.broadcasted_iota(jnp.int32, (TR, LANES), 1)
        df = jnp.zeros((TR, LANES), F32)
        for h in range(HEADS):
            dvo_ref[:, DH * h:DH * (h + 1)] = dv_ref[:, SLAB * h:SLAB * h + DH].astype(BF16)
            row_sum = dq_ref[:, SLAB * h + DH:SLAB * h + DH + 1]
            col_sum = dk_ref[:, SLAB * h + DH + AUG_ONE:SLAB * h + DH + AUG_ONE + 1]
            df = jnp.where(lane == h, row_sum - col_sum, df)
        row = lax.broadcasted_iota(jnp.int32, (TR, TR), 0)
        col = lax.broadcasted_iota(jnp.int32, (TR, TR), 1)
        dlogf = _dot((col >= row).astype(F32), df, NN, lax.Precision.HIGHEST) + carry_ref[0:1, :]
        carry_ref[...] = jnp.broadcast_to(dlogf[0:1, :], carry_ref.shape)
        dfg = dlogf * _sigmoid(-(fg_ref[...] + b_ref[...]))
        dfg_ref[...] = dfg.astype(BF16)
        vec_ref[2:3, :] += jnp.sum(dfg, axis=0, keepdims=True)

    o = jax.ShapeDtypeStruct((s, AW), BF16)
    rev = lambda width, col=0: pl.BlockSpec((TR, width), lambda i, col=col: (nb - 1 - i, col))
    wide = rev(HEADS * SLAB)
    outs, _ = _call(
        body, (dqp, dkp, dvp, proj, proj, proj, bf_pad, gq, gk), name="qkv_post", grid=(nb,),
        in_specs=[wide, wide, wide, rev(AW, 0), rev(AW, 1), rev(LANES, FG_BLOCK), _full_spec((1, LANES)),
                  _full_spec((1, DH)), _full_spec((1, DH))],
        out_specs=[rev(AW), rev(AW), rev(AW), rev(LANES), _full_spec((SUB, LANES))],
        out_shape=[o, o, o, jax.ShapeDtypeStruct((s, LANES), BF16), jax.ShapeDtypeStruct((SUB, LANES), F32)],
        scratch_shapes=[pltpu.VMEM((SUB, LANES), F32)])
    return outs


TF = 256
NJ = DFF // TF
FFN_ROWS_FWD = 1024
FFN_ROWS_BWD = 1024


def _ffn_fwd(h2, wup_t, cw, wd):
    s = h2.shape[0]
    tr = FFN_ROWS_FWD
    nr = s // tr

    def body(h_ref, wu_ref, cg_ref, cv_ref, wd_ref, pg_ref, pv_ref, y_ref, halo_ref, act_ref):
        r, j = pl.program_id(0), pl.program_id(1)
        hv = h_ref[...]
        pg = _dot(hv, wu_ref[0], NT).astype(BF16)
        pv = _dot(hv, wu_ref[1], NT).astype(BF16)
        pg_ref[...] = pg
        pv_ref[...] = pv
        pgf, pvf = pg.astype(F32), pv.astype(F32)
        ug, _, _ = _conv_taps(pgf, jnp.where(r > 0, halo_ref[j, 0], 0.0), cg_ref[...])
        uv, _, _ = _conv_taps(pvf, jnp.where(r > 0, halo_ref[j, 1], 0.0), cv_ref[...])
        halo_ref[j, 0] = pgf[tr - SUB:tr, :]
        halo_ref[j, 1] = pvf[tr - SUB:tr, :]
        act = (ug * _sigmoid(ug) * uv).astype(BF16)
        for t in range(NJ):
            @pl.when(j == t)
            def _(t=t):
                act_ref[:, t * TF:(t + 1) * TF] = act

        @pl.when(j == NJ - 1)
        def _():
            y_ref[...] = _dot(act_ref[...], wd_ref[...])

    pre = jax.ShapeDtypeStruct((s, DFF), BF16)
    return pl.pallas_call(
        body, name="ffn_fwd", grid=(nr, NJ),
        in_specs=[pl.BlockSpec((tr, D), lambda r, j: (r, 0)),
                  pl.BlockSpec((2, TF, D), lambda r, j: (0, j, 0)),
                  pl.BlockSpec((3, TF), lambda r, j: (0, j)),
                  pl.BlockSpec((3, TF), lambda r, j: (0, NJ + j)),
                  pl.BlockSpec((DFF, D), lambda r, j: (0, 0))],
        out_specs=(pl.BlockSpec((tr, TF), lambda r, j: (r, j)),
                   pl.BlockSpec((tr, TF), lambda r, j: (r, j)),
                   pl.BlockSpec((tr, D), lambda r, j: (r, 0))),
        out_shape=(pre, pre, jax.ShapeDtypeStruct((s, D), F32)),
        scratch_shapes=[pltpu.VMEM((NJ, 2, SUB, TF), F32), pltpu.VMEM((tr, DFF), BF16)],
        compiler_params=_params(("arbitrary", "arbitrary"), 56),
    )(h2, wup_t, cw, cw, wd)


def _ffn_bwd(dy, h2, pre_g, pre_v, wup_t, cw, wd):
    s = h2.shape[0]
    tr = FFN_ROWS_BWD
    nr = s // tr
    hb = tr // (2 * SUB)

    def body(dy_ref, h_ref, pg_ref, pv_ref, hg_ref, hv_ref, wu_ref, cg_ref, cv_ref, wd_ref,
             dh_ref, dwu_ref, dwd_ref, dcg_ref, dcv_ref, nxt_ref, awu_ref, awd_ref):
        j, r = pl.program_id(0), pl.program_id(1)
        rr = nr - 1 - r
        row0 = pl.multiple_of(rr * tr, tr)
        cwg, cwv = cg_ref[...], cv_ref[...]
        pg, pv = pg_ref[...].astype(F32), pv_ref[...].astype(F32)
        ug, g1, g2 = _conv_taps(pg, jnp.where(rr > 0, hg_ref[SUB:2 * SUB, :].astype(F32), 0.0), cwg)
        uv, v1, v2 = _conv_taps(pv, jnp.where(rr > 0, hv_ref[SUB:2 * SUB, :].astype(F32), 0.0), cwv)
        sg = _sigmoid(ug)
        sil = ug * sg
        act = (sil * uv).astype(BF16)
        dyv = dy_ref[...]
        da = _dot(dyv, wd_ref[...], NT)
        dug = da * uv * (sg * (1.0 + ug * (1.0 - sg)))
        duv = da * sil
        dpg = _conv_taps_t(dug, jnp.where(r > 0, nxt_ref[0], 0.0), cwg)
        dpv = _conv_taps_t(duv, jnp.where(r > 0, nxt_ref[1], 0.0), cwv)
        nxt_ref[0] = dug[0:SUB, :]
        nxt_ref[1] = duv[0:SUB, :]
        dpgb, dpvb = dpg.astype(BF16), dpv.astype(BF16)
        hv = h_ref[...]
        dwd = _dot(act, dyv, TN)
        dpb = jnp.concatenate([dpgb, dpvb], axis=1)
        dwu = _dot(dpb, hv, TN)
        dh = _dot(dpb, wu_ref[...].reshape(2 * TF, D))

        def taps(du, x0, x1, x2):
            return (jnp.sum(du * x2, axis=0, keepdims=True), jnp.sum(du * x1, axis=0, keepdims=True),
                    jnp.sum(du * x0, axis=0, keepdims=True))

        tg, tv = taps(dug, pg, g1, g2), taps(duv, pv, v1, v2)

        @pl.when(r == 0)
        def _():
            awd_ref[...] = dwd
            awu_ref[...] = dwu
            dcg_ref[...] = jnp.zeros_like(dcg_ref)
            dcv_ref[...] = jnp.zeros_like(dcv_ref)

        @pl.when(r > 0)
        def _():
            awd_ref[...] += dwd
            awu_ref[...] += dwu

        @pl.when(r == nr - 1)
        def _():
            dwd_ref[...] = awd_ref[...].astype(BF16)
            dwu_ref[...] = awu_ref[...].astype(BF16).reshape(2, TF, D)

        for t in range(3):
            dcg_ref[t:t + 1, :] += tg[t]
            dcv_ref[t:t + 1, :] += tv[t]

        @pl.when(j == 0)
        def _():
            dh_ref[pl.ds(row0, tr), :] = dh

        @pl.when(j > 0)
        def _():
            dh_ref[pl.ds(row0, tr), :] += dh

    rows = lambda j, r: (nr - 1 - r, 0)
    tile = lambda j, r: (nr - 1 - r, j)
    halo = lambda j, r: (jnp.maximum((nr - 1 - r) * hb - 1, 0), j)
    return pl.pallas_call(
        body, name="ffn_bwd", grid=(NJ, nr),
        in_specs=[pl.BlockSpec((tr, D), rows), pl.BlockSpec((tr, D), rows),
                  pl.BlockSpec((tr, TF), tile), pl.BlockSpec((tr, TF), tile),
                  pl.BlockSpec((2 * SUB, TF), halo), pl.BlockSpec((2 * SUB, TF), halo),
                  pl.BlockSpec((2, TF, D), lambda j, r: (0, j, 0)),
                  pl.BlockSpec((3, TF), lambda j, r: (0, j)), pl.BlockSpec((3, TF), lambda j, r: (0, NJ + j)),
                  pl.BlockSpec((TF, D), lambda j, r: (j, 0))],
        out_specs=(pl.BlockSpec((s, D), lambda j, r: (0, 0)),
                   pl.BlockSpec((2, TF, D), lambda j, r: (0, j, 0)),
                   pl.BlockSpec((TF, D), lambda j, r: (j, 0)),
                   pl.BlockSpec((SUB, TF), lambda j, r: (0, j)), pl.BlockSpec((SUB, TF), lambda j, r: (0, j))),
        out_shape=(jax.ShapeDtypeStruct((s, D), F32),
                   jax.ShapeDtypeStruct((2, DFF, D), BF16), jax.ShapeDtypeStruct((DFF, D), BF16),
                   jax.ShapeDtypeStruct((SUB, DFF), F32), jax.ShapeDtypeStruct((SUB, DFF), F32)),
        scratch_shapes=[pltpu.VMEM((2, SUB, TF), F32), pltpu.VMEM((2 * TF, D), F32), pltpu.VMEM((TF, D), F32)],
        compiler_params=_params(("arbitrary", "arbitrary"), 56),
    )(dy, h2, pre_g, pre_v, pre_g, pre_v, wup_t, cw, cw, wd)


def _adam(w, g, m, v):
    m = ADAM_B1 * m + (1.0 - ADAM_B1) * g
    v = ADAM_B2 * v + (1.0 - ADAM_B2) * (g * g)
    m_hat = m / (1.0 - ADAM_B1 ** ADAM_STEP)
    v_hat = v / (1.0 - ADAM_B2 ** ADAM_STEP)
    delta = -ADAM_LR * (m_hat / (jnp.sqrt(v_hat) + ADAM_EPS) + ADAM_WD * w)
    return delta, m, v


NCHIP = NDEV // 2


def _pair_add(mine, theirs, tr, name):
    _, _, rws, cols = mine.shape

    def body(a_ref, b_ref, o_ref):
        c = lax.axis_index("c")
        o_ref[0] = (a_ref[0, c].astype(F32) + b_ref[0].astype(F32)).astype(BF16)

    (out,), _ = _call(
        body, (mine, theirs), name=name, grid=(NCHIP, rws // tr),
        in_specs=[pl.BlockSpec((1, 2, tr, cols), lambda q, i: (q, 0, i, 0)),
                  pl.BlockSpec((1, tr, cols), lambda q, i: (q, i, 0))],
        out_specs=[pl.BlockSpec((1, tr, cols), lambda q, i: (q, i, 0))],
        out_shape=[jax.ShapeDtypeStruct((NCHIP, rws, cols), BF16)], vmem_mb=16)
    return out


def _adamw_sharded(parts, w, m, v, tr, name, hosted=None):
    rws, cols = w.shape
    n_parts = parts.shape[0]

    def body(p_ref, w_ref, m_ref, v_ref, g_ref, d_ref, mo_ref, vo_ref):
        g = p_ref[0].astype(F32)
        for q in range(1, n_parts):
            g = g + p_ref[q].astype(F32)
        g_ref[...] = g
        d_ref[...], mo_ref[...], vo_ref[...] = _adam(w_ref[...], g, m_ref[...], v_ref[...])

    blk = pl.BlockSpec((tr, cols), lambda i: (i, 0))
    o = jax.ShapeDtypeStruct((rws, cols), F32)
    outs, moved = _call(
        body, (parts, w, m, v), name=name, grid=(rws // tr,),
        in_specs=[pl.BlockSpec((n_parts, tr, cols), lambda i: (0, i, 0)), blk, blk, blk],
        out_specs=[blk, blk, blk, blk], out_shape=[o, o, o, o], vmem_mb=44 if tr > 256 else 24, hosted=hosted)
    return (outs, moved) if hosted else outs


def _adamw_ada(c_all, dmod_my, w, m, v):
    rws, cols = w.shape
    tr = 256

    def body(c_ref, dm_ref, w_ref, m_ref, v_ref, g_ref, d_ref, mo_ref, vo_ref):
        cv = c_ref[...]
        act = cv * _sigmoid(cv)
        g = _dot(act, dm_ref[...], TN, lax.Precision.HIGHEST)
        g_ref[...] = g
        d_ref[...], mo_ref[...], vo_ref[...] = _adam(w_ref[...], g, m_ref[...], v_ref[...])

    blk = pl.BlockSpec((tr, cols), lambda i: (i, 0))
    o = jax.ShapeDtypeStruct((rws, cols), F32)
    return pl.pallas_call(
        body, name="adamw_ada", grid=(rws // tr,),
        in_specs=[pl.BlockSpec((NDEV, tr), lambda i: (0, i)), _full_spec((NDEV, cols)), blk, blk, blk],
        out_specs=(blk, blk, blk, blk), out_shape=(o, o, o, o),
        compiler_params=_params(("parallel",), 32),
    )(c_all, dmod_my, w, m, v)


REP_ROWS = 16
ROW_N1, ROW_N2, ROW_LOSS, ROW_MISC = 6, 7, 8, 9
LANE_BF, LANE_GQ, LANE_GK = 0, 128, 256


def _adamw_small(rep_all, conv_all, wmv):
    n_ff = wmv[6][0].shape[1]

    def body(*refs):
        rep_ref, conv_ref = refs[:2]
        ins = refs[2:2 + 24]
        outs = refs[2 + 24:]
        loss_ref, outs = outs[0], outs[1:]
        g_rep = rep_ref[0]
        g_conv = conv_ref[0]
        for d in range(1, NDEV):
            g_rep = g_rep + rep_ref[d]
            g_conv = g_conv + conv_ref[d]
        loss_ref[...] = (0.5 / D) * jnp.sum(g_rep[ROW_LOSS:ROW_LOSS + 1, :], axis=-1, keepdims=True)
        grads = [
            None,
            g_rep[ROW_N1:ROW_N1 + 1, :],
            g_rep[ROW_MISC:ROW_MISC + 1, LANE_BF:LANE_BF + HEADS],
            g_rep[ROW_MISC:ROW_MISC + 1, LANE_GQ:LANE_GQ + DH],
            g_rep[ROW_MISC:ROW_MISC + 1, LANE_GK:LANE_GK + DH],
            g_rep[ROW_N2:ROW_N2 + 1, :],
            g_conv[0:3, 0:n_ff],
            g_conv[0:3, n_ff:n_ff + DH],
        ]
        for p in range(8):
            w_ref, m_ref, v_ref = ins[3 * p:3 * p + 3]
            g_ref, d_ref, mo_ref, vo_ref = outs[4 * p:4 * p + 4]
            if p == 0:
                for nmod in range(NMOD):
                    sl = slice(D * nmod, D * (nmod + 1))
                    g = g_rep[nmod:nmod + 1, :]
                    g_ref[:, sl] = g
                    d_ref[:, sl], mo_ref[:, sl], vo_ref[:, sl] = _adam(w_ref[:, sl], g, m_ref[:, sl], v_ref[:, sl])
            else:
                g = grads[p]
                g_ref[...] = g
                d_ref[...], mo_ref[...], vo_ref[...] = _adam(w_ref[...], g, m_ref[...], v_ref[...])

    flat = [a for trio in wmv for a in trio]
    out_shape = [jax.ShapeDtypeStruct((1, 1), F32)]
    for trio in wmv:
        out_shape += [jax.ShapeDtypeStruct(trio[0].shape, F32)] * 4
    ins = [rep_all, conv_all] + flat
    return pl.pallas_call(
        body, name="adamw_small", grid=(1,),
        in_specs=[_full_spec(a.shape) for a in ins], out_specs=tuple(_full_spec(o.shape) for o in out_shape),
        out_shape=tuple(out_shape), compiler_params=_params(("arbitrary",), 32),
    )(*ins)


FG_FIRST = 3 * AW
N_IN = DIN // NDEV


def _w_in_runs():
    runs = []
    for d in range(NDEV):
        lo, hi = N_IN * d, N_IN * (d + 1)
        for a, b, shift in ((0, FG_FIRST, 0), (FG_FIRST, FG_FIRST + HEADS, DIN - HEADS - FG_FIRST),
                            (FG_FIRST + HEADS, DIN, -HEADS)):
            a, b = max(a, lo), min(b, hi)
            if a < b:
                runs.append((d, a - lo, a + shift, b - a))
    return runs


W_IN_ROWS = 256
N_IN_PAD = 512


def _identity(n):
    return (lax.broadcasted_iota(jnp.int32, (n, n), 0) == lax.broadcasted_iota(jnp.int32, (n, n), 1)).astype(BF16)


def _assemble_w_in(g_in, hosted):
    def body(g_ref, o_ref, t_ref):
        eye = _identity(W_IN_ROWS)
        shard = None
        for d, src, dst, width in _w_in_runs():
            if d != shard:
                t_ref[:, 0:N_IN] = _dot(eye, g_ref[d], NT).astype(BF16)
                shard = d
            o_ref[:, dst:dst + width] = t_ref[:, src:src + width]
        o_ref[:, DIN:DINP] = jnp.zeros((W_IN_ROWS, DINP - DIN), o_ref.dtype)

    (out,), moved = _call(
        body, (g_in,), name="assemble_w_in", grid=(D // W_IN_ROWS,),
        in_specs=[pl.BlockSpec((NDEV, N_IN, W_IN_ROWS), lambda i: (0, 0, i))],
        out_specs=[pl.BlockSpec((W_IN_ROWS, DINP), lambda i: (i, 0))],
        out_shape=[jax.ShapeDtypeStruct((D, DINP), g_in.dtype)],
        scratch_shapes=[pltpu.VMEM((W_IN_ROWS, N_IN_PAD), BF16)], vmem_mb=16, hosted=hosted)
    return out, moved


def _scatter_dw_in(dwp):
    def body(w_ref, o_ref, t_ref):
        eye = _identity(W_IN_ROWS)
        runs = _w_in_runs()
        for i, (d, src, dst, width) in enumerate(runs):
            t_ref[:, src:src + width] = w_ref[:, dst:dst + width]
            if i + 1 == len(runs) or runs[i + 1][0] != d:
                o_ref[d // 2, d % 2] = _dot(t_ref[:, 0:N_IN], eye, TN).astype(BF16)

    (out,), _ = _call(
        body, (dwp,), name="scatter_dw_in", grid=(D // W_IN_ROWS,),
        in_specs=[pl.BlockSpec((W_IN_ROWS, DINP), lambda i: (i, 0))],
        out_specs=[pl.BlockSpec((NCHIP, 2, N_IN, W_IN_ROWS), lambda i: (0, 0, 0, i))],
        out_shape=[jax.ShapeDtypeStruct((NCHIP, 2, N_IN, D), dwp.dtype)],
        scratch_shapes=[pltpu.VMEM((W_IN_ROWS, N_IN_PAD), BF16)], vmem_mb=16)
    return out


def kernel(x, c, w_ada, b_ada, norm1_g, w_in, b_forget, q_norm_g, k_norm_g, conv_mix_w, w_out, norm2_g, w_up, ffn_conv_w, w_down, loss_target, m_w_ada, m_b_ada, m_norm1_g, m_w_in, m_b_forget, m_q_norm_g, m_k_norm_g, m_conv_mix_w, m_w_out, m_norm2_g, m_w_up, m_ffn_conv_w, m_w_down, v_w_ada, v_b_ada, v_norm1_g, v_w_in, v_b_forget, v_q_norm_g, v_k_norm_g, v_conv_mix_w, v_w_out, v_norm2_g, v_w_up, v_ffn_conv_w, v_w_down):
    me = 4 * lax.axis_index("x") + 2 * lax.axis_index("y") + lax.axis_index("c")
    xs, tgt = x[0], loss_target[0]
    n_ada = w_ada.shape[2]
    n_ff = w_up.shape[2]

    conv_w = jnp.concatenate([ffn_conv_w[0], conv_mix_w[0]], axis=1)
    conv_w = jnp.concatenate([conv_w, jnp.zeros((SUB - 3, conv_w.shape[1]), F32)], axis=0)
    c_all, conv_all, g_in = _exchange(
        [(c.reshape(SUB, D // SUB), "ag"), (conv_w, "ag"), (jnp.transpose(w_in[0]).astype(BF16), "ag2")],
        "exchange_w_in")
    g_in, w_out_b, w_up_b, w_down_b = lax.optimization_barrier(
        (g_in, w_out[0].astype(BF16), jnp.transpose(w_up[0]).astype(BF16), w_down[0].astype(BF16)))
    g_out, g_up, g_down = _sequencer_exchange(
        [(w_out_b, "ag2"), (w_up_b, "ag2"), (w_down_b, "ag2")], "gather_weights", collective_id=1)
    c_all = c_all.reshape(NDEV, D)
    cw_ffn = jnp.transpose(conv_all[:, :3, :n_ff], (1, 0, 2)).reshape(3, 2 * DFF)
    cw_mix = jnp.transpose(conv_all[:, :3, n_ff:], (1, 0, 2)).reshape(3, CW)

    b_my = lax.dynamic_slice(b_ada, (0, me * n_ada), (1, n_ada))
    mod_part = _ada_fwd(c_all, w_ada[0], b_my)
    w_in_p, (mod_rows,) = _assemble_w_in(
        g_in, [(jnp.broadcast_to(mod_part[:, None, :], (NDEV, SUB, n_ada)), "a2a")])
    mod = mod_rows[:, 0, :].reshape(NMOD, D)
    mod = jnp.concatenate([mod, jnp.zeros((SUB - NMOD, D), F32)], axis=0)

    h = _norm_mod_fwd(xs, mod, norm1_g)
    proj = _mm(h, w_in_p, "nn", F32, 1024, 640, "proj_fwd")
    bf_pad = jnp.concatenate([b_forget, jnp.zeros((1, LANES - HEADS), F32)], axis=1)
    qp, kp, vp = _qkv_prep(proj, bf_pad, q_norm_g, k_norm_g)
    attn, lse = _attn_fwd(qp, kp, vp)
    w_out_f = g_out.reshape(D, D)
    w_up_t = g_up.reshape(2, DFF, D)
    w_down_f = g_down.reshape(DFF, D)
    conv = _mixconv_fwd(proj, cw_mix)
    mixed = jnp.concatenate([attn, conv], axis=1).astype(BF16)
    z = _mm(mixed, w_out_f, "nn", F32, 1024, 1024, "out_fwd")
    x1, h2 = _resid_norm2(xs, z, mod, norm2_g)
    pre_g, pre_v, y = _ffn_fwd(h2, w_up_t, cw_ffn, w_down_f)
    dout, dy, vec_l = _loss_head(x1, y, tgt, mod)

    dh2, dwup_t, dwd, dcw_g, dcw_v = _ffn_bwd(dy, h2, pre_g, pre_v, w_up_t, cw_ffn, w_down_f)
    s_down = dwd.reshape(NCHIP, 2, DFF // NDEV, D)
    s_up = dwup_t.reshape(NCHIP, 2, n_ff, D)
    dx1, dz, vec_2, (t_up, t_down) = _norm_mod_bwd(dh2, x1, dout, z, mod, norm2_g, 4, 2, "norm2_bwd",
                                                   hosted=[(s_up, "pair"), (s_down, "pair")])
    dwout = _mm(mixed, dz, "tn", BF16, 1024, 1024, "out_bwd_w")
    s_out = dwout.reshape(NCHIP, 2, D // NDEV, D)
    dmixed, (t_out,) = _mm(dz, w_out_f, "nt", F32, 1024, 1024, "out_bwd_x", hosted=[(s_out, "pair")])
    c_out = _pair_add(s_out, t_out, 128, "pair_add_out")
    c_up = _pair_add(s_up, t_up, 176, "pair_add_up")
    c_down = _pair_add(s_down, t_down, 176, "pair_add_down")
    dxin, dbg, dcg, dcw_mix = _mixconv_bwd(dmixed, proj, cw_mix)
    (dqp, dkp, dvp), (p_up, p_down, p_out) = _attn_bwd(
        qp, kp, vp, dmixed, attn, lse, [(c_up, "chips"), (c_down, "chips"), (c_out, "chips")])
    dq, dk, dvb, dfg, vec_qk = _qkv_post(dqp, dkp, dvp, proj, bf_pad, q_norm_g, k_norm_g)
    dproj = jnp.concatenate([dq, dk, dvb, dxin, dbg, dcg, dfg], axis=1)
    dwin_p = _mm(h, dproj, "tn", BF16, 1024, 640, "proj_bwd_w")
    s_in = _scatter_dw_in(dwin_p).reshape(NDEV, N_IN, D)
    (p_in,) = _sequencer_exchange([(s_in, "a2a")], "scatter_dw_in_partials", collective_id=2, all_peers=True)
    dh = _mm(dproj, w_in_p, "nt", F32, 1024, 512, "proj_bwd_x", vmem_mb=36)
    grad_x, vec_1 = _norm_mod_bwd(dh, xs, dx1, None, mod, norm1_g, 1, None, "norm1_bwd")

    gap = lambda n: jnp.zeros((1, n), F32)
    misc = jnp.concatenate([
        vec_qk[2:3, :HEADS], gap(LANE_GQ - LANE_BF - HEADS), vec_qk[0:1, :DH], gap(LANE_GK - LANE_GQ - DH),
        vec_qk[1:2, :DH], gap(D - LANE_GK - DH)], axis=1)
    rep = jnp.concatenate([
        vec_1[0:1], vec_1[1:2], vec_2[3:4], vec_2[0:1], vec_2[1:2], vec_l[0:1],
        vec_1[2:3], vec_2[2:3], vec_l[1:2], misc, jnp.zeros((REP_ROWS - 10, D), F32)], axis=0)
    dcw_ffn = jnp.concatenate([dcw_g, dcw_v], axis=1).reshape(SUB, NDEV, n_ff)
    dcw_all = jnp.concatenate([jnp.transpose(dcw_ffn, (1, 0, 2)),
                               jnp.transpose(dcw_mix.reshape(SUB, NDEV, DH), (1, 0, 2))], axis=2)
    r_up = _adamw_sharded(p_up, jnp.transpose(w_up[0]), jnp.transpose(m_w_up[0]), jnp.transpose(v_w_up[0]), 176,
                          "adamw_up")
    r_down = _adamw_sharded(p_down, w_down[0], m_w_down[0], v_w_down[0], 176, "adamw_down")
    rep, dcw_all, r_up, r_down = lax.optimization_barrier((rep, dcw_all, r_up, r_down))
    r_up = tuple(jnp.transpose(a) for a in r_up)
    r_out, (rep_all, conv_parts) = _adamw_sharded(p_out, w_out[0], m_w_out[0], v_w_out[0], 128, "adamw_out",
                                                  hosted=[(rep, "ag"), (dcw_all, "a2a")])
    dmod_my = lax.dynamic_slice(rep_all[:, :NMOD, :].reshape(NDEV, NMOD * D), (0, me * n_ada), (NDEV, n_ada))
    r_ada = _adamw_ada(c_all, dmod_my, w_ada[0], m_w_ada[0], v_w_ada[0])
    r_in = _adamw_sharded(p_in, jnp.transpose(w_in[0]), jnp.transpose(m_w_in[0]), jnp.transpose(v_w_in[0]), N_IN,
                          "adamw_in")
    r_in = tuple(jnp.transpose(a) for a in r_in)
    small = _adamw_small(rep_all, conv_parts, [
        [b_ada, m_b_ada, v_b_ada], [norm1_g, m_norm1_g, v_norm1_g], [b_forget, m_b_forget, v_b_forget],
        [q_norm_g, m_q_norm_g, v_q_norm_g], [k_norm_g, m_k_norm_g, v_k_norm_g], [norm2_g, m_norm2_g, v_norm2_g],
        [ffn_conv_w[0], m_ffn_conv_w[0], v_ffn_conv_w[0]], [conv_mix_w[0], m_conv_mix_w[0], v_conv_mix_w[0]]])
    loss = small[0].reshape(())
    r_bada, r_n1, r_bf, r_gq, r_gk, r_n2, r_cf, r_cm = [small[1 + 4 * p:5 + 4 * p] for p in range(8)]
    lead = lambda t: tuple(a[None] for a in t)
    per_w = [lead(r_ada), r_bada, r_n1, lead(r_in), r_bf, r_gq, r_gk, lead(r_cm), lead(r_out), r_n2,
             lead(r_up), lead(r_cf), lead(r_down)]
    outs = [loss, grad_x[None]]
    for field in range(4):
        outs += [t[field] for t in per_w]
    return tuple(outs)
```

```python
import functools

import jax
import jax.numpy as jnp
import numpy as np
from jax import lax
from jax.experimental import pallas as pl
from jax.experimental.pallas import tpu as pltpu
from jax.experimental.pallas import tpu_sc as plsc

F32 = jnp.float32
BF16 = jnp.bfloat16

NDEV = 8
D = 1024
HEADS = 8
DH = 64
AW = 512
CW = 512
DFF = 2816
DIN = 3080
DINP = 3200
NMOD = 6
EPS = 1e-6
QK_SCALE = 0.125
LANES = 128
SUB = 8

ADAM_LR = 0.001
ADAM_B1 = 0.9
ADAM_B2 = 0.999
ADAM_EPS = 1e-08
ADAM_WD = 0.01
ADAM_STEP = 10

MESH = pl.DeviceIdType.MESH
ANY = pl.BlockSpec(memory_space=pl.ANY)

NN = (((1,), (0,)), ((), ()))
NT = (((1,), (1,)), ((), ()))
TN = (((0,), (0,)), ((), ()))


def _dot(a, b, dims=NN, precision=None):
    return lax.dot_general(a, b, dims, precision=precision, preferred_element_type=F32)


def _params(sem=None, vmem_mb=None):
    kw = {}
    if sem is not None:
        kw["dimension_semantics"] = sem
    if vmem_mb is not None:
        kw["vmem_limit_bytes"] = vmem_mb * 1024 * 1024
    return pltpu.CompilerParams(**kw)


def _sigmoid(x):
    return 0.5 * jnp.tanh(0.5 * x) + 0.5


class _Exchange:
    def __init__(self, items):
        self.arrays = [pltpu.with_memory_space_constraint(a, pltpu.HBM) for a, _ in items]
        self.modes = [m for _, m in items]
        self.n = len(items)
        self.out_shape = []
        for a, m in items:
            sh = {"ag": (NDEV,) + a.shape, "ag2": (NDEV,) + a.shape, "pair": a.shape[:1] + a.shape[2:]}.get(m, a.shape)
            self.out_shape.append(jax.ShapeDtypeStruct(sh, a.dtype))
        self.scratch = [pltpu.SemaphoreType.DMA((self.n, NDEV - 1)), pltpu.SemaphoreType.DMA((self.n, NDEV - 1)),
                        pltpu.SemaphoreType.DMA((self.n,))]

    def _plan(self, srcs, outs, sems):
        send_sems, recv_sems, loc_sems = sems
        x, y, c = lax.axis_index("x"), lax.axis_index("y"), lax.axis_index("c")
        me, my_chip = 4 * x + 2 * y + c, 2 * x + y
        sib = (x, y, 1 - c)
        local, first, landed, forwards, arrivals = [], [], [], [], []

        def remote(a, k, src, dst, to):
            return pltpu.make_async_remote_copy(src_ref=src, dst_ref=dst, send_sem=send_sems.at[a, k],
                                                recv_sem=recv_sems.at[a, k], device_id=to, device_id_type=MESH)

        for a, mode in enumerate(self.modes):
            src, out = srcs[a], outs[a]
            if mode in ("ag", "a2a"):
                piece = (lambda slot, src=src: src) if mode == "ag" else (lambda slot, src=src: src.at[slot])
                local.append(pltpu.make_async_copy(piece(me), out.at[me], loc_sems.at[a]))
                for r in range(1, NDEV):
                    px = 1 - x if (r >> 2) & 1 else x
                    py = 1 - y if (r >> 1) & 1 else y
                    pc = 1 - c if r & 1 else c
                    pidx = 4 * px + 2 * py + pc
                    first.append(remote(a, r - 1, piece(pidx), out.at[me], (px, py, pc)))
                    arrivals.append(remote(a, r - 1, piece(pidx), out.at[pidx], (px, py, pc)))
            elif mode == "ag2":
                local.append(pltpu.make_async_copy(src, out.at[me], loc_sems.at[a]))
                first.append(remote(a, 0, src, out.at[me], sib))
                arrivals.append(remote(a, 0, src, out.at[me + 1 - 2 * c], sib))
                for j, (px, py) in enumerate([(1 - x, y), (x, 1 - y), (1 - x, 1 - y)]):
                    theirs = out.at[4 * px + 2 * py + c]
                    first.append(remote(a, 1 + j, src, out.at[me], (px, py, c)))
                    landed.append(remote(a, 1 + j, src, theirs, (px, py, c)))
                    forwards.append(remote(a, 4 + j, theirs, theirs, sib))
                    arrivals.append(remote(a, 4 + j, src, out.at[4 * px + 2 * py + 1 - c], sib))
            elif mode == "pair":
                for q in range(NDEV // 2):
                    first.append(remote(a, q, src.at[q, 1 - c], out.at[q], sib))
                    arrivals.append(remote(a, q, src.at[q, 1 - c], out.at[q], sib))
            else:
                assert mode == "chips", mode
                local.append(pltpu.make_async_copy(src.at[my_chip], out.at[my_chip], loc_sems.at[a]))
                for j, (px, py) in enumerate([(1 - x, y), (x, 1 - y), (1 - x, 1 - y)]):
                    q = 2 * px + py
                    first.append(remote(a, 1 + j, src.at[q], out.at[my_chip], (px, py, c)))
                    arrivals.append(remote(a, 1 + j, src.at[q], out.at[q], (px, py, c)))
        return local, first, landed, forwards, arrivals

    def start(self, srcs, outs, sems):
        local, first, _, _, _ = self._plan(srcs, outs, sems)
        for cp in local + first:
            cp.start()

    def wait(self, srcs, outs, sems):
        local, first, landed, forwards, arrivals = self._plan(srcs, outs, sems)
        for cp, fwd in zip(landed, forwards):
            cp.wait_recv()
            fwd.start()
        for cp in arrivals:
            cp.wait_recv()
        for cp in first + forwards:
            cp.wait_send()
        for cp in local:
            cp.wait()


def _exchange(items, name):
    ex = _Exchange(items)
    n = ex.n

    def body(*refs):
        srcs, outs, sems = refs[:n], refs[n:2 * n], refs[2 * n:]
        ex.start(srcs, outs, sems)
        ex.wait(srcs, outs, sems)

    return pl.pallas_call(
        body, name=name,
        out_shape=tuple(ex.out_shape),
        in_specs=[ANY] * n, out_specs=tuple([ANY] * n),
        scratch_shapes=ex.scratch,
        compiler_params=pltpu.CompilerParams(has_side_effects=True),
    )(*ex.arrays)


def _sequencer_exchange(items, name, collective_id, all_peers=False):
    ex = _Exchange(items)
    srcs = [jax.new_ref(a, memory_space=pltpu.MemorySpace.HBM) for a in ex.arrays]
    outs = [jax.empty_ref(sh, memory_space=pltpu.MemorySpace.HBM) for sh in ex.out_shape]

    @pl.kernel(mesh=plsc.ScalarSubcoreMesh(axis_name="sequencer", num_cores=1), name=name,
               scratch_types=tuple(ex.scratch), compiler_params=pltpu.CompilerParams(collective_id=collective_id))
    def launch(send_sems, recv_sems, loc_sems):
        x, y, c = lax.axis_index("x"), lax.axis_index("y"), lax.axis_index("c")
        barrier = pltpu.get_barrier_semaphore()
        peers = [(x, y, 1 - c), (1 - x, y, c), (x, 1 - y, c), (1 - x, 1 - y, c)]
        if all_peers:
            peers += [(1 - x, y, 1 - c), (x, 1 - y, 1 - c), (1 - x, 1 - y, 1 - c)]
        for peer in peers:
            pl.semaphore_signal(barrier, inc=1, device_id=peer, device_id_type=MESH)
        pl.semaphore_wait(barrier, len(peers))
        sems = (send_sems, recv_sems, loc_sems)
        ex.start(srcs, outs, sems)
        ex.wait(srcs, outs, sems)

    launch()
    return [o[...] for o in outs]


def _call(body, inputs, *, name, grid, in_specs, out_specs, out_shape, scratch_shapes=(), vmem_mb=None, hosted=None):
    out_specs, out_shape, scratch_shapes = tuple(out_specs), tuple(out_shape), list(scratch_shapes)
    if not hosted:
        res = pl.pallas_call(
            body, name=name, grid=grid, in_specs=list(in_specs), out_specs=out_specs, out_shape=out_shape,
            scratch_shapes=scratch_shapes, compiler_params=_params(("arbitrary",) * len(grid), vmem_mb),
        )(*inputs)
        return tuple(res), ()
    ex = _Exchange(hosted)
    n, n_in, n_out, n_scr = ex.n, len(inputs), len(out_shape), len(scratch_shapes)

    def hosting_body(*refs):
        ins, srcs = refs[:n_in], refs[n_in:n_in + n]
        outs, landing = refs[n_in + n:n_in + n + n_out], refs[n_in + n + n_out:n_in + 2 * n + n_out]
        scratch, sems = refs[n_in + 2 * n + n_out:n_in + 2 * n + n_out + n_scr], refs[n_in + 2 * n + n_out + n_scr:]
        first = functools.reduce(jnp.logical_and, [pl.program_id(d) == 0 for d in range(len(grid))])
        last = functools.reduce(jnp.logical_and, [pl.program_id(d) == grid[d] - 1 for d in range(len(grid))])

        @pl.when(first)
        def _():
            ex.start(srcs, landing, sems)

        body(*ins, *outs, *scratch)

        @pl.when(last)
        def _():
            ex.wait(srcs, landing, sems)

    res = pl.pallas_call(
        hosting_body, name=name, grid=grid,
        in_specs=list(in_specs) + [ANY] * n, out_specs=out_specs + tuple([ANY] * n),
        out_shape=out_shape + tuple(ex.out_shape), scratch_shapes=scratch_shapes + ex.scratch,
        compiler_params=_params(("arbitrary",) * len(grid), vmem_mb),
    )(*inputs, *ex.arrays)
    return tuple(res[:n_out]), tuple(res[n_out:])


def _mm(a, b, mode, out_dtype, tm, tn, name, hosted=None, vmem_mb=24):
    if mode == "nn":
        (m, k), n = a.shape, b.shape[1]
        a_spec = pl.BlockSpec((tm, k), lambda i, j: (i, 0))
        b_spec = pl.BlockSpec((k, tn), lambda i, j: (0, j))
        dims = NN
    elif mode == "nt":
        (m, k), n = a.shape, b.shape[0]
        a_spec = pl.BlockSpec((tm, k), lambda i, j: (i, 0))
        b_spec = pl.BlockSpec((tn, k), lambda i, j: (j, 0))
        dims = NT
    else:
        (k, m), n = a.shape, b.shape[1]
        a_spec = pl.BlockSpec((k, tm), lambda i, j: (0, i))
        b_spec = pl.BlockSpec((k, tn), lambda i, j: (0, j))
        dims = TN
    assert m % tm == 0 and n % tn == 0, (m, n, tm, tn)

    def body(a_ref, b_ref, o_ref):
        o_ref[...] = _dot(a_ref[...], b_ref[...], dims).astype(o_ref.dtype)

    (out,), moved = _call(
        body, (a, b), name=name, grid=(m // tm, n // tn),
        in_specs=[a_spec, b_spec], out_specs=[pl.BlockSpec((tm, tn), lambda i, j: (i, j))],
        out_shape=[jax.ShapeDtypeStruct((m, n), out_dtype)], vmem_mb=vmem_mb, hosted=hosted)
    return (out, moved) if hosted else out


def _shift_down(x, k, fill):
    y = pltpu.roll(x, k, 0)
    row = lax.broadcasted_iota(jnp.int32, (SUB, x.shape[1]), 0)
    head = y[0:SUB, :]
    for t in range(k):
        head = jnp.where(row == t, fill[t], head)
    return jnp.concatenate([head, y[SUB:, :]], axis=0)


def _shift_up(x, k, fill):
    n = x.shape[0]
    y = pltpu.roll(x, n - k, 0)
    row = lax.broadcasted_iota(jnp.int32, (SUB, x.shape[1]), 0)
    tail = y[n - SUB:, :]
    for t in range(k):
        tail = jnp.where(row == SUB - k + t, fill[t], tail)
    return jnp.concatenate([y[:n - SUB, :], tail], axis=0)


def _conv_taps(x, halo, w):
    if halo is None:
        f1, f2 = [0.0], [0.0, 0.0]
    else:
        f1, f2 = [halo[7:8, :]], [halo[6:7, :], halo[7:8, :]]
    s1 = _shift_down(x, 1, f1)
    s2 = _shift_down(x, 2, f2)
    u = w[2:3, :] * x + w[1:2, :] * s1 + w[0:1, :] * s2
    return u, s1, s2


def _conv_taps_t(du, nxt, w):
    if nxt is None:
        f1, f2 = [0.0], [0.0, 0.0]
    else:
        f1, f2 = [nxt[0:1, :]], [nxt[0:1, :], nxt[1:2, :]]
    return w[2:3, :] * du + w[1:2, :] * _shift_up(du, 1, f1) + w[0:1, :] * _shift_up(du, 2, f2)


def _ada_fwd(c_all, w_ada, b_my):
    def body(c_ref, w_ref, b_ref, o_ref):
        cv = c_ref[...]
        act = cv * _sigmoid(cv)
        o_ref[...] = _dot(act, w_ref[...], NN, lax.Precision.HIGHEST) + b_ref[...]

    out = jax.ShapeDtypeStruct((NDEV, w_ada.shape[1]), F32)
    return pl.pallas_call(
        body, name="ada_fwd", grid=(1,),
        in_specs=[_full_spec(c_all.shape), _full_spec(w_ada.shape), _full_spec(b_my.shape)],
        out_specs=_full_spec(out.shape), out_shape=out, compiler_params=_params(("arbitrary",), 32),
    )(c_all, w_ada, b_my)


TR = 256
TRE = 512


def _row_spec(width, col=0, rows=TR):
    return pl.BlockSpec((rows, width), lambda i, col=col: (i, col))


def _erow(width):
    return _row_spec(width, rows=TRE)


def _full_spec(shape):
    return pl.BlockSpec(shape, lambda i: (0,) * len(shape))


def _norm_mod_fwd(x, mod, g):
    s = x.shape[0]

    def body(x_ref, mod_ref, g_ref, h_ref):
        xv = x_ref[...]
        r = lax.rsqrt(jnp.mean(xv * xv, axis=-1, keepdims=True) + EPS)
        nrm = xv * r * g_ref[...]
        h_ref[...] = (nrm * (1.0 + mod_ref[1:2, :]) + mod_ref[0:1, :]).astype(BF16)

    return pl.pallas_call(
        body, name="norm1_fwd", grid=(s // TRE,),
        in_specs=[_erow(D), _full_spec((SUB, D)), _full_spec((1, D))],
        out_specs=_erow(D), out_shape=jax.ShapeDtypeStruct((s, D), BF16),
        compiler_params=_params(("parallel",), 16),
    )(x, mod, g)


SLAB = 2 * DH
AUG_F, AUG_ONE, AUG_LSE = 0, 3, 6


def _split3(x):
    hi = x.astype(BF16).astype(F32)
    r1 = x - hi
    mid = r1.astype(BF16).astype(F32)
    return hi, mid, r1 - mid


def _lanes3(lane, first, pieces, other):
    out = other
    for k in range(3):
        out = jnp.where(lane == first + k, pieces[k], out)
    return out


def _aug_placement():
    eq = np.zeros((3 * LANES, HEADS * SLAB), np.float32)
    ek = np.zeros((3 * LANES, HEADS * SLAB), np.float32)
    ones = np.zeros((SUB, HEADS * SLAB), np.float32)
    for h in range(HEADS):
        aug = SLAB * h + DH
        for k in range(3):
            eq[LANES * k + h, aug + AUG_F + k] = 1.0
            ek[LANES * k + h, aug + AUG_ONE + k] = -1.0
            ones[0, aug + AUG_ONE + k] = 1.0
            ones[1, aug + AUG_F + k] = ones[1, aug + AUG_LSE + k] = 1.0
            ones[2, aug + k] = 1.0
    return jnp.asarray(eq, BF16), jnp.asarray(ek, BF16), jnp.asarray(ones)


FG_BLOCK = (3 * AW + 3 * CW) // LANES


def _qkv_prep(proj, bf_pad, gq, gk, cw_mix):
    s = proj.shape[0]

    def body(q_ref, k_ref, v_ref, fg_ref, b_ref, gq_ref, gk_ref, eq_ref, ek_ref, ones_ref, xin_ref, bg_ref, cg_ref,
             cw_ref, qo_ref, ko_ref, vo_ref, conv_ref, carry_ref, halo_ref):
        first = pl.program_id(0) == 0

        @pl.when(first)
        def _():
            carry_ref[...] = jnp.zeros_like(carry_ref)

        cx = cg_ref[...] * xin_ref[...]
        cv, _, _ = _conv_taps(cx, jnp.where(first, 0.0, halo_ref[...]), cw_ref[...])
        conv_ref[...] = bg_ref[...] * cv
        halo_ref[...] = cx[TR - SUB:TR, :]

        z = fg_ref[...] + b_ref[...]
        logf = jnp.minimum(z, 0.0) - jnp.log1p(jnp.exp(-jnp.abs(z)))
        row = lax.broadcasted_iota(jnp.int32, (TR, TR), 0)
        col = lax.broadcasted_iota(jnp.int32, (TR, TR), 1)
        fcum = _dot((col <= row).astype(F32), logf, NN, lax.Precision.HIGHEST) + carry_ref[0:1, :]
        carry_ref[...] = jnp.broadcast_to(fcum[TR - 1:TR, :], carry_ref.shape)
        f3 = jnp.concatenate(_split3(fcum), axis=1).astype(BF16)
        qo_ref[...] = (_dot(f3, eq_ref[...]) + ones_ref[0:1, :]).astype(BF16)
        ko_ref[...] = (_dot(f3, ek_ref[...]) + ones_ref[1:2, :]).astype(BF16)
        vo_ref[...] = jnp.broadcast_to(ones_ref[2:3, :], vo_ref.shape).astype(BF16)
        for h in range(HEADS):
            sl = slice(DH * h, DH * (h + 1))
            lo = slice(SLAB * h, SLAB * h + DH)
            qh = q_ref[:, sl]
            r = lax.rsqrt(jnp.mean(qh * qh, axis=-1, keepdims=True) + EPS)
            qo_ref[:, lo] = (qh * r * gq_ref[...] * QK_SCALE).astype(BF16)
            kh = k_ref[:, sl]
            r = lax.rsqrt(jnp.mean(kh * kh, axis=-1, keepdims=True) + EPS)
            ko_ref[:, lo] = (kh * r * gk_ref[...]).astype(BF16)
            vo_ref[:, lo] = v_ref[:, sl].astype(BF16)

    eq, ek, ones = _aug_placement()
    o = jax.ShapeDtypeStruct((s, HEADS * SLAB), BF16)
    wide = _row_spec(HEADS * SLAB)
    outs, _ = _call(
        body, (proj, proj, proj, proj, bf_pad, gq, gk, eq, ek, ones, proj, proj, proj, cw_mix), name="qkv_prep",
        grid=(s // TR,),
        in_specs=[_row_spec(AW, 0), _row_spec(AW, 1), _row_spec(AW, 2), _row_spec(LANES, FG_BLOCK),
                  _full_spec((1, LANES)), _full_spec((1, DH)), _full_spec((1, DH)), _full_spec(eq.shape),
                  _full_spec(ek.shape), _full_spec(ones.shape),
                  _row_spec(CW, 3), _row_spec(CW, 4), _row_spec(CW, 5), _full_spec((3, CW))],
        out_specs=[wide, wide, wide, _row_spec(CW)],
        out_shape=[o, o, o, jax.ShapeDtypeStruct((s, CW), F32)],
        scratch_shapes=[pltpu.VMEM((SUB, LANES), F32), pltpu.VMEM((SUB, CW), F32)], vmem_mb=24)
    return outs


def _resid_norm2(x, z, mod, g):
    s = x.shape[0]

    def body(x_ref, z_ref, mod_ref, g_ref, x1_ref, h_ref):
        x1 = x_ref[...] + mod_ref[2:3, :] * z_ref[...]
        x1_ref[...] = x1
        r = lax.rsqrt(jnp.mean(x1 * x1, axis=-1, keepdims=True) + EPS)
        nrm = x1 * r * g_ref[...]
        h_ref[...] = (nrm * (1.0 + mod_ref[4:5, :]) + mod_ref[3:4, :]).astype(BF16)

    return pl.pallas_call(
        body, name="resid_norm2", grid=(s // TRE,),
        in_specs=[_erow(D), _erow(D), _full_spec((SUB, D)), _full_spec((1, D))],
        out_specs=(_erow(D), _erow(D)),
        out_shape=(jax.ShapeDtypeStruct((s, D), F32), jax.ShapeDtypeStruct((s, D), BF16)),
        compiler_params=_params(("parallel",), 24),
    )(x, z, mod, g)


def _loss_head(x1, y, tgt, mod):
    s = x1.shape[0]

    def body(x1_ref, y_ref, t_ref, mod_ref, dout_ref, dy_ref, vec_ref):
        @pl.when(pl.program_id(0) == 0)
        def _():
            vec_ref[...] = jnp.zeros_like(vec_ref)

        yv = y_ref[...]
        g2 = mod_ref[5:6, :]
        diff = x1_ref[...] + g2 * yv - t_ref[...]
        dout = diff * (1.0 / D)
        dout_ref[...] = dout
        dy_ref[...] = (g2 * dout).astype(BF16)
        vec_ref[0:1, :] += jnp.sum(dout * yv, axis=0, keepdims=True)
        vec_ref[1:2, :] += jnp.sum(diff * diff, axis=0, keepdims=True)

    return pl.pallas_call(
        body, name="loss_head", grid=(s // TRE,),
        in_specs=[_erow(D), _erow(D), _erow(D), _full_spec((SUB, D))],
        out_specs=(_erow(D), _erow(D), _full_spec((SUB, D))),
        out_shape=(jax.ShapeDtypeStruct((s, D), F32), jax.ShapeDtypeStruct((s, D), BF16),
                   jax.ShapeDtypeStruct((SUB, D), F32)),
        compiler_params=_params(("arbitrary",), 24),
    )(x1, y, tgt, mod)


def _norm_mod_bwd(dh, xin, dres, zin, mod, g, scale_row, gate_row, name, hosted=None):
    s = dh.shape[0]
    with_gate = gate_row is not None

    def body(*refs):
        if with_gate:
            dh_ref, x_ref, dres_ref, z_ref, mod_ref, g_ref, dx_ref, dz_ref, vec_ref = refs
        else:
            dh_ref, x_ref, dres_ref, mod_ref, g_ref, dx_ref, vec_ref = refs

        @pl.when(pl.program_id(0) == 0)
        def _():
            vec_ref[...] = jnp.zeros_like(vec_ref)

        xv = x_ref[...]
        dhv = dh_ref[...]
        gv = g_ref[...]
        r = lax.rsqrt(jnp.mean(xv * xv, axis=-1, keepdims=True) + EPS)
        xh = xv * r
        dn = dhv * (1.0 + mod_ref[scale_row:scale_row + 1, :])
        dxh = dn * gv
        dx = dres_ref[...] + r * (dxh - xh * jnp.mean(dxh * xh, axis=-1, keepdims=True))
        dx_ref[...] = dx
        vec_ref[0:1, :] += jnp.sum(dhv, axis=0, keepdims=True)
        vec_ref[1:2, :] += jnp.sum(dhv * (xh * gv), axis=0, keepdims=True)
        vec_ref[2:3, :] += jnp.sum(dn * xh, axis=0, keepdims=True)
        if with_gate:
            dz_ref[...] = (mod_ref[gate_row:gate_row + 1, :] * dx).astype(BF16)
            vec_ref[3:4, :] += jnp.sum(dx * z_ref[...], axis=0, keepdims=True)

    ins = [dh, xin, dres] + ([zin] if with_gate else []) + [mod, g]
    in_specs = [_erow(D)] * (4 if with_gate else 3) + [_full_spec((SUB, D)), _full_spec((1, D))]
    out_specs = [_erow(D)] + ([_erow(D)] if with_gate else []) + [_full_spec((SUB, D))]
    out_shape = [jax.ShapeDtypeStruct((s, D), F32)] + ([jax.ShapeDtypeStruct((s, D), BF16)] if with_gate else []) \
        + [jax.ShapeDtypeStruct((SUB, D), F32)]
    outs, moved = _call(body, ins, name=name, grid=(s // TRE,), in_specs=in_specs, out_specs=out_specs,
                        out_shape=out_shape, vmem_mb=32, hosted=hosted)
    return outs + (moved,) if hosted else outs


XIN_BLOCK = 3 * AW // LANES
BG_BLOCK = XIN_BLOCK + CW // LANES
CG_BLOCK = BG_BLOCK + CW // LANES


def _seq_spec(s, first_block):
    return pl.BlockSpec((s, LANES), lambda j, fb=first_block: (0, fb + j))


def _mixconv_bwd(dmixed, proj, w):
    s = proj.shape[0]

    def body(d_ref, xin_ref, bg_ref, cg_ref, w_ref, dxin_ref, dbg_ref, dcg_ref, dw_ref):
        wv = w_ref[...]
        xin, cg, dconv = xin_ref[...], cg_ref[...], d_ref[...]
        cx = cg * xin
        cv, s1, s2 = _conv_taps(cx, None, wv)
        dbg_ref[...] = (dconv * cv).astype(BF16)
        dcv = dconv * bg_ref[...]
        dw_ref[...] = jnp.zeros_like(dw_ref)
        dw_ref[0:1, :] = jnp.sum(dcv * s2, axis=0, keepdims=True)
        dw_ref[1:2, :] = jnp.sum(dcv * s1, axis=0, keepdims=True)
        dw_ref[2:3, :] = jnp.sum(dcv * cx, axis=0, keepdims=True)
        dcx = _conv_taps_t(dcv, None, wv)
        dcg_ref[...] = (dcx * xin).astype(BF16)
        dxin_ref[...] = (dcx * cg).astype(BF16)

    o = jax.ShapeDtypeStruct((s, CW), BF16)
    return pl.pallas_call(
        body, name="mixconv_bwd", grid=(CW // LANES,),
        in_specs=[_seq_spec(s, AW // LANES), _seq_spec(s, XIN_BLOCK), _seq_spec(s, BG_BLOCK), _seq_spec(s, CG_BLOCK),
                  pl.BlockSpec((3, LANES), lambda j: (0, j))],
        out_specs=(_seq_spec(s, 0), _seq_spec(s, 0), _seq_spec(s, 0), pl.BlockSpec((SUB, LANES), lambda j: (0, j))),
        out_shape=(o, o, o, jax.ShapeDtypeStruct((SUB, CW), F32)),
        compiler_params=_params(("parallel",), 32),
    )(dmixed, proj, proj, proj, w)


TA = 512
NEG = -1e30


def _causal_mask():
    row = lax.broadcasted_iota(jnp.int32, (TA, TA), 0)
    col = lax.broadcasted_iota(jnp.int32, (TA, TA), 1)
    return col <= row


def _attn_fwd(qp, kp, vp):
    s = qp.shape[0]
    nq = s // TA

    def body(q_ref, k_ref, v_ref, o_ref, lse_ref):
        i = pl.program_id(1)
        slabs = [slice(SLAB * hh, SLAB * (hh + 1)) for hh in range(2)]
        q = [q_ref[:, sl] for sl in slabs]

        def block(j, carry, masked):
            keys = pl.ds(pl.multiple_of(j * TA, TA), TA)
            ms, acc = carry
            m_out, parts = [], []
            for hh in range(2):
                sc = _dot(q[hh], k_ref[keys, slabs[hh]], NT)
                if masked:
                    sc = jnp.where(_causal_mask(), sc, NEG)
                m_new = jnp.maximum(ms[hh], jnp.max(sc, axis=-1, keepdims=True))
                p = jnp.exp(sc - m_new)
                parts.append(jnp.exp(ms[hh] - m_new) * acc[:, slabs[hh]] + _dot(p.astype(BF16), v_ref[keys, slabs[hh]]))
                m_out.append(m_new)
            return tuple(m_out), jnp.concatenate(parts, axis=1)

        init = ((jnp.full((TA, 1), NEG, F32), jnp.full((TA, 1), NEG, F32)), jnp.zeros((TA, 2 * SLAB), F32))
        carry = lax.fori_loop(0, i, lambda j, cr: block(j, cr, False), init)
        ms, acc = block(i, carry, True)
        for hh in range(2):
            l = acc[:, SLAB * hh + DH:SLAB * hh + DH + 1]
            o_ref[:, DH * hh:DH * (hh + 1)] = acc[:, SLAB * hh:SLAB * hh + DH] / l
            lse_ref[0, :, hh:hh + 1] = ms[hh] + jnp.log(l)

    (o, lse), _ = _call(
        body, (qp, kp, vp), name="attn_fwd", grid=(HEADS // 2, nq),
        in_specs=[pl.BlockSpec((TA, 2 * SLAB), lambda p, i: (i, p)),
                  pl.BlockSpec((s, 2 * SLAB), lambda p, i: (0, p)),
                  pl.BlockSpec((s, 2 * SLAB), lambda p, i: (0, p))],
        out_specs=[pl.BlockSpec((TA, LANES), lambda p, i: (i, p)), pl.BlockSpec((1, TA, 2), lambda p, i: (p, i, 0))],
        out_shape=[jax.ShapeDtypeStruct((s, AW), F32), jax.ShapeDtypeStruct((HEADS // 2, s, 2), F32)],
        vmem_mb=24)
    return o, lse


def _attn_bwd(qp, kp, vp, dmixed, o, lse, hosted):
    s = qp.shape[0]
    nq = s // TA

    def body(q_ref, k_ref, v_ref, do_ref, o_ref, lse_ref, dq_ref, dk_ref, dv_ref, qb_ref, dob_ref):
        dk_ref[...] = jnp.zeros_like(dk_ref)
        dv_ref[...] = jnp.zeros_like(dv_ref)
        slabs = [slice(SLAB * hh, SLAB * (hh + 1)) for hh in range(2)]
        lane = lax.broadcasted_iota(jnp.int32, (TA, DH), 1)

        def q_block(i, _):
            i0 = pl.multiple_of(i * TA, TA)
            rows = pl.ds(i0, TA)
            for hh in range(2):
                half = slice(DH * hh, DH * (hh + 1))
                do = do_ref[rows, half]
                delta = jnp.sum(do * o_ref[rows, half], axis=-1, keepdims=True)
                dob_ref[hh, :, 0:DH] = do.astype(BF16)
                dob_ref[hh, :, DH:SLAB] = _lanes3(lane, 0, [-d for d in _split3(delta)], 0.0).astype(BF16)
                lse3 = _split3(lse_ref[0, rows, hh:hh + 1])
                qb_ref[hh, :, 0:DH] = q_ref[rows, SLAB * hh:SLAB * hh + DH]
                aug = q_ref[rows, SLAB * hh + DH:SLAB * (hh + 1)].astype(F32)
                qb_ref[hh, :, DH:SLAB] = _lanes3(lane, AUG_LSE, [-x for x in lse3], aug).astype(BF16)

            def block(j, dq, masked):
                keys = pl.ds(pl.multiple_of(j * TA, TA), TA)
                dv, dk, dqc = [], [], []
                for hh in range(2):
                    q, dob = qb_ref[hh], dob_ref[hh]
                    k = k_ref[keys, slabs[hh]]
                    sc = _dot(q, k, NT)
                    if masked:
                        sc = jnp.where(_causal_mask(), sc, NEG)
                    p = jnp.exp(sc)
                    dv.append(_dot(p.astype(BF16), dob, TN))
                    ds = (p * _dot(dob, v_ref[keys, slabs[hh]], NT)).astype(BF16)
                    dk.append(_dot(ds, q, TN))
                    dqc.append(_dot(ds, k))
                dv_ref[keys, :] += jnp.concatenate(dv, axis=1)
                dk_ref[keys, :] += jnp.concatenate(dk, axis=1)
                return dq + jnp.concatenate(dqc, axis=1)

            dq = lax.fori_loop(0, i, lambda j, acc: block(j, acc, False), jnp.zeros((TA, 2 * SLAB), F32))
            dq_ref[rows, :] = block(i, dq, True)
            return 0

        lax.fori_loop(0, nq, q_block, 0)

    pair = lambda p: (0, p)
    slab2 = pl.BlockSpec((s, 2 * SLAB), pair)
    seq = pl.BlockSpec((s, LANES), pair)
    small = pl.BlockSpec((1, s, 2), lambda p: (p, 0, 0))
    o32 = jax.ShapeDtypeStruct((s, HEADS * SLAB), F32)
    return _call(
        body, (qp, kp, vp, dmixed, o, lse), name="attn_bwd", grid=(HEADS // 2,),
        in_specs=[slab2, slab2, slab2, seq, seq, small], out_specs=[slab2, slab2, slab2], out_shape=[o32, o32, o32],
        scratch_shapes=[pltpu.VMEM((2, TA, SLAB), BF16), pltpu.VMEM((2, TA, SLAB), BF16)], vmem_mb=40, hosted=hosted)


def _qkv_post(dqp, dkp, dvp, proj, bf_pad, gq, gk):
    s = proj.shape[0]
    nb = s // TR

    def body(dq_ref, dk_ref, dv_ref, q_ref, k_ref, fg_ref, b_ref, gq_ref, gk_ref, dqo_ref, dko_ref, dvo_ref, dfg_ref,
             vec_ref, carry_ref):
        @pl.when(pl.program_id(0) == 0)
        def _():
            vec_ref[...] = jnp.zeros_like(vec_ref)
            carry_ref[...] = jnp.zeros_like(carry_ref)

        def one(d_ref, x_ref, g_ref, o_ref, row, scale):
            dg = jnp.zeros((1, DH), F32)
            for h in range(HEADS):
                sl = slice(DH * h, DH * (h + 1))
                xv = x_ref[:, sl]
                r = lax.rsqrt(jnp.mean(xv * xv, axis=-1, keepdims=True) + EPS)
                xh = xv * r
                dn = d_ref[:, SLAB * h:SLAB * h + DH] * scale
                dg = dg + jnp.sum(dn * xh, axis=0, keepdims=True)
                dxh = dn * g_ref[...]
                o_ref[:, sl] = (r * (dxh - xh * jnp.mean(dxh * xh, axis=-1, keepdims=True))).astype(BF16)
            vec_ref[row:row + 1, 0:DH] += dg

        one(dq_ref, q_ref, gq_ref, dqo_ref, 0, QK_SCALE)
        one(dk_ref, k_ref, gk_ref, dko_ref, 1, 1.0)
        lane = lax.broadcasted_iota(jnp.int32, (TR, LANES), 1)
        df = jnp.zeros((TR, LANES), F32)
        for h in range(HEADS):
            dvo_ref[:, DH * h:DH * (h + 1)] = dv_ref[:, SLAB * h:SLAB * h + DH].astype(BF16)
            row_sum = dq_ref[:, SLAB * h + DH:SLAB * h + DH + 1]
            col_sum = dk_ref[:, SLAB * h + DH + AUG_ONE:SLAB * h + DH + AUG_ONE + 1]
            df = jnp.where(lane == h, row_sum - col_sum, df)
        row = lax.broadcasted_iota(jnp.int32, (TR, TR), 0)
        col = lax.broadcasted_iota(jnp.int32, (TR, TR), 1)
        dlogf = _dot((col >= row).astype(F32), df, NN, lax.Precision.HIGHEST) + carry_ref[0:1, :]
        carry_ref[...] = jnp.broadcast_to(dlogf[0:1, :], carry_ref.shape)
        dfg = dlogf * _sigmoid(-(fg_ref[...] + b_ref[...]))
        dfg_ref[...] = dfg.astype(BF16)
        vec_ref[2:3, :] += jnp.sum(dfg, axis=0, keepdims=True)

    o = jax.ShapeDtypeStruct((s, AW), BF16)
    rev = lambda width, col=0: pl.BlockSpec((TR, width), lambda i, col=col: (nb - 1 - i, col))
    wide = rev(HEADS * SLAB)
    outs, _ = _call(
        body, (dqp, dkp, dvp, proj, proj, proj, bf_pad, gq, gk), name="qkv_post", grid=(nb,),
        in_specs=[wide, wide, wide, rev(AW, 0), rev(AW, 1), rev(LANES, FG_BLOCK), _full_spec((1, LANES)),
                  _full_spec((1, DH)), _full_spec((1, DH))],
        out_specs=[rev(AW), rev(AW), rev(AW), rev(LANES), _full_spec((SUB, LANES))],
        out_shape=[o, o, o, jax.ShapeDtypeStruct((s, LANES), BF16), jax.ShapeDtypeStruct((SUB, LANES), F32)],
        scratch_shapes=[pltpu.VMEM((SUB, LANES), F32)])
    return outs


TF = 256
NJ = DFF // TF
FFN_ROWS_FWD = 1024
FFN_ROWS_BWD = 1024


def _ffn_fwd(h2, wup_t, cw, wd):
    s = h2.shape[0]
    tr = FFN_ROWS_FWD
    nr = s // tr

    def body(h_ref, wu_ref, cg_ref, cv_ref, wd_ref, pg_ref, pv_ref, y_ref, halo_ref, act_ref):
        r, j = pl.program_id(0), pl.program_id(1)
        hv = h_ref[...]
        pg = _dot(hv, wu_ref[0], NT).astype(BF16)
        pv = _dot(hv, wu_ref[1], NT).astype(BF16)
        pg_ref[...] = pg
        pv_ref[...] = pv
        pgf, pvf = pg.astype(F32), pv.astype(F32)
        ug, _, _ = _conv_taps(pgf, jnp.where(r > 0, halo_ref[j, 0], 0.0), cg_ref[...])
        uv, _, _ = _conv_taps(pvf, jnp.where(r > 0, halo_ref[j, 1], 0.0), cv_ref[...])
        halo_ref[j, 0] = pgf[tr - SUB:tr, :]
        halo_ref[j, 1] = pvf[tr - SUB:tr, :]
        act = (ug * _sigmoid(ug) * uv).astype(BF16)
        for t in range(NJ):
            @pl.when(j == t)
            def _(t=t):
                act_ref[:, t * TF:(t + 1) * TF] = act

        @pl.when(j == NJ - 1)
        def _():
            y_ref[...] = _dot(act_ref[...], wd_ref[...])

    pre = jax.ShapeDtypeStruct((s, DFF), BF16)
    return pl.pallas_call(
        body, name="ffn_fwd", grid=(nr, NJ),
        in_specs=[pl.BlockSpec((tr, D), lambda r, j: (r, 0)),
                  pl.BlockSpec((2, TF, D), lambda r, j: (0, j, 0)),
                  pl.BlockSpec((3, TF), lambda r, j: (0, j)),
                  pl.BlockSpec((3, TF), lambda r, j: (0, NJ + j)),
                  pl.BlockSpec((DFF, D), lambda r, j: (0, 0))],
        out_specs=(pl.BlockSpec((tr, TF), lambda r, j: (r, j)),
                   pl.BlockSpec((tr, TF), lambda r, j: (r, j)),
                   pl.BlockSpec((tr, D), lambda r, j: (r, 0))),
        out_shape=(pre, pre, jax.ShapeDtypeStruct((s, D), F32)),
        scratch_shapes=[pltpu.VMEM((NJ, 2, SUB, TF), F32), pltpu.VMEM((tr, DFF), BF16)],
        compiler_params=_params(("arbitrary", "arbitrary"), 56),
    )(h2, wup_t, cw, cw, wd)


def _ffn_bwd(dy, h2, pre_g, pre_v, wup_t, cw, wd):
    s = h2.shape[0]
    tr = FFN_ROWS_BWD
    nr = s // tr
    hb = tr // (2 * SUB)

    def body(dy_ref, h_ref, pg_ref, pv_ref, hg_ref, hv_ref, wu_ref, cg_ref, cv_ref, wd_ref,
             dh_ref, dwu_ref, dwd_ref, dcg_ref, dcv_ref, nxt_ref, awu_ref, awd_ref):
        j, r = pl.program_id(0), pl.program_id(1)
        rr = nr - 1 - r
        row0 = pl.multiple_of(rr * tr, tr)
        cwg, cwv = cg_ref[...], cv_ref[...]
        pg, pv = pg_ref[...].astype(F32), pv_ref[...].astype(F32)
        ug, g1, g2 = _conv_taps(pg, jnp.where(rr > 0, hg_ref[SUB:2 * SUB, :].astype(F32), 0.0), cwg)
        uv, v1, v2 = _conv_taps(pv, jnp.where(rr > 0, hv_ref[SUB:2 * SUB, :].astype(F32), 0.0), cwv)
        sg = _sigmoid(ug)
        sil = ug * sg
        act = (sil * uv).astype(BF16)
        dyv = dy_ref[...]
        da = _dot(dyv, wd_ref[...], NT)
        dug = da * uv * (sg * (1.0 + ug * (1.0 - sg)))
        duv = da * sil
        dpg = _conv_taps_t(dug, jnp.where(r > 0, nxt_ref[0], 0.0), cwg)
        dpv = _conv_taps_t(duv, jnp.where(r > 0, nxt_ref[1], 0.0), cwv)
        nxt_ref[0] = dug[0:SUB, :]
        nxt_ref[1] = duv[0:SUB, :]
        dpgb, dpvb = dpg.astype(BF16), dpv.astype(BF16)
        hv = h_ref[...]
        dwd = _dot(act, dyv, TN)
        dpb = jnp.concatenate([dpgb, dpvb], axis=1)
        dwu = _dot(dpb, hv, TN)
        dh = _dot(dpb, wu_ref[...].reshape(2 * TF, D))

        def taps(du, x0, x1, x2):
            return (jnp.sum(du * x2, axis=0, keepdims=True), jnp.sum(du * x1, axis=0, keepdims=True),
                    jnp.sum(du * x0, axis=0, keepdims=True))

        tg, tv = taps(dug, pg, g1, g2), taps(duv, pv, v1, v2)

        @pl.when(r == 0)
        def _():
            awd_ref[...] = dwd
            awu_ref[...] = dwu
            dcg_ref[...] = jnp.zeros_like(dcg_ref)
            dcv_ref[...] = jnp.zeros_like(dcv_ref)

        @pl.when(r > 0)
        def _():
            awd_ref[...] += dwd
            awu_ref[...] += dwu

        @pl.when(r == nr - 1)
        def _():
            dwd_ref[...] = awd_ref[...].astype(BF16)
            dwu_ref[...] = awu_ref[...].astype(BF16).reshape(2, TF, D)

        for t in range(3):
            dcg_ref[t:t + 1, :] += tg[t]
            dcv_ref[t:t + 1, :] += tv[t]

        @pl.when(j == 0)
        def _():
            dh_ref[pl.ds(row0, tr), :] = dh

        @pl.when(j > 0)
        def _():
            dh_ref[pl.ds(row0, tr), :] += dh

    rows = lambda j, r: (nr - 1 - r, 0)
    tile = lambda j, r: (nr - 1 - r, j)
    halo = lambda j, r: (jnp.maximum((nr - 1 - r) * hb - 1, 0), j)
    return pl.pallas_call(
        body, name="ffn_bwd", grid=(NJ, nr),
        in_specs=[pl.BlockSpec((tr, D), rows), pl.BlockSpec((tr, D), rows),
                  pl.BlockSpec((tr, TF), tile), pl.BlockSpec((tr, TF), tile),
                  pl.BlockSpec((2 * SUB, TF), halo), pl.BlockSpec((2 * SUB, TF), halo),
                  pl.BlockSpec((2, TF, D), lambda j, r: (0, j, 0)),
                  pl.BlockSpec((3, TF), lambda j, r: (0, j)), pl.BlockSpec((3, TF), lambda j, r: (0, NJ + j)),
                  pl.BlockSpec((TF, D), lambda j, r: (j, 0))],
        out_specs=(pl.BlockSpec((s, D), lambda j, r: (0, 0)),
                   pl.BlockSpec((2, TF, D), lambda j, r: (0, j, 0)),
                   pl.BlockSpec((TF, D), lambda j, r: (j, 0)),
                   pl.BlockSpec((SUB, TF), lambda j, r: (0, j)), pl.BlockSpec((SUB, TF), lambda j, r: (0, j))),
        out_shape=(jax.ShapeDtypeStruct((s, D), F32),
                   jax.ShapeDtypeStruct((2, DFF, D), BF16), jax.ShapeDtypeStruct((DFF, D), BF16),
                   jax.ShapeDtypeStruct((SUB, DFF), F32), jax.ShapeDtypeStruct((SUB, DFF), F32)),
        scratch_shapes=[pltpu.VMEM((2, SUB, TF), F32), pltpu.VMEM((2 * TF, D), F32), pltpu.VMEM((TF, D), F32)],
        compiler_params=_params(("arbitrary", "arbitrary"), 56),
    )(dy, h2, pre_g, pre_v, pre_g, pre_v, wup_t, cw, cw, wd)


def _adam(w, g, m, v):
    m = ADAM_B1 * m + (1.0 - ADAM_B1) * g
    v = ADAM_B2 * v + (1.0 - ADAM_B2) * (g * g)
    m_hat = m / (1.0 - ADAM_B1 ** ADAM_STEP)
    v_hat = v / (1.0 - ADAM_B2 ** ADAM_STEP)
    delta = -ADAM_LR * (m_hat / (jnp.sqrt(v_hat) + ADAM_EPS) + ADAM_WD * w)
    return delta, m, v


NCHIP = NDEV // 2


def _pair_add(mine, theirs, tr, name):
    _, _, rws, cols = mine.shape

    def body(a_ref, b_ref, o_ref):
        c = lax.axis_index("c")
        o_ref[0] = (a_ref[0, c].astype(F32) + b_ref[0].astype(F32)).astype(BF16)

    (out,), _ = _call(
        body, (mine, theirs), name=name, grid=(NCHIP, rws // tr),
        in_specs=[pl.BlockSpec((1, 2, tr, cols), lambda q, i: (q, 0, i, 0)),
                  pl.BlockSpec((1, tr, cols), lambda q, i: (q, i, 0))],
        out_specs=[pl.BlockSpec((1, tr, cols), lambda q, i: (q, i, 0))],
        out_shape=[jax.ShapeDtypeStruct((NCHIP, rws, cols), BF16)], vmem_mb=16)
    return out


def _adamw_sharded(parts, w, m, v, tr, name, hosted=None):
    rws, cols = w.shape
    n_parts = parts.shape[0]

    def body(p_ref, w_ref, m_ref, v_ref, g_ref, d_ref, mo_ref, vo_ref):
        g = p_ref[0].astype(F32)
        for q in range(1, n_parts):
            g = g + p_ref[q].astype(F32)
        g_ref[...] = g
        d_ref[...], mo_ref[...], vo_ref[...] = _adam(w_ref[...], g, m_ref[...], v_ref[...])

    blk = pl.BlockSpec((tr, cols), lambda i: (i, 0))
    o = jax.ShapeDtypeStruct((rws, cols), F32)
    outs, moved = _call(
        body, (parts, w, m, v), name=name, grid=(rws // tr,),
        in_specs=[pl.BlockSpec((n_parts, tr, cols), lambda i: (0, i, 0)), blk, blk, blk],
        out_specs=[blk, blk, blk, blk], out_shape=[o, o, o, o], vmem_mb=44 if tr > 256 else 24, hosted=hosted)
    return (outs, moved) if hosted else outs


def _adamw_ada(c_all, dmod_my, w, m, v):
    rws, cols = w.shape
    tr = 256

    def body(c_ref, dm_ref, w_ref, m_ref, v_ref, g_ref, d_ref, mo_ref, vo_ref):
        cv = c_ref[...]
        act = cv * _sigmoid(cv)
        g = _dot(act, dm_ref[...], TN, lax.Precision.HIGHEST)
        g_ref[...] = g
        d_ref[...], mo_ref[...], vo_ref[...] = _adam(w_ref[...], g, m_ref[...], v_ref[...])

    blk = pl.BlockSpec((tr, cols), lambda i: (i, 0))
    o = jax.ShapeDtypeStruct((rws, cols), F32)
    return pl.pallas_call(
        body, name="adamw_ada", grid=(rws // tr,),
        in_specs=[pl.BlockSpec((NDEV, tr), lambda i: (0, i)), _full_spec((NDEV, cols)), blk, blk, blk],
        out_specs=(blk, blk, blk, blk), out_shape=(o, o, o, o),
        compiler_params=_params(("parallel",), 32),
    )(c_all, dmod_my, w, m, v)


REP_ROWS = 16
ROW_N1, ROW_N2, ROW_LOSS, ROW_MISC = 6, 7, 8, 9
LANE_BF, LANE_GQ, LANE_GK = 0, 128, 256


def _adamw_small(rep_all, conv_all, wmv):
    n_ff = wmv[6][0].shape[1]

    def body(*refs):
        rep_ref, conv_ref = refs[:2]
        ins = refs[2:2 + 24]
        outs = refs[2 + 24:]
        loss_ref, outs = outs[0], outs[1:]
        g_rep = rep_ref[0]
        g_conv = conv_ref[0]
        for d in range(1, NDEV):
            g_rep = g_rep + rep_ref[d]
            g_conv = g_conv + conv_ref[d]
        loss_ref[...] = (0.5 / D) * jnp.sum(g_rep[ROW_LOSS:ROW_LOSS + 1, :], axis=-1, keepdims=True)
        grads = [
            None,
            g_rep[ROW_N1:ROW_N1 + 1, :],
            g_rep[ROW_MISC:ROW_MISC + 1, LANE_BF:LANE_BF + HEADS],
            g_rep[ROW_MISC:ROW_MISC + 1, LANE_GQ:LANE_GQ + DH],
            g_rep[ROW_MISC:ROW_MISC + 1, LANE_GK:LANE_GK + DH],
            g_rep[ROW_N2:ROW_N2 + 1, :],
            g_conv[0:3, 0:n_ff],
            g_conv[0:3, n_ff:n_ff + DH],
        ]
        for p in range(8):
            w_ref, m_ref, v_ref = ins[3 * p:3 * p + 3]
            g_ref, d_ref, mo_ref, vo_ref = outs[4 * p:4 * p + 4]
            if p == 0:
                for nmod in range(NMOD):
                    sl = slice(D * nmod, D * (nmod + 1))
                    g = g_rep[nmod:nmod + 1, :]
                    g_ref[:, sl] = g
                    d_ref[:, sl], mo_ref[:, sl], vo_ref[:, sl] = _adam(w_ref[:, sl], g, m_ref[:, sl], v_ref[:, sl])
            else:
                g = grads[p]
                g_ref[...] = g
                d_ref[...], mo_ref[...], vo_ref[...] = _adam(w_ref[...], g, m_ref[...], v_ref[...])

    flat = [a for trio in wmv for a in trio]
    out_shape = [jax.ShapeDtypeStruct((1, 1), F32)]
    for trio in wmv:
        out_shape += [jax.ShapeDtypeStruct(trio[0].shape, F32)] * 4
    ins = [rep_all, conv_all] + flat
    return pl.pallas_call(
        body, name="adamw_small", grid=(1,),
        in_specs=[_full_spec(a.shape) for a in ins], out_specs=tuple(_full_spec(o.shape) for o in out_shape),
        out_shape=tuple(out_shape), compiler_params=_params(("arbitrary",), 32),
    )(*ins)


FG_FIRST = 3 * AW
N_IN = DIN // NDEV


def _w_in_runs():
    runs = []
    for d in range(NDEV):
        lo, hi = N_IN * d, N_IN * (d + 1)
        for a, b, shift in ((0, FG_FIRST, 0), (FG_FIRST, FG_FIRST + HEADS, DIN - HEADS - FG_FIRST),
                            (FG_FIRST + HEADS, DIN, -HEADS)):
            a, b = max(a, lo), min(b, hi)
            if a < b:
                runs.append((d, a - lo, a + shift, b - a))
    return runs


W_IN_ROWS = 256
N_IN_PAD = 512


def _identity(n):
    return (lax.broadcasted_iota(jnp.int32, (n, n), 0) == lax.broadcasted_iota(jnp.int32, (n, n), 1)).astype(BF16)


def _assemble_w_in(g_in, hosted):
    def body(g_ref, o_ref, t_ref):
        eye = _identity(W_IN_ROWS)
        shard = None
        for d, src, dst, width in _w_in_runs():
            if d != shard:
                t_ref[:, 0:N_IN] = _dot(eye, g_ref[d], NT).astype(BF16)
                shard = d
            o_ref[:, dst:dst + width] = t_ref[:, src:src + width]
        o_ref[:, DIN:DINP] = jnp.zeros((W_IN_ROWS, DINP - DIN), o_ref.dtype)

    (out,), moved = _call(
        body, (g_in,), name="assemble_w_in", grid=(D // W_IN_ROWS,),
        in_specs=[pl.BlockSpec((NDEV, N_IN, W_IN_ROWS), lambda i: (0, 0, i))],
        out_specs=[pl.BlockSpec((W_IN_ROWS, DINP), lambda i: (i, 0))],
        out_shape=[jax.ShapeDtypeStruct((D, DINP), g_in.dtype)],
        scratch_shapes=[pltpu.VMEM((W_IN_ROWS, N_IN_PAD), BF16)], vmem_mb=16, hosted=hosted)
    return out, moved


def _scatter_dw_in(dwp):
    def body(w_ref, o_ref, t_ref):
        eye = _identity(W_IN_ROWS)
        runs = _w_in_runs()
        for i, (d, src, dst, width) in enumerate(runs):
            t_ref[:, src:src + width] = w_ref[:, dst:dst + width]
            if i + 1 == len(runs) or runs[i + 1][0] != d:
                o_ref[d // 2, d % 2] = _dot(t_ref[:, 0:N_IN], eye, TN).astype(BF16)

    (out,), _ = _call(
        body, (dwp,), name="scatter_dw_in", grid=(D // W_IN_ROWS,),
        in_specs=[pl.BlockSpec((W_IN_ROWS, DINP), lambda i: (i, 0))],
        out_specs=[pl.BlockSpec((NCHIP, 2, N_IN, W_IN_ROWS), lambda i: (0, 0, 0, i))],
        out_shape=[jax.ShapeDtypeStruct((NCHIP, 2, N_IN, D), dwp.dtype)],
        scratch_shapes=[pltpu.VMEM((W_IN_ROWS, N_IN_PAD), BF16)], vmem_mb=16)
    return out


def kernel(x, c, w_ada, b_ada, norm1_g, w_in, b_forget, q_norm_g, k_norm_g, conv_mix_w, w_out, norm2_g, w_up, ffn_conv_w, w_down, loss_target, m_w_ada, m_b_ada, m_norm1_g, m_w_in, m_b_forget, m_q_norm_g, m_k_norm_g, m_conv_mix_w, m_w_out, m_norm2_g, m_w_up, m_ffn_conv_w, m_w_down, v_w_ada, v_b_ada, v_norm1_g, v_w_in, v_b_forget, v_q_norm_g, v_k_norm_g, v_conv_mix_w, v_w_out, v_norm2_g, v_w_up, v_ffn_conv_w, v_w_down):
    me = 4 * lax.axis_index("x") + 2 * lax.axis_index("y") + lax.axis_index("c")
    xs, tgt = x[0], loss_target[0]
    n_ada = w_ada.shape[2]
    n_ff = w_up.shape[2]

    conv_w = jnp.concatenate([ffn_conv_w[0], conv_mix_w[0]], axis=1)
    conv_w = jnp.concatenate([conv_w, jnp.zeros((SUB - 3, conv_w.shape[1]), F32)], axis=0)
    c_all, conv_all, g_in = _exchange(
        [(c.reshape(SUB, D // SUB), "ag"), (conv_w, "ag"), (jnp.transpose(w_in[0]).astype(BF16), "ag2")],
        "exchange_w_in")
    g_in, w_out_b, w_up_b, w_down_b = lax.optimization_barrier(
        (g_in, w_out[0].astype(BF16), jnp.transpose(w_up[0]).astype(BF16), w_down[0].astype(BF16)))
    g_out, g_up, g_down = _sequencer_exchange(
        [(w_out_b, "ag2"), (w_up_b, "ag2"), (w_down_b, "ag2")], "gather_weights", collective_id=1)
    c_all = c_all.reshape(NDEV, D)
    cw_ffn = jnp.transpose(conv_all[:, :3, :n_ff], (1, 0, 2)).reshape(3, 2 * DFF)
    cw_mix = jnp.transpose(conv_all[:, :3, n_ff:], (1, 0, 2)).reshape(3, CW)

    b_my = lax.dynamic_slice(b_ada, (0, me * n_ada), (1, n_ada))
    mod_part = _ada_fwd(c_all, w_ada[0], b_my)
    w_in_p, (mod_rows,) = _assemble_w_in(
        g_in, [(jnp.broadcast_to(mod_part[:, None, :], (NDEV, SUB, n_ada)), "a2a")])
    mod = mod_rows[:, 0, :].reshape(NMOD, D)
    mod = jnp.concatenate([mod, jnp.zeros((SUB - NMOD, D), F32)], axis=0)

    h = _norm_mod_fwd(xs, mod, norm1_g)
    proj = _mm(h, w_in_p, "nn", F32, 1024, 640, "proj_fwd")
    bf_pad = jnp.concatenate([b_forget, jnp.zeros((1, LANES - HEADS), F32)], axis=1)
    qp, kp, vp, conv = _qkv_prep(proj, bf_pad, q_norm_g, k_norm_g, cw_mix)
    attn, lse = _attn_fwd(qp, kp, vp)
    w_out_f = g_out.reshape(D, D)
    w_up_t = g_up.reshape(2, DFF, D)
    w_down_f = g_down.reshape(DFF, D)
    mixed = jnp.concatenate([attn, conv], axis=1).astype(BF16)
    z = _mm(mixed, w_out_f, "nn", F32, 1024, 1024, "out_fwd")
    x1, h2 = _resid_norm2(xs, z, mod, norm2_g)
    pre_g, pre_v, y = _ffn_fwd(h2, w_up_t, cw_ffn, w_down_f)
    dout, dy, vec_l = _loss_head(x1, y, tgt, mod)

    dh2, dwup_t, dwd, dcw_g, dcw_v = _ffn_bwd(dy, h2, pre_g, pre_v, w_up_t, cw_ffn, w_down_f)
    s_down = dwd.reshape(NCHIP, 2, DFF // NDEV, D)
    s_up = dwup_t.reshape(NCHIP, 2, n_ff, D)
    dx1, dz, vec_2, (t_up, t_down) = _norm_mod_bwd(dh2, x1, dout, z, mod, norm2_g, 4, 2, "norm2_bwd",
                                                   hosted=[(s_up, "pair"), (s_down, "pair")])
    dwout = _mm(mixed, dz, "tn", BF16, 1024, 1024, "out_bwd_w")
    s_out = dwout.reshape(NCHIP, 2, D // NDEV, D)
    dmixed, (t_out,) = _mm(dz, w_out_f, "nt", F32, 1024, 1024, "out_bwd_x", hosted=[(s_out, "pair")])
    c_out = _pair_add(s_out, t_out, 128, "pair_add_out")
    c_up = _pair_add(s_up, t_up, 176, "pair_add_up")
    c_down = _pair_add(s_down, t_down, 176, "pair_add_down")
    dxin, dbg, dcg, dcw_mix = _mixconv_bwd(dmixed, proj, cw_mix)
    (dqp, dkp, dvp), (p_up, p_down, p_out) = _attn_bwd(
        qp, kp, vp, dmixed, attn, lse, [(c_up, "chips"), (c_down, "chips"), (c_out, "chips")])
    dq, dk, dvb, dfg, vec_qk = _qkv_post(dqp, dkp, dvp, proj, bf_pad, q_norm_g, k_norm_g)
    dproj = jnp.concatenate([dq, dk, dvb, dxin, dbg, dcg, dfg], axis=1)
    dwin_p = _mm(h, dproj, "tn", BF16, 1024, 640, "proj_bwd_w")
    s_in = _scatter_dw_in(dwin_p).reshape(NDEV, N_IN, D)
    (p_in,) = _sequencer_exchange([(s_in, "a2a")], "scatter_dw_in_partials", collective_id=2, all_peers=True)
    dh = _mm(dproj, w_in_p, "nt", F32, 1024, 512, "proj_bwd_x", vmem_mb=36)
    grad_x, vec_1 = _norm_mod_bwd(dh, xs, dx1, None, mod, norm1_g, 1, None, "norm1_bwd")

    gap = lambda n: jnp.zeros((1, n), F32)
    misc = jnp.concatenate([
        vec_qk[2:3, :HEADS], gap(LANE_GQ - LANE_BF - HEADS), vec_qk[0:1, :DH], gap(LANE_GK - LANE_GQ - DH),
        vec_qk[1:2, :DH], gap(D - LANE_GK - DH)], axis=1)
    rep = jnp.concatenate([
        vec_1[0:1], vec_1[1:2], vec_2[3:4], vec_2[0:1], vec_2[1:2], vec_l[0:1],
        vec_1[2:3], vec_2[2:3], vec_l[1:2], misc, jnp.zeros((REP_ROWS - 10, D), F32)], axis=0)
    dcw_ffn = jnp.concatenate([dcw_g, dcw_v], axis=1).reshape(SUB, NDEV, n_ff)
    dcw_all = jnp.concatenate([jnp.transpose(dcw_ffn, (1, 0, 2)),
                               jnp.transpose(dcw_mix.reshape(SUB, NDEV, DH), (1, 0, 2))], axis=2)
    r_up = _adamw_sharded(p_up, jnp.transpose(w_up[0]), jnp.transpose(m_w_up[0]), jnp.transpose(v_w_up[0]), 176,
                          "adamw_up")
    r_down = _adamw_sharded(p_down, w_down[0], m_w_down[0], v_w_down[0], 176, "adamw_down")
    rep, dcw_all, r_up, r_down = lax.optimization_barrier((rep, dcw_all, r_up, r_down))
    r_up = tuple(jnp.transpose(a) for a in r_up)
    r_out, (rep_all, conv_parts) = _adamw_sharded(p_out, w_out[0], m_w_out[0], v_w_out[0], 128, "adamw_out",
                                                  hosted=[(rep, "ag"), (dcw_all, "a2a")])
    dmod_my = lax.dynamic_slice(rep_all[:, :NMOD, :].reshape(NDEV, NMOD * D), (0, me * n_ada), (NDEV, n_ada))
    r_ada = _adamw_ada(c_all, dmod_my, w_ada[0], m_w_ada[0], v_w_ada[0])
    r_in = _adamw_sharded(p_in, jnp.transpose(w_in[0]), jnp.transpose(m_w_in[0]), jnp.transpose(v_w_in[0]), N_IN,
                          "adamw_in")
    r_in = tuple(jnp.transpose(a) for a in r_in)
    small = _adamw_small(rep_all, conv_parts, [
        [b_ada, m_b_ada, v_b_ada], [norm1_g, m_norm1_g, v_norm1_g], [b_forget, m_b_forget, v_b_forget],
        [q_norm_g, m_q_norm_g, v_q_norm_g], [k_norm_g, m_k_norm_g, v_k_norm_g], [norm2_g, m_norm2_g, v_norm2_g],
        [ffn_conv_w[0], m_ffn_conv_w[0], v_ffn_conv_w[0]], [conv_mix_w[0], m_conv_mix_w[0], v_conv_mix_w[0]]])
    loss = small[0].reshape(())
    r_bada, r_n1, r_bf, r_gq, r_gk, r_n2, r_cf, r_cm = [small[1 + 4 * p:5 + 4 * p] for p in range(8)]
    lead = lambda t: tuple(a[None] for a in t)
    per_w = [lead(r_ada), r_bada, r_n1, lead(r_in), r_bf, r_gq, r_gk, lead(r_cm), lead(r_out), r_n2,
             lead(r_up), lead(r_cf), lead(r_down)]
    outs = [loss, grad_x[None]]
    for field in range(4):
        outs += [t[field] for t in per_w]
    return tuple(outs)
```

```python
import functools

import jax
import jax.numpy as jnp
import numpy as np
from jax import lax
from jax.experimental import pallas as pl
from jax.experimental.pallas import tpu as pltpu
from jax.experimental.pallas import tpu_sc as plsc

F32 = jnp.float32
BF16 = jnp.bfloat16

NDEV = 8
D = 1024
HEADS = 8
DH = 64
AW = 512
CW = 512
DFF = 2816
DIN = 3080
DINP = 3200
NMOD = 6
EPS = 1e-6
QK_SCALE = 0.125
LANES = 128
SUB = 8

ADAM_LR = 0.001
ADAM_B1 = 0.9
ADAM_B2 = 0.999
ADAM_EPS = 1e-08
ADAM_WD = 0.01
ADAM_STEP = 10

MESH = pl.DeviceIdType.MESH
ANY = pl.BlockSpec(memory_space=pl.ANY)

NN = (((1,), (0,)), ((), ()))
NT = (((1,), (1,)), ((), ()))
TN = (((0,), (0,)), ((), ()))


def _dot(a, b, dims=NN, precision=None):
    return lax.dot_general(a, b, dims, precision=precision, preferred_element_type=F32)


def _params(sem=None, vmem_mb=None):
    kw = {}
    if sem is not None:
        kw["dimension_semantics"] = sem
    if vmem_mb is not None:
        kw["vmem_limit_bytes"] = vmem_mb * 1024 * 1024
    return pltpu.CompilerParams(**kw)


def _sigmoid(x):
    return 0.5 * jnp.tanh(0.5 * x) + 0.5


class _Exchange:
    def __init__(self, items):
        self.arrays = [pltpu.with_memory_space_constraint(a, pltpu.HBM) for a, _ in items]
        self.modes = [m for _, m in items]
        self.n = len(items)
        self.out_shape = []
        for a, m in items:
            sh = {"ag": (NDEV,) + a.shape, "ag2": (NDEV,) + a.shape, "pair": a.shape[:1] + a.shape[2:]}.get(m, a.shape)
            self.out_shape.append(jax.ShapeDtypeStruct(sh, a.dtype))
        self.scratch = [pltpu.SemaphoreType.DMA((self.n, NDEV - 1)), pltpu.SemaphoreType.DMA((self.n, NDEV - 1)),
                        pltpu.SemaphoreType.DMA((self.n,))]

    def _plan(self, srcs, outs, sems):
        send_sems, recv_sems, loc_sems = sems
        x, y, c = lax.axis_index("x"), lax.axis_index("y"), lax.axis_index("c")
        me, my_chip = 4 * x + 2 * y + c, 2 * x + y
        sib = (x, y, 1 - c)
        local, first, landed, forwards, arrivals = [], [], [], [], []

        def remote(a, k, src, dst, to):
            return pltpu.make_async_remote_copy(src_ref=src, dst_ref=dst, send_sem=send_sems.at[a, k],
                                                recv_sem=recv_sems.at[a, k], device_id=to, device_id_type=MESH)

        for a, mode in enumerate(self.modes):
            src, out = srcs[a], outs[a]
            if mode in ("ag", "a2a"):
                piece = (lambda slot, src=src: src) if mode == "ag" else (lambda slot, src=src: src.at[slot])
                local.append(pltpu.make_async_copy(piece(me), out.at[me], loc_sems.at[a]))
                for r in range(1, NDEV):
                    px = 1 - x if (r >> 2) & 1 else x
                    py = 1 - y if (r >> 1) & 1 else y
                    pc = 1 - c if r & 1 else c
                    pidx = 4 * px + 2 * py + pc
                    first.append(remote(a, r - 1, piece(pidx), out.at[me], (px, py, pc)))
                    arrivals.append(remote(a, r - 1, piece(pidx), out.at[pidx], (px, py, pc)))
            elif mode == "ag2":
                local.append(pltpu.make_async_copy(src, out.at[me], loc_sems.at[a]))
                first.append(remote(a, 0, src, out.at[me], sib))
                arrivals.append(remote(a, 0, src, out.at[me + 1 - 2 * c], sib))
                for j, (px, py) in enumerate([(1 - x, y), (x, 1 - y), (1 - x, 1 - y)]):
                    theirs = out.at[4 * px + 2 * py + c]
                    first.append(remote(a, 1 + j, src, out.at[me], (px, py, c)))
                    landed.append(remote(a, 1 + j, src, theirs, (px, py, c)))
                    forwards.append(remote(a, 4 + j, theirs, theirs, sib))
                    arrivals.append(remote(a, 4 + j, src, out.at[4 * px + 2 * py + 1 - c], sib))
            elif mode == "pair":
                for q in range(NDEV // 2):
                    first.append(remote(a, q, src.at[q, 1 - c], out.at[q], sib))
                    arrivals.append(remote(a, q, src.at[q, 1 - c], out.at[q], sib))
            else:
                assert mode == "chips", mode
                local.append(pltpu.make_async_copy(src.at[my_chip], out.at[my_chip], loc_sems.at[a]))
                for j, (px, py) in enumerate([(1 - x, y), (x, 1 - y), (1 - x, 1 - y)]):
                    q = 2 * px + py
                    first.append(remote(a, 1 + j, src.at[q], out.at[my_chip], (px, py, c)))
                    arrivals.append(remote(a, 1 + j, src.at[q], out.at[q], (px, py, c)))
        return local, first, landed, forwards, arrivals

    def start(self, srcs, outs, sems):
        local, first, _, _, _ = self._plan(srcs, outs, sems)
        for cp in local + first:
            cp.start()

    def wait(self, srcs, outs, sems):
        local, first, landed, forwards, arrivals = self._plan(srcs, outs, sems)
        for cp, fwd in zip(landed, forwards):
            cp.wait_recv()
            fwd.start()
        for cp in arrivals:
            cp.wait_recv()
        for cp in first + forwards:
            cp.wait_send()
        for cp in local:
            cp.wait()


def _exchange(items, name):
    ex = _Exchange(items)
    n = ex.n

    def body(*refs):
        srcs, outs, sems = refs[:n], refs[n:2 * n], refs[2 * n:]
        ex.start(srcs, outs, sems)
        ex.wait(srcs, outs, sems)

    return pl.pallas_call(
        body, name=name,
        out_shape=tuple(ex.out_shape),
        in_specs=[ANY] * n, out_specs=tuple([ANY] * n),
        scratch_shapes=ex.scratch,
        compiler_params=pltpu.CompilerParams(has_side_effects=True),
    )(*ex.arrays)


def _sequencer_exchange(items, name, collective_id, all_peers=False):
    ex = _Exchange(items)
    srcs = [jax.new_ref(a, memory_space=pltpu.MemorySpace.HBM) for a in ex.arrays]
    outs = [jax.empty_ref(sh, memory_space=pltpu.MemorySpace.HBM) for sh in ex.out_shape]

    @pl.kernel(mesh=plsc.ScalarSubcoreMesh(axis_name="sequencer", num_cores=1), name=name,
               scratch_types=tuple(ex.scratch), compiler_params=pltpu.CompilerParams(collective_id=collective_id))
    def launch(send_sems, recv_sems, loc_sems):
        x, y, c = lax.axis_index("x"), lax.axis_index("y"), lax.axis_index("c")
        barrier = pltpu.get_barrier_semaphore()
        peers = [(x, y, 1 - c), (1 - x, y, c), (x, 1 - y, c), (1 - x, 1 - y, c)]
        if all_peers:
            peers += [(1 - x, y, 1 - c), (x, 1 - y, 1 - c), (1 - x, 1 - y, 1 - c)]
        for peer in peers:
            pl.semaphore_signal(barrier, inc=1, device_id=peer, device_id_type=MESH)
        pl.semaphore_wait(barrier, len(peers))
        sems = (send_sems, recv_sems, loc_sems)
        ex.start(srcs, outs, sems)
        ex.wait(srcs, outs, sems)

    launch()
    return [o[...] for o in outs]


def _call(body, inputs, *, name, grid, in_specs, out_specs, out_shape, scratch_shapes=(), vmem_mb=None, hosted=None):
    out_specs, out_shape, scratch_shapes = tuple(out_specs), tuple(out_shape), list(scratch_shapes)
    if not hosted:
        res = pl.pallas_call(
            body, name=name, grid=grid, in_specs=list(in_specs), out_specs=out_specs, out_shape=out_shape,
            scratch_shapes=scratch_shapes, compiler_params=_params(("arbitrary",) * len(grid), vmem_mb),
        )(*inputs)
        return tuple(res), ()
    ex = _Exchange(hosted)
    n, n_in, n_out, n_scr = ex.n, len(inputs), len(out_shape), len(scratch_shapes)

    def hosting_body(*refs):
        ins, srcs = refs[:n_in], refs[n_in:n_in + n]
        outs, landing = refs[n_in + n:n_in + n + n_out], refs[n_in + n + n_out:n_in + 2 * n + n_out]
        scratch, sems = refs[n_in + 2 * n + n_out:n_in + 2 * n + n_out + n_scr], refs[n_in + 2 * n + n_out + n_scr:]
        first = functools.reduce(jnp.logical_and, [pl.program_id(d) == 0 for d in range(len(grid))])
        last = functools.reduce(jnp.logical_and, [pl.program_id(d) == grid[d] - 1 for d in range(len(grid))])

        @pl.when(first)
        def _():
            ex.start(srcs, landing, sems)

        body(*ins, *outs, *scratch)

        @pl.when(last)
        def _():
            ex.wait(srcs, landing, sems)

    res = pl.pallas_call(
        hosting_body, name=name, grid=grid,
        in_specs=list(in_specs) + [ANY] * n, out_specs=out_specs + tuple([ANY] * n),
        out_shape=out_shape + tuple(ex.out_shape), scratch_shapes=scratch_shapes + ex.scratch,
        compiler_params=_params(("arbitrary",) * len(grid), vmem_mb),
    )(*inputs, *ex.arrays)
    return tuple(res[:n_out]), tuple(res[n_out:])


def _mm(a, b, mode, out_dtype, tm, tn, name, hosted=None, vmem_mb=24):
    if mode == "nn":
        (m, k), n = a.shape, b.shape[1]
        a_spec = pl.BlockSpec((tm, k), lambda i, j: (i, 0))
        b_spec = pl.BlockSpec((k, tn), lambda i, j: (0, j))
        dims = NN
    elif mode == "nt":
        (m, k), n = a.shape, b.shape[0]
        a_spec = pl.BlockSpec((tm, k), lambda i, j: (i, 0))
        b_spec = pl.BlockSpec((tn, k), lambda i, j: (j, 0))
        dims = NT
    else:
        (k, m), n = a.shape, b.shape[1]
        a_spec = pl.BlockSpec((k, tm), lambda i, j: (0, i))
        b_spec = pl.BlockSpec((k, tn), lambda i, j: (0, j))
        dims = TN
    assert m % tm == 0 and n % tn == 0, (m, n, tm, tn)

    def body(a_ref, b_ref, o_ref):
        o_ref[...] = _dot(a_ref[...], b_ref[...], dims).astype(o_ref.dtype)

    (out,), moved = _call(
        body, (a, b), name=name, grid=(m // tm, n // tn),
        in_specs=[a_spec, b_spec], out_specs=[pl.BlockSpec((tm, tn), lambda i, j: (i, j))],
        out_shape=[jax.ShapeDtypeStruct((m, n), out_dtype)], vmem_mb=vmem_mb, hosted=hosted)
    return (out, moved) if hosted else out


def _shift_down(x, k, fill):
    y = pltpu.roll(x, k, 0)
    row = lax.broadcasted_iota(jnp.int32, (SUB, x.shape[1]), 0)
    head = y[0:SUB, :]
    for t in range(k):
        head = jnp.where(row == t, fill[t], head)
    return jnp.concatenate([head, y[SUB:, :]], axis=0)


def _shift_up(x, k, fill):
    n = x.shape[0]
    y = pltpu.roll(x, n - k, 0)
    row = lax.broadcasted_iota(jnp.int32, (SUB, x.shape[1]), 0)
    tail = y[n - SUB:, :]
    for t in range(k):
        tail = jnp.where(row == SUB - k + t, fill[t], tail)
    return jnp.concatenate([y[:n - SUB, :], tail], axis=0)


def _conv_taps(x, halo, w):
    if halo is None:
        f1, f2 = [0.0], [0.0, 0.0]
    else:
        f1, f2 = [halo[7:8, :]], [halo[6:7, :], halo[7:8, :]]
    s1 = _shift_down(x, 1, f1)
    s2 = _shift_down(x, 2, f2)
    u = w[2:3, :] * x + w[1:2, :] * s1 + w[0:1, :] * s2
    return u, s1, s2


def _conv_taps_t(du, nxt, w):
    if nxt is None:
        f1, f2 = [0.0], [0.0, 0.0]
    else:
        f1, f2 = [nxt[0:1, :]], [nxt[0:1, :], nxt[1:2, :]]
    return w[2:3, :] * du + w[1:2, :] * _shift_up(du, 1, f1) + w[0:1, :] * _shift_up(du, 2, f2)


def _ada_fwd(c_all, w_ada, b_my):
    def body(c_ref, w_ref, b_ref, o_ref):
        cv = c_ref[...]
        act = cv * _sigmoid(cv)
        o_ref[...] = _dot(act, w_ref[...], NN, lax.Precision.HIGHEST) + b_ref[...]

    out = jax.ShapeDtypeStruct((NDEV, w_ada.shape[1]), F32)
    return pl.pallas_call(
        body, name="ada_fwd", grid=(1,),
        in_specs=[_full_spec(c_all.shape), _full_spec(w_ada.shape), _full_spec(b_my.shape)],
        out_specs=_full_spec(out.shape), out_shape=out, compiler_params=_params(("arbitrary",), 32),
    )(c_all, w_ada, b_my)


TR = 256
TRE = 512


def _row_spec(width, col=0, rows=TR):
    return pl.BlockSpec((rows, width), lambda i, col=col: (i, col))


def _erow(width):
    return _row_spec(width, rows=TRE)


def _full_spec(shape):
    return pl.BlockSpec(shape, lambda i: (0,) * len(shape))


def _norm_mod_fwd(x, mod, g):
    s = x.shape[0]

    def body(x_ref, mod_ref, g_ref, h_ref):
        xv = x_ref[...]
        r = lax.rsqrt(jnp.mean(xv * xv, axis=-1, keepdims=True) + EPS)
        nrm = xv * r * g_ref[...]
        h_ref[...] = (nrm * (1.0 + mod_ref[1:2, :]) + mod_ref[0:1, :]).astype(BF16)

    return pl.pallas_call(
        body, name="norm1_fwd", grid=(s // TRE,),
        in_specs=[_erow(D), _full_spec((SUB, D)), _full_spec((1, D))],
        out_specs=_erow(D), out_shape=jax.ShapeDtypeStruct((s, D), BF16),
        compiler_params=_params(("parallel",), 16),
    )(x, mod, g)


SLAB = 2 * DH
AUG_F, AUG_ONE, AUG_LSE = 0, 3, 6


def _split3(x):
    hi = x.astype(BF16).astype(F32)
    r1 = x - hi
    mid = r1.astype(BF16).astype(F32)
    return hi, mid, r1 - mid


def _lanes3(lane, first, pieces, other):
    out = other
    for k in range(3):
        out = jnp.where(lane == first + k, pieces[k], out)
    return out


def _aug_placement():
    eq = np.zeros((3 * LANES, HEADS * SLAB), np.float32)
    ek = np.zeros((3 * LANES, HEADS * SLAB), np.float32)
    ones = np.zeros((SUB, HEADS * SLAB), np.float32)
    for h in range(HEADS):
        aug = SLAB * h + DH
        for k in range(3):
            eq[LANES * k + h, aug + AUG_F + k] = 1.0
            ek[LANES * k + h, aug + AUG_ONE + k] = -1.0
            ones[0, aug + AUG_ONE + k] = 1.0
            ones[1, aug + AUG_F + k] = ones[1, aug + AUG_LSE + k] = 1.0
            ones[2, aug + k] = 1.0
    return jnp.asarray(eq, BF16), jnp.asarray(ek, BF16), jnp.asarray(ones)


FG_BLOCK = (3 * AW + 3 * CW) // LANES


def _qkv_prep(proj, bf_pad, gq, gk, cw_mix):
    s = proj.shape[0]

    def body(q_ref, k_ref, v_ref, fg_ref, b_ref, gq_ref, gk_ref, eq_ref, ek_ref, ones_ref, xin_ref, bg_ref, cg_ref,
             cw_ref, qo_ref, ko_ref, vo_ref, conv_ref, carry_ref, halo_ref):
        first = pl.program_id(0) == 0

        @pl.when(first)
        def _():
            carry_ref[...] = jnp.zeros_like(carry_ref)

        cx = cg_ref[...] * xin_ref[...]
        cv, _, _ = _conv_taps(cx, jnp.where(first, 0.0, halo_ref[...]), cw_ref[...])
        conv_ref[...] = bg_ref[...] * cv
        halo_ref[...] = cx[TR - SUB:TR, :]

        z = fg_ref[...] + b_ref[...]
        logf = jnp.minimum(z, 0.0) - jnp.log1p(jnp.exp(-jnp.abs(z)))
        row = lax.broadcasted_iota(jnp.int32, (TR, TR), 0)
        col = lax.broadcasted_iota(jnp.int32, (TR, TR), 1)
        fcum = _dot((col <= row).astype(F32), logf, NN, lax.Precision.HIGHEST) + carry_ref[0:1, :]
        carry_ref[...] = jnp.broadcast_to(fcum[TR - 1:TR, :], carry_ref.shape)
        f3 = jnp.concatenate(_split3(fcum), axis=1).astype(BF16)
        qo_ref[...] = (_dot(f3, eq_ref[...]) + ones_ref[0:1, :]).astype(BF16)
        ko_ref[...] = (_dot(f3, ek_ref[...]) + ones_ref[1:2, :]).astype(BF16)
        vo_ref[...] = jnp.broadcast_to(ones_ref[2:3, :], vo_ref.shape).astype(BF16)
        for h in range(HEADS):
            sl = slice(DH * h, DH * (h + 1))
            lo = slice(SLAB * h, SLAB * h + DH)
            qh = q_ref[:, sl]
            r = lax.rsqrt(jnp.mean(qh * qh, axis=-1, keepdims=True) + EPS)
            qo_ref[:, lo] = (qh * r * gq_ref[...] * QK_SCALE).astype(BF16)
            kh = k_ref[:, sl]
            r = lax.rsqrt(jnp.mean(kh * kh, axis=-1, keepdims=True) + EPS)
            ko_ref[:, lo] = (kh * r * gk_ref[...]).astype(BF16)
            vo_ref[:, lo] = v_ref[:, sl].astype(BF16)

    eq, ek, ones = _aug_placement()
    o = jax.ShapeDtypeStruct((s, HEADS * SLAB), BF16)
    wide = _row_spec(HEADS * SLAB)
    outs, _ = _call(
        body, (proj, proj, proj, proj, bf_pad, gq, gk, eq, ek, ones, proj, proj, proj, cw_mix), name="qkv_prep",
        grid=(s // TR,),
        in_specs=[_row_spec(AW, 0), _row_spec(AW, 1), _row_spec(AW, 2), _row_spec(LANES, FG_BLOCK),
                  _full_spec((1, LANES)), _full_spec((1, DH)), _full_spec((1, DH)), _full_spec(eq.shape),
                  _full_spec(ek.shape), _full_spec(ones.shape),
                  _row_spec(CW, 3), _row_spec(CW, 4), _row_spec(CW, 5), _full_spec((3, CW))],
        out_specs=[wide, wide, wide, _row_spec(CW)],
        out_shape=[o, o, o, jax.ShapeDtypeStruct((s, CW), F32)],
        scratch_shapes=[pltpu.VMEM((SUB, LANES), F32), pltpu.VMEM((SUB, CW), F32)], vmem_mb=24)
    return outs


def _resid_norm2(x, z, mod, g):
    s = x.shape[0]

    def body(x_ref, z_ref, mod_ref, g_ref, x1_ref, h_ref):
        x1 = x_ref[...] + mod_ref[2:3, :] * z_ref[...]
        x1_ref[...] = x1
        r = lax.rsqrt(jnp.mean(x1 * x1, axis=-1, keepdims=True) + EPS)
        nrm = x1 * r * g_ref[...]
        h_ref[...] = (nrm * (1.0 + mod_ref[4:5, :]) + mod_ref[3:4, :]).astype(BF16)

    return pl.pallas_call(
        body, name="resid_norm2", grid=(s // TRE,),
        in_specs=[_erow(D), _erow(D), _full_spec((SUB, D)), _full_spec((1, D))],
        out_specs=(_erow(D), _erow(D)),
        out_shape=(jax.ShapeDtypeStruct((s, D), F32), jax.ShapeDtypeStruct((s, D), BF16)),
        compiler_params=_params(("parallel",), 24),
    )(x, z, mod, g)


def _loss_head(x1, y, tgt, mod):
    s = x1.shape[0]

    def body(x1_ref, y_ref, t_ref, mod_ref, dout_ref, dy_ref, vec_ref):
        @pl.when(pl.program_id(0) == 0)
        def _():
            vec_ref[...] = jnp.zeros_like(vec_ref)

        yv = y_ref[...]
        g2 = mod_ref[5:6, :]
        diff = x1_ref[...] + g2 * yv - t_ref[...]
        dout = diff * (1.0 / D)
        dout_ref[...] = dout
        dy_ref[...] = (g2 * dout).astype(BF16)
        vec_ref[0:1, :] += jnp.sum(dout * yv, axis=0, keepdims=True)
        vec_ref[1:2, :] += jnp.sum(diff * diff, axis=0, keepdims=True)

    return pl.pallas_call(
        body, name="loss_head", grid=(s // TRE,),
        in_specs=[_erow(D), _erow(D), _erow(D), _full_spec((SUB, D))],
        out_specs=(_erow(D), _erow(D), _full_spec((SUB, D))),
        out_shape=(jax.ShapeDtypeStruct((s, D), F32), jax.ShapeDtypeStruct((s, D), BF16),
                   jax.ShapeDtypeStruct((SUB, D), F32)),
        compiler_params=_params(("arbitrary",), 24),
    )(x1, y, tgt, mod)


def _norm_mod_bwd(dh, xin, dres, zin, mod, g, scale_row, gate_row, name, hosted=None):
    s = dh.shape[0]
    with_gate = gate_row is not None

    def body(*refs):
        if with_gate:
            dh_ref, x_ref, dres_ref, z_ref, mod_ref, g_ref, dx_ref, dz_ref, vec_ref = refs
        else:
            dh_ref, x_ref, dres_ref, mod_ref, g_ref, dx_ref, vec_ref = refs

        @pl.when(pl.program_id(0) == 0)
        def _():
            vec_ref[...] = jnp.zeros_like(vec_ref)

        xv = x_ref[...]
        dhv = dh_ref[...]
        gv = g_ref[...]
        r = lax.rsqrt(jnp.mean(xv * xv, axis=-1, keepdims=True) + EPS)
        xh = xv * r
        dn = dhv * (1.0 + mod_ref[scale_row:scale_row + 1, :])
        dxh = dn * gv
        dx = dres_ref[...] + r * (dxh - xh * jnp.mean(dxh * xh, axis=-1, keepdims=True))
        dx_ref[...] = dx
        vec_ref[0:1, :] += jnp.sum(dhv, axis=0, keepdims=True)
        vec_ref[1:2, :] += jnp.sum(dhv * (xh * gv), axis=0, keepdims=True)
        vec_ref[2:3, :] += jnp.sum(dn * xh, axis=0, keepdims=True)
        if with_gate:
            dz_ref[...] = (mod_ref[gate_row:gate_row + 1, :] * dx).astype(BF16)
            vec_ref[3:4, :] += jnp.sum(dx * z_ref[...], axis=0, keepdims=True)

    ins = [dh, xin, dres] + ([zin] if with_gate else []) + [mod, g]
    in_specs = [_erow(D)] * (4 if with_gate else 3) + [_full_spec((SUB, D)), _full_spec((1, D))]
    out_specs = [_erow(D)] + ([_erow(D)] if with_gate else []) + [_full_spec((SUB, D))]
    out_shape = [jax.ShapeDtypeStruct((s, D), F32)] + ([jax.ShapeDtypeStruct((s, D), BF16)] if with_gate else []) \
        + [jax.ShapeDtypeStruct((SUB, D), F32)]
    outs, moved = _call(body, ins, name=name, grid=(s // TRE,), in_specs=in_specs, out_specs=out_specs,
                        out_shape=out_shape, vmem_mb=32, hosted=hosted)
    return outs + (moved,) if hosted else outs


TA = 512
NEG = -1e30


def _causal_mask():
    row = lax.broadcasted_iota(jnp.int32, (TA, TA), 0)
    col = lax.broadcasted_iota(jnp.int32, (TA, TA), 1)
    return col <= row


def _attn_fwd(qp, kp, vp):
    s = qp.shape[0]
    nq = s // TA

    def body(q_ref, k_ref, v_ref, o_ref, lse_ref):
        i = pl.program_id(1)
        slabs = [slice(SLAB * hh, SLAB * (hh + 1)) for hh in range(2)]
        q = [q_ref[:, sl] for sl in slabs]

        def block(j, carry, masked):
            keys = pl.ds(pl.multiple_of(j * TA, TA), TA)
            ms, acc = carry
            m_out, parts = [], []
            for hh in range(2):
                sc = _dot(q[hh], k_ref[keys, slabs[hh]], NT)
                if masked:
                    sc = jnp.where(_causal_mask(), sc, NEG)
                m_new = jnp.maximum(ms[hh], jnp.max(sc, axis=-1, keepdims=True))
                p = jnp.exp(sc - m_new)
                parts.append(jnp.exp(ms[hh] - m_new) * acc[:, slabs[hh]] + _dot(p.astype(BF16), v_ref[keys, slabs[hh]]))
                m_out.append(m_new)
            return tuple(m_out), jnp.concatenate(parts, axis=1)

        init = ((jnp.full((TA, 1), NEG, F32), jnp.full((TA, 1), NEG, F32)), jnp.zeros((TA, 2 * SLAB), F32))
        carry = lax.fori_loop(0, i, lambda j, cr: block(j, cr, False), init)
        ms, acc = block(i, carry, True)
        for hh in range(2):
            l = acc[:, SLAB * hh + DH:SLAB * hh + DH + 1]
            o_ref[:, DH * hh:DH * (hh + 1)] = acc[:, SLAB * hh:SLAB * hh + DH] / l
            lse_ref[0, :, hh:hh + 1] = ms[hh] + jnp.log(l)

    (o, lse), _ = _call(
        body, (qp, kp, vp), name="attn_fwd", grid=(HEADS // 2, nq),
        in_specs=[pl.BlockSpec((TA, 2 * SLAB), lambda p, i: (i, p)),
                  pl.BlockSpec((s, 2 * SLAB), lambda p, i: (0, p)),
                  pl.BlockSpec((s, 2 * SLAB), lambda p, i: (0, p))],
        out_specs=[pl.BlockSpec((TA, LANES), lambda p, i: (i, p)), pl.BlockSpec((1, TA, 2), lambda p, i: (p, i, 0))],
        out_shape=[jax.ShapeDtypeStruct((s, AW), F32), jax.ShapeDtypeStruct((HEADS // 2, s, 2), F32)],
        vmem_mb=24)
    return o, lse


def _attn_bwd(qp, kp, vp, dmixed, o, lse, hosted):
    s = qp.shape[0]
    nq = s // TA

    def body(q_ref, k_ref, v_ref, do_ref, o_ref, lse_ref, dq_ref, dk_ref, dv_ref, qb_ref, dob_ref):
        dk_ref[...] = jnp.zeros_like(dk_ref)
        dv_ref[...] = jnp.zeros_like(dv_ref)
        slabs = [slice(SLAB * hh, SLAB * (hh + 1)) for hh in range(2)]
        lane = lax.broadcasted_iota(jnp.int32, (TA, DH), 1)

        def q_block(i, _):
            i0 = pl.multiple_of(i * TA, TA)
            rows = pl.ds(i0, TA)
            for hh in range(2):
                half = slice(DH * hh, DH * (hh + 1))
                do = do_ref[rows, half]
                delta = jnp.sum(do * o_ref[rows, half], axis=-1, keepdims=True)
                dob_ref[hh, :, 0:DH] = do.astype(BF16)
                dob_ref[hh, :, DH:SLAB] = _lanes3(lane, 0, [-d for d in _split3(delta)], 0.0).astype(BF16)
                lse3 = _split3(lse_ref[0, rows, hh:hh + 1])
                qb_ref[hh, :, 0:DH] = q_ref[rows, SLAB * hh:SLAB * hh + DH]
                aug = q_ref[rows, SLAB * hh + DH:SLAB * (hh + 1)].astype(F32)
                qb_ref[hh, :, DH:SLAB] = _lanes3(lane, AUG_LSE, [-x for x in lse3], aug).astype(BF16)

            def block(j, dq, masked):
                keys = pl.ds(pl.multiple_of(j * TA, TA), TA)
                dv, dk, dqc = [], [], []
                for hh in range(2):
                    q, dob = qb_ref[hh], dob_ref[hh]
                    k = k_ref[keys, slabs[hh]]
                    sc = _dot(q, k, NT)
                    if masked:
                        sc = jnp.where(_causal_mask(), sc, NEG)
                    p = jnp.exp(sc)
                    dv.append(_dot(p.astype(BF16), dob, TN))
                    ds = (p * _dot(dob, v_ref[keys, slabs[hh]], NT)).astype(BF16)
                    dk.append(_dot(ds, q, TN))
                    dqc.append(_dot(ds, k))
                dv_ref[keys, :] += jnp.concatenate(dv, axis=1)
                dk_ref[keys, :] += jnp.concatenate(dk, axis=1)
                return dq + jnp.concatenate(dqc, axis=1)

            dq = lax.fori_loop(0, i, lambda j, acc: block(j, acc, False), jnp.zeros((TA, 2 * SLAB), F32))
            dq_ref[rows, :] = block(i, dq, True)
            return 0

        lax.fori_loop(0, nq, q_block, 0)

    pair = lambda p: (0, p)
    slab2 = pl.BlockSpec((s, 2 * SLAB), pair)
    seq = pl.BlockSpec((s, LANES), pair)
    small = pl.BlockSpec((1, s, 2), lambda p: (p, 0, 0))
    o32 = jax.ShapeDtypeStruct((s, HEADS * SLAB), F32)
    return _call(
        body, (qp, kp, vp, dmixed, o, lse), name="attn_bwd", grid=(HEADS // 2,),
        in_specs=[slab2, slab2, slab2, seq, seq, small], out_specs=[slab2, slab2, slab2], out_shape=[o32, o32, o32],
        scratch_shapes=[pltpu.VMEM((2, TA, SLAB), BF16), pltpu.VMEM((2, TA, SLAB), BF16)], vmem_mb=40, hosted=hosted)


def _qkv_post(dqp, dkp, dvp, dmixed, proj, bf_pad, gq, gk, cw_mix):
    s = proj.shape[0]
    nb = s // TR

    def body(dq_ref, dk_ref, dv_ref, q_ref, k_ref, fg_ref, b_ref, gq_ref, gk_ref, dc_ref, xin_ref, bg_ref, cg_ref,
             xinh_ref, cgh_ref, cw_ref, dqo_ref, dko_ref, dvo_ref, dfg_ref, vec_ref, dxin_ref, dbg_ref, dcg_ref, dcw_ref,
             carry_ref, nxt_ref):
        i = pl.program_id(0)

        @pl.when(i == 0)
        def _():
            vec_ref[...] = jnp.zeros_like(vec_ref)
            carry_ref[...] = jnp.zeros_like(carry_ref)
            dcw_ref[...] = jnp.zeros_like(dcw_ref)

        wv = cw_ref[...]
        xin, cg, dconv = xin_ref[...], cg_ref[...], dc_ref[...]
        cx = cg * xin
        cv, s1, s2 = _conv_taps(cx, jnp.where(i < nb - 1, cgh_ref[...] * xinh_ref[...], 0.0), wv)
        dbg_ref[...] = (dconv * cv).astype(BF16)
        dcv = dconv * bg_ref[...]
        dcw_ref[0:1, :] += jnp.sum(dcv * s2, axis=0, keepdims=True)
        dcw_ref[1:2, :] += jnp.sum(dcv * s1, axis=0, keepdims=True)
        dcw_ref[2:3, :] += jnp.sum(dcv * cx, axis=0, keepdims=True)
        dcx = _conv_taps_t(dcv, jnp.where(i > 0, nxt_ref[...], 0.0), wv)
        nxt_ref[...] = dcv[0:SUB, :]
        dcg_ref[...] = (dcx * xin).astype(BF16)
        dxin_ref[...] = (dcx * cg).astype(BF16)

        def one(d_ref, x_ref, g_ref, o_ref, row, scale):
            dg = jnp.zeros((1, DH), F32)
            for h in range(HEADS):
                sl = slice(DH * h, DH * (h + 1))
                xv = x_ref[:, sl]
                r = lax.rsqrt(jnp.mean(xv * xv, axis=-1, keepdims=True) + EPS)
                xh = xv * r
                dn = d_ref[:, SLAB * h:SLAB * h + DH] * scale
                dg = dg + jnp.sum(dn * xh, axis=0, keepdims=True)
                dxh = dn * g_ref[...]
                o_ref[:, sl] = (r * (dxh - xh * jnp.mean(dxh * xh, axis=-1, keepdims=True))).astype(BF16)
            vec_ref[row:row + 1, 0:DH] += dg

        one(dq_ref, q_ref, gq_ref, dqo_ref, 0, QK_SCALE)
        one(dk_ref, k_ref, gk_ref, dko_ref, 1, 1.0)
        lane = lax.broadcasted_iota(jnp.int32, (TR, LANES), 1)
        df = jnp.zeros((TR, LANES), F32)
        for h in range(HEADS):
            dvo_ref[:, DH * h:DH * (h + 1)] = dv_ref[:, SLAB * h:SLAB * h + DH].astype(BF16)
            row_sum = dq_ref[:, SLAB * h + DH:SLAB * h + DH + 1]
            col_sum = dk_ref[:, SLAB * h + DH + AUG_ONE:SLAB * h + DH + AUG_ONE + 1]
            df = jnp.where(lane == h, row_sum - col_sum, df)
        row = lax.broadcasted_iota(jnp.int32, (TR, TR), 0)
        col = lax.broadcasted_iota(jnp.int32, (TR, TR), 1)
        dlogf = _dot((col >= row).astype(F32), df, NN, lax.Precision.HIGHEST) + carry_ref[0:1, :]
        carry_ref[...] = jnp.broadcast_to(dlogf[0:1, :], carry_ref.shape)
        dfg = dlogf * _sigmoid(-(fg_ref[...] + b_ref[...]))
        dfg_ref[...] = dfg.astype(BF16)
        vec_ref[2:3, :] += jnp.sum(dfg, axis=0, keepdims=True)

    o = jax.ShapeDtypeStruct((s, AW), BF16)
    rev = lambda width, col=0: pl.BlockSpec((TR, width), lambda i, col=col: (nb - 1 - i, col))
    wide = rev(HEADS * SLAB)
    halo = lambda col: pl.BlockSpec((SUB, CW), lambda i, col=col: (jnp.maximum((nb - 1 - i) * (TR // SUB) - 1, 0), col))
    outs, _ = _call(
        body, (dqp, dkp, dvp, proj, proj, proj, bf_pad, gq, gk, dmixed, proj, proj, proj, proj, proj, cw_mix),
        name="qkv_post", grid=(nb,),
        in_specs=[wide, wide, wide, rev(AW, 0), rev(AW, 1), rev(LANES, FG_BLOCK), _full_spec((1, LANES)),
                  _full_spec((1, DH)), _full_spec((1, DH)),
                  rev(CW, 1), rev(CW, 3), rev(CW, 4), rev(CW, 5), halo(3), halo(5), _full_spec((3, CW))],
        out_specs=[rev(AW), rev(AW), rev(AW), rev(LANES), _full_spec((SUB, LANES)),
                   rev(CW), rev(CW), rev(CW), _full_spec((SUB, CW))],
        out_shape=[o, o, o, jax.ShapeDtypeStruct((s, LANES), BF16), jax.ShapeDtypeStruct((SUB, LANES), F32),
                   o, o, o, jax.ShapeDtypeStruct((SUB, CW), F32)],
        scratch_shapes=[pltpu.VMEM((SUB, LANES), F32), pltpu.VMEM((SUB, CW), F32)], vmem_mb=32)
    return outs


TF = 256
NJ = DFF // TF
FFN_ROWS_FWD = 1024
FFN_ROWS_BWD = 1024


def _ffn_fwd(h2, wup_t, cw, wd):
    s = h2.shape[0]
    tr = FFN_ROWS_FWD
    nr = s // tr

    def body(h_ref, wu_ref, cg_ref, cv_ref, wd_ref, pg_ref, pv_ref, y_ref, halo_ref, act_ref):
        r, j = pl.program_id(0), pl.program_id(1)
        hv = h_ref[...]
        pg = _dot(hv, wu_ref[0], NT).astype(BF16)
        pv = _dot(hv, wu_ref[1], NT).astype(BF16)
        pg_ref[...] = pg
        pv_ref[...] = pv
        pgf, pvf = pg.astype(F32), pv.astype(F32)
        ug, _, _ = _conv_taps(pgf, jnp.where(r > 0, halo_ref[j, 0], 0.0), cg_ref[...])
        uv, _, _ = _conv_taps(pvf, jnp.where(r > 0, halo_ref[j, 1], 0.0), cv_ref[...])
        halo_ref[j, 0] = pgf[tr - SUB:tr, :]
        halo_ref[j, 1] = pvf[tr - SUB:tr, :]
        act = (ug * _sigmoid(ug) * uv).astype(BF16)
        for t in range(NJ):
            @pl.when(j == t)
            def _(t=t):
                act_ref[:, t * TF:(t + 1) * TF] = act

        @pl.when(j == NJ - 1)
        def _():
            y_ref[...] = _dot(act_ref[...], wd_ref[...])

    pre = jax.ShapeDtypeStruct((s, DFF), BF16)
    return pl.pallas_call(
        body, name="ffn_fwd", grid=(nr, NJ),
        in_specs=[pl.BlockSpec((tr, D), lambda r, j: (r, 0)),
                  pl.BlockSpec((2, TF, D), lambda r, j: (0, j, 0)),
                  pl.BlockSpec((3, TF), lambda r, j: (0, j)),
                  pl.BlockSpec((3, TF), lambda r, j: (0, NJ + j)),
                  pl.BlockSpec((DFF, D), lambda r, j: (0, 0))],
        out_specs=(pl.BlockSpec((tr, TF), lambda r, j: (r, j)),
                   pl.BlockSpec((tr, TF), lambda r, j: (r, j)),
                   pl.BlockSpec((tr, D), lambda r, j: (r, 0))),
        out_shape=(pre, pre, jax.ShapeDtypeStruct((s, D), F32)),
        scratch_shapes=[pltpu.VMEM((NJ, 2, SUB, TF), F32), pltpu.VMEM((tr, DFF), BF16)],
        compiler_params=_params(("arbitrary", "arbitrary"), 56),
    )(h2, wup_t, cw, cw, wd)


def _ffn_bwd(dy, h2, pre_g, pre_v, wup_t, cw, wd):
    s = h2.shape[0]
    tr = FFN_ROWS_BWD
    nr = s // tr
    hb = tr // (2 * SUB)

    def body(dy_ref, h_ref, pg_ref, pv_ref, hg_ref, hv_ref, wu_ref, cg_ref, cv_ref, wd_ref,
             dh_ref, dwu_ref, dwd_ref, dcg_ref, dcv_ref, nxt_ref, awu_ref, awd_ref):
        j, r = pl.program_id(0), pl.program_id(1)
        rr = nr - 1 - r
        row0 = pl.multiple_of(rr * tr, tr)
        cwg, cwv = cg_ref[...], cv_ref[...]
        pg, pv = pg_ref[...].astype(F32), pv_ref[...].astype(F32)
        ug, g1, g2 = _conv_taps(pg, jnp.where(rr > 0, hg_ref[SUB:2 * SUB, :].astype(F32), 0.0), cwg)
        uv, v1, v2 = _conv_taps(pv, jnp.where(rr > 0, hv_ref[SUB:2 * SUB, :].astype(F32), 0.0), cwv)
        sg = _sigmoid(ug)
        sil = ug * sg
        act = (sil * uv).astype(BF16)
        dyv = dy_ref[...]
        da = _dot(dyv, wd_ref[...], NT)
        dug = da * uv * (sg * (1.0 + ug * (1.0 - sg)))
        duv = da * sil
        dpg = _conv_taps_t(dug, jnp.where(r > 0, nxt_ref[0], 0.0), cwg)
        dpv = _conv_taps_t(duv, jnp.where(r > 0, nxt_ref[1], 0.0), cwv)
        nxt_ref[0] = dug[0:SUB, :]
        nxt_ref[1] = duv[0:SUB, :]
        dpgb, dpvb = dpg.astype(BF16), dpv.astype(BF16)
        hv = h_ref[...]
        dwd = _dot(act, dyv, TN)
        dpb = jnp.concatenate([dpgb, dpvb], axis=1)
        dwu = _dot(dpb, hv, TN)
        dh = _dot(dpb, wu_ref[...].reshape(2 * TF, D))

        def taps(du, x0, x1, x2):
            return (jnp.sum(du * x2, axis=0, keepdims=True), jnp.sum(du * x1, axis=0, keepdims=True),
                    jnp.sum(du * x0, axis=0, keepdims=True))

        tg, tv = taps(dug, pg, g1, g2), taps(duv, pv, v1, v2)

        @pl.when(r == 0)
        def _():
            awd_ref[...] = dwd
            awu_ref[...] = dwu
            dcg_ref[...] = jnp.zeros_like(dcg_ref)
            dcv_ref[...] = jnp.zeros_like(dcv_ref)

        @pl.when(r > 0)
        def _():
            awd_ref[...] += dwd
            awu_ref[...] += dwu

        @pl.when(r == nr - 1)
        def _():
            dwd_ref[...] = awd_ref[...].astype(BF16)
            dwu_ref[...] = awu_ref[...].astype(BF16).reshape(2, TF, D)

        for t in range(3):
            dcg_ref[t:t + 1, :] += tg[t]
            dcv_ref[t:t + 1, :] += tv[t]

        @pl.when(j == 0)
        def _():
            dh_ref[pl.ds(row0, tr), :] = dh

        @pl.when(j > 0)
        def _():
            dh_ref[pl.ds(row0, tr), :] += dh

    rows = lambda j, r: (nr - 1 - r, 0)
    tile = lambda j, r: (nr - 1 - r, j)
    halo = lambda j, r: (jnp.maximum((nr - 1 - r) * hb - 1, 0), j)
    return pl.pallas_call(
        body, name="ffn_bwd", grid=(NJ, nr),
        in_specs=[pl.BlockSpec((tr, D), rows), pl.BlockSpec((tr, D), rows),
                  pl.BlockSpec((tr, TF), tile), pl.BlockSpec((tr, TF), tile),
                  pl.BlockSpec((2 * SUB, TF), halo), pl.BlockSpec((2 * SUB, TF), halo),
                  pl.BlockSpec((2, TF, D), lambda j, r: (0, j, 0)),
                  pl.BlockSpec((3, TF), lambda j, r: (0, j)), pl.BlockSpec((3, TF), lambda j, r: (0, NJ + j)),
                  pl.BlockSpec((TF, D), lambda j, r: (j, 0))],
        out_specs=(pl.BlockSpec((s, D), lambda j, r: (0, 0)),
                   pl.BlockSpec((2, TF, D), lambda j, r: (0, j, 0)),
                   pl.BlockSpec((TF, D), lambda j, r: (j, 0)),
                   pl.BlockSpec((SUB, TF), lambda j, r: (0, j)), pl.BlockSpec((SUB, TF), lambda j, r: (0, j))),
        out_shape=(jax.ShapeDtypeStruct((s, D), F32),
                   jax.ShapeDtypeStruct((2, DFF, D), BF16), jax.ShapeDtypeStruct((DFF, D), BF16),
                   jax.ShapeDtypeStruct((SUB, DFF), F32), jax.ShapeDtypeStruct((SUB, DFF), F32)),
        scratch_shapes=[pltpu.VMEM((2, SUB, TF), F32), pltpu.VMEM((2 * TF, D), F32), pltpu.VMEM((TF, D), F32)],
        compiler_params=_params(("arbitrary", "arbitrary"), 56),
    )(dy, h2, pre_g, pre_v, pre_g, pre_v, wup_t, cw, cw, wd)


def _adam(w, g, m, v):
    m = ADAM_B1 * m + (1.0 - ADAM_B1) * g
    v = ADAM_B2 * v + (1.0 - ADAM_B2) * (g * g)
    m_hat = m / (1.0 - ADAM_B1 ** ADAM_STEP)
    v_hat = v / (1.0 - ADAM_B2 ** ADAM_STEP)
    delta = -ADAM_LR * (m_hat / (jnp.sqrt(v_hat) + ADAM_EPS) + ADAM_WD * w)
    return delta, m, v


NCHIP = NDEV // 2


def _pair_add(mine, theirs, tr, name):
    _, _, rws, cols = mine.shape

    def body(a_ref, b_ref, o_ref):
        c = lax.axis_index("c")
        o_ref[0] = (a_ref[0, c].astype(F32) + b_ref[0].astype(F32)).astype(BF16)

    (out,), _ = _call(
        body, (mine, theirs), name=name, grid=(NCHIP, rws // tr),
        in_specs=[pl.BlockSpec((1, 2, tr, cols), lambda q, i: (q, 0, i, 0)),
                  pl.BlockSpec((1, tr, cols), lambda q, i: (q, i, 0))],
        out_specs=[pl.BlockSpec((1, tr, cols), lambda q, i: (q, i, 0))],
        out_shape=[jax.ShapeDtypeStruct((NCHIP, rws, cols), BF16)], vmem_mb=16)
    return out


def _adamw_sharded(parts, w, m, v, tr, name, hosted=None):
    rws, cols = w.shape
    n_parts = parts.shape[0]

    def body(p_ref, w_ref, m_ref, v_ref, g_ref, d_ref, mo_ref, vo_ref):
        g = p_ref[0].astype(F32)
        for q in range(1, n_parts):
            g = g + p_ref[q].astype(F32)
        g_ref[...] = g
        d_ref[...], mo_ref[...], vo_ref[...] = _adam(w_ref[...], g, m_ref[...], v_ref[...])

    blk = pl.BlockSpec((tr, cols), lambda i: (i, 0))
    o = jax.ShapeDtypeStruct((rws, cols), F32)
    outs, moved = _call(
        body, (parts, w, m, v), name=name, grid=(rws // tr,),
        in_specs=[pl.BlockSpec((n_parts, tr, cols), lambda i: (0, i, 0)), blk, blk, blk],
        out_specs=[blk, blk, blk, blk], out_shape=[o, o, o, o], vmem_mb=44 if tr > 256 else 24, hosted=hosted)
    return (outs, moved) if hosted else outs


def _adamw_ada(c_all, dmod_my, w, m, v):
    rws, cols = w.shape
    tr = 256

    def body(c_ref, dm_ref, w_ref, m_ref, v_ref, g_ref, d_ref, mo_ref, vo_ref):
        cv = c_ref[...]
        act = cv * _sigmoid(cv)
        g = _dot(act, dm_ref[...], TN, lax.Precision.HIGHEST)
        g_ref[...] = g
        d_ref[...], mo_ref[...], vo_ref[...] = _adam(w_ref[...], g, m_ref[...], v_ref[...])

    blk = pl.BlockSpec((tr, cols), lambda i: (i, 0))
    o = jax.ShapeDtypeStruct((rws, cols), F32)
    return pl.pallas_call(
        body, name="adamw_ada", grid=(rws // tr,),
        in_specs=[pl.BlockSpec((NDEV, tr), lambda i: (0, i)), _full_spec((NDEV, cols)), blk, blk, blk],
        out_specs=(blk, blk, blk, blk), out_shape=(o, o, o, o),
        compiler_params=_params(("parallel",), 32),
    )(c_all, dmod_my, w, m, v)


REP_ROWS = 16
ROW_N1, ROW_N2, ROW_LOSS, ROW_MISC = 6, 7, 8, 9
LANE_BF, LANE_GQ, LANE_GK = 0, 128, 256


def _adamw_small(rep_all, conv_all, wmv):
    n_ff = wmv[6][0].shape[1]

    def body(*refs):
        rep_ref, conv_ref = refs[:2]
        ins = refs[2:2 + 24]
        outs = refs[2 + 24:]
        loss_ref, outs = outs[0], outs[1:]
        g_rep = rep_ref[0]
        g_conv = conv_ref[0]
        for d in range(1, NDEV):
            g_rep = g_rep + rep_ref[d]
            g_conv = g_conv + conv_ref[d]
        loss_ref[...] = (0.5 / D) * jnp.sum(g_rep[ROW_LOSS:ROW_LOSS + 1, :], axis=-1, keepdims=True)
        grads = [
            None,
            g_rep[ROW_N1:ROW_N1 + 1, :],
            g_rep[ROW_MISC:ROW_MISC + 1, LANE_BF:LANE_BF + HEADS],
            g_rep[ROW_MISC:ROW_MISC + 1, LANE_GQ:LANE_GQ + DH],
            g_rep[ROW_MISC:ROW_MISC + 1, LANE_GK:LANE_GK + DH],
            g_rep[ROW_N2:ROW_N2 + 1, :],
            g_conv[0:3, 0:n_ff],
            g_conv[0:3, n_ff:n_ff + DH],
        ]
        for p in range(8):
            w_ref, m_ref, v_ref = ins[3 * p:3 * p + 3]
            g_ref, d_ref, mo_ref, vo_ref = outs[4 * p:4 * p + 4]
            if p == 0:
                for nmod in range(NMOD):
                    sl = slice(D * nmod, D * (nmod + 1))
                    g = g_rep[nmod:nmod + 1, :]
                    g_ref[:, sl] = g
                    d_ref[:, sl], mo_ref[:, sl], vo_ref[:, sl] = _adam(w_ref[:, sl], g, m_ref[:, sl], v_ref[:, sl])
            else:
                g = grads[p]
                g_ref[...] = g
                d_ref[...], mo_ref[...], vo_ref[...] = _adam(w_ref[...], g, m_ref[...], v_ref[...])

    flat = [a for trio in wmv for a in trio]
    out_shape = [jax.ShapeDtypeStruct((1, 1), F32)]
    for trio in wmv:
        out_shape += [jax.ShapeDtypeStruct(trio[0].shape, F32)] * 4
    ins = [rep_all, conv_all] + flat
    return pl.pallas_call(
        body, name="adamw_small", grid=(1,),
        in_specs=[_full_spec(a.shape) for a in ins], out_specs=tuple(_full_spec(o.shape) for o in out_shape),
        out_shape=tuple(out_shape), compiler_params=_params(("arbitrary",), 32),
    )(*ins)


FG_FIRST = 3 * AW
N_IN = DIN // NDEV


def _w_in_runs():
    runs = []
    for d in range(NDEV):
        lo, hi = N_IN * d, N_IN * (d + 1)
        for a, b, shift in ((0, FG_FIRST, 0), (FG_FIRST, FG_FIRST + HEADS, DIN - HEADS - FG_FIRST),
                            (FG_FIRST + HEADS, DIN, -HEADS)):
            a, b = max(a, lo), min(b, hi)
            if a < b:
                runs.append((d, a - lo, a + shift, b - a))
    return runs


W_IN_ROWS = 256
N_IN_PAD = 512


def _identity(n):
    return (lax.broadcasted_iota(jnp.int32, (n, n), 0) == lax.broadcasted_iota(jnp.int32, (n, n), 1)).astype(BF16)


def _assemble_w_in(g_in, hosted):
    def body(g_ref, o_ref, t_ref):
        eye = _identity(W_IN_ROWS)
        shard = None
        for d, src, dst, width in _w_in_runs():
            if d != shard:
                t_ref[:, 0:N_IN] = _dot(eye, g_ref[d], NT).astype(BF16)
                shard = d
            o_ref[:, dst:dst + width] = t_ref[:, src:src + width]
        o_ref[:, DIN:DINP] = jnp.zeros((W_IN_ROWS, DINP - DIN), o_ref.dtype)

    (out,), moved = _call(
        body, (g_in,), name="assemble_w_in", grid=(D // W_IN_ROWS,),
        in_specs=[pl.BlockSpec((NDEV, N_IN, W_IN_ROWS), lambda i: (0, 0, i))],
        out_specs=[pl.BlockSpec((W_IN_ROWS, DINP), lambda i: (i, 0))],
        out_shape=[jax.ShapeDtypeStruct((D, DINP), g_in.dtype)],
        scratch_shapes=[pltpu.VMEM((W_IN_ROWS, N_IN_PAD), BF16)], vmem_mb=16, hosted=hosted)
    return out, moved


def _scatter_dw_in(dwp):
    def body(w_ref, o_ref, t_ref):
        eye = _identity(W_IN_ROWS)
        runs = _w_in_runs()
        for i, (d, src, dst, width) in enumerate(runs):
            t_ref[:, src:src + width] = w_ref[:, dst:dst + width]
            if i + 1 == len(runs) or runs[i + 1][0] != d:
                o_ref[d // 2, d % 2] = _dot(t_ref[:, 0:N_IN], eye, TN).astype(BF16)

    (out,), _ = _call(
        body, (dwp,), name="scatter_dw_in", grid=(D // W_IN_ROWS,),
        in_specs=[pl.BlockSpec((W_IN_ROWS, DINP), lambda i: (i, 0))],
        out_specs=[pl.BlockSpec((NCHIP, 2, N_IN, W_IN_ROWS), lambda i: (0, 0, 0, i))],
        out_shape=[jax.ShapeDtypeStruct((NCHIP, 2, N_IN, D), dwp.dtype)],
        scratch_shapes=[pltpu.VMEM((W_IN_ROWS, N_IN_PAD), BF16)], vmem_mb=16)
    return out


def kernel(x, c, w_ada, b_ada, norm1_g, w_in, b_forget, q_norm_g, k_norm_g, conv_mix_w, w_out, norm2_g, w_up, ffn_conv_w, w_down, loss_target, m_w_ada, m_b_ada, m_norm1_g, m_w_in, m_b_forget, m_q_norm_g, m_k_norm_g, m_conv_mix_w, m_w_out, m_norm2_g, m_w_up, m_ffn_conv_w, m_w_down, v_w_ada, v_b_ada, v_norm1_g, v_w_in, v_b_forget, v_q_norm_g, v_k_norm_g, v_conv_mix_w, v_w_out, v_norm2_g, v_w_up, v_ffn_conv_w, v_w_down):
    me = 4 * lax.axis_index("x") + 2 * lax.axis_index("y") + lax.axis_index("c")
    xs, tgt = x[0], loss_target[0]
    n_ada = w_ada.shape[2]
    n_ff = w_up.shape[2]

    conv_w = jnp.concatenate([ffn_conv_w[0], conv_mix_w[0]], axis=1)
    conv_w = jnp.concatenate([conv_w, jnp.zeros((SUB - 3, conv_w.shape[1]), F32)], axis=0)
    c_all, conv_all, g_in = _exchange(
        [(c.reshape(SUB, D // SUB), "ag"), (conv_w, "ag"), (jnp.transpose(w_in[0]).astype(BF16), "ag2")],
        "exchange_w_in")
    g_in, w_out_b, w_up_b, w_down_b = lax.optimization_barrier(
        (g_in, w_out[0].astype(BF16), jnp.transpose(w_up[0]).astype(BF16), w_down[0].astype(BF16)))
    g_out, g_up, g_down = _sequencer_exchange(
        [(w_out_b, "ag2"), (w_up_b, "ag2"), (w_down_b, "ag2")], "gather_weights", collective_id=1)
    c_all = c_all.reshape(NDEV, D)
    cw_ffn = jnp.transpose(conv_all[:, :3, :n_ff], (1, 0, 2)).reshape(3, 2 * DFF)
    cw_mix = jnp.transpose(conv_all[:, :3, n_ff:], (1, 0, 2)).reshape(3, CW)

    b_my = lax.dynamic_slice(b_ada, (0, me * n_ada), (1, n_ada))
    mod_part = _ada_fwd(c_all, w_ada[0], b_my)
    w_in_p, (mod_rows,) = _assemble_w_in(
        g_in, [(jnp.broadcast_to(mod_part[:, None, :], (NDEV, SUB, n_ada)), "a2a")])
    mod = mod_rows[:, 0, :].reshape(NMOD, D)
    mod = jnp.concatenate([mod, jnp.zeros((SUB - NMOD, D), F32)], axis=0)

    h = _norm_mod_fwd(xs, mod, norm1_g)
    proj = _mm(h, w_in_p, "nn", F32, 1024, 640, "proj_fwd")
    bf_pad = jnp.concatenate([b_forget, jnp.zeros((1, LANES - HEADS), F32)], axis=1)
    qp, kp, vp, conv = _qkv_prep(proj, bf_pad, q_norm_g, k_norm_g, cw_mix)
    attn, lse = _attn_fwd(qp, kp, vp)
    w_out_f = g_out.reshape(D, D)
    w_up_t = g_up.reshape(2, DFF, D)
    w_down_f = g_down.reshape(DFF, D)
    mixed = jnp.concatenate([attn, conv], axis=1).astype(BF16)
    z = _mm(mixed, w_out_f, "nn", F32, 1024, 1024, "out_fwd")
    x1, h2 = _resid_norm2(xs, z, mod, norm2_g)
    pre_g, pre_v, y = _ffn_fwd(h2, w_up_t, cw_ffn, w_down_f)
    dout, dy, vec_l = _loss_head(x1, y, tgt, mod)

    dh2, dwup_t, dwd, dcw_g, dcw_v = _ffn_bwd(dy, h2, pre_g, pre_v, w_up_t, cw_ffn, w_down_f)
    s_down = dwd.reshape(NCHIP, 2, DFF // NDEV, D)
    s_up = dwup_t.reshape(NCHIP, 2, n_ff, D)
    dx1, dz, vec_2, (t_up, t_down) = _norm_mod_bwd(dh2, x1, dout, z, mod, norm2_g, 4, 2, "norm2_bwd",
                                                   hosted=[(s_up, "pair"), (s_down, "pair")])
    dwout = _mm(mixed, dz, "tn", BF16, 1024, 1024, "out_bwd_w")
    s_out = dwout.reshape(NCHIP, 2, D // NDEV, D)
    dmixed, (t_out,) = _mm(dz, w_out_f, "nt", F32, 1024, 1024, "out_bwd_x", hosted=[(s_out, "pair")])
    c_out = _pair_add(s_out, t_out, 128, "pair_add_out")
    c_up = _pair_add(s_up, t_up, 176, "pair_add_up")
    c_down = _pair_add(s_down, t_down, 176, "pair_add_down")
    (dqp, dkp, dvp), (p_up, p_down, p_out) = _attn_bwd(
        qp, kp, vp, dmixed, attn, lse, [(c_up, "chips"), (c_down, "chips"), (c_out, "chips")])
    dq, dk, dvb, dfg, vec_qk, dxin, dbg, dcg, dcw_mix = _qkv_post(
        dqp, dkp, dvp, dmixed, proj, bf_pad, q_norm_g, k_norm_g, cw_mix)
    dproj = jnp.concatenate([dq, dk, dvb, dxin, dbg, dcg, dfg], axis=1)
    dwin_p = _mm(h, dproj, "tn", BF16, 1024, 640, "proj_bwd_w")
    s_in = _scatter_dw_in(dwin_p).reshape(NDEV, N_IN, D)
    (p_in,) = _sequencer_exchange([(s_in, "a2a")], "scatter_dw_in_partials", collective_id=2, all_peers=True)
    dh = _mm(dproj, w_in_p, "nt", F32, 1024, 512, "proj_bwd_x", vmem_mb=36)
    grad_x, vec_1 = _norm_mod_bwd(dh, xs, dx1, None, mod, norm1_g, 1, None, "norm1_bwd")

    gap = lambda n: jnp.zeros((1, n), F32)
    misc = jnp.concatenate([
        vec_qk[2:3, :HEADS], gap(LANE_GQ - LANE_BF - HEADS), vec_qk[0:1, :DH], gap(LANE_GK - LANE_GQ - DH),
        vec_qk[1:2, :DH], gap(D - LANE_GK - DH)], axis=1)
    rep = jnp.concatenate([
        vec_1[0:1], vec_1[1:2], vec_2[3:4], vec_2[0:1], vec_2[1:2], vec_l[0:1],
        vec_1[2:3], vec_2[2:3], vec_l[1:2], misc, jnp.zeros((REP_ROWS - 10, D), F32)], axis=0)
    dcw_ffn = jnp.concatenate([dcw_g, dcw_v], axis=1).reshape(SUB, NDEV, n_ff)
    dcw_all = jnp.concatenate([jnp.transpose(dcw_ffn, (1, 0, 2)),
                               jnp.transpose(dcw_mix.reshape(SUB, NDEV, DH), (1, 0, 2))], axis=2)
    r_up = _adamw_sharded(p_up, jnp.transpose(w_up[0]), jnp.transpose(m_w_up[0]), jnp.transpose(v_w_up[0]), 176,
                          "adamw_up")
    r_down = _adamw_sharded(p_down, w_down[0], m_w_down[0], v_w_down[0], 176, "adamw_down")
    rep, dcw_all, r_up, r_down = lax.optimization_barrier((rep, dcw_all, r_up, r_down))
    r_up = tuple(jnp.transpose(a) for a in r_up)
    r_out, (rep_all, conv_parts) = _adamw_sharded(p_out, w_out[0], m_w_out[0], v_w_out[0], 128, "adamw_out",
                                                  hosted=[(rep, "ag"), (dcw_all, "a2a")])
    dmod_my = lax.dynamic_slice(rep_all[:, :NMOD, :].reshape(NDEV, NMOD * D), (0, me * n_ada), (NDEV, n_ada))
    r_ada = _adamw_ada(c_all, dmod_my, w_ada[0], m_w_ada[0], v_w_ada[0])
    r_in = _adamw_sharded(p_in, jnp.transpose(w_in[0]), jnp.transpose(m_w_in[0]), jnp.transpose(v_w_in[0]), N_IN,
                          "adamw_in")
    r_in = tuple(jnp.transpose(a) for a in r_in)
    small = _adamw_small(rep_all, conv_parts, [
        [b_ada, m_b_ada, v_b_ada], [norm1_g, m_norm1_g, v_norm1_g], [b_forget, m_b_forget, v_b_forget],
        [q_norm_g, m_q_norm_g, v_q_norm_g], [k_norm_g, m_k_norm_g, v_k_norm_g], [norm2_g, m_norm2_g, v_norm2_g],
        [ffn_conv_w[0], m_ffn_conv_w[0], v_ffn_conv_w[0]], [conv_mix_w[0], m_conv_mix_w[0], v_conv_mix_w[0]]])
    loss = small[0].reshape(())
    r_bada, r_n1, r_bf, r_gq, r_gk, r_n2, r_cf, r_cm = [small[1 + 4 * p:5 + 4 * p] for p in range(8)]
    lead = lambda t: tuple(a[None] for a in t)
    per_w = [lead(r_ada), r_bada, r_n1, lead(r_in), r_bf, r_gq, r_gk, lead(r_cm), lead(r_out), r_n2,
             lead(r_up), lead(r_cf), lead(r_down)]
    outs = [loss, grad_x[None]]
    for field in range(4):
        outs += [t[field] for t in per_w]
    return tuple(outs)
```

```python
import functools

import jax
import jax.numpy as jnp
import numpy as np
from jax import lax
from jax.experimental import pallas as pl
from jax.experimental.pallas import tpu as pltpu
from jax.experimental.pallas import tpu_sc as plsc

F32 = jnp.float32
BF16 = jnp.bfloat16

NDEV = 8
D = 1024
HEADS = 8
DH = 64
AW = 512
CW = 512
DFF = 2816
DIN = 3080
DINP = 3200
NMOD = 6
EPS = 1e-6
QK_SCALE = 0.125
LANES = 128
SUB = 8

ADAM_LR = 0.001
ADAM_B1 = 0.9
ADAM_B2 = 0.999
ADAM_EPS = 1e-08
ADAM_WD = 0.01
ADAM_STEP = 10

MESH = pl.DeviceIdType.MESH
ANY = pl.BlockSpec(memory_space=pl.ANY)

NN = (((1,), (0,)), ((), ()))
NT = (((1,), (1,)), ((), ()))
TN = (((0,), (0,)), ((), ()))


def _dot(a, b, dims=NN, precision=None):
    return lax.dot_general(a, b, dims, precision=precision, preferred_element_type=F32)


def _params(sem=None, vmem_mb=None):
    kw = {}
    if sem is not None:
        kw["dimension_semantics"] = sem
    if vmem_mb is not None:
        kw["vmem_limit_bytes"] = vmem_mb * 1024 * 1024
    return pltpu.CompilerParams(**kw)


def _sigmoid(x):
    return 0.5 * jnp.tanh(0.5 * x) + 0.5


class _Exchange:
    def __init__(self, items):
        self.arrays = [pltpu.with_memory_space_constraint(a, pltpu.HBM) for a, _ in items]
        self.modes = [m for _, m in items]
        self.n = len(items)
        self.out_shape = []
        for a, m in items:
            sh = {"ag": (NDEV,) + a.shape, "ag2": (NDEV,) + a.shape, "pair": a.shape[:1] + a.shape[2:]}.get(m, a.shape)
            self.out_shape.append(jax.ShapeDtypeStruct(sh, a.dtype))
        self.scratch = [pltpu.SemaphoreType.DMA((self.n, NDEV - 1)), pltpu.SemaphoreType.DMA((self.n, NDEV - 1)),
                        pltpu.SemaphoreType.DMA((self.n,))]

    def _plan(self, srcs, outs, sems):
        send_sems, recv_sems, loc_sems = sems
        x, y, c = lax.axis_index("x"), lax.axis_index("y"), lax.axis_index("c")
        me, my_chip = 4 * x + 2 * y + c, 2 * x + y
        sib = (x, y, 1 - c)
        local, first, landed, forwards, arrivals = [], [], [], [], []

        def remote(a, k, src, dst, to):
            return pltpu.make_async_remote_copy(src_ref=src, dst_ref=dst, send_sem=send_sems.at[a, k],
                                                recv_sem=recv_sems.at[a, k], device_id=to, device_id_type=MESH)

        for a, mode in enumerate(self.modes):
            src, out = srcs[a], outs[a]
            if mode in ("ag", "a2a"):
                piece = (lambda slot, src=src: src) if mode == "ag" else (lambda slot, src=src: src.at[slot])
                local.append(pltpu.make_async_copy(piece(me), out.at[me], loc_sems.at[a]))
                for r in range(1, NDEV):
                    px = 1 - x if (r >> 2) & 1 else x
                    py = 1 - y if (r >> 1) & 1 else y
                    pc = 1 - c if r & 1 else c
                    pidx = 4 * px + 2 * py + pc
                    first.append(remote(a, r - 1, piece(pidx), out.at[me], (px, py, pc)))
                    arrivals.append(remote(a, r - 1, piece(pidx), out.at[pidx], (px, py, pc)))
            elif mode == "ag2":
                local.append(pltpu.make_async_copy(src, out.at[me], loc_sems.at[a]))
                first.append(remote(a, 0, src, out.at[me], sib))
                arrivals.append(remote(a, 0, src, out.at[me + 1 - 2 * c], sib))
                for j, (px, py) in enumerate([(1 - x, y), (x, 1 - y), (1 - x, 1 - y)]):
                    theirs = out.at[4 * px + 2 * py + c]
                    first.append(remote(a, 1 + j, src, out.at[me], (px, py, c)))
                    landed.append(remote(a, 1 + j, src, theirs, (px, py, c)))
                    forwards.append(remote(a, 4 + j, theirs, theirs, sib))
                    arrivals.append(remote(a, 4 + j, src, out.at[4 * px + 2 * py + 1 - c], sib))
            elif mode == "pair":
                for q in range(NDEV // 2):
                    first.append(remote(a, q, src.at[q, 1 - c], out.at[q], sib))
                    arrivals.append(remote(a, q, src.at[q, 1 - c], out.at[q], sib))
            else:
                assert mode == "chips", mode
                local.append(pltpu.make_async_copy(src.at[my_chip], out.at[my_chip], loc_sems.at[a]))
                for j, (px, py) in enumerate([(1 - x, y), (x, 1 - y), (1 - x, 1 - y)]):
                    q = 2 * px + py
                    first.append(remote(a, 1 + j, src.at[q], out.at[my_chip], (px, py, c)))
                    arrivals.append(remote(a, 1 + j, src.at[q], out.at[q], (px, py, c)))
        return local, first, landed, forwards, arrivals

    def start(self, srcs, outs, sems):
        local, first, _, _, _ = self._plan(srcs, outs, sems)
        for cp in local + first:
            cp.start()

    def wait(self, srcs, outs, sems):
        local, first, landed, forwards, arrivals = self._plan(srcs, outs, sems)
        for cp, fwd in zip(landed, forwards):
            cp.wait_recv()
            fwd.start()
        for cp in arrivals:
            cp.wait_recv()
        for cp in first + forwards:
            cp.wait_send()
        for cp in local:
            cp.wait()


def _exchange(items, name):
    ex = _Exchange(items)
    n = ex.n

    def body(*refs):
        srcs, outs, sems = refs[:n], refs[n:2 * n], refs[2 * n:]
        ex.start(srcs, outs, sems)
        ex.wait(srcs, outs, sems)

    return pl.pallas_call(
        body, name=name,
        out_shape=tuple(ex.out_shape),
        in_specs=[ANY] * n, out_specs=tuple([ANY] * n),
        scratch_shapes=ex.scratch,
        compiler_params=pltpu.CompilerParams(has_side_effects=True),
    )(*ex.arrays)


def _sequencer_exchange(items, name, collective_id, all_peers=False):
    ex = _Exchange(items)
    srcs = [jax.new_ref(a, memory_space=pltpu.MemorySpace.HBM) for a in ex.arrays]
    outs = [jax.empty_ref(sh, memory_space=pltpu.MemorySpace.HBM) for sh in ex.out_shape]

    @pl.kernel(mesh=plsc.ScalarSubcoreMesh(axis_name="sequencer", num_cores=1), name=name,
               scratch_types=tuple(ex.scratch), compiler_params=pltpu.CompilerParams(collective_id=collective_id))
    def launch(send_sems, recv_sems, loc_sems):
        x, y, c = lax.axis_index("x"), lax.axis_index("y"), lax.axis_index("c")
        barrier = pltpu.get_barrier_semaphore()
        peers = [(x, y, 1 - c), (1 - x, y, c), (x, 1 - y, c), (1 - x, 1 - y, c)]
        if all_peers:
            peers += [(1 - x, y, 1 - c), (x, 1 - y, 1 - c), (1 - x, 1 - y, 1 - c)]
        for peer in peers:
            pl.semaphore_signal(barrier, inc=1, device_id=peer, device_id_type=MESH)
        pl.semaphore_wait(barrier, len(peers))
        sems = (send_sems, recv_sems, loc_sems)
        ex.start(srcs, outs, sems)
        ex.wait(srcs, outs, sems)

    launch()
    return [o[...] for o in outs]


def _call(body, inputs, *, name, grid, in_specs, out_specs, out_shape, scratch_shapes=(), vmem_mb=None, hosted=None):
    out_specs, out_shape, scratch_shapes = tuple(out_specs), tuple(out_shape), list(scratch_shapes)
    if not hosted:
        res = pl.pallas_call(
            body, name=name, grid=grid, in_specs=list(in_specs), out_specs=out_specs, out_shape=out_shape,
            scratch_shapes=scratch_shapes, compiler_params=_params(("arbitrary",) * len(grid), vmem_mb),
        )(*inputs)
        return tuple(res), ()
    ex = _Exchange(hosted)
    n, n_in, n_out, n_scr = ex.n, len(inputs), len(out_shape), len(scratch_shapes)

    def hosting_body(*refs):
        ins, srcs = refs[:n_in], refs[n_in:n_in + n]
        outs, landing = refs[n_in + n:n_in + n + n_out], refs[n_in + n + n_out:n_in + 2 * n + n_out]
        scratch, sems = refs[n_in + 2 * n + n_out:n_in + 2 * n + n_out + n_scr], refs[n_in + 2 * n + n_out + n_scr:]
        first = functools.reduce(jnp.logical_and, [pl.program_id(d) == 0 for d in range(len(grid))])
        last = functools.reduce(jnp.logical_and, [pl.program_id(d) == grid[d] - 1 for d in range(len(grid))])

        @pl.when(first)
        def _():
            ex.start(srcs, landing, sems)

        body(*ins, *outs, *scratch)

        @pl.when(last)
        def _():
            ex.wait(srcs, landing, sems)

    res = pl.pallas_call(
        hosting_body, name=name, grid=grid,
        in_specs=list(in_specs) + [ANY] * n, out_specs=out_specs + tuple([ANY] * n),
        out_shape=out_shape + tuple(ex.out_shape), scratch_shapes=scratch_shapes + ex.scratch,
        compiler_params=_params(("arbitrary",) * len(grid), vmem_mb),
    )(*inputs, *ex.arrays)
    return tuple(res[:n_out]), tuple(res[n_out:])


def _mm(a, b, mode, out_dtype, tm, tn, name, hosted=None, vmem_mb=24):
    if mode == "nn":
        (m, k), n = a.shape, b.shape[1]
        a_spec = pl.BlockSpec((tm, k), lambda i, j: (i, 0))
        b_spec = pl.BlockSpec((k, tn), lambda i, j: (0, j))
        dims = NN
    elif mode == "nt":
        (m, k), n = a.shape, b.shape[0]
        a_spec = pl.BlockSpec((tm, k), lambda i, j: (i, 0))
        b_spec = pl.BlockSpec((tn, k), lambda i, j: (j, 0))
        dims = NT
    else:
        (k, m), n = a.shape, b.shape[1]
        a_spec = pl.BlockSpec((k, tm), lambda i, j: (0, i))
        b_spec = pl.BlockSpec((k, tn), lambda i, j: (0, j))
        dims = TN
    assert m % tm == 0 and n % tn == 0, (m, n, tm, tn)

    def body(a_ref, b_ref, o_ref):
        o_ref[...] = _dot(a_ref[...], b_ref[...], dims).astype(o_ref.dtype)

    (out,), moved = _call(
        body, (a, b), name=name, grid=(m // tm, n // tn),
        in_specs=[a_spec, b_spec], out_specs=[pl.BlockSpec((tm, tn), lambda i, j: (i, j))],
        out_shape=[jax.ShapeDtypeStruct((m, n), out_dtype)], vmem_mb=vmem_mb, hosted=hosted)
    return (out, moved) if hosted else out


def _shift_down(x, k, fill):
    y = pltpu.roll(x, k, 0)
    row = lax.broadcasted_iota(jnp.int32, (SUB, x.shape[1]), 0)
    head = y[0:SUB, :]
    for t in range(k):
        head = jnp.where(row == t, fill[t], head)
    return jnp.concatenate([head, y[SUB:, :]], axis=0)


def _shift_up(x, k, fill):
    n = x.shape[0]
    y = pltpu.roll(x, n - k, 0)
    row = lax.broadcasted_iota(jnp.int32, (SUB, x.shape[1]), 0)
    tail = y[n - SUB:, :]
    for t in range(k):
        tail = jnp.where(row == SUB - k + t, fill[t], tail)
    return jnp.concatenate([y[:n - SUB, :], tail], axis=0)


def _conv_taps(x, halo, w):
    if halo is None:
        f1, f2 = [0.0], [0.0, 0.0]
    else:
        f1, f2 = [halo[7:8, :]], [halo[6:7, :], halo[7:8, :]]
    s1 = _shift_down(x, 1, f1)
    s2 = _shift_down(x, 2, f2)
    u = w[2:3, :] * x + w[1:2, :] * s1 + w[0:1, :] * s2
    return u, s1, s2


def _conv_taps_t(du, nxt, w):
    if nxt is None:
        f1, f2 = [0.0], [0.0, 0.0]
    else:
        f1, f2 = [nxt[0:1, :]], [nxt[0:1, :], nxt[1:2, :]]
    return w[2:3, :] * du + w[1:2, :] * _shift_up(du, 1, f1) + w[0:1, :] * _shift_up(du, 2, f2)


def _ada_fwd(c_all, w_ada, b_my):
    def body(c_ref, w_ref, b_ref, o_ref):
        cv = c_ref[...]
        act = cv * _sigmoid(cv)
        o_ref[...] = _dot(act, w_ref[...], NN, lax.Precision.HIGHEST) + b_ref[...]

    out = jax.ShapeDtypeStruct((NDEV, w_ada.shape[1]), F32)
    return pl.pallas_call(
        body, name="ada_fwd", grid=(1,),
        in_specs=[_full_spec(c_all.shape), _full_spec(w_ada.shape), _full_spec(b_my.shape)],
        out_specs=_full_spec(out.shape), out_shape=out, compiler_params=_params(("arbitrary",), 32),
    )(c_all, w_ada, b_my)


TR = 256
TRE = 512


def _row_spec(width, col=0, rows=TR):
    return pl.BlockSpec((rows, width), lambda i, col=col: (i, col))


def _erow(width):
    return _row_spec(width, rows=TRE)


def _full_spec(shape):
    return pl.BlockSpec(shape, lambda i: (0,) * len(shape))


def _norm_mod_fwd(x, mod, g):
    s = x.shape[0]

    def body(x_ref, mod_ref, g_ref, h_ref):
        xv = x_ref[...]
        r = lax.rsqrt(jnp.mean(xv * xv, axis=-1, keepdims=True) + EPS)
        nrm = xv * r * g_ref[...]
        h_ref[...] = (nrm * (1.0 + mod_ref[1:2, :]) + mod_ref[0:1, :]).astype(BF16)

    return pl.pallas_call(
        body, name="norm1_fwd", grid=(s // TRE,),
        in_specs=[_erow(D), _full_spec((SUB, D)), _full_spec((1, D))],
        out_specs=_erow(D), out_shape=jax.ShapeDtypeStruct((s, D), BF16),
        compiler_params=_params(("parallel",), 16),
    )(x, mod, g)


SLAB = 2 * DH
AUG_F, AUG_ONE, AUG_LSE = 0, 3, 6


def _split3(x):
    hi = x.astype(BF16).astype(F32)
    r1 = x - hi
    mid = r1.astype(BF16).astype(F32)
    return hi, mid, r1 - mid


def _lanes3(lane, first, pieces, other):
    out = other
    for k in range(3):
        out = jnp.where(lane == first + k, pieces[k], out)
    return out


def _aug_placement():
    eq = np.zeros((3 * LANES, HEADS * SLAB), np.float32)
    ek = np.zeros((3 * LANES, HEADS * SLAB), np.float32)
    ones = np.zeros((SUB, HEADS * SLAB), np.float32)
    for h in range(HEADS):
        aug = SLAB * h + DH
        for k in range(3):
            eq[LANES * k + h, aug + AUG_F + k] = 1.0
            ek[LANES * k + h, aug + AUG_ONE + k] = -1.0
            ones[0, aug + AUG_ONE + k] = 1.0
            ones[1, aug + AUG_F + k] = ones[1, aug + AUG_LSE + k] = 1.0
            ones[2, aug + k] = 1.0
    return jnp.asarray(eq, BF16), jnp.asarray(ek, BF16), jnp.asarray(ones)


FG_BLOCK = (3 * AW + 3 * CW) // LANES


def _qkv_prep(proj, bf_pad, gq, gk, cw_mix):
    s = proj.shape[0]

    def body(q_ref, k_ref, v_ref, fg_ref, b_ref, gq_ref, gk_ref, eq_ref, ek_ref, ones_ref, xin_ref, bg_ref, cg_ref,
             cw_ref, qo_ref, ko_ref, vo_ref, conv_ref, carry_ref, halo_ref):
        first = pl.program_id(0) == 0

        @pl.when(first)
        def _():
            carry_ref[...] = jnp.zeros_like(carry_ref)

        cx = cg_ref[...] * xin_ref[...]
        cv, _, _ = _conv_taps(cx, jnp.where(first, 0.0, halo_ref[...]), cw_ref[...])
        conv_ref[...] = bg_ref[...] * cv
        halo_ref[...] = cx[TR - SUB:TR, :]

        z = fg_ref[...] + b_ref[...]
        logf = jnp.minimum(z, 0.0) - jnp.log1p(jnp.exp(-jnp.abs(z)))
        row = lax.broadcasted_iota(jnp.int32, (TR, TR), 0)
        col = lax.broadcasted_iota(jnp.int32, (TR, TR), 1)
        fcum = _dot((col <= row).astype(F32), logf, NN, lax.Precision.HIGHEST) + carry_ref[0:1, :]
        carry_ref[...] = jnp.broadcast_to(fcum[TR - 1:TR, :], carry_ref.shape)
        f3 = jnp.concatenate(_split3(fcum), axis=1).astype(BF16)
        qo_ref[...] = (_dot(f3, eq_ref[...]) + ones_ref[0:1, :]).astype(BF16)
        ko_ref[...] = (_dot(f3, ek_ref[...]) + ones_ref[1:2, :]).astype(BF16)
        vo_ref[...] = jnp.broadcast_to(ones_ref[2:3, :], vo_ref.shape).astype(BF16)
        for h in range(HEADS):
            sl = slice(DH * h, DH * (h + 1))
            lo = slice(SLAB * h, SLAB * h + DH)
            qh = q_ref[:, sl]
            r = lax.rsqrt(jnp.mean(qh * qh, axis=-1, keepdims=True) + EPS)
            qo_ref[:, lo] = (qh * r * gq_ref[...] * QK_SCALE).astype(BF16)
            kh = k_ref[:, sl]
            r = lax.rsqrt(jnp.mean(kh * kh, axis=-1, keepdims=True) + EPS)
            ko_ref[:, lo] = (kh * r * gk_ref[...]).astype(BF16)
            vo_ref[:, lo] = v_ref[:, sl].astype(BF16)

    eq, ek, ones = _aug_placement()
    o = jax.ShapeDtypeStruct((s, HEADS * SLAB), BF16)
    wide = _row_spec(HEADS * SLAB)
    outs, _ = _call(
        body, (proj, proj, proj, proj, bf_pad, gq, gk, eq, ek, ones, proj, proj, proj, cw_mix), name="qkv_prep",
        grid=(s // TR,),
        in_specs=[_row_spec(AW, 0), _row_spec(AW, 1), _row_spec(AW, 2), _row_spec(LANES, FG_BLOCK),
                  _full_spec((1, LANES)), _full_spec((1, DH)), _full_spec((1, DH)), _full_spec(eq.shape),
                  _full_spec(ek.shape), _full_spec(ones.shape),
                  _row_spec(CW, 3), _row_spec(CW, 4), _row_spec(CW, 5), _full_spec((3, CW))],
        out_specs=[wide, wide, wide, _row_spec(CW)],
        out_shape=[o, o, o, jax.ShapeDtypeStruct((s, CW), F32)],
        scratch_shapes=[pltpu.VMEM((SUB, LANES), F32), pltpu.VMEM((SUB, CW), F32)], vmem_mb=24)
    return outs


def _resid_norm2(x, z, mod, g):
    s = x.shape[0]

    def body(x_ref, z_ref, mod_ref, g_ref, x1_ref, h_ref):
        x1 = x_ref[...] + mod_ref[2:3, :] * z_ref[...]
        x1_ref[...] = x1
        r = lax.rsqrt(jnp.mean(x1 * x1, axis=-1, keepdims=True) + EPS)
        nrm = x1 * r * g_ref[...]
        h_ref[...] = (nrm * (1.0 + mod_ref[4:5, :]) + mod_ref[3:4, :]).astype(BF16)

    return pl.pallas_call(
        body, name="resid_norm2", grid=(s // TRE,),
        in_specs=[_erow(D), _erow(D), _full_spec((SUB, D)), _full_spec((1, D))],
        out_specs=(_erow(D), _erow(D)),
        out_shape=(jax.ShapeDtypeStruct((s, D), F32), jax.ShapeDtypeStruct((s, D), BF16)),
        compiler_params=_params(("parallel",), 24),
    )(x, z, mod, g)


def _loss_head(x1, y, tgt, mod):
    s = x1.shape[0]

    def body(x1_ref, y_ref, t_ref, mod_ref, dout_ref, dy_ref, vec_ref):
        @pl.when(pl.program_id(0) == 0)
        def _():
            vec_ref[...] = jnp.zeros_like(vec_ref)

        yv = y_ref[...]
        g2 = mod_ref[5:6, :]
        diff = x1_ref[...] + g2 * yv - t_ref[...]
        dout = diff * (1.0 / D)
        dout_ref[...] = dout
        dy_ref[...] = (g2 * dout).astype(BF16)
        vec_ref[0:1, :] += jnp.sum(dout * yv, axis=0, keepdims=True)
        vec_ref[1:2, :] += jnp.sum(diff * diff, axis=0, keepdims=True)

    return pl.pallas_call(
        body, name="loss_head", grid=(s // TRE,),
        in_specs=[_erow(D), _erow(D), _erow(D), _full_spec((SUB, D))],
        out_specs=(_erow(D), _erow(D), _full_spec((SUB, D))),
        out_shape=(jax.ShapeDtypeStruct((s, D), F32), jax.ShapeDtypeStruct((s, D), BF16),
                   jax.ShapeDtypeStruct((SUB, D), F32)),
        compiler_params=_params(("arbitrary",), 24),
    )(x1, y, tgt, mod)


def _norm_mod_bwd(dh, xin, dres, zin, mod, g, scale_row, gate_row, name, hosted=None):
    s = dh.shape[0]
    with_gate = gate_row is not None

    def body(*refs):
        if with_gate:
            dh_ref, x_ref, dres_ref, z_ref, mod_ref, g_ref, dx_ref, dz_ref, vec_ref = refs
        else:
            dh_ref, x_ref, dres_ref, mod_ref, g_ref, dx_ref, vec_ref = refs

        @pl.when(pl.program_id(0) == 0)
        def _():
            vec_ref[...] = jnp.zeros_like(vec_ref)

        xv = x_ref[...]
        dhv = dh_ref[...]
        gv = g_ref[...]
        r = lax.rsqrt(jnp.mean(xv * xv, axis=-1, keepdims=True) + EPS)
        xh = xv * r
        dn = dhv * (1.0 + mod_ref[scale_row:scale_row + 1, :])
        dxh = dn * gv
        dx = dres_ref[...] + r * (dxh - xh * jnp.mean(dxh * xh, axis=-1, keepdims=True))
        dx_ref[...] = dx
        vec_ref[0:1, :] += jnp.sum(dhv, axis=0, keepdims=True)
        vec_ref[1:2, :] += jnp.sum(dhv * (xh * gv), axis=0, keepdims=True)
        vec_ref[2:3, :] += jnp.sum(dn * xh, axis=0, keepdims=True)
        if with_gate:
            dz_ref[...] = (mod_ref[gate_row:gate_row + 1, :] * dx).astype(BF16)
            vec_ref[3:4, :] += jnp.sum(dx * z_ref[...], axis=0, keepdims=True)

    ins = [dh, xin, dres] + ([zin] if with_gate else []) + [mod, g]
    in_specs = [_erow(D)] * (4 if with_gate else 3) + [_full_spec((SUB, D)), _full_spec((1, D))]
    out_specs = [_erow(D)] + ([_erow(D)] if with_gate else []) + [_full_spec((SUB, D))]
    out_shape = [jax.ShapeDtypeStruct((s, D), F32)] + ([jax.ShapeDtypeStruct((s, D), BF16)] if with_gate else []) \
        + [jax.ShapeDtypeStruct((SUB, D), F32)]
    outs, moved = _call(body, ins, name=name, grid=(s // TRE,), in_specs=in_specs, out_specs=out_specs,
                        out_shape=out_shape, vmem_mb=32, hosted=hosted)
    return outs + (moved,) if hosted else outs


TA = 512
NEG = -1e30


def _causal_mask():
    row = lax.broadcasted_iota(jnp.int32, (TA, TA), 0)
    col = lax.broadcasted_iota(jnp.int32, (TA, TA), 1)
    return col <= row


def _attn_fwd(qp, kp, vp):
    s = qp.shape[0]
    nq = s // TA

    def body(q_ref, k_ref, v_ref, o_ref, lse_ref):
        i = pl.program_id(1)
        slabs = [slice(SLAB * hh, SLAB * (hh + 1)) for hh in range(2)]
        q = [q_ref[:, sl] for sl in slabs]

        def block(j, carry, masked):
            keys = pl.ds(pl.multiple_of(j * TA, TA), TA)
            ms, acc = carry
            m_out, parts = [], []
            for hh in range(2):
                sc = _dot(q[hh], k_ref[keys, slabs[hh]], NT)
                if masked:
                    sc = jnp.where(_causal_mask(), sc, NEG)
                m_new = jnp.maximum(ms[hh], jnp.max(sc, axis=-1, keepdims=True))
                p = jnp.exp(sc - m_new)
                parts.append(jnp.exp(ms[hh] - m_new) * acc[:, slabs[hh]] + _dot(p.astype(BF16), v_ref[keys, slabs[hh]]))
                m_out.append(m_new)
            return tuple(m_out), jnp.concatenate(parts, axis=1)

        init = ((jnp.full((TA, 1), NEG, F32), jnp.full((TA, 1), NEG, F32)), jnp.zeros((TA, 2 * SLAB), F32))
        carry = lax.fori_loop(0, i, lambda j, cr: block(j, cr, False), init)
        ms, acc = block(i, carry, True)
        for hh in range(2):
            l = acc[:, SLAB * hh + DH:SLAB * hh + DH + 1]
            o_ref[:, DH * hh:DH * (hh + 1)] = acc[:, SLAB * hh:SLAB * hh + DH] / l
            lse_ref[0, :, hh:hh + 1] = ms[hh] + jnp.log(l)

    (o, lse), _ = _call(
        body, (qp, kp, vp), name="attn_fwd", grid=(HEADS // 2, nq),
        in_specs=[pl.BlockSpec((TA, 2 * SLAB), lambda p, i: (i, p)),
                  pl.BlockSpec((s, 2 * SLAB), lambda p, i: (0, p)),
                  pl.BlockSpec((s, 2 * SLAB), lambda p, i: (0, p))],
        out_specs=[pl.BlockSpec((TA, LANES), lambda p, i: (i, p)), pl.BlockSpec((1, TA, 2), lambda p, i: (p, i, 0))],
        out_shape=[jax.ShapeDtypeStruct((s, AW), F32), jax.ShapeDtypeStruct((HEADS // 2, s, 2), F32)],
        vmem_mb=24)
    return o, lse


def _attn_bwd(qp, kp, vp, dmixed, o, lse, hosted):
    s = qp.shape[0]
    nq = s // TA

    def body(q_ref, k_ref, v_ref, do_ref, o_ref, lse_ref, dq_ref, dk_ref, dv_ref, qb_ref, dob_ref):
        dk_ref[...] = jnp.zeros_like(dk_ref)
        dv_ref[...] = jnp.zeros_like(dv_ref)
        slabs = [slice(SLAB * hh, SLAB * (hh + 1)) for hh in range(2)]
        lane = lax.broadcasted_iota(jnp.int32, (TA, DH), 1)

        def q_block(i, _):
            i0 = pl.multiple_of(i * TA, TA)
            rows = pl.ds(i0, TA)
            for hh in range(2):
                half = slice(DH * hh, DH * (hh + 1))
                do = do_ref[rows, half]
                delta = jnp.sum(do * o_ref[rows, half], axis=-1, keepdims=True)
                dob_ref[hh, :, 0:DH] = do.astype(BF16)
                dob_ref[hh, :, DH:SLAB] = _lanes3(lane, 0, [-d for d in _split3(delta)], 0.0).astype(BF16)
                lse3 = _split3(lse_ref[0, rows, hh:hh + 1])
                qb_ref[hh, :, 0:DH] = q_ref[rows, SLAB * hh:SLAB * hh + DH]
                aug = q_ref[rows, SLAB * hh + DH:SLAB * (hh + 1)].astype(F32)
                qb_ref[hh, :, DH:SLAB] = _lanes3(lane, AUG_LSE, [-x for x in lse3], aug).astype(BF16)

            def block(j, dq, masked):
                keys = pl.ds(pl.multiple_of(j * TA, TA), TA)
                dv, dk, dqc = [], [], []
                for hh in range(2):
                    q, dob = qb_ref[hh], dob_ref[hh]
                    k = k_ref[keys, slabs[hh]]
                    sc = _dot(q, k, NT)
                    if masked:
                        sc = jnp.where(_causal_mask(), sc, NEG)
                    p = jnp.exp(sc)
                    dv.append(_dot(p.astype(BF16), dob, TN))
                    ds = (p * _dot(dob, v_ref[keys, slabs[hh]], NT)).astype(BF16)
                    dk.append(_dot(ds, q, TN))
                    dqc.append(_dot(ds, k))
                dv_ref[keys, :] += jnp.concatenate(dv, axis=1)
                dk_ref[keys, :] += jnp.concatenate(dk, axis=1)
                return dq + jnp.concatenate(dqc, axis=1)

            dq = lax.fori_loop(0, i, lambda j, acc: block(j, acc, False), jnp.zeros((TA, 2 * SLAB), F32))
            dq_ref[rows, :] = block(i, dq, True)
            return 0

        lax.fori_loop(0, nq, q_block, 0)

    pair = lambda p: (0, p)
    slab2 = pl.BlockSpec((s, 2 * SLAB), pair)
    seq = pl.BlockSpec((s, LANES), pair)
    small = pl.BlockSpec((1, s, 2), lambda p: (p, 0, 0))
    o32 = jax.ShapeDtypeStruct((s, HEADS * SLAB), F32)
    return _call(
        body, (qp, kp, vp, dmixed, o, lse), name="attn_bwd", grid=(HEADS // 2,),
        in_specs=[slab2, slab2, slab2, seq, seq, small], out_specs=[slab2, slab2, slab2], out_shape=[o32, o32, o32],
        scratch_shapes=[pltpu.VMEM((2, TA, SLAB), BF16), pltpu.VMEM((2, TA, SLAB), BF16)], vmem_mb=40, hosted=hosted)


def _qkv_post(dqp, dkp, dvp, dmixed, proj, bf_pad, gq, gk, cw_mix):
    s = proj.shape[0]
    nb = s // TR

    def body(dq_ref, dk_ref, dv_ref, q_ref, k_ref, fg_ref, b_ref, gq_ref, gk_ref, dc_ref, xin_ref, bg_ref, cg_ref,
             xinh_ref, cgh_ref, cw_ref, dp_ref, vec_ref, dcw_ref, carry_ref, nxt_ref):
        i = pl.program_id(0)
        col_k, col_v, col_xin, col_bg, col_cg, col_fg = AW, 2 * AW, 3 * AW, 3 * AW + CW, 3 * AW + 2 * CW, 3 * AW + 3 * CW

        @pl.when(i == 0)
        def _():
            vec_ref[...] = jnp.zeros_like(vec_ref)
            carry_ref[...] = jnp.zeros_like(carry_ref)
            dcw_ref[...] = jnp.zeros_like(dcw_ref)

        wv = cw_ref[...]
        xin, cg, dconv = xin_ref[...], cg_ref[...], dc_ref[...]
        cx = cg * xin
        cv, s1, s2 = _conv_taps(cx, jnp.where(i < nb - 1, cgh_ref[...] * xinh_ref[...], 0.0), wv)
        dp_ref[:, col_bg:col_bg + CW] = (dconv * cv).astype(BF16)
        dcv = dconv * bg_ref[...]
        dcw_ref[0:1, :] += jnp.sum(dcv * s2, axis=0, keepdims=True)
        dcw_ref[1:2, :] += jnp.sum(dcv * s1, axis=0, keepdims=True)
        dcw_ref[2:3, :] += jnp.sum(dcv * cx, axis=0, keepdims=True)
        dcx = _conv_taps_t(dcv, jnp.where(i > 0, nxt_ref[...], 0.0), wv)
        nxt_ref[...] = dcv[0:SUB, :]
        dp_ref[:, col_cg:col_cg + CW] = (dcx * xin).astype(BF16)
        dp_ref[:, col_xin:col_xin + CW] = (dcx * cg).astype(BF16)

        def one(d_ref, x_ref, g_ref, base, row, scale):
            dg = jnp.zeros((1, DH), F32)
            for h in range(HEADS):
                sl = slice(DH * h, DH * (h + 1))
                xv = x_ref[:, sl]
                r = lax.rsqrt(jnp.mean(xv * xv, axis=-1, keepdims=True) + EPS)
                xh = xv * r
                dn = d_ref[:, SLAB * h:SLAB * h + DH] * scale
                dg = dg + jnp.sum(dn * xh, axis=0, keepdims=True)
                dxh = dn * g_ref[...]
                dp_ref[:, base + DH * h:base + DH * (h + 1)] = (
                    r * (dxh - xh * jnp.mean(dxh * xh, axis=-1, keepdims=True))).astype(BF16)
            vec_ref[row:row + 1, 0:DH] += dg

        one(dq_ref, q_ref, gq_ref, 0, 0, QK_SCALE)
        one(dk_ref, k_ref, gk_ref, col_k, 1, 1.0)
        lane = lax.broadcasted_iota(jnp.int32, (TR, LANES), 1)
        df = jnp.zeros((TR, LANES), F32)
        for h in range(HEADS):
            dp_ref[:, col_v + DH * h:col_v + DH * (h + 1)] = dv_ref[:, SLAB * h:SLAB * h + DH].astype(BF16)
            row_sum = dq_ref[:, SLAB * h + DH:SLAB * h + DH + 1]
            col_sum = dk_ref[:, SLAB * h + DH + AUG_ONE:SLAB * h + DH + AUG_ONE + 1]
            df = jnp.where(lane == h, row_sum - col_sum, df)
        row = lax.broadcasted_iota(jnp.int32, (TR, TR), 0)
        col = lax.broadcasted_iota(jnp.int32, (TR, TR), 1)
        dlogf = _dot((col >= row).astype(F32), df, NN, lax.Precision.HIGHEST) + carry_ref[0:1, :]
        carry_ref[...] = jnp.broadcast_to(dlogf[0:1, :], carry_ref.shape)
        dfg = dlogf * _sigmoid(-(fg_ref[...] + b_ref[...]))
        dp_ref[:, col_fg:DINP] = dfg.astype(BF16)
        vec_ref[2:3, :] += jnp.sum(dfg, axis=0, keepdims=True)

    rev = lambda width, col=0: pl.BlockSpec((TR, width), lambda i, col=col: (nb - 1 - i, col))
    wide = rev(HEADS * SLAB)
    halo = lambda col: pl.BlockSpec((SUB, CW), lambda i, col=col: (jnp.maximum((nb - 1 - i) * (TR // SUB) - 1, 0), col))
    outs, _ = _call(
        body, (dqp, dkp, dvp, proj, proj, proj, bf_pad, gq, gk, dmixed, proj, proj, proj, proj, proj, cw_mix),
        name="qkv_post", grid=(nb,),
        in_specs=[wide, wide, wide, rev(AW, 0), rev(AW, 1), rev(LANES, FG_BLOCK), _full_spec((1, LANES)),
                  _full_spec((1, DH)), _full_spec((1, DH)),
                  rev(CW, 1), rev(CW, 3), rev(CW, 4), rev(CW, 5), halo(3), halo(5), _full_spec((3, CW))],
        out_specs=[rev(DINP), _full_spec((SUB, LANES)), _full_spec((SUB, CW))],
        out_shape=[jax.ShapeDtypeStruct((s, DINP), BF16), jax.ShapeDtypeStruct((SUB, LANES), F32),
                   jax.ShapeDtypeStruct((SUB, CW), F32)],
        scratch_shapes=[pltpu.VMEM((SUB, LANES), F32), pltpu.VMEM((SUB, CW), F32)], vmem_mb=32)
    return outs


TF = 256
NJ = DFF // TF
FFN_ROWS_FWD = 1024
FFN_ROWS_BWD = 1024


def _ffn_fwd(h2, wup_t, cw, wd):
    s = h2.shape[0]
    tr = FFN_ROWS_FWD
    nr = s // tr

    def body(h_ref, wu_ref, cg_ref, cv_ref, wd_ref, pg_ref, pv_ref, y_ref, halo_ref, act_ref):
        r, j = pl.program_id(0), pl.program_id(1)
        hv = h_ref[...]
        pg = _dot(hv, wu_ref[0], NT).astype(BF16)
        pv = _dot(hv, wu_ref[1], NT).astype(BF16)
        pg_ref[...] = pg
        pv_ref[...] = pv
        pgf, pvf = pg.astype(F32), pv.astype(F32)
        ug, _, _ = _conv_taps(pgf, jnp.where(r > 0, halo_ref[j, 0], 0.0), cg_ref[...])
        uv, _, _ = _conv_taps(pvf, jnp.where(r > 0, halo_ref[j, 1], 0.0), cv_ref[...])
        halo_ref[j, 0] = pgf[tr - SUB:tr, :]
        halo_ref[j, 1] = pvf[tr - SUB:tr, :]
        act = (ug * _sigmoid(ug) * uv).astype(BF16)
        for t in range(NJ):
            @pl.when(j == t)
            def _(t=t):
                act_ref[:, t * TF:(t + 1) * TF] = act

        @pl.when(j == NJ - 1)
        def _():
            y_ref[...] = _dot(act_ref[...], wd_ref[...])

    pre = jax.ShapeDtypeStruct((s, DFF), BF16)
    return pl.pallas_call(
        body, name="ffn_fwd", grid=(nr, NJ),
        in_specs=[pl.BlockSpec((tr, D), lambda r, j: (r, 0)),
                  pl.BlockSpec((2, TF, D), lambda r, j: (0, j, 0)),
                  pl.BlockSpec((3, TF), lambda r, j: (0, j)),
                  pl.BlockSpec((3, TF), lambda r, j: (0, NJ + j)),
                  pl.BlockSpec((DFF, D), lambda r, j: (0, 0))],
        out_specs=(pl.BlockSpec((tr, TF), lambda r, j: (r, j)),
                   pl.BlockSpec((tr, TF), lambda r, j: (r, j)),
                   pl.BlockSpec((tr, D), lambda r, j: (r, 0))),
        out_shape=(pre, pre, jax.ShapeDtypeStruct((s, D), F32)),
        scratch_shapes=[pltpu.VMEM((NJ, 2, SUB, TF), F32), pltpu.VMEM((tr, DFF), BF16)],
        compiler_params=_params(("arbitrary", "arbitrary"), 56),
    )(h2, wup_t, cw, cw, wd)


def _ffn_bwd(dy, h2, pre_g, pre_v, wup_t, cw, wd):
    s = h2.shape[0]
    tr = FFN_ROWS_BWD
    nr = s // tr
    hb = tr // (2 * SUB)

    def body(dy_ref, h_ref, pg_ref, pv_ref, hg_ref, hv_ref, wu_ref, cg_ref, cv_ref, wd_ref,
             dh_ref, dwu_ref, dwd_ref, dcg_ref, dcv_ref, nxt_ref, awu_ref, awd_ref):
        j, r = pl.program_id(0), pl.program_id(1)
        rr = nr - 1 - r
        row0 = pl.multiple_of(rr * tr, tr)
        cwg, cwv = cg_ref[...], cv_ref[...]
        pg, pv = pg_ref[...].astype(F32), pv_ref[...].astype(F32)
        ug, g1, g2 = _conv_taps(pg, jnp.where(rr > 0, hg_ref[SUB:2 * SUB, :].astype(F32), 0.0), cwg)
        uv, v1, v2 = _conv_taps(pv, jnp.where(rr > 0, hv_ref[SUB:2 * SUB, :].astype(F32), 0.0), cwv)
        sg = _sigmoid(ug)
        sil = ug * sg
        act = (sil * uv).astype(BF16)
        dyv = dy_ref[...]
        da = _dot(dyv, wd_ref[...], NT)
        dug = da * uv * (sg * (1.0 + ug * (1.0 - sg)))
        duv = da * sil
        dpg = _conv_taps_t(dug, jnp.where(r > 0, nxt_ref[0], 0.0), cwg)
        dpv = _conv_taps_t(duv, jnp.where(r > 0, nxt_ref[1], 0.0), cwv)
        nxt_ref[0] = dug[0:SUB, :]
        nxt_ref[1] = duv[0:SUB, :]
        dpgb, dpvb = dpg.astype(BF16), dpv.astype(BF16)
        hv = h_ref[...]
        dwd = _dot(act, dyv, TN)
        dpb = jnp.concatenate([dpgb, dpvb], axis=1)
        dwu = _dot(dpb, hv, TN)
        dh = _dot(dpb, wu_ref[...].reshape(2 * TF, D))

        def taps(du, x0, x1, x2):
            return (jnp.sum(du * x2, axis=0, keepdims=True), jnp.sum(du * x1, axis=0, keepdims=True),
                    jnp.sum(du * x0, axis=0, keepdims=True))

        tg, tv = taps(dug, pg, g1, g2), taps(duv, pv, v1, v2)

        @pl.when(r == 0)
        def _():
            awd_ref[...] = dwd
            awu_ref[...] = dwu
            dcg_ref[...] = jnp.zeros_like(dcg_ref)
            dcv_ref[...] = jnp.zeros_like(dcv_ref)

        @pl.when(r > 0)
        def _():
            awd_ref[...] += dwd
            awu_ref[...] += dwu

        @pl.when(r == nr - 1)
        def _():
            dwd_ref[...] = awd_ref[...].astype(BF16)
            dwu_ref[...] = awu_ref[...].astype(BF16).reshape(2, TF, D)

        for t in range(3):
            dcg_ref[t:t + 1, :] += tg[t]
            dcv_ref[t:t + 1, :] += tv[t]

        @pl.when(j == 0)
        def _():
            dh_ref[pl.ds(row0, tr), :] = dh

        @pl.when(j > 0)
        def _():
            dh_ref[pl.ds(row0, tr), :] += dh

    rows = lambda j, r: (nr - 1 - r, 0)
    tile = lambda j, r: (nr - 1 - r, j)
    halo = lambda j, r: (jnp.maximum((nr - 1 - r) * hb - 1, 0), j)
    return pl.pallas_call(
        body, name="ffn_bwd", grid=(NJ, nr),
        in_specs=[pl.BlockSpec((tr, D), rows), pl.BlockSpec((tr, D), rows),
                  pl.BlockSpec((tr, TF), tile), pl.BlockSpec((tr, TF), tile),
                  pl.BlockSpec((2 * SUB, TF), halo), pl.BlockSpec((2 * SUB, TF), halo),
                  pl.BlockSpec((2, TF, D), lambda j, r: (0, j, 0)),
                  pl.BlockSpec((3, TF), lambda j, r: (0, j)), pl.BlockSpec((3, TF), lambda j, r: (0, NJ + j)),
                  pl.BlockSpec((TF, D), lambda j, r: (j, 0))],
        out_specs=(pl.BlockSpec((s, D), lambda j, r: (0, 0)),
                   pl.BlockSpec((2, TF, D), lambda j, r: (0, j, 0)),
                   pl.BlockSpec((TF, D), lambda j, r: (j, 0)),
                   pl.BlockSpec((SUB, TF), lambda j, r: (0, j)), pl.BlockSpec((SUB, TF), lambda j, r: (0, j))),
        out_shape=(jax.ShapeDtypeStruct((s, D), F32),
                   jax.ShapeDtypeStruct((2, DFF, D), BF16), jax.ShapeDtypeStruct((DFF, D), BF16),
                   jax.ShapeDtypeStruct((SUB, DFF), F32), jax.ShapeDtypeStruct((SUB, DFF), F32)),
        scratch_shapes=[pltpu.VMEM((2, SUB, TF), F32), pltpu.VMEM((2 * TF, D), F32), pltpu.VMEM((TF, D), F32)],
        compiler_params=_params(("arbitrary", "arbitrary"), 56),
    )(dy, h2, pre_g, pre_v, pre_g, pre_v, wup_t, cw, cw, wd)


def _adam(w, g, m, v):
    m = ADAM_B1 * m + (1.0 - ADAM_B1) * g
    v = ADAM_B2 * v + (1.0 - ADAM_B2) * (g * g)
    m_hat = m / (1.0 - ADAM_B1 ** ADAM_STEP)
    v_hat = v / (1.0 - ADAM_B2 ** ADAM_STEP)
    delta = -ADAM_LR * (m_hat / (jnp.sqrt(v_hat) + ADAM_EPS) + ADAM_WD * w)
    return delta, m, v


NCHIP = NDEV // 2


def _pair_add(mine, theirs, tr, name):
    _, _, rws, cols = mine.shape

    def body(a_ref, b_ref, o_ref):
        c = lax.axis_index("c")
        o_ref[0] = (a_ref[0, c].astype(F32) + b_ref[0].astype(F32)).astype(BF16)

    (out,), _ = _call(
        body, (mine, theirs), name=name, grid=(NCHIP, rws // tr),
        in_specs=[pl.BlockSpec((1, 2, tr, cols), lambda q, i: (q, 0, i, 0)),
                  pl.BlockSpec((1, tr, cols), lambda q, i: (q, i, 0))],
        out_specs=[pl.BlockSpec((1, tr, cols), lambda q, i: (q, i, 0))],
        out_shape=[jax.ShapeDtypeStruct((NCHIP, rws, cols), BF16)], vmem_mb=16)
    return out


def _adamw_sharded(parts, w, m, v, tr, name, hosted=None):
    rws, cols = w.shape
    n_parts = parts.shape[0]

    def body(p_ref, w_ref, m_ref, v_ref, g_ref, d_ref, mo_ref, vo_ref):
        g = p_ref[0].astype(F32)
        for q in range(1, n_parts):
            g = g + p_ref[q].astype(F32)
        g_ref[...] = g
        d_ref[...], mo_ref[...], vo_ref[...] = _adam(w_ref[...], g, m_ref[...], v_ref[...])

    blk = pl.BlockSpec((tr, cols), lambda i: (i, 0))
    o = jax.ShapeDtypeStruct((rws, cols), F32)
    outs, moved = _call(
        body, (parts, w, m, v), name=name, grid=(rws // tr,),
        in_specs=[pl.BlockSpec((n_parts, tr, cols), lambda i: (0, i, 0)), blk, blk, blk],
        out_specs=[blk, blk, blk, blk], out_shape=[o, o, o, o], vmem_mb=44 if tr > 256 else 24, hosted=hosted)
    return (outs, moved) if hosted else outs


def _adamw_ada(c_all, dmod_my, w, m, v):
    rws, cols = w.shape
    tr = 256

    def body(c_ref, dm_ref, w_ref, m_ref, v_ref, g_ref, d_ref, mo_ref, vo_ref):
        cv = c_ref[...]
        act = cv * _sigmoid(cv)
        g = _dot(act, dm_ref[...], TN, lax.Precision.HIGHEST)
        g_ref[...] = g
        d_ref[...], mo_ref[...], vo_ref[...] = _adam(w_ref[...], g, m_ref[...], v_ref[...])

    blk = pl.BlockSpec((tr, cols), lambda i: (i, 0))
    o = jax.ShapeDtypeStruct((rws, cols), F32)
    return pl.pallas_call(
        body, name="adamw_ada", grid=(rws // tr,),
        in_specs=[pl.BlockSpec((NDEV, tr), lambda i: (0, i)), _full_spec((NDEV, cols)), blk, blk, blk],
        out_specs=(blk, blk, blk, blk), out_shape=(o, o, o, o),
        compiler_params=_params(("parallel",), 32),
    )(c_all, dmod_my, w, m, v)


REP_ROWS = 16
ROW_N1, ROW_N2, ROW_LOSS, ROW_MISC = 6, 7, 8, 9
LANE_BF, LANE_GQ, LANE_GK = 0, 128, 256


def _adamw_small(rep_all, conv_all, wmv):
    n_ff = wmv[6][0].shape[1]

    def body(*refs):
        rep_ref, conv_ref = refs[:2]
        ins = refs[2:2 + 24]
        outs = refs[2 + 24:]
        loss_ref, outs = outs[0], outs[1:]
        g_rep = rep_ref[0]
        g_conv = conv_ref[0]
        for d in range(1, NDEV):
            g_rep = g_rep + rep_ref[d]
            g_conv = g_conv + conv_ref[d]
        loss_ref[...] = (0.5 / D) * jnp.sum(g_rep[ROW_LOSS:ROW_LOSS + 1, :], axis=-1, keepdims=True)
        grads = [
            None,
            g_rep[ROW_N1:ROW_N1 + 1, :],
            g_rep[ROW_MISC:ROW_MISC + 1, LANE_BF:LANE_BF + HEADS],
            g_rep[ROW_MISC:ROW_MISC + 1, LANE_GQ:LANE_GQ + DH],
            g_rep[ROW_MISC:ROW_MISC + 1, LANE_GK:LANE_GK + DH],
            g_rep[ROW_N2:ROW_N2 + 1, :],
            g_conv[0:3, 0:n_ff],
            g_conv[0:3, n_ff:n_ff + DH],
        ]
        for p in range(8):
            w_ref, m_ref, v_ref = ins[3 * p:3 * p + 3]
            g_ref, d_ref, mo_ref, vo_ref = outs[4 * p:4 * p + 4]
            if p == 0:
                for nmod in range(NMOD):
                    sl = slice(D * nmod, D * (nmod + 1))
                    g = g_rep[nmod:nmod + 1, :]
                    g_ref[:, sl] = g
                    d_ref[:, sl], mo_ref[:, sl], vo_ref[:, sl] = _adam(w_ref[:, sl], g, m_ref[:, sl], v_ref[:, sl])
            else:
                g = grads[p]
                g_ref[...] = g
                d_ref[...], mo_ref[...], vo_ref[...] = _adam(w_ref[...], g, m_ref[...], v_ref[...])

    flat = [a for trio in wmv for a in trio]
    out_shape = [jax.ShapeDtypeStruct((1, 1), F32)]
    for trio in wmv:
        out_shape += [jax.ShapeDtypeStruct(trio[0].shape, F32)] * 4
    ins = [rep_all, conv_all] + flat
    return pl.pallas_call(
        body, name="adamw_small", grid=(1,),
        in_specs=[_full_spec(a.shape) for a in ins], out_specs=tuple(_full_spec(o.shape) for o in out_shape),
        out_shape=tuple(out_shape), compiler_params=_params(("arbitrary",), 32),
    )(*ins)


FG_FIRST = 3 * AW
N_IN = DIN // NDEV


def _w_in_runs():
    runs = []
    for d in range(NDEV):
        lo, hi = N_IN * d, N_IN * (d + 1)
        for a, b, shift in ((0, FG_FIRST, 0), (FG_FIRST, FG_FIRST + HEADS, DIN - HEADS - FG_FIRST),
                            (FG_FIRST + HEADS, DIN, -HEADS)):
            a, b = max(a, lo), min(b, hi)
            if a < b:
                runs.append((d, a - lo, a + shift, b - a))
    return runs


W_IN_ROWS = 256
N_IN_PAD = 512


def _identity(n):
    return (lax.broadcasted_iota(jnp.int32, (n, n), 0) == lax.broadcasted_iota(jnp.int32, (n, n), 1)).astype(BF16)


def _assemble_w_in(g_in, hosted):
    def body(g_ref, o_ref, t_ref):
        eye = _identity(W_IN_ROWS)
        shard = None
        for d, src, dst, width in _w_in_runs():
            if d != shard:
                t_ref[:, 0:N_IN] = _dot(eye, g_ref[d], NT).astype(BF16)
                shard = d
            o_ref[:, dst:dst + width] = t_ref[:, src:src + width]
        o_ref[:, DIN:DINP] = jnp.zeros((W_IN_ROWS, DINP - DIN), o_ref.dtype)

    (out,), moved = _call(
        body, (g_in,), name="assemble_w_in", grid=(D // W_IN_ROWS,),
        in_specs=[pl.BlockSpec((NDEV, N_IN, W_IN_ROWS), lambda i: (0, 0, i))],
        out_specs=[pl.BlockSpec((W_IN_ROWS, DINP), lambda i: (i, 0))],
        out_shape=[jax.ShapeDtypeStruct((D, DINP), g_in.dtype)],
        scratch_shapes=[pltpu.VMEM((W_IN_ROWS, N_IN_PAD), BF16)], vmem_mb=16, hosted=hosted)
    return out, moved


def _scatter_dw_in(dwp):
    def body(w_ref, o_ref, t_ref):
        eye = _identity(W_IN_ROWS)
        runs = _w_in_runs()
        for i, (d, src, dst, width) in enumerate(runs):
            t_ref[:, src:src + width] = w_ref[:, dst:dst + width]
            if i + 1 == len(runs) or runs[i + 1][0] != d:
                o_ref[d // 2, d % 2] = _dot(t_ref[:, 0:N_IN], eye, TN).astype(BF16)

    (out,), _ = _call(
        body, (dwp,), name="scatter_dw_in", grid=(D // W_IN_ROWS,),
        in_specs=[pl.BlockSpec((W_IN_ROWS, DINP), lambda i: (i, 0))],
        out_specs=[pl.BlockSpec((NCHIP, 2, N_IN, W_IN_ROWS), lambda i: (0, 0, 0, i))],
        out_shape=[jax.ShapeDtypeStruct((NCHIP, 2, N_IN, D), dwp.dtype)],
        scratch_shapes=[pltpu.VMEM((W_IN_ROWS, N_IN_PAD), BF16)], vmem_mb=16)
    return out


def kernel(x, c, w_ada, b_ada, norm1_g, w_in, b_forget, q_norm_g, k_norm_g, conv_mix_w, w_out, norm2_g, w_up, ffn_conv_w, w_down, loss_target, m_w_ada, m_b_ada, m_norm1_g, m_w_in, m_b_forget, m_q_norm_g, m_k_norm_g, m_conv_mix_w, m_w_out, m_norm2_g, m_w_up, m_ffn_conv_w, m_w_down, v_w_ada, v_b_ada, v_norm1_g, v_w_in, v_b_forget, v_q_norm_g, v_k_norm_g, v_conv_mix_w, v_w_out, v_norm2_g, v_w_up, v_ffn_conv_w, v_w_down):
    me = 4 * lax.axis_index("x") + 2 * lax.axis_index("y") + lax.axis_index("c")
    xs, tgt = x[0], loss_target[0]
    n_ada = w_ada.shape[2]
    n_ff = w_up.shape[2]

    conv_w = jnp.concatenate([ffn_conv_w[0], conv_mix_w[0]], axis=1)
    conv_w = jnp.concatenate([conv_w, jnp.zeros((SUB - 3, conv_w.shape[1]), F32)], axis=0)
    c_all, conv_all, g_in = _exchange(
        [(c.reshape(SUB, D // SUB), "ag"), (conv_w, "ag"), (jnp.transpose(w_in[0]).astype(BF16), "ag2")],
        "exchange_w_in")
    g_in, w_out_b, w_up_b, w_down_b = lax.optimization_barrier(
        (g_in, w_out[0].astype(BF16), jnp.transpose(w_up[0]).astype(BF16), w_down[0].astype(BF16)))
    g_out, g_up, g_down = _sequencer_exchange(
        [(w_out_b, "ag2"), (w_up_b, "ag2"), (w_down_b, "ag2")], "gather_weights", collective_id=1)
    c_all = c_all.reshape(NDEV, D)
    cw_ffn = jnp.transpose(conv_all[:, :3, :n_ff], (1, 0, 2)).reshape(3, 2 * DFF)
    cw_mix = jnp.transpose(conv_all[:, :3, n_ff:], (1, 0, 2)).reshape(3, CW)

    b_my = lax.dynamic_slice(b_ada, (0, me * n_ada), (1, n_ada))
    mod_part = _ada_fwd(c_all, w_ada[0], b_my)
    w_in_p, (mod_rows,) = _assemble_w_in(
        g_in, [(jnp.broadcast_to(mod_part[:, None, :], (NDEV, SUB, n_ada)), "a2a")])
    mod = mod_rows[:, 0, :].reshape(NMOD, D)
    mod = jnp.concatenate([mod, jnp.zeros((SUB - NMOD, D), F32)], axis=0)

    h = _norm_mod_fwd(xs, mod, norm1_g)
    proj = _mm(h, w_in_p, "nn", F32, 1024, 640, "proj_fwd")
    bf_pad = jnp.concatenate([b_forget, jnp.zeros((1, LANES - HEADS), F32)], axis=1)
    qp, kp, vp, conv = _qkv_prep(proj, bf_pad, q_norm_g, k_norm_g, cw_mix)
    attn, lse = _attn_fwd(qp, kp, vp)
    w_out_f = g_out.reshape(D, D)
    w_up_t = g_up.reshape(2, DFF, D)
    w_down_f = g_down.reshape(DFF, D)
    mixed = jnp.concatenate([attn, conv], axis=1).astype(BF16)
    z = _mm(mixed, w_out_f, "nn", F32, 1024, 1024, "out_fwd")
    x1, h2 = _resid_norm2(xs, z, mod, norm2_g)
    pre_g, pre_v, y = _ffn_fwd(h2, w_up_t, cw_ffn, w_down_f)
    dout, dy, vec_l = _loss_head(x1, y, tgt, mod)

    dh2, dwup_t, dwd, dcw_g, dcw_v = _ffn_bwd(dy, h2, pre_g, pre_v, w_up_t, cw_ffn, w_down_f)
    s_down = dwd.reshape(NCHIP, 2, DFF // NDEV, D)
    s_up = dwup_t.reshape(NCHIP, 2, n_ff, D)
    dx1, dz, vec_2, (t_up, t_down) = _norm_mod_bwd(dh2, x1, dout, z, mod, norm2_g, 4, 2, "norm2_bwd",
                                                   hosted=[(s_up, "pair"), (s_down, "pair")])
    dwout = _mm(mixed, dz, "tn", BF16, 1024, 1024, "out_bwd_w")
    s_out = dwout.reshape(NCHIP, 2, D // NDEV, D)
    dmixed, (t_out,) = _mm(dz, w_out_f, "nt", F32, 1024, 1024, "out_bwd_x", hosted=[(s_out, "pair")])
    c_out = _pair_add(s_out, t_out, 128, "pair_add_out")
    c_up = _pair_add(s_up, t_up, 176, "pair_add_up")
    c_down = _pair_add(s_down, t_down, 176, "pair_add_down")
    (dqp, dkp, dvp), (p_up, p_down, p_out) = _attn_bwd(
        qp, kp, vp, dmixed, attn, lse, [(c_up, "chips"), (c_down, "chips"), (c_out, "chips")])
    dproj, vec_qk, dcw_mix = _qkv_post(dqp, dkp, dvp, dmixed, proj, bf_pad, q_norm_g, k_norm_g, cw_mix)
    dwin_p = _mm(h, dproj, "tn", BF16, 1024, 640, "proj_bwd_w")
    s_in = _scatter_dw_in(dwin_p).reshape(NDEV, N_IN, D)
    (p_in,) = _sequencer_exchange([(s_in, "a2a")], "scatter_dw_in_partials", collective_id=2, all_peers=True)
    dh = _mm(dproj, w_in_p, "nt", F32, 1024, 512, "proj_bwd_x", vmem_mb=36)
    grad_x, vec_1 = _norm_mod_bwd(dh, xs, dx1, None, mod, norm1_g, 1, None, "norm1_bwd")

    gap = lambda n: jnp.zeros((1, n), F32)
    misc = jnp.concatenate([
        vec_qk[2:3, :HEADS], gap(LANE_GQ - LANE_BF - HEADS), vec_qk[0:1, :DH], gap(LANE_GK - LANE_GQ - DH),
        vec_qk[1:2, :DH], gap(D - LANE_GK - DH)], axis=1)
    rep = jnp.concatenate([
        vec_1[0:1], vec_1[1:2], vec_2[3:4], vec_2[0:1], vec_2[1:2], vec_l[0:1],
        vec_1[2:3], vec_2[2:3], vec_l[1:2], misc, jnp.zeros((REP_ROWS - 10, D), F32)], axis=0)
    dcw_ffn = jnp.concatenate([dcw_g, dcw_v], axis=1).reshape(SUB, NDEV, n_ff)
    dcw_all = jnp.concatenate([jnp.transpose(dcw_ffn, (1, 0, 2)),
                               jnp.transpose(dcw_mix.reshape(SUB, NDEV, DH), (1, 0, 2))], axis=2)
    r_up = _adamw_sharded(p_up, jnp.transpose(w_up[0]), jnp.transpose(m_w_up[0]), jnp.transpose(v_w_up[0]), 176,
                          "adamw_up")
    r_down = _adamw_sharded(p_down, w_down[0], m_w_down[0], v_w_down[0], 176, "adamw_down")
    rep, dcw_all, r_up, r_down = lax.optimization_barrier((rep, dcw_all, r_up, r_down))
    r_up = tuple(jnp.transpose(a) for a in r_up)
    r_out, (rep_all, conv_parts) = _adamw_sharded(p_out, w_out[0], m_w_out[0], v_w_out[0], 128, "adamw_out",
                                                  hosted=[(rep, "ag"), (dcw_all, "a2a")])
    dmod_my = lax.dynamic_slice(rep_all[:, :NMOD, :].reshape(NDEV, NMOD * D), (0, me * n_ada), (NDEV, n_ada))
    r_ada = _adamw_ada(c_all, dmod_my, w_ada[0], m_w_ada[0], v_w_ada[0])
    r_in = _adamw_sharded(p_in, jnp.transpose(w_in[0]), jnp.transpose(m_w_in[0]), jnp.transpose(v_w_in[0]), N_IN,
                          "adamw_in")
    r_in = tuple(jnp.transpose(a) for a in r_in)
    small = _adamw_small(rep_all, conv_parts, [
        [b_ada, m_b_ada, v_b_ada], [norm1_g, m_norm1_g, v_norm1_g], [b_forget, m_b_forget, v_b_forget],
        [q_norm_g, m_q_norm_g, v_q_norm_g], [k_norm_g, m_k_norm_g, v_k_norm_g], [norm2_g, m_norm2_g, v_norm2_g],
        [ffn_conv_w[0], m_ffn_conv_w[0], v_ffn_conv_w[0]], [conv_mix_w[0], m_conv_mix_w[0], v_conv_mix_w[0]]])
    loss = small[0].reshape(())
    r_bada, r_n1, r_bf, r_gq, r_gk, r_n2, r_cf, r_cm = [small[1 + 4 * p:5 + 4 * p] for p in range(8)]
    lead = lambda t: tuple(a[None] for a in t)
    per_w = [lead(r_ada), r_bada, r_n1, lead(r_in), r_bf, r_gq, r_gk, lead(r_cm), lead(r_out), r_n2,
             lead(r_up), lead(r_cf), lead(r_down)]
    outs = [loss, grad_x[None]]
    for field in range(4):
        outs += [t[field] for t in per_w]
    return tuple(outs)
```

```python
import functools

import jax
import jax.numpy as jnp
import numpy as np
from jax import lax
from jax.experimental import pallas as pl
from jax.experimental.pallas import tpu as pltpu
from jax.experimental.pallas import tpu_sc as plsc

F32 = jnp.float32
BF16 = jnp.bfloat16

NDEV = 8
D = 1024
HEADS = 8
DH = 64
AW = 512
CW = 512
DFF = 2816
DIN = 3080
DINP = 3200
NMOD = 6
EPS = 1e-6
QK_SCALE = 0.125
LANES = 128
SUB = 8

ADAM_LR = 0.001
ADAM_B1 = 0.9
ADAM_B2 = 0.999
ADAM_EPS = 1e-08
ADAM_WD = 0.01
ADAM_STEP = 10

MESH = pl.DeviceIdType.MESH
ANY = pl.BlockSpec(memory_space=pl.ANY)

NN = (((1,), (0,)), ((), ()))
NT = (((1,), (1,)), ((), ()))
TN = (((0,), (0,)), ((), ()))


def _dot(a, b, dims=NN, precision=None):
    return lax.dot_general(a, b, dims, precision=precision, preferred_element_type=F32)


def _params(sem=None, vmem_mb=None):
    kw = {}
    if sem is not None:
        kw["dimension_semantics"] = sem
    if vmem_mb is not None:
        kw["vmem_limit_bytes"] = vmem_mb * 1024 * 1024
    return pltpu.CompilerParams(**kw)


def _sigmoid(x):
    return 0.5 * jnp.tanh(0.5 * x) + 0.5


class _Exchange:
    def __init__(self, items):
        self.arrays = [pltpu.with_memory_space_constraint(a, pltpu.HBM) for a, _ in items]
        self.modes = [m for _, m in items]
        self.n = len(items)
        self.out_shape = []
        for a, m in items:
            sh = {"ag": (NDEV,) + a.shape, "ag2": (NDEV,) + a.shape, "pair": a.shape[:1] + a.shape[2:]}.get(m, a.shape)
            self.out_shape.append(jax.ShapeDtypeStruct(sh, a.dtype))
        self.scratch = [pltpu.SemaphoreType.DMA((self.n, NDEV - 1)), pltpu.SemaphoreType.DMA((self.n, NDEV - 1)),
                        pltpu.SemaphoreType.DMA((self.n,))]

    def _plan(self, srcs, outs, sems):
        send_sems, recv_sems, loc_sems = sems
        x, y, c = lax.axis_index("x"), lax.axis_index("y"), lax.axis_index("c")
        me, my_chip = 4 * x + 2 * y + c, 2 * x + y
        sib = (x, y, 1 - c)
        local, first, landed, forwards, arrivals = [], [], [], [], []

        def remote(a, k, src, dst, to):
            return pltpu.make_async_remote_copy(src_ref=src, dst_ref=dst, send_sem=send_sems.at[a, k],
                                                recv_sem=recv_sems.at[a, k], device_id=to, device_id_type=MESH)

        for a, mode in enumerate(self.modes):
            src, out = srcs[a], outs[a]
            if mode in ("ag", "a2a"):
                piece = (lambda slot, src=src: src) if mode == "ag" else (lambda slot, src=src: src.at[slot])
                local.append(pltpu.make_async_copy(piece(me), out.at[me], loc_sems.at[a]))
                for r in range(1, NDEV):
                    px = 1 - x if (r >> 2) & 1 else x
                    py = 1 - y if (r >> 1) & 1 else y
                    pc = 1 - c if r & 1 else c
                    pidx = 4 * px + 2 * py + pc
                    first.append(remote(a, r - 1, piece(pidx), out.at[me], (px, py, pc)))
                    arrivals.append(remote(a, r - 1, piece(pidx), out.at[pidx], (px, py, pc)))
            elif mode == "ag2":
                local.append(pltpu.make_async_copy(src, out.at[me], loc_sems.at[a]))
                first.append(remote(a, 0, src, out.at[me], sib))
                arrivals.append(remote(a, 0, src, out.at[me + 1 - 2 * c], sib))
                for j, (px, py) in enumerate([(1 - x, y), (x, 1 - y), (1 - x, 1 - y)]):
                    theirs = out.at[4 * px + 2 * py + c]
                    first.append(remote(a, 1 + j, src, out.at[me], (px, py, c)))
                    landed.append(remote(a, 1 + j, src, theirs, (px, py, c)))
                    forwards.append(remote(a, 4 + j, theirs, theirs, sib))
                    arrivals.append(remote(a, 4 + j, src, out.at[4 * px + 2 * py + 1 - c], sib))
            elif mode == "pair":
                for q in range(NDEV // 2):
                    first.append(remote(a, q, src.at[q, 1 - c], out.at[q], sib))
                    arrivals.append(remote(a, q, src.at[q, 1 - c], out.at[q], sib))
            else:
                assert mode == "chips", mode
                local.append(pltpu.make_async_copy(src.at[my_chip], out.at[my_chip], loc_sems.at[a]))
                for j, (px, py) in enumerate([(1 - x, y), (x, 1 - y), (1 - x, 1 - y)]):
                    q = 2 * px + py
                    first.append(remote(a, 1 + j, src.at[q], out.at[my_chip], (px, py, c)))
                    arrivals.append(remote(a, 1 + j, src.at[q], out.at[q], (px, py, c)))
        return local, first, landed, forwards, arrivals

    def start(self, srcs, outs, sems):
        local, first, _, _, _ = self._plan(srcs, outs, sems)
        for cp in local + first:
            cp.start()

    def wait(self, srcs, outs, sems):
        local, first, landed, forwards, arrivals = self._plan(srcs, outs, sems)
        for cp, fwd in zip(landed, forwards):
            cp.wait_recv()
            fwd.start()
        for cp in arrivals:
            cp.wait_recv()
        for cp in first + forwards:
            cp.wait_send()
        for cp in local:
            cp.wait()


def _exchange(items, name):
    ex = _Exchange(items)
    n = ex.n

    def body(*refs):
        srcs, outs, sems = refs[:n], refs[n:2 * n], refs[2 * n:]
        ex.start(srcs, outs, sems)
        ex.wait(srcs, outs, sems)

    return pl.pallas_call(
        body, name=name,
        out_shape=tuple(ex.out_shape),
        in_specs=[ANY] * n, out_specs=tuple([ANY] * n),
        scratch_shapes=ex.scratch,
        compiler_params=pltpu.CompilerParams(has_side_effects=True),
    )(*ex.arrays)


def _sequencer_exchange(items, name, collective_id, all_peers=False):
    ex = _Exchange(items)
    srcs = [jax.new_ref(a, memory_space=pltpu.MemorySpace.HBM) for a in ex.arrays]
    outs = [jax.empty_ref(sh, memory_space=pltpu.MemorySpace.HBM) for sh in ex.out_shape]

    @pl.kernel(mesh=plsc.ScalarSubcoreMesh(axis_name="sequencer", num_cores=1), name=name,
               scratch_types=tuple(ex.scratch), compiler_params=pltpu.CompilerParams(collective_id=collective_id))
    def launch(send_sems, recv_sems, loc_sems):
        x, y, c = lax.axis_index("x"), lax.axis_index("y"), lax.axis_index("c")
        barrier = pltpu.get_barrier_semaphore()
        peers = [(x, y, 1 - c), (1 - x, y, c), (x, 1 - y, c), (1 - x, 1 - y, c)]
        if all_peers:
            peers += [(1 - x, y, 1 - c), (x, 1 - y, 1 - c), (1 - x, 1 - y, 1 - c)]
        for peer in peers:
            pl.semaphore_signal(barrier, inc=1, device_id=peer, device_id_type=MESH)
        pl.semaphore_wait(barrier, len(peers))
        sems = (send_sems, recv_sems, loc_sems)
        ex.start(srcs, outs, sems)
        ex.wait(srcs, outs, sems)

    launch()
    return [o[...] for o in outs]


def _call(body, inputs, *, name, grid, in_specs, out_specs, out_shape, scratch_shapes=(), vmem_mb=None, hosted=None):
    out_specs, out_shape, scratch_shapes = tuple(out_specs), tuple(out_shape), list(scratch_shapes)
    if not hosted:
        res = pl.pallas_call(
            body, name=name, grid=grid, in_specs=list(in_specs), out_specs=out_specs, out_shape=out_shape,
            scratch_shapes=scratch_shapes, compiler_params=_params(("arbitrary",) * len(grid), vmem_mb),
        )(*inputs)
        return tuple(res), ()
    ex = _Exchange(hosted)
    n, n_in, n_out, n_scr = ex.n, len(inputs), len(out_shape), len(scratch_shapes)

    def hosting_body(*refs):
        ins, srcs = refs[:n_in], refs[n_in:n_in + n]
        outs, landing = refs[n_in + n:n_in + n + n_out], refs[n_in + n + n_out:n_in + 2 * n + n_out]
        scratch, sems = refs[n_in + 2 * n + n_out:n_in + 2 * n + n_out + n_scr], refs[n_in + 2 * n + n_out + n_scr:]
        first = functools.reduce(jnp.logical_and, [pl.program_id(d) == 0 for d in range(len(grid))])
        last = functools.reduce(jnp.logical_and, [pl.program_id(d) == grid[d] - 1 for d in range(len(grid))])

        @pl.when(first)
        def _():
            ex.start(srcs, landing, sems)

        body(*ins, *outs, *scratch)

        @pl.when(last)
        def _():
            ex.wait(srcs, landing, sems)

    res = pl.pallas_call(
        hosting_body, name=name, grid=grid,
        in_specs=list(in_specs) + [ANY] * n, out_specs=out_specs + tuple([ANY] * n),
        out_shape=out_shape + tuple(ex.out_shape), scratch_shapes=scratch_shapes + ex.scratch,
        compiler_params=_params(("arbitrary",) * len(grid), vmem_mb),
    )(*inputs, *ex.arrays)
    return tuple(res[:n_out]), tuple(res[n_out:])


def _mm(a, b, mode, out_dtype, tm, tn, name, hosted=None, vmem_mb=24):
    if mode == "nn":
        (m, k), n = a.shape, b.shape[1]
        a_spec = pl.BlockSpec((tm, k), lambda i, j: (i, 0))
        b_spec = pl.BlockSpec((k, tn), lambda i, j: (0, j))
        dims = NN
    elif mode == "nt":
        (m, k), n = a.shape, b.shape[0]
        a_spec = pl.BlockSpec((tm, k), lambda i, j: (i, 0))
        b_spec = pl.BlockSpec((tn, k), lambda i, j: (j, 0))
        dims = NT
    else:
        (k, m), n = a.shape, b.shape[1]
        a_spec = pl.BlockSpec((k, tm), lambda i, j: (0, i))
        b_spec = pl.BlockSpec((k, tn), lambda i, j: (0, j))
        dims = TN
    assert m % tm == 0 and n % tn == 0, (m, n, tm, tn)

    def body(a_ref, b_ref, o_ref):
        o_ref[...] = _dot(a_ref[...], b_ref[...], dims).astype(o_ref.dtype)

    (out,), moved = _call(
        body, (a, b), name=name, grid=(m // tm, n // tn),
        in_specs=[a_spec, b_spec], out_specs=[pl.BlockSpec((tm, tn), lambda i, j: (i, j))],
        out_shape=[jax.ShapeDtypeStruct((m, n), out_dtype)], vmem_mb=vmem_mb, hosted=hosted)
    return (out, moved) if hosted else out


def _shift_down(x, k, fill):
    y = pltpu.roll(x, k, 0)
    row = lax.broadcasted_iota(jnp.int32, (SUB, x.shape[1]), 0)
    head = y[0:SUB, :]
    for t in range(k):
        head = jnp.where(row == t, fill[t], head)
    return jnp.concatenate([head, y[SUB:, :]], axis=0)


def _shift_up(x, k, fill):
    n = x.shape[0]
    y = pltpu.roll(x, n - k, 0)
    row = lax.broadcasted_iota(jnp.int32, (SUB, x.shape[1]), 0)
    tail = y[n - SUB:, :]
    for t in range(k):
        tail = jnp.where(row == SUB - k + t, fill[t], tail)
    return jnp.concatenate([y[:n - SUB, :], tail], axis=0)


def _conv_taps(x, halo, w):
    if halo is None:
        f1, f2 = [0.0], [0.0, 0.0]
    else:
        f1, f2 = [halo[7:8, :]], [halo[6:7, :], halo[7:8, :]]
    s1 = _shift_down(x, 1, f1)
    s2 = _shift_down(x, 2, f2)
    u = w[2:3, :] * x + w[1:2, :] * s1 + w[0:1, :] * s2
    return u, s1, s2


def _conv_taps_t(du, nxt, w):
    if nxt is None:
        f1, f2 = [0.0], [0.0, 0.0]
    else:
        f1, f2 = [nxt[0:1, :]], [nxt[0:1, :], nxt[1:2, :]]
    return w[2:3, :] * du + w[1:2, :] * _shift_up(du, 1, f1) + w[0:1, :] * _shift_up(du, 2, f2)


def _ada_fwd(c_all, w_ada, b_my):
    def body(c_ref, w_ref, b_ref, o_ref):
        cv = c_ref[...]
        act = cv * _sigmoid(cv)
        o_ref[...] = _dot(act, w_ref[...], NN, lax.Precision.HIGHEST) + b_ref[...]

    out = jax.ShapeDtypeStruct((NDEV, w_ada.shape[1]), F32)
    return pl.pallas_call(
        body, name="ada_fwd", grid=(1,),
        in_specs=[_full_spec(c_all.shape), _full_spec(w_ada.shape), _full_spec(b_my.shape)],
        out_specs=_full_spec(out.shape), out_shape=out, compiler_params=_params(("arbitrary",), 32),
    )(c_all, w_ada, b_my)


TR = 256
TRE = 512


def _row_spec(width, col=0, rows=TR):
    return pl.BlockSpec((rows, width), lambda i, col=col: (i, col))


def _erow(width):
    return _row_spec(width, rows=TRE)


def _full_spec(shape):
    return pl.BlockSpec(shape, lambda i: (0,) * len(shape))


def _norm_mod_fwd(x, mod, g):
    s = x.shape[0]

    def body(x_ref, mod_ref, g_ref, h_ref):
        xv = x_ref[...]
        r = lax.rsqrt(jnp.mean(xv * xv, axis=-1, keepdims=True) + EPS)
        nrm = xv * r * g_ref[...]
        h_ref[...] = (nrm * (1.0 + mod_ref[1:2, :]) + mod_ref[0:1, :]).astype(BF16)

    return pl.pallas_call(
        body, name="norm1_fwd", grid=(s // TRE,),
        in_specs=[_erow(D), _full_spec((SUB, D)), _full_spec((1, D))],
        out_specs=_erow(D), out_shape=jax.ShapeDtypeStruct((s, D), BF16),
        compiler_params=_params(("parallel",), 16),
    )(x, mod, g)


SLAB = 2 * DH
AUG_F, AUG_ONE, AUG_LSE = 0, 3, 6


def _split3(x):
    hi = x.astype(BF16).astype(F32)
    r1 = x - hi
    mid = r1.astype(BF16).astype(F32)
    return hi, mid, r1 - mid


def _lanes3(lane, first, pieces, other):
    out = other
    for k in range(3):
        out = jnp.where(lane == first + k, pieces[k], out)
    return out


def _aug_placement():
    eq = np.zeros((3 * LANES, HEADS * SLAB), np.float32)
    ek = np.zeros((3 * LANES, HEADS * SLAB), np.float32)
    ones = np.zeros((SUB, HEADS * SLAB), np.float32)
    for h in range(HEADS):
        aug = SLAB * h + DH
        for k in range(3):
            eq[LANES * k + h, aug + AUG_F + k] = 1.0
            ek[LANES * k + h, aug + AUG_ONE + k] = -1.0
            ones[0, aug + AUG_ONE + k] = 1.0
            ones[1, aug + AUG_F + k] = ones[1, aug + AUG_LSE + k] = 1.0
            ones[2, aug + k] = 1.0
    return jnp.asarray(eq, BF16), jnp.asarray(ek, BF16), jnp.asarray(ones)


FG_BLOCK = (3 * AW + 3 * CW) // LANES


def _qkv_prep(proj, bf_pad, gq, gk, cw_mix):
    s = proj.shape[0]

    def body(q_ref, k_ref, v_ref, fg_ref, b_ref, gq_ref, gk_ref, eq_ref, ek_ref, ones_ref, xin_ref, bg_ref, cg_ref,
             cw_ref, qo_ref, ko_ref, vo_ref, conv_ref, carry_ref, halo_ref):
        first = pl.program_id(0) == 0

        @pl.when(first)
        def _():
            carry_ref[...] = jnp.zeros_like(carry_ref)

        cx = cg_ref[...] * xin_ref[...]
        cv, _, _ = _conv_taps(cx, jnp.where(first, 0.0, halo_ref[...]), cw_ref[...])
        conv_ref[...] = bg_ref[...] * cv
        halo_ref[...] = cx[TR - SUB:TR, :]

        z = fg_ref[...] + b_ref[...]
        logf = jnp.minimum(z, 0.0) - jnp.log1p(jnp.exp(-jnp.abs(z)))
        row = lax.broadcasted_iota(jnp.int32, (TR, TR), 0)
        col = lax.broadcasted_iota(jnp.int32, (TR, TR), 1)
        fcum = _dot((col <= row).astype(F32), logf, NN, lax.Precision.HIGHEST) + carry_ref[0:1, :]
        carry_ref[...] = jnp.broadcast_to(fcum[TR - 1:TR, :], carry_ref.shape)
        f3 = jnp.concatenate(_split3(fcum), axis=1).astype(BF16)
        qo_ref[...] = (_dot(f3, eq_ref[...]) + ones_ref[0:1, :]).astype(BF16)
        ko_ref[...] = (_dot(f3, ek_ref[...]) + ones_ref[1:2, :]).astype(BF16)
        vo_ref[...] = jnp.broadcast_to(ones_ref[2:3, :], vo_ref.shape).astype(BF16)
        for h in range(HEADS):
            sl = slice(DH * h, DH * (h + 1))
            lo = slice(SLAB * h, SLAB * h + DH)
            qh = q_ref[:, sl]
            r = lax.rsqrt(jnp.mean(qh * qh, axis=-1, keepdims=True) + EPS)
            qo_ref[:, lo] = (qh * r * gq_ref[...] * QK_SCALE).astype(BF16)
            kh = k_ref[:, sl]
            r = lax.rsqrt(jnp.mean(kh * kh, axis=-1, keepdims=True) + EPS)
            ko_ref[:, lo] = (kh * r * gk_ref[...]).astype(BF16)
            vo_ref[:, lo] = v_ref[:, sl].astype(BF16)

    eq, ek, ones = _aug_placement()
    o = jax.ShapeDtypeStruct((s, HEADS * SLAB), BF16)
    wide = _row_spec(HEADS * SLAB)
    outs, _ = _call(
        body, (proj, proj, proj, proj, bf_pad, gq, gk, eq, ek, ones, proj, proj, proj, cw_mix), name="qkv_prep",
        grid=(s // TR,),
        in_specs=[_row_spec(AW, 0), _row_spec(AW, 1), _row_spec(AW, 2), _row_spec(LANES, FG_BLOCK),
                  _full_spec((1, LANES)), _full_spec((1, DH)), _full_spec((1, DH)), _full_spec(eq.shape),
                  _full_spec(ek.shape), _full_spec(ones.shape),
                  _row_spec(CW, 3), _row_spec(CW, 4), _row_spec(CW, 5), _full_spec((3, CW))],
        out_specs=[wide, wide, wide, _row_spec(CW)],
        out_shape=[o, o, o, jax.ShapeDtypeStruct((s, CW), F32)],
        scratch_shapes=[pltpu.VMEM((SUB, LANES), F32), pltpu.VMEM((SUB, CW), F32)], vmem_mb=24)
    return outs


def _resid_norm2(x, z, mod, g):
    s = x.shape[0]

    def body(x_ref, z_ref, mod_ref, g_ref, x1_ref, h_ref):
        x1 = x_ref[...] + mod_ref[2:3, :] * z_ref[...]
        x1_ref[...] = x1
        r = lax.rsqrt(jnp.mean(x1 * x1, axis=-1, keepdims=True) + EPS)
        nrm = x1 * r * g_ref[...]
        h_ref[...] = (nrm * (1.0 + mod_ref[4:5, :]) + mod_ref[3:4, :]).astype(BF16)

    return pl.pallas_call(
        body, name="resid_norm2", grid=(s // TRE,),
        in_specs=[_erow(D), _erow(D), _full_spec((SUB, D)), _full_spec((1, D))],
        out_specs=(_erow(D), _erow(D)),
        out_shape=(jax.ShapeDtypeStruct((s, D), F32), jax.ShapeDtypeStruct((s, D), BF16)),
        compiler_params=_params(("parallel",), 24),
    )(x, z, mod, g)


def _loss_head(x1, y, tgt, mod):
    s = x1.shape[0]

    def body(x1_ref, y_ref, t_ref, mod_ref, dout_ref, dy_ref, vec_ref):
        @pl.when(pl.program_id(0) == 0)
        def _():
            vec_ref[...] = jnp.zeros_like(vec_ref)

        yv = y_ref[...]
        g2 = mod_ref[5:6, :]
        diff = x1_ref[...] + g2 * yv - t_ref[...]
        dout = diff * (1.0 / D)
        dout_ref[...] = dout
        dy_ref[...] = (g2 * dout).astype(BF16)
        vec_ref[0:1, :] += jnp.sum(dout * yv, axis=0, keepdims=True)
        vec_ref[1:2, :] += jnp.sum(diff * diff, axis=0, keepdims=True)

    return pl.pallas_call(
        body, name="loss_head", grid=(s // TRE,),
        in_specs=[_erow(D), _erow(D), _erow(D), _full_spec((SUB, D))],
        out_specs=(_erow(D), _erow(D), _full_spec((SUB, D))),
        out_shape=(jax.ShapeDtypeStruct((s, D), F32), jax.ShapeDtypeStruct((s, D), BF16),
                   jax.ShapeDtypeStruct((SUB, D), F32)),
        compiler_params=_params(("arbitrary",), 24),
    )(x1, y, tgt, mod)


def _norm_mod_bwd(dh, xin, dres, zin, mod, g, scale_row, gate_row, name, hosted=None):
    s = dh.shape[0]
    with_gate = gate_row is not None

    def body(*refs):
        if with_gate:
            dh_ref, x_ref, dres_ref, z_ref, mod_ref, g_ref, dx_ref, dz_ref, vec_ref = refs
        else:
            dh_ref, x_ref, dres_ref, mod_ref, g_ref, dx_ref, vec_ref = refs

        @pl.when(pl.program_id(0) == 0)
        def _():
            vec_ref[...] = jnp.zeros_like(vec_ref)

        xv = x_ref[...]
        dhv = dh_ref[...]
        gv = g_ref[...]
        r = lax.rsqrt(jnp.mean(xv * xv, axis=-1, keepdims=True) + EPS)
        xh = xv * r
        dn = dhv * (1.0 + mod_ref[scale_row:scale_row + 1, :])
        dxh = dn * gv
        dx = dres_ref[...] + r * (dxh - xh * jnp.mean(dxh * xh, axis=-1, keepdims=True))
        dx_ref[...] = dx
        vec_ref[0:1, :] += jnp.sum(dhv, axis=0, keepdims=True)
        vec_ref[1:2, :] += jnp.sum(dhv * (xh * gv), axis=0, keepdims=True)
        vec_ref[2:3, :] += jnp.sum(dn * xh, axis=0, keepdims=True)
        if with_gate:
            dz_ref[...] = (mod_ref[gate_row:gate_row + 1, :] * dx).astype(BF16)
            vec_ref[3:4, :] += jnp.sum(dx * z_ref[...], axis=0, keepdims=True)

    ins = [dh, xin, dres] + ([zin] if with_gate else []) + [mod, g]
    in_specs = [_erow(D)] * (4 if with_gate else 3) + [_full_spec((SUB, D)), _full_spec((1, D))]
    out_specs = [_erow(D)] + ([_erow(D)] if with_gate else []) + [_full_spec((SUB, D))]
    out_shape = [jax.ShapeDtypeStruct((s, D), F32)] + ([jax.ShapeDtypeStruct((s, D), BF16)] if with_gate else []) \
        + [jax.ShapeDtypeStruct((SUB, D), F32)]
    outs, moved = _call(body, ins, name=name, grid=(s // TRE,), in_specs=in_specs, out_specs=out_specs,
                        out_shape=out_shape, vmem_mb=32, hosted=hosted)
    return outs + (moved,) if hosted else outs


TA = 512
NEG = -1e30


def _causal_mask():
    row = lax.broadcasted_iota(jnp.int32, (TA, TA), 0)
    col = lax.broadcasted_iota(jnp.int32, (TA, TA), 1)
    return col <= row


def _attn_fwd(qp, kp, vp):
    s = qp.shape[0]
    nq = s // TA

    def body(q_ref, k_ref, v_ref, o_ref, lse_ref):
        i = pl.program_id(1)
        slabs = [slice(SLAB * hh, SLAB * (hh + 1)) for hh in range(2)]
        q = [q_ref[:, sl] for sl in slabs]

        def block(j, carry, masked):
            keys = pl.ds(pl.multiple_of(j * TA, TA), TA)
            ms, acc = carry
            m_out, parts = [], []
            for hh in range(2):
                sc = _dot(q[hh], k_ref[keys, slabs[hh]], NT)
                if masked:
                    sc = jnp.where(_causal_mask(), sc, NEG)
                m_new = jnp.maximum(ms[hh], jnp.max(sc, axis=-1, keepdims=True))
                p = jnp.exp(sc - m_new)
                parts.append(jnp.exp(ms[hh] - m_new) * acc[:, slabs[hh]] + _dot(p.astype(BF16), v_ref[keys, slabs[hh]]))
                m_out.append(m_new)
            return tuple(m_out), jnp.concatenate(parts, axis=1)

        init = ((jnp.full((TA, 1), NEG, F32), jnp.full((TA, 1), NEG, F32)), jnp.zeros((TA, 2 * SLAB), F32))
        carry = lax.fori_loop(0, i, lambda j, cr: block(j, cr, False), init)
        ms, acc = block(i, carry, True)
        for hh in range(2):
            l = acc[:, SLAB * hh + DH:SLAB * hh + DH + 1]
            o_ref[:, DH * hh:DH * (hh + 1)] = acc[:, SLAB * hh:SLAB * hh + DH] / l
            lse_ref[0, :, hh:hh + 1] = ms[hh] + jnp.log(l)

    (o, lse), _ = _call(
        body, (qp, kp, vp), name="attn_fwd", grid=(HEADS // 2, nq),
        in_specs=[pl.BlockSpec((TA, 2 * SLAB), lambda p, i: (i, p)),
                  pl.BlockSpec((s, 2 * SLAB), lambda p, i: (0, p)),
                  pl.BlockSpec((s, 2 * SLAB), lambda p, i: (0, p))],
        out_specs=[pl.BlockSpec((TA, LANES), lambda p, i: (i, p)), pl.BlockSpec((1, TA, 2), lambda p, i: (p, i, 0))],
        out_shape=[jax.ShapeDtypeStruct((s, AW), F32), jax.ShapeDtypeStruct((HEADS // 2, s, 2), F32)],
        vmem_mb=24)
    return o, lse


def _attn_bwd(qp, kp, vp, dmixed, o, lse, hosted):
    s = qp.shape[0]
    nq = s // TA

    def body(q_ref, k_ref, v_ref, do_ref, o_ref, lse_ref, dq_ref, dk_ref, dv_ref, qb_ref, dob_ref):
        dk_ref[...] = jnp.zeros_like(dk_ref)
        dv_ref[...] = jnp.zeros_like(dv_ref)
        slabs = [slice(SLAB * hh, SLAB * (hh + 1)) for hh in range(2)]
        lane = lax.broadcasted_iota(jnp.int32, (TA, DH), 1)

        def q_block(i, _):
            i0 = pl.multiple_of(i * TA, TA)
            rows = pl.ds(i0, TA)
            for hh in range(2):
                half = slice(DH * hh, DH * (hh + 1))
                do = do_ref[rows, half]
                delta = jnp.sum(do * o_ref[rows, half], axis=-1, keepdims=True)
                dob_ref[hh, :, 0:DH] = do.astype(BF16)
                dob_ref[hh, :, DH:SLAB] = _lanes3(lane, 0, [-d for d in _split3(delta)], 0.0).astype(BF16)
                lse3 = _split3(lse_ref[0, rows, hh:hh + 1])
                qb_ref[hh, :, 0:DH] = q_ref[rows, SLAB * hh:SLAB * hh + DH]
                aug = q_ref[rows, SLAB * hh + DH:SLAB * (hh + 1)].astype(F32)
                qb_ref[hh, :, DH:SLAB] = _lanes3(lane, AUG_LSE, [-x for x in lse3], aug).astype(BF16)

            def block(j, dq, masked):
                keys = pl.ds(pl.multiple_of(j * TA, TA), TA)
                dv, dk, dqc = [], [], []
                for hh in range(2):
                    q, dob = qb_ref[hh], dob_ref[hh]
                    k = k_ref[keys, slabs[hh]]
                    sc = _dot(q, k, NT)
                    if masked:
                        sc = jnp.where(_causal_mask(), sc, NEG)
                    p = jnp.exp(sc)
                    dv.append(_dot(p.astype(BF16), dob, TN))
                    ds = (p * _dot(dob, v_ref[keys, slabs[hh]], NT)).astype(BF16)
                    dk.append(_dot(ds, q, TN))
                    dqc.append(_dot(ds, k))
                dv_ref[keys, :] += jnp.concatenate(dv, axis=1)
                dk_ref[keys, :] += jnp.concatenate(dk, axis=1)
                return dq + jnp.concatenate(dqc, axis=1)

            dq = lax.fori_loop(0, i, lambda j, acc: block(j, acc, False), jnp.zeros((TA, 2 * SLAB), F32))
            dq_ref[rows, :] = block(i, dq, True)
            return 0

        lax.fori_loop(0, nq, q_block, 0)

    pair = lambda p: (0, p)
    slab2 = pl.BlockSpec((s, 2 * SLAB), pair)
    seq = pl.BlockSpec((s, LANES), pair)
    small = pl.BlockSpec((1, s, 2), lambda p: (p, 0, 0))
    o32 = jax.ShapeDtypeStruct((s, HEADS * SLAB), F32)
    return _call(
        body, (qp, kp, vp, dmixed, o, lse), name="attn_bwd", grid=(HEADS // 2,),
        in_specs=[slab2, slab2, slab2, seq, seq, small], out_specs=[slab2, slab2, slab2], out_shape=[o32, o32, o32],
        scratch_shapes=[pltpu.VMEM((2, TA, SLAB), BF16), pltpu.VMEM((2, TA, SLAB), BF16)], vmem_mb=36, hosted=hosted)


def _qkv_post(dqp, dkp, dvp, dmixed, proj, bf_pad, gq, gk, cw_mix):
    s = proj.shape[0]
    nb = s // TR

    def body(dq_ref, dk_ref, dv_ref, q_ref, k_ref, fg_ref, b_ref, gq_ref, gk_ref, dc_ref, xin_ref, bg_ref, cg_ref,
             xinh_ref, cgh_ref, cw_ref, dqo_ref, dko_ref, dvo_ref, dfg_ref, vec_ref, dxin_ref, dbg_ref, dcg_ref, dcw_ref,
             carry_ref, nxt_ref):
        i = pl.program_id(0)

        @pl.when(i == 0)
        def _():
            vec_ref[...] = jnp.zeros_like(vec_ref)
            carry_ref[...] = jnp.zeros_like(carry_ref)
            dcw_ref[...] = jnp.zeros_like(dcw_ref)

        wv = cw_ref[...]
        xin, cg, dconv = xin_ref[...], cg_ref[...], dc_ref[...]
        cx = cg * xin
        cv, s1, s2 = _conv_taps(cx, jnp.where(i < nb - 1, cgh_ref[...] * xinh_ref[...], 0.0), wv)
        dbg_ref[...] = (dconv * cv).astype(BF16)
        dcv = dconv * bg_ref[...]
        dcw_ref[0:1, :] += jnp.sum(dcv * s2, axis=0, keepdims=True)
        dcw_ref[1:2, :] += jnp.sum(dcv * s1, axis=0, keepdims=True)
        dcw_ref[2:3, :] += jnp.sum(dcv * cx, axis=0, keepdims=True)
        dcx = _conv_taps_t(dcv, jnp.where(i > 0, nxt_ref[...], 0.0), wv)
        nxt_ref[...] = dcv[0:SUB, :]
        dcg_ref[...] = (dcx * xin).astype(BF16)
        dxin_ref[...] = (dcx * cg).astype(BF16)

        def one(d_ref, x_ref, g_ref, o_ref, row, scale):
            dg = jnp.zeros((1, DH), F32)
            for h in range(HEADS):
                sl = slice(DH * h, DH * (h + 1))
                xv = x_ref[:, sl]
                r = lax.rsqrt(jnp.mean(xv * xv, axis=-1, keepdims=True) + EPS)
                xh = xv * r
                dn = d_ref[:, SLAB * h:SLAB * h + DH] * scale
                dg = dg + jnp.sum(dn * xh, axis=0, keepdims=True)
                dxh = dn * g_ref[...]
                o_ref[:, sl] = (r * (dxh - xh * jnp.mean(dxh * xh, axis=-1, keepdims=True))).astype(BF16)
            vec_ref[row:row + 1, 0:DH] += dg

        one(dq_ref, q_ref, gq_ref, dqo_ref, 0, QK_SCALE)
        one(dk_ref, k_ref, gk_ref, dko_ref, 1, 1.0)
        lane = lax.broadcasted_iota(jnp.int32, (TR, LANES), 1)
        df = jnp.zeros((TR, LANES), F32)
        for h in range(HEADS):
            dvo_ref[:, DH * h:DH * (h + 1)] = dv_ref[:, SLAB * h:SLAB * h + DH].astype(BF16)
            row_sum = dq_ref[:, SLAB * h + DH:SLAB * h + DH + 1]
            col_sum = dk_ref[:, SLAB * h + DH + AUG_ONE:SLAB * h + DH + AUG_ONE + 1]
            df = jnp.where(lane == h, row_sum - col_sum, df)
        row = lax.broadcasted_iota(jnp.int32, (TR, TR), 0)
        col = lax.broadcasted_iota(jnp.int32, (TR, TR), 1)
        dlogf = _dot((col >= row).astype(F32), df, NN, lax.Precision.HIGHEST) + carry_ref[0:1, :]
        carry_ref[...] = jnp.broadcast_to(dlogf[0:1, :], carry_ref.shape)
        dfg = dlogf * _sigmoid(-(fg_ref[...] + b_ref[...]))
        dfg_ref[...] = dfg.astype(BF16)
        vec_ref[2:3, :] += jnp.sum(dfg, axis=0, keepdims=True)

    o = jax.ShapeDtypeStruct((s, AW), BF16)
    rev = lambda width, col=0: pl.BlockSpec((TR, width), lambda i, col=col: (nb - 1 - i, col))
    wide = rev(HEADS * SLAB)
    halo = lambda col: pl.BlockSpec((SUB, CW), lambda i, col=col: (jnp.maximum((nb - 1 - i) * (TR // SUB) - 1, 0), col))
    outs, _ = _call(
        body, (dqp, dkp, dvp, proj, proj, proj, bf_pad, gq, gk, dmixed, proj, proj, proj, proj, proj, cw_mix),
        name="qkv_post", grid=(nb,),
        in_specs=[wide, wide, wide, rev(AW, 0), rev(AW, 1), rev(LANES, FG_BLOCK), _full_spec((1, LANES)),
                  _full_spec((1, DH)), _full_spec((1, DH)),
                  rev(CW, 1), rev(CW, 3), rev(CW, 4), rev(CW, 5), halo(3), halo(5), _full_spec((3, CW))],
        out_specs=[rev(AW), rev(AW), rev(AW), rev(LANES), _full_spec((SUB, LANES)),
                   rev(CW), rev(CW), rev(CW), _full_spec((SUB, CW))],
        out_shape=[o, o, o, jax.ShapeDtypeStruct((s, LANES), BF16), jax.ShapeDtypeStruct((SUB, LANES), F32),
                   o, o, o, jax.ShapeDtypeStruct((SUB, CW), F32)],
        scratch_shapes=[pltpu.VMEM((SUB, LANES), F32), pltpu.VMEM((SUB, CW), F32)], vmem_mb=24)
    return outs


TF = 256
NJ = DFF // TF
FFN_ROWS_FWD = 1024
FFN_ROWS_BWD = 1024


def _ffn_fwd(h2, wup_t, cw, wd):
    s = h2.shape[0]
    tr = FFN_ROWS_FWD
    nr = s // tr

    def body(h_ref, wu_ref, cg_ref, cv_ref, wd_ref, pg_ref, pv_ref, y_ref, halo_ref, act_ref):
        r, j = pl.program_id(0), pl.program_id(1)
        hv = h_ref[...]
        pg = _dot(hv, wu_ref[0], NT).astype(BF16)
        pv = _dot(hv, wu_ref[1], NT).astype(BF16)
        pg_ref[...] = pg
        pv_ref[...] = pv
        pgf, pvf = pg.astype(F32), pv.astype(F32)
        ug, _, _ = _conv_taps(pgf, jnp.where(r > 0, halo_ref[j, 0], 0.0), cg_ref[...])
        uv, _, _ = _conv_taps(pvf, jnp.where(r > 0, halo_ref[j, 1], 0.0), cv_ref[...])
        halo_ref[j, 0] = pgf[tr - SUB:tr, :]
        halo_ref[j, 1] = pvf[tr - SUB:tr, :]
        act = (ug * _sigmoid(ug) * uv).astype(BF16)
        for t in range(NJ):
            @pl.when(j == t)
            def _(t=t):
                act_ref[:, t * TF:(t + 1) * TF] = act

        @pl.when(j == NJ - 1)
        def _():
            y_ref[...] = _dot(act_ref[...], wd_ref[...])

    pre = jax.ShapeDtypeStruct((s, DFF), BF16)
    return pl.pallas_call(
        body, name="ffn_fwd", grid=(nr, NJ),
        in_specs=[pl.BlockSpec((tr, D), lambda r, j: (r, 0)),
                  pl.BlockSpec((2, TF, D), lambda r, j: (0, j, 0)),
                  pl.BlockSpec((3, TF), lambda r, j: (0, j)),
                  pl.BlockSpec((3, TF), lambda r, j: (0, NJ + j)),
                  pl.BlockSpec((DFF, D), lambda r, j: (0, 0))],
        out_specs=(pl.BlockSpec((tr, TF), lambda r, j: (r, j)),
                   pl.BlockSpec((tr, TF), lambda r, j: (r, j)),
                   pl.BlockSpec((tr, D), lambda r, j: (r, 0))),
        out_shape=(pre, pre, jax.ShapeDtypeStruct((s, D), F32)),
        scratch_shapes=[pltpu.VMEM((NJ, 2, SUB, TF), F32), pltpu.VMEM((tr, DFF), BF16)],
        compiler_params=_params(("arbitrary", "arbitrary"), 52),
    )(h2, wup_t, cw, cw, wd)


def _ffn_bwd(dy, h2, pre_g, pre_v, wup_t, cw, wd):
    s = h2.shape[0]
    tr = FFN_ROWS_BWD
    nr = s // tr
    hb = tr // (2 * SUB)

    def body(dy_ref, h_ref, pg_ref, pv_ref, hg_ref, hv_ref, wu_ref, cg_ref, cv_ref, wd_ref,
             dh_ref, dwu_ref, dwd_ref, dcg_ref, dcv_ref, nxt_ref, awu_ref, awd_ref):
        j, r = pl.program_id(0), pl.program_id(1)
        rr = nr - 1 - r
        row0 = pl.multiple_of(rr * tr, tr)
        cwg, cwv = cg_ref[...], cv_ref[...]
        pg, pv = pg_ref[...].astype(F32), pv_ref[...].astype(F32)
        ug, g1, g2 = _conv_taps(pg, jnp.where(rr > 0, hg_ref[SUB:2 * SUB, :].astype(F32), 0.0), cwg)
        uv, v1, v2 = _conv_taps(pv, jnp.where(rr > 0, hv_ref[SUB:2 * SUB, :].astype(F32), 0.0), cwv)
        sg = _sigmoid(ug)
        sil = ug * sg
        act = (sil * uv).astype(BF16)
        dyv = dy_ref[...]
        da = _dot(dyv, wd_ref[...], NT)
        dug = da * uv * (sg * (1.0 + ug * (1.0 - sg)))
        duv = da * sil
        dpg = _conv_taps_t(dug, jnp.where(r > 0, nxt_ref[0], 0.0), cwg)
        dpv = _conv_taps_t(duv, jnp.where(r > 0, nxt_ref[1], 0.0), cwv)
        nxt_ref[0] = dug[0:SUB, :]
        nxt_ref[1] = duv[0:SUB, :]
        dpgb, dpvb = dpg.astype(BF16), dpv.astype(BF16)
        hv = h_ref[...]
        dwd = _dot(act, dyv, TN)
        dpb = jnp.concatenate([dpgb, dpvb], axis=1)
        dwu = _dot(dpb, hv, TN)
        dh = _dot(dpb, wu_ref[...].reshape(2 * TF, D))

        def taps(du, x0, x1, x2):
            return (jnp.sum(du * x2, axis=0, keepdims=True), jnp.sum(du * x1, axis=0, keepdims=True),
                    jnp.sum(du * x0, axis=0, keepdims=True))

        tg, tv = taps(dug, pg, g1, g2), taps(duv, pv, v1, v2)

        @pl.when(r == 0)
        def _():
            awd_ref[...] = dwd
            awu_ref[...] = dwu
            dcg_ref[...] = jnp.zeros_like(dcg_ref)
            dcv_ref[...] = jnp.zeros_like(dcv_ref)

        @pl.when(r > 0)
        def _():
            awd_ref[...] += dwd
            awu_ref[...] += dwu

        @pl.when(r == nr - 1)
        def _():
            dwd_ref[...] = awd_ref[...].astype(BF16)
            dwu_ref[...] = awu_ref[...].astype(BF16).reshape(2, TF, D)

        for t in range(3):
            dcg_ref[t:t + 1, :] += tg[t]
            dcv_ref[t:t + 1, :] += tv[t]

        @pl.when(j == 0)
        def _():
            dh_ref[pl.ds(row0, tr), :] = dh

        @pl.when(j > 0)
        def _():
            dh_ref[pl.ds(row0, tr), :] += dh

    rows = lambda j, r: (nr - 1 - r, 0)
    tile = lambda j, r: (nr - 1 - r, j)
    halo = lambda j, r: (jnp.maximum((nr - 1 - r) * hb - 1, 0), j)
    return pl.pallas_call(
        body, name="ffn_bwd", grid=(NJ, nr),
        in_specs=[pl.BlockSpec((tr, D), rows), pl.BlockSpec((tr, D), rows),
                  pl.BlockSpec((tr, TF), tile), pl.BlockSpec((tr, TF), tile),
                  pl.BlockSpec((2 * SUB, TF), halo), pl.BlockSpec((2 * SUB, TF), halo),
                  pl.BlockSpec((2, TF, D), lambda j, r: (0, j, 0)),
                  pl.BlockSpec((3, TF), lambda j, r: (0, j)), pl.BlockSpec((3, TF), lambda j, r: (0, NJ + j)),
                  pl.BlockSpec((TF, D), lambda j, r: (j, 0))],
        out_specs=(pl.BlockSpec((s, D), lambda j, r: (0, 0)),
                   pl.BlockSpec((2, TF, D), lambda j, r: (0, j, 0)),
                   pl.BlockSpec((TF, D), lambda j, r: (j, 0)),
                   pl.BlockSpec((SUB, TF), lambda j, r: (0, j)), pl.BlockSpec((SUB, TF), lambda j, r: (0, j))),
        out_shape=(jax.ShapeDtypeStruct((s, D), F32),
                   jax.ShapeDtypeStruct((2, DFF, D), BF16), jax.ShapeDtypeStruct((DFF, D), BF16),
                   jax.ShapeDtypeStruct((SUB, DFF), F32), jax.ShapeDtypeStruct((SUB, DFF), F32)),
        scratch_shapes=[pltpu.VMEM((2, SUB, TF), F32), pltpu.VMEM((2 * TF, D), F32), pltpu.VMEM((TF, D), F32)],
        compiler_params=_params(("arbitrary", "arbitrary"), 52),
    )(dy, h2, pre_g, pre_v, pre_g, pre_v, wup_t, cw, cw, wd)


def _adam(w, g, m, v):
    m = ADAM_B1 * m + (1.0 - ADAM_B1) * g
    v = ADAM_B2 * v + (1.0 - ADAM_B2) * (g * g)
    m_hat = m / (1.0 - ADAM_B1 ** ADAM_STEP)
    v_hat = v / (1.0 - ADAM_B2 ** ADAM_STEP)
    delta = -ADAM_LR * (m_hat / (jnp.sqrt(v_hat) + ADAM_EPS) + ADAM_WD * w)
    return delta, m, v


NCHIP = NDEV // 2


def _pair_add(mine, theirs, tr, name):
    _, _, rws, cols = mine.shape

    def body(a_ref, b_ref, o_ref):
        c = lax.axis_index("c")
        o_ref[0] = (a_ref[0, c].astype(F32) + b_ref[0].astype(F32)).astype(BF16)

    (out,), _ = _call(
        body, (mine, theirs), name=name, grid=(NCHIP, rws // tr),
        in_specs=[pl.BlockSpec((1, 2, tr, cols), lambda q, i: (q, 0, i, 0)),
                  pl.BlockSpec((1, tr, cols), lambda q, i: (q, i, 0))],
        out_specs=[pl.BlockSpec((1, tr, cols), lambda q, i: (q, i, 0))],
        out_shape=[jax.ShapeDtypeStruct((NCHIP, rws, cols), BF16)], vmem_mb=16)
    return out


def _adamw_sharded(parts, w, m, v, tr, name, hosted=None):
    rws, cols = w.shape
    n_parts = parts.shape[0]

    def body(p_ref, w_ref, m_ref, v_ref, g_ref, d_ref, mo_ref, vo_ref):
        g = p_ref[0].astype(F32)
        for q in range(1, n_parts):
            g = g + p_ref[q].astype(F32)
        g_ref[...] = g
        d_ref[...], mo_ref[...], vo_ref[...] = _adam(w_ref[...], g, m_ref[...], v_ref[...])

    blk = pl.BlockSpec((tr, cols), lambda i: (i, 0))
    o = jax.ShapeDtypeStruct((rws, cols), F32)
    outs, moved = _call(
        body, (parts, w, m, v), name=name, grid=(rws // tr,),
        in_specs=[pl.BlockSpec((n_parts, tr, cols), lambda i: (0, i, 0)), blk, blk, blk],
        out_specs=[blk, blk, blk, blk], out_shape=[o, o, o, o], vmem_mb=44 if tr > 256 else 24, hosted=hosted)
    return (outs, moved) if hosted else outs


def _adamw_ada(c_all, dmod_my, w, m, v):
    rws, cols = w.shape
    tr = 256

    def body(c_ref, dm_ref, w_ref, m_ref, v_ref, g_ref, d_ref, mo_ref, vo_ref):
        cv = c_ref[...]
        act = cv * _sigmoid(cv)
        g = _dot(act, dm_ref[...], TN, lax.Precision.HIGHEST)
        g_ref[...] = g
        d_ref[...], mo_ref[...], vo_ref[...] = _adam(w_ref[...], g, m_ref[...], v_ref[...])

    blk = pl.BlockSpec((tr, cols), lambda i: (i, 0))
    o = jax.ShapeDtypeStruct((rws, cols), F32)
    return pl.pallas_call(
        body, name="adamw_ada", grid=(rws // tr,),
        in_specs=[pl.BlockSpec((NDEV, tr), lambda i: (0, i)), _full_spec((NDEV, cols)), blk, blk, blk],
        out_specs=(blk, blk, blk, blk), out_shape=(o, o, o, o),
        compiler_params=_params(("parallel",), 32),
    )(c_all, dmod_my, w, m, v)


REP_ROWS = 16
ROW_N1, ROW_N2, ROW_LOSS, ROW_MISC = 6, 7, 8, 9
LANE_BF, LANE_GQ, LANE_GK = 0, 128, 256


def _adamw_small(rep_all, conv_all, wmv):
    n_ff = wmv[6][0].shape[1]

    def body(*refs):
        rep_ref, conv_ref = refs[:2]
        ins = refs[2:2 + 24]
        outs = refs[2 + 24:]
        loss_ref, outs = outs[0], outs[1:]
        g_rep = rep_ref[0]
        g_conv = conv_ref[0]
        for d in range(1, NDEV):
            g_rep = g_rep + rep_ref[d]
            g_conv = g_conv + conv_ref[d]
        loss_ref[...] = (0.5 / D) * jnp.sum(g_rep[ROW_LOSS:ROW_LOSS + 1, :], axis=-1, keepdims=True)
        grads = [
            None,
            g_rep[ROW_N1:ROW_N1 + 1, :],
            g_rep[ROW_MISC:ROW_MISC + 1, LANE_BF:LANE_BF + HEADS],
            g_rep[ROW_MISC:ROW_MISC + 1, LANE_GQ:LANE_GQ + DH],
            g_rep[ROW_MISC:ROW_MISC + 1, LANE_GK:LANE_GK + DH],
            g_rep[ROW_N2:ROW_N2 + 1, :],
            g_conv[0:3, 0:n_ff],
            g_conv[0:3, n_ff:n_ff + DH],
        ]
        for p in range(8):
            w_ref, m_ref, v_ref = ins[3 * p:3 * p + 3]
            g_ref, d_ref, mo_ref, vo_ref = outs[4 * p:4 * p + 4]
            if p == 0:
                for nmod in range(NMOD):
                    sl = slice(D * nmod, D * (nmod + 1))
                    g = g_rep[nmod:nmod + 1, :]
                    g_ref[:, sl] = g
                    d_ref[:, sl], mo_ref[:, sl], vo_ref[:, sl] = _adam(w_ref[:, sl], g, m_ref[:, sl], v_ref[:, sl])
            else:
                g = grads[p]
                g_ref[...] = g
                d_ref[...], mo_ref[...], vo_ref[...] = _adam(w_ref[...], g, m_ref[...], v_ref[...])

    flat = [a for trio in wmv for a in trio]
    out_shape = [jax.ShapeDtypeStruct((1, 1), F32)]
    for trio in wmv:
        out_shape += [jax.ShapeDtypeStruct(trio[0].shape, F32)] * 4
    ins = [rep_all, conv_all] + flat
    return pl.pallas_call(
        body, name="adamw_small", grid=(1,),
        in_specs=[_full_spec(a.shape) for a in ins], out_specs=tuple(_full_spec(o.shape) for o in out_shape),
        out_shape=tuple(out_shape), compiler_params=_params(("arbitrary",), 32),
    )(*ins)


FG_FIRST = 3 * AW
N_IN = DIN // NDEV


def _w_in_runs():
    runs = []
    for d in range(NDEV):
        lo, hi = N_IN * d, N_IN * (d + 1)
        for a, b, shift in ((0, FG_FIRST, 0), (FG_FIRST, FG_FIRST + HEADS, DIN - HEADS - FG_FIRST),
                            (FG_FIRST + HEADS, DIN, -HEADS)):
            a, b = max(a, lo), min(b, hi)
            if a < b:
                runs.append((d, a - lo, a + shift, b - a))
    return runs


W_IN_ROWS = 256
N_IN_PAD = 512


def _identity(n):
    return (lax.broadcasted_iota(jnp.int32, (n, n), 0) == lax.broadcasted_iota(jnp.int32, (n, n), 1)).astype(BF16)


def _assemble_w_in(g_in, hosted):
    def body(g_ref, o_ref, t_ref):
        eye = _identity(W_IN_ROWS)
        shard = None
        for d, src, dst, width in _w_in_runs():
            if d != shard:
                t_ref[:, 0:N_IN] = _dot(eye, g_ref[d], NT).astype(BF16)
                shard = d
            o_ref[:, dst:dst + width] = t_ref[:, src:src + width]
        o_ref[:, DIN:DINP] = jnp.zeros((W_IN_ROWS, DINP - DIN), o_ref.dtype)

    (out,), moved = _call(
        body, (g_in,), name="assemble_w_in", grid=(D // W_IN_ROWS,),
        in_specs=[pl.BlockSpec((NDEV, N_IN, W_IN_ROWS), lambda i: (0, 0, i))],
        out_specs=[pl.BlockSpec((W_IN_ROWS, DINP), lambda i: (i, 0))],
        out_shape=[jax.ShapeDtypeStruct((D, DINP), g_in.dtype)],
        scratch_shapes=[pltpu.VMEM((W_IN_ROWS, N_IN_PAD), BF16)], vmem_mb=16, hosted=hosted)
    return out, moved


def _scatter_dw_in(dwp):
    def body(w_ref, o_ref, t_ref):
        eye = _identity(W_IN_ROWS)
        runs = _w_in_runs()
        for i, (d, src, dst, width) in enumerate(runs):
            t_ref[:, src:src + width] = w_ref[:, dst:dst + width]
            if i + 1 == len(runs) or runs[i + 1][0] != d:
                o_ref[d // 2, d % 2] = _dot(t_ref[:, 0:N_IN], eye, TN).astype(BF16)

    (out,), _ = _call(
        body, (dwp,), name="scatter_dw_in", grid=(D // W_IN_ROWS,),
        in_specs=[pl.BlockSpec((W_IN_ROWS, DINP), lambda i: (i, 0))],
        out_specs=[pl.BlockSpec((NCHIP, 2, N_IN, W_IN_ROWS), lambda i: (0, 0, 0, i))],
        out_shape=[jax.ShapeDtypeStruct((NCHIP, 2, N_IN, D), dwp.dtype)],
        scratch_shapes=[pltpu.VMEM((W_IN_ROWS, N_IN_PAD), BF16)], vmem_mb=16)
    return out


def kernel(x, c, w_ada, b_ada, norm1_g, w_in, b_forget, q_norm_g, k_norm_g, conv_mix_w, w_out, norm2_g, w_up, ffn_conv_w, w_down, loss_target, m_w_ada, m_b_ada, m_norm1_g, m_w_in, m_b_forget, m_q_norm_g, m_k_norm_g, m_conv_mix_w, m_w_out, m_norm2_g, m_w_up, m_ffn_conv_w, m_w_down, v_w_ada, v_b_ada, v_norm1_g, v_w_in, v_b_forget, v_q_norm_g, v_k_norm_g, v_conv_mix_w, v_w_out, v_norm2_g, v_w_up, v_ffn_conv_w, v_w_down):
    me = 4 * lax.axis_index("x") + 2 * lax.axis_index("y") + lax.axis_index("c")
    xs, tgt = x[0], loss_target[0]
    n_ada = w_ada.shape[2]
    n_ff = w_up.shape[2]

    conv_w = jnp.concatenate([ffn_conv_w[0], conv_mix_w[0]], axis=1)
    conv_w = jnp.concatenate([conv_w, jnp.zeros((SUB - 3, conv_w.shape[1]), F32)], axis=0)
    c_all, conv_all, g_in = _exchange(
        [(c.reshape(SUB, D // SUB), "ag"), (conv_w, "ag"), (jnp.transpose(w_in[0]).astype(BF16), "ag2")],
        "exchange_w_in")
    g_in, w_out_b, w_up_b, w_down_b = lax.optimization_barrier(
        (g_in, w_out[0].astype(BF16), jnp.transpose(w_up[0]).astype(BF16), w_down[0].astype(BF16)))
    g_out, g_up, g_down = _sequencer_exchange(
        [(w_out_b, "ag2"), (w_up_b, "ag2"), (w_down_b, "ag2")], "gather_weights", collective_id=1)
    c_all = c_all.reshape(NDEV, D)
    cw_ffn = jnp.transpose(conv_all[:, :3, :n_ff], (1, 0, 2)).reshape(3, 2 * DFF)
    cw_mix = jnp.transpose(conv_all[:, :3, n_ff:], (1, 0, 2)).reshape(3, CW)

    b_my = lax.dynamic_slice(b_ada, (0, me * n_ada), (1, n_ada))
    mod_part = _ada_fwd(c_all, w_ada[0], b_my)
    w_in_p, (mod_rows,) = _assemble_w_in(
        g_in, [(jnp.broadcast_to(mod_part[:, None, :], (NDEV, SUB, n_ada)), "a2a")])
    mod = mod_rows[:, 0, :].reshape(NMOD, D)
    mod = jnp.concatenate([mod, jnp.zeros((SUB - NMOD, D), F32)], axis=0)

    h = _norm_mod_fwd(xs, mod, norm1_g)
    proj = _mm(h, w_in_p, "nn", F32, 1024, 640, "proj_fwd")
    bf_pad = jnp.concatenate([b_forget, jnp.zeros((1, LANES - HEADS), F32)], axis=1)
    qp, kp, vp, conv = _qkv_prep(proj, bf_pad, q_norm_g, k_norm_g, cw_mix)
    attn, lse = _attn_fwd(qp, kp, vp)
    w_out_f = g_out.reshape(D, D)
    w_up_t = g_up.reshape(2, DFF, D)
    w_down_f = g_down.reshape(DFF, D)
    mixed = jnp.concatenate([attn, conv], axis=1).astype(BF16)
    z = _mm(mixed, w_out_f, "nn", F32, 1024, 1024, "out_fwd")
    x1, h2 = _resid_norm2(xs, z, mod, norm2_g)
    pre_g, pre_v, y = _ffn_fwd(h2, w_up_t, cw_ffn, w_down_f)
    dout, dy, vec_l = _loss_head(x1, y, tgt, mod)

    dh2, dwup_t, dwd, dcw_g, dcw_v = _ffn_bwd(dy, h2, pre_g, pre_v, w_up_t, cw_ffn, w_down_f)
    s_down = dwd.reshape(NCHIP, 2, DFF // NDEV, D)
    s_up = dwup_t.reshape(NCHIP, 2, n_ff, D)
    dx1, dz, vec_2, (t_up, t_down) = _norm_mod_bwd(dh2, x1, dout, z, mod, norm2_g, 4, 2, "norm2_bwd",
                                                   hosted=[(s_up, "pair"), (s_down, "pair")])
    dwout = _mm(mixed, dz, "tn", BF16, 1024, 1024, "out_bwd_w")
    s_out = dwout.reshape(NCHIP, 2, D // NDEV, D)
    dmixed, (t_out,) = _mm(dz, w_out_f, "nt", F32, 1024, 1024, "out_bwd_x", hosted=[(s_out, "pair")])
    c_out = _pair_add(s_out, t_out, 128, "pair_add_out")
    c_up = _pair_add(s_up, t_up, 176, "pair_add_up")
    c_down = _pair_add(s_down, t_down, 176, "pair_add_down")
    (dqp, dkp, dvp), (p_up, p_down, p_out) = _attn_bwd(
        qp, kp, vp, dmixed, attn, lse, [(c_up, "chips"), (c_down, "chips"), (c_out, "chips")])
    dq, dk, dvb, dfg, vec_qk, dxin, dbg, dcg, dcw_mix = _qkv_post(
        dqp, dkp, dvp, dmixed, proj, bf_pad, q_norm_g, k_norm_g, cw_mix)
    dproj = jnp.concatenate([dq, dk, dvb, dxin, dbg, dcg, dfg], axis=1)
    dwin_p = _mm(h, dproj, "tn", BF16, 1024, 640, "proj_bwd_w")
    s_in = _scatter_dw_in(dwin_p).reshape(NDEV, N_IN, D)
    (p_in,) = _sequencer_exchange([(s_in, "a2a")], "scatter_dw_in_partials", collective_id=2, all_peers=True)
    dh = _mm(dproj, w_in_p, "nt", F32, 1024, 512, "proj_bwd_x", vmem_mb=36)
    grad_x, vec_1 = _norm_mod_bwd(dh, xs, dx1, None, mod, norm1_g, 1, None, "norm1_bwd")

    gap = lambda n: jnp.zeros((1, n), F32)
    misc = jnp.concatenate([
        vec_qk[2:3, :HEADS], gap(LANE_GQ - LANE_BF - HEADS), vec_qk[0:1, :DH], gap(LANE_GK - LANE_GQ - DH),
        vec_qk[1:2, :DH], gap(D - LANE_GK - DH)], axis=1)
    rep = jnp.concatenate([
        vec_1[0:1], vec_1[1:2], vec_2[3:4], vec_2[0:1], vec_2[1:2], vec_l[0:1],
        vec_1[2:3], vec_2[2:3], vec_l[1:2], misc, jnp.zeros((REP_ROWS - 10, D), F32)], axis=0)
    dcw_ffn = jnp.concatenate([dcw_g, dcw_v], axis=1).reshape(SUB, NDEV, n_ff)
    dcw_all = jnp.concatenate([jnp.transpose(dcw_ffn, (1, 0, 2)),
                               jnp.transpose(dcw_mix.reshape(SUB, NDEV, DH), (1, 0, 2))], axis=2)
    r_up = _adamw_sharded(p_up, jnp.transpose(w_up[0]), jnp.transpose(m_w_up[0]), jnp.transpose(v_w_up[0]), 176,
                          "adamw_up")
    r_down = _adamw_sharded(p_down, w_down[0], m_w_down[0], v_w_down[0], 176, "adamw_down")
    rep, dcw_all, r_up, r_down = lax.optimization_barrier((rep, dcw_all, r_up, r_down))
    r_up = tuple(jnp.transpose(a) for a in r_up)
    r_out, (rep_all, conv_parts) = _adamw_sharded(p_out, w_out[0], m_w_out[0], v_w_out[0], 128, "adamw_out",
                                                  hosted=[(rep, "ag"), (dcw_all, "a2a")])
    dmod_my = lax.dynamic_slice(rep_all[:, :NMOD, :].reshape(NDEV, NMOD * D), (0, me * n_ada), (NDEV, n_ada))
    r_ada = _adamw_ada(c_all, dmod_my, w_ada[0], m_w_ada[0], v_w_ada[0])
    r_in = _adamw_sharded(p_in, jnp.transpose(w_in[0]), jnp.transpose(m_w_in[0]), jnp.transpose(v_w_in[0]), N_IN,
                          "adamw_in")
    r_in = tuple(jnp.transpose(a) for a in r_in)
    small = _adamw_small(rep_all, conv_parts, [
        [b_ada, m_b_ada, v_b_ada], [norm1_g, m_norm1_g, v_norm1_g], [b_forget, m_b_forget, v_b_forget],
        [q_norm_g, m_q_norm_g, v_q_norm_g], [k_norm_g, m_k_norm_g, v_k_norm_g], [norm2_g, m_norm2_g, v_norm2_g],
        [ffn_conv_w[0], m_ffn_conv_w[0], v_ffn_conv_w[0]], [conv_mix_w[0], m_conv_mix_w[0], v_conv_mix_w[0]]])
    loss = small[0].reshape(())
    r_bada, r_n1, r_bf, r_gq, r_gk, r_n2, r_cf, r_cm = [small[1 + 4 * p:5 + 4 * p] for p in range(8)]
    lead = lambda t: tuple(a[None] for a in t)
    per_w = [lead(r_ada), r_bada, r_n1, lead(r_in), r_bf, r_gq, r_gk, lead(r_cm), lead(r_out), r_n2,
             lead(r_up), lead(r_cf), lead(r_down)]
    outs = [loss, grad_x[None]]
    for field in range(4):
        outs += [t[field] for t in per_w]
    return tuple(outs)
```

```python
import functools

import jax
import jax.numpy as jnp
import numpy as np
from jax import lax
from jax.experimental import pallas as pl
from jax.experimental.pallas import tpu as pltpu
from jax.experimental.pallas import tpu_sc as plsc

F32 = jnp.float32
BF16 = jnp.bfloat16

NDEV = 8
D = 1024
HEADS = 8
DH = 64
AW = 512
CW = 512
DFF = 2816
DIN = 3080
DINP = 3200
NMOD = 6
EPS = 1e-6
QK_SCALE = 0.125
LANES = 128
SUB = 8

ADAM_LR = 0.001
ADAM_B1 = 0.9
ADAM_B2 = 0.999
ADAM_EPS = 1e-08
ADAM_WD = 0.01
ADAM_STEP = 10

MESH = pl.DeviceIdType.MESH
ANY = pl.BlockSpec(memory_space=pl.ANY)

NN = (((1,), (0,)), ((), ()))
NT = (((1,), (1,)), ((), ()))
TN = (((0,), (0,)), ((), ()))


def _dot(a, b, dims=NN, precision=None):
    return lax.dot_general(a, b, dims, precision=precision, preferred_element_type=F32)


def _params(sem=None, vmem_mb=None):
    kw = {}
    if sem is not None:
        kw["dimension_semantics"] = sem
    if vmem_mb is not None:
        kw["vmem_limit_bytes"] = vmem_mb * 1024 * 1024
    return pltpu.CompilerParams(**kw)


def _sigmoid(x):
    return 0.5 * jnp.tanh(0.5 * x) + 0.5


class _Exchange:
    def __init__(self, items):
        self.arrays = [pltpu.with_memory_space_constraint(a, pltpu.HBM) for a, _ in items]
        self.modes = [m for _, m in items]
        self.n = len(items)
        self.out_shape = []
        for a, m in items:
            sh = {"ag": (NDEV,) + a.shape, "ag2": (NDEV,) + a.shape, "pair": a.shape[:1] + a.shape[2:]}.get(m, a.shape)
            self.out_shape.append(jax.ShapeDtypeStruct(sh, a.dtype))
        self.scratch = [pltpu.SemaphoreType.DMA((self.n, NDEV - 1)), pltpu.SemaphoreType.DMA((self.n, NDEV - 1)),
                        pltpu.SemaphoreType.DMA((self.n,))]

    def _plan(self, srcs, outs, sems):
        send_sems, recv_sems, loc_sems = sems
        x, y, c = lax.axis_index("x"), lax.axis_index("y"), lax.axis_index("c")
        me, my_chip = 4 * x + 2 * y + c, 2 * x + y
        sib = (x, y, 1 - c)
        local, first, landed, forwards, arrivals = [], [], [], [], []

        def remote(a, k, src, dst, to):
            return pltpu.make_async_remote_copy(src_ref=src, dst_ref=dst, send_sem=send_sems.at[a, k],
                                                recv_sem=recv_sems.at[a, k], device_id=to, device_id_type=MESH)

        for a, mode in enumerate(self.modes):
            src, out = srcs[a], outs[a]
            if mode in ("ag", "a2a"):
                piece = (lambda slot, src=src: src) if mode == "ag" else (lambda slot, src=src: src.at[slot])
                local.append(pltpu.make_async_copy(piece(me), out.at[me], loc_sems.at[a]))
                for r in range(1, NDEV):
                    px = 1 - x if (r >> 2) & 1 else x
                    py = 1 - y if (r >> 1) & 1 else y
                    pc = 1 - c if r & 1 else c
                    pidx = 4 * px + 2 * py + pc
                    first.append(remote(a, r - 1, piece(pidx), out.at[me], (px, py, pc)))
                    arrivals.append(remote(a, r - 1, piece(pidx), out.at[pidx], (px, py, pc)))
            elif mode == "ag2":
                local.append(pltpu.make_async_copy(src, out.at[me], loc_sems.at[a]))
                first.append(remote(a, 0, src, out.at[me], sib))
                arrivals.append(remote(a, 0, src, out.at[me + 1 - 2 * c], sib))
                for j, (px, py) in enumerate([(1 - x, y), (x, 1 - y), (1 - x, 1 - y)]):
                    theirs = out.at[4 * px + 2 * py + c]
                    first.append(remote(a, 1 + j, src, out.at[me], (px, py, c)))
                    landed.append(remote(a, 1 + j, src, theirs, (px, py, c)))
                    forwards.append(remote(a, 4 + j, theirs, theirs, sib))
                    arrivals.append(remote(a, 4 + j, src, out.at[4 * px + 2 * py + 1 - c], sib))
            elif mode == "pair":
                for q in range(NDEV // 2):
                    first.append(remote(a, q, src.at[q, 1 - c], out.at[q], sib))
                    arrivals.append(remote(a, q, src.at[q, 1 - c], out.at[q], sib))
            else:
                assert mode == "chips", mode
                local.append(pltpu.make_async_copy(src.at[my_chip], out.at[my_chip], loc_sems.at[a]))
                for j, (px, py) in enumerate([(1 - x, y), (x, 1 - y), (1 - x, 1 - y)]):
                    q = 2 * px + py
                    first.append(remote(a, 1 + j, src.at[q], out.at[my_chip], (px, py, c)))
                    arrivals.append(remote(a, 1 + j, src.at[q], out.at[q], (px, py, c)))
        return local, first, landed, forwards, arrivals

    def start(self, srcs, outs, sems):
        local, first, _, _, _ = self._plan(srcs, outs, sems)
        for cp in local + first:
            cp.start()

    def wait(self, srcs, outs, sems):
        local, first, landed, forwards, arrivals = self._plan(srcs, outs, sems)
        for cp, fwd in zip(landed, forwards):
            cp.wait_recv()
            fwd.start()
        for cp in arrivals:
            cp.wait_recv()
        for cp in first + forwards:
            cp.wait_send()
        for cp in local:
            cp.wait()


def _exchange(items, name):
    ex = _Exchange(items)
    n = ex.n

    def body(*refs):
        srcs, outs, sems = refs[:n], refs[n:2 * n], refs[2 * n:]
        ex.start(srcs, outs, sems)
        ex.wait(srcs, outs, sems)

    return pl.pallas_call(
        body, name=name,
        out_shape=tuple(ex.out_shape),
        in_specs=[ANY] * n, out_specs=tuple([ANY] * n),
        scratch_shapes=ex.scratch,
        compiler_params=pltpu.CompilerParams(has_side_effects=True),
    )(*ex.arrays)


def _sequencer_exchange(items, name, collective_id, all_peers=False):
    ex = _Exchange(items)
    srcs = [jax.new_ref(a, memory_space=pltpu.MemorySpace.HBM) for a in ex.arrays]
    outs = [jax.empty_ref(sh, memory_space=pltpu.MemorySpace.HBM) for sh in ex.out_shape]

    @pl.kernel(mesh=plsc.ScalarSubcoreMesh(axis_name="sequencer", num_cores=1), name=name,
               scratch_types=tuple(ex.scratch), compiler_params=pltpu.CompilerParams(collective_id=collective_id))
    def launch(send_sems, recv_sems, loc_sems):
        x, y, c = lax.axis_index("x"), lax.axis_index("y"), lax.axis_index("c")
        barrier = pltpu.get_barrier_semaphore()
        peers = [(x, y, 1 - c), (1 - x, y, c), (x, 1 - y, c), (1 - x, 1 - y, c)]
        if all_peers:
            peers += [(1 - x, y, 1 - c), (x, 1 - y, 1 - c), (1 - x, 1 - y, 1 - c)]
        for peer in peers:
            pl.semaphore_signal(barrier, inc=1, device_id=peer, device_id_type=MESH)
        pl.semaphore_wait(barrier, len(peers))
        sems = (send_sems, recv_sems, loc_sems)
        ex.start(srcs, outs, sems)
        ex.wait(srcs, outs, sems)

    launch()
    return [o[...] for o in outs]


def _call(body, inputs, *, name, grid, in_specs, out_specs, out_shape, scratch_shapes=(), vmem_mb=None, hosted=None):
    out_specs, out_shape, scratch_shapes = tuple(out_specs), tuple(out_shape), list(scratch_shapes)
    if not hosted:
        res = pl.pallas_call(
            body, name=name, grid=grid, in_specs=list(in_specs), out_specs=out_specs, out_shape=out_shape,
            scratch_shapes=scratch_shapes, compiler_params=_params(("arbitrary",) * len(grid), vmem_mb),
        )(*inputs)
        return tuple(res), ()
    ex = _Exchange(hosted)
    n, n_in, n_out, n_scr = ex.n, len(inputs), len(out_shape), len(scratch_shapes)

    def hosting_body(*refs):
        ins, srcs = refs[:n_in], refs[n_in:n_in + n]
        outs, landing = refs[n_in + n:n_in + n + n_out], refs[n_in + n + n_out:n_in + 2 * n + n_out]
        scratch, sems = refs[n_in + 2 * n + n_out:n_in + 2 * n + n_out + n_scr], refs[n_in + 2 * n + n_out + n_scr:]
        first = functools.reduce(jnp.logical_and, [pl.program_id(d) == 0 for d in range(len(grid))])
        last = functools.reduce(jnp.logical_and, [pl.program_id(d) == grid[d] - 1 for d in range(len(grid))])

        @pl.when(first)
        def _():
            ex.start(srcs, landing, sems)

        body(*ins, *outs, *scratch)

        @pl.when(last)
        def _():
            ex.wait(srcs, landing, sems)

    res = pl.pallas_call(
        hosting_body, name=name, grid=grid,
        in_specs=list(in_specs) + [ANY] * n, out_specs=out_specs + tuple([ANY] * n),
        out_shape=out_shape + tuple(ex.out_shape), scratch_shapes=scratch_shapes + ex.scratch,
        compiler_params=_params(("arbitrary",) * len(grid), vmem_mb),
    )(*inputs, *ex.arrays)
    return tuple(res[:n_out]), tuple(res[n_out:])


def _mm(a, b, mode, out_dtype, tm, tn, name, hosted=None, vmem_mb=24):
    if mode == "nn":
        (m, k), n = a.shape, b.shape[1]
        a_spec = pl.BlockSpec((tm, k), lambda i, j: (i, 0))
        b_spec = pl.BlockSpec((k, tn), lambda i, j: (0, j))
        dims = NN
    elif mode == "nt":
        (m, k), n = a.shape, b.shape[0]
        a_spec = pl.BlockSpec((tm, k), lambda i, j: (i, 0))
        b_spec = pl.BlockSpec((tn, k), lambda i, j: (j, 0))
        dims = NT
    else:
        (k, m), n = a.shape, b.shape[1]
        a_spec = pl.BlockSpec((k, tm), lambda i, j: (0, i))
        b_spec = pl.BlockSpec((k, tn), lambda i, j: (0, j))
        dims = TN
    assert m % tm == 0 and n % tn == 0, (m, n, tm, tn)

    def body(a_ref, b_ref, o_ref):
        o_ref[...] = _dot(a_ref[...], b_ref[...], dims).astype(o_ref.dtype)

    (out,), moved = _call(
        body, (a, b), name=name, grid=(m // tm, n // tn),
        in_specs=[a_spec, b_spec], out_specs=[pl.BlockSpec((tm, tn), lambda i, j: (i, j))],
        out_shape=[jax.ShapeDtypeStruct((m, n), out_dtype)], vmem_mb=vmem_mb, hosted=hosted)
    return (out, moved) if hosted else out


def _shift_down(x, k, fill):
    y = pltpu.roll(x, k, 0)
    row = lax.broadcasted_iota(jnp.int32, (SUB, x.shape[1]), 0)
    head = y[0:SUB, :]
    for t in range(k):
        head = jnp.where(row == t, fill[t], head)
    return jnp.concatenate([head, y[SUB:, :]], axis=0)


def _shift_up(x, k, fill):
    n = x.shape[0]
    y = pltpu.roll(x, n - k, 0)
    row = lax.broadcasted_iota(jnp.int32, (SUB, x.shape[1]), 0)
    tail = y[n - SUB:, :]
    for t in range(k):
        tail = jnp.where(row == SUB - k + t, fill[t], tail)
    return jnp.concatenate([y[:n - SUB, :], tail], axis=0)


def _conv_taps(x, halo, w):
    if halo is None:
        f1, f2 = [0.0], [0.0, 0.0]
    else:
        f1, f2 = [halo[7:8, :]], [halo[6:7, :], halo[7:8, :]]
    s1 = _shift_down(x, 1, f1)
    s2 = _shift_down(x, 2, f2)
    u = w[2:3, :] * x + w[1:2, :] * s1 + w[0:1, :] * s2
    return u, s1, s2


def _conv_taps_t(du, nxt, w):
    if nxt is None:
        f1, f2 = [0.0], [0.0, 0.0]
    else:
        f1, f2 = [nxt[0:1, :]], [nxt[0:1, :], nxt[1:2, :]]
    return w[2:3, :] * du + w[1:2, :] * _shift_up(du, 1, f1) + w[0:1, :] * _shift_up(du, 2, f2)


def _ada_fwd(c_all, w_ada, b_my):
    def body(c_ref, w_ref, b_ref, o_ref):
        cv = c_ref[...]
        act = cv * _sigmoid(cv)
        o_ref[...] = _dot(act, w_ref[...], NN, lax.Precision.HIGHEST) + b_ref[...]

    out = jax.ShapeDtypeStruct((NDEV, w_ada.shape[1]), F32)
    return pl.pallas_call(
        body, name="ada_fwd", grid=(1,),
        in_specs=[_full_spec(c_all.shape), _full_spec(w_ada.shape), _full_spec(b_my.shape)],
        out_specs=_full_spec(out.shape), out_shape=out, compiler_params=_params(("arbitrary",), 32),
    )(c_all, w_ada, b_my)


TR = 256
TRE = 512


def _row_spec(width, col=0, rows=TR):
    return pl.BlockSpec((rows, width), lambda i, col=col: (i, col))


def _erow(width):
    return _row_spec(width, rows=TRE)


def _full_spec(shape):
    return pl.BlockSpec(shape, lambda i: (0,) * len(shape))


def _norm_mod_fwd(x, mod, g):
    s = x.shape[0]

    def body(x_ref, mod_ref, g_ref, h_ref):
        xv = x_ref[...]
        r = lax.rsqrt(jnp.mean(xv * xv, axis=-1, keepdims=True) + EPS)
        nrm = xv * r * g_ref[...]
        h_ref[...] = (nrm * (1.0 + mod_ref[1:2, :]) + mod_ref[0:1, :]).astype(BF16)

    return pl.pallas_call(
        body, name="norm1_fwd", grid=(s // TRE,),
        in_specs=[_erow(D), _full_spec((SUB, D)), _full_spec((1, D))],
        out_specs=_erow(D), out_shape=jax.ShapeDtypeStruct((s, D), BF16),
        compiler_params=_params(("parallel",), 16),
    )(x, mod, g)


SLAB = 2 * DH
AUG_F, AUG_ONE, AUG_LSE = 0, 3, 6


def _split3(x):
    hi = x.astype(BF16).astype(F32)
    r1 = x - hi
    mid = r1.astype(BF16).astype(F32)
    return hi, mid, r1 - mid


def _lanes3(lane, first, pieces, other):
    out = other
    for k in range(3):
        out = jnp.where(lane == first + k, pieces[k], out)
    return out


def _aug_placement():
    eq = np.zeros((3 * LANES, HEADS * SLAB), np.float32)
    ek = np.zeros((3 * LANES, HEADS * SLAB), np.float32)
    ones = np.zeros((SUB, HEADS * SLAB), np.float32)
    for h in range(HEADS):
        aug = SLAB * h + DH
        for k in range(3):
            eq[LANES * k + h, aug + AUG_F + k] = 1.0
            ek[LANES * k + h, aug + AUG_ONE + k] = -1.0
            ones[0, aug + AUG_ONE + k] = 1.0
            ones[1, aug + AUG_F + k] = ones[1, aug + AUG_LSE + k] = 1.0
            ones[2, aug + k] = 1.0
    return jnp.asarray(eq, BF16), jnp.asarray(ek, BF16), jnp.asarray(ones)


FG_BLOCK = (3 * AW + 3 * CW) // LANES


def _qkv_prep(proj, bf_pad, gq, gk, cw_mix):
    s = proj.shape[0]

    def body(q_ref, k_ref, v_ref, fg_ref, b_ref, gq_ref, gk_ref, eq_ref, ek_ref, ones_ref, xin_ref, bg_ref, cg_ref,
             cw_ref, qo_ref, ko_ref, vo_ref, conv_ref, carry_ref, halo_ref):
        first = pl.program_id(0) == 0

        @pl.when(first)
        def _():
            carry_ref[...] = jnp.zeros_like(carry_ref)

        cx = cg_ref[...] * xin_ref[...]
        cv, _, _ = _conv_taps(cx, jnp.where(first, 0.0, halo_ref[...]), cw_ref[...])
        conv_ref[...] = bg_ref[...] * cv
        halo_ref[...] = cx[TR - SUB:TR, :]

        z = fg_ref[...] + b_ref[...]
        logf = jnp.minimum(z, 0.0) - jnp.log1p(jnp.exp(-jnp.abs(z)))
        row = lax.broadcasted_iota(jnp.int32, (TR, TR), 0)
        col = lax.broadcasted_iota(jnp.int32, (TR, TR), 1)
        fcum = _dot((col <= row).astype(F32), logf, NN, lax.Precision.HIGHEST) + carry_ref[0:1, :]
        carry_ref[...] = jnp.broadcast_to(fcum[TR - 1:TR, :], carry_ref.shape)
        f3 = jnp.concatenate(_split3(fcum), axis=1).astype(BF16)
        qo_ref[...] = (_dot(f3, eq_ref[...]) + ones_ref[0:1, :]).astype(BF16)
        ko_ref[...] = (_dot(f3, ek_ref[...]) + ones_ref[1:2, :]).astype(BF16)
        vo_ref[...] = jnp.broadcast_to(ones_ref[2:3, :], vo_ref.shape).astype(BF16)
        for h in range(HEADS):
            sl = slice(DH * h, DH * (h + 1))
            lo = slice(SLAB * h, SLAB * h + DH)
            qh = q_ref[:, sl]
            r = lax.rsqrt(jnp.mean(qh * qh, axis=-1, keepdims=True) + EPS)
            qo_ref[:, lo] = (qh * r * gq_ref[...] * QK_SCALE).astype(BF16)
            kh = k_ref[:, sl]
            r = lax.rsqrt(jnp.mean(kh * kh, axis=-1, keepdims=True) + EPS)
            ko_ref[:, lo] = (kh * r * gk_ref[...]).astype(BF16)
            vo_ref[:, lo] = v_ref[:, sl].astype(BF16)

    eq, ek, ones = _aug_placement()
    o = jax.ShapeDtypeStruct((s, HEADS * SLAB), BF16)
    wide = _row_spec(HEADS * SLAB)
    outs, _ = _call(
        body, (proj, proj, proj, proj, bf_pad, gq, gk, eq, ek, ones, proj, proj, proj, cw_mix), name="qkv_prep",
        grid=(s // TR,),
        in_specs=[_row_spec(AW, 0), _row_spec(AW, 1), _row_spec(AW, 2), _row_spec(LANES, FG_BLOCK),
                  _full_spec((1, LANES)), _full_spec((1, DH)), _full_spec((1, DH)), _full_spec(eq.shape),
                  _full_spec(ek.shape), _full_spec(ones.shape),
                  _row_spec(CW, 3), _row_spec(CW, 4), _row_spec(CW, 5), _full_spec((3, CW))],
        out_specs=[wide, wide, wide, _row_spec(CW)],
        out_shape=[o, o, o, jax.ShapeDtypeStruct((s, CW), F32)],
        scratch_shapes=[pltpu.VMEM((SUB, LANES), F32), pltpu.VMEM((SUB, CW), F32)], vmem_mb=24)
    return outs


def _resid_norm2(x, z, mod, g):
    s = x.shape[0]

    def body(x_ref, z_ref, mod_ref, g_ref, x1_ref, h_ref):
        x1 = x_ref[...] + mod_ref[2:3, :] * z_ref[...]
        x1_ref[...] = x1
        r = lax.rsqrt(jnp.mean(x1 * x1, axis=-1, keepdims=True) + EPS)
        nrm = x1 * r * g_ref[...]
        h_ref[...] = (nrm * (1.0 + mod_ref[4:5, :]) + mod_ref[3:4, :]).astype(BF16)

    return pl.pallas_call(
        body, name="resid_norm2", grid=(s // TRE,),
        in_specs=[_erow(D), _erow(D), _full_spec((SUB, D)), _full_spec((1, D))],
        out_specs=(_erow(D), _erow(D)),
        out_shape=(jax.ShapeDtypeStruct((s, D), F32), jax.ShapeDtypeStruct((s, D), BF16)),
        compiler_params=_params(("parallel",), 24),
    )(x, z, mod, g)


def _loss_head(x1, y, tgt, mod):
    s = x1.shape[0]

    def body(x1_ref, y_ref, t_ref, mod_ref, dout_ref, dy_ref, vec_ref):
        @pl.when(pl.program_id(0) == 0)
        def _():
            vec_ref[...] = jnp.zeros_like(vec_ref)

        yv = y_ref[...]
        g2 = mod_ref[5:6, :]
        diff = x1_ref[...] + g2 * yv - t_ref[...]
        dout = diff * (1.0 / D)
        dout_ref[...] = dout
        dy_ref[...] = (g2 * dout).astype(BF16)
        vec_ref[0:1, :] += jnp.sum(dout * yv, axis=0, keepdims=True)
        vec_ref[1:2, :] += jnp.sum(diff * diff, axis=0, keepdims=True)

    return pl.pallas_call(
        body, name="loss_head", grid=(s // TRE,),
        in_specs=[_erow(D), _erow(D), _erow(D), _full_spec((SUB, D))],
        out_specs=(_erow(D), _erow(D), _full_spec((SUB, D))),
        out_shape=(jax.ShapeDtypeStruct((s, D), F32), jax.ShapeDtypeStruct((s, D), BF16),
                   jax.ShapeDtypeStruct((SUB, D), F32)),
        compiler_params=_params(("arbitrary",), 24),
    )(x1, y, tgt, mod)


def _norm_mod_bwd(dh, xin, dres, zin, mod, g, scale_row, gate_row, name, hosted=None):
    s = dh.shape[0]
    with_gate = gate_row is not None

    def body(*refs):
        if with_gate:
            dh_ref, x_ref, dres_ref, z_ref, mod_ref, g_ref, dx_ref, dz_ref, vec_ref = refs
        else:
            dh_ref, x_ref, dres_ref, mod_ref, g_ref, dx_ref, vec_ref = refs

        @pl.when(pl.program_id(0) == 0)
        def _():
            vec_ref[...] = jnp.zeros_like(vec_ref)

        xv = x_ref[...]
        dhv = dh_ref[...]
        gv = g_ref[...]
        r = lax.rsqrt(jnp.mean(xv * xv, axis=-1, keepdims=True) + EPS)
        xh = xv * r
        dn = dhv * (1.0 + mod_ref[scale_row:scale_row + 1, :])
        dxh = dn * gv
        dx = dres_ref[...] + r * (dxh - xh * jnp.mean(dxh * xh, axis=-1, keepdims=True))
        dx_ref[...] = dx
        vec_ref[0:1, :] += jnp.sum(dhv, axis=0, keepdims=True)
        vec_ref[1:2, :] += jnp.sum(dhv * (xh * gv), axis=0, keepdims=True)
        vec_ref[2:3, :] += jnp.sum(dn * xh, axis=0, keepdims=True)
        if with_gate:
            dz_ref[...] = (mod_ref[gate_row:gate_row + 1, :] * dx).astype(BF16)
            vec_ref[3:4, :] += jnp.sum(dx * z_ref[...], axis=0, keepdims=True)

    ins = [dh, xin, dres] + ([zin] if with_gate else []) + [mod, g]
    in_specs = [_erow(D)] * (4 if with_gate else 3) + [_full_spec((SUB, D)), _full_spec((1, D))]
    out_specs = [_erow(D)] + ([_erow(D)] if with_gate else []) + [_full_spec((SUB, D))]
    out_shape = [jax.ShapeDtypeStruct((s, D), F32)] + ([jax.ShapeDtypeStruct((s, D), BF16)] if with_gate else []) \
        + [jax.ShapeDtypeStruct((SUB, D), F32)]
    outs, moved = _call(body, ins, name=name, grid=(s // TRE,), in_specs=in_specs, out_specs=out_specs,
                        out_shape=out_shape, vmem_mb=32, hosted=hosted)
    return outs + (moved,) if hosted else outs


TA = 512
NEG = -1e30


def _causal_mask():
    row = lax.broadcasted_iota(jnp.int32, (TA, TA), 0)
    col = lax.broadcasted_iota(jnp.int32, (TA, TA), 1)
    return col <= row


def _attn_fwd(qp, kp, vp):
    s = qp.shape[0]
    nq = s // TA

    def body(q_ref, k_ref, v_ref, o_ref, lse_ref):
        i = pl.program_id(1)
        slabs = [slice(SLAB * hh, SLAB * (hh + 1)) for hh in range(2)]
        q = [q_ref[:, sl] for sl in slabs]

        def block(j, carry, masked):
            keys = pl.ds(pl.multiple_of(j * TA, TA), TA)
            ms, acc = carry
            m_out, parts = [], []
            for hh in range(2):
                sc = _dot(q[hh], k_ref[keys, slabs[hh]], NT)
                if masked:
                    sc = jnp.where(_causal_mask(), sc, NEG)
                m_new = jnp.maximum(ms[hh], jnp.max(sc, axis=-1, keepdims=True))
                p = jnp.exp(sc - m_new)
                parts.append(jnp.exp(ms[hh] - m_new) * acc[:, slabs[hh]] + _dot(p.astype(BF16), v_ref[keys, slabs[hh]]))
                m_out.append(m_new)
            return tuple(m_out), jnp.concatenate(parts, axis=1)

        init = ((jnp.full((TA, 1), NEG, F32), jnp.full((TA, 1), NEG, F32)), jnp.zeros((TA, 2 * SLAB), F32))
        carry = lax.fori_loop(0, i, lambda j, cr: block(j, cr, False), init)
        ms, acc = block(i, carry, True)
        for hh in range(2):
            l = acc[:, SLAB * hh + DH:SLAB * hh + DH + 1]
            o_ref[:, DH * hh:DH * (hh + 1)] = acc[:, SLAB * hh:SLAB * hh + DH] / l
            lse_ref[0, :, hh:hh + 1] = ms[hh] + jnp.log(l)

    (o, lse), _ = _call(
        body, (qp, kp, vp), name="attn_fwd", grid=(HEADS // 2, nq),
        in_specs=[pl.BlockSpec((TA, 2 * SLAB), lambda p, i: (i, p)),
                  pl.BlockSpec((s, 2 * SLAB), lambda p, i: (0, p)),
                  pl.BlockSpec((s, 2 * SLAB), lambda p, i: (0, p))],
        out_specs=[pl.BlockSpec((TA, LANES), lambda p, i: (i, p)), pl.BlockSpec((1, TA, 2), lambda p, i: (p, i, 0))],
        out_shape=[jax.ShapeDtypeStruct((s, AW), F32), jax.ShapeDtypeStruct((HEADS // 2, s, 2), F32)],
        vmem_mb=24)
    return o, lse


def _attn_bwd(qp, kp, vp, dmixed, o, lse, hosted):
    s = qp.shape[0]
    nq = s // TA

    def body(q_ref, k_ref, v_ref, do_ref, o_ref, lse_ref, dq_ref, dk_ref, dv_ref, qb_ref, dob_ref):
        dk_ref[...] = jnp.zeros_like(dk_ref)
        dv_ref[...] = jnp.zeros_like(dv_ref)
        slabs = [slice(SLAB * hh, SLAB * (hh + 1)) for hh in range(2)]
        lane = lax.broadcasted_iota(jnp.int32, (TA, DH), 1)

        def q_block(i, _):
            i0 = pl.multiple_of(i * TA, TA)
            rows = pl.ds(i0, TA)
            for hh in range(2):
                half = slice(DH * hh, DH * (hh + 1))
                do = do_ref[rows, half]
                delta = jnp.sum(do * o_ref[rows, half], axis=-1, keepdims=True)
                dob_ref[hh, :, 0:DH] = do.astype(BF16)
                dob_ref[hh, :, DH:SLAB] = _lanes3(lane, 0, [-d for d in _split3(delta)], 0.0).astype(BF16)
                lse3 = _split3(lse_ref[0, rows, hh:hh + 1])
                qb_ref[hh, :, 0:DH] = q_ref[rows, SLAB * hh:SLAB * hh + DH]
                aug = q_ref[rows, SLAB * hh + DH:SLAB * (hh + 1)].astype(F32)
                qb_ref[hh, :, DH:SLAB] = _lanes3(lane, AUG_LSE, [-x for x in lse3], aug).astype(BF16)

            def block(j, dq, masked):
                keys = pl.ds(pl.multiple_of(j * TA, TA), TA)
                dv, dk, dqc = [], [], []
                for hh in range(2):
                    q, dob = qb_ref[hh], dob_ref[hh]
                    k = k_ref[keys, slabs[hh]]
                    sc = _dot(q, k, NT)
                    if masked:
                        sc = jnp.where(_causal_mask(), sc, NEG)
                    p = jnp.exp(sc)
                    dv.append(_dot(p.astype(BF16), dob, TN))
                    ds = (p * _dot(dob, v_ref[keys, slabs[hh]], NT)).astype(BF16)
                    dk.append(_dot(ds, q, TN))
                    dqc.append(_dot(ds, k))
                dv_ref[keys, :] += jnp.concatenate(dv, axis=1)
                dk_ref[keys, :] += jnp.concatenate(dk, axis=1)
                return dq + jnp.concatenate(dqc, axis=1)

            dq = lax.fori_loop(0, i, lambda j, acc: block(j, acc, False), jnp.zeros((TA, 2 * SLAB), F32))
            dq_ref[rows, :] = block(i, dq, True)
            return 0

        lax.fori_loop(0, nq, q_block, 0)

    pair = lambda p: (0, p)
    slab2 = pl.BlockSpec((s, 2 * SLAB), pair)
    seq = pl.BlockSpec((s, LANES), pair)
    small = pl.BlockSpec((1, s, 2), lambda p: (p, 0, 0))
    o32 = jax.ShapeDtypeStruct((s, HEADS * SLAB), F32)
    return _call(
        body, (qp, kp, vp, dmixed, o, lse), name="attn_bwd", grid=(HEADS // 2,),
        in_specs=[slab2, slab2, slab2, seq, seq, small], out_specs=[slab2, slab2, slab2], out_shape=[o32, o32, o32],
        scratch_shapes=[pltpu.VMEM((2, TA, SLAB), BF16), pltpu.VMEM((2, TA, SLAB), BF16)], vmem_mb=40, hosted=hosted)


def _qkv_post(dqp, dkp, dvp, dmixed, proj, bf_pad, gq, gk, cw_mix):
    s = proj.shape[0]
    nb = s // TR

    def body(dq_ref, dk_ref, dv_ref, q_ref, k_ref, fg_ref, b_ref, gq_ref, gk_ref, dc_ref, xin_ref, bg_ref, cg_ref,
             xinh_ref, cgh_ref, cw_ref, dqo_ref, dko_ref, dvo_ref, dfg_ref, vec_ref, dxin_ref, dbg_ref, dcg_ref, dcw_ref,
             carry_ref, nxt_ref):
        i = pl.program_id(0)

        @pl.when(i == 0)
        def _():
            vec_ref[...] = jnp.zeros_like(vec_ref)
            carry_ref[...] = jnp.zeros_like(carry_ref)
            dcw_ref[...] = jnp.zeros_like(dcw_ref)

        wv = cw_ref[...]
        xin, cg, dconv = xin_ref[...], cg_ref[...], dc_ref[...]
        cx = cg * xin
        cv, s1, s2 = _conv_taps(cx, jnp.where(i < nb - 1, cgh_ref[...] * xinh_ref[...], 0.0), wv)
        dbg_ref[...] = (dconv * cv).astype(BF16)
        dcv = dconv * bg_ref[...]
        dcw_ref[0:1, :] += jnp.sum(dcv * s2, axis=0, keepdims=True)
        dcw_ref[1:2, :] += jnp.sum(dcv * s1, axis=0, keepdims=True)
        dcw_ref[2:3, :] += jnp.sum(dcv * cx, axis=0, keepdims=True)
        dcx = _conv_taps_t(dcv, jnp.where(i > 0, nxt_ref[...], 0.0), wv)
        nxt_ref[...] = dcv[0:SUB, :]
        dcg_ref[...] = (dcx * xin).astype(BF16)
        dxin_ref[...] = (dcx * cg).astype(BF16)

        def one(d_ref, x_ref, g_ref, o_ref, row, scale):
            dg = jnp.zeros((1, DH), F32)
            for h in range(HEADS):
                sl = slice(DH * h, DH * (h + 1))
                xv = x_ref[:, sl]
                r = lax.rsqrt(jnp.mean(xv * xv, axis=-1, keepdims=True) + EPS)
                xh = xv * r
                dn = d_ref[:, SLAB * h:SLAB * h + DH] * scale
                dg = dg + jnp.sum(dn * xh, axis=0, keepdims=True)
                dxh = dn * g_ref[...]
                o_ref[:, sl] = (r * (dxh - xh * jnp.mean(dxh * xh, axis=-1, keepdims=True))).astype(BF16)
            vec_ref[row:row + 1, 0:DH] += dg

        one(dq_ref, q_ref, gq_ref, dqo_ref, 0, QK_SCALE)
        one(dk_ref, k_ref, gk_ref, dko_ref, 1, 1.0)
        lane = lax.broadcasted_iota(jnp.int32, (TR, LANES), 1)
        df = jnp.zeros((TR, LANES), F32)
        for h in range(HEADS):
            dvo_ref[:, DH * h:DH * (h + 1)] = dv_ref[:, SLAB * h:SLAB * h + DH].astype(BF16)
            row_sum = dq_ref[:, SLAB * h + DH:SLAB * h + DH + 1]
            col_sum = dk_ref[:, SLAB * h + DH + AUG_ONE:SLAB * h + DH + AUG_ONE + 1]
            df = jnp.where(lane == h, row_sum - col_sum, df)
        row = lax.broadcasted_iota(jnp.int32, (TR, TR), 0)
        col = lax.broadcasted_iota(jnp.int32, (TR, TR), 1)
        dlogf = _dot((col >= row).astype(F32), df, NN, lax.Precision.HIGHEST) + carry_ref[0:1, :]
        carry_ref[...] = jnp.broadcast_to(dlogf[0:1, :], carry_ref.shape)
        dfg = dlogf * _sigmoid(-(fg_ref[...] + b_ref[...]))
        dfg_ref[...] = dfg.astype(BF16)
        vec_ref[2:3, :] += jnp.sum(dfg, axis=0, keepdims=True)

    o = jax.ShapeDtypeStruct((s, AW), BF16)
    rev = lambda width, col=0: pl.BlockSpec((TR, width), lambda i, col=col: (nb - 1 - i, col))
    wide = rev(HEADS * SLAB)
    halo = lambda col: pl.BlockSpec((SUB, CW), lambda i, col=col: (jnp.maximum((nb - 1 - i) * (TR // SUB) - 1, 0), col))
    outs, _ = _call(
        body, (dqp, dkp, dvp, proj, proj, proj, bf_pad, gq, gk, dmixed, proj, proj, proj, proj, proj, cw_mix),
        name="qkv_post", grid=(nb,),
        in_specs=[wide, wide, wide, rev(AW, 0), rev(AW, 1), rev(LANES, FG_BLOCK), _full_spec((1, LANES)),
                  _full_spec((1, DH)), _full_spec((1, DH)),
                  rev(CW, 1), rev(CW, 3), rev(CW, 4), rev(CW, 5), halo(3), halo(5), _full_spec((3, CW))],
        out_specs=[rev(AW), rev(AW), rev(AW), rev(LANES), _full_spec((SUB, LANES)),
                   rev(CW), rev(CW), rev(CW), _full_spec((SUB, CW))],
        out_shape=[o, o, o, jax.ShapeDtypeStruct((s, LANES), BF16), jax.ShapeDtypeStruct((SUB, LANES), F32),
                   o, o, o, jax.ShapeDtypeStruct((SUB, CW), F32)],
        scratch_shapes=[pltpu.VMEM((SUB, LANES), F32), pltpu.VMEM((SUB, CW), F32)], vmem_mb=32)
    return outs


TF = 256
NJ = DFF // TF
FFN_ROWS_FWD = 1024
FFN_ROWS_BWD = 1024


def _ffn_fwd(h2, wup_t, cw, wd):
    s = h2.shape[0]
    tr = FFN_ROWS_FWD
    nr = s // tr

    def body(h_ref, wu_ref, cg_ref, cv_ref, wd_ref, pg_ref, pv_ref, y_ref, halo_ref, act_ref):
        r, j = pl.program_id(0), pl.program_id(1)
        hv = h_ref[...]
        pg = _dot(hv, wu_ref[0], NT).astype(BF16)
        pv = _dot(hv, wu_ref[1], NT).astype(BF16)
        pg_ref[...] = pg
        pv_ref[...] = pv
        pgf, pvf = pg.astype(F32), pv.astype(F32)
        ug, _, _ = _conv_taps(pgf, jnp.where(r > 0, halo_ref[j, 0], 0.0), cg_ref[...])
        uv, _, _ = _conv_taps(pvf, jnp.where(r > 0, halo_ref[j, 1], 0.0), cv_ref[...])
        halo_ref[j, 0] = pgf[tr - SUB:tr, :]
        halo_ref[j, 1] = pvf[tr - SUB:tr, :]
        act = (ug * _sigmoid(ug) * uv).astype(BF16)
        for t in range(NJ):
            @pl.when(j == t)
            def _(t=t):
                act_ref[:, t * TF:(t + 1) * TF] = act

        @pl.when(j == NJ - 1)
        def _():
            y_ref[...] = _dot(act_ref[...], wd_ref[...])

    pre = jax.ShapeDtypeStruct((s, DFF), BF16)
    return pl.pallas_call(
        body, name="ffn_fwd", grid=(nr, NJ),
        in_specs=[pl.BlockSpec((tr, D), lambda r, j: (r, 0)),
                  pl.BlockSpec((2, TF, D), lambda r, j: (0, j, 0)),
                  pl.BlockSpec((3, TF), lambda r, j: (0, j)),
                  pl.BlockSpec((3, TF), lambda r, j: (0, NJ + j)),
                  pl.BlockSpec((DFF, D), lambda r, j: (0, 0))],
        out_specs=(pl.BlockSpec((tr, TF), lambda r, j: (r, j)),
                   pl.BlockSpec((tr, TF), lambda r, j: (r, j)),
                   pl.BlockSpec((tr, D), lambda r, j: (r, 0))),
        out_shape=(pre, pre, jax.ShapeDtypeStruct((s, D), F32)),
        scratch_shapes=[pltpu.VMEM((NJ, 2, SUB, TF), F32), pltpu.VMEM((tr, DFF), BF16)],
        compiler_params=_params(("arbitrary", "arbitrary"), 56),
    )(h2, wup_t, cw, cw, wd)


def _ffn_bwd(dy, h2, pre_g, pre_v, wup_t, cw, wd):
    s = h2.shape[0]
    tr = FFN_ROWS_BWD
    nr = s // tr
    hb = tr // (2 * SUB)

    def body(dy_ref, h_ref, pg_ref, pv_ref, hg_ref, hv_ref, wu_ref, cg_ref, cv_ref, wd_ref,
             dh_ref, dwu_ref, dwd_ref, dcg_ref, dcv_ref, nxt_ref, awu_ref, awd_ref):
        j, r = pl.program_id(0), pl.program_id(1)
        rr = nr - 1 - r
        row0 = pl.multiple_of(rr * tr, tr)
        cwg, cwv = cg_ref[...], cv_ref[...]
        pg, pv = pg_ref[...].astype(F32), pv_ref[...].astype(F32)
        ug, g1, g2 = _conv_taps(pg, jnp.where(rr > 0, hg_ref[SUB:2 * SUB, :].astype(F32), 0.0), cwg)
        uv, v1, v2 = _conv_taps(pv, jnp.where(rr > 0, hv_ref[SUB:2 * SUB, :].astype(F32), 0.0), cwv)
        sg = _sigmoid(ug)
        sil = ug * sg
        act = (sil * uv).astype(BF16)
        dyv = dy_ref[...]
        da = _dot(dyv, wd_ref[...], NT)
        dug = da * uv * (sg * (1.0 + ug * (1.0 - sg)))
        duv = da * sil
        dpg = _conv_taps_t(dug, jnp.where(r > 0, nxt_ref[0], 0.0), cwg)
        dpv = _conv_taps_t(duv, jnp.where(r > 0, nxt_ref[1], 0.0), cwv)
        nxt_ref[0] = dug[0:SUB, :]
        nxt_ref[1] = duv[0:SUB, :]
        dpgb, dpvb = dpg.astype(BF16), dpv.astype(BF16)
        hv = h_ref[...]
        dwd = _dot(act, dyv, TN)
        dpb = jnp.concatenate([dpgb, dpvb], axis=1)
        dwu = _dot(dpb, hv, TN)
        dh = _dot(dpb, wu_ref[...].reshape(2 * TF, D))

        def taps(du, x0, x1, x2):
            return (jnp.sum(du * x2, axis=0, keepdims=True), jnp.sum(du * x1, axis=0, keepdims=True),
                    jnp.sum(du * x0, axis=0, keepdims=True))

        tg, tv = taps(dug, pg, g1, g2), taps(duv, pv, v1, v2)

        @pl.when(r == 0)
        def _():
            awd_ref[...] = dwd
            awu_ref[...] = dwu
            dcg_ref[...] = jnp.zeros_like(dcg_ref)
            dcv_ref[...] = jnp.zeros_like(dcv_ref)

        @pl.when(r > 0)
        def _():
            awd_ref[...] += dwd
            awu_ref[...] += dwu

        @pl.when(r == nr - 1)
        def _():
            dwd_ref[...] = awd_ref[...].astype(BF16)
            dwu_ref[...] = awu_ref[...].astype(BF16).reshape(2, TF, D)

        for t in range(3):
            dcg_ref[t:t + 1, :] += tg[t]
            dcv_ref[t:t + 1, :] += tv[t]

        @pl.when(j == 0)
        def _():
            dh_ref[pl.ds(row0, tr), :] = dh

        @pl.when(j > 0)
        def _():
            dh_ref[pl.ds(row0, tr), :] += dh

    rows = lambda j, r: (nr - 1 - r, 0)
    tile = lambda j, r: (nr - 1 - r, j)
    halo = lambda j, r: (jnp.maximum((nr - 1 - r) * hb - 1, 0), j)
    return pl.pallas_call(
        body, name="ffn_bwd", grid=(NJ, nr),
        in_specs=[pl.BlockSpec((tr, D), rows), pl.BlockSpec((tr, D), rows),
                  pl.BlockSpec((tr, TF), tile), pl.BlockSpec((tr, TF), tile),
                  pl.BlockSpec((2 * SUB, TF), halo), pl.BlockSpec((2 * SUB, TF), halo),
                  pl.BlockSpec((2, TF, D), lambda j, r: (0, j, 0)),
                  pl.BlockSpec((3, TF), lambda j, r: (0, j)), pl.BlockSpec((3, TF), lambda j, r: (0, NJ + j)),
                  pl.BlockSpec((TF, D), lambda j, r: (j, 0))],
        out_specs=(pl.BlockSpec((s, D), lambda j, r: (0, 0), pipeline_mode=pl.Buffered(1)),
                   pl.BlockSpec((2, TF, D), lambda j, r: (0, j, 0)),
                   pl.BlockSpec((TF, D), lambda j, r: (j, 0)),
                   pl.BlockSpec((SUB, TF), lambda j, r: (0, j)), pl.BlockSpec((SUB, TF), lambda j, r: (0, j))),
        out_shape=(jax.ShapeDtypeStruct((s, D), F32),
                   jax.ShapeDtypeStruct((2, DFF, D), BF16), jax.ShapeDtypeStruct((DFF, D), BF16),
                   jax.ShapeDtypeStruct((SUB, DFF), F32), jax.ShapeDtypeStruct((SUB, DFF), F32)),
        scratch_shapes=[pltpu.VMEM((2, SUB, TF), F32), pltpu.VMEM((2 * TF, D), F32), pltpu.VMEM((TF, D), F32)],
        compiler_params=_params(("arbitrary", "arbitrary"), 40),
    )(dy, h2, pre_g, pre_v, pre_g, pre_v, wup_t, cw, cw, wd)


def _adam(w, g, m, v):
    m = ADAM_B1 * m + (1.0 - ADAM_B1) * g
    v = ADAM_B2 * v + (1.0 - ADAM_B2) * (g * g)
    m_hat = m / (1.0 - ADAM_B1 ** ADAM_STEP)
    v_hat = v / (1.0 - ADAM_B2 ** ADAM_STEP)
    delta = -ADAM_LR * (m_hat / (jnp.sqrt(v_hat) + ADAM_EPS) + ADAM_WD * w)
    return delta, m, v


NCHIP = NDEV // 2


def _pair_add(mine, theirs, tr, name):
    _, _, rws, cols = mine.shape

    def body(a_ref, b_ref, o_ref):
        c = lax.axis_index("c")
        o_ref[0] = (a_ref[0, c].astype(F32) + b_ref[0].astype(F32)).astype(BF16)

    (out,), _ = _call(
        body, (mine, theirs), name=name, grid=(NCHIP, rws // tr),
        in_specs=[pl.BlockSpec((1, 2, tr, cols), lambda q, i: (q, 0, i, 0)),
                  pl.BlockSpec((1, tr, cols), lambda q, i: (q, i, 0))],
        out_specs=[pl.BlockSpec((1, tr, cols), lambda q, i: (q, i, 0))],
        out_shape=[jax.ShapeDtypeStruct((NCHIP, rws, cols), BF16)], vmem_mb=16)
    return out


def _adamw_sharded(parts, w, m, v, tr, name, hosted=None):
    rws, cols = w.shape
    n_parts = parts.shape[0]

    def body(p_ref, w_ref, m_ref, v_ref, g_ref, d_ref, mo_ref, vo_ref):
        g = p_ref[0].astype(F32)
        for q in range(1, n_parts):
            g = g + p_ref[q].astype(F32)
        g_ref[...] = g
        d_ref[...], mo_ref[...], vo_ref[...] = _adam(w_ref[...], g, m_ref[...], v_ref[...])

    blk = pl.BlockSpec((tr, cols), lambda i: (i, 0))
    o = jax.ShapeDtypeStruct((rws, cols), F32)
    outs, moved = _call(
        body, (parts, w, m, v), name=name, grid=(rws // tr,),
        in_specs=[pl.BlockSpec((n_parts, tr, cols), lambda i: (0, i, 0)), blk, blk, blk],
        out_specs=[blk, blk, blk, blk], out_shape=[o, o, o, o], vmem_mb=44 if tr > 256 else 24, hosted=hosted)
    return (outs, moved) if hosted else outs


def _adamw_ada(c_all, dmod_my, w, m, v):
    rws, cols = w.shape
    tr = 256

    def body(c_ref, dm_ref, w_ref, m_ref, v_ref, g_ref, d_ref, mo_ref, vo_ref):
        cv = c_ref[...]
        act = cv * _sigmoid(cv)
        g = _dot(act, dm_ref[...], TN, lax.Precision.HIGHEST)
        g_ref[...] = g
        d_ref[...], mo_ref[...], vo_ref[...] = _adam(w_ref[...], g, m_ref[...], v_ref[...])

    blk = pl.BlockSpec((tr, cols), lambda i: (i, 0))
    o = jax.ShapeDtypeStruct((rws, cols), F32)
    return pl.pallas_call(
        body, name="adamw_ada", grid=(rws // tr,),
        in_specs=[pl.BlockSpec((NDEV, tr), lambda i: (0, i)), _full_spec((NDEV, cols)), blk, blk, blk],
        out_specs=(blk, blk, blk, blk), out_shape=(o, o, o, o),
        compiler_params=_params(("parallel",), 32),
    )(c_all, dmod_my, w, m, v)


REP_ROWS = 16
ROW_N1, ROW_N2, ROW_LOSS, ROW_MISC = 6, 7, 8, 9
LANE_BF, LANE_GQ, LANE_GK = 0, 128, 256


def _adamw_small(rep_all, conv_all, wmv):
    n_ff = wmv[6][0].shape[1]

    def body(*refs):
        rep_ref, conv_ref = refs[:2]
        ins = refs[2:2 + 24]
        outs = refs[2 + 24:]
        loss_ref, outs = outs[0], outs[1:]
        g_rep = rep_ref[0]
        g_conv = conv_ref[0]
        for d in range(1, NDEV):
            g_rep = g_rep + rep_ref[d]
            g_conv = g_conv + conv_ref[d]
        loss_ref[...] = (0.5 / D) * jnp.sum(g_rep[ROW_LOSS:ROW_LOSS + 1, :], axis=-1, keepdims=True)
        grads = [
            None,
            g_rep[ROW_N1:ROW_N1 + 1, :],
            g_rep[ROW_MISC:ROW_MISC + 1, LANE_BF:LANE_BF + HEADS],
            g_rep[ROW_MISC:ROW_MISC + 1, LANE_GQ:LANE_GQ + DH],
            g_rep[ROW_MISC:ROW_MISC + 1, LANE_GK:LANE_GK + DH],
            g_rep[ROW_N2:ROW_N2 + 1, :],
            g_conv[0:3, 0:n_ff],
            g_conv[0:3, n_ff:n_ff + DH],
        ]
        for p in range(8):
            w_ref, m_ref, v_ref = ins[3 * p:3 * p + 3]
            g_ref, d_ref, mo_ref, vo_ref = outs[4 * p:4 * p + 4]
            if p == 0:
                for nmod in range(NMOD):
                    sl = slice(D * nmod, D * (nmod + 1))
                    g = g_rep[nmod:nmod + 1, :]
                    g_ref[:, sl] = g
                    d_ref[:, sl], mo_ref[:, sl], vo_ref[:, sl] = _adam(w_ref[:, sl], g, m_ref[:, sl], v_ref[:, sl])
            else:
                g = grads[p]
                g_ref[...] = g
                d_ref[...], mo_ref[...], vo_ref[...] = _adam(w_ref[...], g, m_ref[...], v_ref[...])

    flat = [a for trio in wmv for a in trio]
    out_shape = [jax.ShapeDtypeStruct((1, 1), F32)]
    for trio in wmv:
        out_shape += [jax.ShapeDtypeStruct(trio[0].shape, F32)] * 4
    ins = [rep_all, conv_all] + flat
    return pl.pallas_call(
        body, name="adamw_small", grid=(1,),
        in_specs=[_full_spec(a.shape) for a in ins], out_specs=tuple(_full_spec(o.shape) for o in out_shape),
        out_shape=tuple(out_shape), compiler_params=_params(("arbitrary",), 32),
    )(*ins)


FG_FIRST = 3 * AW
N_IN = DIN // NDEV


def _w_in_runs():
    runs = []
    for d in range(NDEV):
        lo, hi = N_IN * d, N_IN * (d + 1)
        for a, b, shift in ((0, FG_FIRST, 0), (FG_FIRST, FG_FIRST + HEADS, DIN - HEADS - FG_FIRST),
                            (FG_FIRST + HEADS, DIN, -HEADS)):
            a, b = max(a, lo), min(b, hi)
            if a < b:
                runs.append((d, a - lo, a + shift, b - a))
    return runs


W_IN_ROWS = 256
N_IN_PAD = 512


def _identity(n):
    return (lax.broadcasted_iota(jnp.int32, (n, n), 0) == lax.broadcasted_iota(jnp.int32, (n, n), 1)).astype(BF16)


def _assemble_w_in(g_in, hosted):
    def body(g_ref, o_ref, t_ref):
        eye = _identity(W_IN_ROWS)
        shard = None
        for d, src, dst, width in _w_in_runs():
            if d != shard:
                t_ref[:, 0:N_IN] = _dot(eye, g_ref[d], NT).astype(BF16)
                shard = d
            o_ref[:, dst:dst + width] = t_ref[:, src:src + width]
        o_ref[:, DIN:DINP] = jnp.zeros((W_IN_ROWS, DINP - DIN), o_ref.dtype)

    (out,), moved = _call(
        body, (g_in,), name="assemble_w_in", grid=(D // W_IN_ROWS,),
        in_specs=[pl.BlockSpec((NDEV, N_IN, W_IN_ROWS), lambda i: (0, 0, i))],
        out_specs=[pl.BlockSpec((W_IN_ROWS, DINP), lambda i: (i, 0))],
        out_shape=[jax.ShapeDtypeStruct((D, DINP), g_in.dtype)],
        scratch_shapes=[pltpu.VMEM((W_IN_ROWS, N_IN_PAD), BF16)], vmem_mb=16, hosted=hosted)
    return out, moved


def _scatter_dw_in(dwp):
    def body(w_ref, o_ref, t_ref):
        eye = _identity(W_IN_ROWS)
        runs = _w_in_runs()
        for i, (d, src, dst, width) in enumerate(runs):
            t_ref[:, src:src + width] = w_ref[:, dst:dst + width]
            if i + 1 == len(runs) or runs[i + 1][0] != d:
                o_ref[d // 2, d % 2] = _dot(t_ref[:, 0:N_IN], eye, TN).astype(BF16)

    (out,), _ = _call(
        body, (dwp,), name="scatter_dw_in", grid=(D // W_IN_ROWS,),
        in_specs=[pl.BlockSpec((W_IN_ROWS, DINP), lambda i: (i, 0))],
        out_specs=[pl.BlockSpec((NCHIP, 2, N_IN, W_IN_ROWS), lambda i: (0, 0, 0, i))],
        out_shape=[jax.ShapeDtypeStruct((NCHIP, 2, N_IN, D), dwp.dtype)],
        scratch_shapes=[pltpu.VMEM((W_IN_ROWS, N_IN_PAD), BF16)], vmem_mb=16)
    return out


def kernel(x, c, w_ada, b_ada, norm1_g, w_in, b_forget, q_norm_g, k_norm_g, conv_mix_w, w_out, norm2_g, w_up, ffn_conv_w, w_down, loss_target, m_w_ada, m_b_ada, m_norm1_g, m_w_in, m_b_forget, m_q_norm_g, m_k_norm_g, m_conv_mix_w, m_w_out, m_norm2_g, m_w_up, m_ffn_conv_w, m_w_down, v_w_ada, v_b_ada, v_norm1_g, v_w_in, v_b_forget, v_q_norm_g, v_k_norm_g, v_conv_mix_w, v_w_out, v_norm2_g, v_w_up, v_ffn_conv_w, v_w_down):
    me = 4 * lax.axis_index("x") + 2 * lax.axis_index("y") + lax.axis_index("c")
    xs, tgt = x[0], loss_target[0]
    n_ada = w_ada.shape[2]
    n_ff = w_up.shape[2]

    conv_w = jnp.concatenate([ffn_conv_w[0], conv_mix_w[0]], axis=1)
    conv_w = jnp.concatenate([conv_w, jnp.zeros((SUB - 3, conv_w.shape[1]), F32)], axis=0)
    c_all, conv_all, g_in = _exchange(
        [(c.reshape(SUB, D // SUB), "ag"), (conv_w, "ag"), (jnp.transpose(w_in[0]).astype(BF16), "ag2")],
        "exchange_w_in")
    g_in, w_out_b, w_up_b, w_down_b = lax.optimization_barrier(
        (g_in, w_out[0].astype(BF16), jnp.transpose(w_up[0]).astype(BF16), w_down[0].astype(BF16)))
    g_out, g_up, g_down = _sequencer_exchange(
        [(w_out_b, "ag2"), (w_up_b, "ag2"), (w_down_b, "ag2")], "gather_weights", collective_id=1)
    c_all = c_all.reshape(NDEV, D)
    cw_ffn = jnp.transpose(conv_all[:, :3, :n_ff], (1, 0, 2)).reshape(3, 2 * DFF)
    cw_mix = jnp.transpose(conv_all[:, :3, n_ff:], (1, 0, 2)).reshape(3, CW)

    b_my = lax.dynamic_slice(b_ada, (0, me * n_ada), (1, n_ada))
    mod_part = _ada_fwd(c_all, w_ada[0], b_my)
    w_in_p, (mod_rows,) = _assemble_w_in(
        g_in, [(jnp.broadcast_to(mod_part[:, None, :], (NDEV, SUB, n_ada)), "a2a")])
    mod = mod_rows[:, 0, :].reshape(NMOD, D)
    mod = jnp.concatenate([mod, jnp.zeros((SUB - NMOD, D), F32)], axis=0)

    h = _norm_mod_fwd(xs, mod, norm1_g)
    proj = _mm(h, w_in_p, "nn", F32, 1024, 640, "proj_fwd")
    bf_pad = jnp.concatenate([b_forget, jnp.zeros((1, LANES - HEADS), F32)], axis=1)
    qp, kp, vp, conv = _qkv_prep(proj, bf_pad, q_norm_g, k_norm_g, cw_mix)
    attn, lse = _attn_fwd(qp, kp, vp)
    w_out_f = g_out.reshape(D, D)
    w_up_t = g_up.reshape(2, DFF, D)
    w_down_f = g_down.reshape(DFF, D)
    mixed = jnp.concatenate([attn, conv], axis=1).astype(BF16)
    z = _mm(mixed, w_out_f, "nn", F32, 1024, 1024, "out_fwd")
    x1, h2 = _resid_norm2(xs, z, mod, norm2_g)
    pre_g, pre_v, y = _ffn_fwd(h2, w_up_t, cw_ffn, w_down_f)
    dout, dy, vec_l = _loss_head(x1, y, tgt, mod)

    dh2, dwup_t, dwd, dcw_g, dcw_v = _ffn_bwd(dy, h2, pre_g, pre_v, w_up_t, cw_ffn, w_down_f)
    s_down = dwd.reshape(NCHIP, 2, DFF // NDEV, D)
    s_up = dwup_t.reshape(NCHIP, 2, n_ff, D)
    dx1, dz, vec_2, (t_up, t_down) = _norm_mod_bwd(dh2, x1, dout, z, mod, norm2_g, 4, 2, "norm2_bwd",
                                                   hosted=[(s_up, "pair"), (s_down, "pair")])
    dwout = _mm(mixed, dz, "tn", BF16, 1024, 1024, "out_bwd_w")
    s_out = dwout.reshape(NCHIP, 2, D // NDEV, D)
    dmixed, (t_out,) = _mm(dz, w_out_f, "nt", F32, 1024, 1024, "out_bwd_x", hosted=[(s_out, "pair")])
    c_out = _pair_add(s_out, t_out, 128, "pair_add_out")
    c_up = _pair_add(s_up, t_up, 176, "pair_add_up")
    c_down = _pair_add(s_down, t_down, 176, "pair_add_down")
    (dqp, dkp, dvp), (p_up, p_down, p_out) = _attn_bwd(
        qp, kp, vp, dmixed, attn, lse, [(c_up, "chips"), (c_down, "chips"), (c_out, "chips")])
    dq, dk, dvb, dfg, vec_qk, dxin, dbg, dcg, dcw_mix = _qkv_post(
        dqp, dkp, dvp, dmixed, proj, bf_pad, q_norm_g, k_norm_g, cw_mix)
    dproj = jnp.concatenate([dq, dk, dvb, dxin, dbg, dcg, dfg], axis=1)
    dwin_p = _mm(h, dproj, "tn", BF16, 1024, 640, "proj_bwd_w")
    s_in = _scatter_dw_in(dwin_p).reshape(NDEV, N_IN, D)
    (p_in,) = _sequencer_exchange([(s_in, "a2a")], "scatter_dw_in_partials", collective_id=2, all_peers=True)
    dh = _mm(dproj, w_in_p, "nt", F32, 1024, 512, "proj_bwd_x", vmem_mb=36)
    grad_x, vec_1 = _norm_mod_bwd(dh, xs, dx1, None, mod, norm1_g, 1, None, "norm1_bwd")

    gap = lambda n: jnp.zeros((1, n), F32)
    misc = jnp.concatenate([
        vec_qk[2:3, :HEADS], gap(LANE_GQ - LANE_BF - HEADS), vec_qk[0:1, :DH], gap(LANE_GK - LANE_GQ - DH),
        vec_qk[1:2, :DH], gap(D - LANE_GK - DH)], axis=1)
    rep = jnp.concatenate([
        vec_1[0:1], vec_1[1:2], vec_2[3:4], vec_2[0:1], vec_2[1:2], vec_l[0:1],
        vec_1[2:3], vec_2[2:3], vec_l[1:2], misc, jnp.zeros((REP_ROWS - 10, D), F32)], axis=0)
    dcw_ffn = jnp.concatenate([dcw_g, dcw_v], axis=1).reshape(SUB, NDEV, n_ff)
    dcw_all = jnp.concatenate([jnp.transpose(dcw_ffn, (1, 0, 2)),
                               jnp.transpose(dcw_mix.reshape(SUB, NDEV, DH), (1, 0, 2))], axis=2)
    r_up = _adamw_sharded(p_up, jnp.transpose(w_up[0]), jnp.transpose(m_w_up[0]), jnp.transpose(v_w_up[0]), 176,
                          "adamw_up")
    r_down = _adamw_sharded(p_down, w_down[0], m_w_down[0], v_w_down[0], 176, "adamw_down")
    rep, dcw_all, r_up, r_down = lax.optimization_barrier((rep, dcw_all, r_up, r_down))
    r_up = tuple(jnp.transpose(a) for a in r_up)
    r_out, (rep_all, conv_parts) = _adamw_sharded(p_out, w_out[0], m_w_out[0], v_w_out[0], 128, "adamw_out",
                                                  hosted=[(rep, "ag"), (dcw_all, "a2a")])
    dmod_my = lax.dynamic_slice(rep_all[:, :NMOD, :].reshape(NDEV, NMOD * D), (0, me * n_ada), (NDEV, n_ada))
    r_ada = _adamw_ada(c_all, dmod_my, w_ada[0], m_w_ada[0], v_w_ada[0])
    r_in = _adamw_sharded(p_in, jnp.transpose(w_in[0]), jnp.transpose(m_w_in[0]), jnp.transpose(v_w_in[0]), N_IN,
                          "adamw_in")
    r_in = tuple(jnp.transpose(a) for a in r_in)
    small = _adamw_small(rep_all, conv_parts, [
        [b_ada, m_b_ada, v_b_ada], [norm1_g, m_norm1_g, v_norm1_g], [b_forget, m_b_forget, v_b_forget],
        [q_norm_g, m_q_norm_g, v_q_norm_g], [k_norm_g, m_k_norm_g, v_k_norm_g], [norm2_g, m_norm2_g, v_norm2_g],
        [ffn_conv_w[0], m_ffn_conv_w[0], v_ffn_conv_w[0]], [conv_mix_w[0], m_conv_mix_w[0], v_conv_mix_w[0]]])
    loss = small[0].reshape(())
    r_bada, r_n1, r_bf, r_gq, r_gk, r_n2, r_cf, r_cm = [small[1 + 4 * p:5 + 4 * p] for p in range(8)]
    lead = lambda t: tuple(a[None] for a in t)
    per_w = [lead(r_ada), r_bada, r_n1, lead(r_in), r_bf, r_gq, r_gk, lead(r_cm), lead(r_out), r_n2,
             lead(r_up), lead(r_cf), lead(r_down)]
    outs = [loss, grad_x[None]]
    for field in range(4):
        outs += [t[field] for t in per_w]
    return tuple(outs)
```
